```python
import math
import jax, jax.numpy as jnp
from jax import lax
import numpy as np

D_MODEL = 1024
BATCH = 16
SEQ = 2048
DEPTH = 1

GLA_HEADS = 4
GLA_DK = 64
GLA_DV = 128
GLA_QK_WIDTH = GLA_HEADS * GLA_DK
GLA_WIDTH = GLA_HEADS * GLA_DV
GATE_RANK = 16
GATE_NORMALIZER = 16.0
CHUNK = 64
CONV_WIDTH = D_MODEL // 2
CONV_GROUPS = 8
CONV_K = 3
MIX_WIDTH = GLA_WIDTH + CONV_WIDTH
D_FF = int(math.ceil(8 * D_MODEL / 3 / 256) * 256)
NORM_EPS = 1e-6

COL_Q = GLA_QK_WIDTH
COL_K = GLA_QK_WIDTH
COL_V = GLA_WIDTH
COL_G = GLA_WIDTH
COL_A = GATE_RANK
COL_CB = CONV_WIDTH
COL_CC = CONV_WIDTH
COL_CH = CONV_WIDTH
IN_COLS = COL_Q + COL_K + COL_V + COL_G + COL_A + COL_CB + COL_CC + COL_CH

kernel_name = "hymba_gla_shortconv_block"


def rmsnorm(x, g):
    xf = x.astype(jnp.float32)
    y = xf * lax.rsqrt(jnp.mean(xf * xf, axis=-1, keepdims=True) + NORM_EPS)
    return (y * g.astype(jnp.float32)).astype(x.dtype)


def gla_chunked(q, k, v, log_a):
    bsz, seq, heads, dk = q.shape
    dv = v.shape[-1]
    n_chunks = seq // CHUNK

    def to_chunks(t):
        return t.astype(jnp.float32).reshape(bsz, n_chunks, CHUNK, heads, t.shape[-1]).transpose(1, 0, 3, 2, 4)

    q, k, v, log_a = to_chunks(q), to_chunks(k), to_chunks(v), to_chunks(log_a)
    b = jnp.cumsum(log_a, axis=-2)
    b_last = b[..., -1:, :]
    q_i = q * jnp.exp(b)
    k_i = k * jnp.exp(-b)
    k_s = k * jnp.exp(b_last - b)

    causal = jnp.tril(jnp.ones((CHUNK, CHUNK), dtype=bool))
    scores = jnp.einsum('nbhck,nbhsk->nbhcs', q_i, k_i)
    scores = jnp.where(causal, scores, 0.0)
    o_intra = jnp.einsum('nbhcs,nbhsv->nbhcv', scores, v)

    chunk_kv = jnp.einsum('nbhsk,nbhsv->nbhkv', k_s, v)
    chunk_decay = jnp.exp(b_last[..., 0, :])

    def step(state, inp):
        dec, kv_c = inp
        return dec[..., None] * state + kv_c, state

    state0 = jnp.zeros((bsz, heads, dk, dv), jnp.float32)
    _, state_prev = lax.scan(step, state0, (chunk_decay, chunk_kv))
    o_inter = jnp.einsum('nbhck,nbhkv->nbhcv', q_i, state_prev)

    o = o_intra + o_inter
    return o.transpose(1, 0, 3, 2, 4).reshape(bsz, seq, heads, dv)


def causal_depthwise_conv(u, w):
    seq = u.shape[1]
    u_pad = jnp.pad(u, ((0, 0), (CONV_K - 1, 0), (0, 0)))
    y = w[0] * u_pad[:, 0:seq]
    for j in range(1, CONV_K):
        y = y + w[j] * u_pad[:, j:j + seq]
    return y


def _fwd_setup_inputs(seed: int = 0) -> dict:
    key = jax.random.key(seed)
    ks = jax.random.split(key, 16)
    f32 = jnp.float32
    n = lambda k, shape, scale: (jax.random.normal(k, shape, f32) * scale)
    return {
        "x": n(ks[0], (BATCH, SEQ, D_MODEL), 1.0),
        "norm1_g": 1.0 + n(ks[1], (DEPTH, D_MODEL), 0.02),
        "w_in": n(ks[2], (DEPTH, D_MODEL, IN_COLS), D_MODEL ** -0.5),
        "w_gate_up": n(ks[3], (DEPTH, GATE_RANK, GLA_QK_WIDTH), GATE_RANK ** -0.5),
        "b_gate": n(ks[4], (DEPTH, GLA_QK_WIDTH), 0.01),
        "gla_norm_g": 1.0 + n(ks[5], (DEPTH, GLA_DV), 0.02),
        "conv_w": n(ks[6], (DEPTH, CONV_K, CONV_WIDTH), CONV_K ** -0.5),
        "w_out": n(ks[7], (DEPTH, MIX_WIDTH, D_MODEL), MIX_WIDTH ** -0.5),
        "norm2_g": 1.0 + n(ks[8], (DEPTH, D_MODEL), 0.02),
        "w_ffn_gate": n(ks[9], (DEPTH, D_MODEL, D_FF), D_MODEL ** -0.5),
        "w_ffn_up": n(ks[10], (DEPTH, D_MODEL, D_FF), D_MODEL ** -0.5),
        "w_ffn_down": n(ks[11], (DEPTH, D_FF, D_MODEL), D_FF ** -0.5),
        "norm_f_g": 1.0 + n(ks[12], (D_MODEL,), 0.02),
    }


def _fwd_reference(x, norm1_g, w_in, w_gate_up, b_gate, gla_norm_g, conv_w, w_out,
              norm2_g, w_ffn_gate, w_ffn_up, w_ffn_down, norm_f_g):
    bsz, seq, _ = x.shape
    splits = np.cumsum([COL_Q, COL_K, COL_V, COL_G, COL_A, COL_CB, COL_CC])
    for l in range(DEPTH):
        h = rmsnorm(x, norm1_g[l])
        proj = h @ w_in[l]
        p_q, p_k, p_v, p_g, p_a, c_b, c_c, c_h = jnp.split(proj, splits, axis=-1)

        q = p_q.reshape(bsz, seq, GLA_HEADS, GLA_DK) * (GLA_DK ** -0.5)
        k = p_k.reshape(bsz, seq, GLA_HEADS, GLA_DK)
        v = p_v.reshape(bsz, seq, GLA_HEADS, GLA_DV)
        gate_logits = (p_a @ w_gate_up[l] + b_gate[l]).astype(jnp.float32)
        log_a = (jax.nn.log_sigmoid(gate_logits) / GATE_NORMALIZER).reshape(bsz, seq, GLA_HEADS, GLA_DK)
        o = gla_chunked(q, k, v, log_a)
        o = rmsnorm(o, gla_norm_g[l]).astype(x.dtype)
        o = o * jax.nn.silu(p_g.reshape(bsz, seq, GLA_HEADS, GLA_DV))
        o_gla = o.reshape(bsz, seq, GLA_WIDTH)

        u = c_c * c_h
        o_conv = c_b * causal_depthwise_conv(u, conv_w[l])

        mixed = jnp.concatenate([o_gla, o_conv], axis=-1) @ w_out[l]
        x = x + mixed

        h2 = rmsnorm(x, norm2_g[l])
        ffn = (jax.nn.silu(h2 @ w_ffn_gate[l]) * (h2 @ w_ffn_up[l])) @ w_ffn_down[l]
        x = x + ffn
    return rmsnorm(x, norm_f_g)


import jax as _jax
import jax.numpy as _jnp

TWIN_FORMAT = 'train_step'
FWD_PARAMS = ['x', 'norm1_g', 'w_in', 'w_gate_up', 'b_gate', 'gla_norm_g', 'conv_w', 'w_out', 'norm2_g', 'w_ffn_gate', 'w_ffn_up', 'w_ffn_down', 'norm_f_g']
TWIN_WEIGHTS = ['norm1_g', 'w_in', 'w_gate_up', 'b_gate', 'gla_norm_g', 'conv_w', 'w_out', 'norm2_g', 'w_ffn_gate', 'w_ffn_up', 'w_ffn_down', 'norm_f_g']
TWIN_DIFF_INPUT = 'x'
TWIN_INPUTS = ['x', 'norm1_g', 'w_in', 'w_gate_up', 'b_gate', 'gla_norm_g', 'conv_w', 'w_out', 'norm2_g', 'w_ffn_gate', 'w_ffn_up', 'w_ffn_down', 'norm_f_g', 'loss_target', 'm_norm1_g', 'm_w_in', 'm_w_gate_up', 'm_b_gate', 'm_gla_norm_g', 'm_conv_w', 'm_w_out', 'm_norm2_g', 'm_w_ffn_gate', 'm_w_ffn_up', 'm_w_ffn_down', 'm_norm_f_g', 'v_norm1_g', 'v_w_in', 'v_w_gate_up', 'v_b_gate', 'v_gla_norm_g', 'v_conv_w', 'v_w_out', 'v_norm2_g', 'v_w_ffn_gate', 'v_w_ffn_up', 'v_w_ffn_down', 'v_norm_f_g']
TWIN_OUTPUTS = ['loss', 'grad_x', 'grad_norm1_g', 'grad_w_in', 'grad_w_gate_up', 'grad_b_gate', 'grad_gla_norm_g', 'grad_conv_w', 'grad_w_out', 'grad_norm2_g', 'grad_w_ffn_gate', 'grad_w_ffn_up', 'grad_w_ffn_down', 'grad_norm_f_g', 'delta_norm1_g', 'delta_w_in', 'delta_w_gate_up', 'delta_b_gate', 'delta_gla_norm_g', 'delta_conv_w', 'delta_w_out', 'delta_norm2_g', 'delta_w_ffn_gate', 'delta_w_ffn_up', 'delta_w_ffn_down', 'delta_norm_f_g', 'new_m_norm1_g', 'new_m_w_in', 'new_m_w_gate_up', 'new_m_b_gate', 'new_m_gla_norm_g', 'new_m_conv_w', 'new_m_w_out', 'new_m_norm2_g', 'new_m_w_ffn_gate', 'new_m_w_ffn_up', 'new_m_w_ffn_down', 'new_m_norm_f_g', 'new_v_norm1_g', 'new_v_w_in', 'new_v_w_gate_up', 'new_v_b_gate', 'new_v_gla_norm_g', 'new_v_conv_w', 'new_v_w_out', 'new_v_norm2_g', 'new_v_w_ffn_gate', 'new_v_w_ffn_up', 'new_v_w_ffn_down', 'new_v_norm_f_g']
TWIN_LEAF_KINDS = {'loss': 'loss', 'grad_x': 'grad_x', 'grad_norm1_g': 'grad_w', 'grad_w_in': 'grad_w', 'grad_w_gate_up': 'grad_w', 'grad_b_gate': 'grad_w', 'grad_gla_norm_g': 'grad_w', 'grad_conv_w': 'grad_w', 'grad_w_out': 'grad_w', 'grad_norm2_g': 'grad_w', 'grad_w_ffn_gate': 'grad_w', 'grad_w_ffn_up': 'grad_w', 'grad_w_ffn_down': 'grad_w', 'grad_norm_f_g': 'grad_w', 'delta_norm1_g': 'delta_w', 'delta_w_in': 'delta_w', 'delta_w_gate_up': 'delta_w', 'delta_b_gate': 'delta_w', 'delta_gla_norm_g': 'delta_w', 'delta_conv_w': 'delta_w', 'delta_w_out': 'delta_w', 'delta_norm2_g': 'delta_w', 'delta_w_ffn_gate': 'delta_w', 'delta_w_ffn_up': 'delta_w', 'delta_w_ffn_down': 'delta_w', 'delta_norm_f_g': 'delta_w', 'new_m_norm1_g': 'new_m', 'new_m_w_in': 'new_m', 'new_m_w_gate_up': 'new_m', 'new_m_b_gate': 'new_m', 'new_m_gla_norm_g': 'new_m', 'new_m_conv_w': 'new_m', 'new_m_w_out': 'new_m', 'new_m_norm2_g': 'new_m', 'new_m_w_ffn_gate': 'new_m', 'new_m_w_ffn_up': 'new_m', 'new_m_w_ffn_down': 'new_m', 'new_m_norm_f_g': 'new_m', 'new_v_norm1_g': 'new_v', 'new_v_w_in': 'new_v', 'new_v_w_gate_up': 'new_v', 'new_v_b_gate': 'new_v', 'new_v_gla_norm_g': 'new_v', 'new_v_conv_w': 'new_v', 'new_v_w_out': 'new_v', 'new_v_norm2_g': 'new_v', 'new_v_w_ffn_gate': 'new_v', 'new_v_w_ffn_up': 'new_v', 'new_v_w_ffn_down': 'new_v', 'new_v_norm_f_g': 'new_v'}


def _forward(args):
    return _fwd_reference(*[args[k] for k in FWD_PARAMS])


def _output_shape():
    out = _jax.eval_shape(lambda: _forward(_fwd_setup_inputs(0)))
    return out.shape, out.dtype

N_MICROBATCH = 1
ADAM_LR = 0.001
ADAM_B1 = 0.9
ADAM_B2 = 0.999
ADAM_EPS = 1e-08
ADAM_WD = 0.01
ADAM_STEP = 10
PER_EXAMPLE_BATCH_AXIS = {'x': 0, 'loss_target': 0}
SHARED_INPUTS = []
_WEIGHT_DTYPES = {'norm1_g': _jnp.float32, 'w_in': _jnp.float32, 'w_gate_up': _jnp.float32, 'b_gate': _jnp.float32, 'gla_norm_g': _jnp.float32, 'conv_w': _jnp.float32, 'w_out': _jnp.float32, 'norm2_g': _jnp.float32, 'w_ffn_gate': _jnp.float32, 'w_ffn_up': _jnp.float32, 'w_ffn_down': _jnp.float32, 'norm_f_g': _jnp.float32}
MOMENT_SCALE = {'norm1_g': 2.266991e-01, 'w_in': 1.294294e-01, 'w_gate_up': 1.446013e-02, 'b_gate': 5.730773e-02, 'gla_norm_g': 1.771854e-01, 'conv_w': 1.467332e-01, 'w_out': 1.230420e-01, 'norm2_g': 1.163703e-01, 'w_ffn_gate': 4.617229e-02, 'w_ffn_up': 4.477532e-02, 'w_ffn_down': 7.419803e-02, 'norm_f_g': 3.201596e+01}


def _to_microbatches(a, axis):
    t = _jnp.moveaxis(a, axis, 0)
    t = t.reshape((N_MICROBATCH, t.shape[0] // N_MICROBATCH) + t.shape[1:])
    return _jnp.moveaxis(t, 1, axis + 1)


def setup_inputs(seed: int = 0) -> dict:
    inp = _fwd_setup_inputs(seed)
    key = _jax.random.fold_in(_jax.random.key(seed), 7919)
    shape, _ = _output_shape()
    out = dict(inp)
    out["loss_target"] = _jax.random.normal(_jax.random.fold_in(key, 0), shape, _jnp.float32)
    for i, name in enumerate(TWIN_WEIGHTS):
        w = inp[name].astype(_jnp.float32)
        if MOMENT_SCALE is None:
            s = _jnp.sqrt(_jnp.mean(_jnp.square(w)) + 1e-30)
        else:
            s = MOMENT_SCALE[name]
        km, kv = _jax.random.split(_jax.random.fold_in(key, i + 1))
        out[name] = w
        out["m_" + name] = s * _jax.random.normal(km, w.shape, _jnp.float32)
        out["v_" + name] = (s * s) * _jax.random.uniform(kv, w.shape, _jnp.float32, 0.5, 1.5)
    if N_MICROBATCH > 1:
        for name, axis in PER_EXAMPLE_BATCH_AXIS.items():
            out[name] = _to_microbatches(out[name], axis)
    return {'x': out['x'], 'norm1_g': out['norm1_g'], 'w_in': out['w_in'], 'w_gate_up': out['w_gate_up'], 'b_gate': out['b_gate'], 'gla_norm_g': out['gla_norm_g'], 'conv_w': out['conv_w'], 'w_out': out['w_out'], 'norm2_g': out['norm2_g'], 'w_ffn_gate': out['w_ffn_gate'], 'w_ffn_up': out['w_ffn_up'], 'w_ffn_down': out['w_ffn_down'], 'norm_f_g': out['norm_f_g'], 'loss_target': out['loss_target'], 'm_norm1_g': out['m_norm1_g'], 'm_w_in': out['m_w_in'], 'm_w_gate_up': out['m_w_gate_up'], 'm_b_gate': out['m_b_gate'], 'm_gla_norm_g': out['m_gla_norm_g'], 'm_conv_w': out['m_conv_w'], 'm_w_out': out['m_w_out'], 'm_norm2_g': out['m_norm2_g'], 'm_w_ffn_gate': out['m_w_ffn_gate'], 'm_w_ffn_up': out['m_w_ffn_up'], 'm_w_ffn_down': out['m_w_ffn_down'], 'm_norm_f_g': out['m_norm_f_g'], 'v_norm1_g': out['v_norm1_g'], 'v_w_in': out['v_w_in'], 'v_w_gate_up': out['v_w_gate_up'], 'v_b_gate': out['v_b_gate'], 'v_gla_norm_g': out['v_gla_norm_g'], 'v_conv_w': out['v_conv_w'], 'v_w_out': out['v_w_out'], 'v_norm2_g': out['v_norm2_g'], 'v_w_ffn_gate': out['v_w_ffn_gate'], 'v_w_ffn_up': out['v_w_ffn_up'], 'v_w_ffn_down': out['v_w_ffn_down'], 'v_norm_f_g': out['v_norm_f_g']}


def _loss(weights, diff, rest, loss_target):
    with _jax.named_scope("forward"):
        args = {**rest, TWIN_DIFF_INPUT: diff, **{k: w.astype(_WEIGHT_DTYPES[k]) for k, w in weights.items()}}
        y = _forward(args)
    with _jax.named_scope("loss_head"):
        err = _jnp.square(y.astype(_jnp.float32) - loss_target)
        return 0.5 * _jnp.sum(_jnp.mean(err, axis=-1)) if err.ndim else 0.5 * err


def _adamw(w, g, m, v):
    m = ADAM_B1 * m + (1.0 - ADAM_B1) * g
    v = ADAM_B2 * v + (1.0 - ADAM_B2) * _jnp.square(g)
    m_hat = m / (1.0 - ADAM_B1 ** ADAM_STEP)
    v_hat = v / (1.0 - ADAM_B2 ** ADAM_STEP)
    delta = -ADAM_LR * (m_hat / (_jnp.sqrt(v_hat) + ADAM_EPS) + ADAM_WD * w)
    return delta, m, v


def reference(x, norm1_g, w_in, w_gate_up, b_gate, gla_norm_g, conv_w, w_out, norm2_g, w_ffn_gate, w_ffn_up, w_ffn_down, norm_f_g, loss_target, m_norm1_g, m_w_in, m_w_gate_up, m_b_gate, m_gla_norm_g, m_conv_w, m_w_out, m_norm2_g, m_w_ffn_gate, m_w_ffn_up, m_w_ffn_down, m_norm_f_g, v_norm1_g, v_w_in, v_w_gate_up, v_b_gate, v_gla_norm_g, v_conv_w, v_w_out, v_norm2_g, v_w_ffn_gate, v_w_ffn_up, v_w_ffn_down, v_norm_f_g):
    given = dict(x=x, norm1_g=norm1_g, w_in=w_in, w_gate_up=w_gate_up, b_gate=b_gate, gla_norm_g=gla_norm_g, conv_w=conv_w, w_out=w_out, norm2_g=norm2_g, w_ffn_gate=w_ffn_gate, w_ffn_up=w_ffn_up, w_ffn_down=w_ffn_down, norm_f_g=norm_f_g, loss_target=loss_target, m_norm1_g=m_norm1_g, m_w_in=m_w_in, m_w_gate_up=m_w_gate_up, m_b_gate=m_b_gate, m_gla_norm_g=m_gla_norm_g, m_conv_w=m_conv_w, m_w_out=m_w_out, m_norm2_g=m_norm2_g, m_w_ffn_gate=m_w_ffn_gate, m_w_ffn_up=m_w_ffn_up, m_w_ffn_down=m_w_ffn_down, m_norm_f_g=m_norm_f_g, v_norm1_g=v_norm1_g, v_w_in=v_w_in, v_w_gate_up=v_w_gate_up, v_b_gate=v_b_gate, v_gla_norm_g=v_gla_norm_g, v_conv_w=v_conv_w, v_w_out=v_w_out, v_norm2_g=v_norm2_g, v_w_ffn_gate=v_w_ffn_gate, v_w_ffn_up=v_w_ffn_up, v_w_ffn_down=v_w_ffn_down, v_norm_f_g=v_norm_f_g)
    weights = {n: given[n] for n in TWIN_WEIGHTS}
    shared = {n: given[n] for n in SHARED_INPUTS}
    per_example = {n: given[n] for n in ['x']}
    grad_fn = _jax.value_and_grad(_loss, argnums=(0, 1))

    def one_microbatch(ex, loss_target):
        ex = dict(ex)
        diff = ex.pop(TWIN_DIFF_INPUT)
        return grad_fn(weights, diff, {**shared, **ex}, loss_target)

    if N_MICROBATCH == 1:
        loss, (grad_w, grad_x) = one_microbatch(per_example, given["loss_target"])
    else:
        def body(carry, xs):
            loss_sum, grad_sum = carry
            l_k, (gw_k, gx_k) = one_microbatch(xs[0], xs[1])
            with _jax.named_scope("update"):
                return (loss_sum + l_k, _jax.tree.map(_jnp.add, grad_sum, gw_k)), gx_k

        init = (_jnp.zeros((), _jnp.float32), _jax.tree.map(_jnp.zeros_like, weights))
        (loss, grad_w), grad_x = _jax.lax.scan(body, init, (per_example, given["loss_target"]))
    with _jax.named_scope("update"):
        delta_w, new_m, new_v = {}, {}, {}
        for n in TWIN_WEIGHTS:
            delta_w[n], new_m[n], new_v[n] = _adamw(weights[n], grad_w[n], given["m_" + n], given["v_" + n])
    return (loss, grad_x, *[grad_w[n] for n in TWIN_WEIGHTS], *[delta_w[n] for n in TWIN_WEIGHTS],
            *[new_m[n] for n in TWIN_WEIGHTS], *[new_v[n] for n in TWIN_WEIGHTS])
```

```python
import functools

import jax
import jax.numpy as jnp
from jax import lax
from jax.experimental import pallas as pl
from jax.experimental.pallas import tpu as pltpu

F32 = jnp.float32
BF16 = jnp.bfloat16
HIGHEST = lax.Precision.HIGHEST
MESH = pl.DeviceIdType.MESH

N_DEV = 8
D = 1024
DFF = 2816
HEADS = 4
DK = 64
DV = 128
NQK = HEADS * DK
NV = HEADS * DV
RANK = 16
CHUNK = 64
CW = 512
CONV_K = 3
IN_COLS = 3088
EPS = 1e-6
INV_GATE_NORM = 1.0 / 16.0
Q_SCALE = DK ** -0.5

PW = 3200
OQ, OK_, OV, OG, OCB, OCC, OCH, OA = 0, 256, 512, 1024, 1536, 2048, 2560, 3072
A_PAD = 128

ADAM_LR = 0.001
ADAM_B1 = 0.9
ADAM_B2 = 0.999
ADAM_EPS = 1e-08
ADAM_WD = 0.01
ADAM_STEP = 10

ROWS_IN = D * (IN_COLS // N_DEV) // 128
ROWS_OUT = (D // N_DEV) * D // 128
ROWS_FF = D * (DFF // N_DEV) // 128
PACK_ROWS = ROWS_IN + ROWS_OUT + 3 * ROWS_FF
SMALL_W_ROWS = 8
SMALL_G_ROWS = 72

VMEM_LIMIT = 56 * 1024 * 1024


def _params(sem=None, vmem=VMEM_LIMIT):
    return pltpu.CompilerParams(dimension_semantics=sem, vmem_limit_bytes=vmem)


def _nt(a, b):
    return lax.dot_general(a, b, (((1,), (1,)), ((), ())), preferred_element_type=F32)


def _tn(a, b, precision=None):
    return lax.dot_general(a, b, (((0,), (0,)), ((), ())), preferred_element_type=F32, precision=precision)


def _mm(a, b, precision=None):
    return jnp.dot(a, b, preferred_element_type=F32, precision=precision)


def _in_proj_fwd(x2d, g1, w_in_p, wgu_p, b_gate, tm):
    t = x2d.shape[0]

    def body(x_ref, g_ref, w_ref, wgu_ref, bg_ref, proj_ref, z_ref, h_ref):
        x = x_ref[...]
        r = lax.rsqrt(jnp.mean(x * x, axis=-1, keepdims=True) + EPS)
        h = ((x * r) * g_ref[...]).astype(BF16)
        h_ref[...] = h
        proj = _mm(h, w_ref[...])
        proj_ref[...] = proj
        pa = proj[:, OA:OA + A_PAD].astype(BF16)
        z_ref[...] = _mm(pa, wgu_ref[...]) + bg_ref[...]

    return pl.pallas_call(
        body,
        name="in_proj_fwd",
        grid=(t // tm,),
        in_specs=[
            pl.BlockSpec((tm, D), lambda i: (i, 0)),
            pl.BlockSpec((1, D), lambda i: (0, 0)),
            pl.BlockSpec((D, PW), lambda i: (0, 0)),
            pl.BlockSpec((A_PAD, NQK), lambda i: (0, 0)),
            pl.BlockSpec((1, NQK), lambda i: (0, 0)),
        ],
        out_specs=[
            pl.BlockSpec((tm, PW), lambda i: (i, 0)),
            pl.BlockSpec((tm, NQK), lambda i: (i, 0)),
            pl.BlockSpec((tm, D), lambda i: (i, 0)),
        ],
        out_shape=[
            jax.ShapeDtypeStruct((t, PW), F32),
            jax.ShapeDtypeStruct((t, NQK), F32),
            jax.ShapeDtypeStruct((t, D), BF16),
        ],
        compiler_params=_params(("arbitrary",)),
    )(x2d, g1, w_in_p, wgu_p, b_gate)


def _head_masks():
    lane = lax.broadcasted_iota(jnp.int32, (1, NQK), 1)
    return [(lane >= DK * h) & (lane < DK * (h + 1)) for h in range(HEADS)]


def _chunk_fwd_parts(q, k, z, tril):
    la = (jnp.minimum(z, 0.0) - jnp.log1p(jnp.exp(-jnp.abs(z)))) * INV_GATE_NORM
    bc = _mm(tril, la, precision=HIGHEST)
    bl = bc[CHUNK - 1:CHUNK, :]
    eb = jnp.exp(bc)
    enb = jnp.exp(-bc)
    ekl = jnp.exp(bl - bc)
    qi = (q * Q_SCALE) * eb
    ki = k * enb
    ks = k * ekl
    decb = jnp.exp(_tn(la, jnp.ones((CHUNK, DV), F32), precision=HIGHEST))
    return la, eb, enb, ekl, qi, ki, ks, decb


def _stack_heads(a, masks):
    return jnp.concatenate([jnp.where(m, a, 0.0) for m in masks], axis=0)


def _causal_stack_mask():
    row = lax.broadcasted_iota(jnp.int32, (HEADS * CHUNK, CHUNK), 0)
    col = lax.broadcasted_iota(jnp.int32, (HEADS * CHUNK, CHUNK), 1)
    return (row & (CHUNK - 1)) >= col


def _conv_taps(u, uprev):
    row = lax.broadcasted_iota(jnp.int32, u.shape, 0)
    u1 = jnp.where(row < 1, pltpu.roll(uprev, 1, 0), pltpu.roll(u, 1, 0))
    u2 = jnp.where(row < 2, pltpu.roll(uprev, 2, 0), pltpu.roll(u, 2, 0))
    return u1, u2


def _mix_fwd(proj3, z3, gng, conv_w):
    nb, s, _ = proj3.shape
    nc = s // CHUNK

    def body(p_ref, z_ref, gng_ref, cw_ref, mix_ref, o_ref, sprev_ref, s_ref, uprev_ref):
        n = pl.program_id(0)

        @pl.when(n == 0)
        def _():
            s_ref[...] = jnp.zeros_like(s_ref)
            uprev_ref[...] = jnp.zeros_like(uprev_ref)

        r_i = lax.broadcasted_iota(jnp.int32, (CHUNK, CHUNK), 0)
        c_i = lax.broadcasted_iota(jnp.int32, (CHUNK, CHUNK), 1)
        tril = (r_i >= c_i).astype(F32)
        masks = _head_masks()
        cmask = _causal_stack_mask()
        gg = gng_ref[...]
        for b in range(nb):
            q = p_ref[b, :, OQ:OQ + NQK]
            k = p_ref[b, :, OK_:OK_ + NQK]
            _, _, _, _, qi, ki, ks, decb = _chunk_fwd_parts(q, k, z_ref[b], tril)
            qs = _stack_heads(qi, masks).astype(BF16)
            sc = jnp.where(cmask, _nt(qs, ki.astype(BF16)), 0.0).astype(BF16)
            ks16 = ks.astype(BF16)
            st = s_ref[b]
            sprev_ref[b, 0] = st
            st16 = st.astype(BF16)
            for h in range(HEADS):
                rows = slice(CHUNK * h, CHUNK * (h + 1))
                cols = slice(DV * h, DV * (h + 1))
                v16 = p_ref[b, :, OV + DV * h:OV + DV * (h + 1)].astype(BF16)
                o = _mm(sc[rows], v16) + _mm(qs[rows], st16)
                o_ref[b, :, cols] = o
                r = lax.rsqrt(jnp.mean(o * o, axis=-1, keepdims=True) + EPS)
                on = (o * r) * gg
                g = p_ref[b, :, OG + DV * h:OG + DV * (h + 1)]
                mix_ref[b, :, cols] = (on * (g * jax.nn.sigmoid(g))).astype(BF16)
                kv = _tn(ks16, v16)
                s_ref[b, rows, :] = decb[rows] * st[rows] + kv[rows]
            u = p_ref[b, :, OCC:OCC + CW] * p_ref[b, :, OCH:OCH + CW]
            u1, u2 = _conv_taps(u, uprev_ref[b])
            yc = cw_ref[0:1, :] * u2 + cw_ref[1:2, :] * u1 + cw_ref[2:3, :] * u
            mix_ref[b, :, NV:NV + CW] = (p_ref[b, :, OCB:OCB + CW] * yc).astype(BF16)
            uprev_ref[b] = u

    return pl.pallas_call(
        body,
        name="mix_fwd",
        grid=(nc,),
        in_specs=[
            pl.BlockSpec((nb, CHUNK, PW), lambda n: (0, n, 0)),
            pl.BlockSpec((nb, CHUNK, NQK), lambda n: (0, n, 0)),
            pl.BlockSpec((1, DV), lambda n: (0, 0)),
            pl.BlockSpec((CONV_K, CW), lambda n: (0, 0)),
        ],
        out_specs=[
            pl.BlockSpec((nb, CHUNK, D), lambda n: (0, n, 0)),
            pl.BlockSpec((nb, CHUNK, NV), lambda n: (0, n, 0)),
            pl.BlockSpec((nb, 1, NQK, DV), lambda n: (0, n, 0, 0)),
        ],
        out_shape=[
            jax.ShapeDtypeStruct((nb, s, D), BF16),
            jax.ShapeDtypeStruct((nb, s, NV), F32),
            jax.ShapeDtypeStruct((nb, nc, NQK, DV), F32),
        ],
        scratch_shapes=[pltpu.VMEM((nb, NQK, DV), F32), pltpu.VMEM((nb, CHUNK, CW), F32)],
        compiler_params=_params(("arbitrary",)),
    )(proj3, z3, gng, conv_w)


def _ffn_fwd_bwd(mix2d, x2d, tgt2d, w_out, w_g, w_u, w_d, g2, gf, tm):
    t = x2d.shape[0]

    def body(mix_ref, x_ref, tgt_ref, g2_ref, gf_ref, wo_hbm, wg_hbm, wu_hbm, wd_hbm,
             dx1_ref, dmix_ref, h2_ref, act_ref, dgate_ref, dup_ref, dx2_ref, dg2_ref, dgf_ref, loss_ref,
             wo, wg, wu, wd):
        i = pl.program_id(0)

        @pl.when(i == 0)
        def _():
            pltpu.sync_copy(wo_hbm, wo)
            pltpu.sync_copy(wg_hbm, wg)
            pltpu.sync_copy(wu_hbm, wu)
            pltpu.sync_copy(wd_hbm, wd)
            dg2_ref[...] = jnp.zeros_like(dg2_ref)
            dgf_ref[...] = jnp.zeros_like(dgf_ref)
            loss_ref[...] = jnp.zeros_like(loss_ref)

        g2v = g2_ref[...]
        gfv = gf_ref[...]
        x1 = x_ref[...] + _mm(mix_ref[...], wo[...])
        r2 = lax.rsqrt(jnp.mean(x1 * x1, axis=-1, keepdims=True) + EPS)
        n2 = x1 * r2
        h2 = (n2 * g2v).astype(BF16)
        h2_ref[...] = h2
        gate = _mm(h2, wg[...])
        up = _mm(h2, wu[...])
        sg = jax.nn.sigmoid(gate)
        sil = gate * sg
        act = (sil * up).astype(BF16)
        act_ref[...] = act
        x2 = x1 + _mm(act, wd[...])
        rf = lax.rsqrt(jnp.mean(x2 * x2, axis=-1, keepdims=True) + EPS)
        nf = x2 * rf
        err = nf * gfv - tgt_ref[...]
        loss_ref[...] += 0.5 * jnp.sum(jnp.mean(err * err, axis=-1, keepdims=True))
        dy = err * (1.0 / D)
        dgf_ref[...] += jnp.sum(dy * nf, axis=0, keepdims=True)
        dnf = dy * gfv
        dx2 = rf * (dnf - nf * jnp.mean(dnf * nf, axis=-1, keepdims=True))
        dx2b = dx2.astype(BF16)
        dx2_ref[...] = dx2b
        dact = _nt(dx2b, wd[...])
        dup = (dact * sil).astype(BF16)
        dgate = ((dact * up) * (sg * (1.0 + gate * (1.0 - sg)))).astype(BF16)
        dup_ref[...] = dup
        dgate_ref[...] = dgate
        dh2 = _nt(dgate, wg[...]) + _nt(dup, wu[...])
        dg2_ref[...] += jnp.sum(dh2 * n2, axis=0, keepdims=True)
        dn2 = dh2 * g2v
        dx1 = dx2 + r2 * (dn2 - n2 * jnp.mean(dn2 * n2, axis=-1, keepdims=True))
        dx1_ref[...] = dx1
        dmix_ref[...] = _nt(dx1.astype(BF16), wo[...])

    tile = lambda w: pl.BlockSpec((tm, w), lambda i: (i, 0))
    vec = pl.BlockSpec((1, D), lambda i: (0, 0))
    hbm = pl.BlockSpec(memory_space=pl.ANY)
    return pl.pallas_call(
        body,
        name="ffn_fwd_bwd",
        grid=(t // tm,),
        in_specs=[tile(D), tile(D), tile(D), vec, vec, hbm, hbm, hbm, hbm],
        out_specs=[tile(D), tile(D), tile(D), tile(DFF), tile(DFF), tile(DFF), tile(D), vec, vec,
                   pl.BlockSpec((1, 128), lambda i: (0, 0))],
        out_shape=[
            jax.ShapeDtypeStruct((t, D), F32),
            jax.ShapeDtypeStruct((t, D), F32),
            jax.ShapeDtypeStruct((t, D), BF16),
            jax.ShapeDtypeStruct((t, DFF), BF16),
            jax.ShapeDtypeStruct((t, DFF), BF16),
            jax.ShapeDtypeStruct((t, DFF), BF16),
            jax.ShapeDtypeStruct((t, D), BF16),
            jax.ShapeDtypeStruct((1, D), F32),
            jax.ShapeDtypeStruct((1, D), F32),
            jax.ShapeDtypeStruct((1, 128), F32),
        ],
        scratch_shapes=[pltpu.VMEM((D, D), BF16), pltpu.VMEM((D, DFF), BF16), pltpu.VMEM((D, DFF), BF16),
                        pltpu.VMEM((DFF, D), BF16)],
        compiler_params=_params(("arbitrary",)),
    )(mix2d, x2d, tgt2d, g2, gf, w_out, w_g, w_u, w_d)


def _tn_matmul(a, b, bm, bn, tk, name):
    t, m = a.shape
    n = b.shape[1]
    nk = t // tk

    def body(a_ref, b_ref, o_ref):
        @pl.when(pl.program_id(2) == 0)
        def _():
            o_ref[...] = jnp.zeros_like(o_ref)

        o_ref[...] += _tn(a_ref[...].astype(BF16), b_ref[...].astype(BF16))

    return pl.pallas_call(
        body,
        name=name,
        grid=(m // bm, n // bn, nk),
        in_specs=[pl.BlockSpec((tk, bm), lambda i, j, k: (k, i)), pl.BlockSpec((tk, bn), lambda i, j, k: (k, j))],
        out_specs=pl.BlockSpec((bm, bn), lambda i, j, k: (i, j)),
        out_shape=jax.ShapeDtypeStruct((m, n), F32),
        compiler_params=_params(("parallel", "parallel", "arbitrary")),
    )(a, b)


def _mix_bwd(proj3, z3, sprev, opre3, dmix3, gng, conv_w, wgu_p):
    nb, s, _ = proj3.shape
    nc = s // CHUNK

    def body(p_ref, pprev_ref, z_ref, sp_ref, o_ref, dm_ref, gng_ref, cw_ref, wgu_ref,
             dproj_ref, dgng_ref, dcw_ref, dbg_ref, dwgu_ref, ds_ref, dycn_ref):
        step = pl.program_id(0)
        n = nc - 1 - step

        @pl.when(step == 0)
        def _():
            ds_ref[...] = jnp.zeros_like(ds_ref)
            dycn_ref[...] = jnp.zeros_like(dycn_ref)
            dgng_ref[...] = jnp.zeros_like(dgng_ref)
            dcw_ref[...] = jnp.zeros_like(dcw_ref)
            dbg_ref[...] = jnp.zeros_like(dbg_ref)
            dwgu_ref[...] = jnp.zeros_like(dwgu_ref)

        r_i = lax.broadcasted_iota(jnp.int32, (CHUNK, CHUNK), 0)
        c_i = lax.broadcasted_iota(jnp.int32, (CHUNK, CHUNK), 1)
        tril = (r_i >= c_i).astype(F32)
        triu = (r_i <= c_i).astype(F32)
        causal = r_i >= c_i
        masks = _head_masks()
        cmask = _causal_stack_mask()
        gg = gng_ref[...]
        last_row = lax.broadcasted_iota(jnp.int32, (CHUNK, NQK), 0) == CHUNK - 1
        ones_r = jnp.ones((8, DV), F32)
        has_prev = (n > 0).astype(F32)
        for b in range(nb):
            q = p_ref[b, :, OQ:OQ + NQK]
            k = p_ref[b, :, OK_:OK_ + NQK]
            z = z_ref[b]
            _, eb, enb, ekl, qi, ki, ks, decb = _chunk_fwd_parts(q, k, z, tril)
            qi16 = qi.astype(BF16)
            ki16 = ki.astype(BF16)
            ks16 = ks.astype(BF16)
            qs = _stack_heads(qi, masks).astype(BF16)
            sc = jnp.where(cmask, _nt(qs, ki16), 0.0).astype(BF16)
            st = sp_ref[b, 0]
            st16 = st.astype(BF16)
            dsn = ds_ref[b]
            dsn16 = dsn.astype(BF16)
            dqi = jnp.zeros((CHUNK, NQK), F32)
            dki = jnp.zeros((CHUNK, NQK), F32)
            dks = jnp.zeros((CHUNK, NQK), F32)
            dgng = jnp.zeros((1, DV), F32)
            for h in range(HEADS):
                rows = slice(CHUNK * h, CHUNK * (h + 1))
                cols = slice(DV * h, DV * (h + 1))
                o = o_ref[b, :, cols]
                r = lax.rsqrt(jnp.mean(o * o, axis=-1, keepdims=True) + EPS)
                nh = o * r
                g = p_ref[b, :, OG + DV * h:OG + DV * (h + 1)]
                sg = jax.nn.sigmoid(g)
                dog = dm_ref[b, :, cols]
                dproj_ref[b, :, OG + DV * h:OG + DV * (h + 1)] = (
                    (dog * (nh * gg)) * (sg * (1.0 + g * (1.0 - sg)))).astype(BF16)
                don = dog * (g * sg)
                dgng = dgng + jnp.sum(don * nh, axis=0, keepdims=True)
                dn = don * gg
                do = r * (dn - nh * jnp.mean(dn * nh, axis=-1, keepdims=True))
                do16 = do.astype(BF16)
                v16 = p_ref[b, :, OV + DV * h:OV + DV * (h + 1)].astype(BF16)
                dp16 = jnp.where(causal, _nt(do16, v16), 0.0).astype(BF16)
                ksm = jnp.where(masks[h], ks16, jnp.zeros_like(ks16))
                dv = _tn(sc[rows], do16) + _mm(ksm, dsn16)
                dproj_ref[b, :, OV + DV * h:OV + DV * (h + 1)] = dv.astype(BF16)
                dqi = dqi + jnp.where(masks[h], _mm(dp16, ki16) + _nt(do16, st16), 0.0)
                dki = dki + jnp.where(masks[h], _tn(dp16, qi16), 0.0)
                dks = dks + jnp.where(masks[h], _nt(v16, dsn16), 0.0)
                qdo = _tn(qi16, do16)
                ds_ref[b, rows, :] = decb[rows] * dsn[rows] + qdo[rows]
            dgng_ref[...] += dgng
            dproj_ref[b, :, OQ:OQ + NQK] = (dqi * (Q_SCALE * eb)).astype(BF16)
            dproj_ref[b, :, OK_:OK_ + NQK] = (dki * enb + dks * ekl).astype(BF16)
            dks_ks = dks * ks
            db = dqi * qi - dki * ki - dks_ks
            dbl = jnp.sum(dks_ks, axis=0, keepdims=True) + lax.dot_general(
                ones_r, dsn * st * decb, (((1,), (1,)), ((), ())),
                preferred_element_type=F32, precision=HIGHEST)[0:1, :]
            db = db + jnp.where(last_row, dbl, 0.0)
            dla = _mm(triu, db, precision=HIGHEST)
            dz = (dla * INV_GATE_NORM) * (1.0 / (1.0 + jnp.exp(z)))
            dbg_ref[...] += jnp.sum(dz, axis=0, keepdims=True)
            dz16 = dz.astype(BF16)
            pa16 = p_ref[b, :, OA:OA + A_PAD].astype(BF16)
            dwgu_ref[...] += _tn(pa16, dz16)
            dproj_ref[b, :, OA:OA + A_PAD] = _nt(dz16, wgu_ref[...]).astype(BF16)
            cb = p_ref[b, :, OCB:OCB + CW]
            cc = p_ref[b, :, OCC:OCC + CW]
            ch = p_ref[b, :, OCH:OCH + CW]
            u = cc * ch
            uprev = (pprev_ref[b, :, 0:CW] * pprev_ref[b, :, CW:2 * CW]) * has_prev
            u1, u2 = _conv_taps(u, uprev)
            w0 = cw_ref[0:1, :]
            w1 = cw_ref[1:2, :]
            w2 = cw_ref[2:3, :]
            yc = w0 * u2 + w1 * u1 + w2 * u
            doc = dm_ref[b, :, NV:NV + CW]
            dproj_ref[b, :, OCB:OCB + CW] = (doc * yc).astype(BF16)
            dyc = doc * cb
            dycn = dycn_ref[b]
            row = lax.broadcasted_iota(jnp.int32, dyc.shape, 0)
            d1 = jnp.where(row >= CHUNK - 1, pltpu.roll(dycn, CHUNK - 1, 0), pltpu.roll(dyc, CHUNK - 1, 0))
            d2 = jnp.where(row >= CHUNK - 2, pltpu.roll(dycn, CHUNK - 2, 0), pltpu.roll(dyc, CHUNK - 2, 0))
            du = w2 * dyc + w1 * d1 + w0 * d2
            dproj_ref[b, :, OCC:OCC + CW] = (du * ch).astype(BF16)
            dproj_ref[b, :, OCH:OCH + CW] = (du * cc).astype(BF16)
            dcw_ref[0:1, :] += jnp.sum(dyc * u2, axis=0, keepdims=True)
            dcw_ref[1:2, :] += jnp.sum(dyc * u1, axis=0, keepdims=True)
            dcw_ref[2:3, :] += jnp.sum(dyc * u, axis=0, keepdims=True)
            dycn_ref[b] = dyc

    rev = lambda w: pl.BlockSpec((nb, CHUNK, w), lambda i: (0, nc - 1 - i, 0))
    const = lambda r, c: pl.BlockSpec((r, c), lambda i: (0, 0))
    return pl.pallas_call(
        body,
        name="mix_bwd",
        grid=(nc,),
        in_specs=[
            rev(PW),
            pl.BlockSpec((nb, CHUNK, 2 * CW), lambda i: (0, jnp.maximum(nc - 2 - i, 0), OCC // (2 * CW))),
            rev(NQK),
            pl.BlockSpec((nb, 1, NQK, DV), lambda i: (0, nc - 1 - i, 0, 0)),
            rev(NV),
            rev(D),
            const(1, DV),
            const(CONV_K, CW),
            const(A_PAD, NQK),
        ],
        out_specs=[rev(PW), const(1, DV), const(8, CW), const(1, NQK), const(A_PAD, NQK)],
        out_shape=[
            jax.ShapeDtypeStruct((nb, s, PW), BF16),
            jax.ShapeDtypeStruct((1, DV), F32),
            jax.ShapeDtypeStruct((8, CW), F32),
            jax.ShapeDtypeStruct((1, NQK), F32),
            jax.ShapeDtypeStruct((A_PAD, NQK), F32),
        ],
        scratch_shapes=[pltpu.VMEM((nb, NQK, DV), F32), pltpu.VMEM((nb, CHUNK, CW), F32)],
        compiler_params=_params(("arbitrary",)),
    )(proj3, proj3, z3, sprev, opre3, dmix3, gng, conv_w, wgu_p)


def _in_proj_bwd(dproj2d, x2d, dx1, g1, w_in_p, tm):
    t = x2d.shape[0]

    def body(dp_ref, x_ref, dx1_ref, g_ref, w_ref, dx_ref, dg1_ref):
        @pl.when(pl.program_id(0) == 0)
        def _():
            dg1_ref[...] = jnp.zeros_like(dg1_ref)

        x = x_ref[...]
        r = lax.rsqrt(jnp.mean(x * x, axis=-1, keepdims=True) + EPS)
        n1 = x * r
        dh = _nt(dp_ref[...], w_ref[...])
        dg1_ref[...] += jnp.sum(dh * n1, axis=0, keepdims=True)
        dn = dh * g_ref[...]
        dx_ref[...] = dx1_ref[...] + r * (dn - n1 * jnp.mean(dn * n1, axis=-1, keepdims=True))

    tile = lambda w: pl.BlockSpec((tm, w), lambda i: (i, 0))
    vec = pl.BlockSpec((1, D), lambda i: (0, 0))
    return pl.pallas_call(
        body,
        name="in_proj_bwd",
        grid=(t // tm,),
        in_specs=[tile(PW), tile(D), tile(D), vec, pl.BlockSpec((D, PW), lambda i: (0, 0))],
        out_specs=[tile(D), vec],
        out_shape=[jax.ShapeDtypeStruct((t, D), F32), jax.ShapeDtypeStruct((1, D), F32)],
        compiler_params=_params(("arbitrary",)),
    )(dproj2d, x2d, dx1, g1, w_in_p)


def _adamw_math(w, g, m, v):
    m = ADAM_B1 * m + (1.0 - ADAM_B1) * g
    v = ADAM_B2 * v + (1.0 - ADAM_B2) * (g * g)
    m_hat = m / (1.0 - ADAM_B1 ** ADAM_STEP)
    v_hat = v / (1.0 - ADAM_B2 ** ADAM_STEP)
    delta = -ADAM_LR * (m_hat / (jnp.sqrt(v_hat) + ADAM_EPS) + ADAM_WD * w)
    return delta, m, v


def _adamw_small(w, g, m, v):
    def body(w_ref, g_ref, m_ref, v_ref, d_ref, mo_ref, vo_ref):
        d, mn, vn = _adamw_math(w_ref[...], g_ref[...], m_ref[...], v_ref[...])
        d_ref[...] = d
        mo_ref[...] = mn
        vo_ref[...] = vn

    shp = jax.ShapeDtypeStruct(w.shape, F32)
    return pl.pallas_call(body, name="adamw_small", out_shape=[shp, shp, shp])(w, g, m, v)


def _local_grads(x, tgt, g1, w_in_p, wgu_p, b_gate, gng, conv_w, w_out, g2, w_g, w_u, w_d, gf):
    nb, s, _ = x.shape
    t = nb * s
    x2d = x.reshape(t, D)
    tgt2d = tgt.reshape(t, D)
    tm = 256
    proj, z, h = _in_proj_fwd(x2d, g1, w_in_p, wgu_p, b_gate, tm)
    proj3 = proj.reshape(nb, s, PW)
    z3 = z.reshape(nb, s, NQK)
    mix3, opre3, sprev = _mix_fwd(proj3, z3, gng, conv_w)
    mix2d = mix3.reshape(t, D)
    dx1, dmix, h2, act, dgate, dup, dx2, dg2, dgf, loss = _ffn_fwd_bwd(
        mix2d, x2d, tgt2d, w_out, w_g, w_u, w_d, g2, gf, tm)
    tk = min(512, t)
    dw_d = _tn_matmul(act, dx2, DFF // 2, D, tk, "dw_ffn_down")
    dw_g = _tn_matmul(h2, dgate, D, DFF // 2, tk, "dw_ffn_gate")
    dw_u = _tn_matmul(h2, dup, D, DFF // 2, tk, "dw_ffn_up")
    dw_o = _tn_matmul(mix2d, dx1, D, D, tk, "dw_out")
    dproj3, dgng, dcw, dbg, dwgu = _mix_bwd(proj3, z3, sprev, opre3, dmix.reshape(nb, s, D), gng, conv_w, wgu_p)
    dproj2d = dproj3.reshape(t, PW)
    dx, dg1 = _in_proj_bwd(dproj2d, x2d, dx1, g1, w_in_p, tm)
    dw_in_p = _tn_matmul(h, dproj2d, D // 2, PW, tk, "dw_in")
    return dict(loss=loss[0, 0], dx=dx.reshape(nb, s, D), dg1=dg1, dw_in_p=dw_in_p, dwgu=dwgu[:RANK], dbg=dbg,
                dgng=dgng, dcw=dcw[:CONV_K], dw_o=dw_o, dg2=dg2, dw_g=dw_g, dw_u=dw_u, dw_d=dw_d, dgf=dgf)


def _permute_in_cols(w):
    pad = jnp.zeros(w.shape[:-1] + (PW - IN_COLS,), w.dtype)
    return jnp.concatenate([w[..., :OCB], w[..., OCB + RANK:], w[..., OCB:OCB + RANK], pad], axis=-1)


def _unpermute_in_cols(w):
    return jnp.concatenate([w[..., :OCB], w[..., OA:OA + RANK], w[..., OCB:OA]], axis=-1)


def _position():
    return lax.axis_index("x"), lax.axis_index("y"), lax.axis_index("c")


def _all_gather_weights(wpack, spack):
    rows = wpack.shape[0]

    def body(w_ref, s_ref, gw_ref, gs_ref, stage, send_sems, recv_sems, ssend, srecv, local_sem):
        x, y, c = _position()
        me = 4 * x + 2 * y + c
        sibling = (x, y, 1 - c)
        chips = [(1 - x, y), (x, 1 - y), (1 - x, 1 - y)]

        def blk(px, py, pc):
            return gw_ref.at[4 * px + 2 * py + pc]

        def copy(k, block, to, src=None):
            return pltpu.make_async_remote_copy(
                src_ref=blk(*block) if src is None else src, dst_ref=blk(*block),
                send_sem=send_sems.at[k], recv_sem=recv_sems.at[k], device_id=to, device_id_type=MESH)

        def small(k, block_id, to):
            return pltpu.make_async_remote_copy(
                src_ref=s_ref, dst_ref=gs_ref.at[block_id], send_sem=ssend.at[k], recv_sem=srecv.at[k],
                device_id=to, device_id_type=MESH)

        stage[...] = w_ref[...].astype(BF16)
        gs_ref[me] = s_ref[...]
        mine = pltpu.make_async_copy(stage, blk(x, y, c), local_sem)
        mine.start()
        first = [copy(0, (x, y, c), sibling, src=stage)]
        first += [copy(1 + j, (x, y, c), (*chip, c), src=stage) for j, chip in enumerate(chips)]
        for cp in first:
            cp.start()
        flips = [(k >> 2, (k >> 1) & 1, k & 1) for k in range(1, N_DEV)]
        peers = [(x ^ fx, y ^ fy, c ^ fc) for fx, fy, fc in flips]
        for k, peer in enumerate(peers):
            small(k, me, peer).start()
        passed = [copy(4 + j, (*chip, c), sibling) for j, chip in enumerate(chips)]
        for j, chip in enumerate(chips):
            copy(1 + j, (*chip, c), (x, y, c)).wait_recv()
            passed[j].start()
        copy(0, sibling, (x, y, c)).wait_recv()
        for j, chip in enumerate(chips):
            copy(4 + j, (*chip, 1 - c), (x, y, c)).wait_recv()
        for cp in first + passed:
            cp.wait_send()
        for k, (px, py, pc) in enumerate(peers):
            cp = small(k, 4 * px + 2 * py + pc, (px, py, pc))
            cp.wait_recv()
            cp.wait_send()
        mine.wait()

    return pl.pallas_call(
        body,
        name="all_gather_weights",
        in_specs=[pl.BlockSpec(memory_space=pltpu.VMEM), pl.BlockSpec(memory_space=pltpu.VMEM)],
        out_specs=[pl.BlockSpec(memory_space=pl.ANY), pl.BlockSpec(memory_space=pltpu.VMEM)],
        out_shape=[jax.ShapeDtypeStruct((N_DEV, rows, 128), BF16),
                   jax.ShapeDtypeStruct((N_DEV, SMALL_W_ROWS, 128), F32)],
        scratch_shapes=[pltpu.VMEM((rows, 128), BF16), pltpu.SemaphoreType.DMA((7,)), pltpu.SemaphoreType.DMA((7,)),
                        pltpu.SemaphoreType.DMA((7,)), pltpu.SemaphoreType.DMA((7,)), pltpu.SemaphoreType.DMA],
        compiler_params=_params(),
    )(wpack, spack)


def _row_block(rows):
    for cand in (2512, 1024, 512, 256, 128, 64, 32, 16):
        if rows % cand == 0:
            return cand
    raise ValueError(rows)


def _cast_for_sibling(g8, c_arr):
    _, rows, _ = g8.shape
    rb = _row_block(rows)

    def body(c_ref, g_ref, o_ref):
        o_ref[...] = g_ref[...].astype(BF16)

    return pl.pallas_call(
        body,
        name="grad_cast_for_sibling",
        grid_spec=pltpu.PrefetchScalarGridSpec(
            num_scalar_prefetch=1, grid=(4, rows // rb),
            in_specs=[pl.BlockSpec((1, rb, 128), lambda i, r, c: (2 * i + 1 - c[0], r, 0))],
            out_specs=pl.BlockSpec((1, rb, 128), lambda i, r, c: (i, r, 0))),
        out_shape=jax.ShapeDtypeStruct((4, rows, 128), BF16),
        compiler_params=_params(("arbitrary", "arbitrary")),
    )(c_arr, g8)


def _exchange_core(gb4):
    _, rows, _ = gb4.shape

    def body(g_ref, r_ref, send_sems, recv_sems):
        x, y, c = _position()
        copies = [pltpu.make_async_remote_copy(
            src_ref=g_ref.at[i], dst_ref=r_ref.at[i], send_sem=send_sems.at[i], recv_sem=recv_sems.at[i],
            device_id=(x, y, 1 - c), device_id_type=MESH) for i in range(4)]
        for cp in copies:
            cp.start()
        for cp in copies:
            cp.wait_recv()
        for cp in copies:
            cp.wait_send()

    return pl.pallas_call(
        body,
        name="grad_exchange_core",
        in_specs=[pl.BlockSpec(memory_space=pl.ANY)],
        out_specs=pl.BlockSpec(memory_space=pl.ANY),
        out_shape=jax.ShapeDtypeStruct((4, rows, 128), BF16),
        scratch_shapes=[pltpu.SemaphoreType.DMA((4,)), pltpu.SemaphoreType.DMA((4,))],
        compiler_params=_params(),
    )(gb4)


def _add_core(g8, r1, c_arr):
    _, rows, _ = g8.shape
    rb = _row_block(rows)

    def body(c_ref, g_ref, r_ref, p_ref, pb_ref):
        p = g_ref[...] + r_ref[...].astype(F32)
        p_ref[...] = p
        pb_ref[...] = p.astype(BF16)

    blk = pl.BlockSpec((1, rb, 128), lambda i, r, c: (i, r, 0))
    return pl.pallas_call(
        body,
        name="grad_add_core",
        grid_spec=pltpu.PrefetchScalarGridSpec(
            num_scalar_prefetch=1, grid=(4, rows // rb),
            in_specs=[pl.BlockSpec((1, rb, 128), lambda i, r, c: (2 * i + c[0], r, 0)), blk],
            out_specs=[blk, blk]),
        out_shape=[jax.ShapeDtypeStruct((4, rows, 128), F32), jax.ShapeDtypeStruct((4, rows, 128), BF16)],
        compiler_params=_params(("arbitrary", "arbitrary")),
    )(c_arr, g8, r1)


def _exchange_chips(pb):
    _, rows, _ = pb.shape

    def body(p_ref, r_ref, send_sems, recv_sems):
        x, y, c = _position()
        copies = []
        for k in range(1, 4):
            tx, ty = x ^ (k >> 1), y ^ (k & 1)
            copies.append(pltpu.make_async_remote_copy(
                src_ref=p_ref.at[2 * tx + ty], dst_ref=r_ref.at[k - 1],
                send_sem=send_sems.at[k - 1], recv_sem=recv_sems.at[k - 1],
                device_id=(tx, ty, c), device_id_type=MESH))
        for cp in copies:
            cp.start()
        for cp in copies:
            cp.wait_recv()
        for cp in copies:
            cp.wait_send()

    return pl.pallas_call(
        body,
        name="grad_exchange_chips",
        in_specs=[pl.BlockSpec(memory_space=pl.ANY)],
        out_specs=pl.BlockSpec(memory_space=pl.ANY),
        out_shape=jax.ShapeDtypeStruct((3, rows, 128), BF16),
        scratch_shapes=[pltpu.SemaphoreType.DMA((3,)), pltpu.SemaphoreType.DMA((3,))],
        compiler_params=_params(),
    )(pb)


def _sum_adamw(p4, r2, wpack, mpack, vpack, chip_arr):
    _, rows, _ = p4.shape
    rb = _row_block(rows)

    def body(chip_ref, p_ref, r_ref, w_ref, m_ref, v_ref, g_out, d_out, m_out, v_out):
        g = p_ref[0]
        for k in range(3):
            g = g + r_ref[k].astype(F32)
        g_out[...] = g
        d, mn, vn = _adamw_math(w_ref[...], g, m_ref[...], v_ref[...])
        d_out[...] = d
        m_out[...] = mn
        v_out[...] = vn

    blk = pl.BlockSpec((rb, 128), lambda r, chip: (r, 0))
    shp = jax.ShapeDtypeStruct((rows, 128), F32)
    return pl.pallas_call(
        body,
        name="grad_sum_adamw",
        grid_spec=pltpu.PrefetchScalarGridSpec(
            num_scalar_prefetch=1, grid=(rows // rb,),
            in_specs=[pl.BlockSpec((1, rb, 128), lambda r, chip: (chip[0], r, 0)),
                      pl.BlockSpec((3, rb, 128), lambda r, chip: (0, r, 0)), blk, blk, blk],
            out_specs=[blk, blk, blk, blk]),
        out_shape=[shp, shp, shp, shp],
        compiler_params=_params(("arbitrary",)),
    )(chip_arr, p4, r2, wpack, mpack, vpack)


def _small_allreduce(spart):
    rows = spart.shape[0]

    def body(s_ref, o_ref, gbuf, send_sems, recv_sems):
        x, y, c = _position()
        me = 4 * x + 2 * y + c
        flips = [(k >> 2, (k >> 1) & 1, k & 1) for k in range(1, N_DEV)]
        peers = [(x ^ fx, y ^ fy, c ^ fc) for fx, fy, fc in flips]

        def copy(k, block_id, to):
            return pltpu.make_async_remote_copy(
                src_ref=s_ref, dst_ref=gbuf.at[block_id], send_sem=send_sems.at[k], recv_sem=recv_sems.at[k],
                device_id=to, device_id_type=MESH)

        for k, peer in enumerate(peers):
            copy(k, me, peer).start()
        gbuf[me] = s_ref[...]
        for k, (px, py, pc) in enumerate(peers):
            cp = copy(k, 4 * px + 2 * py + pc, (px, py, pc))
            cp.wait_recv()
            cp.wait_send()
        acc = gbuf[0]
        for d in range(1, N_DEV):
            acc = acc + gbuf[d]
        o_ref[...] = acc

    return pl.pallas_call(
        body,
        name="small_allreduce",
        in_specs=[pl.BlockSpec(memory_space=pltpu.VMEM)],
        out_specs=pl.BlockSpec(memory_space=pltpu.VMEM),
        out_shape=jax.ShapeDtypeStruct((rows, 128), F32),
        scratch_shapes=[pltpu.VMEM((N_DEV, rows, 128), F32), pltpu.SemaphoreType.DMA((7,)),
                        pltpu.SemaphoreType.DMA((7,))],
        compiler_params=_params(),
    )(spart)


def _rows128(a, rows=None):
    flat = a.reshape(-1)
    if rows is not None and flat.shape[0] < rows * 128:
        flat = jnp.concatenate([flat, jnp.zeros((rows * 128 - flat.shape[0],), flat.dtype)])
    return flat.reshape(-1, 128)


def _pack_big(w_in, w_out, w_g, w_u, w_d):
    return jnp.concatenate([_rows128(a) for a in (w_in, w_out, w_g, w_u, w_d)], axis=0)


def _unpack_big(p):
    o = 0
    outs = []
    for rows, shape in ((ROWS_IN, (1, D, IN_COLS // N_DEV)), (ROWS_OUT, (1, D // N_DEV, D)),
                        (ROWS_FF, (1, D, DFF // N_DEV)), (ROWS_FF, (1, D, DFF // N_DEV)),
                        (ROWS_FF, (1, DFF // N_DEV, D))):
        outs.append(p[o:o + rows].reshape(shape))
        o += rows
    return outs


def _cols_gathered(g, width):
    return g.reshape(N_DEV, D, width).transpose(1, 0, 2).reshape(D, N_DEV * width)


def _cols_scattered(w, width):
    return w.reshape(D, N_DEV, width).transpose(1, 0, 2).reshape(N_DEV, D * width // 128, 128)


SMALL_SPECS = (("norm1_g", 8), ("norm2_g", 8), ("norm_f_g", 8), ("b_gate", 2), ("gla_norm_g", 1),
               ("w_gate_up", 4), ("conv_w", 2))
SMALL_PACK_ROWS = 40


def _pack_small(parts):
    rows = [_rows128(parts[name], r) for name, r in SMALL_SPECS]
    used = sum(r for _, r in SMALL_SPECS)
    rows.append(jnp.zeros((SMALL_PACK_ROWS - used, 128), F32))
    return jnp.concatenate(rows, axis=0)


def _unpack_small(p, shapes):
    o = 0
    out = {}
    for name, r in SMALL_SPECS:
        n = 1
        for s in shapes[name]:
            n *= s
        out[name] = p[o:o + r].reshape(-1)[:n].reshape(shapes[name])
        o += r
    return out


def kernel(x, norm1_g, w_in, w_gate_up, b_gate, gla_norm_g, conv_w, w_out, norm2_g, w_ffn_gate, w_ffn_up, w_ffn_down, norm_f_g, loss_target, m_norm1_g, m_w_in, m_w_gate_up, m_b_gate, m_gla_norm_g, m_conv_w, m_w_out, m_norm2_g, m_w_ffn_gate, m_w_ffn_up, m_w_ffn_down, m_norm_f_g, v_norm1_g, v_w_in, v_w_gate_up, v_b_gate, v_gla_norm_g, v_conv_w, v_w_out, v_norm2_g, v_w_ffn_gate, v_w_ffn_up, v_w_ffn_down, v_norm_f_g):
    xi, yi, ci = _position()
    me = 4 * xi + 2 * yi + ci
    c_arr = jnp.reshape(ci, (1,)).astype(jnp.int32)
    chip_arr = jnp.reshape(2 * xi + yi, (1,)).astype(jnp.int32)
    wi_w = IN_COLS // N_DEV
    ff_w = DFF // N_DEV

    wpack = _pack_big(w_in, w_out, w_ffn_gate, w_ffn_up, w_ffn_down)
    spack = jnp.concatenate([_rows128(w_gate_up, 4), _rows128(conv_w, 2), jnp.zeros((2, 128), F32)], axis=0)
    gw, gs = _all_gather_weights(wpack, spack)
    o1 = ROWS_IN
    o2 = o1 + ROWS_OUT
    o3 = o2 + ROWS_FF
    o4 = o3 + ROWS_FF
    w_in_p = _permute_in_cols(_cols_gathered(gw[:, :o1], wi_w))
    w_out_f = gw[:, o1:o2].reshape(D, D)
    w_g_f = _cols_gathered(gw[:, o2:o3], ff_w)
    w_u_f = _cols_gathered(gw[:, o3:o4], ff_w)
    w_d_f = gw[:, o4:].reshape(DFF, D)
    wgu_f = gs[:, 0:4].reshape(N_DEV, RANK, NQK // N_DEV).transpose(1, 0, 2).reshape(RANK, NQK)
    conv_f = gs[:, 4:6].reshape(N_DEV, 256)[:, :CONV_K * CW // N_DEV].reshape(
        N_DEV, CONV_K, CW // N_DEV).transpose(1, 0, 2).reshape(CONV_K, CW)
    wgu_p = jnp.concatenate([wgu_f, jnp.zeros((A_PAD - RANK, NQK), F32)], axis=0).astype(BF16)

    r = _local_grads(x, loss_target, norm1_g, w_in_p, wgu_p, b_gate, gla_norm_g, conv_f, w_out_f, norm2_g,
                     w_g_f, w_u_f, w_d_f, norm_f_g.reshape(1, D))

    g8 = jnp.concatenate([
        _cols_scattered(_unpermute_in_cols(r["dw_in_p"]), wi_w),
        r["dw_o"].reshape(N_DEV, ROWS_OUT, 128),
        _cols_scattered(r["dw_g"], ff_w),
        _cols_scattered(r["dw_u"], ff_w),
        r["dw_d"].reshape(N_DEV, ROWS_FF, 128),
    ], axis=1)
    gb4 = _cast_for_sibling(g8, c_arr)
    r1 = _exchange_core(gb4)
    p4, pb = _add_core(g8, r1, c_arr)
    r2 = _exchange_chips(pb)
    mpack = _pack_big(m_w_in, m_w_out, m_w_ffn_gate, m_w_ffn_up, m_w_ffn_down)
    vpack = _pack_big(v_w_in, v_w_out, v_w_ffn_gate, v_w_ffn_up, v_w_ffn_down)
    gpack, dpack, mnew, vnew = _sum_adamw(p4, r2, wpack, mpack, vpack, chip_arr)
    big = {}
    for kind, p in (("grad", gpack), ("delta", dpack), ("m", mnew), ("v", vnew)):
        big[kind] = dict(zip(("w_in", "w_out", "w_ffn_gate", "w_ffn_up", "w_ffn_down"), _unpack_big(p)))

    spart = jnp.concatenate([
        _rows128(r["dg1"]), _rows128(r["dg2"]), _rows128(r["dgf"]), _rows128(r["dbg"]), _rows128(r["dgng"]),
        _rows128(r["dwgu"]), _rows128(r["dcw"]), jnp.full((1, 128), r["loss"], F32)], axis=0)
    ssum = _small_allreduce(spart)
    loss = ssum[SMALL_G_ROWS - 1, 0]
    dwgu_full = ssum[27:59].reshape(RANK, NQK)
    dcw_full = ssum[59:71].reshape(CONV_K, CW)
    small_g = {
        "norm1_g": ssum[0:8].reshape(1, D), "norm2_g": ssum[8:16].reshape(1, D), "norm_f_g": ssum[16:24].reshape(D),
        "b_gate": ssum[24:26].reshape(1, NQK), "gla_norm_g": ssum[26:27].reshape(1, DV),
        "w_gate_up": lax.dynamic_slice(dwgu_full, (0, (NQK // N_DEV) * me), (RANK, NQK // N_DEV))[None],
        "conv_w": lax.dynamic_slice(dcw_full, (0, (CW // N_DEV) * me), (CONV_K, CW // N_DEV))[None],
    }
    small_w = {"norm1_g": norm1_g, "norm2_g": norm2_g, "norm_f_g": norm_f_g, "b_gate": b_gate,
               "gla_norm_g": gla_norm_g, "w_gate_up": w_gate_up, "conv_w": conv_w}
    small_m = {"norm1_g": m_norm1_g, "norm2_g": m_norm2_g, "norm_f_g": m_norm_f_g, "b_gate": m_b_gate,
               "gla_norm_g": m_gla_norm_g, "w_gate_up": m_w_gate_up, "conv_w": m_conv_w}
    small_v = {"norm1_g": v_norm1_g, "norm2_g": v_norm2_g, "norm_f_g": v_norm_f_g, "b_gate": v_b_gate,
               "gla_norm_g": v_gla_norm_g, "w_gate_up": v_w_gate_up, "conv_w": v_conv_w}
    shapes = {k: v.shape for k, v in small_w.items()}
    sd, sm, sv = _adamw_small(_pack_small(small_w), _pack_small(small_g), _pack_small(small_m), _pack_small(small_v))
    small = {"grad": small_g, "delta": _unpack_small(sd, shapes), "m": _unpack_small(sm, shapes),
             "v": _unpack_small(sv, shapes)}

    names = ("norm1_g", "w_in", "w_gate_up", "b_gate", "gla_norm_g", "conv_w", "w_out", "norm2_g",
             "w_ffn_gate", "w_ffn_up", "w_ffn_down", "norm_f_g")
    outs = [loss, r["dx"]]
    for kind in ("grad", "delta", "m", "v"):
        for name in names:
            outs.append(big[kind][name] if name in big[kind] else small[kind][name])
    return tuple(outs)
```

```python
import jax
import jax.numpy as jnp
from jax import lax
from jax.experimental import pallas as pl
from jax.experimental.pallas import tpu as pltpu

F32 = jnp.float32
BF16 = jnp.bfloat16
HIGHEST = lax.Precision.HIGHEST
MESH = pl.DeviceIdType.MESH

N_DEV = 8
D = 1024
DFF = 2816
HEADS = 4
DK = 64
DV = 128
NQK = HEADS * DK
NV = HEADS * DV
RANK = 16
CHUNK = 64
CW = 512
CONV_K = 3
IN_COLS = 3088
EPS = 1e-6
INV_GATE_NORM = 1.0 / 16.0
Q_SCALE = DK ** -0.5

PW = 3200
OQ, OK_, OV, OG, OCB, OCC, OCH, OA = 0, 256, 512, 1024, 1536, 2048, 2560, 3072
A_PAD = 128

ADAM_LR = 0.001
ADAM_B1 = 0.9
ADAM_B2 = 0.999
ADAM_EPS = 1e-08
ADAM_WD = 0.01
ADAM_STEP = 10

IN_W = IN_COLS // N_DEV
IN_ROWS = 400
FF_W = DFF // N_DEV
OUT_ROWS = D // N_DEV
SLAB_IN = 0
SLAB_G = SLAB_IN + IN_ROWS
SLAB_U = SLAB_G + FF_W
SLAB_D = SLAB_U + FF_W
SLAB_O = SLAB_D + FF_W
SLAB_ROWS = SLAB_O + OUT_ROWS
SMALL_W_ROWS = 8
SMALL_G_ROWS = 72

VMEM_LIMIT = 56 * 1024 * 1024


def _params(sem=None, vmem=VMEM_LIMIT):
    return pltpu.CompilerParams(dimension_semantics=sem, vmem_limit_bytes=vmem)


def _nt(a, b):
    return lax.dot_general(a, b, (((1,), (1,)), ((), ())), preferred_element_type=F32)


def _tn(a, b, precision=None):
    return lax.dot_general(a, b, (((0,), (0,)), ((), ())), preferred_element_type=F32, precision=precision)


def _mm(a, b, precision=None):
    return jnp.dot(a, b, preferred_element_type=F32, precision=precision)


def _in_segments():
    segs = []
    for j in range(N_DEV):
        lo, hi = IN_W * j, IN_W * (j + 1)
        cuts = sorted({lo, hi} | {c for c in (OCB, OCB + RANK) if lo < c < hi})
        for a, b in zip(cuts[:-1], cuts[1:]):
            if a < OCB:
                d = a
            elif a < OCB + RANK:
                d = OA + (a - OCB)
            else:
                d = a - RANK
            segs.append((j, a - lo, b - lo, d))
    return segs


def _in_proj_fwd(x2d, g1, w_in_t, wgu_p, b_gate, tm):
    t = x2d.shape[0]

    def body(x_ref, g_ref, w_ref, wgu_ref, bg_ref, proj_ref, z_ref, h_ref):
        x = x_ref[...]
        r = lax.rsqrt(jnp.mean(x * x, axis=-1, keepdims=True) + EPS)
        h = ((x * r) * g_ref[...]).astype(BF16)
        h_ref[...] = h
        proj = _nt(h, w_ref[...])
        proj_ref[...] = proj
        pa = proj[:, OA:OA + A_PAD].astype(BF16)
        z_ref[...] = _mm(pa, wgu_ref[...]) + bg_ref[...]

    return pl.pallas_call(
        body,
        name="in_proj_fwd",
        grid=(t // tm,),
        in_specs=[
            pl.BlockSpec((tm, D), lambda i: (i, 0)),
            pl.BlockSpec((1, D), lambda i: (0, 0)),
            pl.BlockSpec((PW, D), lambda i: (0, 0)),
            pl.BlockSpec((A_PAD, NQK), lambda i: (0, 0)),
            pl.BlockSpec((1, NQK), lambda i: (0, 0)),
        ],
        out_specs=[
            pl.BlockSpec((tm, PW), lambda i: (i, 0)),
            pl.BlockSpec((tm, NQK), lambda i: (i, 0)),
            pl.BlockSpec((tm, D), lambda i: (i, 0)),
        ],
        out_shape=[
            jax.ShapeDtypeStruct((t, PW), F32),
            jax.ShapeDtypeStruct((t, NQK), F32),
            jax.ShapeDtypeStruct((t, D), BF16),
        ],
        compiler_params=_params(("arbitrary",)),
    )(x2d, g1, w_in_t, wgu_p, b_gate)


def _head_masks():
    lane = lax.broadcasted_iota(jnp.int32, (1, NQK), 1)
    return [(lane >= DK * h) & (lane < DK * (h + 1)) for h in range(HEADS)]


def _chunk_fwd_parts(q, k, z, tril):
    la = (jnp.minimum(z, 0.0) - jnp.log1p(jnp.exp(-jnp.abs(z)))) * INV_GATE_NORM
    bc = _mm(tril, la, precision=HIGHEST)
    bl = bc[CHUNK - 1:CHUNK, :]
    eb = jnp.exp(bc)
    enb = jnp.exp(-bc)
    ekl = jnp.exp(bl - bc)
    qi = (q * Q_SCALE) * eb
    ki = k * enb
    ks = k * ekl
    decb = jnp.exp(_tn(la, jnp.ones((CHUNK, DV), F32), precision=HIGHEST))
    return la, eb, enb, ekl, qi, ki, ks, decb


def _stack_heads(a, masks):
    return jnp.concatenate([jnp.where(m, a, 0.0) for m in masks], axis=0)


def _causal_stack_mask():
    row = lax.broadcasted_iota(jnp.int32, (HEADS * CHUNK, CHUNK), 0)
    col = lax.broadcasted_iota(jnp.int32, (HEADS * CHUNK, CHUNK), 1)
    return (row & (CHUNK - 1)) >= col


def _conv_taps(u, uprev):
    row = lax.broadcasted_iota(jnp.int32, u.shape, 0)
    u1 = jnp.where(row < 1, pltpu.roll(uprev, 1, 0), pltpu.roll(u, 1, 0))
    u2 = jnp.where(row < 2, pltpu.roll(uprev, 2, 0), pltpu.roll(u, 2, 0))
    return u1, u2


def _mix_fwd(proj3, z3, gng, conv_w):
    nb, s, _ = proj3.shape
    nc = s // CHUNK

    def body(p_ref, z_ref, gng_ref, cw_ref, mix_ref, o_ref, sprev_ref, s_ref, uprev_ref):
        n = pl.program_id(0)

        @pl.when(n == 0)
        def _():
            s_ref[...] = jnp.zeros_like(s_ref)
            uprev_ref[...] = jnp.zeros_like(uprev_ref)

        r_i = lax.broadcasted_iota(jnp.int32, (CHUNK, CHUNK), 0)
        c_i = lax.broadcasted_iota(jnp.int32, (CHUNK, CHUNK), 1)
        tril = (r_i >= c_i).astype(F32)
        masks = _head_masks()
        cmask = _causal_stack_mask()
        gg = gng_ref[...]
        for b in range(nb):
            q = p_ref[b, :, OQ:OQ + NQK]
            k = p_ref[b, :, OK_:OK_ + NQK]
            _, _, _, _, qi, ki, ks, decb = _chunk_fwd_parts(q, k, z_ref[b], tril)
            qs = _stack_heads(qi, masks).astype(BF16)
            sc = jnp.where(cmask, _nt(qs, ki.astype(BF16)), 0.0).astype(BF16)
            ks16 = ks.astype(BF16)
            st = s_ref[b]
            sprev_ref[b, 0] = st
            st16 = st.astype(BF16)
            for h in range(HEADS):
                rows = slice(CHUNK * h, CHUNK * (h + 1))
                cols = slice(DV * h, DV * (h + 1))
                v16 = p_ref[b, :, OV + DV * h:OV + DV * (h + 1)].astype(BF16)
                o = _mm(sc[rows], v16) + _mm(qs[rows], st16)
                o_ref[b, :, cols] = o
                r = lax.rsqrt(jnp.mean(o * o, axis=-1, keepdims=True) + EPS)
                on = (o * r) * gg
                g = p_ref[b, :, OG + DV * h:OG + DV * (h + 1)]
                mix_ref[b, :, cols] = (on * (g * jax.nn.sigmoid(g))).astype(BF16)
                kv = _tn(ks16, v16)
                s_ref[b, rows, :] = decb[rows] * st[rows] + kv[rows]
            u = p_ref[b, :, OCC:OCC + CW] * p_ref[b, :, OCH:OCH + CW]
            u1, u2 = _conv_taps(u, uprev_ref[b])
            yc = cw_ref[0:1, :] * u2 + cw_ref[1:2, :] * u1 + cw_ref[2:3, :] * u
            mix_ref[b, :, NV:NV + CW] = (p_ref[b, :, OCB:OCB + CW] * yc).astype(BF16)
            uprev_ref[b] = u

    return pl.pallas_call(
        body,
        name="mix_fwd",
        grid=(nc,),
        in_specs=[
            pl.BlockSpec((nb, CHUNK, PW), lambda n: (0, n, 0)),
            pl.BlockSpec((nb, CHUNK, NQK), lambda n: (0, n, 0)),
            pl.BlockSpec((1, DV), lambda n: (0, 0)),
            pl.BlockSpec((CONV_K, CW), lambda n: (0, 0)),
        ],
        out_specs=[
            pl.BlockSpec((nb, CHUNK, D), lambda n: (0, n, 0)),
            pl.BlockSpec((nb, CHUNK, NV), lambda n: (0, n, 0)),
            pl.BlockSpec((nb, 1, NQK, DV), lambda n: (0, n, 0, 0)),
        ],
        out_shape=[
            jax.ShapeDtypeStruct((nb, s, D), BF16),
            jax.ShapeDtypeStruct((nb, s, NV), F32),
            jax.ShapeDtypeStruct((nb, nc, NQK, DV), F32),
        ],
        scratch_shapes=[pltpu.VMEM((nb, NQK, DV), F32), pltpu.VMEM((nb, CHUNK, CW), F32)],
        compiler_params=_params(("arbitrary",)),
    )(proj3, z3, gng, conv_w)


def _ffn_fwd_bwd(mix2d, x2d, tgt2d, gw, g2, gf, tm):
    t = x2d.shape[0]

    def body(mix_ref, x_ref, tgt_ref, g2_ref, gf_ref, gw_hbm,
             dx1_ref, dmix_ref, h2_ref, act_ref, dgate_ref, dup_ref, dx2_ref, dg2_ref, dgf_ref, loss_ref,
             wo, wg, wu, wd, wsem):
        i = pl.program_id(0)

        @pl.when(i == 0)
        def _():
            copies = []
            for n, (dst, off, rows) in enumerate(((wg, SLAB_G, FF_W), (wu, SLAB_U, FF_W), (wd, SLAB_D, FF_W),
                                                  (wo, SLAB_O, OUT_ROWS))):
                for j in range(N_DEV):
                    copies.append(pltpu.make_async_copy(
                        gw_hbm.at[j, pl.ds(off, rows), :], dst.at[pl.ds(rows * j, rows), :], wsem.at[N_DEV * n + j]))
            for cp in copies:
                cp.start()
            dg2_ref[...] = jnp.zeros_like(dg2_ref)
            dgf_ref[...] = jnp.zeros_like(dgf_ref)
            loss_ref[...] = jnp.zeros_like(loss_ref)
            for cp in copies:
                cp.wait()

        g2v = g2_ref[...]
        gfv = gf_ref[...]
        x1 = x_ref[...] + _mm(mix_ref[...], wo[...])
        r2 = lax.rsqrt(jnp.mean(x1 * x1, axis=-1, keepdims=True) + EPS)
        n2 = x1 * r2
        h2 = (n2 * g2v).astype(BF16)
        h2_ref[...] = h2
        gate = _nt(h2, wg[...])
        up = _nt(h2, wu[...])
        sg = jax.nn.sigmoid(gate)
        sil = gate * sg
        act = (sil * up).astype(BF16)
        act_ref[...] = act
        x2 = x1 + _mm(act, wd[...])
        rf = lax.rsqrt(jnp.mean(x2 * x2, axis=-1, keepdims=True) + EPS)
        nf = x2 * rf
        err = nf * gfv - tgt_ref[...]
        loss_ref[...] += 0.5 * jnp.sum(jnp.mean(err * err, axis=-1, keepdims=True))
        dy = err * (1.0 / D)
        dgf_ref[...] += jnp.sum(dy * nf, axis=0, keepdims=True)
        dnf = dy * gfv
        dx2 = rf * (dnf - nf * jnp.mean(dnf * nf, axis=-1, keepdims=True))
        dx2b = dx2.astype(BF16)
        dx2_ref[...] = dx2b
        dact = _nt(dx2b, wd[...])
        dup = (dact * sil).astype(BF16)
        dgate = ((dact * up) * (sg * (1.0 + gate * (1.0 - sg)))).astype(BF16)
        dup_ref[...] = dup
        dgate_ref[...] = dgate
        dh2 = _mm(dgate, wg[...]) + _mm(dup, wu[...])
        dg2_ref[...] += jnp.sum(dh2 * n2, axis=0, keepdims=True)
        dn2 = dh2 * g2v
        dx1 = dx2 + r2 * (dn2 - n2 * jnp.mean(dn2 * n2, axis=-1, keepdims=True))
        dx1_ref[...] = dx1
        dmix_ref[...] = _nt(dx1.astype(BF16), wo[...])

    tile = lambda w: pl.BlockSpec((tm, w), lambda i: (i, 0))
    vec = pl.BlockSpec((1, D), lambda i: (0, 0))
    hbm = pl.BlockSpec(memory_space=pl.ANY)
    return pl.pallas_call(
        body,
        name="ffn_fwd_bwd",
        grid=(t // tm,),
        in_specs=[tile(D), tile(D), tile(D), vec, vec, hbm],
        out_specs=[tile(D), tile(D), tile(D), tile(DFF), tile(DFF), tile(DFF), tile(D), vec, vec,
                   pl.BlockSpec((1, 128), lambda i: (0, 0))],
        out_shape=[
            jax.ShapeDtypeStruct((t, D), F32),
            jax.ShapeDtypeStruct((t, D), F32),
            jax.ShapeDtypeStruct((t, D), BF16),
            jax.ShapeDtypeStruct((t, DFF), BF16),
            jax.ShapeDtypeStruct((t, DFF), BF16),
            jax.ShapeDtypeStruct((t, DFF), BF16),
            jax.ShapeDtypeStruct((t, D), BF16),
            jax.ShapeDtypeStruct((1, D), F32),
            jax.ShapeDtypeStruct((1, D), F32),
            jax.ShapeDtypeStruct((1, 128), F32),
        ],
        scratch_shapes=[pltpu.VMEM((D, D), BF16), pltpu.VMEM((DFF, D), BF16), pltpu.VMEM((DFF, D), BF16),
                        pltpu.VMEM((DFF, D), BF16), pltpu.SemaphoreType.DMA((4 * N_DEV,))],
        compiler_params=_params(("arbitrary",)),
    )(mix2d, x2d, tgt2d, g2, gf, gw)


def _tn_matmul(a, b, bm, bn, tk, name, with_bf16):
    t, m = a.shape
    n = b.shape[1]
    nk = t // tk

    def body(a_ref, b_ref, o_ref, *ob_ref):
        k = pl.program_id(2)

        @pl.when(k == 0)
        def _():
            o_ref[...] = jnp.zeros_like(o_ref)

        o_ref[...] += _tn(a_ref[...].astype(BF16), b_ref[...].astype(BF16))
        if with_bf16:
            @pl.when(k == nk - 1)
            def _():
                ob_ref[0][...] = o_ref[...].astype(BF16)

    out_blk = pl.BlockSpec((bm, bn), lambda i, j, k: (i, j))
    return pl.pallas_call(
        body,
        name=name,
        grid=(m // bm, n // bn, nk),
        in_specs=[pl.BlockSpec((tk, bm), lambda i, j, k: (k, i)), pl.BlockSpec((tk, bn), lambda i, j, k: (k, j))],
        out_specs=[out_blk, out_blk] if with_bf16 else out_blk,
        out_shape=([jax.ShapeDtypeStruct((m, n), F32), jax.ShapeDtypeStruct((m, n), BF16)] if with_bf16
                   else jax.ShapeDtypeStruct((m, n), F32)),
        compiler_params=_params(("parallel", "parallel", "arbitrary")),
    )(a, b)


def _mix_bwd(proj3, z3, sprev, opre3, dmix3, gng, conv_w, wgu_p):
    nb, s, _ = proj3.shape
    nc = s // CHUNK

    def body(p_ref, pprev_ref, z_ref, sp_ref, o_ref, dm_ref, gng_ref, cw_ref, wgu_ref,
             dproj_ref, dgng_ref, dcw_ref, dbg_ref, dwgu_ref, ds_ref, dycn_ref):
        step = pl.program_id(0)
        n = nc - 1 - step

        @pl.when(step == 0)
        def _():
            ds_ref[...] = jnp.zeros_like(ds_ref)
            dycn_ref[...] = jnp.zeros_like(dycn_ref)
            dgng_ref[...] = jnp.zeros_like(dgng_ref)
            dcw_ref[...] = jnp.zeros_like(dcw_ref)
            dbg_ref[...] = jnp.zeros_like(dbg_ref)
            dwgu_ref[...] = jnp.zeros_like(dwgu_ref)

        r_i = lax.broadcasted_iota(jnp.int32, (CHUNK, CHUNK), 0)
        c_i = lax.broadcasted_iota(jnp.int32, (CHUNK, CHUNK), 1)
        tril = (r_i >= c_i).astype(F32)
        triu = (r_i <= c_i).astype(F32)
        causal = r_i >= c_i
        masks = _head_masks()
        cmask = _causal_stack_mask()
        gg = gng_ref[...]
        last_row = lax.broadcasted_iota(jnp.int32, (CHUNK, NQK), 0) == CHUNK - 1
        ones_r = jnp.ones((8, DV), F32)
        has_prev = (n > 0).astype(F32)
        for b in range(nb):
            q = p_ref[b, :, OQ:OQ + NQK]
            k = p_ref[b, :, OK_:OK_ + NQK]
            z = z_ref[b]
            _, eb, enb, ekl, qi, ki, ks, decb = _chunk_fwd_parts(q, k, z, tril)
            qi16 = qi.astype(BF16)
            ki16 = ki.astype(BF16)
            ks16 = ks.astype(BF16)
            qs = _stack_heads(qi, masks).astype(BF16)
            sc = jnp.where(cmask, _nt(qs, ki16), 0.0).astype(BF16)
            st = sp_ref[b, 0]
            st16 = st.astype(BF16)
            dsn = ds_ref[b]
            dsn16 = dsn.astype(BF16)
            dqi = jnp.zeros((CHUNK, NQK), F32)
            dki = jnp.zeros((CHUNK, NQK), F32)
            dks = jnp.zeros((CHUNK, NQK), F32)
            dgng = jnp.zeros((1, DV), F32)
            for h in range(HEADS):
                rows = slice(CHUNK * h, CHUNK * (h + 1))
                cols = slice(DV * h, DV * (h + 1))
                o = o_ref[b, :, cols]
                r = lax.rsqrt(jnp.mean(o * o, axis=-1, keepdims=True) + EPS)
                nh = o * r
                g = p_ref[b, :, OG + DV * h:OG + DV * (h + 1)]
                sg = jax.nn.sigmoid(g)
                dog = dm_ref[b, :, cols]
                dproj_ref[b, :, OG + DV * h:OG + DV * (h + 1)] = (
                    (dog * (nh * gg)) * (sg * (1.0 + g * (1.0 - sg)))).astype(BF16)
                don = dog * (g * sg)
                dgng = dgng + jnp.sum(don * nh, axis=0, keepdims=True)
                dn = don * gg
                do = r * (dn - nh * jnp.mean(dn * nh, axis=-1, keepdims=True))
                do16 = do.astype(BF16)
                v16 = p_ref[b, :, OV + DV * h:OV + DV * (h + 1)].astype(BF16)
                dp16 = jnp.where(causal, _nt(do16, v16), 0.0).astype(BF16)
                ksm = jnp.where(masks[h], ks16, jnp.zeros_like(ks16))
                dv = _tn(sc[rows], do16) + _mm(ksm, dsn16)
                dproj_ref[b, :, OV + DV * h:OV + DV * (h + 1)] = dv.astype(BF16)
                dqi = dqi + jnp.where(masks[h], _mm(dp16, ki16) + _nt(do16, st16), 0.0)
                dki = dki + jnp.where(masks[h], _tn(dp16, qi16), 0.0)
                dks = dks + jnp.where(masks[h], _nt(v16, dsn16), 0.0)
                qdo = _tn(qi16, do16)
                ds_ref[b, rows, :] = decb[rows] * dsn[rows] + qdo[rows]
            dgng_ref[...] += dgng
            dproj_ref[b, :, OQ:OQ + NQK] = (dqi * (Q_SCALE * eb)).astype(BF16)
            dproj_ref[b, :, OK_:OK_ + NQK] = (dki * enb + dks * ekl).astype(BF16)
            dks_ks = dks * ks
            db = dqi * qi - dki * ki - dks_ks
            dbl = jnp.sum(dks_ks, axis=0, keepdims=True) + lax.dot_general(
                ones_r, dsn * st * decb, (((1,), (1,)), ((), ())),
                preferred_element_type=F32, precision=HIGHEST)[0:1, :]
            db = db + jnp.where(last_row, dbl, 0.0)
            dla = _mm(triu, db, precision=HIGHEST)
            dz = (dla * INV_GATE_NORM) * (1.0 / (1.0 + jnp.exp(z)))
            dbg_ref[...] += jnp.sum(dz, axis=0, keepdims=True)
            dz16 = dz.astype(BF16)
            pa16 = p_ref[b, :, OA:OA + A_PAD].astype(BF16)
            dwgu_ref[...] += _tn(pa16, dz16)
            dproj_ref[b, :, OA:OA + A_PAD] = _nt(dz16, wgu_ref[...]).astype(BF16)
            cb = p_ref[b, :, OCB:OCB + CW]
            cc = p_ref[b, :, OCC:OCC + CW]
            ch = p_ref[b, :, OCH:OCH + CW]
            u = cc * ch
            uprev = (pprev_ref[b, :, 0:CW] * pprev_ref[b, :, CW:2 * CW]) * has_prev
            u1, u2 = _conv_taps(u, uprev)
            w0 = cw_ref[0:1, :]
            w1 = cw_ref[1:2, :]
            w2 = cw_ref[2:3, :]
            yc = w0 * u2 + w1 * u1 + w2 * u
            doc = dm_ref[b, :, NV:NV + CW]
            dproj_ref[b, :, OCB:OCB + CW] = (doc * yc).astype(BF16)
            dyc = doc * cb
            dycn = dycn_ref[b]
            row = lax.broadcasted_iota(jnp.int32, dyc.shape, 0)
            d1 = jnp.where(row >= CHUNK - 1, pltpu.roll(dycn, CHUNK - 1, 0), pltpu.roll(dyc, CHUNK - 1, 0))
            d2 = jnp.where(row >= CHUNK - 2, pltpu.roll(dycn, CHUNK - 2, 0), pltpu.roll(dyc, CHUNK - 2, 0))
            du = w2 * dyc + w1 * d1 + w0 * d2
            dproj_ref[b, :, OCC:OCC + CW] = (du * ch).astype(BF16)
            dproj_ref[b, :, OCH:OCH + CW] = (du * cc).astype(BF16)
            dcw_ref[0:1, :] += jnp.sum(dyc * u2, axis=0, keepdims=True)
            dcw_ref[1:2, :] += jnp.sum(dyc * u1, axis=0, keepdims=True)
            dcw_ref[2:3, :] += jnp.sum(dyc * u, axis=0, keepdims=True)
            dycn_ref[b] = dyc

    rev = lambda w: pl.BlockSpec((nb, CHUNK, w), lambda i: (0, nc - 1 - i, 0))
    const = lambda r, c: pl.BlockSpec((r, c), lambda i: (0, 0))
    return pl.pallas_call(
        body,
        name="mix_bwd",
        grid=(nc,),
        in_specs=[
            rev(PW),
            pl.BlockSpec((nb, CHUNK, 2 * CW), lambda i: (0, jnp.maximum(nc - 2 - i, 0), OCC // (2 * CW))),
            rev(NQK),
            pl.BlockSpec((nb, 1, NQK, DV), lambda i: (0, nc - 1 - i, 0, 0)),
            rev(NV),
            rev(D),
            const(1, DV),
            const(CONV_K, CW),
            const(A_PAD, NQK),
        ],
        out_specs=[rev(PW), const(1, DV), const(8, CW), const(1, NQK), const(A_PAD, NQK)],
        out_shape=[
            jax.ShapeDtypeStruct((nb, s, PW), BF16),
            jax.ShapeDtypeStruct((1, DV), F32),
            jax.ShapeDtypeStruct((8, CW), F32),
            jax.ShapeDtypeStruct((1, NQK), F32),
            jax.ShapeDtypeStruct((A_PAD, NQK), F32),
        ],
        scratch_shapes=[pltpu.VMEM((nb, NQK, DV), F32), pltpu.VMEM((nb, CHUNK, CW), F32)],
        compiler_params=_params(("arbitrary",)),
    )(proj3, proj3, z3, sprev, opre3, dmix3, gng, conv_w, wgu_p)


def _in_proj_bwd(dproj2d, x2d, dx1, g1, w_in_t, tm):
    t = x2d.shape[0]

    def body(dp_ref, x_ref, dx1_ref, g_ref, w_ref, dx_ref, dg1_ref):
        @pl.when(pl.program_id(0) == 0)
        def _():
            dg1_ref[...] = jnp.zeros_like(dg1_ref)

        x = x_ref[...]
        r = lax.rsqrt(jnp.mean(x * x, axis=-1, keepdims=True) + EPS)
        n1 = x * r
        dh = _mm(dp_ref[...], w_ref[...])
        dg1_ref[...] += jnp.sum(dh * n1, axis=0, keepdims=True)
        dn = dh * g_ref[...]
        dx_ref[...] = dx1_ref[...] + r * (dn - n1 * jnp.mean(dn * n1, axis=-1, keepdims=True))

    tile = lambda w: pl.BlockSpec((tm, w), lambda i: (i, 0))
    vec = pl.BlockSpec((1, D), lambda i: (0, 0))
    return pl.pallas_call(
        body,
        name="in_proj_bwd",
        grid=(t // tm,),
        in_specs=[tile(PW), tile(D), tile(D), vec, pl.BlockSpec((PW, D), lambda i: (0, 0))],
        out_specs=[tile(D), vec],
        out_shape=[jax.ShapeDtypeStruct((t, D), F32), jax.ShapeDtypeStruct((1, D), F32)],
        compiler_params=_params(("arbitrary",)),
    )(dproj2d, x2d, dx1, g1, w_in_t)


def _adamw_math(w, g, m, v):
    m = ADAM_B1 * m + (1.0 - ADAM_B1) * g
    v = ADAM_B2 * v + (1.0 - ADAM_B2) * (g * g)
    m_hat = m / (1.0 - ADAM_B1 ** ADAM_STEP)
    v_hat = v / (1.0 - ADAM_B2 ** ADAM_STEP)
    delta = -ADAM_LR * (m_hat / (jnp.sqrt(v_hat) + ADAM_EPS) + ADAM_WD * w)
    return delta, m, v


def _adamw_small(w, g, m, v):
    def body(w_ref, g_ref, m_ref, v_ref, d_ref, mo_ref, vo_ref):
        d, mn, vn = _adamw_math(w_ref[...], g_ref[...], m_ref[...], v_ref[...])
        d_ref[...] = d
        mo_ref[...] = mn
        vo_ref[...] = vn

    shp = jax.ShapeDtypeStruct(w.shape, F32)
    return pl.pallas_call(body, name="adamw_small", out_shape=[shp, shp, shp])(w, g, m, v)


def _position():
    return lax.axis_index("x"), lax.axis_index("y"), lax.axis_index("c")


def _all_gather_weights(w_in, w_g, w_u, w_d, w_o, spack):
    def body(wi_ref, wg_ref, wu_ref, wd_ref, wo_ref, s_ref, gw_ref, gs_ref,
             stage, send_sems, recv_sems, ssend, srecv, local_sem):
        x, y, c = _position()
        me = 4 * x + 2 * y + c
        sibling = (x, y, 1 - c)
        chips = [(1 - x, y), (x, 1 - y), (1 - x, 1 - y)]

        def blk(px, py, pc):
            return gw_ref.at[4 * px + 2 * py + pc]

        def copy(k, block, to, src=None):
            return pltpu.make_async_remote_copy(
                src_ref=blk(*block) if src is None else src, dst_ref=blk(*block),
                send_sem=send_sems.at[k], recv_sem=recv_sems.at[k], device_id=to, device_id_type=MESH)

        def small(k, block_id, to):
            return pltpu.make_async_remote_copy(
                src_ref=s_ref, dst_ref=gs_ref.at[block_id], send_sem=ssend.at[k], recv_sem=srecv.at[k],
                device_id=to, device_id_type=MESH)

        stage[SLAB_IN:SLAB_IN + IN_W, :] = wi_ref[...].T.astype(BF16)
        stage[SLAB_IN + IN_W:SLAB_G, :] = jnp.zeros((IN_ROWS - IN_W, D), BF16)
        stage[SLAB_G:SLAB_U, :] = wg_ref[...].T.astype(BF16)
        stage[SLAB_U:SLAB_D, :] = wu_ref[...].T.astype(BF16)
        stage[SLAB_D:SLAB_O, :] = wd_ref[...].astype(BF16)
        stage[SLAB_O:SLAB_ROWS, :] = wo_ref[...].astype(BF16)
        gs_ref[me] = s_ref[...]
        mine = pltpu.make_async_copy(stage, blk(x, y, c), local_sem)
        mine.start()
        first = [copy(0, (x, y, c), sibling, src=stage)]
        first += [copy(1 + j, (x, y, c), (*chip, c), src=stage) for j, chip in enumerate(chips)]
        for cp in first:
            cp.start()
        flips = [(k >> 2, (k >> 1) & 1, k & 1) for k in range(1, N_DEV)]
        peers = [(x ^ fx, y ^ fy, c ^ fc) for fx, fy, fc in flips]
        for k, peer in enumerate(peers):
            small(k, me, peer).start()
        passed = [copy(4 + j, (*chip, c), sibling) for j, chip in enumerate(chips)]
        for j, chip in enumerate(chips):
            copy(1 + j, (*chip, c), (x, y, c)).wait_recv()
            passed[j].start()
        copy(0, sibling, (x, y, c)).wait_recv()
        for j, chip in enumerate(chips):
            copy(4 + j, (*chip, 1 - c), (x, y, c)).wait_recv()
        for cp in first + passed:
            cp.wait_send()
        for k, (px, py, pc) in enumerate(peers):
            cp = small(k, 4 * px + 2 * py + pc, (px, py, pc))
            cp.wait_recv()
            cp.wait_send()
        mine.wait()

    vm = pl.BlockSpec(memory_space=pltpu.VMEM)
    return pl.pallas_call(
        body,
        name="all_gather_weights",
        in_specs=[vm] * 6,
        out_specs=[pl.BlockSpec(memory_space=pl.ANY), vm],
        out_shape=[jax.ShapeDtypeStruct((N_DEV, SLAB_ROWS, D), BF16),
                   jax.ShapeDtypeStruct((N_DEV, SMALL_W_ROWS, 128), F32)],
        scratch_shapes=[pltpu.VMEM((SLAB_ROWS, D), BF16), pltpu.SemaphoreType.DMA((7,)),
                        pltpu.SemaphoreType.DMA((7,)), pltpu.SemaphoreType.DMA((7,)), pltpu.SemaphoreType.DMA((7,)),
                        pltpu.SemaphoreType.DMA],
        compiler_params=_params(),
    )(w_in, w_g, w_u, w_d, w_o, spack)


def _permute_w_in(gw):
    def body(gw_hbm, o_ref, buf, sems):
        copies = [pltpu.make_async_copy(gw_hbm.at[j, pl.ds(SLAB_IN, IN_ROWS), :], buf.at[j], sems.at[j])
                  for j in range(N_DEV)]
        for cp in copies:
            cp.start()
        o_ref[IN_COLS:PW, :] = jnp.zeros((PW - IN_COLS, D), BF16)
        for cp in copies:
            cp.wait()
        for j, lo, hi, d in _in_segments():
            o_ref[d:d + hi - lo, :] = buf[j, lo:hi, :]

    return pl.pallas_call(
        body,
        name="permute_w_in",
        in_specs=[pl.BlockSpec(memory_space=pl.ANY)],
        out_specs=pl.BlockSpec(memory_space=pltpu.VMEM),
        out_shape=jax.ShapeDtypeStruct((PW, D), BF16),
        scratch_shapes=[pltpu.VMEM((N_DEV, IN_ROWS, D), BF16), pltpu.SemaphoreType.DMA((N_DEV,))],
        compiler_params=_params(),
    )(gw)


def _unpermute_dw_in(dw_t):
    def body(d_ref, g_ref, gb_ref):
        for j in range(N_DEV):
            g_ref[j, IN_W:IN_ROWS, :] = jnp.zeros((IN_ROWS - IN_W, D), F32)
        for j, lo, hi, d in _in_segments():
            g_ref[j, lo:hi, :] = d_ref[d:d + hi - lo, :]
        for j in range(N_DEV):
            gb_ref[j] = g_ref[j].astype(BF16)

    vm = pl.BlockSpec(memory_space=pltpu.VMEM)
    return pl.pallas_call(
        body,
        name="unpermute_dw_in",
        in_specs=[vm],
        out_specs=[vm, vm],
        out_shape=[jax.ShapeDtypeStruct((N_DEV, IN_ROWS, D), F32), jax.ShapeDtypeStruct((N_DEV, IN_ROWS, D), BF16)],
        compiler_params=_params(),
    )(dw_t)


def _exchange_core(gbs):
    n = len(gbs)

    def body(*refs):
        g_refs, r_refs, send_sems, recv_sems = refs[:n], refs[n:2 * n], refs[2 * n], refs[2 * n + 1]
        x, y, c = _position()
        copies = [pltpu.make_async_remote_copy(
            src_ref=g_refs[a].at[2 * i + 1 - c], dst_ref=r_refs[a].at[i],
            send_sem=send_sems.at[4 * a + i], recv_sem=recv_sems.at[4 * a + i],
            device_id=(x, y, 1 - c), device_id_type=MESH) for a in range(n) for i in range(4)]
        for cp in copies:
            cp.start()
        for cp in copies:
            cp.wait_recv()
        for cp in copies:
            cp.wait_send()

    hbm = pl.BlockSpec(memory_space=pl.ANY)
    return pl.pallas_call(
        body,
        name="grad_exchange_core",
        in_specs=[hbm] * n,
        out_specs=[hbm] * n,
        out_shape=[jax.ShapeDtypeStruct((4,) + g.shape[1:], BF16) for g in gbs],
        scratch_shapes=[pltpu.SemaphoreType.DMA((4 * n,)), pltpu.SemaphoreType.DMA((4 * n,))],
        compiler_params=_params(),
    )(*gbs)


def _add_core(g8, r1, c_arr, name):
    _, rows, _ = g8.shape

    def body(c_ref, g_ref, r_ref, p_ref, pb_ref):
        p = g_ref[...] + r_ref[...].astype(F32)
        p_ref[...] = p
        pb_ref[...] = p.astype(BF16)

    blk = pl.BlockSpec((1, rows, D), lambda i, c: (i, 0, 0))
    return pl.pallas_call(
        body,
        name=name,
        grid_spec=pltpu.PrefetchScalarGridSpec(
            num_scalar_prefetch=1, grid=(4,),
            in_specs=[pl.BlockSpec((1, rows, D), lambda i, c: (2 * i + c[0], 0, 0)), blk],
            out_specs=[blk, blk]),
        out_shape=[jax.ShapeDtypeStruct((4, rows, D), F32), jax.ShapeDtypeStruct((4, rows, D), BF16)],
        compiler_params=_params(("arbitrary",)),
    )(c_arr, g8, r1)


def _exchange_chips(pbs):
    n = len(pbs)

    def body(*refs):
        p_refs, r_refs, send_sems, recv_sems = refs[:n], refs[n:2 * n], refs[2 * n], refs[2 * n + 1]
        x, y, c = _position()
        copies = []
        for a in range(n):
            for k in range(1, 4):
                tx, ty = x ^ (k >> 1), y ^ (k & 1)
                copies.append(pltpu.make_async_remote_copy(
                    src_ref=p_refs[a].at[2 * tx + ty], dst_ref=r_refs[a].at[k - 1],
                    send_sem=send_sems.at[3 * a + k - 1], recv_sem=recv_sems.at[3 * a + k - 1],
                    device_id=(tx, ty, c), device_id_type=MESH))
        for cp in copies:
            cp.start()
        for cp in copies:
            cp.wait_recv()
        for cp in copies:
            cp.wait_send()

    hbm = pl.BlockSpec(memory_space=pl.ANY)
    return pl.pallas_call(
        body,
        name="grad_exchange_chips",
        in_specs=[hbm] * n,
        out_specs=[hbm] * n,
        out_shape=[jax.ShapeDtypeStruct((3,) + p.shape[1:], BF16) for p in pbs],
        scratch_shapes=[pltpu.SemaphoreType.DMA((3 * n,)), pltpu.SemaphoreType.DMA((3 * n,))],
        compiler_params=_params(),
    )(*pbs)


def _finish_weight(p4, r2, w, m, v, chip_arr, cols, name):
    _, rows, _ = p4.shape

    def body(chip_ref, p_ref, r_ref, w_ref, m_ref, v_ref, g_out, d_out, m_out, v_out):
        g = p_ref[0]
        for k in range(3):
            g = g + r_ref[k].astype(F32)
        if cols is not None:
            g = g[0:cols, :].T
        g_out[...] = g
        d, mn, vn = _adamw_math(w_ref[...], g, m_ref[...], v_ref[...])
        d_out[...] = d
        m_out[...] = mn
        v_out[...] = vn

    full = lambda a: pl.BlockSpec(a.shape, lambda i, chip: (0,) * a.ndim)
    shp = jax.ShapeDtypeStruct(w.shape, F32)
    return pl.pallas_call(
        body,
        name=name,
        grid_spec=pltpu.PrefetchScalarGridSpec(
            num_scalar_prefetch=1, grid=(1,),
            in_specs=[pl.BlockSpec((1, rows, D), lambda i, chip: (chip[0], 0, 0)), full(r2), full(w), full(m), full(v)],
            out_specs=[full(w)] * 4),
        out_shape=[shp, shp, shp, shp],
        compiler_params=_params(("arbitrary",)),
    )(chip_arr, p4, r2, w, m, v)


def _small_allreduce(spart):
    rows = spart.shape[0]

    def body(s_ref, o_ref, gbuf, send_sems, recv_sems):
        x, y, c = _position()
        me = 4 * x + 2 * y + c
        flips = [(k >> 2, (k >> 1) & 1, k & 1) for k in range(1, N_DEV)]
        peers = [(x ^ fx, y ^ fy, c ^ fc) for fx, fy, fc in flips]

        def copy(k, block_id, to):
            return pltpu.make_async_remote_copy(
                src_ref=s_ref, dst_ref=gbuf.at[block_id], send_sem=send_sems.at[k], recv_sem=recv_sems.at[k],
                device_id=to, device_id_type=MESH)

        for k, peer in enumerate(peers):
            copy(k, me, peer).start()
        gbuf[me] = s_ref[...]
        for k, (px, py, pc) in enumerate(peers):
            cp = copy(k, 4 * px + 2 * py + pc, (px, py, pc))
            cp.wait_recv()
            cp.wait_send()
        acc = gbuf[0]
        for d in range(1, N_DEV):
            acc = acc + gbuf[d]
        o_ref[...] = acc

    return pl.pallas_call(
        body,
        name="small_allreduce",
        in_specs=[pl.BlockSpec(memory_space=pltpu.VMEM)],
        out_specs=pl.BlockSpec(memory_space=pltpu.VMEM),
        out_shape=jax.ShapeDtypeStruct((rows, 128), F32),
        scratch_shapes=[pltpu.VMEM((N_DEV, rows, 128), F32), pltpu.SemaphoreType.DMA((7,)),
                        pltpu.SemaphoreType.DMA((7,))],
        compiler_params=_params(),
    )(spart)


def _rows128(a, rows=None):
    flat = a.reshape(-1)
    if rows is not None and flat.shape[0] < rows * 128:
        flat = jnp.concatenate([flat, jnp.zeros((rows * 128 - flat.shape[0],), flat.dtype)])
    return flat.reshape(-1, 128)


SMALL_SPECS = (("norm1_g", 8), ("norm2_g", 8), ("norm_f_g", 8), ("b_gate", 2), ("gla_norm_g", 1),
               ("w_gate_up", 4), ("conv_w", 2))
SMALL_PACK_ROWS = 40


def _pack_small(parts):
    rows = [_rows128(parts[name], r) for name, r in SMALL_SPECS]
    used = sum(r for _, r in SMALL_SPECS)
    rows.append(jnp.zeros((SMALL_PACK_ROWS - used, 128), F32))
    return jnp.concatenate(rows, axis=0)


def _unpack_small(p, shapes):
    o = 0
    out = {}
    for name, r in SMALL_SPECS:
        n = 1
        for s in shapes[name]:
            n *= s
        out[name] = p[o:o + r].reshape(-1)[:n].reshape(shapes[name])
        o += r
    return out


def kernel(x, norm1_g, w_in, w_gate_up, b_gate, gla_norm_g, conv_w, w_out, norm2_g, w_ffn_gate, w_ffn_up, w_ffn_down, norm_f_g, loss_target, m_norm1_g, m_w_in, m_w_gate_up, m_b_gate, m_gla_norm_g, m_conv_w, m_w_out, m_norm2_g, m_w_ffn_gate, m_w_ffn_up, m_w_ffn_down, m_norm_f_g, v_norm1_g, v_w_in, v_w_gate_up, v_b_gate, v_gla_norm_g, v_conv_w, v_w_out, v_norm2_g, v_w_ffn_gate, v_w_ffn_up, v_w_ffn_down, v_norm_f_g):
    xi, yi, ci = _position()
    me = 4 * xi + 2 * yi + ci
    c_arr = jnp.reshape(ci, (1,)).astype(jnp.int32)
    chip_arr = jnp.reshape(2 * xi + yi, (1,)).astype(jnp.int32)
    nb, s, _ = x.shape
    t = nb * s

    spack = jnp.concatenate([_rows128(w_gate_up, 4), _rows128(conv_w, 2), jnp.zeros((2, 128), F32)], axis=0)
    gw, gs = _all_gather_weights(w_in[0], w_ffn_gate[0], w_ffn_up[0], w_ffn_down[0], w_out[0], spack)
    w_in_t = _permute_w_in(gw)
    wgu_f = gs[:, 0:4].reshape(N_DEV, RANK, NQK // N_DEV).transpose(1, 0, 2).reshape(RANK, NQK)
    conv_f = gs[:, 4:6].reshape(N_DEV, 256)[:, :CONV_K * CW // N_DEV].reshape(
        N_DEV, CONV_K, CW // N_DEV).transpose(1, 0, 2).reshape(CONV_K, CW)
    wgu_p = jnp.concatenate([wgu_f, jnp.zeros((A_PAD - RANK, NQK), F32)], axis=0).astype(BF16)

    x2d = x.reshape(t, D)
    tgt2d = loss_target.reshape(t, D)
    tm = 256
    tk = min(512, t)
    proj, z, h = _in_proj_fwd(x2d, norm1_g, w_in_t, wgu_p, b_gate, tm)
    proj3 = proj.reshape(nb, s, PW)
    z3 = z.reshape(nb, s, NQK)
    mix3, opre3, sprev = _mix_fwd(proj3, z3, gla_norm_g, conv_f)
    mix2d = mix3.reshape(t, D)
    dx1, dmix, h2, act, dgate, dup, dx2, dg2, dgf, loss_part = _ffn_fwd_bwd(
        mix2d, x2d, tgt2d, gw, norm2_g, norm_f_g.reshape(1, D), tm)
    dw_d, dwb_d = _tn_matmul(act, dx2, DFF // 2, D, tk, "dw_ffn_down", True)
    dw_g, dwb_g = _tn_matmul(dgate, h2, DFF // 2, D, tk, "dw_ffn_gate", True)
    dw_u, dwb_u = _tn_matmul(dup, h2, DFF // 2, D, tk, "dw_ffn_up", True)
    dw_o, dwb_o = _tn_matmul(mix2d, dx1, D, D, tk, "dw_out", True)
    dproj3, dgng, dcw, dbg, dwgu = _mix_bwd(proj3, z3, sprev, opre3, dmix.reshape(nb, s, D), gla_norm_g, conv_f,
                                            wgu_p)
    dproj2d = dproj3.reshape(t, PW)
    dx, dg1 = _in_proj_bwd(dproj2d, x2d, dx1, norm1_g, w_in_t, tm)
    dw_in_t = _tn_matmul(dproj2d, h, PW, D, tk, "dw_in", False)
    g_in, gb_in = _unpermute_dw_in(dw_in_t)

    by_owner = lambda a, rows: a.reshape(N_DEV, rows, D)
    g8 = [g_in, by_owner(dw_g, FF_W), by_owner(dw_u, FF_W), by_owner(dw_d, FF_W), by_owner(dw_o, OUT_ROWS)]
    gb8 = [gb_in, by_owner(dwb_g, FF_W), by_owner(dwb_u, FF_W), by_owner(dwb_d, FF_W), by_owner(dwb_o, OUT_ROWS)]
    tags = ("w_in", "w_ffn_gate", "w_ffn_up", "w_ffn_down", "w_out")
    r1 = _exchange_core(gb8)
    p4, pb = zip(*[_add_core(g, r, c_arr, "grad_add_core_" + tag) for g, r, tag in zip(g8, r1, tags)])
    r2 = _exchange_chips(list(pb))
    shard_w = (w_in, w_ffn_gate, w_ffn_up, w_ffn_down, w_out)
    shard_m = (m_w_in, m_w_ffn_gate, m_w_ffn_up, m_w_ffn_down, m_w_out)
    shard_v = (v_w_in, v_w_ffn_gate, v_w_ffn_up, v_w_ffn_down, v_w_out)
    widths = (IN_W, FF_W, FF_W, None, None)
    big = {"grad": {}, "delta": {}, "m": {}, "v": {}}
    for tag, p, r, w, m, v, cols in zip(tags, p4, r2, shard_w, shard_m, shard_v, widths):
        outs = _finish_weight(p, r, w[0], m[0], v[0], chip_arr, cols, "finish_" + tag)
        for kind, o in zip(("grad", "delta", "m", "v"), outs):
            big[kind][tag] = o[None]

    spart = jnp.concatenate([
        _rows128(dg1), _rows128(dg2), _rows128(dgf), _rows128(dbg), _rows128(dgng),
        _rows128(dwgu[:RANK]), _rows128(dcw[:CONV_K]), jnp.full((1, 128), loss_part[0, 0], F32)], axis=0)
    ssum = _small_allreduce(spart)
    loss = ssum[SMALL_G_ROWS - 1, 0]
    dwgu_full = ssum[27:59].reshape(RANK, NQK)
    dcw_full = ssum[59:71].reshape(CONV_K, CW)
    small_g = {
        "norm1_g": ssum[0:8].reshape(1, D), "norm2_g": ssum[8:16].reshape(1, D), "norm_f_g": ssum[16:24].reshape(D),
        "b_gate": ssum[24:26].reshape(1, NQK), "gla_norm_g": ssum[26:27].reshape(1, DV),
        "w_gate_up": lax.dynamic_slice(dwgu_full, (0, (NQK // N_DEV) * me), (RANK, NQK // N_DEV))[None],
        "conv_w": lax.dynamic_slice(dcw_full, (0, (CW // N_DEV) * me), (CONV_K, CW // N_DEV))[None],
    }
    small_w = {"norm1_g": norm1_g, "norm2_g": norm2_g, "norm_f_g": norm_f_g, "b_gate": b_gate,
               "gla_norm_g": gla_norm_g, "w_gate_up": w_gate_up, "conv_w": conv_w}
    small_m = {"norm1_g": m_norm1_g, "norm2_g": m_norm2_g, "norm_f_g": m_norm_f_g, "b_gate": m_b_gate,
               "gla_norm_g": m_gla_norm_g, "w_gate_up": m_w_gate_up, "conv_w": m_conv_w}
    small_v = {"norm1_g": v_norm1_g, "norm2_g": v_norm2_g, "norm_f_g": v_norm_f_g, "b_gate": v_b_gate,
               "gla_norm_g": v_gla_norm_g, "w_gate_up": v_w_gate_up, "conv_w": v_conv_w}
    shapes = {k: v.shape for k, v in small_w.items()}
    sd, sm, sv = _adamw_small(_pack_small(small_w), _pack_small(small_g), _pack_small(small_m), _pack_small(small_v))
    small = {"grad": small_g, "delta": _unpack_small(sd, shapes), "m": _unpack_small(sm, shapes),
             "v": _unpack_small(sv, shapes)}

    names = ("norm1_g", "w_in", "w_gate_up", "b_gate", "gla_norm_g", "conv_w", "w_out", "norm2_g",
             "w_ffn_gate", "w_ffn_up", "w_ffn_down", "norm_f_g")
    outs = [loss, dx.reshape(nb, s, D)]
    for kind in ("grad", "delta", "m", "v"):
        for name in names:
            outs.append(big[kind][name] if name in big[kind] else small[kind][name])
    return tuple(outs)
```

```python
import jax
import jax.numpy as jnp
from jax import lax
from jax.experimental import pallas as pl
from jax.experimental.pallas import tpu as pltpu

F32 = jnp.float32
BF16 = jnp.bfloat16
HIGHEST = lax.Precision.HIGHEST
MESH = pl.DeviceIdType.MESH

N_DEV = 8
D = 1024
DFF = 2816
HEADS = 4
DK = 64
DV = 128
NQK = HEADS * DK
NV = HEADS * DV
RANK = 16
CHUNK = 64
CW = 512
CONV_K = 3
IN_COLS = 3088
EPS = 1e-6
INV_GATE_NORM = 1.0 / 16.0
Q_SCALE = DK ** -0.5

PW = 3200
OQ, OK_, OV, OG, OCB, OCC, OCH, OA = 0, 256, 512, 1024, 1536, 2048, 2560, 3072
A_PAD = 128

ADAM_LR = 0.001
ADAM_B1 = 0.9
ADAM_B2 = 0.999
ADAM_EPS = 1e-08
ADAM_WD = 0.01
ADAM_STEP = 10

IN_W = IN_COLS // N_DEV
IN_ROWS = 400
FF_W = DFF // N_DEV
OUT_ROWS = D // N_DEV
SLAB_IN = 0
SLAB_G = SLAB_IN + IN_ROWS
SLAB_U = SLAB_G + FF_W
SLAB_D = SLAB_U + FF_W
SLAB_O = SLAB_D + FF_W
SLAB_ROWS = SLAB_O + OUT_ROWS
SMALL_W_ROWS = 8
SMALL_G_ROWS = 72

VMEM_LIMIT = 56 * 1024 * 1024


def _params(sem=None, vmem=VMEM_LIMIT):
    return pltpu.CompilerParams(dimension_semantics=sem, vmem_limit_bytes=vmem)


def _nt(a, b):
    return lax.dot_general(a, b, (((1,), (1,)), ((), ())), preferred_element_type=F32)


def _tn(a, b, precision=None):
    return lax.dot_general(a, b, (((0,), (0,)), ((), ())), preferred_element_type=F32, precision=precision)


def _mm(a, b, precision=None):
    return jnp.dot(a, b, preferred_element_type=F32, precision=precision)


def _in_segments():
    segs = []
    for j in range(N_DEV):
        lo, hi = IN_W * j, IN_W * (j + 1)
        cuts = sorted({lo, hi} | {c for c in (OCB, OCB + RANK) if lo < c < hi})
        for a, b in zip(cuts[:-1], cuts[1:]):
            if a < OCB:
                d = a
            elif a < OCB + RANK:
                d = OA + (a - OCB)
            else:
                d = a - RANK
            segs.append((j, a - lo, b - lo, d))
    return segs


def _in_proj_fwd(x2d, g1, w_in_t, wgu_p, b_gate, tm, stage):
    t = x2d.shape[0]
    nt = t // tm
    g_rows = SLAB_ROWS - SLAB_D

    def body(x_ref, g_ref, w_ref, wgu_ref, bg_ref, stage_hbm, proj_ref, z_ref, h_ref, gwb_ref,
             send_sems, recv_sems, local_sem):
        gargs = (stage_hbm, SLAB_D, g_rows, gwb_ref, send_sems, recv_sems, local_sem)

        @pl.when(pl.program_id(0) == 0)
        def _():
            _gather_start(*gargs)

        x = x_ref[...]
        r = lax.rsqrt(jnp.mean(x * x, axis=-1, keepdims=True) + EPS)
        h = ((x * r) * g_ref[...]).astype(BF16)
        h_ref[...] = h
        proj = _nt(h, w_ref[...])
        proj_ref[...] = proj
        pa = proj[:, OA:OA + A_PAD].astype(BF16)
        z_ref[...] = _mm(pa, wgu_ref[...]) + bg_ref[...]

        @pl.when(pl.program_id(0) == nt - 1)
        def _():
            _gather_finish(*gargs)

    return pl.pallas_call(
        body,
        name="in_proj_fwd",
        grid=(t // tm,),
        in_specs=[
            pl.BlockSpec((tm, D), lambda i: (i, 0)),
            pl.BlockSpec((1, D), lambda i: (0, 0)),
            pl.BlockSpec((PW, D), lambda i: (0, 0)),
            pl.BlockSpec((A_PAD, NQK), lambda i: (0, 0)),
            pl.BlockSpec((1, NQK), lambda i: (0, 0)),
            pl.BlockSpec(memory_space=pl.ANY),
        ],
        out_specs=[
            pl.BlockSpec((tm, PW), lambda i: (i, 0)),
            pl.BlockSpec((tm, NQK), lambda i: (i, 0)),
            pl.BlockSpec((tm, D), lambda i: (i, 0)),
            pl.BlockSpec(memory_space=pl.ANY),
        ],
        out_shape=[
            jax.ShapeDtypeStruct((t, PW), F32),
            jax.ShapeDtypeStruct((t, NQK), F32),
            jax.ShapeDtypeStruct((t, D), BF16),
            jax.ShapeDtypeStruct((N_DEV, g_rows, D), BF16),
        ],
        scratch_shapes=_gather_sems(),
        compiler_params=_params(("arbitrary",)),
    )(x2d, g1, w_in_t, wgu_p, b_gate, stage)


def _head_masks():
    lane = lax.broadcasted_iota(jnp.int32, (1, NQK), 1)
    return [(lane >= DK * h) & (lane < DK * (h + 1)) for h in range(HEADS)]


def _chunk_fwd_parts(q, k, z, tril):
    la = (jnp.minimum(z, 0.0) - jnp.log1p(jnp.exp(-jnp.abs(z)))) * INV_GATE_NORM
    bc = _mm(tril, la, precision=HIGHEST)
    bl = bc[CHUNK - 1:CHUNK, :]
    eb = jnp.exp(bc)
    enb = jnp.exp(-bc)
    ekl = jnp.exp(bl - bc)
    qi = (q * Q_SCALE) * eb
    ki = k * enb
    ks = k * ekl
    decb = jnp.exp(_tn(la, jnp.ones((CHUNK, DV), F32), precision=HIGHEST))
    return la, eb, enb, ekl, qi, ki, ks, decb


def _stack_heads(a, masks):
    return jnp.concatenate([jnp.where(m, a, 0.0) for m in masks], axis=0)


def _causal_stack_mask():
    row = lax.broadcasted_iota(jnp.int32, (HEADS * CHUNK, CHUNK), 0)
    col = lax.broadcasted_iota(jnp.int32, (HEADS * CHUNK, CHUNK), 1)
    return (row & (CHUNK - 1)) >= col


def _conv_taps(u, uprev):
    row = lax.broadcasted_iota(jnp.int32, u.shape, 0)
    u1 = jnp.where(row < 1, pltpu.roll(uprev, 1, 0), pltpu.roll(u, 1, 0))
    u2 = jnp.where(row < 2, pltpu.roll(uprev, 2, 0), pltpu.roll(u, 2, 0))
    return u1, u2


def _mix_fwd(proj3, z3, gng, conv_w, stage):
    nb, s, _ = proj3.shape
    nc = s // CHUNK
    g_rows = SLAB_D - SLAB_G

    def body(p_ref, z_ref, gng_ref, cw_ref, stage_hbm, mix_ref, o_ref, sprev_ref, gwa_ref, s_ref, uprev_ref,
             send_sems, recv_sems, local_sem):
        n = pl.program_id(0)
        gargs = (stage_hbm, SLAB_G, g_rows, gwa_ref, send_sems, recv_sems, local_sem)

        @pl.when(n == 0)
        def _():
            _gather_start(*gargs)
            s_ref[...] = jnp.zeros_like(s_ref)
            uprev_ref[...] = jnp.zeros_like(uprev_ref)

        r_i = lax.broadcasted_iota(jnp.int32, (CHUNK, CHUNK), 0)
        c_i = lax.broadcasted_iota(jnp.int32, (CHUNK, CHUNK), 1)
        tril = (r_i >= c_i).astype(F32)
        masks = _head_masks()
        cmask = _causal_stack_mask()
        gg = gng_ref[...]
        for b in range(nb):
            q = p_ref[b, :, OQ:OQ + NQK]
            k = p_ref[b, :, OK_:OK_ + NQK]
            _, _, _, _, qi, ki, ks, decb = _chunk_fwd_parts(q, k, z_ref[b], tril)
            qs = _stack_heads(qi, masks).astype(BF16)
            sc = jnp.where(cmask, _nt(qs, ki.astype(BF16)), 0.0).astype(BF16)
            ks16 = ks.astype(BF16)
            st = s_ref[b]
            sprev_ref[b, 0] = st
            st16 = st.astype(BF16)
            for h in range(HEADS):
                rows = slice(CHUNK * h, CHUNK * (h + 1))
                cols = slice(DV * h, DV * (h + 1))
                v16 = p_ref[b, :, OV + DV * h:OV + DV * (h + 1)].astype(BF16)
                o = _mm(sc[rows], v16) + _mm(qs[rows], st16)
                o_ref[b, :, cols] = o
                r = lax.rsqrt(jnp.mean(o * o, axis=-1, keepdims=True) + EPS)
                on = (o * r) * gg
                g = p_ref[b, :, OG + DV * h:OG + DV * (h + 1)]
                mix_ref[b, :, cols] = (on * (g * jax.nn.sigmoid(g))).astype(BF16)
                kv = _tn(ks16, v16)
                s_ref[b, rows, :] = decb[rows] * st[rows] + kv[rows]
            u = p_ref[b, :, OCC:OCC + CW] * p_ref[b, :, OCH:OCH + CW]
            u1, u2 = _conv_taps(u, uprev_ref[b])
            yc = cw_ref[0:1, :] * u2 + cw_ref[1:2, :] * u1 + cw_ref[2:3, :] * u
            mix_ref[b, :, NV:NV + CW] = (p_ref[b, :, OCB:OCB + CW] * yc).astype(BF16)
            uprev_ref[b] = u

        @pl.when(n == nc - 1)
        def _():
            _gather_finish(*gargs)

    return pl.pallas_call(
        body,
        name="mix_fwd",
        grid=(nc,),
        in_specs=[
            pl.BlockSpec((nb, CHUNK, PW), lambda n: (0, n, 0)),
            pl.BlockSpec((nb, CHUNK, NQK), lambda n: (0, n, 0)),
            pl.BlockSpec((1, DV), lambda n: (0, 0)),
            pl.BlockSpec((CONV_K, CW), lambda n: (0, 0)),
            pl.BlockSpec(memory_space=pl.ANY),
        ],
        out_specs=[
            pl.BlockSpec((nb, CHUNK, D), lambda n: (0, n, 0)),
            pl.BlockSpec((nb, CHUNK, NV), lambda n: (0, n, 0)),
            pl.BlockSpec((nb, 1, NQK, DV), lambda n: (0, n, 0, 0)),
            pl.BlockSpec(memory_space=pl.ANY),
        ],
        out_shape=[
            jax.ShapeDtypeStruct((nb, s, D), BF16),
            jax.ShapeDtypeStruct((nb, s, NV), F32),
            jax.ShapeDtypeStruct((nb, nc, NQK, DV), F32),
            jax.ShapeDtypeStruct((N_DEV, g_rows, D), BF16),
        ],
        scratch_shapes=[pltpu.VMEM((nb, NQK, DV), F32), pltpu.VMEM((nb, CHUNK, CW), F32)] + _gather_sems(),
        compiler_params=_params(("arbitrary",)),
    )(proj3, z3, gng, conv_w, stage)


def _ffn_fwd_bwd(mix2d, x2d, tgt2d, gwa, gwb, g2, gf, tm):
    t = x2d.shape[0]

    def body(mix_ref, x_ref, tgt_ref, g2_ref, gf_ref, gwa_hbm, gwb_hbm,
             dx1_ref, dmix_ref, h2_ref, act_ref, dgate_ref, dup_ref, dx2_ref, dg2_ref, dgf_ref, loss_ref,
             wo, wg, wu, wd, wsem):
        i = pl.program_id(0)

        @pl.when(i == 0)
        def _():
            copies = []
            for n, (dst, src, off, rows) in enumerate(((wg, gwa_hbm, 0, FF_W), (wu, gwa_hbm, FF_W, FF_W),
                                                       (wd, gwb_hbm, 0, FF_W), (wo, gwb_hbm, FF_W, OUT_ROWS))):
                for j in range(N_DEV):
                    copies.append(pltpu.make_async_copy(
                        src.at[j, pl.ds(off, rows), :], dst.at[pl.ds(rows * j, rows), :], wsem.at[N_DEV * n + j]))
            for cp in copies:
                cp.start()
            dg2_ref[...] = jnp.zeros_like(dg2_ref)
            dgf_ref[...] = jnp.zeros_like(dgf_ref)
            loss_ref[...] = jnp.zeros_like(loss_ref)
            for cp in copies:
                cp.wait()

        g2v = g2_ref[...]
        gfv = gf_ref[...]
        x1 = x_ref[...] + _mm(mix_ref[...], wo[...])
        r2 = lax.rsqrt(jnp.mean(x1 * x1, axis=-1, keepdims=True) + EPS)
        n2 = x1 * r2
        h2 = (n2 * g2v).astype(BF16)
        h2_ref[...] = h2
        gate = _nt(h2, wg[...])
        up = _nt(h2, wu[...])
        sg = jax.nn.sigmoid(gate)
        sil = gate * sg
        act = (sil * up).astype(BF16)
        act_ref[...] = act
        x2 = x1 + _mm(act, wd[...])
        rf = lax.rsqrt(jnp.mean(x2 * x2, axis=-1, keepdims=True) + EPS)
        nf = x2 * rf
        err = nf * gfv - tgt_ref[...]
        loss_ref[...] += 0.5 * jnp.sum(jnp.mean(err * err, axis=-1, keepdims=True))
        dy = err * (1.0 / D)
        dgf_ref[...] += jnp.sum(dy * nf, axis=0, keepdims=True)
        dnf = dy * gfv
        dx2 = rf * (dnf - nf * jnp.mean(dnf * nf, axis=-1, keepdims=True))
        dx2b = dx2.astype(BF16)
        dx2_ref[...] = dx2b
        dact = _nt(dx2b, wd[...])
        dup = (dact * sil).astype(BF16)
        dgate = ((dact * up) * (sg * (1.0 + gate * (1.0 - sg)))).astype(BF16)
        dup_ref[...] = dup
        dgate_ref[...] = dgate
        dh2 = _mm(dgate, wg[...]) + _mm(dup, wu[...])
        dg2_ref[...] += jnp.sum(dh2 * n2, axis=0, keepdims=True)
        dn2 = dh2 * g2v
        dx1 = dx2 + r2 * (dn2 - n2 * jnp.mean(dn2 * n2, axis=-1, keepdims=True))
        dx1_ref[...] = dx1
        dmix_ref[...] = _nt(dx1.astype(BF16), wo[...])

    tile = lambda w: pl.BlockSpec((tm, w), lambda i: (i, 0))
    vec = pl.BlockSpec((1, D), lambda i: (0, 0))
    hbm = pl.BlockSpec(memory_space=pl.ANY)
    return pl.pallas_call(
        body,
        name="ffn_fwd_bwd",
        grid=(t // tm,),
        in_specs=[tile(D), tile(D), tile(D), vec, vec, hbm, hbm],
        out_specs=[tile(D), tile(D), tile(D), tile(DFF), tile(DFF), tile(DFF), tile(D), vec, vec,
                   pl.BlockSpec((1, 128), lambda i: (0, 0))],
        out_shape=[
            jax.ShapeDtypeStruct((t, D), F32),
            jax.ShapeDtypeStruct((t, D), F32),
            jax.ShapeDtypeStruct((t, D), BF16),
            jax.ShapeDtypeStruct((t, DFF), BF16),
            jax.ShapeDtypeStruct((t, DFF), BF16),
            jax.ShapeDtypeStruct((t, DFF), BF16),
            jax.ShapeDtypeStruct((t, D), BF16),
            jax.ShapeDtypeStruct((1, D), F32),
            jax.ShapeDtypeStruct((1, D), F32),
            jax.ShapeDtypeStruct((1, 128), F32),
        ],
        scratch_shapes=[pltpu.VMEM((D, D), BF16), pltpu.VMEM((DFF, D), BF16), pltpu.VMEM((DFF, D), BF16),
                        pltpu.VMEM((DFF, D), BF16), pltpu.SemaphoreType.DMA((4 * N_DEV,))],
        compiler_params=_params(("arbitrary",)),
    )(mix2d, x2d, tgt2d, g2, gf, gwa, gwb)


def _tn_matmul(a, b, bm, bn, tk, name, with_bf16):
    t, m = a.shape
    n = b.shape[1]
    nk = t // tk

    def body(a_ref, b_ref, o_ref, *ob_ref):
        k = pl.program_id(2)

        @pl.when(k == 0)
        def _():
            o_ref[...] = jnp.zeros_like(o_ref)

        o_ref[...] += _tn(a_ref[...].astype(BF16), b_ref[...].astype(BF16))
        if with_bf16:
            @pl.when(k == nk - 1)
            def _():
                ob_ref[0][...] = o_ref[...].astype(BF16)

    out_blk = pl.BlockSpec((bm, bn), lambda i, j, k: (i, j))
    return pl.pallas_call(
        body,
        name=name,
        grid=(m // bm, n // bn, nk),
        in_specs=[pl.BlockSpec((tk, bm), lambda i, j, k: (k, i)), pl.BlockSpec((tk, bn), lambda i, j, k: (k, j))],
        out_specs=[out_blk, out_blk] if with_bf16 else out_blk,
        out_shape=([jax.ShapeDtypeStruct((m, n), F32), jax.ShapeDtypeStruct((m, n), BF16)] if with_bf16
                   else jax.ShapeDtypeStruct((m, n), F32)),
        compiler_params=_params(("parallel", "parallel", "arbitrary")),
    )(a, b)


def _mix_bwd(proj3, z3, sprev, opre3, dmix3, gng, conv_w, wgu_p, pbs):
    nb, s, _ = proj3.shape
    nc = s // CHUNK
    na = len(pbs)

    def body(*refs):
        (p_ref, pprev_ref, z_ref, sp_ref, o_ref, dm_ref, gng_ref, cw_ref, wgu_ref) = refs[:9]
        pb_refs = refs[9:9 + na]
        (dproj_ref, dgng_ref, dcw_ref, dbg_ref, dwgu_ref) = refs[9 + na:14 + na]
        r2_refs = refs[14 + na:14 + 2 * na]
        ds_ref, dycn_ref, send_sems, recv_sems = refs[14 + 2 * na:]
        step = pl.program_id(0)
        n = nc - 1 - step

        @pl.when(step == 0)
        def _():
            for cp in _stage2_copies(pb_refs, r2_refs, send_sems, recv_sems):
                cp.start()
            ds_ref[...] = jnp.zeros_like(ds_ref)
            dycn_ref[...] = jnp.zeros_like(dycn_ref)
            dgng_ref[...] = jnp.zeros_like(dgng_ref)
            dcw_ref[...] = jnp.zeros_like(dcw_ref)
            dbg_ref[...] = jnp.zeros_like(dbg_ref)
            dwgu_ref[...] = jnp.zeros_like(dwgu_ref)

        r_i = lax.broadcasted_iota(jnp.int32, (CHUNK, CHUNK), 0)
        c_i = lax.broadcasted_iota(jnp.int32, (CHUNK, CHUNK), 1)
        tril = (r_i >= c_i).astype(F32)
        triu = (r_i <= c_i).astype(F32)
        causal = r_i >= c_i
        masks = _head_masks()
        cmask = _causal_stack_mask()
        gg = gng_ref[...]
        last_row = lax.broadcasted_iota(jnp.int32, (CHUNK, NQK), 0) == CHUNK - 1
        ones_r = jnp.ones((8, DV), F32)
        has_prev = (n > 0).astype(F32)
        for b in range(nb):
            q = p_ref[b, :, OQ:OQ + NQK]
            k = p_ref[b, :, OK_:OK_ + NQK]
            z = z_ref[b]
            _, eb, enb, ekl, qi, ki, ks, decb = _chunk_fwd_parts(q, k, z, tril)
            qi16 = qi.astype(BF16)
            ki16 = ki.astype(BF16)
            ks16 = ks.astype(BF16)
            qs = _stack_heads(qi, masks).astype(BF16)
            sc = jnp.where(cmask, _nt(qs, ki16), 0.0).astype(BF16)
            st = sp_ref[b, 0]
            st16 = st.astype(BF16)
            dsn = ds_ref[b]
            dsn16 = dsn.astype(BF16)
            dqi = jnp.zeros((CHUNK, NQK), F32)
            dki = jnp.zeros((CHUNK, NQK), F32)
            dks = jnp.zeros((CHUNK, NQK), F32)
            dgng = jnp.zeros((1, DV), F32)
            for h in range(HEADS):
                rows = slice(CHUNK * h, CHUNK * (h + 1))
                cols = slice(DV * h, DV * (h + 1))
                o = o_ref[b, :, cols]
                r = lax.rsqrt(jnp.mean(o * o, axis=-1, keepdims=True) + EPS)
                nh = o * r
                g = p_ref[b, :, OG + DV * h:OG + DV * (h + 1)]
                sg = jax.nn.sigmoid(g)
                dog = dm_ref[b, :, cols]
                dproj_ref[b, :, OG + DV * h:OG + DV * (h + 1)] = (
                    (dog * (nh * gg)) * (sg * (1.0 + g * (1.0 - sg)))).astype(BF16)
                don = dog * (g * sg)
                dgng = dgng + jnp.sum(don * nh, axis=0, keepdims=True)
                dn = don * gg
                do = r * (dn - nh * jnp.mean(dn * nh, axis=-1, keepdims=True))
                do16 = do.astype(BF16)
                v16 = p_ref[b, :, OV + DV * h:OV + DV * (h + 1)].astype(BF16)
                dp16 = jnp.where(causal, _nt(do16, v16), 0.0).astype(BF16)
                ksm = jnp.where(masks[h], ks16, jnp.zeros_like(ks16))
                dv = _tn(sc[rows], do16) + _mm(ksm, dsn16)
                dproj_ref[b, :, OV + DV * h:OV + DV * (h + 1)] = dv.astype(BF16)
                dqi = dqi + jnp.where(masks[h], _mm(dp16, ki16) + _nt(do16, st16), 0.0)
                dki = dki + jnp.where(masks[h], _tn(dp16, qi16), 0.0)
                dks = dks + jnp.where(masks[h], _nt(v16, dsn16), 0.0)
                qdo = _tn(qi16, do16)
                ds_ref[b, rows, :] = decb[rows] * dsn[rows] + qdo[rows]
            dgng_ref[...] += dgng
            dproj_ref[b, :, OQ:OQ + NQK] = (dqi * (Q_SCALE * eb)).astype(BF16)
            dproj_ref[b, :, OK_:OK_ + NQK] = (dki * enb + dks * ekl).astype(BF16)
            dks_ks = dks * ks
            db = dqi * qi - dki * ki - dks_ks
            dbl = jnp.sum(dks_ks, axis=0, keepdims=True) + lax.dot_general(
                ones_r, dsn * st * decb, (((1,), (1,)), ((), ())),
                preferred_element_type=F32, precision=HIGHEST)[0:1, :]
            db = db + jnp.where(last_row, dbl, 0.0)
            dla = _mm(triu, db, precision=HIGHEST)
            dz = (dla * INV_GATE_NORM) * (1.0 / (1.0 + jnp.exp(z)))
            dbg_ref[...] += jnp.sum(dz, axis=0, keepdims=True)
            dz16 = dz.astype(BF16)
            pa16 = p_ref[b, :, OA:OA + A_PAD].astype(BF16)
            dwgu_ref[...] += _tn(pa16, dz16)
            dproj_ref[b, :, OA:OA + A_PAD] = _nt(dz16, wgu_ref[...]).astype(BF16)
            cb = p_ref[b, :, OCB:OCB + CW]
            cc = p_ref[b, :, OCC:OCC + CW]
            ch = p_ref[b, :, OCH:OCH + CW]
            u = cc * ch
            uprev = (pprev_ref[b, :, 0:CW] * pprev_ref[b, :, CW:2 * CW]) * has_prev
            u1, u2 = _conv_taps(u, uprev)
            w0 = cw_ref[0:1, :]
            w1 = cw_ref[1:2, :]
            w2 = cw_ref[2:3, :]
            yc = w0 * u2 + w1 * u1 + w2 * u
            doc = dm_ref[b, :, NV:NV + CW]
            dproj_ref[b, :, OCB:OCB + CW] = (doc * yc).astype(BF16)
            dyc = doc * cb
            dycn = dycn_ref[b]
            row = lax.broadcasted_iota(jnp.int32, dyc.shape, 0)
            d1 = jnp.where(row >= CHUNK - 1, pltpu.roll(dycn, CHUNK - 1, 0), pltpu.roll(dyc, CHUNK - 1, 0))
            d2 = jnp.where(row >= CHUNK - 2, pltpu.roll(dycn, CHUNK - 2, 0), pltpu.roll(dyc, CHUNK - 2, 0))
            du = w2 * dyc + w1 * d1 + w0 * d2
            dproj_ref[b, :, OCC:OCC + CW] = (du * ch).astype(BF16)
            dproj_ref[b, :, OCH:OCH + CW] = (du * cc).astype(BF16)
            dcw_ref[0:1, :] += jnp.sum(dyc * u2, axis=0, keepdims=True)
            dcw_ref[1:2, :] += jnp.sum(dyc * u1, axis=0, keepdims=True)
            dcw_ref[2:3, :] += jnp.sum(dyc * u, axis=0, keepdims=True)
            dycn_ref[b] = dyc

        @pl.when(step == nc - 1)
        def _():
            copies = _stage2_copies(pb_refs, r2_refs, send_sems, recv_sems)
            for cp in copies:
                cp.wait_recv()
            for cp in copies:
                cp.wait_send()

    rev = lambda w: pl.BlockSpec((nb, CHUNK, w), lambda i: (0, nc - 1 - i, 0))
    const = lambda r, c: pl.BlockSpec((r, c), lambda i: (0, 0))
    hbm = pl.BlockSpec(memory_space=pl.ANY)
    return pl.pallas_call(
        body,
        name="mix_bwd",
        grid=(nc,),
        in_specs=[
            rev(PW),
            pl.BlockSpec((nb, CHUNK, 2 * CW), lambda i: (0, jnp.maximum(nc - 2 - i, 0), OCC // (2 * CW))),
            rev(NQK),
            pl.BlockSpec((nb, 1, NQK, DV), lambda i: (0, nc - 1 - i, 0, 0)),
            rev(NV),
            rev(D),
            const(1, DV),
            const(CONV_K, CW),
            const(A_PAD, NQK),
        ] + [hbm] * na,
        out_specs=[rev(PW), const(1, DV), const(8, CW), const(1, NQK), const(A_PAD, NQK)] + [hbm] * na,
        out_shape=[
            jax.ShapeDtypeStruct((nb, s, PW), BF16),
            jax.ShapeDtypeStruct((1, DV), F32),
            jax.ShapeDtypeStruct((8, CW), F32),
            jax.ShapeDtypeStruct((1, NQK), F32),
            jax.ShapeDtypeStruct((A_PAD, NQK), F32),
        ] + [jax.ShapeDtypeStruct((3,) + p.shape[1:], BF16) for p in pbs],
        scratch_shapes=[pltpu.VMEM((nb, NQK, DV), F32), pltpu.VMEM((nb, CHUNK, CW), F32),
                        pltpu.SemaphoreType.DMA((3 * na,)), pltpu.SemaphoreType.DMA((3 * na,))],
        compiler_params=_params(("arbitrary",)),
    )(proj3, proj3, z3, sprev, opre3, dmix3, gng, conv_w, wgu_p, *pbs)


def _in_proj_bwd(dproj2d, x2d, dx1, g1, w_in_t, tm, pb):
    t = x2d.shape[0]
    nt = t // tm

    def body(dp_ref, x_ref, dx1_ref, g_ref, w_ref, pb_ref, dx_ref, dg1_ref, r2_ref, send_sems, recv_sems):
        @pl.when(pl.program_id(0) == 0)
        def _():
            for cp in _stage2_copies([pb_ref], [r2_ref], send_sems, recv_sems):
                cp.start()
            dg1_ref[...] = jnp.zeros_like(dg1_ref)

        x = x_ref[...]
        r = lax.rsqrt(jnp.mean(x * x, axis=-1, keepdims=True) + EPS)
        n1 = x * r
        dh = _mm(dp_ref[...], w_ref[...])
        dg1_ref[...] += jnp.sum(dh * n1, axis=0, keepdims=True)
        dn = dh * g_ref[...]
        dx_ref[...] = dx1_ref[...] + r * (dn - n1 * jnp.mean(dn * n1, axis=-1, keepdims=True))

        @pl.when(pl.program_id(0) == nt - 1)
        def _():
            copies = _stage2_copies([pb_ref], [r2_ref], send_sems, recv_sems)
            for cp in copies:
                cp.wait_recv()
            for cp in copies:
                cp.wait_send()

    tile = lambda w: pl.BlockSpec((tm, w), lambda i: (i, 0))
    vec = pl.BlockSpec((1, D), lambda i: (0, 0))
    hbm = pl.BlockSpec(memory_space=pl.ANY)
    return pl.pallas_call(
        body,
        name="in_proj_bwd",
        grid=(nt,),
        in_specs=[tile(PW), tile(D), tile(D), vec, pl.BlockSpec((PW, D), lambda i: (0, 0)), hbm],
        out_specs=[tile(D), vec, hbm],
        out_shape=[jax.ShapeDtypeStruct((t, D), F32), jax.ShapeDtypeStruct((1, D), F32),
                   jax.ShapeDtypeStruct((3,) + pb.shape[1:], BF16)],
        scratch_shapes=[pltpu.SemaphoreType.DMA((3,)), pltpu.SemaphoreType.DMA((3,))],
        compiler_params=_params(("arbitrary",)),
    )(dproj2d, x2d, dx1, g1, w_in_t, pb)


def _adamw_math(w, g, m, v):
    m = ADAM_B1 * m + (1.0 - ADAM_B1) * g
    v = ADAM_B2 * v + (1.0 - ADAM_B2) * (g * g)
    m_hat = m / (1.0 - ADAM_B1 ** ADAM_STEP)
    v_hat = v / (1.0 - ADAM_B2 ** ADAM_STEP)
    delta = -ADAM_LR * (m_hat / (jnp.sqrt(v_hat) + ADAM_EPS) + ADAM_WD * w)
    return delta, m, v


def _adamw_small(w, g, m, v):
    def body(w_ref, g_ref, m_ref, v_ref, d_ref, mo_ref, vo_ref):
        d, mn, vn = _adamw_math(w_ref[...], g_ref[...], m_ref[...], v_ref[...])
        d_ref[...] = d
        mo_ref[...] = mn
        vo_ref[...] = vn

    shp = jax.ShapeDtypeStruct(w.shape, F32)
    return pl.pallas_call(body, name="adamw_small", out_shape=[shp, shp, shp])(w, g, m, v)


def _position():
    return lax.axis_index("x"), lax.axis_index("y"), lax.axis_index("c")


def _prep_slab(w_in, w_gt, w_ut, w_d, w_o):
    def body(wi_ref, wg_ref, wu_ref, wd_ref, wo_ref, stage):
        stage[SLAB_IN:SLAB_IN + IN_W, :] = wi_ref[...].T.astype(BF16)
        stage[SLAB_IN + IN_W:SLAB_G, :] = jnp.zeros((IN_ROWS - IN_W, D), BF16)
        stage[SLAB_G:SLAB_U, :] = wg_ref[...].astype(BF16)
        stage[SLAB_U:SLAB_D, :] = wu_ref[...].astype(BF16)
        stage[SLAB_D:SLAB_O, :] = wd_ref[...].astype(BF16)
        stage[SLAB_O:SLAB_ROWS, :] = wo_ref[...].astype(BF16)

    vm = pl.BlockSpec(memory_space=pltpu.VMEM)
    return pl.pallas_call(
        body,
        name="prep_slab",
        in_specs=[vm] * 5,
        out_specs=vm,
        out_shape=jax.ShapeDtypeStruct((SLAB_ROWS, D), BF16),
        compiler_params=_params(),
    )(w_in, w_gt, w_ut, w_d, w_o)


GATHER_SEMS = 7


def _gather_copies(stage, lo, rows, gx, send_sems, recv_sems, local_sem):
    x, y, c = _position()
    me = (x, y, c)
    sibling = (x, y, 1 - c)
    chips = [(1 - x, y), (x, 1 - y), (1 - x, 1 - y)]
    src = stage.at[pl.ds(lo, rows), :]

    def blk(px, py, pc):
        return gx.at[4 * px + 2 * py + pc]

    def copy(k, block, to, from_stage=False):
        return pltpu.make_async_remote_copy(
            src_ref=src if from_stage else blk(*block), dst_ref=blk(*block),
            send_sem=send_sems.at[k], recv_sem=recv_sems.at[k], device_id=to, device_id_type=MESH)

    mine = pltpu.make_async_copy(src, blk(*me), local_sem)
    first = [copy(0, me, sibling, True)] + [copy(1 + j, me, (*chip, c), True) for j, chip in enumerate(chips)]
    passed = [copy(4 + j, (*chip, c), sibling) for j, chip in enumerate(chips)]
    arrivals = ([copy(0, sibling, me)] + [copy(1 + j, (*chip, c), me) for j, chip in enumerate(chips)]
                + [copy(4 + j, (*chip, 1 - c), me) for j, chip in enumerate(chips)])
    return mine, first, passed, arrivals


def _gather_start(*args):
    mine, first, _, _ = _gather_copies(*args)
    mine.start()
    for cp in first:
        cp.start()


def _gather_finish(*args):
    mine, first, passed, arrivals = _gather_copies(*args)
    for j in range(3):
        arrivals[1 + j].wait_recv()
        passed[j].start()
    arrivals[0].wait_recv()
    for j in range(3):
        arrivals[4 + j].wait_recv()
    for cp in first + passed:
        cp.wait_send()
    mine.wait()


def _gather_sems():
    return [pltpu.SemaphoreType.DMA((GATHER_SEMS,)), pltpu.SemaphoreType.DMA((GATHER_SEMS,)), pltpu.SemaphoreType.DMA]


def _gather_w_in(stage, wgu_s, conv_s):
    def body(stage_hbm, wgu_ref, conv_ref, w_ref, gwgu_ref, gconv_ref, buf, send_sems, recv_sems, local_sem,
             ssend, srecv):
        x, y, c = _position()
        me = 4 * x + 2 * y + c
        args = (stage_hbm, SLAB_IN, IN_ROWS, buf, send_sems, recv_sems, local_sem)
        _gather_start(*args)
        flips = [(k >> 2, (k >> 1) & 1, k & 1) for k in range(1, N_DEV)]
        peers = [(x ^ fx, y ^ fy, c ^ fc) for fx, fy, fc in flips]

        def small(k, block_id, to):
            return [pltpu.make_async_remote_copy(
                src_ref=s, dst_ref=g.at[block_id], send_sem=ssend.at[2 * k + n], recv_sem=srecv.at[2 * k + n],
                device_id=to, device_id_type=MESH)
                for n, (s, g) in enumerate(((wgu_ref, gwgu_ref), (conv_ref, gconv_ref)))]

        gwgu_ref[me] = wgu_ref[...]
        gconv_ref[me] = conv_ref[...]
        for k, peer in enumerate(peers):
            for cp in small(k, me, peer):
                cp.start()
        w_ref[IN_COLS:PW, :] = jnp.zeros((PW - IN_COLS, D), BF16)
        _gather_finish(*args)
        for k, (px, py, pc) in enumerate(peers):
            for cp in small(k, 4 * px + 2 * py + pc, (px, py, pc)):
                cp.wait_recv()
                cp.wait_send()
        for j, lo, hi, d in _in_segments():
            w_ref[d:d + hi - lo, :] = buf[j, lo:hi, :]

    vm = pl.BlockSpec(memory_space=pltpu.VMEM)
    hbm = pl.BlockSpec(memory_space=pl.ANY)
    return pl.pallas_call(
        body,
        name="gather_w_in",
        in_specs=[hbm, vm, vm],
        out_specs=[vm, vm, vm],
        out_shape=[jax.ShapeDtypeStruct((PW, D), BF16),
                   jax.ShapeDtypeStruct((N_DEV,) + wgu_s.shape, F32),
                   jax.ShapeDtypeStruct((N_DEV,) + conv_s.shape, F32)],
        scratch_shapes=[pltpu.VMEM((N_DEV, IN_ROWS, D), BF16)] + _gather_sems()
        + [pltpu.SemaphoreType.DMA((14,)), pltpu.SemaphoreType.DMA((14,))],
        compiler_params=_params(),
    )(stage, wgu_s, conv_s)


def _unpermute_dw_in(dw_t):
    def body(d_ref, g_ref, gb_ref):
        for j in range(N_DEV):
            g_ref[j, IN_W:IN_ROWS, :] = jnp.zeros((IN_ROWS - IN_W, D), F32)
        for j, lo, hi, d in _in_segments():
            g_ref[j, lo:hi, :] = d_ref[d:d + hi - lo, :]
        for j in range(N_DEV):
            gb_ref[j] = g_ref[j].astype(BF16)

    vm = pl.BlockSpec(memory_space=pltpu.VMEM)
    return pl.pallas_call(
        body,
        name="unpermute_dw_in",
        in_specs=[vm],
        out_specs=[vm, vm],
        out_shape=[jax.ShapeDtypeStruct((N_DEV, IN_ROWS, D), F32), jax.ShapeDtypeStruct((N_DEV, IN_ROWS, D), BF16)],
        compiler_params=_params(),
    )(dw_t)


def _exchange_core(gbs, name):
    n = len(gbs)

    def body(*refs):
        g_refs, r_refs, send_sems, recv_sems = refs[:n], refs[n:2 * n], refs[2 * n], refs[2 * n + 1]
        x, y, c = _position()
        copies = [pltpu.make_async_remote_copy(
            src_ref=g_refs[a].at[2 * i + 1 - c], dst_ref=r_refs[a].at[i],
            send_sem=send_sems.at[4 * a + i], recv_sem=recv_sems.at[4 * a + i],
            device_id=(x, y, 1 - c), device_id_type=MESH) for a in range(n) for i in range(4)]
        for cp in copies:
            cp.start()
        for cp in copies:
            cp.wait_recv()
        for cp in copies:
            cp.wait_send()

    hbm = pl.BlockSpec(memory_space=pl.ANY)
    return pl.pallas_call(
        body,
        name=name,
        in_specs=[hbm] * n,
        out_specs=[hbm] * n,
        out_shape=[jax.ShapeDtypeStruct((4,) + g.shape[1:], BF16) for g in gbs],
        scratch_shapes=[pltpu.SemaphoreType.DMA((4 * n,)), pltpu.SemaphoreType.DMA((4 * n,))],
        compiler_params=_params(),
    )(*gbs)


def _add_core(g8, r1, c_arr, name):
    _, rows, _ = g8.shape

    def body(c_ref, g_ref, r_ref, p_ref, pb_ref):
        p = g_ref[...] + r_ref[...].astype(F32)
        p_ref[...] = p
        pb_ref[...] = p.astype(BF16)

    blk = pl.BlockSpec((1, rows, D), lambda i, c: (i, 0, 0))
    return pl.pallas_call(
        body,
        name=name,
        grid_spec=pltpu.PrefetchScalarGridSpec(
            num_scalar_prefetch=1, grid=(4,),
            in_specs=[pl.BlockSpec((1, rows, D), lambda i, c: (2 * i + c[0], 0, 0)), blk],
            out_specs=[blk, blk]),
        out_shape=[jax.ShapeDtypeStruct((4, rows, D), F32), jax.ShapeDtypeStruct((4, rows, D), BF16)],
        compiler_params=_params(("arbitrary",)),
    )(c_arr, g8, r1)


def _stage2_copies(p_refs, r_refs, send_sems, recv_sems):
    x, y, c = _position()
    copies = []
    for a in range(len(p_refs)):
        for k in range(1, 4):
            tx, ty = x ^ (k >> 1), y ^ (k & 1)
            copies.append(pltpu.make_async_remote_copy(
                src_ref=p_refs[a].at[2 * tx + ty], dst_ref=r_refs[a].at[k - 1],
                send_sem=send_sems.at[3 * a + k - 1], recv_sem=recv_sems.at[3 * a + k - 1],
                device_id=(tx, ty, c), device_id_type=MESH))
    return copies


def _finish_weight(p4, r2, w, m, v, chip_arr, cols, name):
    _, rows, _ = p4.shape

    def body(chip_ref, p_ref, r_ref, w_ref, m_ref, v_ref, g_out, d_out, m_out, v_out):
        g = p_ref[0]
        for k in range(3):
            g = g + r_ref[k].astype(F32)
        if cols is not None:
            g = g[0:cols, :].T
        g_out[...] = g
        d, mn, vn = _adamw_math(w_ref[...], g, m_ref[...], v_ref[...])
        d_out[...] = d
        m_out[...] = mn
        v_out[...] = vn

    full = lambda a: pl.BlockSpec(a.shape, lambda i, chip: (0,) * a.ndim)
    shp = jax.ShapeDtypeStruct(w.shape, F32)
    return pl.pallas_call(
        body,
        name=name,
        grid_spec=pltpu.PrefetchScalarGridSpec(
            num_scalar_prefetch=1, grid=(1,),
            in_specs=[pl.BlockSpec((1, rows, D), lambda i, chip: (chip[0], 0, 0)), full(r2), full(w), full(m), full(v)],
            out_specs=[full(w)] * 4),
        out_shape=[shp, shp, shp, shp],
        compiler_params=_params(("arbitrary",)),
    )(chip_arr, p4, r2, w, m, v)


def _small_allreduce(spart):
    rows = spart.shape[0]

    def body(s_ref, o_ref, gbuf, send_sems, recv_sems):
        x, y, c = _position()
        me = 4 * x + 2 * y + c
        flips = [(k >> 2, (k >> 1) & 1, k & 1) for k in range(1, N_DEV)]
        peers = [(x ^ fx, y ^ fy, c ^ fc) for fx, fy, fc in flips]

        def copy(k, block_id, to):
            return pltpu.make_async_remote_copy(
                src_ref=s_ref, dst_ref=gbuf.at[block_id], send_sem=send_sems.at[k], recv_sem=recv_sems.at[k],
                device_id=to, device_id_type=MESH)

        for k, peer in enumerate(peers):
            copy(k, me, peer).start()
        gbuf[me] = s_ref[...]
        for k, (px, py, pc) in enumerate(peers):
            cp = copy(k, 4 * px + 2 * py + pc, (px, py, pc))
            cp.wait_recv()
            cp.wait_send()
        acc = gbuf[0]
        for d in range(1, N_DEV):
            acc = acc + gbuf[d]
        o_ref[...] = acc

    return pl.pallas_call(
        body,
        name="small_allreduce",
        in_specs=[pl.BlockSpec(memory_space=pltpu.VMEM)],
        out_specs=pl.BlockSpec(memory_space=pltpu.VMEM),
        out_shape=jax.ShapeDtypeStruct((rows, 128), F32),
        scratch_shapes=[pltpu.VMEM((N_DEV, rows, 128), F32), pltpu.SemaphoreType.DMA((7,)),
                        pltpu.SemaphoreType.DMA((7,))],
        compiler_params=_params(),
    )(spart)


def _rows128(a, rows=None):
    flat = a.reshape(-1)
    if rows is not None and flat.shape[0] < rows * 128:
        flat = jnp.concatenate([flat, jnp.zeros((rows * 128 - flat.shape[0],), flat.dtype)])
    return flat.reshape(-1, 128)


SMALL_SPECS = (("norm1_g", 8), ("norm2_g", 8), ("norm_f_g", 8), ("b_gate", 2), ("gla_norm_g", 1),
               ("w_gate_up", 4), ("conv_w", 2))
SMALL_PACK_ROWS = 40


def _pack_small(parts):
    rows = [_rows128(parts[name], r) for name, r in SMALL_SPECS]
    used = sum(r for _, r in SMALL_SPECS)
    rows.append(jnp.zeros((SMALL_PACK_ROWS - used, 128), F32))
    return jnp.concatenate(rows, axis=0)


def _unpack_small(p, shapes):
    o = 0
    out = {}
    for name, r in SMALL_SPECS:
        n = 1
        for s in shapes[name]:
            n *= s
        out[name] = p[o:o + r].reshape(-1)[:n].reshape(shapes[name])
        o += r
    return out


def kernel(x, norm1_g, w_in, w_gate_up, b_gate, gla_norm_g, conv_w, w_out, norm2_g, w_ffn_gate, w_ffn_up, w_ffn_down, norm_f_g, loss_target, m_norm1_g, m_w_in, m_w_gate_up, m_b_gate, m_gla_norm_g, m_conv_w, m_w_out, m_norm2_g, m_w_ffn_gate, m_w_ffn_up, m_w_ffn_down, m_norm_f_g, v_norm1_g, v_w_in, v_w_gate_up, v_b_gate, v_gla_norm_g, v_conv_w, v_w_out, v_norm2_g, v_w_ffn_gate, v_w_ffn_up, v_w_ffn_down, v_norm_f_g):
    xi, yi, ci = _position()
    me = 4 * xi + 2 * yi + ci
    c_arr = jnp.reshape(ci, (1,)).astype(jnp.int32)
    chip_arr = jnp.reshape(2 * xi + yi, (1,)).astype(jnp.int32)
    nb, s, _ = x.shape
    t = nb * s

    tr = lambda a: a[0].T
    stage = _prep_slab(w_in[0], tr(w_ffn_gate), tr(w_ffn_up), w_ffn_down[0], w_out[0])
    w_in_t, gwgu, gconv = _gather_w_in(stage, w_gate_up[0], conv_w[0])
    wgu_f = gwgu.transpose(1, 0, 2).reshape(RANK, NQK)
    conv_f = gconv.transpose(1, 0, 2).reshape(CONV_K, CW)
    wgu_p = jnp.concatenate([wgu_f, jnp.zeros((A_PAD - RANK, NQK), F32)], axis=0).astype(BF16)

    x2d = x.reshape(t, D)
    tgt2d = loss_target.reshape(t, D)
    tm = 256
    tk = min(512, t)
    proj, z, h, gwb = _in_proj_fwd(x2d, norm1_g, w_in_t, wgu_p, b_gate, tm, stage)
    proj3 = proj.reshape(nb, s, PW)
    z3 = z.reshape(nb, s, NQK)
    mix3, opre3, sprev, gwa = _mix_fwd(proj3, z3, gla_norm_g, conv_f, stage)
    mix2d = mix3.reshape(t, D)
    dx1, dmix, h2, act, dgate, dup, dx2, dg2, dgf, loss_part = _ffn_fwd_bwd(
        mix2d, x2d, tgt2d, gwa, gwb, norm2_g, norm_f_g.reshape(1, D), tm)
    dw_d, dwb_d = _tn_matmul(act, dx2, DFF // 2, D, tk, "dw_ffn_down", True)
    dw_g, dwb_g = _tn_matmul(dgate, h2, DFF // 2, D, tk, "dw_ffn_gate", True)
    dw_u, dwb_u = _tn_matmul(dup, h2, DFF // 2, D, tk, "dw_ffn_up", True)
    dw_o, dwb_o = _tn_matmul(mix2d, dx1, D, D, tk, "dw_out", True)
    by_owner = lambda a, rows: a.reshape(N_DEV, rows, D)
    g8 = [by_owner(dw_g, FF_W), by_owner(dw_u, FF_W), by_owner(dw_d, FF_W), by_owner(dw_o, OUT_ROWS)]
    gb8 = [by_owner(dwb_g, FF_W), by_owner(dwb_u, FF_W), by_owner(dwb_d, FF_W), by_owner(dwb_o, OUT_ROWS)]
    tags = ("w_ffn_gate", "w_ffn_up", "w_ffn_down", "w_out")
    r1 = _exchange_core(gb8, "grad_exchange_core_ffn")
    p4, pb = zip(*[_add_core(g, r, c_arr, "grad_add_core_" + tag) for g, r, tag in zip(g8, r1, tags)])
    mb = _mix_bwd(proj3, z3, sprev, opre3, dmix.reshape(nb, s, D), gla_norm_g, conv_f, wgu_p, list(pb))
    dproj3, dgng, dcw, dbg, dwgu = mb[:5]
    r2 = list(mb[5:])
    dproj2d = dproj3.reshape(t, PW)
    dw_in_t = _tn_matmul(dproj2d, h, PW, D, tk, "dw_in", False)
    g_in, gb_in = _unpermute_dw_in(dw_in_t)
    (r1_in,) = _exchange_core([gb_in], "grad_exchange_core_in")
    p4_in, pb_in = _add_core(g_in, r1_in, c_arr, "grad_add_core_w_in")
    dx, dg1, r2_in = _in_proj_bwd(dproj2d, x2d, dx1, norm1_g, w_in_t, tm, pb_in)

    tags = ("w_in",) + tags
    p4 = (p4_in,) + tuple(p4)
    r2 = [r2_in] + r2
    shard_w = (w_in[0], tr(w_ffn_gate), tr(w_ffn_up), w_ffn_down[0], w_out[0])
    shard_m = (m_w_in[0], tr(m_w_ffn_gate), tr(m_w_ffn_up), m_w_ffn_down[0], m_w_out[0])
    shard_v = (v_w_in[0], tr(v_w_ffn_gate), tr(v_w_ffn_up), v_w_ffn_down[0], v_w_out[0])
    widths = (IN_W, None, None, None, None)
    transposed = (False, True, True, False, False)
    big = {"grad": {}, "delta": {}, "m": {}, "v": {}}
    for tag, p, r, w, m, v, cols, tp in zip(tags, p4, r2, shard_w, shard_m, shard_v, widths, transposed):
        outs = _finish_weight(p, r, w, m, v, chip_arr, cols, "finish_" + tag)
        for kind, o in zip(("grad", "delta", "m", "v"), outs):
            big[kind][tag] = o.T[None] if tp else o[None]

    spart = jnp.concatenate([
        _rows128(dg1), _rows128(dg2), _rows128(dgf), _rows128(dbg), _rows128(dgng),
        _rows128(dwgu[:RANK]), _rows128(dcw[:CONV_K]), jnp.full((1, 128), loss_part[0, 0], F32)], axis=0)
    ssum = _small_allreduce(spart)
    loss = ssum[SMALL_G_ROWS - 1, 0]
    dwgu_full = ssum[27:59].reshape(RANK, NQK)
    dcw_full = ssum[59:71].reshape(CONV_K, CW)
    small_g = {
        "norm1_g": ssum[0:8].reshape(1, D), "norm2_g": ssum[8:16].reshape(1, D), "norm_f_g": ssum[16:24].reshape(D),
        "b_gate": ssum[24:26].reshape(1, NQK), "gla_norm_g": ssum[26:27].reshape(1, DV),
        "w_gate_up": lax.dynamic_slice(dwgu_full, (0, (NQK // N_DEV) * me), (RANK, NQK // N_DEV))[None],
        "conv_w": lax.dynamic_slice(dcw_full, (0, (CW // N_DEV) * me), (CONV_K, CW // N_DEV))[None],
    }
    small_w = {"norm1_g": norm1_g, "norm2_g": norm2_g, "norm_f_g": norm_f_g, "b_gate": b_gate,
               "gla_norm_g": gla_norm_g, "w_gate_up": w_gate_up, "conv_w": conv_w}
    small_m = {"norm1_g": m_norm1_g, "norm2_g": m_norm2_g, "norm_f_g": m_norm_f_g, "b_gate": m_b_gate,
               "gla_norm_g": m_gla_norm_g, "w_gate_up": m_w_gate_up, "conv_w": m_conv_w}
    small_v = {"norm1_g": v_norm1_g, "norm2_g": v_norm2_g, "norm_f_g": v_norm_f_g, "b_gate": v_b_gate,
               "gla_norm_g": v_gla_norm_g, "w_gate_up": v_w_gate_up, "conv_w": v_conv_w}
    shapes = {k: v.shape for k, v in small_w.items()}
    sd, sm, sv = _adamw_small(_pack_small(small_w), _pack_small(small_g), _pack_small(small_m), _pack_small(small_v))
    small = {"grad": small_g, "delta": _unpack_small(sd, shapes), "m": _unpack_small(sm, shapes),
             "v": _unpack_small(sv, shapes)}

    names = ("norm1_g", "w_in", "w_gate_up", "b_gate", "gla_norm_g", "conv_w", "w_out", "norm2_g",
             "w_ffn_gate", "w_ffn_up", "w_ffn_down", "norm_f_g")
    outs = [loss, dx.reshape(nb, s, D)]
    for kind in ("grad", "delta", "m", "v"):
        for name in names:
            outs.append(big[kind][name] if name in big[kind] else small[kind][name])
    return tuple(outs)
```

```python
import jax
import jax.numpy as jnp
from jax import lax
from jax.experimental import pallas as pl
from jax.experimental.pallas import tpu as pltpu

F32 = jnp.float32
BF16 = jnp.bfloat16
HIGHEST = lax.Precision.HIGHEST
MESH = pl.DeviceIdType.MESH

N_DEV = 8
D = 1024
DFF = 2816
HEADS = 4
DK = 64
DV = 128
NQK = HEADS * DK
NV = HEADS * DV
RANK = 16
CHUNK = 64
CW = 512
CONV_K = 3
IN_COLS = 3088
EPS = 1e-6
INV_GATE_NORM = 1.0 / 16.0
Q_SCALE = DK ** -0.5

PW = 3200
OQ, OK_, OV, OG, OCB, OCC, OCH, OA = 0, 256, 512, 1024, 1536, 2048, 2560, 3072
A_PAD = 128

ADAM_LR = 0.001
ADAM_B1 = 0.9
ADAM_B2 = 0.999
ADAM_EPS = 1e-08
ADAM_WD = 0.01
ADAM_STEP = 10

IN_W = IN_COLS // N_DEV
IN_ROWS = 400
FF_W = DFF // N_DEV
OUT_ROWS = D // N_DEV
SLAB_IN = 0
SLAB_G = SLAB_IN + IN_ROWS
SLAB_U = SLAB_G + FF_W
SLAB_D = SLAB_U + FF_W
SLAB_O = SLAB_D + FF_W
SLAB_ROWS = SLAB_O + OUT_ROWS

VMEM_LIMIT = 56 * 1024 * 1024


def _params(sem=None, vmem=VMEM_LIMIT):
    return pltpu.CompilerParams(dimension_semantics=sem, vmem_limit_bytes=vmem)


def _nt(a, b):
    return lax.dot_general(a, b, (((1,), (1,)), ((), ())), preferred_element_type=F32)


def _tn(a, b, precision=None):
    return lax.dot_general(a, b, (((0,), (0,)), ((), ())), preferred_element_type=F32, precision=precision)


def _mm(a, b, precision=None):
    return jnp.dot(a, b, preferred_element_type=F32, precision=precision)


def _in_segments():
    segs = []
    for j in range(N_DEV):
        lo, hi = IN_W * j, IN_W * (j + 1)
        cuts = sorted({lo, hi} | {c for c in (OCB, OCB + RANK) if lo < c < hi})
        for a, b in zip(cuts[:-1], cuts[1:]):
            if a < OCB:
                d = a
            elif a < OCB + RANK:
                d = OA + (a - OCB)
            else:
                d = a - RANK
            segs.append((j, a - lo, b - lo, d))
    return segs


def _in_proj_fwd(x2d, g1, w_in_t, wgu_p, b_gate, tm, stage):
    t = x2d.shape[0]
    nt = t // tm
    g_rows = SLAB_ROWS - SLAB_D

    def body(x_ref, g_ref, w_ref, wgu_ref, bg_ref, stage_hbm, proj_ref, z_ref, h_ref, gwb_ref,
             send_sems, recv_sems, local_sem):
        gargs = (stage_hbm, SLAB_D, g_rows, gwb_ref, send_sems, recv_sems, local_sem)

        @pl.when(pl.program_id(0) == 0)
        def _():
            _gather_start(*gargs)

        x = x_ref[...]
        r = lax.rsqrt(jnp.mean(x * x, axis=-1, keepdims=True) + EPS)
        h = ((x * r) * g_ref[...]).astype(BF16)
        h_ref[...] = h
        proj = _nt(h, w_ref[...])
        proj_ref[...] = proj
        pa = proj[:, OA:OA + A_PAD].astype(BF16)
        z_ref[...] = _mm(pa, wgu_ref[...]) + bg_ref[...]

        @pl.when(pl.program_id(0) == nt - 1)
        def _():
            _gather_finish(*gargs)

    return pl.pallas_call(
        body,
        name="in_proj_fwd",
        grid=(t // tm,),
        in_specs=[
            pl.BlockSpec((tm, D), lambda i: (i, 0)),
            pl.BlockSpec((1, D), lambda i: (0, 0)),
            pl.BlockSpec((PW, D), lambda i: (0, 0)),
            pl.BlockSpec((A_PAD, NQK), lambda i: (0, 0)),
            pl.BlockSpec((1, NQK), lambda i: (0, 0)),
            pl.BlockSpec(memory_space=pl.ANY),
        ],
        out_specs=[
            pl.BlockSpec((tm, PW), lambda i: (i, 0)),
            pl.BlockSpec((tm, NQK), lambda i: (i, 0)),
            pl.BlockSpec((tm, D), lambda i: (i, 0)),
            pl.BlockSpec(memory_space=pl.ANY),
        ],
        out_shape=[
            jax.ShapeDtypeStruct((t, PW), F32),
            jax.ShapeDtypeStruct((t, NQK), F32),
            jax.ShapeDtypeStruct((t, D), BF16),
            jax.ShapeDtypeStruct((N_DEV, g_rows, D), BF16),
        ],
        scratch_shapes=_gather_sems(),
        compiler_params=_params(("arbitrary",)),
    )(x2d, g1, w_in_t, wgu_p, b_gate, stage)


def _head_masks():
    lane = lax.broadcasted_iota(jnp.int32, (1, NQK), 1)
    return [(lane >= DK * h) & (lane < DK * (h + 1)) for h in range(HEADS)]


def _chunk_fwd_parts(q, k, z, tril):
    la = (jnp.minimum(z, 0.0) - jnp.log1p(jnp.exp(-jnp.abs(z)))) * INV_GATE_NORM
    bc = _mm(tril, la, precision=HIGHEST)
    bl = bc[CHUNK - 1:CHUNK, :]
    eb = jnp.exp(bc)
    enb = jnp.exp(-bc)
    ekl = jnp.exp(bl - bc)
    qi = (q * Q_SCALE) * eb
    ki = k * enb
    ks = k * ekl
    decb = jnp.exp(_tn(la, jnp.ones((CHUNK, DV), F32), precision=HIGHEST))
    return la, eb, enb, ekl, qi, ki, ks, decb


def _stack_heads(a, masks):
    return jnp.concatenate([jnp.where(m, a, 0.0) for m in masks], axis=0)


def _causal_stack_mask():
    row = lax.broadcasted_iota(jnp.int32, (HEADS * CHUNK, CHUNK), 0)
    col = lax.broadcasted_iota(jnp.int32, (HEADS * CHUNK, CHUNK), 1)
    return (row & (CHUNK - 1)) >= col


def _conv_taps(u, uprev):
    row = lax.broadcasted_iota(jnp.int32, u.shape, 0)
    u1 = jnp.where(row < 1, pltpu.roll(uprev, 1, 0), pltpu.roll(u, 1, 0))
    u2 = jnp.where(row < 2, pltpu.roll(uprev, 2, 0), pltpu.roll(u, 2, 0))
    return u1, u2


def _mix_fwd(proj3, z3, gng, conv_w, stage):
    nb, s, _ = proj3.shape
    nc = s // CHUNK
    g_rows = SLAB_D - SLAB_G

    def body(p_ref, z_ref, gng_ref, cw_ref, stage_hbm, mix_ref, o_ref, sprev_ref, gwa_ref, s_ref, uprev_ref,
             send_sems, recv_sems, local_sem):
        n = pl.program_id(0)
        gargs = (stage_hbm, SLAB_G, g_rows, gwa_ref, send_sems, recv_sems, local_sem)

        @pl.when(n == 0)
        def _():
            _gather_start(*gargs)
            s_ref[...] = jnp.zeros_like(s_ref)
            uprev_ref[...] = jnp.zeros_like(uprev_ref)

        r_i = lax.broadcasted_iota(jnp.int32, (CHUNK, CHUNK), 0)
        c_i = lax.broadcasted_iota(jnp.int32, (CHUNK, CHUNK), 1)
        tril = (r_i >= c_i).astype(F32)
        masks = _head_masks()
        cmask = _causal_stack_mask()
        gg = gng_ref[...]
        for b in range(nb):
            q = p_ref[b, :, OQ:OQ + NQK]
            k = p_ref[b, :, OK_:OK_ + NQK]
            _, _, _, _, qi, ki, ks, decb = _chunk_fwd_parts(q, k, z_ref[b], tril)
            qs = _stack_heads(qi, masks).astype(BF16)
            sc = jnp.where(cmask, _nt(qs, ki.astype(BF16)), 0.0).astype(BF16)
            ks16 = ks.astype(BF16)
            st = s_ref[b]
            sprev_ref[b, 0] = st
            st16 = st.astype(BF16)
            for h in range(HEADS):
                rows = slice(CHUNK * h, CHUNK * (h + 1))
                cols = slice(DV * h, DV * (h + 1))
                v16 = p_ref[b, :, OV + DV * h:OV + DV * (h + 1)].astype(BF16)
                o = _mm(sc[rows], v16) + _mm(qs[rows], st16)
                o_ref[b, :, cols] = o
                r = lax.rsqrt(jnp.mean(o * o, axis=-1, keepdims=True) + EPS)
                on = (o * r) * gg
                g = p_ref[b, :, OG + DV * h:OG + DV * (h + 1)]
                mix_ref[b, :, cols] = (on * (g * jax.nn.sigmoid(g))).astype(BF16)
                kv = _tn(ks16, v16)
                s_ref[b, rows, :] = decb[rows] * st[rows] + kv[rows]
            u = p_ref[b, :, OCC:OCC + CW] * p_ref[b, :, OCH:OCH + CW]
            u1, u2 = _conv_taps(u, uprev_ref[b])
            yc = cw_ref[0:1, :] * u2 + cw_ref[1:2, :] * u1 + cw_ref[2:3, :] * u
            mix_ref[b, :, NV:NV + CW] = (p_ref[b, :, OCB:OCB + CW] * yc).astype(BF16)
            uprev_ref[b] = u

        @pl.when(n == nc - 1)
        def _():
            _gather_finish(*gargs)

    return pl.pallas_call(
        body,
        name="mix_fwd",
        grid=(nc,),
        in_specs=[
            pl.BlockSpec((nb, CHUNK, PW), lambda n: (0, n, 0)),
            pl.BlockSpec((nb, CHUNK, NQK), lambda n: (0, n, 0)),
            pl.BlockSpec((1, DV), lambda n: (0, 0)),
            pl.BlockSpec((CONV_K, CW), lambda n: (0, 0)),
            pl.BlockSpec(memory_space=pl.ANY),
        ],
        out_specs=[
            pl.BlockSpec((nb, CHUNK, D), lambda n: (0, n, 0)),
            pl.BlockSpec((nb, CHUNK, NV), lambda n: (0, n, 0)),
            pl.BlockSpec((nb, 1, NQK, DV), lambda n: (0, n, 0, 0)),
            pl.BlockSpec(memory_space=pl.ANY),
        ],
        out_shape=[
            jax.ShapeDtypeStruct((nb, s, D), BF16),
            jax.ShapeDtypeStruct((nb, s, NV), F32),
            jax.ShapeDtypeStruct((nb, nc, NQK, DV), F32),
            jax.ShapeDtypeStruct((N_DEV, g_rows, D), BF16),
        ],
        scratch_shapes=[pltpu.VMEM((nb, NQK, DV), F32), pltpu.VMEM((nb, CHUNK, CW), F32)] + _gather_sems(),
        compiler_params=_params(("arbitrary",)),
    )(proj3, z3, gng, conv_w, stage)


def _ffn_fwd_bwd(mix2d, x2d, tgt2d, gwa, gwb, g2, gf, tm):
    t = x2d.shape[0]

    def body(mix_ref, x_ref, tgt_ref, g2_ref, gf_ref, gwa_hbm, gwb_hbm,
             dx1_ref, dmix_ref, h2_ref, act_ref, dgate_ref, dup_ref, dx2_ref, dg2_ref, dgf_ref, loss_ref,
             wo, wg, wu, wd, wsem):
        i = pl.program_id(0)

        @pl.when(i == 0)
        def _():
            copies = []
            for n, (dst, src, off, rows) in enumerate(((wg, gwa_hbm, 0, FF_W), (wu, gwa_hbm, FF_W, FF_W),
                                                       (wd, gwb_hbm, 0, FF_W), (wo, gwb_hbm, FF_W, OUT_ROWS))):
                for j in range(N_DEV):
                    copies.append(pltpu.make_async_copy(
                        src.at[j, pl.ds(off, rows), :], dst.at[pl.ds(rows * j, rows), :], wsem.at[N_DEV * n + j]))
            for cp in copies:
                cp.start()
            dg2_ref[...] = jnp.zeros_like(dg2_ref)
            dgf_ref[...] = jnp.zeros_like(dgf_ref)
            loss_ref[...] = jnp.zeros_like(loss_ref)
            for cp in copies:
                cp.wait()

        g2v = g2_ref[...]
        gfv = gf_ref[...]
        x1 = x_ref[...] + _mm(mix_ref[...], wo[...])
        r2 = lax.rsqrt(jnp.mean(x1 * x1, axis=-1, keepdims=True) + EPS)
        n2 = x1 * r2
        h2 = (n2 * g2v).astype(BF16)
        h2_ref[...] = h2
        gate = _nt(h2, wg[...])
        up = _nt(h2, wu[...])
        sg = jax.nn.sigmoid(gate)
        sil = gate * sg
        act = (sil * up).astype(BF16)
        act_ref[...] = act
        x2 = x1 + _mm(act, wd[...])
        rf = lax.rsqrt(jnp.mean(x2 * x2, axis=-1, keepdims=True) + EPS)
        nf = x2 * rf
        err = nf * gfv - tgt_ref[...]
        loss_ref[...] += 0.5 * jnp.sum(jnp.mean(err * err, axis=-1, keepdims=True))
        dy = err * (1.0 / D)
        dgf_ref[...] += jnp.sum(dy * nf, axis=0, keepdims=True)
        dnf = dy * gfv
        dx2 = rf * (dnf - nf * jnp.mean(dnf * nf, axis=-1, keepdims=True))
        dx2b = dx2.astype(BF16)
        dx2_ref[...] = dx2b
        dact = _nt(dx2b, wd[...])
        dup = (dact * sil).astype(BF16)
        dgate = ((dact * up) * (sg * (1.0 + gate * (1.0 - sg)))).astype(BF16)
        dup_ref[...] = dup
        dgate_ref[...] = dgate
        dh2 = _mm(dgate, wg[...]) + _mm(dup, wu[...])
        dg2_ref[...] += jnp.sum(dh2 * n2, axis=0, keepdims=True)
        dn2 = dh2 * g2v
        dx1 = dx2 + r2 * (dn2 - n2 * jnp.mean(dn2 * n2, axis=-1, keepdims=True))
        dx1_ref[...] = dx1
        dmix_ref[...] = _nt(dx1.astype(BF16), wo[...])

    tile = lambda w: pl.BlockSpec((tm, w), lambda i: (i, 0))
    vec = pl.BlockSpec((1, D), lambda i: (0, 0))
    hbm = pl.BlockSpec(memory_space=pl.ANY)
    return pl.pallas_call(
        body,
        name="ffn_fwd_bwd",
        grid=(t // tm,),
        in_specs=[tile(D), tile(D), tile(D), vec, vec, hbm, hbm],
        out_specs=[tile(D), tile(D), tile(D), tile(DFF), tile(DFF), tile(DFF), tile(D), vec, vec,
                   pl.BlockSpec((1, 128), lambda i: (0, 0))],
        out_shape=[
            jax.ShapeDtypeStruct((t, D), F32),
            jax.ShapeDtypeStruct((t, D), F32),
            jax.ShapeDtypeStruct((t, D), BF16),
            jax.ShapeDtypeStruct((t, DFF), BF16),
            jax.ShapeDtypeStruct((t, DFF), BF16),
            jax.ShapeDtypeStruct((t, DFF), BF16),
            jax.ShapeDtypeStruct((t, D), BF16),
            jax.ShapeDtypeStruct((1, D), F32),
            jax.ShapeDtypeStruct((1, D), F32),
            jax.ShapeDtypeStruct((1, 128), F32),
        ],
        scratch_shapes=[pltpu.VMEM((D, D), BF16), pltpu.VMEM((DFF, D), BF16), pltpu.VMEM((DFF, D), BF16),
                        pltpu.VMEM((DFF, D), BF16), pltpu.SemaphoreType.DMA((4 * N_DEV,))],
        compiler_params=_params(("arbitrary",)),
    )(mix2d, x2d, tgt2d, g2, gf, gwa, gwb)


def _tn_matmul(a, b, bm, bn, tk, name, with_bf16):
    t, m = a.shape
    n = b.shape[1]
    nk = t // tk

    def body(a_ref, b_ref, o_ref, *ob_ref):
        k = pl.program_id(2)

        @pl.when(k == 0)
        def _():
            o_ref[...] = jnp.zeros_like(o_ref)

        o_ref[...] += _tn(a_ref[...].astype(BF16), b_ref[...].astype(BF16))
        if with_bf16:
            @pl.when(k == nk - 1)
            def _():
                ob_ref[0][...] = o_ref[...].astype(BF16)

    out_blk = pl.BlockSpec((bm, bn), lambda i, j, k: (i, j))
    return pl.pallas_call(
        body,
        name=name,
        grid=(m // bm, n // bn, nk),
        in_specs=[pl.BlockSpec((tk, bm), lambda i, j, k: (k, i)), pl.BlockSpec((tk, bn), lambda i, j, k: (k, j))],
        out_specs=[out_blk, out_blk] if with_bf16 else out_blk,
        out_shape=([jax.ShapeDtypeStruct((m, n), F32), jax.ShapeDtypeStruct((m, n), BF16)] if with_bf16
                   else jax.ShapeDtypeStruct((m, n), F32)),
        compiler_params=_params(("parallel", "parallel", "arbitrary")),
    )(a, b)


def _mix_bwd(proj3, z3, sprev, opre3, dmix3, gng, conv_w, wgu_p, pbs):
    nb, s, _ = proj3.shape
    nc = s // CHUNK
    na = len(pbs)

    def body(*refs):
        (p_ref, pprev_ref, z_ref, sp_ref, o_ref, dm_ref, gng_ref, cw_ref, wgu_ref) = refs[:9]
        pb_refs = refs[9:9 + na]
        (dproj_ref, dgng_ref, dcw_ref, dbg_ref, dwgu_ref) = refs[9 + na:14 + na]
        r2_refs = refs[14 + na:14 + 2 * na]
        ds_ref, dycn_ref, send_sems, recv_sems = refs[14 + 2 * na:]
        step = pl.program_id(0)
        n = nc - 1 - step

        @pl.when(step == 0)
        def _():
            for cp in _stage2_copies(pb_refs, r2_refs, send_sems, recv_sems):
                cp.start()
            ds_ref[...] = jnp.zeros_like(ds_ref)
            dycn_ref[...] = jnp.zeros_like(dycn_ref)
            dgng_ref[...] = jnp.zeros_like(dgng_ref)
            dcw_ref[...] = jnp.zeros_like(dcw_ref)
            dbg_ref[...] = jnp.zeros_like(dbg_ref)
            dwgu_ref[...] = jnp.zeros_like(dwgu_ref)

        r_i = lax.broadcasted_iota(jnp.int32, (CHUNK, CHUNK), 0)
        c_i = lax.broadcasted_iota(jnp.int32, (CHUNK, CHUNK), 1)
        tril = (r_i >= c_i).astype(F32)
        triu = (r_i <= c_i).astype(F32)
        causal = r_i >= c_i
        masks = _head_masks()
        cmask = _causal_stack_mask()
        gg = gng_ref[...]
        last_row = lax.broadcasted_iota(jnp.int32, (CHUNK, NQK), 0) == CHUNK - 1
        ones_r = jnp.ones((8, DV), F32)
        has_prev = (n > 0).astype(F32)
        for b in range(nb):
            q = p_ref[b, :, OQ:OQ + NQK]
            k = p_ref[b, :, OK_:OK_ + NQK]
            z = z_ref[b]
            _, eb, enb, ekl, qi, ki, ks, decb = _chunk_fwd_parts(q, k, z, tril)
            qi16 = qi.astype(BF16)
            ki16 = ki.astype(BF16)
            ks16 = ks.astype(BF16)
            qs = _stack_heads(qi, masks).astype(BF16)
            sc = jnp.where(cmask, _nt(qs, ki16), 0.0).astype(BF16)
            st = sp_ref[b, 0]
            st16 = st.astype(BF16)
            dsn = ds_ref[b]
            dsn16 = dsn.astype(BF16)
            dqi = jnp.zeros((CHUNK, NQK), F32)
            dki = jnp.zeros((CHUNK, NQK), F32)
            dks = jnp.zeros((CHUNK, NQK), F32)
            dgng = jnp.zeros((1, DV), F32)
            for h in range(HEADS):
                rows = slice(CHUNK * h, CHUNK * (h + 1))
                cols = slice(DV * h, DV * (h + 1))
                o = o_ref[b, :, cols]
                r = lax.rsqrt(jnp.mean(o * o, axis=-1, keepdims=True) + EPS)
                nh = o * r
                g = p_ref[b, :, OG + DV * h:OG + DV * (h + 1)]
                sg = jax.nn.sigmoid(g)
                dog = dm_ref[b, :, cols]
                dproj_ref[b, :, OG + DV * h:OG + DV * (h + 1)] = (
                    (dog * (nh * gg)) * (sg * (1.0 + g * (1.0 - sg)))).astype(BF16)
                don = dog * (g * sg)
                dgng = dgng + jnp.sum(don * nh, axis=0, keepdims=True)
                dn = don * gg
                do = r * (dn - nh * jnp.mean(dn * nh, axis=-1, keepdims=True))
                do16 = do.astype(BF16)
                v16 = p_ref[b, :, OV + DV * h:OV + DV * (h + 1)].astype(BF16)
                dp16 = jnp.where(causal, _nt(do16, v16), 0.0).astype(BF16)
                ksm = jnp.where(masks[h], ks16, jnp.zeros_like(ks16))
                dv = _tn(sc[rows], do16) + _mm(ksm, dsn16)
                dproj_ref[b, :, OV + DV * h:OV + DV * (h + 1)] = dv.astype(BF16)
                dqi = dqi + jnp.where(masks[h], _mm(dp16, ki16) + _nt(do16, st16), 0.0)
                dki = dki + jnp.where(masks[h], _tn(dp16, qi16), 0.0)
                dks = dks + jnp.where(masks[h], _nt(v16, dsn16), 0.0)
                qdo = _tn(qi16, do16)
                ds_ref[b, rows, :] = decb[rows] * dsn[rows] + qdo[rows]
            dgng_ref[...] += dgng
            dproj_ref[b, :, OQ:OQ + NQK] = (dqi * (Q_SCALE * eb)).astype(BF16)
            dproj_ref[b, :, OK_:OK_ + NQK] = (dki * enb + dks * ekl).astype(BF16)
            dks_ks = dks * ks
            db = dqi * qi - dki * ki - dks_ks
            dbl = jnp.sum(dks_ks, axis=0, keepdims=True) + lax.dot_general(
                ones_r, dsn * st * decb, (((1,), (1,)), ((), ())),
                preferred_element_type=F32, precision=HIGHEST)[0:1, :]
            db = db + jnp.where(last_row, dbl, 0.0)
            dla = _mm(triu, db, precision=HIGHEST)
            dz = (dla * INV_GATE_NORM) * (1.0 / (1.0 + jnp.exp(z)))
            dbg_ref[...] += jnp.sum(dz, axis=0, keepdims=True)
            dz16 = dz.astype(BF16)
            pa16 = p_ref[b, :, OA:OA + A_PAD].astype(BF16)
            dwgu_ref[...] += _tn(pa16, dz16)
            dproj_ref[b, :, OA:OA + A_PAD] = _nt(dz16, wgu_ref[...]).astype(BF16)
            cb = p_ref[b, :, OCB:OCB + CW]
            cc = p_ref[b, :, OCC:OCC + CW]
            ch = p_ref[b, :, OCH:OCH + CW]
            u = cc * ch
            uprev = (pprev_ref[b, :, 0:CW] * pprev_ref[b, :, CW:2 * CW]) * has_prev
            u1, u2 = _conv_taps(u, uprev)
            w0 = cw_ref[0:1, :]
            w1 = cw_ref[1:2, :]
            w2 = cw_ref[2:3, :]
            yc = w0 * u2 + w1 * u1 + w2 * u
            doc = dm_ref[b, :, NV:NV + CW]
            dproj_ref[b, :, OCB:OCB + CW] = (doc * yc).astype(BF16)
            dyc = doc * cb
            dycn = dycn_ref[b]
            row = lax.broadcasted_iota(jnp.int32, dyc.shape, 0)
            d1 = jnp.where(row >= CHUNK - 1, pltpu.roll(dycn, CHUNK - 1, 0), pltpu.roll(dyc, CHUNK - 1, 0))
            d2 = jnp.where(row >= CHUNK - 2, pltpu.roll(dycn, CHUNK - 2, 0), pltpu.roll(dyc, CHUNK - 2, 0))
            du = w2 * dyc + w1 * d1 + w0 * d2
            dproj_ref[b, :, OCC:OCC + CW] = (du * ch).astype(BF16)
            dproj_ref[b, :, OCH:OCH + CW] = (du * cc).astype(BF16)
            dcw_ref[0:1, :] += jnp.sum(dyc * u2, axis=0, keepdims=True)
            dcw_ref[1:2, :] += jnp.sum(dyc * u1, axis=0, keepdims=True)
            dcw_ref[2:3, :] += jnp.sum(dyc * u, axis=0, keepdims=True)
            dycn_ref[b] = dyc

        @pl.when(step == nc - 1)
        def _():
            copies = _stage2_copies(pb_refs, r2_refs, send_sems, recv_sems)
            for cp in copies:
                cp.wait_recv()
            for cp in copies:
                cp.wait_send()

    rev = lambda w: pl.BlockSpec((nb, CHUNK, w), lambda i: (0, nc - 1 - i, 0))
    const = lambda r, c: pl.BlockSpec((r, c), lambda i: (0, 0))
    hbm = pl.BlockSpec(memory_space=pl.ANY)
    return pl.pallas_call(
        body,
        name="mix_bwd",
        grid=(nc,),
        in_specs=[
            rev(PW),
            pl.BlockSpec((nb, CHUNK, 2 * CW), lambda i: (0, jnp.maximum(nc - 2 - i, 0), OCC // (2 * CW))),
            rev(NQK),
            pl.BlockSpec((nb, 1, NQK, DV), lambda i: (0, nc - 1 - i, 0, 0)),
            rev(NV),
            rev(D),
            const(1, DV),
            const(CONV_K, CW),
            const(A_PAD, NQK),
        ] + [hbm] * na,
        out_specs=[rev(PW), const(1, DV), const(8, CW), const(1, NQK), const(A_PAD, NQK)] + [hbm] * na,
        out_shape=[
            jax.ShapeDtypeStruct((nb, s, PW), BF16),
            jax.ShapeDtypeStruct((1, DV), F32),
            jax.ShapeDtypeStruct((8, CW), F32),
            jax.ShapeDtypeStruct((1, NQK), F32),
            jax.ShapeDtypeStruct((A_PAD, NQK), F32),
        ] + [jax.ShapeDtypeStruct((3,) + p.shape[1:], BF16) for p in pbs],
        scratch_shapes=[pltpu.VMEM((nb, NQK, DV), F32), pltpu.VMEM((nb, CHUNK, CW), F32),
                        pltpu.SemaphoreType.DMA((3 * na,)), pltpu.SemaphoreType.DMA((3 * na,))],
        compiler_params=_params(("arbitrary",)),
    )(proj3, proj3, z3, sprev, opre3, dmix3, gng, conv_w, wgu_p, *pbs)


def _in_proj_bwd(dproj2d, x2d, dx1, g1, w_in_t, tm, pb):
    t = x2d.shape[0]
    nt = t // tm

    def body(dp_ref, x_ref, dx1_ref, g_ref, w_ref, pb_ref, dx_ref, dg1_ref, r2_ref, send_sems, recv_sems):
        @pl.when(pl.program_id(0) == 0)
        def _():
            for cp in _stage2_copies([pb_ref], [r2_ref], send_sems, recv_sems):
                cp.start()
            dg1_ref[...] = jnp.zeros_like(dg1_ref)

        x = x_ref[...]
        r = lax.rsqrt(jnp.mean(x * x, axis=-1, keepdims=True) + EPS)
        n1 = x * r
        dh = _mm(dp_ref[...], w_ref[...])
        dg1_ref[...] += jnp.sum(dh * n1, axis=0, keepdims=True)
        dn = dh * g_ref[...]
        dx_ref[...] = dx1_ref[...] + r * (dn - n1 * jnp.mean(dn * n1, axis=-1, keepdims=True))

        @pl.when(pl.program_id(0) == nt - 1)
        def _():
            copies = _stage2_copies([pb_ref], [r2_ref], send_sems, recv_sems)
            for cp in copies:
                cp.wait_recv()
            for cp in copies:
                cp.wait_send()

    tile = lambda w: pl.BlockSpec((tm, w), lambda i: (i, 0))
    vec = pl.BlockSpec((1, D), lambda i: (0, 0))
    hbm = pl.BlockSpec(memory_space=pl.ANY)
    return pl.pallas_call(
        body,
        name="in_proj_bwd",
        grid=(nt,),
        in_specs=[tile(PW), tile(D), tile(D), vec, pl.BlockSpec((PW, D), lambda i: (0, 0)), hbm],
        out_specs=[tile(D), vec, hbm],
        out_shape=[jax.ShapeDtypeStruct((t, D), F32), jax.ShapeDtypeStruct((1, D), F32),
                   jax.ShapeDtypeStruct((3,) + pb.shape[1:], BF16)],
        scratch_shapes=[pltpu.SemaphoreType.DMA((3,)), pltpu.SemaphoreType.DMA((3,))],
        compiler_params=_params(("arbitrary",)),
    )(dproj2d, x2d, dx1, g1, w_in_t, pb)


def _adamw_math(w, g, m, v):
    m = ADAM_B1 * m + (1.0 - ADAM_B1) * g
    v = ADAM_B2 * v + (1.0 - ADAM_B2) * (g * g)
    m_hat = m / (1.0 - ADAM_B1 ** ADAM_STEP)
    v_hat = v / (1.0 - ADAM_B2 ** ADAM_STEP)
    delta = -ADAM_LR * (m_hat / (jnp.sqrt(v_hat) + ADAM_EPS) + ADAM_WD * w)
    return delta, m, v


def _position():
    return lax.axis_index("x"), lax.axis_index("y"), lax.axis_index("c")


def _prep_slab(w_it, w_gt, w_ut, w_d, w_o):
    def body(wi_ref, wg_ref, wu_ref, wd_ref, wo_ref, stage):
        stage[SLAB_IN:SLAB_IN + IN_W, :] = wi_ref[...].astype(BF16)
        stage[SLAB_IN + IN_W:SLAB_G, :] = jnp.zeros((IN_ROWS - IN_W, D), BF16)
        stage[SLAB_G:SLAB_U, :] = wg_ref[...].astype(BF16)
        stage[SLAB_U:SLAB_D, :] = wu_ref[...].astype(BF16)
        stage[SLAB_D:SLAB_O, :] = wd_ref[...].astype(BF16)
        stage[SLAB_O:SLAB_ROWS, :] = wo_ref[...].astype(BF16)

    vm = pl.BlockSpec(memory_space=pltpu.VMEM)
    return pl.pallas_call(
        body,
        name="prep_slab",
        in_specs=[vm] * 5,
        out_specs=vm,
        out_shape=jax.ShapeDtypeStruct((SLAB_ROWS, D), BF16),
        compiler_params=_params(),
    )(w_it, w_gt, w_ut, w_d, w_o)


GATHER_SEMS = 7


def _gather_copies(stage, lo, rows, gx, send_sems, recv_sems, local_sem):
    x, y, c = _position()
    me = (x, y, c)
    sibling = (x, y, 1 - c)
    chips = [(1 - x, y), (x, 1 - y), (1 - x, 1 - y)]
    src = stage.at[pl.ds(lo, rows), :]

    def blk(px, py, pc):
        return gx.at[4 * px + 2 * py + pc]

    def copy(k, block, to, from_stage=False):
        return pltpu.make_async_remote_copy(
            src_ref=src if from_stage else blk(*block), dst_ref=blk(*block),
            send_sem=send_sems.at[k], recv_sem=recv_sems.at[k], device_id=to, device_id_type=MESH)

    mine = pltpu.make_async_copy(src, blk(*me), local_sem)
    first = [copy(0, me, sibling, True)] + [copy(1 + j, me, (*chip, c), True) for j, chip in enumerate(chips)]
    passed = [copy(4 + j, (*chip, c), sibling) for j, chip in enumerate(chips)]
    arrivals = ([copy(0, sibling, me)] + [copy(1 + j, (*chip, c), me) for j, chip in enumerate(chips)]
                + [copy(4 + j, (*chip, 1 - c), me) for j, chip in enumerate(chips)])
    return mine, first, passed, arrivals


def _gather_start(*args):
    mine, first, _, _ = _gather_copies(*args)
    mine.start()
    for cp in first:
        cp.start()


def _gather_finish(*args):
    mine, first, passed, arrivals = _gather_copies(*args)
    for j in range(3):
        arrivals[1 + j].wait_recv()
        passed[j].start()
    arrivals[0].wait_recv()
    for j in range(3):
        arrivals[4 + j].wait_recv()
    for cp in first + passed:
        cp.wait_send()
    mine.wait()


def _gather_sems():
    return [pltpu.SemaphoreType.DMA((GATHER_SEMS,)), pltpu.SemaphoreType.DMA((GATHER_SEMS,)), pltpu.SemaphoreType.DMA]


def _gather_w_in(stage, wgu_s, conv_s):
    def body(stage_hbm, wgu_ref, conv_ref, w_ref, gwgu_ref, gconv_ref, buf, send_sems, recv_sems, local_sem,
             ssend, srecv):
        x, y, c = _position()
        me = 4 * x + 2 * y + c
        args = (stage_hbm, SLAB_IN, IN_ROWS, buf, send_sems, recv_sems, local_sem)
        _gather_start(*args)
        flips = [(k >> 2, (k >> 1) & 1, k & 1) for k in range(1, N_DEV)]
        peers = [(x ^ fx, y ^ fy, c ^ fc) for fx, fy, fc in flips]

        def small(k, block_id, to):
            return [pltpu.make_async_remote_copy(
                src_ref=s, dst_ref=g.at[block_id], send_sem=ssend.at[2 * k + n], recv_sem=srecv.at[2 * k + n],
                device_id=to, device_id_type=MESH)
                for n, (s, g) in enumerate(((wgu_ref, gwgu_ref), (conv_ref, gconv_ref)))]

        gwgu_ref[me] = wgu_ref[...]
        gconv_ref[me] = conv_ref[...]
        for k, peer in enumerate(peers):
            for cp in small(k, me, peer):
                cp.start()
        w_ref[IN_COLS:PW, :] = jnp.zeros((PW - IN_COLS, D), BF16)
        _gather_finish(*args)
        for k, (px, py, pc) in enumerate(peers):
            for cp in small(k, 4 * px + 2 * py + pc, (px, py, pc)):
                cp.wait_recv()
                cp.wait_send()
        for j, lo, hi, d in _in_segments():
            w_ref[d:d + hi - lo, :] = buf[j, lo:hi, :]

    vm = pl.BlockSpec(memory_space=pltpu.VMEM)
    hbm = pl.BlockSpec(memory_space=pl.ANY)
    return pl.pallas_call(
        body,
        name="gather_w_in",
        in_specs=[hbm, vm, vm],
        out_specs=[vm, vm, vm],
        out_shape=[jax.ShapeDtypeStruct((PW, D), BF16),
                   jax.ShapeDtypeStruct((N_DEV,) + wgu_s.shape, F32),
                   jax.ShapeDtypeStruct((N_DEV,) + conv_s.shape, F32)],
        scratch_shapes=[pltpu.VMEM((N_DEV, IN_ROWS, D), BF16)] + _gather_sems()
        + [pltpu.SemaphoreType.DMA((14,)), pltpu.SemaphoreType.DMA((14,))],
        compiler_params=_params(),
    )(stage, wgu_s, conv_s)


def _unpermute_dw_in(dw_t):
    def body(d_ref, g_ref, gb_ref):
        for j in range(N_DEV):
            g_ref[j, IN_W:IN_ROWS, :] = jnp.zeros((IN_ROWS - IN_W, D), F32)
        for j, lo, hi, d in _in_segments():
            g_ref[j, lo:hi, :] = d_ref[d:d + hi - lo, :]
        for j in range(N_DEV):
            gb_ref[j] = g_ref[j].astype(BF16)

    vm = pl.BlockSpec(memory_space=pltpu.VMEM)
    return pl.pallas_call(
        body,
        name="unpermute_dw_in",
        in_specs=[vm],
        out_specs=[vm, vm],
        out_shape=[jax.ShapeDtypeStruct((N_DEV, IN_ROWS, D), F32), jax.ShapeDtypeStruct((N_DEV, IN_ROWS, D), BF16)],
        compiler_params=_params(),
    )(dw_t)


def _exchange_core(gbs, name):
    n = len(gbs)

    def body(*refs):
        g_refs, r_refs, send_sems, recv_sems = refs[:n], refs[n:2 * n], refs[2 * n], refs[2 * n + 1]
        x, y, c = _position()
        copies = [pltpu.make_async_remote_copy(
            src_ref=g_refs[a].at[2 * i + 1 - c], dst_ref=r_refs[a].at[i],
            send_sem=send_sems.at[4 * a + i], recv_sem=recv_sems.at[4 * a + i],
            device_id=(x, y, 1 - c), device_id_type=MESH) for a in range(n) for i in range(4)]
        for cp in copies:
            cp.start()
        for cp in copies:
            cp.wait_recv()
        for cp in copies:
            cp.wait_send()

    hbm = pl.BlockSpec(memory_space=pl.ANY)
    return pl.pallas_call(
        body,
        name=name,
        in_specs=[hbm] * n,
        out_specs=[hbm] * n,
        out_shape=[jax.ShapeDtypeStruct((4,) + g.shape[1:], BF16) for g in gbs],
        scratch_shapes=[pltpu.SemaphoreType.DMA((4 * n,)), pltpu.SemaphoreType.DMA((4 * n,))],
        compiler_params=_params(),
    )(*gbs)


def _add_core(g8, r1, pos_arr, name):
    _, rows, _ = g8.shape

    def body(pos_ref, g_ref, r_ref, pb_ref):
        pb_ref[...] = (g_ref[...] + r_ref[...].astype(F32)).astype(BF16)

    return pl.pallas_call(
        body,
        name=name,
        grid_spec=pltpu.PrefetchScalarGridSpec(
            num_scalar_prefetch=1, grid=(3,),
            in_specs=[pl.BlockSpec((1, rows, D), lambda k, pos: (2 * (pos[1] ^ (k + 1)) + pos[0], 0, 0)),
                      pl.BlockSpec((1, rows, D), lambda k, pos: (pos[1] ^ (k + 1), 0, 0))],
            out_specs=pl.BlockSpec((1, rows, D), lambda k, pos: (k, 0, 0))),
        out_shape=jax.ShapeDtypeStruct((3, rows, D), BF16),
        compiler_params=_params(("arbitrary",)),
    )(pos_arr, g8, r1)


def _stage2_copies(p_refs, r_refs, send_sems, recv_sems):
    x, y, c = _position()
    copies = []
    for a in range(len(p_refs)):
        for k in range(1, 4):
            copies.append(pltpu.make_async_remote_copy(
                src_ref=p_refs[a].at[k - 1], dst_ref=r_refs[a].at[k - 1],
                send_sem=send_sems.at[3 * a + k - 1], recv_sem=recv_sems.at[3 * a + k - 1],
                device_id=(x ^ (k >> 1), y ^ (k & 1), c), device_id_type=MESH))
    return copies


def _finish_weight(g8, r1, r2, w, m, v, pos_arr, name):
    _, rows, _ = g8.shape
    wr = w.shape[0]

    def body(pos_ref, g_ref, r1_ref, r2_ref, w_ref, m_ref, v_ref, g_out, d_out, m_out, v_out):
        g = g_ref[0] + r1_ref[0].astype(F32)
        for k in range(3):
            g = g + r2_ref[k].astype(F32)
        g = g[0:wr, :]
        g_out[...] = g
        d, mn, vn = _adamw_math(w_ref[...], g, m_ref[...], v_ref[...])
        d_out[...] = d
        m_out[...] = mn
        v_out[...] = vn

    full = lambda a: pl.BlockSpec(a.shape, lambda i, pos: (0,) * a.ndim)
    shp = jax.ShapeDtypeStruct(w.shape, F32)
    return pl.pallas_call(
        body,
        name=name,
        grid_spec=pltpu.PrefetchScalarGridSpec(
            num_scalar_prefetch=1, grid=(1,),
            in_specs=[pl.BlockSpec((1, rows, D), lambda i, pos: (2 * pos[1] + pos[0], 0, 0)),
                      pl.BlockSpec((1, rows, D), lambda i, pos: (pos[1], 0, 0)),
                      full(r2), full(w), full(m), full(v)],
            out_specs=[full(w)] * 4),
        out_shape=[shp, shp, shp, shp],
        compiler_params=_params(("arbitrary",)),
    )(pos_arr, g8, r1, r2, w, m, v)


SMALL_NAMES = ("norm1_g", "norm2_g", "norm_f_g", "b_gate", "gla_norm_g", "w_gate_up", "conv_w")
SMALL_PACK_ROWS = 32
WGU_W = NQK // N_DEV
CONV_W = CW // N_DEV


def _small_step(parts, loss_part, ws, ms, vs):
    n = len(SMALL_NAMES)

    def body(*refs):
        dg1, dg2, dgf, dbg, dgng, dwgu, dcw, lp = refs[:8]
        w_refs, m_refs, v_refs = refs[8:8 + n], refs[8 + n:8 + 2 * n], refs[8 + 2 * n:8 + 3 * n]
        loss_ref = refs[8 + 3 * n]
        outs = refs[9 + 3 * n:9 + 7 * n]
        pack, gbuf, send_sems, recv_sems = refs[9 + 7 * n:]
        x, y, c = _position()
        me = 4 * x + 2 * y + c
        flips = [(k >> 2, (k >> 1) & 1, k & 1) for k in range(1, N_DEV)]
        peers = [(x ^ fx, y ^ fy, c ^ fc) for fx, fy, fc in flips]

        def copy(k, block_id, to):
            return pltpu.make_async_remote_copy(
                src_ref=pack, dst_ref=gbuf.at[block_id], send_sem=send_sems.at[k], recv_sem=recv_sems.at[k],
                device_id=to, device_id_type=MESH)

        pack[...] = jnp.zeros_like(pack)
        pack[0:1, :] = dg1[...]
        pack[1:2, :] = dg2[...]
        pack[2:3, :] = dgf[...]
        pack[3:4, 0:NQK] = dbg[...]
        pack[3:4, NQK:NQK + DV] = dgng[...]
        pack[3:4, NQK + DV:NQK + 2 * DV] = lp[...]
        pack[8:8 + RANK, 0:NQK] = dwgu[0:RANK, :]
        pack[24:24 + CONV_K, 0:CW] = dcw[0:CONV_K, :]
        for k, peer in enumerate(peers):
            copy(k, me, peer).start()
        gbuf[me] = pack[...]
        for k, (px, py, pc) in enumerate(peers):
            cp = copy(k, 4 * px + 2 * py + pc, (px, py, pc))
            cp.wait_recv()
            cp.wait_send()
        acc = gbuf[0]
        for d in range(1, N_DEV):
            acc = acc + gbuf[d]
        loss_ref[...] = acc[3:4, NQK + DV:NQK + DV + 1]

        def my_columns(full, width):
            r = lax.broadcasted_iota(jnp.int32, (full.shape[1], width), 0)
            col = lax.broadcasted_iota(jnp.int32, (full.shape[1], width), 1)
            sel = (r == width * me + col).astype(F32)
            return _mm(full, sel, precision=HIGHEST)

        grads = [acc[0:1, :], acc[1:2, :], acc[2:3, :], acc[3:4, 0:NQK], acc[3:4, NQK:NQK + DV],
                 my_columns(acc[8:8 + RANK, 0:NQK], WGU_W), my_columns(acc[24:24 + CONV_K, 0:CW], CONV_W)]
        for i, g in enumerate(grads):
            d, mn, vn = _adamw_math(w_refs[i][...], g, m_refs[i][...], v_refs[i][...])
            outs[4 * i][...] = g
            outs[4 * i + 1][...] = d
            outs[4 * i + 2][...] = mn
            outs[4 * i + 3][...] = vn

    vm = pl.BlockSpec(memory_space=pltpu.VMEM)
    out_shape = [jax.ShapeDtypeStruct((1, 1), F32)]
    for w in ws:
        out_shape += [jax.ShapeDtypeStruct(w.shape, F32)] * 4
    return pl.pallas_call(
        body,
        name="small_step",
        in_specs=[vm] * (8 + 3 * n),
        out_specs=[vm] * (1 + 4 * n),
        out_shape=out_shape,
        scratch_shapes=[pltpu.VMEM((SMALL_PACK_ROWS, D), F32), pltpu.VMEM((N_DEV, SMALL_PACK_ROWS, D), F32),
                        pltpu.SemaphoreType.DMA((7,)), pltpu.SemaphoreType.DMA((7,))],
        compiler_params=_params(),
    )(*parts, loss_part, *ws, *ms, *vs)


def kernel(x, norm1_g, w_in, w_gate_up, b_gate, gla_norm_g, conv_w, w_out, norm2_g, w_ffn_gate, w_ffn_up, w_ffn_down, norm_f_g, loss_target, m_norm1_g, m_w_in, m_w_gate_up, m_b_gate, m_gla_norm_g, m_conv_w, m_w_out, m_norm2_g, m_w_ffn_gate, m_w_ffn_up, m_w_ffn_down, m_norm_f_g, v_norm1_g, v_w_in, v_w_gate_up, v_b_gate, v_gla_norm_g, v_conv_w, v_w_out, v_norm2_g, v_w_ffn_gate, v_w_ffn_up, v_w_ffn_down, v_norm_f_g):
    xi, yi, ci = _position()
    pos_arr = jnp.stack([ci, 2 * xi + yi]).astype(jnp.int32)
    nb, s, _ = x.shape
    t = nb * s

    tr = lambda a: a[0].T
    stage = _prep_slab(tr(w_in), tr(w_ffn_gate), tr(w_ffn_up), w_ffn_down[0], w_out[0])
    w_in_t, gwgu, gconv = _gather_w_in(stage, w_gate_up[0], conv_w[0])
    wgu_f = gwgu.transpose(1, 0, 2).reshape(RANK, NQK)
    conv_f = gconv.transpose(1, 0, 2).reshape(CONV_K, CW)
    wgu_p = jnp.concatenate([wgu_f, jnp.zeros((A_PAD - RANK, NQK), F32)], axis=0).astype(BF16)

    x2d = x.reshape(t, D)
    tgt2d = loss_target.reshape(t, D)
    tm = 256
    tk = min(2048, t)
    proj, z, h, gwb = _in_proj_fwd(x2d, norm1_g, w_in_t, wgu_p, b_gate, tm, stage)
    proj3 = proj.reshape(nb, s, PW)
    z3 = z.reshape(nb, s, NQK)
    mix3, opre3, sprev, gwa = _mix_fwd(proj3, z3, gla_norm_g, conv_f, stage)
    mix2d = mix3.reshape(t, D)
    dx1, dmix, h2, act, dgate, dup, dx2, dg2, dgf, loss_part = _ffn_fwd_bwd(
        mix2d, x2d, tgt2d, gwa, gwb, norm2_g, norm_f_g.reshape(1, D), tm)
    dw_d, dwb_d = _tn_matmul(act, dx2, DFF // 2, D, tk, "dw_ffn_down", True)
    dw_g, dwb_g = _tn_matmul(dgate, h2, DFF // 2, D, tk, "dw_ffn_gate", True)
    dw_u, dwb_u = _tn_matmul(dup, h2, DFF // 2, D, tk, "dw_ffn_up", True)
    dw_o, dwb_o = _tn_matmul(mix2d, dx1, D, D, tk, "dw_out", True)
    by_owner = lambda a, rows: a.reshape(N_DEV, rows, D)
    g8 = [by_owner(dw_g, FF_W), by_owner(dw_u, FF_W), by_owner(dw_d, FF_W), by_owner(dw_o, OUT_ROWS)]
    gb8 = [by_owner(dwb_g, FF_W), by_owner(dwb_u, FF_W), by_owner(dwb_d, FF_W), by_owner(dwb_o, OUT_ROWS)]
    tags = ("w_ffn_gate", "w_ffn_up", "w_ffn_down", "w_out")
    r1 = _exchange_core(gb8, "grad_exchange_core_ffn")
    pb = [_add_core(g, r, pos_arr, "grad_add_core_" + tag) for g, r, tag in zip(g8, r1, tags)]
    mb = _mix_bwd(proj3, z3, sprev, opre3, dmix.reshape(nb, s, D), gla_norm_g, conv_f, wgu_p, pb)
    dproj3, dgng, dcw, dbg, dwgu = mb[:5]
    r2 = list(mb[5:])
    dproj2d = dproj3.reshape(t, PW)
    dw_in_t = _tn_matmul(dproj2d, h, PW, D, min(1024, t), "dw_in", False)
    g_in, gb_in = _unpermute_dw_in(dw_in_t)
    (r1_in,) = _exchange_core([gb_in], "grad_exchange_core_in")
    pb_in = _add_core(g_in, r1_in, pos_arr, "grad_add_core_w_in")
    dx, dg1, r2_in = _in_proj_bwd(dproj2d, x2d, dx1, norm1_g, w_in_t, tm, pb_in)

    tags = ("w_in",) + tags
    g8 = [g_in] + g8
    r1 = [r1_in] + list(r1)
    r2 = [r2_in] + r2
    shard_w = (tr(w_in), tr(w_ffn_gate), tr(w_ffn_up), w_ffn_down[0], w_out[0])
    shard_m = (tr(m_w_in), tr(m_w_ffn_gate), tr(m_w_ffn_up), m_w_ffn_down[0], m_w_out[0])
    shard_v = (tr(v_w_in), tr(v_w_ffn_gate), tr(v_w_ffn_up), v_w_ffn_down[0], v_w_out[0])
    transposed = (True, True, True, False, False)
    results = {}
    for tag, g, ra, rb, w, m, v, tp in zip(tags, g8, r1, r2, shard_w, shard_m, shard_v, transposed):
        outs = _finish_weight(g, ra, rb, w, m, v, pos_arr, "finish_" + tag)
        results[tag] = [o.T[None] if tp else o[None] for o in outs]

    small_w = (norm1_g, norm2_g, norm_f_g.reshape(1, D), b_gate, gla_norm_g, w_gate_up[0], conv_w[0])
    small_m = (m_norm1_g, m_norm2_g, m_norm_f_g.reshape(1, D), m_b_gate, m_gla_norm_g, m_w_gate_up[0], m_conv_w[0])
    small_v = (v_norm1_g, v_norm2_g, v_norm_f_g.reshape(1, D), v_b_gate, v_gla_norm_g, v_w_gate_up[0], v_conv_w[0])
    so = _small_step((dg1, dg2, dgf, dbg, dgng, dwgu, dcw), loss_part, small_w, small_m, small_v)
    loss = so[0].reshape(())
    shapes = {"norm_f_g": (D,), "w_gate_up": (1, RANK, WGU_W), "conv_w": (1, CONV_K, CONV_W)}
    for i, name in enumerate(SMALL_NAMES):
        results[name] = [o.reshape(shapes[name]) if name in shapes else o for o in so[1 + 4 * i:5 + 4 * i]]

    names = ("norm1_g", "w_in", "w_gate_up", "b_gate", "gla_norm_g", "conv_w", "w_out", "norm2_g",
             "w_ffn_gate", "w_ffn_up", "w_ffn_down", "norm_f_g")
    outs = [loss, dx.reshape(nb, s, D)]
    for kind in range(4):
        for name in names:
            outs.append(results[name][kind])
    return tuple(outs)
```

```python
import jax
import jax.numpy as jnp
from jax import lax
from jax.experimental import pallas as pl
from jax.experimental.pallas import tpu as pltpu

F32 = jnp.float32
BF16 = jnp.bfloat16
HIGHEST = lax.Precision.HIGHEST
MESH = pl.DeviceIdType.MESH

N_DEV = 8
D = 1024
DFF = 2816
HEADS = 4
DK = 64
DV = 128
NQK = HEADS * DK
NV = HEADS * DV
RANK = 16
CHUNK = 64
CW = 512
CONV_K = 3
IN_COLS = 3088
EPS = 1e-6
INV_GATE_NORM = 1.0 / 16.0
Q_SCALE = DK ** -0.5

PW = 3200
OQ, OK_, OV, OG, OCB, OCC, OCH, OA = 0, 256, 512, 1024, 1536, 2048, 2560, 3072
A_PAD = 128

ADAM_LR = 0.001
ADAM_B1 = 0.9
ADAM_B2 = 0.999
ADAM_EPS = 1e-08
ADAM_WD = 0.01
ADAM_STEP = 10

IN_W = IN_COLS // N_DEV
IN_ROWS = 400
FF_W = DFF // N_DEV
OUT_ROWS = D // N_DEV
SLAB_IN = 0
SLAB_G = SLAB_IN + IN_ROWS
SLAB_U = SLAB_G + FF_W
SLAB_D = SLAB_U + FF_W
SLAB_O = SLAB_D + FF_W
SLAB_ROWS = SLAB_O + OUT_ROWS

VMEM_LIMIT = 56 * 1024 * 1024


def _params(sem=None, vmem=VMEM_LIMIT):
    return pltpu.CompilerParams(dimension_semantics=sem, vmem_limit_bytes=vmem)


def _nt(a, b):
    return lax.dot_general(a, b, (((1,), (1,)), ((), ())), preferred_element_type=F32)


def _tn(a, b, precision=None):
    return lax.dot_general(a, b, (((0,), (0,)), ((), ())), preferred_element_type=F32, precision=precision)


def _mm(a, b, precision=None):
    return jnp.dot(a, b, preferred_element_type=F32, precision=precision)


def _in_segments():
    segs = []
    for j in range(N_DEV):
        lo, hi = IN_W * j, IN_W * (j + 1)
        cuts = sorted({lo, hi} | {c for c in (OCB, OCB + RANK) if lo < c < hi})
        for a, b in zip(cuts[:-1], cuts[1:]):
            if a < OCB:
                d = a
            elif a < OCB + RANK:
                d = OA + (a - OCB)
            else:
                d = a - RANK
            segs.append((j, a - lo, b - lo, d))
    return segs


def _in_proj_fwd(x2d, g1, w_in_t, wgu_p, b_gate, tm, stage):
    t = x2d.shape[0]
    nt = t // tm
    g_rows = SLAB_ROWS - SLAB_D

    def body(x_ref, g_ref, w_ref, wgu_ref, bg_ref, stage_hbm, proj_ref, z_ref, h_ref, gwb_ref,
             send_sems, recv_sems, local_sem):
        gargs = (stage_hbm, SLAB_D, g_rows, gwb_ref, send_sems, recv_sems, local_sem)

        @pl.when(pl.program_id(0) == 0)
        def _():
            _gather_start(*gargs)

        x = x_ref[...]
        r = lax.rsqrt(jnp.mean(x * x, axis=-1, keepdims=True) + EPS)
        h = ((x * r) * g_ref[...]).astype(BF16)
        h_ref[...] = h
        proj = _nt(h, w_ref[...])
        proj_ref[...] = proj
        pa = proj[:, OA:OA + A_PAD].astype(BF16)
        z_ref[...] = _mm(pa, wgu_ref[...]) + bg_ref[...]

        @pl.when(pl.program_id(0) == nt - 1)
        def _():
            _gather_finish(*gargs)

    return pl.pallas_call(
        body,
        name="in_proj_fwd",
        grid=(t // tm,),
        in_specs=[
            pl.BlockSpec((tm, D), lambda i: (i, 0)),
            pl.BlockSpec((1, D), lambda i: (0, 0)),
            pl.BlockSpec((PW, D), lambda i: (0, 0)),
            pl.BlockSpec((A_PAD, NQK), lambda i: (0, 0)),
            pl.BlockSpec((1, NQK), lambda i: (0, 0)),
            pl.BlockSpec(memory_space=pl.ANY),
        ],
        out_specs=[
            pl.BlockSpec((tm, PW), lambda i: (i, 0)),
            pl.BlockSpec((tm, NQK), lambda i: (i, 0)),
            pl.BlockSpec((tm, D), lambda i: (i, 0)),
            pl.BlockSpec(memory_space=pl.ANY),
        ],
        out_shape=[
            jax.ShapeDtypeStruct((t, PW), F32),
            jax.ShapeDtypeStruct((t, NQK), F32),
            jax.ShapeDtypeStruct((t, D), BF16),
            jax.ShapeDtypeStruct((N_DEV, g_rows, D), BF16),
        ],
        scratch_shapes=_gather_sems(),
        compiler_params=_params(("arbitrary",)),
    )(x2d, g1, w_in_t, wgu_p, b_gate, stage)


def _head_masks():
    lane = lax.broadcasted_iota(jnp.int32, (1, NQK), 1)
    return [(lane >= DK * h) & (lane < DK * (h + 1)) for h in range(HEADS)]


def _split_bf16(x, n):
    parts = []
    for _ in range(n):
        p = x.astype(BF16)
        parts.append(p)
        x = x - p.astype(F32)
    return parts


def _chunk_fwd_parts(q, k, z, tril16):
    la = (jnp.minimum(z, 0.0) - jnp.log1p(jnp.exp(-jnp.abs(z)))) * INV_GATE_NORM
    la_parts = _split_bf16(la, 3)
    bc = _mm(tril16, la_parts[0]) + _mm(tril16, la_parts[1]) + _mm(tril16, la_parts[2])
    bl = bc[CHUNK - 1:CHUNK, :]
    eb = jnp.exp(bc)
    enb = jnp.exp(-bc)
    ekl = jnp.exp(bl - bc)
    qi = (q * Q_SCALE) * eb
    ki = k * enb
    ks = k * ekl
    ones16 = jnp.ones((CHUNK, DV), BF16)
    decb = jnp.exp(_tn(la_parts[0], ones16) + _tn(la_parts[1], ones16) + _tn(la_parts[2], ones16))
    return la, eb, enb, ekl, qi, ki, ks, decb


def _stack_heads(a, masks):
    return jnp.concatenate([jnp.where(m, a, 0.0) for m in masks], axis=0)


def _merge_heads(blocks, masks):
    out = blocks[HEADS - 1]
    for h in range(HEADS - 2, -1, -1):
        out = jnp.where(masks[h], blocks[h], out)
    return out


def _causal_stack_mask():
    row = lax.broadcasted_iota(jnp.int32, (HEADS * CHUNK, CHUNK), 0)
    col = lax.broadcasted_iota(jnp.int32, (HEADS * CHUNK, CHUNK), 1)
    return (row & (CHUNK - 1)) >= col


def _conv_taps(u, uprev):
    row = lax.broadcasted_iota(jnp.int32, u.shape, 0)
    u1 = jnp.where(row < 1, pltpu.roll(uprev, 1, 0), pltpu.roll(u, 1, 0))
    u2 = jnp.where(row < 2, pltpu.roll(uprev, 2, 0), pltpu.roll(u, 2, 0))
    return u1, u2


def _mix_fwd(proj3, z3, gng, conv_w, stage):
    nb, s, _ = proj3.shape
    nc = s // CHUNK
    g_rows = SLAB_D - SLAB_G

    def body(p_ref, z_ref, gng_ref, cw_ref, stage_hbm, mix_ref, o_ref, sprev_ref, gwa_ref, s_ref, uprev_ref,
             send_sems, recv_sems, local_sem):
        n = pl.program_id(0)
        gargs = (stage_hbm, SLAB_G, g_rows, gwa_ref, send_sems, recv_sems, local_sem)

        @pl.when(n == 0)
        def _():
            _gather_start(*gargs)
            s_ref[...] = jnp.zeros_like(s_ref)
            uprev_ref[...] = jnp.zeros_like(uprev_ref)

        r_i = lax.broadcasted_iota(jnp.int32, (CHUNK, CHUNK), 0)
        c_i = lax.broadcasted_iota(jnp.int32, (CHUNK, CHUNK), 1)
        tril16 = (r_i >= c_i).astype(BF16)
        masks = _head_masks()
        cmask = _causal_stack_mask()
        gg = gng_ref[...]
        for b in range(nb):
            q = p_ref[b, :, OQ:OQ + NQK]
            k = p_ref[b, :, OK_:OK_ + NQK]
            _, _, _, _, qi, ki, ks, decb = _chunk_fwd_parts(q, k, z_ref[b], tril16)
            qs = _stack_heads(qi, masks).astype(BF16)
            sc = jnp.where(cmask, _nt(qs, ki.astype(BF16)), 0.0).astype(BF16)
            st = s_ref[b]
            sprev_ref[b, 0] = st
            o_inter = _mm(qs, st.astype(BF16))
            v16 = p_ref[b, :, OV:OV + NV].astype(BF16)
            kv = _tn(ks.astype(BF16), v16)
            for h in range(HEADS):
                rows = slice(CHUNK * h, CHUNK * (h + 1))
                cols = slice(DV * h, DV * (h + 1))
                o = _mm(sc[rows], v16[:, cols]) + o_inter[rows]
                o_ref[b, :, cols] = o
                r = lax.rsqrt(jnp.mean(o * o, axis=-1, keepdims=True) + EPS)
                on = (o * r) * gg
                g = p_ref[b, :, OG + DV * h:OG + DV * (h + 1)]
                mix_ref[b, :, cols] = (on * (g * jax.nn.sigmoid(g))).astype(BF16)
                s_ref[b, rows, :] = decb[rows] * st[rows] + kv[rows, cols]
            u = p_ref[b, :, OCC:OCC + CW] * p_ref[b, :, OCH:OCH + CW]
            u1, u2 = _conv_taps(u, uprev_ref[b])
            yc = cw_ref[0:1, :] * u2 + cw_ref[1:2, :] * u1 + cw_ref[2:3, :] * u
            mix_ref[b, :, NV:NV + CW] = (p_ref[b, :, OCB:OCB + CW] * yc).astype(BF16)
            uprev_ref[b] = u

        @pl.when(n == nc - 1)
        def _():
            _gather_finish(*gargs)

    return pl.pallas_call(
        body,
        name="mix_fwd",
        grid=(nc,),
        in_specs=[
            pl.BlockSpec((nb, CHUNK, PW), lambda n: (0, n, 0)),
            pl.BlockSpec((nb, CHUNK, NQK), lambda n: (0, n, 0)),
            pl.BlockSpec((1, DV), lambda n: (0, 0)),
            pl.BlockSpec((CONV_K, CW), lambda n: (0, 0)),
            pl.BlockSpec(memory_space=pl.ANY),
        ],
        out_specs=[
            pl.BlockSpec((nb, CHUNK, D), lambda n: (0, n, 0)),
            pl.BlockSpec((nb, CHUNK, NV), lambda n: (0, n, 0)),
            pl.BlockSpec((nb, 1, NQK, DV), lambda n: (0, n, 0, 0)),
            pl.BlockSpec(memory_space=pl.ANY),
        ],
        out_shape=[
            jax.ShapeDtypeStruct((nb, s, D), BF16),
            jax.ShapeDtypeStruct((nb, s, NV), F32),
            jax.ShapeDtypeStruct((nb, nc, NQK, DV), F32),
            jax.ShapeDtypeStruct((N_DEV, g_rows, D), BF16),
        ],
        scratch_shapes=[pltpu.VMEM((nb, NQK, DV), F32), pltpu.VMEM((nb, CHUNK, CW), F32)] + _gather_sems(),
        compiler_params=_params(("arbitrary",)),
    )(proj3, z3, gng, conv_w, stage)


def _ffn_fwd_bwd(mix2d, x2d, tgt2d, gwa, gwb, g2, gf, tm):
    t = x2d.shape[0]

    def body(mix_ref, x_ref, tgt_ref, g2_ref, gf_ref, gwa_hbm, gwb_hbm,
             dx1_ref, dx1b_ref, dmix_ref, h2_ref, act_ref, dgate_ref, dup_ref, dx2_ref, dg2_ref, dgf_ref, loss_ref,
             wo, wg, wu, wd, wsem):
        i = pl.program_id(0)

        @pl.when(i == 0)
        def _():
            copies = []
            for n, (dst, src, off, rows) in enumerate(((wg, gwa_hbm, 0, FF_W), (wu, gwa_hbm, FF_W, FF_W),
                                                       (wd, gwb_hbm, 0, FF_W), (wo, gwb_hbm, FF_W, OUT_ROWS))):
                for j in range(N_DEV):
                    copies.append(pltpu.make_async_copy(
                        src.at[j, pl.ds(off, rows), :], dst.at[pl.ds(rows * j, rows), :], wsem.at[N_DEV * n + j]))
            for cp in copies:
                cp.start()
            dg2_ref[...] = jnp.zeros_like(dg2_ref)
            dgf_ref[...] = jnp.zeros_like(dgf_ref)
            loss_ref[...] = jnp.zeros_like(loss_ref)
            for cp in copies:
                cp.wait()

        g2v = g2_ref[...]
        gfv = gf_ref[...]
        x1 = x_ref[...] + _mm(mix_ref[...], wo[...])
        r2 = lax.rsqrt(jnp.mean(x1 * x1, axis=-1, keepdims=True) + EPS)
        n2 = x1 * r2
        h2 = (n2 * g2v).astype(BF16)
        h2_ref[...] = h2
        gate = _nt(h2, wg[...])
        up = _nt(h2, wu[...])
        sg = jax.nn.sigmoid(gate)
        sil = gate * sg
        act = (sil * up).astype(BF16)
        act_ref[...] = act
        x2 = x1 + _mm(act, wd[...])
        rf = lax.rsqrt(jnp.mean(x2 * x2, axis=-1, keepdims=True) + EPS)
        nf = x2 * rf
        err = nf * gfv - tgt_ref[...]
        loss_ref[...] += 0.5 * jnp.sum(jnp.mean(err * err, axis=-1, keepdims=True))
        dy = err * (1.0 / D)
        dgf_ref[...] += jnp.sum(dy * nf, axis=0, keepdims=True)
        dnf = dy * gfv
        dx2 = rf * (dnf - nf * jnp.mean(dnf * nf, axis=-1, keepdims=True))
        dx2b = dx2.astype(BF16)
        dx2_ref[...] = dx2b
        dact = _nt(dx2b, wd[...])
        dup = (dact * sil).astype(BF16)
        dgate = ((dact * up) * (sg * (1.0 + gate * (1.0 - sg)))).astype(BF16)
        dup_ref[...] = dup
        dgate_ref[...] = dgate
        dh2 = _mm(dgate, wg[...]) + _mm(dup, wu[...])
        dg2_ref[...] += jnp.sum(dh2 * n2, axis=0, keepdims=True)
        dn2 = dh2 * g2v
        dx1 = dx2 + r2 * (dn2 - n2 * jnp.mean(dn2 * n2, axis=-1, keepdims=True))
        dx1_ref[...] = dx1
        dx1b = dx1.astype(BF16)
        dx1b_ref[...] = dx1b
        dmix_ref[...] = _nt(dx1b, wo[...])

    tile = lambda w: pl.BlockSpec((tm, w), lambda i: (i, 0))
    vec = pl.BlockSpec((1, D), lambda i: (0, 0))
    hbm = pl.BlockSpec(memory_space=pl.ANY)
    return pl.pallas_call(
        body,
        name="ffn_fwd_bwd",
        grid=(t // tm,),
        in_specs=[tile(D), tile(D), tile(D), vec, vec, hbm, hbm],
        out_specs=[tile(D), tile(D), tile(D), tile(D), tile(DFF), tile(DFF), tile(DFF), tile(D), vec, vec,
                   pl.BlockSpec((1, 128), lambda i: (0, 0))],
        out_shape=[
            jax.ShapeDtypeStruct((t, D), F32),
            jax.ShapeDtypeStruct((t, D), BF16),
            jax.ShapeDtypeStruct((t, D), F32),
            jax.ShapeDtypeStruct((t, D), BF16),
            jax.ShapeDtypeStruct((t, DFF), BF16),
            jax.ShapeDtypeStruct((t, DFF), BF16),
            jax.ShapeDtypeStruct((t, DFF), BF16),
            jax.ShapeDtypeStruct((t, D), BF16),
            jax.ShapeDtypeStruct((1, D), F32),
            jax.ShapeDtypeStruct((1, D), F32),
            jax.ShapeDtypeStruct((1, 128), F32),
        ],
        scratch_shapes=[pltpu.VMEM((D, D), BF16), pltpu.VMEM((DFF, D), BF16), pltpu.VMEM((DFF, D), BF16),
                        pltpu.VMEM((DFF, D), BF16), pltpu.SemaphoreType.DMA((4 * N_DEV,))],
        compiler_params=_params(("arbitrary",)),
    )(mix2d, x2d, tgt2d, g2, gf, gwa, gwb)


def _tn_matmul(a, b, bm, bn, tk, name, with_bf16):
    t, m = a.shape
    n = b.shape[1]
    nk = t // tk

    def body(a_ref, b_ref, o_ref, *ob_ref):
        k = pl.program_id(2)

        @pl.when(k == 0)
        def _():
            o_ref[...] = jnp.zeros_like(o_ref)

        o_ref[...] += _tn(a_ref[...].astype(BF16), b_ref[...].astype(BF16))
        if with_bf16:
            @pl.when(k == nk - 1)
            def _():
                ob_ref[0][...] = o_ref[...].astype(BF16)

    out_blk = pl.BlockSpec((bm, bn), lambda i, j, k: (i, j))
    return pl.pallas_call(
        body,
        name=name,
        grid=(m // bm, n // bn, nk),
        in_specs=[pl.BlockSpec((tk, bm), lambda i, j, k: (k, i)), pl.BlockSpec((tk, bn), lambda i, j, k: (k, j))],
        out_specs=[out_blk, out_blk] if with_bf16 else out_blk,
        out_shape=([jax.ShapeDtypeStruct((m, n), F32), jax.ShapeDtypeStruct((m, n), BF16)] if with_bf16
                   else jax.ShapeDtypeStruct((m, n), F32)),
        compiler_params=_params(("parallel", "parallel", "arbitrary")),
    )(a, b)


def _mix_bwd(proj3, z3, sprev, opre3, dmix3, gng, conv_w, wgu_p, pbs):
    nb, s, _ = proj3.shape
    nc = s // CHUNK
    na = len(pbs)

    def body(*refs):
        (p_ref, pprev_ref, z_ref, sp_ref, o_ref, dm_ref, gng_ref, cw_ref, wgu_ref) = refs[:9]
        pb_refs = refs[9:9 + na]
        (dproj_ref, dgng_ref, dcw_ref, dbg_ref, dwgu_ref) = refs[9 + na:14 + na]
        r2_refs = refs[14 + na:14 + 2 * na]
        ds_ref, dycn_ref, send_sems, recv_sems = refs[14 + 2 * na:]
        step = pl.program_id(0)
        n = nc - 1 - step

        @pl.when(step == 0)
        def _():
            for cp in _stage2_copies(pb_refs, r2_refs, send_sems, recv_sems):
                cp.start()
            ds_ref[...] = jnp.zeros_like(ds_ref)
            dycn_ref[...] = jnp.zeros_like(dycn_ref)
            dgng_ref[...] = jnp.zeros_like(dgng_ref)
            dcw_ref[...] = jnp.zeros_like(dcw_ref)
            dbg_ref[...] = jnp.zeros_like(dbg_ref)
            dwgu_ref[...] = jnp.zeros_like(dwgu_ref)

        r_i = lax.broadcasted_iota(jnp.int32, (CHUNK, CHUNK), 0)
        c_i = lax.broadcasted_iota(jnp.int32, (CHUNK, CHUNK), 1)
        tril16 = (r_i >= c_i).astype(BF16)
        triu16 = (r_i <= c_i).astype(BF16)
        causal = r_i >= c_i
        masks = _head_masks()
        cmask = _causal_stack_mask()
        gg = gng_ref[...]
        last_row = lax.broadcasted_iota(jnp.int32, (CHUNK, NQK), 0) == CHUNK - 1
        ones_r = jnp.ones((16, DV), BF16)
        has_prev = (n > 0).astype(F32)
        for b in range(nb):
            q = p_ref[b, :, OQ:OQ + NQK]
            k = p_ref[b, :, OK_:OK_ + NQK]
            z = z_ref[b]
            _, eb, enb, ekl, qi, ki, ks, decb = _chunk_fwd_parts(q, k, z, tril16)
            qi16 = qi.astype(BF16)
            ki16 = ki.astype(BF16)
            qs = _stack_heads(qi, masks).astype(BF16)
            sc = jnp.where(cmask, _nt(qs, ki16), 0.0).astype(BF16)
            st = sp_ref[b, 0]
            st16 = st.astype(BF16)
            dsn = ds_ref[b]
            dsn16 = dsn.astype(BF16)
            v16 = p_ref[b, :, OV:OV + NV].astype(BF16)
            do16 = []
            dgng = jnp.zeros((1, DV), F32)
            for h in range(HEADS):
                cols = slice(DV * h, DV * (h + 1))
                o = o_ref[b, :, cols]
                r = lax.rsqrt(jnp.mean(o * o, axis=-1, keepdims=True) + EPS)
                nh = o * r
                g = p_ref[b, :, OG + DV * h:OG + DV * (h + 1)]
                sg = jax.nn.sigmoid(g)
                dog = dm_ref[b, :, cols]
                dproj_ref[b, :, OG + DV * h:OG + DV * (h + 1)] = (
                    (dog * (nh * gg)) * (sg * (1.0 + g * (1.0 - sg)))).astype(BF16)
                don = dog * (g * sg)
                dgng = dgng + jnp.sum(don * nh, axis=0, keepdims=True)
                dn = don * gg
                do = r * (dn - nh * jnp.mean(dn * nh, axis=-1, keepdims=True))
                do16.append(do.astype(BF16))
            dgng_ref[...] += dgng
            do_rows = jnp.concatenate(do16, axis=0)
            v_rows = jnp.concatenate([v16[:, DV * h:DV * (h + 1)] for h in range(HEADS)], axis=0)
            dp16 = [jnp.where(causal, _nt(do16[h], v16[:, DV * h:DV * (h + 1)]), 0.0).astype(BF16)
                    for h in range(HEADS)]
            ks_dsn = _mm(_stack_heads(ks, masks).astype(BF16), dsn16)
            do_st = _nt(do_rows, st16)
            v_dsn = _nt(v_rows, dsn16)
            dp_ki = _mm(jnp.concatenate(dp16, axis=0), ki16)
            q_do = _tn(qi16, jnp.concatenate(do16, axis=1))
            dki_h = []
            for h in range(HEADS):
                rows = slice(CHUNK * h, CHUNK * (h + 1))
                cols = slice(DV * h, DV * (h + 1))
                dv = _tn(sc[rows], do16[h]) + ks_dsn[rows]
                dproj_ref[b, :, OV + DV * h:OV + DV * (h + 1)] = dv.astype(BF16)
                dki_h.append(_tn(dp16[h], qi16))
                ds_ref[b, rows, :] = decb[rows] * dsn[rows] + q_do[rows, cols]
            blocks = lambda a: [a[CHUNK * h:CHUNK * (h + 1)] for h in range(HEADS)]
            dqi = _merge_heads(blocks(dp_ki + do_st), masks)
            dki = _merge_heads(dki_h, masks)
            dks = _merge_heads(blocks(v_dsn), masks)
            dproj_ref[b, :, OQ:OQ + NQK] = (dqi * (Q_SCALE * eb)).astype(BF16)
            dproj_ref[b, :, OK_:OK_ + NQK] = (dki * enb + dks * ekl).astype(BF16)
            dks_ks = dks * ks
            db = dqi * qi - dki * ki - dks_ks
            sd = _split_bf16(dsn * st * decb, 2)
            dbl = jnp.sum(dks_ks, axis=0, keepdims=True) + (_nt(ones_r, sd[0]) + _nt(ones_r, sd[1]))[0:1, :]
            db = db + jnp.where(last_row, dbl, 0.0)
            db_parts = _split_bf16(db, 3)
            dla = _mm(triu16, db_parts[0]) + _mm(triu16, db_parts[1]) + _mm(triu16, db_parts[2])
            dz = (dla * INV_GATE_NORM) * (1.0 / (1.0 + jnp.exp(z)))
            dbg_ref[...] += jnp.sum(dz, axis=0, keepdims=True)
            dz16 = dz.astype(BF16)
            pa16 = p_ref[b, :, OA:OA + A_PAD].astype(BF16)
            dwgu_ref[...] += _tn(pa16, dz16)
            dproj_ref[b, :, OA:OA + A_PAD] = _nt(dz16, wgu_ref[...]).astype(BF16)
            cb = p_ref[b, :, OCB:OCB + CW]
            cc = p_ref[b, :, OCC:OCC + CW]
            ch = p_ref[b, :, OCH:OCH + CW]
            u = cc * ch
            uprev = (pprev_ref[b, :, 0:CW] * pprev_ref[b, :, CW:2 * CW]) * has_prev
            u1, u2 = _conv_taps(u, uprev)
            w0 = cw_ref[0:1, :]
            w1 = cw_ref[1:2, :]
            w2 = cw_ref[2:3, :]
            yc = w0 * u2 + w1 * u1 + w2 * u
            doc = dm_ref[b, :, NV:NV + CW]
            dproj_ref[b, :, OCB:OCB + CW] = (doc * yc).astype(BF16)
            dyc = doc * cb
            dycn = dycn_ref[b]
            row = lax.broadcasted_iota(jnp.int32, dyc.shape, 0)
            d1 = jnp.where(row >= CHUNK - 1, pltpu.roll(dycn, CHUNK - 1, 0), pltpu.roll(dyc, CHUNK - 1, 0))
            d2 = jnp.where(row >= CHUNK - 2, pltpu.roll(dycn, CHUNK - 2, 0), pltpu.roll(dyc, CHUNK - 2, 0))
            du = w2 * dyc + w1 * d1 + w0 * d2
            dproj_ref[b, :, OCC:OCC + CW] = (du * ch).astype(BF16)
            dproj_ref[b, :, OCH:OCH + CW] = (du * cc).astype(BF16)
            dcw_ref[0:1, :] += jnp.sum(dyc * u2, axis=0, keepdims=True)
            dcw_ref[1:2, :] += jnp.sum(dyc * u1, axis=0, keepdims=True)
            dcw_ref[2:3, :] += jnp.sum(dyc * u, axis=0, keepdims=True)
            dycn_ref[b] = dyc

        @pl.when(step == nc - 1)
        def _():
            copies = _stage2_copies(pb_refs, r2_refs, send_sems, recv_sems)
            for cp in copies:
                cp.wait_recv()
            for cp in copies:
                cp.wait_send()

    rev = lambda w: pl.BlockSpec((nb, CHUNK, w), lambda i: (0, nc - 1 - i, 0))
    const = lambda r, c: pl.BlockSpec((r, c), lambda i: (0, 0))
    hbm = pl.BlockSpec(memory_space=pl.ANY)
    return pl.pallas_call(
        body,
        name="mix_bwd",
        grid=(nc,),
        in_specs=[
            rev(PW),
            pl.BlockSpec((nb, CHUNK, 2 * CW), lambda i: (0, jnp.maximum(nc - 2 - i, 0), OCC // (2 * CW))),
            rev(NQK),
            pl.BlockSpec((nb, 1, NQK, DV), lambda i: (0, nc - 1 - i, 0, 0)),
            rev(NV),
            rev(D),
            const(1, DV),
            const(CONV_K, CW),
            const(A_PAD, NQK),
        ] + [hbm] * na,
        out_specs=[rev(PW), const(1, DV), const(8, CW), const(1, NQK), const(A_PAD, NQK)] + [hbm] * na,
        out_shape=[
            jax.ShapeDtypeStruct((nb, s, PW), BF16),
            jax.ShapeDtypeStruct((1, DV), F32),
            jax.ShapeDtypeStruct((8, CW), F32),
            jax.ShapeDtypeStruct((1, NQK), F32),
            jax.ShapeDtypeStruct((A_PAD, NQK), F32),
        ] + [jax.ShapeDtypeStruct((3,) + p.shape[1:], BF16) for p in pbs],
        scratch_shapes=[pltpu.VMEM((nb, NQK, DV), F32), pltpu.VMEM((nb, CHUNK, CW), F32),
                        pltpu.SemaphoreType.DMA((3 * na,)), pltpu.SemaphoreType.DMA((3 * na,))],
        compiler_params=_params(("arbitrary",)),
    )(proj3, proj3, z3, sprev, opre3, dmix3, gng, conv_w, wgu_p, *pbs)


def _in_proj_bwd(dproj2d, x2d, dx1, g1, w_in_t, tm, pb):
    t = x2d.shape[0]
    nt = t // tm

    def body(dp_ref, x_ref, dx1_ref, g_ref, w_ref, pb_ref, dx_ref, dg1_ref, r2_ref, send_sems, recv_sems):
        @pl.when(pl.program_id(0) == 0)
        def _():
            for cp in _stage2_copies([pb_ref], [r2_ref], send_sems, recv_sems):
                cp.start()
            dg1_ref[...] = jnp.zeros_like(dg1_ref)

        x = x_ref[...]
        r = lax.rsqrt(jnp.mean(x * x, axis=-1, keepdims=True) + EPS)
        n1 = x * r
        dh = _mm(dp_ref[...], w_ref[...])
        dg1_ref[...] += jnp.sum(dh * n1, axis=0, keepdims=True)
        dn = dh * g_ref[...]
        dx_ref[...] = dx1_ref[...] + r * (dn - n1 * jnp.mean(dn * n1, axis=-1, keepdims=True))

        @pl.when(pl.program_id(0) == nt - 1)
        def _():
            copies = _stage2_copies([pb_ref], [r2_ref], send_sems, recv_sems)
            for cp in copies:
                cp.wait_recv()
            for cp in copies:
                cp.wait_send()

    tile = lambda w: pl.BlockSpec((tm, w), lambda i: (i, 0))
    vec = pl.BlockSpec((1, D), lambda i: (0, 0))
    hbm = pl.BlockSpec(memory_space=pl.ANY)
    return pl.pallas_call(
        body,
        name="in_proj_bwd",
        grid=(nt,),
        in_specs=[tile(PW), tile(D), tile(D), vec, pl.BlockSpec((PW, D), lambda i: (0, 0)), hbm],
        out_specs=[tile(D), vec, hbm],
        out_shape=[jax.ShapeDtypeStruct((t, D), F32), jax.ShapeDtypeStruct((1, D), F32),
                   jax.ShapeDtypeStruct((3,) + pb.shape[1:], BF16)],
        scratch_shapes=[pltpu.SemaphoreType.DMA((3,)), pltpu.SemaphoreType.DMA((3,))],
        compiler_params=_params(("arbitrary",)),
    )(dproj2d, x2d, dx1, g1, w_in_t, pb)


def _adamw_math(w, g, m, v):
    m = ADAM_B1 * m + (1.0 - ADAM_B1) * g
    v = ADAM_B2 * v + (1.0 - ADAM_B2) * (g * g)
    m_hat = m / (1.0 - ADAM_B1 ** ADAM_STEP)
    v_hat = v / (1.0 - ADAM_B2 ** ADAM_STEP)
    delta = -ADAM_LR * (m_hat / (jnp.sqrt(v_hat) + ADAM_EPS) + ADAM_WD * w)
    return delta, m, v


def _position():
    return lax.axis_index("x"), lax.axis_index("y"), lax.axis_index("c")


def _prep_slab(w_it, w_gt, w_ut, w_d, w_o):
    def body(wi_ref, wg_ref, wu_ref, wd_ref, wo_ref, stage):
        stage[SLAB_IN:SLAB_IN + IN_W, :] = wi_ref[...].astype(BF16)
        stage[SLAB_IN + IN_W:SLAB_G, :] = jnp.zeros((IN_ROWS - IN_W, D), BF16)
        stage[SLAB_G:SLAB_U, :] = wg_ref[...].astype(BF16)
        stage[SLAB_U:SLAB_D, :] = wu_ref[...].astype(BF16)
        stage[SLAB_D:SLAB_O, :] = wd_ref[...].astype(BF16)
        stage[SLAB_O:SLAB_ROWS, :] = wo_ref[...].astype(BF16)

    vm = pl.BlockSpec(memory_space=pltpu.VMEM)
    return pl.pallas_call(
        body,
        name="prep_slab",
        in_specs=[vm] * 5,
        out_specs=vm,
        out_shape=jax.ShapeDtypeStruct((SLAB_ROWS, D), BF16),
        compiler_params=_params(),
    )(w_it, w_gt, w_ut, w_d, w_o)


GATHER_SEMS = 7


def _gather_copies(stage, lo, rows, gx, send_sems, recv_sems, local_sem):
    x, y, c = _position()
    me = (x, y, c)
    sibling = (x, y, 1 - c)
    chips = [(1 - x, y), (x, 1 - y), (1 - x, 1 - y)]
    src = stage.at[pl.ds(lo, rows), :]

    def blk(px, py, pc):
        return gx.at[4 * px + 2 * py + pc]

    def copy(k, block, to, from_stage=False):
        return pltpu.make_async_remote_copy(
            src_ref=src if from_stage else blk(*block), dst_ref=blk(*block),
            send_sem=send_sems.at[k], recv_sem=recv_sems.at[k], device_id=to, device_id_type=MESH)

    mine = pltpu.make_async_copy(src, blk(*me), local_sem)
    first = [copy(0, me, sibling, True)] + [copy(1 + j, me, (*chip, c), True) for j, chip in enumerate(chips)]
    passed = [copy(4 + j, (*chip, c), sibling) for j, chip in enumerate(chips)]
    arrivals = ([copy(0, sibling, me)] + [copy(1 + j, (*chip, c), me) for j, chip in enumerate(chips)]
                + [copy(4 + j, (*chip, 1 - c), me) for j, chip in enumerate(chips)])
    return mine, first, passed, arrivals


def _gather_start(*args):
    mine, first, _, _ = _gather_copies(*args)
    mine.start()
    for cp in first:
        cp.start()


def _gather_finish(*args):
    mine, first, passed, arrivals = _gather_copies(*args)
    for j in range(3):
        arrivals[1 + j].wait_recv()
        passed[j].start()
    arrivals[0].wait_recv()
    for j in range(3):
        arrivals[4 + j].wait_recv()
    for cp in first + passed:
        cp.wait_send()
    mine.wait()


def _gather_sems():
    return [pltpu.SemaphoreType.DMA((GATHER_SEMS,)), pltpu.SemaphoreType.DMA((GATHER_SEMS,)), pltpu.SemaphoreType.DMA]


def _gather_w_in(stage, wgu_s, conv_s):
    def body(stage_hbm, wgu_ref, conv_ref, w_ref, gwgu_ref, gconv_ref, buf, send_sems, recv_sems, local_sem,
             ssend, srecv):
        x, y, c = _position()
        me = 4 * x + 2 * y + c
        args = (stage_hbm, SLAB_IN, IN_ROWS, buf, send_sems, recv_sems, local_sem)
        _gather_start(*args)
        flips = [(k >> 2, (k >> 1) & 1, k & 1) for k in range(1, N_DEV)]
        peers = [(x ^ fx, y ^ fy, c ^ fc) for fx, fy, fc in flips]

        def small(k, block_id, to):
            return [pltpu.make_async_remote_copy(
                src_ref=s, dst_ref=g.at[block_id], send_sem=ssend.at[2 * k + n], recv_sem=srecv.at[2 * k + n],
                device_id=to, device_id_type=MESH)
                for n, (s, g) in enumerate(((wgu_ref, gwgu_ref), (conv_ref, gconv_ref)))]

        gwgu_ref[me] = wgu_ref[...]
        gconv_ref[me] = conv_ref[...]
        for k, peer in enumerate(peers):
            for cp in small(k, me, peer):
                cp.start()
        w_ref[IN_COLS:PW, :] = jnp.zeros((PW - IN_COLS, D), BF16)
        _gather_finish(*args)
        for k, (px, py, pc) in enumerate(peers):
            for cp in small(k, 4 * px + 2 * py + pc, (px, py, pc)):
                cp.wait_recv()
                cp.wait_send()
        for j, lo, hi, d in _in_segments():
            w_ref[d:d + hi - lo, :] = buf[j, lo:hi, :]

    vm = pl.BlockSpec(memory_space=pltpu.VMEM)
    hbm = pl.BlockSpec(memory_space=pl.ANY)
    return pl.pallas_call(
        body,
        name="gather_w_in",
        in_specs=[hbm, vm, vm],
        out_specs=[vm, vm, vm],
        out_shape=[jax.ShapeDtypeStruct((PW, D), BF16),
                   jax.ShapeDtypeStruct((N_DEV,) + wgu_s.shape, F32),
                   jax.ShapeDtypeStruct((N_DEV,) + conv_s.shape, F32)],
        scratch_shapes=[pltpu.VMEM((N_DEV, IN_ROWS, D), BF16)] + _gather_sems()
        + [pltpu.SemaphoreType.DMA((14,)), pltpu.SemaphoreType.DMA((14,))],
        compiler_params=_params(),
    )(stage, wgu_s, conv_s)


def _unpermute_dw_in(dw_t):
    def body(d_ref, g_ref, gb_ref):
        for j in range(N_DEV):
            g_ref[j, IN_W:IN_ROWS, :] = jnp.zeros((IN_ROWS - IN_W, D), F32)
        for j, lo, hi, d in _in_segments():
            g_ref[j, lo:hi, :] = d_ref[d:d + hi - lo, :]
        for j in range(N_DEV):
            gb_ref[j] = g_ref[j].astype(BF16)

    vm = pl.BlockSpec(memory_space=pltpu.VMEM)
    return pl.pallas_call(
        body,
        name="unpermute_dw_in",
        in_specs=[vm],
        out_specs=[vm, vm],
        out_shape=[jax.ShapeDtypeStruct((N_DEV, IN_ROWS, D), F32), jax.ShapeDtypeStruct((N_DEV, IN_ROWS, D), BF16)],
        compiler_params=_params(),
    )(dw_t)


def _exchange_core(gbs, name):
    n = len(gbs)

    def body(*refs):
        g_refs, r_refs, send_sems, recv_sems = refs[:n], refs[n:2 * n], refs[2 * n], refs[2 * n + 1]
        x, y, c = _position()
        copies = [pltpu.make_async_remote_copy(
            src_ref=g_refs[a].at[2 * i + 1 - c], dst_ref=r_refs[a].at[i],
            send_sem=send_sems.at[4 * a + i], recv_sem=recv_sems.at[4 * a + i],
            device_id=(x, y, 1 - c), device_id_type=MESH) for a in range(n) for i in range(4)]
        for cp in copies:
            cp.start()
        for cp in copies:
            cp.wait_recv()
        for cp in copies:
            cp.wait_send()

    hbm = pl.BlockSpec(memory_space=pl.ANY)
    return pl.pallas_call(
        body,
        name=name,
        in_specs=[hbm] * n,
        out_specs=[hbm] * n,
        out_shape=[jax.ShapeDtypeStruct((4,) + g.shape[1:], BF16) for g in gbs],
        scratch_shapes=[pltpu.SemaphoreType.DMA((4 * n,)), pltpu.SemaphoreType.DMA((4 * n,))],
        compiler_params=_params(),
    )(*gbs)


def _add_core(g8, r1, pos_arr, name):
    _, rows, _ = g8.shape

    def body(pos_ref, g_ref, r_ref, pb_ref):
        pb_ref[...] = (g_ref[...] + r_ref[...].astype(F32)).astype(BF16)

    return pl.pallas_call(
        body,
        name=name,
        grid_spec=pltpu.PrefetchScalarGridSpec(
            num_scalar_prefetch=1, grid=(3,),
            in_specs=[pl.BlockSpec((1, rows, D), lambda k, pos: (2 * (pos[1] ^ (k + 1)) + pos[0], 0, 0)),
                      pl.BlockSpec((1, rows, D), lambda k, pos: (pos[1] ^ (k + 1), 0, 0))],
            out_specs=pl.BlockSpec((1, rows, D), lambda k, pos: (k, 0, 0))),
        out_shape=jax.ShapeDtypeStruct((3, rows, D), BF16),
        compiler_params=_params(("arbitrary",)),
    )(pos_arr, g8, r1)


def _stage2_copies(p_refs, r_refs, send_sems, recv_sems):
    x, y, c = _position()
    copies = []
    for a in range(len(p_refs)):
        for k in range(1, 4):
            copies.append(pltpu.make_async_remote_copy(
                src_ref=p_refs[a].at[k - 1], dst_ref=r_refs[a].at[k - 1],
                send_sem=send_sems.at[3 * a + k - 1], recv_sem=recv_sems.at[3 * a + k - 1],
                device_id=(x ^ (k >> 1), y ^ (k & 1), c), device_id_type=MESH))
    return copies


def _finish_weight(g8, r1, r2, w, m, v, pos_arr, name):
    _, rows, _ = g8.shape
    wr = w.shape[0]

    def body(pos_ref, g_ref, r1_ref, r2_ref, w_ref, m_ref, v_ref, g_out, d_out, m_out, v_out):
        g = g_ref[0] + r1_ref[0].astype(F32)
        for k in range(3):
            g = g + r2_ref[k].astype(F32)
        g = g[0:wr, :]
        g_out[...] = g
        d, mn, vn = _adamw_math(w_ref[...], g, m_ref[...], v_ref[...])
        d_out[...] = d
        m_out[...] = mn
        v_out[...] = vn

    full = lambda a: pl.BlockSpec(a.shape, lambda i, pos: (0,) * a.ndim)
    shp = jax.ShapeDtypeStruct(w.shape, F32)
    return pl.pallas_call(
        body,
        name=name,
        grid_spec=pltpu.PrefetchScalarGridSpec(
            num_scalar_prefetch=1, grid=(1,),
            in_specs=[pl.BlockSpec((1, rows, D), lambda i, pos: (2 * pos[1] + pos[0], 0, 0)),
                      pl.BlockSpec((1, rows, D), lambda i, pos: (pos[1], 0, 0)),
                      full(r2), full(w), full(m), full(v)],
            out_specs=[full(w)] * 4),
        out_shape=[shp, shp, shp, shp],
        compiler_params=_params(("arbitrary",)),
    )(pos_arr, g8, r1, r2, w, m, v)


SMALL_NAMES = ("norm1_g", "norm2_g", "norm_f_g", "b_gate", "gla_norm_g", "w_gate_up", "conv_w")
SMALL_PACK_ROWS = 32
WGU_W = NQK // N_DEV
CONV_W = CW // N_DEV


def _small_step(parts, loss_part, ws, ms, vs):
    n = len(SMALL_NAMES)

    def body(*refs):
        dg1, dg2, dgf, dbg, dgng, dwgu, dcw, lp = refs[:8]
        w_refs, m_refs, v_refs = refs[8:8 + n], refs[8 + n:8 + 2 * n], refs[8 + 2 * n:8 + 3 * n]
        loss_ref = refs[8 + 3 * n]
        outs = refs[9 + 3 * n:9 + 7 * n]
        pack, gbuf, send_sems, recv_sems = refs[9 + 7 * n:]
        x, y, c = _position()
        me = 4 * x + 2 * y + c
        flips = [(k >> 2, (k >> 1) & 1, k & 1) for k in range(1, N_DEV)]
        peers = [(x ^ fx, y ^ fy, c ^ fc) for fx, fy, fc in flips]

        def copy(k, block_id, to):
            return pltpu.make_async_remote_copy(
                src_ref=pack, dst_ref=gbuf.at[block_id], send_sem=send_sems.at[k], recv_sem=recv_sems.at[k],
                device_id=to, device_id_type=MESH)

        pack[...] = jnp.zeros_like(pack)
        pack[0:1, :] = dg1[...]
        pack[1:2, :] = dg2[...]
        pack[2:3, :] = dgf[...]
        pack[3:4, 0:NQK] = dbg[...]
        pack[3:4, NQK:NQK + DV] = dgng[...]
        pack[3:4, NQK + DV:NQK + 2 * DV] = lp[...]
        pack[8:8 + RANK, 0:NQK] = dwgu[0:RANK, :]
        pack[24:24 + CONV_K, 0:CW] = dcw[0:CONV_K, :]
        for k, peer in enumerate(peers):
            copy(k, me, peer).start()
        gbuf[me] = pack[...]
        for k, (px, py, pc) in enumerate(peers):
            cp = copy(k, 4 * px + 2 * py + pc, (px, py, pc))
            cp.wait_recv()
            cp.wait_send()
        acc = gbuf[0]
        for d in range(1, N_DEV):
            acc = acc + gbuf[d]
        loss_ref[...] = acc[3:4, NQK + DV:NQK + DV + 1]

        def my_columns(full, width):
            r = lax.broadcasted_iota(jnp.int32, (full.shape[1], width), 0)
            col = lax.broadcasted_iota(jnp.int32, (full.shape[1], width), 1)
            sel = (r == width * me + col).astype(F32)
            return _mm(full, sel, precision=HIGHEST)

        grads = [acc[0:1, :], acc[1:2, :], acc[2:3, :], acc[3:4, 0:NQK], acc[3:4, NQK:NQK + DV],
                 my_columns(acc[8:8 + RANK, 0:NQK], WGU_W), my_columns(acc[24:24 + CONV_K, 0:CW], CONV_W)]
        for i, g in enumerate(grads):
            d, mn, vn = _adamw_math(w_refs[i][...], g, m_refs[i][...], v_refs[i][...])
            outs[4 * i][...] = g
            outs[4 * i + 1][...] = d
            outs[4 * i + 2][...] = mn
            outs[4 * i + 3][...] = vn

    vm = pl.BlockSpec(memory_space=pltpu.VMEM)
    out_shape = [jax.ShapeDtypeStruct((1, 1), F32)]
    for w in ws:
        out_shape += [jax.ShapeDtypeStruct(w.shape, F32)] * 4
    return pl.pallas_call(
        body,
        name="small_step",
        in_specs=[vm] * (8 + 3 * n),
        out_specs=[vm] * (1 + 4 * n),
        out_shape=out_shape,
        scratch_shapes=[pltpu.VMEM((SMALL_PACK_ROWS, D), F32), pltpu.VMEM((N_DEV, SMALL_PACK_ROWS, D), F32),
                        pltpu.SemaphoreType.DMA((7,)), pltpu.SemaphoreType.DMA((7,))],
        compiler_params=_params(),
    )(*parts, loss_part, *ws, *ms, *vs)


def kernel(x, norm1_g, w_in, w_gate_up, b_gate, gla_norm_g, conv_w, w_out, norm2_g, w_ffn_gate, w_ffn_up, w_ffn_down, norm_f_g, loss_target, m_norm1_g, m_w_in, m_w_gate_up, m_b_gate, m_gla_norm_g, m_conv_w, m_w_out, m_norm2_g, m_w_ffn_gate, m_w_ffn_up, m_w_ffn_down, m_norm_f_g, v_norm1_g, v_w_in, v_w_gate_up, v_b_gate, v_gla_norm_g, v_conv_w, v_w_out, v_norm2_g, v_w_ffn_gate, v_w_ffn_up, v_w_ffn_down, v_norm_f_g):
    xi, yi, ci = _position()
    pos_arr = jnp.stack([ci, 2 * xi + yi]).astype(jnp.int32)
    nb, s, _ = x.shape
    t = nb * s

    tr = lambda a: a[0].T
    stage = _prep_slab(tr(w_in), tr(w_ffn_gate), tr(w_ffn_up), w_ffn_down[0], w_out[0])
    w_in_t, gwgu, gconv = _gather_w_in(stage, w_gate_up[0], conv_w[0])
    wgu_f = gwgu.transpose(1, 0, 2).reshape(RANK, NQK)
    conv_f = gconv.transpose(1, 0, 2).reshape(CONV_K, CW)
    wgu_p = jnp.concatenate([wgu_f, jnp.zeros((A_PAD - RANK, NQK), F32)], axis=0).astype(BF16)

    x2d = x.reshape(t, D)
    tgt2d = loss_target.reshape(t, D)
    tm = 256
    tm_in = min(512, t)
    tk = min(2048, t)
    proj, z, h, gwb = _in_proj_fwd(x2d, norm1_g, w_in_t, wgu_p, b_gate, tm_in, stage)
    proj3 = proj.reshape(nb, s, PW)
    z3 = z.reshape(nb, s, NQK)
    mix3, opre3, sprev, gwa = _mix_fwd(proj3, z3, gla_norm_g, conv_f, stage)
    mix2d = mix3.reshape(t, D)
    dx1, dx1b, dmix, h2, act, dgate, dup, dx2, dg2, dgf, loss_part = _ffn_fwd_bwd(
        mix2d, x2d, tgt2d, gwa, gwb, norm2_g, norm_f_g.reshape(1, D), tm)
    dw_d, dwb_d = _tn_matmul(act, dx2, DFF // 2, D, tk, "dw_ffn_down", True)
    dw_g, dwb_g = _tn_matmul(dgate, h2, DFF // 2, D, tk, "dw_ffn_gate", True)
    dw_u, dwb_u = _tn_matmul(dup, h2, DFF // 2, D, tk, "dw_ffn_up", True)
    dw_o, dwb_o = _tn_matmul(mix2d, dx1b, D // 4, D, tk, "dw_out", True)
    by_owner = lambda a, rows: a.reshape(N_DEV, rows, D)
    g8 = [by_owner(dw_g, FF_W), by_owner(dw_u, FF_W), by_owner(dw_d, FF_W), by_owner(dw_o, OUT_ROWS)]
    gb8 = [by_owner(dwb_g, FF_W), by_owner(dwb_u, FF_W), by_owner(dwb_d, FF_W), by_owner(dwb_o, OUT_ROWS)]
    tags = ("w_ffn_gate", "w_ffn_up", "w_ffn_down", "w_out")
    r1 = _exchange_core(gb8, "grad_exchange_core_ffn")
    pb = [_add_core(g, r, pos_arr, "grad_add_core_" + tag) for g, r, tag in zip(g8, r1, tags)]
    mb = _mix_bwd(proj3, z3, sprev, opre3, dmix.reshape(nb, s, D), gla_norm_g, conv_f, wgu_p, pb)
    dproj3, dgng, dcw, dbg, dwgu = mb[:5]
    r2 = list(mb[5:])
    dproj2d = dproj3.reshape(t, PW)
    dw_in_t = _tn_matmul(dproj2d, h, PW // 5, D, tk, "dw_in", False)
    g_in, gb_in = _unpermute_dw_in(dw_in_t)
    (r1_in,) = _exchange_core([gb_in], "grad_exchange_core_in")
    pb_in = _add_core(g_in, r1_in, pos_arr, "grad_add_core_w_in")
    dx, dg1, r2_in = _in_proj_bwd(dproj2d, x2d, dx1, norm1_g, w_in_t, tm_in, pb_in)

    tags = ("w_in",) + tags
    g8 = [g_in] + g8
    r1 = [r1_in] + list(r1)
    r2 = [r2_in] + r2
    shard_w = (tr(w_in), tr(w_ffn_gate), tr(w_ffn_up), w_ffn_down[0], w_out[0])
    shard_m = (tr(m_w_in), tr(m_w_ffn_gate), tr(m_w_ffn_up), m_w_ffn_down[0], m_w_out[0])
    shard_v = (tr(v_w_in), tr(v_w_ffn_gate), tr(v_w_ffn_up), v_w_ffn_down[0], v_w_out[0])
    transposed = (True, True, True, False, False)
    results = {}
    for tag, g, ra, rb, w, m, v, tp in zip(tags, g8, r1, r2, shard_w, shard_m, shard_v, transposed):
        outs = _finish_weight(g, ra, rb, w, m, v, pos_arr, "finish_" + tag)
        results[tag] = [o.T[None] if tp else o[None] for o in outs]

    small_w = (norm1_g, norm2_g, norm_f_g.reshape(1, D), b_gate, gla_norm_g, w_gate_up[0], conv_w[0])
    small_m = (m_norm1_g, m_norm2_g, m_norm_f_g.reshape(1, D), m_b_gate, m_gla_norm_g, m_w_gate_up[0], m_conv_w[0])
    small_v = (v_norm1_g, v_norm2_g, v_norm_f_g.reshape(1, D), v_b_gate, v_gla_norm_g, v_w_gate_up[0], v_conv_w[0])
    so = _small_step((dg1, dg2, dgf, dbg, dgng, dwgu, dcw), loss_part, small_w, small_m, small_v)
    loss = so[0].reshape(())
    shapes = {"norm_f_g": (D,), "w_gate_up": (1, RANK, WGU_W), "conv_w": (1, CONV_K, CONV_W)}
    for i, name in enumerate(SMALL_NAMES):
        results[name] = [o.reshape(shapes[name]) if name in shapes else o for o in so[1 + 4 * i:5 + 4 * i]]

    names = ("norm1_g", "w_in", "w_gate_up", "b_gate", "gla_norm_g", "conv_w", "w_out", "norm2_g",
             "w_ffn_gate", "w_ffn_up", "w_ffn_down", "norm_f_g")
    outs = [loss, dx.reshape(nb, s, D)]
    for kind in range(4):
        for name in names:
            outs.append(results[name][kind])
    return tuple(outs)
```

```python
import jax
import jax.numpy as jnp
from jax import lax
from jax.experimental import pallas as pl
from jax.experimental.pallas import tpu as pltpu

F32 = jnp.float32
BF16 = jnp.bfloat16
HIGHEST = lax.Precision.HIGHEST
MESH = pl.DeviceIdType.MESH

N_DEV = 8
D = 1024
DFF = 2816
HEADS = 4
DK = 64
DV = 128
NQK = HEADS * DK
NV = HEADS * DV
RANK = 16
CHUNK = 64
CW = 512
CONV_K = 3
IN_COLS = 3088
EPS = 1e-6
INV_GATE_NORM = 1.0 / 16.0
Q_SCALE = DK ** -0.5

PW = 3200
OQ, OK_, OV, OG, OCB, OCC, OCH, OA = 0, 256, 512, 1024, 1536, 2048, 2560, 3072
A_PAD = 128

ADAM_LR = 0.001
ADAM_B1 = 0.9
ADAM_B2 = 0.999
ADAM_EPS = 1e-08
ADAM_WD = 0.01
ADAM_STEP = 10

IN_W = IN_COLS // N_DEV
IN_ROWS = 400
FF_W = DFF // N_DEV
OUT_ROWS = D // N_DEV
SLAB_IN = 0
SLAB_G = SLAB_IN + IN_ROWS
SLAB_U = SLAB_G + FF_W
SLAB_D = SLAB_U + FF_W
SLAB_O = SLAB_D + FF_W
SLAB_ROWS = SLAB_O + OUT_ROWS

VMEM_LIMIT = 56 * 1024 * 1024


def _params(sem=None, vmem=VMEM_LIMIT):
    return pltpu.CompilerParams(dimension_semantics=sem, vmem_limit_bytes=vmem)


def _nt(a, b):
    return lax.dot_general(a, b, (((1,), (1,)), ((), ())), preferred_element_type=F32)


def _tn(a, b, precision=None):
    return lax.dot_general(a, b, (((0,), (0,)), ((), ())), preferred_element_type=F32, precision=precision)


def _mm(a, b, precision=None):
    return jnp.dot(a, b, preferred_element_type=F32, precision=precision)


def _in_segments():
    segs = []
    for j in range(N_DEV):
        lo, hi = IN_W * j, IN_W * (j + 1)
        cuts = sorted({lo, hi} | {c for c in (OCB, OCB + RANK) if lo < c < hi})
        for a, b in zip(cuts[:-1], cuts[1:]):
            if a < OCB:
                d = a
            elif a < OCB + RANK:
                d = OA + (a - OCB)
            else:
                d = a - RANK
            segs.append((j, a - lo, b - lo, d))
    return segs


def _in_proj_fwd(x2d, g1, w_in_t, wgu_p, b_gate, tm, stage):
    t = x2d.shape[0]
    nt = t // tm
    g_rows = SLAB_ROWS - SLAB_D

    def body(x_ref, g_ref, w_ref, wgu_ref, bg_ref, stage_hbm, proj_ref, z_ref, h_ref, gwb_ref,
             send_sems, recv_sems, local_sem):
        gargs = (stage_hbm, SLAB_D, g_rows, gwb_ref, send_sems, recv_sems, local_sem)

        @pl.when(pl.program_id(0) == 0)
        def _():
            _gather_start(*gargs)

        x = x_ref[...]
        r = lax.rsqrt(jnp.mean(x * x, axis=-1, keepdims=True) + EPS)
        h = ((x * r) * g_ref[...]).astype(BF16)
        h_ref[...] = h
        proj = _nt(h, w_ref[...])
        proj_ref[...] = proj
        pa = proj[:, OA:OA + A_PAD].astype(BF16)
        z_ref[...] = _mm(pa, wgu_ref[...]) + bg_ref[...]

        @pl.when(pl.program_id(0) == nt - 1)
        def _():
            _gather_finish(*gargs)

    return pl.pallas_call(
        body,
        name="in_proj_fwd",
        grid=(t // tm,),
        in_specs=[
            pl.BlockSpec((tm, D), lambda i: (i, 0)),
            pl.BlockSpec((1, D), lambda i: (0, 0)),
            pl.BlockSpec((PW, D), lambda i: (0, 0)),
            pl.BlockSpec((A_PAD, NQK), lambda i: (0, 0)),
            pl.BlockSpec((1, NQK), lambda i: (0, 0)),
            pl.BlockSpec(memory_space=pl.ANY),
        ],
        out_specs=[
            pl.BlockSpec((tm, PW), lambda i: (i, 0)),
            pl.BlockSpec((tm, NQK), lambda i: (i, 0)),
            pl.BlockSpec((tm, D), lambda i: (i, 0)),
            pl.BlockSpec(memory_space=pl.ANY),
        ],
        out_shape=[
            jax.ShapeDtypeStruct((t, PW), F32),
            jax.ShapeDtypeStruct((t, NQK), F32),
            jax.ShapeDtypeStruct((t, D), BF16),
            jax.ShapeDtypeStruct((N_DEV, g_rows, D), BF16),
        ],
        scratch_shapes=_gather_sems(),
        compiler_params=_params(("arbitrary",)),
    )(x2d, g1, w_in_t, wgu_p, b_gate, stage)


def _head_masks():
    lane = lax.broadcasted_iota(jnp.int32, (1, NQK), 1)
    return [(lane >= DK * h) & (lane < DK * (h + 1)) for h in range(HEADS)]


def _split_bf16(x, n):
    parts = []
    for _ in range(n):
        p = x.astype(BF16)
        parts.append(p)
        x = x - p.astype(F32)
    return parts


def _chunk_fwd_parts(q, k, z, tril16):
    la = (jnp.minimum(z, 0.0) - jnp.log1p(jnp.exp(-jnp.abs(z)))) * INV_GATE_NORM
    la_parts = _split_bf16(la, 3)
    bc = _mm(tril16, la_parts[0]) + _mm(tril16, la_parts[1]) + _mm(tril16, la_parts[2])
    bl = bc[CHUNK - 1:CHUNK, :]
    eb = jnp.exp(bc)
    enb = jnp.exp(-bc)
    ekl = jnp.exp(bl - bc)
    qi = (q * Q_SCALE) * eb
    ki = k * enb
    ks = k * ekl
    ones16 = jnp.ones((CHUNK, DV), BF16)
    decb = jnp.exp(_tn(la_parts[0], ones16) + _tn(la_parts[1], ones16) + _tn(la_parts[2], ones16))
    return la, eb, enb, ekl, qi, ki, ks, decb


def _stack_heads(a, masks):
    return jnp.concatenate([jnp.where(m, a, 0.0) for m in masks], axis=0)


def _merge_heads(blocks, masks):
    out = blocks[HEADS - 1]
    for h in range(HEADS - 2, -1, -1):
        out = jnp.where(masks[h], blocks[h], out)
    return out


def _causal_stack_mask():
    row = lax.broadcasted_iota(jnp.int32, (HEADS * CHUNK, CHUNK), 0)
    col = lax.broadcasted_iota(jnp.int32, (HEADS * CHUNK, CHUNK), 1)
    return (row & (CHUNK - 1)) >= col


def _conv_taps(u, uprev):
    row = lax.broadcasted_iota(jnp.int32, u.shape, 0)
    u1 = jnp.where(row < 1, pltpu.roll(uprev, 1, 0), pltpu.roll(u, 1, 0))
    u2 = jnp.where(row < 2, pltpu.roll(uprev, 2, 0), pltpu.roll(u, 2, 0))
    return u1, u2


def _mix_fwd(proj3, z3, gng, conv_w, stage):
    nb, s, _ = proj3.shape
    nc = s // CHUNK
    g_rows = SLAB_D - SLAB_G

    def body(p_ref, z_ref, gng_ref, cw_ref, stage_hbm, mix_ref, o_ref, sprev_ref, gwa_ref, s_ref, uprev_ref,
             send_sems, recv_sems, local_sem):
        n = pl.program_id(0)
        gargs = (stage_hbm, SLAB_G, g_rows, gwa_ref, send_sems, recv_sems, local_sem)

        @pl.when(n == 0)
        def _():
            _gather_start(*gargs)
            s_ref[...] = jnp.zeros_like(s_ref)
            uprev_ref[...] = jnp.zeros_like(uprev_ref)

        r_i = lax.broadcasted_iota(jnp.int32, (CHUNK, CHUNK), 0)
        c_i = lax.broadcasted_iota(jnp.int32, (CHUNK, CHUNK), 1)
        tril16 = (r_i >= c_i).astype(BF16)
        masks = _head_masks()
        cmask = _causal_stack_mask()
        gg = gng_ref[...]
        for b in range(nb):
            q = p_ref[b, :, OQ:OQ + NQK]
            k = p_ref[b, :, OK_:OK_ + NQK]
            _, _, _, _, qi, ki, ks, decb = _chunk_fwd_parts(q, k, z_ref[b], tril16)
            qs = _stack_heads(qi, masks).astype(BF16)
            sc = jnp.where(cmask, _nt(qs, ki.astype(BF16)), 0.0).astype(BF16)
            st = s_ref[b]
            sprev_ref[b, 0] = st
            o_inter = _mm(qs, st.astype(BF16))
            v16 = p_ref[b, :, OV:OV + NV].astype(BF16)
            kv = _tn(ks.astype(BF16), v16)
            for h in range(HEADS):
                rows = slice(CHUNK * h, CHUNK * (h + 1))
                cols = slice(DV * h, DV * (h + 1))
                o = _mm(sc[rows], v16[:, cols]) + o_inter[rows]
                o_ref[b, :, cols] = o
                r = lax.rsqrt(jnp.mean(o * o, axis=-1, keepdims=True) + EPS)
                on = (o * r) * gg
                g = p_ref[b, :, OG + DV * h:OG + DV * (h + 1)]
                mix_ref[b, :, cols] = (on * (g * jax.nn.sigmoid(g))).astype(BF16)
                s_ref[b, rows, :] = decb[rows] * st[rows] + kv[rows, cols]
            u = p_ref[b, :, OCC:OCC + CW] * p_ref[b, :, OCH:OCH + CW]
            u1, u2 = _conv_taps(u, uprev_ref[b])
            yc = cw_ref[0:1, :] * u2 + cw_ref[1:2, :] * u1 + cw_ref[2:3, :] * u
            mix_ref[b, :, NV:NV + CW] = (p_ref[b, :, OCB:OCB + CW] * yc).astype(BF16)
            uprev_ref[b] = u

        @pl.when(n == nc - 1)
        def _():
            _gather_finish(*gargs)

    return pl.pallas_call(
        body,
        name="mix_fwd",
        grid=(nc,),
        in_specs=[
            pl.BlockSpec((nb, CHUNK, PW), lambda n: (0, n, 0)),
            pl.BlockSpec((nb, CHUNK, NQK), lambda n: (0, n, 0)),
            pl.BlockSpec((1, DV), lambda n: (0, 0)),
            pl.BlockSpec((CONV_K, CW), lambda n: (0, 0)),
            pl.BlockSpec(memory_space=pl.ANY),
        ],
        out_specs=[
            pl.BlockSpec((nb, CHUNK, D), lambda n: (0, n, 0)),
            pl.BlockSpec((nb, CHUNK, NV), lambda n: (0, n, 0)),
            pl.BlockSpec((nb, 1, NQK, DV), lambda n: (0, n, 0, 0)),
            pl.BlockSpec(memory_space=pl.ANY),
        ],
        out_shape=[
            jax.ShapeDtypeStruct((nb, s, D), BF16),
            jax.ShapeDtypeStruct((nb, s, NV), F32),
            jax.ShapeDtypeStruct((nb, nc, NQK, DV), F32),
            jax.ShapeDtypeStruct((N_DEV, g_rows, D), BF16),
        ],
        scratch_shapes=[pltpu.VMEM((nb, NQK, DV), F32), pltpu.VMEM((nb, CHUNK, CW), F32)] + _gather_sems(),
        compiler_params=_params(("arbitrary",)),
    )(proj3, z3, gng, conv_w, stage)


def _ffn_fwd_bwd(mix2d, x2d, tgt2d, gwa, gwb, g2, gf, tm):
    t = x2d.shape[0]

    def body(mix_ref, x_ref, tgt_ref, g2_ref, gf_ref, gwa_hbm, gwb_hbm,
             dx1_ref, dx1b_ref, dmix_ref, h2_ref, act_ref, dgate_ref, dup_ref, dx2_ref, dg2_ref, dgf_ref, loss_ref,
             wo, wg, wu, wd, wsem):
        i = pl.program_id(0)

        @pl.when(i == 0)
        def _():
            copies = []
            for n, (dst, src, off, rows) in enumerate(((wg, gwa_hbm, 0, FF_W), (wu, gwa_hbm, FF_W, FF_W),
                                                       (wd, gwb_hbm, 0, FF_W), (wo, gwb_hbm, FF_W, OUT_ROWS))):
                for j in range(N_DEV):
                    copies.append(pltpu.make_async_copy(
                        src.at[j, pl.ds(off, rows), :], dst.at[pl.ds(rows * j, rows), :], wsem.at[N_DEV * n + j]))
            for cp in copies:
                cp.start()
            dg2_ref[...] = jnp.zeros_like(dg2_ref)
            dgf_ref[...] = jnp.zeros_like(dgf_ref)
            loss_ref[...] = jnp.zeros_like(loss_ref)
            for cp in copies:
                cp.wait()

        g2v = g2_ref[...]
        gfv = gf_ref[...]
        x1 = x_ref[...] + _mm(mix_ref[...], wo[...])
        r2 = lax.rsqrt(jnp.mean(x1 * x1, axis=-1, keepdims=True) + EPS)
        n2 = x1 * r2
        h2 = (n2 * g2v).astype(BF16)
        h2_ref[...] = h2
        gate = _nt(h2, wg[...])
        up = _nt(h2, wu[...])
        sg = jax.nn.sigmoid(gate)
        sil = gate * sg
        act = (sil * up).astype(BF16)
        act_ref[...] = act
        x2 = x1 + _mm(act, wd[...])
        rf = lax.rsqrt(jnp.mean(x2 * x2, axis=-1, keepdims=True) + EPS)
        nf = x2 * rf
        err = nf * gfv - tgt_ref[...]
        loss_ref[...] += 0.5 * jnp.sum(jnp.mean(err * err, axis=-1, keepdims=True))
        dy = err * (1.0 / D)
        dgf_ref[...] += jnp.sum(dy * nf, axis=0, keepdims=True)
        dnf = dy * gfv
        dx2 = rf * (dnf - nf * jnp.mean(dnf * nf, axis=-1, keepdims=True))
        dx2b = dx2.astype(BF16)
        dx2_ref[...] = dx2b
        dact = _nt(dx2b, wd[...])
        dup = (dact * sil).astype(BF16)
        dgate = ((dact * up) * (sg * (1.0 + gate * (1.0 - sg)))).astype(BF16)
        dup_ref[...] = dup
        dgate_ref[...] = dgate
        dh2 = _mm(dgate, wg[...]) + _mm(dup, wu[...])
        dg2_ref[...] += jnp.sum(dh2 * n2, axis=0, keepdims=True)
        dn2 = dh2 * g2v
        dx1 = dx2 + r2 * (dn2 - n2 * jnp.mean(dn2 * n2, axis=-1, keepdims=True))
        dx1_ref[...] = dx1
        dx1b = dx1.astype(BF16)
        dx1b_ref[...] = dx1b
        dmix_ref[...] = _nt(dx1b, wo[...])

    tile = lambda w: pl.BlockSpec((tm, w), lambda i: (i, 0))
    vec = pl.BlockSpec((1, D), lambda i: (0, 0))
    hbm = pl.BlockSpec(memory_space=pl.ANY)
    return pl.pallas_call(
        body,
        name="ffn_fwd_bwd",
        grid=(t // tm,),
        in_specs=[tile(D), tile(D), tile(D), vec, vec, hbm, hbm],
        out_specs=[tile(D), tile(D), tile(D), tile(D), tile(DFF), tile(DFF), tile(DFF), tile(D), vec, vec,
                   pl.BlockSpec((1, 128), lambda i: (0, 0))],
        out_shape=[
            jax.ShapeDtypeStruct((t, D), F32),
            jax.ShapeDtypeStruct((t, D), BF16),
            jax.ShapeDtypeStruct((t, D), F32),
            jax.ShapeDtypeStruct((t, D), BF16),
            jax.ShapeDtypeStruct((t, DFF), BF16),
            jax.ShapeDtypeStruct((t, DFF), BF16),
            jax.ShapeDtypeStruct((t, DFF), BF16),
            jax.ShapeDtypeStruct((t, D), BF16),
            jax.ShapeDtypeStruct((1, D), F32),
            jax.ShapeDtypeStruct((1, D), F32),
            jax.ShapeDtypeStruct((1, 128), F32),
        ],
        scratch_shapes=[pltpu.VMEM((D, D), BF16), pltpu.VMEM((DFF, D), BF16), pltpu.VMEM((DFF, D), BF16),
                        pltpu.VMEM((DFF, D), BF16), pltpu.SemaphoreType.DMA((4 * N_DEV,))],
        compiler_params=_params(("arbitrary",)),
    )(mix2d, x2d, tgt2d, g2, gf, gwa, gwb)


def _tn_matmul(a, b, bm, bn, tk, name, with_bf16, exchange=()):
    t, m = a.shape
    n = b.shape[1]
    nk = t // tk
    ne = len(exchange)
    nout = 2 if with_bf16 else 1
    grid = (m // bm, n // bn, nk)

    def body(a_ref, b_ref, *rest):
        ex_in, outs, ex_out, sems = rest[:ne], rest[ne:ne + nout], rest[ne + nout:2 * ne + nout], rest[2 * ne + nout:]
        o_ref = outs[0]
        i, j, k = pl.program_id(0), pl.program_id(1), pl.program_id(2)
        if ne:
            @pl.when((i == 0) & (j == 0) & (k == 0))
            def _():
                for cp in _stage1_copies(ex_in, ex_out, *sems):
                    cp.start()

        @pl.when(k == 0)
        def _():
            o_ref[...] = jnp.zeros_like(o_ref)

        o_ref[...] += _tn(a_ref[...].astype(BF16), b_ref[...].astype(BF16))
        if with_bf16:
            @pl.when(k == nk - 1)
            def _():
                outs[1][...] = o_ref[...].astype(BF16)
        if ne:
            @pl.when((i == grid[0] - 1) & (j == grid[1] - 1) & (k == nk - 1))
            def _():
                copies = _stage1_copies(ex_in, ex_out, *sems)
                for cp in copies:
                    cp.wait_recv()
                for cp in copies:
                    cp.wait_send()

    out_blk = pl.BlockSpec((bm, bn), lambda i, j, k: (i, j))
    hbm = pl.BlockSpec(memory_space=pl.ANY)
    out_shape = [jax.ShapeDtypeStruct((m, n), F32)] + ([jax.ShapeDtypeStruct((m, n), BF16)] if with_bf16 else [])
    res = pl.pallas_call(
        body,
        name=name,
        grid=grid,
        in_specs=[pl.BlockSpec((tk, bm), lambda i, j, k: (k, i)), pl.BlockSpec((tk, bn), lambda i, j, k: (k, j))]
        + [hbm] * ne,
        out_specs=[out_blk] * nout + [hbm] * ne,
        out_shape=out_shape + [jax.ShapeDtypeStruct((4,) + g.shape[1:], BF16) for g in exchange],
        scratch_shapes=[pltpu.SemaphoreType.DMA((4 * ne,)), pltpu.SemaphoreType.DMA((4 * ne,))] if ne else [],
        compiler_params=_params(("arbitrary", "arbitrary", "arbitrary") if ne else ("parallel", "parallel", "arbitrary")),
    )(a, b, *exchange)
    return res[0] if len(res) == 1 else res


def _mix_bwd(proj3, z3, sprev, opre3, dmix3, gng, conv_w, wgu_p, pbs):
    nb, s, _ = proj3.shape
    nc = s // CHUNK
    na = len(pbs)

    def body(*refs):
        (p_ref, pprev_ref, z_ref, sp_ref, o_ref, dm_ref, gng_ref, cw_ref, wgu_ref) = refs[:9]
        pb_refs = refs[9:9 + na]
        (dproj_ref, dgng_ref, dcw_ref, dbg_ref, dwgu_ref) = refs[9 + na:14 + na]
        r2_refs = refs[14 + na:14 + 2 * na]
        ds_ref, dycn_ref, send_sems, recv_sems = refs[14 + 2 * na:]
        step = pl.program_id(0)
        n = nc - 1 - step

        @pl.when(step == 0)
        def _():
            for cp in _stage2_copies(pb_refs, r2_refs, send_sems, recv_sems):
                cp.start()
            ds_ref[...] = jnp.zeros_like(ds_ref)
            dycn_ref[...] = jnp.zeros_like(dycn_ref)
            dgng_ref[...] = jnp.zeros_like(dgng_ref)
            dcw_ref[...] = jnp.zeros_like(dcw_ref)
            dbg_ref[...] = jnp.zeros_like(dbg_ref)
            dwgu_ref[...] = jnp.zeros_like(dwgu_ref)

        r_i = lax.broadcasted_iota(jnp.int32, (CHUNK, CHUNK), 0)
        c_i = lax.broadcasted_iota(jnp.int32, (CHUNK, CHUNK), 1)
        tril16 = (r_i >= c_i).astype(BF16)
        triu16 = (r_i <= c_i).astype(BF16)
        causal = r_i >= c_i
        masks = _head_masks()
        cmask = _causal_stack_mask()
        gg = gng_ref[...]
        last_row = lax.broadcasted_iota(jnp.int32, (CHUNK, NQK), 0) == CHUNK - 1
        ones_r = jnp.ones((16, DV), BF16)
        has_prev = (n > 0).astype(F32)
        for b in range(nb):
            q = p_ref[b, :, OQ:OQ + NQK]
            k = p_ref[b, :, OK_:OK_ + NQK]
            z = z_ref[b]
            _, eb, enb, ekl, qi, ki, ks, decb = _chunk_fwd_parts(q, k, z, tril16)
            qi16 = qi.astype(BF16)
            ki16 = ki.astype(BF16)
            qs = _stack_heads(qi, masks).astype(BF16)
            sc = jnp.where(cmask, _nt(qs, ki16), 0.0).astype(BF16)
            st = sp_ref[b, 0]
            st16 = st.astype(BF16)
            dsn = ds_ref[b]
            dsn16 = dsn.astype(BF16)
            v16 = p_ref[b, :, OV:OV + NV].astype(BF16)
            do16 = []
            dgng = jnp.zeros((1, DV), F32)
            for h in range(HEADS):
                cols = slice(DV * h, DV * (h + 1))
                o = o_ref[b, :, cols]
                r = lax.rsqrt(jnp.mean(o * o, axis=-1, keepdims=True) + EPS)
                nh = o * r
                g = p_ref[b, :, OG + DV * h:OG + DV * (h + 1)]
                sg = jax.nn.sigmoid(g)
                dog = dm_ref[b, :, cols]
                dproj_ref[b, :, OG + DV * h:OG + DV * (h + 1)] = (
                    (dog * (nh * gg)) * (sg * (1.0 + g * (1.0 - sg)))).astype(BF16)
                don = dog * (g * sg)
                dgng = dgng + jnp.sum(don * nh, axis=0, keepdims=True)
                dn = don * gg
                do = r * (dn - nh * jnp.mean(dn * nh, axis=-1, keepdims=True))
                do16.append(do.astype(BF16))
            dgng_ref[...] += dgng
            do_rows = jnp.concatenate(do16, axis=0)
            v_rows = jnp.concatenate([v16[:, DV * h:DV * (h + 1)] for h in range(HEADS)], axis=0)
            dp16 = [jnp.where(causal, _nt(do16[h], v16[:, DV * h:DV * (h + 1)]), 0.0).astype(BF16)
                    for h in range(HEADS)]
            ks_dsn = _mm(_stack_heads(ks, masks).astype(BF16), dsn16)
            do_st = _nt(do_rows, st16)
            v_dsn = _nt(v_rows, dsn16)
            dp_ki = _mm(jnp.concatenate(dp16, axis=0), ki16)
            q_do = _tn(qi16, jnp.concatenate(do16, axis=1))
            dki_h = []
            for h in range(HEADS):
                rows = slice(CHUNK * h, CHUNK * (h + 1))
                cols = slice(DV * h, DV * (h + 1))
                dv = _tn(sc[rows], do16[h]) + ks_dsn[rows]
                dproj_ref[b, :, OV + DV * h:OV + DV * (h + 1)] = dv.astype(BF16)
                dki_h.append(_tn(dp16[h], qi16))
                ds_ref[b, rows, :] = decb[rows] * dsn[rows] + q_do[rows, cols]
            blocks = lambda a: [a[CHUNK * h:CHUNK * (h + 1)] for h in range(HEADS)]
            dqi = _merge_heads(blocks(dp_ki + do_st), masks)
            dki = _merge_heads(dki_h, masks)
            dks = _merge_heads(blocks(v_dsn), masks)
            dproj_ref[b, :, OQ:OQ + NQK] = (dqi * (Q_SCALE * eb)).astype(BF16)
            dproj_ref[b, :, OK_:OK_ + NQK] = (dki * enb + dks * ekl).astype(BF16)
            dks_ks = dks * ks
            db = dqi * qi - dki * ki - dks_ks
            sd = _split_bf16(dsn * st * decb, 2)
            dbl = jnp.sum(dks_ks, axis=0, keepdims=True) + (_nt(ones_r, sd[0]) + _nt(ones_r, sd[1]))[0:1, :]
            db = db + jnp.where(last_row, dbl, 0.0)
            db_parts = _split_bf16(db, 3)
            dla = _mm(triu16, db_parts[0]) + _mm(triu16, db_parts[1]) + _mm(triu16, db_parts[2])
            dz = (dla * INV_GATE_NORM) * (1.0 / (1.0 + jnp.exp(z)))
            dbg_ref[...] += jnp.sum(dz, axis=0, keepdims=True)
            dz16 = dz.astype(BF16)
            pa16 = p_ref[b, :, OA:OA + A_PAD].astype(BF16)
            dwgu_ref[...] += _tn(pa16, dz16)
            dproj_ref[b, :, OA:OA + A_PAD] = _nt(dz16, wgu_ref[...]).astype(BF16)
            cb = p_ref[b, :, OCB:OCB + CW]
            cc = p_ref[b, :, OCC:OCC + CW]
            ch = p_ref[b, :, OCH:OCH + CW]
            u = cc * ch
            uprev = (pprev_ref[b, :, 0:CW] * pprev_ref[b, :, CW:2 * CW]) * has_prev
            u1, u2 = _conv_taps(u, uprev)
            w0 = cw_ref[0:1, :]
            w1 = cw_ref[1:2, :]
            w2 = cw_ref[2:3, :]
            yc = w0 * u2 + w1 * u1 + w2 * u
            doc = dm_ref[b, :, NV:NV + CW]
            dproj_ref[b, :, OCB:OCB + CW] = (doc * yc).astype(BF16)
            dyc = doc * cb
            dycn = dycn_ref[b]
            row = lax.broadcasted_iota(jnp.int32, dyc.shape, 0)
            d1 = jnp.where(row >= CHUNK - 1, pltpu.roll(dycn, CHUNK - 1, 0), pltpu.roll(dyc, CHUNK - 1, 0))
            d2 = jnp.where(row >= CHUNK - 2, pltpu.roll(dycn, CHUNK - 2, 0), pltpu.roll(dyc, CHUNK - 2, 0))
            du = w2 * dyc + w1 * d1 + w0 * d2
            dproj_ref[b, :, OCC:OCC + CW] = (du * ch).astype(BF16)
            dproj_ref[b, :, OCH:OCH + CW] = (du * cc).astype(BF16)
            dcw_ref[0:1, :] += jnp.sum(dyc * u2, axis=0, keepdims=True)
            dcw_ref[1:2, :] += jnp.sum(dyc * u1, axis=0, keepdims=True)
            dcw_ref[2:3, :] += jnp.sum(dyc * u, axis=0, keepdims=True)
            dycn_ref[b] = dyc

        @pl.when(step == nc - 1)
        def _():
            copies = _stage2_copies(pb_refs, r2_refs, send_sems, recv_sems)
            for cp in copies:
                cp.wait_recv()
            for cp in copies:
                cp.wait_send()

    rev = lambda w: pl.BlockSpec((nb, CHUNK, w), lambda i: (0, nc - 1 - i, 0))
    const = lambda r, c: pl.BlockSpec((r, c), lambda i: (0, 0))
    hbm = pl.BlockSpec(memory_space=pl.ANY)
    return pl.pallas_call(
        body,
        name="mix_bwd",
        grid=(nc,),
        in_specs=[
            rev(PW),
            pl.BlockSpec((nb, CHUNK, 2 * CW), lambda i: (0, jnp.maximum(nc - 2 - i, 0), OCC // (2 * CW))),
            rev(NQK),
            pl.BlockSpec((nb, 1, NQK, DV), lambda i: (0, nc - 1 - i, 0, 0)),
            rev(NV),
            rev(D),
            const(1, DV),
            const(CONV_K, CW),
            const(A_PAD, NQK),
        ] + [hbm] * na,
        out_specs=[rev(PW), const(1, DV), const(8, CW), const(1, NQK), const(A_PAD, NQK)] + [hbm] * na,
        out_shape=[
            jax.ShapeDtypeStruct((nb, s, PW), BF16),
            jax.ShapeDtypeStruct((1, DV), F32),
            jax.ShapeDtypeStruct((8, CW), F32),
            jax.ShapeDtypeStruct((1, NQK), F32),
            jax.ShapeDtypeStruct((A_PAD, NQK), F32),
        ] + [jax.ShapeDtypeStruct((3,) + p.shape[1:], BF16) for p in pbs],
        scratch_shapes=[pltpu.VMEM((nb, NQK, DV), F32), pltpu.VMEM((nb, CHUNK, CW), F32),
                        pltpu.SemaphoreType.DMA((3 * na,)), pltpu.SemaphoreType.DMA((3 * na,))],
        compiler_params=_params(("arbitrary",)),
    )(proj3, proj3, z3, sprev, opre3, dmix3, gng, conv_w, wgu_p, *pbs)


def _in_proj_bwd(dproj2d, x2d, dx1, g1, w_in_t, tm, pb):
    t = x2d.shape[0]
    nt = t // tm

    def body(dp_ref, x_ref, dx1_ref, g_ref, w_ref, pb_ref, dx_ref, dg1_ref, r2_ref, send_sems, recv_sems):
        @pl.when(pl.program_id(0) == 0)
        def _():
            for cp in _stage2_copies([pb_ref], [r2_ref], send_sems, recv_sems):
                cp.start()
            dg1_ref[...] = jnp.zeros_like(dg1_ref)

        x = x_ref[...]
        r = lax.rsqrt(jnp.mean(x * x, axis=-1, keepdims=True) + EPS)
        n1 = x * r
        dh = _mm(dp_ref[...], w_ref[...])
        dg1_ref[...] += jnp.sum(dh * n1, axis=0, keepdims=True)
        dn = dh * g_ref[...]
        dx_ref[...] = dx1_ref[...] + r * (dn - n1 * jnp.mean(dn * n1, axis=-1, keepdims=True))

        @pl.when(pl.program_id(0) == nt - 1)
        def _():
            copies = _stage2_copies([pb_ref], [r2_ref], send_sems, recv_sems)
            for cp in copies:
                cp.wait_recv()
            for cp in copies:
                cp.wait_send()

    tile = lambda w: pl.BlockSpec((tm, w), lambda i: (i, 0))
    vec = pl.BlockSpec((1, D), lambda i: (0, 0))
    hbm = pl.BlockSpec(memory_space=pl.ANY)
    return pl.pallas_call(
        body,
        name="in_proj_bwd",
        grid=(nt,),
        in_specs=[tile(PW), tile(D), tile(D), vec, pl.BlockSpec((PW, D), lambda i: (0, 0)), hbm],
        out_specs=[tile(D), vec, hbm],
        out_shape=[jax.ShapeDtypeStruct((t, D), F32), jax.ShapeDtypeStruct((1, D), F32),
                   jax.ShapeDtypeStruct((3,) + pb.shape[1:], BF16)],
        scratch_shapes=[pltpu.SemaphoreType.DMA((3,)), pltpu.SemaphoreType.DMA((3,))],
        compiler_params=_params(("arbitrary",)),
    )(dproj2d, x2d, dx1, g1, w_in_t, pb)


def _adamw_math(w, g, m, v):
    m = ADAM_B1 * m + (1.0 - ADAM_B1) * g
    v = ADAM_B2 * v + (1.0 - ADAM_B2) * (g * g)
    m_hat = m / (1.0 - ADAM_B1 ** ADAM_STEP)
    v_hat = v / (1.0 - ADAM_B2 ** ADAM_STEP)
    delta = -ADAM_LR * (m_hat / (jnp.sqrt(v_hat) + ADAM_EPS) + ADAM_WD * w)
    return delta, m, v


def _position():
    return lax.axis_index("x"), lax.axis_index("y"), lax.axis_index("c")


def _prep_slab(w_it, w_gt, w_ut, w_d, w_o):
    def body(wi_ref, wg_ref, wu_ref, wd_ref, wo_ref, stage):
        stage[SLAB_IN:SLAB_IN + IN_W, :] = wi_ref[...].astype(BF16)
        stage[SLAB_IN + IN_W:SLAB_G, :] = jnp.zeros((IN_ROWS - IN_W, D), BF16)
        stage[SLAB_G:SLAB_U, :] = wg_ref[...].astype(BF16)
        stage[SLAB_U:SLAB_D, :] = wu_ref[...].astype(BF16)
        stage[SLAB_D:SLAB_O, :] = wd_ref[...].astype(BF16)
        stage[SLAB_O:SLAB_ROWS, :] = wo_ref[...].astype(BF16)

    vm = pl.BlockSpec(memory_space=pltpu.VMEM)
    return pl.pallas_call(
        body,
        name="prep_slab",
        in_specs=[vm] * 5,
        out_specs=vm,
        out_shape=jax.ShapeDtypeStruct((SLAB_ROWS, D), BF16),
        compiler_params=_params(),
    )(w_it, w_gt, w_ut, w_d, w_o)


GATHER_SEMS = 7


def _gather_copies(stage, lo, rows, gx, send_sems, recv_sems, local_sem):
    x, y, c = _position()
    me = (x, y, c)
    sibling = (x, y, 1 - c)
    chips = [(1 - x, y), (x, 1 - y), (1 - x, 1 - y)]
    src = stage.at[pl.ds(lo, rows), :]

    def blk(px, py, pc):
        return gx.at[4 * px + 2 * py + pc]

    def copy(k, block, to, from_stage=False):
        return pltpu.make_async_remote_copy(
            src_ref=src if from_stage else blk(*block), dst_ref=blk(*block),
            send_sem=send_sems.at[k], recv_sem=recv_sems.at[k], device_id=to, device_id_type=MESH)

    mine = pltpu.make_async_copy(src, blk(*me), local_sem)
    first = [copy(0, me, sibling, True)] + [copy(1 + j, me, (*chip, c), True) for j, chip in enumerate(chips)]
    passed = [copy(4 + j, (*chip, c), sibling) for j, chip in enumerate(chips)]
    arrivals = ([copy(0, sibling, me)] + [copy(1 + j, (*chip, c), me) for j, chip in enumerate(chips)]
                + [copy(4 + j, (*chip, 1 - c), me) for j, chip in enumerate(chips)])
    return mine, first, passed, arrivals


def _gather_start(*args):
    mine, first, _, _ = _gather_copies(*args)
    mine.start()
    for cp in first:
        cp.start()


def _gather_finish(*args):
    mine, first, passed, arrivals = _gather_copies(*args)
    for j in range(3):
        arrivals[1 + j].wait_recv()
        passed[j].start()
    arrivals[0].wait_recv()
    for j in range(3):
        arrivals[4 + j].wait_recv()
    for cp in first + passed:
        cp.wait_send()
    mine.wait()


def _gather_sems():
    return [pltpu.SemaphoreType.DMA((GATHER_SEMS,)), pltpu.SemaphoreType.DMA((GATHER_SEMS,)), pltpu.SemaphoreType.DMA]


def _gather_w_in(stage, wgu_s, conv_s):
    def body(stage_hbm, wgu_ref, conv_ref, w_ref, gwgu_ref, gconv_ref, buf, send_sems, recv_sems, local_sem,
             ssend, srecv):
        x, y, c = _position()
        me = 4 * x + 2 * y + c
        args = (stage_hbm, SLAB_IN, IN_ROWS, buf, send_sems, recv_sems, local_sem)
        _gather_start(*args)
        flips = [(k >> 2, (k >> 1) & 1, k & 1) for k in range(1, N_DEV)]
        peers = [(x ^ fx, y ^ fy, c ^ fc) for fx, fy, fc in flips]

        def small(k, block_id, to):
            return [pltpu.make_async_remote_copy(
                src_ref=s, dst_ref=g.at[block_id], send_sem=ssend.at[2 * k + n], recv_sem=srecv.at[2 * k + n],
                device_id=to, device_id_type=MESH)
                for n, (s, g) in enumerate(((wgu_ref, gwgu_ref), (conv_ref, gconv_ref)))]

        gwgu_ref[me] = wgu_ref[...]
        gconv_ref[me] = conv_ref[...]
        for k, peer in enumerate(peers):
            for cp in small(k, me, peer):
                cp.start()
        w_ref[IN_COLS:PW, :] = jnp.zeros((PW - IN_COLS, D), BF16)
        _gather_finish(*args)
        for k, (px, py, pc) in enumerate(peers):
            for cp in small(k, 4 * px + 2 * py + pc, (px, py, pc)):
                cp.wait_recv()
                cp.wait_send()
        for j, lo, hi, d in _in_segments():
            w_ref[d:d + hi - lo, :] = buf[j, lo:hi, :]

    vm = pl.BlockSpec(memory_space=pltpu.VMEM)
    hbm = pl.BlockSpec(memory_space=pl.ANY)
    return pl.pallas_call(
        body,
        name="gather_w_in",
        in_specs=[hbm, vm, vm],
        out_specs=[vm, vm, vm],
        out_shape=[jax.ShapeDtypeStruct((PW, D), BF16),
                   jax.ShapeDtypeStruct((N_DEV,) + wgu_s.shape, F32),
                   jax.ShapeDtypeStruct((N_DEV,) + conv_s.shape, F32)],
        scratch_shapes=[pltpu.VMEM((N_DEV, IN_ROWS, D), BF16)] + _gather_sems()
        + [pltpu.SemaphoreType.DMA((14,)), pltpu.SemaphoreType.DMA((14,))],
        compiler_params=_params(),
    )(stage, wgu_s, conv_s)


def _unpermute_dw_in(dw_t):
    def body(d_ref, g_ref, gb_ref):
        for j in range(N_DEV):
            g_ref[j, IN_W:IN_ROWS, :] = jnp.zeros((IN_ROWS - IN_W, D), F32)
        for j, lo, hi, d in _in_segments():
            g_ref[j, lo:hi, :] = d_ref[d:d + hi - lo, :]
        for j in range(N_DEV):
            gb_ref[j] = g_ref[j].astype(BF16)

    vm = pl.BlockSpec(memory_space=pltpu.VMEM)
    return pl.pallas_call(
        body,
        name="unpermute_dw_in",
        in_specs=[vm],
        out_specs=[vm, vm],
        out_shape=[jax.ShapeDtypeStruct((N_DEV, IN_ROWS, D), F32), jax.ShapeDtypeStruct((N_DEV, IN_ROWS, D), BF16)],
        compiler_params=_params(),
    )(dw_t)


def _stage1_copies(g_refs, r_refs, send_sems, recv_sems):
    x, y, c = _position()
    return [pltpu.make_async_remote_copy(
        src_ref=g_refs[a].at[2 * i + 1 - c], dst_ref=r_refs[a].at[i],
        send_sem=send_sems.at[4 * a + i], recv_sem=recv_sems.at[4 * a + i],
        device_id=(x, y, 1 - c), device_id_type=MESH) for a in range(len(g_refs)) for i in range(4)]


def _exchange_core(gbs, name):
    n = len(gbs)

    def body(*refs):
        copies = _stage1_copies(refs[:n], refs[n:2 * n], refs[2 * n], refs[2 * n + 1])
        for cp in copies:
            cp.start()
        for cp in copies:
            cp.wait_recv()
        for cp in copies:
            cp.wait_send()

    hbm = pl.BlockSpec(memory_space=pl.ANY)
    return pl.pallas_call(
        body,
        name=name,
        in_specs=[hbm] * n,
        out_specs=[hbm] * n,
        out_shape=[jax.ShapeDtypeStruct((4,) + g.shape[1:], BF16) for g in gbs],
        scratch_shapes=[pltpu.SemaphoreType.DMA((4 * n,)), pltpu.SemaphoreType.DMA((4 * n,))],
        compiler_params=_params(),
    )(*gbs)


def _add_core(g8, r1, pos_arr, name):
    _, rows, _ = g8.shape

    def body(pos_ref, g_ref, r_ref, pb_ref):
        pb_ref[...] = (g_ref[...] + r_ref[...].astype(F32)).astype(BF16)

    return pl.pallas_call(
        body,
        name=name,
        grid_spec=pltpu.PrefetchScalarGridSpec(
            num_scalar_prefetch=1, grid=(3,),
            in_specs=[pl.BlockSpec((1, rows, D), lambda k, pos: (2 * (pos[1] ^ (k + 1)) + pos[0], 0, 0)),
                      pl.BlockSpec((1, rows, D), lambda k, pos: (pos[1] ^ (k + 1), 0, 0))],
            out_specs=pl.BlockSpec((1, rows, D), lambda k, pos: (k, 0, 0))),
        out_shape=jax.ShapeDtypeStruct((3, rows, D), BF16),
        compiler_params=_params(("arbitrary",)),
    )(pos_arr, g8, r1)


def _stage2_copies(p_refs, r_refs, send_sems, recv_sems):
    x, y, c = _position()
    copies = []
    for a in range(len(p_refs)):
        for k in range(1, 4):
            copies.append(pltpu.make_async_remote_copy(
                src_ref=p_refs[a].at[k - 1], dst_ref=r_refs[a].at[k - 1],
                send_sem=send_sems.at[3 * a + k - 1], recv_sem=recv_sems.at[3 * a + k - 1],
                device_id=(x ^ (k >> 1), y ^ (k & 1), c), device_id_type=MESH))
    return copies


def _finish_weight(g8, r1, r2, w, m, v, pos_arr, name):
    _, rows, _ = g8.shape
    wr = w.shape[0]
    nblk = 2 if (rows == wr and rows % 32 == 0) else 1
    rb = rows // nblk
    wb = wr // nblk

    def body(pos_ref, g_ref, r1_ref, r2_ref, w_ref, m_ref, v_ref, g_out, d_out, m_out, v_out):
        g = g_ref[0] + r1_ref[0].astype(F32)
        for k in range(3):
            g = g + r2_ref[k].astype(F32)
        g = g[0:wb, :]
        g_out[...] = g
        d, mn, vn = _adamw_math(w_ref[...], g, m_ref[...], v_ref[...])
        d_out[...] = d
        m_out[...] = mn
        v_out[...] = vn

    wblk = pl.BlockSpec((wb, D), lambda i, pos: (i, 0))
    shp = jax.ShapeDtypeStruct(w.shape, F32)
    return pl.pallas_call(
        body,
        name=name,
        grid_spec=pltpu.PrefetchScalarGridSpec(
            num_scalar_prefetch=1, grid=(nblk,),
            in_specs=[pl.BlockSpec((1, rb, D), lambda i, pos: (2 * pos[1] + pos[0], i, 0)),
                      pl.BlockSpec((1, rb, D), lambda i, pos: (pos[1], i, 0)),
                      pl.BlockSpec((3, rb, D), lambda i, pos: (0, i, 0)), wblk, wblk, wblk],
            out_specs=[wblk] * 4),
        out_shape=[shp, shp, shp, shp],
        compiler_params=_params(("arbitrary",)),
    )(pos_arr, g8, r1, r2, w, m, v)


SMALL_NAMES = ("norm1_g", "norm2_g", "norm_f_g", "b_gate", "gla_norm_g", "w_gate_up", "conv_w")
SMALL_PACK_ROWS = 32
WGU_W = NQK // N_DEV
CONV_W = CW // N_DEV


def _small_allreduce(parts, loss_part):
    def body(dg1, dg2, dgf, dbg, dgng, dwgu, dcw, lp, acc_ref, pack, gbuf, send_sems, recv_sems):
        x, y, c = _position()
        me = 4 * x + 2 * y + c
        flips = [(k >> 2, (k >> 1) & 1, k & 1) for k in range(1, N_DEV)]
        peers = [(x ^ fx, y ^ fy, c ^ fc) for fx, fy, fc in flips]

        def copy(k, block_id, to):
            return pltpu.make_async_remote_copy(
                src_ref=pack, dst_ref=gbuf.at[block_id], send_sem=send_sems.at[k], recv_sem=recv_sems.at[k],
                device_id=to, device_id_type=MESH)

        pack[...] = jnp.zeros_like(pack)
        pack[0:1, :] = dg1[...]
        pack[1:2, :] = dg2[...]
        pack[2:3, :] = dgf[...]
        pack[3:4, 0:NQK] = dbg[...]
        pack[3:4, NQK:NQK + DV] = dgng[...]
        pack[3:4, NQK + DV:NQK + 2 * DV] = lp[...]
        pack[8:8 + RANK, 0:NQK] = dwgu[0:RANK, :]
        pack[24:24 + CONV_K, 0:CW] = dcw[0:CONV_K, :]
        for k, peer in enumerate(peers):
            copy(k, me, peer).start()
        gbuf[me] = pack[...]
        for k, (px, py, pc) in enumerate(peers):
            cp = copy(k, 4 * px + 2 * py + pc, (px, py, pc))
            cp.wait_recv()
            cp.wait_send()
        acc = gbuf[0]
        for d in range(1, N_DEV):
            acc = acc + gbuf[d]
        acc_ref[...] = acc

    vm = pl.BlockSpec(memory_space=pltpu.VMEM)
    return pl.pallas_call(
        body,
        name="small_allreduce",
        in_specs=[vm] * 8,
        out_specs=vm,
        out_shape=jax.ShapeDtypeStruct((SMALL_PACK_ROWS, D), F32),
        scratch_shapes=[pltpu.VMEM((SMALL_PACK_ROWS, D), F32), pltpu.VMEM((N_DEV, SMALL_PACK_ROWS, D), F32),
                        pltpu.SemaphoreType.DMA((7,)), pltpu.SemaphoreType.DMA((7,))],
        compiler_params=_params(),
    )(*parts, loss_part)


def _small_adamw(sums, ws, ms, vs):
    n = len(SMALL_NAMES)

    def body(*refs):
        acc_ref = refs[0]
        w_refs, m_refs, v_refs = refs[1:1 + n], refs[1 + n:1 + 2 * n], refs[1 + 2 * n:1 + 3 * n]
        loss_ref = refs[1 + 3 * n]
        outs = refs[2 + 3 * n:]
        x, y, c = _position()
        me = 4 * x + 2 * y + c
        acc = acc_ref[...]
        loss_ref[...] = acc[3:4, NQK + DV:NQK + DV + 1]

        def my_columns(full, width):
            r = lax.broadcasted_iota(jnp.int32, (full.shape[1], width), 0)
            col = lax.broadcasted_iota(jnp.int32, (full.shape[1], width), 1)
            sel = (r == width * me + col).astype(F32)
            return _mm(full, sel, precision=HIGHEST)

        grads = [acc[0:1, :], acc[1:2, :], acc[2:3, :], acc[3:4, 0:NQK], acc[3:4, NQK:NQK + DV],
                 my_columns(acc[8:8 + RANK, 0:NQK], WGU_W), my_columns(acc[24:24 + CONV_K, 0:CW], CONV_W)]
        for i, g in enumerate(grads):
            d, mn, vn = _adamw_math(w_refs[i][...], g, m_refs[i][...], v_refs[i][...])
            outs[4 * i][...] = g
            outs[4 * i + 1][...] = d
            outs[4 * i + 2][...] = mn
            outs[4 * i + 3][...] = vn

    vm = pl.BlockSpec(memory_space=pltpu.VMEM)
    out_shape = [jax.ShapeDtypeStruct((1, 1), F32)]
    for w in ws:
        out_shape += [jax.ShapeDtypeStruct(w.shape, F32)] * 4
    return pl.pallas_call(
        body,
        name="small_adamw",
        in_specs=[vm] * (1 + 3 * n),
        out_specs=[vm] * (1 + 4 * n),
        out_shape=out_shape,
        compiler_params=_params(),
    )(sums, *ws, *ms, *vs)


def kernel(x, norm1_g, w_in, w_gate_up, b_gate, gla_norm_g, conv_w, w_out, norm2_g, w_ffn_gate, w_ffn_up, w_ffn_down, norm_f_g, loss_target, m_norm1_g, m_w_in, m_w_gate_up, m_b_gate, m_gla_norm_g, m_conv_w, m_w_out, m_norm2_g, m_w_ffn_gate, m_w_ffn_up, m_w_ffn_down, m_norm_f_g, v_norm1_g, v_w_in, v_w_gate_up, v_b_gate, v_gla_norm_g, v_conv_w, v_w_out, v_norm2_g, v_w_ffn_gate, v_w_ffn_up, v_w_ffn_down, v_norm_f_g):
    xi, yi, ci = _position()
    pos_arr = jnp.stack([ci, 2 * xi + yi]).astype(jnp.int32)
    nb, s, _ = x.shape
    t = nb * s

    tr = lambda a: a[0].T
    stage = _prep_slab(tr(w_in), tr(w_ffn_gate), tr(w_ffn_up), w_ffn_down[0], w_out[0])
    w_in_t, gwgu, gconv = _gather_w_in(stage, w_gate_up[0], conv_w[0])
    wgu_f = gwgu.transpose(1, 0, 2).reshape(RANK, NQK)
    conv_f = gconv.transpose(1, 0, 2).reshape(CONV_K, CW)
    wgu_p = jnp.concatenate([wgu_f, jnp.zeros((A_PAD - RANK, NQK), F32)], axis=0).astype(BF16)

    x2d = x.reshape(t, D)
    tgt2d = loss_target.reshape(t, D)
    tm = 256
    tm_in = min(512, t)
    tk = min(2048, t)
    proj, z, h, gwb = _in_proj_fwd(x2d, norm1_g, w_in_t, wgu_p, b_gate, tm_in, stage)
    proj3 = proj.reshape(nb, s, PW)
    z3 = z.reshape(nb, s, NQK)
    mix3, opre3, sprev, gwa = _mix_fwd(proj3, z3, gla_norm_g, conv_f, stage)
    mix2d = mix3.reshape(t, D)
    dx1, dx1b, dmix, h2, act, dgate, dup, dx2, dg2, dgf, loss_part = _ffn_fwd_bwd(
        mix2d, x2d, tgt2d, gwa, gwb, norm2_g, norm_f_g.reshape(1, D), tm)
    dw_d, dwb_d = _tn_matmul(act, dx2, DFF // 2, D, tk, "dw_ffn_down", True)
    dw_g, dwb_g = _tn_matmul(dgate, h2, DFF // 2, D, tk, "dw_ffn_gate", True)
    dw_u, dwb_u = _tn_matmul(dup, h2, DFF // 2, D, tk, "dw_ffn_up", True)
    by_owner = lambda a, rows: a.reshape(N_DEV, rows, D)
    gb_ffn = [by_owner(dwb_g, FF_W), by_owner(dwb_u, FF_W), by_owner(dwb_d, FF_W)]
    dw_o, dwb_o, *r1_ffn = _tn_matmul(mix2d, dx1b, D // 4, D, tk, "dw_out", True, gb_ffn)
    g8 = [by_owner(dw_g, FF_W), by_owner(dw_u, FF_W), by_owner(dw_d, FF_W), by_owner(dw_o, OUT_ROWS)]
    tags = ("w_ffn_gate", "w_ffn_up", "w_ffn_down", "w_out")
    r1 = list(r1_ffn) + list(_exchange_core([by_owner(dwb_o, OUT_ROWS)], "grad_exchange_core_out"))
    pb = [_add_core(g, r, pos_arr, "grad_add_core_" + tag) for g, r, tag in zip(g8, r1, tags)]
    mb = _mix_bwd(proj3, z3, sprev, opre3, dmix.reshape(nb, s, D), gla_norm_g, conv_f, wgu_p, pb)
    dproj3, dgng, dcw, dbg, dwgu = mb[:5]
    r2 = list(mb[5:])
    dproj2d = dproj3.reshape(t, PW)
    dw_in_t = _tn_matmul(dproj2d, h, PW // 5, D, tk, "dw_in", False)
    g_in, gb_in = _unpermute_dw_in(dw_in_t)
    (r1_in,) = _exchange_core([gb_in], "grad_exchange_core_in")
    pb_in = _add_core(g_in, r1_in, pos_arr, "grad_add_core_w_in")
    dx, dg1, r2_in = _in_proj_bwd(dproj2d, x2d, dx1, norm1_g, w_in_t, tm_in, pb_in)

    tags = ("w_in",) + tags
    g8 = [g_in] + g8
    r1 = [r1_in] + list(r1)
    r2 = [r2_in] + r2
    shard_w = (tr(w_in), tr(w_ffn_gate), tr(w_ffn_up), w_ffn_down[0], w_out[0])
    shard_m = (tr(m_w_in), tr(m_w_ffn_gate), tr(m_w_ffn_up), m_w_ffn_down[0], m_w_out[0])
    shard_v = (tr(v_w_in), tr(v_w_ffn_gate), tr(v_w_ffn_up), v_w_ffn_down[0], v_w_out[0])
    transposed = (True, True, True, False, False)
    results = {}
    for tag, g, ra, rb, w, m, v, tp in zip(tags, g8, r1, r2, shard_w, shard_m, shard_v, transposed):
        outs = _finish_weight(g, ra, rb, w, m, v, pos_arr, "finish_" + tag)
        results[tag] = [o.T[None] if tp else o[None] for o in outs]

    small_w = (norm1_g, norm2_g, norm_f_g.reshape(1, D), b_gate, gla_norm_g, w_gate_up[0], conv_w[0])
    small_m = (m_norm1_g, m_norm2_g, m_norm_f_g.reshape(1, D), m_b_gate, m_gla_norm_g, m_w_gate_up[0], m_conv_w[0])
    small_v = (v_norm1_g, v_norm2_g, v_norm_f_g.reshape(1, D), v_b_gate, v_gla_norm_g, v_w_gate_up[0], v_conv_w[0])
    so = _small_adamw(_small_allreduce((dg1, dg2, dgf, dbg, dgng, dwgu, dcw), loss_part), small_w, small_m, small_v)
    loss = so[0].reshape(())
    shapes = {"norm_f_g": (D,), "w_gate_up": (1, RANK, WGU_W), "conv_w": (1, CONV_K, CONV_W)}
    for i, name in enumerate(SMALL_NAMES):
        results[name] = [o.reshape(shapes[name]) if name in shapes else o for o in so[1 + 4 * i:5 + 4 * i]]

    names = ("norm1_g", "w_in", "w_gate_up", "b_gate", "gla_norm_g", "conv_w", "w_out", "norm2_g",
             "w_ffn_gate", "w_ffn_up", "w_ffn_down", "norm_f_g")
    outs = [loss, dx.reshape(nb, s, D)]
    for kind in range(4):
        for name in names:
            outs.append(results[name][kind])
    return tuple(outs)
```

```python
import jax
import jax.numpy as jnp
from jax import lax
from jax.experimental import pallas as pl
from jax.experimental.pallas import tpu as pltpu

F32 = jnp.float32
BF16 = jnp.bfloat16
HIGHEST = lax.Precision.HIGHEST
MESH = pl.DeviceIdType.MESH

N_DEV = 8
D = 1024
DFF = 2816
HEADS = 4
DK = 64
DV = 128
NQK = HEADS * DK
NV = HEADS * DV
RANK = 16
CHUNK = 64
CW = 512
CONV_K = 3
IN_COLS = 3088
EPS = 1e-6
INV_GATE_NORM = 1.0 / 16.0
Q_SCALE = DK ** -0.5

PW = 3200
OQ, OK_, OV, OG, OCB, OCC, OCH, OA = 0, 256, 512, 1024, 1536, 2048, 2560, 3072
A_PAD = 128

ADAM_LR = 0.001
ADAM_B1 = 0.9
ADAM_B2 = 0.999
ADAM_EPS = 1e-08
ADAM_WD = 0.01
ADAM_STEP = 10

IN_W = IN_COLS // N_DEV
IN_ROWS = 400
FF_W = DFF // N_DEV
OUT_ROWS = D // N_DEV
SLAB_IN = 0
SLAB_G = SLAB_IN + IN_ROWS
SLAB_U = SLAB_G + FF_W
SLAB_D = SLAB_U + FF_W
SLAB_O = SLAB_D + FF_W
SLAB_ROWS = SLAB_O + OUT_ROWS

VMEM_LIMIT = 56 * 1024 * 1024


def _params(sem=None, vmem=VMEM_LIMIT):
    return pltpu.CompilerParams(dimension_semantics=sem, vmem_limit_bytes=vmem)


def _nt(a, b):
    return lax.dot_general(a, b, (((1,), (1,)), ((), ())), preferred_element_type=F32)


def _tn(a, b, precision=None):
    return lax.dot_general(a, b, (((0,), (0,)), ((), ())), preferred_element_type=F32, precision=precision)


def _mm(a, b, precision=None):
    return jnp.dot(a, b, preferred_element_type=F32, precision=precision)


def _in_segments():
    segs = []
    for j in range(N_DEV):
        lo, hi = IN_W * j, IN_W * (j + 1)
        cuts = sorted({lo, hi} | {c for c in (OCB, OCB + RANK) if lo < c < hi})
        for a, b in zip(cuts[:-1], cuts[1:]):
            if a < OCB:
                d = a
            elif a < OCB + RANK:
                d = OA + (a - OCB)
            else:
                d = a - RANK
            segs.append((j, a - lo, b - lo, d))
    return segs


def _in_proj_fwd(x2d, g1, w_in_t, wgu_p, b_gate, tm, stage):
    t = x2d.shape[0]
    nt = t // tm
    g_rows = SLAB_ROWS - SLAB_D

    def body(x_ref, g_ref, w_ref, wgu_ref, bg_ref, stage_hbm, proj_ref, z_ref, h_ref, gwb_ref,
             send_sems, recv_sems, local_sem):
        gargs = (stage_hbm, SLAB_D, g_rows, gwb_ref, send_sems, recv_sems, local_sem)

        @pl.when(pl.program_id(0) == 0)
        def _():
            _gather_start(*gargs)

        x = x_ref[...]
        r = lax.rsqrt(jnp.mean(x * x, axis=-1, keepdims=True) + EPS)
        h = ((x * r) * g_ref[...]).astype(BF16)
        h_ref[...] = h
        proj = _nt(h, w_ref[...])
        proj_ref[...] = proj
        pa = proj[:, OA:OA + A_PAD].astype(BF16)
        z_ref[...] = _mm(pa, wgu_ref[...]) + bg_ref[...]

        @pl.when(pl.program_id(0) == nt - 1)
        def _():
            _gather_finish(*gargs)

    return pl.pallas_call(
        body,
        name="in_proj_fwd",
        grid=(t // tm,),
        in_specs=[
            pl.BlockSpec((tm, D), lambda i: (i, 0)),
            pl.BlockSpec((1, D), lambda i: (0, 0)),
            pl.BlockSpec((PW, D), lambda i: (0, 0)),
            pl.BlockSpec((A_PAD, NQK), lambda i: (0, 0)),
            pl.BlockSpec((1, NQK), lambda i: (0, 0)),
            pl.BlockSpec(memory_space=pl.ANY),
        ],
        out_specs=[
            pl.BlockSpec((tm, PW), lambda i: (i, 0)),
            pl.BlockSpec((tm, NQK), lambda i: (i, 0)),
            pl.BlockSpec((tm, D), lambda i: (i, 0)),
            pl.BlockSpec(memory_space=pl.ANY),
        ],
        out_shape=[
            jax.ShapeDtypeStruct((t, PW), F32),
            jax.ShapeDtypeStruct((t, NQK), F32),
            jax.ShapeDtypeStruct((t, D), BF16),
            jax.ShapeDtypeStruct((N_DEV, g_rows, D), BF16),
        ],
        scratch_shapes=_gather_sems(),
        compiler_params=_params(("arbitrary",)),
    )(x2d, g1, w_in_t, wgu_p, b_gate, stage)


def _head_masks():
    lane = lax.broadcasted_iota(jnp.int32, (1, NQK), 1)
    return [(lane >= DK * h) & (lane < DK * (h + 1)) for h in range(HEADS)]


def _split_bf16(x, n):
    parts = []
    for _ in range(n):
        p = x.astype(BF16)
        parts.append(p)
        x = x - p.astype(F32)
    return parts


def _chunk_fwd_parts(q, k, z, tril16):
    la = (jnp.minimum(z, 0.0) - jnp.log1p(jnp.exp(-jnp.abs(z)))) * INV_GATE_NORM
    la_parts = _split_bf16(la, 3)
    bc = _mm(tril16, la_parts[0]) + _mm(tril16, la_parts[1]) + _mm(tril16, la_parts[2])
    bl = bc[CHUNK - 1:CHUNK, :]
    eb = jnp.exp(bc)
    enb = jnp.exp(-bc)
    ekl = jnp.exp(bl - bc)
    qi = (q * Q_SCALE) * eb
    ki = k * enb
    ks = k * ekl
    ones16 = jnp.ones((CHUNK, DV), BF16)
    decb = jnp.exp(_tn(la_parts[0], ones16) + _tn(la_parts[1], ones16) + _tn(la_parts[2], ones16))
    return la, eb, enb, ekl, qi, ki, ks, decb


def _stack_heads(a, masks):
    return jnp.concatenate([jnp.where(m, a, 0.0) for m in masks], axis=0)


def _merge_heads(blocks, masks):
    out = blocks[HEADS - 1]
    for h in range(HEADS - 2, -1, -1):
        out = jnp.where(masks[h], blocks[h], out)
    return out


def _causal_stack_mask():
    row = lax.broadcasted_iota(jnp.int32, (HEADS * CHUNK, CHUNK), 0)
    col = lax.broadcasted_iota(jnp.int32, (HEADS * CHUNK, CHUNK), 1)
    return (row & (CHUNK - 1)) >= col


def _conv_taps(u, uprev):
    row = lax.broadcasted_iota(jnp.int32, u.shape, 0)
    u1 = jnp.where(row < 1, pltpu.roll(uprev, 1, 0), pltpu.roll(u, 1, 0))
    u2 = jnp.where(row < 2, pltpu.roll(uprev, 2, 0), pltpu.roll(u, 2, 0))
    return u1, u2


def _mix_fwd(proj3, z3, gng, conv_w, stage):
    nb, s, _ = proj3.shape
    nc = s // CHUNK
    g_rows = SLAB_D - SLAB_G

    def body(p_ref, z_ref, gng_ref, cw_ref, stage_hbm, mix_ref, o_ref, sprev_ref, gwa_ref, s_ref, uprev_ref,
             send_sems, recv_sems, local_sem):
        n = pl.program_id(0)
        gargs = (stage_hbm, SLAB_G, g_rows, gwa_ref, send_sems, recv_sems, local_sem)

        @pl.when(n == 0)
        def _():
            _gather_start(*gargs)
            s_ref[...] = jnp.zeros_like(s_ref)
            uprev_ref[...] = jnp.zeros_like(uprev_ref)

        r_i = lax.broadcasted_iota(jnp.int32, (CHUNK, CHUNK), 0)
        c_i = lax.broadcasted_iota(jnp.int32, (CHUNK, CHUNK), 1)
        tril16 = (r_i >= c_i).astype(BF16)
        masks = _head_masks()
        cmask = _causal_stack_mask()
        gg = gng_ref[...]
        for b in range(nb):
            q = p_ref[b, :, OQ:OQ + NQK]
            k = p_ref[b, :, OK_:OK_ + NQK]
            _, _, _, _, qi, ki, ks, decb = _chunk_fwd_parts(q, k, z_ref[b], tril16)
            qs = _stack_heads(qi, masks).astype(BF16)
            sc = jnp.where(cmask, _nt(qs, ki.astype(BF16)), 0.0).astype(BF16)
            st = s_ref[b]
            sprev_ref[b, 0] = st
            o_inter = _mm(qs, st.astype(BF16))
            v16 = p_ref[b, :, OV:OV + NV].astype(BF16)
            kv = _tn(ks.astype(BF16), v16)
            for h in range(HEADS):
                rows = slice(CHUNK * h, CHUNK * (h + 1))
                cols = slice(DV * h, DV * (h + 1))
                o = _mm(sc[rows], v16[:, cols]) + o_inter[rows]
                o_ref[b, :, cols] = o
                r = lax.rsqrt(jnp.mean(o * o, axis=-1, keepdims=True) + EPS)
                on = (o * r) * gg
                g = p_ref[b, :, OG + DV * h:OG + DV * (h + 1)]
                mix_ref[b, :, cols] = (on * (g * jax.nn.sigmoid(g))).astype(BF16)
                s_ref[b, rows, :] = decb[rows] * st[rows] + kv[rows, cols]
            u = p_ref[b, :, OCC:OCC + CW] * p_ref[b, :, OCH:OCH + CW]
            u1, u2 = _conv_taps(u, uprev_ref[b])
            yc = cw_ref[0:1, :] * u2 + cw_ref[1:2, :] * u1 + cw_ref[2:3, :] * u
            mix_ref[b, :, NV:NV + CW] = (p_ref[b, :, OCB:OCB + CW] * yc).astype(BF16)
            uprev_ref[b] = u

        @pl.when(n == nc - 1)
        def _():
            _gather_finish(*gargs)

    return pl.pallas_call(
        body,
        name="mix_fwd",
        grid=(nc,),
        in_specs=[
            pl.BlockSpec((nb, CHUNK, PW), lambda n: (0, n, 0)),
            pl.BlockSpec((nb, CHUNK, NQK), lambda n: (0, n, 0)),
            pl.BlockSpec((1, DV), lambda n: (0, 0)),
            pl.BlockSpec((CONV_K, CW), lambda n: (0, 0)),
            pl.BlockSpec(memory_space=pl.ANY),
        ],
        out_specs=[
            pl.BlockSpec((nb, CHUNK, D), lambda n: (0, n, 0)),
            pl.BlockSpec((nb, CHUNK, NV), lambda n: (0, n, 0)),
            pl.BlockSpec((nb, 1, NQK, DV), lambda n: (0, n, 0, 0)),
            pl.BlockSpec(memory_space=pl.ANY),
        ],
        out_shape=[
            jax.ShapeDtypeStruct((nb, s, D), BF16),
            jax.ShapeDtypeStruct((nb, s, NV), F32),
            jax.ShapeDtypeStruct((nb, nc, NQK, DV), F32),
            jax.ShapeDtypeStruct((N_DEV, g_rows, D), BF16),
        ],
        scratch_shapes=[pltpu.VMEM((nb, NQK, DV), F32), pltpu.VMEM((nb, CHUNK, CW), F32)] + _gather_sems(),
        compiler_params=_params(("arbitrary",)),
    )(proj3, z3, gng, conv_w, stage)


def _ffn_fwd_bwd(mix2d, x2d, tgt2d, gwa, gwb, g2, gf, tm):
    t = x2d.shape[0]

    def body(mix_ref, x_ref, tgt_ref, g2_ref, gf_ref, gwa_hbm, gwb_hbm,
             dx1_ref, dx1b_ref, dmix_ref, adu_ref, hb_ref, dg2_ref, dgf_ref, loss_ref,
             wo, wg, wu, wd, wsem):
        i = pl.program_id(0)

        def weight_copies(n, dst, src, off, rows):
            return [pltpu.make_async_copy(src.at[j, pl.ds(off, rows), :], dst.at[pl.ds(rows * j, rows), :],
                                          wsem.at[N_DEV * n + j]) for j in range(N_DEV)]

        loads = (weight_copies(0, wo, gwb_hbm, FF_W, OUT_ROWS), weight_copies(1, wg, gwa_hbm, 0, FF_W),
                 weight_copies(2, wu, gwa_hbm, FF_W, FF_W), weight_copies(3, wd, gwb_hbm, 0, FF_W))

        def arrive(n):
            @pl.when(i == 0)
            def _():
                for cp in loads[n]:
                    cp.wait()

        @pl.when(i == 0)
        def _():
            for group in loads:
                for cp in group:
                    cp.start()
            dg2_ref[...] = jnp.zeros_like(dg2_ref)
            dgf_ref[...] = jnp.zeros_like(dgf_ref)
            loss_ref[...] = jnp.zeros_like(loss_ref)

        g2v = g2_ref[...]
        gfv = gf_ref[...]
        arrive(0)
        x1 = x_ref[...] + _mm(mix_ref[...], wo[...])
        r2 = lax.rsqrt(jnp.mean(x1 * x1, axis=-1, keepdims=True) + EPS)
        n2 = x1 * r2
        h2 = (n2 * g2v).astype(BF16)
        hb_ref[1] = h2
        arrive(1)
        gate = _nt(h2, wg[...])
        arrive(2)
        up = _nt(h2, wu[...])
        sg = jax.nn.sigmoid(gate)
        sil = gate * sg
        act = (sil * up).astype(BF16)
        adu_ref[0] = act
        arrive(3)
        x2 = x1 + _mm(act, wd[...])
        rf = lax.rsqrt(jnp.mean(x2 * x2, axis=-1, keepdims=True) + EPS)
        nf = x2 * rf
        err = nf * gfv - tgt_ref[...]
        loss_ref[...] += 0.5 * jnp.sum(jnp.mean(err * err, axis=-1, keepdims=True))
        dy = err * (1.0 / D)
        dgf_ref[...] += jnp.sum(dy * nf, axis=0, keepdims=True)
        dnf = dy * gfv
        dx2 = rf * (dnf - nf * jnp.mean(dnf * nf, axis=-1, keepdims=True))
        dx2b = dx2.astype(BF16)
        hb_ref[0] = dx2b
        dact = _nt(dx2b, wd[...])
        dup = (dact * sil).astype(BF16)
        dgate = ((dact * up) * (sg * (1.0 + gate * (1.0 - sg)))).astype(BF16)
        adu_ref[2] = dup
        adu_ref[1] = dgate
        dh2 = _mm(dgate, wg[...]) + _mm(dup, wu[...])
        dg2_ref[...] += jnp.sum(dh2 * n2, axis=0, keepdims=True)
        dn2 = dh2 * g2v
        dx1 = dx2 + r2 * (dn2 - n2 * jnp.mean(dn2 * n2, axis=-1, keepdims=True))
        dx1_ref[...] = dx1
        dx1b = dx1.astype(BF16)
        dx1b_ref[...] = dx1b
        dmix_ref[...] = _nt(dx1b, wo[...])

    tile = lambda w: pl.BlockSpec((tm, w), lambda i: (i, 0))
    vec = pl.BlockSpec((1, D), lambda i: (0, 0))
    hbm = pl.BlockSpec(memory_space=pl.ANY)
    return pl.pallas_call(
        body,
        name="ffn_fwd_bwd",
        grid=(t // tm,),
        in_specs=[tile(D), tile(D), tile(D), vec, vec, hbm, hbm],
        out_specs=[tile(D), tile(D), tile(D), pl.BlockSpec((3, tm, DFF), lambda i: (0, i, 0)),
                   pl.BlockSpec((2, tm, D), lambda i: (0, i, 0)), vec, vec,
                   pl.BlockSpec((1, 128), lambda i: (0, 0))],
        out_shape=[
            jax.ShapeDtypeStruct((t, D), F32),
            jax.ShapeDtypeStruct((t, D), BF16),
            jax.ShapeDtypeStruct((t, D), F32),
            jax.ShapeDtypeStruct((3, t, DFF), BF16),
            jax.ShapeDtypeStruct((2, t, D), BF16),
            jax.ShapeDtypeStruct((1, D), F32),
            jax.ShapeDtypeStruct((1, D), F32),
            jax.ShapeDtypeStruct((1, 128), F32),
        ],
        scratch_shapes=[pltpu.VMEM((D, D), BF16), pltpu.VMEM((DFF, D), BF16), pltpu.VMEM((DFF, D), BF16),
                        pltpu.VMEM((DFF, D), BF16), pltpu.SemaphoreType.DMA((4 * N_DEV,))],
        compiler_params=_params(("arbitrary",)),
    )(mix2d, x2d, tgt2d, g2, gf, gwa, gwb)


def _tn_matmul(a, b, bm, bn, tk, name, with_bf16, exchange=None):
    t, m = a.shape
    n = b.shape[1]
    nk = t // tk
    ne = 0 if exchange is None else exchange.shape[0]
    nout = 2 if with_bf16 else 1
    grid = (m // bm, n // bn, nk)

    def body(a_ref, b_ref, *rest):
        if ne:
            ex_in, leads = [rest[0]] * ne, list(range(ne))
            outs, ex_out, sems = rest[1:1 + nout], rest[1 + nout:1 + nout + ne], rest[1 + nout + ne:]
        else:
            outs = rest
        o_ref = outs[0]
        i, j, k = pl.program_id(0), pl.program_id(1), pl.program_id(2)
        if ne:
            @pl.when((i == 0) & (j == 0) & (k == 0))
            def _():
                for cp in _stage1_copies(ex_in, leads, ex_out, *sems):
                    cp.start()

        @pl.when(k == 0)
        def _():
            o_ref[...] = jnp.zeros_like(o_ref)

        o_ref[...] += _tn(a_ref[...].astype(BF16), b_ref[...].astype(BF16))
        if with_bf16:
            @pl.when(k == nk - 1)
            def _():
                outs[1][...] = o_ref[...].astype(BF16)
        if ne:
            @pl.when((i == grid[0] - 1) & (j == grid[1] - 1) & (k == nk - 1))
            def _():
                copies = _stage1_copies(ex_in, leads, ex_out, *sems)
                for cp in copies:
                    cp.wait_recv()
                for cp in copies:
                    cp.wait_send()

    out_blk = pl.BlockSpec((bm, bn), lambda i, j, k: (i, j))
    hbm = pl.BlockSpec(memory_space=pl.ANY)
    out_shape = [jax.ShapeDtypeStruct((m, n), F32)] + ([jax.ShapeDtypeStruct((m, n), BF16)] if with_bf16 else [])
    if ne:
        out_shape += [jax.ShapeDtypeStruct((4,) + exchange.shape[2:], BF16)] * ne
    res = pl.pallas_call(
        body,
        name=name,
        grid=grid,
        in_specs=[pl.BlockSpec((tk, bm), lambda i, j, k: (k, i)), pl.BlockSpec((tk, bn), lambda i, j, k: (k, j))]
        + [hbm] * (ne > 0),
        out_specs=[out_blk] * nout + [hbm] * ne,
        out_shape=out_shape,
        scratch_shapes=[pltpu.SemaphoreType.DMA((4 * ne,)), pltpu.SemaphoreType.DMA((4 * ne,))] if ne else [],
        compiler_params=_params(("arbitrary", "arbitrary", "arbitrary") if ne else ("parallel", "parallel", "arbitrary")),
    )(*((a, b) if exchange is None else (a, b, exchange)))
    return res[0] if len(res) == 1 else res


def _dw_ffn(adu, hb, tk):
    _, t, _ = adu.shape
    bm = DFF // 2
    nk = t // tk

    def body(a_ref, b_ref, o_ref, ob_ref):
        k = pl.program_id(2)

        @pl.when(k == 0)
        def _():
            o_ref[...] = jnp.zeros_like(o_ref)

        o_ref[...] += _tn(a_ref[...], b_ref[...])

        @pl.when(k == nk - 1)
        def _():
            ob_ref[...] = o_ref[...].astype(BF16)

    out_blk = pl.BlockSpec((None, bm, D), lambda p, i, k: (p, i, 0))
    return pl.pallas_call(
        body,
        name="dw_ffn",
        grid=(3, DFF // bm, nk),
        in_specs=[pl.BlockSpec((None, tk, bm), lambda p, i, k: (p, k, i)),
                  pl.BlockSpec((None, tk, D), lambda p, i, k: (jnp.minimum(p, 1), k, 0))],
        out_specs=[out_blk, out_blk],
        out_shape=[jax.ShapeDtypeStruct((3, DFF, D), F32), jax.ShapeDtypeStruct((3, DFF, D), BF16)],
        compiler_params=_params(("arbitrary", "arbitrary", "arbitrary")),
    )(adu, hb)


def _at_owner(ref, lead, idx):
    return ref.at[idx] if lead is None else ref.at[lead, idx]


def _mix_bwd(proj3, z3, sprev, opre3, dmix3, gng, conv_w, wgu_p, pbs):
    nb, s, _ = proj3.shape
    nc = s // CHUNK
    na = len(pbs)

    def body(*refs):
        (p_ref, pprev_ref, z_ref, sp_ref, o_ref, dm_ref, gng_ref, cw_ref, wgu_ref) = refs[:9]
        pb_refs = refs[9:9 + na]
        (dproj_ref, dgng_ref, dcw_ref, dbg_ref, dwgu_ref) = refs[9 + na:14 + na]
        r2_refs = refs[14 + na:14 + 2 * na]
        ds_ref, dycn_ref, send_sems, recv_sems = refs[14 + 2 * na:]
        step = pl.program_id(0)
        n = nc - 1 - step

        @pl.when(step == 0)
        def _():
            for cp in _stage2_copies(pb_refs, r2_refs, send_sems, recv_sems):
                cp.start()
            ds_ref[...] = jnp.zeros_like(ds_ref)
            dycn_ref[...] = jnp.zeros_like(dycn_ref)
            dgng_ref[...] = jnp.zeros_like(dgng_ref)
            dcw_ref[...] = jnp.zeros_like(dcw_ref)
            dbg_ref[...] = jnp.zeros_like(dbg_ref)
            dwgu_ref[...] = jnp.zeros_like(dwgu_ref)

        r_i = lax.broadcasted_iota(jnp.int32, (CHUNK, CHUNK), 0)
        c_i = lax.broadcasted_iota(jnp.int32, (CHUNK, CHUNK), 1)
        tril16 = (r_i >= c_i).astype(BF16)
        triu16 = (r_i <= c_i).astype(BF16)
        causal = r_i >= c_i
        masks = _head_masks()
        cmask = _causal_stack_mask()
        gg = gng_ref[...]
        last_row = lax.broadcasted_iota(jnp.int32, (CHUNK, NQK), 0) == CHUNK - 1
        ones_r = jnp.ones((16, DV), BF16)
        has_prev = (n > 0).astype(F32)
        for b in range(nb):
            q = p_ref[b, :, OQ:OQ + NQK]
            k = p_ref[b, :, OK_:OK_ + NQK]
            z = z_ref[b]
            _, eb, enb, ekl, qi, ki, ks, decb = _chunk_fwd_parts(q, k, z, tril16)
            qi16 = qi.astype(BF16)
            ki16 = ki.astype(BF16)
            qs = _stack_heads(qi, masks).astype(BF16)
            sc = jnp.where(cmask, _nt(qs, ki16), 0.0).astype(BF16)
            st = sp_ref[b, 0]
            st16 = st.astype(BF16)
            dsn = ds_ref[b]
            dsn16 = dsn.astype(BF16)
            v16 = p_ref[b, :, OV:OV + NV].astype(BF16)
            do16 = []
            dgng = jnp.zeros((1, DV), F32)
            for h in range(HEADS):
                cols = slice(DV * h, DV * (h + 1))
                o = o_ref[b, :, cols]
                r = lax.rsqrt(jnp.mean(o * o, axis=-1, keepdims=True) + EPS)
                nh = o * r
                g = p_ref[b, :, OG + DV * h:OG + DV * (h + 1)]
                sg = jax.nn.sigmoid(g)
                dog = dm_ref[b, :, cols]
                dproj_ref[b, :, OG + DV * h:OG + DV * (h + 1)] = (
                    (dog * (nh * gg)) * (sg * (1.0 + g * (1.0 - sg)))).astype(BF16)
                don = dog * (g * sg)
                dgng = dgng + jnp.sum(don * nh, axis=0, keepdims=True)
                dn = don * gg
                do = r * (dn - nh * jnp.mean(dn * nh, axis=-1, keepdims=True))
                do16.append(do.astype(BF16))
            dgng_ref[...] += dgng
            do_rows = jnp.concatenate(do16, axis=0)
            v_rows = jnp.concatenate([v16[:, DV * h:DV * (h + 1)] for h in range(HEADS)], axis=0)
            dp16 = [jnp.where(causal, _nt(do16[h], v16[:, DV * h:DV * (h + 1)]), 0.0).astype(BF16)
                    for h in range(HEADS)]
            ks_dsn = _mm(_stack_heads(ks, masks).astype(BF16), dsn16)
            do_st = _nt(do_rows, st16)
            v_dsn = _nt(v_rows, dsn16)
            dp_ki = _mm(jnp.concatenate(dp16, axis=0), ki16)
            q_do = _tn(qi16, jnp.concatenate(do16, axis=1))
            dki_h = []
            for h in range(HEADS):
                rows = slice(CHUNK * h, CHUNK * (h + 1))
                cols = slice(DV * h, DV * (h + 1))
                dv = _tn(sc[rows], do16[h]) + ks_dsn[rows]
                dproj_ref[b, :, OV + DV * h:OV + DV * (h + 1)] = dv.astype(BF16)
                dki_h.append(_tn(dp16[h], qi16))
                ds_ref[b, rows, :] = decb[rows] * dsn[rows] + q_do[rows, cols]
            blocks = lambda a: [a[CHUNK * h:CHUNK * (h + 1)] for h in range(HEADS)]
            dqi = _merge_heads(blocks(dp_ki + do_st), masks)
            dki = _merge_heads(dki_h, masks)
            dks = _merge_heads(blocks(v_dsn), masks)
            dproj_ref[b, :, OQ:OQ + NQK] = (dqi * (Q_SCALE * eb)).astype(BF16)
            dproj_ref[b, :, OK_:OK_ + NQK] = (dki * enb + dks * ekl).astype(BF16)
            dks_ks = dks * ks
            db = dqi * qi - dki * ki - dks_ks
            sd = _split_bf16(dsn * st * decb, 2)
            dbl = jnp.sum(dks_ks, axis=0, keepdims=True) + (_nt(ones_r, sd[0]) + _nt(ones_r, sd[1]))[0:1, :]
            db = db + jnp.where(last_row, dbl, 0.0)
            db_parts = _split_bf16(db, 3)
            dla = _mm(triu16, db_parts[0]) + _mm(triu16, db_parts[1]) + _mm(triu16, db_parts[2])
            dz = (dla * INV_GATE_NORM) * (1.0 / (1.0 + jnp.exp(z)))
            dbg_ref[...] += jnp.sum(dz, axis=0, keepdims=True)
            dz16 = dz.astype(BF16)
            pa16 = p_ref[b, :, OA:OA + A_PAD].astype(BF16)
            dwgu_ref[...] += _tn(pa16, dz16)
            dproj_ref[b, :, OA:OA + A_PAD] = _nt(dz16, wgu_ref[...]).astype(BF16)
            cb = p_ref[b, :, OCB:OCB + CW]
            cc = p_ref[b, :, OCC:OCC + CW]
            ch = p_ref[b, :, OCH:OCH + CW]
            u = cc * ch
            uprev = (pprev_ref[b, :, 0:CW] * pprev_ref[b, :, CW:2 * CW]) * has_prev
            u1, u2 = _conv_taps(u, uprev)
            w0 = cw_ref[0:1, :]
            w1 = cw_ref[1:2, :]
            w2 = cw_ref[2:3, :]
            yc = w0 * u2 + w1 * u1 + w2 * u
            doc = dm_ref[b, :, NV:NV + CW]
            dproj_ref[b, :, OCB:OCB + CW] = (doc * yc).astype(BF16)
            dyc = doc * cb
            dycn = dycn_ref[b]
            row = lax.broadcasted_iota(jnp.int32, dyc.shape, 0)
            d1 = jnp.where(row >= CHUNK - 1, pltpu.roll(dycn, CHUNK - 1, 0), pltpu.roll(dyc, CHUNK - 1, 0))
            d2 = jnp.where(row >= CHUNK - 2, pltpu.roll(dycn, CHUNK - 2, 0), pltpu.roll(dyc, CHUNK - 2, 0))
            du = w2 * dyc + w1 * d1 + w0 * d2
            dproj_ref[b, :, OCC:OCC + CW] = (du * ch).astype(BF16)
            dproj_ref[b, :, OCH:OCH + CW] = (du * cc).astype(BF16)
            dcw_ref[0:1, :] += jnp.sum(dyc * u2, axis=0, keepdims=True)
            dcw_ref[1:2, :] += jnp.sum(dyc * u1, axis=0, keepdims=True)
            dcw_ref[2:3, :] += jnp.sum(dyc * u, axis=0, keepdims=True)
            dycn_ref[b] = dyc

        @pl.when(step == nc - 1)
        def _():
            copies = _stage2_copies(pb_refs, r2_refs, send_sems, recv_sems)
            for cp in copies:
                cp.wait_recv()
            for cp in copies:
                cp.wait_send()

    rev = lambda w: pl.BlockSpec((nb, CHUNK, w), lambda i: (0, nc - 1 - i, 0))
    const = lambda r, c: pl.BlockSpec((r, c), lambda i: (0, 0))
    hbm = pl.BlockSpec(memory_space=pl.ANY)
    return pl.pallas_call(
        body,
        name="mix_bwd",
        grid=(nc,),
        in_specs=[
            rev(PW),
            pl.BlockSpec((nb, CHUNK, 2 * CW), lambda i: (0, jnp.maximum(nc - 2 - i, 0), OCC // (2 * CW))),
            rev(NQK),
            pl.BlockSpec((nb, 1, NQK, DV), lambda i: (0, nc - 1 - i, 0, 0)),
            rev(NV),
            rev(D),
            const(1, DV),
            const(CONV_K, CW),
            const(A_PAD, NQK),
        ] + [hbm] * na,
        out_specs=[rev(PW), const(1, DV), const(8, CW), const(1, NQK), const(A_PAD, NQK)] + [hbm] * na,
        out_shape=[
            jax.ShapeDtypeStruct((nb, s, PW), BF16),
            jax.ShapeDtypeStruct((1, DV), F32),
            jax.ShapeDtypeStruct((8, CW), F32),
            jax.ShapeDtypeStruct((1, NQK), F32),
            jax.ShapeDtypeStruct((A_PAD, NQK), F32),
        ] + [jax.ShapeDtypeStruct((3,) + p.shape[1:], BF16) for p in pbs],
        scratch_shapes=[pltpu.VMEM((nb, NQK, DV), F32), pltpu.VMEM((nb, CHUNK, CW), F32),
                        pltpu.SemaphoreType.DMA((3 * na,)), pltpu.SemaphoreType.DMA((3 * na,))],
        compiler_params=_params(("arbitrary",)),
    )(proj3, proj3, z3, sprev, opre3, dmix3, gng, conv_w, wgu_p, *pbs)


def _in_proj_bwd(dproj2d, x2d, dx1, g1, w_in_t, tm, pb):
    t = x2d.shape[0]
    nt = t // tm

    def body(dp_ref, x_ref, dx1_ref, g_ref, w_ref, pb_ref, dx_ref, dg1_ref, r2_ref, send_sems, recv_sems):
        @pl.when(pl.program_id(0) == 0)
        def _():
            for cp in _stage2_copies([pb_ref], [r2_ref], send_sems, recv_sems):
                cp.start()
            dg1_ref[...] = jnp.zeros_like(dg1_ref)

        x = x_ref[...]
        r = lax.rsqrt(jnp.mean(x * x, axis=-1, keepdims=True) + EPS)
        n1 = x * r
        dh = _mm(dp_ref[...], w_ref[...])
        dg1_ref[...] += jnp.sum(dh * n1, axis=0, keepdims=True)
        dn = dh * g_ref[...]
        dx_ref[...] = dx1_ref[...] + r * (dn - n1 * jnp.mean(dn * n1, axis=-1, keepdims=True))

        @pl.when(pl.program_id(0) == nt - 1)
        def _():
            copies = _stage2_copies([pb_ref], [r2_ref], send_sems, recv_sems)
            for cp in copies:
                cp.wait_recv()
            for cp in copies:
                cp.wait_send()

    tile = lambda w: pl.BlockSpec((tm, w), lambda i: (i, 0))
    vec = pl.BlockSpec((1, D), lambda i: (0, 0))
    hbm = pl.BlockSpec(memory_space=pl.ANY)
    return pl.pallas_call(
        body,
        name="in_proj_bwd",
        grid=(nt,),
        in_specs=[tile(PW), tile(D), tile(D), vec, pl.BlockSpec((PW, D), lambda i: (0, 0)), hbm],
        out_specs=[tile(D), vec, hbm],
        out_shape=[jax.ShapeDtypeStruct((t, D), F32), jax.ShapeDtypeStruct((1, D), F32),
                   jax.ShapeDtypeStruct((3,) + pb.shape[1:], BF16)],
        scratch_shapes=[pltpu.SemaphoreType.DMA((3,)), pltpu.SemaphoreType.DMA((3,))],
        compiler_params=_params(("arbitrary",)),
    )(dproj2d, x2d, dx1, g1, w_in_t, pb)


def _adamw_math(w, g, m, v):
    m = ADAM_B1 * m + (1.0 - ADAM_B1) * g
    v = ADAM_B2 * v + (1.0 - ADAM_B2) * (g * g)
    m_hat = m / (1.0 - ADAM_B1 ** ADAM_STEP)
    v_hat = v / (1.0 - ADAM_B2 ** ADAM_STEP)
    delta = -ADAM_LR * (m_hat / (jnp.sqrt(v_hat) + ADAM_EPS) + ADAM_WD * w)
    return delta, m, v


def _position():
    return lax.axis_index("x"), lax.axis_index("y"), lax.axis_index("c")


def _prep_slab(w_it, w_gt, w_ut, w_d, w_o):
    def body(wi_ref, wg_ref, wu_ref, wd_ref, wo_ref, stage):
        stage[SLAB_IN:SLAB_IN + IN_W, :] = wi_ref[...].astype(BF16)
        stage[SLAB_IN + IN_W:SLAB_G, :] = jnp.zeros((IN_ROWS - IN_W, D), BF16)
        stage[SLAB_G:SLAB_U, :] = wg_ref[...].astype(BF16)
        stage[SLAB_U:SLAB_D, :] = wu_ref[...].astype(BF16)
        stage[SLAB_D:SLAB_O, :] = wd_ref[...].astype(BF16)
        stage[SLAB_O:SLAB_ROWS, :] = wo_ref[...].astype(BF16)

    vm = pl.BlockSpec(memory_space=pltpu.VMEM)
    return pl.pallas_call(
        body,
        name="prep_slab",
        in_specs=[vm] * 5,
        out_specs=vm,
        out_shape=jax.ShapeDtypeStruct((SLAB_ROWS, D), BF16),
        compiler_params=_params(),
    )(w_it, w_gt, w_ut, w_d, w_o)


GATHER_SEMS = 7


def _gather_copies(stage, lo, rows, gx, send_sems, recv_sems, local_sem):
    x, y, c = _position()
    me = (x, y, c)
    sibling = (x, y, 1 - c)
    chips = [(1 - x, y), (x, 1 - y), (1 - x, 1 - y)]
    src = stage.at[pl.ds(lo, rows), :]

    def blk(px, py, pc):
        return gx.at[4 * px + 2 * py + pc]

    def copy(k, block, to, from_stage=False):
        return pltpu.make_async_remote_copy(
            src_ref=src if from_stage else blk(*block), dst_ref=blk(*block),
            send_sem=send_sems.at[k], recv_sem=recv_sems.at[k], device_id=to, device_id_type=MESH)

    mine = pltpu.make_async_copy(src, blk(*me), local_sem)
    first = [copy(0, me, sibling, True)] + [copy(1 + j, me, (*chip, c), True) for j, chip in enumerate(chips)]
    passed = [copy(4 + j, (*chip, c), sibling) for j, chip in enumerate(chips)]
    arrivals = ([copy(0, sibling, me)] + [copy(1 + j, (*chip, c), me) for j, chip in enumerate(chips)]
                + [copy(4 + j, (*chip, 1 - c), me) for j, chip in enumerate(chips)])
    return mine, first, passed, arrivals


def _gather_start(*args):
    mine, first, _, _ = _gather_copies(*args)
    mine.start()
    for cp in first:
        cp.start()


def _gather_finish(*args):
    mine, first, passed, arrivals = _gather_copies(*args)
    for j in range(3):
        arrivals[1 + j].wait_recv()
        passed[j].start()
    arrivals[0].wait_recv()
    for j in range(3):
        arrivals[4 + j].wait_recv()
    for cp in first + passed:
        cp.wait_send()
    mine.wait()


def _gather_sems():
    return [pltpu.SemaphoreType.DMA((GATHER_SEMS,)), pltpu.SemaphoreType.DMA((GATHER_SEMS,)), pltpu.SemaphoreType.DMA]


def _gather_w_in(stage, wgu_s, conv_s):
    def body(stage_hbm, wgu_ref, conv_ref, w_ref, gwgu_ref, gconv_ref, buf, send_sems, recv_sems, local_sem,
             ssend, srecv):
        x, y, c = _position()
        me = 4 * x + 2 * y + c
        args = (stage_hbm, SLAB_IN, IN_ROWS, buf, send_sems, recv_sems, local_sem)
        _gather_start(*args)
        flips = [(k >> 2, (k >> 1) & 1, k & 1) for k in range(1, N_DEV)]
        peers = [(x ^ fx, y ^ fy, c ^ fc) for fx, fy, fc in flips]

        def small(k, block_id, to):
            return [pltpu.make_async_remote_copy(
                src_ref=s, dst_ref=g.at[block_id], send_sem=ssend.at[2 * k + n], recv_sem=srecv.at[2 * k + n],
                device_id=to, device_id_type=MESH)
                for n, (s, g) in enumerate(((wgu_ref, gwgu_ref), (conv_ref, gconv_ref)))]

        gwgu_ref[me] = wgu_ref[...]
        gconv_ref[me] = conv_ref[...]
        for k, peer in enumerate(peers):
            for cp in small(k, me, peer):
                cp.start()
        w_ref[IN_COLS:PW, :] = jnp.zeros((PW - IN_COLS, D), BF16)
        _gather_finish(*args)
        for k, (px, py, pc) in enumerate(peers):
            for cp in small(k, 4 * px + 2 * py + pc, (px, py, pc)):
                cp.wait_recv()
                cp.wait_send()
        for j, lo, hi, d in _in_segments():
            w_ref[d:d + hi - lo, :] = buf[j, lo:hi, :]

    vm = pl.BlockSpec(memory_space=pltpu.VMEM)
    hbm = pl.BlockSpec(memory_space=pl.ANY)
    return pl.pallas_call(
        body,
        name="gather_w_in",
        in_specs=[hbm, vm, vm],
        out_specs=[vm, vm, vm],
        out_shape=[jax.ShapeDtypeStruct((PW, D), BF16),
                   jax.ShapeDtypeStruct((N_DEV,) + wgu_s.shape, F32),
                   jax.ShapeDtypeStruct((N_DEV,) + conv_s.shape, F32)],
        scratch_shapes=[pltpu.VMEM((N_DEV, IN_ROWS, D), BF16)] + _gather_sems()
        + [pltpu.SemaphoreType.DMA((14,)), pltpu.SemaphoreType.DMA((14,))],
        compiler_params=_params(),
    )(stage, wgu_s, conv_s)


def _unpermute_dw_in(dw_t):
    def body(d_ref, g_ref, gb_ref):
        for j in range(N_DEV):
            g_ref[j, IN_W:IN_ROWS, :] = jnp.zeros((IN_ROWS - IN_W, D), F32)
        for j, lo, hi, d in _in_segments():
            g_ref[j, lo:hi, :] = d_ref[d:d + hi - lo, :]
        for j in range(N_DEV):
            gb_ref[j] = g_ref[j].astype(BF16)

    vm = pl.BlockSpec(memory_space=pltpu.VMEM)
    return pl.pallas_call(
        body,
        name="unpermute_dw_in",
        in_specs=[vm],
        out_specs=[vm, vm],
        out_shape=[jax.ShapeDtypeStruct((N_DEV, IN_ROWS, D), F32), jax.ShapeDtypeStruct((N_DEV, IN_ROWS, D), BF16)],
        compiler_params=_params(),
    )(dw_t)


def _stage1_copies(g_refs, leads, r_refs, send_sems, recv_sems):
    x, y, c = _position()
    return [pltpu.make_async_remote_copy(
        src_ref=_at_owner(g_refs[a], leads[a], 2 * i + 1 - c), dst_ref=r_refs[a].at[i],
        send_sem=send_sems.at[4 * a + i], recv_sem=recv_sems.at[4 * a + i],
        device_id=(x, y, 1 - c), device_id_type=MESH) for a in range(len(g_refs)) for i in range(4)]


def _exchange_core(gbs, name):
    n = len(gbs)

    def body(*refs):
        copies = _stage1_copies(refs[:n], [None] * n, refs[n:2 * n], refs[2 * n], refs[2 * n + 1])
        for cp in copies:
            cp.start()
        for cp in copies:
            cp.wait_recv()
        for cp in copies:
            cp.wait_send()

    hbm = pl.BlockSpec(memory_space=pl.ANY)
    return pl.pallas_call(
        body,
        name=name,
        in_specs=[hbm] * n,
        out_specs=[hbm] * n,
        out_shape=[jax.ShapeDtypeStruct((4,) + g.shape[1:], BF16) for g in gbs],
        scratch_shapes=[pltpu.SemaphoreType.DMA((4 * n,)), pltpu.SemaphoreType.DMA((4 * n,))],
        compiler_params=_params(),
    )(*gbs)


def _add_core(g8, r1, pos_arr, name, lead=None):
    rows = g8.shape[-2]
    if lead is None:
        g_spec = pl.BlockSpec((1, rows, D), lambda k, pos: (2 * (pos[1] ^ (k + 1)) + pos[0], 0, 0))
    else:
        g_spec = pl.BlockSpec((None, 1, rows, D), lambda k, pos: (lead, 2 * (pos[1] ^ (k + 1)) + pos[0], 0, 0))

    def body(pos_ref, g_ref, r_ref, pb_ref):
        pb_ref[...] = (g_ref[...] + r_ref[...].astype(F32)).astype(BF16)

    return pl.pallas_call(
        body,
        name=name,
        grid_spec=pltpu.PrefetchScalarGridSpec(
            num_scalar_prefetch=1, grid=(3,),
            in_specs=[g_spec,
                      pl.BlockSpec((1, rows, D), lambda k, pos: (pos[1] ^ (k + 1), 0, 0))],
            out_specs=pl.BlockSpec((1, rows, D), lambda k, pos: (k, 0, 0))),
        out_shape=jax.ShapeDtypeStruct((3, rows, D), BF16),
        compiler_params=_params(("arbitrary",)),
    )(pos_arr, g8, r1)


def _stage2_copies(p_refs, r_refs, send_sems, recv_sems):
    x, y, c = _position()
    copies = []
    for a in range(len(p_refs)):
        for k in range(1, 4):
            copies.append(pltpu.make_async_remote_copy(
                src_ref=p_refs[a].at[k - 1], dst_ref=r_refs[a].at[k - 1],
                send_sem=send_sems.at[3 * a + k - 1], recv_sem=recv_sems.at[3 * a + k - 1],
                device_id=(x ^ (k >> 1), y ^ (k & 1), c), device_id_type=MESH))
    return copies


def _finish_weight(g8, r1, r2, w, m, v, pos_arr, name, lead=None):
    rows = g8.shape[-2]
    wr = w.shape[0]
    nblk = 2 if (rows == wr and rows % 32 == 0) else 1
    rb = rows // nblk
    wb = wr // nblk
    if lead is None:
        g_spec = pl.BlockSpec((1, rb, D), lambda i, pos: (2 * pos[1] + pos[0], i, 0))
    else:
        g_spec = pl.BlockSpec((None, 1, rb, D), lambda i, pos: (lead, 2 * pos[1] + pos[0], i, 0))

    def body(pos_ref, g_ref, r1_ref, r2_ref, w_ref, m_ref, v_ref, g_out, d_out, m_out, v_out):
        g = g_ref[0] + r1_ref[0].astype(F32)
        for k in range(3):
            g = g + r2_ref[k].astype(F32)
        g = g[0:wb, :]
        g_out[...] = g
        d, mn, vn = _adamw_math(w_ref[...], g, m_ref[...], v_ref[...])
        d_out[...] = d
        m_out[...] = mn
        v_out[...] = vn

    wblk = pl.BlockSpec((wb, D), lambda i, pos: (i, 0))
    shp = jax.ShapeDtypeStruct(w.shape, F32)
    return pl.pallas_call(
        body,
        name=name,
        grid_spec=pltpu.PrefetchScalarGridSpec(
            num_scalar_prefetch=1, grid=(nblk,),
            in_specs=[g_spec,
                      pl.BlockSpec((1, rb, D), lambda i, pos: (pos[1], i, 0)),
                      pl.BlockSpec((3, rb, D), lambda i, pos: (0, i, 0)), wblk, wblk, wblk],
            out_specs=[wblk] * 4),
        out_shape=[shp, shp, shp, shp],
        compiler_params=_params(("arbitrary",)),
    )(pos_arr, g8, r1, r2, w, m, v)


SMALL_NAMES = ("norm1_g", "norm2_g", "norm_f_g", "b_gate", "gla_norm_g", "w_gate_up", "conv_w")
SMALL_PACK_ROWS = 32
WGU_W = NQK // N_DEV
CONV_W = CW // N_DEV


def _small_allreduce(parts, loss_part):
    def body(dg1, dg2, dgf, dbg, dgng, dwgu, dcw, lp, acc_ref, pack, gbuf, send_sems, recv_sems):
        x, y, c = _position()
        me = 4 * x + 2 * y + c
        flips = [(k >> 2, (k >> 1) & 1, k & 1) for k in range(1, N_DEV)]
        peers = [(x ^ fx, y ^ fy, c ^ fc) for fx, fy, fc in flips]

        def copy(k, block_id, to):
            return pltpu.make_async_remote_copy(
                src_ref=pack, dst_ref=gbuf.at[block_id], send_sem=send_sems.at[k], recv_sem=recv_sems.at[k],
                device_id=to, device_id_type=MESH)

        pack[...] = jnp.zeros_like(pack)
        pack[0:1, :] = dg1[...]
        pack[1:2, :] = dg2[...]
        pack[2:3, :] = dgf[...]
        pack[3:4, 0:NQK] = dbg[...]
        pack[3:4, NQK:NQK + DV] = dgng[...]
        pack[3:4, NQK + DV:NQK + 2 * DV] = lp[...]
        pack[8:8 + RANK, 0:NQK] = dwgu[0:RANK, :]
        pack[24:24 + CONV_K, 0:CW] = dcw[0:CONV_K, :]
        for k, peer in enumerate(peers):
            copy(k, me, peer).start()
        gbuf[me] = pack[...]
        for k, (px, py, pc) in enumerate(peers):
            cp = copy(k, 4 * px + 2 * py + pc, (px, py, pc))
            cp.wait_recv()
            cp.wait_send()
        acc = gbuf[0]
        for d in range(1, N_DEV):
            acc = acc + gbuf[d]
        acc_ref[...] = acc

    vm = pl.BlockSpec(memory_space=pltpu.VMEM)
    return pl.pallas_call(
        body,
        name="small_allreduce",
        in_specs=[vm] * 8,
        out_specs=vm,
        out_shape=jax.ShapeDtypeStruct((SMALL_PACK_ROWS, D), F32),
        scratch_shapes=[pltpu.VMEM((SMALL_PACK_ROWS, D), F32), pltpu.VMEM((N_DEV, SMALL_PACK_ROWS, D), F32),
                        pltpu.SemaphoreType.DMA((7,)), pltpu.SemaphoreType.DMA((7,))],
        compiler_params=_params(),
    )(*parts, loss_part)


def _small_adamw(sums, ws, ms, vs):
    n = len(SMALL_NAMES)

    def body(*refs):
        acc_ref = refs[0]
        w_refs, m_refs, v_refs = refs[1:1 + n], refs[1 + n:1 + 2 * n], refs[1 + 2 * n:1 + 3 * n]
        loss_ref = refs[1 + 3 * n]
        outs = refs[2 + 3 * n:]
        x, y, c = _position()
        me = 4 * x + 2 * y + c
        acc = acc_ref[...]
        loss_ref[...] = acc[3:4, NQK + DV:NQK + DV + 1]

        def my_columns(full, width):
            r = lax.broadcasted_iota(jnp.int32, (full.shape[1], width), 0)
            col = lax.broadcasted_iota(jnp.int32, (full.shape[1], width), 1)
            sel = (r == width * me + col).astype(F32)
            return _mm(full, sel, precision=HIGHEST)

        grads = [acc[0:1, :], acc[1:2, :], acc[2:3, :], acc[3:4, 0:NQK], acc[3:4, NQK:NQK + DV],
                 my_columns(acc[8:8 + RANK, 0:NQK], WGU_W), my_columns(acc[24:24 + CONV_K, 0:CW], CONV_W)]
        for i, g in enumerate(grads):
            d, mn, vn = _adamw_math(w_refs[i][...], g, m_refs[i][...], v_refs[i][...])
            outs[4 * i][...] = g
            outs[4 * i + 1][...] = d
            outs[4 * i + 2][...] = mn
            outs[4 * i + 3][...] = vn

    vm = pl.BlockSpec(memory_space=pltpu.VMEM)
    out_shape = [jax.ShapeDtypeStruct((1, 1), F32)]
    for w in ws:
        out_shape += [jax.ShapeDtypeStruct(w.shape, F32)] * 4
    return pl.pallas_call(
        body,
        name="small_adamw",
        in_specs=[vm] * (1 + 3 * n),
        out_specs=[vm] * (1 + 4 * n),
        out_shape=out_shape,
        compiler_params=_params(),
    )(sums, *ws, *ms, *vs)


def kernel(x, norm1_g, w_in, w_gate_up, b_gate, gla_norm_g, conv_w, w_out, norm2_g, w_ffn_gate, w_ffn_up, w_ffn_down, norm_f_g, loss_target, m_norm1_g, m_w_in, m_w_gate_up, m_b_gate, m_gla_norm_g, m_conv_w, m_w_out, m_norm2_g, m_w_ffn_gate, m_w_ffn_up, m_w_ffn_down, m_norm_f_g, v_norm1_g, v_w_in, v_w_gate_up, v_b_gate, v_gla_norm_g, v_conv_w, v_w_out, v_norm2_g, v_w_ffn_gate, v_w_ffn_up, v_w_ffn_down, v_norm_f_g):
    xi, yi, ci = _position()
    pos_arr = jnp.stack([ci, 2 * xi + yi]).astype(jnp.int32)
    nb, s, _ = x.shape
    t = nb * s

    tr = lambda a: a[0].T
    stage = _prep_slab(tr(w_in), tr(w_ffn_gate), tr(w_ffn_up), w_ffn_down[0], w_out[0])
    w_in_t, gwgu, gconv = _gather_w_in(stage, w_gate_up[0], conv_w[0])
    wgu_f = gwgu.transpose(1, 0, 2).reshape(RANK, NQK)
    conv_f = gconv.transpose(1, 0, 2).reshape(CONV_K, CW)
    wgu_p = jnp.concatenate([wgu_f, jnp.zeros((A_PAD - RANK, NQK), F32)], axis=0).astype(BF16)

    x2d = x.reshape(t, D)
    tgt2d = loss_target.reshape(t, D)
    tm = 256
    tm_in = min(512, t)
    tk = min(2048, t)
    proj, z, h, gwb = _in_proj_fwd(x2d, norm1_g, w_in_t, wgu_p, b_gate, tm_in, stage)
    proj3 = proj.reshape(nb, s, PW)
    z3 = z.reshape(nb, s, NQK)
    mix3, opre3, sprev, gwa = _mix_fwd(proj3, z3, gla_norm_g, conv_f, stage)
    mix2d = mix3.reshape(t, D)
    dx1, dx1b, dmix, adu, hb, dg2, dgf, loss_part = _ffn_fwd_bwd(
        mix2d, x2d, tgt2d, gwa, gwb, norm2_g, norm_f_g.reshape(1, D), tm)
    dw3, dwb3 = _dw_ffn(adu, hb, tk)
    dw3 = dw3.reshape(3, N_DEV, FF_W, D)
    dw_o, dwb_o, *r1_ffn = _tn_matmul(mix2d, dx1b, D // 4, D, tk, "dw_out", True,
                                      dwb3.reshape(3, N_DEV, FF_W, D))
    g8 = [dw3, dw3, dw3, dw_o.reshape(N_DEV, OUT_ROWS, D)]
    leads = [0, 1, 2, None]
    tags = ("w_ffn_down", "w_ffn_gate", "w_ffn_up", "w_out")
    r1 = list(r1_ffn) + list(_exchange_core([dwb_o.reshape(N_DEV, OUT_ROWS, D)], "grad_exchange_core_out"))
    pb = [_add_core(g, r, pos_arr, "grad_add_core_" + tag, lead)
          for g, r, tag, lead in zip(g8, r1, tags, leads)]
    mb = _mix_bwd(proj3, z3, sprev, opre3, dmix.reshape(nb, s, D), gla_norm_g, conv_f, wgu_p, pb)
    dproj3, dgng, dcw, dbg, dwgu = mb[:5]
    r2 = list(mb[5:])
    dproj2d = dproj3.reshape(t, PW)
    dw_in_t = _tn_matmul(dproj2d, h, PW // 5, D, tk, "dw_in", False)
    g_in, gb_in = _unpermute_dw_in(dw_in_t)
    (r1_in,) = _exchange_core([gb_in], "grad_exchange_core_in")
    pb_in = _add_core(g_in, r1_in, pos_arr, "grad_add_core_w_in")
    dx, dg1, r2_in = _in_proj_bwd(dproj2d, x2d, dx1, norm1_g, w_in_t, tm_in, pb_in)

    tags = ("w_in",) + tags
    g8 = [g_in] + g8
    leads = [None] + leads
    r1 = [r1_in] + list(r1)
    r2 = [r2_in] + r2
    shard_w = (tr(w_in), w_ffn_down[0], tr(w_ffn_gate), tr(w_ffn_up), w_out[0])
    shard_m = (tr(m_w_in), m_w_ffn_down[0], tr(m_w_ffn_gate), tr(m_w_ffn_up), m_w_out[0])
    shard_v = (tr(v_w_in), v_w_ffn_down[0], tr(v_w_ffn_gate), tr(v_w_ffn_up), v_w_out[0])
    transposed = (True, False, True, True, False)
    results = {}
    for tag, g, lead, ra, rb, w, m, v, tp in zip(tags, g8, leads, r1, r2, shard_w, shard_m, shard_v, transposed):
        outs = _finish_weight(g, ra, rb, w, m, v, pos_arr, "finish_" + tag, lead)
        results[tag] = [o.T[None] if tp else o[None] for o in outs]

    small_w = (norm1_g, norm2_g, norm_f_g.reshape(1, D), b_gate, gla_norm_g, w_gate_up[0], conv_w[0])
    small_m = (m_norm1_g, m_norm2_g, m_norm_f_g.reshape(1, D), m_b_gate, m_gla_norm_g, m_w_gate_up[0], m_conv_w[0])
    small_v = (v_norm1_g, v_norm2_g, v_norm_f_g.reshape(1, D), v_b_gate, v_gla_norm_g, v_w_gate_up[0], v_conv_w[0])
    so = _small_adamw(_small_allreduce((dg1, dg2, dgf, dbg, dgng, dwgu, dcw), loss_part), small_w, small_m, small_v)
    loss = so[0].reshape(())
    shapes = {"norm_f_g": (D,), "w_gate_up": (1, RANK, WGU_W), "conv_w": (1, CONV_K, CONV_W)}
    for i, name in enumerate(SMALL_NAMES):
        results[name] = [o.reshape(shapes[name]) if name in shapes else o for o in so[1 + 4 * i:5 + 4 * i]]

    names = ("norm1_g", "w_in", "w_gate_up", "b_gate", "gla_norm_g", "conv_w", "w_out", "norm2_g",
             "w_ffn_gate", "w_ffn_up", "w_ffn_down", "norm_f_g")
    outs = [loss, dx.reshape(nb, s, D)]
    for kind in range(4):
        for name in names:
            outs.append(results[name][kind])
    return tuple(outs)
```

```python
import jax
import jax.numpy as jnp
from jax import lax
from jax.experimental import pallas as pl
from jax.experimental.pallas import tpu as pltpu

F32 = jnp.float32
BF16 = jnp.bfloat16
HIGHEST = lax.Precision.HIGHEST
MESH = pl.DeviceIdType.MESH

N_DEV = 8
D = 1024
DFF = 2816
HEADS = 4
DK = 64
DV = 128
NQK = HEADS * DK
NV = HEADS * DV
RANK = 16
CHUNK = 64
CW = 512
CONV_K = 3
IN_COLS = 3088
EPS = 1e-6
INV_GATE_NORM = 1.0 / 16.0
Q_SCALE = DK ** -0.5

PW = 3200
OQ, OK_, OV, OG, OCB, OCC, OCH, OA = 0, 256, 512, 1024, 1536, 2048, 2560, 3072
A_PAD = 128

ADAM_LR = 0.001
ADAM_B1 = 0.9
ADAM_B2 = 0.999
ADAM_EPS = 1e-08
ADAM_WD = 0.01
ADAM_STEP = 10

IN_W = IN_COLS // N_DEV
IN_ROWS = 400
FF_W = DFF // N_DEV
OUT_ROWS = D // N_DEV
SLAB_IN = 0
SLAB_G = SLAB_IN + IN_ROWS
SLAB_U = SLAB_G + FF_W
SLAB_D = SLAB_U + FF_W
SLAB_O = SLAB_D + FF_W
SLAB_ROWS = SLAB_O + OUT_ROWS

VMEM_LIMIT = 56 * 1024 * 1024


def _params(sem=None, vmem=VMEM_LIMIT):
    return pltpu.CompilerParams(dimension_semantics=sem, vmem_limit_bytes=vmem)


def _nt(a, b):
    return lax.dot_general(a, b, (((1,), (1,)), ((), ())), preferred_element_type=F32)


def _tn(a, b, precision=None):
    return lax.dot_general(a, b, (((0,), (0,)), ((), ())), preferred_element_type=F32, precision=precision)


def _mm(a, b, precision=None):
    return jnp.dot(a, b, preferred_element_type=F32, precision=precision)


def _in_segments():
    segs = []
    for j in range(N_DEV):
        lo, hi = IN_W * j, IN_W * (j + 1)
        cuts = sorted({lo, hi} | {c for c in (OCB, OCB + RANK) if lo < c < hi})
        for a, b in zip(cuts[:-1], cuts[1:]):
            if a < OCB:
                d = a
            elif a < OCB + RANK:
                d = OA + (a - OCB)
            else:
                d = a - RANK
            segs.append((j, a - lo, b - lo, d))
    return segs


def _in_proj_fwd(x2d, g1, w_in_t, wgu_p, b_gate, tm, stage):
    t = x2d.shape[0]
    nt = t // tm
    g_rows = SLAB_ROWS - SLAB_D

    def body(x_ref, g_ref, w_ref, wgu_ref, bg_ref, stage_hbm, proj_ref, z_ref, h_ref, gwb_ref,
             send_sems, recv_sems, local_sem):
        gargs = (stage_hbm, SLAB_D, g_rows, gwb_ref, send_sems, recv_sems, local_sem)

        @pl.when(pl.program_id(0) == 0)
        def _():
            _gather_start(*gargs)

        x = x_ref[...]
        r = lax.rsqrt(jnp.mean(x * x, axis=-1, keepdims=True) + EPS)
        h = ((x * r) * g_ref[...]).astype(BF16)
        h_ref[...] = h
        proj = _nt(h, w_ref[...])
        proj_ref[...] = proj
        pa = proj[:, OA:OA + A_PAD].astype(BF16)
        z_ref[...] = _mm(pa, wgu_ref[...]) + bg_ref[...]

        @pl.when(pl.program_id(0) == nt - 1)
        def _():
            _gather_finish(*gargs)

    return pl.pallas_call(
        body,
        name="in_proj_fwd",
        grid=(t // tm,),
        in_specs=[
            pl.BlockSpec((tm, D), lambda i: (i, 0)),
            pl.BlockSpec((1, D), lambda i: (0, 0)),
            pl.BlockSpec((PW, D), lambda i: (0, 0)),
            pl.BlockSpec((A_PAD, NQK), lambda i: (0, 0)),
            pl.BlockSpec((1, NQK), lambda i: (0, 0)),
            pl.BlockSpec(memory_space=pl.ANY),
        ],
        out_specs=[
            pl.BlockSpec((tm, PW), lambda i: (i, 0)),
            pl.BlockSpec((tm, NQK), lambda i: (i, 0)),
            pl.BlockSpec((tm, D), lambda i: (i, 0)),
            pl.BlockSpec(memory_space=pl.ANY),
        ],
        out_shape=[
            jax.ShapeDtypeStruct((t, PW), F32),
            jax.ShapeDtypeStruct((t, NQK), F32),
            jax.ShapeDtypeStruct((t, D), BF16),
            jax.ShapeDtypeStruct((N_DEV, g_rows, D), BF16),
        ],
        scratch_shapes=_gather_sems(),
        compiler_params=_params(("arbitrary",)),
    )(x2d, g1, w_in_t, wgu_p, b_gate, stage)


def _head_masks():
    lane = lax.broadcasted_iota(jnp.int32, (1, NQK), 1)
    return [(lane >= DK * h) & (lane < DK * (h + 1)) for h in range(HEADS)]


def _split_bf16(x, n):
    parts = []
    for _ in range(n):
        p = x.astype(BF16)
        parts.append(p)
        x = x - p.astype(F32)
    return parts


def _chunk_fwd_parts(q, k, z, tril16):
    la = (jnp.minimum(z, 0.0) - jnp.log1p(jnp.exp(-jnp.abs(z)))) * INV_GATE_NORM
    la_parts = _split_bf16(la, 3)
    bc = _mm(tril16, la_parts[0]) + _mm(tril16, la_parts[1]) + _mm(tril16, la_parts[2])
    bl = bc[CHUNK - 1:CHUNK, :]
    eb = jnp.exp(bc)
    enb = jnp.exp(-bc)
    ekl = jnp.exp(bl - bc)
    qi = (q * Q_SCALE) * eb
    ki = k * enb
    ks = k * ekl
    ones16 = jnp.ones((CHUNK, DV), BF16)
    decb = jnp.exp(_tn(la_parts[0], ones16) + _tn(la_parts[1], ones16) + _tn(la_parts[2], ones16))
    return la, eb, enb, ekl, qi, ki, ks, decb


def _stack_heads(a, masks):
    return jnp.concatenate([jnp.where(m, a, 0.0) for m in masks], axis=0)


def _merge_heads(blocks, masks):
    out = blocks[HEADS - 1]
    for h in range(HEADS - 2, -1, -1):
        out = jnp.where(masks[h], blocks[h], out)
    return out


def _causal_stack_mask():
    row = lax.broadcasted_iota(jnp.int32, (HEADS * CHUNK, CHUNK), 0)
    col = lax.broadcasted_iota(jnp.int32, (HEADS * CHUNK, CHUNK), 1)
    return (row & (CHUNK - 1)) >= col


def _conv_taps(u, uprev):
    row = lax.broadcasted_iota(jnp.int32, u.shape, 0)
    u1 = jnp.where(row < 1, pltpu.roll(uprev, 1, 0), pltpu.roll(u, 1, 0))
    u2 = jnp.where(row < 2, pltpu.roll(uprev, 2, 0), pltpu.roll(u, 2, 0))
    return u1, u2


def _mix_fwd(proj3, z3, gng, conv_w, stage):
    nb, s, _ = proj3.shape
    nc = s // CHUNK
    g_rows = SLAB_D - SLAB_G

    def body(p_ref, z_ref, gng_ref, cw_ref, stage_hbm, mix_ref, o_ref, sprev_ref, gwa_ref, s_ref, uprev_ref,
             send_sems, recv_sems, local_sem):
        n = pl.program_id(0)
        gargs = (stage_hbm, SLAB_G, g_rows, gwa_ref, send_sems, recv_sems, local_sem)

        @pl.when(n == 0)
        def _():
            _gather_start(*gargs)
            s_ref[...] = jnp.zeros_like(s_ref)
            uprev_ref[...] = jnp.zeros_like(uprev_ref)

        r_i = lax.broadcasted_iota(jnp.int32, (CHUNK, CHUNK), 0)
        c_i = lax.broadcasted_iota(jnp.int32, (CHUNK, CHUNK), 1)
        tril16 = (r_i >= c_i).astype(BF16)
        masks = _head_masks()
        cmask = _causal_stack_mask()
        gg = gng_ref[...]
        for b in range(nb):
            q = p_ref[b, :, OQ:OQ + NQK]
            k = p_ref[b, :, OK_:OK_ + NQK]
            _, _, _, _, qi, ki, ks, decb = _chunk_fwd_parts(q, k, z_ref[b], tril16)
            qs = _stack_heads(qi, masks).astype(BF16)
            sc = jnp.where(cmask, _nt(qs, ki.astype(BF16)), 0.0).astype(BF16)
            st = s_ref[b]
            sprev_ref[b, 0] = st
            o_inter = _mm(qs, st.astype(BF16))
            v16 = p_ref[b, :, OV:OV + NV].astype(BF16)
            kv = _tn(ks.astype(BF16), v16)
            for h in range(HEADS):
                rows = slice(CHUNK * h, CHUNK * (h + 1))
                cols = slice(DV * h, DV * (h + 1))
                o = _mm(sc[rows], v16[:, cols]) + o_inter[rows]
                o_ref[b, :, cols] = o
                r = lax.rsqrt(jnp.mean(o * o, axis=-1, keepdims=True) + EPS)
                on = (o * r) * gg
                g = p_ref[b, :, OG + DV * h:OG + DV * (h + 1)]
                mix_ref[b, :, cols] = (on * (g * jax.nn.sigmoid(g))).astype(BF16)
                s_ref[b, rows, :] = decb[rows] * st[rows] + kv[rows, cols]
            u = p_ref[b, :, OCC:OCC + CW] * p_ref[b, :, OCH:OCH + CW]
            u1, u2 = _conv_taps(u, uprev_ref[b])
            yc = cw_ref[0:1, :] * u2 + cw_ref[1:2, :] * u1 + cw_ref[2:3, :] * u
            mix_ref[b, :, NV:NV + CW] = (p_ref[b, :, OCB:OCB + CW] * yc).astype(BF16)
            uprev_ref[b] = u

        @pl.when(n == nc - 1)
        def _():
            _gather_finish(*gargs)

    return pl.pallas_call(
        body,
        name="mix_fwd",
        grid=(nc,),
        in_specs=[
            pl.BlockSpec((nb, CHUNK, PW), lambda n: (0, n, 0)),
            pl.BlockSpec((nb, CHUNK, NQK), lambda n: (0, n, 0)),
            pl.BlockSpec((1, DV), lambda n: (0, 0)),
            pl.BlockSpec((CONV_K, CW), lambda n: (0, 0)),
            pl.BlockSpec(memory_space=pl.ANY),
        ],
        out_specs=[
            pl.BlockSpec((nb, CHUNK, D), lambda n: (0, n, 0)),
            pl.BlockSpec((nb, CHUNK, NV), lambda n: (0, n, 0)),
            pl.BlockSpec((nb, 1, NQK, DV), lambda n: (0, n, 0, 0)),
            pl.BlockSpec(memory_space=pl.ANY),
        ],
        out_shape=[
            jax.ShapeDtypeStruct((nb, s, D), BF16),
            jax.ShapeDtypeStruct((nb, s, NV), F32),
            jax.ShapeDtypeStruct((nb, nc, NQK, DV), F32),
            jax.ShapeDtypeStruct((N_DEV, g_rows, D), BF16),
        ],
        scratch_shapes=[pltpu.VMEM((nb, NQK, DV), F32), pltpu.VMEM((nb, CHUNK, CW), F32)] + _gather_sems(),
        compiler_params=_params(("arbitrary",)),
    )(proj3, z3, gng, conv_w, stage)


def _ffn_fwd_bwd(mix2d, x2d, tgt2d, gwa, gwb, g2, gf, tm):
    t = x2d.shape[0]

    def body(mix_ref, x_ref, tgt_ref, g2_ref, gf_ref, gwa_hbm, gwb_hbm,
             dx1_ref, dx1b_ref, dmix_ref, adu_ref, hb_ref, dg2_ref, dgf_ref, loss_ref,
             wo, wg, wu, wd, wsem):
        i = pl.program_id(0)

        def weight_copies(n, dst, src, off, rows):
            return [pltpu.make_async_copy(src.at[j, pl.ds(off, rows), :], dst.at[pl.ds(rows * j, rows), :],
                                          wsem.at[N_DEV * n + j]) for j in range(N_DEV)]

        loads = (weight_copies(0, wo, gwb_hbm, FF_W, OUT_ROWS), weight_copies(1, wg, gwa_hbm, 0, FF_W),
                 weight_copies(2, wu, gwa_hbm, FF_W, FF_W), weight_copies(3, wd, gwb_hbm, 0, FF_W))

        @pl.when(i == 0)
        def _():
            for group in loads:
                for cp in group:
                    cp.start()
            dg2_ref[...] = jnp.zeros_like(dg2_ref)
            dgf_ref[...] = jnp.zeros_like(dgf_ref)
            loss_ref[...] = jnp.zeros_like(loss_ref)
            for group in loads:
                for cp in group:
                    cp.wait()

        g2v = g2_ref[...]
        gfv = gf_ref[...]
        x1 = x_ref[...] + _mm(mix_ref[...], wo[...])
        r2 = lax.rsqrt(jnp.mean(x1 * x1, axis=-1, keepdims=True) + EPS)
        n2 = x1 * r2
        h2 = (n2 * g2v).astype(BF16)
        hb_ref[1] = h2
        gate = _nt(h2, wg[...])
        up = _nt(h2, wu[...])
        sg = jax.nn.sigmoid(gate)
        sil = gate * sg
        act = (sil * up).astype(BF16)
        adu_ref[0] = act
        x2 = x1 + _mm(act, wd[...])
        rf = lax.rsqrt(jnp.mean(x2 * x2, axis=-1, keepdims=True) + EPS)
        nf = x2 * rf
        err = nf * gfv - tgt_ref[...]
        loss_ref[...] += 0.5 * jnp.sum(jnp.mean(err * err, axis=-1, keepdims=True))
        dy = err * (1.0 / D)
        dgf_ref[...] += jnp.sum(dy * nf, axis=0, keepdims=True)
        dnf = dy * gfv
        dx2 = rf * (dnf - nf * jnp.mean(dnf * nf, axis=-1, keepdims=True))
        dx2b = dx2.astype(BF16)
        hb_ref[0] = dx2b
        dact = _nt(dx2b, wd[...])
        dup = (dact * sil).astype(BF16)
        dgate = ((dact * up) * (sg * (1.0 + gate * (1.0 - sg)))).astype(BF16)
        adu_ref[2] = dup
        adu_ref[1] = dgate
        dh2 = _mm(dgate, wg[...]) + _mm(dup, wu[...])
        dg2_ref[...] += jnp.sum(dh2 * n2, axis=0, keepdims=True)
        dn2 = dh2 * g2v
        dx1 = dx2 + r2 * (dn2 - n2 * jnp.mean(dn2 * n2, axis=-1, keepdims=True))
        dx1_ref[...] = dx1
        dx1b = dx1.astype(BF16)
        dx1b_ref[...] = dx1b
        dmix_ref[...] = _nt(dx1b, wo[...])

    tile = lambda w: pl.BlockSpec((tm, w), lambda i: (i, 0))
    vec = pl.BlockSpec((1, D), lambda i: (0, 0))
    hbm = pl.BlockSpec(memory_space=pl.ANY)
    return pl.pallas_call(
        body,
        name="ffn_fwd_bwd",
        grid=(t // tm,),
        in_specs=[tile(D), tile(D), tile(D), vec, vec, hbm, hbm],
        out_specs=[tile(D), tile(D), tile(D), pl.BlockSpec((3, tm, DFF), lambda i: (0, i, 0)),
                   pl.BlockSpec((2, tm, D), lambda i: (0, i, 0)), vec, vec,
                   pl.BlockSpec((1, 128), lambda i: (0, 0))],
        out_shape=[
            jax.ShapeDtypeStruct((t, D), F32),
            jax.ShapeDtypeStruct((t, D), BF16),
            jax.ShapeDtypeStruct((t, D), F32),
            jax.ShapeDtypeStruct((3, t, DFF), BF16),
            jax.ShapeDtypeStruct((2, t, D), BF16),
            jax.ShapeDtypeStruct((1, D), F32),
            jax.ShapeDtypeStruct((1, D), F32),
            jax.ShapeDtypeStruct((1, 128), F32),
        ],
        scratch_shapes=[pltpu.VMEM((D, D), BF16), pltpu.VMEM((DFF, D), BF16), pltpu.VMEM((DFF, D), BF16),
                        pltpu.VMEM((DFF, D), BF16), pltpu.SemaphoreType.DMA((4 * N_DEV,))],
        compiler_params=_params(("arbitrary",)),
    )(mix2d, x2d, tgt2d, g2, gf, gwa, gwb)


def _stage1_rider(stack):
    n = stack.shape[0]
    return dict(inputs=[stack], out_shape=[jax.ShapeDtypeStruct((4,) + stack.shape[2:], BF16)] * n, nsem=4 * n,
                copies=lambda ins, outs, send, recv: _stage1_copies([ins[0]] * n, list(range(n)), outs, send, recv))


def _stage2_rider(pbs):
    return dict(inputs=list(pbs), out_shape=[jax.ShapeDtypeStruct(p.shape, BF16) for p in pbs], nsem=3 * len(pbs),
                copies=_stage2_copies)


def _tn_matmul(a, b, bm, bn, tk, name, with_bf16, rider=None):
    t, m = a.shape
    n = b.shape[1]
    nk = t // tk
    nout = 2 if with_bf16 else 1
    grid = (m // bm, n // bn, nk)
    r_in = [] if rider is None else rider["inputs"]
    r_out = [] if rider is None else rider["out_shape"]

    def body(a_ref, b_ref, *rest):
        ins, outs = rest[:len(r_in)], rest[len(r_in):len(r_in) + nout]
        r_outs, sems = rest[len(r_in) + nout:len(r_in) + nout + len(r_out)], rest[len(r_in) + nout + len(r_out):]
        o_ref = outs[0]
        i, j, k = pl.program_id(0), pl.program_id(1), pl.program_id(2)
        if rider is not None:
            @pl.when((i == 0) & (j == 0) & (k == 0))
            def _():
                for cp in rider["copies"](ins, r_outs, *sems):
                    cp.start()

        @pl.when(k == 0)
        def _():
            o_ref[...] = jnp.zeros_like(o_ref)

        o_ref[...] += _tn(a_ref[...].astype(BF16), b_ref[...].astype(BF16))
        if with_bf16:
            @pl.when(k == nk - 1)
            def _():
                outs[1][...] = o_ref[...].astype(BF16)
        if rider is not None:
            @pl.when((i == grid[0] - 1) & (j == grid[1] - 1) & (k == nk - 1))
            def _():
                copies = rider["copies"](ins, r_outs, *sems)
                for cp in copies:
                    cp.wait_recv()
                for cp in copies:
                    cp.wait_send()

    out_blk = pl.BlockSpec((bm, bn), lambda i, j, k: (i, j))
    hbm = pl.BlockSpec(memory_space=pl.ANY)
    out_shape = [jax.ShapeDtypeStruct((m, n), F32)] + ([jax.ShapeDtypeStruct((m, n), BF16)] if with_bf16 else [])
    res = pl.pallas_call(
        body,
        name=name,
        grid=grid,
        in_specs=[pl.BlockSpec((tk, bm), lambda i, j, k: (k, i)), pl.BlockSpec((tk, bn), lambda i, j, k: (k, j))]
        + [hbm] * len(r_in),
        out_specs=[out_blk] * nout + [hbm] * len(r_out),
        out_shape=out_shape + list(r_out),
        scratch_shapes=([] if rider is None else
                        [pltpu.SemaphoreType.DMA((rider["nsem"],)), pltpu.SemaphoreType.DMA((rider["nsem"],))]),
        compiler_params=_params(("parallel", "parallel", "arbitrary") if rider is None
                                else ("arbitrary", "arbitrary", "arbitrary")),
    )(a, b, *r_in)
    return res[0] if len(res) == 1 else res


def _dw_ffn(adu, hb, tk):
    _, t, _ = adu.shape
    bm = DFF // 2
    nk = t // tk

    def body(a_ref, b_ref, o_ref, ob_ref):
        k = pl.program_id(2)

        @pl.when(k == 0)
        def _():
            o_ref[...] = jnp.zeros_like(o_ref)

        o_ref[...] += _tn(a_ref[...], b_ref[...])

        @pl.when(k == nk - 1)
        def _():
            ob_ref[...] = o_ref[...].astype(BF16)

    out_blk = pl.BlockSpec((None, bm, D), lambda p, i, k: (p, i, 0))
    return pl.pallas_call(
        body,
        name="dw_ffn",
        grid=(3, DFF // bm, nk),
        in_specs=[pl.BlockSpec((None, tk, bm), lambda p, i, k: (p, k, i)),
                  pl.BlockSpec((None, tk, D), lambda p, i, k: (jnp.minimum(p, 1), k, 0))],
        out_specs=[out_blk, out_blk],
        out_shape=[jax.ShapeDtypeStruct((3, DFF, D), F32), jax.ShapeDtypeStruct((3, DFF, D), BF16)],
        compiler_params=_params(("arbitrary", "arbitrary", "arbitrary")),
    )(adu, hb)


def _at_owner(ref, lead, idx):
    return ref.at[idx] if lead is None else ref.at[lead, idx]


def _mix_bwd(proj3, z3, sprev, opre3, dmix3, gng, conv_w, wgu_p, pbs):
    nb, s, _ = proj3.shape
    nc = s // CHUNK
    na = len(pbs)

    def body(*refs):
        (p_ref, pprev_ref, z_ref, sp_ref, o_ref, dm_ref, gng_ref, cw_ref, wgu_ref) = refs[:9]
        pb_refs = refs[9:9 + na]
        (dproj_ref, dgng_ref, dcw_ref, dbg_ref, dwgu_ref) = refs[9 + na:14 + na]
        r2_refs = refs[14 + na:14 + 2 * na]
        ds_ref, dycn_ref, send_sems, recv_sems = refs[14 + 2 * na:]
        step = pl.program_id(0)
        n = nc - 1 - step

        @pl.when(step == 0)
        def _():
            for cp in _stage2_copies(pb_refs, r2_refs, send_sems, recv_sems):
                cp.start()
            ds_ref[...] = jnp.zeros_like(ds_ref)
            dycn_ref[...] = jnp.zeros_like(dycn_ref)
            dgng_ref[...] = jnp.zeros_like(dgng_ref)
            dcw_ref[...] = jnp.zeros_like(dcw_ref)
            dbg_ref[...] = jnp.zeros_like(dbg_ref)
            dwgu_ref[...] = jnp.zeros_like(dwgu_ref)

        r_i = lax.broadcasted_iota(jnp.int32, (CHUNK, CHUNK), 0)
        c_i = lax.broadcasted_iota(jnp.int32, (CHUNK, CHUNK), 1)
        tril16 = (r_i >= c_i).astype(BF16)
        triu16 = (r_i <= c_i).astype(BF16)
        causal = r_i >= c_i
        masks = _head_masks()
        cmask = _causal_stack_mask()
        gg = gng_ref[...]
        last_row = lax.broadcasted_iota(jnp.int32, (CHUNK, NQK), 0) == CHUNK - 1
        ones_r = jnp.ones((16, DV), BF16)
        has_prev = (n > 0).astype(F32)
        for b in range(nb):
            q = p_ref[b, :, OQ:OQ + NQK]
            k = p_ref[b, :, OK_:OK_ + NQK]
            z = z_ref[b]
            _, eb, enb, ekl, qi, ki, ks, decb = _chunk_fwd_parts(q, k, z, tril16)
            qi16 = qi.astype(BF16)
            ki16 = ki.astype(BF16)
            qs = _stack_heads(qi, masks).astype(BF16)
            sc = jnp.where(cmask, _nt(qs, ki16), 0.0).astype(BF16)
            st = sp_ref[b, 0]
            st16 = st.astype(BF16)
            dsn = ds_ref[b]
            dsn16 = dsn.astype(BF16)
            v16 = p_ref[b, :, OV:OV + NV].astype(BF16)
            do16 = []
            dgng = jnp.zeros((1, DV), F32)
            for h in range(HEADS):
                cols = slice(DV * h, DV * (h + 1))
                o = o_ref[b, :, cols]
                r = lax.rsqrt(jnp.mean(o * o, axis=-1, keepdims=True) + EPS)
                nh = o * r
                g = p_ref[b, :, OG + DV * h:OG + DV * (h + 1)]
                sg = jax.nn.sigmoid(g)
                dog = dm_ref[b, :, cols]
                dproj_ref[b, :, OG + DV * h:OG + DV * (h + 1)] = (
                    (dog * (nh * gg)) * (sg * (1.0 + g * (1.0 - sg)))).astype(BF16)
                don = dog * (g * sg)
                dgng = dgng + jnp.sum(don * nh, axis=0, keepdims=True)
                dn = don * gg
                do = r * (dn - nh * jnp.mean(dn * nh, axis=-1, keepdims=True))
                do16.append(do.astype(BF16))
            dgng_ref[...] += dgng
            do_rows = jnp.concatenate(do16, axis=0)
            v_rows = jnp.concatenate([v16[:, DV * h:DV * (h + 1)] for h in range(HEADS)], axis=0)
            dp16 = [jnp.where(causal, _nt(do16[h], v16[:, DV * h:DV * (h + 1)]), 0.0).astype(BF16)
                    for h in range(HEADS)]
            ks_dsn = _mm(_stack_heads(ks, masks).astype(BF16), dsn16)
            do_st = _nt(do_rows, st16)
            v_dsn = _nt(v_rows, dsn16)
            dp_ki = _mm(jnp.concatenate(dp16, axis=0), ki16)
            q_do = _tn(qi16, jnp.concatenate(do16, axis=1))
            dki_h = []
            for h in range(HEADS):
                rows = slice(CHUNK * h, CHUNK * (h + 1))
                cols = slice(DV * h, DV * (h + 1))
                dv = _tn(sc[rows], do16[h]) + ks_dsn[rows]
                dproj_ref[b, :, OV + DV * h:OV + DV * (h + 1)] = dv.astype(BF16)
                dki_h.append(_tn(dp16[h], qi16))
                ds_ref[b, rows, :] = decb[rows] * dsn[rows] + q_do[rows, cols]
            blocks = lambda a: [a[CHUNK * h:CHUNK * (h + 1)] for h in range(HEADS)]
            dqi = _merge_heads(blocks(dp_ki + do_st), masks)
            dki = _merge_heads(dki_h, masks)
            dks = _merge_heads(blocks(v_dsn), masks)
            dproj_ref[b, :, OQ:OQ + NQK] = (dqi * (Q_SCALE * eb)).astype(BF16)
            dproj_ref[b, :, OK_:OK_ + NQK] = (dki * enb + dks * ekl).astype(BF16)
            dks_ks = dks * ks
            db = dqi * qi - dki * ki - dks_ks
            sd = _split_bf16(dsn * st * decb, 2)
            dbl = jnp.sum(dks_ks, axis=0, keepdims=True) + (_nt(ones_r, sd[0]) + _nt(ones_r, sd[1]))[0:1, :]
            db = db + jnp.where(last_row, dbl, 0.0)
            db_parts = _split_bf16(db, 3)
            dla = _mm(triu16, db_parts[0]) + _mm(triu16, db_parts[1]) + _mm(triu16, db_parts[2])
            dz = (dla * INV_GATE_NORM) * (1.0 / (1.0 + jnp.exp(z)))
            dbg_ref[...] += jnp.sum(dz, axis=0, keepdims=True)
            dz16 = dz.astype(BF16)
            pa16 = p_ref[b, :, OA:OA + A_PAD].astype(BF16)
            dwgu_ref[...] += _tn(pa16, dz16)
            dproj_ref[b, :, OA:OA + A_PAD] = _nt(dz16, wgu_ref[...]).astype(BF16)
            cb = p_ref[b, :, OCB:OCB + CW]
            cc = p_ref[b, :, OCC:OCC + CW]
            ch = p_ref[b, :, OCH:OCH + CW]
            u = cc * ch
            uprev = (pprev_ref[b, :, 0:CW] * pprev_ref[b, :, CW:2 * CW]) * has_prev
            u1, u2 = _conv_taps(u, uprev)
            w0 = cw_ref[0:1, :]
            w1 = cw_ref[1:2, :]
            w2 = cw_ref[2:3, :]
            yc = w0 * u2 + w1 * u1 + w2 * u
            doc = dm_ref[b, :, NV:NV + CW]
            dproj_ref[b, :, OCB:OCB + CW] = (doc * yc).astype(BF16)
            dyc = doc * cb
            dycn = dycn_ref[b]
            row = lax.broadcasted_iota(jnp.int32, dyc.shape, 0)
            d1 = jnp.where(row >= CHUNK - 1, pltpu.roll(dycn, CHUNK - 1, 0), pltpu.roll(dyc, CHUNK - 1, 0))
            d2 = jnp.where(row >= CHUNK - 2, pltpu.roll(dycn, CHUNK - 2, 0), pltpu.roll(dyc, CHUNK - 2, 0))
            du = w2 * dyc + w1 * d1 + w0 * d2
            dproj_ref[b, :, OCC:OCC + CW] = (du * ch).astype(BF16)
            dproj_ref[b, :, OCH:OCH + CW] = (du * cc).astype(BF16)
            dcw_ref[0:1, :] += jnp.sum(dyc * u2, axis=0, keepdims=True)
            dcw_ref[1:2, :] += jnp.sum(dyc * u1, axis=0, keepdims=True)
            dcw_ref[2:3, :] += jnp.sum(dyc * u, axis=0, keepdims=True)
            dycn_ref[b] = dyc

        @pl.when(step == nc - 1)
        def _():
            copies = _stage2_copies(pb_refs, r2_refs, send_sems, recv_sems)
            for cp in copies:
                cp.wait_recv()
            for cp in copies:
                cp.wait_send()

    rev = lambda w: pl.BlockSpec((nb, CHUNK, w), lambda i: (0, nc - 1 - i, 0))
    const = lambda r, c: pl.BlockSpec((r, c), lambda i: (0, 0))
    hbm = pl.BlockSpec(memory_space=pl.ANY)
    return pl.pallas_call(
        body,
        name="mix_bwd",
        grid=(nc,),
        in_specs=[
            rev(PW),
            pl.BlockSpec((nb, CHUNK, 2 * CW), lambda i: (0, jnp.maximum(nc - 2 - i, 0), OCC // (2 * CW))),
            rev(NQK),
            pl.BlockSpec((nb, 1, NQK, DV), lambda i: (0, nc - 1 - i, 0, 0)),
            rev(NV),
            rev(D),
            const(1, DV),
            const(CONV_K, CW),
            const(A_PAD, NQK),
        ] + [hbm] * na,
        out_specs=[rev(PW), const(1, DV), const(8, CW), const(1, NQK), const(A_PAD, NQK)] + [hbm] * na,
        out_shape=[
            jax.ShapeDtypeStruct((nb, s, PW), BF16),
            jax.ShapeDtypeStruct((1, DV), F32),
            jax.ShapeDtypeStruct((8, CW), F32),
            jax.ShapeDtypeStruct((1, NQK), F32),
            jax.ShapeDtypeStruct((A_PAD, NQK), F32),
        ] + [jax.ShapeDtypeStruct((3,) + p.shape[1:], BF16) for p in pbs],
        scratch_shapes=[pltpu.VMEM((nb, NQK, DV), F32), pltpu.VMEM((nb, CHUNK, CW), F32),
                        pltpu.SemaphoreType.DMA((3 * na,)), pltpu.SemaphoreType.DMA((3 * na,))],
        compiler_params=_params(("arbitrary",)),
    )(proj3, proj3, z3, sprev, opre3, dmix3, gng, conv_w, wgu_p, *pbs)


SMALL_PACK_ROWS = 32


def _in_proj_bwd(dproj2d, x2d, dx1, g1, w_in_t, tm, pb, small_parts):
    t = x2d.shape[0]
    nt = t // tm

    def body(dp_ref, x_ref, dx1_ref, g_ref, w_ref, pb_ref, dg2, dgf, dbg, dgng, dwgu, dcw, lp,
             dx_ref, sums_ref, r2_ref, dg1_acc, pack, gbuf, pack1, gbuf1, send_sems, recv_sems,
             ssend, srecv, ssend1, srecv1):
        x, y, c = _position()
        me = 4 * x + 2 * y + c
        flips = [(k >> 2, (k >> 1) & 1, k & 1) for k in range(1, N_DEV)]
        peers = [(x ^ fx, y ^ fy, c ^ fc) for fx, fy, fc in flips]

        def small_copies(src, dst, send, recv, arrivals):
            return [pltpu.make_async_remote_copy(
                src_ref=src, dst_ref=dst.at[4 * px + 2 * py + pc if arrivals else me],
                send_sem=send.at[k], recv_sem=recv.at[k], device_id=(px, py, pc), device_id_type=MESH)
                for k, (px, py, pc) in enumerate(peers)]

        @pl.when(pl.program_id(0) == 0)
        def _():
            for cp in _stage2_copies([pb_ref], [r2_ref], send_sems, recv_sems):
                cp.start()
            dg1_acc[...] = jnp.zeros_like(dg1_acc)
            pack[...] = jnp.zeros_like(pack)
            pack[1:2, :] = dg2[...]
            pack[2:3, :] = dgf[...]
            pack[3:4, 0:NQK] = dbg[...]
            pack[3:4, NQK:NQK + DV] = dgng[...]
            pack[3:4, NQK + DV:NQK + 2 * DV] = lp[...]
            pack[8:8 + RANK, 0:NQK] = dwgu[0:RANK, :]
            pack[24:24 + CONV_K, 0:CW] = dcw[0:CONV_K, :]
            for cp in small_copies(pack, gbuf, ssend, srecv, False):
                cp.start()
            gbuf[me] = pack[...]

        xv = x_ref[...]
        r = lax.rsqrt(jnp.mean(xv * xv, axis=-1, keepdims=True) + EPS)
        n1 = xv * r
        dh = _mm(dp_ref[...], w_ref[...])
        dg1_acc[...] += jnp.sum(dh * n1, axis=0, keepdims=True)
        dn = dh * g_ref[...]
        dx_ref[...] = dx1_ref[...] + r * (dn - n1 * jnp.mean(dn * n1, axis=-1, keepdims=True))

        @pl.when(pl.program_id(0) == nt - 1)
        def _():
            pack1[...] = jnp.zeros_like(pack1)
            pack1[0:1, :] = dg1_acc[...]
            for cp in small_copies(pack1, gbuf1, ssend1, srecv1, False):
                cp.start()
            gbuf1[me] = pack1[...]
            copies = _stage2_copies([pb_ref], [r2_ref], send_sems, recv_sems)
            for cp in copies:
                cp.wait_recv()
            for cp in copies:
                cp.wait_send()
            for src, dst, send, recv in ((pack, gbuf, ssend, srecv), (pack1, gbuf1, ssend1, srecv1)):
                for cp in small_copies(src, dst, send, recv, True):
                    cp.wait_recv()
                    cp.wait_send()
            acc = gbuf[0]
            acc1 = gbuf1[0]
            for d in range(1, N_DEV):
                acc = acc + gbuf[d]
                acc1 = acc1 + gbuf1[d]
            sums_ref[...] = acc
            sums_ref[0:1, :] = acc1[0:1, :]

    tile = lambda w: pl.BlockSpec((tm, w), lambda i: (i, 0))
    vec = pl.BlockSpec((1, D), lambda i: (0, 0))
    hbm = pl.BlockSpec(memory_space=pl.ANY)
    whole = lambda a: pl.BlockSpec(a.shape, lambda i: (0,) * a.ndim)
    return pl.pallas_call(
        body,
        name="in_proj_bwd",
        grid=(nt,),
        in_specs=[tile(PW), tile(D), tile(D), vec, pl.BlockSpec((PW, D), lambda i: (0, 0)), hbm]
        + [whole(a) for a in small_parts],
        out_specs=[tile(D), pl.BlockSpec((SMALL_PACK_ROWS, D), lambda i: (0, 0)), hbm],
        out_shape=[jax.ShapeDtypeStruct((t, D), F32), jax.ShapeDtypeStruct((SMALL_PACK_ROWS, D), F32),
                   jax.ShapeDtypeStruct((3,) + pb.shape[1:], BF16)],
        scratch_shapes=[pltpu.VMEM((1, D), F32),
                        pltpu.VMEM((SMALL_PACK_ROWS, D), F32), pltpu.VMEM((N_DEV, SMALL_PACK_ROWS, D), F32),
                        pltpu.VMEM((8, D), F32), pltpu.VMEM((N_DEV, 8, D), F32),
                        pltpu.SemaphoreType.DMA((3,)), pltpu.SemaphoreType.DMA((3,)),
                        pltpu.SemaphoreType.DMA((7,)), pltpu.SemaphoreType.DMA((7,)),
                        pltpu.SemaphoreType.DMA((7,)), pltpu.SemaphoreType.DMA((7,))],
        compiler_params=_params(("arbitrary",)),
    )(dproj2d, x2d, dx1, g1, w_in_t, pb, *small_parts)


def _adamw_math(w, g, m, v):
    m = ADAM_B1 * m + (1.0 - ADAM_B1) * g
    v = ADAM_B2 * v + (1.0 - ADAM_B2) * (g * g)
    m_hat = m / (1.0 - ADAM_B1 ** ADAM_STEP)
    v_hat = v / (1.0 - ADAM_B2 ** ADAM_STEP)
    delta = -ADAM_LR * (m_hat / (jnp.sqrt(v_hat) + ADAM_EPS) + ADAM_WD * w)
    return delta, m, v


def _position():
    return lax.axis_index("x"), lax.axis_index("y"), lax.axis_index("c")


def _prep_slab(w_it, w_gt, w_ut, w_d, w_o):
    def body(wi_ref, wg_ref, wu_ref, wd_ref, wo_ref, stage):
        stage[SLAB_IN:SLAB_IN + IN_W, :] = wi_ref[...].astype(BF16)
        stage[SLAB_IN + IN_W:SLAB_G, :] = jnp.zeros((IN_ROWS - IN_W, D), BF16)
        stage[SLAB_G:SLAB_U, :] = wg_ref[...].astype(BF16)
        stage[SLAB_U:SLAB_D, :] = wu_ref[...].astype(BF16)
        stage[SLAB_D:SLAB_O, :] = wd_ref[...].astype(BF16)
        stage[SLAB_O:SLAB_ROWS, :] = wo_ref[...].astype(BF16)

    vm = pl.BlockSpec(memory_space=pltpu.VMEM)
    return pl.pallas_call(
        body,
        name="prep_slab",
        in_specs=[vm] * 5,
        out_specs=vm,
        out_shape=jax.ShapeDtypeStruct((SLAB_ROWS, D), BF16),
        compiler_params=_params(),
    )(w_it, w_gt, w_ut, w_d, w_o)


GATHER_SEMS = 7


def _gather_copies(stage, lo, rows, gx, send_sems, recv_sems, local_sem):
    x, y, c = _position()
    me = (x, y, c)
    sibling = (x, y, 1 - c)
    chips = [(1 - x, y), (x, 1 - y), (1 - x, 1 - y)]
    src = stage.at[pl.ds(lo, rows), :]

    def blk(px, py, pc):
        return gx.at[4 * px + 2 * py + pc]

    def copy(k, block, to, from_stage=False):
        return pltpu.make_async_remote_copy(
            src_ref=src if from_stage else blk(*block), dst_ref=blk(*block),
            send_sem=send_sems.at[k], recv_sem=recv_sems.at[k], device_id=to, device_id_type=MESH)

    mine = pltpu.make_async_copy(src, blk(*me), local_sem)
    first = [copy(0, me, sibling, True)] + [copy(1 + j, me, (*chip, c), True) for j, chip in enumerate(chips)]
    passed = [copy(4 + j, (*chip, c), sibling) for j, chip in enumerate(chips)]
    arrivals = ([copy(0, sibling, me)] + [copy(1 + j, (*chip, c), me) for j, chip in enumerate(chips)]
                + [copy(4 + j, (*chip, 1 - c), me) for j, chip in enumerate(chips)])
    return mine, first, passed, arrivals


def _gather_start(*args):
    mine, first, _, _ = _gather_copies(*args)
    mine.start()
    for cp in first:
        cp.start()


def _gather_finish(*args):
    mine, first, passed, arrivals = _gather_copies(*args)
    for j in range(3):
        arrivals[1 + j].wait_recv()
        passed[j].start()
    arrivals[0].wait_recv()
    for j in range(3):
        arrivals[4 + j].wait_recv()
    for cp in first + passed:
        cp.wait_send()
    mine.wait()


def _gather_sems():
    return [pltpu.SemaphoreType.DMA((GATHER_SEMS,)), pltpu.SemaphoreType.DMA((GATHER_SEMS,)), pltpu.SemaphoreType.DMA]


def _gather_w_in(stage, wgu_s, conv_s):
    def body(stage_hbm, wgu_ref, conv_ref, w_ref, gwgu_ref, gconv_ref, buf, send_sems, recv_sems, local_sem,
             ssend, srecv):
        x, y, c = _position()
        me = 4 * x + 2 * y + c
        args = (stage_hbm, SLAB_IN, IN_ROWS, buf, send_sems, recv_sems, local_sem)
        _gather_start(*args)
        flips = [(k >> 2, (k >> 1) & 1, k & 1) for k in range(1, N_DEV)]
        peers = [(x ^ fx, y ^ fy, c ^ fc) for fx, fy, fc in flips]

        def small(k, block_id, to):
            return [pltpu.make_async_remote_copy(
                src_ref=s, dst_ref=g.at[block_id], send_sem=ssend.at[2 * k + n], recv_sem=srecv.at[2 * k + n],
                device_id=to, device_id_type=MESH)
                for n, (s, g) in enumerate(((wgu_ref, gwgu_ref), (conv_ref, gconv_ref)))]

        gwgu_ref[me] = wgu_ref[...]
        gconv_ref[me] = conv_ref[...]
        for k, peer in enumerate(peers):
            for cp in small(k, me, peer):
                cp.start()
        w_ref[IN_COLS:PW, :] = jnp.zeros((PW - IN_COLS, D), BF16)
        _gather_finish(*args)
        for k, (px, py, pc) in enumerate(peers):
            for cp in small(k, 4 * px + 2 * py + pc, (px, py, pc)):
                cp.wait_recv()
                cp.wait_send()
        for j, lo, hi, d in _in_segments():
            w_ref[d:d + hi - lo, :] = buf[j, lo:hi, :]

    vm = pl.BlockSpec(memory_space=pltpu.VMEM)
    hbm = pl.BlockSpec(memory_space=pl.ANY)
    return pl.pallas_call(
        body,
        name="gather_w_in",
        in_specs=[hbm, vm, vm],
        out_specs=[vm, vm, vm],
        out_shape=[jax.ShapeDtypeStruct((PW, D), BF16),
                   jax.ShapeDtypeStruct((N_DEV,) + wgu_s.shape, F32),
                   jax.ShapeDtypeStruct((N_DEV,) + conv_s.shape, F32)],
        scratch_shapes=[pltpu.VMEM((N_DEV, IN_ROWS, D), BF16)] + _gather_sems()
        + [pltpu.SemaphoreType.DMA((14,)), pltpu.SemaphoreType.DMA((14,))],
        compiler_params=_params(),
    )(stage, wgu_s, conv_s)


def _unpermute_dw_in(dw_t):
    def body(d_ref, g_ref, gb_ref):
        for j in range(N_DEV):
            g_ref[j, IN_W:IN_ROWS, :] = jnp.zeros((IN_ROWS - IN_W, D), F32)
        for j, lo, hi, d in _in_segments():
            g_ref[j, lo:hi, :] = d_ref[d:d + hi - lo, :]
        for j in range(N_DEV):
            gb_ref[j] = g_ref[j].astype(BF16)

    vm = pl.BlockSpec(memory_space=pltpu.VMEM)
    return pl.pallas_call(
        body,
        name="unpermute_dw_in",
        in_specs=[vm],
        out_specs=[vm, vm],
        out_shape=[jax.ShapeDtypeStruct((N_DEV, IN_ROWS, D), F32), jax.ShapeDtypeStruct((N_DEV, IN_ROWS, D), BF16)],
        compiler_params=_params(),
    )(dw_t)


def _stage1_copies(g_refs, leads, r_refs, send_sems, recv_sems):
    x, y, c = _position()
    return [pltpu.make_async_remote_copy(
        src_ref=_at_owner(g_refs[a], leads[a], 2 * i + 1 - c), dst_ref=r_refs[a].at[i],
        send_sem=send_sems.at[4 * a + i], recv_sem=recv_sems.at[4 * a + i],
        device_id=(x, y, 1 - c), device_id_type=MESH) for a in range(len(g_refs)) for i in range(4)]


def _exchange_core(gbs, name):
    n = len(gbs)

    def body(*refs):
        copies = _stage1_copies(refs[:n], [None] * n, refs[n:2 * n], refs[2 * n], refs[2 * n + 1])
        for cp in copies:
            cp.start()
        for cp in copies:
            cp.wait_recv()
        for cp in copies:
            cp.wait_send()

    hbm = pl.BlockSpec(memory_space=pl.ANY)
    return pl.pallas_call(
        body,
        name=name,
        in_specs=[hbm] * n,
        out_specs=[hbm] * n,
        out_shape=[jax.ShapeDtypeStruct((4,) + g.shape[1:], BF16) for g in gbs],
        scratch_shapes=[pltpu.SemaphoreType.DMA((4 * n,)), pltpu.SemaphoreType.DMA((4 * n,))],
        compiler_params=_params(),
    )(*gbs)


def _add_core(g8, r1, pos_arr, name, lead=None):
    rows = g8.shape[-2]
    if lead is None:
        g_spec = pl.BlockSpec((1, rows, D), lambda k, pos: (2 * (pos[1] ^ (k + 1)) + pos[0], 0, 0))
    else:
        g_spec = pl.BlockSpec((None, 1, rows, D), lambda k, pos: (lead, 2 * (pos[1] ^ (k + 1)) + pos[0], 0, 0))

    def body(pos_ref, g_ref, r_ref, pb_ref):
        pb_ref[...] = (g_ref[...] + r_ref[...].astype(F32)).astype(BF16)

    return pl.pallas_call(
        body,
        name=name,
        grid_spec=pltpu.PrefetchScalarGridSpec(
            num_scalar_prefetch=1, grid=(3,),
            in_specs=[g_spec,
                      pl.BlockSpec((1, rows, D), lambda k, pos: (pos[1] ^ (k + 1), 0, 0))],
            out_specs=pl.BlockSpec((1, rows, D), lambda k, pos: (k, 0, 0))),
        out_shape=jax.ShapeDtypeStruct((3, rows, D), BF16),
        compiler_params=_params(("arbitrary",)),
    )(pos_arr, g8, r1)


def _stage2_copies(p_refs, r_refs, send_sems, recv_sems):
    x, y, c = _position()
    copies = []
    for a in range(len(p_refs)):
        for k in range(1, 4):
            copies.append(pltpu.make_async_remote_copy(
                src_ref=p_refs[a].at[k - 1], dst_ref=r_refs[a].at[k - 1],
                send_sem=send_sems.at[3 * a + k - 1], recv_sem=recv_sems.at[3 * a + k - 1],
                device_id=(x ^ (k >> 1), y ^ (k & 1), c), device_id_type=MESH))
    return copies


def _finish_weight(g8, r1, r2, w, m, v, pos_arr, name, lead=None):
    rows = g8.shape[-2]
    wr = w.shape[0]
    nblk = 2 if (rows == wr and rows % 32 == 0) else 1
    rb = rows // nblk
    wb = wr // nblk
    if lead is None:
        g_spec = pl.BlockSpec((1, rb, D), lambda i, pos: (2 * pos[1] + pos[0], i, 0))
    else:
        g_spec = pl.BlockSpec((None, 1, rb, D), lambda i, pos: (lead, 2 * pos[1] + pos[0], i, 0))

    def body(pos_ref, g_ref, r1_ref, r2_ref, w_ref, m_ref, v_ref, g_out, d_out, m_out, v_out):
        g = g_ref[0] + r1_ref[0].astype(F32)
        for k in range(3):
            g = g + r2_ref[k].astype(F32)
        g = g[0:wb, :]
        g_out[...] = g
        d, mn, vn = _adamw_math(w_ref[...], g, m_ref[...], v_ref[...])
        d_out[...] = d
        m_out[...] = mn
        v_out[...] = vn

    wblk = pl.BlockSpec((wb, D), lambda i, pos: (i, 0))
    shp = jax.ShapeDtypeStruct(w.shape, F32)
    return pl.pallas_call(
        body,
        name=name,
        grid_spec=pltpu.PrefetchScalarGridSpec(
            num_scalar_prefetch=1, grid=(nblk,),
            in_specs=[g_spec,
                      pl.BlockSpec((1, rb, D), lambda i, pos: (pos[1], i, 0)),
                      pl.BlockSpec((3, rb, D), lambda i, pos: (0, i, 0)), wblk, wblk, wblk],
            out_specs=[wblk] * 4),
        out_shape=[shp, shp, shp, shp],
        compiler_params=_params(("arbitrary",)),
    )(pos_arr, g8, r1, r2, w, m, v)


SMALL_NAMES = ("norm1_g", "norm2_g", "norm_f_g", "b_gate", "gla_norm_g", "w_gate_up", "conv_w")
WGU_W = NQK // N_DEV
CONV_W = CW // N_DEV


def _small_adamw(sums, ws, ms, vs):
    n = len(SMALL_NAMES)

    def body(*refs):
        acc_ref = refs[0]
        w_refs, m_refs, v_refs = refs[1:1 + n], refs[1 + n:1 + 2 * n], refs[1 + 2 * n:1 + 3 * n]
        loss_ref = refs[1 + 3 * n]
        outs = refs[2 + 3 * n:]
        x, y, c = _position()
        me = 4 * x + 2 * y + c
        acc = acc_ref[...]
        loss_ref[...] = acc[3:4, NQK + DV:NQK + DV + 1]

        def my_columns(full, width):
            r = lax.broadcasted_iota(jnp.int32, (full.shape[1], width), 0)
            col = lax.broadcasted_iota(jnp.int32, (full.shape[1], width), 1)
            sel = (r == width * me + col).astype(F32)
            return _mm(full, sel, precision=HIGHEST)

        grads = [acc[0:1, :], acc[1:2, :], acc[2:3, :], acc[3:4, 0:NQK], acc[3:4, NQK:NQK + DV],
                 my_columns(acc[8:8 + RANK, 0:NQK], WGU_W), my_columns(acc[24:24 + CONV_K, 0:CW], CONV_W)]
        for i, g in enumerate(grads):
            d, mn, vn = _adamw_math(w_refs[i][...], g, m_refs[i][...], v_refs[i][...])
            outs[4 * i][...] = g
            outs[4 * i + 1][...] = d
            outs[4 * i + 2][...] = mn
            outs[4 * i + 3][...] = vn

    vm = pl.BlockSpec(memory_space=pltpu.VMEM)
    out_shape = [jax.ShapeDtypeStruct((1, 1), F32)]
    for w in ws:
        out_shape += [jax.ShapeDtypeStruct(w.shape, F32)] * 4
    return pl.pallas_call(
        body,
        name="small_adamw",
        in_specs=[vm] * (1 + 3 * n),
        out_specs=[vm] * (1 + 4 * n),
        out_shape=out_shape,
        compiler_params=_params(),
    )(sums, *ws, *ms, *vs)


def kernel(x, norm1_g, w_in, w_gate_up, b_gate, gla_norm_g, conv_w, w_out, norm2_g, w_ffn_gate, w_ffn_up, w_ffn_down, norm_f_g, loss_target, m_norm1_g, m_w_in, m_w_gate_up, m_b_gate, m_gla_norm_g, m_conv_w, m_w_out, m_norm2_g, m_w_ffn_gate, m_w_ffn_up, m_w_ffn_down, m_norm_f_g, v_norm1_g, v_w_in, v_w_gate_up, v_b_gate, v_gla_norm_g, v_conv_w, v_w_out, v_norm2_g, v_w_ffn_gate, v_w_ffn_up, v_w_ffn_down, v_norm_f_g):
    xi, yi, ci = _position()
    pos_arr = jnp.stack([ci, 2 * xi + yi]).astype(jnp.int32)
    nb, s, _ = x.shape
    t = nb * s

    tr = lambda a: a[0].T
    stage = _prep_slab(tr(w_in), tr(w_ffn_gate), tr(w_ffn_up), w_ffn_down[0], w_out[0])
    w_in_t, gwgu, gconv = _gather_w_in(stage, w_gate_up[0], conv_w[0])
    wgu_f = gwgu.transpose(1, 0, 2).reshape(RANK, NQK)
    conv_f = gconv.transpose(1, 0, 2).reshape(CONV_K, CW)
    wgu_p = jnp.concatenate([wgu_f, jnp.zeros((A_PAD - RANK, NQK), F32)], axis=0).astype(BF16)

    x2d = x.reshape(t, D)
    tgt2d = loss_target.reshape(t, D)
    tm = 256
    tm_in = min(512, t)
    tk = min(2048, t)
    proj, z, h, gwb = _in_proj_fwd(x2d, norm1_g, w_in_t, wgu_p, b_gate, tm_in, stage)
    proj3 = proj.reshape(nb, s, PW)
    z3 = z.reshape(nb, s, NQK)
    mix3, opre3, sprev, gwa = _mix_fwd(proj3, z3, gla_norm_g, conv_f, stage)
    mix2d = mix3.reshape(t, D)
    dx1, dx1b, dmix, adu, hb, dg2, dgf, loss_part = _ffn_fwd_bwd(
        mix2d, x2d, tgt2d, gwa, gwb, norm2_g, norm_f_g.reshape(1, D), tm)
    dw3, dwb3 = _dw_ffn(adu, hb, tk)
    dw3 = dw3.reshape(3, N_DEV, FF_W, D)
    dw_o, dwb_o, *r1_ffn = _tn_matmul(mix2d, dx1b, D // 4, D, tk, "dw_out", True,
                                      _stage1_rider(dwb3.reshape(3, N_DEV, FF_W, D)))
    g8 = [dw3, dw3, dw3, dw_o.reshape(N_DEV, OUT_ROWS, D)]
    leads = [0, 1, 2, None]
    tags = ("w_ffn_down", "w_ffn_gate", "w_ffn_up", "w_out")
    r1 = list(r1_ffn) + list(_exchange_core([dwb_o.reshape(N_DEV, OUT_ROWS, D)], "grad_exchange_core_out"))
    pb = [_add_core(g, r, pos_arr, "grad_add_core_" + tag, lead)
          for g, r, tag, lead in zip(g8, r1, tags, leads)]
    mb = _mix_bwd(proj3, z3, sprev, opre3, dmix.reshape(nb, s, D), gla_norm_g, conv_f, wgu_p, [pb[0], pb[1], pb[3]])
    dproj3, dgng, dcw, dbg, dwgu = mb[:5]
    dproj2d = dproj3.reshape(t, PW)
    dw_in_t, r2_up = _tn_matmul(dproj2d, h, PW // 5, D, tk, "dw_in", False, _stage2_rider([pb[2]]))
    r2 = [mb[5], mb[6], r2_up, mb[7]]
    g_in, gb_in = _unpermute_dw_in(dw_in_t)
    (r1_in,) = _exchange_core([gb_in], "grad_exchange_core_in")
    pb_in = _add_core(g_in, r1_in, pos_arr, "grad_add_core_w_in")
    dx, small_sums, r2_in = _in_proj_bwd(dproj2d, x2d, dx1, norm1_g, w_in_t, tm_in, pb_in,
                                         (dg2, dgf, dbg, dgng, dwgu, dcw, loss_part))

    tags = ("w_in",) + tags
    g8 = [g_in] + g8
    leads = [None] + leads
    r1 = [r1_in] + list(r1)
    r2 = [r2_in] + r2
    shard_w = (tr(w_in), w_ffn_down[0], tr(w_ffn_gate), tr(w_ffn_up), w_out[0])
    shard_m = (tr(m_w_in), m_w_ffn_down[0], tr(m_w_ffn_gate), tr(m_w_ffn_up), m_w_out[0])
    shard_v = (tr(v_w_in), v_w_ffn_down[0], tr(v_w_ffn_gate), tr(v_w_ffn_up), v_w_out[0])
    transposed = (True, False, True, True, False)
    results = {}
    for tag, g, lead, ra, rb, w, m, v, tp in zip(tags, g8, leads, r1, r2, shard_w, shard_m, shard_v, transposed):
        outs = _finish_weight(g, ra, rb, w, m, v, pos_arr, "finish_" + tag, lead)
        results[tag] = [o.T[None] if tp else o[None] for o in outs]

    small_w = (norm1_g, norm2_g, norm_f_g.reshape(1, D), b_gate, gla_norm_g, w_gate_up[0], conv_w[0])
    small_m = (m_norm1_g, m_norm2_g, m_norm_f_g.reshape(1, D), m_b_gate, m_gla_norm_g, m_w_gate_up[0], m_conv_w[0])
    small_v = (v_norm1_g, v_norm2_g, v_norm_f_g.reshape(1, D), v_b_gate, v_gla_norm_g, v_w_gate_up[0], v_conv_w[0])
    so = _small_adamw(small_sums, small_w, small_m, small_v)
    loss = so[0].reshape(())
    shapes = {"norm_f_g": (D,), "w_gate_up": (1, RANK, WGU_W), "conv_w": (1, CONV_K, CONV_W)}
    for i, name in enumerate(SMALL_NAMES):
        results[name] = [o.reshape(shapes[name]) if name in shapes else o for o in so[1 + 4 * i:5 + 4 * i]]

    names = ("norm1_g", "w_in", "w_gate_up", "b_gate", "gla_norm_g", "conv_w", "w_out", "norm2_g",
             "w_ffn_gate", "w_ffn_up", "w_ffn_down", "norm_f_g")
    outs = [loss, dx.reshape(nb, s, D)]
    for kind in range(4):
        for name in names:
            outs.append(results[name][kind])
    return tuple(outs)
```

```python
import jax
import jax.numpy as jnp
from jax import lax
from jax.experimental import pallas as pl
from jax.experimental.pallas import tpu as pltpu

F32 = jnp.float32
BF16 = jnp.bfloat16
HIGHEST = lax.Precision.HIGHEST
MESH = pl.DeviceIdType.MESH

N_DEV = 8
D = 1024
DFF = 2816
HEADS = 4
DK = 64
DV = 128
NQK = HEADS * DK
NV = HEADS * DV
RANK = 16
CHUNK = 64
CW = 512
CONV_K = 3
IN_COLS = 3088
EPS = 1e-6
INV_GATE_NORM = 1.0 / 16.0
Q_SCALE = DK ** -0.5

PW = 3200
OQ, OK_, OV, OG, OCB, OCC, OCH, OA = 0, 256, 512, 1024, 1536, 2048, 2560, 3072
A_PAD = 128

ADAM_LR = 0.001
ADAM_B1 = 0.9
ADAM_B2 = 0.999
ADAM_EPS = 1e-08
ADAM_WD = 0.01
ADAM_STEP = 10

IN_W = IN_COLS // N_DEV
IN_ROWS = 400
FF_W = DFF // N_DEV
OUT_ROWS = D // N_DEV
SLAB_IN = 0
SLAB_G = SLAB_IN + IN_ROWS
SLAB_U = SLAB_G + FF_W
SLAB_D = SLAB_U + FF_W
SLAB_O = SLAB_D + FF_W
SLAB_ROWS = SLAB_O + OUT_ROWS

VMEM_LIMIT = 56 * 1024 * 1024


def _params(sem=None, vmem=VMEM_LIMIT):
    return pltpu.CompilerParams(dimension_semantics=sem, vmem_limit_bytes=vmem)


def _nt(a, b):
    return lax.dot_general(a, b, (((1,), (1,)), ((), ())), preferred_element_type=F32)


def _tn(a, b, precision=None):
    return lax.dot_general(a, b, (((0,), (0,)), ((), ())), preferred_element_type=F32, precision=precision)


def _mm(a, b, precision=None):
    return jnp.dot(a, b, preferred_element_type=F32, precision=precision)


def _in_segments():
    segs = []
    for j in range(N_DEV):
        lo, hi = IN_W * j, IN_W * (j + 1)
        cuts = sorted({lo, hi} | {c for c in (OCB, OCB + RANK) if lo < c < hi})
        for a, b in zip(cuts[:-1], cuts[1:]):
            if a < OCB:
                d = a
            elif a < OCB + RANK:
                d = OA + (a - OCB)
            else:
                d = a - RANK
            segs.append((j, a - lo, b - lo, d))
    return segs


def _in_proj_fwd(x2d, g1, w_in_t, wgu_p, b_gate, tm, stage):
    t = x2d.shape[0]
    nt = t // tm
    g_rows = SLAB_ROWS - SLAB_D

    def body(x_ref, g_ref, w_ref, wgu_ref, bg_ref, stage_hbm, proj_ref, z_ref, h_ref, gwb_ref,
             send_sems, recv_sems, local_sem):
        gargs = (stage_hbm, SLAB_D, g_rows, gwb_ref, send_sems, recv_sems, local_sem)

        @pl.when(pl.program_id(0) == 0)
        def _():
            _gather_start(*gargs)

        x = x_ref[...]
        r = lax.rsqrt(jnp.mean(x * x, axis=-1, keepdims=True) + EPS)
        h = ((x * r) * g_ref[...]).astype(BF16)
        h_ref[...] = h
        proj = _nt(h, w_ref[...])
        proj_ref[...] = proj
        pa = proj[:, OA:OA + A_PAD].astype(BF16)
        z_ref[...] = _mm(pa, wgu_ref[...]) + bg_ref[...]

        @pl.when(pl.program_id(0) == nt - 1)
        def _():
            _gather_finish(*gargs)

    return pl.pallas_call(
        body,
        name="in_proj_fwd",
        grid=(t // tm,),
        in_specs=[
            pl.BlockSpec((tm, D), lambda i: (i, 0)),
            pl.BlockSpec((1, D), lambda i: (0, 0)),
            pl.BlockSpec((PW, D), lambda i: (0, 0)),
            pl.BlockSpec((A_PAD, NQK), lambda i: (0, 0)),
            pl.BlockSpec((1, NQK), lambda i: (0, 0)),
            pl.BlockSpec(memory_space=pl.ANY),
        ],
        out_specs=[
            pl.BlockSpec((tm, PW), lambda i: (i, 0)),
            pl.BlockSpec((tm, NQK), lambda i: (i, 0)),
            pl.BlockSpec((tm, D), lambda i: (i, 0)),
            pl.BlockSpec(memory_space=pl.ANY),
        ],
        out_shape=[
            jax.ShapeDtypeStruct((t, PW), F32),
            jax.ShapeDtypeStruct((t, NQK), F32),
            jax.ShapeDtypeStruct((t, D), BF16),
            jax.ShapeDtypeStruct((N_DEV, g_rows, D), BF16),
        ],
        scratch_shapes=_gather_sems(),
        compiler_params=_params(("arbitrary",)),
    )(x2d, g1, w_in_t, wgu_p, b_gate, stage)


def _head_masks():
    lane = lax.broadcasted_iota(jnp.int32, (1, NQK), 1)
    return [(lane >= DK * h) & (lane < DK * (h + 1)) for h in range(HEADS)]


def _split_bf16(x, n):
    parts = []
    for _ in range(n):
        p = x.astype(BF16)
        parts.append(p)
        x = x - p.astype(F32)
    return parts


def _chunk_fwd_parts(q, k, z, tril16):
    la = (jnp.minimum(z, 0.0) - jnp.log1p(jnp.exp(-jnp.abs(z)))) * INV_GATE_NORM
    la_parts = _split_bf16(la, 3)
    bc = _mm(tril16, la_parts[0]) + _mm(tril16, la_parts[1]) + _mm(tril16, la_parts[2])
    bl = bc[CHUNK - 1:CHUNK, :]
    eb = jnp.exp(bc)
    enb = jnp.exp(-bc)
    ekl = jnp.exp(bl - bc)
    qi = (q * Q_SCALE) * eb
    ki = k * enb
    ks = k * ekl
    ones16 = jnp.ones((CHUNK, DV), BF16)
    decb = jnp.exp(_tn(la_parts[0], ones16) + _tn(la_parts[1], ones16) + _tn(la_parts[2], ones16))
    return la, eb, enb, ekl, qi, ki, ks, decb


def _stack_heads(a, masks):
    return jnp.concatenate([jnp.where(m, a, 0.0) for m in masks], axis=0)


def _merge_heads(blocks, masks):
    out = blocks[HEADS - 1]
    for h in range(HEADS - 2, -1, -1):
        out = jnp.where(masks[h], blocks[h], out)
    return out


def _causal_stack_mask():
    row = lax.broadcasted_iota(jnp.int32, (HEADS * CHUNK, CHUNK), 0)
    col = lax.broadcasted_iota(jnp.int32, (HEADS * CHUNK, CHUNK), 1)
    return (row & (CHUNK - 1)) >= col


def _conv_taps(u, uprev):
    row = lax.broadcasted_iota(jnp.int32, u.shape, 0)
    u1 = jnp.where(row < 1, pltpu.roll(uprev, 1, 0), pltpu.roll(u, 1, 0))
    u2 = jnp.where(row < 2, pltpu.roll(uprev, 2, 0), pltpu.roll(u, 2, 0))
    return u1, u2


def _mix_fwd(proj3, z3, gng, conv_w, stage):
    nb, s, _ = proj3.shape
    nc = s // CHUNK
    g_rows = SLAB_D - SLAB_G

    def body(p_ref, z_ref, gng_ref, cw_ref, stage_hbm, mix_ref, o_ref, sprev_ref, gwa_ref, s_ref, uprev_ref,
             send_sems, recv_sems, local_sem):
        n = pl.program_id(0)
        gargs = (stage_hbm, SLAB_G, g_rows, gwa_ref, send_sems, recv_sems, local_sem)

        @pl.when(n == 0)
        def _():
            _gather_start(*gargs)
            s_ref[...] = jnp.zeros_like(s_ref)
            uprev_ref[...] = jnp.zeros_like(uprev_ref)

        r_i = lax.broadcasted_iota(jnp.int32, (CHUNK, CHUNK), 0)
        c_i = lax.broadcasted_iota(jnp.int32, (CHUNK, CHUNK), 1)
        tril16 = (r_i >= c_i).astype(BF16)
        masks = _head_masks()
        cmask = _causal_stack_mask()
        gg = gng_ref[...]
        for b in range(nb):
            q = p_ref[b, :, OQ:OQ + NQK]
            k = p_ref[b, :, OK_:OK_ + NQK]
            _, _, _, _, qi, ki, ks, decb = _chunk_fwd_parts(q, k, z_ref[b], tril16)
            qs = _stack_heads(qi, masks).astype(BF16)
            sc = jnp.where(cmask, _nt(qs, ki.astype(BF16)), 0.0).astype(BF16)
            st = s_ref[b]
            sprev_ref[b, 0] = st
            o_inter = _mm(qs, st.astype(BF16))
            v16 = p_ref[b, :, OV:OV + NV].astype(BF16)
            kv = _tn(ks.astype(BF16), v16)
            for h in range(HEADS):
                rows = slice(CHUNK * h, CHUNK * (h + 1))
                cols = slice(DV * h, DV * (h + 1))
                o = _mm(sc[rows], v16[:, cols]) + o_inter[rows]
                o_ref[b, :, cols] = o
                r = lax.rsqrt(jnp.mean(o * o, axis=-1, keepdims=True) + EPS)
                on = (o * r) * gg
                g = p_ref[b, :, OG + DV * h:OG + DV * (h + 1)]
                mix_ref[b, :, cols] = (on * (g * jax.nn.sigmoid(g))).astype(BF16)
                s_ref[b, rows, :] = decb[rows] * st[rows] + kv[rows, cols]
            u = p_ref[b, :, OCC:OCC + CW] * p_ref[b, :, OCH:OCH + CW]
            u1, u2 = _conv_taps(u, uprev_ref[b])
            yc = cw_ref[0:1, :] * u2 + cw_ref[1:2, :] * u1 + cw_ref[2:3, :] * u
            mix_ref[b, :, NV:NV + CW] = (p_ref[b, :, OCB:OCB + CW] * yc).astype(BF16)
            uprev_ref[b] = u

        @pl.when(n == nc - 1)
        def _():
            _gather_finish(*gargs)

    return pl.pallas_call(
        body,
        name="mix_fwd",
        grid=(nc,),
        in_specs=[
            pl.BlockSpec((nb, CHUNK, PW), lambda n: (0, n, 0)),
            pl.BlockSpec((nb, CHUNK, NQK), lambda n: (0, n, 0)),
            pl.BlockSpec((1, DV), lambda n: (0, 0)),
            pl.BlockSpec((CONV_K, CW), lambda n: (0, 0)),
            pl.BlockSpec(memory_space=pl.ANY),
        ],
        out_specs=[
            pl.BlockSpec((nb, CHUNK, D), lambda n: (0, n, 0)),
            pl.BlockSpec((nb, CHUNK, NV), lambda n: (0, n, 0)),
            pl.BlockSpec((nb, 1, NQK, DV), lambda n: (0, n, 0, 0)),
            pl.BlockSpec(memory_space=pl.ANY),
        ],
        out_shape=[
            jax.ShapeDtypeStruct((nb, s, D), BF16),
            jax.ShapeDtypeStruct((nb, s, NV), F32),
            jax.ShapeDtypeStruct((nb, nc, NQK, DV), F32),
            jax.ShapeDtypeStruct((N_DEV, g_rows, D), BF16),
        ],
        scratch_shapes=[pltpu.VMEM((nb, NQK, DV), F32), pltpu.VMEM((nb, CHUNK, CW), F32)] + _gather_sems(),
        compiler_params=_params(("arbitrary",)),
    )(proj3, z3, gng, conv_w, stage)


def _ffn_fwd_bwd(mix2d, x2d, tgt2d, gwa, gwb, g2, gf, tm):
    t = x2d.shape[0]

    def body(mix_ref, x_ref, tgt_ref, g2_ref, gf_ref, gwa_hbm, gwb_hbm,
             dx1_ref, dx1b_ref, dmix_ref, adu_ref, hb_ref, dg2_ref, dgf_ref, loss_ref,
             wo, wg, wu, wd, wsem):
        i = pl.program_id(0)

        def weight_copies(n, dst, src, off, rows):
            return [pltpu.make_async_copy(src.at[j, pl.ds(off, rows), :], dst.at[pl.ds(rows * j, rows), :],
                                          wsem.at[N_DEV * n + j]) for j in range(N_DEV)]

        loads = (weight_copies(0, wo, gwb_hbm, FF_W, OUT_ROWS), weight_copies(1, wg, gwa_hbm, 0, FF_W),
                 weight_copies(2, wu, gwa_hbm, FF_W, FF_W), weight_copies(3, wd, gwb_hbm, 0, FF_W))

        @pl.when(i == 0)
        def _():
            for group in loads:
                for cp in group:
                    cp.start()
            dg2_ref[...] = jnp.zeros_like(dg2_ref)
            dgf_ref[...] = jnp.zeros_like(dgf_ref)
            loss_ref[...] = jnp.zeros_like(loss_ref)
            for group in loads:
                for cp in group:
                    cp.wait()

        g2v = g2_ref[...]
        gfv = gf_ref[...]
        x1 = x_ref[...] + _mm(mix_ref[...], wo[...])
        r2 = lax.rsqrt(jnp.mean(x1 * x1, axis=-1, keepdims=True) + EPS)
        n2 = x1 * r2
        h2 = (n2 * g2v).astype(BF16)
        hb_ref[1] = h2
        gate = _nt(h2, wg[...])
        up = _nt(h2, wu[...])
        sg = jax.nn.sigmoid(gate)
        sil = gate * sg
        act = (sil * up).astype(BF16)
        adu_ref[0] = act
        x2 = x1 + _mm(act, wd[...])
        rf = lax.rsqrt(jnp.mean(x2 * x2, axis=-1, keepdims=True) + EPS)
        nf = x2 * rf
        err = nf * gfv - tgt_ref[...]
        loss_ref[...] += 0.5 * jnp.sum(jnp.mean(err * err, axis=-1, keepdims=True))
        dy = err * (1.0 / D)
        dgf_ref[...] += jnp.sum(dy * nf, axis=0, keepdims=True)
        dnf = dy * gfv
        dx2 = rf * (dnf - nf * jnp.mean(dnf * nf, axis=-1, keepdims=True))
        dx2b = dx2.astype(BF16)
        hb_ref[0] = dx2b
        dact = _nt(dx2b, wd[...])
        dup = (dact * sil).astype(BF16)
        dgate = ((dact * up) * (sg * (1.0 + gate * (1.0 - sg)))).astype(BF16)
        adu_ref[2] = dup
        adu_ref[1] = dgate
        dh2 = _mm(dgate, wg[...]) + _mm(dup, wu[...])
        dg2_ref[...] += jnp.sum(dh2 * n2, axis=0, keepdims=True)
        dn2 = dh2 * g2v
        dx1 = dx2 + r2 * (dn2 - n2 * jnp.mean(dn2 * n2, axis=-1, keepdims=True))
        dx1_ref[...] = dx1
        dx1b = dx1.astype(BF16)
        dx1b_ref[...] = dx1b
        dmix_ref[...] = _nt(dx1b, wo[...])

    tile = lambda w: pl.BlockSpec((tm, w), lambda i: (i, 0))
    vec = pl.BlockSpec((1, D), lambda i: (0, 0))
    hbm = pl.BlockSpec(memory_space=pl.ANY)
    return pl.pallas_call(
        body,
        name="ffn_fwd_bwd",
        grid=(t // tm,),
        in_specs=[tile(D), tile(D), tile(D), vec, vec, hbm, hbm],
        out_specs=[tile(D), tile(D), tile(D), pl.BlockSpec((3, tm, DFF), lambda i: (0, i, 0)),
                   pl.BlockSpec((2, tm, D), lambda i: (0, i, 0)), vec, vec,
                   pl.BlockSpec((1, 128), lambda i: (0, 0))],
        out_shape=[
            jax.ShapeDtypeStruct((t, D), F32),
            jax.ShapeDtypeStruct((t, D), BF16),
            jax.ShapeDtypeStruct((t, D), F32),
            jax.ShapeDtypeStruct((3, t, DFF), BF16),
            jax.ShapeDtypeStruct((2, t, D), BF16),
            jax.ShapeDtypeStruct((1, D), F32),
            jax.ShapeDtypeStruct((1, D), F32),
            jax.ShapeDtypeStruct((1, 128), F32),
        ],
        scratch_shapes=[pltpu.VMEM((D, D), BF16), pltpu.VMEM((DFF, D), BF16), pltpu.VMEM((DFF, D), BF16),
                        pltpu.VMEM((DFF, D), BF16), pltpu.SemaphoreType.DMA((4 * N_DEV,))],
        compiler_params=_params(("arbitrary",)),
    )(mix2d, x2d, tgt2d, g2, gf, gwa, gwb)


def _stage1_rider(stack):
    n = stack.shape[0]
    return dict(inputs=[stack], out_shape=[jax.ShapeDtypeStruct((4,) + stack.shape[2:], BF16)] * n, nsem=4 * n,
                copies=lambda ins, outs, send, recv: _stage1_copies([ins[0]] * n, list(range(n)), outs, send, recv))


def _stage2_rider(pbs):
    return dict(inputs=list(pbs), out_shape=[jax.ShapeDtypeStruct(p.shape, BF16) for p in pbs], nsem=3 * len(pbs),
                copies=_stage2_copies)


def _tn_matmul(a, b, bm, bn, tk, name, with_bf16, rider=None):
    t, m = a.shape
    n = b.shape[1]
    nk = t // tk
    nout = 2 if with_bf16 else 1
    grid = (m // bm, n // bn, nk)
    r_in = [] if rider is None else rider["inputs"]
    r_out = [] if rider is None else rider["out_shape"]

    def body(a_ref, b_ref, *rest):
        ins, outs = rest[:len(r_in)], rest[len(r_in):len(r_in) + nout]
        r_outs, sems = rest[len(r_in) + nout:len(r_in) + nout + len(r_out)], rest[len(r_in) + nout + len(r_out):]
        o_ref = outs[0]
        i, j, k = pl.program_id(0), pl.program_id(1), pl.program_id(2)
        if rider is not None:
            @pl.when((i == 0) & (j == 0) & (k == 0))
            def _():
                for cp in rider["copies"](ins, r_outs, *sems):
                    cp.start()

        @pl.when(k == 0)
        def _():
            o_ref[...] = jnp.zeros_like(o_ref)

        o_ref[...] += _tn(a_ref[...].astype(BF16), b_ref[...].astype(BF16))
        if with_bf16:
            @pl.when(k == nk - 1)
            def _():
                outs[1][...] = o_ref[...].astype(BF16)
        if rider is not None:
            @pl.when((i == grid[0] - 1) & (j == grid[1] - 1) & (k == nk - 1))
            def _():
                copies = rider["copies"](ins, r_outs, *sems)
                for cp in copies:
                    cp.wait_recv()
                for cp in copies:
                    cp.wait_send()

    out_blk = pl.BlockSpec((bm, bn), lambda i, j, k: (i, j))
    hbm = pl.BlockSpec(memory_space=pl.ANY)
    out_shape = [jax.ShapeDtypeStruct((m, n), F32)] + ([jax.ShapeDtypeStruct((m, n), BF16)] if with_bf16 else [])
    res = pl.pallas_call(
        body,
        name=name,
        grid=grid,
        in_specs=[pl.BlockSpec((tk, bm), lambda i, j, k: (k, i)), pl.BlockSpec((tk, bn), lambda i, j, k: (k, j))]
        + [hbm] * len(r_in),
        out_specs=[out_blk] * nout + [hbm] * len(r_out),
        out_shape=out_shape + list(r_out),
        scratch_shapes=([] if rider is None else
                        [pltpu.SemaphoreType.DMA((rider["nsem"],)), pltpu.SemaphoreType.DMA((rider["nsem"],))]),
        compiler_params=_params(("parallel", "parallel", "arbitrary") if rider is None
                                else ("arbitrary", "arbitrary", "arbitrary")),
    )(a, b, *r_in)
    return res[0] if len(res) == 1 else res


def _dw_ffn(adu, hb, tk):
    _, t, _ = adu.shape
    bm = DFF // 2
    nk = t // tk

    def body(a_ref, b_ref, o_ref, ob_ref):
        k = pl.program_id(2)

        @pl.when(k == 0)
        def _():
            o_ref[...] = jnp.zeros_like(o_ref)

        o_ref[...] += _tn(a_ref[...], b_ref[...])

        @pl.when(k == nk - 1)
        def _():
            ob_ref[...] = o_ref[...].astype(BF16)

    out_blk = pl.BlockSpec((None, bm, D), lambda p, i, k: (p, i, 0))
    return pl.pallas_call(
        body,
        name="dw_ffn",
        grid=(3, DFF // bm, nk),
        in_specs=[pl.BlockSpec((None, tk, bm), lambda p, i, k: (p, k, i)),
                  pl.BlockSpec((None, tk, D), lambda p, i, k: (jnp.minimum(p, 1), k, 0))],
        out_specs=[out_blk, out_blk],
        out_shape=[jax.ShapeDtypeStruct((3, DFF, D), F32), jax.ShapeDtypeStruct((3, DFF, D), BF16)],
        compiler_params=_params(("arbitrary", "arbitrary", "arbitrary")),
    )(adu, hb)


def _at_owner(ref, lead, idx):
    return ref.at[idx] if lead is None else ref.at[lead, idx]


def _mix_bwd(proj3, z3, sprev, opre3, dmix3, gng, conv_w, wgu_p, pbs):
    nb, s, _ = proj3.shape
    nc = s // CHUNK
    na = len(pbs)

    def body(*refs):
        (p_ref, pprev_ref, z_ref, sp_ref, o_ref, dm_ref, gng_ref, cw_ref, wgu_ref) = refs[:9]
        pb_refs = refs[9:9 + na]
        (dproj_ref, dgng_ref, dcw_ref, dbg_ref, dwgu_ref) = refs[9 + na:14 + na]
        r2_refs = refs[14 + na:14 + 2 * na]
        ds_ref, dycn_ref, send_sems, recv_sems = refs[14 + 2 * na:]
        step = pl.program_id(0)
        n = nc - 1 - step

        @pl.when(step == 0)
        def _():
            for cp in _stage2_copies(pb_refs, r2_refs, send_sems, recv_sems):
                cp.start()
            ds_ref[...] = jnp.zeros_like(ds_ref)
            dycn_ref[...] = jnp.zeros_like(dycn_ref)
            dgng_ref[...] = jnp.zeros_like(dgng_ref)
            dcw_ref[...] = jnp.zeros_like(dcw_ref)
            dbg_ref[...] = jnp.zeros_like(dbg_ref)
            dwgu_ref[...] = jnp.zeros_like(dwgu_ref)

        r_i = lax.broadcasted_iota(jnp.int32, (CHUNK, CHUNK), 0)
        c_i = lax.broadcasted_iota(jnp.int32, (CHUNK, CHUNK), 1)
        tril16 = (r_i >= c_i).astype(BF16)
        triu16 = (r_i <= c_i).astype(BF16)
        causal = r_i >= c_i
        masks = _head_masks()
        cmask = _causal_stack_mask()
        gg = gng_ref[...]
        last_row = lax.broadcasted_iota(jnp.int32, (CHUNK, NQK), 0) == CHUNK - 1
        ones_r = jnp.ones((16, DV), BF16)
        has_prev = (n > 0).astype(F32)
        for b in range(nb):
            q = p_ref[b, :, OQ:OQ + NQK]
            k = p_ref[b, :, OK_:OK_ + NQK]
            z = z_ref[b]
            _, eb, enb, ekl, qi, ki, ks, decb = _chunk_fwd_parts(q, k, z, tril16)
            qi16 = qi.astype(BF16)
            ki16 = ki.astype(BF16)
            qs = _stack_heads(qi, masks).astype(BF16)
            sc = jnp.where(cmask, _nt(qs, ki16), 0.0).astype(BF16)
            st = sp_ref[b, 0]
            st16 = st.astype(BF16)
            dsn = ds_ref[b]
            dsn16 = dsn.astype(BF16)
            v16 = p_ref[b, :, OV:OV + NV].astype(BF16)
            do16 = []
            dgng = jnp.zeros((1, DV), F32)
            for h in range(HEADS):
                cols = slice(DV * h, DV * (h + 1))
                o = o_ref[b, :, cols]
                r = lax.rsqrt(jnp.mean(o * o, axis=-1, keepdims=True) + EPS)
                nh = o * r
                g = p_ref[b, :, OG + DV * h:OG + DV * (h + 1)]
                sg = jax.nn.sigmoid(g)
                dog = dm_ref[b, :, cols]
                dproj_ref[b, :, OG + DV * h:OG + DV * (h + 1)] = (
                    (dog * (nh * gg)) * (sg * (1.0 + g * (1.0 - sg)))).astype(BF16)
                don = dog * (g * sg)
                dgng = dgng + jnp.sum(don * nh, axis=0, keepdims=True)
                dn = don * gg
                do = r * (dn - nh * jnp.mean(dn * nh, axis=-1, keepdims=True))
                do16.append(do.astype(BF16))
            dgng_ref[...] += dgng
            do_rows = jnp.concatenate(do16, axis=0)
            v_rows = jnp.concatenate([v16[:, DV * h:DV * (h + 1)] for h in range(HEADS)], axis=0)
            dp16 = [jnp.where(causal, _nt(do16[h], v16[:, DV * h:DV * (h + 1)]), 0.0).astype(BF16)
                    for h in range(HEADS)]
            ks_dsn = _mm(_stack_heads(ks, masks).astype(BF16), dsn16)
            do_st = _nt(do_rows, st16)
            v_dsn = _nt(v_rows, dsn16)
            dp_ki = _mm(jnp.concatenate(dp16, axis=0), ki16)
            q_do = _tn(qi16, jnp.concatenate(do16, axis=1))
            dki_h = []
            for h in range(HEADS):
                rows = slice(CHUNK * h, CHUNK * (h + 1))
                cols = slice(DV * h, DV * (h + 1))
                dv = _tn(sc[rows], do16[h]) + ks_dsn[rows]
                dproj_ref[b, :, OV + DV * h:OV + DV * (h + 1)] = dv.astype(BF16)
                dki_h.append(_tn(dp16[h], qi16))
                ds_ref[b, rows, :] = decb[rows] * dsn[rows] + q_do[rows, cols]
            blocks = lambda a: [a[CHUNK * h:CHUNK * (h + 1)] for h in range(HEADS)]
            dqi = _merge_heads(blocks(dp_ki + do_st), masks)
            dki = _merge_heads(dki_h, masks)
            dks = _merge_heads(blocks(v_dsn), masks)
            dproj_ref[b, :, OQ:OQ + NQK] = (dqi * (Q_SCALE * eb)).astype(BF16)
            dproj_ref[b, :, OK_:OK_ + NQK] = (dki * enb + dks * ekl).astype(BF16)
            dks_ks = dks * ks
            db = dqi * qi - dki * ki - dks_ks
            sd = _split_bf16(dsn * st * decb, 2)
            dbl = jnp.sum(dks_ks, axis=0, keepdims=True) + (_nt(ones_r, sd[0]) + _nt(ones_r, sd[1]))[0:1, :]
            db = db + jnp.where(last_row, dbl, 0.0)
            db_parts = _split_bf16(db, 3)
            dla = _mm(triu16, db_parts[0]) + _mm(triu16, db_parts[1]) + _mm(triu16, db_parts[2])
            dz = (dla * INV_GATE_NORM) * (1.0 / (1.0 + jnp.exp(z)))
            dbg_ref[...] += jnp.sum(dz, axis=0, keepdims=True)
            dz16 = dz.astype(BF16)
            pa16 = p_ref[b, :, OA:OA + A_PAD].astype(BF16)
            dwgu_ref[...] += _tn(pa16, dz16)
            dproj_ref[b, :, OA:OA + A_PAD] = _nt(dz16, wgu_ref[...]).astype(BF16)
            cb = p_ref[b, :, OCB:OCB + CW]
            cc = p_ref[b, :, OCC:OCC + CW]
            ch = p_ref[b, :, OCH:OCH + CW]
            u = cc * ch
            uprev = (pprev_ref[b, :, 0:CW] * pprev_ref[b, :, CW:2 * CW]) * has_prev
            u1, u2 = _conv_taps(u, uprev)
            w0 = cw_ref[0:1, :]
            w1 = cw_ref[1:2, :]
            w2 = cw_ref[2:3, :]
            yc = w0 * u2 + w1 * u1 + w2 * u
            doc = dm_ref[b, :, NV:NV + CW]
            dproj_ref[b, :, OCB:OCB + CW] = (doc * yc).astype(BF16)
            dyc = doc * cb
            dycn = dycn_ref[b]
            row = lax.broadcasted_iota(jnp.int32, dyc.shape, 0)
            d1 = jnp.where(row >= CHUNK - 1, pltpu.roll(dycn, CHUNK - 1, 0), pltpu.roll(dyc, CHUNK - 1, 0))
            d2 = jnp.where(row >= CHUNK - 2, pltpu.roll(dycn, CHUNK - 2, 0), pltpu.roll(dyc, CHUNK - 2, 0))
            du = w2 * dyc + w1 * d1 + w0 * d2
            dproj_ref[b, :, OCC:OCC + CW] = (du * ch).astype(BF16)
            dproj_ref[b, :, OCH:OCH + CW] = (du * cc).astype(BF16)
            dcw_ref[0:1, :] += jnp.sum(dyc * u2, axis=0, keepdims=True)
            dcw_ref[1:2, :] += jnp.sum(dyc * u1, axis=0, keepdims=True)
            dcw_ref[2:3, :] += jnp.sum(dyc * u, axis=0, keepdims=True)
            dycn_ref[b] = dyc

        @pl.when(step == nc - 1)
        def _():
            copies = _stage2_copies(pb_refs, r2_refs, send_sems, recv_sems)
            for cp in copies:
                cp.wait_recv()
            for cp in copies:
                cp.wait_send()

    rev = lambda w: pl.BlockSpec((nb, CHUNK, w), lambda i: (0, nc - 1 - i, 0))
    const = lambda r, c: pl.BlockSpec((r, c), lambda i: (0, 0))
    hbm = pl.BlockSpec(memory_space=pl.ANY)
    return pl.pallas_call(
        body,
        name="mix_bwd",
        grid=(nc,),
        in_specs=[
            rev(PW),
            pl.BlockSpec((nb, CHUNK, 2 * CW), lambda i: (0, jnp.maximum(nc - 2 - i, 0), OCC // (2 * CW))),
            rev(NQK),
            pl.BlockSpec((nb, 1, NQK, DV), lambda i: (0, nc - 1 - i, 0, 0)),
            rev(NV),
            rev(D),
            const(1, DV),
            const(CONV_K, CW),
            const(A_PAD, NQK),
        ] + [hbm] * na,
        out_specs=[rev(PW), const(1, DV), const(8, CW), const(1, NQK), const(A_PAD, NQK)] + [hbm] * na,
        out_shape=[
            jax.ShapeDtypeStruct((nb, s, PW), BF16),
            jax.ShapeDtypeStruct((1, DV), F32),
            jax.ShapeDtypeStruct((8, CW), F32),
            jax.ShapeDtypeStruct((1, NQK), F32),
            jax.ShapeDtypeStruct((A_PAD, NQK), F32),
        ] + [jax.ShapeDtypeStruct((3,) + p.shape[1:], BF16) for p in pbs],
        scratch_shapes=[pltpu.VMEM((nb, NQK, DV), F32), pltpu.VMEM((nb, CHUNK, CW), F32),
                        pltpu.SemaphoreType.DMA((3 * na,)), pltpu.SemaphoreType.DMA((3 * na,))],
        compiler_params=_params(("arbitrary",)),
    )(proj3, proj3, z3, sprev, opre3, dmix3, gng, conv_w, wgu_p, *pbs)


SMALL_PACK_ROWS = 16


def _wgu_slot(r):
    return 4 + r // 4, NQK * (r % 4)


CONV_SLOTS = ((8, 0), (8, CW), (9, 0))


def _in_proj_bwd(dproj2d, x2d, dx1, g1, w_in_t, tm, pb, small_parts):
    t = x2d.shape[0]
    nt = t // tm

    def body(dp_ref, x_ref, dx1_ref, g_ref, w_ref, pb_ref, dg2, dgf, dbg, dgng, dwgu, dcw, lp,
             dx_ref, sums_ref, r2_ref, dg1_acc, pack, gbuf, pack1, gbuf1, send_sems, recv_sems,
             ssend, srecv, ssend1, srecv1):
        x, y, c = _position()
        me = 4 * x + 2 * y + c
        flips = [(k >> 2, (k >> 1) & 1, k & 1) for k in range(1, N_DEV)]
        peers = [(x ^ fx, y ^ fy, c ^ fc) for fx, fy, fc in flips]

        def small_copies(src, dst, send, recv, arrivals):
            return [pltpu.make_async_remote_copy(
                src_ref=src, dst_ref=dst.at[4 * px + 2 * py + pc if arrivals else me],
                send_sem=send.at[k], recv_sem=recv.at[k], device_id=(px, py, pc), device_id_type=MESH)
                for k, (px, py, pc) in enumerate(peers)]

        @pl.when(pl.program_id(0) == 0)
        def _():
            for cp in _stage2_copies([pb_ref], [r2_ref], send_sems, recv_sems):
                cp.start()
            dg1_acc[...] = jnp.zeros_like(dg1_acc)
            pack[...] = jnp.zeros_like(pack)
            pack[1:2, :] = dg2[...]
            pack[2:3, :] = dgf[...]
            pack[3:4, 0:NQK] = dbg[...]
            pack[3:4, NQK:NQK + DV] = dgng[...]
            pack[3:4, NQK + DV:NQK + 2 * DV] = lp[...]
            for r in range(RANK):
                row, lane = _wgu_slot(r)
                pack[row:row + 1, lane:lane + NQK] = dwgu[r:r + 1, :]
            for r, (row, lane) in enumerate(CONV_SLOTS):
                pack[row:row + 1, lane:lane + CW] = dcw[r:r + 1, :]
            for cp in small_copies(pack, gbuf, ssend, srecv, False):
                cp.start()
            gbuf[me] = pack[...]

        xv = x_ref[...]
        r = lax.rsqrt(jnp.mean(xv * xv, axis=-1, keepdims=True) + EPS)
        n1 = xv * r
        dh = _mm(dp_ref[...], w_ref[...])
        dg1_acc[...] += jnp.sum(dh * n1, axis=0, keepdims=True)
        dn = dh * g_ref[...]
        dx_ref[...] = dx1_ref[...] + r * (dn - n1 * jnp.mean(dn * n1, axis=-1, keepdims=True))

        @pl.when(pl.program_id(0) == nt - 1)
        def _():
            pack1[...] = jnp.zeros_like(pack1)
            pack1[0:1, :] = dg1_acc[...]
            for cp in small_copies(pack1, gbuf1, ssend1, srecv1, False):
                cp.start()
            gbuf1[me] = pack1[...]
            copies = _stage2_copies([pb_ref], [r2_ref], send_sems, recv_sems)
            for cp in copies:
                cp.wait_recv()
            for cp in copies:
                cp.wait_send()
            for src, dst, send, recv in ((pack, gbuf, ssend, srecv), (pack1, gbuf1, ssend1, srecv1)):
                for cp in small_copies(src, dst, send, recv, True):
                    cp.wait_recv()
                    cp.wait_send()
            acc = gbuf[0]
            acc1 = gbuf1[0]
            for d in range(1, N_DEV):
                acc = acc + gbuf[d]
                acc1 = acc1 + gbuf1[d]
            sums_ref[...] = acc
            sums_ref[0:1, :] = acc1[0:1, :]

    tile = lambda w: pl.BlockSpec((tm, w), lambda i: (i, 0))
    vec = pl.BlockSpec((1, D), lambda i: (0, 0))
    hbm = pl.BlockSpec(memory_space=pl.ANY)
    whole = lambda a: pl.BlockSpec(a.shape, lambda i: (0,) * a.ndim)
    return pl.pallas_call(
        body,
        name="in_proj_bwd",
        grid=(nt,),
        in_specs=[tile(PW), tile(D), tile(D), vec, pl.BlockSpec((PW, D), lambda i: (0, 0)), hbm]
        + [whole(a) for a in small_parts],
        out_specs=[tile(D), pl.BlockSpec((SMALL_PACK_ROWS, D), lambda i: (0, 0)), hbm],
        out_shape=[jax.ShapeDtypeStruct((t, D), F32), jax.ShapeDtypeStruct((SMALL_PACK_ROWS, D), F32),
                   jax.ShapeDtypeStruct((3,) + pb.shape[1:], BF16)],
        scratch_shapes=[pltpu.VMEM((1, D), F32),
                        pltpu.VMEM((SMALL_PACK_ROWS, D), F32), pltpu.VMEM((N_DEV, SMALL_PACK_ROWS, D), F32),
                        pltpu.VMEM((8, D), F32), pltpu.VMEM((N_DEV, 8, D), F32),
                        pltpu.SemaphoreType.DMA((3,)), pltpu.SemaphoreType.DMA((3,)),
                        pltpu.SemaphoreType.DMA((7,)), pltpu.SemaphoreType.DMA((7,)),
                        pltpu.SemaphoreType.DMA((7,)), pltpu.SemaphoreType.DMA((7,))],
        compiler_params=_params(("arbitrary",)),
    )(dproj2d, x2d, dx1, g1, w_in_t, pb, *small_parts)


def _adamw_math(w, g, m, v):
    m = ADAM_B1 * m + (1.0 - ADAM_B1) * g
    v = ADAM_B2 * v + (1.0 - ADAM_B2) * (g * g)
    m_hat = m / (1.0 - ADAM_B1 ** ADAM_STEP)
    v_hat = v / (1.0 - ADAM_B2 ** ADAM_STEP)
    delta = -ADAM_LR * (m_hat / (jnp.sqrt(v_hat) + ADAM_EPS) + ADAM_WD * w)
    return delta, m, v


def _position():
    return lax.axis_index("x"), lax.axis_index("y"), lax.axis_index("c")


def _prep_slab(w_it, w_gt, w_ut, w_d, w_o):
    def body(wi_ref, wg_ref, wu_ref, wd_ref, wo_ref, stage):
        stage[SLAB_IN:SLAB_IN + IN_W, :] = wi_ref[...].astype(BF16)
        stage[SLAB_IN + IN_W:SLAB_G, :] = jnp.zeros((IN_ROWS - IN_W, D), BF16)
        stage[SLAB_G:SLAB_U, :] = wg_ref[...].astype(BF16)
        stage[SLAB_U:SLAB_D, :] = wu_ref[...].astype(BF16)
        stage[SLAB_D:SLAB_O, :] = wd_ref[...].astype(BF16)
        stage[SLAB_O:SLAB_ROWS, :] = wo_ref[...].astype(BF16)

    vm = pl.BlockSpec(memory_space=pltpu.VMEM)
    return pl.pallas_call(
        body,
        name="prep_slab",
        in_specs=[vm] * 5,
        out_specs=vm,
        out_shape=jax.ShapeDtypeStruct((SLAB_ROWS, D), BF16),
        compiler_params=_params(),
    )(w_it, w_gt, w_ut, w_d, w_o)


GATHER_SEMS = 7


def _gather_copies(stage, lo, rows, gx, send_sems, recv_sems, local_sem):
    x, y, c = _position()
    me = (x, y, c)
    sibling = (x, y, 1 - c)
    chips = [(1 - x, y), (x, 1 - y), (1 - x, 1 - y)]
    src = stage.at[pl.ds(lo, rows), :]

    def blk(px, py, pc):
        return gx.at[4 * px + 2 * py + pc]

    def copy(k, block, to, from_stage=False):
        return pltpu.make_async_remote_copy(
            src_ref=src if from_stage else blk(*block), dst_ref=blk(*block),
            send_sem=send_sems.at[k], recv_sem=recv_sems.at[k], device_id=to, device_id_type=MESH)

    mine = pltpu.make_async_copy(src, blk(*me), local_sem)
    first = [copy(0, me, sibling, True)] + [copy(1 + j, me, (*chip, c), True) for j, chip in enumerate(chips)]
    passed = [copy(4 + j, (*chip, c), sibling) for j, chip in enumerate(chips)]
    arrivals = ([copy(0, sibling, me)] + [copy(1 + j, (*chip, c), me) for j, chip in enumerate(chips)]
                + [copy(4 + j, (*chip, 1 - c), me) for j, chip in enumerate(chips)])
    return mine, first, passed, arrivals


def _gather_start(*args):
    mine, first, _, _ = _gather_copies(*args)
    mine.start()
    for cp in first:
        cp.start()


def _gather_finish(*args):
    mine, first, passed, arrivals = _gather_copies(*args)
    for j in range(3):
        arrivals[1 + j].wait_recv()
        passed[j].start()
    arrivals[0].wait_recv()
    for j in range(3):
        arrivals[4 + j].wait_recv()
    for cp in first + passed:
        cp.wait_send()
    mine.wait()


def _gather_sems():
    return [pltpu.SemaphoreType.DMA((GATHER_SEMS,)), pltpu.SemaphoreType.DMA((GATHER_SEMS,)), pltpu.SemaphoreType.DMA]


def _gather_w_in(stage, wgu_s, conv_s):
    def body(stage_hbm, wgu_ref, conv_ref, w_ref, gwgu_ref, gconv_ref, buf, send_sems, recv_sems, local_sem,
             ssend, srecv):
        x, y, c = _position()
        me = 4 * x + 2 * y + c
        args = (stage_hbm, SLAB_IN, IN_ROWS, buf, send_sems, recv_sems, local_sem)
        _gather_start(*args)
        flips = [(k >> 2, (k >> 1) & 1, k & 1) for k in range(1, N_DEV)]
        peers = [(x ^ fx, y ^ fy, c ^ fc) for fx, fy, fc in flips]

        def small(k, block_id, to):
            return [pltpu.make_async_remote_copy(
                src_ref=s, dst_ref=g.at[block_id], send_sem=ssend.at[2 * k + n], recv_sem=srecv.at[2 * k + n],
                device_id=to, device_id_type=MESH)
                for n, (s, g) in enumerate(((wgu_ref, gwgu_ref), (conv_ref, gconv_ref)))]

        gwgu_ref[me] = wgu_ref[...]
        gconv_ref[me] = conv_ref[...]
        for k, peer in enumerate(peers):
            for cp in small(k, me, peer):
                cp.start()
        w_ref[IN_COLS:PW, :] = jnp.zeros((PW - IN_COLS, D), BF16)
        _gather_finish(*args)
        for k, (px, py, pc) in enumerate(peers):
            for cp in small(k, 4 * px + 2 * py + pc, (px, py, pc)):
                cp.wait_recv()
                cp.wait_send()
        for j, lo, hi, d in _in_segments():
            w_ref[d:d + hi - lo, :] = buf[j, lo:hi, :]

    vm = pl.BlockSpec(memory_space=pltpu.VMEM)
    hbm = pl.BlockSpec(memory_space=pl.ANY)
    return pl.pallas_call(
        body,
        name="gather_w_in",
        in_specs=[hbm, vm, vm],
        out_specs=[vm, vm, vm],
        out_shape=[jax.ShapeDtypeStruct((PW, D), BF16),
                   jax.ShapeDtypeStruct((N_DEV,) + wgu_s.shape, F32),
                   jax.ShapeDtypeStruct((N_DEV,) + conv_s.shape, F32)],
        scratch_shapes=[pltpu.VMEM((N_DEV, IN_ROWS, D), BF16)] + _gather_sems()
        + [pltpu.SemaphoreType.DMA((14,)), pltpu.SemaphoreType.DMA((14,))],
        compiler_params=_params(),
    )(stage, wgu_s, conv_s)


def _w_in_core_reduce(dw_t):
    def body(d_ref, own_ref, sib_ref, pb_ref, g, gb, r1, send_sems, recv_sems):
        x, y, c = _position()
        chip = 2 * x + y
        for j in range(N_DEV):
            g[j, IN_W:IN_ROWS, :] = jnp.zeros((IN_ROWS - IN_W, D), F32)
        for j, lo, hi, d in _in_segments():
            g[j, lo:hi, :] = d_ref[d:d + hi - lo, :]
        for j in range(N_DEV):
            gb[j] = g[j].astype(BF16)
        copies = _stage1_copies([gb], [None], [r1], send_sems, recv_sems)
        for cp in copies:
            cp.start()
        own_ref[0] = g[2 * chip + c]
        for cp in copies:
            cp.wait_recv()
        sib_ref[0] = r1[chip]
        for k in range(1, 4):
            t = chip ^ k
            pb_ref[k - 1] = (g[2 * t + c] + r1[t].astype(F32)).astype(BF16)
        for cp in copies:
            cp.wait_send()

    vm = pl.BlockSpec(memory_space=pltpu.VMEM)
    return pl.pallas_call(
        body,
        name="w_in_core_reduce",
        in_specs=[vm],
        out_specs=[vm, vm, vm],
        out_shape=[jax.ShapeDtypeStruct((1, IN_ROWS, D), F32), jax.ShapeDtypeStruct((1, IN_ROWS, D), BF16),
                   jax.ShapeDtypeStruct((3, IN_ROWS, D), BF16)],
        scratch_shapes=[pltpu.VMEM((N_DEV, IN_ROWS, D), F32), pltpu.VMEM((N_DEV, IN_ROWS, D), BF16),
                        pltpu.VMEM((4, IN_ROWS, D), BF16), pltpu.SemaphoreType.DMA((4,)),
                        pltpu.SemaphoreType.DMA((4,))],
        compiler_params=_params(),
    )(dw_t)


def _stage1_copies(g_refs, leads, r_refs, send_sems, recv_sems):
    x, y, c = _position()
    return [pltpu.make_async_remote_copy(
        src_ref=_at_owner(g_refs[a], leads[a], 2 * i + 1 - c), dst_ref=r_refs[a].at[i],
        send_sem=send_sems.at[4 * a + i], recv_sem=recv_sems.at[4 * a + i],
        device_id=(x, y, 1 - c), device_id_type=MESH) for a in range(len(g_refs)) for i in range(4)]


def _exchange_core(gbs, name):
    n = len(gbs)

    def body(*refs):
        copies = _stage1_copies(refs[:n], [None] * n, refs[n:2 * n], refs[2 * n], refs[2 * n + 1])
        for cp in copies:
            cp.start()
        for cp in copies:
            cp.wait_recv()
        for cp in copies:
            cp.wait_send()

    hbm = pl.BlockSpec(memory_space=pl.ANY)
    return pl.pallas_call(
        body,
        name=name,
        in_specs=[hbm] * n,
        out_specs=[hbm] * n,
        out_shape=[jax.ShapeDtypeStruct((4,) + g.shape[1:], BF16) for g in gbs],
        scratch_shapes=[pltpu.SemaphoreType.DMA((4 * n,)), pltpu.SemaphoreType.DMA((4 * n,))],
        compiler_params=_params(),
    )(*gbs)


def _add_core(g8, r1, pos_arr, name, lead=None):
    rows = g8.shape[-2]
    if lead is None:
        g_spec = pl.BlockSpec((1, rows, D), lambda k, pos: (2 * (pos[1] ^ (k + 1)) + pos[0], 0, 0))
    else:
        g_spec = pl.BlockSpec((None, 1, rows, D), lambda k, pos: (lead, 2 * (pos[1] ^ (k + 1)) + pos[0], 0, 0))

    def body(pos_ref, g_ref, r_ref, pb_ref):
        pb_ref[...] = (g_ref[...] + r_ref[...].astype(F32)).astype(BF16)

    return pl.pallas_call(
        body,
        name=name,
        grid_spec=pltpu.PrefetchScalarGridSpec(
            num_scalar_prefetch=1, grid=(3,),
            in_specs=[g_spec,
                      pl.BlockSpec((1, rows, D), lambda k, pos: (pos[1] ^ (k + 1), 0, 0))],
            out_specs=pl.BlockSpec((1, rows, D), lambda k, pos: (k, 0, 0))),
        out_shape=jax.ShapeDtypeStruct((3, rows, D), BF16),
        compiler_params=_params(("arbitrary",)),
    )(pos_arr, g8, r1)


def _stage2_copies(p_refs, r_refs, send_sems, recv_sems):
    x, y, c = _position()
    copies = []
    for a in range(len(p_refs)):
        for k in range(1, 4):
            copies.append(pltpu.make_async_remote_copy(
                src_ref=p_refs[a].at[k - 1], dst_ref=r_refs[a].at[k - 1],
                send_sem=send_sems.at[3 * a + k - 1], recv_sem=recv_sems.at[3 * a + k - 1],
                device_id=(x ^ (k >> 1), y ^ (k & 1), c), device_id_type=MESH))
    return copies


def _finish_weight(g8, r1, r2, w, m, v, pos_arr, name, lead=None):
    rows = g8.shape[-2]
    wr = w.shape[0]
    nblk = 2 if (rows == wr and rows % 32 == 0) else 1
    rb = rows // nblk
    wb = wr // nblk
    own = g8.shape[0] == 1
    if own:
        g_spec = pl.BlockSpec((1, rb, D), lambda i, pos: (0, i, 0))
    elif lead is None:
        g_spec = pl.BlockSpec((1, rb, D), lambda i, pos: (2 * pos[1] + pos[0], i, 0))
    else:
        g_spec = pl.BlockSpec((None, 1, rb, D), lambda i, pos: (lead, 2 * pos[1] + pos[0], i, 0))
    r1_spec = pl.BlockSpec((1, rb, D), (lambda i, pos: (0, i, 0)) if own else (lambda i, pos: (pos[1], i, 0)))

    def body(pos_ref, g_ref, r1_ref, r2_ref, w_ref, m_ref, v_ref, g_out, d_out, m_out, v_out):
        g = g_ref[0] + r1_ref[0].astype(F32)
        for k in range(3):
            g = g + r2_ref[k].astype(F32)
        g = g[0:wb, :]
        g_out[...] = g
        d, mn, vn = _adamw_math(w_ref[...], g, m_ref[...], v_ref[...])
        d_out[...] = d
        m_out[...] = mn
        v_out[...] = vn

    wblk = pl.BlockSpec((wb, D), lambda i, pos: (i, 0))
    shp = jax.ShapeDtypeStruct(w.shape, F32)
    return pl.pallas_call(
        body,
        name=name,
        grid_spec=pltpu.PrefetchScalarGridSpec(
            num_scalar_prefetch=1, grid=(nblk,),
            in_specs=[g_spec, r1_spec,
                      pl.BlockSpec((3, rb, D), lambda i, pos: (0, i, 0)), wblk, wblk, wblk],
            out_specs=[wblk] * 4),
        out_shape=[shp, shp, shp, shp],
        compiler_params=_params(("arbitrary",)),
    )(pos_arr, g8, r1, r2, w, m, v)


SMALL_NAMES = ("norm1_g", "norm2_g", "norm_f_g", "b_gate", "gla_norm_g", "w_gate_up", "conv_w")
WGU_W = NQK // N_DEV
CONV_W = CW // N_DEV


def _small_adamw(sums, ws, ms, vs):
    n = len(SMALL_NAMES)

    def body(*refs):
        acc_ref = refs[0]
        w_refs, m_refs, v_refs = refs[1:1 + n], refs[1 + n:1 + 2 * n], refs[1 + 2 * n:1 + 3 * n]
        loss_ref = refs[1 + 3 * n]
        outs = refs[2 + 3 * n:]
        x, y, c = _position()
        me = 4 * x + 2 * y + c
        acc = acc_ref[...]
        loss_ref[...] = acc[3:4, NQK + DV:NQK + DV + 1]

        def my_columns(full, width):
            r = lax.broadcasted_iota(jnp.int32, (full.shape[1], width), 0)
            col = lax.broadcasted_iota(jnp.int32, (full.shape[1], width), 1)
            sel = (r == width * me + col).astype(F32)
            return _mm(full, sel, precision=HIGHEST)

        dwgu = jnp.concatenate([acc[row:row + 1, lane:lane + NQK] for row, lane in map(_wgu_slot, range(RANK))], axis=0)
        dcw = jnp.concatenate([acc[row:row + 1, lane:lane + CW] for row, lane in CONV_SLOTS], axis=0)
        grads = [acc[0:1, :], acc[1:2, :], acc[2:3, :], acc[3:4, 0:NQK], acc[3:4, NQK:NQK + DV],
                 my_columns(dwgu, WGU_W), my_columns(dcw, CONV_W)]
        for i, g in enumerate(grads):
            d, mn, vn = _adamw_math(w_refs[i][...], g, m_refs[i][...], v_refs[i][...])
            outs[4 * i][...] = g
            outs[4 * i + 1][...] = d
            outs[4 * i + 2][...] = mn
            outs[4 * i + 3][...] = vn

    vm = pl.BlockSpec(memory_space=pltpu.VMEM)
    out_shape = [jax.ShapeDtypeStruct((1, 1), F32)]
    for w in ws:
        out_shape += [jax.ShapeDtypeStruct(w.shape, F32)] * 4
    return pl.pallas_call(
        body,
        name="small_adamw",
        in_specs=[vm] * (1 + 3 * n),
        out_specs=[vm] * (1 + 4 * n),
        out_shape=out_shape,
        compiler_params=_params(),
    )(sums, *ws, *ms, *vs)


def kernel(x, norm1_g, w_in, w_gate_up, b_gate, gla_norm_g, conv_w, w_out, norm2_g, w_ffn_gate, w_ffn_up, w_ffn_down, norm_f_g, loss_target, m_norm1_g, m_w_in, m_w_gate_up, m_b_gate, m_gla_norm_g, m_conv_w, m_w_out, m_norm2_g, m_w_ffn_gate, m_w_ffn_up, m_w_ffn_down, m_norm_f_g, v_norm1_g, v_w_in, v_w_gate_up, v_b_gate, v_gla_norm_g, v_conv_w, v_w_out, v_norm2_g, v_w_ffn_gate, v_w_ffn_up, v_w_ffn_down, v_norm_f_g):
    xi, yi, ci = _position()
    pos_arr = jnp.stack([ci, 2 * xi + yi]).astype(jnp.int32)
    nb, s, _ = x.shape
    t = nb * s

    tr = lambda a: a[0].T
    stage = _prep_slab(tr(w_in), tr(w_ffn_gate), tr(w_ffn_up), w_ffn_down[0], w_out[0])
    w_in_t, gwgu, gconv = _gather_w_in(stage, w_gate_up[0], conv_w[0])
    wgu_f = gwgu.transpose(1, 0, 2).reshape(RANK, NQK)
    conv_f = gconv.transpose(1, 0, 2).reshape(CONV_K, CW)
    wgu_p = jnp.concatenate([wgu_f, jnp.zeros((A_PAD - RANK, NQK), F32)], axis=0).astype(BF16)

    x2d = x.reshape(t, D)
    tgt2d = loss_target.reshape(t, D)
    tm = 256
    tm_in = min(512, t)
    tk = min(2048, t)
    proj, z, h, gwb = _in_proj_fwd(x2d, norm1_g, w_in_t, wgu_p, b_gate, tm_in, stage)
    proj3 = proj.reshape(nb, s, PW)
    z3 = z.reshape(nb, s, NQK)
    mix3, opre3, sprev, gwa = _mix_fwd(proj3, z3, gla_norm_g, conv_f, stage)
    mix2d = mix3.reshape(t, D)
    dx1, dx1b, dmix, adu, hb, dg2, dgf, loss_part = _ffn_fwd_bwd(
        mix2d, x2d, tgt2d, gwa, gwb, norm2_g, norm_f_g.reshape(1, D), tm)
    dw3, dwb3 = _dw_ffn(adu, hb, tk)
    dw3 = dw3.reshape(3, N_DEV, FF_W, D)
    dw_o, dwb_o, *r1_ffn = _tn_matmul(mix2d, dx1b, D // 4, D, tk, "dw_out", True,
                                      _stage1_rider(dwb3.reshape(3, N_DEV, FF_W, D)))
    g8 = [dw3, dw3, dw3, dw_o.reshape(N_DEV, OUT_ROWS, D)]
    leads = [0, 1, 2, None]
    tags = ("w_ffn_down", "w_ffn_gate", "w_ffn_up", "w_out")
    r1 = list(r1_ffn) + list(_exchange_core([dwb_o.reshape(N_DEV, OUT_ROWS, D)], "grad_exchange_core_out"))
    pb = [_add_core(g, r, pos_arr, "grad_add_core_" + tag, lead)
          for g, r, tag, lead in zip(g8, r1, tags, leads)]
    mb = _mix_bwd(proj3, z3, sprev, opre3, dmix.reshape(nb, s, D), gla_norm_g, conv_f, wgu_p, [pb[0], pb[1], pb[3]])
    dproj3, dgng, dcw, dbg, dwgu = mb[:5]
    dproj2d = dproj3.reshape(t, PW)
    dw_in_t, r2_up = _tn_matmul(dproj2d, h, PW // 5, D, tk, "dw_in", False, _stage2_rider([pb[2]]))
    r2 = [mb[5], mb[6], r2_up, mb[7]]
    g_in, r1_in, pb_in = _w_in_core_reduce(dw_in_t)
    dx, small_sums, r2_in = _in_proj_bwd(dproj2d, x2d, dx1, norm1_g, w_in_t, tm_in, pb_in,
                                         (dg2, dgf, dbg, dgng, dwgu, dcw, loss_part))

    tags = ("w_in",) + tags
    g8 = [g_in] + g8
    leads = [None] + leads
    r1 = [r1_in] + list(r1)
    r2 = [r2_in] + r2
    shard_w = (tr(w_in), w_ffn_down[0], tr(w_ffn_gate), tr(w_ffn_up), w_out[0])
    shard_m = (tr(m_w_in), m_w_ffn_down[0], tr(m_w_ffn_gate), tr(m_w_ffn_up), m_w_out[0])
    shard_v = (tr(v_w_in), v_w_ffn_down[0], tr(v_w_ffn_gate), tr(v_w_ffn_up), v_w_out[0])
    transposed = (True, False, True, True, False)
    results = {}
    for tag, g, lead, ra, rb, w, m, v, tp in zip(tags, g8, leads, r1, r2, shard_w, shard_m, shard_v, transposed):
        outs = _finish_weight(g, ra, rb, w, m, v, pos_arr, "finish_" + tag, lead)
        results[tag] = [o.T[None] if tp else o[None] for o in outs]

    small_w = (norm1_g, norm2_g, norm_f_g.reshape(1, D), b_gate, gla_norm_g, w_gate_up[0], conv_w[0])
    small_m = (m_norm1_g, m_norm2_g, m_norm_f_g.reshape(1, D), m_b_gate, m_gla_norm_g, m_w_gate_up[0], m_conv_w[0])
    small_v = (v_norm1_g, v_norm2_g, v_norm_f_g.reshape(1, D), v_b_gate, v_gla_norm_g, v_w_gate_up[0], v_conv_w[0])
    so = _small_adamw(small_sums, small_w, small_m, small_v)
    loss = so[0].reshape(())
    shapes = {"norm_f_g": (D,), "w_gate_up": (1, RANK, WGU_W), "conv_w": (1, CONV_K, CONV_W)}
    for i, name in enumerate(SMALL_NAMES):
        results[name] = [o.reshape(shapes[name]) if name in shapes else o for o in so[1 + 4 * i:5 + 4 * i]]

    names = ("norm1_g", "w_in", "w_gate_up", "b_gate", "gla_norm_g", "conv_w", "w_out", "norm2_g",
             "w_ffn_gate", "w_ffn_up", "w_ffn_down", "norm_f_g")
    outs = [loss, dx.reshape(nb, s, D)]
    for kind in range(4):
        for name in names:
            outs.append(results[name][kind])
    return tuple(outs)
```

```python
import jax
import jax.numpy as jnp
from jax import lax
from jax.experimental import pallas as pl
from jax.experimental.pallas import tpu as pltpu

F32 = jnp.float32
BF16 = jnp.bfloat16
HIGHEST = lax.Precision.HIGHEST
MESH = pl.DeviceIdType.MESH

N_DEV = 8
D = 1024
DFF = 2816
HEADS = 4
DK = 64
DV = 128
NQK = HEADS * DK
NV = HEADS * DV
RANK = 16
CHUNK = 64
CW = 512
CONV_K = 3
IN_COLS = 3088
EPS = 1e-6
INV_GATE_NORM = 1.0 / 16.0
Q_SCALE = DK ** -0.5

PW = 3200
OQ, OK_, OV, OG, OCB, OCC, OCH, OA = 0, 256, 512, 1024, 1536, 2048, 2560, 3072
A_PAD = 128

ADAM_LR = 0.001
ADAM_B1 = 0.9
ADAM_B2 = 0.999
ADAM_EPS = 1e-08
ADAM_WD = 0.01
ADAM_STEP = 10

IN_W = IN_COLS // N_DEV
IN_ROWS = 400
FF_W = DFF // N_DEV
OUT_ROWS = D // N_DEV
SLAB_IN = 0
SLAB_G = SLAB_IN + IN_ROWS
SLAB_U = SLAB_G + FF_W
SLAB_D = SLAB_U + FF_W
SLAB_O = SLAB_D + FF_W
SLAB_ROWS = SLAB_O + OUT_ROWS

VMEM_LIMIT = 56 * 1024 * 1024


def _params(sem=None, vmem=VMEM_LIMIT):
    return pltpu.CompilerParams(dimension_semantics=sem, vmem_limit_bytes=vmem)


def _nt(a, b):
    return lax.dot_general(a, b, (((1,), (1,)), ((), ())), preferred_element_type=F32)


def _tn(a, b, precision=None):
    return lax.dot_general(a, b, (((0,), (0,)), ((), ())), preferred_element_type=F32, precision=precision)


def _mm(a, b, precision=None):
    return jnp.dot(a, b, preferred_element_type=F32, precision=precision)


def _in_segments():
    segs = []
    for j in range(N_DEV):
        lo, hi = IN_W * j, IN_W * (j + 1)
        cuts = sorted({lo, hi} | {c for c in (OCB, OCB + RANK) if lo < c < hi})
        for a, b in zip(cuts[:-1], cuts[1:]):
            if a < OCB:
                d = a
            elif a < OCB + RANK:
                d = OA + (a - OCB)
            else:
                d = a - RANK
            segs.append((j, a - lo, b - lo, d))
    return segs


def _in_proj_fwd(x2d, g1, w_in_t, wgu_p, b_gate, tm, stage):
    t = x2d.shape[0]
    nt = t // tm
    g_rows = SLAB_ROWS - SLAB_D

    def body(x_ref, g_ref, w_ref, wgu_ref, bg_ref, stage_hbm, proj_ref, z_ref, h_ref, gwb_ref,
             send_sems, recv_sems, local_sem):
        gargs = (stage_hbm, SLAB_D, g_rows, gwb_ref, send_sems, recv_sems, local_sem)

        @pl.when(pl.program_id(0) == 0)
        def _():
            _gather_start(*gargs)

        x = x_ref[...]
        r = lax.rsqrt(jnp.mean(x * x, axis=-1, keepdims=True) + EPS)
        h = ((x * r) * g_ref[...]).astype(BF16)
        h_ref[...] = h
        proj = _nt(h, w_ref[...])
        proj_ref[...] = proj
        pa = proj[:, OA:OA + A_PAD].astype(BF16)
        z_ref[...] = _mm(pa, wgu_ref[...]) + bg_ref[...]

        @pl.when(pl.program_id(0) == nt - 1)
        def _():
            _gather_finish(*gargs)

    return pl.pallas_call(
        body,
        name="in_proj_fwd",
        grid=(t // tm,),
        in_specs=[
            pl.BlockSpec((tm, D), lambda i: (i, 0)),
            pl.BlockSpec((1, D), lambda i: (0, 0)),
            pl.BlockSpec((PW, D), lambda i: (0, 0)),
            pl.BlockSpec((A_PAD, NQK), lambda i: (0, 0)),
            pl.BlockSpec((1, NQK), lambda i: (0, 0)),
            pl.BlockSpec(memory_space=pl.ANY),
        ],
        out_specs=[
            pl.BlockSpec((tm, PW), lambda i: (i, 0)),
            pl.BlockSpec((tm, NQK), lambda i: (i, 0)),
            pl.BlockSpec((tm, D), lambda i: (i, 0)),
            pl.BlockSpec(memory_space=pl.ANY),
        ],
        out_shape=[
            jax.ShapeDtypeStruct((t, PW), F32),
            jax.ShapeDtypeStruct((t, NQK), F32),
            jax.ShapeDtypeStruct((t, D), BF16),
            jax.ShapeDtypeStruct((N_DEV, g_rows, D), BF16),
        ],
        scratch_shapes=_gather_sems(),
        compiler_params=_params(("arbitrary",)),
    )(x2d, g1, w_in_t, wgu_p, b_gate, stage)


def _head_masks():
    lane = lax.broadcasted_iota(jnp.int32, (1, NQK), 1)
    return [(lane >= DK * h) & (lane < DK * (h + 1)) for h in range(HEADS)]


def _split_bf16(x, n):
    parts = []
    for _ in range(n):
        p = x.astype(BF16)
        parts.append(p)
        x = x - p.astype(F32)
    return parts


def _chunk_fwd_parts(q, k, z, tril16):
    la = (jnp.minimum(z, 0.0) - jnp.log1p(jnp.exp(-jnp.abs(z)))) * INV_GATE_NORM
    la_parts = _split_bf16(la, 3)
    bc = _mm(tril16, la_parts[0]) + _mm(tril16, la_parts[1]) + _mm(tril16, la_parts[2])
    bl = bc[CHUNK - 1:CHUNK, :]
    eb = jnp.exp(bc)
    enb = jnp.exp(-bc)
    ekl = jnp.exp(bl - bc)
    qi = (q * Q_SCALE) * eb
    ki = k * enb
    ks = k * ekl
    ones16 = jnp.ones((CHUNK, DV), BF16)
    decb = jnp.exp(_tn(la_parts[0], ones16) + _tn(la_parts[1], ones16) + _tn(la_parts[2], ones16))
    return la, eb, enb, ekl, qi, ki, ks, decb


def _stack_heads(a, masks):
    return jnp.concatenate([jnp.where(m, a, 0.0) for m in masks], axis=0)


def _merge_heads(blocks, masks):
    out = blocks[HEADS - 1]
    for h in range(HEADS - 2, -1, -1):
        out = jnp.where(masks[h], blocks[h], out)
    return out


def _causal_stack_mask():
    row = lax.broadcasted_iota(jnp.int32, (HEADS * CHUNK, CHUNK), 0)
    col = lax.broadcasted_iota(jnp.int32, (HEADS * CHUNK, CHUNK), 1)
    return (row & (CHUNK - 1)) >= col


def _conv_taps(u, uprev):
    row = lax.broadcasted_iota(jnp.int32, u.shape, 0)
    u1 = jnp.where(row < 1, pltpu.roll(uprev, 1, 0), pltpu.roll(u, 1, 0))
    u2 = jnp.where(row < 2, pltpu.roll(uprev, 2, 0), pltpu.roll(u, 2, 0))
    return u1, u2


def _mix_fwd(proj3, z3, gng, conv_w, stage):
    nb, s, _ = proj3.shape
    nc = s // CHUNK
    g_rows = SLAB_D - SLAB_G

    def body(p_ref, z_ref, gng_ref, cw_ref, stage_hbm, mix_ref, o_ref, sprev_ref, gwa_ref, s_ref, uprev_ref,
             send_sems, recv_sems, local_sem):
        n = pl.program_id(0)
        gargs = (stage_hbm, SLAB_G, g_rows, gwa_ref, send_sems, recv_sems, local_sem)

        @pl.when(n == 0)
        def _():
            _gather_start(*gargs)
            s_ref[...] = jnp.zeros_like(s_ref)
            uprev_ref[...] = jnp.zeros_like(uprev_ref)

        r_i = lax.broadcasted_iota(jnp.int32, (CHUNK, CHUNK), 0)
        c_i = lax.broadcasted_iota(jnp.int32, (CHUNK, CHUNK), 1)
        tril16 = (r_i >= c_i).astype(BF16)
        masks = _head_masks()
        cmask = _causal_stack_mask()
        gg = gng_ref[...]
        for b in range(nb):
            q = p_ref[b, :, OQ:OQ + NQK]
            k = p_ref[b, :, OK_:OK_ + NQK]
            _, _, _, _, qi, ki, ks, decb = _chunk_fwd_parts(q, k, z_ref[b], tril16)
            qs = _stack_heads(qi, masks).astype(BF16)
            sc = jnp.where(cmask, _nt(qs, ki.astype(BF16)), 0.0).astype(BF16)
            st = s_ref[b]
            sprev_ref[b, 0] = st
            o_inter = _mm(qs, st.astype(BF16))
            v16 = p_ref[b, :, OV:OV + NV].astype(BF16)
            kv = _tn(ks.astype(BF16), v16)
            for h in range(HEADS):
                rows = slice(CHUNK * h, CHUNK * (h + 1))
                cols = slice(DV * h, DV * (h + 1))
                o = _mm(sc[rows], v16[:, cols]) + o_inter[rows]
                o_ref[b, :, cols] = o
                r = lax.rsqrt(jnp.mean(o * o, axis=-1, keepdims=True) + EPS)
                on = (o * r) * gg
                g = p_ref[b, :, OG + DV * h:OG + DV * (h + 1)]
                mix_ref[b, :, cols] = (on * (g * jax.nn.sigmoid(g))).astype(BF16)
                s_ref[b, rows, :] = decb[rows] * st[rows] + kv[rows, cols]
            u = p_ref[b, :, OCC:OCC + CW] * p_ref[b, :, OCH:OCH + CW]
            u1, u2 = _conv_taps(u, uprev_ref[b])
            yc = cw_ref[0:1, :] * u2 + cw_ref[1:2, :] * u1 + cw_ref[2:3, :] * u
            mix_ref[b, :, NV:NV + CW] = (p_ref[b, :, OCB:OCB + CW] * yc).astype(BF16)
            uprev_ref[b] = u

        @pl.when(n == nc - 1)
        def _():
            _gather_finish(*gargs)

    return pl.pallas_call(
        body,
        name="mix_fwd",
        grid=(nc,),
        in_specs=[
            pl.BlockSpec((nb, CHUNK, PW), lambda n: (0, n, 0)),
            pl.BlockSpec((nb, CHUNK, NQK), lambda n: (0, n, 0)),
            pl.BlockSpec((1, DV), lambda n: (0, 0)),
            pl.BlockSpec((CONV_K, CW), lambda n: (0, 0)),
            pl.BlockSpec(memory_space=pl.ANY),
        ],
        out_specs=[
            pl.BlockSpec((nb, CHUNK, D), lambda n: (0, n, 0)),
            pl.BlockSpec((nb, CHUNK, NV), lambda n: (0, n, 0)),
            pl.BlockSpec((nb, 1, NQK, DV), lambda n: (0, n, 0, 0)),
            pl.BlockSpec(memory_space=pl.ANY),
        ],
        out_shape=[
            jax.ShapeDtypeStruct((nb, s, D), BF16),
            jax.ShapeDtypeStruct((nb, s, NV), F32),
            jax.ShapeDtypeStruct((nb, nc, NQK, DV), F32),
            jax.ShapeDtypeStruct((N_DEV, g_rows, D), BF16),
        ],
        scratch_shapes=[pltpu.VMEM((nb, NQK, DV), F32), pltpu.VMEM((nb, CHUNK, CW), F32)] + _gather_sems(),
        compiler_params=_params(("arbitrary",)),
    )(proj3, z3, gng, conv_w, stage)


def _ffn_fwd_bwd(mix2d, x2d, tgt2d, gwa, gwb, g2, gf, tm):
    t = x2d.shape[0]

    def body(mix_ref, x_ref, tgt_ref, g2_ref, gf_ref, gwa_hbm, gwb_hbm,
             dx1_ref, dx1b_ref, dmix_ref, adu_ref, hb_ref, dg2_ref, dgf_ref, loss_ref,
             wo, wg, wu, wd, wsem):
        i = pl.program_id(0)

        def weight_copies(n, dst, src, off, rows):
            return [pltpu.make_async_copy(src.at[j, pl.ds(off, rows), :], dst.at[pl.ds(rows * j, rows), :],
                                          wsem.at[N_DEV * n + j]) for j in range(N_DEV)]

        loads = (weight_copies(0, wo, gwb_hbm, FF_W, OUT_ROWS), weight_copies(1, wg, gwa_hbm, 0, FF_W),
                 weight_copies(2, wu, gwa_hbm, FF_W, FF_W), weight_copies(3, wd, gwb_hbm, 0, FF_W))

        @pl.when(i == 0)
        def _():
            for group in loads:
                for cp in group:
                    cp.start()
            dg2_ref[...] = jnp.zeros_like(dg2_ref)
            dgf_ref[...] = jnp.zeros_like(dgf_ref)
            loss_ref[...] = jnp.zeros_like(loss_ref)
            for group in loads:
                for cp in group:
                    cp.wait()

        g2v = g2_ref[...]
        gfv = gf_ref[...]
        x1 = x_ref[...] + _mm(mix_ref[...], wo[...])
        r2 = lax.rsqrt(jnp.mean(x1 * x1, axis=-1, keepdims=True) + EPS)
        n2 = x1 * r2
        h2 = (n2 * g2v).astype(BF16)
        hb_ref[1] = h2
        gate = _nt(h2, wg[...])
        up = _nt(h2, wu[...])
        sg = jax.nn.sigmoid(gate)
        sil = gate * sg
        act = (sil * up).astype(BF16)
        adu_ref[0] = act
        x2 = x1 + _mm(act, wd[...])
        rf = lax.rsqrt(jnp.mean(x2 * x2, axis=-1, keepdims=True) + EPS)
        nf = x2 * rf
        err = nf * gfv - tgt_ref[...]
        loss_ref[...] += 0.5 * jnp.sum(jnp.mean(err * err, axis=-1, keepdims=True))
        dy = err * (1.0 / D)
        dgf_ref[...] += jnp.sum(dy * nf, axis=0, keepdims=True)
        dnf = dy * gfv
        dx2 = rf * (dnf - nf * jnp.mean(dnf * nf, axis=-1, keepdims=True))
        dx2b = dx2.astype(BF16)
        hb_ref[0] = dx2b
        dact = _nt(dx2b, wd[...])
        dup = (dact * sil).astype(BF16)
        dgate = ((dact * up) * (sg * (1.0 + gate * (1.0 - sg)))).astype(BF16)
        adu_ref[2] = dup
        adu_ref[1] = dgate
        dh2 = _mm(dgate, wg[...]) + _mm(dup, wu[...])
        dg2_ref[...] += jnp.sum(dh2 * n2, axis=0, keepdims=True)
        dn2 = dh2 * g2v
        dx1 = dx2 + r2 * (dn2 - n2 * jnp.mean(dn2 * n2, axis=-1, keepdims=True))
        dx1_ref[...] = dx1
        dx1b = dx1.astype(BF16)
        dx1b_ref[...] = dx1b
        dmix_ref[...] = _nt(dx1b, wo[...])

    tile = lambda w: pl.BlockSpec((tm, w), lambda i: (i, 0))
    vec = pl.BlockSpec((1, D), lambda i: (0, 0))
    hbm = pl.BlockSpec(memory_space=pl.ANY)
    return pl.pallas_call(
        body,
        name="ffn_fwd_bwd",
        grid=(t // tm,),
        in_specs=[tile(D), tile(D), tile(D), vec, vec, hbm, hbm],
        out_specs=[tile(D), tile(D), tile(D), pl.BlockSpec((3, tm, DFF), lambda i: (0, i, 0)),
                   pl.BlockSpec((2, tm, D), lambda i: (0, i, 0)), vec, vec,
                   pl.BlockSpec((1, 128), lambda i: (0, 0))],
        out_shape=[
            jax.ShapeDtypeStruct((t, D), F32),
            jax.ShapeDtypeStruct((t, D), BF16),
            jax.ShapeDtypeStruct((t, D), F32),
            jax.ShapeDtypeStruct((3, t, DFF), BF16),
            jax.ShapeDtypeStruct((2, t, D), BF16),
            jax.ShapeDtypeStruct((1, D), F32),
            jax.ShapeDtypeStruct((1, D), F32),
            jax.ShapeDtypeStruct((1, 128), F32),
        ],
        scratch_shapes=[pltpu.VMEM((D, D), BF16), pltpu.VMEM((DFF, D), BF16), pltpu.VMEM((DFF, D), BF16),
                        pltpu.VMEM((DFF, D), BF16), pltpu.SemaphoreType.DMA((4 * N_DEV,))],
        compiler_params=_params(("arbitrary",)),
    )(mix2d, x2d, tgt2d, g2, gf, gwa, gwb)


def _stage1_rider(stack):
    n = stack.shape[0]
    return dict(inputs=[stack], out_shape=[jax.ShapeDtypeStruct((4,) + stack.shape[2:], BF16)] * n, nsem=4 * n,
                copies=lambda ins, outs, send, recv: _stage1_copies([ins[0]] * n, list(range(n)), outs, send, recv))


def _stage2_rider(pbs):
    return dict(inputs=list(pbs), out_shape=[jax.ShapeDtypeStruct(p.shape, BF16) for p in pbs], nsem=3 * len(pbs),
                copies=_stage2_copies)


def _tn_matmul(a, b, bm, bn, tk, name, with_bf16, rider=None):
    t, m = a.shape
    n = b.shape[1]
    nk = t // tk
    nout = 2 if with_bf16 else 1
    grid = (m // bm, n // bn, nk)
    r_in = [] if rider is None else rider["inputs"]
    r_out = [] if rider is None else rider["out_shape"]

    def body(a_ref, b_ref, *rest):
        ins, outs = rest[:len(r_in)], rest[len(r_in):len(r_in) + nout]
        r_outs, sems = rest[len(r_in) + nout:len(r_in) + nout + len(r_out)], rest[len(r_in) + nout + len(r_out):]
        o_ref = outs[0]
        i, j, k = pl.program_id(0), pl.program_id(1), pl.program_id(2)
        if rider is not None:
            @pl.when((i == 0) & (j == 0) & (k == 0))
            def _():
                for cp in rider["copies"](ins, r_outs, *sems):
                    cp.start()

        @pl.when(k == 0)
        def _():
            o_ref[...] = jnp.zeros_like(o_ref)

        o_ref[...] += _tn(a_ref[...].astype(BF16), b_ref[...].astype(BF16))
        if with_bf16:
            @pl.when(k == nk - 1)
            def _():
                outs[1][...] = o_ref[...].astype(BF16)
        if rider is not None:
            @pl.when((i == grid[0] - 1) & (j == grid[1] - 1) & (k == nk - 1))
            def _():
                copies = rider["copies"](ins, r_outs, *sems)
                for cp in copies:
                    cp.wait_recv()
                for cp in copies:
                    cp.wait_send()

    out_blk = pl.BlockSpec((bm, bn), lambda i, j, k: (i, j))
    hbm = pl.BlockSpec(memory_space=pl.ANY)
    out_shape = [jax.ShapeDtypeStruct((m, n), F32)] + ([jax.ShapeDtypeStruct((m, n), BF16)] if with_bf16 else [])
    res = pl.pallas_call(
        body,
        name=name,
        grid=grid,
        in_specs=[pl.BlockSpec((tk, bm), lambda i, j, k: (k, i)), pl.BlockSpec((tk, bn), lambda i, j, k: (k, j))]
        + [hbm] * len(r_in),
        out_specs=[out_blk] * nout + [hbm] * len(r_out),
        out_shape=out_shape + list(r_out),
        scratch_shapes=([] if rider is None else
                        [pltpu.SemaphoreType.DMA((rider["nsem"],)), pltpu.SemaphoreType.DMA((rider["nsem"],))]),
        compiler_params=_params(("parallel", "parallel", "arbitrary") if rider is None
                                else ("arbitrary", "arbitrary", "arbitrary")),
    )(a, b, *r_in)
    return res[0] if len(res) == 1 else res


def _dw_ffn(adu, hb, tk):
    _, t, _ = adu.shape
    bm = DFF // 2
    nk = t // tk

    def body(a_ref, b_ref, o_ref, ob_ref):
        k = pl.program_id(2)

        @pl.when(k == 0)
        def _():
            o_ref[...] = jnp.zeros_like(o_ref)

        o_ref[...] += _tn(a_ref[...], b_ref[...])

        @pl.when(k == nk - 1)
        def _():
            ob_ref[...] = o_ref[...].astype(BF16)

    out_blk = pl.BlockSpec((None, bm, D), lambda p, i, k: (p, i, 0))
    return pl.pallas_call(
        body,
        name="dw_ffn",
        grid=(3, DFF // bm, nk),
        in_specs=[pl.BlockSpec((None, tk, bm), lambda p, i, k: (p, k, i)),
                  pl.BlockSpec((None, tk, D), lambda p, i, k: (jnp.minimum(p, 1), k, 0))],
        out_specs=[out_blk, out_blk],
        out_shape=[jax.ShapeDtypeStruct((3, DFF, D), F32), jax.ShapeDtypeStruct((3, DFF, D), BF16)],
        compiler_params=_params(("arbitrary", "arbitrary", "arbitrary")),
    )(adu, hb)


def _at_owner(ref, lead, idx):
    return ref.at[idx] if lead is None else ref.at[lead, idx]


def _mix_bwd(proj3, z3, sprev, opre3, dmix3, gng, conv_w, wgu_p, pbs):
    nb, s, _ = proj3.shape
    nc = s // CHUNK
    na = len(pbs)

    def body(*refs):
        (p_ref, pprev_ref, z_ref, sp_ref, o_ref, dm_ref, gng_ref, cw_ref, wgu_ref) = refs[:9]
        pb_refs = refs[9:9 + na]
        (dproj_ref, dgng_ref, dcw_ref, dbg_ref, dwgu_ref) = refs[9 + na:14 + na]
        r2_refs = refs[14 + na:14 + 2 * na]
        ds_ref, dycn_ref, send_sems, recv_sems = refs[14 + 2 * na:]
        step = pl.program_id(0)
        n = nc - 1 - step

        @pl.when(step == 0)
        def _():
            for cp in _stage2_copies(pb_refs, r2_refs, send_sems, recv_sems):
                cp.start()
            ds_ref[...] = jnp.zeros_like(ds_ref)
            dycn_ref[...] = jnp.zeros_like(dycn_ref)
            dgng_ref[...] = jnp.zeros_like(dgng_ref)
            dcw_ref[...] = jnp.zeros_like(dcw_ref)
            dbg_ref[...] = jnp.zeros_like(dbg_ref)
            dwgu_ref[...] = jnp.zeros_like(dwgu_ref)

        r_i = lax.broadcasted_iota(jnp.int32, (CHUNK, CHUNK), 0)
        c_i = lax.broadcasted_iota(jnp.int32, (CHUNK, CHUNK), 1)
        tril16 = (r_i >= c_i).astype(BF16)
        triu16 = (r_i <= c_i).astype(BF16)
        causal = r_i >= c_i
        masks = _head_masks()
        cmask = _causal_stack_mask()
        gg = gng_ref[...]
        last_row = lax.broadcasted_iota(jnp.int32, (CHUNK, NQK), 0) == CHUNK - 1
        ones_r = jnp.ones((16, DV), BF16)
        has_prev = (n > 0).astype(F32)
        for b in range(nb):
            q = p_ref[b, :, OQ:OQ + NQK]
            k = p_ref[b, :, OK_:OK_ + NQK]
            z = z_ref[b]
            _, eb, enb, ekl, qi, ki, ks, decb = _chunk_fwd_parts(q, k, z, tril16)
            qi16 = qi.astype(BF16)
            ki16 = ki.astype(BF16)
            qs = _stack_heads(qi, masks).astype(BF16)
            sc = jnp.where(cmask, _nt(qs, ki16), 0.0).astype(BF16)
            st = sp_ref[b, 0]
            st16 = st.astype(BF16)
            dsn = ds_ref[b]
            dsn16 = dsn.astype(BF16)
            v16 = p_ref[b, :, OV:OV + NV].astype(BF16)
            do16 = []
            dgng = jnp.zeros((1, DV), F32)
            for h in range(HEADS):
                cols = slice(DV * h, DV * (h + 1))
                o = o_ref[b, :, cols]
                r = lax.rsqrt(jnp.mean(o * o, axis=-1, keepdims=True) + EPS)
                nh = o * r
                g = p_ref[b, :, OG + DV * h:OG + DV * (h + 1)]
                sg = jax.nn.sigmoid(g)
                dog = dm_ref[b, :, cols]
                dproj_ref[b, :, OG + DV * h:OG + DV * (h + 1)] = (
                    (dog * (nh * gg)) * (sg * (1.0 + g * (1.0 - sg)))).astype(BF16)
                don = dog * (g * sg)
                dgng = dgng + jnp.sum(don * nh, axis=0, keepdims=True)
                dn = don * gg
                do = r * (dn - nh * jnp.mean(dn * nh, axis=-1, keepdims=True))
                do16.append(do.astype(BF16))
            dgng_ref[...] += dgng
            do_rows = jnp.concatenate(do16, axis=0)
            v_rows = jnp.concatenate([v16[:, DV * h:DV * (h + 1)] for h in range(HEADS)], axis=0)
            dp16 = [jnp.where(causal, _nt(do16[h], v16[:, DV * h:DV * (h + 1)]), 0.0).astype(BF16)
                    for h in range(HEADS)]
            ks_dsn = _mm(_stack_heads(ks, masks).astype(BF16), dsn16)
            do_st = _nt(do_rows, st16)
            v_dsn = _nt(v_rows, dsn16)
            dp_ki = _mm(jnp.concatenate(dp16, axis=0), ki16)
            q_do = _tn(qi16, jnp.concatenate(do16, axis=1))
            dki_h = []
            for h in range(HEADS):
                rows = slice(CHUNK * h, CHUNK * (h + 1))
                cols = slice(DV * h, DV * (h + 1))
                dv = _tn(sc[rows], do16[h]) + ks_dsn[rows]
                dproj_ref[b, :, OV + DV * h:OV + DV * (h + 1)] = dv.astype(BF16)
                dki_h.append(_tn(dp16[h], qi16))
                ds_ref[b, rows, :] = decb[rows] * dsn[rows] + q_do[rows, cols]
            blocks = lambda a: [a[CHUNK * h:CHUNK * (h + 1)] for h in range(HEADS)]
            dqi = _merge_heads(blocks(dp_ki + do_st), masks)
            dki = _merge_heads(dki_h, masks)
            dks = _merge_heads(blocks(v_dsn), masks)
            dproj_ref[b, :, OQ:OQ + NQK] = (dqi * (Q_SCALE * eb)).astype(BF16)
            dproj_ref[b, :, OK_:OK_ + NQK] = (dki * enb + dks * ekl).astype(BF16)
            dks_ks = dks * ks
            db = dqi * qi - dki * ki - dks_ks
            sd = _split_bf16(dsn * st * decb, 2)
            dbl = jnp.sum(dks_ks, axis=0, keepdims=True) + (_nt(ones_r, sd[0]) + _nt(ones_r, sd[1]))[0:1, :]
            db = db + jnp.where(last_row, dbl, 0.0)
            db_parts = _split_bf16(db, 3)
            dla = _mm(triu16, db_parts[0]) + _mm(triu16, db_parts[1]) + _mm(triu16, db_parts[2])
            dz = (dla * INV_GATE_NORM) * (1.0 / (1.0 + jnp.exp(z)))
            dbg_ref[...] += jnp.sum(dz, axis=0, keepdims=True)
            dz16 = dz.astype(BF16)
            pa16 = p_ref[b, :, OA:OA + A_PAD].astype(BF16)
            dwgu_ref[...] += _tn(pa16, dz16)
            dproj_ref[b, :, OA:OA + A_PAD] = _nt(dz16, wgu_ref[...]).astype(BF16)
            cb = p_ref[b, :, OCB:OCB + CW]
            cc = p_ref[b, :, OCC:OCC + CW]
            ch = p_ref[b, :, OCH:OCH + CW]
            u = cc * ch
            uprev = (pprev_ref[b, :, 0:CW] * pprev_ref[b, :, CW:2 * CW]) * has_prev
            u1, u2 = _conv_taps(u, uprev)
            w0 = cw_ref[0:1, :]
            w1 = cw_ref[1:2, :]
            w2 = cw_ref[2:3, :]
            yc = w0 * u2 + w1 * u1 + w2 * u
            doc = dm_ref[b, :, NV:NV + CW]
            dproj_ref[b, :, OCB:OCB + CW] = (doc * yc).astype(BF16)
            dyc = doc * cb
            dycn = dycn_ref[b]
            row = lax.broadcasted_iota(jnp.int32, dyc.shape, 0)
            d1 = jnp.where(row >= CHUNK - 1, pltpu.roll(dycn, CHUNK - 1, 0), pltpu.roll(dyc, CHUNK - 1, 0))
            d2 = jnp.where(row >= CHUNK - 2, pltpu.roll(dycn, CHUNK - 2, 0), pltpu.roll(dyc, CHUNK - 2, 0))
            du = w2 * dyc + w1 * d1 + w0 * d2
            dproj_ref[b, :, OCC:OCC + CW] = (du * ch).astype(BF16)
            dproj_ref[b, :, OCH:OCH + CW] = (du * cc).astype(BF16)
            dcw_ref[0:1, :] += jnp.sum(dyc * u2, axis=0, keepdims=True)
            dcw_ref[1:2, :] += jnp.sum(dyc * u1, axis=0, keepdims=True)
            dcw_ref[2:3, :] += jnp.sum(dyc * u, axis=0, keepdims=True)
            dycn_ref[b] = dyc

        @pl.when(step == nc - 1)
        def _():
            copies = _stage2_copies(pb_refs, r2_refs, send_sems, recv_sems)
            for cp in copies:
                cp.wait_recv()
            for cp in copies:
                cp.wait_send()

    rev = lambda w: pl.BlockSpec((nb, CHUNK, w), lambda i: (0, nc - 1 - i, 0))
    const = lambda r, c: pl.BlockSpec((r, c), lambda i: (0, 0))
    hbm = pl.BlockSpec(memory_space=pl.ANY)
    return pl.pallas_call(
        body,
        name="mix_bwd",
        grid=(nc,),
        in_specs=[
            rev(PW),
            pl.BlockSpec((nb, CHUNK, 2 * CW), lambda i: (0, jnp.maximum(nc - 2 - i, 0), OCC // (2 * CW))),
            rev(NQK),
            pl.BlockSpec((nb, 1, NQK, DV), lambda i: (0, nc - 1 - i, 0, 0)),
            rev(NV),
            rev(D),
            const(1, DV),
            const(CONV_K, CW),
            const(A_PAD, NQK),
        ] + [hbm] * na,
        out_specs=[rev(PW), const(1, DV), const(8, CW), const(1, NQK), const(A_PAD, NQK)] + [hbm] * na,
        out_shape=[
            jax.ShapeDtypeStruct((nb, s, PW), BF16),
            jax.ShapeDtypeStruct((1, DV), F32),
            jax.ShapeDtypeStruct((8, CW), F32),
            jax.ShapeDtypeStruct((1, NQK), F32),
            jax.ShapeDtypeStruct((A_PAD, NQK), F32),
        ] + [jax.ShapeDtypeStruct((3,) + p.shape[1:], BF16) for p in pbs],
        scratch_shapes=[pltpu.VMEM((nb, NQK, DV), F32), pltpu.VMEM((nb, CHUNK, CW), F32),
                        pltpu.SemaphoreType.DMA((3 * na,)), pltpu.SemaphoreType.DMA((3 * na,))],
        compiler_params=_params(("arbitrary",)),
    )(proj3, proj3, z3, sprev, opre3, dmix3, gng, conv_w, wgu_p, *pbs)


SMALL_PACK_ROWS = 16


def _wgu_slot(r):
    return 4 + r // 4, NQK * (r % 4)


CONV_SLOTS = ((8, 0), (8, CW), (9, 0))


def _in_proj_bwd(dproj2d, x2d, dx1, g1, w_in_t, tm, pb, small_parts):
    t = x2d.shape[0]
    nt = t // tm

    def body(dp_ref, x_ref, dx1_ref, g_ref, w_ref, pb_ref, dg2, dgf, dbg, dgng, dwgu, dcw, lp,
             dx_ref, sums_ref, r2_ref, dg1_acc, pack, gbuf, pack1, gbuf1, send_sems, recv_sems,
             ssend, srecv, ssend1, srecv1):
        x, y, c = _position()
        me = 4 * x + 2 * y + c
        flips = [(k >> 2, (k >> 1) & 1, k & 1) for k in range(1, N_DEV)]
        peers = [(x ^ fx, y ^ fy, c ^ fc) for fx, fy, fc in flips]

        def small_copies(src, dst, send, recv, arrivals):
            return [pltpu.make_async_remote_copy(
                src_ref=src, dst_ref=dst.at[4 * px + 2 * py + pc if arrivals else me],
                send_sem=send.at[k], recv_sem=recv.at[k], device_id=(px, py, pc), device_id_type=MESH)
                for k, (px, py, pc) in enumerate(peers)]

        @pl.when(pl.program_id(0) == 0)
        def _():
            for cp in _stage2_copies([pb_ref], [r2_ref], send_sems, recv_sems):
                cp.start()
            dg1_acc[...] = jnp.zeros_like(dg1_acc)
            pack[...] = jnp.zeros_like(pack)
            pack[1:2, :] = dg2[...]
            pack[2:3, :] = dgf[...]
            pack[3:4, 0:NQK] = dbg[...]
            pack[3:4, NQK:NQK + DV] = dgng[...]
            pack[3:4, NQK + DV:NQK + 2 * DV] = lp[...]
            for r in range(RANK):
                row, lane = _wgu_slot(r)
                pack[row:row + 1, lane:lane + NQK] = dwgu[r:r + 1, :]
            for r, (row, lane) in enumerate(CONV_SLOTS):
                pack[row:row + 1, lane:lane + CW] = dcw[r:r + 1, :]
            for cp in small_copies(pack, gbuf, ssend, srecv, False):
                cp.start()
            gbuf[me] = pack[...]

        xv = x_ref[...]
        r = lax.rsqrt(jnp.mean(xv * xv, axis=-1, keepdims=True) + EPS)
        n1 = xv * r
        dh = _mm(dp_ref[...], w_ref[...])
        dg1_acc[...] += jnp.sum(dh * n1, axis=0, keepdims=True)
        dn = dh * g_ref[...]
        dx_ref[...] = dx1_ref[...] + r * (dn - n1 * jnp.mean(dn * n1, axis=-1, keepdims=True))

        @pl.when(pl.program_id(0) == nt - 1)
        def _():
            pack1[...] = jnp.zeros_like(pack1)
            pack1[0:1, :] = dg1_acc[...]
            for cp in small_copies(pack1, gbuf1, ssend1, srecv1, False):
                cp.start()
            gbuf1[me] = pack1[...]
            copies = _stage2_copies([pb_ref], [r2_ref], send_sems, recv_sems)
            for cp in copies:
                cp.wait_recv()
            for cp in copies:
                cp.wait_send()
            for src, dst, send, recv in ((pack, gbuf, ssend, srecv), (pack1, gbuf1, ssend1, srecv1)):
                for cp in small_copies(src, dst, send, recv, True):
                    cp.wait_recv()
                    cp.wait_send()
            acc = gbuf[0]
            acc1 = gbuf1[0]
            for d in range(1, N_DEV):
                acc = acc + gbuf[d]
                acc1 = acc1 + gbuf1[d]
            sums_ref[...] = acc
            sums_ref[0:1, :] = acc1[0:1, :]

    tile = lambda w: pl.BlockSpec((tm, w), lambda i: (i, 0))
    vec = pl.BlockSpec((1, D), lambda i: (0, 0))
    hbm = pl.BlockSpec(memory_space=pl.ANY)
    whole = lambda a: pl.BlockSpec(a.shape, lambda i: (0,) * a.ndim)
    return pl.pallas_call(
        body,
        name="in_proj_bwd",
        grid=(nt,),
        in_specs=[tile(PW), tile(D), tile(D), vec, pl.BlockSpec((PW, D), lambda i: (0, 0)), hbm]
        + [whole(a) for a in small_parts],
        out_specs=[tile(D), pl.BlockSpec((SMALL_PACK_ROWS, D), lambda i: (0, 0)), hbm],
        out_shape=[jax.ShapeDtypeStruct((t, D), F32), jax.ShapeDtypeStruct((SMALL_PACK_ROWS, D), F32),
                   jax.ShapeDtypeStruct((3,) + pb.shape[1:], BF16)],
        scratch_shapes=[pltpu.VMEM((1, D), F32),
                        pltpu.VMEM((SMALL_PACK_ROWS, D), F32), pltpu.VMEM((N_DEV, SMALL_PACK_ROWS, D), F32),
                        pltpu.VMEM((8, D), F32), pltpu.VMEM((N_DEV, 8, D), F32),
                        pltpu.SemaphoreType.DMA((3,)), pltpu.SemaphoreType.DMA((3,)),
                        pltpu.SemaphoreType.DMA((7,)), pltpu.SemaphoreType.DMA((7,)),
                        pltpu.SemaphoreType.DMA((7,)), pltpu.SemaphoreType.DMA((7,))],
        compiler_params=_params(("arbitrary",)),
    )(dproj2d, x2d, dx1, g1, w_in_t, pb, *small_parts)


def _adamw_math(w, g, m, v):
    m = ADAM_B1 * m + (1.0 - ADAM_B1) * g
    v = ADAM_B2 * v + (1.0 - ADAM_B2) * (g * g)
    m_hat = m / (1.0 - ADAM_B1 ** ADAM_STEP)
    v_hat = v / (1.0 - ADAM_B2 ** ADAM_STEP)
    delta = -ADAM_LR * (m_hat / (jnp.sqrt(v_hat) + ADAM_EPS) + ADAM_WD * w)
    return delta, m, v


def _position():
    return lax.axis_index("x"), lax.axis_index("y"), lax.axis_index("c")


def _prep_slab(w_it, w_gt, w_ut, w_d, w_o):
    def body(wi_ref, wg_ref, wu_ref, wd_ref, wo_ref, stage):
        stage[SLAB_IN:SLAB_IN + IN_W, :] = wi_ref[...].astype(BF16)
        stage[SLAB_IN + IN_W:SLAB_G, :] = jnp.zeros((IN_ROWS - IN_W, D), BF16)
        stage[SLAB_G:SLAB_U, :] = wg_ref[...].astype(BF16)
        stage[SLAB_U:SLAB_D, :] = wu_ref[...].astype(BF16)
        stage[SLAB_D:SLAB_O, :] = wd_ref[...].astype(BF16)
        stage[SLAB_O:SLAB_ROWS, :] = wo_ref[...].astype(BF16)

    vm = pl.BlockSpec(memory_space=pltpu.VMEM)
    return pl.pallas_call(
        body,
        name="prep_slab",
        in_specs=[vm] * 5,
        out_specs=vm,
        out_shape=jax.ShapeDtypeStruct((SLAB_ROWS, D), BF16),
        compiler_params=_params(),
    )(w_it, w_gt, w_ut, w_d, w_o)


GATHER_SEMS = 7


def _gather_copies(stage, lo, rows, gx, send_sems, recv_sems, local_sem):
    x, y, c = _position()
    me = (x, y, c)
    sibling = (x, y, 1 - c)
    chips = [(1 - x, y), (x, 1 - y), (1 - x, 1 - y)]
    src = stage.at[pl.ds(lo, rows), :]

    def blk(px, py, pc):
        return gx.at[4 * px + 2 * py + pc]

    def copy(k, block, to, from_stage=False):
        return pltpu.make_async_remote_copy(
            src_ref=src if from_stage else blk(*block), dst_ref=blk(*block),
            send_sem=send_sems.at[k], recv_sem=recv_sems.at[k], device_id=to, device_id_type=MESH)

    mine = pltpu.make_async_copy(src, blk(*me), local_sem)
    first = [copy(0, me, sibling, True)] + [copy(1 + j, me, (*chip, c), True) for j, chip in enumerate(chips)]
    passed = [copy(4 + j, (*chip, c), sibling) for j, chip in enumerate(chips)]
    arrivals = ([copy(0, sibling, me)] + [copy(1 + j, (*chip, c), me) for j, chip in enumerate(chips)]
                + [copy(4 + j, (*chip, 1 - c), me) for j, chip in enumerate(chips)])
    return mine, first, passed, arrivals


def _gather_start(*args):
    mine, first, _, _ = _gather_copies(*args)
    mine.start()
    for cp in first:
        cp.start()


def _gather_finish(*args):
    mine, first, passed, arrivals = _gather_copies(*args)
    for j in range(3):
        arrivals[1 + j].wait_recv()
        passed[j].start()
    arrivals[0].wait_recv()
    for j in range(3):
        arrivals[4 + j].wait_recv()
    for cp in first + passed:
        cp.wait_send()
    mine.wait()


def _gather_sems():
    return [pltpu.SemaphoreType.DMA((GATHER_SEMS,)), pltpu.SemaphoreType.DMA((GATHER_SEMS,)), pltpu.SemaphoreType.DMA]


def _gather_w_in(stage, wgu_s, conv_s):
    def body(stage_hbm, wgu_ref, conv_ref, w_ref, gwgu_ref, gconv_ref, buf, send_sems, recv_sems, local_sem,
             ssend, srecv):
        x, y, c = _position()
        me = 4 * x + 2 * y + c
        args = (stage_hbm, SLAB_IN, IN_ROWS, buf, send_sems, recv_sems, local_sem)
        _gather_start(*args)
        flips = [(k >> 2, (k >> 1) & 1, k & 1) for k in range(1, N_DEV)]
        peers = [(x ^ fx, y ^ fy, c ^ fc) for fx, fy, fc in flips]

        def small(k, block_id, to):
            return [pltpu.make_async_remote_copy(
                src_ref=s, dst_ref=g.at[block_id], send_sem=ssend.at[2 * k + n], recv_sem=srecv.at[2 * k + n],
                device_id=to, device_id_type=MESH)
                for n, (s, g) in enumerate(((wgu_ref, gwgu_ref), (conv_ref, gconv_ref)))]

        gwgu_ref[me] = wgu_ref[...]
        gconv_ref[me] = conv_ref[...]
        for k, peer in enumerate(peers):
            for cp in small(k, me, peer):
                cp.start()
        w_ref[IN_COLS:PW, :] = jnp.zeros((PW - IN_COLS, D), BF16)
        _gather_finish(*args)
        for k, (px, py, pc) in enumerate(peers):
            for cp in small(k, 4 * px + 2 * py + pc, (px, py, pc)):
                cp.wait_recv()
                cp.wait_send()
        for j, lo, hi, d in _in_segments():
            w_ref[d:d + hi - lo, :] = buf[j, lo:hi, :]

    vm = pl.BlockSpec(memory_space=pltpu.VMEM)
    hbm = pl.BlockSpec(memory_space=pl.ANY)
    return pl.pallas_call(
        body,
        name="gather_w_in",
        in_specs=[hbm, vm, vm],
        out_specs=[vm, vm, vm],
        out_shape=[jax.ShapeDtypeStruct((PW, D), BF16),
                   jax.ShapeDtypeStruct((N_DEV,) + wgu_s.shape, F32),
                   jax.ShapeDtypeStruct((N_DEV,) + conv_s.shape, F32)],
        scratch_shapes=[pltpu.VMEM((N_DEV, IN_ROWS, D), BF16)] + _gather_sems()
        + [pltpu.SemaphoreType.DMA((14,)), pltpu.SemaphoreType.DMA((14,))],
        compiler_params=_params(),
    )(stage, wgu_s, conv_s)


def _w_in_core_reduce(dw_t):
    def body(d_ref, own_ref, sib_ref, pb_ref, g, gb, r1, send_sems, recv_sems):
        x, y, c = _position()
        chip = 2 * x + y
        for j in range(N_DEV):
            g[j, IN_W:IN_ROWS, :] = jnp.zeros((IN_ROWS - IN_W, D), F32)
        for j, lo, hi, d in _in_segments():
            g[j, lo:hi, :] = d_ref[d:d + hi - lo, :]
        for j in range(N_DEV):
            gb[j] = g[j].astype(BF16)
        copies = _stage1_copies([gb], [None], [r1], send_sems, recv_sems)
        for cp in copies:
            cp.start()
        own_ref[0] = g[2 * chip + c]
        for cp in copies:
            cp.wait_recv()
        sib_ref[0] = r1[chip]
        for k in range(1, 4):
            t = chip ^ k
            pb_ref[k - 1] = (g[2 * t + c] + r1[t].astype(F32)).astype(BF16)
        for cp in copies:
            cp.wait_send()

    vm = pl.BlockSpec(memory_space=pltpu.VMEM)
    return pl.pallas_call(
        body,
        name="w_in_core_reduce",
        in_specs=[vm],
        out_specs=[vm, vm, vm],
        out_shape=[jax.ShapeDtypeStruct((1, IN_ROWS, D), F32), jax.ShapeDtypeStruct((1, IN_ROWS, D), BF16),
                   jax.ShapeDtypeStruct((3, IN_ROWS, D), BF16)],
        scratch_shapes=[pltpu.VMEM((N_DEV, IN_ROWS, D), F32), pltpu.VMEM((N_DEV, IN_ROWS, D), BF16),
                        pltpu.VMEM((4, IN_ROWS, D), BF16), pltpu.SemaphoreType.DMA((4,)),
                        pltpu.SemaphoreType.DMA((4,))],
        compiler_params=_params(),
    )(dw_t)


def _stage1_copies(g_refs, leads, r_refs, send_sems, recv_sems):
    x, y, c = _position()
    return [pltpu.make_async_remote_copy(
        src_ref=_at_owner(g_refs[a], leads[a], 2 * i + 1 - c), dst_ref=r_refs[a].at[i],
        send_sem=send_sems.at[4 * a + i], recv_sem=recv_sems.at[4 * a + i],
        device_id=(x, y, 1 - c), device_id_type=MESH) for a in range(len(g_refs)) for i in range(4)]


def _ffn_core_reduce(dw3, r1_ffn, dw_o, dwb_o, pos_arr):
    def body(pos_ref, g0, g1, g2, ra, rb, rc, go, go_own, gbo_hbm, p0, p1, p2, po, own_ref, sib_ref,
             r1o, send_sems, recv_sems):
        k = pl.program_id(0)
        chip = pos_ref[1]

        @pl.when(k == 0)
        def _():
            copies = _stage1_copies([gbo_hbm], [None], [r1o], send_sems, recv_sems)
            for cp in copies:
                cp.start()
            for cp in copies:
                cp.wait_recv()

        for g, r, p in ((g0, ra, p0), (g1, rb, p1), (g2, rc, p2)):
            p[...] = (g[...] + r[...].astype(F32)).astype(BF16)
        po[0] = (go[0] + r1o[chip ^ (k + 1)].astype(F32)).astype(BF16)

        @pl.when(k == 2)
        def _():
            own_ref[...] = go_own[...]
            sib_ref[0] = r1o[chip]
            for cp in _stage1_copies([gbo_hbm], [None], [r1o], send_sems, recv_sems):
                cp.wait_send()

    other = lambda k, pos: 2 * (pos[1] ^ (k + 1)) + pos[0]
    g_spec = lambda lead: pl.BlockSpec((None, 1, FF_W, D), lambda k, pos: (lead, other(k, pos), 0, 0))
    r_spec = pl.BlockSpec((1, FF_W, D), lambda k, pos: (pos[1] ^ (k + 1), 0, 0))
    slot = lambda rows: pl.BlockSpec((1, rows, D), lambda k, pos: (k, 0, 0))
    one = pl.BlockSpec((1, OUT_ROWS, D), lambda k, pos: (0, 0, 0))
    return pl.pallas_call(
        body,
        name="ffn_core_reduce",
        grid_spec=pltpu.PrefetchScalarGridSpec(
            num_scalar_prefetch=1, grid=(3,),
            in_specs=[g_spec(0), g_spec(1), g_spec(2), r_spec, r_spec, r_spec,
                      pl.BlockSpec((1, OUT_ROWS, D), lambda k, pos: (other(k, pos), 0, 0)),
                      pl.BlockSpec((1, OUT_ROWS, D), lambda k, pos: (2 * pos[1] + pos[0], 0, 0)),
                      pl.BlockSpec(memory_space=pl.ANY)],
            out_specs=[slot(FF_W), slot(FF_W), slot(FF_W), slot(OUT_ROWS), one, one],
            scratch_shapes=[pltpu.VMEM((4, OUT_ROWS, D), BF16), pltpu.SemaphoreType.DMA((4,)),
                            pltpu.SemaphoreType.DMA((4,))]),
        out_shape=[jax.ShapeDtypeStruct((3, FF_W, D), BF16)] * 3 + [
            jax.ShapeDtypeStruct((3, OUT_ROWS, D), BF16), jax.ShapeDtypeStruct((1, OUT_ROWS, D), F32),
            jax.ShapeDtypeStruct((1, OUT_ROWS, D), BF16)],
        compiler_params=_params(("arbitrary",)),
    )(pos_arr, dw3, dw3, dw3, *r1_ffn, dw_o, dw_o, dwb_o)


def _stage2_copies(p_refs, r_refs, send_sems, recv_sems):
    x, y, c = _position()
    copies = []
    for a in range(len(p_refs)):
        for k in range(1, 4):
            copies.append(pltpu.make_async_remote_copy(
                src_ref=p_refs[a].at[k - 1], dst_ref=r_refs[a].at[k - 1],
                send_sem=send_sems.at[3 * a + k - 1], recv_sem=recv_sems.at[3 * a + k - 1],
                device_id=(x ^ (k >> 1), y ^ (k & 1), c), device_id_type=MESH))
    return copies


def _finish_weight(g8, r1, r2, w, m, v, pos_arr, name, lead=None):
    rows = g8.shape[-2]
    wr = w.shape[0]
    nblk = 2 if (rows == wr and rows % 32 == 0) else 1
    rb = rows // nblk
    wb = wr // nblk
    own = g8.shape[0] == 1
    if own:
        g_spec = pl.BlockSpec((1, rb, D), lambda i, pos: (0, i, 0))
    elif lead is None:
        g_spec = pl.BlockSpec((1, rb, D), lambda i, pos: (2 * pos[1] + pos[0], i, 0))
    else:
        g_spec = pl.BlockSpec((None, 1, rb, D), lambda i, pos: (lead, 2 * pos[1] + pos[0], i, 0))
    r1_spec = pl.BlockSpec((1, rb, D), (lambda i, pos: (0, i, 0)) if own else (lambda i, pos: (pos[1], i, 0)))

    def body(pos_ref, g_ref, r1_ref, r2_ref, w_ref, m_ref, v_ref, g_out, d_out, m_out, v_out):
        g = g_ref[0] + r1_ref[0].astype(F32)
        for k in range(3):
            g = g + r2_ref[k].astype(F32)
        g = g[0:wb, :]
        g_out[...] = g
        d, mn, vn = _adamw_math(w_ref[...], g, m_ref[...], v_ref[...])
        d_out[...] = d
        m_out[...] = mn
        v_out[...] = vn

    wblk = pl.BlockSpec((wb, D), lambda i, pos: (i, 0))
    shp = jax.ShapeDtypeStruct(w.shape, F32)
    return pl.pallas_call(
        body,
        name=name,
        grid_spec=pltpu.PrefetchScalarGridSpec(
            num_scalar_prefetch=1, grid=(nblk,),
            in_specs=[g_spec, r1_spec,
                      pl.BlockSpec((3, rb, D), lambda i, pos: (0, i, 0)), wblk, wblk, wblk],
            out_specs=[wblk] * 4),
        out_shape=[shp, shp, shp, shp],
        compiler_params=_params(("arbitrary",)),
    )(pos_arr, g8, r1, r2, w, m, v)


SMALL_NAMES = ("norm1_g", "norm2_g", "norm_f_g", "b_gate", "gla_norm_g", "w_gate_up", "conv_w")
WGU_W = NQK // N_DEV
CONV_W = CW // N_DEV


def _small_adamw(sums, ws, ms, vs):
    n = len(SMALL_NAMES)

    def body(*refs):
        acc_ref = refs[0]
        w_refs, m_refs, v_refs = refs[1:1 + n], refs[1 + n:1 + 2 * n], refs[1 + 2 * n:1 + 3 * n]
        loss_ref = refs[1 + 3 * n]
        outs = refs[2 + 3 * n:]
        x, y, c = _position()
        me = 4 * x + 2 * y + c
        acc = acc_ref[...]
        loss_ref[...] = acc[3:4, NQK + DV:NQK + DV + 1]

        def my_columns(full, width):
            r = lax.broadcasted_iota(jnp.int32, (full.shape[1], width), 0)
            col = lax.broadcasted_iota(jnp.int32, (full.shape[1], width), 1)
            sel = (r == width * me + col).astype(F32)
            return _mm(full, sel, precision=HIGHEST)

        dwgu = jnp.concatenate([acc[row:row + 1, lane:lane + NQK] for row, lane in map(_wgu_slot, range(RANK))], axis=0)
        dcw = jnp.concatenate([acc[row:row + 1, lane:lane + CW] for row, lane in CONV_SLOTS], axis=0)
        grads = [acc[0:1, :], acc[1:2, :], acc[2:3, :], acc[3:4, 0:NQK], acc[3:4, NQK:NQK + DV],
                 my_columns(dwgu, WGU_W), my_columns(dcw, CONV_W)]
        for i, g in enumerate(grads):
            d, mn, vn = _adamw_math(w_refs[i][...], g, m_refs[i][...], v_refs[i][...])
            outs[4 * i][...] = g
            outs[4 * i + 1][...] = d
            outs[4 * i + 2][...] = mn
            outs[4 * i + 3][...] = vn

    vm = pl.BlockSpec(memory_space=pltpu.VMEM)
    out_shape = [jax.ShapeDtypeStruct((1, 1), F32)]
    for w in ws:
        out_shape += [jax.ShapeDtypeStruct(w.shape, F32)] * 4
    return pl.pallas_call(
        body,
        name="small_adamw",
        in_specs=[vm] * (1 + 3 * n),
        out_specs=[vm] * (1 + 4 * n),
        out_shape=out_shape,
        compiler_params=_params(),
    )(sums, *ws, *ms, *vs)


def kernel(x, norm1_g, w_in, w_gate_up, b_gate, gla_norm_g, conv_w, w_out, norm2_g, w_ffn_gate, w_ffn_up, w_ffn_down, norm_f_g, loss_target, m_norm1_g, m_w_in, m_w_gate_up, m_b_gate, m_gla_norm_g, m_conv_w, m_w_out, m_norm2_g, m_w_ffn_gate, m_w_ffn_up, m_w_ffn_down, m_norm_f_g, v_norm1_g, v_w_in, v_w_gate_up, v_b_gate, v_gla_norm_g, v_conv_w, v_w_out, v_norm2_g, v_w_ffn_gate, v_w_ffn_up, v_w_ffn_down, v_norm_f_g):
    xi, yi, ci = _position()
    pos_arr = jnp.stack([ci, 2 * xi + yi]).astype(jnp.int32)
    nb, s, _ = x.shape
    t = nb * s

    tr = lambda a: a[0].T
    stage = _prep_slab(tr(w_in), tr(w_ffn_gate), tr(w_ffn_up), w_ffn_down[0], w_out[0])
    w_in_t, gwgu, gconv = _gather_w_in(stage, w_gate_up[0], conv_w[0])
    wgu_f = gwgu.transpose(1, 0, 2).reshape(RANK, NQK)
    conv_f = gconv.transpose(1, 0, 2).reshape(CONV_K, CW)
    wgu_p = jnp.concatenate([wgu_f, jnp.zeros((A_PAD - RANK, NQK), F32)], axis=0).astype(BF16)

    x2d = x.reshape(t, D)
    tgt2d = loss_target.reshape(t, D)
    tm = 256
    tm_in = min(512, t)
    tk = min(2048, t)
    proj, z, h, gwb = _in_proj_fwd(x2d, norm1_g, w_in_t, wgu_p, b_gate, tm_in, stage)
    proj3 = proj.reshape(nb, s, PW)
    z3 = z.reshape(nb, s, NQK)
    mix3, opre3, sprev, gwa = _mix_fwd(proj3, z3, gla_norm_g, conv_f, stage)
    mix2d = mix3.reshape(t, D)
    dx1, dx1b, dmix, adu, hb, dg2, dgf, loss_part = _ffn_fwd_bwd(
        mix2d, x2d, tgt2d, gwa, gwb, norm2_g, norm_f_g.reshape(1, D), tm)
    dw3, dwb3 = _dw_ffn(adu, hb, tk)
    dw3 = dw3.reshape(3, N_DEV, FF_W, D)
    dw_o, dwb_o, *r1_ffn = _tn_matmul(mix2d, dx1b, D // 4, D, tk, "dw_out", True,
                                      _stage1_rider(dwb3.reshape(3, N_DEV, FF_W, D)))
    *pb, o_own, o_sib = _ffn_core_reduce(dw3, r1_ffn, dw_o.reshape(N_DEV, OUT_ROWS, D),
                                         dwb_o.reshape(N_DEV, OUT_ROWS, D), pos_arr)
    g8 = [dw3, dw3, dw3, o_own]
    leads = [0, 1, 2, None]
    tags = ("w_ffn_down", "w_ffn_gate", "w_ffn_up", "w_out")
    r1 = list(r1_ffn) + [o_sib]
    mb = _mix_bwd(proj3, z3, sprev, opre3, dmix.reshape(nb, s, D), gla_norm_g, conv_f, wgu_p, [pb[0], pb[1], pb[3]])
    dproj3, dgng, dcw, dbg, dwgu = mb[:5]
    dproj2d = dproj3.reshape(t, PW)
    dw_in_t, r2_up = _tn_matmul(dproj2d, h, PW // 5, D, tk, "dw_in", False, _stage2_rider([pb[2]]))
    r2 = [mb[5], mb[6], r2_up, mb[7]]
    g_in, r1_in, pb_in = _w_in_core_reduce(dw_in_t)
    dx, small_sums, r2_in = _in_proj_bwd(dproj2d, x2d, dx1, norm1_g, w_in_t, tm_in, pb_in,
                                         (dg2, dgf, dbg, dgng, dwgu, dcw, loss_part))

    tags = ("w_in",) + tags
    g8 = [g_in] + g8
    leads = [None] + leads
    r1 = [r1_in] + list(r1)
    r2 = [r2_in] + r2
    shard_w = (tr(w_in), w_ffn_down[0], tr(w_ffn_gate), tr(w_ffn_up), w_out[0])
    shard_m = (tr(m_w_in), m_w_ffn_down[0], tr(m_w_ffn_gate), tr(m_w_ffn_up), m_w_out[0])
    shard_v = (tr(v_w_in), v_w_ffn_down[0], tr(v_w_ffn_gate), tr(v_w_ffn_up), v_w_out[0])
    transposed = (True, False, True, True, False)
    results = {}
    for tag, g, lead, ra, rb, w, m, v, tp in zip(tags, g8, leads, r1, r2, shard_w, shard_m, shard_v, transposed):
        outs = _finish_weight(g, ra, rb, w, m, v, pos_arr, "finish_" + tag, lead)
        results[tag] = [o.T[None] if tp else o[None] for o in outs]

    small_w = (norm1_g, norm2_g, norm_f_g.reshape(1, D), b_gate, gla_norm_g, w_gate_up[0], conv_w[0])
    small_m = (m_norm1_g, m_norm2_g, m_norm_f_g.reshape(1, D), m_b_gate, m_gla_norm_g, m_w_gate_up[0], m_conv_w[0])
    small_v = (v_norm1_g, v_norm2_g, v_norm_f_g.reshape(1, D), v_b_gate, v_gla_norm_g, v_w_gate_up[0], v_conv_w[0])
    so = _small_adamw(small_sums, small_w, small_m, small_v)
    loss = so[0].reshape(())
    shapes = {"norm_f_g": (D,), "w_gate_up": (1, RANK, WGU_W), "conv_w": (1, CONV_K, CONV_W)}
    for i, name in enumerate(SMALL_NAMES):
        results[name] = [o.reshape(shapes[name]) if name in shapes else o for o in so[1 + 4 * i:5 + 4 * i]]

    names = ("norm1_g", "w_in", "w_gate_up", "b_gate", "gla_norm_g", "conv_w", "w_out", "norm2_g",
             "w_ffn_gate", "w_ffn_up", "w_ffn_down", "norm_f_g")
    outs = [loss, dx.reshape(nb, s, D)]
    for kind in range(4):
        for name in names:
            outs.append(results[name][kind])
    return tuple(outs)
```

```python
import jax
import jax.numpy as jnp
from jax import lax
from jax.experimental import pallas as pl
from jax.experimental.pallas import tpu as pltpu

F32 = jnp.float32
BF16 = jnp.bfloat16
HIGHEST = lax.Precision.HIGHEST
MESH = pl.DeviceIdType.MESH

N_DEV = 8
D = 1024
DFF = 2816
HEADS = 4
DK = 64
DV = 128
NQK = HEADS * DK
NV = HEADS * DV
RANK = 16
CHUNK = 64
CW = 512
CONV_K = 3
IN_COLS = 3088
EPS = 1e-6
INV_GATE_NORM = 1.0 / 16.0
Q_SCALE = DK ** -0.5

PW = 3200
OQ, OK_, OV, OG, OCB, OCC, OCH, OA = 0, 256, 512, 1024, 1536, 2048, 2560, 3072
A_PAD = 128

ADAM_LR = 0.001
ADAM_B1 = 0.9
ADAM_B2 = 0.999
ADAM_EPS = 1e-08
ADAM_WD = 0.01
ADAM_STEP = 10

IN_W = IN_COLS // N_DEV
IN_ROWS = 400
FF_W = DFF // N_DEV
OUT_ROWS = D // N_DEV
SLAB_IN = 0
SLAB_G = SLAB_IN + IN_ROWS
SLAB_U = SLAB_G + FF_W
SLAB_D = SLAB_U + FF_W
SLAB_O = SLAB_D + FF_W
SLAB_ROWS = SLAB_O + OUT_ROWS

VMEM_LIMIT = 56 * 1024 * 1024


def _params(sem=None, vmem=VMEM_LIMIT):
    return pltpu.CompilerParams(dimension_semantics=sem, vmem_limit_bytes=vmem)


def _nt(a, b):
    return lax.dot_general(a, b, (((1,), (1,)), ((), ())), preferred_element_type=F32)


def _tn(a, b, precision=None):
    return lax.dot_general(a, b, (((0,), (0,)), ((), ())), preferred_element_type=F32, precision=precision)


def _mm(a, b, precision=None):
    return jnp.dot(a, b, preferred_element_type=F32, precision=precision)


def _in_segments():
    segs = []
    for j in range(N_DEV):
        lo, hi = IN_W * j, IN_W * (j + 1)
        cuts = sorted({lo, hi} | {c for c in (OCB, OCB + RANK) if lo < c < hi})
        for a, b in zip(cuts[:-1], cuts[1:]):
            if a < OCB:
                d = a
            elif a < OCB + RANK:
                d = OA + (a - OCB)
            else:
                d = a - RANK
            segs.append((j, a - lo, b - lo, d))
    return segs


def _in_proj_fwd(x2d, g1, w_in_t, wgu_p, b_gate, tm, stage):
    t = x2d.shape[0]
    nt = t // tm
    g_rows = SLAB_ROWS - SLAB_D

    def body(x_ref, g_ref, w_ref, wgu_ref, bg_ref, stage_hbm, proj_ref, z_ref, h_ref, gwb_ref,
             send_sems, recv_sems, local_sem):
        gargs = (stage_hbm, SLAB_D, g_rows, gwb_ref, send_sems, recv_sems, local_sem)

        @pl.when(pl.program_id(0) == 0)
        def _():
            _gather_start(*gargs)

        x = x_ref[...]
        r = lax.rsqrt(jnp.mean(x * x, axis=-1, keepdims=True) + EPS)
        h = ((x * r) * g_ref[...]).astype(BF16)
        h_ref[...] = h
        proj = _nt(h, w_ref[...])
        proj_ref[...] = proj
        pa = proj[:, OA:OA + A_PAD].astype(BF16)
        z_ref[...] = _mm(pa, wgu_ref[...]) + bg_ref[...]

        @pl.when(pl.program_id(0) == nt - 1)
        def _():
            _gather_finish(*gargs)

    return pl.pallas_call(
        body,
        name="in_proj_fwd",
        grid=(t // tm,),
        in_specs=[
            pl.BlockSpec((tm, D), lambda i: (i, 0)),
            pl.BlockSpec((1, D), lambda i: (0, 0)),
            pl.BlockSpec((PW, D), lambda i: (0, 0)),
            pl.BlockSpec((A_PAD, NQK), lambda i: (0, 0)),
            pl.BlockSpec((1, NQK), lambda i: (0, 0)),
            pl.BlockSpec(memory_space=pl.ANY),
        ],
        out_specs=[
            pl.BlockSpec((tm, PW), lambda i: (i, 0)),
            pl.BlockSpec((tm, NQK), lambda i: (i, 0)),
            pl.BlockSpec((tm, D), lambda i: (i, 0)),
            pl.BlockSpec(memory_space=pl.ANY),
        ],
        out_shape=[
            jax.ShapeDtypeStruct((t, PW), F32),
            jax.ShapeDtypeStruct((t, NQK), F32),
            jax.ShapeDtypeStruct((t, D), BF16),
            jax.ShapeDtypeStruct((N_DEV, g_rows, D), BF16),
        ],
        scratch_shapes=_gather_sems(),
        compiler_params=_params(("arbitrary",)),
    )(x2d, g1, w_in_t, wgu_p, b_gate, stage)


def _head_masks():
    lane = lax.broadcasted_iota(jnp.int32, (1, NQK), 1)
    return [(lane >= DK * h) & (lane < DK * (h + 1)) for h in range(HEADS)]


def _split_bf16(x, n):
    parts = []
    for _ in range(n):
        p = x.astype(BF16)
        parts.append(p)
        x = x - p.astype(F32)
    return parts


def _chunk_fwd_parts(q, k, z, tril16):
    la = (jnp.minimum(z, 0.0) - jnp.log1p(jnp.exp(-jnp.abs(z)))) * INV_GATE_NORM
    la_parts = _split_bf16(la, 3)
    bc = _mm(tril16, la_parts[0]) + _mm(tril16, la_parts[1]) + _mm(tril16, la_parts[2])
    bl = bc[CHUNK - 1:CHUNK, :]
    eb = jnp.exp(bc)
    enb = jnp.exp(-bc)
    ekl = jnp.exp(bl - bc)
    qi = (q * Q_SCALE) * eb
    ki = k * enb
    ks = k * ekl
    ones16 = jnp.ones((CHUNK, DV), BF16)
    decb = jnp.exp(_tn(la_parts[0], ones16) + _tn(la_parts[1], ones16) + _tn(la_parts[2], ones16))
    return la, eb, enb, ekl, qi, ki, ks, decb


def _stack_heads(a, masks):
    return jnp.concatenate([jnp.where(m, a, 0.0) for m in masks], axis=0)


def _merge_heads(blocks, masks):
    out = blocks[HEADS - 1]
    for h in range(HEADS - 2, -1, -1):
        out = jnp.where(masks[h], blocks[h], out)
    return out


def _causal_stack_mask():
    row = lax.broadcasted_iota(jnp.int32, (HEADS * CHUNK, CHUNK), 0)
    col = lax.broadcasted_iota(jnp.int32, (HEADS * CHUNK, CHUNK), 1)
    return (row & (CHUNK - 1)) >= col


def _conv_taps(u, uprev):
    row = lax.broadcasted_iota(jnp.int32, u.shape, 0)
    u1 = jnp.where(row < 1, pltpu.roll(uprev, 1, 0), pltpu.roll(u, 1, 0))
    u2 = jnp.where(row < 2, pltpu.roll(uprev, 2, 0), pltpu.roll(u, 2, 0))
    return u1, u2


def _mix_fwd(proj3, z3, gng, conv_w, stage):
    nb, s, _ = proj3.shape
    nc = s // CHUNK
    g_rows = SLAB_D - SLAB_G

    def body(p_ref, z_ref, gng_ref, cw_ref, stage_hbm, mix_ref, o_ref, sprev_ref, gwa_ref, s_ref, uprev_ref,
             send_sems, recv_sems, local_sem):
        n = pl.program_id(0)
        gargs = (stage_hbm, SLAB_G, g_rows, gwa_ref, send_sems, recv_sems, local_sem)

        @pl.when(n == 0)
        def _():
            _gather_start(*gargs)
            s_ref[...] = jnp.zeros_like(s_ref)
            uprev_ref[...] = jnp.zeros_like(uprev_ref)

        r_i = lax.broadcasted_iota(jnp.int32, (CHUNK, CHUNK), 0)
        c_i = lax.broadcasted_iota(jnp.int32, (CHUNK, CHUNK), 1)
        tril16 = (r_i >= c_i).astype(BF16)
        masks = _head_masks()
        cmask = _causal_stack_mask()
        gg = gng_ref[...]
        for b in range(nb):
            q = p_ref[b, :, OQ:OQ + NQK]
            k = p_ref[b, :, OK_:OK_ + NQK]
            _, _, _, _, qi, ki, ks, decb = _chunk_fwd_parts(q, k, z_ref[b], tril16)
            qs = _stack_heads(qi, masks).astype(BF16)
            sc = jnp.where(cmask, _nt(qs, ki.astype(BF16)), 0.0).astype(BF16)
            st = s_ref[b]
            sprev_ref[b, 0] = st
            o_inter = _mm(qs, st.astype(BF16))
            v16 = p_ref[b, :, OV:OV + NV].astype(BF16)
            kv = _tn(ks.astype(BF16), v16)
            for h in range(HEADS):
                rows = slice(CHUNK * h, CHUNK * (h + 1))
                cols = slice(DV * h, DV * (h + 1))
                o = _mm(sc[rows], v16[:, cols]) + o_inter[rows]
                o_ref[b, :, cols] = o
                r = lax.rsqrt(jnp.mean(o * o, axis=-1, keepdims=True) + EPS)
                on = (o * r) * gg
                g = p_ref[b, :, OG + DV * h:OG + DV * (h + 1)]
                mix_ref[b, :, cols] = (on * (g * jax.nn.sigmoid(g))).astype(BF16)
                s_ref[b, rows, :] = decb[rows] * st[rows] + kv[rows, cols]
            u = p_ref[b, :, OCC:OCC + CW] * p_ref[b, :, OCH:OCH + CW]
            u1, u2 = _conv_taps(u, uprev_ref[b])
            yc = cw_ref[0:1, :] * u2 + cw_ref[1:2, :] * u1 + cw_ref[2:3, :] * u
            mix_ref[b, :, NV:NV + CW] = (p_ref[b, :, OCB:OCB + CW] * yc).astype(BF16)
            uprev_ref[b] = u

        @pl.when(n == nc - 1)
        def _():
            _gather_finish(*gargs)

    return pl.pallas_call(
        body,
        name="mix_fwd",
        grid=(nc,),
        in_specs=[
            pl.BlockSpec((nb, CHUNK, PW), lambda n: (0, n, 0)),
            pl.BlockSpec((nb, CHUNK, NQK), lambda n: (0, n, 0)),
            pl.BlockSpec((1, DV), lambda n: (0, 0)),
            pl.BlockSpec((CONV_K, CW), lambda n: (0, 0)),
            pl.BlockSpec(memory_space=pl.ANY),
        ],
        out_specs=[
            pl.BlockSpec((nb, CHUNK, D), lambda n: (0, n, 0)),
            pl.BlockSpec((nb, CHUNK, NV), lambda n: (0, n, 0)),
            pl.BlockSpec((nb, 1, NQK, DV), lambda n: (0, n, 0, 0)),
            pl.BlockSpec(memory_space=pl.ANY),
        ],
        out_shape=[
            jax.ShapeDtypeStruct((nb, s, D), BF16),
            jax.ShapeDtypeStruct((nb, s, NV), F32),
            jax.ShapeDtypeStruct((nb, nc, NQK, DV), F32),
            jax.ShapeDtypeStruct((N_DEV, g_rows, D), BF16),
        ],
        scratch_shapes=[pltpu.VMEM((nb, NQK, DV), F32), pltpu.VMEM((nb, CHUNK, CW), F32)] + _gather_sems(),
        compiler_params=_params(("arbitrary",)),
    )(proj3, z3, gng, conv_w, stage)


def _ffn_fwd_bwd(mix2d, x2d, tgt2d, gwa, gwb, g2, gf, tm):
    t = x2d.shape[0]

    def body(mix_ref, x_ref, tgt_ref, g2_ref, gf_ref, gwa_hbm, gwb_hbm,
             dx1_ref, dx1b_ref, dmix_ref, adu_ref, hb_ref, dg2_ref, dgf_ref, loss_ref,
             wo, wg, wu, wd, wsem):
        i = pl.program_id(0)

        def weight_copies(n, dst, src, off, rows):
            return [pltpu.make_async_copy(src.at[j, pl.ds(off, rows), :], dst.at[pl.ds(rows * j, rows), :],
                                          wsem.at[N_DEV * n + j]) for j in range(N_DEV)]

        loads = (weight_copies(0, wo, gwb_hbm, FF_W, OUT_ROWS), weight_copies(1, wg, gwa_hbm, 0, FF_W),
                 weight_copies(2, wu, gwa_hbm, FF_W, FF_W), weight_copies(3, wd, gwb_hbm, 0, FF_W))

        def tile(arrive):
            g2v = g2_ref[...]
            gfv = gf_ref[...]
            arrive(0)
            x1 = x_ref[...] + _mm(mix_ref[...], wo[...])
            r2 = lax.rsqrt(jnp.mean(x1 * x1, axis=-1, keepdims=True) + EPS)
            n2 = x1 * r2
            h2 = (n2 * g2v).astype(BF16)
            hb_ref[1] = h2
            arrive(1)
            gate = _nt(h2, wg[...])
            arrive(2)
            up = _nt(h2, wu[...])
            sg = jax.nn.sigmoid(gate)
            sil = gate * sg
            act = (sil * up).astype(BF16)
            adu_ref[0] = act
            arrive(3)
            x2 = x1 + _mm(act, wd[...])
            rf = lax.rsqrt(jnp.mean(x2 * x2, axis=-1, keepdims=True) + EPS)
            nf = x2 * rf
            err = nf * gfv - tgt_ref[...]
            loss_ref[...] += 0.5 * jnp.sum(jnp.mean(err * err, axis=-1, keepdims=True))
            dy = err * (1.0 / D)
            dgf_ref[...] += jnp.sum(dy * nf, axis=0, keepdims=True)
            dnf = dy * gfv
            dx2 = rf * (dnf - nf * jnp.mean(dnf * nf, axis=-1, keepdims=True))
            dx2b = dx2.astype(BF16)
            hb_ref[0] = dx2b
            dact = _nt(dx2b, wd[...])
            dup = (dact * sil).astype(BF16)
            dgate = ((dact * up) * (sg * (1.0 + gate * (1.0 - sg)))).astype(BF16)
            adu_ref[2] = dup
            adu_ref[1] = dgate
            dh2 = _mm(dgate, wg[...]) + _mm(dup, wu[...])
            dg2_ref[...] += jnp.sum(dh2 * n2, axis=0, keepdims=True)
            dn2 = dh2 * g2v
            dx1 = dx2 + r2 * (dn2 - n2 * jnp.mean(dn2 * n2, axis=-1, keepdims=True))
            dx1_ref[...] = dx1
            dx1b = dx1.astype(BF16)
            dx1b_ref[...] = dx1b
            dmix_ref[...] = _nt(dx1b, wo[...])

        @pl.when(i == 0)
        def _():
            for group in loads:
                for cp in group:
                    cp.start()
            dg2_ref[...] = jnp.zeros_like(dg2_ref)
            dgf_ref[...] = jnp.zeros_like(dgf_ref)
            loss_ref[...] = jnp.zeros_like(loss_ref)

            def arrive(n):
                for cp in loads[n]:
                    cp.wait()

            tile(arrive)

        @pl.when(i > 0)
        def _():
            tile(lambda n: None)

    tile = lambda w: pl.BlockSpec((tm, w), lambda i: (i, 0))
    vec = pl.BlockSpec((1, D), lambda i: (0, 0))
    hbm = pl.BlockSpec(memory_space=pl.ANY)
    return pl.pallas_call(
        body,
        name="ffn_fwd_bwd",
        grid=(t // tm,),
        in_specs=[tile(D), tile(D), tile(D), vec, vec, hbm, hbm],
        out_specs=[tile(D), tile(D), tile(D), pl.BlockSpec((3, tm, DFF), lambda i: (0, i, 0)),
                   pl.BlockSpec((2, tm, D), lambda i: (0, i, 0)), vec, vec,
                   pl.BlockSpec((1, 128), lambda i: (0, 0))],
        out_shape=[
            jax.ShapeDtypeStruct((t, D), F32),
            jax.ShapeDtypeStruct((t, D), BF16),
            jax.ShapeDtypeStruct((t, D), F32),
            jax.ShapeDtypeStruct((3, t, DFF), BF16),
            jax.ShapeDtypeStruct((2, t, D), BF16),
            jax.ShapeDtypeStruct((1, D), F32),
            jax.ShapeDtypeStruct((1, D), F32),
            jax.ShapeDtypeStruct((1, 128), F32),
        ],
        scratch_shapes=[pltpu.VMEM((D, D), BF16), pltpu.VMEM((DFF, D), BF16), pltpu.VMEM((DFF, D), BF16),
                        pltpu.VMEM((DFF, D), BF16), pltpu.SemaphoreType.DMA((4 * N_DEV,))],
        compiler_params=_params(("arbitrary",)),
    )(mix2d, x2d, tgt2d, g2, gf, gwa, gwb)


def _stage1_rider(stack):
    n = stack.shape[0]
    return dict(inputs=[stack], out_shape=[jax.ShapeDtypeStruct((4,) + stack.shape[2:], BF16)] * n, nsem=4 * n,
                copies=lambda ins, outs, send, recv: _stage1_copies([ins[0]] * n, list(range(n)), outs, send, recv))


def _stage2_rider(pbs):
    return dict(inputs=list(pbs), out_shape=[jax.ShapeDtypeStruct(p.shape, BF16) for p in pbs], nsem=3 * len(pbs),
                copies=_stage2_copies)


def _tn_matmul(a, b, bm, bn, tk, name, with_bf16, rider=None):
    t, m = a.shape
    n = b.shape[1]
    nk = t // tk
    nout = 2 if with_bf16 else 1
    grid = (m // bm, n // bn, nk)
    r_in = [] if rider is None else rider["inputs"]
    r_out = [] if rider is None else rider["out_shape"]

    def body(a_ref, b_ref, *rest):
        ins, outs = rest[:len(r_in)], rest[len(r_in):len(r_in) + nout]
        r_outs, sems = rest[len(r_in) + nout:len(r_in) + nout + len(r_out)], rest[len(r_in) + nout + len(r_out):]
        o_ref = outs[0]
        i, j, k = pl.program_id(0), pl.program_id(1), pl.program_id(2)
        if rider is not None:
            @pl.when((i == 0) & (j == 0) & (k == 0))
            def _():
                for cp in rider["copies"](ins, r_outs, *sems):
                    cp.start()

        @pl.when(k == 0)
        def _():
            o_ref[...] = jnp.zeros_like(o_ref)

        o_ref[...] += _tn(a_ref[...].astype(BF16), b_ref[...].astype(BF16))
        if with_bf16:
            @pl.when(k == nk - 1)
            def _():
                outs[1][...] = o_ref[...].astype(BF16)
        if rider is not None:
            @pl.when((i == grid[0] - 1) & (j == grid[1] - 1) & (k == nk - 1))
            def _():
                copies = rider["copies"](ins, r_outs, *sems)
                for cp in copies:
                    cp.wait_recv()
                for cp in copies:
                    cp.wait_send()

    out_blk = pl.BlockSpec((bm, bn), lambda i, j, k: (i, j))
    hbm = pl.BlockSpec(memory_space=pl.ANY)
    out_shape = [jax.ShapeDtypeStruct((m, n), F32)] + ([jax.ShapeDtypeStruct((m, n), BF16)] if with_bf16 else [])
    res = pl.pallas_call(
        body,
        name=name,
        grid=grid,
        in_specs=[pl.BlockSpec((tk, bm), lambda i, j, k: (k, i)), pl.BlockSpec((tk, bn), lambda i, j, k: (k, j))]
        + [hbm] * len(r_in),
        out_specs=[out_blk] * nout + [hbm] * len(r_out),
        out_shape=out_shape + list(r_out),
        scratch_shapes=([] if rider is None else
                        [pltpu.SemaphoreType.DMA((rider["nsem"],)), pltpu.SemaphoreType.DMA((rider["nsem"],))]),
        compiler_params=_params(("parallel", "parallel", "arbitrary") if rider is None
                                else ("arbitrary", "arbitrary", "arbitrary")),
    )(a, b, *r_in)
    return res[0] if len(res) == 1 else res


def _dw_ffn(adu, hb, tk):
    _, t, _ = adu.shape
    bm = DFF // 2
    nk = t // tk

    def body(a_ref, b_ref, o_ref, ob_ref):
        k = pl.program_id(2)

        @pl.when(k == 0)
        def _():
            o_ref[...] = jnp.zeros_like(o_ref)

        o_ref[...] += _tn(a_ref[...], b_ref[...])

        @pl.when(k == nk - 1)
        def _():
            ob_ref[...] = o_ref[...].astype(BF16)

    out_blk = pl.BlockSpec((None, bm, D), lambda p, i, k: (p, i, 0))
    return pl.pallas_call(
        body,
        name="dw_ffn",
        grid=(3, DFF // bm, nk),
        in_specs=[pl.BlockSpec((None, tk, bm), lambda p, i, k: (p, k, i)),
                  pl.BlockSpec((None, tk, D), lambda p, i, k: (jnp.minimum(p, 1), k, 0))],
        out_specs=[out_blk, out_blk],
        out_shape=[jax.ShapeDtypeStruct((3, DFF, D), F32), jax.ShapeDtypeStruct((3, DFF, D), BF16)],
        compiler_params=_params(("arbitrary", "arbitrary", "arbitrary")),
    )(adu, hb)


def _at_owner(ref, lead, idx):
    return ref.at[idx] if lead is None else ref.at[lead, idx]


def _mix_bwd(proj3, z3, sprev, opre3, dmix3, gng, conv_w, wgu_p, pbs):
    nb, s, _ = proj3.shape
    nc = s // CHUNK
    na = len(pbs)

    def body(*refs):
        (p_ref, pprev_ref, z_ref, sp_ref, o_ref, dm_ref, gng_ref, cw_ref, wgu_ref) = refs[:9]
        pb_refs = refs[9:9 + na]
        (dproj_ref, dgng_ref, dcw_ref, dbg_ref, dwgu_ref) = refs[9 + na:14 + na]
        r2_refs = refs[14 + na:14 + 2 * na]
        ds_ref, dycn_ref, send_sems, recv_sems = refs[14 + 2 * na:]
        step = pl.program_id(0)
        n = nc - 1 - step

        @pl.when(step == 0)
        def _():
            for cp in _stage2_copies(pb_refs, r2_refs, send_sems, recv_sems):
                cp.start()
            ds_ref[...] = jnp.zeros_like(ds_ref)
            dycn_ref[...] = jnp.zeros_like(dycn_ref)
            dgng_ref[...] = jnp.zeros_like(dgng_ref)
            dcw_ref[...] = jnp.zeros_like(dcw_ref)
            dbg_ref[...] = jnp.zeros_like(dbg_ref)
            dwgu_ref[...] = jnp.zeros_like(dwgu_ref)

        r_i = lax.broadcasted_iota(jnp.int32, (CHUNK, CHUNK), 0)
        c_i = lax.broadcasted_iota(jnp.int32, (CHUNK, CHUNK), 1)
        tril16 = (r_i >= c_i).astype(BF16)
        triu16 = (r_i <= c_i).astype(BF16)
        causal = r_i >= c_i
        masks = _head_masks()
        cmask = _causal_stack_mask()
        gg = gng_ref[...]
        last_row = lax.broadcasted_iota(jnp.int32, (CHUNK, NQK), 0) == CHUNK - 1
        ones_r = jnp.ones((16, DV), BF16)
        has_prev = (n > 0).astype(F32)
        for b in range(nb):
            q = p_ref[b, :, OQ:OQ + NQK]
            k = p_ref[b, :, OK_:OK_ + NQK]
            z = z_ref[b]
            _, eb, enb, ekl, qi, ki, ks, decb = _chunk_fwd_parts(q, k, z, tril16)
            qi16 = qi.astype(BF16)
            ki16 = ki.astype(BF16)
            qs = _stack_heads(qi, masks).astype(BF16)
            sc = jnp.where(cmask, _nt(qs, ki16), 0.0).astype(BF16)
            st = sp_ref[b, 0]
            st16 = st.astype(BF16)
            dsn = ds_ref[b]
            dsn16 = dsn.astype(BF16)
            v16 = p_ref[b, :, OV:OV + NV].astype(BF16)
            do16 = []
            dgng = jnp.zeros((1, DV), F32)
            for h in range(HEADS):
                cols = slice(DV * h, DV * (h + 1))
                o = o_ref[b, :, cols]
                r = lax.rsqrt(jnp.mean(o * o, axis=-1, keepdims=True) + EPS)
                nh = o * r
                g = p_ref[b, :, OG + DV * h:OG + DV * (h + 1)]
                sg = jax.nn.sigmoid(g)
                dog = dm_ref[b, :, cols]
                dproj_ref[b, :, OG + DV * h:OG + DV * (h + 1)] = (
                    (dog * (nh * gg)) * (sg * (1.0 + g * (1.0 - sg)))).astype(BF16)
                don = dog * (g * sg)
                dgng = dgng + jnp.sum(don * nh, axis=0, keepdims=True)
                dn = don * gg
                do = r * (dn - nh * jnp.mean(dn * nh, axis=-1, keepdims=True))
                do16.append(do.astype(BF16))
            dgng_ref[...] += dgng
            do_rows = jnp.concatenate(do16, axis=0)
            v_rows = jnp.concatenate([v16[:, DV * h:DV * (h + 1)] for h in range(HEADS)], axis=0)
            dp16 = [jnp.where(causal, _nt(do16[h], v16[:, DV * h:DV * (h + 1)]), 0.0).astype(BF16)
                    for h in range(HEADS)]
            ks_dsn = _mm(_stack_heads(ks, masks).astype(BF16), dsn16)
            do_st = _nt(do_rows, st16)
            v_dsn = _nt(v_rows, dsn16)
            dp_ki = _mm(jnp.concatenate(dp16, axis=0), ki16)
            q_do = _tn(qi16, jnp.concatenate(do16, axis=1))
            dki_h = []
            for h in range(HEADS):
                rows = slice(CHUNK * h, CHUNK * (h + 1))
                cols = slice(DV * h, DV * (h + 1))
                dv = _tn(sc[rows], do16[h]) + ks_dsn[rows]
                dproj_ref[b, :, OV + DV * h:OV + DV * (h + 1)] = dv.astype(BF16)
                dki_h.append(_tn(dp16[h], qi16))
                ds_ref[b, rows, :] = decb[rows] * dsn[rows] + q_do[rows, cols]
            blocks = lambda a: [a[CHUNK * h:CHUNK * (h + 1)] for h in range(HEADS)]
            dqi = _merge_heads(blocks(dp_ki + do_st), masks)
            dki = _merge_heads(dki_h, masks)
            dks = _merge_heads(blocks(v_dsn), masks)
            dproj_ref[b, :, OQ:OQ + NQK] = (dqi * (Q_SCALE * eb)).astype(BF16)
            dproj_ref[b, :, OK_:OK_ + NQK] = (dki * enb + dks * ekl).astype(BF16)
            dks_ks = dks * ks
            db = dqi * qi - dki * ki - dks_ks
            sd = _split_bf16(dsn * st * decb, 2)
            dbl = jnp.sum(dks_ks, axis=0, keepdims=True) + (_nt(ones_r, sd[0]) + _nt(ones_r, sd[1]))[0:1, :]
            db = db + jnp.where(last_row, dbl, 0.0)
            db_parts = _split_bf16(db, 3)
            dla = _mm(triu16, db_parts[0]) + _mm(triu16, db_parts[1]) + _mm(triu16, db_parts[2])
            dz = (dla * INV_GATE_NORM) * (1.0 / (1.0 + jnp.exp(z)))
            dbg_ref[...] += jnp.sum(dz, axis=0, keepdims=True)
            dz16 = dz.astype(BF16)
            pa16 = p_ref[b, :, OA:OA + A_PAD].astype(BF16)
            dwgu_ref[...] += _tn(pa16, dz16)
            dproj_ref[b, :, OA:OA + A_PAD] = _nt(dz16, wgu_ref[...]).astype(BF16)
            cb = p_ref[b, :, OCB:OCB + CW]
            cc = p_ref[b, :, OCC:OCC + CW]
            ch = p_ref[b, :, OCH:OCH + CW]
            u = cc * ch
            uprev = (pprev_ref[b, :, 0:CW] * pprev_ref[b, :, CW:2 * CW]) * has_prev
            u1, u2 = _conv_taps(u, uprev)
            w0 = cw_ref[0:1, :]
            w1 = cw_ref[1:2, :]
            w2 = cw_ref[2:3, :]
            yc = w0 * u2 + w1 * u1 + w2 * u
            doc = dm_ref[b, :, NV:NV + CW]
            dproj_ref[b, :, OCB:OCB + CW] = (doc * yc).astype(BF16)
            dyc = doc * cb
            dycn = dycn_ref[b]
            row = lax.broadcasted_iota(jnp.int32, dyc.shape, 0)
            d1 = jnp.where(row >= CHUNK - 1, pltpu.roll(dycn, CHUNK - 1, 0), pltpu.roll(dyc, CHUNK - 1, 0))
            d2 = jnp.where(row >= CHUNK - 2, pltpu.roll(dycn, CHUNK - 2, 0), pltpu.roll(dyc, CHUNK - 2, 0))
            du = w2 * dyc + w1 * d1 + w0 * d2
            dproj_ref[b, :, OCC:OCC + CW] = (du * ch).astype(BF16)
            dproj_ref[b, :, OCH:OCH + CW] = (du * cc).astype(BF16)
            dcw_ref[0:1, :] += jnp.sum(dyc * u2, axis=0, keepdims=True)
            dcw_ref[1:2, :] += jnp.sum(dyc * u1, axis=0, keepdims=True)
            dcw_ref[2:3, :] += jnp.sum(dyc * u, axis=0, keepdims=True)
            dycn_ref[b] = dyc

        @pl.when(step == nc - 1)
        def _():
            copies = _stage2_copies(pb_refs, r2_refs, send_sems, recv_sems)
            for cp in copies:
                cp.wait_recv()
            for cp in copies:
                cp.wait_send()

    rev = lambda w: pl.BlockSpec((nb, CHUNK, w), lambda i: (0, nc - 1 - i, 0))
    const = lambda r, c: pl.BlockSpec((r, c), lambda i: (0, 0))
    hbm = pl.BlockSpec(memory_space=pl.ANY)
    return pl.pallas_call(
        body,
        name="mix_bwd",
        grid=(nc,),
        in_specs=[
            rev(PW),
            pl.BlockSpec((nb, CHUNK, 2 * CW), lambda i: (0, jnp.maximum(nc - 2 - i, 0), OCC // (2 * CW))),
            rev(NQK),
            pl.BlockSpec((nb, 1, NQK, DV), lambda i: (0, nc - 1 - i, 0, 0)),
            rev(NV),
            rev(D),
            const(1, DV),
            const(CONV_K, CW),
            const(A_PAD, NQK),
        ] + [hbm] * na,
        out_specs=[rev(PW), const(1, DV), const(8, CW), const(1, NQK), const(A_PAD, NQK)] + [hbm] * na,
        out_shape=[
            jax.ShapeDtypeStruct((nb, s, PW), BF16),
            jax.ShapeDtypeStruct((1, DV), F32),
            jax.ShapeDtypeStruct((8, CW), F32),
            jax.ShapeDtypeStruct((1, NQK), F32),
            jax.ShapeDtypeStruct((A_PAD, NQK), F32),
        ] + [jax.ShapeDtypeStruct((3,) + p.shape[1:], BF16) for p in pbs],
        scratch_shapes=[pltpu.VMEM((nb, NQK, DV), F32), pltpu.VMEM((nb, CHUNK, CW), F32),
                        pltpu.SemaphoreType.DMA((3 * na,)), pltpu.SemaphoreType.DMA((3 * na,))],
        compiler_params=_params(("arbitrary",)),
    )(proj3, proj3, z3, sprev, opre3, dmix3, gng, conv_w, wgu_p, *pbs)


SMALL_PACK_ROWS = 16


def _wgu_slot(r):
    return 4 + r // 4, NQK * (r % 4)


CONV_SLOTS = ((8, 0), (8, CW), (9, 0))


def _in_proj_bwd(dproj2d, x2d, dx1, g1, w_in_t, tm, pb, small_parts):
    t = x2d.shape[0]
    nt = t // tm

    def body(dp_ref, x_ref, dx1_ref, g_ref, w_ref, pb_ref, dg2, dgf, dbg, dgng, dwgu, dcw, lp,
             dx_ref, sums_ref, r2_ref, dg1_acc, pack, gbuf, pack1, gbuf1, send_sems, recv_sems,
             ssend, srecv, ssend1, srecv1):
        x, y, c = _position()
        me = 4 * x + 2 * y + c
        flips = [(k >> 2, (k >> 1) & 1, k & 1) for k in range(1, N_DEV)]
        peers = [(x ^ fx, y ^ fy, c ^ fc) for fx, fy, fc in flips]

        def small_copies(src, dst, send, recv, arrivals):
            return [pltpu.make_async_remote_copy(
                src_ref=src, dst_ref=dst.at[4 * px + 2 * py + pc if arrivals else me],
                send_sem=send.at[k], recv_sem=recv.at[k], device_id=(px, py, pc), device_id_type=MESH)
                for k, (px, py, pc) in enumerate(peers)]

        @pl.when(pl.program_id(0) == 0)
        def _():
            for cp in _stage2_copies([pb_ref], [r2_ref], send_sems, recv_sems):
                cp.start()
            dg1_acc[...] = jnp.zeros_like(dg1_acc)
            pack[...] = jnp.zeros_like(pack)
            pack[1:2, :] = dg2[...]
            pack[2:3, :] = dgf[...]
            pack[3:4, 0:NQK] = dbg[...]
            pack[3:4, NQK:NQK + DV] = dgng[...]
            pack[3:4, NQK + DV:NQK + 2 * DV] = lp[...]
            for r in range(RANK):
                row, lane = _wgu_slot(r)
                pack[row:row + 1, lane:lane + NQK] = dwgu[r:r + 1, :]
            for r, (row, lane) in enumerate(CONV_SLOTS):
                pack[row:row + 1, lane:lane + CW] = dcw[r:r + 1, :]
            for cp in small_copies(pack, gbuf, ssend, srecv, False):
                cp.start()
            gbuf[me] = pack[...]

        xv = x_ref[...]
        r = lax.rsqrt(jnp.mean(xv * xv, axis=-1, keepdims=True) + EPS)
        n1 = xv * r
        dh = _mm(dp_ref[...], w_ref[...])
        dg1_acc[...] += jnp.sum(dh * n1, axis=0, keepdims=True)
        dn = dh * g_ref[...]
        dx_ref[...] = dx1_ref[...] + r * (dn - n1 * jnp.mean(dn * n1, axis=-1, keepdims=True))

        @pl.when(pl.program_id(0) == nt - 1)
        def _():
            pack1[...] = jnp.zeros_like(pack1)
            pack1[0:1, :] = dg1_acc[...]
            for cp in small_copies(pack1, gbuf1, ssend1, srecv1, False):
                cp.start()
            gbuf1[me] = pack1[...]
            copies = _stage2_copies([pb_ref], [r2_ref], send_sems, recv_sems)
            for cp in copies:
                cp.wait_recv()
            for cp in copies:
                cp.wait_send()
            for src, dst, send, recv in ((pack, gbuf, ssend, srecv), (pack1, gbuf1, ssend1, srecv1)):
                for cp in small_copies(src, dst, send, recv, True):
                    cp.wait_recv()
                    cp.wait_send()
            acc = gbuf[0]
            acc1 = gbuf1[0]
            for d in range(1, N_DEV):
                acc = acc + gbuf[d]
                acc1 = acc1 + gbuf1[d]
            sums_ref[...] = acc
            sums_ref[0:1, :] = acc1[0:1, :]

    tile = lambda w: pl.BlockSpec((tm, w), lambda i: (i, 0))
    vec = pl.BlockSpec((1, D), lambda i: (0, 0))
    hbm = pl.BlockSpec(memory_space=pl.ANY)
    whole = lambda a: pl.BlockSpec(a.shape, lambda i: (0,) * a.ndim)
    return pl.pallas_call(
        body,
        name="in_proj_bwd",
        grid=(nt,),
        in_specs=[tile(PW), tile(D), tile(D), vec, pl.BlockSpec((PW, D), lambda i: (0, 0)), hbm]
        + [whole(a) for a in small_parts],
        out_specs=[tile(D), pl.BlockSpec((SMALL_PACK_ROWS, D), lambda i: (0, 0)), hbm],
        out_shape=[jax.ShapeDtypeStruct((t, D), F32), jax.ShapeDtypeStruct((SMALL_PACK_ROWS, D), F32),
                   jax.ShapeDtypeStruct((3,) + pb.shape[1:], BF16)],
        scratch_shapes=[pltpu.VMEM((1, D), F32),
                        pltpu.VMEM((SMALL_PACK_ROWS, D), F32), pltpu.VMEM((N_DEV, SMALL_PACK_ROWS, D), F32),
                        pltpu.VMEM((8, D), F32), pltpu.VMEM((N_DEV, 8, D), F32),
                        pltpu.SemaphoreType.DMA((3,)), pltpu.SemaphoreType.DMA((3,)),
                        pltpu.SemaphoreType.DMA((7,)), pltpu.SemaphoreType.DMA((7,)),
                        pltpu.SemaphoreType.DMA((7,)), pltpu.SemaphoreType.DMA((7,))],
        compiler_params=_params(("arbitrary",)),
    )(dproj2d, x2d, dx1, g1, w_in_t, pb, *small_parts)


def _adamw_math(w, g, m, v):
    m = ADAM_B1 * m + (1.0 - ADAM_B1) * g
    v = ADAM_B2 * v + (1.0 - ADAM_B2) * (g * g)
    m_hat = m / (1.0 - ADAM_B1 ** ADAM_STEP)
    v_hat = v / (1.0 - ADAM_B2 ** ADAM_STEP)
    delta = -ADAM_LR * (m_hat / (jnp.sqrt(v_hat) + ADAM_EPS) + ADAM_WD * w)
    return delta, m, v


def _position():
    return lax.axis_index("x"), lax.axis_index("y"), lax.axis_index("c")


def _prep_slab(w_it, w_gt, w_ut, w_d, w_o):
    def body(wi_ref, wg_ref, wu_ref, wd_ref, wo_ref, stage):
        stage[SLAB_IN:SLAB_IN + IN_W, :] = wi_ref[...].astype(BF16)
        stage[SLAB_IN + IN_W:SLAB_G, :] = jnp.zeros((IN_ROWS - IN_W, D), BF16)
        stage[SLAB_G:SLAB_U, :] = wg_ref[...].astype(BF16)
        stage[SLAB_U:SLAB_D, :] = wu_ref[...].astype(BF16)
        stage[SLAB_D:SLAB_O, :] = wd_ref[...].astype(BF16)
        stage[SLAB_O:SLAB_ROWS, :] = wo_ref[...].astype(BF16)

    vm = pl.BlockSpec(memory_space=pltpu.VMEM)
    return pl.pallas_call(
        body,
        name="prep_slab",
        in_specs=[vm] * 5,
        out_specs=vm,
        out_shape=jax.ShapeDtypeStruct((SLAB_ROWS, D), BF16),
        compiler_params=_params(),
    )(w_it, w_gt, w_ut, w_d, w_o)


GATHER_SEMS = 7


def _gather_copies(stage, lo, rows, gx, send_sems, recv_sems, local_sem):
    x, y, c = _position()
    me = (x, y, c)
    sibling = (x, y, 1 - c)
    chips = [(1 - x, y), (x, 1 - y), (1 - x, 1 - y)]
    src = stage.at[pl.ds(lo, rows), :]

    def blk(px, py, pc):
        return gx.at[4 * px + 2 * py + pc]

    def copy(k, block, to, from_stage=False):
        return pltpu.make_async_remote_copy(
            src_ref=src if from_stage else blk(*block), dst_ref=blk(*block),
            send_sem=send_sems.at[k], recv_sem=recv_sems.at[k], device_id=to, device_id_type=MESH)

    mine = pltpu.make_async_copy(src, blk(*me), local_sem)
    first = [copy(0, me, sibling, True)] + [copy(1 + j, me, (*chip, c), True) for j, chip in enumerate(chips)]
    passed = [copy(4 + j, (*chip, c), sibling) for j, chip in enumerate(chips)]
    arrivals = ([copy(0, sibling, me)] + [copy(1 + j, (*chip, c), me) for j, chip in enumerate(chips)]
                + [copy(4 + j, (*chip, 1 - c), me) for j, chip in enumerate(chips)])
    return mine, first, passed, arrivals


def _gather_start(*args):
    mine, first, _, _ = _gather_copies(*args)
    mine.start()
    for cp in first:
        cp.start()


def _gather_finish(*args):
    mine, first, passed, arrivals = _gather_copies(*args)
    for j in range(3):
        arrivals[1 + j].wait_recv()
        passed[j].start()
    arrivals[0].wait_recv()
    for j in range(3):
        arrivals[4 + j].wait_recv()
    for cp in first + passed:
        cp.wait_send()
    mine.wait()


def _gather_sems():
    return [pltpu.SemaphoreType.DMA((GATHER_SEMS,)), pltpu.SemaphoreType.DMA((GATHER_SEMS,)), pltpu.SemaphoreType.DMA]


def _gather_w_in(stage, wgu_s, conv_s):
    def body(stage_hbm, wgu_ref, conv_ref, w_ref, gwgu_ref, gconv_ref, buf, send_sems, recv_sems, local_sem,
             ssend, srecv):
        x, y, c = _position()
        me = 4 * x + 2 * y + c
        args = (stage_hbm, SLAB_IN, IN_ROWS, buf, send_sems, recv_sems, local_sem)
        _gather_start(*args)
        flips = [(k >> 2, (k >> 1) & 1, k & 1) for k in range(1, N_DEV)]
        peers = [(x ^ fx, y ^ fy, c ^ fc) for fx, fy, fc in flips]

        def small(k, block_id, to):
            return [pltpu.make_async_remote_copy(
                src_ref=s, dst_ref=g.at[block_id], send_sem=ssend.at[2 * k + n], recv_sem=srecv.at[2 * k + n],
                device_id=to, device_id_type=MESH)
                for n, (s, g) in enumerate(((wgu_ref, gwgu_ref), (conv_ref, gconv_ref)))]

        gwgu_ref[me] = wgu_ref[...]
        gconv_ref[me] = conv_ref[...]
        for k, peer in enumerate(peers):
            for cp in small(k, me, peer):
                cp.start()
        w_ref[IN_COLS:PW, :] = jnp.zeros((PW - IN_COLS, D), BF16)
        _gather_finish(*args)
        for k, (px, py, pc) in enumerate(peers):
            for cp in small(k, 4 * px + 2 * py + pc, (px, py, pc)):
                cp.wait_recv()
                cp.wait_send()
        for j, lo, hi, d in _in_segments():
            w_ref[d:d + hi - lo, :] = buf[j, lo:hi, :]

    vm = pl.BlockSpec(memory_space=pltpu.VMEM)
    hbm = pl.BlockSpec(memory_space=pl.ANY)
    return pl.pallas_call(
        body,
        name="gather_w_in",
        in_specs=[hbm, vm, vm],
        out_specs=[vm, vm, vm],
        out_shape=[jax.ShapeDtypeStruct((PW, D), BF16),
                   jax.ShapeDtypeStruct((N_DEV,) + wgu_s.shape, F32),
                   jax.ShapeDtypeStruct((N_DEV,) + conv_s.shape, F32)],
        scratch_shapes=[pltpu.VMEM((N_DEV, IN_ROWS, D), BF16)] + _gather_sems()
        + [pltpu.SemaphoreType.DMA((14,)), pltpu.SemaphoreType.DMA((14,))],
        compiler_params=_params(),
    )(stage, wgu_s, conv_s)


def _w_in_core_reduce(dw_t):
    def body(d_ref, own_ref, sib_ref, pb_ref, g, gb, r1, send_sems, recv_sems):
        x, y, c = _position()
        chip = 2 * x + y
        for j in range(N_DEV):
            g[j, IN_W:IN_ROWS, :] = jnp.zeros((IN_ROWS - IN_W, D), F32)
        for j, lo, hi, d in _in_segments():
            g[j, lo:hi, :] = d_ref[d:d + hi - lo, :]
        for j in range(N_DEV):
            gb[j] = g[j].astype(BF16)
        copies = _stage1_copies([gb], [None], [r1], send_sems, recv_sems)
        for cp in copies:
            cp.start()
        own_ref[0] = g[2 * chip + c]
        for cp in copies:
            cp.wait_recv()
        sib_ref[0] = r1[chip]
        for k in range(1, 4):
            t = chip ^ k
            pb_ref[k - 1] = (g[2 * t + c] + r1[t].astype(F32)).astype(BF16)
        for cp in copies:
            cp.wait_send()

    vm = pl.BlockSpec(memory_space=pltpu.VMEM)
    return pl.pallas_call(
        body,
        name="w_in_core_reduce",
        in_specs=[vm],
        out_specs=[vm, vm, vm],
        out_shape=[jax.ShapeDtypeStruct((1, IN_ROWS, D), F32), jax.ShapeDtypeStruct((1, IN_ROWS, D), BF16),
                   jax.ShapeDtypeStruct((3, IN_ROWS, D), BF16)],
        scratch_shapes=[pltpu.VMEM((N_DEV, IN_ROWS, D), F32), pltpu.VMEM((N_DEV, IN_ROWS, D), BF16),
                        pltpu.VMEM((4, IN_ROWS, D), BF16), pltpu.SemaphoreType.DMA((4,)),
                        pltpu.SemaphoreType.DMA((4,))],
        compiler_params=_params(),
    )(dw_t)


def _stage1_copies(g_refs, leads, r_refs, send_sems, recv_sems):
    x, y, c = _position()
    return [pltpu.make_async_remote_copy(
        src_ref=_at_owner(g_refs[a], leads[a], 2 * i + 1 - c), dst_ref=r_refs[a].at[i],
        send_sem=send_sems.at[4 * a + i], recv_sem=recv_sems.at[4 * a + i],
        device_id=(x, y, 1 - c), device_id_type=MESH) for a in range(len(g_refs)) for i in range(4)]


def _ffn_core_reduce(dw3, r1_ffn, dw_o, dwb_o, pos_arr):
    def body(pos_ref, g0, g1, g2, ra, rb, rc, go, go_own, gbo_hbm, p0, p1, p2, po, own_ref, sib_ref,
             r1o, send_sems, recv_sems):
        k = pl.program_id(0)
        chip = pos_ref[1]

        @pl.when(k == 0)
        def _():
            copies = _stage1_copies([gbo_hbm], [None], [r1o], send_sems, recv_sems)
            for cp in copies:
                cp.start()
            for cp in copies:
                cp.wait_recv()

        for g, r, p in ((g0, ra, p0), (g1, rb, p1), (g2, rc, p2)):
            p[...] = (g[...] + r[...].astype(F32)).astype(BF16)
        po[0] = (go[0] + r1o[chip ^ (k + 1)].astype(F32)).astype(BF16)

        @pl.when(k == 2)
        def _():
            own_ref[...] = go_own[...]
            sib_ref[0] = r1o[chip]
            for cp in _stage1_copies([gbo_hbm], [None], [r1o], send_sems, recv_sems):
                cp.wait_send()

    other = lambda k, pos: 2 * (pos[1] ^ (k + 1)) + pos[0]
    g_spec = lambda lead: pl.BlockSpec((None, 1, FF_W, D), lambda k, pos: (lead, other(k, pos), 0, 0))
    r_spec = pl.BlockSpec((1, FF_W, D), lambda k, pos: (pos[1] ^ (k + 1), 0, 0))
    slot = lambda rows: pl.BlockSpec((1, rows, D), lambda k, pos: (k, 0, 0))
    one = pl.BlockSpec((1, OUT_ROWS, D), lambda k, pos: (0, 0, 0))
    return pl.pallas_call(
        body,
        name="ffn_core_reduce",
        grid_spec=pltpu.PrefetchScalarGridSpec(
            num_scalar_prefetch=1, grid=(3,),
            in_specs=[g_spec(0), g_spec(1), g_spec(2), r_spec, r_spec, r_spec,
                      pl.BlockSpec((1, OUT_ROWS, D), lambda k, pos: (other(k, pos), 0, 0)),
                      pl.BlockSpec((1, OUT_ROWS, D), lambda k, pos: (2 * pos[1] + pos[0], 0, 0)),
                      pl.BlockSpec(memory_space=pl.ANY)],
            out_specs=[slot(FF_W), slot(FF_W), slot(FF_W), slot(OUT_ROWS), one, one],
            scratch_shapes=[pltpu.VMEM((4, OUT_ROWS, D), BF16), pltpu.SemaphoreType.DMA((4,)),
                            pltpu.SemaphoreType.DMA((4,))]),
        out_shape=[jax.ShapeDtypeStruct((3, FF_W, D), BF16)] * 3 + [
            jax.ShapeDtypeStruct((3, OUT_ROWS, D), BF16), jax.ShapeDtypeStruct((1, OUT_ROWS, D), F32),
            jax.ShapeDtypeStruct((1, OUT_ROWS, D), BF16)],
        compiler_params=_params(("arbitrary",)),
    )(pos_arr, dw3, dw3, dw3, *r1_ffn, dw_o, dw_o, dwb_o)


def _stage2_copies(p_refs, r_refs, send_sems, recv_sems):
    x, y, c = _position()
    copies = []
    for a in range(len(p_refs)):
        for k in range(1, 4):
            copies.append(pltpu.make_async_remote_copy(
                src_ref=p_refs[a].at[k - 1], dst_ref=r_refs[a].at[k - 1],
                send_sem=send_sems.at[3 * a + k - 1], recv_sem=recv_sems.at[3 * a + k - 1],
                device_id=(x ^ (k >> 1), y ^ (k & 1), c), device_id_type=MESH))
    return copies


def _finish_weights(items, pos_arr, name, nblk):
    n = len(items)
    in_specs, out_specs, out_shape, operands, wbs = [], [], [], [], []
    for g8, lead, r1, r2, w, m, v in items:
        rows, wr = g8.shape[-2], w.shape[0]
        assert rows % nblk == 0 and wr % nblk == 0 and (nblk == 1 or (rows == wr and rows % (16 * nblk) == 0))
        rb, wb = rows // nblk, wr // nblk
        own = g8.shape[0] == 1
        if own:
            g_spec = pl.BlockSpec((1, rb, D), lambda i, pos: (0, i, 0))
        elif lead is None:
            g_spec = pl.BlockSpec((1, rb, D), lambda i, pos: (2 * pos[1] + pos[0], i, 0))
        else:
            g_spec = pl.BlockSpec((None, 1, rb, D), lambda i, pos, lead=lead: (lead, 2 * pos[1] + pos[0], i, 0))
        r1_spec = pl.BlockSpec((1, rb, D), (lambda i, pos: (0, i, 0)) if own else (lambda i, pos: (pos[1], i, 0)))
        wblk = pl.BlockSpec((wb, D), lambda i, pos: (i, 0))
        in_specs += [g_spec, r1_spec, pl.BlockSpec((3, rb, D), lambda i, pos: (0, i, 0)), wblk, wblk, wblk]
        out_specs += [wblk] * 4
        out_shape += [jax.ShapeDtypeStruct(w.shape, F32)] * 4
        operands += [g8, r1, r2, w, m, v]
        wbs.append(wb)

    def body(pos_ref, *refs):
        for a in range(n):
            g_ref, r1_ref, r2_ref, w_ref, m_ref, v_ref = refs[6 * a:6 * a + 6]
            g_out, d_out, m_out, v_out = refs[6 * n + 4 * a:6 * n + 4 * a + 4]
            g = g_ref[0] + r1_ref[0].astype(F32)
            for k in range(3):
                g = g + r2_ref[k].astype(F32)
            g = g[0:wbs[a], :]
            g_out[...] = g
            d, mn, vn = _adamw_math(w_ref[...], g, m_ref[...], v_ref[...])
            d_out[...] = d
            m_out[...] = mn
            v_out[...] = vn

    return pl.pallas_call(
        body,
        name=name,
        grid_spec=pltpu.PrefetchScalarGridSpec(
            num_scalar_prefetch=1, grid=(nblk,), in_specs=in_specs, out_specs=out_specs),
        out_shape=out_shape,
        compiler_params=_params(("arbitrary",)),
    )(pos_arr, *operands)


SMALL_NAMES = ("norm1_g", "norm2_g", "norm_f_g", "b_gate", "gla_norm_g", "w_gate_up", "conv_w")
WGU_W = NQK // N_DEV
CONV_W = CW // N_DEV


def _small_adamw(sums, ws, ms, vs):
    n = len(SMALL_NAMES)

    def body(*refs):
        acc_ref = refs[0]
        w_refs, m_refs, v_refs = refs[1:1 + n], refs[1 + n:1 + 2 * n], refs[1 + 2 * n:1 + 3 * n]
        loss_ref = refs[1 + 3 * n]
        outs = refs[2 + 3 * n:]
        x, y, c = _position()
        me = 4 * x + 2 * y + c
        acc = acc_ref[...]
        loss_ref[...] = acc[3:4, NQK + DV:NQK + DV + 1]

        def my_columns(full, width):
            r = lax.broadcasted_iota(jnp.int32, (full.shape[1], width), 0)
            col = lax.broadcasted_iota(jnp.int32, (full.shape[1], width), 1)
            sel = (r == width * me + col).astype(F32)
            return _mm(full, sel, precision=HIGHEST)

        dwgu = jnp.concatenate([acc[row:row + 1, lane:lane + NQK] for row, lane in map(_wgu_slot, range(RANK))], axis=0)
        dcw = jnp.concatenate([acc[row:row + 1, lane:lane + CW] for row, lane in CONV_SLOTS], axis=0)
        grads = [acc[0:1, :], acc[1:2, :], acc[2:3, :], acc[3:4, 0:NQK], acc[3:4, NQK:NQK + DV],
                 my_columns(dwgu, WGU_W), my_columns(dcw, CONV_W)]
        for i, g in enumerate(grads):
            d, mn, vn = _adamw_math(w_refs[i][...], g, m_refs[i][...], v_refs[i][...])
            outs[4 * i][...] = g
            outs[4 * i + 1][...] = d
            outs[4 * i + 2][...] = mn
            outs[4 * i + 3][...] = vn

    vm = pl.BlockSpec(memory_space=pltpu.VMEM)
    out_shape = [jax.ShapeDtypeStruct((1, 1), F32)]
    for w in ws:
        out_shape += [jax.ShapeDtypeStruct(w.shape, F32)] * 4
    return pl.pallas_call(
        body,
        name="small_adamw",
        in_specs=[vm] * (1 + 3 * n),
        out_specs=[vm] * (1 + 4 * n),
        out_shape=out_shape,
        compiler_params=_params(),
    )(sums, *ws, *ms, *vs)


def kernel(x, norm1_g, w_in, w_gate_up, b_gate, gla_norm_g, conv_w, w_out, norm2_g, w_ffn_gate, w_ffn_up, w_ffn_down, norm_f_g, loss_target, m_norm1_g, m_w_in, m_w_gate_up, m_b_gate, m_gla_norm_g, m_conv_w, m_w_out, m_norm2_g, m_w_ffn_gate, m_w_ffn_up, m_w_ffn_down, m_norm_f_g, v_norm1_g, v_w_in, v_w_gate_up, v_b_gate, v_gla_norm_g, v_conv_w, v_w_out, v_norm2_g, v_w_ffn_gate, v_w_ffn_up, v_w_ffn_down, v_norm_f_g):
    xi, yi, ci = _position()
    pos_arr = jnp.stack([ci, 2 * xi + yi]).astype(jnp.int32)
    nb, s, _ = x.shape
    t = nb * s

    tr = lambda a: a[0].T
    stage = _prep_slab(tr(w_in), tr(w_ffn_gate), tr(w_ffn_up), w_ffn_down[0], w_out[0])
    w_in_t, gwgu, gconv = _gather_w_in(stage, w_gate_up[0], conv_w[0])
    wgu_f = gwgu.transpose(1, 0, 2).reshape(RANK, NQK)
    conv_f = gconv.transpose(1, 0, 2).reshape(CONV_K, CW)
    wgu_p = jnp.concatenate([wgu_f, jnp.zeros((A_PAD - RANK, NQK), F32)], axis=0).astype(BF16)

    x2d = x.reshape(t, D)
    tgt2d = loss_target.reshape(t, D)
    tm = 256
    tm_in = min(512, t)
    tk = min(2048, t)
    proj, z, h, gwb = _in_proj_fwd(x2d, norm1_g, w_in_t, wgu_p, b_gate, tm_in, stage)
    proj3 = proj.reshape(nb, s, PW)
    z3 = z.reshape(nb, s, NQK)
    mix3, opre3, sprev, gwa = _mix_fwd(proj3, z3, gla_norm_g, conv_f, stage)
    mix2d = mix3.reshape(t, D)
    dx1, dx1b, dmix, adu, hb, dg2, dgf, loss_part = _ffn_fwd_bwd(
        mix2d, x2d, tgt2d, gwa, gwb, norm2_g, norm_f_g.reshape(1, D), tm)
    dw3, dwb3 = _dw_ffn(adu, hb, tk)
    dw3 = dw3.reshape(3, N_DEV, FF_W, D)
    dw_o, dwb_o, *r1_ffn = _tn_matmul(mix2d, dx1b, D // 4, D, tk, "dw_out", True,
                                      _stage1_rider(dwb3.reshape(3, N_DEV, FF_W, D)))
    *pb, o_own, o_sib = _ffn_core_reduce(dw3, r1_ffn, dw_o.reshape(N_DEV, OUT_ROWS, D),
                                         dwb_o.reshape(N_DEV, OUT_ROWS, D), pos_arr)
    g8 = [dw3, dw3, dw3, o_own]
    leads = [0, 1, 2, None]
    tags = ("w_ffn_down", "w_ffn_gate", "w_ffn_up", "w_out")
    r1 = list(r1_ffn) + [o_sib]
    mb = _mix_bwd(proj3, z3, sprev, opre3, dmix.reshape(nb, s, D), gla_norm_g, conv_f, wgu_p, [pb[0], pb[1], pb[3]])
    dproj3, dgng, dcw, dbg, dwgu = mb[:5]
    dproj2d = dproj3.reshape(t, PW)
    dw_in_t, r2_up = _tn_matmul(dproj2d, h, PW // 5, D, tk, "dw_in", False, _stage2_rider([pb[2]]))
    r2 = [mb[5], mb[6], r2_up, mb[7]]
    g_in, r1_in, pb_in = _w_in_core_reduce(dw_in_t)
    dx, small_sums, r2_in = _in_proj_bwd(dproj2d, x2d, dx1, norm1_g, w_in_t, tm_in, pb_in,
                                         (dg2, dgf, dbg, dgng, dwgu, dcw, loss_part))

    tags = ("w_in",) + tags
    g8 = [g_in] + g8
    leads = [None] + leads
    r1 = [r1_in] + list(r1)
    r2 = [r2_in] + r2
    shard_w = (tr(w_in), w_ffn_down[0], tr(w_ffn_gate), tr(w_ffn_up), w_out[0])
    shard_m = (tr(m_w_in), m_w_ffn_down[0], tr(m_w_ffn_gate), tr(m_w_ffn_up), m_w_out[0])
    shard_v = (tr(v_w_in), v_w_ffn_down[0], tr(v_w_ffn_gate), tr(v_w_ffn_up), v_w_out[0])
    transposed = (True, False, True, True, False)
    items = list(zip(g8, leads, r1, r2, shard_w, shard_m, shard_v))
    flat = list(_finish_weights(items[1:], pos_arr, "finish_ffn_out", 2))
    flat = list(_finish_weights(items[:1], pos_arr, "finish_w_in", 1)) + flat
    results = {}
    for i, (tag, tp) in enumerate(zip(tags, transposed)):
        results[tag] = [o.T[None] if tp else o[None] for o in flat[4 * i:4 * i + 4]]

    small_w = (norm1_g, norm2_g, norm_f_g.reshape(1, D), b_gate, gla_norm_g, w_gate_up[0], conv_w[0])
    small_m = (m_norm1_g, m_norm2_g, m_norm_f_g.reshape(1, D), m_b_gate, m_gla_norm_g, m_w_gate_up[0], m_conv_w[0])
    small_v = (v_norm1_g, v_norm2_g, v_norm_f_g.reshape(1, D), v_b_gate, v_gla_norm_g, v_w_gate_up[0], v_conv_w[0])
    so = _small_adamw(small_sums, small_w, small_m, small_v)
    loss = so[0].reshape(())
    shapes = {"norm_f_g": (D,), "w_gate_up": (1, RANK, WGU_W), "conv_w": (1, CONV_K, CONV_W)}
    for i, name in enumerate(SMALL_NAMES):
        results[name] = [o.reshape(shapes[name]) if name in shapes else o for o in so[1 + 4 * i:5 + 4 * i]]

    names = ("norm1_g", "w_in", "w_gate_up", "b_gate", "gla_norm_g", "conv_w", "w_out", "norm2_g",
             "w_ffn_gate", "w_ffn_up", "w_ffn_down", "norm_f_g")
    outs = [loss, dx.reshape(nb, s, D)]
    for kind in range(4):
        for name in names:
            outs.append(results[name][kind])
    return tuple(outs)
```

```python
import jax
import jax.numpy as jnp
from jax import lax
from jax.experimental import pallas as pl
from jax.experimental.pallas import tpu as pltpu

F32 = jnp.float32
BF16 = jnp.bfloat16
HIGHEST = lax.Precision.HIGHEST
MESH = pl.DeviceIdType.MESH

N_DEV = 8
D = 1024
DFF = 2816
HEADS = 4
DK = 64
DV = 128
NQK = HEADS * DK
NV = HEADS * DV
RANK = 16
CHUNK = 64
CW = 512
CONV_K = 3
IN_COLS = 3088
EPS = 1e-6
INV_GATE_NORM = 1.0 / 16.0
Q_SCALE = DK ** -0.5

PW = 3200
OQ, OK_, OV, OG, OCB, OCC, OCH, OA = 0, 256, 512, 1024, 1536, 2048, 2560, 3072
A_PAD = 128

ADAM_LR = 0.001
ADAM_B1 = 0.9
ADAM_B2 = 0.999
ADAM_EPS = 1e-08
ADAM_WD = 0.01
ADAM_STEP = 10

IN_W = IN_COLS // N_DEV
IN_ROWS = 400
FF_W = DFF // N_DEV
OUT_ROWS = D // N_DEV
SLAB_IN = 0
SLAB_G = SLAB_IN + IN_ROWS
SLAB_U = SLAB_G + FF_W
SLAB_D = SLAB_U + FF_W
SLAB_O = SLAB_D + FF_W
SLAB_ROWS = SLAB_O + OUT_ROWS

VMEM_LIMIT = 56 * 1024 * 1024


def _params(sem=None, vmem=VMEM_LIMIT):
    return pltpu.CompilerParams(dimension_semantics=sem, vmem_limit_bytes=vmem)


def _nt(a, b):
    return lax.dot_general(a, b, (((1,), (1,)), ((), ())), preferred_element_type=F32)


def _tn(a, b, precision=None):
    return lax.dot_general(a, b, (((0,), (0,)), ((), ())), preferred_element_type=F32, precision=precision)


def _mm(a, b, precision=None):
    return jnp.dot(a, b, preferred_element_type=F32, precision=precision)


def _in_segments():
    segs = []
    for j in range(N_DEV):
        lo, hi = IN_W * j, IN_W * (j + 1)
        cuts = sorted({lo, hi} | {c for c in (OCB, OCB + RANK) if lo < c < hi})
        for a, b in zip(cuts[:-1], cuts[1:]):
            if a < OCB:
                d = a
            elif a < OCB + RANK:
                d = OA + (a - OCB)
            else:
                d = a - RANK
            segs.append((j, a - lo, b - lo, d))
    return segs


def _in_proj_fwd(x2d, g1, w_in_t, wgu_p, b_gate, tm, stage):
    t = x2d.shape[0]
    nt = t // tm
    g_rows = SLAB_ROWS - SLAB_D

    def body(x_ref, g_ref, w_ref, wgu_ref, bg_ref, stage_hbm, proj_ref, z_ref, h_ref, gwb_ref,
             send_sems, recv_sems, local_sem):
        gargs = (stage_hbm, SLAB_D, g_rows, gwb_ref, send_sems, recv_sems, local_sem)

        @pl.when(pl.program_id(0) == 0)
        def _():
            _gather_start(*gargs)

        x = x_ref[...]
        r = lax.rsqrt(jnp.mean(x * x, axis=-1, keepdims=True) + EPS)
        h = ((x * r) * g_ref[...]).astype(BF16)
        h_ref[...] = h
        proj = _nt(h, w_ref[...])
        proj_ref[...] = proj
        pa = proj[:, OA:OA + A_PAD].astype(BF16)
        z_ref[...] = _mm(pa, wgu_ref[...]) + bg_ref[...]

        @pl.when(pl.program_id(0) == nt - 1)
        def _():
            _gather_finish(*gargs)

    return pl.pallas_call(
        body,
        name="in_proj_fwd",
        grid=(t // tm,),
        in_specs=[
            pl.BlockSpec((tm, D), lambda i: (i, 0)),
            pl.BlockSpec((1, D), lambda i: (0, 0)),
            pl.BlockSpec((PW, D), lambda i: (0, 0)),
            pl.BlockSpec((A_PAD, NQK), lambda i: (0, 0)),
            pl.BlockSpec((1, NQK), lambda i: (0, 0)),
            pl.BlockSpec(memory_space=pl.ANY),
        ],
        out_specs=[
            pl.BlockSpec((tm, PW), lambda i: (i, 0)),
            pl.BlockSpec((tm, NQK), lambda i: (i, 0)),
            pl.BlockSpec((tm, D), lambda i: (i, 0)),
            pl.BlockSpec(memory_space=pl.ANY),
        ],
        out_shape=[
            jax.ShapeDtypeStruct((t, PW), F32),
            jax.ShapeDtypeStruct((t, NQK), F32),
            jax.ShapeDtypeStruct((t, D), BF16),
            jax.ShapeDtypeStruct((N_DEV, g_rows, D), BF16),
        ],
        scratch_shapes=_gather_sems(),
        compiler_params=_params(("arbitrary",)),
    )(x2d, g1, w_in_t, wgu_p, b_gate, stage)


def _head_masks():
    lane = lax.broadcasted_iota(jnp.int32, (1, NQK), 1)
    return [(lane >= DK * h) & (lane < DK * (h + 1)) for h in range(HEADS)]


def _split_bf16(x, n):
    parts = []
    for _ in range(n):
        p = x.astype(BF16)
        parts.append(p)
        x = x - p.astype(F32)
    return parts


def _chunk_fwd_parts(q, k, z, tril16):
    la = (jnp.minimum(z, 0.0) - jnp.log1p(jnp.exp(-jnp.abs(z)))) * INV_GATE_NORM
    la_parts = _split_bf16(la, 3)
    bc = _mm(tril16, la_parts[0]) + _mm(tril16, la_parts[1]) + _mm(tril16, la_parts[2])
    bl = bc[CHUNK - 1:CHUNK, :]
    eb = jnp.exp(bc)
    enb = jnp.exp(-bc)
    ekl = jnp.exp(bl - bc)
    qi = (q * Q_SCALE) * eb
    ki = k * enb
    ks = k * ekl
    ones16 = jnp.ones((CHUNK, DV), BF16)
    decb = jnp.exp(_tn(la_parts[0], ones16) + _tn(la_parts[1], ones16) + _tn(la_parts[2], ones16))
    return la, eb, enb, ekl, qi, ki, ks, decb


def _stack_heads(a, masks):
    return jnp.concatenate([jnp.where(m, a, 0.0) for m in masks], axis=0)


def _merge_heads(blocks, masks):
    out = blocks[HEADS - 1]
    for h in range(HEADS - 2, -1, -1):
        out = jnp.where(masks[h], blocks[h], out)
    return out


def _causal_stack_mask():
    row = lax.broadcasted_iota(jnp.int32, (HEADS * CHUNK, CHUNK), 0)
    col = lax.broadcasted_iota(jnp.int32, (HEADS * CHUNK, CHUNK), 1)
    return (row & (CHUNK - 1)) >= col


def _conv_taps(u, uprev):
    row = lax.broadcasted_iota(jnp.int32, u.shape, 0)
    u1 = jnp.where(row < 1, pltpu.roll(uprev, 1, 0), pltpu.roll(u, 1, 0))
    u2 = jnp.where(row < 2, pltpu.roll(uprev, 2, 0), pltpu.roll(u, 2, 0))
    return u1, u2


def _mix_fwd(proj3, z3, gng, conv_w, stage):
    nb, s, _ = proj3.shape
    nc = s // CHUNK
    g_rows = SLAB_D - SLAB_G

    def body(p_ref, z_ref, gng_ref, cw_ref, stage_hbm, mix_ref, o_ref, sprev_ref, gwa_ref, s_ref, uprev_ref,
             send_sems, recv_sems, local_sem):
        n = pl.program_id(0)
        gargs = (stage_hbm, SLAB_G, g_rows, gwa_ref, send_sems, recv_sems, local_sem)

        @pl.when(n == 0)
        def _():
            _gather_start(*gargs)
            s_ref[...] = jnp.zeros_like(s_ref)
            uprev_ref[...] = jnp.zeros_like(uprev_ref)

        r_i = lax.broadcasted_iota(jnp.int32, (CHUNK, CHUNK), 0)
        c_i = lax.broadcasted_iota(jnp.int32, (CHUNK, CHUNK), 1)
        tril16 = (r_i >= c_i).astype(BF16)
        masks = _head_masks()
        cmask = _causal_stack_mask()
        gg = gng_ref[...]
        for b in range(nb):
            q = p_ref[b, :, OQ:OQ + NQK]
            k = p_ref[b, :, OK_:OK_ + NQK]
            _, _, _, _, qi, ki, ks, decb = _chunk_fwd_parts(q, k, z_ref[b], tril16)
            qs = _stack_heads(qi, masks).astype(BF16)
            sc = jnp.where(cmask, _nt(qs, ki.astype(BF16)), 0.0).astype(BF16)
            st = s_ref[b]
            sprev_ref[b, 0] = st
            o_inter = _mm(qs, st.astype(BF16))
            v16 = p_ref[b, :, OV:OV + NV].astype(BF16)
            kv = _tn(ks.astype(BF16), v16)
            for h in range(HEADS):
                rows = slice(CHUNK * h, CHUNK * (h + 1))
                cols = slice(DV * h, DV * (h + 1))
                o = _mm(sc[rows], v16[:, cols]) + o_inter[rows]
                o_ref[b, :, cols] = o
                r = lax.rsqrt(jnp.mean(o * o, axis=-1, keepdims=True) + EPS)
                on = (o * r) * gg
                g = p_ref[b, :, OG + DV * h:OG + DV * (h + 1)]
                mix_ref[b, :, cols] = (on * (g * jax.nn.sigmoid(g))).astype(BF16)
                s_ref[b, rows, :] = decb[rows] * st[rows] + kv[rows, cols]
            u = p_ref[b, :, OCC:OCC + CW] * p_ref[b, :, OCH:OCH + CW]
            u1, u2 = _conv_taps(u, uprev_ref[b])
            yc = cw_ref[0:1, :] * u2 + cw_ref[1:2, :] * u1 + cw_ref[2:3, :] * u
            mix_ref[b, :, NV:NV + CW] = (p_ref[b, :, OCB:OCB + CW] * yc).astype(BF16)
            uprev_ref[b] = u

        @pl.when(n == nc - 1)
        def _():
            _gather_finish(*gargs)

    return pl.pallas_call(
        body,
        name="mix_fwd",
        grid=(nc,),
        in_specs=[
            pl.BlockSpec((nb, CHUNK, PW), lambda n: (0, n, 0)),
            pl.BlockSpec((nb, CHUNK, NQK), lambda n: (0, n, 0)),
            pl.BlockSpec((1, DV), lambda n: (0, 0)),
            pl.BlockSpec((CONV_K, CW), lambda n: (0, 0)),
            pl.BlockSpec(memory_space=pl.ANY),
        ],
        out_specs=[
            pl.BlockSpec((nb, CHUNK, D), lambda n: (0, n, 0)),
            pl.BlockSpec((nb, CHUNK, NV), lambda n: (0, n, 0)),
            pl.BlockSpec((nb, 1, NQK, DV), lambda n: (0, n, 0, 0)),
            pl.BlockSpec(memory_space=pl.ANY),
        ],
        out_shape=[
            jax.ShapeDtypeStruct((nb, s, D), BF16),
            jax.ShapeDtypeStruct((nb, s, NV), F32),
            jax.ShapeDtypeStruct((nb, nc, NQK, DV), F32),
            jax.ShapeDtypeStruct((N_DEV, g_rows, D), BF16),
        ],
        scratch_shapes=[pltpu.VMEM((nb, NQK, DV), F32), pltpu.VMEM((nb, CHUNK, CW), F32)] + _gather_sems(),
        compiler_params=_params(("arbitrary",)),
    )(proj3, z3, gng, conv_w, stage)


def _ffn_fwd_bwd(mix2d, x2d, tgt2d, gwa, gwb, g2, gf, tm):
    t = x2d.shape[0]

    def body(mix_ref, x_ref, tgt_ref, g2_ref, gf_ref, gwa_hbm, gwb_hbm,
             dx1_ref, dx1b_ref, dmix_ref, adu_ref, hb_ref, dg2_ref, dgf_ref, loss_ref,
             wo, wg, wu, wd, wsem):
        i = pl.program_id(0)

        def weight_copies(n, dst, src, off, rows):
            return [pltpu.make_async_copy(src.at[j, pl.ds(off, rows), :], dst.at[pl.ds(rows * j, rows), :],
                                          wsem.at[N_DEV * n + j]) for j in range(N_DEV)]

        loads = (weight_copies(0, wo, gwb_hbm, FF_W, OUT_ROWS), weight_copies(1, wg, gwa_hbm, 0, FF_W),
                 weight_copies(2, wu, gwa_hbm, FF_W, FF_W), weight_copies(3, wd, gwb_hbm, 0, FF_W))

        @pl.when(i == 0)
        def _():
            for group in loads:
                for cp in group:
                    cp.start()
            dg2_ref[...] = jnp.zeros_like(dg2_ref)
            dgf_ref[...] = jnp.zeros_like(dgf_ref)
            loss_ref[...] = jnp.zeros_like(loss_ref)
            for group in loads:
                for cp in group:
                    cp.wait()

        g2v = g2_ref[...]
        gfv = gf_ref[...]
        x1 = x_ref[...] + _mm(mix_ref[...], wo[...])
        r2 = lax.rsqrt(jnp.mean(x1 * x1, axis=-1, keepdims=True) + EPS)
        n2 = x1 * r2
        h2 = (n2 * g2v).astype(BF16)
        hb_ref[1] = h2
        gate = _nt(h2, wg[...])
        up = _nt(h2, wu[...])
        sg = jax.nn.sigmoid(gate)
        sil = gate * sg
        act = (sil * up).astype(BF16)
        adu_ref[0] = act
        x2 = x1 + _mm(act, wd[...])
        rf = lax.rsqrt(jnp.mean(x2 * x2, axis=-1, keepdims=True) + EPS)
        nf = x2 * rf
        err = nf * gfv - tgt_ref[...]
        loss_ref[...] += 0.5 * jnp.sum(jnp.mean(err * err, axis=-1, keepdims=True))
        dy = err * (1.0 / D)
        dgf_ref[...] += jnp.sum(dy * nf, axis=0, keepdims=True)
        dnf = dy * gfv
        dx2 = rf * (dnf - nf * jnp.mean(dnf * nf, axis=-1, keepdims=True))
        dx2b = dx2.astype(BF16)
        hb_ref[0] = dx2b
        dact = _nt(dx2b, wd[...])
        dup = (dact * sil).astype(BF16)
        dgate = ((dact * up) * (sg * (1.0 + gate * (1.0 - sg)))).astype(BF16)
        adu_ref[2] = dup
        adu_ref[1] = dgate
        dh2 = _mm(dgate, wg[...]) + _mm(dup, wu[...])
        dg2_ref[...] += jnp.sum(dh2 * n2, axis=0, keepdims=True)
        dn2 = dh2 * g2v
        dx1 = dx2 + r2 * (dn2 - n2 * jnp.mean(dn2 * n2, axis=-1, keepdims=True))
        dx1_ref[...] = dx1
        dx1b = dx1.astype(BF16)
        dx1b_ref[...] = dx1b
        dmix_ref[...] = _nt(dx1b, wo[...])

    tile = lambda w: pl.BlockSpec((tm, w), lambda i: (i, 0))
    vec = pl.BlockSpec((1, D), lambda i: (0, 0))
    hbm = pl.BlockSpec(memory_space=pl.ANY)
    return pl.pallas_call(
        body,
        name="ffn_fwd_bwd",
        grid=(t // tm,),
        in_specs=[tile(D), tile(D), tile(D), vec, vec, hbm, hbm],
        out_specs=[tile(D), tile(D), tile(D), pl.BlockSpec((3, tm, DFF), lambda i: (0, i, 0)),
                   pl.BlockSpec((2, tm, D), lambda i: (0, i, 0)), vec, vec,
                   pl.BlockSpec((1, 128), lambda i: (0, 0))],
        out_shape=[
            jax.ShapeDtypeStruct((t, D), F32),
            jax.ShapeDtypeStruct((t, D), BF16),
            jax.ShapeDtypeStruct((t, D), F32),
            jax.ShapeDtypeStruct((3, t, DFF), BF16),
            jax.ShapeDtypeStruct((2, t, D), BF16),
            jax.ShapeDtypeStruct((1, D), F32),
            jax.ShapeDtypeStruct((1, D), F32),
            jax.ShapeDtypeStruct((1, 128), F32),
        ],
        scratch_shapes=[pltpu.VMEM((D, D), BF16), pltpu.VMEM((DFF, D), BF16), pltpu.VMEM((DFF, D), BF16),
                        pltpu.VMEM((DFF, D), BF16), pltpu.SemaphoreType.DMA((4 * N_DEV,))],
        compiler_params=_params(("arbitrary",)),
    )(mix2d, x2d, tgt2d, g2, gf, gwa, gwb)


def _stage1_rider(stack):
    n = stack.shape[0]
    return dict(inputs=[stack], out_shape=[jax.ShapeDtypeStruct((4,) + stack.shape[2:], BF16)] * n, nsem=4 * n,
                copies=lambda ins, outs, send, recv: _stage1_copies([ins[0]] * n, list(range(n)), outs, send, recv))


def _stage2_rider(pbs):
    return dict(inputs=list(pbs), out_shape=[jax.ShapeDtypeStruct(p.shape, BF16) for p in pbs], nsem=3 * len(pbs),
                copies=_stage2_copies)


def _tn_matmul(a, b, bm, bn, tk, name, with_bf16, rider=None):
    t, m = a.shape
    n = b.shape[1]
    nk = t // tk
    nout = 2 if with_bf16 else 1
    grid = (m // bm, n // bn, nk)
    r_in = [] if rider is None else rider["inputs"]
    r_out = [] if rider is None else rider["out_shape"]

    def body(a_ref, b_ref, *rest):
        ins, outs = rest[:len(r_in)], rest[len(r_in):len(r_in) + nout]
        r_outs, sems = rest[len(r_in) + nout:len(r_in) + nout + len(r_out)], rest[len(r_in) + nout + len(r_out):]
        o_ref = outs[0]
        i, j, k = pl.program_id(0), pl.program_id(1), pl.program_id(2)
        if rider is not None:
            @pl.when((i == 0) & (j == 0) & (k == 0))
            def _():
                for cp in rider["copies"](ins, r_outs, *sems):
                    cp.start()

        @pl.when(k == 0)
        def _():
            o_ref[...] = jnp.zeros_like(o_ref)

        o_ref[...] += _tn(a_ref[...].astype(BF16), b_ref[...].astype(BF16))
        if with_bf16:
            @pl.when(k == nk - 1)
            def _():
                outs[1][...] = o_ref[...].astype(BF16)
        if rider is not None:
            @pl.when((i == grid[0] - 1) & (j == grid[1] - 1) & (k == nk - 1))
            def _():
                copies = rider["copies"](ins, r_outs, *sems)
                for cp in copies:
                    cp.wait_recv()
                for cp in copies:
                    cp.wait_send()

    out_blk = pl.BlockSpec((bm, bn), lambda i, j, k: (i, j))
    hbm = pl.BlockSpec(memory_space=pl.ANY)
    out_shape = [jax.ShapeDtypeStruct((m, n), F32)] + ([jax.ShapeDtypeStruct((m, n), BF16)] if with_bf16 else [])
    res = pl.pallas_call(
        body,
        name=name,
        grid=grid,
        in_specs=[pl.BlockSpec((tk, bm), lambda i, j, k: (k, i)), pl.BlockSpec((tk, bn), lambda i, j, k: (k, j))]
        + [hbm] * len(r_in),
        out_specs=[out_blk] * nout + [hbm] * len(r_out),
        out_shape=out_shape + list(r_out),
        scratch_shapes=([] if rider is None else
                        [pltpu.SemaphoreType.DMA((rider["nsem"],)), pltpu.SemaphoreType.DMA((rider["nsem"],))]),
        compiler_params=_params(("parallel", "parallel", "arbitrary") if rider is None
                                else ("arbitrary", "arbitrary", "arbitrary")),
    )(a, b, *r_in)
    return res[0] if len(res) == 1 else res


def _dw_ffn(adu, hb, tk):
    _, t, _ = adu.shape
    bm = DFF // 2
    nk = t // tk

    def body(a_ref, b_ref, o_ref, ob_ref):
        k = pl.program_id(2)

        @pl.when(k == 0)
        def _():
            o_ref[...] = jnp.zeros_like(o_ref)

        o_ref[...] += _tn(a_ref[...], b_ref[...])

        @pl.when(k == nk - 1)
        def _():
            ob_ref[...] = o_ref[...].astype(BF16)

    out_blk = pl.BlockSpec((None, bm, D), lambda p, i, k: (p, i, 0))
    return pl.pallas_call(
        body,
        name="dw_ffn",
        grid=(3, DFF // bm, nk),
        in_specs=[pl.BlockSpec((None, tk, bm), lambda p, i, k: (p, k, i)),
                  pl.BlockSpec((None, tk, D), lambda p, i, k: (jnp.minimum(p, 1), k, 0))],
        out_specs=[out_blk, out_blk],
        out_shape=[jax.ShapeDtypeStruct((3, DFF, D), F32), jax.ShapeDtypeStruct((3, DFF, D), BF16)],
        compiler_params=_params(("arbitrary", "arbitrary", "arbitrary")),
    )(adu, hb)


def _at_owner(ref, lead, idx):
    return ref.at[idx] if lead is None else ref.at[lead, idx]


def _mix_bwd(proj3, z3, sprev, opre3, dmix3, gng, conv_w, wgu_p, pbs):
    nb, s, _ = proj3.shape
    nc = s // CHUNK
    na = len(pbs)

    def body(*refs):
        (p_ref, pprev_ref, z_ref, sp_ref, o_ref, dm_ref, gng_ref, cw_ref, wgu_ref) = refs[:9]
        pb_refs = refs[9:9 + na]
        (dproj_ref, dgng_ref, dcw_ref, dbg_ref, dwgu_ref) = refs[9 + na:14 + na]
        r2_refs = refs[14 + na:14 + 2 * na]
        ds_ref, dycn_ref, send_sems, recv_sems = refs[14 + 2 * na:]
        step = pl.program_id(0)
        n = nc - 1 - step

        @pl.when(step == 0)
        def _():
            for cp in _stage2_copies(pb_refs, r2_refs, send_sems, recv_sems):
                cp.start()
            ds_ref[...] = jnp.zeros_like(ds_ref)
            dycn_ref[...] = jnp.zeros_like(dycn_ref)
            dgng_ref[...] = jnp.zeros_like(dgng_ref)
            dcw_ref[...] = jnp.zeros_like(dcw_ref)
            dbg_ref[...] = jnp.zeros_like(dbg_ref)
            dwgu_ref[...] = jnp.zeros_like(dwgu_ref)

        r_i = lax.broadcasted_iota(jnp.int32, (CHUNK, CHUNK), 0)
        c_i = lax.broadcasted_iota(jnp.int32, (CHUNK, CHUNK), 1)
        tril16 = (r_i >= c_i).astype(BF16)
        triu16 = (r_i <= c_i).astype(BF16)
        causal = r_i >= c_i
        masks = _head_masks()
        cmask = _causal_stack_mask()
        gg = gng_ref[...]
        last_row = lax.broadcasted_iota(jnp.int32, (CHUNK, NQK), 0) == CHUNK - 1
        ones_r = jnp.ones((16, DV), BF16)
        has_prev = (n > 0).astype(F32)
        for b in range(nb):
            q = p_ref[b, :, OQ:OQ + NQK]
            k = p_ref[b, :, OK_:OK_ + NQK]
            z = z_ref[b]
            _, eb, enb, ekl, qi, ki, ks, decb = _chunk_fwd_parts(q, k, z, tril16)
            qi16 = qi.astype(BF16)
            ki16 = ki.astype(BF16)
            qs = _stack_heads(qi, masks).astype(BF16)
            sc = jnp.where(cmask, _nt(qs, ki16), 0.0).astype(BF16)
            st = sp_ref[b, 0]
            st16 = st.astype(BF16)
            dsn = ds_ref[b]
            dsn16 = dsn.astype(BF16)
            v16 = p_ref[b, :, OV:OV + NV].astype(BF16)
            do16 = []
            dgng = jnp.zeros((1, DV), F32)
            for h in range(HEADS):
                cols = slice(DV * h, DV * (h + 1))
                o = o_ref[b, :, cols]
                r = lax.rsqrt(jnp.mean(o * o, axis=-1, keepdims=True) + EPS)
                nh = o * r
                g = p_ref[b, :, OG + DV * h:OG + DV * (h + 1)]
                sg = jax.nn.sigmoid(g)
                dog = dm_ref[b, :, cols]
                dproj_ref[b, :, OG + DV * h:OG + DV * (h + 1)] = (
                    (dog * (nh * gg)) * (sg * (1.0 + g * (1.0 - sg)))).astype(BF16)
                don = dog * (g * sg)
                dgng = dgng + jnp.sum(don * nh, axis=0, keepdims=True)
                dn = don * gg
                do = r * (dn - nh * jnp.mean(dn * nh, axis=-1, keepdims=True))
                do16.append(do.astype(BF16))
            dgng_ref[...] += dgng
            do_rows = jnp.concatenate(do16, axis=0)
            v_rows = jnp.concatenate([v16[:, DV * h:DV * (h + 1)] for h in range(HEADS)], axis=0)
            dp16 = [jnp.where(causal, _nt(do16[h], v16[:, DV * h:DV * (h + 1)]), 0.0).astype(BF16)
                    for h in range(HEADS)]
            ks_dsn = _mm(_stack_heads(ks, masks).astype(BF16), dsn16)
            do_st = _nt(do_rows, st16)
            v_dsn = _nt(v_rows, dsn16)
            dp_ki = _mm(jnp.concatenate(dp16, axis=0), ki16)
            q_do = _tn(qi16, jnp.concatenate(do16, axis=1))
            dki_h = []
            for h in range(HEADS):
                rows = slice(CHUNK * h, CHUNK * (h + 1))
                cols = slice(DV * h, DV * (h + 1))
                dv = _tn(sc[rows], do16[h]) + ks_dsn[rows]
                dproj_ref[b, :, OV + DV * h:OV + DV * (h + 1)] = dv.astype(BF16)
                dki_h.append(_tn(dp16[h], qi16))
                ds_ref[b, rows, :] = decb[rows] * dsn[rows] + q_do[rows, cols]
            blocks = lambda a: [a[CHUNK * h:CHUNK * (h + 1)] for h in range(HEADS)]
            dqi = _merge_heads(blocks(dp_ki + do_st), masks)
            dki = _merge_heads(dki_h, masks)
            dks = _merge_heads(blocks(v_dsn), masks)
            dproj_ref[b, :, OQ:OQ + NQK] = (dqi * (Q_SCALE * eb)).astype(BF16)
            dproj_ref[b, :, OK_:OK_ + NQK] = (dki * enb + dks * ekl).astype(BF16)
            dks_ks = dks * ks
            db = dqi * qi - dki * ki - dks_ks
            sd = _split_bf16(dsn * st * decb, 2)
            dbl = jnp.sum(dks_ks, axis=0, keepdims=True) + (_nt(ones_r, sd[0]) + _nt(ones_r, sd[1]))[0:1, :]
            db = db + jnp.where(last_row, dbl, 0.0)
            db_parts = _split_bf16(db, 3)
            dla = _mm(triu16, db_parts[0]) + _mm(triu16, db_parts[1]) + _mm(triu16, db_parts[2])
            dz = (dla * INV_GATE_NORM) * (1.0 / (1.0 + jnp.exp(z)))
            dbg_ref[...] += jnp.sum(dz, axis=0, keepdims=True)
            dz16 = dz.astype(BF16)
            pa16 = p_ref[b, :, OA:OA + A_PAD].astype(BF16)
            dwgu_ref[...] += _tn(pa16, dz16)
            dproj_ref[b, :, OA:OA + A_PAD] = _nt(dz16, wgu_ref[...]).astype(BF16)
            cb = p_ref[b, :, OCB:OCB + CW]
            cc = p_ref[b, :, OCC:OCC + CW]
            ch = p_ref[b, :, OCH:OCH + CW]
            u = cc * ch
            uprev = (pprev_ref[b, :, 0:CW] * pprev_ref[b, :, CW:2 * CW]) * has_prev
            u1, u2 = _conv_taps(u, uprev)
            w0 = cw_ref[0:1, :]
            w1 = cw_ref[1:2, :]
            w2 = cw_ref[2:3, :]
            yc = w0 * u2 + w1 * u1 + w2 * u
            doc = dm_ref[b, :, NV:NV + CW]
            dproj_ref[b, :, OCB:OCB + CW] = (doc * yc).astype(BF16)
            dyc = doc * cb
            dycn = dycn_ref[b]
            row = lax.broadcasted_iota(jnp.int32, dyc.shape, 0)
            d1 = jnp.where(row >= CHUNK - 1, pltpu.roll(dycn, CHUNK - 1, 0), pltpu.roll(dyc, CHUNK - 1, 0))
            d2 = jnp.where(row >= CHUNK - 2, pltpu.roll(dycn, CHUNK - 2, 0), pltpu.roll(dyc, CHUNK - 2, 0))
            du = w2 * dyc + w1 * d1 + w0 * d2
            dproj_ref[b, :, OCC:OCC + CW] = (du * ch).astype(BF16)
            dproj_ref[b, :, OCH:OCH + CW] = (du * cc).astype(BF16)
            dcw_ref[0:1, :] += jnp.sum(dyc * u2, axis=0, keepdims=True)
            dcw_ref[1:2, :] += jnp.sum(dyc * u1, axis=0, keepdims=True)
            dcw_ref[2:3, :] += jnp.sum(dyc * u, axis=0, keepdims=True)
            dycn_ref[b] = dyc

        @pl.when(step == nc - 1)
        def _():
            copies = _stage2_copies(pb_refs, r2_refs, send_sems, recv_sems)
            for cp in copies:
                cp.wait_recv()
            for cp in copies:
                cp.wait_send()

    rev = lambda w: pl.BlockSpec((nb, CHUNK, w), lambda i: (0, nc - 1 - i, 0))
    const = lambda r, c: pl.BlockSpec((r, c), lambda i: (0, 0))
    hbm = pl.BlockSpec(memory_space=pl.ANY)
    return pl.pallas_call(
        body,
        name="mix_bwd",
        grid=(nc,),
        in_specs=[
            rev(PW),
            pl.BlockSpec((nb, CHUNK, 2 * CW), lambda i: (0, jnp.maximum(nc - 2 - i, 0), OCC // (2 * CW))),
            rev(NQK),
            pl.BlockSpec((nb, 1, NQK, DV), lambda i: (0, nc - 1 - i, 0, 0)),
            rev(NV),
            rev(D),
            const(1, DV),
            const(CONV_K, CW),
            const(A_PAD, NQK),
        ] + [hbm] * na,
        out_specs=[rev(PW), const(1, DV), const(8, CW), const(1, NQK), const(A_PAD, NQK)] + [hbm] * na,
        out_shape=[
            jax.ShapeDtypeStruct((nb, s, PW), BF16),
            jax.ShapeDtypeStruct((1, DV), F32),
            jax.ShapeDtypeStruct((8, CW), F32),
            jax.ShapeDtypeStruct((1, NQK), F32),
            jax.ShapeDtypeStruct((A_PAD, NQK), F32),
        ] + [jax.ShapeDtypeStruct((3,) + p.shape[1:], BF16) for p in pbs],
        scratch_shapes=[pltpu.VMEM((nb, NQK, DV), F32), pltpu.VMEM((nb, CHUNK, CW), F32),
                        pltpu.SemaphoreType.DMA((3 * na,)), pltpu.SemaphoreType.DMA((3 * na,))],
        compiler_params=_params(("arbitrary",)),
    )(proj3, proj3, z3, sprev, opre3, dmix3, gng, conv_w, wgu_p, *pbs)


SMALL_PACK_ROWS = 16


def _wgu_slot(r):
    return 4 + r // 4, NQK * (r % 4)


CONV_SLOTS = ((8, 0), (8, CW), (9, 0))


def _in_proj_bwd(dproj2d, x2d, dx1, g1, w_in_t, tm, pb, small_parts):
    t = x2d.shape[0]
    nt = t // tm

    def body(dp_ref, x_ref, dx1_ref, g_ref, w_ref, pb_ref, dg2, dgf, dbg, dgng, dwgu, dcw, lp,
             dx_ref, sums_ref, r2_ref, dg1_acc, pack, gbuf, pack1, gbuf1, send_sems, recv_sems,
             ssend, srecv, ssend1, srecv1):
        x, y, c = _position()
        me = 4 * x + 2 * y + c
        flips = [(k >> 2, (k >> 1) & 1, k & 1) for k in range(1, N_DEV)]
        peers = [(x ^ fx, y ^ fy, c ^ fc) for fx, fy, fc in flips]

        def small_copies(src, dst, send, recv, arrivals):
            return [pltpu.make_async_remote_copy(
                src_ref=src, dst_ref=dst.at[4 * px + 2 * py + pc if arrivals else me],
                send_sem=send.at[k], recv_sem=recv.at[k], device_id=(px, py, pc), device_id_type=MESH)
                for k, (px, py, pc) in enumerate(peers)]

        @pl.when(pl.program_id(0) == 0)
        def _():
            for cp in _stage2_copies([pb_ref], [r2_ref], send_sems, recv_sems):
                cp.start()
            dg1_acc[...] = jnp.zeros_like(dg1_acc)
            pack[...] = jnp.zeros_like(pack)
            pack[1:2, :] = dg2[...]
            pack[2:3, :] = dgf[...]
            pack[3:4, 0:NQK] = dbg[...]
            pack[3:4, NQK:NQK + DV] = dgng[...]
            pack[3:4, NQK + DV:NQK + 2 * DV] = lp[...]
            for r in range(RANK):
                row, lane = _wgu_slot(r)
                pack[row:row + 1, lane:lane + NQK] = dwgu[r:r + 1, :]
            for r, (row, lane) in enumerate(CONV_SLOTS):
                pack[row:row + 1, lane:lane + CW] = dcw[r:r + 1, :]
            for cp in small_copies(pack, gbuf, ssend, srecv, False):
                cp.start()
            gbuf[me] = pack[...]

        xv = x_ref[...]
        r = lax.rsqrt(jnp.mean(xv * xv, axis=-1, keepdims=True) + EPS)
        n1 = xv * r
        dh = _mm(dp_ref[...], w_ref[...])
        dg1_acc[...] += jnp.sum(dh * n1, axis=0, keepdims=True)
        dn = dh * g_ref[...]
        dx_ref[...] = dx1_ref[...] + r * (dn - n1 * jnp.mean(dn * n1, axis=-1, keepdims=True))

        @pl.when(pl.program_id(0) == nt - 1)
        def _():
            pack1[...] = jnp.zeros_like(pack1)
            pack1[0:1, :] = dg1_acc[...]
            for cp in small_copies(pack1, gbuf1, ssend1, srecv1, False):
                cp.start()
            gbuf1[me] = pack1[...]
            copies = _stage2_copies([pb_ref], [r2_ref], send_sems, recv_sems)
            for cp in copies:
                cp.wait_recv()
            for cp in copies:
                cp.wait_send()
            for src, dst, send, recv in ((pack, gbuf, ssend, srecv), (pack1, gbuf1, ssend1, srecv1)):
                for cp in small_copies(src, dst, send, recv, True):
                    cp.wait_recv()
                    cp.wait_send()
            acc = gbuf[0]
            acc1 = gbuf1[0]
            for d in range(1, N_DEV):
                acc = acc + gbuf[d]
                acc1 = acc1 + gbuf1[d]
            sums_ref[...] = acc
            sums_ref[0:1, :] = acc1[0:1, :]

    tile = lambda w: pl.BlockSpec((tm, w), lambda i: (i, 0))
    vec = pl.BlockSpec((1, D), lambda i: (0, 0))
    hbm = pl.BlockSpec(memory_space=pl.ANY)
    whole = lambda a: pl.BlockSpec(a.shape, lambda i: (0,) * a.ndim)
    return pl.pallas_call(
        body,
        name="in_proj_bwd",
        grid=(nt,),
        in_specs=[tile(PW), tile(D), tile(D), vec, pl.BlockSpec((PW, D), lambda i: (0, 0)), hbm]
        + [whole(a) for a in small_parts],
        out_specs=[tile(D), pl.BlockSpec((SMALL_PACK_ROWS, D), lambda i: (0, 0)), hbm],
        out_shape=[jax.ShapeDtypeStruct((t, D), F32), jax.ShapeDtypeStruct((SMALL_PACK_ROWS, D), F32),
                   jax.ShapeDtypeStruct((3,) + pb.shape[1:], BF16)],
        scratch_shapes=[pltpu.VMEM((1, D), F32),
                        pltpu.VMEM((SMALL_PACK_ROWS, D), F32), pltpu.VMEM((N_DEV, SMALL_PACK_ROWS, D), F32),
                        pltpu.VMEM((8, D), F32), pltpu.VMEM((N_DEV, 8, D), F32),
                        pltpu.SemaphoreType.DMA((3,)), pltpu.SemaphoreType.DMA((3,)),
                        pltpu.SemaphoreType.DMA((7,)), pltpu.SemaphoreType.DMA((7,)),
                        pltpu.SemaphoreType.DMA((7,)), pltpu.SemaphoreType.DMA((7,))],
        compiler_params=_params(("arbitrary",)),
    )(dproj2d, x2d, dx1, g1, w_in_t, pb, *small_parts)


def _adamw_math(w, g, m, v):
    m = ADAM_B1 * m + (1.0 - ADAM_B1) * g
    v = ADAM_B2 * v + (1.0 - ADAM_B2) * (g * g)
    m_hat = m / (1.0 - ADAM_B1 ** ADAM_STEP)
    v_hat = v / (1.0 - ADAM_B2 ** ADAM_STEP)
    delta = -ADAM_LR * (m_hat / (jnp.sqrt(v_hat) + ADAM_EPS) + ADAM_WD * w)
    return delta, m, v


def _position():
    return lax.axis_index("x"), lax.axis_index("y"), lax.axis_index("c")


def _prep_slab(w_it, w_gt, w_ut, w_d, w_o):
    def body(wi_ref, wg_ref, wu_ref, wd_ref, wo_ref, stage):
        stage[SLAB_IN:SLAB_IN + IN_W, :] = wi_ref[...].astype(BF16)
        stage[SLAB_IN + IN_W:SLAB_G, :] = jnp.zeros((IN_ROWS - IN_W, D), BF16)
        stage[SLAB_G:SLAB_U, :] = wg_ref[...].astype(BF16)
        stage[SLAB_U:SLAB_D, :] = wu_ref[...].astype(BF16)
        stage[SLAB_D:SLAB_O, :] = wd_ref[...].astype(BF16)
        stage[SLAB_O:SLAB_ROWS, :] = wo_ref[...].astype(BF16)

    vm = pl.BlockSpec(memory_space=pltpu.VMEM)
    return pl.pallas_call(
        body,
        name="prep_slab",
        in_specs=[vm] * 5,
        out_specs=vm,
        out_shape=jax.ShapeDtypeStruct((SLAB_ROWS, D), BF16),
        compiler_params=_params(),
    )(w_it, w_gt, w_ut, w_d, w_o)


GATHER_PARTS = 2
GATHER_SEMS = 7 * GATHER_PARTS


def _gather_copies(stage, lo, rows, gx, send_sems, recv_sems, local_sem):
    x, y, c = _position()
    me = (x, y, c)
    sibling = (x, y, 1 - c)
    chips = [(1 - x, y), (x, 1 - y), (1 - x, 1 - y)]
    part = -(-rows // (16 * GATHER_PARTS)) * 16
    bounds = [(p * part, min(part, rows - p * part)) for p in range(GATHER_PARTS)]

    def blk(px, py, pc, off, n):
        return gx.at[4 * px + 2 * py + pc, pl.ds(off, n), :]

    mine = pltpu.make_async_copy(stage.at[pl.ds(lo, rows), :], gx.at[4 * x + 2 * y + c], local_sem)
    parts = []
    for p, (off, n) in enumerate(bounds):
        def copy(k, block, to, from_stage=False, p=p, off=off, n=n):
            return pltpu.make_async_remote_copy(
                src_ref=stage.at[pl.ds(lo + off, n), :] if from_stage else blk(*block, off, n),
                dst_ref=blk(*block, off, n), send_sem=send_sems.at[7 * p + k], recv_sem=recv_sems.at[7 * p + k],
                device_id=to, device_id_type=MESH)

        first = [copy(0, me, sibling, True)] + [copy(1 + j, me, (*chip, c), True) for j, chip in enumerate(chips)]
        passed = [copy(4 + j, (*chip, c), sibling) for j, chip in enumerate(chips)]
        arrivals = ([copy(0, sibling, me)] + [copy(1 + j, (*chip, c), me) for j, chip in enumerate(chips)]
                    + [copy(4 + j, (*chip, 1 - c), me) for j, chip in enumerate(chips)])
        parts.append((first, passed, arrivals))
    return mine, parts


def _gather_start(*args):
    mine, parts = _gather_copies(*args)
    mine.start()
    for first, _, _ in parts:
        for cp in first:
            cp.start()


def _gather_finish(*args):
    mine, parts = _gather_copies(*args)
    for _, passed, arrivals in parts:
        for j in range(3):
            arrivals[1 + j].wait_recv()
            passed[j].start()
    for first, passed, arrivals in parts:
        arrivals[0].wait_recv()
        for j in range(3):
            arrivals[4 + j].wait_recv()
        for cp in first + passed:
            cp.wait_send()
    mine.wait()


def _gather_sems():
    return [pltpu.SemaphoreType.DMA((GATHER_SEMS,)), pltpu.SemaphoreType.DMA((GATHER_SEMS,)), pltpu.SemaphoreType.DMA]


def _gather_w_in(stage, wgu_s, conv_s):
    def body(stage_hbm, wgu_ref, conv_ref, w_ref, gwgu_ref, gconv_ref, buf, send_sems, recv_sems, local_sem,
             ssend, srecv):
        x, y, c = _position()
        me = 4 * x + 2 * y + c
        args = (stage_hbm, SLAB_IN, IN_ROWS, buf, send_sems, recv_sems, local_sem)
        _gather_start(*args)
        flips = [(k >> 2, (k >> 1) & 1, k & 1) for k in range(1, N_DEV)]
        peers = [(x ^ fx, y ^ fy, c ^ fc) for fx, fy, fc in flips]

        def small(k, block_id, to):
            return [pltpu.make_async_remote_copy(
                src_ref=s, dst_ref=g.at[block_id], send_sem=ssend.at[2 * k + n], recv_sem=srecv.at[2 * k + n],
                device_id=to, device_id_type=MESH)
                for n, (s, g) in enumerate(((wgu_ref, gwgu_ref), (conv_ref, gconv_ref)))]

        gwgu_ref[me] = wgu_ref[...]
        gconv_ref[me] = conv_ref[...]
        for k, peer in enumerate(peers):
            for cp in small(k, me, peer):
                cp.start()
        w_ref[IN_COLS:PW, :] = jnp.zeros((PW - IN_COLS, D), BF16)
        _gather_finish(*args)
        for k, (px, py, pc) in enumerate(peers):
            for cp in small(k, 4 * px + 2 * py + pc, (px, py, pc)):
                cp.wait_recv()
                cp.wait_send()
        for j, lo, hi, d in _in_segments():
            w_ref[d:d + hi - lo, :] = buf[j, lo:hi, :]

    vm = pl.BlockSpec(memory_space=pltpu.VMEM)
    hbm = pl.BlockSpec(memory_space=pl.ANY)
    return pl.pallas_call(
        body,
        name="gather_w_in",
        in_specs=[hbm, vm, vm],
        out_specs=[vm, vm, vm],
        out_shape=[jax.ShapeDtypeStruct((PW, D), BF16),
                   jax.ShapeDtypeStruct((N_DEV,) + wgu_s.shape, F32),
                   jax.ShapeDtypeStruct((N_DEV,) + conv_s.shape, F32)],
        scratch_shapes=[pltpu.VMEM((N_DEV, IN_ROWS, D), BF16)] + _gather_sems()
        + [pltpu.SemaphoreType.DMA((14,)), pltpu.SemaphoreType.DMA((14,))],
        compiler_params=_params(),
    )(stage, wgu_s, conv_s)


def _w_in_core_reduce(dw_t):
    def body(d_ref, own_ref, sib_ref, pb_ref, g, gb, r1, send_sems, recv_sems):
        x, y, c = _position()
        chip = 2 * x + y
        for j in range(N_DEV):
            g[j, IN_W:IN_ROWS, :] = jnp.zeros((IN_ROWS - IN_W, D), F32)
        for j, lo, hi, d in _in_segments():
            g[j, lo:hi, :] = d_ref[d:d + hi - lo, :]
        for j in range(N_DEV):
            gb[j] = g[j].astype(BF16)
        copies = _stage1_copies([gb], [None], [r1], send_sems, recv_sems)
        for cp in copies:
            cp.start()
        own_ref[0] = g[2 * chip + c]
        for cp in copies:
            cp.wait_recv()
        sib_ref[0] = r1[chip]
        for k in range(1, 4):
            t = chip ^ k
            pb_ref[k - 1] = (g[2 * t + c] + r1[t].astype(F32)).astype(BF16)
        for cp in copies:
            cp.wait_send()

    vm = pl.BlockSpec(memory_space=pltpu.VMEM)
    return pl.pallas_call(
        body,
        name="w_in_core_reduce",
        in_specs=[vm],
        out_specs=[vm, vm, vm],
        out_shape=[jax.ShapeDtypeStruct((1, IN_ROWS, D), F32), jax.ShapeDtypeStruct((1, IN_ROWS, D), BF16),
                   jax.ShapeDtypeStruct((3, IN_ROWS, D), BF16)],
        scratch_shapes=[pltpu.VMEM((N_DEV, IN_ROWS, D), F32), pltpu.VMEM((N_DEV, IN_ROWS, D), BF16),
                        pltpu.VMEM((4, IN_ROWS, D), BF16), pltpu.SemaphoreType.DMA((4,)),
                        pltpu.SemaphoreType.DMA((4,))],
        compiler_params=_params(),
    )(dw_t)


def _stage1_copies(g_refs, leads, r_refs, send_sems, recv_sems):
    x, y, c = _position()
    return [pltpu.make_async_remote_copy(
        src_ref=_at_owner(g_refs[a], leads[a], 2 * i + 1 - c), dst_ref=r_refs[a].at[i],
        send_sem=send_sems.at[4 * a + i], recv_sem=recv_sems.at[4 * a + i],
        device_id=(x, y, 1 - c), device_id_type=MESH) for a in range(len(g_refs)) for i in range(4)]


def _ffn_core_reduce(dw3, r1_ffn, dw_o, dwb_o, pos_arr):
    def body(pos_ref, g0, g1, g2, ra, rb, rc, go, go_own, gbo_hbm, p0, p1, p2, po, own_ref, sib_ref,
             r1o, send_sems, recv_sems):
        k = pl.program_id(0)
        chip = pos_ref[1]

        @pl.when(k == 0)
        def _():
            copies = _stage1_copies([gbo_hbm], [None], [r1o], send_sems, recv_sems)
            for cp in copies:
                cp.start()
            for cp in copies:
                cp.wait_recv()

        for g, r, p in ((g0, ra, p0), (g1, rb, p1), (g2, rc, p2)):
            p[...] = (g[...] + r[...].astype(F32)).astype(BF16)
        po[0] = (go[0] + r1o[chip ^ (k + 1)].astype(F32)).astype(BF16)

        @pl.when(k == 2)
        def _():
            own_ref[...] = go_own[...]
            sib_ref[0] = r1o[chip]
            for cp in _stage1_copies([gbo_hbm], [None], [r1o], send_sems, recv_sems):
                cp.wait_send()

    other = lambda k, pos: 2 * (pos[1] ^ (k + 1)) + pos[0]
    g_spec = lambda lead: pl.BlockSpec((None, 1, FF_W, D), lambda k, pos: (lead, other(k, pos), 0, 0))
    r_spec = pl.BlockSpec((1, FF_W, D), lambda k, pos: (pos[1] ^ (k + 1), 0, 0))
    slot = lambda rows: pl.BlockSpec((1, rows, D), lambda k, pos: (k, 0, 0))
    one = pl.BlockSpec((1, OUT_ROWS, D), lambda k, pos: (0, 0, 0))
    return pl.pallas_call(
        body,
        name="ffn_core_reduce",
        grid_spec=pltpu.PrefetchScalarGridSpec(
            num_scalar_prefetch=1, grid=(3,),
            in_specs=[g_spec(0), g_spec(1), g_spec(2), r_spec, r_spec, r_spec,
                      pl.BlockSpec((1, OUT_ROWS, D), lambda k, pos: (other(k, pos), 0, 0)),
                      pl.BlockSpec((1, OUT_ROWS, D), lambda k, pos: (2 * pos[1] + pos[0], 0, 0)),
                      pl.BlockSpec(memory_space=pl.ANY)],
            out_specs=[slot(FF_W), slot(FF_W), slot(FF_W), slot(OUT_ROWS), one, one],
            scratch_shapes=[pltpu.VMEM((4, OUT_ROWS, D), BF16), pltpu.SemaphoreType.DMA((4,)),
                            pltpu.SemaphoreType.DMA((4,))]),
        out_shape=[jax.ShapeDtypeStruct((3, FF_W, D), BF16)] * 3 + [
            jax.ShapeDtypeStruct((3, OUT_ROWS, D), BF16), jax.ShapeDtypeStruct((1, OUT_ROWS, D), F32),
            jax.ShapeDtypeStruct((1, OUT_ROWS, D), BF16)],
        compiler_params=_params(("arbitrary",)),
    )(pos_arr, dw3, dw3, dw3, *r1_ffn, dw_o, dw_o, dwb_o)


def _stage2_copies(p_refs, r_refs, send_sems, recv_sems):
    x, y, c = _position()
    copies = []
    for a in range(len(p_refs)):
        for k in range(1, 4):
            copies.append(pltpu.make_async_remote_copy(
                src_ref=p_refs[a].at[k - 1], dst_ref=r_refs[a].at[k - 1],
                send_sem=send_sems.at[3 * a + k - 1], recv_sem=recv_sems.at[3 * a + k - 1],
                device_id=(x ^ (k >> 1), y ^ (k & 1), c), device_id_type=MESH))
    return copies


def _finish_weights(items, pos_arr, name, nblk):
    n = len(items)
    in_specs, out_specs, out_shape, operands, wbs = [], [], [], [], []
    for g8, lead, r1, r2, w, m, v in items:
        rows, wr = g8.shape[-2], w.shape[0]
        assert rows % nblk == 0 and wr % nblk == 0 and (nblk == 1 or (rows == wr and rows % (16 * nblk) == 0))
        rb, wb = rows // nblk, wr // nblk
        own = g8.shape[0] == 1
        if own:
            g_spec = pl.BlockSpec((1, rb, D), lambda i, pos: (0, i, 0))
        elif lead is None:
            g_spec = pl.BlockSpec((1, rb, D), lambda i, pos: (2 * pos[1] + pos[0], i, 0))
        else:
            g_spec = pl.BlockSpec((None, 1, rb, D), lambda i, pos, lead=lead: (lead, 2 * pos[1] + pos[0], i, 0))
        r1_spec = pl.BlockSpec((1, rb, D), (lambda i, pos: (0, i, 0)) if own else (lambda i, pos: (pos[1], i, 0)))
        wblk = pl.BlockSpec((wb, D), lambda i, pos: (i, 0))
        in_specs += [g_spec, r1_spec, pl.BlockSpec((3, rb, D), lambda i, pos: (0, i, 0)), wblk, wblk, wblk]
        out_specs += [wblk] * 4
        out_shape += [jax.ShapeDtypeStruct(w.shape, F32)] * 4
        operands += [g8, r1, r2, w, m, v]
        wbs.append(wb)

    def body(pos_ref, *refs):
        for a in range(n):
            g_ref, r1_ref, r2_ref, w_ref, m_ref, v_ref = refs[6 * a:6 * a + 6]
            g_out, d_out, m_out, v_out = refs[6 * n + 4 * a:6 * n + 4 * a + 4]
            g = g_ref[0] + r1_ref[0].astype(F32)
            for k in range(3):
                g = g + r2_ref[k].astype(F32)
            g = g[0:wbs[a], :]
            g_out[...] = g
            d, mn, vn = _adamw_math(w_ref[...], g, m_ref[...], v_ref[...])
            d_out[...] = d
            m_out[...] = mn
            v_out[...] = vn

    return pl.pallas_call(
        body,
        name=name,
        grid_spec=pltpu.PrefetchScalarGridSpec(
            num_scalar_prefetch=1, grid=(nblk,), in_specs=in_specs, out_specs=out_specs),
        out_shape=out_shape,
        compiler_params=_params(("arbitrary",)),
    )(pos_arr, *operands)


SMALL_NAMES = ("norm1_g", "norm2_g", "norm_f_g", "b_gate", "gla_norm_g", "w_gate_up", "conv_w")
WGU_W = NQK // N_DEV
CONV_W = CW // N_DEV


def _small_adamw(sums, ws, ms, vs):
    n = len(SMALL_NAMES)

    def body(*refs):
        acc_ref = refs[0]
        w_refs, m_refs, v_refs = refs[1:1 + n], refs[1 + n:1 + 2 * n], refs[1 + 2 * n:1 + 3 * n]
        loss_ref = refs[1 + 3 * n]
        outs = refs[2 + 3 * n:]
        x, y, c = _position()
        me = 4 * x + 2 * y + c
        acc = acc_ref[...]
        loss_ref[...] = acc[3:4, NQK + DV:NQK + DV + 1]

        def my_columns(full, width):
            r = lax.broadcasted_iota(jnp.int32, (full.shape[1], width), 0)
            col = lax.broadcasted_iota(jnp.int32, (full.shape[1], width), 1)
            sel = (r == width * me + col).astype(F32)
            return _mm(full, sel, precision=HIGHEST)

        dwgu = jnp.concatenate([acc[row:row + 1, lane:lane + NQK] for row, lane in map(_wgu_slot, range(RANK))], axis=0)
        dcw = jnp.concatenate([acc[row:row + 1, lane:lane + CW] for row, lane in CONV_SLOTS], axis=0)
        grads = [acc[0:1, :], acc[1:2, :], acc[2:3, :], acc[3:4, 0:NQK], acc[3:4, NQK:NQK + DV],
                 my_columns(dwgu, WGU_W), my_columns(dcw, CONV_W)]
        for i, g in enumerate(grads):
            d, mn, vn = _adamw_math(w_refs[i][...], g, m_refs[i][...], v_refs[i][...])
            outs[4 * i][...] = g
            outs[4 * i + 1][...] = d
            outs[4 * i + 2][...] = mn
            outs[4 * i + 3][...] = vn

    vm = pl.BlockSpec(memory_space=pltpu.VMEM)
    out_shape = [jax.ShapeDtypeStruct((1, 1), F32)]
    for w in ws:
        out_shape += [jax.ShapeDtypeStruct(w.shape, F32)] * 4
    return pl.pallas_call(
        body,
        name="small_adamw",
        in_specs=[vm] * (1 + 3 * n),
        out_specs=[vm] * (1 + 4 * n),
        out_shape=out_shape,
        compiler_params=_params(),
    )(sums, *ws, *ms, *vs)


def kernel(x, norm1_g, w_in, w_gate_up, b_gate, gla_norm_g, conv_w, w_out, norm2_g, w_ffn_gate, w_ffn_up, w_ffn_down, norm_f_g, loss_target, m_norm1_g, m_w_in, m_w_gate_up, m_b_gate, m_gla_norm_g, m_conv_w, m_w_out, m_norm2_g, m_w_ffn_gate, m_w_ffn_up, m_w_ffn_down, m_norm_f_g, v_norm1_g, v_w_in, v_w_gate_up, v_b_gate, v_gla_norm_g, v_conv_w, v_w_out, v_norm2_g, v_w_ffn_gate, v_w_ffn_up, v_w_ffn_down, v_norm_f_g):
    xi, yi, ci = _position()
    pos_arr = jnp.stack([ci, 2 * xi + yi]).astype(jnp.int32)
    nb, s, _ = x.shape
    t = nb * s

    tr = lambda a: a[0].T
    stage = _prep_slab(tr(w_in), tr(w_ffn_gate), tr(w_ffn_up), w_ffn_down[0], w_out[0])
    w_in_t, gwgu, gconv = _gather_w_in(stage, w_gate_up[0], conv_w[0])
    wgu_f = gwgu.transpose(1, 0, 2).reshape(RANK, NQK)
    conv_f = gconv.transpose(1, 0, 2).reshape(CONV_K, CW)
    wgu_p = jnp.concatenate([wgu_f, jnp.zeros((A_PAD - RANK, NQK), F32)], axis=0).astype(BF16)

    x2d = x.reshape(t, D)
    tgt2d = loss_target.reshape(t, D)
    tm = 256
    tm_in = min(512, t)
    tk = min(2048, t)
    proj, z, h, gwb = _in_proj_fwd(x2d, norm1_g, w_in_t, wgu_p, b_gate, tm_in, stage)
    proj3 = proj.reshape(nb, s, PW)
    z3 = z.reshape(nb, s, NQK)
    mix3, opre3, sprev, gwa = _mix_fwd(proj3, z3, gla_norm_g, conv_f, stage)
    mix2d = mix3.reshape(t, D)
    dx1, dx1b, dmix, adu, hb, dg2, dgf, loss_part = _ffn_fwd_bwd(
        mix2d, x2d, tgt2d, gwa, gwb, norm2_g, norm_f_g.reshape(1, D), tm)
    dw3, dwb3 = _dw_ffn(adu, hb, tk)
    dw3 = dw3.reshape(3, N_DEV, FF_W, D)
    dw_o, dwb_o, *r1_ffn = _tn_matmul(mix2d, dx1b, D // 4, D, tk, "dw_out", True,
                                      _stage1_rider(dwb3.reshape(3, N_DEV, FF_W, D)))
    *pb, o_own, o_sib = _ffn_core_reduce(dw3, r1_ffn, dw_o.reshape(N_DEV, OUT_ROWS, D),
                                         dwb_o.reshape(N_DEV, OUT_ROWS, D), pos_arr)
    g8 = [dw3, dw3, dw3, o_own]
    leads = [0, 1, 2, None]
    tags = ("w_ffn_down", "w_ffn_gate", "w_ffn_up", "w_out")
    r1 = list(r1_ffn) + [o_sib]
    mb = _mix_bwd(proj3, z3, sprev, opre3, dmix.reshape(nb, s, D), gla_norm_g, conv_f, wgu_p, [pb[0], pb[1], pb[3]])
    dproj3, dgng, dcw, dbg, dwgu = mb[:5]
    dproj2d = dproj3.reshape(t, PW)
    dw_in_t, r2_up = _tn_matmul(dproj2d, h, PW // 5, D, tk, "dw_in", False, _stage2_rider([pb[2]]))
    r2 = [mb[5], mb[6], r2_up, mb[7]]
    g_in, r1_in, pb_in = _w_in_core_reduce(dw_in_t)
    dx, small_sums, r2_in = _in_proj_bwd(dproj2d, x2d, dx1, norm1_g, w_in_t, tm_in, pb_in,
                                         (dg2, dgf, dbg, dgng, dwgu, dcw, loss_part))

    tags = ("w_in",) + tags
    g8 = [g_in] + g8
    leads = [None] + leads
    r1 = [r1_in] + list(r1)
    r2 = [r2_in] + r2
    shard_w = (tr(w_in), w_ffn_down[0], tr(w_ffn_gate), tr(w_ffn_up), w_out[0])
    shard_m = (tr(m_w_in), m_w_ffn_down[0], tr(m_w_ffn_gate), tr(m_w_ffn_up), m_w_out[0])
    shard_v = (tr(v_w_in), v_w_ffn_down[0], tr(v_w_ffn_gate), tr(v_w_ffn_up), v_w_out[0])
    transposed = (True, False, True, True, False)
    items = list(zip(g8, leads, r1, r2, shard_w, shard_m, shard_v))
    flat = list(_finish_weights(items[1:], pos_arr, "finish_ffn_out", 2))
    flat = list(_finish_weights(items[:1], pos_arr, "finish_w_in", 1)) + flat
    results = {}
    for i, (tag, tp) in enumerate(zip(tags, transposed)):
        results[tag] = [o.T[None] if tp else o[None] for o in flat[4 * i:4 * i + 4]]

    small_w = (norm1_g, norm2_g, norm_f_g.reshape(1, D), b_gate, gla_norm_g, w_gate_up[0], conv_w[0])
    small_m = (m_norm1_g, m_norm2_g, m_norm_f_g.reshape(1, D), m_b_gate, m_gla_norm_g, m_w_gate_up[0], m_conv_w[0])
    small_v = (v_norm1_g, v_norm2_g, v_norm_f_g.reshape(1, D), v_b_gate, v_gla_norm_g, v_w_gate_up[0], v_conv_w[0])
    so = _small_adamw(small_sums, small_w, small_m, small_v)
    loss = so[0].reshape(())
    shapes = {"norm_f_g": (D,), "w_gate_up": (1, RANK, WGU_W), "conv_w": (1, CONV_K, CONV_W)}
    for i, name in enumerate(SMALL_NAMES):
        results[name] = [o.reshape(shapes[name]) if name in shapes else o for o in so[1 + 4 * i:5 + 4 * i]]

    names = ("norm1_g", "w_in", "w_gate_up", "b_gate", "gla_norm_g", "conv_w", "w_out", "norm2_g",
             "w_ffn_gate", "w_ffn_up", "w_ffn_down", "norm_f_g")
    outs = [loss, dx.reshape(nb, s, D)]
    for kind in range(4):
        for name in names:
            outs.append(results[name][kind])
    return tuple(outs)
```

```python
import jax
import jax.numpy as jnp
from jax import lax
from jax.experimental import pallas as pl
from jax.experimental.pallas import tpu as pltpu

F32 = jnp.float32
BF16 = jnp.bfloat16
HIGHEST = lax.Precision.HIGHEST
MESH = pl.DeviceIdType.MESH

N_DEV = 8
D = 1024
DFF = 2816
HEADS = 4
DK = 64
DV = 128
NQK = HEADS * DK
NV = HEADS * DV
RANK = 16
CHUNK = 64
CW = 512
CONV_K = 3
IN_COLS = 3088
EPS = 1e-6
INV_GATE_NORM = 1.0 / 16.0
Q_SCALE = DK ** -0.5

PW = 3200
OQ, OK_, OV, OG, OCB, OCC, OCH, OA = 0, 256, 512, 1024, 1536, 2048, 2560, 3072
A_PAD = 128

ADAM_LR = 0.001
ADAM_B1 = 0.9
ADAM_B2 = 0.999
ADAM_EPS = 1e-08
ADAM_WD = 0.01
ADAM_STEP = 10

IN_W = IN_COLS // N_DEV
IN_ROWS = 400
FF_W = DFF // N_DEV
OUT_ROWS = D // N_DEV
SLAB_IN = 0
SLAB_G = SLAB_IN + IN_ROWS
SLAB_U = SLAB_G + FF_W
SLAB_D = SLAB_U + FF_W
SLAB_O = SLAB_D + FF_W
SLAB_ROWS = SLAB_O + OUT_ROWS

VMEM_LIMIT = 56 * 1024 * 1024


def _params(sem=None, vmem=VMEM_LIMIT):
    return pltpu.CompilerParams(dimension_semantics=sem, vmem_limit_bytes=vmem)


def _nt(a, b):
    return lax.dot_general(a, b, (((1,), (1,)), ((), ())), preferred_element_type=F32)


def _tn(a, b, precision=None):
    return lax.dot_general(a, b, (((0,), (0,)), ((), ())), preferred_element_type=F32, precision=precision)


def _mm(a, b, precision=None):
    return jnp.dot(a, b, preferred_element_type=F32, precision=precision)


def _in_segments():
    segs = []
    for j in range(N_DEV):
        lo, hi = IN_W * j, IN_W * (j + 1)
        cuts = sorted({lo, hi} | {c for c in (OCB, OCB + RANK) if lo < c < hi})
        for a, b in zip(cuts[:-1], cuts[1:]):
            if a < OCB:
                d = a
            elif a < OCB + RANK:
                d = OA + (a - OCB)
            else:
                d = a - RANK
            segs.append((j, a - lo, b - lo, d))
    return segs


def _in_proj_fwd(x2d, g1, w_in_t, wgu_p, b_gate, tm, stage):
    t = x2d.shape[0]
    nt = t // tm
    g_rows = SLAB_ROWS - SLAB_D

    def body(x_ref, g_ref, w_ref, wgu_ref, bg_ref, stage_hbm, proj_ref, z_ref, h_ref, gwb_ref,
             send_sems, recv_sems, local_sem):
        gargs = (stage_hbm, SLAB_D, g_rows, gwb_ref, send_sems, recv_sems, local_sem)

        @pl.when(pl.program_id(0) == 0)
        def _():
            _gather_start(*gargs)

        x = x_ref[...]
        r = lax.rsqrt(jnp.mean(x * x, axis=-1, keepdims=True) + EPS)
        h = ((x * r) * g_ref[...]).astype(BF16)
        h_ref[...] = h
        proj = _nt(h, w_ref[...])
        proj_ref[...] = proj
        pa = proj[:, OA:OA + A_PAD].astype(BF16)
        z_ref[...] = _mm(pa, wgu_ref[...]) + bg_ref[...]

        @pl.when(pl.program_id(0) == nt - 1)
        def _():
            _gather_finish(*gargs)

    return pl.pallas_call(
        body,
        name="in_proj_fwd",
        grid=(t // tm,),
        in_specs=[
            pl.BlockSpec((tm, D), lambda i: (i, 0)),
            pl.BlockSpec((1, D), lambda i: (0, 0)),
            pl.BlockSpec((PW, D), lambda i: (0, 0)),
            pl.BlockSpec((A_PAD, NQK), lambda i: (0, 0)),
            pl.BlockSpec((1, NQK), lambda i: (0, 0)),
            pl.BlockSpec(memory_space=pl.ANY),
        ],
        out_specs=[
            pl.BlockSpec((tm, PW), lambda i: (i, 0)),
            pl.BlockSpec((tm, NQK), lambda i: (i, 0)),
            pl.BlockSpec((tm, D), lambda i: (i, 0)),
            pl.BlockSpec(memory_space=pl.ANY),
        ],
        out_shape=[
            jax.ShapeDtypeStruct((t, PW), F32),
            jax.ShapeDtypeStruct((t, NQK), F32),
            jax.ShapeDtypeStruct((t, D), BF16),
            jax.ShapeDtypeStruct((N_DEV, g_rows, D), BF16),
        ],
        scratch_shapes=_gather_sems(),
        compiler_params=_params(("arbitrary",)),
    )(x2d, g1, w_in_t, wgu_p, b_gate, stage)


def _head_masks():
    lane = lax.broadcasted_iota(jnp.int32, (1, NQK), 1)
    return [(lane >= DK * h) & (lane < DK * (h + 1)) for h in range(HEADS)]


def _split_bf16(x, n):
    parts = []
    for _ in range(n):
        p = x.astype(BF16)
        parts.append(p)
        x = x - p.astype(F32)
    return parts


def _chunk_fwd_parts(q, k, z, tril16):
    la = (jnp.minimum(z, 0.0) - jnp.log1p(jnp.exp(-jnp.abs(z)))) * INV_GATE_NORM
    la_parts = _split_bf16(la, 3)
    bc = _mm(tril16, la_parts[0]) + _mm(tril16, la_parts[1]) + _mm(tril16, la_parts[2])
    bl = bc[CHUNK - 1:CHUNK, :]
    eb = jnp.exp(bc)
    enb = jnp.exp(-bc)
    ekl = jnp.exp(bl - bc)
    qi = (q * Q_SCALE) * eb
    ki = k * enb
    ks = k * ekl
    ones16 = jnp.ones((CHUNK, DV), BF16)
    decb = jnp.exp(_tn(la_parts[0], ones16) + _tn(la_parts[1], ones16) + _tn(la_parts[2], ones16))
    return la, eb, enb, ekl, qi, ki, ks, decb


def _stack_heads(a, masks):
    return jnp.concatenate([jnp.where(m, a, 0.0) for m in masks], axis=0)


def _merge_heads(blocks, masks):
    out = blocks[HEADS - 1]
    for h in range(HEADS - 2, -1, -1):
        out = jnp.where(masks[h], blocks[h], out)
    return out


def _causal_stack_mask():
    row = lax.broadcasted_iota(jnp.int32, (HEADS * CHUNK, CHUNK), 0)
    col = lax.broadcasted_iota(jnp.int32, (HEADS * CHUNK, CHUNK), 1)
    return (row & (CHUNK - 1)) >= col


def _conv_taps(u, uprev):
    row = lax.broadcasted_iota(jnp.int32, u.shape, 0)
    u1 = jnp.where(row < 1, pltpu.roll(uprev, 1, 0), pltpu.roll(u, 1, 0))
    u2 = jnp.where(row < 2, pltpu.roll(uprev, 2, 0), pltpu.roll(u, 2, 0))
    return u1, u2


def _mix_fwd(proj3, z3, gng, conv_w, stage):
    nb, s, _ = proj3.shape
    nc = s // CHUNK
    g_rows = SLAB_D - SLAB_G

    def body(p_ref, z_ref, gng_ref, cw_ref, stage_hbm, mix_ref, o_ref, sprev_ref, gwa_ref, s_ref, uprev_ref,
             send_sems, recv_sems, local_sem):
        n = pl.program_id(0)
        gargs = (stage_hbm, SLAB_G, g_rows, gwa_ref, send_sems, recv_sems, local_sem)

        @pl.when(n == 0)
        def _():
            _gather_start(*gargs)
            s_ref[...] = jnp.zeros_like(s_ref)
            uprev_ref[...] = jnp.zeros_like(uprev_ref)

        r_i = lax.broadcasted_iota(jnp.int32, (CHUNK, CHUNK), 0)
        c_i = lax.broadcasted_iota(jnp.int32, (CHUNK, CHUNK), 1)
        tril16 = (r_i >= c_i).astype(BF16)
        masks = _head_masks()
        cmask = _causal_stack_mask()
        gg = gng_ref[...]
        for b in range(nb):
            q = p_ref[b, :, OQ:OQ + NQK]
            k = p_ref[b, :, OK_:OK_ + NQK]
            _, _, _, _, qi, ki, ks, decb = _chunk_fwd_parts(q, k, z_ref[b], tril16)
            qs = _stack_heads(qi, masks).astype(BF16)
            sc = jnp.where(cmask, _nt(qs, ki.astype(BF16)), 0.0).astype(BF16)
            st = s_ref[b]
            sprev_ref[b, 0] = st
            o_inter = _mm(qs, st.astype(BF16))
            v16 = p_ref[b, :, OV:OV + NV].astype(BF16)
            kv = _tn(ks.astype(BF16), v16)
            for h in range(HEADS):
                rows = slice(CHUNK * h, CHUNK * (h + 1))
                cols = slice(DV * h, DV * (h + 1))
                o = _mm(sc[rows], v16[:, cols]) + o_inter[rows]
                o_ref[b, :, cols] = o
                r = lax.rsqrt(jnp.mean(o * o, axis=-1, keepdims=True) + EPS)
                on = (o * r) * gg
                g = p_ref[b, :, OG + DV * h:OG + DV * (h + 1)]
                mix_ref[b, :, cols] = (on * (g * jax.nn.sigmoid(g))).astype(BF16)
                s_ref[b, rows, :] = decb[rows] * st[rows] + kv[rows, cols]
            u = p_ref[b, :, OCC:OCC + CW] * p_ref[b, :, OCH:OCH + CW]
            u1, u2 = _conv_taps(u, uprev_ref[b])
            yc = cw_ref[0:1, :] * u2 + cw_ref[1:2, :] * u1 + cw_ref[2:3, :] * u
            mix_ref[b, :, NV:NV + CW] = (p_ref[b, :, OCB:OCB + CW] * yc).astype(BF16)
            uprev_ref[b] = u

        @pl.when(n == nc - 1)
        def _():
            _gather_finish(*gargs)

    return pl.pallas_call(
        body,
        name="mix_fwd",
        grid=(nc,),
        in_specs=[
            pl.BlockSpec((nb, CHUNK, PW), lambda n: (0, n, 0)),
            pl.BlockSpec((nb, CHUNK, NQK), lambda n: (0, n, 0)),
            pl.BlockSpec((1, DV), lambda n: (0, 0)),
            pl.BlockSpec((CONV_K, CW), lambda n: (0, 0)),
            pl.BlockSpec(memory_space=pl.ANY),
        ],
        out_specs=[
            pl.BlockSpec((nb, CHUNK, D), lambda n: (0, n, 0)),
            pl.BlockSpec((nb, CHUNK, NV), lambda n: (0, n, 0)),
            pl.BlockSpec((nb, 1, NQK, DV), lambda n: (0, n, 0, 0)),
            pl.BlockSpec(memory_space=pl.ANY),
        ],
        out_shape=[
            jax.ShapeDtypeStruct((nb, s, D), BF16),
            jax.ShapeDtypeStruct((nb, s, NV), F32),
            jax.ShapeDtypeStruct((nb, nc, NQK, DV), F32),
            jax.ShapeDtypeStruct((N_DEV, g_rows, D), BF16),
        ],
        scratch_shapes=[pltpu.VMEM((nb, NQK, DV), F32), pltpu.VMEM((nb, CHUNK, CW), F32)] + _gather_sems(),
        compiler_params=_params(("arbitrary",)),
    )(proj3, z3, gng, conv_w, stage)


def _ffn_fwd_bwd(mix2d, x2d, tgt2d, gwa, gwb, g2, gf, tm):
    t = x2d.shape[0]

    def body(mix_ref, x_ref, tgt_ref, g2_ref, gf_ref, gwa_hbm, gwb_hbm,
             dx1_ref, dx1b_ref, dmix_ref, adu_ref, hb_ref, dg2_ref, dgf_ref, loss_ref,
             wo, wg, wu, wd, wsem):
        i = pl.program_id(0)

        def weight_copies(n, dst, src, off, rows):
            return [pltpu.make_async_copy(src.at[j, pl.ds(off, rows), :], dst.at[pl.ds(rows * j, rows), :],
                                          wsem.at[N_DEV * n + j]) for j in range(N_DEV)]

        loads = (weight_copies(0, wo, gwb_hbm, FF_W, OUT_ROWS), weight_copies(1, wg, gwa_hbm, 0, FF_W),
                 weight_copies(2, wu, gwa_hbm, FF_W, FF_W), weight_copies(3, wd, gwb_hbm, 0, FF_W))

        @pl.when(i == 0)
        def _():
            for group in loads:
                for cp in group:
                    cp.start()
            dg2_ref[...] = jnp.zeros_like(dg2_ref)
            dgf_ref[...] = jnp.zeros_like(dgf_ref)
            loss_ref[...] = jnp.zeros_like(loss_ref)
            for group in loads:
                for cp in group:
                    cp.wait()

        g2v = g2_ref[...]
        gfv = gf_ref[...]
        x1 = x_ref[...] + _mm(mix_ref[...], wo[...])
        r2 = lax.rsqrt(jnp.mean(x1 * x1, axis=-1, keepdims=True) + EPS)
        n2 = x1 * r2
        h2 = (n2 * g2v).astype(BF16)
        hb_ref[1] = h2
        gate = _nt(h2, wg[...])
        up = _nt(h2, wu[...])
        sg = jax.nn.sigmoid(gate)
        sil = gate * sg
        act = (sil * up).astype(BF16)
        adu_ref[0] = act
        x2 = x1 + _mm(act, wd[...])
        rf = lax.rsqrt(jnp.mean(x2 * x2, axis=-1, keepdims=True) + EPS)
        nf = x2 * rf
        err = nf * gfv - tgt_ref[...]
        loss_ref[...] += 0.5 * jnp.sum(jnp.mean(err * err, axis=-1, keepdims=True))
        dy = err * (1.0 / D)
        dgf_ref[...] += jnp.sum(dy * nf, axis=0, keepdims=True)
        dnf = dy * gfv
        dx2 = rf * (dnf - nf * jnp.mean(dnf * nf, axis=-1, keepdims=True))
        dx2b = dx2.astype(BF16)
        hb_ref[0] = dx2b
        dact = _nt(dx2b, wd[...])
        dup = (dact * sil).astype(BF16)
        dgate = ((dact * up) * (sg * (1.0 + gate * (1.0 - sg)))).astype(BF16)
        adu_ref[2] = dup
        adu_ref[1] = dgate
        dh2 = _mm(dgate, wg[...]) + _mm(dup, wu[...])
        dg2_ref[...] += jnp.sum(dh2 * n2, axis=0, keepdims=True)
        dn2 = dh2 * g2v
        dx1 = dx2 + r2 * (dn2 - n2 * jnp.mean(dn2 * n2, axis=-1, keepdims=True))
        dx1_ref[...] = dx1
        dx1b = dx1.astype(BF16)
        dx1b_ref[...] = dx1b
        dmix_ref[...] = _nt(dx1b, wo[...])

    tile = lambda w: pl.BlockSpec((tm, w), lambda i: (i, 0))
    vec = pl.BlockSpec((1, D), lambda i: (0, 0))
    hbm = pl.BlockSpec(memory_space=pl.ANY)
    return pl.pallas_call(
        body,
        name="ffn_fwd_bwd",
        grid=(t // tm,),
        in_specs=[tile(D), tile(D), tile(D), vec, vec, hbm, hbm],
        out_specs=[tile(D), tile(D), tile(D), pl.BlockSpec((3, tm, DFF), lambda i: (0, i, 0)),
                   pl.BlockSpec((2, tm, D), lambda i: (0, i, 0)), vec, vec,
                   pl.BlockSpec((1, 128), lambda i: (0, 0))],
        out_shape=[
            jax.ShapeDtypeStruct((t, D), F32),
            jax.ShapeDtypeStruct((t, D), BF16),
            jax.ShapeDtypeStruct((t, D), F32),
            jax.ShapeDtypeStruct((3, t, DFF), BF16),
            jax.ShapeDtypeStruct((2, t, D), BF16),
            jax.ShapeDtypeStruct((1, D), F32),
            jax.ShapeDtypeStruct((1, D), F32),
            jax.ShapeDtypeStruct((1, 128), F32),
        ],
        scratch_shapes=[pltpu.VMEM((D, D), BF16), pltpu.VMEM((DFF, D), BF16), pltpu.VMEM((DFF, D), BF16),
                        pltpu.VMEM((DFF, D), BF16), pltpu.SemaphoreType.DMA((4 * N_DEV,))],
        compiler_params=_params(("arbitrary",)),
    )(mix2d, x2d, tgt2d, g2, gf, gwa, gwb)


def _stage1_rider(stack):
    n = stack.shape[0]
    return dict(inputs=[stack], out_shape=[jax.ShapeDtypeStruct((4,) + stack.shape[2:], BF16)] * n, nsem=4 * n,
                copies=lambda ins, outs, send, recv: _stage1_copies([ins[0]] * n, list(range(n)), outs, send, recv))


def _stage2_rider(pbs):
    return dict(inputs=list(pbs), out_shape=[jax.ShapeDtypeStruct(p.shape, BF16) for p in pbs], nsem=3 * len(pbs),
                copies=_stage2_copies)


def _tn_matmul(a, b, bm, bn, tk, name, with_bf16, rider=None):
    t, m = a.shape
    n = b.shape[1]
    nk = t // tk
    nout = 2 if with_bf16 else 1
    grid = (m // bm, n // bn, nk)
    r_in = [] if rider is None else rider["inputs"]
    r_out = [] if rider is None else rider["out_shape"]

    def body(a_ref, b_ref, *rest):
        ins, outs = rest[:len(r_in)], rest[len(r_in):len(r_in) + nout]
        r_outs, sems = rest[len(r_in) + nout:len(r_in) + nout + len(r_out)], rest[len(r_in) + nout + len(r_out):]
        o_ref = outs[0]
        i, j, k = pl.program_id(0), pl.program_id(1), pl.program_id(2)
        if rider is not None:
            @pl.when((i == 0) & (j == 0) & (k == 0))
            def _():
                for cp in rider["copies"](ins, r_outs, *sems):
                    cp.start()

        @pl.when(k == 0)
        def _():
            o_ref[...] = jnp.zeros_like(o_ref)

        o_ref[...] += _tn(a_ref[...].astype(BF16), b_ref[...].astype(BF16))
        if with_bf16:
            @pl.when(k == nk - 1)
            def _():
                outs[1][...] = o_ref[...].astype(BF16)
        if rider is not None:
            @pl.when((i == grid[0] - 1) & (j == grid[1] - 1) & (k == nk - 1))
            def _():
                copies = rider["copies"](ins, r_outs, *sems)
                for cp in copies:
                    cp.wait_recv()
                for cp in copies:
                    cp.wait_send()

    out_blk = pl.BlockSpec((bm, bn), lambda i, j, k: (i, j))
    hbm = pl.BlockSpec(memory_space=pl.ANY)
    out_shape = [jax.ShapeDtypeStruct((m, n), F32)] + ([jax.ShapeDtypeStruct((m, n), BF16)] if with_bf16 else [])
    res = pl.pallas_call(
        body,
        name=name,
        grid=grid,
        in_specs=[pl.BlockSpec((tk, bm), lambda i, j, k: (k, i)), pl.BlockSpec((tk, bn), lambda i, j, k: (k, j))]
        + [hbm] * len(r_in),
        out_specs=[out_blk] * nout + [hbm] * len(r_out),
        out_shape=out_shape + list(r_out),
        scratch_shapes=([] if rider is None else
                        [pltpu.SemaphoreType.DMA((rider["nsem"],)), pltpu.SemaphoreType.DMA((rider["nsem"],))]),
        compiler_params=_params(("parallel", "parallel", "arbitrary") if rider is None
                                else ("arbitrary", "arbitrary", "arbitrary")),
    )(a, b, *r_in)
    return res[0] if len(res) == 1 else res


def _dw_ffn(adu, hb, tk):
    _, t, _ = adu.shape
    bm = DFF // 2
    nk = t // tk

    def body(a_ref, b_ref, o_ref, ob_ref):
        k = pl.program_id(2)

        @pl.when(k == 0)
        def _():
            o_ref[...] = jnp.zeros_like(o_ref)

        o_ref[...] += _tn(a_ref[...], b_ref[...])

        @pl.when(k == nk - 1)
        def _():
            ob_ref[...] = o_ref[...].astype(BF16)

    out_blk = pl.BlockSpec((None, bm, D), lambda p, i, k: (p, i, 0))
    return pl.pallas_call(
        body,
        name="dw_ffn",
        grid=(3, DFF // bm, nk),
        in_specs=[pl.BlockSpec((None, tk, bm), lambda p, i, k: (p, k, i)),
                  pl.BlockSpec((None, tk, D), lambda p, i, k: (jnp.minimum(p, 1), k, 0))],
        out_specs=[out_blk, out_blk],
        out_shape=[jax.ShapeDtypeStruct((3, DFF, D), F32), jax.ShapeDtypeStruct((3, DFF, D), BF16)],
        compiler_params=_params(("arbitrary", "arbitrary", "arbitrary")),
    )(adu, hb)


def _at_owner(ref, lead, idx):
    return ref.at[idx] if lead is None else ref.at[lead, idx]


def _mix_bwd(proj3, z3, sprev, opre3, dmix3, gng, conv_w, wgu_p, pbs):
    nb, s, _ = proj3.shape
    nc = s // CHUNK
    na = len(pbs)

    def body(*refs):
        (p_ref, pprev_ref, z_ref, sp_ref, o_ref, dm_ref, gng_ref, cw_ref, wgu_ref) = refs[:9]
        pb_refs = refs[9:9 + na]
        (dproj_ref, dgng_ref, dcw_ref, dbg_ref, dwgu_ref) = refs[9 + na:14 + na]
        r2_refs = refs[14 + na:14 + 2 * na]
        ds_ref, dycn_ref, send_sems, recv_sems = refs[14 + 2 * na:]
        step = pl.program_id(0)
        n = nc - 1 - step

        @pl.when(step == 0)
        def _():
            for cp in _stage2_copies(pb_refs, r2_refs, send_sems, recv_sems):
                cp.start()
            ds_ref[...] = jnp.zeros_like(ds_ref)
            dycn_ref[...] = jnp.zeros_like(dycn_ref)
            dgng_ref[...] = jnp.zeros_like(dgng_ref)
            dcw_ref[...] = jnp.zeros_like(dcw_ref)
            dbg_ref[...] = jnp.zeros_like(dbg_ref)
            dwgu_ref[...] = jnp.zeros_like(dwgu_ref)

        r_i = lax.broadcasted_iota(jnp.int32, (CHUNK, CHUNK), 0)
        c_i = lax.broadcasted_iota(jnp.int32, (CHUNK, CHUNK), 1)
        tril16 = (r_i >= c_i).astype(BF16)
        triu16 = (r_i <= c_i).astype(BF16)
        causal = r_i >= c_i
        masks = _head_masks()
        cmask = _causal_stack_mask()
        gg = gng_ref[...]
        last_row = lax.broadcasted_iota(jnp.int32, (CHUNK, NQK), 0) == CHUNK - 1
        ones_r = jnp.ones((16, DV), BF16)
        has_prev = (n > 0).astype(F32)
        for b in range(nb):
            q = p_ref[b, :, OQ:OQ + NQK]
            k = p_ref[b, :, OK_:OK_ + NQK]
            z = z_ref[b]
            _, eb, enb, ekl, qi, ki, ks, decb = _chunk_fwd_parts(q, k, z, tril16)
            qi16 = qi.astype(BF16)
            ki16 = ki.astype(BF16)
            qs = _stack_heads(qi, masks).astype(BF16)
            sc = jnp.where(cmask, _nt(qs, ki16), 0.0).astype(BF16)
            st = sp_ref[b, 0]
            st16 = st.astype(BF16)
            dsn = ds_ref[b]
            dsn16 = dsn.astype(BF16)
            v16 = p_ref[b, :, OV:OV + NV].astype(BF16)
            do16 = []
            dgng = jnp.zeros((1, DV), F32)
            for h in range(HEADS):
                cols = slice(DV * h, DV * (h + 1))
                o = o_ref[b, :, cols]
                r = lax.rsqrt(jnp.mean(o * o, axis=-1, keepdims=True) + EPS)
                nh = o * r
                g = p_ref[b, :, OG + DV * h:OG + DV * (h + 1)]
                sg = jax.nn.sigmoid(g)
                dog = dm_ref[b, :, cols]
                dproj_ref[b, :, OG + DV * h:OG + DV * (h + 1)] = (
                    (dog * (nh * gg)) * (sg * (1.0 + g * (1.0 - sg)))).astype(BF16)
                don = dog * (g * sg)
                dgng = dgng + jnp.sum(don * nh, axis=0, keepdims=True)
                dn = don * gg
                do = r * (dn - nh * jnp.mean(dn * nh, axis=-1, keepdims=True))
                do16.append(do.astype(BF16))
            dgng_ref[...] += dgng
            do_rows = jnp.concatenate(do16, axis=0)
            v_rows = jnp.concatenate([v16[:, DV * h:DV * (h + 1)] for h in range(HEADS)], axis=0)
            dp16 = [jnp.where(causal, _nt(do16[h], v16[:, DV * h:DV * (h + 1)]), 0.0).astype(BF16)
                    for h in range(HEADS)]
            ks_dsn = _mm(_stack_heads(ks, masks).astype(BF16), dsn16)
            do_st = _nt(do_rows, st16)
            v_dsn = _nt(v_rows, dsn16)
            dp_ki = _mm(jnp.concatenate(dp16, axis=0), ki16)
            q_do = _tn(qi16, jnp.concatenate(do16, axis=1))
            dki_h = []
            for h in range(HEADS):
                rows = slice(CHUNK * h, CHUNK * (h + 1))
                cols = slice(DV * h, DV * (h + 1))
                dv = _tn(sc[rows], do16[h]) + ks_dsn[rows]
                dproj_ref[b, :, OV + DV * h:OV + DV * (h + 1)] = dv.astype(BF16)
                dki_h.append(_tn(dp16[h], qi16))
                ds_ref[b, rows, :] = decb[rows] * dsn[rows] + q_do[rows, cols]
            blocks = lambda a: [a[CHUNK * h:CHUNK * (h + 1)] for h in range(HEADS)]
            dqi = _merge_heads(blocks(dp_ki + do_st), masks)
            dki = _merge_heads(dki_h, masks)
            dks = _merge_heads(blocks(v_dsn), masks)
            dproj_ref[b, :, OQ:OQ + NQK] = (dqi * (Q_SCALE * eb)).astype(BF16)
            dproj_ref[b, :, OK_:OK_ + NQK] = (dki * enb + dks * ekl).astype(BF16)
            dks_ks = dks * ks
            db = dqi * qi - dki * ki - dks_ks
            sd = _split_bf16(dsn * st * decb, 2)
            dbl = jnp.sum(dks_ks, axis=0, keepdims=True) + (_nt(ones_r, sd[0]) + _nt(ones_r, sd[1]))[0:1, :]
            db = db + jnp.where(last_row, dbl, 0.0)
            db_parts = _split_bf16(db, 3)
            dla = _mm(triu16, db_parts[0]) + _mm(triu16, db_parts[1]) + _mm(triu16, db_parts[2])
            dz = (dla * INV_GATE_NORM) * (1.0 / (1.0 + jnp.exp(z)))
            dbg_ref[...] += jnp.sum(dz, axis=0, keepdims=True)
            dz16 = dz.astype(BF16)
            pa16 = p_ref[b, :, OA:OA + A_PAD].astype(BF16)
            dwgu_ref[...] += _tn(pa16, dz16)
            dproj_ref[b, :, OA:OA + A_PAD] = _nt(dz16, wgu_ref[...]).astype(BF16)
            cb = p_ref[b, :, OCB:OCB + CW]
            cc = p_ref[b, :, OCC:OCC + CW]
            ch = p_ref[b, :, OCH:OCH + CW]
            u = cc * ch
            uprev = (pprev_ref[b, :, 0:CW] * pprev_ref[b, :, CW:2 * CW]) * has_prev
            u1, u2 = _conv_taps(u, uprev)
            w0 = cw_ref[0:1, :]
            w1 = cw_ref[1:2, :]
            w2 = cw_ref[2:3, :]
            yc = w0 * u2 + w1 * u1 + w2 * u
            doc = dm_ref[b, :, NV:NV + CW]
            dproj_ref[b, :, OCB:OCB + CW] = (doc * yc).astype(BF16)
            dyc = doc * cb
            dycn = dycn_ref[b]
            row = lax.broadcasted_iota(jnp.int32, dyc.shape, 0)
            d1 = jnp.where(row >= CHUNK - 1, pltpu.roll(dycn, CHUNK - 1, 0), pltpu.roll(dyc, CHUNK - 1, 0))
            d2 = jnp.where(row >= CHUNK - 2, pltpu.roll(dycn, CHUNK - 2, 0), pltpu.roll(dyc, CHUNK - 2, 0))
            du = w2 * dyc + w1 * d1 + w0 * d2
            dproj_ref[b, :, OCC:OCC + CW] = (du * ch).astype(BF16)
            dproj_ref[b, :, OCH:OCH + CW] = (du * cc).astype(BF16)
            dcw_ref[0:1, :] += jnp.sum(dyc * u2, axis=0, keepdims=True)
            dcw_ref[1:2, :] += jnp.sum(dyc * u1, axis=0, keepdims=True)
            dcw_ref[2:3, :] += jnp.sum(dyc * u, axis=0, keepdims=True)
            dycn_ref[b] = dyc

        @pl.when(step == nc - 1)
        def _():
            copies = _stage2_copies(pb_refs, r2_refs, send_sems, recv_sems)
            for cp in copies:
                cp.wait_recv()
            for cp in copies:
                cp.wait_send()

    rev = lambda w: pl.BlockSpec((nb, CHUNK, w), lambda i: (0, nc - 1 - i, 0))
    const = lambda r, c: pl.BlockSpec((r, c), lambda i: (0, 0))
    hbm = pl.BlockSpec(memory_space=pl.ANY)
    return pl.pallas_call(
        body,
        name="mix_bwd",
        grid=(nc,),
        in_specs=[
            rev(PW),
            pl.BlockSpec((nb, CHUNK, 2 * CW), lambda i: (0, jnp.maximum(nc - 2 - i, 0), OCC // (2 * CW))),
            rev(NQK),
            pl.BlockSpec((nb, 1, NQK, DV), lambda i: (0, nc - 1 - i, 0, 0)),
            rev(NV),
            rev(D),
            const(1, DV),
            const(CONV_K, CW),
            const(A_PAD, NQK),
        ] + [hbm] * na,
        out_specs=[rev(PW), const(1, DV), const(8, CW), const(1, NQK), const(A_PAD, NQK)] + [hbm] * na,
        out_shape=[
            jax.ShapeDtypeStruct((nb, s, PW), BF16),
            jax.ShapeDtypeStruct((1, DV), F32),
            jax.ShapeDtypeStruct((8, CW), F32),
            jax.ShapeDtypeStruct((1, NQK), F32),
            jax.ShapeDtypeStruct((A_PAD, NQK), F32),
        ] + [jax.ShapeDtypeStruct((3,) + p.shape[1:], BF16) for p in pbs],
        scratch_shapes=[pltpu.VMEM((nb, NQK, DV), F32), pltpu.VMEM((nb, CHUNK, CW), F32),
                        pltpu.SemaphoreType.DMA((3 * na,)), pltpu.SemaphoreType.DMA((3 * na,))],
        compiler_params=_params(("arbitrary",)),
    )(proj3, proj3, z3, sprev, opre3, dmix3, gng, conv_w, wgu_p, *pbs)


SMALL_PACK_ROWS = 16


def _wgu_slot(r):
    return 4 + r // 4, NQK * (r % 4)


CONV_SLOTS = ((8, 0), (8, CW), (9, 0))


def _in_proj_bwd(dproj2d, x2d, dx1, g1, w_in_t, tm, pb, small_parts):
    t = x2d.shape[0]
    nt = t // tm

    def body(dp_ref, x_ref, dx1_ref, g_ref, w_ref, pb_ref, dg2, dgf, dbg, dgng, dwgu, dcw, lp,
             dx_ref, sums_ref, r2_ref, dg1_acc, pack, gbuf, pack1, gbuf1, send_sems, recv_sems,
             ssend, srecv, ssend1, srecv1):
        x, y, c = _position()
        me = 4 * x + 2 * y + c
        flips = [(k >> 2, (k >> 1) & 1, k & 1) for k in range(1, N_DEV)]
        peers = [(x ^ fx, y ^ fy, c ^ fc) for fx, fy, fc in flips]

        def small_copies(src, dst, send, recv, arrivals):
            return [pltpu.make_async_remote_copy(
                src_ref=src, dst_ref=dst.at[4 * px + 2 * py + pc if arrivals else me],
                send_sem=send.at[k], recv_sem=recv.at[k], device_id=(px, py, pc), device_id_type=MESH)
                for k, (px, py, pc) in enumerate(peers)]

        @pl.when(pl.program_id(0) == 0)
        def _():
            for cp in _stage2_copies([pb_ref], [r2_ref], send_sems, recv_sems):
                cp.start()
            dg1_acc[...] = jnp.zeros_like(dg1_acc)
            pack[...] = jnp.zeros_like(pack)
            pack[1:2, :] = dg2[...]
            pack[2:3, :] = dgf[...]
            pack[3:4, 0:NQK] = dbg[...]
            pack[3:4, NQK:NQK + DV] = dgng[...]
            pack[3:4, NQK + DV:NQK + 2 * DV] = lp[...]
            for r in range(RANK):
                row, lane = _wgu_slot(r)
                pack[row:row + 1, lane:lane + NQK] = dwgu[r:r + 1, :]
            for r, (row, lane) in enumerate(CONV_SLOTS):
                pack[row:row + 1, lane:lane + CW] = dcw[r:r + 1, :]
            for cp in small_copies(pack, gbuf, ssend, srecv, False):
                cp.start()
            gbuf[me] = pack[...]

        xv = x_ref[...]
        r = lax.rsqrt(jnp.mean(xv * xv, axis=-1, keepdims=True) + EPS)
        n1 = xv * r
        dh = _mm(dp_ref[...], w_ref[...])
        dg1_acc[...] += jnp.sum(dh * n1, axis=0, keepdims=True)
        dn = dh * g_ref[...]
        dx_ref[...] = dx1_ref[...] + r * (dn - n1 * jnp.mean(dn * n1, axis=-1, keepdims=True))

        @pl.when(pl.program_id(0) == nt - 1)
        def _():
            pack1[...] = jnp.zeros_like(pack1)
            pack1[0:1, :] = dg1_acc[...]
            for cp in small_copies(pack1, gbuf1, ssend1, srecv1, False):
                cp.start()
            gbuf1[me] = pack1[...]
            copies = _stage2_copies([pb_ref], [r2_ref], send_sems, recv_sems)
            for cp in copies:
                cp.wait_recv()
            for cp in copies:
                cp.wait_send()
            for src, dst, send, recv in ((pack, gbuf, ssend, srecv), (pack1, gbuf1, ssend1, srecv1)):
                for cp in small_copies(src, dst, send, recv, True):
                    cp.wait_recv()
                    cp.wait_send()
            acc = gbuf[0]
            acc1 = gbuf1[0]
            for d in range(1, N_DEV):
                acc = acc + gbuf[d]
                acc1 = acc1 + gbuf1[d]
            sums_ref[...] = acc
            sums_ref[0:1, :] = acc1[0:1, :]

    tile = lambda w: pl.BlockSpec((tm, w), lambda i: (i, 0))
    vec = pl.BlockSpec((1, D), lambda i: (0, 0))
    hbm = pl.BlockSpec(memory_space=pl.ANY)
    whole = lambda a: pl.BlockSpec(a.shape, lambda i: (0,) * a.ndim)
    return pl.pallas_call(
        body,
        name="in_proj_bwd",
        grid=(nt,),
        in_specs=[tile(PW), tile(D), tile(D), vec, pl.BlockSpec((PW, D), lambda i: (0, 0)), hbm]
        + [whole(a) for a in small_parts],
        out_specs=[tile(D), pl.BlockSpec((SMALL_PACK_ROWS, D), lambda i: (0, 0)), hbm],
        out_shape=[jax.ShapeDtypeStruct((t, D), F32), jax.ShapeDtypeStruct((SMALL_PACK_ROWS, D), F32),
                   jax.ShapeDtypeStruct((3,) + pb.shape[1:], BF16)],
        scratch_shapes=[pltpu.VMEM((1, D), F32),
                        pltpu.VMEM((SMALL_PACK_ROWS, D), F32), pltpu.VMEM((N_DEV, SMALL_PACK_ROWS, D), F32),
                        pltpu.VMEM((8, D), F32), pltpu.VMEM((N_DEV, 8, D), F32),
                        pltpu.SemaphoreType.DMA((3,)), pltpu.SemaphoreType.DMA((3,)),
                        pltpu.SemaphoreType.DMA((7,)), pltpu.SemaphoreType.DMA((7,)),
                        pltpu.SemaphoreType.DMA((7,)), pltpu.SemaphoreType.DMA((7,))],
        compiler_params=_params(("arbitrary",)),
    )(dproj2d, x2d, dx1, g1, w_in_t, pb, *small_parts)


def _adamw_math(w, g, m, v):
    m = ADAM_B1 * m + (1.0 - ADAM_B1) * g
    v = ADAM_B2 * v + (1.0 - ADAM_B2) * (g * g)
    m_hat = m / (1.0 - ADAM_B1 ** ADAM_STEP)
    v_hat = v / (1.0 - ADAM_B2 ** ADAM_STEP)
    delta = -ADAM_LR * (m_hat / (jnp.sqrt(v_hat) + ADAM_EPS) + ADAM_WD * w)
    return delta, m, v


def _position():
    return lax.axis_index("x"), lax.axis_index("y"), lax.axis_index("c")


GATHER_PARTS = 2
GATHER_SEMS = 7 * GATHER_PARTS


def _gather_copies(stage, lo, rows, gx, send_sems, recv_sems, local_sem):
    x, y, c = _position()
    me = (x, y, c)
    sibling = (x, y, 1 - c)
    chips = [(1 - x, y), (x, 1 - y), (1 - x, 1 - y)]
    part = -(-rows // (16 * GATHER_PARTS)) * 16
    bounds = [(p * part, min(part, rows - p * part)) for p in range(GATHER_PARTS)]

    def blk(px, py, pc, off, n):
        return gx.at[4 * px + 2 * py + pc, pl.ds(off, n), :]

    mine = pltpu.make_async_copy(stage.at[pl.ds(lo, rows), :], gx.at[4 * x + 2 * y + c], local_sem)
    parts = []
    for p, (off, n) in enumerate(bounds):
        def copy(k, block, to, from_stage=False, p=p, off=off, n=n):
            return pltpu.make_async_remote_copy(
                src_ref=stage.at[pl.ds(lo + off, n), :] if from_stage else blk(*block, off, n),
                dst_ref=blk(*block, off, n), send_sem=send_sems.at[7 * p + k], recv_sem=recv_sems.at[7 * p + k],
                device_id=to, device_id_type=MESH)

        first = [copy(0, me, sibling, True)] + [copy(1 + j, me, (*chip, c), True) for j, chip in enumerate(chips)]
        passed = [copy(4 + j, (*chip, c), sibling) for j, chip in enumerate(chips)]
        arrivals = ([copy(0, sibling, me)] + [copy(1 + j, (*chip, c), me) for j, chip in enumerate(chips)]
                    + [copy(4 + j, (*chip, 1 - c), me) for j, chip in enumerate(chips)])
        parts.append((first, passed, arrivals))
    return mine, parts


def _gather_start(*args):
    mine, parts = _gather_copies(*args)
    mine.start()
    for first, _, _ in parts:
        for cp in first:
            cp.start()


def _gather_finish(*args):
    mine, parts = _gather_copies(*args)
    for _, passed, arrivals in parts:
        for j in range(3):
            arrivals[1 + j].wait_recv()
            passed[j].start()
    for first, passed, arrivals in parts:
        arrivals[0].wait_recv()
        for j in range(3):
            arrivals[4 + j].wait_recv()
        for cp in first + passed:
            cp.wait_send()
    mine.wait()


def _gather_sems():
    return [pltpu.SemaphoreType.DMA((GATHER_SEMS,)), pltpu.SemaphoreType.DMA((GATHER_SEMS,)), pltpu.SemaphoreType.DMA]


def _gather_w_in(w_it, w_gt, w_ut, w_d, w_o, wgu_s, conv_s):
    def body(wi_ref, wg_ref, wu_ref, wd_ref, wo_ref, wgu_ref, conv_ref, w_ref, gwgu_ref, gconv_ref, stage,
             buf, send_sems, recv_sems, local_sem, ssend, srecv):
        x, y, c = _position()
        me = 4 * x + 2 * y + c
        stage[SLAB_IN:SLAB_IN + IN_W, :] = wi_ref[...].astype(BF16)
        stage[SLAB_IN + IN_W:SLAB_G, :] = jnp.zeros((IN_ROWS - IN_W, D), BF16)
        args = (stage, SLAB_IN, IN_ROWS, buf, send_sems, recv_sems, local_sem)
        _gather_start(*args)
        stage[SLAB_G:SLAB_U, :] = wg_ref[...].astype(BF16)
        stage[SLAB_U:SLAB_D, :] = wu_ref[...].astype(BF16)
        stage[SLAB_D:SLAB_O, :] = wd_ref[...].astype(BF16)
        stage[SLAB_O:SLAB_ROWS, :] = wo_ref[...].astype(BF16)
        flips = [(k >> 2, (k >> 1) & 1, k & 1) for k in range(1, N_DEV)]
        peers = [(x ^ fx, y ^ fy, c ^ fc) for fx, fy, fc in flips]

        def small(k, block_id, to):
            return [pltpu.make_async_remote_copy(
                src_ref=s, dst_ref=g.at[block_id], send_sem=ssend.at[2 * k + n], recv_sem=srecv.at[2 * k + n],
                device_id=to, device_id_type=MESH)
                for n, (s, g) in enumerate(((wgu_ref, gwgu_ref), (conv_ref, gconv_ref)))]

        gwgu_ref[me] = wgu_ref[...]
        gconv_ref[me] = conv_ref[...]
        for k, peer in enumerate(peers):
            for cp in small(k, me, peer):
                cp.start()
        w_ref[IN_COLS:PW, :] = jnp.zeros((PW - IN_COLS, D), BF16)
        _gather_finish(*args)
        for k, (px, py, pc) in enumerate(peers):
            for cp in small(k, 4 * px + 2 * py + pc, (px, py, pc)):
                cp.wait_recv()
                cp.wait_send()
        for j, lo, hi, d in _in_segments():
            w_ref[d:d + hi - lo, :] = buf[j, lo:hi, :]

    vm = pl.BlockSpec(memory_space=pltpu.VMEM)
    return pl.pallas_call(
        body,
        name="gather_w_in",
        in_specs=[vm] * 7,
        out_specs=[vm] * 4,
        out_shape=[jax.ShapeDtypeStruct((PW, D), BF16),
                   jax.ShapeDtypeStruct((N_DEV,) + wgu_s.shape, F32),
                   jax.ShapeDtypeStruct((N_DEV,) + conv_s.shape, F32),
                   jax.ShapeDtypeStruct((SLAB_ROWS, D), BF16)],
        scratch_shapes=[pltpu.VMEM((N_DEV, IN_ROWS, D), BF16)] + _gather_sems()
        + [pltpu.SemaphoreType.DMA((14,)), pltpu.SemaphoreType.DMA((14,))],
        compiler_params=_params(),
    )(w_it, w_gt, w_ut, w_d, w_o, wgu_s, conv_s)


def _w_in_core_reduce(dw_t):
    def body(d_ref, own_ref, sib_ref, pb_ref, g, gb, r1, send_sems, recv_sems):
        x, y, c = _position()
        chip = 2 * x + y
        for j in range(N_DEV):
            g[j, IN_W:IN_ROWS, :] = jnp.zeros((IN_ROWS - IN_W, D), F32)
        for j, lo, hi, d in _in_segments():
            g[j, lo:hi, :] = d_ref[d:d + hi - lo, :]
        for j in range(N_DEV):
            gb[j] = g[j].astype(BF16)
        copies = _stage1_copies([gb], [None], [r1], send_sems, recv_sems)
        for cp in copies:
            cp.start()
        own_ref[0] = g[2 * chip + c]
        for cp in copies:
            cp.wait_recv()
        sib_ref[0] = r1[chip]
        for k in range(1, 4):
            t = chip ^ k
            pb_ref[k - 1] = (g[2 * t + c] + r1[t].astype(F32)).astype(BF16)
        for cp in copies:
            cp.wait_send()

    vm = pl.BlockSpec(memory_space=pltpu.VMEM)
    return pl.pallas_call(
        body,
        name="w_in_core_reduce",
        in_specs=[vm],
        out_specs=[vm, vm, vm],
        out_shape=[jax.ShapeDtypeStruct((1, IN_ROWS, D), F32), jax.ShapeDtypeStruct((1, IN_ROWS, D), BF16),
                   jax.ShapeDtypeStruct((3, IN_ROWS, D), BF16)],
        scratch_shapes=[pltpu.VMEM((N_DEV, IN_ROWS, D), F32), pltpu.VMEM((N_DEV, IN_ROWS, D), BF16),
                        pltpu.VMEM((4, IN_ROWS, D), BF16), pltpu.SemaphoreType.DMA((4,)),
                        pltpu.SemaphoreType.DMA((4,))],
        compiler_params=_params(),
    )(dw_t)


def _stage1_copies(g_refs, leads, r_refs, send_sems, recv_sems):
    x, y, c = _position()
    return [pltpu.make_async_remote_copy(
        src_ref=_at_owner(g_refs[a], leads[a], 2 * i + 1 - c), dst_ref=r_refs[a].at[i],
        send_sem=send_sems.at[4 * a + i], recv_sem=recv_sems.at[4 * a + i],
        device_id=(x, y, 1 - c), device_id_type=MESH) for a in range(len(g_refs)) for i in range(4)]


def _ffn_core_reduce(dw3, r1_ffn, dw_o, dwb_o, pos_arr):
    def body(pos_ref, g0, g1, g2, ra, rb, rc, go, go_own, gbo_hbm, p0, p1, p2, po, own_ref, sib_ref,
             r1o, send_sems, recv_sems):
        k = pl.program_id(0)
        chip = pos_ref[1]

        @pl.when(k == 0)
        def _():
            copies = _stage1_copies([gbo_hbm], [None], [r1o], send_sems, recv_sems)
            for cp in copies:
                cp.start()
            for cp in copies:
                cp.wait_recv()

        for g, r, p in ((g0, ra, p0), (g1, rb, p1), (g2, rc, p2)):
            p[...] = (g[...] + r[...].astype(F32)).astype(BF16)
        po[0] = (go[0] + r1o[chip ^ (k + 1)].astype(F32)).astype(BF16)

        @pl.when(k == 2)
        def _():
            own_ref[...] = go_own[...]
            sib_ref[0] = r1o[chip]
            for cp in _stage1_copies([gbo_hbm], [None], [r1o], send_sems, recv_sems):
                cp.wait_send()

    other = lambda k, pos: 2 * (pos[1] ^ (k + 1)) + pos[0]
    g_spec = lambda lead: pl.BlockSpec((None, 1, FF_W, D), lambda k, pos: (lead, other(k, pos), 0, 0))
    r_spec = pl.BlockSpec((1, FF_W, D), lambda k, pos: (pos[1] ^ (k + 1), 0, 0))
    slot = lambda rows: pl.BlockSpec((1, rows, D), lambda k, pos: (k, 0, 0))
    one = pl.BlockSpec((1, OUT_ROWS, D), lambda k, pos: (0, 0, 0))
    return pl.pallas_call(
        body,
        name="ffn_core_reduce",
        grid_spec=pltpu.PrefetchScalarGridSpec(
            num_scalar_prefetch=1, grid=(3,),
            in_specs=[g_spec(0), g_spec(1), g_spec(2), r_spec, r_spec, r_spec,
                      pl.BlockSpec((1, OUT_ROWS, D), lambda k, pos: (other(k, pos), 0, 0)),
                      pl.BlockSpec((1, OUT_ROWS, D), lambda k, pos: (2 * pos[1] + pos[0], 0, 0)),
                      pl.BlockSpec(memory_space=pl.ANY)],
            out_specs=[slot(FF_W), slot(FF_W), slot(FF_W), slot(OUT_ROWS), one, one],
            scratch_shapes=[pltpu.VMEM((4, OUT_ROWS, D), BF16), pltpu.SemaphoreType.DMA((4,)),
                            pltpu.SemaphoreType.DMA((4,))]),
        out_shape=[jax.ShapeDtypeStruct((3, FF_W, D), BF16)] * 3 + [
            jax.ShapeDtypeStruct((3, OUT_ROWS, D), BF16), jax.ShapeDtypeStruct((1, OUT_ROWS, D), F32),
            jax.ShapeDtypeStruct((1, OUT_ROWS, D), BF16)],
        compiler_params=_params(("arbitrary",)),
    )(pos_arr, dw3, dw3, dw3, *r1_ffn, dw_o, dw_o, dwb_o)


def _stage2_copies(p_refs, r_refs, send_sems, recv_sems):
    x, y, c = _position()
    copies = []
    for a in range(len(p_refs)):
        for k in range(1, 4):
            copies.append(pltpu.make_async_remote_copy(
                src_ref=p_refs[a].at[k - 1], dst_ref=r_refs[a].at[k - 1],
                send_sem=send_sems.at[3 * a + k - 1], recv_sem=recv_sems.at[3 * a + k - 1],
                device_id=(x ^ (k >> 1), y ^ (k & 1), c), device_id_type=MESH))
    return copies


def _finish_weights(items, pos_arr, name, nblk):
    n = len(items)
    in_specs, out_specs, out_shape, operands, wbs = [], [], [], [], []
    for g8, lead, r1, r2, w, m, v in items:
        rows, wr = g8.shape[-2], w.shape[0]
        assert rows % nblk == 0 and wr % nblk == 0 and (nblk == 1 or (rows == wr and rows % (16 * nblk) == 0))
        rb, wb = rows // nblk, wr // nblk
        own = g8.shape[0] == 1
        if own:
            g_spec = pl.BlockSpec((1, rb, D), lambda i, pos: (0, i, 0))
        elif lead is None:
            g_spec = pl.BlockSpec((1, rb, D), lambda i, pos: (2 * pos[1] + pos[0], i, 0))
        else:
            g_spec = pl.BlockSpec((None, 1, rb, D), lambda i, pos, lead=lead: (lead, 2 * pos[1] + pos[0], i, 0))
        r1_spec = pl.BlockSpec((1, rb, D), (lambda i, pos: (0, i, 0)) if own else (lambda i, pos: (pos[1], i, 0)))
        wblk = pl.BlockSpec((wb, D), lambda i, pos: (i, 0))
        in_specs += [g_spec, r1_spec, pl.BlockSpec((3, rb, D), lambda i, pos: (0, i, 0)), wblk, wblk, wblk]
        out_specs += [wblk] * 4
        out_shape += [jax.ShapeDtypeStruct(w.shape, F32)] * 4
        operands += [g8, r1, r2, w, m, v]
        wbs.append(wb)

    def body(pos_ref, *refs):
        for a in range(n):
            g_ref, r1_ref, r2_ref, w_ref, m_ref, v_ref = refs[6 * a:6 * a + 6]
            g_out, d_out, m_out, v_out = refs[6 * n + 4 * a:6 * n + 4 * a + 4]
            g = g_ref[0] + r1_ref[0].astype(F32)
            for k in range(3):
                g = g + r2_ref[k].astype(F32)
            g = g[0:wbs[a], :]
            g_out[...] = g
            d, mn, vn = _adamw_math(w_ref[...], g, m_ref[...], v_ref[...])
            d_out[...] = d
            m_out[...] = mn
            v_out[...] = vn

    return pl.pallas_call(
        body,
        name=name,
        grid_spec=pltpu.PrefetchScalarGridSpec(
            num_scalar_prefetch=1, grid=(nblk,), in_specs=in_specs, out_specs=out_specs),
        out_shape=out_shape,
        compiler_params=_params(("arbitrary",)),
    )(pos_arr, *operands)


SMALL_NAMES = ("norm1_g", "norm2_g", "norm_f_g", "b_gate", "gla_norm_g", "w_gate_up", "conv_w")
WGU_W = NQK // N_DEV
CONV_W = CW // N_DEV


def _small_adamw(sums, ws, ms, vs):
    n = len(SMALL_NAMES)

    def body(*refs):
        acc_ref = refs[0]
        w_refs, m_refs, v_refs = refs[1:1 + n], refs[1 + n:1 + 2 * n], refs[1 + 2 * n:1 + 3 * n]
        loss_ref = refs[1 + 3 * n]
        outs = refs[2 + 3 * n:]
        x, y, c = _position()
        me = 4 * x + 2 * y + c
        acc = acc_ref[...]
        loss_ref[...] = acc[3:4, NQK + DV:NQK + DV + 1]

        def my_columns(full, width):
            r = lax.broadcasted_iota(jnp.int32, (full.shape[1], width), 0)
            col = lax.broadcasted_iota(jnp.int32, (full.shape[1], width), 1)
            sel = (r == width * me + col).astype(F32)
            return _mm(full, sel, precision=HIGHEST)

        dwgu = jnp.concatenate([acc[row:row + 1, lane:lane + NQK] for row, lane in map(_wgu_slot, range(RANK))], axis=0)
        dcw = jnp.concatenate([acc[row:row + 1, lane:lane + CW] for row, lane in CONV_SLOTS], axis=0)
        grads = [acc[0:1, :], acc[1:2, :], acc[2:3, :], acc[3:4, 0:NQK], acc[3:4, NQK:NQK + DV],
                 my_columns(dwgu, WGU_W), my_columns(dcw, CONV_W)]
        for i, g in enumerate(grads):
            d, mn, vn = _adamw_math(w_refs[i][...], g, m_refs[i][...], v_refs[i][...])
            outs[4 * i][...] = g
            outs[4 * i + 1][...] = d
            outs[4 * i + 2][...] = mn
            outs[4 * i + 3][...] = vn

    vm = pl.BlockSpec(memory_space=pltpu.VMEM)
    out_shape = [jax.ShapeDtypeStruct((1, 1), F32)]
    for w in ws:
        out_shape += [jax.ShapeDtypeStruct(w.shape, F32)] * 4
    return pl.pallas_call(
        body,
        name="small_adamw",
        in_specs=[vm] * (1 + 3 * n),
        out_specs=[vm] * (1 + 4 * n),
        out_shape=out_shape,
        compiler_params=_params(),
    )(sums, *ws, *ms, *vs)


def kernel(x, norm1_g, w_in, w_gate_up, b_gate, gla_norm_g, conv_w, w_out, norm2_g, w_ffn_gate, w_ffn_up, w_ffn_down, norm_f_g, loss_target, m_norm1_g, m_w_in, m_w_gate_up, m_b_gate, m_gla_norm_g, m_conv_w, m_w_out, m_norm2_g, m_w_ffn_gate, m_w_ffn_up, m_w_ffn_down, m_norm_f_g, v_norm1_g, v_w_in, v_w_gate_up, v_b_gate, v_gla_norm_g, v_conv_w, v_w_out, v_norm2_g, v_w_ffn_gate, v_w_ffn_up, v_w_ffn_down, v_norm_f_g):
    xi, yi, ci = _position()
    pos_arr = jnp.stack([ci, 2 * xi + yi]).astype(jnp.int32)
    nb, s, _ = x.shape
    t = nb * s

    tr = lambda a: a[0].T
    w_in_t, gwgu, gconv, stage = _gather_w_in(tr(w_in), tr(w_ffn_gate), tr(w_ffn_up), w_ffn_down[0], w_out[0],
                                              w_gate_up[0], conv_w[0])
    wgu_f = gwgu.transpose(1, 0, 2).reshape(RANK, NQK)
    conv_f = gconv.transpose(1, 0, 2).reshape(CONV_K, CW)
    wgu_p = jnp.concatenate([wgu_f, jnp.zeros((A_PAD - RANK, NQK), F32)], axis=0).astype(BF16)

    x2d = x.reshape(t, D)
    tgt2d = loss_target.reshape(t, D)
    tm = 256
    tm_in = min(512, t)
    tk = min(2048, t)
    proj, z, h, gwb = _in_proj_fwd(x2d, norm1_g, w_in_t, wgu_p, b_gate, tm_in, stage)
    proj3 = proj.reshape(nb, s, PW)
    z3 = z.reshape(nb, s, NQK)
    mix3, opre3, sprev, gwa = _mix_fwd(proj3, z3, gla_norm_g, conv_f, stage)
    mix2d = mix3.reshape(t, D)
    dx1, dx1b, dmix, adu, hb, dg2, dgf, loss_part = _ffn_fwd_bwd(
        mix2d, x2d, tgt2d, gwa, gwb, norm2_g, norm_f_g.reshape(1, D), tm)
    dw3, dwb3 = _dw_ffn(adu, hb, tk)
    dw3 = dw3.reshape(3, N_DEV, FF_W, D)
    dw_o, dwb_o, *r1_ffn = _tn_matmul(mix2d, dx1b, D // 4, D, tk, "dw_out", True,
                                      _stage1_rider(dwb3.reshape(3, N_DEV, FF_W, D)))
    *pb, o_own, o_sib = _ffn_core_reduce(dw3, r1_ffn, dw_o.reshape(N_DEV, OUT_ROWS, D),
                                         dwb_o.reshape(N_DEV, OUT_ROWS, D), pos_arr)
    g8 = [dw3, dw3, dw3, o_own]
    leads = [0, 1, 2, None]
    tags = ("w_ffn_down", "w_ffn_gate", "w_ffn_up", "w_out")
    r1 = list(r1_ffn) + [o_sib]
    mb = _mix_bwd(proj3, z3, sprev, opre3, dmix.reshape(nb, s, D), gla_norm_g, conv_f, wgu_p, [pb[0], pb[1], pb[3]])
    dproj3, dgng, dcw, dbg, dwgu = mb[:5]
    dproj2d = dproj3.reshape(t, PW)
    dw_in_t, r2_up = _tn_matmul(dproj2d, h, PW // 5, D, tk, "dw_in", False, _stage2_rider([pb[2]]))
    r2 = [mb[5], mb[6], r2_up, mb[7]]
    g_in, r1_in, pb_in = _w_in_core_reduce(dw_in_t)
    dx, small_sums, r2_in = _in_proj_bwd(dproj2d, x2d, dx1, norm1_g, w_in_t, tm_in, pb_in,
                                         (dg2, dgf, dbg, dgng, dwgu, dcw, loss_part))

    tags = ("w_in",) + tags
    g8 = [g_in] + g8
    leads = [None] + leads
    r1 = [r1_in] + list(r1)
    r2 = [r2_in] + r2
    shard_w = (tr(w_in), w_ffn_down[0], tr(w_ffn_gate), tr(w_ffn_up), w_out[0])
    shard_m = (tr(m_w_in), m_w_ffn_down[0], tr(m_w_ffn_gate), tr(m_w_ffn_up), m_w_out[0])
    shard_v = (tr(v_w_in), v_w_ffn_down[0], tr(v_w_ffn_gate), tr(v_w_ffn_up), v_w_out[0])
    transposed = (True, False, True, True, False)
    items = list(zip(g8, leads, r1, r2, shard_w, shard_m, shard_v))
    flat = list(_finish_weights(items[1:], pos_arr, "finish_ffn_out", 2))
    flat = list(_finish_weights(items[:1], pos_arr, "finish_w_in", 1)) + flat
    results = {}
    for i, (tag, tp) in enumerate(zip(tags, transposed)):
        results[tag] = [o.T[None] if tp else o[None] for o in flat[4 * i:4 * i + 4]]

    small_w = (norm1_g, norm2_g, norm_f_g.reshape(1, D), b_gate, gla_norm_g, w_gate_up[0], conv_w[0])
    small_m = (m_norm1_g, m_norm2_g, m_norm_f_g.reshape(1, D), m_b_gate, m_gla_norm_g, m_w_gate_up[0], m_conv_w[0])
    small_v = (v_norm1_g, v_norm2_g, v_norm_f_g.reshape(1, D), v_b_gate, v_gla_norm_g, v_w_gate_up[0], v_conv_w[0])
    so = _small_adamw(small_sums, small_w, small_m, small_v)
    loss = so[0].reshape(())
    shapes = {"norm_f_g": (D,), "w_gate_up": (1, RANK, WGU_W), "conv_w": (1, CONV_K, CONV_W)}
    for i, name in enumerate(SMALL_NAMES):
        results[name] = [o.reshape(shapes[name]) if name in shapes else o for o in so[1 + 4 * i:5 + 4 * i]]

    names = ("norm1_g", "w_in", "w_gate_up", "b_gate", "gla_norm_g", "conv_w", "w_out", "norm2_g",
             "w_ffn_gate", "w_ffn_up", "w_ffn_down", "norm_f_g")
    outs = [loss, dx.reshape(nb, s, D)]
    for kind in range(4):
        for name in names:
            outs.append(results[name][kind])
    return tuple(outs)
```

```python
import jax
import jax.numpy as jnp
from jax import lax
from jax.experimental import pallas as pl
from jax.experimental.pallas import tpu as pltpu

F32 = jnp.float32
BF16 = jnp.bfloat16
HIGHEST = lax.Precision.HIGHEST
MESH = pl.DeviceIdType.MESH

N_DEV = 8
D = 1024
DFF = 2816
HEADS = 4
DK = 64
DV = 128
NQK = HEADS * DK
NV = HEADS * DV
RANK = 16
CHUNK = 64
CW = 512
CONV_K = 3
IN_COLS = 3088
EPS = 1e-6
INV_GATE_NORM = 1.0 / 16.0
Q_SCALE = DK ** -0.5

PW = 3200
OQ, OK_, OV, OG, OCB, OCC, OCH, OA = 0, 256, 512, 1024, 1536, 2048, 2560, 3072
A_PAD = 128

ADAM_LR = 0.001
ADAM_B1 = 0.9
ADAM_B2 = 0.999
ADAM_EPS = 1e-08
ADAM_WD = 0.01
ADAM_STEP = 10

IN_W = IN_COLS // N_DEV
IN_ROWS = 400
FF_W = DFF // N_DEV
OUT_ROWS = D // N_DEV
SLAB_IN = 0
SLAB_G = SLAB_IN + IN_ROWS
SLAB_U = SLAB_G + FF_W
SLAB_D = SLAB_U + FF_W
SLAB_O = SLAB_D + FF_W
SLAB_ROWS = SLAB_O + OUT_ROWS

VMEM_LIMIT = 56 * 1024 * 1024


def _params(sem=None, vmem=VMEM_LIMIT):
    return pltpu.CompilerParams(dimension_semantics=sem, vmem_limit_bytes=vmem)


def _nt(a, b):
    return lax.dot_general(a, b, (((1,), (1,)), ((), ())), preferred_element_type=F32)


def _tn(a, b, precision=None):
    return lax.dot_general(a, b, (((0,), (0,)), ((), ())), preferred_element_type=F32, precision=precision)


def _mm(a, b, precision=None):
    return jnp.dot(a, b, preferred_element_type=F32, precision=precision)


def _in_segments():
    segs = []
    for j in range(N_DEV):
        lo, hi = IN_W * j, IN_W * (j + 1)
        cuts = sorted({lo, hi} | {c for c in (OCB, OCB + RANK) if lo < c < hi})
        for a, b in zip(cuts[:-1], cuts[1:]):
            if a < OCB:
                d = a
            elif a < OCB + RANK:
                d = OA + (a - OCB)
            else:
                d = a - RANK
            segs.append((j, a - lo, b - lo, d))
    return segs


def _in_proj_fwd(x2d, g1, w_in_t, wgu_p, b_gate, tm, stage):
    t = x2d.shape[0]
    nt = t // tm
    g_rows = SLAB_ROWS - SLAB_D

    def body(x_ref, g_ref, w_ref, wgu_ref, bg_ref, stage_hbm, proj_ref, z_ref, h_ref, gwb_ref,
             send_sems, recv_sems, local_sem):
        gargs = (stage_hbm, SLAB_D, g_rows, gwb_ref, send_sems, recv_sems, local_sem)

        @pl.when(pl.program_id(0) == 0)
        def _():
            _gather_start(*gargs)

        x = x_ref[...]
        r = lax.rsqrt(jnp.mean(x * x, axis=-1, keepdims=True) + EPS)
        h = ((x * r) * g_ref[...]).astype(BF16)
        h_ref[...] = h
        proj = _nt(h, w_ref[...])
        proj_ref[...] = proj
        pa = proj[:, OA:OA + A_PAD].astype(BF16)
        z_ref[...] = _mm(pa, wgu_ref[...]) + bg_ref[...]

        @pl.when(pl.program_id(0) == nt - 1)
        def _():
            _gather_finish(*gargs)

    return pl.pallas_call(
        body,
        name="in_proj_fwd",
        grid=(t // tm,),
        in_specs=[
            pl.BlockSpec((tm, D), lambda i: (i, 0)),
            pl.BlockSpec((1, D), lambda i: (0, 0)),
            pl.BlockSpec((PW, D), lambda i: (0, 0)),
            pl.BlockSpec((A_PAD, NQK), lambda i: (0, 0)),
            pl.BlockSpec((1, NQK), lambda i: (0, 0)),
            pl.BlockSpec(memory_space=pl.ANY),
        ],
        out_specs=[
            pl.BlockSpec((tm, PW), lambda i: (i, 0)),
            pl.BlockSpec((tm, NQK), lambda i: (i, 0)),
            pl.BlockSpec((tm, D), lambda i: (i, 0)),
            pl.BlockSpec(memory_space=pl.ANY),
        ],
        out_shape=[
            jax.ShapeDtypeStruct((t, PW), F32),
            jax.ShapeDtypeStruct((t, NQK), F32),
            jax.ShapeDtypeStruct((t, D), BF16),
            jax.ShapeDtypeStruct((N_DEV, g_rows, D), BF16),
        ],
        scratch_shapes=_gather_sems(),
        compiler_params=_params(("arbitrary",)),
    )(x2d, g1, w_in_t, wgu_p, b_gate, stage)


def _head_masks():
    lane = lax.broadcasted_iota(jnp.int32, (1, NQK), 1)
    return [(lane >= DK * h) & (lane < DK * (h + 1)) for h in range(HEADS)]


def _split_bf16(x, n):
    parts = []
    for _ in range(n):
        p = x.astype(BF16)
        parts.append(p)
        x = x - p.astype(F32)
    return parts


def _chunk_fwd_parts(q, k, z, tril16):
    la = (jnp.minimum(z, 0.0) - jnp.log1p(jnp.exp(-jnp.abs(z)))) * INV_GATE_NORM
    la_parts = _split_bf16(la, 3)
    bc = _mm(tril16, la_parts[0]) + _mm(tril16, la_parts[1]) + _mm(tril16, la_parts[2])
    bl = bc[CHUNK - 1:CHUNK, :]
    eb = jnp.exp(bc)
    enb = jnp.exp(-bc)
    ekl = jnp.exp(bl - bc)
    qi = (q * Q_SCALE) * eb
    ki = k * enb
    ks = k * ekl
    ones16 = jnp.ones((CHUNK, DV), BF16)
    decb = jnp.exp(_tn(la_parts[0], ones16) + _tn(la_parts[1], ones16) + _tn(la_parts[2], ones16))
    return la, eb, enb, ekl, qi, ki, ks, decb


def _stack_heads(a, masks):
    return jnp.concatenate([jnp.where(m, a, 0.0) for m in masks], axis=0)


def _merge_heads(blocks, masks):
    out = blocks[HEADS - 1]
    for h in range(HEADS - 2, -1, -1):
        out = jnp.where(masks[h], blocks[h], out)
    return out


def _causal_stack_mask():
    row = lax.broadcasted_iota(jnp.int32, (HEADS * CHUNK, CHUNK), 0)
    col = lax.broadcasted_iota(jnp.int32, (HEADS * CHUNK, CHUNK), 1)
    return (row & (CHUNK - 1)) >= col


def _conv_taps(u, uprev):
    row = lax.broadcasted_iota(jnp.int32, u.shape, 0)
    u1 = jnp.where(row < 1, pltpu.roll(uprev, 1, 0), pltpu.roll(u, 1, 0))
    u2 = jnp.where(row < 2, pltpu.roll(uprev, 2, 0), pltpu.roll(u, 2, 0))
    return u1, u2


def _mix_fwd(proj3, z3, gng, conv_w, stage):
    nb, s, _ = proj3.shape
    nc = s // CHUNK
    g_rows = SLAB_D - SLAB_G

    def body(p_ref, z_ref, gng_ref, cw_ref, stage_hbm, mix_ref, o_ref, sprev_ref, gwa_ref, s_ref, uprev_ref,
             send_sems, recv_sems, local_sem):
        n = pl.program_id(0)
        gargs = (stage_hbm, SLAB_G, g_rows, gwa_ref, send_sems, recv_sems, local_sem)

        @pl.when(n == 0)
        def _():
            _gather_start(*gargs)
            s_ref[...] = jnp.zeros_like(s_ref)
            uprev_ref[...] = jnp.zeros_like(uprev_ref)

        r_i = lax.broadcasted_iota(jnp.int32, (CHUNK, CHUNK), 0)
        c_i = lax.broadcasted_iota(jnp.int32, (CHUNK, CHUNK), 1)
        tril16 = (r_i >= c_i).astype(BF16)
        masks = _head_masks()
        cmask = _causal_stack_mask()
        gg = gng_ref[...]
        for b in range(nb):
            q = p_ref[b, :, OQ:OQ + NQK]
            k = p_ref[b, :, OK_:OK_ + NQK]
            _, _, _, _, qi, ki, ks, decb = _chunk_fwd_parts(q, k, z_ref[b], tril16)
            qs = _stack_heads(qi, masks).astype(BF16)
            sc = jnp.where(cmask, _nt(qs, ki.astype(BF16)), 0.0).astype(BF16)
            st = s_ref[b]
            sprev_ref[b, 0] = st
            o_inter = _mm(qs, st.astype(BF16))
            v16 = p_ref[b, :, OV:OV + NV].astype(BF16)
            kv = _tn(ks.astype(BF16), v16)
            for h in range(HEADS):
                rows = slice(CHUNK * h, CHUNK * (h + 1))
                cols = slice(DV * h, DV * (h + 1))
                o = _mm(sc[rows], v16[:, cols]) + o_inter[rows]
                o_ref[b, :, cols] = o
                r = lax.rsqrt(jnp.mean(o * o, axis=-1, keepdims=True) + EPS)
                on = (o * r) * gg
                g = p_ref[b, :, OG + DV * h:OG + DV * (h + 1)]
                mix_ref[b, :, cols] = (on * (g * jax.nn.sigmoid(g))).astype(BF16)
                s_ref[b, rows, :] = decb[rows] * st[rows] + kv[rows, cols]
            u = p_ref[b, :, OCC:OCC + CW] * p_ref[b, :, OCH:OCH + CW]
            u1, u2 = _conv_taps(u, uprev_ref[b])
            yc = cw_ref[0:1, :] * u2 + cw_ref[1:2, :] * u1 + cw_ref[2:3, :] * u
            mix_ref[b, :, NV:NV + CW] = (p_ref[b, :, OCB:OCB + CW] * yc).astype(BF16)
            uprev_ref[b] = u

        @pl.when(n == nc - 1)
        def _():
            _gather_finish(*gargs)

    return pl.pallas_call(
        body,
        name="mix_fwd",
        grid=(nc,),
        in_specs=[
            pl.BlockSpec((nb, CHUNK, PW), lambda n: (0, n, 0)),
            pl.BlockSpec((nb, CHUNK, NQK), lambda n: (0, n, 0)),
            pl.BlockSpec((1, DV), lambda n: (0, 0)),
            pl.BlockSpec((CONV_K, CW), lambda n: (0, 0)),
            pl.BlockSpec(memory_space=pl.ANY),
        ],
        out_specs=[
            pl.BlockSpec((nb, CHUNK, D), lambda n: (0, n, 0)),
            pl.BlockSpec((nb, CHUNK, NV), lambda n: (0, n, 0)),
            pl.BlockSpec((nb, 1, NQK, DV), lambda n: (0, n, 0, 0)),
            pl.BlockSpec(memory_space=pl.ANY),
        ],
        out_shape=[
            jax.ShapeDtypeStruct((nb, s, D), BF16),
            jax.ShapeDtypeStruct((nb, s, NV), F32),
            jax.ShapeDtypeStruct((nb, nc, NQK, DV), F32),
            jax.ShapeDtypeStruct((N_DEV, g_rows, D), BF16),
        ],
        scratch_shapes=[pltpu.VMEM((nb, NQK, DV), F32), pltpu.VMEM((nb, CHUNK, CW), F32)] + _gather_sems(),
        compiler_params=_params(("arbitrary",)),
    )(proj3, z3, gng, conv_w, stage)


def _ffn_fwd_bwd(mix2d, x2d, tgt2d, gwa, gwb, g2, gf, tm):
    t = x2d.shape[0]

    def body(mix_ref, x_ref, tgt_ref, g2_ref, gf_ref, gwa_hbm, gwb_hbm,
             dx1_ref, dx1b_ref, dmix_ref, adu_ref, hb_ref, dg2_ref, dgf_ref, loss_ref,
             wo, wg, wu, wd, wsem):
        i = pl.program_id(0)

        def weight_copies(n, dst, src, off, rows):
            return [pltpu.make_async_copy(src.at[j, pl.ds(off, rows), :], dst.at[pl.ds(rows * j, rows), :],
                                          wsem.at[N_DEV * n + j]) for j in range(N_DEV)]

        loads = (weight_copies(0, wo, gwb_hbm, FF_W, OUT_ROWS), weight_copies(1, wg, gwa_hbm, 0, FF_W),
                 weight_copies(2, wu, gwa_hbm, FF_W, FF_W), weight_copies(3, wd, gwb_hbm, 0, FF_W))

        @pl.when(i == 0)
        def _():
            for group in loads:
                for cp in group:
                    cp.start()
            dg2_ref[...] = jnp.zeros_like(dg2_ref)
            dgf_ref[...] = jnp.zeros_like(dgf_ref)
            loss_ref[...] = jnp.zeros_like(loss_ref)
            for group in loads:
                for cp in group:
                    cp.wait()

        g2v = g2_ref[...]
        gfv = gf_ref[...]
        x1 = x_ref[...] + _mm(mix_ref[...], wo[...])
        r2 = lax.rsqrt(jnp.mean(x1 * x1, axis=-1, keepdims=True) + EPS)
        n2 = x1 * r2
        h2 = (n2 * g2v).astype(BF16)
        hb_ref[1] = h2
        gate = _nt(h2, wg[...])
        up = _nt(h2, wu[...])
        sg = jax.nn.sigmoid(gate)
        sil = gate * sg
        act = (sil * up).astype(BF16)
        adu_ref[0] = act
        x2 = x1 + _mm(act, wd[...])
        rf = lax.rsqrt(jnp.mean(x2 * x2, axis=-1, keepdims=True) + EPS)
        nf = x2 * rf
        err = nf * gfv - tgt_ref[...]
        loss_ref[...] += 0.5 * jnp.sum(jnp.mean(err * err, axis=-1, keepdims=True))
        dy = err * (1.0 / D)
        dgf_ref[...] += jnp.sum(dy * nf, axis=0, keepdims=True)
        dnf = dy * gfv
        dx2 = rf * (dnf - nf * jnp.mean(dnf * nf, axis=-1, keepdims=True))
        dx2b = dx2.astype(BF16)
        hb_ref[0] = dx2b
        dact = _nt(dx2b, wd[...])
        dup = (dact * sil).astype(BF16)
        dgate = ((dact * up) * (sg * (1.0 + gate * (1.0 - sg)))).astype(BF16)
        adu_ref[2] = dup
        adu_ref[1] = dgate
        dh2 = _mm(dgate, wg[...]) + _mm(dup, wu[...])
        dg2_ref[...] += jnp.sum(dh2 * n2, axis=0, keepdims=True)
        dn2 = dh2 * g2v
        dx1 = dx2 + r2 * (dn2 - n2 * jnp.mean(dn2 * n2, axis=-1, keepdims=True))
        dx1_ref[...] = dx1
        dx1b = dx1.astype(BF16)
        dx1b_ref[...] = dx1b
        dmix_ref[...] = _nt(dx1b, wo[...])

    tile = lambda w: pl.BlockSpec((tm, w), lambda i: (i, 0))
    vec = pl.BlockSpec((1, D), lambda i: (0, 0))
    hbm = pl.BlockSpec(memory_space=pl.ANY)
    return pl.pallas_call(
        body,
        name="ffn_fwd_bwd",
        grid=(t // tm,),
        in_specs=[tile(D), tile(D), tile(D), vec, vec, hbm, hbm],
        out_specs=[tile(D), tile(D), tile(D), pl.BlockSpec((3, tm, DFF), lambda i: (0, i, 0)),
                   pl.BlockSpec((2, tm, D), lambda i: (0, i, 0)), vec, vec,
                   pl.BlockSpec((1, 128), lambda i: (0, 0))],
        out_shape=[
            jax.ShapeDtypeStruct((t, D), F32),
            jax.ShapeDtypeStruct((t, D), BF16),
            jax.ShapeDtypeStruct((t, D), F32),
            jax.ShapeDtypeStruct((3, t, DFF), BF16),
            jax.ShapeDtypeStruct((2, t, D), BF16),
            jax.ShapeDtypeStruct((1, D), F32),
            jax.ShapeDtypeStruct((1, D), F32),
            jax.ShapeDtypeStruct((1, 128), F32),
        ],
        scratch_shapes=[pltpu.VMEM((D, D), BF16), pltpu.VMEM((DFF, D), BF16), pltpu.VMEM((DFF, D), BF16),
                        pltpu.VMEM((DFF, D), BF16), pltpu.SemaphoreType.DMA((4 * N_DEV,))],
        compiler_params=_params(("arbitrary",)),
    )(mix2d, x2d, tgt2d, g2, gf, gwa, gwb)


def _stage2_rider(pbs):
    return dict(inputs=list(pbs), out_shape=[jax.ShapeDtypeStruct(p.shape, BF16) for p in pbs], nsem=3 * len(pbs),
                copies=_stage2_copies)


def _tn_matmul(a, b, bm, bn, tk, name, with_bf16, rider=None):
    t, m = a.shape
    n = b.shape[1]
    nk = t // tk
    nout = 2 if with_bf16 else 1
    grid = (m // bm, n // bn, nk)
    r_in = [] if rider is None else rider["inputs"]
    r_out = [] if rider is None else rider["out_shape"]

    def body(a_ref, b_ref, *rest):
        ins, outs = rest[:len(r_in)], rest[len(r_in):len(r_in) + nout]
        r_outs, sems = rest[len(r_in) + nout:len(r_in) + nout + len(r_out)], rest[len(r_in) + nout + len(r_out):]
        o_ref = outs[0]
        i, j, k = pl.program_id(0), pl.program_id(1), pl.program_id(2)
        if rider is not None:
            @pl.when((i == 0) & (j == 0) & (k == 0))
            def _():
                for cp in rider["copies"](ins, r_outs, *sems):
                    cp.start()

        @pl.when(k == 0)
        def _():
            o_ref[...] = jnp.zeros_like(o_ref)

        o_ref[...] += _tn(a_ref[...].astype(BF16), b_ref[...].astype(BF16))
        if with_bf16:
            @pl.when(k == nk - 1)
            def _():
                outs[1][...] = o_ref[...].astype(BF16)
        if rider is not None:
            @pl.when((i == grid[0] - 1) & (j == grid[1] - 1) & (k == nk - 1))
            def _():
                copies = rider["copies"](ins, r_outs, *sems)
                for cp in copies:
                    cp.wait_recv()
                for cp in copies:
                    cp.wait_send()

    out_blk = pl.BlockSpec((bm, bn), lambda i, j, k: (i, j))
    hbm = pl.BlockSpec(memory_space=pl.ANY)
    out_shape = [jax.ShapeDtypeStruct((m, n), F32)] + ([jax.ShapeDtypeStruct((m, n), BF16)] if with_bf16 else [])
    res = pl.pallas_call(
        body,
        name=name,
        grid=grid,
        in_specs=[pl.BlockSpec((tk, bm), lambda i, j, k: (k, i)), pl.BlockSpec((tk, bn), lambda i, j, k: (k, j))]
        + [hbm] * len(r_in),
        out_specs=[out_blk] * nout + [hbm] * len(r_out),
        out_shape=out_shape + list(r_out),
        scratch_shapes=([] if rider is None else
                        [pltpu.SemaphoreType.DMA((rider["nsem"],)), pltpu.SemaphoreType.DMA((rider["nsem"],))]),
        compiler_params=_params(("parallel", "parallel", "arbitrary") if rider is None
                                else ("arbitrary", "arbitrary", "arbitrary")),
    )(a, b, *r_in)
    return res[0] if len(res) == 1 else res


def _dw_ffn(adu, hb, tk):
    _, t, _ = adu.shape
    bm = DFF // 2
    nk = t // tk

    def body(a_ref, b_ref, o_ref, ob_ref):
        k = pl.program_id(2)

        @pl.when(k == 0)
        def _():
            o_ref[...] = jnp.zeros_like(o_ref)

        o_ref[...] += _tn(a_ref[...], b_ref[...])

        @pl.when(k == nk - 1)
        def _():
            ob_ref[...] = o_ref[...].astype(BF16)

    out_blk = pl.BlockSpec((None, bm, D), lambda p, i, k: (p, i, 0))
    return pl.pallas_call(
        body,
        name="dw_ffn",
        grid=(3, DFF // bm, nk),
        in_specs=[pl.BlockSpec((None, tk, bm), lambda p, i, k: (p, k, i)),
                  pl.BlockSpec((None, tk, D), lambda p, i, k: (jnp.minimum(p, 1), k, 0))],
        out_specs=[out_blk, out_blk],
        out_shape=[jax.ShapeDtypeStruct((3, DFF, D), F32), jax.ShapeDtypeStruct((3, DFF, D), BF16)],
        compiler_params=_params(("arbitrary", "arbitrary", "arbitrary")),
    )(adu, hb)


def _mix_bwd(proj3, z3, sprev, opre3, dmix3, gng, conv_w, wgu_p, pbs):
    nb, s, _ = proj3.shape
    nc = s // CHUNK
    na = len(pbs)

    def body(*refs):
        (p_ref, pprev_ref, z_ref, sp_ref, o_ref, dm_ref, gng_ref, cw_ref, wgu_ref) = refs[:9]
        pb_refs = refs[9:9 + na]
        (dproj_ref, dgng_ref, dcw_ref, dbg_ref, dwgu_ref) = refs[9 + na:14 + na]
        r2_refs = refs[14 + na:14 + 2 * na]
        ds_ref, dycn_ref, send_sems, recv_sems = refs[14 + 2 * na:]
        step = pl.program_id(0)
        n = nc - 1 - step

        @pl.when(step == 0)
        def _():
            for cp in _stage2_copies(pb_refs, r2_refs, send_sems, recv_sems):
                cp.start()
            ds_ref[...] = jnp.zeros_like(ds_ref)
            dycn_ref[...] = jnp.zeros_like(dycn_ref)
            dgng_ref[...] = jnp.zeros_like(dgng_ref)
            dcw_ref[...] = jnp.zeros_like(dcw_ref)
            dbg_ref[...] = jnp.zeros_like(dbg_ref)
            dwgu_ref[...] = jnp.zeros_like(dwgu_ref)

        r_i = lax.broadcasted_iota(jnp.int32, (CHUNK, CHUNK), 0)
        c_i = lax.broadcasted_iota(jnp.int32, (CHUNK, CHUNK), 1)
        tril16 = (r_i >= c_i).astype(BF16)
        triu16 = (r_i <= c_i).astype(BF16)
        causal = r_i >= c_i
        masks = _head_masks()
        cmask = _causal_stack_mask()
        gg = gng_ref[...]
        last_row = lax.broadcasted_iota(jnp.int32, (CHUNK, NQK), 0) == CHUNK - 1
        ones_r = jnp.ones((16, DV), BF16)
        has_prev = (n > 0).astype(F32)
        for b in range(nb):
            q = p_ref[b, :, OQ:OQ + NQK]
            k = p_ref[b, :, OK_:OK_ + NQK]
            z = z_ref[b]
            _, eb, enb, ekl, qi, ki, ks, decb = _chunk_fwd_parts(q, k, z, tril16)
            qi16 = qi.astype(BF16)
            ki16 = ki.astype(BF16)
            qs = _stack_heads(qi, masks).astype(BF16)
            sc = jnp.where(cmask, _nt(qs, ki16), 0.0).astype(BF16)
            st = sp_ref[b, 0]
            st16 = st.astype(BF16)
            dsn = ds_ref[b]
            dsn16 = dsn.astype(BF16)
            v16 = p_ref[b, :, OV:OV + NV].astype(BF16)
            do16 = []
            dgng = jnp.zeros((1, DV), F32)
            for h in range(HEADS):
                cols = slice(DV * h, DV * (h + 1))
                o = o_ref[b, :, cols]
                r = lax.rsqrt(jnp.mean(o * o, axis=-1, keepdims=True) + EPS)
                nh = o * r
                g = p_ref[b, :, OG + DV * h:OG + DV * (h + 1)]
                sg = jax.nn.sigmoid(g)
                dog = dm_ref[b, :, cols]
                dproj_ref[b, :, OG + DV * h:OG + DV * (h + 1)] = (
                    (dog * (nh * gg)) * (sg * (1.0 + g * (1.0 - sg)))).astype(BF16)
                don = dog * (g * sg)
                dgng = dgng + jnp.sum(don * nh, axis=0, keepdims=True)
                dn = don * gg
                do = r * (dn - nh * jnp.mean(dn * nh, axis=-1, keepdims=True))
                do16.append(do.astype(BF16))
            dgng_ref[...] += dgng
            do_rows = jnp.concatenate(do16, axis=0)
            v_rows = jnp.concatenate([v16[:, DV * h:DV * (h + 1)] for h in range(HEADS)], axis=0)
            dp16 = [jnp.where(causal, _nt(do16[h], v16[:, DV * h:DV * (h + 1)]), 0.0).astype(BF16)
                    for h in range(HEADS)]
            ks_dsn = _mm(_stack_heads(ks, masks).astype(BF16), dsn16)
            do_st = _nt(do_rows, st16)
            v_dsn = _nt(v_rows, dsn16)
            dp_ki = _mm(jnp.concatenate(dp16, axis=0), ki16)
            q_do = _tn(qi16, jnp.concatenate(do16, axis=1))
            dki_h = []
            for h in range(HEADS):
                rows = slice(CHUNK * h, CHUNK * (h + 1))
                cols = slice(DV * h, DV * (h + 1))
                dv = _tn(sc[rows], do16[h]) + ks_dsn[rows]
                dproj_ref[b, :, OV + DV * h:OV + DV * (h + 1)] = dv.astype(BF16)
                dki_h.append(_tn(dp16[h], qi16))
                ds_ref[b, rows, :] = decb[rows] * dsn[rows] + q_do[rows, cols]
            blocks = lambda a: [a[CHUNK * h:CHUNK * (h + 1)] for h in range(HEADS)]
            dqi = _merge_heads(blocks(dp_ki + do_st), masks)
            dki = _merge_heads(dki_h, masks)
            dks = _merge_heads(blocks(v_dsn), masks)
            dproj_ref[b, :, OQ:OQ + NQK] = (dqi * (Q_SCALE * eb)).astype(BF16)
            dproj_ref[b, :, OK_:OK_ + NQK] = (dki * enb + dks * ekl).astype(BF16)
            dks_ks = dks * ks
            db = dqi * qi - dki * ki - dks_ks
            sd = _split_bf16(dsn * st * decb, 2)
            dbl = jnp.sum(dks_ks, axis=0, keepdims=True) + (_nt(ones_r, sd[0]) + _nt(ones_r, sd[1]))[0:1, :]
            db = db + jnp.where(last_row, dbl, 0.0)
            db_parts = _split_bf16(db, 3)
            dla = _mm(triu16, db_parts[0]) + _mm(triu16, db_parts[1]) + _mm(triu16, db_parts[2])
            dz = (dla * INV_GATE_NORM) * (1.0 / (1.0 + jnp.exp(z)))
            dbg_ref[...] += jnp.sum(dz, axis=0, keepdims=True)
            dz16 = dz.astype(BF16)
            pa16 = p_ref[b, :, OA:OA + A_PAD].astype(BF16)
            dwgu_ref[...] += _tn(pa16, dz16)
            dproj_ref[b, :, OA:OA + A_PAD] = _nt(dz16, wgu_ref[...]).astype(BF16)
            cb = p_ref[b, :, OCB:OCB + CW]
            cc = p_ref[b, :, OCC:OCC + CW]
            ch = p_ref[b, :, OCH:OCH + CW]
            u = cc * ch
            uprev = (pprev_ref[b, :, 0:CW] * pprev_ref[b, :, CW:2 * CW]) * has_prev
            u1, u2 = _conv_taps(u, uprev)
            w0 = cw_ref[0:1, :]
            w1 = cw_ref[1:2, :]
            w2 = cw_ref[2:3, :]
            yc = w0 * u2 + w1 * u1 + w2 * u
            doc = dm_ref[b, :, NV:NV + CW]
            dproj_ref[b, :, OCB:OCB + CW] = (doc * yc).astype(BF16)
            dyc = doc * cb
            dycn = dycn_ref[b]
            row = lax.broadcasted_iota(jnp.int32, dyc.shape, 0)
            d1 = jnp.where(row >= CHUNK - 1, pltpu.roll(dycn, CHUNK - 1, 0), pltpu.roll(dyc, CHUNK - 1, 0))
            d2 = jnp.where(row >= CHUNK - 2, pltpu.roll(dycn, CHUNK - 2, 0), pltpu.roll(dyc, CHUNK - 2, 0))
            du = w2 * dyc + w1 * d1 + w0 * d2
            dproj_ref[b, :, OCC:OCC + CW] = (du * ch).astype(BF16)
            dproj_ref[b, :, OCH:OCH + CW] = (du * cc).astype(BF16)
            dcw_ref[0:1, :] += jnp.sum(dyc * u2, axis=0, keepdims=True)
            dcw_ref[1:2, :] += jnp.sum(dyc * u1, axis=0, keepdims=True)
            dcw_ref[2:3, :] += jnp.sum(dyc * u, axis=0, keepdims=True)
            dycn_ref[b] = dyc

        @pl.when(step == nc - 1)
        def _():
            copies = _stage2_copies(pb_refs, r2_refs, send_sems, recv_sems)
            for cp in copies:
                cp.wait_recv()
            for cp in copies:
                cp.wait_send()

    rev = lambda w: pl.BlockSpec((nb, CHUNK, w), lambda i: (0, nc - 1 - i, 0))
    const = lambda r, c: pl.BlockSpec((r, c), lambda i: (0, 0))
    hbm = pl.BlockSpec(memory_space=pl.ANY)
    return pl.pallas_call(
        body,
        name="mix_bwd",
        grid=(nc,),
        in_specs=[
            rev(PW),
            pl.BlockSpec((nb, CHUNK, 2 * CW), lambda i: (0, jnp.maximum(nc - 2 - i, 0), OCC // (2 * CW))),
            rev(NQK),
            pl.BlockSpec((nb, 1, NQK, DV), lambda i: (0, nc - 1 - i, 0, 0)),
            rev(NV),
            rev(D),
            const(1, DV),
            const(CONV_K, CW),
            const(A_PAD, NQK),
        ] + [hbm] * na,
        out_specs=[rev(PW), const(1, DV), const(8, CW), const(1, NQK), const(A_PAD, NQK)] + [hbm] * na,
        out_shape=[
            jax.ShapeDtypeStruct((nb, s, PW), BF16),
            jax.ShapeDtypeStruct((1, DV), F32),
            jax.ShapeDtypeStruct((8, CW), F32),
            jax.ShapeDtypeStruct((1, NQK), F32),
            jax.ShapeDtypeStruct((A_PAD, NQK), F32),
        ] + [jax.ShapeDtypeStruct((3,) + p.shape[1:], BF16) for p in pbs],
        scratch_shapes=[pltpu.VMEM((nb, NQK, DV), F32), pltpu.VMEM((nb, CHUNK, CW), F32),
                        pltpu.SemaphoreType.DMA((3 * na,)), pltpu.SemaphoreType.DMA((3 * na,))],
        compiler_params=_params(("arbitrary",)),
    )(proj3, proj3, z3, sprev, opre3, dmix3, gng, conv_w, wgu_p, *pbs)


SMALL_PACK_ROWS = 16


def _wgu_slot(r):
    return 4 + r // 4, NQK * (r % 4)


CONV_SLOTS = ((8, 0), (8, CW), (9, 0))


def _in_proj_bwd(dproj2d, x2d, dx1, g1, w_in_t, tm, pb, small_parts):
    t = x2d.shape[0]
    nt = t // tm

    def body(dp_ref, x_ref, dx1_ref, g_ref, w_ref, pb_ref, dg2, dgf, dbg, dgng, dwgu, dcw, lp,
             dx_ref, sums_ref, r2_ref, dg1_acc, pack, gbuf, pack1, gbuf1, send_sems, recv_sems,
             ssend, srecv, ssend1, srecv1):
        x, y, c = _position()
        me = 4 * x + 2 * y + c
        flips = [(k >> 2, (k >> 1) & 1, k & 1) for k in range(1, N_DEV)]
        peers = [(x ^ fx, y ^ fy, c ^ fc) for fx, fy, fc in flips]

        def small_copies(src, dst, send, recv, arrivals):
            return [pltpu.make_async_remote_copy(
                src_ref=src, dst_ref=dst.at[4 * px + 2 * py + pc if arrivals else me],
                send_sem=send.at[k], recv_sem=recv.at[k], device_id=(px, py, pc), device_id_type=MESH)
                for k, (px, py, pc) in enumerate(peers)]

        @pl.when(pl.program_id(0) == 0)
        def _():
            for cp in _stage2_copies([pb_ref], [r2_ref], send_sems, recv_sems):
                cp.start()
            dg1_acc[...] = jnp.zeros_like(dg1_acc)
            pack[...] = jnp.zeros_like(pack)
            pack[1:2, :] = dg2[...]
            pack[2:3, :] = dgf[...]
            pack[3:4, 0:NQK] = dbg[...]
            pack[3:4, NQK:NQK + DV] = dgng[...]
            pack[3:4, NQK + DV:NQK + 2 * DV] = lp[...]
            for r in range(RANK):
                row, lane = _wgu_slot(r)
                pack[row:row + 1, lane:lane + NQK] = dwgu[r:r + 1, :]
            for r, (row, lane) in enumerate(CONV_SLOTS):
                pack[row:row + 1, lane:lane + CW] = dcw[r:r + 1, :]
            for cp in small_copies(pack, gbuf, ssend, srecv, False):
                cp.start()
            gbuf[me] = pack[...]

        xv = x_ref[...]
        r = lax.rsqrt(jnp.mean(xv * xv, axis=-1, keepdims=True) + EPS)
        n1 = xv * r
        dh = _mm(dp_ref[...], w_ref[...])
        dg1_acc[...] += jnp.sum(dh * n1, axis=0, keepdims=True)
        dn = dh * g_ref[...]
        dx_ref[...] = dx1_ref[...] + r * (dn - n1 * jnp.mean(dn * n1, axis=-1, keepdims=True))

        @pl.when(pl.program_id(0) == nt - 1)
        def _():
            pack1[...] = jnp.zeros_like(pack1)
            pack1[0:1, :] = dg1_acc[...]
            for cp in small_copies(pack1, gbuf1, ssend1, srecv1, False):
                cp.start()
            gbuf1[me] = pack1[...]
            copies = _stage2_copies([pb_ref], [r2_ref], send_sems, recv_sems)
            for cp in copies:
                cp.wait_recv()
            for cp in copies:
                cp.wait_send()
            for src, dst, send, recv in ((pack, gbuf, ssend, srecv), (pack1, gbuf1, ssend1, srecv1)):
                for cp in small_copies(src, dst, send, recv, True):
                    cp.wait_recv()
                    cp.wait_send()
            acc = gbuf[0]
            acc1 = gbuf1[0]
            for d in range(1, N_DEV):
                acc = acc + gbuf[d]
                acc1 = acc1 + gbuf1[d]
            sums_ref[...] = acc
            sums_ref[0:1, :] = acc1[0:1, :]

    tile = lambda w: pl.BlockSpec((tm, w), lambda i: (i, 0))
    vec = pl.BlockSpec((1, D), lambda i: (0, 0))
    hbm = pl.BlockSpec(memory_space=pl.ANY)
    whole = lambda a: pl.BlockSpec(a.shape, lambda i: (0,) * a.ndim)
    return pl.pallas_call(
        body,
        name="in_proj_bwd",
        grid=(nt,),
        in_specs=[tile(PW), tile(D), tile(D), vec, pl.BlockSpec((PW, D), lambda i: (0, 0)), hbm]
        + [whole(a) for a in small_parts],
        out_specs=[tile(D), pl.BlockSpec((SMALL_PACK_ROWS, D), lambda i: (0, 0)), hbm],
        out_shape=[jax.ShapeDtypeStruct((t, D), F32), jax.ShapeDtypeStruct((SMALL_PACK_ROWS, D), F32),
                   jax.ShapeDtypeStruct((3,) + pb.shape[1:], BF16)],
        scratch_shapes=[pltpu.VMEM((1, D), F32),
                        pltpu.VMEM((SMALL_PACK_ROWS, D), F32), pltpu.VMEM((N_DEV, SMALL_PACK_ROWS, D), F32),
                        pltpu.VMEM((8, D), F32), pltpu.VMEM((N_DEV, 8, D), F32),
                        pltpu.SemaphoreType.DMA((3,)), pltpu.SemaphoreType.DMA((3,)),
                        pltpu.SemaphoreType.DMA((7,)), pltpu.SemaphoreType.DMA((7,)),
                        pltpu.SemaphoreType.DMA((7,)), pltpu.SemaphoreType.DMA((7,))],
        compiler_params=_params(("arbitrary",)),
    )(dproj2d, x2d, dx1, g1, w_in_t, pb, *small_parts)


def _adamw_math(w, g, m, v):
    m = ADAM_B1 * m + (1.0 - ADAM_B1) * g
    v = ADAM_B2 * v + (1.0 - ADAM_B2) * (g * g)
    m_hat = m / (1.0 - ADAM_B1 ** ADAM_STEP)
    v_hat = v / (1.0 - ADAM_B2 ** ADAM_STEP)
    delta = -ADAM_LR * (m_hat / (jnp.sqrt(v_hat) + ADAM_EPS) + ADAM_WD * w)
    return delta, m, v


def _position():
    return lax.axis_index("x"), lax.axis_index("y"), lax.axis_index("c")


GATHER_PARTS = 2
GATHER_SEMS = 7 * GATHER_PARTS


def _gather_copies(stage, lo, rows, gx, send_sems, recv_sems, local_sem):
    x, y, c = _position()
    me = (x, y, c)
    sibling = (x, y, 1 - c)
    chips = [(1 - x, y), (x, 1 - y), (1 - x, 1 - y)]
    part = -(-rows // (16 * GATHER_PARTS)) * 16
    bounds = [(p * part, min(part, rows - p * part)) for p in range(GATHER_PARTS)]

    def blk(px, py, pc, off, n):
        return gx.at[4 * px + 2 * py + pc, pl.ds(off, n), :]

    mine = pltpu.make_async_copy(stage.at[pl.ds(lo, rows), :], gx.at[4 * x + 2 * y + c], local_sem)
    parts = []
    for p, (off, n) in enumerate(bounds):
        def copy(k, block, to, from_stage=False, p=p, off=off, n=n):
            return pltpu.make_async_remote_copy(
                src_ref=stage.at[pl.ds(lo + off, n), :] if from_stage else blk(*block, off, n),
                dst_ref=blk(*block, off, n), send_sem=send_sems.at[7 * p + k], recv_sem=recv_sems.at[7 * p + k],
                device_id=to, device_id_type=MESH)

        first = [copy(0, me, sibling, True)] + [copy(1 + j, me, (*chip, c), True) for j, chip in enumerate(chips)]
        passed = [copy(4 + j, (*chip, c), sibling) for j, chip in enumerate(chips)]
        arrivals = ([copy(0, sibling, me)] + [copy(1 + j, (*chip, c), me) for j, chip in enumerate(chips)]
                    + [copy(4 + j, (*chip, 1 - c), me) for j, chip in enumerate(chips)])
        parts.append((first, passed, arrivals))
    return mine, parts


def _gather_start(*args):
    mine, parts = _gather_copies(*args)
    mine.start()
    for first, _, _ in parts:
        for cp in first:
            cp.start()


def _gather_finish(*args):
    mine, parts = _gather_copies(*args)
    for _, passed, arrivals in parts:
        for j in range(3):
            arrivals[1 + j].wait_recv()
            passed[j].start()
    for first, passed, arrivals in parts:
        arrivals[0].wait_recv()
        for j in range(3):
            arrivals[4 + j].wait_recv()
        for cp in first + passed:
            cp.wait_send()
    mine.wait()


def _gather_sems():
    return [pltpu.SemaphoreType.DMA((GATHER_SEMS,)), pltpu.SemaphoreType.DMA((GATHER_SEMS,)), pltpu.SemaphoreType.DMA]


def _gather_w_in(w_it, w_gt, w_ut, w_d, w_o, wgu_s, conv_s):
    def body(wi_ref, wg_ref, wu_ref, wd_ref, wo_ref, wgu_ref, conv_ref, w_ref, gwgu_ref, gconv_ref, stage,
             buf, send_sems, recv_sems, local_sem, ssend, srecv):
        x, y, c = _position()
        me = 4 * x + 2 * y + c
        stage[SLAB_IN:SLAB_IN + IN_W, :] = wi_ref[...].astype(BF16)
        stage[SLAB_IN + IN_W:SLAB_G, :] = jnp.zeros((IN_ROWS - IN_W, D), BF16)
        args = (stage, SLAB_IN, IN_ROWS, buf, send_sems, recv_sems, local_sem)
        _gather_start(*args)
        stage[SLAB_G:SLAB_U, :] = wg_ref[...].astype(BF16)
        stage[SLAB_U:SLAB_D, :] = wu_ref[...].astype(BF16)
        stage[SLAB_D:SLAB_O, :] = wd_ref[...].astype(BF16)
        stage[SLAB_O:SLAB_ROWS, :] = wo_ref[...].astype(BF16)
        flips = [(k >> 2, (k >> 1) & 1, k & 1) for k in range(1, N_DEV)]
        peers = [(x ^ fx, y ^ fy, c ^ fc) for fx, fy, fc in flips]

        def small(k, block_id, to):
            return [pltpu.make_async_remote_copy(
                src_ref=s, dst_ref=g.at[block_id], send_sem=ssend.at[2 * k + n], recv_sem=srecv.at[2 * k + n],
                device_id=to, device_id_type=MESH)
                for n, (s, g) in enumerate(((wgu_ref, gwgu_ref), (conv_ref, gconv_ref)))]

        gwgu_ref[me] = wgu_ref[...]
        gconv_ref[me] = conv_ref[...]
        for k, peer in enumerate(peers):
            for cp in small(k, me, peer):
                cp.start()
        w_ref[IN_COLS:PW, :] = jnp.zeros((PW - IN_COLS, D), BF16)
        _gather_finish(*args)
        for k, (px, py, pc) in enumerate(peers):
            for cp in small(k, 4 * px + 2 * py + pc, (px, py, pc)):
                cp.wait_recv()
                cp.wait_send()
        for j, lo, hi, d in _in_segments():
            w_ref[d:d + hi - lo, :] = buf[j, lo:hi, :]

    vm = pl.BlockSpec(memory_space=pltpu.VMEM)
    return pl.pallas_call(
        body,
        name="gather_w_in",
        in_specs=[vm] * 7,
        out_specs=[vm] * 4,
        out_shape=[jax.ShapeDtypeStruct((PW, D), BF16),
                   jax.ShapeDtypeStruct((N_DEV,) + wgu_s.shape, F32),
                   jax.ShapeDtypeStruct((N_DEV,) + conv_s.shape, F32),
                   jax.ShapeDtypeStruct((SLAB_ROWS, D), BF16)],
        scratch_shapes=[pltpu.VMEM((N_DEV, IN_ROWS, D), BF16)] + _gather_sems()
        + [pltpu.SemaphoreType.DMA((14,)), pltpu.SemaphoreType.DMA((14,))],
        compiler_params=_params(),
    )(w_it, w_gt, w_ut, w_d, w_o, wgu_s, conv_s)


def _w_in_core_reduce(dw_t):
    def body(d_ref, own_ref, sib_ref, pb_ref, g, gb, r1, send_sems, recv_sems):
        x, y, c = _position()
        chip = 2 * x + y
        for j in range(N_DEV):
            g[j, IN_W:IN_ROWS, :] = jnp.zeros((IN_ROWS - IN_W, D), F32)
        for j, lo, hi, d in _in_segments():
            g[j, lo:hi, :] = d_ref[d:d + hi - lo, :]
        for j in range(N_DEV):
            gb[j] = g[j].astype(BF16)
        copies = _stage1_copies(gb, r1, send_sems, recv_sems)
        for cp in copies:
            cp.start()
        own_ref[0] = g[2 * chip + c]
        for cp in copies:
            cp.wait_recv()
        sib_ref[0] = r1[chip]
        for k in range(1, 4):
            t = chip ^ k
            pb_ref[k - 1] = (g[2 * t + c] + r1[t].astype(F32)).astype(BF16)
        for cp in copies:
            cp.wait_send()

    vm = pl.BlockSpec(memory_space=pltpu.VMEM)
    return pl.pallas_call(
        body,
        name="w_in_core_reduce",
        in_specs=[vm],
        out_specs=[vm, vm, vm],
        out_shape=[jax.ShapeDtypeStruct((1, IN_ROWS, D), F32), jax.ShapeDtypeStruct((1, IN_ROWS, D), BF16),
                   jax.ShapeDtypeStruct((3, IN_ROWS, D), BF16)],
        scratch_shapes=[pltpu.VMEM((N_DEV, IN_ROWS, D), F32), pltpu.VMEM((N_DEV, IN_ROWS, D), BF16),
                        pltpu.VMEM((4, IN_ROWS, D), BF16), pltpu.SemaphoreType.DMA((4,)),
                        pltpu.SemaphoreType.DMA((4,))],
        compiler_params=_params(),
    )(dw_t)


def _stage1_copies(g_ref, r_ref, send_sems, recv_sems):
    x, y, c = _position()
    return [pltpu.make_async_remote_copy(
        src_ref=g_ref.at[2 * i + 1 - c], dst_ref=r_ref.at[i], send_sem=send_sems.at[i], recv_sem=recv_sems.at[i],
        device_id=(x, y, 1 - c), device_id_type=MESH) for i in range(4)]


def _ffn_core_reduce(dw3, dwb3, dw_o, dwb_o, pos_arr):
    def body(pos_ref, g0, g1, g2, go, gb3_hbm, gbo_hbm, p0, p1, p2, po, s0, s1, s2, so,
             r1f, r1o, send_sems, recv_sems):
        k = pl.program_id(0)
        x, y, c = _position()
        chip = 2 * x + y

        def copies(p):
            src = 2 * (chip ^ ((p + 1) & 3)) + 1 - c
            pairs = [(gb3_hbm.at[a, src], r1f.at[a, p]) for a in range(3)] + [(gbo_hbm.at[src], r1o.at[p])]
            return [pltpu.make_async_remote_copy(
                src_ref=s, dst_ref=d, send_sem=send_sems.at[4 * p + a], recv_sem=recv_sems.at[4 * p + a],
                device_id=(x, y, 1 - c), device_id_type=MESH) for a, (s, d) in enumerate(pairs)]

        @pl.when(k == 0)
        def _():
            for p in range(4):
                for cp in copies(p):
                    cp.start()

        for p in range(3):
            @pl.when(k == p)
            def _():
                for cp in copies(p):
                    cp.wait_recv()

        for a, (g, pb) in enumerate(((g0, p0), (g1, p1), (g2, p2))):
            pb[...] = (g[...] + r1f[a, k][None].astype(F32)).astype(BF16)
        po[...] = (go[...] + r1o[k][None].astype(F32)).astype(BF16)

        @pl.when(k == 2)
        def _():
            for cp in copies(3):
                cp.wait_recv()
            for a, s in enumerate((s0, s1, s2)):
                s[0] = r1f[a, 3]
            so[0] = r1o[3]
            for p in range(4):
                for cp in copies(p):
                    cp.wait_send()

    other = lambda k, pos: 2 * (pos[1] ^ (k + 1)) + pos[0]
    g_spec = lambda lead: pl.BlockSpec((None, 1, FF_W, D), lambda k, pos: (lead, other(k, pos), 0, 0))
    slot = lambda rows: pl.BlockSpec((1, rows, D), lambda k, pos: (k, 0, 0))
    one = lambda rows: pl.BlockSpec((1, rows, D), lambda k, pos: (0, 0, 0))
    hbm = pl.BlockSpec(memory_space=pl.ANY)
    return pl.pallas_call(
        body,
        name="ffn_core_reduce",
        grid_spec=pltpu.PrefetchScalarGridSpec(
            num_scalar_prefetch=1, grid=(3,),
            in_specs=[g_spec(0), g_spec(1), g_spec(2),
                      pl.BlockSpec((1, OUT_ROWS, D), lambda k, pos: (other(k, pos), 0, 0)), hbm, hbm],
            out_specs=[slot(FF_W), slot(FF_W), slot(FF_W), slot(OUT_ROWS),
                       one(FF_W), one(FF_W), one(FF_W), one(OUT_ROWS)],
            scratch_shapes=[pltpu.VMEM((3, 4, FF_W, D), BF16), pltpu.VMEM((4, OUT_ROWS, D), BF16),
                            pltpu.SemaphoreType.DMA((16,)), pltpu.SemaphoreType.DMA((16,))]),
        out_shape=[jax.ShapeDtypeStruct((3, FF_W, D), BF16)] * 3 + [jax.ShapeDtypeStruct((3, OUT_ROWS, D), BF16)]
        + [jax.ShapeDtypeStruct((1, FF_W, D), BF16)] * 3 + [jax.ShapeDtypeStruct((1, OUT_ROWS, D), BF16)],
        compiler_params=_params(("arbitrary",)),
    )(pos_arr, dw3, dw3, dw3, dw_o, dwb3, dwb_o)


def _stage2_copies(p_refs, r_refs, send_sems, recv_sems):
    x, y, c = _position()
    copies = []
    for a in range(len(p_refs)):
        for k in range(1, 4):
            copies.append(pltpu.make_async_remote_copy(
                src_ref=p_refs[a].at[k - 1], dst_ref=r_refs[a].at[k - 1],
                send_sem=send_sems.at[3 * a + k - 1], recv_sem=recv_sems.at[3 * a + k - 1],
                device_id=(x ^ (k >> 1), y ^ (k & 1), c), device_id_type=MESH))
    return copies


def _finish_weights(items, pos_arr, name, nblk):
    n = len(items)
    in_specs, out_specs, out_shape, operands, wbs = [], [], [], [], []
    for g8, lead, r1, r2, w, m, v in items:
        rows, wr = g8.shape[-2], w.shape[0]
        assert rows % nblk == 0 and wr % nblk == 0 and (nblk == 1 or (rows == wr and rows % (16 * nblk) == 0))
        rb, wb = rows // nblk, wr // nblk
        if lead is not None:
            g_spec = pl.BlockSpec((None, 1, rb, D), lambda i, pos, lead=lead: (lead, 2 * pos[1] + pos[0], i, 0))
        elif g8.shape[0] == 1:
            g_spec = pl.BlockSpec((1, rb, D), lambda i, pos: (0, i, 0))
        else:
            g_spec = pl.BlockSpec((1, rb, D), lambda i, pos: (2 * pos[1] + pos[0], i, 0))
        r1_spec = pl.BlockSpec((1, rb, D), lambda i, pos: (0, i, 0))
        wblk = pl.BlockSpec((wb, D), lambda i, pos: (i, 0))
        in_specs += [g_spec, r1_spec, pl.BlockSpec((3, rb, D), lambda i, pos: (0, i, 0)), wblk, wblk, wblk]
        out_specs += [wblk] * 4
        out_shape += [jax.ShapeDtypeStruct(w.shape, F32)] * 4
        operands += [g8, r1, r2, w, m, v]
        wbs.append(wb)

    def body(pos_ref, *refs):
        for a in range(n):
            g_ref, r1_ref, r2_ref, w_ref, m_ref, v_ref = refs[6 * a:6 * a + 6]
            g_out, d_out, m_out, v_out = refs[6 * n + 4 * a:6 * n + 4 * a + 4]
            g = g_ref[0] + r1_ref[0].astype(F32)
            for k in range(3):
                g = g + r2_ref[k].astype(F32)
            g = g[0:wbs[a], :]
            g_out[...] = g
            d, mn, vn = _adamw_math(w_ref[...], g, m_ref[...], v_ref[...])
            d_out[...] = d
            m_out[...] = mn
            v_out[...] = vn

    return pl.pallas_call(
        body,
        name=name,
        grid_spec=pltpu.PrefetchScalarGridSpec(
            num_scalar_prefetch=1, grid=(nblk,), in_specs=in_specs, out_specs=out_specs),
        out_shape=out_shape,
        compiler_params=_params(("arbitrary",)),
    )(pos_arr, *operands)


SMALL_NAMES = ("norm1_g", "norm2_g", "norm_f_g", "b_gate", "gla_norm_g", "w_gate_up", "conv_w")
WGU_W = NQK // N_DEV
CONV_W = CW // N_DEV


def _small_adamw(sums, ws, ms, vs):
    n = len(SMALL_NAMES)

    def body(*refs):
        acc_ref = refs[0]
        w_refs, m_refs, v_refs = refs[1:1 + n], refs[1 + n:1 + 2 * n], refs[1 + 2 * n:1 + 3 * n]
        loss_ref = refs[1 + 3 * n]
        outs = refs[2 + 3 * n:]
        x, y, c = _position()
        me = 4 * x + 2 * y + c
        acc = acc_ref[...]
        loss_ref[...] = acc[3:4, NQK + DV:NQK + DV + 1]

        def my_columns(full, width):
            r = lax.broadcasted_iota(jnp.int32, (full.shape[1], width), 0)
            col = lax.broadcasted_iota(jnp.int32, (full.shape[1], width), 1)
            sel = (r == width * me + col).astype(F32)
            return _mm(full, sel, precision=HIGHEST)

        dwgu = jnp.concatenate([acc[row:row + 1, lane:lane + NQK] for row, lane in map(_wgu_slot, range(RANK))], axis=0)
        dcw = jnp.concatenate([acc[row:row + 1, lane:lane + CW] for row, lane in CONV_SLOTS], axis=0)
        grads = [acc[0:1, :], acc[1:2, :], acc[2:3, :], acc[3:4, 0:NQK], acc[3:4, NQK:NQK + DV],
                 my_columns(dwgu, WGU_W), my_columns(dcw, CONV_W)]
        for i, g in enumerate(grads):
            d, mn, vn = _adamw_math(w_refs[i][...], g, m_refs[i][...], v_refs[i][...])
            outs[4 * i][...] = g
            outs[4 * i + 1][...] = d
            outs[4 * i + 2][...] = mn
            outs[4 * i + 3][...] = vn

    vm = pl.BlockSpec(memory_space=pltpu.VMEM)
    out_shape = [jax.ShapeDtypeStruct((1, 1), F32)]
    for w in ws:
        out_shape += [jax.ShapeDtypeStruct(w.shape, F32)] * 4
    return pl.pallas_call(
        body,
        name="small_adamw",
        in_specs=[vm] * (1 + 3 * n),
        out_specs=[vm] * (1 + 4 * n),
        out_shape=out_shape,
        compiler_params=_params(),
    )(sums, *ws, *ms, *vs)


def kernel(x, norm1_g, w_in, w_gate_up, b_gate, gla_norm_g, conv_w, w_out, norm2_g, w_ffn_gate, w_ffn_up, w_ffn_down, norm_f_g, loss_target, m_norm1_g, m_w_in, m_w_gate_up, m_b_gate, m_gla_norm_g, m_conv_w, m_w_out, m_norm2_g, m_w_ffn_gate, m_w_ffn_up, m_w_ffn_down, m_norm_f_g, v_norm1_g, v_w_in, v_w_gate_up, v_b_gate, v_gla_norm_g, v_conv_w, v_w_out, v_norm2_g, v_w_ffn_gate, v_w_ffn_up, v_w_ffn_down, v_norm_f_g):
    xi, yi, ci = _position()
    pos_arr = jnp.stack([ci, 2 * xi + yi]).astype(jnp.int32)
    nb, s, _ = x.shape
    t = nb * s

    tr = lambda a: a[0].T
    w_in_t, gwgu, gconv, stage = _gather_w_in(tr(w_in), tr(w_ffn_gate), tr(w_ffn_up), w_ffn_down[0], w_out[0],
                                              w_gate_up[0], conv_w[0])
    wgu_f = gwgu.transpose(1, 0, 2).reshape(RANK, NQK)
    conv_f = gconv.transpose(1, 0, 2).reshape(CONV_K, CW)
    wgu_p = jnp.concatenate([wgu_f, jnp.zeros((A_PAD - RANK, NQK), F32)], axis=0).astype(BF16)

    x2d = x.reshape(t, D)
    tgt2d = loss_target.reshape(t, D)
    tm = 256
    tm_in = min(512, t)
    tk = min(2048, t)
    proj, z, h, gwb = _in_proj_fwd(x2d, norm1_g, w_in_t, wgu_p, b_gate, tm_in, stage)
    proj3 = proj.reshape(nb, s, PW)
    z3 = z.reshape(nb, s, NQK)
    mix3, opre3, sprev, gwa = _mix_fwd(proj3, z3, gla_norm_g, conv_f, stage)
    mix2d = mix3.reshape(t, D)
    dx1, dx1b, dmix, adu, hb, dg2, dgf, loss_part = _ffn_fwd_bwd(
        mix2d, x2d, tgt2d, gwa, gwb, norm2_g, norm_f_g.reshape(1, D), tm)
    dw3, dwb3 = _dw_ffn(adu, hb, tk)
    dw3 = dw3.reshape(3, N_DEV, FF_W, D)
    dw_o, dwb_o = _tn_matmul(mix2d, dx1b, D // 4, D, tk, "dw_out", True)
    dw_o = dw_o.reshape(N_DEV, OUT_ROWS, D)
    *pb, sib_d, sib_g, sib_u, sib_o = _ffn_core_reduce(
        dw3, dwb3.reshape(3, N_DEV, FF_W, D), dw_o, dwb_o.reshape(N_DEV, OUT_ROWS, D), pos_arr)
    g8 = [dw3, dw3, dw3, dw_o]
    leads = [0, 1, 2, None]
    tags = ("w_ffn_down", "w_ffn_gate", "w_ffn_up", "w_out")
    r1 = [sib_d, sib_g, sib_u, sib_o]
    mb = _mix_bwd(proj3, z3, sprev, opre3, dmix.reshape(nb, s, D), gla_norm_g, conv_f, wgu_p, [pb[0], pb[1], pb[3]])
    dproj3, dgng, dcw, dbg, dwgu = mb[:5]
    dproj2d = dproj3.reshape(t, PW)
    dw_in_t, r2_up = _tn_matmul(dproj2d, h, PW // 5, D, tk, "dw_in", False, _stage2_rider([pb[2]]))
    r2 = [mb[5], mb[6], r2_up, mb[7]]
    g_in, r1_in, pb_in = _w_in_core_reduce(dw_in_t)
    dx, small_sums, r2_in = _in_proj_bwd(dproj2d, x2d, dx1, norm1_g, w_in_t, tm_in, pb_in,
                                         (dg2, dgf, dbg, dgng, dwgu, dcw, loss_part))

    tags = ("w_in",) + tags
    g8 = [g_in] + g8
    leads = [None] + leads
    r1 = [r1_in] + list(r1)
    r2 = [r2_in] + r2
    shard_w = (tr(w_in), w_ffn_down[0], tr(w_ffn_gate), tr(w_ffn_up), w_out[0])
    shard_m = (tr(m_w_in), m_w_ffn_down[0], tr(m_w_ffn_gate), tr(m_w_ffn_up), m_w_out[0])
    shard_v = (tr(v_w_in), v_w_ffn_down[0], tr(v_w_ffn_gate), tr(v_w_ffn_up), v_w_out[0])
    transposed = (True, False, True, True, False)
    items = list(zip(g8, leads, r1, r2, shard_w, shard_m, shard_v))
    flat = list(_finish_weights(items[1:], pos_arr, "finish_ffn_out", 2))
    flat = list(_finish_weights(items[:1], pos_arr, "finish_w_in", 1)) + flat
    results = {}
    for i, (tag, tp) in enumerate(zip(tags, transposed)):
        results[tag] = [o.T[None] if tp else o[None] for o in flat[4 * i:4 * i + 4]]

    small_w = (norm1_g, norm2_g, norm_f_g.reshape(1, D), b_gate, gla_norm_g, w_gate_up[0], conv_w[0])
    small_m = (m_norm1_g, m_norm2_g, m_norm_f_g.reshape(1, D), m_b_gate, m_gla_norm_g, m_w_gate_up[0], m_conv_w[0])
    small_v = (v_norm1_g, v_norm2_g, v_norm_f_g.reshape(1, D), v_b_gate, v_gla_norm_g, v_w_gate_up[0], v_conv_w[0])
    so = _small_adamw(small_sums, small_w, small_m, small_v)
    loss = so[0].reshape(())
    shapes = {"norm_f_g": (D,), "w_gate_up": (1, RANK, WGU_W), "conv_w": (1, CONV_K, CONV_W)}
    for i, name in enumerate(SMALL_NAMES):
        results[name] = [o.reshape(shapes[name]) if name in shapes else o for o in so[1 + 4 * i:5 + 4 * i]]

    names = ("norm1_g", "w_in", "w_gate_up", "b_gate", "gla_norm_g", "conv_w", "w_out", "norm2_g",
             "w_ffn_gate", "w_ffn_up", "w_ffn_down", "norm_f_g")
    outs = [loss, dx.reshape(nb, s, D)]
    for kind in range(4):
        for name in names:
            outs.append(results[name][kind])
    return tuple(outs)
```

```python
import jax
import jax.numpy as jnp
from jax import lax
from jax.experimental import pallas as pl
from jax.experimental.pallas import tpu as pltpu

F32 = jnp.float32
BF16 = jnp.bfloat16
HIGHEST = lax.Precision.HIGHEST
MESH = pl.DeviceIdType.MESH

N_DEV = 8
D = 1024
DFF = 2816
HEADS = 4
DK = 64
DV = 128
NQK = HEADS * DK
NV = HEADS * DV
RANK = 16
CHUNK = 64
CW = 512
CONV_K = 3
IN_COLS = 3088
EPS = 1e-6
INV_GATE_NORM = 1.0 / 16.0
Q_SCALE = DK ** -0.5

PW = 3200
OQ, OK_, OV, OG, OCB, OCC, OCH, OA = 0, 256, 512, 1024, 1536, 2048, 2560, 3072
A_PAD = 128

ADAM_LR = 0.001
ADAM_B1 = 0.9
ADAM_B2 = 0.999
ADAM_EPS = 1e-08
ADAM_WD = 0.01
ADAM_STEP = 10

IN_W = IN_COLS // N_DEV
IN_ROWS = 400
FF_W = DFF // N_DEV
OUT_ROWS = D // N_DEV
SLAB_IN = 0
SLAB_G = SLAB_IN + IN_ROWS
SLAB_U = SLAB_G + FF_W
SLAB_D = SLAB_U + FF_W
SLAB_O = SLAB_D + FF_W
SLAB_ROWS = SLAB_O + OUT_ROWS

VMEM_LIMIT = 56 * 1024 * 1024


def _params(sem=None, vmem=VMEM_LIMIT):
    return pltpu.CompilerParams(dimension_semantics=sem, vmem_limit_bytes=vmem)


def _nt(a, b):
    return lax.dot_general(a, b, (((1,), (1,)), ((), ())), preferred_element_type=F32)


def _tn(a, b, precision=None):
    return lax.dot_general(a, b, (((0,), (0,)), ((), ())), preferred_element_type=F32, precision=precision)


def _mm(a, b, precision=None):
    return jnp.dot(a, b, preferred_element_type=F32, precision=precision)


def _in_segments():
    segs = []
    for j in range(N_DEV):
        lo, hi = IN_W * j, IN_W * (j + 1)
        cuts = sorted({lo, hi} | {c for c in (OCB, OCB + RANK) if lo < c < hi})
        for a, b in zip(cuts[:-1], cuts[1:]):
            if a < OCB:
                d = a
            elif a < OCB + RANK:
                d = OA + (a - OCB)
            else:
                d = a - RANK
            segs.append((j, a - lo, b - lo, d))
    return segs


def _in_proj_fwd(x2d, g1, w_in_t, wgu_p, b_gate, tm, stage):
    t = x2d.shape[0]
    nt = t // tm
    g_rows = SLAB_ROWS - SLAB_D

    def body(x_ref, g_ref, w_ref, wgu_ref, bg_ref, stage_hbm, proj_ref, z_ref, h_ref, gwb_ref,
             send_sems, recv_sems, local_sem):
        gargs = (stage_hbm, SLAB_D, g_rows, gwb_ref, send_sems, recv_sems, local_sem)

        @pl.when(pl.program_id(0) == 0)
        def _():
            _gather_start(*gargs)

        x = x_ref[...]
        r = lax.rsqrt(jnp.mean(x * x, axis=-1, keepdims=True) + EPS)
        h = ((x * r) * g_ref[...]).astype(BF16)
        h_ref[...] = h
        proj = _nt(h, w_ref[...])
        proj_ref[...] = proj
        pa = proj[:, OA:OA + A_PAD].astype(BF16)
        z_ref[...] = _mm(pa, wgu_ref[...]) + bg_ref[...]

        @pl.when(pl.program_id(0) == nt - 1)
        def _():
            _gather_finish(*gargs)

    return pl.pallas_call(
        body,
        name="in_proj_fwd",
        grid=(t // tm,),
        in_specs=[
            pl.BlockSpec((tm, D), lambda i: (i, 0)),
            pl.BlockSpec((1, D), lambda i: (0, 0)),
            pl.BlockSpec((PW, D), lambda i: (0, 0)),
            pl.BlockSpec((A_PAD, NQK), lambda i: (0, 0)),
            pl.BlockSpec((1, NQK), lambda i: (0, 0)),
            pl.BlockSpec(memory_space=pl.ANY),
        ],
        out_specs=[
            pl.BlockSpec((tm, PW), lambda i: (i, 0)),
            pl.BlockSpec((tm, NQK), lambda i: (i, 0)),
            pl.BlockSpec((tm, D), lambda i: (i, 0)),
            pl.BlockSpec(memory_space=pl.ANY),
        ],
        out_shape=[
            jax.ShapeDtypeStruct((t, PW), F32),
            jax.ShapeDtypeStruct((t, NQK), F32),
            jax.ShapeDtypeStruct((t, D), BF16),
            jax.ShapeDtypeStruct((N_DEV, g_rows, D), BF16),
        ],
        scratch_shapes=_gather_sems(),
        compiler_params=_params(("arbitrary",)),
    )(x2d, g1, w_in_t, wgu_p, b_gate, stage)


def _head_masks():
    lane = lax.broadcasted_iota(jnp.int32, (1, NQK), 1)
    return [(lane >= DK * h) & (lane < DK * (h + 1)) for h in range(HEADS)]


def _split_bf16(x, n):
    parts = []
    for _ in range(n):
        p = x.astype(BF16)
        parts.append(p)
        x = x - p.astype(F32)
    return parts


def _chunk_fwd_parts(q, k, z, tril16):
    la = (jnp.minimum(z, 0.0) - jnp.log1p(jnp.exp(-jnp.abs(z)))) * INV_GATE_NORM
    la_parts = _split_bf16(la, 3)
    bc = _mm(tril16, la_parts[0]) + _mm(tril16, la_parts[1]) + _mm(tril16, la_parts[2])
    bl = bc[CHUNK - 1:CHUNK, :]
    eb = jnp.exp(bc)
    enb = jnp.exp(-bc)
    ekl = jnp.exp(bl - bc)
    qi = (q * Q_SCALE) * eb
    ki = k * enb
    ks = k * ekl
    ones16 = jnp.ones((CHUNK, DV), BF16)
    decb = jnp.exp(_tn(la_parts[0], ones16) + _tn(la_parts[1], ones16) + _tn(la_parts[2], ones16))
    return la, eb, enb, ekl, qi, ki, ks, decb


def _stack_heads(a, masks):
    return jnp.concatenate([jnp.where(m, a, 0.0) for m in masks], axis=0)


def _merge_heads(blocks, masks):
    out = blocks[HEADS - 1]
    for h in range(HEADS - 2, -1, -1):
        out = jnp.where(masks[h], blocks[h], out)
    return out


def _causal_stack_mask():
    row = lax.broadcasted_iota(jnp.int32, (HEADS * CHUNK, CHUNK), 0)
    col = lax.broadcasted_iota(jnp.int32, (HEADS * CHUNK, CHUNK), 1)
    return (row & (CHUNK - 1)) >= col


def _conv_taps(u, uprev):
    row = lax.broadcasted_iota(jnp.int32, u.shape, 0)
    u1 = jnp.where(row < 1, pltpu.roll(uprev, 1, 0), pltpu.roll(u, 1, 0))
    u2 = jnp.where(row < 2, pltpu.roll(uprev, 2, 0), pltpu.roll(u, 2, 0))
    return u1, u2


def _mix_fwd(proj3, z3, gng, conv_w, stage):
    nb, s, _ = proj3.shape
    nc = s // CHUNK
    g_rows = SLAB_D - SLAB_G

    def body(p_ref, z_ref, gng_ref, cw_ref, stage_hbm, mix_ref, o_ref, sprev_ref, gwa_ref, s_ref, uprev_ref,
             send_sems, recv_sems, local_sem):
        n = pl.program_id(0)
        gargs = (stage_hbm, SLAB_G, g_rows, gwa_ref, send_sems, recv_sems, local_sem)

        @pl.when(n == 0)
        def _():
            _gather_start(*gargs)
            s_ref[...] = jnp.zeros_like(s_ref)
            uprev_ref[...] = jnp.zeros_like(uprev_ref)

        r_i = lax.broadcasted_iota(jnp.int32, (CHUNK, CHUNK), 0)
        c_i = lax.broadcasted_iota(jnp.int32, (CHUNK, CHUNK), 1)
        tril16 = (r_i >= c_i).astype(BF16)
        masks = _head_masks()
        cmask = _causal_stack_mask()
        gg = gng_ref[...]
        for b in range(nb):
            q = p_ref[b, :, OQ:OQ + NQK]
            k = p_ref[b, :, OK_:OK_ + NQK]
            _, _, _, _, qi, ki, ks, decb = _chunk_fwd_parts(q, k, z_ref[b], tril16)
            qs = _stack_heads(qi, masks).astype(BF16)
            sc = jnp.where(cmask, _nt(qs, ki.astype(BF16)), 0.0).astype(BF16)
            st = s_ref[b]
            sprev_ref[b, 0] = st
            o_inter = _mm(qs, st.astype(BF16))
            v16 = p_ref[b, :, OV:OV + NV].astype(BF16)
            kv = _tn(ks.astype(BF16), v16)
            for h in range(HEADS):
                rows = slice(CHUNK * h, CHUNK * (h + 1))
                cols = slice(DV * h, DV * (h + 1))
                o = _mm(sc[rows], v16[:, cols]) + o_inter[rows]
                o_ref[b, :, cols] = o
                r = lax.rsqrt(jnp.mean(o * o, axis=-1, keepdims=True) + EPS)
                on = (o * r) * gg
                g = p_ref[b, :, OG + DV * h:OG + DV * (h + 1)]
                mix_ref[b, :, cols] = (on * (g * jax.nn.sigmoid(g))).astype(BF16)
                s_ref[b, rows, :] = decb[rows] * st[rows] + kv[rows, cols]
            u = p_ref[b, :, OCC:OCC + CW] * p_ref[b, :, OCH:OCH + CW]
            u1, u2 = _conv_taps(u, uprev_ref[b])
            yc = cw_ref[0:1, :] * u2 + cw_ref[1:2, :] * u1 + cw_ref[2:3, :] * u
            mix_ref[b, :, NV:NV + CW] = (p_ref[b, :, OCB:OCB + CW] * yc).astype(BF16)
            uprev_ref[b] = u

        @pl.when(n == nc - 1)
        def _():
            _gather_finish(*gargs)

    return pl.pallas_call(
        body,
        name="mix_fwd",
        grid=(nc,),
        in_specs=[
            pl.BlockSpec((nb, CHUNK, PW), lambda n: (0, n, 0)),
            pl.BlockSpec((nb, CHUNK, NQK), lambda n: (0, n, 0)),
            pl.BlockSpec((1, DV), lambda n: (0, 0)),
            pl.BlockSpec((CONV_K, CW), lambda n: (0, 0)),
            pl.BlockSpec(memory_space=pl.ANY),
        ],
        out_specs=[
            pl.BlockSpec((nb, CHUNK, D), lambda n: (0, n, 0)),
            pl.BlockSpec((nb, CHUNK, NV), lambda n: (0, n, 0)),
            pl.BlockSpec((nb, 1, NQK, DV), lambda n: (0, n, 0, 0)),
            pl.BlockSpec(memory_space=pl.ANY),
        ],
        out_shape=[
            jax.ShapeDtypeStruct((nb, s, D), BF16),
            jax.ShapeDtypeStruct((nb, s, NV), F32),
            jax.ShapeDtypeStruct((nb, nc, NQK, DV), F32),
            jax.ShapeDtypeStruct((N_DEV, g_rows, D), BF16),
        ],
        scratch_shapes=[pltpu.VMEM((nb, NQK, DV), F32), pltpu.VMEM((nb, CHUNK, CW), F32)] + _gather_sems(),
        compiler_params=_params(("arbitrary",)),
    )(proj3, z3, gng, conv_w, stage)


def _ffn_fwd_bwd(mix2d, x2d, tgt2d, gwa, gwb, g2, gf, tm):
    t = x2d.shape[0]

    def body(mix_ref, x_ref, tgt_ref, g2_ref, gf_ref, gwa_hbm, gwb_hbm,
             dx1_ref, dx1b_ref, dmix_ref, adu_ref, hb_ref, dg2_ref, dgf_ref, loss_ref,
             wo, wg, wu, wd, wsem):
        i = pl.program_id(0)

        def weight_copies(n, dst, src, off, rows):
            return [pltpu.make_async_copy(src.at[j, pl.ds(off, rows), :], dst.at[pl.ds(rows * j, rows), :],
                                          wsem.at[N_DEV * n + j]) for j in range(N_DEV)]

        loads = (weight_copies(0, wo, gwb_hbm, FF_W, OUT_ROWS), weight_copies(1, wg, gwa_hbm, 0, FF_W),
                 weight_copies(2, wu, gwa_hbm, FF_W, FF_W), weight_copies(3, wd, gwb_hbm, 0, FF_W))

        @pl.when(i == 0)
        def _():
            for group in loads:
                for cp in group:
                    cp.start()
            dg2_ref[...] = jnp.zeros_like(dg2_ref)
            dgf_ref[...] = jnp.zeros_like(dgf_ref)
            loss_ref[...] = jnp.zeros_like(loss_ref)
            for group in loads:
                for cp in group:
                    cp.wait()

        g2v = g2_ref[...]
        gfv = gf_ref[...]
        x1 = x_ref[...] + _mm(mix_ref[...], wo[...])
        r2 = lax.rsqrt(jnp.mean(x1 * x1, axis=-1, keepdims=True) + EPS)
        n2 = x1 * r2
        h2 = (n2 * g2v).astype(BF16)
        hb_ref[1] = h2
        gate = _nt(h2, wg[...])
        up = _nt(h2, wu[...])
        sg = jax.nn.sigmoid(gate)
        sil = gate * sg
        act = (sil * up).astype(BF16)
        adu_ref[0] = act
        x2 = x1 + _mm(act, wd[...])
        rf = lax.rsqrt(jnp.mean(x2 * x2, axis=-1, keepdims=True) + EPS)
        nf = x2 * rf
        err = nf * gfv - tgt_ref[...]
        loss_ref[...] += 0.5 * jnp.sum(jnp.mean(err * err, axis=-1, keepdims=True))
        dy = err * (1.0 / D)
        dgf_ref[...] += jnp.sum(dy * nf, axis=0, keepdims=True)
        dnf = dy * gfv
        dx2 = rf * (dnf - nf * jnp.mean(dnf * nf, axis=-1, keepdims=True))
        dx2b = dx2.astype(BF16)
        hb_ref[0] = dx2b
        dact = _nt(dx2b, wd[...])
        dup = (dact * sil).astype(BF16)
        dgate = ((dact * up) * (sg * (1.0 + gate * (1.0 - sg)))).astype(BF16)
        adu_ref[2] = dup
        adu_ref[1] = dgate
        dh2 = _mm(dgate, wg[...]) + _mm(dup, wu[...])
        dg2_ref[...] += jnp.sum(dh2 * n2, axis=0, keepdims=True)
        dn2 = dh2 * g2v
        dx1 = dx2 + r2 * (dn2 - n2 * jnp.mean(dn2 * n2, axis=-1, keepdims=True))
        dx1_ref[...] = dx1
        dx1b = dx1.astype(BF16)
        dx1b_ref[...] = dx1b
        dmix_ref[...] = _nt(dx1b, wo[...])

    tile = lambda w: pl.BlockSpec((tm, w), lambda i: (i, 0))
    vec = pl.BlockSpec((1, D), lambda i: (0, 0))
    hbm = pl.BlockSpec(memory_space=pl.ANY)
    return pl.pallas_call(
        body,
        name="ffn_fwd_bwd",
        grid=(t // tm,),
        in_specs=[tile(D), tile(D), tile(D), vec, vec, hbm, hbm],
        out_specs=[tile(D), tile(D), tile(D), pl.BlockSpec((3, tm, DFF), lambda i: (0, i, 0)),
                   pl.BlockSpec((2, tm, D), lambda i: (0, i, 0)), vec, vec,
                   pl.BlockSpec((1, 128), lambda i: (0, 0))],
        out_shape=[
            jax.ShapeDtypeStruct((t, D), F32),
            jax.ShapeDtypeStruct((t, D), BF16),
            jax.ShapeDtypeStruct((t, D), F32),
            jax.ShapeDtypeStruct((3, t, DFF), BF16),
            jax.ShapeDtypeStruct((2, t, D), BF16),
            jax.ShapeDtypeStruct((1, D), F32),
            jax.ShapeDtypeStruct((1, D), F32),
            jax.ShapeDtypeStruct((1, 128), F32),
        ],
        scratch_shapes=[pltpu.VMEM((D, D), BF16), pltpu.VMEM((DFF, D), BF16), pltpu.VMEM((DFF, D), BF16),
                        pltpu.VMEM((DFF, D), BF16), pltpu.SemaphoreType.DMA((4 * N_DEV,))],
        compiler_params=_params(("arbitrary",)),
    )(mix2d, x2d, tgt2d, g2, gf, gwa, gwb)


def _stage2_rider(pbs):
    return dict(inputs=list(pbs), out_shape=[jax.ShapeDtypeStruct(p.shape, BF16) for p in pbs], nsem=3 * len(pbs),
                copies=_stage2_copies)


def _tn_matmul(a, b, bm, bn, tk, name, with_bf16, rider=None):
    t, m = a.shape
    n = b.shape[1]
    nk = t // tk
    nout = 2 if with_bf16 else 1
    grid = (m // bm, n // bn, nk)
    r_in = [] if rider is None else rider["inputs"]
    r_out = [] if rider is None else rider["out_shape"]

    def body(a_ref, b_ref, *rest):
        ins, outs = rest[:len(r_in)], rest[len(r_in):len(r_in) + nout]
        r_outs, sems = rest[len(r_in) + nout:len(r_in) + nout + len(r_out)], rest[len(r_in) + nout + len(r_out):]
        o_ref = outs[0]
        i, j, k = pl.program_id(0), pl.program_id(1), pl.program_id(2)
        if rider is not None:
            @pl.when((i == 0) & (j == 0) & (k == 0))
            def _():
                for cp in rider["copies"](ins, r_outs, *sems):
                    cp.start()

        @pl.when(k == 0)
        def _():
            o_ref[...] = jnp.zeros_like(o_ref)

        o_ref[...] += _tn(a_ref[...].astype(BF16), b_ref[...].astype(BF16))
        if with_bf16:
            @pl.when(k == nk - 1)
            def _():
                outs[1][...] = o_ref[...].astype(BF16)
        if rider is not None:
            @pl.when((i == grid[0] - 1) & (j == grid[1] - 1) & (k == nk - 1))
            def _():
                copies = rider["copies"](ins, r_outs, *sems)
                for cp in copies:
                    cp.wait_recv()
                for cp in copies:
                    cp.wait_send()

    out_blk = pl.BlockSpec((bm, bn), lambda i, j, k: (i, j))
    hbm = pl.BlockSpec(memory_space=pl.ANY)
    out_shape = [jax.ShapeDtypeStruct((m, n), F32)] + ([jax.ShapeDtypeStruct((m, n), BF16)] if with_bf16 else [])
    res = pl.pallas_call(
        body,
        name=name,
        grid=grid,
        in_specs=[pl.BlockSpec((tk, bm), lambda i, j, k: (k, i)), pl.BlockSpec((tk, bn), lambda i, j, k: (k, j))]
        + [hbm] * len(r_in),
        out_specs=[out_blk] * nout + [hbm] * len(r_out),
        out_shape=out_shape + list(r_out),
        scratch_shapes=([] if rider is None else
                        [pltpu.SemaphoreType.DMA((rider["nsem"],)), pltpu.SemaphoreType.DMA((rider["nsem"],))]),
        compiler_params=_params(("parallel", "parallel", "arbitrary") if rider is None
                                else ("arbitrary", "arbitrary", "arbitrary")),
    )(a, b, *r_in)
    return res[0] if len(res) == 1 else res


def _dw_ffn(adu, hb, tk):
    _, t, _ = adu.shape
    bm = DFF // 2
    nk = t // tk

    def body(a_ref, b_ref, o_ref, ob_ref):
        k = pl.program_id(2)

        @pl.when(k == 0)
        def _():
            o_ref[...] = jnp.zeros_like(o_ref)

        o_ref[...] += _tn(a_ref[...], b_ref[...])

        @pl.when(k == nk - 1)
        def _():
            ob_ref[...] = o_ref[...].astype(BF16)

    out_blk = pl.BlockSpec((None, bm, D), lambda p, i, k: (p, i, 0))
    return pl.pallas_call(
        body,
        name="dw_ffn",
        grid=(3, DFF // bm, nk),
        in_specs=[pl.BlockSpec((None, tk, bm), lambda p, i, k: (p, k, i)),
                  pl.BlockSpec((None, tk, D), lambda p, i, k: (jnp.minimum(p, 1), k, 0))],
        out_specs=[out_blk, out_blk],
        out_shape=[jax.ShapeDtypeStruct((3, DFF, D), F32), jax.ShapeDtypeStruct((3, DFF, D), BF16)],
        compiler_params=_params(("arbitrary", "arbitrary", "arbitrary")),
    )(adu, hb)


def _mix_bwd(proj3, z3, sprev, opre3, dmix3, gng, conv_w, wgu_p, pbs):
    nb, s, _ = proj3.shape
    nc = s // CHUNK
    na = len(pbs)

    def body(*refs):
        (p_ref, pprev_ref, z_ref, sp_ref, o_ref, dm_ref, gng_ref, cw_ref, wgu_ref) = refs[:9]
        pb_refs = refs[9:9 + na]
        (dproj_ref, dgng_ref, dcw_ref, dbg_ref, dwgu_ref) = refs[9 + na:14 + na]
        r2_refs = refs[14 + na:14 + 2 * na]
        ds_ref, dycn_ref, send_sems, recv_sems = refs[14 + 2 * na:]
        step = pl.program_id(0)
        n = nc - 1 - step

        @pl.when(step == 0)
        def _():
            for cp in _stage2_copies(pb_refs, r2_refs, send_sems, recv_sems):
                cp.start()
            ds_ref[...] = jnp.zeros_like(ds_ref)
            dycn_ref[...] = jnp.zeros_like(dycn_ref)
            dgng_ref[...] = jnp.zeros_like(dgng_ref)
            dcw_ref[...] = jnp.zeros_like(dcw_ref)
            dbg_ref[...] = jnp.zeros_like(dbg_ref)
            dwgu_ref[...] = jnp.zeros_like(dwgu_ref)

        r_i = lax.broadcasted_iota(jnp.int32, (CHUNK, CHUNK), 0)
        c_i = lax.broadcasted_iota(jnp.int32, (CHUNK, CHUNK), 1)
        tril16 = (r_i >= c_i).astype(BF16)
        triu16 = (r_i <= c_i).astype(BF16)
        causal = r_i >= c_i
        masks = _head_masks()
        cmask = _causal_stack_mask()
        gg = gng_ref[...]
        last_row = lax.broadcasted_iota(jnp.int32, (CHUNK, NQK), 0) == CHUNK - 1
        ones_r = jnp.ones((16, DV), BF16)
        has_prev = (n > 0).astype(F32)
        for b in range(nb):
            q = p_ref[b, :, OQ:OQ + NQK]
            k = p_ref[b, :, OK_:OK_ + NQK]
            z = z_ref[b]
            _, eb, enb, ekl, qi, ki, ks, decb = _chunk_fwd_parts(q, k, z, tril16)
            qi16 = qi.astype(BF16)
            ki16 = ki.astype(BF16)
            qs = _stack_heads(qi, masks).astype(BF16)
            sc = jnp.where(cmask, _nt(qs, ki16), 0.0).astype(BF16)
            st = sp_ref[b, 0]
            st16 = st.astype(BF16)
            dsn = ds_ref[b]
            dsn16 = dsn.astype(BF16)
            v16 = p_ref[b, :, OV:OV + NV].astype(BF16)
            do16 = []
            dgng = jnp.zeros((1, DV), F32)
            for h in range(HEADS):
                cols = slice(DV * h, DV * (h + 1))
                o = o_ref[b, :, cols]
                r = lax.rsqrt(jnp.mean(o * o, axis=-1, keepdims=True) + EPS)
                nh = o * r
                g = p_ref[b, :, OG + DV * h:OG + DV * (h + 1)]
                sg = jax.nn.sigmoid(g)
                dog = dm_ref[b, :, cols]
                dproj_ref[b, :, OG + DV * h:OG + DV * (h + 1)] = (
                    (dog * (nh * gg)) * (sg * (1.0 + g * (1.0 - sg)))).astype(BF16)
                don = dog * (g * sg)
                dgng = dgng + jnp.sum(don * nh, axis=0, keepdims=True)
                dn = don * gg
                do = r * (dn - nh * jnp.mean(dn * nh, axis=-1, keepdims=True))
                do16.append(do.astype(BF16))
            dgng_ref[...] += dgng
            do_rows = jnp.concatenate(do16, axis=0)
            v_rows = jnp.concatenate([v16[:, DV * h:DV * (h + 1)] for h in range(HEADS)], axis=0)
            dp16 = [jnp.where(causal, _nt(do16[h], v16[:, DV * h:DV * (h + 1)]), 0.0).astype(BF16)
                    for h in range(HEADS)]
            ks_dsn = _mm(_stack_heads(ks, masks).astype(BF16), dsn16)
            do_st = _nt(do_rows, st16)
            v_dsn = _nt(v_rows, dsn16)
            dp_ki = _mm(jnp.concatenate(dp16, axis=0), ki16)
            q_do = _tn(qi16, jnp.concatenate(do16, axis=1))
            dki_h = []
            for h in range(HEADS):
                rows = slice(CHUNK * h, CHUNK * (h + 1))
                cols = slice(DV * h, DV * (h + 1))
                dv = _tn(sc[rows], do16[h]) + ks_dsn[rows]
                dproj_ref[b, :, OV + DV * h:OV + DV * (h + 1)] = dv.astype(BF16)
                dki_h.append(_tn(dp16[h], qi16))
                ds_ref[b, rows, :] = decb[rows] * dsn[rows] + q_do[rows, cols]
            blocks = lambda a: [a[CHUNK * h:CHUNK * (h + 1)] for h in range(HEADS)]
            dqi = _merge_heads(blocks(dp_ki + do_st), masks)
            dki = _merge_heads(dki_h, masks)
            dks = _merge_heads(blocks(v_dsn), masks)
            dproj_ref[b, :, OQ:OQ + NQK] = (dqi * (Q_SCALE * eb)).astype(BF16)
            dproj_ref[b, :, OK_:OK_ + NQK] = (dki * enb + dks * ekl).astype(BF16)
            dks_ks = dks * ks
            db = dqi * qi - dki * ki - dks_ks
            sd = _split_bf16(dsn * st * decb, 2)
            dbl = jnp.sum(dks_ks, axis=0, keepdims=True) + (_nt(ones_r, sd[0]) + _nt(ones_r, sd[1]))[0:1, :]
            db = db + jnp.where(last_row, dbl, 0.0)
            db_parts = _split_bf16(db, 3)
            dla = _mm(triu16, db_parts[0]) + _mm(triu16, db_parts[1]) + _mm(triu16, db_parts[2])
            dz = (dla * INV_GATE_NORM) * (1.0 / (1.0 + jnp.exp(z)))
            dbg_ref[...] += jnp.sum(dz, axis=0, keepdims=True)
            dz16 = dz.astype(BF16)
            pa16 = p_ref[b, :, OA:OA + A_PAD].astype(BF16)
            dwgu_ref[...] += _tn(pa16, dz16)
            dproj_ref[b, :, OA:OA + A_PAD] = _nt(dz16, wgu_ref[...]).astype(BF16)
            cb = p_ref[b, :, OCB:OCB + CW]
            cc = p_ref[b, :, OCC:OCC + CW]
            ch = p_ref[b, :, OCH:OCH + CW]
            u = cc * ch
            uprev = (pprev_ref[b, :, 0:CW] * pprev_ref[b, :, CW:2 * CW]) * has_prev
            u1, u2 = _conv_taps(u, uprev)
            w0 = cw_ref[0:1, :]
            w1 = cw_ref[1:2, :]
            w2 = cw_ref[2:3, :]
            yc = w0 * u2 + w1 * u1 + w2 * u
            doc = dm_ref[b, :, NV:NV + CW]
            dproj_ref[b, :, OCB:OCB + CW] = (doc * yc).astype(BF16)
            dyc = doc * cb
            dycn = dycn_ref[b]
            row = lax.broadcasted_iota(jnp.int32, dyc.shape, 0)
            d1 = jnp.where(row >= CHUNK - 1, pltpu.roll(dycn, CHUNK - 1, 0), pltpu.roll(dyc, CHUNK - 1, 0))
            d2 = jnp.where(row >= CHUNK - 2, pltpu.roll(dycn, CHUNK - 2, 0), pltpu.roll(dyc, CHUNK - 2, 0))
            du = w2 * dyc + w1 * d1 + w0 * d2
            dproj_ref[b, :, OCC:OCC + CW] = (du * ch).astype(BF16)
            dproj_ref[b, :, OCH:OCH + CW] = (du * cc).astype(BF16)
            dcw_ref[0:1, :] += jnp.sum(dyc * u2, axis=0, keepdims=True)
            dcw_ref[1:2, :] += jnp.sum(dyc * u1, axis=0, keepdims=True)
            dcw_ref[2:3, :] += jnp.sum(dyc * u, axis=0, keepdims=True)
            dycn_ref[b] = dyc

        @pl.when(step == nc - 1)
        def _():
            copies = _stage2_copies(pb_refs, r2_refs, send_sems, recv_sems)
            for cp in copies:
                cp.wait_recv()
            for cp in copies:
                cp.wait_send()

    rev = lambda w: pl.BlockSpec((nb, CHUNK, w), lambda i: (0, nc - 1 - i, 0))
    const = lambda r, c: pl.BlockSpec((r, c), lambda i: (0, 0))
    hbm = pl.BlockSpec(memory_space=pl.ANY)
    return pl.pallas_call(
        body,
        name="mix_bwd",
        grid=(nc,),
        in_specs=[
            rev(PW),
            pl.BlockSpec((nb, CHUNK, 2 * CW), lambda i: (0, jnp.maximum(nc - 2 - i, 0), OCC // (2 * CW))),
            rev(NQK),
            pl.BlockSpec((nb, 1, NQK, DV), lambda i: (0, nc - 1 - i, 0, 0)),
            rev(NV),
            rev(D),
            const(1, DV),
            const(CONV_K, CW),
            const(A_PAD, NQK),
        ] + [hbm] * na,
        out_specs=[rev(PW), const(1, DV), const(8, CW), const(1, NQK), const(A_PAD, NQK)] + [hbm] * na,
        out_shape=[
            jax.ShapeDtypeStruct((nb, s, PW), BF16),
            jax.ShapeDtypeStruct((1, DV), F32),
            jax.ShapeDtypeStruct((8, CW), F32),
            jax.ShapeDtypeStruct((1, NQK), F32),
            jax.ShapeDtypeStruct((A_PAD, NQK), F32),
        ] + [jax.ShapeDtypeStruct((3,) + p.shape[1:], BF16) for p in pbs],
        scratch_shapes=[pltpu.VMEM((nb, NQK, DV), F32), pltpu.VMEM((nb, CHUNK, CW), F32),
                        pltpu.SemaphoreType.DMA((3 * na,)), pltpu.SemaphoreType.DMA((3 * na,))],
        compiler_params=_params(("arbitrary",)),
    )(proj3, proj3, z3, sprev, opre3, dmix3, gng, conv_w, wgu_p, *pbs)


SMALL_PACK_ROWS = 16


def _wgu_slot(r):
    return 4 + r // 4, NQK * (r % 4)


CONV_SLOTS = ((8, 0), (8, CW), (9, 0))


def _in_proj_bwd(dproj2d, x2d, dx1, g1, w_in_t, tm, pb, small_parts):
    t = x2d.shape[0]
    nt = t // tm

    def body(dp_ref, x_ref, dx1_ref, g_ref, w_ref, pb_ref, dg2, dgf, dbg, dgng, dwgu, dcw, lp,
             dx_ref, sums_ref, r2_ref, dg1_acc, pack, gbuf, pack1, gbuf1, send_sems, recv_sems,
             ssend, srecv, ssend1, srecv1):
        x, y, c = _position()
        me = 4 * x + 2 * y + c
        flips = [(k >> 2, (k >> 1) & 1, k & 1) for k in range(1, N_DEV)]
        peers = [(x ^ fx, y ^ fy, c ^ fc) for fx, fy, fc in flips]

        def small_copies(src, dst, send, recv, arrivals):
            return [pltpu.make_async_remote_copy(
                src_ref=src, dst_ref=dst.at[4 * px + 2 * py + pc if arrivals else me],
                send_sem=send.at[k], recv_sem=recv.at[k], device_id=(px, py, pc), device_id_type=MESH)
                for k, (px, py, pc) in enumerate(peers)]

        @pl.when(pl.program_id(0) == 0)
        def _():
            for cp in _stage2_copies([pb_ref], [r2_ref], send_sems, recv_sems):
                cp.start()
            dg1_acc[...] = jnp.zeros_like(dg1_acc)
            pack[...] = jnp.zeros_like(pack)
            pack[1:2, :] = dg2[...]
            pack[2:3, :] = dgf[...]
            pack[3:4, 0:NQK] = dbg[...]
            pack[3:4, NQK:NQK + DV] = dgng[...]
            pack[3:4, NQK + DV:NQK + 2 * DV] = lp[...]
            for r in range(RANK):
                row, lane = _wgu_slot(r)
                pack[row:row + 1, lane:lane + NQK] = dwgu[r:r + 1, :]
            for r, (row, lane) in enumerate(CONV_SLOTS):
                pack[row:row + 1, lane:lane + CW] = dcw[r:r + 1, :]
            for cp in small_copies(pack, gbuf, ssend, srecv, False):
                cp.start()
            gbuf[me] = pack[...]

        xv = x_ref[...]
        r = lax.rsqrt(jnp.mean(xv * xv, axis=-1, keepdims=True) + EPS)
        n1 = xv * r
        dh = _mm(dp_ref[...], w_ref[...])
        dg1_acc[...] += jnp.sum(dh * n1, axis=0, keepdims=True)
        dn = dh * g_ref[...]
        dx_ref[...] = dx1_ref[...] + r * (dn - n1 * jnp.mean(dn * n1, axis=-1, keepdims=True))

        @pl.when(pl.program_id(0) == nt - 1)
        def _():
            pack1[...] = jnp.zeros_like(pack1)
            pack1[0:1, :] = dg1_acc[...]
            for cp in small_copies(pack1, gbuf1, ssend1, srecv1, False):
                cp.start()
            gbuf1[me] = pack1[...]
            copies = _stage2_copies([pb_ref], [r2_ref], send_sems, recv_sems)
            for cp in copies:
                cp.wait_recv()
            for cp in copies:
                cp.wait_send()
            for src, dst, send, recv in ((pack, gbuf, ssend, srecv), (pack1, gbuf1, ssend1, srecv1)):
                for cp in small_copies(src, dst, send, recv, True):
                    cp.wait_recv()
                    cp.wait_send()
            acc = gbuf[0]
            acc1 = gbuf1[0]
            for d in range(1, N_DEV):
                acc = acc + gbuf[d]
                acc1 = acc1 + gbuf1[d]
            sums_ref[...] = acc
            sums_ref[0:1, :] = acc1[0:1, :]

    tile = lambda w: pl.BlockSpec((tm, w), lambda i: (i, 0))
    vec = pl.BlockSpec((1, D), lambda i: (0, 0))
    hbm = pl.BlockSpec(memory_space=pl.ANY)
    whole = lambda a: pl.BlockSpec(a.shape, lambda i: (0,) * a.ndim)
    return pl.pallas_call(
        body,
        name="in_proj_bwd",
        grid=(nt,),
        in_specs=[tile(PW), tile(D), tile(D), vec, pl.BlockSpec((PW, D), lambda i: (0, 0)), hbm]
        + [whole(a) for a in small_parts],
        out_specs=[tile(D), pl.BlockSpec((SMALL_PACK_ROWS, D), lambda i: (0, 0)), hbm],
        out_shape=[jax.ShapeDtypeStruct((t, D), F32), jax.ShapeDtypeStruct((SMALL_PACK_ROWS, D), F32),
                   jax.ShapeDtypeStruct((3,) + pb.shape[1:], BF16)],
        scratch_shapes=[pltpu.VMEM((1, D), F32),
                        pltpu.VMEM((SMALL_PACK_ROWS, D), F32), pltpu.VMEM((N_DEV, SMALL_PACK_ROWS, D), F32),
                        pltpu.VMEM((8, D), F32), pltpu.VMEM((N_DEV, 8, D), F32),
                        pltpu.SemaphoreType.DMA((3,)), pltpu.SemaphoreType.DMA((3,)),
                        pltpu.SemaphoreType.DMA((7,)), pltpu.SemaphoreType.DMA((7,)),
                        pltpu.SemaphoreType.DMA((7,)), pltpu.SemaphoreType.DMA((7,))],
        compiler_params=_params(("arbitrary",)),
    )(dproj2d, x2d, dx1, g1, w_in_t, pb, *small_parts)


def _adamw_math(w, g, m, v):
    m = ADAM_B1 * m + (1.0 - ADAM_B1) * g
    v = ADAM_B2 * v + (1.0 - ADAM_B2) * (g * g)
    m_hat = m / (1.0 - ADAM_B1 ** ADAM_STEP)
    v_hat = v / (1.0 - ADAM_B2 ** ADAM_STEP)
    delta = -ADAM_LR * (m_hat / (jnp.sqrt(v_hat) + ADAM_EPS) + ADAM_WD * w)
    return delta, m, v


def _position():
    return lax.axis_index("x"), lax.axis_index("y"), lax.axis_index("c")


GATHER_PARTS = 2
GATHER_SEMS = 7 * GATHER_PARTS


def _gather_copies(stage, lo, rows, gx, send_sems, recv_sems, local_sem):
    x, y, c = _position()
    me = (x, y, c)
    sibling = (x, y, 1 - c)
    chips = [(1 - x, y), (x, 1 - y), (1 - x, 1 - y)]
    part = -(-rows // (16 * GATHER_PARTS)) * 16
    bounds = [(p * part, min(part, rows - p * part)) for p in range(GATHER_PARTS)]

    def blk(px, py, pc, off, n):
        return gx.at[4 * px + 2 * py + pc, pl.ds(off, n), :]

    mine = pltpu.make_async_copy(stage.at[pl.ds(lo, rows), :], gx.at[4 * x + 2 * y + c], local_sem)
    parts = []
    for p, (off, n) in enumerate(bounds):
        def copy(k, block, to, from_stage=False, p=p, off=off, n=n):
            return pltpu.make_async_remote_copy(
                src_ref=stage.at[pl.ds(lo + off, n), :] if from_stage else blk(*block, off, n),
                dst_ref=blk(*block, off, n), send_sem=send_sems.at[7 * p + k], recv_sem=recv_sems.at[7 * p + k],
                device_id=to, device_id_type=MESH)

        first = [copy(0, me, sibling, True)] + [copy(1 + j, me, (*chip, c), True) for j, chip in enumerate(chips)]
        passed = [copy(4 + j, (*chip, c), sibling) for j, chip in enumerate(chips)]
        arrivals = ([copy(0, sibling, me)] + [copy(1 + j, (*chip, c), me) for j, chip in enumerate(chips)]
                    + [copy(4 + j, (*chip, 1 - c), me) for j, chip in enumerate(chips)])
        parts.append((first, passed, arrivals))
    return mine, parts


def _gather_start(*args):
    mine, parts = _gather_copies(*args)
    mine.start()
    for first, _, _ in parts:
        for cp in first:
            cp.start()


def _gather_finish(*args):
    mine, parts = _gather_copies(*args)
    for _, passed, arrivals in parts:
        for j in range(3):
            arrivals[1 + j].wait_recv()
            passed[j].start()
    for first, passed, arrivals in parts:
        arrivals[0].wait_recv()
        for j in range(3):
            arrivals[4 + j].wait_recv()
        for cp in first + passed:
            cp.wait_send()
    mine.wait()


def _gather_sems():
    return [pltpu.SemaphoreType.DMA((GATHER_SEMS,)), pltpu.SemaphoreType.DMA((GATHER_SEMS,)), pltpu.SemaphoreType.DMA]


def _gather_w_in(w_it, w_gt, w_ut, w_d, w_o, wgu_s, conv_s):
    def body(wi_ref, wg_ref, wu_ref, wd_ref, wo_ref, wgu_ref, conv_ref, w_ref, gwgu_ref, gconv_ref, stage,
             buf, send_sems, recv_sems, local_sem, ssend, srecv):
        x, y, c = _position()
        me = 4 * x + 2 * y + c
        stage[SLAB_IN:SLAB_IN + IN_W, :] = wi_ref[...].astype(BF16)
        stage[SLAB_IN + IN_W:SLAB_G, :] = jnp.zeros((IN_ROWS - IN_W, D), BF16)
        args = (stage, SLAB_IN, IN_ROWS, buf, send_sems, recv_sems, local_sem)
        _gather_start(*args)
        stage[SLAB_G:SLAB_U, :] = wg_ref[...].astype(BF16)
        stage[SLAB_U:SLAB_D, :] = wu_ref[...].astype(BF16)
        stage[SLAB_D:SLAB_O, :] = wd_ref[...].astype(BF16)
        stage[SLAB_O:SLAB_ROWS, :] = wo_ref[...].astype(BF16)
        flips = [(k >> 2, (k >> 1) & 1, k & 1) for k in range(1, N_DEV)]
        peers = [(x ^ fx, y ^ fy, c ^ fc) for fx, fy, fc in flips]

        def small(k, block_id, to):
            return [pltpu.make_async_remote_copy(
                src_ref=s, dst_ref=g.at[block_id], send_sem=ssend.at[2 * k + n], recv_sem=srecv.at[2 * k + n],
                device_id=to, device_id_type=MESH)
                for n, (s, g) in enumerate(((wgu_ref, gwgu_ref), (conv_ref, gconv_ref)))]

        gwgu_ref[me] = wgu_ref[...]
        gconv_ref[me] = conv_ref[...]
        for k, peer in enumerate(peers):
            for cp in small(k, me, peer):
                cp.start()
        w_ref[IN_COLS:PW, :] = jnp.zeros((PW - IN_COLS, D), BF16)
        _gather_finish(*args)
        for k, (px, py, pc) in enumerate(peers):
            for cp in small(k, 4 * px + 2 * py + pc, (px, py, pc)):
                cp.wait_recv()
                cp.wait_send()
        for j, lo, hi, d in _in_segments():
            w_ref[d:d + hi - lo, :] = buf[j, lo:hi, :]

    vm = pl.BlockSpec(memory_space=pltpu.VMEM)
    return pl.pallas_call(
        body,
        name="gather_w_in",
        in_specs=[vm] * 7,
        out_specs=[vm] * 4,
        out_shape=[jax.ShapeDtypeStruct((PW, D), BF16),
                   jax.ShapeDtypeStruct((N_DEV,) + wgu_s.shape, F32),
                   jax.ShapeDtypeStruct((N_DEV,) + conv_s.shape, F32),
                   jax.ShapeDtypeStruct((SLAB_ROWS, D), BF16)],
        scratch_shapes=[pltpu.VMEM((N_DEV, IN_ROWS, D), BF16)] + _gather_sems()
        + [pltpu.SemaphoreType.DMA((14,)), pltpu.SemaphoreType.DMA((14,))],
        compiler_params=_params(),
    )(w_it, w_gt, w_ut, w_d, w_o, wgu_s, conv_s)


def _w_in_core_reduce(dw_t):
    def body(d_ref, own_ref, sib_ref, pb_ref, g, gb, r1, send_sems, recv_sems):
        x, y, c = _position()
        chip = 2 * x + y
        for j in range(N_DEV):
            g[j, IN_W:IN_ROWS, :] = jnp.zeros((IN_ROWS - IN_W, D), F32)
        for j, lo, hi, d in _in_segments():
            g[j, lo:hi, :] = d_ref[d:d + hi - lo, :]
        for j in range(N_DEV):
            gb[j] = g[j].astype(BF16)
        copies = _stage1_copies(gb, r1, send_sems, recv_sems)
        for cp in copies:
            cp.start()
        own_ref[0] = g[2 * chip + c]
        for cp in copies:
            cp.wait_recv()
        sib_ref[0] = r1[chip]
        for k in range(1, 4):
            t = chip ^ k
            pb_ref[k - 1] = (g[2 * t + c] + r1[t].astype(F32)).astype(BF16)
        for cp in copies:
            cp.wait_send()

    vm = pl.BlockSpec(memory_space=pltpu.VMEM)
    return pl.pallas_call(
        body,
        name="w_in_core_reduce",
        in_specs=[vm],
        out_specs=[vm, vm, vm],
        out_shape=[jax.ShapeDtypeStruct((1, IN_ROWS, D), F32), jax.ShapeDtypeStruct((1, IN_ROWS, D), BF16),
                   jax.ShapeDtypeStruct((3, IN_ROWS, D), BF16)],
        scratch_shapes=[pltpu.VMEM((N_DEV, IN_ROWS, D), F32), pltpu.VMEM((N_DEV, IN_ROWS, D), BF16),
                        pltpu.VMEM((4, IN_ROWS, D), BF16), pltpu.SemaphoreType.DMA((4,)),
                        pltpu.SemaphoreType.DMA((4,))],
        compiler_params=_params(),
    )(dw_t)


def _stage1_copies(g_ref, r_ref, send_sems, recv_sems):
    x, y, c = _position()
    return [pltpu.make_async_remote_copy(
        src_ref=g_ref.at[2 * i + 1 - c], dst_ref=r_ref.at[i], send_sem=send_sems.at[i], recv_sem=recv_sems.at[i],
        device_id=(x, y, 1 - c), device_id_type=MESH) for i in range(4)]


def _ffn_core_reduce(dw3, dwb3, dw_o, dwb_o, pos_arr):
    def body(pos_ref, g0, g1, g2, go, gb3_hbm, gbo_hbm, p0, p1, p2, po, s0, s1, s2, so,
             r1f, r1o, send_sems, recv_sems):
        k = pl.program_id(0)
        x, y, c = _position()
        chip = 2 * x + y

        def copies(p):
            src = 2 * (chip ^ ((p + 1) & 3)) + 1 - c
            pairs = [(gb3_hbm.at[a, src], r1f.at[a, p]) for a in range(3)] + [(gbo_hbm.at[src], r1o.at[p])]
            return [pltpu.make_async_remote_copy(
                src_ref=s, dst_ref=d, send_sem=send_sems.at[4 * p + a], recv_sem=recv_sems.at[4 * p + a],
                device_id=(x, y, 1 - c), device_id_type=MESH) for a, (s, d) in enumerate(pairs)]

        @pl.when(k == 0)
        def _():
            for p in range(4):
                for cp in copies(p):
                    cp.start()

        for p in range(3):
            @pl.when(k == p)
            def _():
                for cp in copies(p):
                    cp.wait_recv()

        for a, (g, pb) in enumerate(((g0, p0), (g1, p1), (g2, p2))):
            pb[...] = (g[...] + r1f[a, k][None].astype(F32)).astype(BF16)
        po[...] = (go[...] + r1o[k][None].astype(F32)).astype(BF16)

        @pl.when(k == 2)
        def _():
            for cp in copies(3):
                cp.wait_recv()
            for a, s in enumerate((s0, s1, s2)):
                s[0] = r1f[a, 3]
            so[0] = r1o[3]
            for p in range(4):
                for cp in copies(p):
                    cp.wait_send()

    other = lambda k, pos: 2 * (pos[1] ^ (k + 1)) + pos[0]
    g_spec = lambda lead: pl.BlockSpec((None, 1, FF_W, D), lambda k, pos: (lead, other(k, pos), 0, 0))
    slot = lambda rows: pl.BlockSpec((1, rows, D), lambda k, pos: (k, 0, 0))
    one = lambda rows: pl.BlockSpec((1, rows, D), lambda k, pos: (0, 0, 0))
    hbm = pl.BlockSpec(memory_space=pl.ANY)
    return pl.pallas_call(
        body,
        name="ffn_core_reduce",
        grid_spec=pltpu.PrefetchScalarGridSpec(
            num_scalar_prefetch=1, grid=(3,),
            in_specs=[g_spec(0), g_spec(1), g_spec(2),
                      pl.BlockSpec((1, OUT_ROWS, D), lambda k, pos: (other(k, pos), 0, 0)), hbm, hbm],
            out_specs=[slot(FF_W), slot(FF_W), slot(FF_W), slot(OUT_ROWS),
                       one(FF_W), one(FF_W), one(FF_W), one(OUT_ROWS)],
            scratch_shapes=[pltpu.VMEM((3, 4, FF_W, D), BF16), pltpu.VMEM((4, OUT_ROWS, D), BF16),
                            pltpu.SemaphoreType.DMA((16,)), pltpu.SemaphoreType.DMA((16,))]),
        out_shape=[jax.ShapeDtypeStruct((3, FF_W, D), BF16)] * 3 + [jax.ShapeDtypeStruct((3, OUT_ROWS, D), BF16)]
        + [jax.ShapeDtypeStruct((1, FF_W, D), BF16)] * 3 + [jax.ShapeDtypeStruct((1, OUT_ROWS, D), BF16)],
        compiler_params=_params(("arbitrary",)),
    )(pos_arr, dw3, dw3, dw3, dw_o, dwb3, dwb_o)


def _stage2_copies(p_refs, r_refs, send_sems, recv_sems):
    x, y, c = _position()
    copies = []
    for a in range(len(p_refs)):
        for k in range(1, 4):
            copies.append(pltpu.make_async_remote_copy(
                src_ref=p_refs[a].at[k - 1], dst_ref=r_refs[a].at[k - 1],
                send_sem=send_sems.at[3 * a + k - 1], recv_sem=recv_sems.at[3 * a + k - 1],
                device_id=(x ^ (k >> 1), y ^ (k & 1), c), device_id_type=MESH))
    return copies


def _finish_weights(items, pos_arr, name, nblk):
    n = len(items)
    in_specs, out_specs, out_shape, operands, wbs = [], [], [], [], []
    for g8, lead, r1, r2, w, m, v in items:
        rows, wr = g8.shape[-2], w.shape[0]
        assert rows % nblk == 0 and wr % nblk == 0 and (nblk == 1 or (rows == wr and rows % (16 * nblk) == 0))
        rb, wb = rows // nblk, wr // nblk
        if lead is not None:
            g_spec = pl.BlockSpec((None, 1, rb, D), lambda i, pos, lead=lead: (lead, 2 * pos[1] + pos[0], i, 0))
        elif g8.shape[0] == 1:
            g_spec = pl.BlockSpec((1, rb, D), lambda i, pos: (0, i, 0))
        else:
            g_spec = pl.BlockSpec((1, rb, D), lambda i, pos: (2 * pos[1] + pos[0], i, 0))
        r1_spec = pl.BlockSpec((1, rb, D), lambda i, pos: (0, i, 0))
        wblk = pl.BlockSpec((wb, D), lambda i, pos: (i, 0))
        in_specs += [g_spec, r1_spec, pl.BlockSpec((3, rb, D), lambda i, pos: (0, i, 0)), wblk, wblk, wblk]
        out_specs += [wblk] * 4
        out_shape += [jax.ShapeDtypeStruct(w.shape, F32)] * 4
        operands += [g8, r1, r2, w, m, v]
        wbs.append(wb)

    def body(pos_ref, *refs):
        for a in range(n):
            g_ref, r1_ref, r2_ref, w_ref, m_ref, v_ref = refs[6 * a:6 * a + 6]
            g_out, d_out, m_out, v_out = refs[6 * n + 4 * a:6 * n + 4 * a + 4]
            g = g_ref[0] + r1_ref[0].astype(F32)
            for k in range(3):
                g = g + r2_ref[k].astype(F32)
            g = g[0:wbs[a], :]
            g_out[...] = g
            d, mn, vn = _adamw_math(w_ref[...], g, m_ref[...], v_ref[...])
            d_out[...] = d
            m_out[...] = mn
            v_out[...] = vn

    return pl.pallas_call(
        body,
        name=name,
        grid_spec=pltpu.PrefetchScalarGridSpec(
            num_scalar_prefetch=1, grid=(nblk,), in_specs=in_specs, out_specs=out_specs),
        out_shape=out_shape,
        compiler_params=_params(("arbitrary",)),
    )(pos_arr, *operands)


SMALL_NAMES = ("norm1_g", "norm2_g", "norm_f_g", "b_gate", "gla_norm_g", "w_gate_up", "conv_w")
WGU_W = NQK // N_DEV
CONV_W = CW // N_DEV


def _small_adamw(sums, ws, ms, vs):
    n = len(SMALL_NAMES)

    def body(*refs):
        acc_ref = refs[0]
        w_refs, m_refs, v_refs = refs[1:1 + n], refs[1 + n:1 + 2 * n], refs[1 + 2 * n:1 + 3 * n]
        loss_ref = refs[1 + 3 * n]
        outs = refs[2 + 3 * n:]
        x, y, c = _position()
        me = 4 * x + 2 * y + c
        acc = acc_ref[...]
        loss_ref[...] = acc[3:4, NQK + DV:NQK + DV + 1]

        def my_columns(full, width):
            r = lax.broadcasted_iota(jnp.int32, (full.shape[1], width), 0)
            col = lax.broadcasted_iota(jnp.int32, (full.shape[1], width), 1)
            sel = (r == width * me + col).astype(F32)
            return _mm(full, sel, precision=HIGHEST)

        dwgu = jnp.concatenate([acc[row:row + 1, lane:lane + NQK] for row, lane in map(_wgu_slot, range(RANK))], axis=0)
        dcw = jnp.concatenate([acc[row:row + 1, lane:lane + CW] for row, lane in CONV_SLOTS], axis=0)
        grads = [acc[0:1, :], acc[1:2, :], acc[2:3, :], acc[3:4, 0:NQK], acc[3:4, NQK:NQK + DV],
                 my_columns(dwgu, WGU_W), my_columns(dcw, CONV_W)]
        for i, g in enumerate(grads):
            d, mn, vn = _adamw_math(w_refs[i][...], g, m_refs[i][...], v_refs[i][...])
            outs[4 * i][...] = g
            outs[4 * i + 1][...] = d
            outs[4 * i + 2][...] = mn
            outs[4 * i + 3][...] = vn

    vm = pl.BlockSpec(memory_space=pltpu.VMEM)
    out_shape = [jax.ShapeDtypeStruct((1, 1), F32)]
    for w in ws:
        out_shape += [jax.ShapeDtypeStruct(w.shape, F32)] * 4
    return pl.pallas_call(
        body,
        name="small_adamw",
        in_specs=[vm] * (1 + 3 * n),
        out_specs=[vm] * (1 + 4 * n),
        out_shape=out_shape,
        compiler_params=_params(),
    )(sums, *ws, *ms, *vs)


def kernel(x, norm1_g, w_in, w_gate_up, b_gate, gla_norm_g, conv_w, w_out, norm2_g, w_ffn_gate, w_ffn_up, w_ffn_down, norm_f_g, loss_target, m_norm1_g, m_w_in, m_w_gate_up, m_b_gate, m_gla_norm_g, m_conv_w, m_w_out, m_norm2_g, m_w_ffn_gate, m_w_ffn_up, m_w_ffn_down, m_norm_f_g, v_norm1_g, v_w_in, v_w_gate_up, v_b_gate, v_gla_norm_g, v_conv_w, v_w_out, v_norm2_g, v_w_ffn_gate, v_w_ffn_up, v_w_ffn_down, v_norm_f_g):
    xi, yi, ci = _position()
    pos_arr = jnp.stack([ci, 2 * xi + yi]).astype(jnp.int32)
    nb, s, _ = x.shape
    t = nb * s

    tr = lambda a: a[0].T
    w_in_t, gwgu, gconv, stage = _gather_w_in(tr(w_in), tr(w_ffn_gate), tr(w_ffn_up), w_ffn_down[0], w_out[0],
                                              w_gate_up[0], conv_w[0])
    wgu_f = gwgu.transpose(1, 0, 2).reshape(RANK, NQK)
    conv_f = gconv.transpose(1, 0, 2).reshape(CONV_K, CW)
    wgu_p = jnp.concatenate([wgu_f, jnp.zeros((A_PAD - RANK, NQK), F32)], axis=0).astype(BF16)

    x2d = x.reshape(t, D)
    tgt2d = loss_target.reshape(t, D)
    tm = 256
    tm_in = min(512, t)
    tk = min(2048, t)
    proj, z, h, gwb = _in_proj_fwd(x2d, norm1_g, w_in_t, wgu_p, b_gate, tm_in, stage)
    proj3 = proj.reshape(nb, s, PW)
    z3 = z.reshape(nb, s, NQK)
    mix3, opre3, sprev, gwa = _mix_fwd(proj3, z3, gla_norm_g, conv_f, stage)
    mix2d = mix3.reshape(t, D)
    dx1, dx1b, dmix, adu, hb, dg2, dgf, loss_part = _ffn_fwd_bwd(
        mix2d, x2d, tgt2d, gwa, gwb, norm2_g, norm_f_g.reshape(1, D), tm)
    dw3, dwb3 = _dw_ffn(adu, hb, tk)
    dw3 = dw3.reshape(3, N_DEV, FF_W, D)
    dw_o, dwb_o = _tn_matmul(mix2d, dx1b, D // 2, D, tk, "dw_out", True)
    dw_o = dw_o.reshape(N_DEV, OUT_ROWS, D)
    *pb, sib_d, sib_g, sib_u, sib_o = _ffn_core_reduce(
        dw3, dwb3.reshape(3, N_DEV, FF_W, D), dw_o, dwb_o.reshape(N_DEV, OUT_ROWS, D), pos_arr)
    g8 = [dw3, dw3, dw3, dw_o]
    leads = [0, 1, 2, None]
    tags = ("w_ffn_down", "w_ffn_gate", "w_ffn_up", "w_out")
    r1 = [sib_d, sib_g, sib_u, sib_o]
    mb = _mix_bwd(proj3, z3, sprev, opre3, dmix.reshape(nb, s, D), gla_norm_g, conv_f, wgu_p, [pb[0], pb[1], pb[3]])
    dproj3, dgng, dcw, dbg, dwgu = mb[:5]
    dproj2d = dproj3.reshape(t, PW)
    dw_in_t, r2_up = _tn_matmul(dproj2d, h, PW // 5, D, tk, "dw_in", False, _stage2_rider([pb[2]]))
    r2 = [mb[5], mb[6], r2_up, mb[7]]
    g_in, r1_in, pb_in = _w_in_core_reduce(dw_in_t)
    dx, small_sums, r2_in = _in_proj_bwd(dproj2d, x2d, dx1, norm1_g, w_in_t, tm_in, pb_in,
                                         (dg2, dgf, dbg, dgng, dwgu, dcw, loss_part))

    tags = ("w_in",) + tags
    g8 = [g_in] + g8
    leads = [None] + leads
    r1 = [r1_in] + list(r1)
    r2 = [r2_in] + r2
    shard_w = (tr(w_in), w_ffn_down[0], tr(w_ffn_gate), tr(w_ffn_up), w_out[0])
    shard_m = (tr(m_w_in), m_w_ffn_down[0], tr(m_w_ffn_gate), tr(m_w_ffn_up), m_w_out[0])
    shard_v = (tr(v_w_in), v_w_ffn_down[0], tr(v_w_ffn_gate), tr(v_w_ffn_up), v_w_out[0])
    transposed = (True, False, True, True, False)
    items = list(zip(g8, leads, r1, r2, shard_w, shard_m, shard_v))
    flat = list(_finish_weights(items[1:], pos_arr, "finish_ffn_out", 2))
    flat = list(_finish_weights(items[:1], pos_arr, "finish_w_in", 1)) + flat
    results = {}
    for i, (tag, tp) in enumerate(zip(tags, transposed)):
        results[tag] = [o.T[None] if tp else o[None] for o in flat[4 * i:4 * i + 4]]

    small_w = (norm1_g, norm2_g, norm_f_g.reshape(1, D), b_gate, gla_norm_g, w_gate_up[0], conv_w[0])
    small_m = (m_norm1_g, m_norm2_g, m_norm_f_g.reshape(1, D), m_b_gate, m_gla_norm_g, m_w_gate_up[0], m_conv_w[0])
    small_v = (v_norm1_g, v_norm2_g, v_norm_f_g.reshape(1, D), v_b_gate, v_gla_norm_g, v_w_gate_up[0], v_conv_w[0])
    so = _small_adamw(small_sums, small_w, small_m, small_v)
    loss = so[0].reshape(())
    shapes = {"norm_f_g": (D,), "w_gate_up": (1, RANK, WGU_W), "conv_w": (1, CONV_K, CONV_W)}
    for i, name in enumerate(SMALL_NAMES):
        results[name] = [o.reshape(shapes[name]) if name in shapes else o for o in so[1 + 4 * i:5 + 4 * i]]

    names = ("norm1_g", "w_in", "w_gate_up", "b_gate", "gla_norm_g", "conv_w", "w_out", "norm2_g",
             "w_ffn_gate", "w_ffn_up", "w_ffn_down", "norm_f_g")
    outs = [loss, dx.reshape(nb, s, D)]
    for kind in range(4):
        for name in names:
            outs.append(results[name][kind])
    return tuple(outs)
```

```python
import jax
import jax.numpy as jnp
from jax import lax
from jax.experimental import pallas as pl
from jax.experimental.pallas import tpu as pltpu

F32 = jnp.float32
BF16 = jnp.bfloat16
HIGHEST = lax.Precision.HIGHEST
MESH = pl.DeviceIdType.MESH

N_DEV = 8
D = 1024
DFF = 2816
HEADS = 4
DK = 64
DV = 128
NQK = HEADS * DK
NV = HEADS * DV
RANK = 16
CHUNK = 64
CW = 512
CONV_K = 3
IN_COLS = 3088
EPS = 1e-6
INV_GATE_NORM = 1.0 / 16.0
Q_SCALE = DK ** -0.5

PW = 3200
OQ, OK_, OV, OG, OCB, OCC, OCH, OA = 0, 256, 512, 1024, 1536, 2048, 2560, 3072
A_PAD = 128

ADAM_LR = 0.001
ADAM_B1 = 0.9
ADAM_B2 = 0.999
ADAM_EPS = 1e-08
ADAM_WD = 0.01
ADAM_STEP = 10

IN_W = IN_COLS // N_DEV
IN_ROWS = 400
FF_W = DFF // N_DEV
OUT_ROWS = D // N_DEV
SLAB_IN = 0
SLAB_G = SLAB_IN + IN_ROWS
SLAB_U = SLAB_G + FF_W
SLAB_D = SLAB_U + FF_W
SLAB_O = SLAB_D + FF_W
SLAB_ROWS = SLAB_O + OUT_ROWS

VMEM_LIMIT = 56 * 1024 * 1024


def _params(sem=None, vmem=VMEM_LIMIT):
    return pltpu.CompilerParams(dimension_semantics=sem, vmem_limit_bytes=vmem)


def _nt(a, b):
    return lax.dot_general(a, b, (((1,), (1,)), ((), ())), preferred_element_type=F32)


def _tn(a, b, precision=None):
    return lax.dot_general(a, b, (((0,), (0,)), ((), ())), preferred_element_type=F32, precision=precision)


def _mm(a, b, precision=None):
    return jnp.dot(a, b, preferred_element_type=F32, precision=precision)


def _in_segments():
    segs = []
    for j in range(N_DEV):
        lo, hi = IN_W * j, IN_W * (j + 1)
        cuts = sorted({lo, hi} | {c for c in (OCB, OCB + RANK) if lo < c < hi})
        for a, b in zip(cuts[:-1], cuts[1:]):
            if a < OCB:
                d = a
            elif a < OCB + RANK:
                d = OA + (a - OCB)
            else:
                d = a - RANK
            segs.append((j, a - lo, b - lo, d))
    return segs


def _in_proj_fwd(x2d, g1, w_in_t, wgu_p, b_gate, tm, stage):
    t = x2d.shape[0]
    nt = t // tm
    g_rows = SLAB_ROWS - SLAB_D

    def body(x_ref, g_ref, w_ref, wgu_ref, bg_ref, stage_hbm, proj_ref, z_ref, h_ref, gwb_ref,
             send_sems, recv_sems, local_sem):
        gargs = (stage_hbm, SLAB_D, g_rows, gwb_ref, send_sems, recv_sems, local_sem)

        @pl.when(pl.program_id(0) == 0)
        def _():
            _gather_start(*gargs)

        x = x_ref[...]
        r = lax.rsqrt(jnp.mean(x * x, axis=-1, keepdims=True) + EPS)
        h = ((x * r) * g_ref[...]).astype(BF16)
        h_ref[...] = h
        proj = _nt(h, w_ref[...])
        proj_ref[...] = proj
        pa = proj[:, OA:OA + A_PAD].astype(BF16)
        z_ref[...] = _mm(pa, wgu_ref[...]) + bg_ref[...]

        @pl.when(pl.program_id(0) == nt - 1)
        def _():
            _gather_finish(*gargs)

    return pl.pallas_call(
        body,
        name="in_proj_fwd",
        grid=(t // tm,),
        in_specs=[
            pl.BlockSpec((tm, D), lambda i: (i, 0)),
            pl.BlockSpec((1, D), lambda i: (0, 0)),
            pl.BlockSpec((PW, D), lambda i: (0, 0)),
            pl.BlockSpec((A_PAD, NQK), lambda i: (0, 0)),
            pl.BlockSpec((1, NQK), lambda i: (0, 0)),
            pl.BlockSpec(memory_space=pl.ANY),
        ],
        out_specs=[
            pl.BlockSpec((tm, PW), lambda i: (i, 0)),
            pl.BlockSpec((tm, NQK), lambda i: (i, 0)),
            pl.BlockSpec((tm, D), lambda i: (i, 0)),
            pl.BlockSpec(memory_space=pl.ANY),
        ],
        out_shape=[
            jax.ShapeDtypeStruct((t, PW), F32),
            jax.ShapeDtypeStruct((t, NQK), F32),
            jax.ShapeDtypeStruct((t, D), BF16),
            jax.ShapeDtypeStruct((N_DEV, g_rows, D), BF16),
        ],
        scratch_shapes=_gather_sems(),
        compiler_params=_params(("arbitrary",)),
    )(x2d, g1, w_in_t, wgu_p, b_gate, stage)


def _head_masks():
    lane = lax.broadcasted_iota(jnp.int32, (1, NQK), 1)
    return [(lane >= DK * h) & (lane < DK * (h + 1)) for h in range(HEADS)]


def _split_bf16(x, n):
    parts = []
    for _ in range(n):
        p = x.astype(BF16)
        parts.append(p)
        x = x - p.astype(F32)
    return parts


def _chunk_fwd_parts(q, k, z, tril16):
    la = (jnp.minimum(z, 0.0) - jnp.log1p(jnp.exp(-jnp.abs(z)))) * INV_GATE_NORM
    la_parts = _split_bf16(la, 3)
    bc = _mm(tril16, la_parts[0]) + _mm(tril16, la_parts[1]) + _mm(tril16, la_parts[2])
    bl = bc[CHUNK - 1:CHUNK, :]
    eb = jnp.exp(bc)
    enb = jnp.exp(-bc)
    ekl = jnp.exp(bl - bc)
    qi = (q * Q_SCALE) * eb
    ki = k * enb
    ks = k * ekl
    ones16 = jnp.ones((CHUNK, DV), BF16)
    decb = jnp.exp(_tn(la_parts[0], ones16) + _tn(la_parts[1], ones16) + _tn(la_parts[2], ones16))
    return la, eb, enb, ekl, qi, ki, ks, decb


def _stack_heads(a, masks):
    return jnp.concatenate([jnp.where(m, a, 0.0) for m in masks], axis=0)


def _merge_heads(blocks, masks):
    out = blocks[HEADS - 1]
    for h in range(HEADS - 2, -1, -1):
        out = jnp.where(masks[h], blocks[h], out)
    return out


def _causal_stack_mask():
    row = lax.broadcasted_iota(jnp.int32, (HEADS * CHUNK, CHUNK), 0)
    col = lax.broadcasted_iota(jnp.int32, (HEADS * CHUNK, CHUNK), 1)
    return (row & (CHUNK - 1)) >= col


def _conv_taps(u, uprev):
    row = lax.broadcasted_iota(jnp.int32, u.shape, 0)
    u1 = jnp.where(row < 1, pltpu.roll(uprev, 1, 0), pltpu.roll(u, 1, 0))
    u2 = jnp.where(row < 2, pltpu.roll(uprev, 2, 0), pltpu.roll(u, 2, 0))
    return u1, u2


def _mix_fwd(proj3, z3, gng, conv_w, stage):
    nb, s, _ = proj3.shape
    nc = s // CHUNK
    g_rows = SLAB_D - SLAB_G

    def body(p_ref, z_ref, gng_ref, cw_ref, stage_hbm, mix_ref, o_ref, sprev_ref, gwa_ref, s_ref, uprev_ref,
             send_sems, recv_sems, local_sem):
        n = pl.program_id(0)
        gargs = (stage_hbm, SLAB_G, g_rows, gwa_ref, send_sems, recv_sems, local_sem)

        @pl.when(n == 0)
        def _():
            _gather_start(*gargs)
            s_ref[...] = jnp.zeros_like(s_ref)
            uprev_ref[...] = jnp.zeros_like(uprev_ref)

        r_i = lax.broadcasted_iota(jnp.int32, (CHUNK, CHUNK), 0)
        c_i = lax.broadcasted_iota(jnp.int32, (CHUNK, CHUNK), 1)
        tril16 = (r_i >= c_i).astype(BF16)
        masks = _head_masks()
        cmask = _causal_stack_mask()
        gg = gng_ref[...]
        for b in range(nb):
            q = p_ref[b, :, OQ:OQ + NQK]
            k = p_ref[b, :, OK_:OK_ + NQK]
            _, _, _, _, qi, ki, ks, decb = _chunk_fwd_parts(q, k, z_ref[b], tril16)
            qs = _stack_heads(qi, masks).astype(BF16)
            sc = jnp.where(cmask, _nt(qs, ki.astype(BF16)), 0.0).astype(BF16)
            st = s_ref[b]
            sprev_ref[b, 0] = st
            o_inter = _mm(qs, st.astype(BF16))
            v16 = p_ref[b, :, OV:OV + NV].astype(BF16)
            kv = _tn(ks.astype(BF16), v16)
            for h in range(HEADS):
                rows = slice(CHUNK * h, CHUNK * (h + 1))
                cols = slice(DV * h, DV * (h + 1))
                o = _mm(sc[rows], v16[:, cols]) + o_inter[rows]
                o_ref[b, :, cols] = o
                r = lax.rsqrt(jnp.mean(o * o, axis=-1, keepdims=True) + EPS)
                on = (o * r) * gg
                g = p_ref[b, :, OG + DV * h:OG + DV * (h + 1)]
                mix_ref[b, :, cols] = (on * (g * jax.nn.sigmoid(g))).astype(BF16)
                s_ref[b, rows, :] = decb[rows] * st[rows] + kv[rows, cols]
            u = p_ref[b, :, OCC:OCC + CW] * p_ref[b, :, OCH:OCH + CW]
            u1, u2 = _conv_taps(u, uprev_ref[b])
            yc = cw_ref[0:1, :] * u2 + cw_ref[1:2, :] * u1 + cw_ref[2:3, :] * u
            mix_ref[b, :, NV:NV + CW] = (p_ref[b, :, OCB:OCB + CW] * yc).astype(BF16)
            uprev_ref[b] = u

        @pl.when(n == nc - 1)
        def _():
            _gather_finish(*gargs)

    return pl.pallas_call(
        body,
        name="mix_fwd",
        grid=(nc,),
        in_specs=[
            pl.BlockSpec((nb, CHUNK, PW), lambda n: (0, n, 0)),
            pl.BlockSpec((nb, CHUNK, NQK), lambda n: (0, n, 0)),
            pl.BlockSpec((1, DV), lambda n: (0, 0)),
            pl.BlockSpec((CONV_K, CW), lambda n: (0, 0)),
            pl.BlockSpec(memory_space=pl.ANY),
        ],
        out_specs=[
            pl.BlockSpec((nb, CHUNK, D), lambda n: (0, n, 0)),
            pl.BlockSpec((nb, CHUNK, NV), lambda n: (0, n, 0)),
            pl.BlockSpec((nb, 1, NQK, DV), lambda n: (0, n, 0, 0)),
            pl.BlockSpec(memory_space=pl.ANY),
        ],
        out_shape=[
            jax.ShapeDtypeStruct((nb, s, D), BF16),
            jax.ShapeDtypeStruct((nb, s, NV), F32),
            jax.ShapeDtypeStruct((nb, nc, NQK, DV), F32),
            jax.ShapeDtypeStruct((N_DEV, g_rows, D), BF16),
        ],
        scratch_shapes=[pltpu.VMEM((nb, NQK, DV), F32), pltpu.VMEM((nb, CHUNK, CW), F32)] + _gather_sems(),
        compiler_params=_params(("arbitrary",)),
    )(proj3, z3, gng, conv_w, stage)


def _ffn_fwd_bwd(mix2d, x2d, tgt2d, gwa, gwb, g2, gf, tm):
    t = x2d.shape[0]

    def body(mix_ref, x_ref, tgt_ref, g2_ref, gf_ref, gwa_hbm, gwb_hbm,
             dx1_ref, dx1b_ref, dmix_ref, adu_ref, hb_ref, dg2_ref, dgf_ref, loss_ref,
             wo, wg, wu, wd, wsem):
        i = pl.program_id(0)

        def weight_copies(n, dst, src, off, rows):
            return [pltpu.make_async_copy(src.at[j, pl.ds(off, rows), :], dst.at[pl.ds(rows * j, rows), :],
                                          wsem.at[N_DEV * n + j]) for j in range(N_DEV)]

        loads = (weight_copies(0, wo, gwb_hbm, FF_W, OUT_ROWS), weight_copies(1, wg, gwa_hbm, 0, FF_W),
                 weight_copies(2, wu, gwa_hbm, FF_W, FF_W), weight_copies(3, wd, gwb_hbm, 0, FF_W))

        @pl.when(i == 0)
        def _():
            for group in loads:
                for cp in group:
                    cp.start()
            dg2_ref[...] = jnp.zeros_like(dg2_ref)
            dgf_ref[...] = jnp.zeros_like(dgf_ref)
            loss_ref[...] = jnp.zeros_like(loss_ref)
            for group in loads:
                for cp in group:
                    cp.wait()

        g2v = g2_ref[...]
        gfv = gf_ref[...]
        x1 = x_ref[...] + _mm(mix_ref[...], wo[...])
        r2 = lax.rsqrt(jnp.mean(x1 * x1, axis=-1, keepdims=True) + EPS)
        n2 = x1 * r2
        h2 = (n2 * g2v).astype(BF16)
        hb_ref[1] = h2
        gate = _nt(h2, wg[...])
        up = _nt(h2, wu[...])
        sg = jax.nn.sigmoid(gate)
        sil = gate * sg
        act = (sil * up).astype(BF16)
        adu_ref[0] = act
        x2 = x1 + _mm(act, wd[...])
        rf = lax.rsqrt(jnp.mean(x2 * x2, axis=-1, keepdims=True) + EPS)
        nf = x2 * rf
        err = nf * gfv - tgt_ref[...]
        loss_ref[...] += 0.5 * jnp.sum(jnp.mean(err * err, axis=-1, keepdims=True))
        dy = err * (1.0 / D)
        dgf_ref[...] += jnp.sum(dy * nf, axis=0, keepdims=True)
        dnf = dy * gfv
        dx2 = rf * (dnf - nf * jnp.mean(dnf * nf, axis=-1, keepdims=True))
        dx2b = dx2.astype(BF16)
        hb_ref[0] = dx2b
        dact = _nt(dx2b, wd[...])
        dup = (dact * sil).astype(BF16)
        dgate = ((dact * up) * (sg * (1.0 + gate * (1.0 - sg)))).astype(BF16)
        adu_ref[2] = dup
        adu_ref[1] = dgate
        dh2 = _mm(dgate, wg[...]) + _mm(dup, wu[...])
        dg2_ref[...] += jnp.sum(dh2 * n2, axis=0, keepdims=True)
        dn2 = dh2 * g2v
        dx1 = dx2 + r2 * (dn2 - n2 * jnp.mean(dn2 * n2, axis=-1, keepdims=True))
        dx1_ref[...] = dx1
        dx1b = dx1.astype(BF16)
        dx1b_ref[...] = dx1b
        dmix_ref[...] = _nt(dx1b, wo[...])

    tile = lambda w: pl.BlockSpec((tm, w), lambda i: (i, 0))
    vec = pl.BlockSpec((1, D), lambda i: (0, 0))
    hbm = pl.BlockSpec(memory_space=pl.ANY)
    return pl.pallas_call(
        body,
        name="ffn_fwd_bwd",
        grid=(t // tm,),
        in_specs=[tile(D), tile(D), tile(D), vec, vec, hbm, hbm],
        out_specs=[tile(D), tile(D), tile(D), pl.BlockSpec((3, tm, DFF), lambda i: (0, i, 0)),
                   pl.BlockSpec((2, tm, D), lambda i: (0, i, 0)), vec, vec,
                   pl.BlockSpec((1, 128), lambda i: (0, 0))],
        out_shape=[
            jax.ShapeDtypeStruct((t, D), F32),
            jax.ShapeDtypeStruct((t, D), BF16),
            jax.ShapeDtypeStruct((t, D), F32),
            jax.ShapeDtypeStruct((3, t, DFF), BF16),
            jax.ShapeDtypeStruct((2, t, D), BF16),
            jax.ShapeDtypeStruct((1, D), F32),
            jax.ShapeDtypeStruct((1, D), F32),
            jax.ShapeDtypeStruct((1, 128), F32),
        ],
        scratch_shapes=[pltpu.VMEM((D, D), BF16), pltpu.VMEM((DFF, D), BF16), pltpu.VMEM((DFF, D), BF16),
                        pltpu.VMEM((DFF, D), BF16), pltpu.SemaphoreType.DMA((4 * N_DEV,))],
        compiler_params=_params(("arbitrary",)),
    )(mix2d, x2d, tgt2d, g2, gf, gwa, gwb)


def _stage2_rider(pbs):
    return dict(inputs=list(pbs), out_shape=[jax.ShapeDtypeStruct(p.shape, BF16) for p in pbs], nsem=3 * len(pbs),
                copies=_stage2_copies)


def _tn_matmul(a, b, bm, bn, tk, name, with_bf16, rider=None):
    t, m = a.shape
    n = b.shape[1]
    nk = t // tk
    nout = 2 if with_bf16 else 1
    grid = (m // bm, n // bn, nk)
    r_in = [] if rider is None else rider["inputs"]
    r_out = [] if rider is None else rider["out_shape"]

    def body(a_ref, b_ref, *rest):
        ins, outs = rest[:len(r_in)], rest[len(r_in):len(r_in) + nout]
        r_outs, sems = rest[len(r_in) + nout:len(r_in) + nout + len(r_out)], rest[len(r_in) + nout + len(r_out):]
        o_ref = outs[0]
        i, j, k = pl.program_id(0), pl.program_id(1), pl.program_id(2)
        if rider is not None:
            @pl.when((i == 0) & (j == 0) & (k == 0))
            def _():
                for cp in rider["copies"](ins, r_outs, *sems):
                    cp.start()

        @pl.when(k == 0)
        def _():
            o_ref[...] = jnp.zeros_like(o_ref)

        o_ref[...] += _tn(a_ref[...].astype(BF16), b_ref[...].astype(BF16))
        if with_bf16:
            @pl.when(k == nk - 1)
            def _():
                outs[1][...] = o_ref[...].astype(BF16)
        if rider is not None:
            @pl.when((i == grid[0] - 1) & (j == grid[1] - 1) & (k == nk - 1))
            def _():
                copies = rider["copies"](ins, r_outs, *sems)
                for cp in copies:
                    cp.wait_recv()
                for cp in copies:
                    cp.wait_send()

    out_blk = pl.BlockSpec((bm, bn), lambda i, j, k: (i, j))
    hbm = pl.BlockSpec(memory_space=pl.ANY)
    out_shape = [jax.ShapeDtypeStruct((m, n), F32)] + ([jax.ShapeDtypeStruct((m, n), BF16)] if with_bf16 else [])
    res = pl.pallas_call(
        body,
        name=name,
        grid=grid,
        in_specs=[pl.BlockSpec((tk, bm), lambda i, j, k: (k, i)), pl.BlockSpec((tk, bn), lambda i, j, k: (k, j))]
        + [hbm] * len(r_in),
        out_specs=[out_blk] * nout + [hbm] * len(r_out),
        out_shape=out_shape + list(r_out),
        scratch_shapes=([] if rider is None else
                        [pltpu.SemaphoreType.DMA((rider["nsem"],)), pltpu.SemaphoreType.DMA((rider["nsem"],))]),
        compiler_params=_params(("parallel", "parallel", "arbitrary") if rider is None
                                else ("arbitrary", "arbitrary", "arbitrary")),
    )(a, b, *r_in)
    return res[0] if len(res) == 1 else res


def _dw_ffn(adu, hb, tk):
    _, t, _ = adu.shape
    bm = DFF // 2
    nk = t // tk

    def body(a_ref, b_ref, o_ref, ob_ref):
        k = pl.program_id(2)

        @pl.when(k == 0)
        def _():
            o_ref[...] = jnp.zeros_like(o_ref)

        o_ref[...] += _tn(a_ref[...], b_ref[...])

        @pl.when(k == nk - 1)
        def _():
            ob_ref[...] = o_ref[...].astype(BF16)

    out_blk = pl.BlockSpec((None, bm, D), lambda p, i, k: (p, i, 0))
    return pl.pallas_call(
        body,
        name="dw_ffn",
        grid=(3, DFF // bm, nk),
        in_specs=[pl.BlockSpec((None, tk, bm), lambda p, i, k: (p, k, i)),
                  pl.BlockSpec((None, tk, D), lambda p, i, k: (jnp.minimum(p, 1), k, 0))],
        out_specs=[out_blk, out_blk],
        out_shape=[jax.ShapeDtypeStruct((3, DFF, D), F32), jax.ShapeDtypeStruct((3, DFF, D), BF16)],
        compiler_params=_params(("arbitrary", "arbitrary", "arbitrary")),
    )(adu, hb)


def _mix_bwd(proj3, z3, sprev, opre3, dmix3, gng, conv_w, wgu_p, pbs):
    nb, s, _ = proj3.shape
    nc = s // CHUNK
    na = len(pbs)

    def body(*refs):
        (p_ref, pprev_ref, z_ref, sp_ref, o_ref, dm_ref, gng_ref, cw_ref, wgu_ref) = refs[:9]
        pb_refs = refs[9:9 + na]
        (dproj_ref, dgng_ref, dcw_ref, dbg_ref, dwgu_ref) = refs[9 + na:14 + na]
        r2_refs = refs[14 + na:14 + 2 * na]
        ds_ref, dycn_ref, send_sems, recv_sems = refs[14 + 2 * na:]
        step = pl.program_id(0)
        n = nc - 1 - step

        @pl.when(step == 0)
        def _():
            for cp in _stage2_copies(pb_refs, r2_refs, send_sems, recv_sems):
                cp.start()
            ds_ref[...] = jnp.zeros_like(ds_ref)
            dycn_ref[...] = jnp.zeros_like(dycn_ref)
            dgng_ref[...] = jnp.zeros_like(dgng_ref)
            dcw_ref[...] = jnp.zeros_like(dcw_ref)
            dbg_ref[...] = jnp.zeros_like(dbg_ref)
            dwgu_ref[...] = jnp.zeros_like(dwgu_ref)

        r_i = lax.broadcasted_iota(jnp.int32, (CHUNK, CHUNK), 0)
        c_i = lax.broadcasted_iota(jnp.int32, (CHUNK, CHUNK), 1)
        tril16 = (r_i >= c_i).astype(BF16)
        triu16 = (r_i <= c_i).astype(BF16)
        causal = r_i >= c_i
        masks = _head_masks()
        cmask = _causal_stack_mask()
        gg = gng_ref[...]
        last_row = lax.broadcasted_iota(jnp.int32, (CHUNK, NQK), 0) == CHUNK - 1
        ones_r = jnp.ones((16, DV), BF16)
        has_prev = (n > 0).astype(F32)
        for b in range(nb):
            q = p_ref[b, :, OQ:OQ + NQK]
            k = p_ref[b, :, OK_:OK_ + NQK]
            z = z_ref[b]
            _, eb, enb, ekl, qi, ki, ks, decb = _chunk_fwd_parts(q, k, z, tril16)
            qi16 = qi.astype(BF16)
            ki16 = ki.astype(BF16)
            qs = _stack_heads(qi, masks).astype(BF16)
            sc = jnp.where(cmask, _nt(qs, ki16), 0.0).astype(BF16)
            st = sp_ref[b, 0]
            st16 = st.astype(BF16)
            dsn = ds_ref[b]
            dsn16 = dsn.astype(BF16)
            v16 = p_ref[b, :, OV:OV + NV].astype(BF16)
            do16 = []
            dgng = jnp.zeros((1, DV), F32)
            for h in range(HEADS):
                cols = slice(DV * h, DV * (h + 1))
                o = o_ref[b, :, cols]
                r = lax.rsqrt(jnp.mean(o * o, axis=-1, keepdims=True) + EPS)
                nh = o * r
                g = p_ref[b, :, OG + DV * h:OG + DV * (h + 1)]
                sg = jax.nn.sigmoid(g)
                dog = dm_ref[b, :, cols]
                dproj_ref[b, :, OG + DV * h:OG + DV * (h + 1)] = (
                    (dog * (nh * gg)) * (sg * (1.0 + g * (1.0 - sg)))).astype(BF16)
                don = dog * (g * sg)
                dgng = dgng + jnp.sum(don * nh, axis=0, keepdims=True)
                dn = don * gg
                do = r * (dn - nh * jnp.mean(dn * nh, axis=-1, keepdims=True))
                do16.append(do.astype(BF16))
            dgng_ref[...] += dgng
            do_rows = jnp.concatenate(do16, axis=0)
            v_rows = jnp.concatenate([v16[:, DV * h:DV * (h + 1)] for h in range(HEADS)], axis=0)
            dp16 = [jnp.where(causal, _nt(do16[h], v16[:, DV * h:DV * (h + 1)]), 0.0).astype(BF16)
                    for h in range(HEADS)]
            ks_dsn = _mm(_stack_heads(ks, masks).astype(BF16), dsn16)
            do_st = _nt(do_rows, st16)
            v_dsn = _nt(v_rows, dsn16)
            dp_ki = _mm(jnp.concatenate(dp16, axis=0), ki16)
            q_do = _tn(qi16, jnp.concatenate(do16, axis=1))
            dki_h = []
            for h in range(HEADS):
                rows = slice(CHUNK * h, CHUNK * (h + 1))
                cols = slice(DV * h, DV * (h + 1))
                dv = _tn(sc[rows], do16[h]) + ks_dsn[rows]
                dproj_ref[b, :, OV + DV * h:OV + DV * (h + 1)] = dv.astype(BF16)
                dki_h.append(_tn(dp16[h], qi16))
                ds_ref[b, rows, :] = decb[rows] * dsn[rows] + q_do[rows, cols]
            blocks = lambda a: [a[CHUNK * h:CHUNK * (h + 1)] for h in range(HEADS)]
            dqi = _merge_heads(blocks(dp_ki + do_st), masks)
            dki = _merge_heads(dki_h, masks)
            dks = _merge_heads(blocks(v_dsn), masks)
            dproj_ref[b, :, OQ:OQ + NQK] = (dqi * (Q_SCALE * eb)).astype(BF16)
            dproj_ref[b, :, OK_:OK_ + NQK] = (dki * enb + dks * ekl).astype(BF16)
            dks_ks = dks * ks
            db = dqi * qi - dki * ki - dks_ks
            sd = _split_bf16(dsn * st * decb, 2)
            dbl = jnp.sum(dks_ks, axis=0, keepdims=True) + (_nt(ones_r, sd[0]) + _nt(ones_r, sd[1]))[0:1, :]
            db = db + jnp.where(last_row, dbl, 0.0)
            db_parts = _split_bf16(db, 3)
            dla = _mm(triu16, db_parts[0]) + _mm(triu16, db_parts[1]) + _mm(triu16, db_parts[2])
            dz = (dla * INV_GATE_NORM) * (1.0 / (1.0 + jnp.exp(z)))
            dbg_ref[...] += jnp.sum(dz, axis=0, keepdims=True)
            dz16 = dz.astype(BF16)
            pa16 = p_ref[b, :, OA:OA + A_PAD].astype(BF16)
            dwgu_ref[...] += _tn(pa16, dz16)
            dproj_ref[b, :, OA:OA + A_PAD] = _nt(dz16, wgu_ref[...]).astype(BF16)
            cb = p_ref[b, :, OCB:OCB + CW]
            cc = p_ref[b, :, OCC:OCC + CW]
            ch = p_ref[b, :, OCH:OCH + CW]
            u = cc * ch
            uprev = (pprev_ref[b, :, 0:CW] * pprev_ref[b, :, CW:2 * CW]) * has_prev
            u1, u2 = _conv_taps(u, uprev)
            w0 = cw_ref[0:1, :]
            w1 = cw_ref[1:2, :]
            w2 = cw_ref[2:3, :]
            yc = w0 * u2 + w1 * u1 + w2 * u
            doc = dm_ref[b, :, NV:NV + CW]
            dproj_ref[b, :, OCB:OCB + CW] = (doc * yc).astype(BF16)
            dyc = doc * cb
            dycn = dycn_ref[b]
            row = lax.broadcasted_iota(jnp.int32, dyc.shape, 0)
            d1 = jnp.where(row >= CHUNK - 1, pltpu.roll(dycn, CHUNK - 1, 0), pltpu.roll(dyc, CHUNK - 1, 0))
            d2 = jnp.where(row >= CHUNK - 2, pltpu.roll(dycn, CHUNK - 2, 0), pltpu.roll(dyc, CHUNK - 2, 0))
            du = w2 * dyc + w1 * d1 + w0 * d2
            dproj_ref[b, :, OCC:OCC + CW] = (du * ch).astype(BF16)
            dproj_ref[b, :, OCH:OCH + CW] = (du * cc).astype(BF16)
            dcw_ref[0:1, :] += jnp.sum(dyc * u2, axis=0, keepdims=True)
            dcw_ref[1:2, :] += jnp.sum(dyc * u1, axis=0, keepdims=True)
            dcw_ref[2:3, :] += jnp.sum(dyc * u, axis=0, keepdims=True)
            dycn_ref[b] = dyc

        @pl.when(step == nc - 1)
        def _():
            copies = _stage2_copies(pb_refs, r2_refs, send_sems, recv_sems)
            for cp in copies:
                cp.wait_recv()
            for cp in copies:
                cp.wait_send()

    rev = lambda w: pl.BlockSpec((nb, CHUNK, w), lambda i: (0, nc - 1 - i, 0))
    const = lambda r, c: pl.BlockSpec((r, c), lambda i: (0, 0))
    hbm = pl.BlockSpec(memory_space=pl.ANY)
    return pl.pallas_call(
        body,
        name="mix_bwd",
        grid=(nc,),
        in_specs=[
            rev(PW),
            pl.BlockSpec((nb, CHUNK, 2 * CW), lambda i: (0, jnp.maximum(nc - 2 - i, 0), OCC // (2 * CW))),
            rev(NQK),
            pl.BlockSpec((nb, 1, NQK, DV), lambda i: (0, nc - 1 - i, 0, 0)),
            rev(NV),
            rev(D),
            const(1, DV),
            const(CONV_K, CW),
            const(A_PAD, NQK),
        ] + [hbm] * na,
        out_specs=[rev(PW), const(1, DV), const(8, CW), const(1, NQK), const(A_PAD, NQK)] + [hbm] * na,
        out_shape=[
            jax.ShapeDtypeStruct((nb, s, PW), BF16),
            jax.ShapeDtypeStruct((1, DV), F32),
            jax.ShapeDtypeStruct((8, CW), F32),
            jax.ShapeDtypeStruct((1, NQK), F32),
            jax.ShapeDtypeStruct((A_PAD, NQK), F32),
        ] + [jax.ShapeDtypeStruct((3,) + p.shape[1:], BF16) for p in pbs],
        scratch_shapes=[pltpu.VMEM((nb, NQK, DV), F32), pltpu.VMEM((nb, CHUNK, CW), F32),
                        pltpu.SemaphoreType.DMA((3 * na,)), pltpu.SemaphoreType.DMA((3 * na,))],
        compiler_params=_params(("arbitrary",)),
    )(proj3, proj3, z3, sprev, opre3, dmix3, gng, conv_w, wgu_p, *pbs)


SMALL_PACK_ROWS = 16


def _wgu_slot(r):
    return 4 + r // 4, NQK * (r % 4)


CONV_SLOTS = ((8, 0), (8, CW), (9, 0))


def _in_proj_bwd(dproj2d, x2d, dx1, g1, w_in_t, tm, pb, small_parts):
    t = x2d.shape[0]
    nt = t // tm

    def body(dp_ref, x_ref, dx1_ref, g_ref, w_ref, pb_ref, dg2, dgf, dbg, dgng, dwgu, dcw, lp,
             dx_ref, sums_ref, r2_ref, dg1_acc, pack, gbuf, pack1, gbuf1, send_sems, recv_sems,
             ssend, srecv, ssend1, srecv1):
        x, y, c = _position()
        me = 4 * x + 2 * y + c
        flips = [(k >> 2, (k >> 1) & 1, k & 1) for k in range(1, N_DEV)]
        peers = [(x ^ fx, y ^ fy, c ^ fc) for fx, fy, fc in flips]

        def small_copies(src, dst, send, recv, arrivals):
            return [pltpu.make_async_remote_copy(
                src_ref=src, dst_ref=dst.at[4 * px + 2 * py + pc if arrivals else me],
                send_sem=send.at[k], recv_sem=recv.at[k], device_id=(px, py, pc), device_id_type=MESH)
                for k, (px, py, pc) in enumerate(peers)]

        @pl.when(pl.program_id(0) == 0)
        def _():
            for cp in _stage2_copies([pb_ref], [r2_ref], send_sems, recv_sems):
                cp.start()
            dg1_acc[...] = jnp.zeros_like(dg1_acc)
            pack[...] = jnp.zeros_like(pack)
            pack[1:2, :] = dg2[...]
            pack[2:3, :] = dgf[...]
            pack[3:4, 0:NQK] = dbg[...]
            pack[3:4, NQK:NQK + DV] = dgng[...]
            pack[3:4, NQK + DV:NQK + 2 * DV] = lp[...]
            for r in range(RANK):
                row, lane = _wgu_slot(r)
                pack[row:row + 1, lane:lane + NQK] = dwgu[r:r + 1, :]
            for r, (row, lane) in enumerate(CONV_SLOTS):
                pack[row:row + 1, lane:lane + CW] = dcw[r:r + 1, :]
            for cp in small_copies(pack, gbuf, ssend, srecv, False):
                cp.start()
            gbuf[me] = pack[...]

        xv = x_ref[...]
        r = lax.rsqrt(jnp.mean(xv * xv, axis=-1, keepdims=True) + EPS)
        n1 = xv * r
        dh = _mm(dp_ref[...], w_ref[...])
        dg1_acc[...] += jnp.sum(dh * n1, axis=0, keepdims=True)
        dn = dh * g_ref[...]
        dx_ref[...] = dx1_ref[...] + r * (dn - n1 * jnp.mean(dn * n1, axis=-1, keepdims=True))

        @pl.when(pl.program_id(0) == nt - 1)
        def _():
            pack1[...] = jnp.zeros_like(pack1)
            pack1[0:1, :] = dg1_acc[...]
            for cp in small_copies(pack1, gbuf1, ssend1, srecv1, False):
                cp.start()
            gbuf1[me] = pack1[...]
            copies = _stage2_copies([pb_ref], [r2_ref], send_sems, recv_sems)
            for cp in copies:
                cp.wait_recv()
            for cp in copies:
                cp.wait_send()
            for src, dst, send, recv in ((pack, gbuf, ssend, srecv), (pack1, gbuf1, ssend1, srecv1)):
                for cp in small_copies(src, dst, send, recv, True):
                    cp.wait_recv()
                    cp.wait_send()
            acc = gbuf[0]
            acc1 = gbuf1[0]
            for d in range(1, N_DEV):
                acc = acc + gbuf[d]
                acc1 = acc1 + gbuf1[d]
            sums_ref[...] = acc
            sums_ref[0:1, :] = acc1[0:1, :]

    tile = lambda w: pl.BlockSpec((tm, w), lambda i: (i, 0))
    vec = pl.BlockSpec((1, D), lambda i: (0, 0))
    hbm = pl.BlockSpec(memory_space=pl.ANY)
    whole = lambda a: pl.BlockSpec(a.shape, lambda i: (0,) * a.ndim)
    return pl.pallas_call(
        body,
        name="in_proj_bwd",
        grid=(nt,),
        in_specs=[tile(PW), tile(D), tile(D), vec, pl.BlockSpec((PW, D), lambda i: (0, 0)), hbm]
        + [whole(a) for a in small_parts],
        out_specs=[tile(D), pl.BlockSpec((SMALL_PACK_ROWS, D), lambda i: (0, 0)), hbm],
        out_shape=[jax.ShapeDtypeStruct((t, D), F32), jax.ShapeDtypeStruct((SMALL_PACK_ROWS, D), F32),
                   jax.ShapeDtypeStruct((3,) + pb.shape[1:], BF16)],
        scratch_shapes=[pltpu.VMEM((1, D), F32),
                        pltpu.VMEM((SMALL_PACK_ROWS, D), F32), pltpu.VMEM((N_DEV, SMALL_PACK_ROWS, D), F32),
                        pltpu.VMEM((8, D), F32), pltpu.VMEM((N_DEV, 8, D), F32),
                        pltpu.SemaphoreType.DMA((3,)), pltpu.SemaphoreType.DMA((3,)),
                        pltpu.SemaphoreType.DMA((7,)), pltpu.SemaphoreType.DMA((7,)),
                        pltpu.SemaphoreType.DMA((7,)), pltpu.SemaphoreType.DMA((7,))],
        compiler_params=_params(("arbitrary",)),
    )(dproj2d, x2d, dx1, g1, w_in_t, pb, *small_parts)


def _get_rows(ref):
    return ref[:, 0, :] if len(ref.shape) == 3 else ref[...]


def _put_rows(ref, val):
    if len(ref.shape) == 3:
        ref[:, 0, :] = val
    else:
        ref[...] = val


def _adamw_math(w, g, m, v):
    m = ADAM_B1 * m + (1.0 - ADAM_B1) * g
    v = ADAM_B2 * v + (1.0 - ADAM_B2) * (g * g)
    m_hat = m / (1.0 - ADAM_B1 ** ADAM_STEP)
    v_hat = v / (1.0 - ADAM_B2 ** ADAM_STEP)
    delta = -ADAM_LR * (m_hat / (jnp.sqrt(v_hat) + ADAM_EPS) + ADAM_WD * w)
    return delta, m, v


def _position():
    return lax.axis_index("x"), lax.axis_index("y"), lax.axis_index("c")


GATHER_PARTS = 2
GATHER_SEMS = 7 * GATHER_PARTS


def _gather_copies(stage, lo, rows, gx, send_sems, recv_sems, local_sem):
    x, y, c = _position()
    me = (x, y, c)
    sibling = (x, y, 1 - c)
    chips = [(1 - x, y), (x, 1 - y), (1 - x, 1 - y)]
    part = -(-rows // (16 * GATHER_PARTS)) * 16
    bounds = [(p * part, min(part, rows - p * part)) for p in range(GATHER_PARTS)]

    def blk(px, py, pc, off, n):
        return gx.at[4 * px + 2 * py + pc, pl.ds(off, n), :]

    mine = pltpu.make_async_copy(stage.at[pl.ds(lo, rows), :], gx.at[4 * x + 2 * y + c], local_sem)
    parts = []
    for p, (off, n) in enumerate(bounds):
        def copy(k, block, to, from_stage=False, p=p, off=off, n=n):
            return pltpu.make_async_remote_copy(
                src_ref=stage.at[pl.ds(lo + off, n), :] if from_stage else blk(*block, off, n),
                dst_ref=blk(*block, off, n), send_sem=send_sems.at[7 * p + k], recv_sem=recv_sems.at[7 * p + k],
                device_id=to, device_id_type=MESH)

        first = [copy(0, me, sibling, True)] + [copy(1 + j, me, (*chip, c), True) for j, chip in enumerate(chips)]
        passed = [copy(4 + j, (*chip, c), sibling) for j, chip in enumerate(chips)]
        arrivals = ([copy(0, sibling, me)] + [copy(1 + j, (*chip, c), me) for j, chip in enumerate(chips)]
                    + [copy(4 + j, (*chip, 1 - c), me) for j, chip in enumerate(chips)])
        parts.append((first, passed, arrivals))
    return mine, parts


def _gather_start(*args):
    mine, parts = _gather_copies(*args)
    mine.start()
    for first, _, _ in parts:
        for cp in first:
            cp.start()


def _gather_finish(*args):
    mine, parts = _gather_copies(*args)
    for _, passed, arrivals in parts:
        for j in range(3):
            arrivals[1 + j].wait_recv()
            passed[j].start()
    for first, passed, arrivals in parts:
        arrivals[0].wait_recv()
        for j in range(3):
            arrivals[4 + j].wait_recv()
        for cp in first + passed:
            cp.wait_send()
    mine.wait()


def _gather_sems():
    return [pltpu.SemaphoreType.DMA((GATHER_SEMS,)), pltpu.SemaphoreType.DMA((GATHER_SEMS,)), pltpu.SemaphoreType.DMA]


def _gather_w_in(w_it, w_gt, w_ut, w_d, w_o, wgu_s, conv_s):
    def body(wi_ref, wg_ref, wu_ref, wd_ref, wo_ref, wgu_ref, conv_ref, w_ref, gwgu_ref, gconv_ref, stage,
             buf, send_sems, recv_sems, local_sem, ssend, srecv):
        x, y, c = _position()
        me = 4 * x + 2 * y + c
        stage[SLAB_IN:SLAB_IN + IN_W, :] = wi_ref[:, 0, :].astype(BF16)
        stage[SLAB_IN + IN_W:SLAB_G, :] = jnp.zeros((IN_ROWS - IN_W, D), BF16)
        args = (stage, SLAB_IN, IN_ROWS, buf, send_sems, recv_sems, local_sem)
        _gather_start(*args)
        stage[SLAB_G:SLAB_U, :] = wg_ref[...].astype(BF16)
        stage[SLAB_U:SLAB_D, :] = wu_ref[...].astype(BF16)
        stage[SLAB_D:SLAB_O, :] = wd_ref[...].astype(BF16)
        stage[SLAB_O:SLAB_ROWS, :] = wo_ref[...].astype(BF16)
        flips = [(k >> 2, (k >> 1) & 1, k & 1) for k in range(1, N_DEV)]
        peers = [(x ^ fx, y ^ fy, c ^ fc) for fx, fy, fc in flips]

        def small(k, block_id, to):
            return [pltpu.make_async_remote_copy(
                src_ref=s, dst_ref=g.at[block_id], send_sem=ssend.at[2 * k + n], recv_sem=srecv.at[2 * k + n],
                device_id=to, device_id_type=MESH)
                for n, (s, g) in enumerate(((wgu_ref, gwgu_ref), (conv_ref, gconv_ref)))]

        gwgu_ref[me] = wgu_ref[...]
        gconv_ref[me] = conv_ref[...]
        for k, peer in enumerate(peers):
            for cp in small(k, me, peer):
                cp.start()
        w_ref[IN_COLS:PW, :] = jnp.zeros((PW - IN_COLS, D), BF16)
        _gather_finish(*args)
        for k, (px, py, pc) in enumerate(peers):
            for cp in small(k, 4 * px + 2 * py + pc, (px, py, pc)):
                cp.wait_recv()
                cp.wait_send()
        for j, lo, hi, d in _in_segments():
            w_ref[d:d + hi - lo, :] = buf[j, lo:hi, :]

    vm = pl.BlockSpec(memory_space=pltpu.VMEM)
    return pl.pallas_call(
        body,
        name="gather_w_in",
        in_specs=[vm] * 7,
        out_specs=[vm] * 4,
        out_shape=[jax.ShapeDtypeStruct((PW, D), BF16),
                   jax.ShapeDtypeStruct((N_DEV,) + wgu_s.shape, F32),
                   jax.ShapeDtypeStruct((N_DEV,) + conv_s.shape, F32),
                   jax.ShapeDtypeStruct((SLAB_ROWS, D), BF16)],
        scratch_shapes=[pltpu.VMEM((N_DEV, IN_ROWS, D), BF16)] + _gather_sems()
        + [pltpu.SemaphoreType.DMA((14,)), pltpu.SemaphoreType.DMA((14,))],
        compiler_params=_params(),
    )(w_it, w_gt, w_ut, w_d, w_o, wgu_s, conv_s)


def _w_in_core_reduce(dw_t):
    def body(d_ref, own_ref, sib_ref, pb_ref, g, gb, r1, send_sems, recv_sems):
        x, y, c = _position()
        chip = 2 * x + y
        for j in range(N_DEV):
            g[j, IN_W:IN_ROWS, :] = jnp.zeros((IN_ROWS - IN_W, D), F32)
        for j, lo, hi, d in _in_segments():
            g[j, lo:hi, :] = d_ref[d:d + hi - lo, :]
        for j in range(N_DEV):
            gb[j] = g[j].astype(BF16)
        copies = _stage1_copies(gb, r1, send_sems, recv_sems)
        for cp in copies:
            cp.start()
        own_ref[0] = g[2 * chip + c]
        for cp in copies:
            cp.wait_recv()
        sib_ref[0] = r1[chip]
        for k in range(1, 4):
            t = chip ^ k
            pb_ref[k - 1] = (g[2 * t + c] + r1[t].astype(F32)).astype(BF16)
        for cp in copies:
            cp.wait_send()

    vm = pl.BlockSpec(memory_space=pltpu.VMEM)
    return pl.pallas_call(
        body,
        name="w_in_core_reduce",
        in_specs=[vm],
        out_specs=[vm, vm, vm],
        out_shape=[jax.ShapeDtypeStruct((1, IN_ROWS, D), F32), jax.ShapeDtypeStruct((1, IN_ROWS, D), BF16),
                   jax.ShapeDtypeStruct((3, IN_ROWS, D), BF16)],
        scratch_shapes=[pltpu.VMEM((N_DEV, IN_ROWS, D), F32), pltpu.VMEM((N_DEV, IN_ROWS, D), BF16),
                        pltpu.VMEM((4, IN_ROWS, D), BF16), pltpu.SemaphoreType.DMA((4,)),
                        pltpu.SemaphoreType.DMA((4,))],
        compiler_params=_params(),
    )(dw_t)


def _stage1_copies(g_ref, r_ref, send_sems, recv_sems):
    x, y, c = _position()
    return [pltpu.make_async_remote_copy(
        src_ref=g_ref.at[2 * i + 1 - c], dst_ref=r_ref.at[i], send_sem=send_sems.at[i], recv_sem=recv_sems.at[i],
        device_id=(x, y, 1 - c), device_id_type=MESH) for i in range(4)]


def _ffn_core_reduce(dw3, dwb3, dw_o, dwb_o, pos_arr):
    def body(pos_ref, g0, g1, g2, go, gb3_hbm, gbo_hbm, p0, p1, p2, po, s0, s1, s2, so,
             r1f, r1o, send_sems, recv_sems):
        k = pl.program_id(0)
        x, y, c = _position()
        chip = 2 * x + y

        def copies(p):
            src = 2 * (chip ^ ((p + 1) & 3)) + 1 - c
            pairs = [(gb3_hbm.at[a, src], r1f.at[a, p]) for a in range(3)] + [(gbo_hbm.at[src], r1o.at[p])]
            return [pltpu.make_async_remote_copy(
                src_ref=s, dst_ref=d, send_sem=send_sems.at[4 * p + a], recv_sem=recv_sems.at[4 * p + a],
                device_id=(x, y, 1 - c), device_id_type=MESH) for a, (s, d) in enumerate(pairs)]

        @pl.when(k == 0)
        def _():
            for p in range(4):
                for cp in copies(p):
                    cp.start()

        for p in range(3):
            @pl.when(k == p)
            def _():
                for cp in copies(p):
                    cp.wait_recv()

        for a, (g, pb) in enumerate(((g0, p0), (g1, p1), (g2, p2))):
            pb[...] = (g[...] + r1f[a, k][None].astype(F32)).astype(BF16)
        po[...] = (go[...] + r1o[k][None].astype(F32)).astype(BF16)

        @pl.when(k == 2)
        def _():
            for cp in copies(3):
                cp.wait_recv()
            for a, s in enumerate((s0, s1, s2)):
                s[0] = r1f[a, 3]
            so[0] = r1o[3]
            for p in range(4):
                for cp in copies(p):
                    cp.wait_send()

    other = lambda k, pos: 2 * (pos[1] ^ (k + 1)) + pos[0]
    g_spec = lambda lead: pl.BlockSpec((None, 1, FF_W, D), lambda k, pos: (lead, other(k, pos), 0, 0))
    slot = lambda rows: pl.BlockSpec((1, rows, D), lambda k, pos: (k, 0, 0))
    one = lambda rows: pl.BlockSpec((1, rows, D), lambda k, pos: (0, 0, 0))
    hbm = pl.BlockSpec(memory_space=pl.ANY)
    return pl.pallas_call(
        body,
        name="ffn_core_reduce",
        grid_spec=pltpu.PrefetchScalarGridSpec(
            num_scalar_prefetch=1, grid=(3,),
            in_specs=[g_spec(0), g_spec(1), g_spec(2),
                      pl.BlockSpec((1, OUT_ROWS, D), lambda k, pos: (other(k, pos), 0, 0)), hbm, hbm],
            out_specs=[slot(FF_W), slot(FF_W), slot(FF_W), slot(OUT_ROWS),
                       one(FF_W), one(FF_W), one(FF_W), one(OUT_ROWS)],
            scratch_shapes=[pltpu.VMEM((3, 4, FF_W, D), BF16), pltpu.VMEM((4, OUT_ROWS, D), BF16),
                            pltpu.SemaphoreType.DMA((16,)), pltpu.SemaphoreType.DMA((16,))]),
        out_shape=[jax.ShapeDtypeStruct((3, FF_W, D), BF16)] * 3 + [jax.ShapeDtypeStruct((3, OUT_ROWS, D), BF16)]
        + [jax.ShapeDtypeStruct((1, FF_W, D), BF16)] * 3 + [jax.ShapeDtypeStruct((1, OUT_ROWS, D), BF16)],
        compiler_params=_params(("arbitrary",)),
    )(pos_arr, dw3, dw3, dw3, dw_o, dwb3, dwb_o)


def _stage2_copies(p_refs, r_refs, send_sems, recv_sems):
    x, y, c = _position()
    copies = []
    for a in range(len(p_refs)):
        for k in range(1, 4):
            copies.append(pltpu.make_async_remote_copy(
                src_ref=p_refs[a].at[k - 1], dst_ref=r_refs[a].at[k - 1],
                send_sem=send_sems.at[3 * a + k - 1], recv_sem=recv_sems.at[3 * a + k - 1],
                device_id=(x ^ (k >> 1), y ^ (k & 1), c), device_id_type=MESH))
    return copies


def _finish_weights(items, pos_arr, name, nblk):
    n = len(items)
    in_specs, out_specs, out_shape, operands, wbs = [], [], [], [], []
    for g8, lead, r1, r2, w, m, v in items:
        rows, wr = g8.shape[-2], w.shape[0]
        assert rows % nblk == 0 and wr % nblk == 0 and (nblk == 1 or (rows == wr and rows % (16 * nblk) == 0))
        rb, wb = rows // nblk, wr // nblk
        if lead is not None:
            g_spec = pl.BlockSpec((None, 1, rb, D), lambda i, pos, lead=lead: (lead, 2 * pos[1] + pos[0], i, 0))
        elif g8.shape[0] == 1:
            g_spec = pl.BlockSpec((1, rb, D), lambda i, pos: (0, i, 0))
        else:
            g_spec = pl.BlockSpec((1, rb, D), lambda i, pos: (2 * pos[1] + pos[0], i, 0))
        r1_spec = pl.BlockSpec((1, rb, D), lambda i, pos: (0, i, 0))
        if w.ndim == 3:
            wblk = pl.BlockSpec((wb, 1, D), lambda i, pos: (i, 0, 0))
        else:
            wblk = pl.BlockSpec((wb, D), lambda i, pos: (i, 0))
        in_specs += [g_spec, r1_spec, pl.BlockSpec((3, rb, D), lambda i, pos: (0, i, 0)), wblk, wblk, wblk]
        out_specs += [wblk] * 4
        out_shape += [jax.ShapeDtypeStruct(w.shape, F32)] * 4
        operands += [g8, r1, r2, w, m, v]
        wbs.append(wb)

    def body(pos_ref, *refs):
        for a in range(n):
            g_ref, r1_ref, r2_ref, w_ref, m_ref, v_ref = refs[6 * a:6 * a + 6]
            g_out, d_out, m_out, v_out = refs[6 * n + 4 * a:6 * n + 4 * a + 4]
            g = g_ref[0] + r1_ref[0].astype(F32)
            for k in range(3):
                g = g + r2_ref[k].astype(F32)
            g = g[0:wbs[a], :]
            d, mn, vn = _adamw_math(_get_rows(w_ref), g, _get_rows(m_ref), _get_rows(v_ref))
            for out, val in ((g_out, g), (d_out, d), (m_out, mn), (v_out, vn)):
                _put_rows(out, val)

    return pl.pallas_call(
        body,
        name=name,
        grid_spec=pltpu.PrefetchScalarGridSpec(
            num_scalar_prefetch=1, grid=(nblk,), in_specs=in_specs, out_specs=out_specs),
        out_shape=out_shape,
        compiler_params=_params(("arbitrary",)),
    )(pos_arr, *operands)


SMALL_NAMES = ("norm1_g", "norm2_g", "norm_f_g", "b_gate", "gla_norm_g", "w_gate_up", "conv_w")
WGU_W = NQK // N_DEV
CONV_W = CW // N_DEV


def _small_adamw(sums, ws, ms, vs):
    n = len(SMALL_NAMES)

    def body(*refs):
        acc_ref = refs[0]
        w_refs, m_refs, v_refs = refs[1:1 + n], refs[1 + n:1 + 2 * n], refs[1 + 2 * n:1 + 3 * n]
        loss_ref = refs[1 + 3 * n]
        outs = refs[2 + 3 * n:]
        x, y, c = _position()
        me = 4 * x + 2 * y + c
        acc = acc_ref[...]
        loss_ref[...] = acc[3:4, NQK + DV:NQK + DV + 1]

        def my_columns(full, width):
            r = lax.broadcasted_iota(jnp.int32, (full.shape[1], width), 0)
            col = lax.broadcasted_iota(jnp.int32, (full.shape[1], width), 1)
            sel = (r == width * me + col).astype(F32)
            return _mm(full, sel, precision=HIGHEST)

        dwgu = jnp.concatenate([acc[row:row + 1, lane:lane + NQK] for row, lane in map(_wgu_slot, range(RANK))], axis=0)
        dcw = jnp.concatenate([acc[row:row + 1, lane:lane + CW] for row, lane in CONV_SLOTS], axis=0)
        grads = [acc[0:1, :], acc[1:2, :], acc[2:3, :], acc[3:4, 0:NQK], acc[3:4, NQK:NQK + DV],
                 my_columns(dwgu, WGU_W), my_columns(dcw, CONV_W)]
        for i, g in enumerate(grads):
            d, mn, vn = _adamw_math(_get_rows(w_refs[i]), g, _get_rows(m_refs[i]), _get_rows(v_refs[i]))
            for out, val in zip(outs[4 * i:4 * i + 4], (g, d, mn, vn)):
                _put_rows(out, val)

    vm = pl.BlockSpec(memory_space=pltpu.VMEM)
    out_shape = [jax.ShapeDtypeStruct((1, 1), F32)]
    for w in ws:
        out_shape += [jax.ShapeDtypeStruct(w.shape, F32)] * 4
    return pl.pallas_call(
        body,
        name="small_adamw",
        in_specs=[vm] * (1 + 3 * n),
        out_specs=[vm] * (1 + 4 * n),
        out_shape=out_shape,
        compiler_params=_params(),
    )(sums, *ws, *ms, *vs)


def kernel(x, norm1_g, w_in, w_gate_up, b_gate, gla_norm_g, conv_w, w_out, norm2_g, w_ffn_gate, w_ffn_up, w_ffn_down, norm_f_g, loss_target, m_norm1_g, m_w_in, m_w_gate_up, m_b_gate, m_gla_norm_g, m_conv_w, m_w_out, m_norm2_g, m_w_ffn_gate, m_w_ffn_up, m_w_ffn_down, m_norm_f_g, v_norm1_g, v_w_in, v_w_gate_up, v_b_gate, v_gla_norm_g, v_conv_w, v_w_out, v_norm2_g, v_w_ffn_gate, v_w_ffn_up, v_w_ffn_down, v_norm_f_g):
    xi, yi, ci = _position()
    pos_arr = jnp.stack([ci, 2 * xi + yi]).astype(jnp.int32)
    nb, s, _ = x.shape
    t = nb * s

    tr = lambda a: a[0].T
    rows_of = lambda a: a.transpose(2, 0, 1)
    conv_rows = lambda a: a.transpose(1, 0, 2)
    w_in_t, gwgu, gconv, stage = _gather_w_in(rows_of(w_in), tr(w_ffn_gate), tr(w_ffn_up), w_ffn_down[0], w_out[0],
                                              w_gate_up[0], conv_rows(conv_w))
    wgu_f = gwgu.transpose(1, 0, 2).reshape(RANK, NQK)
    conv_f = gconv.transpose(1, 2, 0, 3).reshape(CONV_K, CW)
    wgu_p = jnp.concatenate([wgu_f, jnp.zeros((A_PAD - RANK, NQK), F32)], axis=0).astype(BF16)

    x2d = x.reshape(t, D)
    tgt2d = loss_target.reshape(t, D)
    tm = 256
    tm_in = min(512, t)
    tk = min(2048, t)
    proj, z, h, gwb = _in_proj_fwd(x2d, norm1_g, w_in_t, wgu_p, b_gate, tm_in, stage)
    proj3 = proj.reshape(nb, s, PW)
    z3 = z.reshape(nb, s, NQK)
    mix3, opre3, sprev, gwa = _mix_fwd(proj3, z3, gla_norm_g, conv_f, stage)
    mix2d = mix3.reshape(t, D)
    dx1, dx1b, dmix, adu, hb, dg2, dgf, loss_part = _ffn_fwd_bwd(
        mix2d, x2d, tgt2d, gwa, gwb, norm2_g, norm_f_g.reshape(1, D), tm)
    dw3, dwb3 = _dw_ffn(adu, hb, tk)
    dw3 = dw3.reshape(3, N_DEV, FF_W, D)
    dw_o, dwb_o = _tn_matmul(mix2d, dx1b, D // 2, D, tk, "dw_out", True)
    dw_o = dw_o.reshape(N_DEV, OUT_ROWS, D)
    *pb, sib_d, sib_g, sib_u, sib_o = _ffn_core_reduce(
        dw3, dwb3.reshape(3, N_DEV, FF_W, D), dw_o, dwb_o.reshape(N_DEV, OUT_ROWS, D), pos_arr)
    g8 = [dw3, dw3, dw3, dw_o]
    leads = [0, 1, 2, None]
    tags = ("w_ffn_down", "w_ffn_gate", "w_ffn_up", "w_out")
    r1 = [sib_d, sib_g, sib_u, sib_o]
    mb = _mix_bwd(proj3, z3, sprev, opre3, dmix.reshape(nb, s, D), gla_norm_g, conv_f, wgu_p, [pb[0], pb[1], pb[3]])
    dproj3, dgng, dcw, dbg, dwgu = mb[:5]
    dproj2d = dproj3.reshape(t, PW)
    dw_in_t, r2_up = _tn_matmul(dproj2d, h, PW // 5, D, tk, "dw_in", False, _stage2_rider([pb[2]]))
    r2 = [mb[5], mb[6], r2_up, mb[7]]
    g_in, r1_in, pb_in = _w_in_core_reduce(dw_in_t)
    dx, small_sums, r2_in = _in_proj_bwd(dproj2d, x2d, dx1, norm1_g, w_in_t, tm_in, pb_in,
                                         (dg2, dgf, dbg, dgng, dwgu, dcw, loss_part))

    tags = ("w_in",) + tags
    g8 = [g_in] + g8
    leads = [None] + leads
    r1 = [r1_in] + list(r1)
    r2 = [r2_in] + r2
    shard_w = (rows_of(w_in), w_ffn_down[0], tr(w_ffn_gate), tr(w_ffn_up), w_out[0])
    shard_m = (rows_of(m_w_in), m_w_ffn_down[0], tr(m_w_ffn_gate), tr(m_w_ffn_up), m_w_out[0])
    shard_v = (rows_of(v_w_in), v_w_ffn_down[0], tr(v_w_ffn_gate), tr(v_w_ffn_up), v_w_out[0])
    back = (lambda o: o.transpose(1, 2, 0), lambda o: o[None], lambda o: o.T[None], lambda o: o.T[None],
            lambda o: o[None])
    items = list(zip(g8, leads, r1, r2, shard_w, shard_m, shard_v))
    flat = list(_finish_weights(items[1:], pos_arr, "finish_ffn_out", 2))
    flat = list(_finish_weights(items[:1], pos_arr, "finish_w_in", 1)) + flat
    results = {}
    for i, (tag, to_shard) in enumerate(zip(tags, back)):
        results[tag] = [to_shard(o) for o in flat[4 * i:4 * i + 4]]

    small_w = (norm1_g, norm2_g, norm_f_g.reshape(1, D), b_gate, gla_norm_g, w_gate_up[0], conv_rows(conv_w))
    small_m = (m_norm1_g, m_norm2_g, m_norm_f_g.reshape(1, D), m_b_gate, m_gla_norm_g, m_w_gate_up[0],
               conv_rows(m_conv_w))
    small_v = (v_norm1_g, v_norm2_g, v_norm_f_g.reshape(1, D), v_b_gate, v_gla_norm_g, v_w_gate_up[0],
               conv_rows(v_conv_w))
    so = _small_adamw(small_sums, small_w, small_m, small_v)
    loss = so[0].reshape(())
    to_shape = {"norm_f_g": lambda o: o.reshape(D), "w_gate_up": lambda o: o[None],
                "conv_w": lambda o: o.transpose(1, 0, 2)}
    for i, name in enumerate(SMALL_NAMES):
        results[name] = [to_shape.get(name, lambda o: o)(o) for o in so[1 + 4 * i:5 + 4 * i]]

    names = ("norm1_g", "w_in", "w_gate_up", "b_gate", "gla_norm_g", "conv_w", "w_out", "norm2_g",
             "w_ffn_gate", "w_ffn_up", "w_ffn_down", "norm_f_g")
    outs = [loss, dx.reshape(nb, s, D)]
    for kind in range(4):
        for name in names:
            outs.append(results[name][kind])
    return tuple(outs)
```

```python
import jax
import jax.numpy as jnp
from jax import lax
from jax.experimental import pallas as pl
from jax.experimental.pallas import tpu as pltpu

F32 = jnp.float32
BF16 = jnp.bfloat16
HIGHEST = lax.Precision.HIGHEST
MESH = pl.DeviceIdType.MESH

N_DEV = 8
D = 1024
DFF = 2816
HEADS = 4
DK = 64
DV = 128
NQK = HEADS * DK
NV = HEADS * DV
RANK = 16
CHUNK = 64
CW = 512
CONV_K = 3
IN_COLS = 3088
EPS = 1e-6
INV_GATE_NORM = 1.0 / 16.0
Q_SCALE = DK ** -0.5

PW = 3200
OQ, OK_, OV, OG, OCB, OCC, OCH, OA = 0, 256, 512, 1024, 1536, 2048, 2560, 3072
A_PAD = 128

ADAM_LR = 0.001
ADAM_B1 = 0.9
ADAM_B2 = 0.999
ADAM_EPS = 1e-08
ADAM_WD = 0.01
ADAM_STEP = 10

IN_W = IN_COLS // N_DEV
IN_ROWS = 400
FF_W = DFF // N_DEV
OUT_ROWS = D // N_DEV
SLAB_IN = 0
SLAB_G = SLAB_IN + IN_ROWS
SLAB_U = SLAB_G + FF_W
SLAB_D = SLAB_U + FF_W
SLAB_O = SLAB_D + FF_W
SLAB_ROWS = SLAB_O + OUT_ROWS

VMEM_LIMIT = 56 * 1024 * 1024


def _params(sem=None, vmem=VMEM_LIMIT):
    return pltpu.CompilerParams(dimension_semantics=sem, vmem_limit_bytes=vmem)


def _nt(a, b):
    return lax.dot_general(a, b, (((1,), (1,)), ((), ())), preferred_element_type=F32)


def _tn(a, b, precision=None):
    return lax.dot_general(a, b, (((0,), (0,)), ((), ())), preferred_element_type=F32, precision=precision)


def _mm(a, b, precision=None):
    return jnp.dot(a, b, preferred_element_type=F32, precision=precision)


def _in_segments():
    segs = []
    for j in range(N_DEV):
        lo, hi = IN_W * j, IN_W * (j + 1)
        cuts = sorted({lo, hi} | {c for c in (OCB, OCB + RANK) if lo < c < hi})
        for a, b in zip(cuts[:-1], cuts[1:]):
            if a < OCB:
                d = a
            elif a < OCB + RANK:
                d = OA + (a - OCB)
            else:
                d = a - RANK
            segs.append((j, a - lo, b - lo, d))
    return segs


def _in_proj_fwd(x2d, g1, w_in_t, wgu_p, b_gate, tm, stage):
    t = x2d.shape[0]
    nt = t // tm
    g_rows = SLAB_ROWS - SLAB_D

    def body(x_ref, g_ref, w_ref, wgu_ref, bg_ref, stage_hbm, proj_ref, z_ref, h_ref, gwb_ref,
             send_sems, recv_sems, local_sem):
        gargs = (stage_hbm, SLAB_D, g_rows, gwb_ref, send_sems, recv_sems, local_sem)

        @pl.when(pl.program_id(0) == 0)
        def _():
            _gather_start(*gargs)

        x = x_ref[...]
        r = lax.rsqrt(jnp.mean(x * x, axis=-1, keepdims=True) + EPS)
        h = ((x * r) * g_ref[...]).astype(BF16)
        h_ref[...] = h
        proj = _nt(h, w_ref[...])
        proj_ref[...] = proj
        pa = proj[:, OA:OA + A_PAD].astype(BF16)
        z_ref[...] = _mm(pa, wgu_ref[...]) + bg_ref[...]

        @pl.when(pl.program_id(0) == nt - 1)
        def _():
            _gather_finish(*gargs)

    return pl.pallas_call(
        body,
        name="in_proj_fwd",
        grid=(t // tm,),
        in_specs=[
            pl.BlockSpec((tm, D), lambda i: (i, 0)),
            pl.BlockSpec((1, D), lambda i: (0, 0)),
            pl.BlockSpec((PW, D), lambda i: (0, 0)),
            pl.BlockSpec((A_PAD, NQK), lambda i: (0, 0)),
            pl.BlockSpec((1, NQK), lambda i: (0, 0)),
            pl.BlockSpec(memory_space=pl.ANY),
        ],
        out_specs=[
            pl.BlockSpec((tm, PW), lambda i: (i, 0)),
            pl.BlockSpec((tm, NQK), lambda i: (i, 0)),
            pl.BlockSpec((tm, D), lambda i: (i, 0)),
            pl.BlockSpec(memory_space=pl.ANY),
        ],
        out_shape=[
            jax.ShapeDtypeStruct((t, PW), F32),
            jax.ShapeDtypeStruct((t, NQK), F32),
            jax.ShapeDtypeStruct((t, D), BF16),
            jax.ShapeDtypeStruct((N_DEV, g_rows, D), BF16),
        ],
        scratch_shapes=_gather_sems(),
        compiler_params=_params(("arbitrary",)),
    )(x2d, g1, w_in_t, wgu_p, b_gate, stage)


def _head_masks():
    lane = lax.broadcasted_iota(jnp.int32, (1, NQK), 1)
    return [(lane >= DK * h) & (lane < DK * (h + 1)) for h in range(HEADS)]


def _split_bf16(x, n):
    parts = []
    for _ in range(n):
        p = x.astype(BF16)
        parts.append(p)
        x = x - p.astype(F32)
    return parts


def _chunk_fwd_parts(q, k, z, tril16):
    la = (jnp.minimum(z, 0.0) - jnp.log1p(jnp.exp(-jnp.abs(z)))) * INV_GATE_NORM
    la_parts = _split_bf16(la, 3)
    bc = _mm(tril16, la_parts[0]) + _mm(tril16, la_parts[1]) + _mm(tril16, la_parts[2])
    bl = bc[CHUNK - 1:CHUNK, :]
    eb = jnp.exp(bc)
    enb = jnp.exp(-bc)
    ekl = jnp.exp(bl - bc)
    qi = (q * Q_SCALE) * eb
    ki = k * enb
    ks = k * ekl
    ones16 = jnp.ones((CHUNK, DV), BF16)
    decb = jnp.exp(_tn(la_parts[0], ones16) + _tn(la_parts[1], ones16) + _tn(la_parts[2], ones16))
    return la, eb, enb, ekl, qi, ki, ks, decb


def _stack_heads(a, masks):
    return jnp.concatenate([jnp.where(m, a, 0.0) for m in masks], axis=0)


def _merge_heads(blocks, masks):
    out = blocks[HEADS - 1]
    for h in range(HEADS - 2, -1, -1):
        out = jnp.where(masks[h], blocks[h], out)
    return out


def _causal_stack_mask():
    row = lax.broadcasted_iota(jnp.int32, (HEADS * CHUNK, CHUNK), 0)
    col = lax.broadcasted_iota(jnp.int32, (HEADS * CHUNK, CHUNK), 1)
    return (row & (CHUNK - 1)) >= col


def _conv_taps(u, uprev):
    row = lax.broadcasted_iota(jnp.int32, u.shape, 0)
    u1 = jnp.where(row < 1, pltpu.roll(uprev, 1, 0), pltpu.roll(u, 1, 0))
    u2 = jnp.where(row < 2, pltpu.roll(uprev, 2, 0), pltpu.roll(u, 2, 0))
    return u1, u2


def _mix_fwd(proj3, z3, gng, conv_w, stage, x3, gwb):
    nb, s, _ = proj3.shape
    nc = s // CHUNK
    g_rows = SLAB_D - SLAB_G

    def body(p_ref, z_ref, gng_ref, cw_ref, stage_hbm, x_ref, gwb_hbm, mix_ref, o_ref, sprev_ref, x1_ref, gwa_ref,
             s_ref, uprev_ref, wo, wsem, send_sems, recv_sems, local_sem):
        n = pl.program_id(0)
        gargs = (stage_hbm, SLAB_G, g_rows, gwa_ref, send_sems, recv_sems, local_sem)

        @pl.when(n == 0)
        def _():
            _gather_start(*gargs)
            loads = [pltpu.make_async_copy(gwb_hbm.at[j, pl.ds(FF_W, OUT_ROWS), :],
                                           wo.at[pl.ds(OUT_ROWS * j, OUT_ROWS), :], wsem.at[j]) for j in range(N_DEV)]
            for cp in loads:
                cp.start()
            s_ref[...] = jnp.zeros_like(s_ref)
            uprev_ref[...] = jnp.zeros_like(uprev_ref)
            for cp in loads:
                cp.wait()

        r_i = lax.broadcasted_iota(jnp.int32, (CHUNK, CHUNK), 0)
        c_i = lax.broadcasted_iota(jnp.int32, (CHUNK, CHUNK), 1)
        tril16 = (r_i >= c_i).astype(BF16)
        masks = _head_masks()
        cmask = _causal_stack_mask()
        gg = gng_ref[...]
        for b in range(nb):
            q = p_ref[b, :, OQ:OQ + NQK]
            k = p_ref[b, :, OK_:OK_ + NQK]
            _, _, _, _, qi, ki, ks, decb = _chunk_fwd_parts(q, k, z_ref[b], tril16)
            qs = _stack_heads(qi, masks).astype(BF16)
            sc = jnp.where(cmask, _nt(qs, ki.astype(BF16)), 0.0).astype(BF16)
            st = s_ref[b]
            sprev_ref[b, 0] = st
            o_inter = _mm(qs, st.astype(BF16))
            v16 = p_ref[b, :, OV:OV + NV].astype(BF16)
            kv = _tn(ks.astype(BF16), v16)
            for h in range(HEADS):
                rows = slice(CHUNK * h, CHUNK * (h + 1))
                cols = slice(DV * h, DV * (h + 1))
                o = _mm(sc[rows], v16[:, cols]) + o_inter[rows]
                o_ref[b, :, cols] = o
                r = lax.rsqrt(jnp.mean(o * o, axis=-1, keepdims=True) + EPS)
                on = (o * r) * gg
                g = p_ref[b, :, OG + DV * h:OG + DV * (h + 1)]
                mix_ref[b, :, cols] = (on * (g * jax.nn.sigmoid(g))).astype(BF16)
                s_ref[b, rows, :] = decb[rows] * st[rows] + kv[rows, cols]
            u = p_ref[b, :, OCC:OCC + CW] * p_ref[b, :, OCH:OCH + CW]
            u1, u2 = _conv_taps(u, uprev_ref[b])
            yc = cw_ref[0:1, :] * u2 + cw_ref[1:2, :] * u1 + cw_ref[2:3, :] * u
            mix_ref[b, :, NV:NV + CW] = (p_ref[b, :, OCB:OCB + CW] * yc).astype(BF16)
            uprev_ref[b] = u
        mixed = _mm(jnp.concatenate([mix_ref[b] for b in range(nb)], axis=0), wo[...])
        for b in range(nb):
            x1_ref[b] = x_ref[b] + mixed[CHUNK * b:CHUNK * (b + 1)]

        @pl.when(n == nc - 1)
        def _():
            _gather_finish(*gargs)

    return pl.pallas_call(
        body,
        name="mix_fwd",
        grid=(nc,),
        in_specs=[
            pl.BlockSpec((nb, CHUNK, PW), lambda n: (0, n, 0)),
            pl.BlockSpec((nb, CHUNK, NQK), lambda n: (0, n, 0)),
            pl.BlockSpec((1, DV), lambda n: (0, 0)),
            pl.BlockSpec((CONV_K, CW), lambda n: (0, 0)),
            pl.BlockSpec(memory_space=pl.ANY),
            pl.BlockSpec((nb, CHUNK, D), lambda n: (0, n, 0)),
            pl.BlockSpec(memory_space=pl.ANY),
        ],
        out_specs=[
            pl.BlockSpec((nb, CHUNK, D), lambda n: (0, n, 0)),
            pl.BlockSpec((nb, CHUNK, NV), lambda n: (0, n, 0)),
            pl.BlockSpec((nb, 1, NQK, DV), lambda n: (0, n, 0, 0)),
            pl.BlockSpec((nb, CHUNK, D), lambda n: (0, n, 0)),
            pl.BlockSpec(memory_space=pl.ANY),
        ],
        out_shape=[
            jax.ShapeDtypeStruct((nb, s, D), BF16),
            jax.ShapeDtypeStruct((nb, s, NV), F32),
            jax.ShapeDtypeStruct((nb, nc, NQK, DV), F32),
            jax.ShapeDtypeStruct((nb, s, D), F32),
            jax.ShapeDtypeStruct((N_DEV, g_rows, D), BF16),
        ],
        scratch_shapes=[pltpu.VMEM((nb, NQK, DV), F32), pltpu.VMEM((nb, CHUNK, CW), F32),
                        pltpu.VMEM((D, D), BF16), pltpu.SemaphoreType.DMA((N_DEV,))] + _gather_sems(),
        compiler_params=_params(("arbitrary",)),
    )(proj3, z3, gng, conv_w, stage, x3, gwb)


def _ffn_fwd_bwd(x1_2d, tgt2d, gwa, gwb, g2, gf, tm):
    t = x1_2d.shape[0]

    def body(x1_ref, tgt_ref, g2_ref, gf_ref, gwa_hbm, gwb_hbm,
             dx1_ref, dx1b_ref, dmix_ref, adu_ref, hb_ref, dg2_ref, dgf_ref, loss_ref,
             wo, wg, wu, wd, wsem):
        i = pl.program_id(0)

        def weight_copies(n, dst, src, off, rows):
            return [pltpu.make_async_copy(src.at[j, pl.ds(off, rows), :], dst.at[pl.ds(rows * j, rows), :],
                                          wsem.at[N_DEV * n + j]) for j in range(N_DEV)]

        loads = (weight_copies(0, wo, gwb_hbm, FF_W, OUT_ROWS), weight_copies(1, wg, gwa_hbm, 0, FF_W),
                 weight_copies(2, wu, gwa_hbm, FF_W, FF_W), weight_copies(3, wd, gwb_hbm, 0, FF_W))

        @pl.when(i == 0)
        def _():
            for group in loads:
                for cp in group:
                    cp.start()
            dg2_ref[...] = jnp.zeros_like(dg2_ref)
            dgf_ref[...] = jnp.zeros_like(dgf_ref)
            loss_ref[...] = jnp.zeros_like(loss_ref)
            for group in loads:
                for cp in group:
                    cp.wait()

        g2v = g2_ref[...]
        gfv = gf_ref[...]
        x1 = x1_ref[...]
        r2 = lax.rsqrt(jnp.mean(x1 * x1, axis=-1, keepdims=True) + EPS)
        n2 = x1 * r2
        h2 = (n2 * g2v).astype(BF16)
        hb_ref[1] = h2
        gate = _nt(h2, wg[...])
        up = _nt(h2, wu[...])
        sg = jax.nn.sigmoid(gate)
        sil = gate * sg
        act = (sil * up).astype(BF16)
        adu_ref[0] = act
        x2 = x1 + _mm(act, wd[...])
        rf = lax.rsqrt(jnp.mean(x2 * x2, axis=-1, keepdims=True) + EPS)
        nf = x2 * rf
        err = nf * gfv - tgt_ref[...]
        loss_ref[...] += 0.5 * jnp.sum(jnp.mean(err * err, axis=-1, keepdims=True))
        dy = err * (1.0 / D)
        dgf_ref[...] += jnp.sum(dy * nf, axis=0, keepdims=True)
        dnf = dy * gfv
        dx2 = rf * (dnf - nf * jnp.mean(dnf * nf, axis=-1, keepdims=True))
        dx2b = dx2.astype(BF16)
        hb_ref[0] = dx2b
        dact = _nt(dx2b, wd[...])
        dup = (dact * sil).astype(BF16)
        dgate = ((dact * up) * (sg * (1.0 + gate * (1.0 - sg)))).astype(BF16)
        adu_ref[2] = dup
        adu_ref[1] = dgate
        dh2 = _mm(dgate, wg[...]) + _mm(dup, wu[...])
        dg2_ref[...] += jnp.sum(dh2 * n2, axis=0, keepdims=True)
        dn2 = dh2 * g2v
        dx1 = dx2 + r2 * (dn2 - n2 * jnp.mean(dn2 * n2, axis=-1, keepdims=True))
        dx1_ref[...] = dx1
        dx1b = dx1.astype(BF16)
        dx1b_ref[...] = dx1b
        dmix_ref[...] = _nt(dx1b, wo[...])

    tile = lambda w: pl.BlockSpec((tm, w), lambda i: (i, 0))
    vec = pl.BlockSpec((1, D), lambda i: (0, 0))
    hbm = pl.BlockSpec(memory_space=pl.ANY)
    return pl.pallas_call(
        body,
        name="ffn_fwd_bwd",
        grid=(t // tm,),
        in_specs=[tile(D), tile(D), vec, vec, hbm, hbm],
        out_specs=[tile(D), tile(D), tile(D), pl.BlockSpec((3, tm, DFF), lambda i: (0, i, 0)),
                   pl.BlockSpec((2, tm, D), lambda i: (0, i, 0)), vec, vec,
                   pl.BlockSpec((1, 128), lambda i: (0, 0))],
        out_shape=[
            jax.ShapeDtypeStruct((t, D), F32),
            jax.ShapeDtypeStruct((t, D), BF16),
            jax.ShapeDtypeStruct((t, D), F32),
            jax.ShapeDtypeStruct((3, t, DFF), BF16),
            jax.ShapeDtypeStruct((2, t, D), BF16),
            jax.ShapeDtypeStruct((1, D), F32),
            jax.ShapeDtypeStruct((1, D), F32),
            jax.ShapeDtypeStruct((1, 128), F32),
        ],
        scratch_shapes=[pltpu.VMEM((D, D), BF16), pltpu.VMEM((DFF, D), BF16), pltpu.VMEM((DFF, D), BF16),
                        pltpu.VMEM((DFF, D), BF16), pltpu.SemaphoreType.DMA((4 * N_DEV,))],
        compiler_params=_params(("arbitrary",)),
    )(x1_2d, tgt2d, g2, gf, gwa, gwb)


def _stage2_rider(pbs):
    return dict(inputs=list(pbs), out_shape=[jax.ShapeDtypeStruct(p.shape, BF16) for p in pbs], nsem=3 * len(pbs),
                copies=_stage2_copies)


def _tn_matmul(a, b, bm, bn, tk, name, with_bf16, rider=None):
    t, m = a.shape
    n = b.shape[1]
    nk = t // tk
    nout = 2 if with_bf16 else 1
    grid = (m // bm, n // bn, nk)
    r_in = [] if rider is None else rider["inputs"]
    r_out = [] if rider is None else rider["out_shape"]

    def body(a_ref, b_ref, *rest):
        ins, outs = rest[:len(r_in)], rest[len(r_in):len(r_in) + nout]
        r_outs, sems = rest[len(r_in) + nout:len(r_in) + nout + len(r_out)], rest[len(r_in) + nout + len(r_out):]
        o_ref = outs[0]
        i, j, k = pl.program_id(0), pl.program_id(1), pl.program_id(2)
        if rider is not None:
            @pl.when((i == 0) & (j == 0) & (k == 0))
            def _():
                for cp in rider["copies"](ins, r_outs, *sems):
                    cp.start()

        @pl.when(k == 0)
        def _():
            o_ref[...] = jnp.zeros_like(o_ref)

        o_ref[...] += _tn(a_ref[...].astype(BF16), b_ref[...].astype(BF16))
        if with_bf16:
            @pl.when(k == nk - 1)
            def _():
                outs[1][...] = o_ref[...].astype(BF16)
        if rider is not None:
            @pl.when((i == grid[0] - 1) & (j == grid[1] - 1) & (k == nk - 1))
            def _():
                copies = rider["copies"](ins, r_outs, *sems)
                for cp in copies:
                    cp.wait_recv()
                for cp in copies:
                    cp.wait_send()

    out_blk = pl.BlockSpec((bm, bn), lambda i, j, k: (i, j))
    hbm = pl.BlockSpec(memory_space=pl.ANY)
    out_shape = [jax.ShapeDtypeStruct((m, n), F32)] + ([jax.ShapeDtypeStruct((m, n), BF16)] if with_bf16 else [])
    res = pl.pallas_call(
        body,
        name=name,
        grid=grid,
        in_specs=[pl.BlockSpec((tk, bm), lambda i, j, k: (k, i)), pl.BlockSpec((tk, bn), lambda i, j, k: (k, j))]
        + [hbm] * len(r_in),
        out_specs=[out_blk] * nout + [hbm] * len(r_out),
        out_shape=out_shape + list(r_out),
        scratch_shapes=([] if rider is None else
                        [pltpu.SemaphoreType.DMA((rider["nsem"],)), pltpu.SemaphoreType.DMA((rider["nsem"],))]),
        compiler_params=_params(("parallel", "parallel", "arbitrary") if rider is None
                                else ("arbitrary", "arbitrary", "arbitrary")),
    )(a, b, *r_in)
    return res[0] if len(res) == 1 else res


def _dw_ffn(adu, hb, tk):
    _, t, _ = adu.shape
    bm = DFF // 2
    nk = t // tk

    def body(a_ref, b_ref, o_ref, ob_ref):
        k = pl.program_id(2)

        @pl.when(k == 0)
        def _():
            o_ref[...] = jnp.zeros_like(o_ref)

        o_ref[...] += _tn(a_ref[...], b_ref[...])

        @pl.when(k == nk - 1)
        def _():
            ob_ref[...] = o_ref[...].astype(BF16)

    out_blk = pl.BlockSpec((None, bm, D), lambda p, i, k: (p, i, 0))
    return pl.pallas_call(
        body,
        name="dw_ffn",
        grid=(3, DFF // bm, nk),
        in_specs=[pl.BlockSpec((None, tk, bm), lambda p, i, k: (p, k, i)),
                  pl.BlockSpec((None, tk, D), lambda p, i, k: (jnp.minimum(p, 1), k, 0))],
        out_specs=[out_blk, out_blk],
        out_shape=[jax.ShapeDtypeStruct((3, DFF, D), F32), jax.ShapeDtypeStruct((3, DFF, D), BF16)],
        compiler_params=_params(("arbitrary", "arbitrary", "arbitrary")),
    )(adu, hb)


def _mix_bwd(proj3, z3, sprev, opre3, dmix3, gng, conv_w, wgu_p, pbs):
    nb, s, _ = proj3.shape
    nc = s // CHUNK
    na = len(pbs)

    def body(*refs):
        (p_ref, pprev_ref, z_ref, sp_ref, o_ref, dm_ref, gng_ref, cw_ref, wgu_ref) = refs[:9]
        pb_refs = refs[9:9 + na]
        (dproj_ref, dgng_ref, dcw_ref, dbg_ref, dwgu_ref) = refs[9 + na:14 + na]
        r2_refs = refs[14 + na:14 + 2 * na]
        ds_ref, dycn_ref, send_sems, recv_sems = refs[14 + 2 * na:]
        step = pl.program_id(0)
        n = nc - 1 - step

        @pl.when(step == 0)
        def _():
            for cp in _stage2_copies(pb_refs, r2_refs, send_sems, recv_sems):
                cp.start()
            ds_ref[...] = jnp.zeros_like(ds_ref)
            dycn_ref[...] = jnp.zeros_like(dycn_ref)
            dgng_ref[...] = jnp.zeros_like(dgng_ref)
            dcw_ref[...] = jnp.zeros_like(dcw_ref)
            dbg_ref[...] = jnp.zeros_like(dbg_ref)
            dwgu_ref[...] = jnp.zeros_like(dwgu_ref)

        r_i = lax.broadcasted_iota(jnp.int32, (CHUNK, CHUNK), 0)
        c_i = lax.broadcasted_iota(jnp.int32, (CHUNK, CHUNK), 1)
        tril16 = (r_i >= c_i).astype(BF16)
        triu16 = (r_i <= c_i).astype(BF16)
        causal = r_i >= c_i
        masks = _head_masks()
        cmask = _causal_stack_mask()
        gg = gng_ref[...]
        last_row = lax.broadcasted_iota(jnp.int32, (CHUNK, NQK), 0) == CHUNK - 1
        ones_r = jnp.ones((16, DV), BF16)
        has_prev = (n > 0).astype(F32)
        for b in range(nb):
            q = p_ref[b, :, OQ:OQ + NQK]
            k = p_ref[b, :, OK_:OK_ + NQK]
            z = z_ref[b]
            _, eb, enb, ekl, qi, ki, ks, decb = _chunk_fwd_parts(q, k, z, tril16)
            qi16 = qi.astype(BF16)
            ki16 = ki.astype(BF16)
            qs = _stack_heads(qi, masks).astype(BF16)
            sc = jnp.where(cmask, _nt(qs, ki16), 0.0).astype(BF16)
            st = sp_ref[b, 0]
            st16 = st.astype(BF16)
            dsn = ds_ref[b]
            dsn16 = dsn.astype(BF16)
            v16 = p_ref[b, :, OV:OV + NV].astype(BF16)
            do16 = []
            dgng = jnp.zeros((1, DV), F32)
            for h in range(HEADS):
                cols = slice(DV * h, DV * (h + 1))
                o = o_ref[b, :, cols]
                r = lax.rsqrt(jnp.mean(o * o, axis=-1, keepdims=True) + EPS)
                nh = o * r
                g = p_ref[b, :, OG + DV * h:OG + DV * (h + 1)]
                sg = jax.nn.sigmoid(g)
                dog = dm_ref[b, :, cols]
                dproj_ref[b, :, OG + DV * h:OG + DV * (h + 1)] = (
                    (dog * (nh * gg)) * (sg * (1.0 + g * (1.0 - sg)))).astype(BF16)
                don = dog * (g * sg)
                dgng = dgng + jnp.sum(don * nh, axis=0, keepdims=True)
                dn = don * gg
                do = r * (dn - nh * jnp.mean(dn * nh, axis=-1, keepdims=True))
                do16.append(do.astype(BF16))
            dgng_ref[...] += dgng
            do_rows = jnp.concatenate(do16, axis=0)
            v_rows = jnp.concatenate([v16[:, DV * h:DV * (h + 1)] for h in range(HEADS)], axis=0)
            dp16 = [jnp.where(causal, _nt(do16[h], v16[:, DV * h:DV * (h + 1)]), 0.0).astype(BF16)
                    for h in range(HEADS)]
            ks_dsn = _mm(_stack_heads(ks, masks).astype(BF16), dsn16)
            do_st = _nt(do_rows, st16)
            v_dsn = _nt(v_rows, dsn16)
            dp_ki = _mm(jnp.concatenate(dp16, axis=0), ki16)
            q_do = _tn(qi16, jnp.concatenate(do16, axis=1))
            dki_h = []
            for h in range(HEADS):
                rows = slice(CHUNK * h, CHUNK * (h + 1))
                cols = slice(DV * h, DV * (h + 1))
                dv = _tn(sc[rows], do16[h]) + ks_dsn[rows]
                dproj_ref[b, :, OV + DV * h:OV + DV * (h + 1)] = dv.astype(BF16)
                dki_h.append(_tn(dp16[h], qi16))
                ds_ref[b, rows, :] = decb[rows] * dsn[rows] + q_do[rows, cols]
            blocks = lambda a: [a[CHUNK * h:CHUNK * (h + 1)] for h in range(HEADS)]
            dqi = _merge_heads(blocks(dp_ki + do_st), masks)
            dki = _merge_heads(dki_h, masks)
            dks = _merge_heads(blocks(v_dsn), masks)
            dproj_ref[b, :, OQ:OQ + NQK] = (dqi * (Q_SCALE * eb)).astype(BF16)
            dproj_ref[b, :, OK_:OK_ + NQK] = (dki * enb + dks * ekl).astype(BF16)
            dks_ks = dks * ks
            db = dqi * qi - dki * ki - dks_ks
            sd = _split_bf16(dsn * st * decb, 2)
            dbl = jnp.sum(dks_ks, axis=0, keepdims=True) + (_nt(ones_r, sd[0]) + _nt(ones_r, sd[1]))[0:1, :]
            db = db + jnp.where(last_row, dbl, 0.0)
            db_parts = _split_bf16(db, 3)
            dla = _mm(triu16, db_parts[0]) + _mm(triu16, db_parts[1]) + _mm(triu16, db_parts[2])
            dz = (dla * INV_GATE_NORM) * (1.0 / (1.0 + jnp.exp(z)))
            dbg_ref[...] += jnp.sum(dz, axis=0, keepdims=True)
            dz16 = dz.astype(BF16)
            pa16 = p_ref[b, :, OA:OA + A_PAD].astype(BF16)
            dwgu_ref[...] += _tn(pa16, dz16)
            dproj_ref[b, :, OA:OA + A_PAD] = _nt(dz16, wgu_ref[...]).astype(BF16)
            cb = p_ref[b, :, OCB:OCB + CW]
            cc = p_ref[b, :, OCC:OCC + CW]
            ch = p_ref[b, :, OCH:OCH + CW]
            u = cc * ch
            uprev = (pprev_ref[b, :, 0:CW] * pprev_ref[b, :, CW:2 * CW]) * has_prev
            u1, u2 = _conv_taps(u, uprev)
            w0 = cw_ref[0:1, :]
            w1 = cw_ref[1:2, :]
            w2 = cw_ref[2:3, :]
            yc = w0 * u2 + w1 * u1 + w2 * u
            doc = dm_ref[b, :, NV:NV + CW]
            dproj_ref[b, :, OCB:OCB + CW] = (doc * yc).astype(BF16)
            dyc = doc * cb
            dycn = dycn_ref[b]
            row = lax.broadcasted_iota(jnp.int32, dyc.shape, 0)
            d1 = jnp.where(row >= CHUNK - 1, pltpu.roll(dycn, CHUNK - 1, 0), pltpu.roll(dyc, CHUNK - 1, 0))
            d2 = jnp.where(row >= CHUNK - 2, pltpu.roll(dycn, CHUNK - 2, 0), pltpu.roll(dyc, CHUNK - 2, 0))
            du = w2 * dyc + w1 * d1 + w0 * d2
            dproj_ref[b, :, OCC:OCC + CW] = (du * ch).astype(BF16)
            dproj_ref[b, :, OCH:OCH + CW] = (du * cc).astype(BF16)
            dcw_ref[0:1, :] += jnp.sum(dyc * u2, axis=0, keepdims=True)
            dcw_ref[1:2, :] += jnp.sum(dyc * u1, axis=0, keepdims=True)
            dcw_ref[2:3, :] += jnp.sum(dyc * u, axis=0, keepdims=True)
            dycn_ref[b] = dyc

        @pl.when(step == nc - 1)
        def _():
            copies = _stage2_copies(pb_refs, r2_refs, send_sems, recv_sems)
            for cp in copies:
                cp.wait_recv()
            for cp in copies:
                cp.wait_send()

    rev = lambda w: pl.BlockSpec((nb, CHUNK, w), lambda i: (0, nc - 1 - i, 0))
    const = lambda r, c: pl.BlockSpec((r, c), lambda i: (0, 0))
    hbm = pl.BlockSpec(memory_space=pl.ANY)
    return pl.pallas_call(
        body,
        name="mix_bwd",
        grid=(nc,),
        in_specs=[
            rev(PW),
            pl.BlockSpec((nb, CHUNK, 2 * CW), lambda i: (0, jnp.maximum(nc - 2 - i, 0), OCC // (2 * CW))),
            rev(NQK),
            pl.BlockSpec((nb, 1, NQK, DV), lambda i: (0, nc - 1 - i, 0, 0)),
            rev(NV),
            rev(D),
            const(1, DV),
            const(CONV_K, CW),
            const(A_PAD, NQK),
        ] + [hbm] * na,
        out_specs=[rev(PW), const(1, DV), const(8, CW), const(1, NQK), const(A_PAD, NQK)] + [hbm] * na,
        out_shape=[
            jax.ShapeDtypeStruct((nb, s, PW), BF16),
            jax.ShapeDtypeStruct((1, DV), F32),
            jax.ShapeDtypeStruct((8, CW), F32),
            jax.ShapeDtypeStruct((1, NQK), F32),
            jax.ShapeDtypeStruct((A_PAD, NQK), F32),
        ] + [jax.ShapeDtypeStruct((3,) + p.shape[1:], BF16) for p in pbs],
        scratch_shapes=[pltpu.VMEM((nb, NQK, DV), F32), pltpu.VMEM((nb, CHUNK, CW), F32),
                        pltpu.SemaphoreType.DMA((3 * na,)), pltpu.SemaphoreType.DMA((3 * na,))],
        compiler_params=_params(("arbitrary",)),
    )(proj3, proj3, z3, sprev, opre3, dmix3, gng, conv_w, wgu_p, *pbs)


SMALL_PACK_ROWS = 16


def _wgu_slot(r):
    return 4 + r // 4, NQK * (r % 4)


CONV_SLOTS = ((8, 0), (8, CW), (9, 0))


def _in_proj_bwd(dproj2d, x2d, dx1, g1, w_in_t, tm, pb, small_parts):
    t = x2d.shape[0]
    nt = t // tm

    def body(dp_ref, x_ref, dx1_ref, g_ref, w_ref, pb_ref, dg2, dgf, dbg, dgng, dwgu, dcw, lp,
             dx_ref, sums_ref, r2_ref, dg1_acc, pack, gbuf, pack1, gbuf1, send_sems, recv_sems,
             ssend, srecv, ssend1, srecv1):
        x, y, c = _position()
        me = 4 * x + 2 * y + c
        flips = [(k >> 2, (k >> 1) & 1, k & 1) for k in range(1, N_DEV)]
        peers = [(x ^ fx, y ^ fy, c ^ fc) for fx, fy, fc in flips]

        def small_copies(src, dst, send, recv, arrivals):
            return [pltpu.make_async_remote_copy(
                src_ref=src, dst_ref=dst.at[4 * px + 2 * py + pc if arrivals else me],
                send_sem=send.at[k], recv_sem=recv.at[k], device_id=(px, py, pc), device_id_type=MESH)
                for k, (px, py, pc) in enumerate(peers)]

        @pl.when(pl.program_id(0) == 0)
        def _():
            for cp in _stage2_copies([pb_ref], [r2_ref], send_sems, recv_sems):
                cp.start()
            dg1_acc[...] = jnp.zeros_like(dg1_acc)
            pack[...] = jnp.zeros_like(pack)
            pack[1:2, :] = dg2[...]
            pack[2:3, :] = dgf[...]
            pack[3:4, 0:NQK] = dbg[...]
            pack[3:4, NQK:NQK + DV] = dgng[...]
            pack[3:4, NQK + DV:NQK + 2 * DV] = lp[...]
            for r in range(RANK):
                row, lane = _wgu_slot(r)
                pack[row:row + 1, lane:lane + NQK] = dwgu[r:r + 1, :]
            for r, (row, lane) in enumerate(CONV_SLOTS):
                pack[row:row + 1, lane:lane + CW] = dcw[r:r + 1, :]
            for cp in small_copies(pack, gbuf, ssend, srecv, False):
                cp.start()
            gbuf[me] = pack[...]

        xv = x_ref[...]
        r = lax.rsqrt(jnp.mean(xv * xv, axis=-1, keepdims=True) + EPS)
        n1 = xv * r
        dh = _mm(dp_ref[...], w_ref[...])
        dg1_acc[...] += jnp.sum(dh * n1, axis=0, keepdims=True)
        dn = dh * g_ref[...]
        dx_ref[...] = dx1_ref[...] + r * (dn - n1 * jnp.mean(dn * n1, axis=-1, keepdims=True))

        @pl.when(pl.program_id(0) == nt - 1)
        def _():
            pack1[...] = jnp.zeros_like(pack1)
            pack1[0:1, :] = dg1_acc[...]
            for cp in small_copies(pack1, gbuf1, ssend1, srecv1, False):
                cp.start()
            gbuf1[me] = pack1[...]
            copies = _stage2_copies([pb_ref], [r2_ref], send_sems, recv_sems)
            for cp in copies:
                cp.wait_recv()
            for cp in copies:
                cp.wait_send()
            for src, dst, send, recv in ((pack, gbuf, ssend, srecv), (pack1, gbuf1, ssend1, srecv1)):
                for cp in small_copies(src, dst, send, recv, True):
                    cp.wait_recv()
                    cp.wait_send()
            acc = gbuf[0]
            acc1 = gbuf1[0]
            for d in range(1, N_DEV):
                acc = acc + gbuf[d]
                acc1 = acc1 + gbuf1[d]
            sums_ref[...] = acc
            sums_ref[0:1, :] = acc1[0:1, :]

    tile = lambda w: pl.BlockSpec((tm, w), lambda i: (i, 0))
    vec = pl.BlockSpec((1, D), lambda i: (0, 0))
    hbm = pl.BlockSpec(memory_space=pl.ANY)
    whole = lambda a: pl.BlockSpec(a.shape, lambda i: (0,) * a.ndim)
    return pl.pallas_call(
        body,
        name="in_proj_bwd",
        grid=(nt,),
        in_specs=[tile(PW), tile(D), tile(D), vec, pl.BlockSpec((PW, D), lambda i: (0, 0)), hbm]
        + [whole(a) for a in small_parts],
        out_specs=[tile(D), pl.BlockSpec((SMALL_PACK_ROWS, D), lambda i: (0, 0)), hbm],
        out_shape=[jax.ShapeDtypeStruct((t, D), F32), jax.ShapeDtypeStruct((SMALL_PACK_ROWS, D), F32),
                   jax.ShapeDtypeStruct((3,) + pb.shape[1:], BF16)],
        scratch_shapes=[pltpu.VMEM((1, D), F32),
                        pltpu.VMEM((SMALL_PACK_ROWS, D), F32), pltpu.VMEM((N_DEV, SMALL_PACK_ROWS, D), F32),
                        pltpu.VMEM((8, D), F32), pltpu.VMEM((N_DEV, 8, D), F32),
                        pltpu.SemaphoreType.DMA((3,)), pltpu.SemaphoreType.DMA((3,)),
                        pltpu.SemaphoreType.DMA((7,)), pltpu.SemaphoreType.DMA((7,)),
                        pltpu.SemaphoreType.DMA((7,)), pltpu.SemaphoreType.DMA((7,))],
        compiler_params=_params(("arbitrary",)),
    )(dproj2d, x2d, dx1, g1, w_in_t, pb, *small_parts)


def _get_rows(ref):
    return ref[:, 0, :] if len(ref.shape) == 3 else ref[...]


def _put_rows(ref, val):
    if len(ref.shape) == 3:
        ref[:, 0, :] = val
    else:
        ref[...] = val


def _adamw_math(w, g, m, v):
    m = ADAM_B1 * m + (1.0 - ADAM_B1) * g
    v = ADAM_B2 * v + (1.0 - ADAM_B2) * (g * g)
    m_hat = m / (1.0 - ADAM_B1 ** ADAM_STEP)
    v_hat = v / (1.0 - ADAM_B2 ** ADAM_STEP)
    delta = -ADAM_LR * (m_hat / (jnp.sqrt(v_hat) + ADAM_EPS) + ADAM_WD * w)
    return delta, m, v


def _position():
    return lax.axis_index("x"), lax.axis_index("y"), lax.axis_index("c")


GATHER_PARTS = 2
GATHER_SEMS = 7 * GATHER_PARTS


def _gather_copies(stage, lo, rows, gx, send_sems, recv_sems, local_sem):
    x, y, c = _position()
    me = (x, y, c)
    sibling = (x, y, 1 - c)
    chips = [(1 - x, y), (x, 1 - y), (1 - x, 1 - y)]
    part = -(-rows // (16 * GATHER_PARTS)) * 16
    bounds = [(p * part, min(part, rows - p * part)) for p in range(GATHER_PARTS)]

    def blk(px, py, pc, off, n):
        return gx.at[4 * px + 2 * py + pc, pl.ds(off, n), :]

    mine = pltpu.make_async_copy(stage.at[pl.ds(lo, rows), :], gx.at[4 * x + 2 * y + c], local_sem)
    parts = []
    for p, (off, n) in enumerate(bounds):
        def copy(k, block, to, from_stage=False, p=p, off=off, n=n):
            return pltpu.make_async_remote_copy(
                src_ref=stage.at[pl.ds(lo + off, n), :] if from_stage else blk(*block, off, n),
                dst_ref=blk(*block, off, n), send_sem=send_sems.at[7 * p + k], recv_sem=recv_sems.at[7 * p + k],
                device_id=to, device_id_type=MESH)

        first = [copy(0, me, sibling, True)] + [copy(1 + j, me, (*chip, c), True) for j, chip in enumerate(chips)]
        passed = [copy(4 + j, (*chip, c), sibling) for j, chip in enumerate(chips)]
        arrivals = ([copy(0, sibling, me)] + [copy(1 + j, (*chip, c), me) for j, chip in enumerate(chips)]
                    + [copy(4 + j, (*chip, 1 - c), me) for j, chip in enumerate(chips)])
        parts.append((first, passed, arrivals))
    return mine, parts


def _gather_start(*args):
    mine, parts = _gather_copies(*args)
    mine.start()
    for first, _, _ in parts:
        for cp in first:
            cp.start()


def _gather_finish(*args):
    mine, parts = _gather_copies(*args)
    for _, passed, arrivals in parts:
        for j in range(3):
            arrivals[1 + j].wait_recv()
            passed[j].start()
    for first, passed, arrivals in parts:
        arrivals[0].wait_recv()
        for j in range(3):
            arrivals[4 + j].wait_recv()
        for cp in first + passed:
            cp.wait_send()
    mine.wait()


def _gather_sems():
    return [pltpu.SemaphoreType.DMA((GATHER_SEMS,)), pltpu.SemaphoreType.DMA((GATHER_SEMS,)), pltpu.SemaphoreType.DMA]


def _gather_w_in(w_it, w_gt, w_ut, w_d, w_o, wgu_s, conv_s):
    def body(wi_ref, wg_ref, wu_ref, wd_ref, wo_ref, wgu_ref, conv_ref, w_ref, gwgu_ref, gconv_ref, stage,
             buf, send_sems, recv_sems, local_sem, ssend, srecv):
        x, y, c = _position()
        me = 4 * x + 2 * y + c
        stage[SLAB_IN:SLAB_IN + IN_W, :] = wi_ref[:, 0, :].astype(BF16)
        stage[SLAB_IN + IN_W:SLAB_G, :] = jnp.zeros((IN_ROWS - IN_W, D), BF16)
        args = (stage, SLAB_IN, IN_ROWS, buf, send_sems, recv_sems, local_sem)
        _gather_start(*args)
        stage[SLAB_G:SLAB_U, :] = wg_ref[...].astype(BF16)
        stage[SLAB_U:SLAB_D, :] = wu_ref[...].astype(BF16)
        stage[SLAB_D:SLAB_O, :] = wd_ref[...].astype(BF16)
        stage[SLAB_O:SLAB_ROWS, :] = wo_ref[...].astype(BF16)
        flips = [(k >> 2, (k >> 1) & 1, k & 1) for k in range(1, N_DEV)]
        peers = [(x ^ fx, y ^ fy, c ^ fc) for fx, fy, fc in flips]

        def small(k, block_id, to):
            return [pltpu.make_async_remote_copy(
                src_ref=s, dst_ref=g.at[block_id], send_sem=ssend.at[2 * k + n], recv_sem=srecv.at[2 * k + n],
                device_id=to, device_id_type=MESH)
                for n, (s, g) in enumerate(((wgu_ref, gwgu_ref), (conv_ref, gconv_ref)))]

        gwgu_ref[me] = wgu_ref[...]
        gconv_ref[me] = conv_ref[...]
        for k, peer in enumerate(peers):
            for cp in small(k, me, peer):
                cp.start()
        w_ref[IN_COLS:PW, :] = jnp.zeros((PW - IN_COLS, D), BF16)
        _gather_finish(*args)
        for k, (px, py, pc) in enumerate(peers):
            for cp in small(k, 4 * px + 2 * py + pc, (px, py, pc)):
                cp.wait_recv()
                cp.wait_send()
        for j, lo, hi, d in _in_segments():
            w_ref[d:d + hi - lo, :] = buf[j, lo:hi, :]

    vm = pl.BlockSpec(memory_space=pltpu.VMEM)
    return pl.pallas_call(
        body,
        name="gather_w_in",
        in_specs=[vm] * 7,
        out_specs=[vm] * 4,
        out_shape=[jax.ShapeDtypeStruct((PW, D), BF16),
                   jax.ShapeDtypeStruct((N_DEV,) + wgu_s.shape, F32),
                   jax.ShapeDtypeStruct((N_DEV,) + conv_s.shape, F32),
                   jax.ShapeDtypeStruct((SLAB_ROWS, D), BF16)],
        scratch_shapes=[pltpu.VMEM((N_DEV, IN_ROWS, D), BF16)] + _gather_sems()
        + [pltpu.SemaphoreType.DMA((14,)), pltpu.SemaphoreType.DMA((14,))],
        compiler_params=_params(),
    )(w_it, w_gt, w_ut, w_d, w_o, wgu_s, conv_s)


def _w_in_core_reduce(dw_t):
    def body(d_ref, own_ref, sib_ref, pb_ref, g, gb, r1, send_sems, recv_sems):
        x, y, c = _position()
        chip = 2 * x + y
        for j in range(N_DEV):
            g[j, IN_W:IN_ROWS, :] = jnp.zeros((IN_ROWS - IN_W, D), F32)
        for j, lo, hi, d in _in_segments():
            g[j, lo:hi, :] = d_ref[d:d + hi - lo, :]
        for j in range(N_DEV):
            gb[j] = g[j].astype(BF16)
        copies = _stage1_copies(gb, r1, send_sems, recv_sems)
        for cp in copies:
            cp.start()
        own_ref[0] = g[2 * chip + c]
        for cp in copies:
            cp.wait_recv()
        sib_ref[0] = r1[chip]
        for k in range(1, 4):
            t = chip ^ k
            pb_ref[k - 1] = (g[2 * t + c] + r1[t].astype(F32)).astype(BF16)
        for cp in copies:
            cp.wait_send()

    vm = pl.BlockSpec(memory_space=pltpu.VMEM)
    return pl.pallas_call(
        body,
        name="w_in_core_reduce",
        in_specs=[vm],
        out_specs=[vm, vm, vm],
        out_shape=[jax.ShapeDtypeStruct((1, IN_ROWS, D), F32), jax.ShapeDtypeStruct((1, IN_ROWS, D), BF16),
                   jax.ShapeDtypeStruct((3, IN_ROWS, D), BF16)],
        scratch_shapes=[pltpu.VMEM((N_DEV, IN_ROWS, D), F32), pltpu.VMEM((N_DEV, IN_ROWS, D), BF16),
                        pltpu.VMEM((4, IN_ROWS, D), BF16), pltpu.SemaphoreType.DMA((4,)),
                        pltpu.SemaphoreType.DMA((4,))],
        compiler_params=_params(),
    )(dw_t)


def _stage1_copies(g_ref, r_ref, send_sems, recv_sems):
    x, y, c = _position()
    return [pltpu.make_async_remote_copy(
        src_ref=g_ref.at[2 * i + 1 - c], dst_ref=r_ref.at[i], send_sem=send_sems.at[i], recv_sem=recv_sems.at[i],
        device_id=(x, y, 1 - c), device_id_type=MESH) for i in range(4)]


def _ffn_core_reduce(dw3, dwb3, dw_o, dwb_o, pos_arr):
    def body(pos_ref, g0, g1, g2, go, gb3_hbm, gbo_hbm, p0, p1, p2, po, s0, s1, s2, so,
             r1f, r1o, send_sems, recv_sems):
        k = pl.program_id(0)
        x, y, c = _position()
        chip = 2 * x + y

        def copies(p):
            src = 2 * (chip ^ ((p + 1) & 3)) + 1 - c
            pairs = [(gb3_hbm.at[a, src], r1f.at[a, p]) for a in range(3)] + [(gbo_hbm.at[src], r1o.at[p])]
            return [pltpu.make_async_remote_copy(
                src_ref=s, dst_ref=d, send_sem=send_sems.at[4 * p + a], recv_sem=recv_sems.at[4 * p + a],
                device_id=(x, y, 1 - c), device_id_type=MESH) for a, (s, d) in enumerate(pairs)]

        @pl.when(k == 0)
        def _():
            for p in range(4):
                for cp in copies(p):
                    cp.start()

        for p in range(3):
            @pl.when(k == p)
            def _():
                for cp in copies(p):
                    cp.wait_recv()

        for a, (g, pb) in enumerate(((g0, p0), (g1, p1), (g2, p2))):
            pb[...] = (g[...] + r1f[a, k][None].astype(F32)).astype(BF16)
        po[...] = (go[...] + r1o[k][None].astype(F32)).astype(BF16)

        @pl.when(k == 2)
        def _():
            for cp in copies(3):
                cp.wait_recv()
            for a, s in enumerate((s0, s1, s2)):
                s[0] = r1f[a, 3]
            so[0] = r1o[3]
            for p in range(4):
                for cp in copies(p):
                    cp.wait_send()

    other = lambda k, pos: 2 * (pos[1] ^ (k + 1)) + pos[0]
    g_spec = lambda lead: pl.BlockSpec((None, 1, FF_W, D), lambda k, pos: (lead, other(k, pos), 0, 0))
    slot = lambda rows: pl.BlockSpec((1, rows, D), lambda k, pos: (k, 0, 0))
    one = lambda rows: pl.BlockSpec((1, rows, D), lambda k, pos: (0, 0, 0))
    hbm = pl.BlockSpec(memory_space=pl.ANY)
    return pl.pallas_call(
        body,
        name="ffn_core_reduce",
        grid_spec=pltpu.PrefetchScalarGridSpec(
            num_scalar_prefetch=1, grid=(3,),
            in_specs=[g_spec(0), g_spec(1), g_spec(2),
                      pl.BlockSpec((1, OUT_ROWS, D), lambda k, pos: (other(k, pos), 0, 0)), hbm, hbm],
            out_specs=[slot(FF_W), slot(FF_W), slot(FF_W), slot(OUT_ROWS),
                       one(FF_W), one(FF_W), one(FF_W), one(OUT_ROWS)],
            scratch_shapes=[pltpu.VMEM((3, 4, FF_W, D), BF16), pltpu.VMEM((4, OUT_ROWS, D), BF16),
                            pltpu.SemaphoreType.DMA((16,)), pltpu.SemaphoreType.DMA((16,))]),
        out_shape=[jax.ShapeDtypeStruct((3, FF_W, D), BF16)] * 3 + [jax.ShapeDtypeStruct((3, OUT_ROWS, D), BF16)]
        + [jax.ShapeDtypeStruct((1, FF_W, D), BF16)] * 3 + [jax.ShapeDtypeStruct((1, OUT_ROWS, D), BF16)],
        compiler_params=_params(("arbitrary",)),
    )(pos_arr, dw3, dw3, dw3, dw_o, dwb3, dwb_o)


def _stage2_copies(p_refs, r_refs, send_sems, recv_sems):
    x, y, c = _position()
    copies = []
    for a in range(len(p_refs)):
        for k in range(1, 4):
            copies.append(pltpu.make_async_remote_copy(
                src_ref=p_refs[a].at[k - 1], dst_ref=r_refs[a].at[k - 1],
                send_sem=send_sems.at[3 * a + k - 1], recv_sem=recv_sems.at[3 * a + k - 1],
                device_id=(x ^ (k >> 1), y ^ (k & 1), c), device_id_type=MESH))
    return copies


def _finish_weights(items, pos_arr, name, nblk):
    n = len(items)
    in_specs, out_specs, out_shape, operands, wbs = [], [], [], [], []
    for g8, lead, r1, r2, w, m, v in items:
        rows, wr = g8.shape[-2], w.shape[0]
        assert rows % nblk == 0 and wr % nblk == 0 and (nblk == 1 or (rows == wr and rows % (16 * nblk) == 0))
        rb, wb = rows // nblk, wr // nblk
        if lead is not None:
            g_spec = pl.BlockSpec((None, 1, rb, D), lambda i, pos, lead=lead: (lead, 2 * pos[1] + pos[0], i, 0))
        elif g8.shape[0] == 1:
            g_spec = pl.BlockSpec((1, rb, D), lambda i, pos: (0, i, 0))
        else:
            g_spec = pl.BlockSpec((1, rb, D), lambda i, pos: (2 * pos[1] + pos[0], i, 0))
        r1_spec = pl.BlockSpec((1, rb, D), lambda i, pos: (0, i, 0))
        if w.ndim == 3:
            wblk = pl.BlockSpec((wb, 1, D), lambda i, pos: (i, 0, 0))
        else:
            wblk = pl.BlockSpec((wb, D), lambda i, pos: (i, 0))
        in_specs += [g_spec, r1_spec, pl.BlockSpec((3, rb, D), lambda i, pos: (0, i, 0)), wblk, wblk, wblk]
        out_specs += [wblk] * 4
        out_shape += [jax.ShapeDtypeStruct(w.shape, F32)] * 4
        operands += [g8, r1, r2, w, m, v]
        wbs.append(wb)

    def body(pos_ref, *refs):
        for a in range(n):
            g_ref, r1_ref, r2_ref, w_ref, m_ref, v_ref = refs[6 * a:6 * a + 6]
            g_out, d_out, m_out, v_out = refs[6 * n + 4 * a:6 * n + 4 * a + 4]
            g = g_ref[0] + r1_ref[0].astype(F32)
            for k in range(3):
                g = g + r2_ref[k].astype(F32)
            g = g[0:wbs[a], :]
            d, mn, vn = _adamw_math(_get_rows(w_ref), g, _get_rows(m_ref), _get_rows(v_ref))
            for out, val in ((g_out, g), (d_out, d), (m_out, mn), (v_out, vn)):
                _put_rows(out, val)

    return pl.pallas_call(
        body,
        name=name,
        grid_spec=pltpu.PrefetchScalarGridSpec(
            num_scalar_prefetch=1, grid=(nblk,), in_specs=in_specs, out_specs=out_specs),
        out_shape=out_shape,
        compiler_params=_params(("arbitrary",)),
    )(pos_arr, *operands)


SMALL_NAMES = ("norm1_g", "norm2_g", "norm_f_g", "b_gate", "gla_norm_g", "w_gate_up", "conv_w")
WGU_W = NQK // N_DEV
CONV_W = CW // N_DEV


def _small_adamw(sums, ws, ms, vs):
    n = len(SMALL_NAMES)

    def body(*refs):
        acc_ref = refs[0]
        w_refs, m_refs, v_refs = refs[1:1 + n], refs[1 + n:1 + 2 * n], refs[1 + 2 * n:1 + 3 * n]
        loss_ref = refs[1 + 3 * n]
        outs = refs[2 + 3 * n:]
        x, y, c = _position()
        me = 4 * x + 2 * y + c
        acc = acc_ref[...]
        loss_ref[...] = acc[3:4, NQK + DV:NQK + DV + 1]

        def my_columns(full, width):
            r = lax.broadcasted_iota(jnp.int32, (full.shape[1], width), 0)
            col = lax.broadcasted_iota(jnp.int32, (full.shape[1], width), 1)
            sel = (r == width * me + col).astype(F32)
            return _mm(full, sel, precision=HIGHEST)

        dwgu = jnp.concatenate([acc[row:row + 1, lane:lane + NQK] for row, lane in map(_wgu_slot, range(RANK))], axis=0)
        dcw = jnp.concatenate([acc[row:row + 1, lane:lane + CW] for row, lane in CONV_SLOTS], axis=0)
        grads = [acc[0:1, :], acc[1:2, :], acc[2:3, :], acc[3:4, 0:NQK], acc[3:4, NQK:NQK + DV],
                 my_columns(dwgu, WGU_W), my_columns(dcw, CONV_W)]
        for i, g in enumerate(grads):
            d, mn, vn = _adamw_math(_get_rows(w_refs[i]), g, _get_rows(m_refs[i]), _get_rows(v_refs[i]))
            for out, val in zip(outs[4 * i:4 * i + 4], (g, d, mn, vn)):
                _put_rows(out, val)

    vm = pl.BlockSpec(memory_space=pltpu.VMEM)
    out_shape = [jax.ShapeDtypeStruct((1, 1), F32)]
    for w in ws:
        out_shape += [jax.ShapeDtypeStruct(w.shape, F32)] * 4
    return pl.pallas_call(
        body,
        name="small_adamw",
        in_specs=[vm] * (1 + 3 * n),
        out_specs=[vm] * (1 + 4 * n),
        out_shape=out_shape,
        compiler_params=_params(),
    )(sums, *ws, *ms, *vs)


def kernel(x, norm1_g, w_in, w_gate_up, b_gate, gla_norm_g, conv_w, w_out, norm2_g, w_ffn_gate, w_ffn_up, w_ffn_down, norm_f_g, loss_target, m_norm1_g, m_w_in, m_w_gate_up, m_b_gate, m_gla_norm_g, m_conv_w, m_w_out, m_norm2_g, m_w_ffn_gate, m_w_ffn_up, m_w_ffn_down, m_norm_f_g, v_norm1_g, v_w_in, v_w_gate_up, v_b_gate, v_gla_norm_g, v_conv_w, v_w_out, v_norm2_g, v_w_ffn_gate, v_w_ffn_up, v_w_ffn_down, v_norm_f_g):
    xi, yi, ci = _position()
    pos_arr = jnp.stack([ci, 2 * xi + yi]).astype(jnp.int32)
    nb, s, _ = x.shape
    t = nb * s

    tr = lambda a: a[0].T
    rows_of = lambda a: a.transpose(2, 0, 1)
    conv_rows = lambda a: a.transpose(1, 0, 2)
    w_in_t, gwgu, gconv, stage = _gather_w_in(rows_of(w_in), tr(w_ffn_gate), tr(w_ffn_up), w_ffn_down[0], w_out[0],
                                              w_gate_up[0], conv_rows(conv_w))
    wgu_f = gwgu.transpose(1, 0, 2).reshape(RANK, NQK)
    conv_f = gconv.transpose(1, 2, 0, 3).reshape(CONV_K, CW)
    wgu_p = jnp.concatenate([wgu_f, jnp.zeros((A_PAD - RANK, NQK), F32)], axis=0).astype(BF16)

    x2d = x.reshape(t, D)
    tgt2d = loss_target.reshape(t, D)
    tm = 256
    tm_in = min(512, t)
    tk = min(2048, t)
    proj, z, h, gwb = _in_proj_fwd(x2d, norm1_g, w_in_t, wgu_p, b_gate, tm_in, stage)
    proj3 = proj.reshape(nb, s, PW)
    z3 = z.reshape(nb, s, NQK)
    mix3, opre3, sprev, x1, gwa = _mix_fwd(proj3, z3, gla_norm_g, conv_f, stage, x, gwb)
    mix2d = mix3.reshape(t, D)
    dx1, dx1b, dmix, adu, hb, dg2, dgf, loss_part = _ffn_fwd_bwd(
        x1.reshape(t, D), tgt2d, gwa, gwb, norm2_g, norm_f_g.reshape(1, D), tm)
    dw3, dwb3 = _dw_ffn(adu, hb, tk)
    dw3 = dw3.reshape(3, N_DEV, FF_W, D)
    dw_o, dwb_o = _tn_matmul(mix2d, dx1b, D // 2, D, tk, "dw_out", True)
    dw_o = dw_o.reshape(N_DEV, OUT_ROWS, D)
    *pb, sib_d, sib_g, sib_u, sib_o = _ffn_core_reduce(
        dw3, dwb3.reshape(3, N_DEV, FF_W, D), dw_o, dwb_o.reshape(N_DEV, OUT_ROWS, D), pos_arr)
    g8 = [dw3, dw3, dw3, dw_o]
    leads = [0, 1, 2, None]
    tags = ("w_ffn_down", "w_ffn_gate", "w_ffn_up", "w_out")
    r1 = [sib_d, sib_g, sib_u, sib_o]
    mb = _mix_bwd(proj3, z3, sprev, opre3, dmix.reshape(nb, s, D), gla_norm_g, conv_f, wgu_p, [pb[0], pb[1], pb[3]])
    dproj3, dgng, dcw, dbg, dwgu = mb[:5]
    dproj2d = dproj3.reshape(t, PW)
    dw_in_t, r2_up = _tn_matmul(dproj2d, h, PW // 5, D, tk, "dw_in", False, _stage2_rider([pb[2]]))
    r2 = [mb[5], mb[6], r2_up, mb[7]]
    g_in, r1_in, pb_in = _w_in_core_reduce(dw_in_t)
    dx, small_sums, r2_in = _in_proj_bwd(dproj2d, x2d, dx1, norm1_g, w_in_t, tm_in, pb_in,
                                         (dg2, dgf, dbg, dgng, dwgu, dcw, loss_part))

    tags = ("w_in",) + tags
    g8 = [g_in] + g8
    leads = [None] + leads
    r1 = [r1_in] + list(r1)
    r2 = [r2_in] + r2
    shard_w = (rows_of(w_in), w_ffn_down[0], tr(w_ffn_gate), tr(w_ffn_up), w_out[0])
    shard_m = (rows_of(m_w_in), m_w_ffn_down[0], tr(m_w_ffn_gate), tr(m_w_ffn_up), m_w_out[0])
    shard_v = (rows_of(v_w_in), v_w_ffn_down[0], tr(v_w_ffn_gate), tr(v_w_ffn_up), v_w_out[0])
    back = (lambda o: o.transpose(1, 2, 0), lambda o: o[None], lambda o: o.T[None], lambda o: o.T[None],
            lambda o: o[None])
    items = list(zip(g8, leads, r1, r2, shard_w, shard_m, shard_v))
    flat = list(_finish_weights(items[1:], pos_arr, "finish_ffn_out", 2))
    flat = list(_finish_weights(items[:1], pos_arr, "finish_w_in", 1)) + flat
    results = {}
    for i, (tag, to_shard) in enumerate(zip(tags, back)):
        results[tag] = [to_shard(o) for o in flat[4 * i:4 * i + 4]]

    small_w = (norm1_g, norm2_g, norm_f_g.reshape(1, D), b_gate, gla_norm_g, w_gate_up[0], conv_rows(conv_w))
    small_m = (m_norm1_g, m_norm2_g, m_norm_f_g.reshape(1, D), m_b_gate, m_gla_norm_g, m_w_gate_up[0],
               conv_rows(m_conv_w))
    small_v = (v_norm1_g, v_norm2_g, v_norm_f_g.reshape(1, D), v_b_gate, v_gla_norm_g, v_w_gate_up[0],
               conv_rows(v_conv_w))
    so = _small_adamw(small_sums, small_w, small_m, small_v)
    loss = so[0].reshape(())
    to_shape = {"norm_f_g": lambda o: o.reshape(D), "w_gate_up": lambda o: o[None],
                "conv_w": lambda o: o.transpose(1, 0, 2)}
    for i, name in enumerate(SMALL_NAMES):
        results[name] = [to_shape.get(name, lambda o: o)(o) for o in so[1 + 4 * i:5 + 4 * i]]

    names = ("norm1_g", "w_in", "w_gate_up", "b_gate", "gla_norm_g", "conv_w", "w_out", "norm2_g",
             "w_ffn_gate", "w_ffn_up", "w_ffn_down", "norm_f_g")
    outs = [loss, dx.reshape(nb, s, D)]
    for kind in range(4):
        for name in names:
            outs.append(results[name][kind])
    return tuple(outs)
```

```python
import jax
import jax.numpy as jnp
from jax import lax
from jax.experimental import pallas as pl
from jax.experimental.pallas import tpu as pltpu

F32 = jnp.float32
BF16 = jnp.bfloat16
HIGHEST = lax.Precision.HIGHEST
MESH = pl.DeviceIdType.MESH

N_DEV = 8
D = 1024
DFF = 2816
HEADS = 4
DK = 64
DV = 128
NQK = HEADS * DK
NV = HEADS * DV
RANK = 16
CHUNK = 64
CW = 512
CONV_K = 3
IN_COLS = 3088
EPS = 1e-6
INV_GATE_NORM = 1.0 / 16.0
Q_SCALE = DK ** -0.5

PW = 3200
OQ, OK_, OV, OG, OCB, OCC, OCH, OA = 0, 256, 512, 1024, 1536, 2048, 2560, 3072
A_PAD = 128

ADAM_LR = 0.001
ADAM_B1 = 0.9
ADAM_B2 = 0.999
ADAM_EPS = 1e-08
ADAM_WD = 0.01
ADAM_STEP = 10

IN_W = IN_COLS // N_DEV
IN_ROWS = 400
FF_W = DFF // N_DEV
OUT_ROWS = D // N_DEV
SLAB_IN = 0
SLAB_G = SLAB_IN + IN_ROWS
SLAB_U = SLAB_G + FF_W
SLAB_D = SLAB_U + FF_W
SLAB_O = SLAB_D + FF_W
SLAB_ROWS = SLAB_O + OUT_ROWS

VMEM_LIMIT = 56 * 1024 * 1024


def _params(sem=None, vmem=VMEM_LIMIT):
    return pltpu.CompilerParams(dimension_semantics=sem, vmem_limit_bytes=vmem)


def _nt(a, b):
    return lax.dot_general(a, b, (((1,), (1,)), ((), ())), preferred_element_type=F32)


def _tn(a, b, precision=None):
    return lax.dot_general(a, b, (((0,), (0,)), ((), ())), preferred_element_type=F32, precision=precision)


def _mm(a, b, precision=None):
    return jnp.dot(a, b, preferred_element_type=F32, precision=precision)


def _in_segments():
    segs = []
    for j in range(N_DEV):
        lo, hi = IN_W * j, IN_W * (j + 1)
        cuts = sorted({lo, hi} | {c for c in (OCB, OCB + RANK) if lo < c < hi})
        for a, b in zip(cuts[:-1], cuts[1:]):
            if a < OCB:
                d = a
            elif a < OCB + RANK:
                d = OA + (a - OCB)
            else:
                d = a - RANK
            segs.append((j, a - lo, b - lo, d))
    return segs


def _in_proj_fwd(x2d, g1, w_in_t, wgu_p, b_gate, tm, stage):
    t = x2d.shape[0]
    nt = t // tm
    g_rows = SLAB_ROWS - SLAB_D

    def body(x_ref, g_ref, w_ref, wgu_ref, bg_ref, stage_hbm, proj_ref, z_ref, h_ref, gwb_ref,
             send_sems, recv_sems, local_sem):
        gargs = (stage_hbm, SLAB_D, g_rows, gwb_ref, send_sems, recv_sems, local_sem)

        @pl.when(pl.program_id(0) == 0)
        def _():
            _gather_start(*gargs)

        x = x_ref[...]
        r = lax.rsqrt(jnp.mean(x * x, axis=-1, keepdims=True) + EPS)
        h = ((x * r) * g_ref[...]).astype(BF16)
        h_ref[...] = h
        proj = _nt(h, w_ref[...])
        proj_ref[...] = proj
        pa = proj[:, OA:OA + A_PAD].astype(BF16)
        z_ref[...] = _mm(pa, wgu_ref[...]) + bg_ref[...]

        @pl.when(pl.program_id(0) == nt - 1)
        def _():
            _gather_finish(*gargs)

    return pl.pallas_call(
        body,
        name="in_proj_fwd",
        grid=(t // tm,),
        in_specs=[
            pl.BlockSpec((tm, D), lambda i: (i, 0)),
            pl.BlockSpec((1, D), lambda i: (0, 0)),
            pl.BlockSpec((PW, D), lambda i: (0, 0)),
            pl.BlockSpec((A_PAD, NQK), lambda i: (0, 0)),
            pl.BlockSpec((1, NQK), lambda i: (0, 0)),
            pl.BlockSpec(memory_space=pl.ANY),
        ],
        out_specs=[
            pl.BlockSpec((tm, PW), lambda i: (i, 0)),
            pl.BlockSpec((tm, NQK), lambda i: (i, 0)),
            pl.BlockSpec((tm, D), lambda i: (i, 0)),
            pl.BlockSpec(memory_space=pl.ANY),
        ],
        out_shape=[
            jax.ShapeDtypeStruct((t, PW), F32),
            jax.ShapeDtypeStruct((t, NQK), F32),
            jax.ShapeDtypeStruct((t, D), BF16),
            jax.ShapeDtypeStruct((N_DEV, g_rows, D), BF16),
        ],
        scratch_shapes=_gather_sems(),
        compiler_params=_params(("arbitrary",)),
    )(x2d, g1, w_in_t, wgu_p, b_gate, stage)


def _head_masks():
    lane = lax.broadcasted_iota(jnp.int32, (1, NQK), 1)
    return [(lane >= DK * h) & (lane < DK * (h + 1)) for h in range(HEADS)]


def _split_bf16(x, n):
    parts = []
    for _ in range(n):
        p = x.astype(BF16)
        parts.append(p)
        x = x - p.astype(F32)
    return parts


def _chunk_fwd_parts(q, k, z, tril16):
    la = (jnp.minimum(z, 0.0) - jnp.log1p(jnp.exp(-jnp.abs(z)))) * INV_GATE_NORM
    la_parts = _split_bf16(la, 3)
    bc = _mm(tril16, la_parts[0]) + _mm(tril16, la_parts[1]) + _mm(tril16, la_parts[2])
    bl = bc[CHUNK - 1:CHUNK, :]
    eb = jnp.exp(bc)
    enb = jnp.exp(-bc)
    ekl = jnp.exp(bl - bc)
    qi = (q * Q_SCALE) * eb
    ki = k * enb
    ks = k * ekl
    ones16 = jnp.ones((CHUNK, DV), BF16)
    decb = jnp.exp(_tn(la_parts[0], ones16) + _tn(la_parts[1], ones16) + _tn(la_parts[2], ones16))
    return la, eb, enb, ekl, qi, ki, ks, decb


def _stack_heads(a, masks):
    return jnp.concatenate([jnp.where(m, a, 0.0) for m in masks], axis=0)


def _merge_heads(blocks, masks):
    out = blocks[HEADS - 1]
    for h in range(HEADS - 2, -1, -1):
        out = jnp.where(masks[h], blocks[h], out)
    return out


def _causal_stack_mask():
    row = lax.broadcasted_iota(jnp.int32, (HEADS * CHUNK, CHUNK), 0)
    col = lax.broadcasted_iota(jnp.int32, (HEADS * CHUNK, CHUNK), 1)
    return (row & (CHUNK - 1)) >= col


def _conv_taps(u, uprev):
    row = lax.broadcasted_iota(jnp.int32, u.shape, 0)
    u1 = jnp.where(row < 1, pltpu.roll(uprev, 1, 0), pltpu.roll(u, 1, 0))
    u2 = jnp.where(row < 2, pltpu.roll(uprev, 2, 0), pltpu.roll(u, 2, 0))
    return u1, u2


def _mix_fwd(proj3, z3, gng, conv_w, stage, x3, gwb):
    nb, s, _ = proj3.shape
    nc = s // CHUNK
    g_rows = SLAB_D - SLAB_G

    def body(p_ref, z_ref, gng_ref, cw_ref, stage_hbm, x_ref, gwb_hbm, mix_ref, o_ref, sprev_ref, x1_ref, gwa_ref,
             s_ref, uprev_ref, wo, wsem, send_sems, recv_sems, local_sem):
        n = pl.program_id(0)
        gargs = (stage_hbm, SLAB_G, g_rows, gwa_ref, send_sems, recv_sems, local_sem)

        @pl.when(n == 0)
        def _():
            _gather_start(*gargs)
            loads = [pltpu.make_async_copy(gwb_hbm.at[j, pl.ds(FF_W, OUT_ROWS), :],
                                           wo.at[pl.ds(OUT_ROWS * j, OUT_ROWS), :], wsem.at[j]) for j in range(N_DEV)]
            for cp in loads:
                cp.start()
            s_ref[...] = jnp.zeros_like(s_ref)
            uprev_ref[...] = jnp.zeros_like(uprev_ref)
            for cp in loads:
                cp.wait()

        r_i = lax.broadcasted_iota(jnp.int32, (CHUNK, CHUNK), 0)
        c_i = lax.broadcasted_iota(jnp.int32, (CHUNK, CHUNK), 1)
        tril16 = (r_i >= c_i).astype(BF16)
        masks = _head_masks()
        cmask = _causal_stack_mask()
        gg = gng_ref[...]
        for b in range(nb):
            q = p_ref[b, :, OQ:OQ + NQK]
            k = p_ref[b, :, OK_:OK_ + NQK]
            _, _, _, _, qi, ki, ks, decb = _chunk_fwd_parts(q, k, z_ref[b], tril16)
            qs = _stack_heads(qi, masks).astype(BF16)
            sc = jnp.where(cmask, _nt(qs, ki.astype(BF16)), 0.0).astype(BF16)
            st = s_ref[b]
            sprev_ref[b, 0] = st
            o_inter = _mm(qs, st.astype(BF16))
            v16 = p_ref[b, :, OV:OV + NV].astype(BF16)
            kv = _tn(ks.astype(BF16), v16)
            for h in range(HEADS):
                rows = slice(CHUNK * h, CHUNK * (h + 1))
                cols = slice(DV * h, DV * (h + 1))
                o = _mm(sc[rows], v16[:, cols]) + o_inter[rows]
                o_ref[b, :, cols] = o
                r = lax.rsqrt(jnp.mean(o * o, axis=-1, keepdims=True) + EPS)
                on = (o * r) * gg
                g = p_ref[b, :, OG + DV * h:OG + DV * (h + 1)]
                mix_ref[b, :, cols] = (on * (g * jax.nn.sigmoid(g))).astype(BF16)
                s_ref[b, rows, :] = decb[rows] * st[rows] + kv[rows, cols]
            u = p_ref[b, :, OCC:OCC + CW] * p_ref[b, :, OCH:OCH + CW]
            u1, u2 = _conv_taps(u, uprev_ref[b])
            yc = cw_ref[0:1, :] * u2 + cw_ref[1:2, :] * u1 + cw_ref[2:3, :] * u
            mix_ref[b, :, NV:NV + CW] = (p_ref[b, :, OCB:OCB + CW] * yc).astype(BF16)
            uprev_ref[b] = u
        mixed = _mm(jnp.concatenate([mix_ref[b] for b in range(nb)], axis=0), wo[...])
        for b in range(nb):
            x1_ref[b] = x_ref[b] + mixed[CHUNK * b:CHUNK * (b + 1)]

        @pl.when(n == nc - 1)
        def _():
            _gather_finish(*gargs)

    return pl.pallas_call(
        body,
        name="mix_fwd",
        grid=(nc,),
        in_specs=[
            pl.BlockSpec((nb, CHUNK, PW), lambda n: (0, n, 0)),
            pl.BlockSpec((nb, CHUNK, NQK), lambda n: (0, n, 0)),
            pl.BlockSpec((1, DV), lambda n: (0, 0)),
            pl.BlockSpec((CONV_K, CW), lambda n: (0, 0)),
            pl.BlockSpec(memory_space=pl.ANY),
            pl.BlockSpec((nb, CHUNK, D), lambda n: (0, n, 0)),
            pl.BlockSpec(memory_space=pl.ANY),
        ],
        out_specs=[
            pl.BlockSpec((nb, CHUNK, D), lambda n: (0, n, 0)),
            pl.BlockSpec((nb, CHUNK, NV), lambda n: (0, n, 0)),
            pl.BlockSpec((nb, 1, NQK, DV), lambda n: (0, n, 0, 0)),
            pl.BlockSpec((nb, CHUNK, D), lambda n: (0, n, 0)),
            pl.BlockSpec(memory_space=pl.ANY),
        ],
        out_shape=[
            jax.ShapeDtypeStruct((nb, s, D), BF16),
            jax.ShapeDtypeStruct((nb, s, NV), F32),
            jax.ShapeDtypeStruct((nb, nc, NQK, DV), F32),
            jax.ShapeDtypeStruct((nb, s, D), F32),
            jax.ShapeDtypeStruct((N_DEV, g_rows, D), BF16),
        ],
        scratch_shapes=[pltpu.VMEM((nb, NQK, DV), F32), pltpu.VMEM((nb, CHUNK, CW), F32),
                        pltpu.VMEM((D, D), BF16), pltpu.SemaphoreType.DMA((N_DEV,))] + _gather_sems(),
        compiler_params=_params(("arbitrary",)),
    )(proj3, z3, gng, conv_w, stage, x3, gwb)


def _ffn_fwd_bwd(x1_2d, tgt2d, gwa, gwb, g2, gf, tm):
    t = x1_2d.shape[0]

    def body(x1_ref, tgt_ref, g2_ref, gf_ref, gwa_hbm, gwb_hbm,
             dx1_ref, dx1b_ref, adu_ref, hb_ref, dg2_ref, dgf_ref, loss_ref,
             wg, wu, wd, wsem):
        i = pl.program_id(0)

        def weight_copies(n, dst, src, off, rows):
            return [pltpu.make_async_copy(src.at[j, pl.ds(off, rows), :], dst.at[pl.ds(rows * j, rows), :],
                                          wsem.at[N_DEV * n + j]) for j in range(N_DEV)]

        loads = (weight_copies(0, wg, gwa_hbm, 0, FF_W), weight_copies(1, wu, gwa_hbm, FF_W, FF_W),
                 weight_copies(2, wd, gwb_hbm, 0, FF_W))

        @pl.when(i == 0)
        def _():
            for group in loads:
                for cp in group:
                    cp.start()
            dg2_ref[...] = jnp.zeros_like(dg2_ref)
            dgf_ref[...] = jnp.zeros_like(dgf_ref)
            loss_ref[...] = jnp.zeros_like(loss_ref)
            for group in loads:
                for cp in group:
                    cp.wait()

        g2v = g2_ref[...]
        gfv = gf_ref[...]
        x1 = x1_ref[...]
        r2 = lax.rsqrt(jnp.mean(x1 * x1, axis=-1, keepdims=True) + EPS)
        n2 = x1 * r2
        h2 = (n2 * g2v).astype(BF16)
        hb_ref[1] = h2
        gate = _nt(h2, wg[...])
        up = _nt(h2, wu[...])
        sg = jax.nn.sigmoid(gate)
        sil = gate * sg
        act = (sil * up).astype(BF16)
        adu_ref[0] = act
        x2 = x1 + _mm(act, wd[...])
        rf = lax.rsqrt(jnp.mean(x2 * x2, axis=-1, keepdims=True) + EPS)
        nf = x2 * rf
        err = nf * gfv - tgt_ref[...]
        loss_ref[...] += 0.5 * jnp.sum(jnp.mean(err * err, axis=-1, keepdims=True))
        dy = err * (1.0 / D)
        dgf_ref[...] += jnp.sum(dy * nf, axis=0, keepdims=True)
        dnf = dy * gfv
        dx2 = rf * (dnf - nf * jnp.mean(dnf * nf, axis=-1, keepdims=True))
        dx2b = dx2.astype(BF16)
        hb_ref[0] = dx2b
        dact = _nt(dx2b, wd[...])
        dup = (dact * sil).astype(BF16)
        dgate = ((dact * up) * (sg * (1.0 + gate * (1.0 - sg)))).astype(BF16)
        adu_ref[2] = dup
        adu_ref[1] = dgate
        dh2 = _mm(dgate, wg[...]) + _mm(dup, wu[...])
        dg2_ref[...] += jnp.sum(dh2 * n2, axis=0, keepdims=True)
        dn2 = dh2 * g2v
        dx1 = dx2 + r2 * (dn2 - n2 * jnp.mean(dn2 * n2, axis=-1, keepdims=True))
        dx1_ref[...] = dx1
        dx1b_ref[...] = dx1.astype(BF16)

    tile = lambda w: pl.BlockSpec((tm, w), lambda i: (i, 0))
    vec = pl.BlockSpec((1, D), lambda i: (0, 0))
    hbm = pl.BlockSpec(memory_space=pl.ANY)
    return pl.pallas_call(
        body,
        name="ffn_fwd_bwd",
        grid=(t // tm,),
        in_specs=[tile(D), tile(D), vec, vec, hbm, hbm],
        out_specs=[tile(D), tile(D), pl.BlockSpec((3, tm, DFF), lambda i: (0, i, 0)),
                   pl.BlockSpec((2, tm, D), lambda i: (0, i, 0)), vec, vec,
                   pl.BlockSpec((1, 128), lambda i: (0, 0))],
        out_shape=[
            jax.ShapeDtypeStruct((t, D), F32),
            jax.ShapeDtypeStruct((t, D), BF16),
            jax.ShapeDtypeStruct((3, t, DFF), BF16),
            jax.ShapeDtypeStruct((2, t, D), BF16),
            jax.ShapeDtypeStruct((1, D), F32),
            jax.ShapeDtypeStruct((1, D), F32),
            jax.ShapeDtypeStruct((1, 128), F32),
        ],
        scratch_shapes=[pltpu.VMEM((DFF, D), BF16), pltpu.VMEM((DFF, D), BF16), pltpu.VMEM((DFF, D), BF16),
                        pltpu.SemaphoreType.DMA((3 * N_DEV,))],
        compiler_params=_params(("arbitrary",)),
    )(x1_2d, tgt2d, g2, gf, gwa, gwb)


def _stage2_rider(pbs):
    return dict(inputs=list(pbs), out_shape=[jax.ShapeDtypeStruct(p.shape, BF16) for p in pbs], nsem=3 * len(pbs),
                copies=_stage2_copies)


def _tn_matmul(a, b, bm, bn, tk, name, with_bf16, rider=None):
    t, m = a.shape
    n = b.shape[1]
    nk = t // tk
    nout = 2 if with_bf16 else 1
    grid = (m // bm, n // bn, nk)
    r_in = [] if rider is None else rider["inputs"]
    r_out = [] if rider is None else rider["out_shape"]

    def body(a_ref, b_ref, *rest):
        ins, outs = rest[:len(r_in)], rest[len(r_in):len(r_in) + nout]
        r_outs, sems = rest[len(r_in) + nout:len(r_in) + nout + len(r_out)], rest[len(r_in) + nout + len(r_out):]
        o_ref = outs[0]
        i, j, k = pl.program_id(0), pl.program_id(1), pl.program_id(2)
        if rider is not None:
            @pl.when((i == 0) & (j == 0) & (k == 0))
            def _():
                for cp in rider["copies"](ins, r_outs, *sems):
                    cp.start()

        @pl.when(k == 0)
        def _():
            o_ref[...] = jnp.zeros_like(o_ref)

        o_ref[...] += _tn(a_ref[...].astype(BF16), b_ref[...].astype(BF16))
        if with_bf16:
            @pl.when(k == nk - 1)
            def _():
                outs[1][...] = o_ref[...].astype(BF16)
        if rider is not None:
            @pl.when((i == grid[0] - 1) & (j == grid[1] - 1) & (k == nk - 1))
            def _():
                copies = rider["copies"](ins, r_outs, *sems)
                for cp in copies:
                    cp.wait_recv()
                for cp in copies:
                    cp.wait_send()

    out_blk = pl.BlockSpec((bm, bn), lambda i, j, k: (i, j))
    hbm = pl.BlockSpec(memory_space=pl.ANY)
    out_shape = [jax.ShapeDtypeStruct((m, n), F32)] + ([jax.ShapeDtypeStruct((m, n), BF16)] if with_bf16 else [])
    res = pl.pallas_call(
        body,
        name=name,
        grid=grid,
        in_specs=[pl.BlockSpec((tk, bm), lambda i, j, k: (k, i)), pl.BlockSpec((tk, bn), lambda i, j, k: (k, j))]
        + [hbm] * len(r_in),
        out_specs=[out_blk] * nout + [hbm] * len(r_out),
        out_shape=out_shape + list(r_out),
        scratch_shapes=([] if rider is None else
                        [pltpu.SemaphoreType.DMA((rider["nsem"],)), pltpu.SemaphoreType.DMA((rider["nsem"],))]),
        compiler_params=_params(("parallel", "parallel", "arbitrary") if rider is None
                                else ("arbitrary", "arbitrary", "arbitrary")),
    )(a, b, *r_in)
    return res[0] if len(res) == 1 else res


def _dw_ffn(adu, hb, tk):
    _, t, _ = adu.shape
    bm = DFF // 2
    nk = t // tk

    def body(a_ref, b_ref, o_ref, ob_ref):
        k = pl.program_id(2)

        @pl.when(k == 0)
        def _():
            o_ref[...] = jnp.zeros_like(o_ref)

        o_ref[...] += _tn(a_ref[...], b_ref[...])

        @pl.when(k == nk - 1)
        def _():
            ob_ref[...] = o_ref[...].astype(BF16)

    out_blk = pl.BlockSpec((None, bm, D), lambda p, i, k: (p, i, 0))
    return pl.pallas_call(
        body,
        name="dw_ffn",
        grid=(3, DFF // bm, nk),
        in_specs=[pl.BlockSpec((None, tk, bm), lambda p, i, k: (p, k, i)),
                  pl.BlockSpec((None, tk, D), lambda p, i, k: (jnp.minimum(p, 1), k, 0))],
        out_specs=[out_blk, out_blk],
        out_shape=[jax.ShapeDtypeStruct((3, DFF, D), F32), jax.ShapeDtypeStruct((3, DFF, D), BF16)],
        compiler_params=_params(("arbitrary", "arbitrary", "arbitrary")),
    )(adu, hb)


def _mix_bwd(proj3, z3, sprev, opre3, dmix3, gng, conv_w, wgu_p, pbs):
    nb, s, _ = proj3.shape
    nc = s // CHUNK
    na = len(pbs)

    def body(*refs):
        (p_ref, pprev_ref, z_ref, sp_ref, o_ref, dm_ref, gng_ref, cw_ref, wgu_ref) = refs[:9]
        pb_refs = refs[9:9 + na]
        (dproj_ref, dgng_ref, dcw_ref, dbg_ref, dwgu_ref) = refs[9 + na:14 + na]
        r2_refs = refs[14 + na:14 + 2 * na]
        ds_ref, dycn_ref, send_sems, recv_sems = refs[14 + 2 * na:]
        step = pl.program_id(0)
        n = nc - 1 - step

        @pl.when(step == 0)
        def _():
            for cp in _stage2_copies(pb_refs, r2_refs, send_sems, recv_sems):
                cp.start()
            ds_ref[...] = jnp.zeros_like(ds_ref)
            dycn_ref[...] = jnp.zeros_like(dycn_ref)
            dgng_ref[...] = jnp.zeros_like(dgng_ref)
            dcw_ref[...] = jnp.zeros_like(dcw_ref)
            dbg_ref[...] = jnp.zeros_like(dbg_ref)
            dwgu_ref[...] = jnp.zeros_like(dwgu_ref)

        r_i = lax.broadcasted_iota(jnp.int32, (CHUNK, CHUNK), 0)
        c_i = lax.broadcasted_iota(jnp.int32, (CHUNK, CHUNK), 1)
        tril16 = (r_i >= c_i).astype(BF16)
        triu16 = (r_i <= c_i).astype(BF16)
        causal = r_i >= c_i
        masks = _head_masks()
        cmask = _causal_stack_mask()
        gg = gng_ref[...]
        last_row = lax.broadcasted_iota(jnp.int32, (CHUNK, NQK), 0) == CHUNK - 1
        ones_r = jnp.ones((16, DV), BF16)
        has_prev = (n > 0).astype(F32)
        for b in range(nb):
            q = p_ref[b, :, OQ:OQ + NQK]
            k = p_ref[b, :, OK_:OK_ + NQK]
            z = z_ref[b]
            _, eb, enb, ekl, qi, ki, ks, decb = _chunk_fwd_parts(q, k, z, tril16)
            qi16 = qi.astype(BF16)
            ki16 = ki.astype(BF16)
            qs = _stack_heads(qi, masks).astype(BF16)
            sc = jnp.where(cmask, _nt(qs, ki16), 0.0).astype(BF16)
            st = sp_ref[b, 0]
            st16 = st.astype(BF16)
            dsn = ds_ref[b]
            dsn16 = dsn.astype(BF16)
            v16 = p_ref[b, :, OV:OV + NV].astype(BF16)
            do16 = []
            dgng = jnp.zeros((1, DV), F32)
            for h in range(HEADS):
                cols = slice(DV * h, DV * (h + 1))
                o = o_ref[b, :, cols]
                r = lax.rsqrt(jnp.mean(o * o, axis=-1, keepdims=True) + EPS)
                nh = o * r
                g = p_ref[b, :, OG + DV * h:OG + DV * (h + 1)]
                sg = jax.nn.sigmoid(g)
                dog = dm_ref[b, :, cols]
                dproj_ref[b, :, OG + DV * h:OG + DV * (h + 1)] = (
                    (dog * (nh * gg)) * (sg * (1.0 + g * (1.0 - sg)))).astype(BF16)
                don = dog * (g * sg)
                dgng = dgng + jnp.sum(don * nh, axis=0, keepdims=True)
                dn = don * gg
                do = r * (dn - nh * jnp.mean(dn * nh, axis=-1, keepdims=True))
                do16.append(do.astype(BF16))
            dgng_ref[...] += dgng
            do_rows = jnp.concatenate(do16, axis=0)
            v_rows = jnp.concatenate([v16[:, DV * h:DV * (h + 1)] for h in range(HEADS)], axis=0)
            dp16 = [jnp.where(causal, _nt(do16[h], v16[:, DV * h:DV * (h + 1)]), 0.0).astype(BF16)
                    for h in range(HEADS)]
            ks_dsn = _mm(_stack_heads(ks, masks).astype(BF16), dsn16)
            do_st = _nt(do_rows, st16)
            v_dsn = _nt(v_rows, dsn16)
            dp_ki = _mm(jnp.concatenate(dp16, axis=0), ki16)
            q_do = _tn(qi16, jnp.concatenate(do16, axis=1))
            dki_h = []
            for h in range(HEADS):
                rows = slice(CHUNK * h, CHUNK * (h + 1))
                cols = slice(DV * h, DV * (h + 1))
                dv = _tn(sc[rows], do16[h]) + ks_dsn[rows]
                dproj_ref[b, :, OV + DV * h:OV + DV * (h + 1)] = dv.astype(BF16)
                dki_h.append(_tn(dp16[h], qi16))
                ds_ref[b, rows, :] = decb[rows] * dsn[rows] + q_do[rows, cols]
            blocks = lambda a: [a[CHUNK * h:CHUNK * (h + 1)] for h in range(HEADS)]
            dqi = _merge_heads(blocks(dp_ki + do_st), masks)
            dki = _merge_heads(dki_h, masks)
            dks = _merge_heads(blocks(v_dsn), masks)
            dproj_ref[b, :, OQ:OQ + NQK] = (dqi * (Q_SCALE * eb)).astype(BF16)
            dproj_ref[b, :, OK_:OK_ + NQK] = (dki * enb + dks * ekl).astype(BF16)
            dks_ks = dks * ks
            db = dqi * qi - dki * ki - dks_ks
            sd = _split_bf16(dsn * st * decb, 2)
            dbl = jnp.sum(dks_ks, axis=0, keepdims=True) + (_nt(ones_r, sd[0]) + _nt(ones_r, sd[1]))[0:1, :]
            db = db + jnp.where(last_row, dbl, 0.0)
            db_parts = _split_bf16(db, 3)
            dla = _mm(triu16, db_parts[0]) + _mm(triu16, db_parts[1]) + _mm(triu16, db_parts[2])
            dz = (dla * INV_GATE_NORM) * (1.0 / (1.0 + jnp.exp(z)))
            dbg_ref[...] += jnp.sum(dz, axis=0, keepdims=True)
            dz16 = dz.astype(BF16)
            pa16 = p_ref[b, :, OA:OA + A_PAD].astype(BF16)
            dwgu_ref[...] += _tn(pa16, dz16)
            dproj_ref[b, :, OA:OA + A_PAD] = _nt(dz16, wgu_ref[...]).astype(BF16)
            cb = p_ref[b, :, OCB:OCB + CW]
            cc = p_ref[b, :, OCC:OCC + CW]
            ch = p_ref[b, :, OCH:OCH + CW]
            u = cc * ch
            uprev = (pprev_ref[b, :, 0:CW] * pprev_ref[b, :, CW:2 * CW]) * has_prev
            u1, u2 = _conv_taps(u, uprev)
            w0 = cw_ref[0:1, :]
            w1 = cw_ref[1:2, :]
            w2 = cw_ref[2:3, :]
            yc = w0 * u2 + w1 * u1 + w2 * u
            doc = dm_ref[b, :, NV:NV + CW]
            dproj_ref[b, :, OCB:OCB + CW] = (doc * yc).astype(BF16)
            dyc = doc * cb
            dycn = dycn_ref[b]
            row = lax.broadcasted_iota(jnp.int32, dyc.shape, 0)
            d1 = jnp.where(row >= CHUNK - 1, pltpu.roll(dycn, CHUNK - 1, 0), pltpu.roll(dyc, CHUNK - 1, 0))
            d2 = jnp.where(row >= CHUNK - 2, pltpu.roll(dycn, CHUNK - 2, 0), pltpu.roll(dyc, CHUNK - 2, 0))
            du = w2 * dyc + w1 * d1 + w0 * d2
            dproj_ref[b, :, OCC:OCC + CW] = (du * ch).astype(BF16)
            dproj_ref[b, :, OCH:OCH + CW] = (du * cc).astype(BF16)
            dcw_ref[0:1, :] += jnp.sum(dyc * u2, axis=0, keepdims=True)
            dcw_ref[1:2, :] += jnp.sum(dyc * u1, axis=0, keepdims=True)
            dcw_ref[2:3, :] += jnp.sum(dyc * u, axis=0, keepdims=True)
            dycn_ref[b] = dyc

        @pl.when(step == nc - 1)
        def _():
            copies = _stage2_copies(pb_refs, r2_refs, send_sems, recv_sems)
            for cp in copies:
                cp.wait_recv()
            for cp in copies:
                cp.wait_send()

    rev = lambda w: pl.BlockSpec((nb, CHUNK, w), lambda i: (0, nc - 1 - i, 0))
    const = lambda r, c: pl.BlockSpec((r, c), lambda i: (0, 0))
    hbm = pl.BlockSpec(memory_space=pl.ANY)
    return pl.pallas_call(
        body,
        name="mix_bwd",
        grid=(nc,),
        in_specs=[
            rev(PW),
            pl.BlockSpec((nb, CHUNK, 2 * CW), lambda i: (0, jnp.maximum(nc - 2 - i, 0), OCC // (2 * CW))),
            rev(NQK),
            pl.BlockSpec((nb, 1, NQK, DV), lambda i: (0, nc - 1 - i, 0, 0)),
            rev(NV),
            rev(D),
            const(1, DV),
            const(CONV_K, CW),
            const(A_PAD, NQK),
        ] + [hbm] * na,
        out_specs=[rev(PW), const(1, DV), const(8, CW), const(1, NQK), const(A_PAD, NQK)] + [hbm] * na,
        out_shape=[
            jax.ShapeDtypeStruct((nb, s, PW), BF16),
            jax.ShapeDtypeStruct((1, DV), F32),
            jax.ShapeDtypeStruct((8, CW), F32),
            jax.ShapeDtypeStruct((1, NQK), F32),
            jax.ShapeDtypeStruct((A_PAD, NQK), F32),
        ] + [jax.ShapeDtypeStruct((3,) + p.shape[1:], BF16) for p in pbs],
        scratch_shapes=[pltpu.VMEM((nb, NQK, DV), F32), pltpu.VMEM((nb, CHUNK, CW), F32),
                        pltpu.SemaphoreType.DMA((3 * na,)), pltpu.SemaphoreType.DMA((3 * na,))],
        compiler_params=_params(("arbitrary",)),
    )(proj3, proj3, z3, sprev, opre3, dmix3, gng, conv_w, wgu_p, *pbs)


SMALL_PACK_ROWS = 16


def _wgu_slot(r):
    return 4 + r // 4, NQK * (r % 4)


CONV_SLOTS = ((8, 0), (8, CW), (9, 0))


def _in_proj_bwd(dproj2d, x2d, dx1, g1, w_in_t, tm, pb, small_parts):
    t = x2d.shape[0]
    nt = t // tm

    def body(dp_ref, x_ref, dx1_ref, g_ref, w_ref, pb_ref, dg2, dgf, dbg, dgng, dwgu, dcw, lp,
             dx_ref, sums_ref, r2_ref, dg1_acc, pack, gbuf, pack1, gbuf1, send_sems, recv_sems,
             ssend, srecv, ssend1, srecv1):
        x, y, c = _position()
        me = 4 * x + 2 * y + c
        flips = [(k >> 2, (k >> 1) & 1, k & 1) for k in range(1, N_DEV)]
        peers = [(x ^ fx, y ^ fy, c ^ fc) for fx, fy, fc in flips]

        def small_copies(src, dst, send, recv, arrivals):
            return [pltpu.make_async_remote_copy(
                src_ref=src, dst_ref=dst.at[4 * px + 2 * py + pc if arrivals else me],
                send_sem=send.at[k], recv_sem=recv.at[k], device_id=(px, py, pc), device_id_type=MESH)
                for k, (px, py, pc) in enumerate(peers)]

        @pl.when(pl.program_id(0) == 0)
        def _():
            for cp in _stage2_copies([pb_ref], [r2_ref], send_sems, recv_sems):
                cp.start()
            dg1_acc[...] = jnp.zeros_like(dg1_acc)
            pack[...] = jnp.zeros_like(pack)
            pack[1:2, :] = dg2[...]
            pack[2:3, :] = dgf[...]
            pack[3:4, 0:NQK] = dbg[...]
            pack[3:4, NQK:NQK + DV] = dgng[...]
            pack[3:4, NQK + DV:NQK + 2 * DV] = lp[...]
            for r in range(RANK):
                row, lane = _wgu_slot(r)
                pack[row:row + 1, lane:lane + NQK] = dwgu[r:r + 1, :]
            for r, (row, lane) in enumerate(CONV_SLOTS):
                pack[row:row + 1, lane:lane + CW] = dcw[r:r + 1, :]
            for cp in small_copies(pack, gbuf, ssend, srecv, False):
                cp.start()
            gbuf[me] = pack[...]

        xv = x_ref[...]
        r = lax.rsqrt(jnp.mean(xv * xv, axis=-1, keepdims=True) + EPS)
        n1 = xv * r
        dh = _mm(dp_ref[...], w_ref[...])
        dg1_acc[...] += jnp.sum(dh * n1, axis=0, keepdims=True)
        dn = dh * g_ref[...]
        dx_ref[...] = dx1_ref[...] + r * (dn - n1 * jnp.mean(dn * n1, axis=-1, keepdims=True))

        @pl.when(pl.program_id(0) == nt - 1)
        def _():
            pack1[...] = jnp.zeros_like(pack1)
            pack1[0:1, :] = dg1_acc[...]
            for cp in small_copies(pack1, gbuf1, ssend1, srecv1, False):
                cp.start()
            gbuf1[me] = pack1[...]
            copies = _stage2_copies([pb_ref], [r2_ref], send_sems, recv_sems)
            for cp in copies:
                cp.wait_recv()
            for cp in copies:
                cp.wait_send()
            for src, dst, send, recv in ((pack, gbuf, ssend, srecv), (pack1, gbuf1, ssend1, srecv1)):
                for cp in small_copies(src, dst, send, recv, True):
                    cp.wait_recv()
                    cp.wait_send()
            acc = gbuf[0]
            acc1 = gbuf1[0]
            for d in range(1, N_DEV):
                acc = acc + gbuf[d]
                acc1 = acc1 + gbuf1[d]
            sums_ref[...] = acc
            sums_ref[0:1, :] = acc1[0:1, :]

    tile = lambda w: pl.BlockSpec((tm, w), lambda i: (i, 0))
    vec = pl.BlockSpec((1, D), lambda i: (0, 0))
    hbm = pl.BlockSpec(memory_space=pl.ANY)
    whole = lambda a: pl.BlockSpec(a.shape, lambda i: (0,) * a.ndim)
    return pl.pallas_call(
        body,
        name="in_proj_bwd",
        grid=(nt,),
        in_specs=[tile(PW), tile(D), tile(D), vec, pl.BlockSpec((PW, D), lambda i: (0, 0)), hbm]
        + [whole(a) for a in small_parts],
        out_specs=[tile(D), pl.BlockSpec((SMALL_PACK_ROWS, D), lambda i: (0, 0)), hbm],
        out_shape=[jax.ShapeDtypeStruct((t, D), F32), jax.ShapeDtypeStruct((SMALL_PACK_ROWS, D), F32),
                   jax.ShapeDtypeStruct((3,) + pb.shape[1:], BF16)],
        scratch_shapes=[pltpu.VMEM((1, D), F32),
                        pltpu.VMEM((SMALL_PACK_ROWS, D), F32), pltpu.VMEM((N_DEV, SMALL_PACK_ROWS, D), F32),
                        pltpu.VMEM((8, D), F32), pltpu.VMEM((N_DEV, 8, D), F32),
                        pltpu.SemaphoreType.DMA((3,)), pltpu.SemaphoreType.DMA((3,)),
                        pltpu.SemaphoreType.DMA((7,)), pltpu.SemaphoreType.DMA((7,)),
                        pltpu.SemaphoreType.DMA((7,)), pltpu.SemaphoreType.DMA((7,))],
        compiler_params=_params(("arbitrary",)),
    )(dproj2d, x2d, dx1, g1, w_in_t, pb, *small_parts)


def _get_rows(ref):
    return ref[:, 0, :] if len(ref.shape) == 3 else ref[...]


def _put_rows(ref, val):
    if len(ref.shape) == 3:
        ref[:, 0, :] = val
    else:
        ref[...] = val


def _adamw_math(w, g, m, v):
    m = ADAM_B1 * m + (1.0 - ADAM_B1) * g
    v = ADAM_B2 * v + (1.0 - ADAM_B2) * (g * g)
    m_hat = m / (1.0 - ADAM_B1 ** ADAM_STEP)
    v_hat = v / (1.0 - ADAM_B2 ** ADAM_STEP)
    delta = -ADAM_LR * (m_hat / (jnp.sqrt(v_hat) + ADAM_EPS) + ADAM_WD * w)
    return delta, m, v


def _position():
    return lax.axis_index("x"), lax.axis_index("y"), lax.axis_index("c")


GATHER_PARTS = 2
GATHER_SEMS = 7 * GATHER_PARTS


def _gather_copies(stage, lo, rows, gx, send_sems, recv_sems, local_sem):
    x, y, c = _position()
    me = (x, y, c)
    sibling = (x, y, 1 - c)
    chips = [(1 - x, y), (x, 1 - y), (1 - x, 1 - y)]
    part = -(-rows // (16 * GATHER_PARTS)) * 16
    bounds = [(p * part, min(part, rows - p * part)) for p in range(GATHER_PARTS)]

    def blk(px, py, pc, off, n):
        return gx.at[4 * px + 2 * py + pc, pl.ds(off, n), :]

    mine = pltpu.make_async_copy(stage.at[pl.ds(lo, rows), :], gx.at[4 * x + 2 * y + c], local_sem)
    parts = []
    for p, (off, n) in enumerate(bounds):
        def copy(k, block, to, from_stage=False, p=p, off=off, n=n):
            return pltpu.make_async_remote_copy(
                src_ref=stage.at[pl.ds(lo + off, n), :] if from_stage else blk(*block, off, n),
                dst_ref=blk(*block, off, n), send_sem=send_sems.at[7 * p + k], recv_sem=recv_sems.at[7 * p + k],
                device_id=to, device_id_type=MESH)

        first = [copy(0, me, sibling, True)] + [copy(1 + j, me, (*chip, c), True) for j, chip in enumerate(chips)]
        passed = [copy(4 + j, (*chip, c), sibling) for j, chip in enumerate(chips)]
        arrivals = ([copy(0, sibling, me)] + [copy(1 + j, (*chip, c), me) for j, chip in enumerate(chips)]
                    + [copy(4 + j, (*chip, 1 - c), me) for j, chip in enumerate(chips)])
        parts.append((first, passed, arrivals))
    return mine, parts


def _gather_start(*args):
    mine, parts = _gather_copies(*args)
    mine.start()
    for first, _, _ in parts:
        for cp in first:
            cp.start()


def _gather_finish(*args):
    mine, parts = _gather_copies(*args)
    for _, passed, arrivals in parts:
        for j in range(3):
            arrivals[1 + j].wait_recv()
            passed[j].start()
    for first, passed, arrivals in parts:
        arrivals[0].wait_recv()
        for j in range(3):
            arrivals[4 + j].wait_recv()
        for cp in first + passed:
            cp.wait_send()
    mine.wait()


def _gather_sems():
    return [pltpu.SemaphoreType.DMA((GATHER_SEMS,)), pltpu.SemaphoreType.DMA((GATHER_SEMS,)), pltpu.SemaphoreType.DMA]


def _gather_w_in(w_it, w_gt, w_ut, w_d, w_o, wgu_s, conv_s):
    def body(wi_ref, wg_ref, wu_ref, wd_ref, wo_ref, wgu_ref, conv_ref, w_ref, gwgu_ref, gconv_ref, stage,
             buf, send_sems, recv_sems, local_sem, ssend, srecv):
        x, y, c = _position()
        me = 4 * x + 2 * y + c
        stage[SLAB_IN:SLAB_IN + IN_W, :] = wi_ref[:, 0, :].astype(BF16)
        stage[SLAB_IN + IN_W:SLAB_G, :] = jnp.zeros((IN_ROWS - IN_W, D), BF16)
        args = (stage, SLAB_IN, IN_ROWS, buf, send_sems, recv_sems, local_sem)
        _gather_start(*args)
        stage[SLAB_G:SLAB_U, :] = wg_ref[...].astype(BF16)
        stage[SLAB_U:SLAB_D, :] = wu_ref[...].astype(BF16)
        stage[SLAB_D:SLAB_O, :] = wd_ref[...].astype(BF16)
        stage[SLAB_O:SLAB_ROWS, :] = wo_ref[...].astype(BF16)
        flips = [(k >> 2, (k >> 1) & 1, k & 1) for k in range(1, N_DEV)]
        peers = [(x ^ fx, y ^ fy, c ^ fc) for fx, fy, fc in flips]

        def small(k, block_id, to):
            return [pltpu.make_async_remote_copy(
                src_ref=s, dst_ref=g.at[block_id], send_sem=ssend.at[2 * k + n], recv_sem=srecv.at[2 * k + n],
                device_id=to, device_id_type=MESH)
                for n, (s, g) in enumerate(((wgu_ref, gwgu_ref), (conv_ref, gconv_ref)))]

        gwgu_ref[me] = wgu_ref[...]
        gconv_ref[me] = conv_ref[...]
        for k, peer in enumerate(peers):
            for cp in small(k, me, peer):
                cp.start()
        w_ref[IN_COLS:PW, :] = jnp.zeros((PW - IN_COLS, D), BF16)
        _gather_finish(*args)
        for k, (px, py, pc) in enumerate(peers):
            for cp in small(k, 4 * px + 2 * py + pc, (px, py, pc)):
                cp.wait_recv()
                cp.wait_send()
        for j, lo, hi, d in _in_segments():
            w_ref[d:d + hi - lo, :] = buf[j, lo:hi, :]

    vm = pl.BlockSpec(memory_space=pltpu.VMEM)
    return pl.pallas_call(
        body,
        name="gather_w_in",
        in_specs=[vm] * 7,
        out_specs=[vm] * 4,
        out_shape=[jax.ShapeDtypeStruct((PW, D), BF16),
                   jax.ShapeDtypeStruct((N_DEV,) + wgu_s.shape, F32),
                   jax.ShapeDtypeStruct((N_DEV,) + conv_s.shape, F32),
                   jax.ShapeDtypeStruct((SLAB_ROWS, D), BF16)],
        scratch_shapes=[pltpu.VMEM((N_DEV, IN_ROWS, D), BF16)] + _gather_sems()
        + [pltpu.SemaphoreType.DMA((14,)), pltpu.SemaphoreType.DMA((14,))],
        compiler_params=_params(),
    )(w_it, w_gt, w_ut, w_d, w_o, wgu_s, conv_s)


def _w_in_core_reduce(dw_t):
    def body(d_ref, own_ref, sib_ref, pb_ref, g, gb, r1, send_sems, recv_sems):
        x, y, c = _position()
        chip = 2 * x + y
        for j in range(N_DEV):
            g[j, IN_W:IN_ROWS, :] = jnp.zeros((IN_ROWS - IN_W, D), F32)
        for j, lo, hi, d in _in_segments():
            g[j, lo:hi, :] = d_ref[d:d + hi - lo, :]
        for j in range(N_DEV):
            gb[j] = g[j].astype(BF16)
        copies = _stage1_copies(gb, r1, send_sems, recv_sems)
        for cp in copies:
            cp.start()
        own_ref[0] = g[2 * chip + c]
        for cp in copies:
            cp.wait_recv()
        sib_ref[0] = r1[chip]
        for k in range(1, 4):
            t = chip ^ k
            pb_ref[k - 1] = (g[2 * t + c] + r1[t].astype(F32)).astype(BF16)
        for cp in copies:
            cp.wait_send()

    vm = pl.BlockSpec(memory_space=pltpu.VMEM)
    return pl.pallas_call(
        body,
        name="w_in_core_reduce",
        in_specs=[vm],
        out_specs=[vm, vm, vm],
        out_shape=[jax.ShapeDtypeStruct((1, IN_ROWS, D), F32), jax.ShapeDtypeStruct((1, IN_ROWS, D), BF16),
                   jax.ShapeDtypeStruct((3, IN_ROWS, D), BF16)],
        scratch_shapes=[pltpu.VMEM((N_DEV, IN_ROWS, D), F32), pltpu.VMEM((N_DEV, IN_ROWS, D), BF16),
                        pltpu.VMEM((4, IN_ROWS, D), BF16), pltpu.SemaphoreType.DMA((4,)),
                        pltpu.SemaphoreType.DMA((4,))],
        compiler_params=_params(),
    )(dw_t)


def _stage1_copies(g_ref, r_ref, send_sems, recv_sems):
    x, y, c = _position()
    return [pltpu.make_async_remote_copy(
        src_ref=g_ref.at[2 * i + 1 - c], dst_ref=r_ref.at[i], send_sem=send_sems.at[i], recv_sem=recv_sems.at[i],
        device_id=(x, y, 1 - c), device_id_type=MESH) for i in range(4)]


def _ffn_core_reduce(dw3, dwb3, dw_o, dwb_o, pos_arr, dx1b, gwb):
    def body(pos_ref, g0, g1, g2, go, gb3_hbm, gbo_hbm, dx1b_ref, gwb_hbm, p0, p1, p2, po, s0, s1, s2, so, dmix_ref,
             r1f, r1o, wo, send_sems, recv_sems, wsem):
        step = pl.program_id(0)
        k = jnp.minimum(step, 2)
        x, y, c = _position()
        chip = 2 * x + y

        @pl.when(step == 0)
        def _():
            loads = [pltpu.make_async_copy(gwb_hbm.at[j, pl.ds(FF_W, OUT_ROWS), :],
                                           wo.at[pl.ds(OUT_ROWS * j, OUT_ROWS), :], wsem.at[j]) for j in range(N_DEV)]
            for cp in loads:
                cp.start()
            for cp in loads:
                cp.wait()

        dmix_ref[...] = _nt(dx1b_ref[...], wo[...])

        def copies(p):
            src = 2 * (chip ^ ((p + 1) & 3)) + 1 - c
            pairs = [(gb3_hbm.at[a, src], r1f.at[a, p]) for a in range(3)] + [(gbo_hbm.at[src], r1o.at[p])]
            return [pltpu.make_async_remote_copy(
                src_ref=s, dst_ref=d, send_sem=send_sems.at[4 * p + a], recv_sem=recv_sems.at[4 * p + a],
                device_id=(x, y, 1 - c), device_id_type=MESH) for a, (s, d) in enumerate(pairs)]

        @pl.when(step == 0)
        def _():
            for p in range(4):
                for cp in copies(p):
                    cp.start()

        for p in range(3):
            @pl.when(step == p)
            def _():
                for cp in copies(p):
                    cp.wait_recv()

        for a, (g, pb) in enumerate(((g0, p0), (g1, p1), (g2, p2))):
            pb[...] = (g[...] + r1f[a, k][None].astype(F32)).astype(BF16)
        po[...] = (go[...] + r1o[k][None].astype(F32)).astype(BF16)

        @pl.when(step == 3)
        def _():
            for cp in copies(3):
                cp.wait_recv()
            for a, s in enumerate((s0, s1, s2)):
                s[0] = r1f[a, 3]
            so[0] = r1o[3]
            for p in range(4):
                for cp in copies(p):
                    cp.wait_send()

    t = dx1b.shape[0]
    other = lambda s, pos: 2 * (pos[1] ^ (jnp.minimum(s, 2) + 1)) + pos[0]
    g_spec = lambda lead: pl.BlockSpec((None, 1, FF_W, D), lambda s, pos: (lead, other(s, pos), 0, 0))
    slot = lambda rows: pl.BlockSpec((1, rows, D), lambda s, pos: (jnp.minimum(s, 2), 0, 0))
    one = lambda rows: pl.BlockSpec((1, rows, D), lambda s, pos: (0, 0, 0))
    quarter = pl.BlockSpec((t // 4, D), lambda s, pos: (s, 0))
    hbm = pl.BlockSpec(memory_space=pl.ANY)
    return pl.pallas_call(
        body,
        name="ffn_core_reduce",
        grid_spec=pltpu.PrefetchScalarGridSpec(
            num_scalar_prefetch=1, grid=(4,),
            in_specs=[g_spec(0), g_spec(1), g_spec(2),
                      pl.BlockSpec((1, OUT_ROWS, D), lambda s, pos: (other(s, pos), 0, 0)), hbm, hbm, quarter, hbm],
            out_specs=[slot(FF_W), slot(FF_W), slot(FF_W), slot(OUT_ROWS),
                       one(FF_W), one(FF_W), one(FF_W), one(OUT_ROWS), quarter],
            scratch_shapes=[pltpu.VMEM((3, 4, FF_W, D), BF16), pltpu.VMEM((4, OUT_ROWS, D), BF16),
                            pltpu.VMEM((D, D), BF16), pltpu.SemaphoreType.DMA((16,)),
                            pltpu.SemaphoreType.DMA((16,)), pltpu.SemaphoreType.DMA((N_DEV,))]),
        out_shape=[jax.ShapeDtypeStruct((3, FF_W, D), BF16)] * 3 + [jax.ShapeDtypeStruct((3, OUT_ROWS, D), BF16)]
        + [jax.ShapeDtypeStruct((1, FF_W, D), BF16)] * 3 + [jax.ShapeDtypeStruct((1, OUT_ROWS, D), BF16),
                                                             jax.ShapeDtypeStruct((t, D), F32)],
        compiler_params=_params(("arbitrary",)),
    )(pos_arr, dw3, dw3, dw3, dw_o, dwb3, dwb_o, dx1b, gwb)


def _stage2_copies(p_refs, r_refs, send_sems, recv_sems):
    x, y, c = _position()
    copies = []
    for a in range(len(p_refs)):
        for k in range(1, 4):
            copies.append(pltpu.make_async_remote_copy(
                src_ref=p_refs[a].at[k - 1], dst_ref=r_refs[a].at[k - 1],
                send_sem=send_sems.at[3 * a + k - 1], recv_sem=recv_sems.at[3 * a + k - 1],
                device_id=(x ^ (k >> 1), y ^ (k & 1), c), device_id_type=MESH))
    return copies


def _finish_weights(items, pos_arr, name, nblk):
    n = len(items)
    in_specs, out_specs, out_shape, operands, wbs = [], [], [], [], []
    for g8, lead, r1, r2, w, m, v in items:
        rows, wr = g8.shape[-2], w.shape[0]
        assert rows % nblk == 0 and wr % nblk == 0 and (nblk == 1 or (rows == wr and rows % (16 * nblk) == 0))
        rb, wb = rows // nblk, wr // nblk
        if lead is not None:
            g_spec = pl.BlockSpec((None, 1, rb, D), lambda i, pos, lead=lead: (lead, 2 * pos[1] + pos[0], i, 0))
        elif g8.shape[0] == 1:
            g_spec = pl.BlockSpec((1, rb, D), lambda i, pos: (0, i, 0))
        else:
            g_spec = pl.BlockSpec((1, rb, D), lambda i, pos: (2 * pos[1] + pos[0], i, 0))
        r1_spec = pl.BlockSpec((1, rb, D), lambda i, pos: (0, i, 0))
        if w.ndim == 3:
            wblk = pl.BlockSpec((wb, 1, D), lambda i, pos: (i, 0, 0))
        else:
            wblk = pl.BlockSpec((wb, D), lambda i, pos: (i, 0))
        in_specs += [g_spec, r1_spec, pl.BlockSpec((3, rb, D), lambda i, pos: (0, i, 0)), wblk, wblk, wblk]
        out_specs += [wblk] * 4
        out_shape += [jax.ShapeDtypeStruct(w.shape, F32)] * 4
        operands += [g8, r1, r2, w, m, v]
        wbs.append(wb)

    def body(pos_ref, *refs):
        for a in range(n):
            g_ref, r1_ref, r2_ref, w_ref, m_ref, v_ref = refs[6 * a:6 * a + 6]
            g_out, d_out, m_out, v_out = refs[6 * n + 4 * a:6 * n + 4 * a + 4]
            g = g_ref[0] + r1_ref[0].astype(F32)
            for k in range(3):
                g = g + r2_ref[k].astype(F32)
            g = g[0:wbs[a], :]
            d, mn, vn = _adamw_math(_get_rows(w_ref), g, _get_rows(m_ref), _get_rows(v_ref))
            for out, val in ((g_out, g), (d_out, d), (m_out, mn), (v_out, vn)):
                _put_rows(out, val)

    return pl.pallas_call(
        body,
        name=name,
        grid_spec=pltpu.PrefetchScalarGridSpec(
            num_scalar_prefetch=1, grid=(nblk,), in_specs=in_specs, out_specs=out_specs),
        out_shape=out_shape,
        compiler_params=_params(("arbitrary",)),
    )(pos_arr, *operands)


SMALL_NAMES = ("norm1_g", "norm2_g", "norm_f_g", "b_gate", "gla_norm_g", "w_gate_up", "conv_w")
WGU_W = NQK // N_DEV
CONV_W = CW // N_DEV


def _small_adamw(sums, ws, ms, vs):
    n = len(SMALL_NAMES)

    def body(*refs):
        acc_ref = refs[0]
        w_refs, m_refs, v_refs = refs[1:1 + n], refs[1 + n:1 + 2 * n], refs[1 + 2 * n:1 + 3 * n]
        loss_ref = refs[1 + 3 * n]
        outs = refs[2 + 3 * n:]
        x, y, c = _position()
        me = 4 * x + 2 * y + c
        acc = acc_ref[...]
        loss_ref[...] = acc[3:4, NQK + DV:NQK + DV + 1]

        def my_columns(full, width):
            r = lax.broadcasted_iota(jnp.int32, (full.shape[1], width), 0)
            col = lax.broadcasted_iota(jnp.int32, (full.shape[1], width), 1)
            sel = (r == width * me + col).astype(F32)
            return _mm(full, sel, precision=HIGHEST)

        dwgu = jnp.concatenate([acc[row:row + 1, lane:lane + NQK] for row, lane in map(_wgu_slot, range(RANK))], axis=0)
        dcw = jnp.concatenate([acc[row:row + 1, lane:lane + CW] for row, lane in CONV_SLOTS], axis=0)
        grads = [acc[0:1, :], acc[1:2, :], acc[2:3, :], acc[3:4, 0:NQK], acc[3:4, NQK:NQK + DV],
                 my_columns(dwgu, WGU_W), my_columns(dcw, CONV_W)]
        for i, g in enumerate(grads):
            d, mn, vn = _adamw_math(_get_rows(w_refs[i]), g, _get_rows(m_refs[i]), _get_rows(v_refs[i]))
            for out, val in zip(outs[4 * i:4 * i + 4], (g, d, mn, vn)):
                _put_rows(out, val)

    vm = pl.BlockSpec(memory_space=pltpu.VMEM)
    out_shape = [jax.ShapeDtypeStruct((1, 1), F32)]
    for w in ws:
        out_shape += [jax.ShapeDtypeStruct(w.shape, F32)] * 4
    return pl.pallas_call(
        body,
        name="small_adamw",
        in_specs=[vm] * (1 + 3 * n),
        out_specs=[vm] * (1 + 4 * n),
        out_shape=out_shape,
        compiler_params=_params(),
    )(sums, *ws, *ms, *vs)


def kernel(x, norm1_g, w_in, w_gate_up, b_gate, gla_norm_g, conv_w, w_out, norm2_g, w_ffn_gate, w_ffn_up, w_ffn_down, norm_f_g, loss_target, m_norm1_g, m_w_in, m_w_gate_up, m_b_gate, m_gla_norm_g, m_conv_w, m_w_out, m_norm2_g, m_w_ffn_gate, m_w_ffn_up, m_w_ffn_down, m_norm_f_g, v_norm1_g, v_w_in, v_w_gate_up, v_b_gate, v_gla_norm_g, v_conv_w, v_w_out, v_norm2_g, v_w_ffn_gate, v_w_ffn_up, v_w_ffn_down, v_norm_f_g):
    xi, yi, ci = _position()
    pos_arr = jnp.stack([ci, 2 * xi + yi]).astype(jnp.int32)
    nb, s, _ = x.shape
    t = nb * s

    tr = lambda a: a[0].T
    rows_of = lambda a: a.transpose(2, 0, 1)
    conv_rows = lambda a: a.transpose(1, 0, 2)
    w_in_t, gwgu, gconv, stage = _gather_w_in(rows_of(w_in), tr(w_ffn_gate), tr(w_ffn_up), w_ffn_down[0], w_out[0],
                                              w_gate_up[0], conv_rows(conv_w))
    wgu_f = gwgu.transpose(1, 0, 2).reshape(RANK, NQK)
    conv_f = gconv.transpose(1, 2, 0, 3).reshape(CONV_K, CW)
    wgu_p = jnp.concatenate([wgu_f, jnp.zeros((A_PAD - RANK, NQK), F32)], axis=0).astype(BF16)

    x2d = x.reshape(t, D)
    tgt2d = loss_target.reshape(t, D)
    tm = 256
    tm_in = min(512, t)
    tk = min(2048, t)
    proj, z, h, gwb = _in_proj_fwd(x2d, norm1_g, w_in_t, wgu_p, b_gate, tm_in, stage)
    proj3 = proj.reshape(nb, s, PW)
    z3 = z.reshape(nb, s, NQK)
    mix3, opre3, sprev, x1, gwa = _mix_fwd(proj3, z3, gla_norm_g, conv_f, stage, x, gwb)
    mix2d = mix3.reshape(t, D)
    dx1, dx1b, adu, hb, dg2, dgf, loss_part = _ffn_fwd_bwd(
        x1.reshape(t, D), tgt2d, gwa, gwb, norm2_g, norm_f_g.reshape(1, D), tm)
    dw3, dwb3 = _dw_ffn(adu, hb, tk)
    dw3 = dw3.reshape(3, N_DEV, FF_W, D)
    dw_o, dwb_o = _tn_matmul(mix2d, dx1b, D // 2, D, tk, "dw_out", True)
    dw_o = dw_o.reshape(N_DEV, OUT_ROWS, D)
    *pb, sib_d, sib_g, sib_u, sib_o, dmix = _ffn_core_reduce(
        dw3, dwb3.reshape(3, N_DEV, FF_W, D), dw_o, dwb_o.reshape(N_DEV, OUT_ROWS, D), pos_arr, dx1b, gwb)
    g8 = [dw3, dw3, dw3, dw_o]
    leads = [0, 1, 2, None]
    tags = ("w_ffn_down", "w_ffn_gate", "w_ffn_up", "w_out")
    r1 = [sib_d, sib_g, sib_u, sib_o]
    mb = _mix_bwd(proj3, z3, sprev, opre3, dmix.reshape(nb, s, D), gla_norm_g, conv_f, wgu_p, [pb[0], pb[1], pb[3]])
    dproj3, dgng, dcw, dbg, dwgu = mb[:5]
    dproj2d = dproj3.reshape(t, PW)
    dw_in_t, r2_up = _tn_matmul(dproj2d, h, PW // 5, D, tk, "dw_in", False, _stage2_rider([pb[2]]))
    r2 = [mb[5], mb[6], r2_up, mb[7]]
    g_in, r1_in, pb_in = _w_in_core_reduce(dw_in_t)
    dx, small_sums, r2_in = _in_proj_bwd(dproj2d, x2d, dx1, norm1_g, w_in_t, tm_in, pb_in,
                                         (dg2, dgf, dbg, dgng, dwgu, dcw, loss_part))

    tags = ("w_in",) + tags
    g8 = [g_in] + g8
    leads = [None] + leads
    r1 = [r1_in] + list(r1)
    r2 = [r2_in] + r2
    shard_w = (rows_of(w_in), w_ffn_down[0], tr(w_ffn_gate), tr(w_ffn_up), w_out[0])
    shard_m = (rows_of(m_w_in), m_w_ffn_down[0], tr(m_w_ffn_gate), tr(m_w_ffn_up), m_w_out[0])
    shard_v = (rows_of(v_w_in), v_w_ffn_down[0], tr(v_w_ffn_gate), tr(v_w_ffn_up), v_w_out[0])
    back = (lambda o: o.transpose(1, 2, 0), lambda o: o[None], lambda o: o.T[None], lambda o: o.T[None],
            lambda o: o[None])
    items = list(zip(g8, leads, r1, r2, shard_w, shard_m, shard_v))
    flat = list(_finish_weights(items[1:], pos_arr, "finish_ffn_out", 2))
    flat = list(_finish_weights(items[:1], pos_arr, "finish_w_in", 1)) + flat
    results = {}
    for i, (tag, to_shard) in enumerate(zip(tags, back)):
        results[tag] = [to_shard(o) for o in flat[4 * i:4 * i + 4]]

    small_w = (norm1_g, norm2_g, norm_f_g.reshape(1, D), b_gate, gla_norm_g, w_gate_up[0], conv_rows(conv_w))
    small_m = (m_norm1_g, m_norm2_g, m_norm_f_g.reshape(1, D), m_b_gate, m_gla_norm_g, m_w_gate_up[0],
               conv_rows(m_conv_w))
    small_v = (v_norm1_g, v_norm2_g, v_norm_f_g.reshape(1, D), v_b_gate, v_gla_norm_g, v_w_gate_up[0],
               conv_rows(v_conv_w))
    so = _small_adamw(small_sums, small_w, small_m, small_v)
    loss = so[0].reshape(())
    to_shape = {"norm_f_g": lambda o: o.reshape(D), "w_gate_up": lambda o: o[None],
                "conv_w": lambda o: o.transpose(1, 0, 2)}
    for i, name in enumerate(SMALL_NAMES):
        results[name] = [to_shape.get(name, lambda o: o)(o) for o in so[1 + 4 * i:5 + 4 * i]]

    names = ("norm1_g", "w_in", "w_gate_up", "b_gate", "gla_norm_g", "conv_w", "w_out", "norm2_g",
             "w_ffn_gate", "w_ffn_up", "w_ffn_down", "norm_f_g")
    outs = [loss, dx.reshape(nb, s, D)]
    for kind in range(4):
        for name in names:
            outs.append(results[name][kind])
    return tuple(outs)
```

```python
import jax
import jax.numpy as jnp
from jax import lax
from jax.experimental import pallas as pl
from jax.experimental.pallas import tpu as pltpu

F32 = jnp.float32
BF16 = jnp.bfloat16
HIGHEST = lax.Precision.HIGHEST
MESH = pl.DeviceIdType.MESH

N_DEV = 8
D = 1024
DFF = 2816
HEADS = 4
DK = 64
DV = 128
NQK = HEADS * DK
NV = HEADS * DV
RANK = 16
CHUNK = 64
CW = 512
CONV_K = 3
IN_COLS = 3088
EPS = 1e-6
INV_GATE_NORM = 1.0 / 16.0
Q_SCALE = DK ** -0.5

PW = 3200
OQ, OK_, OV, OG, OCB, OCC, OCH, OA = 0, 256, 512, 1024, 1536, 2048, 2560, 3072
A_PAD = 128

ADAM_LR = 0.001
ADAM_B1 = 0.9
ADAM_B2 = 0.999
ADAM_EPS = 1e-08
ADAM_WD = 0.01
ADAM_STEP = 10

IN_W = IN_COLS // N_DEV
IN_ROWS = 400
FF_W = DFF // N_DEV
OUT_ROWS = D // N_DEV
SLAB_IN = 0
SLAB_G = SLAB_IN + IN_ROWS
SLAB_U = SLAB_G + FF_W
SLAB_D = SLAB_U + FF_W
SLAB_O = SLAB_D + FF_W
SLAB_ROWS = SLAB_O + OUT_ROWS

VMEM_LIMIT = 56 * 1024 * 1024


def _params(sem=None, vmem=VMEM_LIMIT):
    return pltpu.CompilerParams(dimension_semantics=sem, vmem_limit_bytes=vmem)


def _nt(a, b):
    return lax.dot_general(a, b, (((1,), (1,)), ((), ())), preferred_element_type=F32)


def _tn(a, b, precision=None):
    return lax.dot_general(a, b, (((0,), (0,)), ((), ())), preferred_element_type=F32, precision=precision)


def _mm(a, b, precision=None):
    return jnp.dot(a, b, preferred_element_type=F32, precision=precision)


def _in_segments():
    segs = []
    for j in range(N_DEV):
        lo, hi = IN_W * j, IN_W * (j + 1)
        cuts = sorted({lo, hi} | {c for c in (OCB, OCB + RANK) if lo < c < hi})
        for a, b in zip(cuts[:-1], cuts[1:]):
            if a < OCB:
                d = a
            elif a < OCB + RANK:
                d = OA + (a - OCB)
            else:
                d = a - RANK
            segs.append((j, a - lo, b - lo, d))
    return segs


def _in_proj_fwd(x2d, g1, w_in_t, wgu_p, b_gate, tm, stage):
    t = x2d.shape[0]
    nt = t // tm
    g_rows = SLAB_ROWS - SLAB_D

    def body(x_ref, g_ref, w_ref, wgu_ref, bg_ref, stage_hbm, proj_ref, z_ref, h_ref, gwb_ref,
             send_sems, recv_sems, local_sem):
        gargs = (stage_hbm, SLAB_D, g_rows, gwb_ref, send_sems, recv_sems, local_sem)

        @pl.when(pl.program_id(0) == 0)
        def _():
            _gather_start(*gargs)

        x = x_ref[...]
        r = lax.rsqrt(jnp.mean(x * x, axis=-1, keepdims=True) + EPS)
        h = ((x * r) * g_ref[...]).astype(BF16)
        h_ref[...] = h
        proj = _nt(h, w_ref[...])
        proj_ref[...] = proj
        pa = proj[:, OA:OA + A_PAD].astype(BF16)
        z_ref[...] = _mm(pa, wgu_ref[...]) + bg_ref[...]

        @pl.when(pl.program_id(0) == nt - 1)
        def _():
            _gather_finish(*gargs)

    return pl.pallas_call(
        body,
        name="in_proj_fwd",
        grid=(t // tm,),
        in_specs=[
            pl.BlockSpec((tm, D), lambda i: (i, 0)),
            pl.BlockSpec((1, D), lambda i: (0, 0)),
            pl.BlockSpec((PW, D), lambda i: (0, 0)),
            pl.BlockSpec((A_PAD, NQK), lambda i: (0, 0)),
            pl.BlockSpec((1, NQK), lambda i: (0, 0)),
            pl.BlockSpec(memory_space=pl.ANY),
        ],
        out_specs=[
            pl.BlockSpec((tm, PW), lambda i: (i, 0)),
            pl.BlockSpec((tm, NQK), lambda i: (i, 0)),
            pl.BlockSpec((tm, D), lambda i: (i, 0)),
            pl.BlockSpec(memory_space=pl.ANY),
        ],
        out_shape=[
            jax.ShapeDtypeStruct((t, PW), F32),
            jax.ShapeDtypeStruct((t, NQK), F32),
            jax.ShapeDtypeStruct((t, D), BF16),
            jax.ShapeDtypeStruct((N_DEV, g_rows, D), BF16),
        ],
        scratch_shapes=_gather_sems(),
        compiler_params=_params(("arbitrary",)),
    )(x2d, g1, w_in_t, wgu_p, b_gate, stage)


def _head_masks():
    lane = lax.broadcasted_iota(jnp.int32, (1, NQK), 1)
    return [(lane >= DK * h) & (lane < DK * (h + 1)) for h in range(HEADS)]


def _split_bf16(x, n):
    parts = []
    for _ in range(n):
        p = x.astype(BF16)
        parts.append(p)
        x = x - p.astype(F32)
    return parts


def _chunk_fwd_parts(q, k, z, tril16):
    la = (jnp.minimum(z, 0.0) - jnp.log1p(jnp.exp(-jnp.abs(z)))) * INV_GATE_NORM
    la_parts = _split_bf16(la, 3)
    bc = _mm(tril16, la_parts[0]) + _mm(tril16, la_parts[1]) + _mm(tril16, la_parts[2])
    bl = bc[CHUNK - 1:CHUNK, :]
    eb = jnp.exp(bc)
    enb = jnp.exp(-bc)
    ekl = jnp.exp(bl - bc)
    qi = (q * Q_SCALE) * eb
    ki = k * enb
    ks = k * ekl
    ones16 = jnp.ones((CHUNK, DV), BF16)
    decb = jnp.exp(_tn(la_parts[0], ones16) + _tn(la_parts[1], ones16) + _tn(la_parts[2], ones16))
    return la, eb, enb, ekl, qi, ki, ks, decb


def _stack_heads(a, masks):
    return jnp.concatenate([jnp.where(m, a, 0.0) for m in masks], axis=0)


def _merge_heads(blocks, masks):
    out = blocks[HEADS - 1]
    for h in range(HEADS - 2, -1, -1):
        out = jnp.where(masks[h], blocks[h], out)
    return out


def _causal_stack_mask():
    row = lax.broadcasted_iota(jnp.int32, (HEADS * CHUNK, CHUNK), 0)
    col = lax.broadcasted_iota(jnp.int32, (HEADS * CHUNK, CHUNK), 1)
    return (row & (CHUNK - 1)) >= col


def _conv_taps(u, uprev):
    row = lax.broadcasted_iota(jnp.int32, u.shape, 0)
    u1 = jnp.where(row < 1, pltpu.roll(uprev, 1, 0), pltpu.roll(u, 1, 0))
    u2 = jnp.where(row < 2, pltpu.roll(uprev, 2, 0), pltpu.roll(u, 2, 0))
    return u1, u2


def _mix_fwd(proj3, z3, gng, conv_w, stage, x3, gwb):
    nb, s, _ = proj3.shape
    nc = s // CHUNK
    g_rows = SLAB_D - SLAB_G

    def body(p_ref, z_ref, gng_ref, cw_ref, stage_hbm, x_ref, gwb_hbm, mix_ref, o_ref, sprev_ref, x1_ref, gwa_ref,
             s_ref, uprev_ref, wo, wsem, send_sems, recv_sems, local_sem):
        n = pl.program_id(0)
        gargs = (stage_hbm, SLAB_G, g_rows, gwa_ref, send_sems, recv_sems, local_sem)

        @pl.when(n == 0)
        def _():
            _gather_start(*gargs)
            loads = [pltpu.make_async_copy(gwb_hbm.at[j, pl.ds(FF_W, OUT_ROWS), :],
                                           wo.at[pl.ds(OUT_ROWS * j, OUT_ROWS), :], wsem.at[j]) for j in range(N_DEV)]
            for cp in loads:
                cp.start()
            s_ref[...] = jnp.zeros_like(s_ref)
            uprev_ref[...] = jnp.zeros_like(uprev_ref)
            for cp in loads:
                cp.wait()

        r_i = lax.broadcasted_iota(jnp.int32, (CHUNK, CHUNK), 0)
        c_i = lax.broadcasted_iota(jnp.int32, (CHUNK, CHUNK), 1)
        tril16 = (r_i >= c_i).astype(BF16)
        masks = _head_masks()
        cmask = _causal_stack_mask()
        gg = gng_ref[...]
        for b in range(nb):
            q = p_ref[b, :, OQ:OQ + NQK]
            k = p_ref[b, :, OK_:OK_ + NQK]
            _, _, _, _, qi, ki, ks, decb = _chunk_fwd_parts(q, k, z_ref[b], tril16)
            qs = _stack_heads(qi, masks).astype(BF16)
            sc = jnp.where(cmask, _nt(qs, ki.astype(BF16)), 0.0).astype(BF16)
            st = s_ref[b]
            sprev_ref[b, 0] = st
            o_inter = _mm(qs, st.astype(BF16))
            v16 = p_ref[b, :, OV:OV + NV].astype(BF16)
            kv = _tn(ks.astype(BF16), v16)
            for h in range(HEADS):
                rows = slice(CHUNK * h, CHUNK * (h + 1))
                cols = slice(DV * h, DV * (h + 1))
                o = _mm(sc[rows], v16[:, cols]) + o_inter[rows]
                o_ref[b, :, cols] = o
                r = lax.rsqrt(jnp.mean(o * o, axis=-1, keepdims=True) + EPS)
                on = (o * r) * gg
                g = p_ref[b, :, OG + DV * h:OG + DV * (h + 1)]
                mix_ref[b, :, cols] = (on * (g * jax.nn.sigmoid(g))).astype(BF16)
                s_ref[b, rows, :] = decb[rows] * st[rows] + kv[rows, cols]
            u = p_ref[b, :, OCC:OCC + CW] * p_ref[b, :, OCH:OCH + CW]
            u1, u2 = _conv_taps(u, uprev_ref[b])
            yc = cw_ref[0:1, :] * u2 + cw_ref[1:2, :] * u1 + cw_ref[2:3, :] * u
            mix_ref[b, :, NV:NV + CW] = (p_ref[b, :, OCB:OCB + CW] * yc).astype(BF16)
            uprev_ref[b] = u
        mixed = _mm(jnp.concatenate([mix_ref[b] for b in range(nb)], axis=0), wo[...])
        for b in range(nb):
            x1_ref[b] = x_ref[b] + mixed[CHUNK * b:CHUNK * (b + 1)]

        @pl.when(n == nc - 1)
        def _():
            _gather_finish(*gargs)

    return pl.pallas_call(
        body,
        name="mix_fwd",
        grid=(nc,),
        in_specs=[
            pl.BlockSpec((nb, CHUNK, PW), lambda n: (0, n, 0)),
            pl.BlockSpec((nb, CHUNK, NQK), lambda n: (0, n, 0)),
            pl.BlockSpec((1, DV), lambda n: (0, 0)),
            pl.BlockSpec((CONV_K, CW), lambda n: (0, 0)),
            pl.BlockSpec(memory_space=pl.ANY),
            pl.BlockSpec((nb, CHUNK, D), lambda n: (0, n, 0)),
            pl.BlockSpec(memory_space=pl.ANY),
        ],
        out_specs=[
            pl.BlockSpec((nb, CHUNK, D), lambda n: (0, n, 0)),
            pl.BlockSpec((nb, CHUNK, NV), lambda n: (0, n, 0)),
            pl.BlockSpec((nb, 1, NQK, DV), lambda n: (0, n, 0, 0)),
            pl.BlockSpec((nb, CHUNK, D), lambda n: (0, n, 0)),
            pl.BlockSpec(memory_space=pl.ANY),
        ],
        out_shape=[
            jax.ShapeDtypeStruct((nb, s, D), BF16),
            jax.ShapeDtypeStruct((nb, s, NV), F32),
            jax.ShapeDtypeStruct((nb, nc, NQK, DV), F32),
            jax.ShapeDtypeStruct((nb, s, D), F32),
            jax.ShapeDtypeStruct((N_DEV, g_rows, D), BF16),
        ],
        scratch_shapes=[pltpu.VMEM((nb, NQK, DV), F32), pltpu.VMEM((nb, CHUNK, CW), F32),
                        pltpu.VMEM((D, D), BF16), pltpu.SemaphoreType.DMA((N_DEV,))] + _gather_sems(),
        compiler_params=_params(("arbitrary",)),
    )(proj3, z3, gng, conv_w, stage, x3, gwb)


def _ffn_fwd_bwd(x1_2d, tgt2d, gwa, gwb, g2, gf, tm):
    t = x1_2d.shape[0]

    def body(x1_ref, tgt_ref, g2_ref, gf_ref, gwa_hbm, gwb_hbm,
             dx1_ref, dx1b_ref, adu_ref, hb_ref, dg2_ref, dgf_ref, loss_ref,
             wg, wu, wd, wsem):
        i = pl.program_id(0)

        def weight_copies(n, dst, src, off, rows):
            return [pltpu.make_async_copy(src.at[j, pl.ds(off, rows), :], dst.at[pl.ds(rows * j, rows), :],
                                          wsem.at[N_DEV * n + j]) for j in range(N_DEV)]

        loads = (weight_copies(0, wg, gwa_hbm, 0, FF_W), weight_copies(1, wu, gwa_hbm, FF_W, FF_W),
                 weight_copies(2, wd, gwb_hbm, 0, FF_W))

        @pl.when(i == 0)
        def _():
            for group in loads:
                for cp in group:
                    cp.start()
            dg2_ref[...] = jnp.zeros_like(dg2_ref)
            dgf_ref[...] = jnp.zeros_like(dgf_ref)
            loss_ref[...] = jnp.zeros_like(loss_ref)
            for group in loads:
                for cp in group:
                    cp.wait()

        g2v = g2_ref[...]
        gfv = gf_ref[...]
        x1 = x1_ref[...]
        r2 = lax.rsqrt(jnp.mean(x1 * x1, axis=-1, keepdims=True) + EPS)
        n2 = x1 * r2
        h2 = (n2 * g2v).astype(BF16)
        hb_ref[1] = h2
        gate = _nt(h2, wg[...])
        up = _nt(h2, wu[...])
        sg = jax.nn.sigmoid(gate)
        sil = gate * sg
        act = (sil * up).astype(BF16)
        adu_ref[0] = act
        x2 = x1 + _mm(act, wd[...])
        rf = lax.rsqrt(jnp.mean(x2 * x2, axis=-1, keepdims=True) + EPS)
        nf = x2 * rf
        err = nf * gfv - tgt_ref[...]
        loss_ref[...] += 0.5 * jnp.sum(jnp.mean(err * err, axis=-1, keepdims=True))
        dy = err * (1.0 / D)
        dgf_ref[...] += jnp.sum(dy * nf, axis=0, keepdims=True)
        dnf = dy * gfv
        dx2 = rf * (dnf - nf * jnp.mean(dnf * nf, axis=-1, keepdims=True))
        dx2b = dx2.astype(BF16)
        hb_ref[0] = dx2b
        dact = _nt(dx2b, wd[...])
        dup = (dact * sil).astype(BF16)
        dgate = ((dact * up) * (sg * (1.0 + gate * (1.0 - sg)))).astype(BF16)
        adu_ref[2] = dup
        adu_ref[1] = dgate
        dh2 = _mm(dgate, wg[...]) + _mm(dup, wu[...])
        dg2_ref[...] += jnp.sum(dh2 * n2, axis=0, keepdims=True)
        dn2 = dh2 * g2v
        dx1 = dx2 + r2 * (dn2 - n2 * jnp.mean(dn2 * n2, axis=-1, keepdims=True))
        dx1_ref[...] = dx1
        dx1b_ref[...] = dx1.astype(BF16)

    tile = lambda w: pl.BlockSpec((tm, w), lambda i: (i, 0))
    vec = pl.BlockSpec((1, D), lambda i: (0, 0))
    hbm = pl.BlockSpec(memory_space=pl.ANY)
    return pl.pallas_call(
        body,
        name="ffn_fwd_bwd",
        grid=(t // tm,),
        in_specs=[tile(D), tile(D), vec, vec, hbm, hbm],
        out_specs=[tile(D), tile(D), pl.BlockSpec((3, tm, DFF), lambda i: (0, i, 0)),
                   pl.BlockSpec((2, tm, D), lambda i: (0, i, 0)), vec, vec,
                   pl.BlockSpec((1, 128), lambda i: (0, 0))],
        out_shape=[
            jax.ShapeDtypeStruct((t, D), F32),
            jax.ShapeDtypeStruct((t, D), BF16),
            jax.ShapeDtypeStruct((3, t, DFF), BF16),
            jax.ShapeDtypeStruct((2, t, D), BF16),
            jax.ShapeDtypeStruct((1, D), F32),
            jax.ShapeDtypeStruct((1, D), F32),
            jax.ShapeDtypeStruct((1, 128), F32),
        ],
        scratch_shapes=[pltpu.VMEM((DFF, D), BF16), pltpu.VMEM((DFF, D), BF16), pltpu.VMEM((DFF, D), BF16),
                        pltpu.SemaphoreType.DMA((3 * N_DEV,))],
        compiler_params=_params(("arbitrary",)),
    )(x1_2d, tgt2d, g2, gf, gwa, gwb)


def _stage2_rider(pbs):
    return dict(inputs=list(pbs), out_shape=[jax.ShapeDtypeStruct(p.shape, BF16) for p in pbs], nsem=3 * len(pbs),
                copies=_stage2_copies)


def _tn_matmul(a, b, bm, bn, tk, name, with_bf16, rider=None):
    t, m = a.shape
    n = b.shape[1]
    nk = t // tk
    nout = 2 if with_bf16 else 1
    grid = (m // bm, n // bn, nk)
    r_in = [] if rider is None else rider["inputs"]
    r_out = [] if rider is None else rider["out_shape"]

    def body(a_ref, b_ref, *rest):
        ins, outs = rest[:len(r_in)], rest[len(r_in):len(r_in) + nout]
        r_outs, sems = rest[len(r_in) + nout:len(r_in) + nout + len(r_out)], rest[len(r_in) + nout + len(r_out):]
        o_ref = outs[0]
        i, j, k = pl.program_id(0), pl.program_id(1), pl.program_id(2)
        if rider is not None:
            @pl.when((i == 0) & (j == 0) & (k == 0))
            def _():
                for cp in rider["copies"](ins, r_outs, *sems):
                    cp.start()

        @pl.when(k == 0)
        def _():
            o_ref[...] = jnp.zeros_like(o_ref)

        o_ref[...] += _tn(a_ref[...].astype(BF16), b_ref[...].astype(BF16))
        if with_bf16:
            @pl.when(k == nk - 1)
            def _():
                outs[1][...] = o_ref[...].astype(BF16)
        if rider is not None:
            @pl.when((i == grid[0] - 1) & (j == grid[1] - 1) & (k == nk - 1))
            def _():
                copies = rider["copies"](ins, r_outs, *sems)
                for cp in copies:
                    cp.wait_recv()
                for cp in copies:
                    cp.wait_send()

    out_blk = pl.BlockSpec((bm, bn), lambda i, j, k: (i, j))
    hbm = pl.BlockSpec(memory_space=pl.ANY)
    out_shape = [jax.ShapeDtypeStruct((m, n), F32)] + ([jax.ShapeDtypeStruct((m, n), BF16)] if with_bf16 else [])
    res = pl.pallas_call(
        body,
        name=name,
        grid=grid,
        in_specs=[pl.BlockSpec((tk, bm), lambda i, j, k: (k, i)), pl.BlockSpec((tk, bn), lambda i, j, k: (k, j))]
        + [hbm] * len(r_in),
        out_specs=[out_blk] * nout + [hbm] * len(r_out),
        out_shape=out_shape + list(r_out),
        scratch_shapes=([] if rider is None else
                        [pltpu.SemaphoreType.DMA((rider["nsem"],)), pltpu.SemaphoreType.DMA((rider["nsem"],))]),
        compiler_params=_params(("parallel", "parallel", "arbitrary") if rider is None
                                else ("arbitrary", "arbitrary", "arbitrary")),
    )(a, b, *r_in)
    return res[0] if len(res) == 1 else res


def _dw_ffn(adu, hb, tk):
    _, t, _ = adu.shape
    bm = DFF // 2
    nk = t // tk

    def body(a_ref, b_ref, o_ref, ob_ref):
        k = pl.program_id(2)

        @pl.when(k == 0)
        def _():
            o_ref[...] = jnp.zeros_like(o_ref)

        o_ref[...] += _tn(a_ref[...], b_ref[...])

        @pl.when(k == nk - 1)
        def _():
            ob_ref[...] = o_ref[...].astype(BF16)

    out_blk = pl.BlockSpec((None, bm, D), lambda p, i, k: (p, i, 0))
    return pl.pallas_call(
        body,
        name="dw_ffn",
        grid=(3, DFF // bm, nk),
        in_specs=[pl.BlockSpec((None, tk, bm), lambda p, i, k: (p, k, i)),
                  pl.BlockSpec((None, tk, D), lambda p, i, k: (jnp.minimum(p, 1), k, 0))],
        out_specs=[out_blk, out_blk],
        out_shape=[jax.ShapeDtypeStruct((3, DFF, D), F32), jax.ShapeDtypeStruct((3, DFF, D), BF16)],
        compiler_params=_params(("arbitrary", "arbitrary", "arbitrary")),
    )(adu, hb)


def _mix_bwd(proj3, z3, sprev, opre3, dmix3, gng, conv_w, wgu_p, pbs):
    nb, s, _ = proj3.shape
    nc = s // CHUNK
    na = len(pbs)

    def body(*refs):
        (p_ref, pprev_ref, z_ref, sp_ref, o_ref, dm_ref, gng_ref, cw_ref, wgu_ref) = refs[:9]
        pb_refs = refs[9:9 + na]
        (dproj_ref, dgng_ref, dcw_ref, dbg_ref, dwgu_ref) = refs[9 + na:14 + na]
        r2_refs = refs[14 + na:14 + 2 * na]
        ds_ref, dycn_ref, send_sems, recv_sems = refs[14 + 2 * na:]
        step = pl.program_id(0)
        n = nc - 1 - step

        @pl.when(step == 0)
        def _():
            for cp in _stage2_copies(pb_refs, r2_refs, send_sems, recv_sems):
                cp.start()
            ds_ref[...] = jnp.zeros_like(ds_ref)
            dycn_ref[...] = jnp.zeros_like(dycn_ref)
            dgng_ref[...] = jnp.zeros_like(dgng_ref)
            dcw_ref[...] = jnp.zeros_like(dcw_ref)
            dbg_ref[...] = jnp.zeros_like(dbg_ref)
            dwgu_ref[...] = jnp.zeros_like(dwgu_ref)

        r_i = lax.broadcasted_iota(jnp.int32, (CHUNK, CHUNK), 0)
        c_i = lax.broadcasted_iota(jnp.int32, (CHUNK, CHUNK), 1)
        tril16 = (r_i >= c_i).astype(BF16)
        triu16 = (r_i <= c_i).astype(BF16)
        causal = r_i >= c_i
        masks = _head_masks()
        cmask = _causal_stack_mask()
        gg = gng_ref[...]
        last_row = lax.broadcasted_iota(jnp.int32, (CHUNK, NQK), 0) == CHUNK - 1
        ones_r = jnp.ones((16, DV), BF16)
        has_prev = (n > 0).astype(F32)
        for b in range(nb):
            q = p_ref[b, :, OQ:OQ + NQK]
            k = p_ref[b, :, OK_:OK_ + NQK]
            z = z_ref[b]
            _, eb, enb, ekl, qi, ki, ks, decb = _chunk_fwd_parts(q, k, z, tril16)
            qi16 = qi.astype(BF16)
            ki16 = ki.astype(BF16)
            qs = _stack_heads(qi, masks).astype(BF16)
            sc = jnp.where(cmask, _nt(qs, ki16), 0.0).astype(BF16)
            st = sp_ref[b, 0]
            st16 = st.astype(BF16)
            dsn = ds_ref[b]
            dsn16 = dsn.astype(BF16)
            v16 = p_ref[b, :, OV:OV + NV].astype(BF16)
            do16 = []
            dgng = jnp.zeros((1, DV), F32)
            for h in range(HEADS):
                cols = slice(DV * h, DV * (h + 1))
                o = o_ref[b, :, cols]
                r = lax.rsqrt(jnp.mean(o * o, axis=-1, keepdims=True) + EPS)
                nh = o * r
                g = p_ref[b, :, OG + DV * h:OG + DV * (h + 1)]
                sg = jax.nn.sigmoid(g)
                dog = dm_ref[b, :, cols]
                dproj_ref[b, :, OG + DV * h:OG + DV * (h + 1)] = (
                    (dog * (nh * gg)) * (sg * (1.0 + g * (1.0 - sg)))).astype(BF16)
                don = dog * (g * sg)
                dgng = dgng + jnp.sum(don * nh, axis=0, keepdims=True)
                dn = don * gg
                do = r * (dn - nh * jnp.mean(dn * nh, axis=-1, keepdims=True))
                do16.append(do.astype(BF16))
            dgng_ref[...] += dgng
            do_rows = jnp.concatenate(do16, axis=0)
            v_rows = jnp.concatenate([v16[:, DV * h:DV * (h + 1)] for h in range(HEADS)], axis=0)
            dp16 = [jnp.where(causal, _nt(do16[h], v16[:, DV * h:DV * (h + 1)]), 0.0).astype(BF16)
                    for h in range(HEADS)]
            ks_dsn = _mm(_stack_heads(ks, masks).astype(BF16), dsn16)
            do_st = _nt(do_rows, st16)
            v_dsn = _nt(v_rows, dsn16)
            dp_ki = _mm(jnp.concatenate(dp16, axis=0), ki16)
            q_do = _tn(qi16, jnp.concatenate(do16, axis=1))
            dki_h = []
            for h in range(HEADS):
                rows = slice(CHUNK * h, CHUNK * (h + 1))
                cols = slice(DV * h, DV * (h + 1))
                dv = _tn(sc[rows], do16[h]) + ks_dsn[rows]
                dproj_ref[b, :, OV + DV * h:OV + DV * (h + 1)] = dv.astype(BF16)
                dki_h.append(_tn(dp16[h], qi16))
                ds_ref[b, rows, :] = decb[rows] * dsn[rows] + q_do[rows, cols]
            blocks = lambda a: [a[CHUNK * h:CHUNK * (h + 1)] for h in range(HEADS)]
            dqi = _merge_heads(blocks(dp_ki + do_st), masks)
            dki = _merge_heads(dki_h, masks)
            dks = _merge_heads(blocks(v_dsn), masks)
            dproj_ref[b, :, OQ:OQ + NQK] = (dqi * (Q_SCALE * eb)).astype(BF16)
            dproj_ref[b, :, OK_:OK_ + NQK] = (dki * enb + dks * ekl).astype(BF16)
            dks_ks = dks * ks
            db = dqi * qi - dki * ki - dks_ks
            sd = _split_bf16(dsn * st * decb, 2)
            dbl = jnp.sum(dks_ks, axis=0, keepdims=True) + (_nt(ones_r, sd[0]) + _nt(ones_r, sd[1]))[0:1, :]
            db = db + jnp.where(last_row, dbl, 0.0)
            db_parts = _split_bf16(db, 3)
            dla = _mm(triu16, db_parts[0]) + _mm(triu16, db_parts[1]) + _mm(triu16, db_parts[2])
            dz = (dla * INV_GATE_NORM) * (1.0 / (1.0 + jnp.exp(z)))
            dbg_ref[...] += jnp.sum(dz, axis=0, keepdims=True)
            dz16 = dz.astype(BF16)
            pa16 = p_ref[b, :, OA:OA + A_PAD].astype(BF16)
            dwgu_ref[...] += _tn(pa16, dz16)
            dproj_ref[b, :, OA:OA + A_PAD] = _nt(dz16, wgu_ref[...]).astype(BF16)
            cb = p_ref[b, :, OCB:OCB + CW]
            cc = p_ref[b, :, OCC:OCC + CW]
            ch = p_ref[b, :, OCH:OCH + CW]
            u = cc * ch
            uprev = (pprev_ref[b, :, 0:CW] * pprev_ref[b, :, CW:2 * CW]) * has_prev
            u1, u2 = _conv_taps(u, uprev)
            w0 = cw_ref[0:1, :]
            w1 = cw_ref[1:2, :]
            w2 = cw_ref[2:3, :]
            yc = w0 * u2 + w1 * u1 + w2 * u
            doc = dm_ref[b, :, NV:NV + CW]
            dproj_ref[b, :, OCB:OCB + CW] = (doc * yc).astype(BF16)
            dyc = doc * cb
            dycn = dycn_ref[b]
            row = lax.broadcasted_iota(jnp.int32, dyc.shape, 0)
            d1 = jnp.where(row >= CHUNK - 1, pltpu.roll(dycn, CHUNK - 1, 0), pltpu.roll(dyc, CHUNK - 1, 0))
            d2 = jnp.where(row >= CHUNK - 2, pltpu.roll(dycn, CHUNK - 2, 0), pltpu.roll(dyc, CHUNK - 2, 0))
            du = w2 * dyc + w1 * d1 + w0 * d2
            dproj_ref[b, :, OCC:OCC + CW] = (du * ch).astype(BF16)
            dproj_ref[b, :, OCH:OCH + CW] = (du * cc).astype(BF16)
            dcw_ref[0:1, :] += jnp.sum(dyc * u2, axis=0, keepdims=True)
            dcw_ref[1:2, :] += jnp.sum(dyc * u1, axis=0, keepdims=True)
            dcw_ref[2:3, :] += jnp.sum(dyc * u, axis=0, keepdims=True)
            dycn_ref[b] = dyc

        @pl.when(step == nc - 1)
        def _():
            copies = _stage2_copies(pb_refs, r2_refs, send_sems, recv_sems)
            for cp in copies:
                cp.wait_recv()
            for cp in copies:
                cp.wait_send()

    rev = lambda w: pl.BlockSpec((nb, CHUNK, w), lambda i: (0, nc - 1 - i, 0))
    const = lambda r, c: pl.BlockSpec((r, c), lambda i: (0, 0))
    hbm = pl.BlockSpec(memory_space=pl.ANY)
    return pl.pallas_call(
        body,
        name="mix_bwd",
        grid=(nc,),
        in_specs=[
            rev(PW),
            pl.BlockSpec((nb, CHUNK, 2 * CW), lambda i: (0, jnp.maximum(nc - 2 - i, 0), OCC // (2 * CW))),
            rev(NQK),
            pl.BlockSpec((nb, 1, NQK, DV), lambda i: (0, nc - 1 - i, 0, 0)),
            rev(NV),
            rev(D),
            const(1, DV),
            const(CONV_K, CW),
            const(A_PAD, NQK),
        ] + [hbm] * na,
        out_specs=[rev(PW), const(1, DV), const(8, CW), const(1, NQK), const(A_PAD, NQK)] + [hbm] * na,
        out_shape=[
            jax.ShapeDtypeStruct((nb, s, PW), BF16),
            jax.ShapeDtypeStruct((1, DV), F32),
            jax.ShapeDtypeStruct((8, CW), F32),
            jax.ShapeDtypeStruct((1, NQK), F32),
            jax.ShapeDtypeStruct((A_PAD, NQK), F32),
        ] + [jax.ShapeDtypeStruct((3,) + p.shape[1:], BF16) for p in pbs],
        scratch_shapes=[pltpu.VMEM((nb, NQK, DV), F32), pltpu.VMEM((nb, CHUNK, CW), F32),
                        pltpu.SemaphoreType.DMA((3 * na,)), pltpu.SemaphoreType.DMA((3 * na,))],
        compiler_params=_params(("arbitrary",)),
    )(proj3, proj3, z3, sprev, opre3, dmix3, gng, conv_w, wgu_p, *pbs)


SMALL_PACK_ROWS = 16


def _wgu_slot(r):
    return 4 + r // 4, NQK * (r % 4)


CONV_SLOTS = ((8, 0), (8, CW), (9, 0))


def _in_proj_bwd(dproj2d, x2d, dx1, g1, w_in_t, tm, pb, small_parts):
    t = x2d.shape[0]
    nt = t // tm

    def body(dp_ref, x_ref, dx1_ref, g_ref, w_ref, pb_ref, dg2, dgf, dbg, dgng, dwgu, dcw, lp,
             dx_ref, sums_ref, r2_ref, dg1_acc, pack, gbuf, pack1, gbuf1, send_sems, recv_sems,
             ssend, srecv, ssend1, srecv1):
        x, y, c = _position()
        me = 4 * x + 2 * y + c
        flips = [(k >> 2, (k >> 1) & 1, k & 1) for k in range(1, N_DEV)]
        peers = [(x ^ fx, y ^ fy, c ^ fc) for fx, fy, fc in flips]

        def small_copies(src, dst, send, recv, arrivals):
            return [pltpu.make_async_remote_copy(
                src_ref=src, dst_ref=dst.at[4 * px + 2 * py + pc if arrivals else me],
                send_sem=send.at[k], recv_sem=recv.at[k], device_id=(px, py, pc), device_id_type=MESH)
                for k, (px, py, pc) in enumerate(peers)]

        @pl.when(pl.program_id(0) == 0)
        def _():
            for cp in _stage2_copies([pb_ref], [r2_ref], send_sems, recv_sems):
                cp.start()
            dg1_acc[...] = jnp.zeros_like(dg1_acc)
            pack[...] = jnp.zeros_like(pack)
            pack[1:2, :] = dg2[...]
            pack[2:3, :] = dgf[...]
            pack[3:4, 0:NQK] = dbg[...]
            pack[3:4, NQK:NQK + DV] = dgng[...]
            pack[3:4, NQK + DV:NQK + 2 * DV] = lp[...]
            for r in range(RANK):
                row, lane = _wgu_slot(r)
                pack[row:row + 1, lane:lane + NQK] = dwgu[r:r + 1, :]
            for r, (row, lane) in enumerate(CONV_SLOTS):
                pack[row:row + 1, lane:lane + CW] = dcw[r:r + 1, :]
            for cp in small_copies(pack, gbuf, ssend, srecv, False):
                cp.start()
            gbuf[me] = pack[...]

        xv = x_ref[...]
        r = lax.rsqrt(jnp.mean(xv * xv, axis=-1, keepdims=True) + EPS)
        n1 = xv * r
        dh = _mm(dp_ref[...], w_ref[...])
        dg1_acc[...] += jnp.sum(dh * n1, axis=0, keepdims=True)
        dn = dh * g_ref[...]
        dx_ref[...] = dx1_ref[...] + r * (dn - n1 * jnp.mean(dn * n1, axis=-1, keepdims=True))

        @pl.when(pl.program_id(0) == nt - 1)
        def _():
            pack1[...] = jnp.zeros_like(pack1)
            pack1[0:1, :] = dg1_acc[...]
            for cp in small_copies(pack1, gbuf1, ssend1, srecv1, False):
                cp.start()
            gbuf1[me] = pack1[...]
            copies = _stage2_copies([pb_ref], [r2_ref], send_sems, recv_sems)
            for cp in copies:
                cp.wait_recv()
            for cp in copies:
                cp.wait_send()
            for src, dst, send, recv in ((pack, gbuf, ssend, srecv), (pack1, gbuf1, ssend1, srecv1)):
                for cp in small_copies(src, dst, send, recv, True):
                    cp.wait_recv()
                    cp.wait_send()
            acc = gbuf[0]
            acc1 = gbuf1[0]
            for d in range(1, N_DEV):
                acc = acc + gbuf[d]
                acc1 = acc1 + gbuf1[d]
            sums_ref[...] = acc
            sums_ref[0:1, :] = acc1[0:1, :]

    tile = lambda w: pl.BlockSpec((tm, w), lambda i: (i, 0))
    vec = pl.BlockSpec((1, D), lambda i: (0, 0))
    hbm = pl.BlockSpec(memory_space=pl.ANY)
    whole = lambda a: pl.BlockSpec(a.shape, lambda i: (0,) * a.ndim)
    return pl.pallas_call(
        body,
        name="in_proj_bwd",
        grid=(nt,),
        in_specs=[tile(PW), tile(D), tile(D), vec, pl.BlockSpec((PW, D), lambda i: (0, 0)), hbm]
        + [whole(a) for a in small_parts],
        out_specs=[tile(D), pl.BlockSpec((SMALL_PACK_ROWS, D), lambda i: (0, 0)), hbm],
        out_shape=[jax.ShapeDtypeStruct((t, D), F32), jax.ShapeDtypeStruct((SMALL_PACK_ROWS, D), F32),
                   jax.ShapeDtypeStruct((3,) + pb.shape[1:], BF16)],
        scratch_shapes=[pltpu.VMEM((1, D), F32),
                        pltpu.VMEM((SMALL_PACK_ROWS, D), F32), pltpu.VMEM((N_DEV, SMALL_PACK_ROWS, D), F32),
                        pltpu.VMEM((8, D), F32), pltpu.VMEM((N_DEV, 8, D), F32),
                        pltpu.SemaphoreType.DMA((3,)), pltpu.SemaphoreType.DMA((3,)),
                        pltpu.SemaphoreType.DMA((7,)), pltpu.SemaphoreType.DMA((7,)),
                        pltpu.SemaphoreType.DMA((7,)), pltpu.SemaphoreType.DMA((7,))],
        compiler_params=_params(("arbitrary",)),
    )(dproj2d, x2d, dx1, g1, w_in_t, pb, *small_parts)


def _get_rows(ref):
    return ref[:, 0, :] if len(ref.shape) == 3 else ref[...]


def _put_rows(ref, val):
    if len(ref.shape) == 3:
        ref[:, 0, :] = val
    else:
        ref[...] = val


def _adamw_math(w, g, m, v):
    m = ADAM_B1 * m + (1.0 - ADAM_B1) * g
    v = ADAM_B2 * v + (1.0 - ADAM_B2) * (g * g)
    m_hat = m / (1.0 - ADAM_B1 ** ADAM_STEP)
    v_hat = v / (1.0 - ADAM_B2 ** ADAM_STEP)
    delta = -ADAM_LR * (m_hat / (jnp.sqrt(v_hat) + ADAM_EPS) + ADAM_WD * w)
    return delta, m, v


def _position():
    return lax.axis_index("x"), lax.axis_index("y"), lax.axis_index("c")


GATHER_PARTS = 2
GATHER_SEMS = 7 * GATHER_PARTS


def _gather_copies(stage, lo, rows, gx, send_sems, recv_sems, local_sem):
    x, y, c = _position()
    me = (x, y, c)
    sibling = (x, y, 1 - c)
    chips = [(1 - x, y), (x, 1 - y), (1 - x, 1 - y)]
    part = -(-rows // (16 * GATHER_PARTS)) * 16
    bounds = [(p * part, min(part, rows - p * part)) for p in range(GATHER_PARTS)]

    def blk(px, py, pc, off, n):
        return gx.at[4 * px + 2 * py + pc, pl.ds(off, n), :]

    mine = pltpu.make_async_copy(stage.at[pl.ds(lo, rows), :], gx.at[4 * x + 2 * y + c], local_sem)
    parts = []
    for p, (off, n) in enumerate(bounds):
        def copy(k, block, to, from_stage=False, p=p, off=off, n=n):
            return pltpu.make_async_remote_copy(
                src_ref=stage.at[pl.ds(lo + off, n), :] if from_stage else blk(*block, off, n),
                dst_ref=blk(*block, off, n), send_sem=send_sems.at[7 * p + k], recv_sem=recv_sems.at[7 * p + k],
                device_id=to, device_id_type=MESH)

        first = [copy(0, me, sibling, True)] + [copy(1 + j, me, (*chip, c), True) for j, chip in enumerate(chips)]
        passed = [copy(4 + j, (*chip, c), sibling) for j, chip in enumerate(chips)]
        arrivals = ([copy(0, sibling, me)] + [copy(1 + j, (*chip, c), me) for j, chip in enumerate(chips)]
                    + [copy(4 + j, (*chip, 1 - c), me) for j, chip in enumerate(chips)])
        parts.append((first, passed, arrivals))
    return mine, parts


def _gather_start(*args):
    mine, parts = _gather_copies(*args)
    mine.start()
    for first, _, _ in parts:
        for cp in first:
            cp.start()


def _gather_finish(*args):
    mine, parts = _gather_copies(*args)
    for _, passed, arrivals in parts:
        for j in range(3):
            arrivals[1 + j].wait_recv()
            passed[j].start()
    for first, passed, arrivals in parts:
        arrivals[0].wait_recv()
        for j in range(3):
            arrivals[4 + j].wait_recv()
        for cp in first + passed:
            cp.wait_send()
    mine.wait()


def _gather_sems():
    return [pltpu.SemaphoreType.DMA((GATHER_SEMS,)), pltpu.SemaphoreType.DMA((GATHER_SEMS,)), pltpu.SemaphoreType.DMA]


def _gather_w_in(w_it, w_gt, w_ut, w_d, w_o, wgu_s, conv_s):
    def body(wi_ref, wg_ref, wu_ref, wd_ref, wo_ref, wgu_ref, conv_ref, w_ref, gwgu_ref, gconv_ref, stage,
             buf, send_sems, recv_sems, local_sem, ssend, srecv):
        x, y, c = _position()
        me = 4 * x + 2 * y + c
        stage[SLAB_IN:SLAB_IN + IN_W, :] = wi_ref[:, 0, :].astype(BF16)
        stage[SLAB_IN + IN_W:SLAB_G, :] = jnp.zeros((IN_ROWS - IN_W, D), BF16)
        args = (stage, SLAB_IN, IN_ROWS, buf, send_sems, recv_sems, local_sem)
        _gather_start(*args)
        stage[SLAB_G:SLAB_U, :] = wg_ref[...].astype(BF16)
        stage[SLAB_U:SLAB_D, :] = wu_ref[...].astype(BF16)
        stage[SLAB_D:SLAB_O, :] = wd_ref[...].astype(BF16)
        stage[SLAB_O:SLAB_ROWS, :] = wo_ref[...].astype(BF16)
        flips = [(k >> 2, (k >> 1) & 1, k & 1) for k in range(1, N_DEV)]
        peers = [(x ^ fx, y ^ fy, c ^ fc) for fx, fy, fc in flips]

        def small(k, block_id, to):
            return [pltpu.make_async_remote_copy(
                src_ref=s, dst_ref=g.at[block_id], send_sem=ssend.at[2 * k + n], recv_sem=srecv.at[2 * k + n],
                device_id=to, device_id_type=MESH)
                for n, (s, g) in enumerate(((wgu_ref, gwgu_ref), (conv_ref, gconv_ref)))]

        gwgu_ref[me] = wgu_ref[...]
        gconv_ref[me] = conv_ref[...]
        for k, peer in enumerate(peers):
            for cp in small(k, me, peer):
                cp.start()
        w_ref[IN_COLS:PW, :] = jnp.zeros((PW - IN_COLS, D), BF16)
        _gather_finish(*args)
        for k, (px, py, pc) in enumerate(peers):
            for cp in small(k, 4 * px + 2 * py + pc, (px, py, pc)):
                cp.wait_recv()
                cp.wait_send()
        for j, lo, hi, d in _in_segments():
            w_ref[d:d + hi - lo, :] = buf[j, lo:hi, :]

    vm = pl.BlockSpec(memory_space=pltpu.VMEM)
    return pl.pallas_call(
        body,
        name="gather_w_in",
        in_specs=[vm] * 7,
        out_specs=[vm] * 4,
        out_shape=[jax.ShapeDtypeStruct((PW, D), BF16),
                   jax.ShapeDtypeStruct((N_DEV,) + wgu_s.shape, F32),
                   jax.ShapeDtypeStruct((N_DEV,) + conv_s.shape, F32),
                   jax.ShapeDtypeStruct((SLAB_ROWS, D), BF16)],
        scratch_shapes=[pltpu.VMEM((N_DEV, IN_ROWS, D), BF16)] + _gather_sems()
        + [pltpu.SemaphoreType.DMA((14,)), pltpu.SemaphoreType.DMA((14,))],
        compiler_params=_params(),
    )(w_it, w_gt, w_ut, w_d, w_o, wgu_s, conv_s)


def _w_in_core_reduce(dw_t):
    def body(d_ref, own_ref, sib_ref, pb_ref, g, gb, r1, send_sems, recv_sems):
        x, y, c = _position()
        chip = 2 * x + y
        for j in range(N_DEV):
            g[j, IN_W:IN_ROWS, :] = jnp.zeros((IN_ROWS - IN_W, D), F32)
        for j, lo, hi, d in _in_segments():
            g[j, lo:hi, :] = d_ref[d:d + hi - lo, :]
        for j in range(N_DEV):
            gb[j] = g[j].astype(BF16)
        copies = _stage1_copies(gb, r1, send_sems, recv_sems)
        for cp in copies:
            cp.start()
        own_ref[0] = g[2 * chip + c]
        for cp in copies:
            cp.wait_recv()
        sib_ref[0] = r1[chip]
        for k in range(1, 4):
            t = chip ^ k
            pb_ref[k - 1] = (g[2 * t + c] + r1[t].astype(F32)).astype(BF16)
        for cp in copies:
            cp.wait_send()

    vm = pl.BlockSpec(memory_space=pltpu.VMEM)
    return pl.pallas_call(
        body,
        name="w_in_core_reduce",
        in_specs=[vm],
        out_specs=[vm, vm, vm],
        out_shape=[jax.ShapeDtypeStruct((1, IN_ROWS, D), F32), jax.ShapeDtypeStruct((1, IN_ROWS, D), BF16),
                   jax.ShapeDtypeStruct((3, IN_ROWS, D), BF16)],
        scratch_shapes=[pltpu.VMEM((N_DEV, IN_ROWS, D), F32), pltpu.VMEM((N_DEV, IN_ROWS, D), BF16),
                        pltpu.VMEM((4, IN_ROWS, D), BF16), pltpu.SemaphoreType.DMA((4,)),
                        pltpu.SemaphoreType.DMA((4,))],
        compiler_params=_params(),
    )(dw_t)


def _stage1_copies(g_ref, r_ref, send_sems, recv_sems):
    x, y, c = _position()
    return [pltpu.make_async_remote_copy(
        src_ref=g_ref.at[2 * i + 1 - c], dst_ref=r_ref.at[i], send_sem=send_sems.at[i], recv_sem=recv_sems.at[i],
        device_id=(x, y, 1 - c), device_id_type=MESH) for i in range(4)]


def _ffn_core_reduce(dw3, dwb3, dw_o, dwb_o, pos_arr, dx1b, gwb):
    def body(pos_ref, g0, g1, g2, go, gb3_hbm, gbo_hbm, dx1b_ref, gwb_hbm, p0, p1, p2, po, s0, s1, s2, so, dmix_ref,
             r1f, r1o, wo, send_sems, recv_sems, wsem):
        step = pl.program_id(0)
        k = jnp.minimum(step, 2)
        x, y, c = _position()
        chip = 2 * x + y

        def copies(p):
            src = 2 * (chip ^ ((p + 1) & 3)) + 1 - c
            pairs = [(gb3_hbm.at[a, src], r1f.at[a, p]) for a in range(3)] + [(gbo_hbm.at[src], r1o.at[p])]
            return [pltpu.make_async_remote_copy(
                src_ref=s, dst_ref=d, send_sem=send_sems.at[4 * p + a], recv_sem=recv_sems.at[4 * p + a],
                device_id=(x, y, 1 - c), device_id_type=MESH) for a, (s, d) in enumerate(pairs)]

        @pl.when(step == 0)
        def _():
            for p in range(4):
                for cp in copies(p):
                    cp.start()
            loads = [pltpu.make_async_copy(gwb_hbm.at[j, pl.ds(FF_W, OUT_ROWS), :],
                                           wo.at[pl.ds(OUT_ROWS * j, OUT_ROWS), :], wsem.at[j]) for j in range(N_DEV)]
            for cp in loads:
                cp.start()
            for cp in loads:
                cp.wait()

        dmix_ref[...] = _nt(dx1b_ref[...], wo[...])

        for p in range(3):
            @pl.when(step == p)
            def _():
                for cp in copies(p):
                    cp.wait_recv()

        for a, (g, pb) in enumerate(((g0, p0), (g1, p1), (g2, p2))):
            pb[...] = (g[...] + r1f[a, k][None].astype(F32)).astype(BF16)
        po[...] = (go[...] + r1o[k][None].astype(F32)).astype(BF16)

        @pl.when(step == 3)
        def _():
            for cp in copies(3):
                cp.wait_recv()
            for a, s in enumerate((s0, s1, s2)):
                s[0] = r1f[a, 3]
            so[0] = r1o[3]
            for p in range(4):
                for cp in copies(p):
                    cp.wait_send()

    t = dx1b.shape[0]
    other = lambda s, pos: 2 * (pos[1] ^ (jnp.minimum(s, 2) + 1)) + pos[0]
    g_spec = lambda lead: pl.BlockSpec((None, 1, FF_W, D), lambda s, pos: (lead, other(s, pos), 0, 0))
    slot = lambda rows: pl.BlockSpec((1, rows, D), lambda s, pos: (jnp.minimum(s, 2), 0, 0))
    one = lambda rows: pl.BlockSpec((1, rows, D), lambda s, pos: (0, 0, 0))
    quarter = pl.BlockSpec((t // 4, D), lambda s, pos: (s, 0))
    hbm = pl.BlockSpec(memory_space=pl.ANY)
    return pl.pallas_call(
        body,
        name="ffn_core_reduce",
        grid_spec=pltpu.PrefetchScalarGridSpec(
            num_scalar_prefetch=1, grid=(4,),
            in_specs=[g_spec(0), g_spec(1), g_spec(2),
                      pl.BlockSpec((1, OUT_ROWS, D), lambda s, pos: (other(s, pos), 0, 0)), hbm, hbm, quarter, hbm],
            out_specs=[slot(FF_W), slot(FF_W), slot(FF_W), slot(OUT_ROWS),
                       one(FF_W), one(FF_W), one(FF_W), one(OUT_ROWS), quarter],
            scratch_shapes=[pltpu.VMEM((3, 4, FF_W, D), BF16), pltpu.VMEM((4, OUT_ROWS, D), BF16),
                            pltpu.VMEM((D, D), BF16), pltpu.SemaphoreType.DMA((16,)),
                            pltpu.SemaphoreType.DMA((16,)), pltpu.SemaphoreType.DMA((N_DEV,))]),
        out_shape=[jax.ShapeDtypeStruct((3, FF_W, D), BF16)] * 3 + [jax.ShapeDtypeStruct((3, OUT_ROWS, D), BF16)]
        + [jax.ShapeDtypeStruct((1, FF_W, D), BF16)] * 3 + [jax.ShapeDtypeStruct((1, OUT_ROWS, D), BF16),
                                                             jax.ShapeDtypeStruct((t, D), F32)],
        compiler_params=_params(("arbitrary",)),
    )(pos_arr, dw3, dw3, dw3, dw_o, dwb3, dwb_o, dx1b, gwb)


def _stage2_copies(p_refs, r_refs, send_sems, recv_sems):
    x, y, c = _position()
    copies = []
    for a in range(len(p_refs)):
        for k in range(1, 4):
            copies.append(pltpu.make_async_remote_copy(
                src_ref=p_refs[a].at[k - 1], dst_ref=r_refs[a].at[k - 1],
                send_sem=send_sems.at[3 * a + k - 1], recv_sem=recv_sems.at[3 * a + k - 1],
                device_id=(x ^ (k >> 1), y ^ (k & 1), c), device_id_type=MESH))
    return copies


def _finish_weights(items, pos_arr, name, nblk):
    n = len(items)
    in_specs, out_specs, out_shape, operands, wbs = [], [], [], [], []
    for g8, lead, r1, r2, w, m, v in items:
        rows, wr = g8.shape[-2], w.shape[0]
        assert rows % nblk == 0 and wr % nblk == 0 and (nblk == 1 or (rows == wr and rows % (16 * nblk) == 0))
        rb, wb = rows // nblk, wr // nblk
        if lead is not None:
            g_spec = pl.BlockSpec((None, 1, rb, D), lambda i, pos, lead=lead: (lead, 2 * pos[1] + pos[0], i, 0))
        elif g8.shape[0] == 1:
            g_spec = pl.BlockSpec((1, rb, D), lambda i, pos: (0, i, 0))
        else:
            g_spec = pl.BlockSpec((1, rb, D), lambda i, pos: (2 * pos[1] + pos[0], i, 0))
        r1_spec = pl.BlockSpec((1, rb, D), lambda i, pos: (0, i, 0))
        if w.ndim == 3:
            wblk = pl.BlockSpec((wb, 1, D), lambda i, pos: (i, 0, 0))
        else:
            wblk = pl.BlockSpec((wb, D), lambda i, pos: (i, 0))
        in_specs += [g_spec, r1_spec, pl.BlockSpec((3, rb, D), lambda i, pos: (0, i, 0)), wblk, wblk, wblk]
        out_specs += [wblk] * 4
        out_shape += [jax.ShapeDtypeStruct(w.shape, F32)] * 4
        operands += [g8, r1, r2, w, m, v]
        wbs.append(wb)

    def body(pos_ref, *refs):
        for a in range(n):
            g_ref, r1_ref, r2_ref, w_ref, m_ref, v_ref = refs[6 * a:6 * a + 6]
            g_out, d_out, m_out, v_out = refs[6 * n + 4 * a:6 * n + 4 * a + 4]
            g = g_ref[0] + r1_ref[0].astype(F32)
            for k in range(3):
                g = g + r2_ref[k].astype(F32)
            g = g[0:wbs[a], :]
            d, mn, vn = _adamw_math(_get_rows(w_ref), g, _get_rows(m_ref), _get_rows(v_ref))
            for out, val in ((g_out, g), (d_out, d), (m_out, mn), (v_out, vn)):
                _put_rows(out, val)

    return pl.pallas_call(
        body,
        name=name,
        grid_spec=pltpu.PrefetchScalarGridSpec(
            num_scalar_prefetch=1, grid=(nblk,), in_specs=in_specs, out_specs=out_specs),
        out_shape=out_shape,
        compiler_params=_params(("arbitrary",)),
    )(pos_arr, *operands)


SMALL_NAMES = ("norm1_g", "norm2_g", "norm_f_g", "b_gate", "gla_norm_g", "w_gate_up", "conv_w")
WGU_W = NQK // N_DEV
CONV_W = CW // N_DEV


def _small_adamw(sums, ws, ms, vs):
    n = len(SMALL_NAMES)

    def body(*refs):
        acc_ref = refs[0]
        w_refs, m_refs, v_refs = refs[1:1 + n], refs[1 + n:1 + 2 * n], refs[1 + 2 * n:1 + 3 * n]
        loss_ref = refs[1 + 3 * n]
        outs = refs[2 + 3 * n:]
        x, y, c = _position()
        me = 4 * x + 2 * y + c
        acc = acc_ref[...]
        loss_ref[...] = acc[3:4, NQK + DV:NQK + DV + 1]

        def my_columns(full, width):
            r = lax.broadcasted_iota(jnp.int32, (full.shape[1], width), 0)
            col = lax.broadcasted_iota(jnp.int32, (full.shape[1], width), 1)
            sel = (r == width * me + col).astype(F32)
            return _mm(full, sel, precision=HIGHEST)

        dwgu = jnp.concatenate([acc[row:row + 1, lane:lane + NQK] for row, lane in map(_wgu_slot, range(RANK))], axis=0)
        dcw = jnp.concatenate([acc[row:row + 1, lane:lane + CW] for row, lane in CONV_SLOTS], axis=0)
        grads = [acc[0:1, :], acc[1:2, :], acc[2:3, :], acc[3:4, 0:NQK], acc[3:4, NQK:NQK + DV],
                 my_columns(dwgu, WGU_W), my_columns(dcw, CONV_W)]
        for i, g in enumerate(grads):
            d, mn, vn = _adamw_math(_get_rows(w_refs[i]), g, _get_rows(m_refs[i]), _get_rows(v_refs[i]))
            for out, val in zip(outs[4 * i:4 * i + 4], (g, d, mn, vn)):
                _put_rows(out, val)

    vm = pl.BlockSpec(memory_space=pltpu.VMEM)
    out_shape = [jax.ShapeDtypeStruct((1, 1), F32)]
    for w in ws:
        out_shape += [jax.ShapeDtypeStruct(w.shape, F32)] * 4
    return pl.pallas_call(
        body,
        name="small_adamw",
        in_specs=[vm] * (1 + 3 * n),
        out_specs=[vm] * (1 + 4 * n),
        out_shape=out_shape,
        compiler_params=_params(),
    )(sums, *ws, *ms, *vs)


def kernel(x, norm1_g, w_in, w_gate_up, b_gate, gla_norm_g, conv_w, w_out, norm2_g, w_ffn_gate, w_ffn_up, w_ffn_down, norm_f_g, loss_target, m_norm1_g, m_w_in, m_w_gate_up, m_b_gate, m_gla_norm_g, m_conv_w, m_w_out, m_norm2_g, m_w_ffn_gate, m_w_ffn_up, m_w_ffn_down, m_norm_f_g, v_norm1_g, v_w_in, v_w_gate_up, v_b_gate, v_gla_norm_g, v_conv_w, v_w_out, v_norm2_g, v_w_ffn_gate, v_w_ffn_up, v_w_ffn_down, v_norm_f_g):
    xi, yi, ci = _position()
    pos_arr = jnp.stack([ci, 2 * xi + yi]).astype(jnp.int32)
    nb, s, _ = x.shape
    t = nb * s

    tr = lambda a: a[0].T
    rows_of = lambda a: a.transpose(2, 0, 1)
    conv_rows = lambda a: a.transpose(1, 0, 2)
    w_in_t, gwgu, gconv, stage = _gather_w_in(rows_of(w_in), tr(w_ffn_gate), tr(w_ffn_up), w_ffn_down[0], w_out[0],
                                              w_gate_up[0], conv_rows(conv_w))
    wgu_f = gwgu.transpose(1, 0, 2).reshape(RANK, NQK)
    conv_f = gconv.transpose(1, 2, 0, 3).reshape(CONV_K, CW)
    wgu_p = jnp.concatenate([wgu_f, jnp.zeros((A_PAD - RANK, NQK), F32)], axis=0).astype(BF16)

    x2d = x.reshape(t, D)
    tgt2d = loss_target.reshape(t, D)
    tm = 256
    tm_in = min(512, t)
    tk = min(2048, t)
    proj, z, h, gwb = _in_proj_fwd(x2d, norm1_g, w_in_t, wgu_p, b_gate, tm_in, stage)
    proj3 = proj.reshape(nb, s, PW)
    z3 = z.reshape(nb, s, NQK)
    mix3, opre3, sprev, x1, gwa = _mix_fwd(proj3, z3, gla_norm_g, conv_f, stage, x, gwb)
    mix2d = mix3.reshape(t, D)
    dx1, dx1b, adu, hb, dg2, dgf, loss_part = _ffn_fwd_bwd(
        x1.reshape(t, D), tgt2d, gwa, gwb, norm2_g, norm_f_g.reshape(1, D), tm)
    dw3, dwb3 = _dw_ffn(adu, hb, tk)
    dw3 = dw3.reshape(3, N_DEV, FF_W, D)
    dw_o, dwb_o = _tn_matmul(mix2d, dx1b, D // 2, D, tk, "dw_out", True)
    dw_o = dw_o.reshape(N_DEV, OUT_ROWS, D)
    *pb, sib_d, sib_g, sib_u, sib_o, dmix = _ffn_core_reduce(
        dw3, dwb3.reshape(3, N_DEV, FF_W, D), dw_o, dwb_o.reshape(N_DEV, OUT_ROWS, D), pos_arr, dx1b, gwb)
    g8 = [dw3, dw3, dw3, dw_o]
    leads = [0, 1, 2, None]
    tags = ("w_ffn_down", "w_ffn_gate", "w_ffn_up", "w_out")
    r1 = [sib_d, sib_g, sib_u, sib_o]
    mb = _mix_bwd(proj3, z3, sprev, opre3, dmix.reshape(nb, s, D), gla_norm_g, conv_f, wgu_p, [pb[0], pb[1], pb[3]])
    dproj3, dgng, dcw, dbg, dwgu = mb[:5]
    dproj2d = dproj3.reshape(t, PW)
    dw_in_t, r2_up = _tn_matmul(dproj2d, h, PW // 5, D, tk, "dw_in", False, _stage2_rider([pb[2]]))
    r2 = [mb[5], mb[6], r2_up, mb[7]]
    g_in, r1_in, pb_in = _w_in_core_reduce(dw_in_t)
    dx, small_sums, r2_in = _in_proj_bwd(dproj2d, x2d, dx1, norm1_g, w_in_t, tm_in, pb_in,
                                         (dg2, dgf, dbg, dgng, dwgu, dcw, loss_part))

    tags = ("w_in",) + tags
    g8 = [g_in] + g8
    leads = [None] + leads
    r1 = [r1_in] + list(r1)
    r2 = [r2_in] + r2
    shard_w = (rows_of(w_in), w_ffn_down[0], tr(w_ffn_gate), tr(w_ffn_up), w_out[0])
    shard_m = (rows_of(m_w_in), m_w_ffn_down[0], tr(m_w_ffn_gate), tr(m_w_ffn_up), m_w_out[0])
    shard_v = (rows_of(v_w_in), v_w_ffn_down[0], tr(v_w_ffn_gate), tr(v_w_ffn_up), v_w_out[0])
    back = (lambda o: o.transpose(1, 2, 0), lambda o: o[None], lambda o: o.T[None], lambda o: o.T[None],
            lambda o: o[None])
    items = list(zip(g8, leads, r1, r2, shard_w, shard_m, shard_v))
    flat = list(_finish_weights(items[1:], pos_arr, "finish_ffn_out", 2))
    flat = list(_finish_weights(items[:1], pos_arr, "finish_w_in", 1)) + flat
    results = {}
    for i, (tag, to_shard) in enumerate(zip(tags, back)):
        results[tag] = [to_shard(o) for o in flat[4 * i:4 * i + 4]]

    small_w = (norm1_g, norm2_g, norm_f_g.reshape(1, D), b_gate, gla_norm_g, w_gate_up[0], conv_rows(conv_w))
    small_m = (m_norm1_g, m_norm2_g, m_norm_f_g.reshape(1, D), m_b_gate, m_gla_norm_g, m_w_gate_up[0],
               conv_rows(m_conv_w))
    small_v = (v_norm1_g, v_norm2_g, v_norm_f_g.reshape(1, D), v_b_gate, v_gla_norm_g, v_w_gate_up[0],
               conv_rows(v_conv_w))
    so = _small_adamw(small_sums, small_w, small_m, small_v)
    loss = so[0].reshape(())
    to_shape = {"norm_f_g": lambda o: o.reshape(D), "w_gate_up": lambda o: o[None],
                "conv_w": lambda o: o.transpose(1, 0, 2)}
    for i, name in enumerate(SMALL_NAMES):
        results[name] = [to_shape.get(name, lambda o: o)(o) for o in so[1 + 4 * i:5 + 4 * i]]

    names = ("norm1_g", "w_in", "w_gate_up", "b_gate", "gla_norm_g", "conv_w", "w_out", "norm2_g",
             "w_ffn_gate", "w_ffn_up", "w_ffn_down", "norm_f_g")
    outs = [loss, dx.reshape(nb, s, D)]
    for kind in range(4):
        for name in names:
            outs.append(results[name][kind])
    return tuple(outs)
```

```python
import jax
import jax.numpy as jnp
from jax import lax
from jax.experimental import pallas as pl
from jax.experimental.pallas import tpu as pltpu

F32 = jnp.float32
BF16 = jnp.bfloat16
HIGHEST = lax.Precision.HIGHEST
MESH = pl.DeviceIdType.MESH

N_DEV = 8
D = 1024
DFF = 2816
HEADS = 4
DK = 64
DV = 128
NQK = HEADS * DK
NV = HEADS * DV
RANK = 16
CHUNK = 64
CW = 512
CONV_K = 3
IN_COLS = 3088
EPS = 1e-6
INV_GATE_NORM = 1.0 / 16.0
Q_SCALE = DK ** -0.5

PW = 3200
OQ, OK_, OV, OG, OCB, OCC, OCH, OA = 0, 256, 512, 1024, 1536, 2048, 2560, 3072
A_PAD = 128

ADAM_LR = 0.001
ADAM_B1 = 0.9
ADAM_B2 = 0.999
ADAM_EPS = 1e-08
ADAM_WD = 0.01
ADAM_STEP = 10

IN_W = IN_COLS // N_DEV
IN_ROWS = 400
FF_W = DFF // N_DEV
OUT_ROWS = D // N_DEV
SLAB_IN = 0
SLAB_G = SLAB_IN + IN_ROWS
SLAB_U = SLAB_G + FF_W
SLAB_D = SLAB_U + FF_W
SLAB_O = SLAB_D + FF_W
SLAB_ROWS = SLAB_O + OUT_ROWS

VMEM_LIMIT = 56 * 1024 * 1024


def _params(sem=None, vmem=VMEM_LIMIT):
    return pltpu.CompilerParams(dimension_semantics=sem, vmem_limit_bytes=vmem)


def _nt(a, b):
    return lax.dot_general(a, b, (((1,), (1,)), ((), ())), preferred_element_type=F32)


def _tn(a, b, precision=None):
    return lax.dot_general(a, b, (((0,), (0,)), ((), ())), preferred_element_type=F32, precision=precision)


def _mm(a, b, precision=None):
    return jnp.dot(a, b, preferred_element_type=F32, precision=precision)


def _in_segments():
    segs = []
    for j in range(N_DEV):
        lo, hi = IN_W * j, IN_W * (j + 1)
        cuts = sorted({lo, hi} | {c for c in (OCB, OCB + RANK) if lo < c < hi})
        for a, b in zip(cuts[:-1], cuts[1:]):
            if a < OCB:
                d = a
            elif a < OCB + RANK:
                d = OA + (a - OCB)
            else:
                d = a - RANK
            segs.append((j, a - lo, b - lo, d))
    return segs


def _in_proj_fwd(x2d, g1, w_in_t, wgu_p, b_gate, tm, stage):
    t = x2d.shape[0]
    nt = t // tm
    g_rows = SLAB_ROWS - SLAB_D

    def body(x_ref, g_ref, w_ref, wgu_ref, bg_ref, stage_hbm, proj_ref, z_ref, h_ref, gwb_ref,
             send_sems, recv_sems, local_sem):
        gargs = (stage_hbm, SLAB_D, g_rows, gwb_ref, send_sems, recv_sems, local_sem)

        @pl.when(pl.program_id(0) == 0)
        def _():
            _gather_start(*gargs)

        x = x_ref[...]
        r = lax.rsqrt(jnp.mean(x * x, axis=-1, keepdims=True) + EPS)
        h = ((x * r) * g_ref[...]).astype(BF16)
        h_ref[...] = h
        proj = _nt(h, w_ref[...])
        proj_ref[...] = proj
        pa = proj[:, OA:OA + A_PAD].astype(BF16)
        z_ref[...] = _mm(pa, wgu_ref[...]) + bg_ref[...]

        @pl.when(pl.program_id(0) == nt - 1)
        def _():
            _gather_finish(*gargs)

    return pl.pallas_call(
        body,
        name="in_proj_fwd",
        grid=(t // tm,),
        in_specs=[
            pl.BlockSpec((tm, D), lambda i: (i, 0)),
            pl.BlockSpec((1, D), lambda i: (0, 0)),
            pl.BlockSpec((PW, D), lambda i: (0, 0)),
            pl.BlockSpec((A_PAD, NQK), lambda i: (0, 0)),
            pl.BlockSpec((1, NQK), lambda i: (0, 0)),
            pl.BlockSpec(memory_space=pl.ANY),
        ],
        out_specs=[
            pl.BlockSpec((tm, PW), lambda i: (i, 0)),
            pl.BlockSpec((tm, NQK), lambda i: (i, 0)),
            pl.BlockSpec((tm, D), lambda i: (i, 0)),
            pl.BlockSpec(memory_space=pl.ANY),
        ],
        out_shape=[
            jax.ShapeDtypeStruct((t, PW), F32),
            jax.ShapeDtypeStruct((t, NQK), F32),
            jax.ShapeDtypeStruct((t, D), BF16),
            jax.ShapeDtypeStruct((N_DEV, g_rows, D), BF16),
        ],
        scratch_shapes=_gather_sems(),
        compiler_params=_params(("arbitrary",)),
    )(x2d, g1, w_in_t, wgu_p, b_gate, stage)


def _head_masks():
    lane = lax.broadcasted_iota(jnp.int32, (1, NQK), 1)
    return [(lane >= DK * h) & (lane < DK * (h + 1)) for h in range(HEADS)]


def _split_bf16(x, n):
    parts = []
    for _ in range(n):
        p = x.astype(BF16)
        parts.append(p)
        x = x - p.astype(F32)
    return parts


def _chunk_fwd_parts(q, k, z, tril16):
    la = (jnp.minimum(z, 0.0) - jnp.log1p(jnp.exp(-jnp.abs(z)))) * INV_GATE_NORM
    la_parts = _split_bf16(la, 3)
    bc = _mm(tril16, la_parts[0]) + _mm(tril16, la_parts[1]) + _mm(tril16, la_parts[2])
    bl = bc[CHUNK - 1:CHUNK, :]
    eb = jnp.exp(bc)
    enb = jnp.exp(-bc)
    ekl = jnp.exp(bl - bc)
    qi = (q * Q_SCALE) * eb
    ki = k * enb
    ks = k * ekl
    ones16 = jnp.ones((CHUNK, DV), BF16)
    decb = jnp.exp(_tn(la_parts[0], ones16) + _tn(la_parts[1], ones16) + _tn(la_parts[2], ones16))
    return la, eb, enb, ekl, qi, ki, ks, decb


def _stack_heads(a, masks):
    return jnp.concatenate([jnp.where(m, a, 0.0) for m in masks], axis=0)


def _merge_heads(blocks, masks):
    out = blocks[HEADS - 1]
    for h in range(HEADS - 2, -1, -1):
        out = jnp.where(masks[h], blocks[h], out)
    return out


def _causal_stack_mask():
    row = lax.broadcasted_iota(jnp.int32, (HEADS * CHUNK, CHUNK), 0)
    col = lax.broadcasted_iota(jnp.int32, (HEADS * CHUNK, CHUNK), 1)
    return (row & (CHUNK - 1)) >= col


def _conv_taps(u, uprev):
    row = lax.broadcasted_iota(jnp.int32, u.shape, 0)
    u1 = jnp.where(row < 1, pltpu.roll(uprev, 1, 0), pltpu.roll(u, 1, 0))
    u2 = jnp.where(row < 2, pltpu.roll(uprev, 2, 0), pltpu.roll(u, 2, 0))
    return u1, u2


def _mix_fwd(proj3, z3, gng, conv_w, stage, x3, gwb):
    nb, s, _ = proj3.shape
    nc = s // CHUNK
    g_rows = SLAB_D - SLAB_G

    def body(p_ref, z_ref, gng_ref, cw_ref, stage_hbm, x_ref, gwb_hbm, mix_ref, o_ref, sprev_ref, x1_ref, gwa_ref,
             s_ref, uprev_ref, wo, wsem, send_sems, recv_sems, local_sem):
        n = pl.program_id(0)
        gargs = (stage_hbm, SLAB_G, g_rows, gwa_ref, send_sems, recv_sems, local_sem)

        @pl.when(n == 0)
        def _():
            _gather_start(*gargs)
            loads = [pltpu.make_async_copy(gwb_hbm.at[j, pl.ds(FF_W, OUT_ROWS), :],
                                           wo.at[pl.ds(OUT_ROWS * j, OUT_ROWS), :], wsem.at[j]) for j in range(N_DEV)]
            for cp in loads:
                cp.start()
            s_ref[...] = jnp.zeros_like(s_ref)
            uprev_ref[...] = jnp.zeros_like(uprev_ref)
            for cp in loads:
                cp.wait()

        r_i = lax.broadcasted_iota(jnp.int32, (CHUNK, CHUNK), 0)
        c_i = lax.broadcasted_iota(jnp.int32, (CHUNK, CHUNK), 1)
        tril16 = (r_i >= c_i).astype(BF16)
        masks = _head_masks()
        cmask = _causal_stack_mask()
        gg = gng_ref[...]
        for b in range(nb):
            q = p_ref[b, :, OQ:OQ + NQK]
            k = p_ref[b, :, OK_:OK_ + NQK]
            _, _, _, _, qi, ki, ks, decb = _chunk_fwd_parts(q, k, z_ref[b], tril16)
            qs = _stack_heads(qi, masks).astype(BF16)
            sc = jnp.where(cmask, _nt(qs, ki.astype(BF16)), 0.0).astype(BF16)
            st = s_ref[b]
            sprev_ref[b, 0] = st
            o_inter = _mm(qs, st.astype(BF16))
            v16 = p_ref[b, :, OV:OV + NV].astype(BF16)
            kv = _tn(ks.astype(BF16), v16)
            for h in range(HEADS):
                rows = slice(CHUNK * h, CHUNK * (h + 1))
                cols = slice(DV * h, DV * (h + 1))
                o = _mm(sc[rows], v16[:, cols]) + o_inter[rows]
                o_ref[b, :, cols] = o
                r = lax.rsqrt(jnp.mean(o * o, axis=-1, keepdims=True) + EPS)
                on = (o * r) * gg
                g = p_ref[b, :, OG + DV * h:OG + DV * (h + 1)]
                mix_ref[b, :, cols] = (on * (g * jax.nn.sigmoid(g))).astype(BF16)
                s_ref[b, rows, :] = decb[rows] * st[rows] + kv[rows, cols]
            u = p_ref[b, :, OCC:OCC + CW] * p_ref[b, :, OCH:OCH + CW]
            u1, u2 = _conv_taps(u, uprev_ref[b])
            yc = cw_ref[0:1, :] * u2 + cw_ref[1:2, :] * u1 + cw_ref[2:3, :] * u
            mix_ref[b, :, NV:NV + CW] = (p_ref[b, :, OCB:OCB + CW] * yc).astype(BF16)
            uprev_ref[b] = u
        mixed = _mm(jnp.concatenate([mix_ref[b] for b in range(nb)], axis=0), wo[...])
        for b in range(nb):
            x1_ref[b] = x_ref[b] + mixed[CHUNK * b:CHUNK * (b + 1)]

        @pl.when(n == nc - 1)
        def _():
            _gather_finish(*gargs)

    return pl.pallas_call(
        body,
        name="mix_fwd",
        grid=(nc,),
        in_specs=[
            pl.BlockSpec((nb, CHUNK, PW), lambda n: (0, n, 0)),
            pl.BlockSpec((nb, CHUNK, NQK), lambda n: (0, n, 0)),
            pl.BlockSpec((1, DV), lambda n: (0, 0)),
            pl.BlockSpec((CONV_K, CW), lambda n: (0, 0)),
            pl.BlockSpec(memory_space=pl.ANY),
            pl.BlockSpec((nb, CHUNK, D), lambda n: (0, n, 0)),
            pl.BlockSpec(memory_space=pl.ANY),
        ],
        out_specs=[
            pl.BlockSpec((nb, CHUNK, D), lambda n: (0, n, 0)),
            pl.BlockSpec((nb, CHUNK, NV), lambda n: (0, n, 0)),
            pl.BlockSpec((nb, 1, NQK, DV), lambda n: (0, n, 0, 0)),
            pl.BlockSpec((nb, CHUNK, D), lambda n: (0, n, 0)),
            pl.BlockSpec(memory_space=pl.ANY),
        ],
        out_shape=[
            jax.ShapeDtypeStruct((nb, s, D), BF16),
            jax.ShapeDtypeStruct((nb, s, NV), F32),
            jax.ShapeDtypeStruct((nb, nc, NQK, DV), F32),
            jax.ShapeDtypeStruct((nb, s, D), F32),
            jax.ShapeDtypeStruct((N_DEV, g_rows, D), BF16),
        ],
        scratch_shapes=[pltpu.VMEM((nb, NQK, DV), F32), pltpu.VMEM((nb, CHUNK, CW), F32),
                        pltpu.VMEM((D, D), BF16), pltpu.SemaphoreType.DMA((N_DEV,))] + _gather_sems(),
        compiler_params=_params(("arbitrary",)),
    )(proj3, z3, gng, conv_w, stage, x3, gwb)


def _ffn_fwd_bwd(x1_2d, tgt2d, gwa, gwb, g2, gf, tm):
    t = x1_2d.shape[0]

    def body(x1_ref, tgt_ref, g2_ref, gf_ref, gwa_hbm, gwb_hbm,
             dx1_ref, dx1b_ref, adu_ref, hb_ref, dg2_ref, dgf_ref, loss_ref,
             wg, wu, wd, wsem):
        i = pl.program_id(0)

        def weight_copies(n, dst, src, off, rows):
            return [pltpu.make_async_copy(src.at[j, pl.ds(off, rows), :], dst.at[pl.ds(rows * j, rows), :],
                                          wsem.at[N_DEV * n + j]) for j in range(N_DEV)]

        loads = (weight_copies(0, wg, gwa_hbm, 0, FF_W), weight_copies(1, wu, gwa_hbm, FF_W, FF_W),
                 weight_copies(2, wd, gwb_hbm, 0, FF_W))

        @pl.when(i == 0)
        def _():
            for group in loads:
                for cp in group:
                    cp.start()
            dg2_ref[...] = jnp.zeros_like(dg2_ref)
            dgf_ref[...] = jnp.zeros_like(dgf_ref)
            loss_ref[...] = jnp.zeros_like(loss_ref)
            for group in loads:
                for cp in group:
                    cp.wait()

        g2v = g2_ref[...]
        gfv = gf_ref[...]
        x1 = x1_ref[...]
        r2 = lax.rsqrt(jnp.mean(x1 * x1, axis=-1, keepdims=True) + EPS)
        n2 = x1 * r2
        h2 = (n2 * g2v).astype(BF16)
        hb_ref[1] = h2
        gate = _nt(h2, wg[...])
        up = _nt(h2, wu[...])
        sg = jax.nn.sigmoid(gate)
        sil = gate * sg
        act = (sil * up).astype(BF16)
        adu_ref[0] = act
        x2 = x1 + _mm(act, wd[...])
        rf = lax.rsqrt(jnp.mean(x2 * x2, axis=-1, keepdims=True) + EPS)
        nf = x2 * rf
        err = nf * gfv - tgt_ref[...]
        loss_ref[...] += 0.5 * jnp.sum(jnp.mean(err * err, axis=-1, keepdims=True))
        dy = err * (1.0 / D)
        dgf_ref[...] += jnp.sum(dy * nf, axis=0, keepdims=True)
        dnf = dy * gfv
        dx2 = rf * (dnf - nf * jnp.mean(dnf * nf, axis=-1, keepdims=True))
        dx2b = dx2.astype(BF16)
        hb_ref[0] = dx2b
        dact = _nt(dx2b, wd[...])
        dup = (dact * sil).astype(BF16)
        dgate = ((dact * up) * (sg * (1.0 + gate * (1.0 - sg)))).astype(BF16)
        adu_ref[2] = dup
        adu_ref[1] = dgate
        dh2 = _mm(dgate, wg[...]) + _mm(dup, wu[...])
        dg2_ref[...] += jnp.sum(dh2 * n2, axis=0, keepdims=True)
        dn2 = dh2 * g2v
        dx1 = dx2 + r2 * (dn2 - n2 * jnp.mean(dn2 * n2, axis=-1, keepdims=True))
        dx1_ref[...] = dx1
        dx1b_ref[...] = dx1.astype(BF16)

    tile = lambda w: pl.BlockSpec((tm, w), lambda i: (i, 0))
    vec = pl.BlockSpec((1, D), lambda i: (0, 0))
    hbm = pl.BlockSpec(memory_space=pl.ANY)
    return pl.pallas_call(
        body,
        name="ffn_fwd_bwd",
        grid=(t // tm,),
        in_specs=[tile(D), tile(D), vec, vec, hbm, hbm],
        out_specs=[tile(D), tile(D), pl.BlockSpec((3, tm, DFF), lambda i: (0, i, 0)),
                   pl.BlockSpec((2, tm, D), lambda i: (0, i, 0)), vec, vec,
                   pl.BlockSpec((1, 128), lambda i: (0, 0))],
        out_shape=[
            jax.ShapeDtypeStruct((t, D), F32),
            jax.ShapeDtypeStruct((t, D), BF16),
            jax.ShapeDtypeStruct((3, t, DFF), BF16),
            jax.ShapeDtypeStruct((2, t, D), BF16),
            jax.ShapeDtypeStruct((1, D), F32),
            jax.ShapeDtypeStruct((1, D), F32),
            jax.ShapeDtypeStruct((1, 128), F32),
        ],
        scratch_shapes=[pltpu.VMEM((DFF, D), BF16), pltpu.VMEM((DFF, D), BF16), pltpu.VMEM((DFF, D), BF16),
                        pltpu.SemaphoreType.DMA((3 * N_DEV,))],
        compiler_params=_params(("arbitrary",)),
    )(x1_2d, tgt2d, g2, gf, gwa, gwb)


def _stage2_rider(pbs, ks=(1, 2, 3)):
    return dict(inputs=list(pbs), out_shape=[jax.ShapeDtypeStruct((len(ks),) + p.shape[1:], BF16) for p in pbs],
                nsem=len(ks) * len(pbs),
                copies=lambda ins, outs, send, recv: _stage2_copies(ins, outs, send, recv, ks))


def _tn_matmul(a, b, bm, bn, tk, name, with_bf16, rider=None):
    t, m = a.shape
    n = b.shape[1]
    nk = t // tk
    nout = 2 if with_bf16 else 1
    grid = (m // bm, n // bn, nk)
    r_in = [] if rider is None else rider["inputs"]
    r_out = [] if rider is None else rider["out_shape"]

    def body(a_ref, b_ref, *rest):
        ins, outs = rest[:len(r_in)], rest[len(r_in):len(r_in) + nout]
        r_outs, sems = rest[len(r_in) + nout:len(r_in) + nout + len(r_out)], rest[len(r_in) + nout + len(r_out):]
        o_ref = outs[0]
        i, j, k = pl.program_id(0), pl.program_id(1), pl.program_id(2)
        if rider is not None:
            @pl.when((i == 0) & (j == 0) & (k == 0))
            def _():
                for cp in rider["copies"](ins, r_outs, *sems):
                    cp.start()

        @pl.when(k == 0)
        def _():
            o_ref[...] = jnp.zeros_like(o_ref)

        o_ref[...] += _tn(a_ref[...].astype(BF16), b_ref[...].astype(BF16))
        if with_bf16:
            @pl.when(k == nk - 1)
            def _():
                outs[1][...] = o_ref[...].astype(BF16)
        if rider is not None:
            @pl.when((i == grid[0] - 1) & (j == grid[1] - 1) & (k == nk - 1))
            def _():
                copies = rider["copies"](ins, r_outs, *sems)
                for cp in copies:
                    cp.wait_recv()
                for cp in copies:
                    cp.wait_send()

    out_blk = pl.BlockSpec((bm, bn), lambda i, j, k: (i, j))
    hbm = pl.BlockSpec(memory_space=pl.ANY)
    out_shape = [jax.ShapeDtypeStruct((m, n), F32)] + ([jax.ShapeDtypeStruct((m, n), BF16)] if with_bf16 else [])
    res = pl.pallas_call(
        body,
        name=name,
        grid=grid,
        in_specs=[pl.BlockSpec((tk, bm), lambda i, j, k: (k, i)), pl.BlockSpec((tk, bn), lambda i, j, k: (k, j))]
        + [hbm] * len(r_in),
        out_specs=[out_blk] * nout + [hbm] * len(r_out),
        out_shape=out_shape + list(r_out),
        scratch_shapes=([] if rider is None else
                        [pltpu.SemaphoreType.DMA((rider["nsem"],)), pltpu.SemaphoreType.DMA((rider["nsem"],))]),
        compiler_params=_params(("parallel", "parallel", "arbitrary") if rider is None
                                else ("arbitrary", "arbitrary", "arbitrary")),
    )(a, b, *r_in)
    return res[0] if len(res) == 1 else res


def _dw_ffn(adu, hb, tk):
    _, t, _ = adu.shape
    bm = DFF // 2
    nk = t // tk

    def body(a_ref, b_ref, o_ref, ob_ref):
        k = pl.program_id(2)

        @pl.when(k == 0)
        def _():
            o_ref[...] = jnp.zeros_like(o_ref)

        o_ref[...] += _tn(a_ref[...], b_ref[...])

        @pl.when(k == nk - 1)
        def _():
            ob_ref[...] = o_ref[...].astype(BF16)

    out_blk = pl.BlockSpec((None, bm, D), lambda p, i, k: (p, i, 0))
    return pl.pallas_call(
        body,
        name="dw_ffn",
        grid=(3, DFF // bm, nk),
        in_specs=[pl.BlockSpec((None, tk, bm), lambda p, i, k: (p, k, i)),
                  pl.BlockSpec((None, tk, D), lambda p, i, k: (jnp.minimum(p, 1), k, 0))],
        out_specs=[out_blk, out_blk],
        out_shape=[jax.ShapeDtypeStruct((3, DFF, D), F32), jax.ShapeDtypeStruct((3, DFF, D), BF16)],
        compiler_params=_params(("arbitrary", "arbitrary", "arbitrary")),
    )(adu, hb)


def _mix_bwd(proj3, z3, sprev, opre3, dmix3, gng, conv_w, wgu_p, pbs):
    nb, s, _ = proj3.shape
    nc = s // CHUNK
    na = len(pbs)

    def body(*refs):
        (p_ref, pprev_ref, z_ref, sp_ref, o_ref, dm_ref, gng_ref, cw_ref, wgu_ref) = refs[:9]
        pb_refs = refs[9:9 + na]
        (dproj_ref, dgng_ref, dcw_ref, dbg_ref, dwgu_ref) = refs[9 + na:14 + na]
        r2_refs = refs[14 + na:14 + 2 * na]
        ds_ref, dycn_ref, send_sems, recv_sems = refs[14 + 2 * na:]
        step = pl.program_id(0)
        n = nc - 1 - step

        @pl.when(step == 0)
        def _():
            for cp in _stage2_copies(pb_refs, r2_refs, send_sems, recv_sems):
                cp.start()
            ds_ref[...] = jnp.zeros_like(ds_ref)
            dycn_ref[...] = jnp.zeros_like(dycn_ref)
            dgng_ref[...] = jnp.zeros_like(dgng_ref)
            dcw_ref[...] = jnp.zeros_like(dcw_ref)
            dbg_ref[...] = jnp.zeros_like(dbg_ref)
            dwgu_ref[...] = jnp.zeros_like(dwgu_ref)

        r_i = lax.broadcasted_iota(jnp.int32, (CHUNK, CHUNK), 0)
        c_i = lax.broadcasted_iota(jnp.int32, (CHUNK, CHUNK), 1)
        tril16 = (r_i >= c_i).astype(BF16)
        triu16 = (r_i <= c_i).astype(BF16)
        causal = r_i >= c_i
        masks = _head_masks()
        cmask = _causal_stack_mask()
        gg = gng_ref[...]
        last_row = lax.broadcasted_iota(jnp.int32, (CHUNK, NQK), 0) == CHUNK - 1
        ones_r = jnp.ones((16, DV), BF16)
        has_prev = (n > 0).astype(F32)
        for b in range(nb):
            q = p_ref[b, :, OQ:OQ + NQK]
            k = p_ref[b, :, OK_:OK_ + NQK]
            z = z_ref[b]
            _, eb, enb, ekl, qi, ki, ks, decb = _chunk_fwd_parts(q, k, z, tril16)
            qi16 = qi.astype(BF16)
            ki16 = ki.astype(BF16)
            qs = _stack_heads(qi, masks).astype(BF16)
            sc = jnp.where(cmask, _nt(qs, ki16), 0.0).astype(BF16)
            st = sp_ref[b, 0]
            st16 = st.astype(BF16)
            dsn = ds_ref[b]
            dsn16 = dsn.astype(BF16)
            v16 = p_ref[b, :, OV:OV + NV].astype(BF16)
            do16 = []
            dgng = jnp.zeros((1, DV), F32)
            for h in range(HEADS):
                cols = slice(DV * h, DV * (h + 1))
                o = o_ref[b, :, cols]
                r = lax.rsqrt(jnp.mean(o * o, axis=-1, keepdims=True) + EPS)
                nh = o * r
                g = p_ref[b, :, OG + DV * h:OG + DV * (h + 1)]
                sg = jax.nn.sigmoid(g)
                dog = dm_ref[b, :, cols]
                dproj_ref[b, :, OG + DV * h:OG + DV * (h + 1)] = (
                    (dog * (nh * gg)) * (sg * (1.0 + g * (1.0 - sg)))).astype(BF16)
                don = dog * (g * sg)
                dgng = dgng + jnp.sum(don * nh, axis=0, keepdims=True)
                dn = don * gg
                do = r * (dn - nh * jnp.mean(dn * nh, axis=-1, keepdims=True))
                do16.append(do.astype(BF16))
            dgng_ref[...] += dgng
            do_rows = jnp.concatenate(do16, axis=0)
            v_rows = jnp.concatenate([v16[:, DV * h:DV * (h + 1)] for h in range(HEADS)], axis=0)
            dp16 = [jnp.where(causal, _nt(do16[h], v16[:, DV * h:DV * (h + 1)]), 0.0).astype(BF16)
                    for h in range(HEADS)]
            ks_dsn = _mm(_stack_heads(ks, masks).astype(BF16), dsn16)
            do_st = _nt(do_rows, st16)
            v_dsn = _nt(v_rows, dsn16)
            dp_ki = _mm(jnp.concatenate(dp16, axis=0), ki16)
            q_do = _tn(qi16, jnp.concatenate(do16, axis=1))
            dki_h = []
            for h in range(HEADS):
                rows = slice(CHUNK * h, CHUNK * (h + 1))
                cols = slice(DV * h, DV * (h + 1))
                dv = _tn(sc[rows], do16[h]) + ks_dsn[rows]
                dproj_ref[b, :, OV + DV * h:OV + DV * (h + 1)] = dv.astype(BF16)
                dki_h.append(_tn(dp16[h], qi16))
                ds_ref[b, rows, :] = decb[rows] * dsn[rows] + q_do[rows, cols]
            blocks = lambda a: [a[CHUNK * h:CHUNK * (h + 1)] for h in range(HEADS)]
            dqi = _merge_heads(blocks(dp_ki + do_st), masks)
            dki = _merge_heads(dki_h, masks)
            dks = _merge_heads(blocks(v_dsn), masks)
            dproj_ref[b, :, OQ:OQ + NQK] = (dqi * (Q_SCALE * eb)).astype(BF16)
            dproj_ref[b, :, OK_:OK_ + NQK] = (dki * enb + dks * ekl).astype(BF16)
            dks_ks = dks * ks
            db = dqi * qi - dki * ki - dks_ks
            sd = _split_bf16(dsn * st * decb, 2)
            dbl = jnp.sum(dks_ks, axis=0, keepdims=True) + (_nt(ones_r, sd[0]) + _nt(ones_r, sd[1]))[0:1, :]
            db = db + jnp.where(last_row, dbl, 0.0)
            db_parts = _split_bf16(db, 3)
            dla = _mm(triu16, db_parts[0]) + _mm(triu16, db_parts[1]) + _mm(triu16, db_parts[2])
            dz = (dla * INV_GATE_NORM) * (1.0 / (1.0 + jnp.exp(z)))
            dbg_ref[...] += jnp.sum(dz, axis=0, keepdims=True)
            dz16 = dz.astype(BF16)
            pa16 = p_ref[b, :, OA:OA + A_PAD].astype(BF16)
            dwgu_ref[...] += _tn(pa16, dz16)
            dproj_ref[b, :, OA:OA + A_PAD] = _nt(dz16, wgu_ref[...]).astype(BF16)
            cb = p_ref[b, :, OCB:OCB + CW]
            cc = p_ref[b, :, OCC:OCC + CW]
            ch = p_ref[b, :, OCH:OCH + CW]
            u = cc * ch
            uprev = (pprev_ref[b, :, 0:CW] * pprev_ref[b, :, CW:2 * CW]) * has_prev
            u1, u2 = _conv_taps(u, uprev)
            w0 = cw_ref[0:1, :]
            w1 = cw_ref[1:2, :]
            w2 = cw_ref[2:3, :]
            yc = w0 * u2 + w1 * u1 + w2 * u
            doc = dm_ref[b, :, NV:NV + CW]
            dproj_ref[b, :, OCB:OCB + CW] = (doc * yc).astype(BF16)
            dyc = doc * cb
            dycn = dycn_ref[b]
            row = lax.broadcasted_iota(jnp.int32, dyc.shape, 0)
            d1 = jnp.where(row >= CHUNK - 1, pltpu.roll(dycn, CHUNK - 1, 0), pltpu.roll(dyc, CHUNK - 1, 0))
            d2 = jnp.where(row >= CHUNK - 2, pltpu.roll(dycn, CHUNK - 2, 0), pltpu.roll(dyc, CHUNK - 2, 0))
            du = w2 * dyc + w1 * d1 + w0 * d2
            dproj_ref[b, :, OCC:OCC + CW] = (du * ch).astype(BF16)
            dproj_ref[b, :, OCH:OCH + CW] = (du * cc).astype(BF16)
            dcw_ref[0:1, :] += jnp.sum(dyc * u2, axis=0, keepdims=True)
            dcw_ref[1:2, :] += jnp.sum(dyc * u1, axis=0, keepdims=True)
            dcw_ref[2:3, :] += jnp.sum(dyc * u, axis=0, keepdims=True)
            dycn_ref[b] = dyc

        @pl.when(step == nc - 1)
        def _():
            copies = _stage2_copies(pb_refs, r2_refs, send_sems, recv_sems)
            for cp in copies:
                cp.wait_recv()
            for cp in copies:
                cp.wait_send()

    rev = lambda w: pl.BlockSpec((nb, CHUNK, w), lambda i: (0, nc - 1 - i, 0))
    const = lambda r, c: pl.BlockSpec((r, c), lambda i: (0, 0))
    hbm = pl.BlockSpec(memory_space=pl.ANY)
    return pl.pallas_call(
        body,
        name="mix_bwd",
        grid=(nc,),
        in_specs=[
            rev(PW),
            pl.BlockSpec((nb, CHUNK, 2 * CW), lambda i: (0, jnp.maximum(nc - 2 - i, 0), OCC // (2 * CW))),
            rev(NQK),
            pl.BlockSpec((nb, 1, NQK, DV), lambda i: (0, nc - 1 - i, 0, 0)),
            rev(NV),
            rev(D),
            const(1, DV),
            const(CONV_K, CW),
            const(A_PAD, NQK),
        ] + [hbm] * na,
        out_specs=[rev(PW), const(1, DV), const(8, CW), const(1, NQK), const(A_PAD, NQK)] + [hbm] * na,
        out_shape=[
            jax.ShapeDtypeStruct((nb, s, PW), BF16),
            jax.ShapeDtypeStruct((1, DV), F32),
            jax.ShapeDtypeStruct((8, CW), F32),
            jax.ShapeDtypeStruct((1, NQK), F32),
            jax.ShapeDtypeStruct((A_PAD, NQK), F32),
        ] + [jax.ShapeDtypeStruct((3,) + p.shape[1:], BF16) for p in pbs],
        scratch_shapes=[pltpu.VMEM((nb, NQK, DV), F32), pltpu.VMEM((nb, CHUNK, CW), F32),
                        pltpu.SemaphoreType.DMA((3 * na,)), pltpu.SemaphoreType.DMA((3 * na,))],
        compiler_params=_params(("arbitrary",)),
    )(proj3, proj3, z3, sprev, opre3, dmix3, gng, conv_w, wgu_p, *pbs)


SMALL_PACK_ROWS = 16


def _wgu_slot(r):
    return 4 + r // 4, NQK * (r % 4)


CONV_SLOTS = ((8, 0), (8, CW), (9, 0))


def _in_proj_bwd(dproj2d, x2d, dx1, g1, w_in_t, tm, pb, small_parts):
    t = x2d.shape[0]
    nt = t // tm

    def body(dp_ref, x_ref, dx1_ref, g_ref, w_ref, pb_ref, dg2, dgf, dbg, dgng, dwgu, dcw, lp,
             dx_ref, sums_ref, r2_ref, dg1_acc, pack, gbuf, pack1, gbuf1, send_sems, recv_sems,
             ssend, srecv, ssend1, srecv1):
        x, y, c = _position()
        me = 4 * x + 2 * y + c
        flips = [(k >> 2, (k >> 1) & 1, k & 1) for k in range(1, N_DEV)]
        peers = [(x ^ fx, y ^ fy, c ^ fc) for fx, fy, fc in flips]

        def small_copies(src, dst, send, recv, arrivals):
            return [pltpu.make_async_remote_copy(
                src_ref=src, dst_ref=dst.at[4 * px + 2 * py + pc if arrivals else me],
                send_sem=send.at[k], recv_sem=recv.at[k], device_id=(px, py, pc), device_id_type=MESH)
                for k, (px, py, pc) in enumerate(peers)]

        @pl.when(pl.program_id(0) == 0)
        def _():
            for cp in _stage2_copies([pb_ref], [r2_ref], send_sems, recv_sems):
                cp.start()
            dg1_acc[...] = jnp.zeros_like(dg1_acc)
            pack[...] = jnp.zeros_like(pack)
            pack[1:2, :] = dg2[...]
            pack[2:3, :] = dgf[...]
            pack[3:4, 0:NQK] = dbg[...]
            pack[3:4, NQK:NQK + DV] = dgng[...]
            pack[3:4, NQK + DV:NQK + 2 * DV] = lp[...]
            for r in range(RANK):
                row, lane = _wgu_slot(r)
                pack[row:row + 1, lane:lane + NQK] = dwgu[r:r + 1, :]
            for r, (row, lane) in enumerate(CONV_SLOTS):
                pack[row:row + 1, lane:lane + CW] = dcw[r:r + 1, :]
            for cp in small_copies(pack, gbuf, ssend, srecv, False):
                cp.start()
            gbuf[me] = pack[...]

        xv = x_ref[...]
        r = lax.rsqrt(jnp.mean(xv * xv, axis=-1, keepdims=True) + EPS)
        n1 = xv * r
        dh = _mm(dp_ref[...], w_ref[...])
        dg1_acc[...] += jnp.sum(dh * n1, axis=0, keepdims=True)
        dn = dh * g_ref[...]
        dx_ref[...] = dx1_ref[...] + r * (dn - n1 * jnp.mean(dn * n1, axis=-1, keepdims=True))

        @pl.when(pl.program_id(0) == nt - 1)
        def _():
            pack1[...] = jnp.zeros_like(pack1)
            pack1[0:1, :] = dg1_acc[...]
            for cp in small_copies(pack1, gbuf1, ssend1, srecv1, False):
                cp.start()
            gbuf1[me] = pack1[...]
            copies = _stage2_copies([pb_ref], [r2_ref], send_sems, recv_sems)
            for cp in copies:
                cp.wait_recv()
            for cp in copies:
                cp.wait_send()
            for src, dst, send, recv in ((pack, gbuf, ssend, srecv), (pack1, gbuf1, ssend1, srecv1)):
                for cp in small_copies(src, dst, send, recv, True):
                    cp.wait_recv()
                    cp.wait_send()
            acc = gbuf[0]
            acc1 = gbuf1[0]
            for d in range(1, N_DEV):
                acc = acc + gbuf[d]
                acc1 = acc1 + gbuf1[d]
            sums_ref[...] = acc
            sums_ref[0:1, :] = acc1[0:1, :]

    tile = lambda w: pl.BlockSpec((tm, w), lambda i: (i, 0))
    vec = pl.BlockSpec((1, D), lambda i: (0, 0))
    hbm = pl.BlockSpec(memory_space=pl.ANY)
    whole = lambda a: pl.BlockSpec(a.shape, lambda i: (0,) * a.ndim)
    return pl.pallas_call(
        body,
        name="in_proj_bwd",
        grid=(nt,),
        in_specs=[tile(PW), tile(D), tile(D), vec, pl.BlockSpec((PW, D), lambda i: (0, 0)), hbm]
        + [whole(a) for a in small_parts],
        out_specs=[tile(D), pl.BlockSpec((SMALL_PACK_ROWS, D), lambda i: (0, 0)), hbm],
        out_shape=[jax.ShapeDtypeStruct((t, D), F32), jax.ShapeDtypeStruct((SMALL_PACK_ROWS, D), F32),
                   jax.ShapeDtypeStruct((3,) + pb.shape[1:], BF16)],
        scratch_shapes=[pltpu.VMEM((1, D), F32),
                        pltpu.VMEM((SMALL_PACK_ROWS, D), F32), pltpu.VMEM((N_DEV, SMALL_PACK_ROWS, D), F32),
                        pltpu.VMEM((8, D), F32), pltpu.VMEM((N_DEV, 8, D), F32),
                        pltpu.SemaphoreType.DMA((3,)), pltpu.SemaphoreType.DMA((3,)),
                        pltpu.SemaphoreType.DMA((7,)), pltpu.SemaphoreType.DMA((7,)),
                        pltpu.SemaphoreType.DMA((7,)), pltpu.SemaphoreType.DMA((7,))],
        compiler_params=_params(("arbitrary",)),
    )(dproj2d, x2d, dx1, g1, w_in_t, pb, *small_parts)


def _get_rows(ref):
    return ref[:, 0, :] if len(ref.shape) == 3 else ref[...]


def _put_rows(ref, val):
    if len(ref.shape) == 3:
        ref[:, 0, :] = val
    else:
        ref[...] = val


def _adamw_math(w, g, m, v):
    m = ADAM_B1 * m + (1.0 - ADAM_B1) * g
    v = ADAM_B2 * v + (1.0 - ADAM_B2) * (g * g)
    m_hat = m / (1.0 - ADAM_B1 ** ADAM_STEP)
    v_hat = v / (1.0 - ADAM_B2 ** ADAM_STEP)
    delta = -ADAM_LR * (m_hat / (jnp.sqrt(v_hat) + ADAM_EPS) + ADAM_WD * w)
    return delta, m, v


def _position():
    return lax.axis_index("x"), lax.axis_index("y"), lax.axis_index("c")


GATHER_PARTS = 2
GATHER_SEMS = 7 * GATHER_PARTS


def _gather_copies(stage, lo, rows, gx, send_sems, recv_sems, local_sem):
    x, y, c = _position()
    me = (x, y, c)
    sibling = (x, y, 1 - c)
    chips = [(1 - x, y), (x, 1 - y), (1 - x, 1 - y)]
    part = -(-rows // (16 * GATHER_PARTS)) * 16
    bounds = [(p * part, min(part, rows - p * part)) for p in range(GATHER_PARTS)]

    def blk(px, py, pc, off, n):
        return gx.at[4 * px + 2 * py + pc, pl.ds(off, n), :]

    mine = pltpu.make_async_copy(stage.at[pl.ds(lo, rows), :], gx.at[4 * x + 2 * y + c], local_sem)
    parts = []
    for p, (off, n) in enumerate(bounds):
        def copy(k, block, to, from_stage=False, p=p, off=off, n=n):
            return pltpu.make_async_remote_copy(
                src_ref=stage.at[pl.ds(lo + off, n), :] if from_stage else blk(*block, off, n),
                dst_ref=blk(*block, off, n), send_sem=send_sems.at[7 * p + k], recv_sem=recv_sems.at[7 * p + k],
                device_id=to, device_id_type=MESH)

        first = [copy(0, me, sibling, True)] + [copy(1 + j, me, (*chip, c), True) for j, chip in enumerate(chips)]
        passed = [copy(4 + j, (*chip, c), sibling) for j, chip in enumerate(chips)]
        arrivals = ([copy(0, sibling, me)] + [copy(1 + j, (*chip, c), me) for j, chip in enumerate(chips)]
                    + [copy(4 + j, (*chip, 1 - c), me) for j, chip in enumerate(chips)])
        parts.append((first, passed, arrivals))
    return mine, parts


def _gather_start(*args):
    mine, parts = _gather_copies(*args)
    mine.start()
    for first, _, _ in parts:
        for cp in first:
            cp.start()


def _gather_finish(*args):
    mine, parts = _gather_copies(*args)
    for _, passed, arrivals in parts:
        for j in range(3):
            arrivals[1 + j].wait_recv()
            passed[j].start()
    for first, passed, arrivals in parts:
        arrivals[0].wait_recv()
        for j in range(3):
            arrivals[4 + j].wait_recv()
        for cp in first + passed:
            cp.wait_send()
    mine.wait()


def _gather_sems():
    return [pltpu.SemaphoreType.DMA((GATHER_SEMS,)), pltpu.SemaphoreType.DMA((GATHER_SEMS,)), pltpu.SemaphoreType.DMA]


def _gather_w_in(w_it, w_gt, w_ut, w_d, w_o, wgu_s, conv_s):
    def body(wi_ref, wg_ref, wu_ref, wd_ref, wo_ref, wgu_ref, conv_ref, w_ref, gwgu_ref, gconv_ref, stage,
             buf, send_sems, recv_sems, local_sem, ssend, srecv):
        x, y, c = _position()
        me = 4 * x + 2 * y + c
        stage[SLAB_IN:SLAB_IN + IN_W, :] = wi_ref[:, 0, :].astype(BF16)
        stage[SLAB_IN + IN_W:SLAB_G, :] = jnp.zeros((IN_ROWS - IN_W, D), BF16)
        args = (stage, SLAB_IN, IN_ROWS, buf, send_sems, recv_sems, local_sem)
        _gather_start(*args)
        stage[SLAB_G:SLAB_U, :] = wg_ref[...].astype(BF16)
        stage[SLAB_U:SLAB_D, :] = wu_ref[...].astype(BF16)
        stage[SLAB_D:SLAB_O, :] = wd_ref[...].astype(BF16)
        stage[SLAB_O:SLAB_ROWS, :] = wo_ref[...].astype(BF16)
        flips = [(k >> 2, (k >> 1) & 1, k & 1) for k in range(1, N_DEV)]
        peers = [(x ^ fx, y ^ fy, c ^ fc) for fx, fy, fc in flips]

        def small(k, block_id, to):
            return [pltpu.make_async_remote_copy(
                src_ref=s, dst_ref=g.at[block_id], send_sem=ssend.at[2 * k + n], recv_sem=srecv.at[2 * k + n],
                device_id=to, device_id_type=MESH)
                for n, (s, g) in enumerate(((wgu_ref, gwgu_ref), (conv_ref, gconv_ref)))]

        gwgu_ref[me] = wgu_ref[...]
        gconv_ref[me] = conv_ref[...]
        for k, peer in enumerate(peers):
            for cp in small(k, me, peer):
                cp.start()
        w_ref[IN_COLS:PW, :] = jnp.zeros((PW - IN_COLS, D), BF16)
        _gather_finish(*args)
        for k, (px, py, pc) in enumerate(peers):
            for cp in small(k, 4 * px + 2 * py + pc, (px, py, pc)):
                cp.wait_recv()
                cp.wait_send()
        for j, lo, hi, d in _in_segments():
            w_ref[d:d + hi - lo, :] = buf[j, lo:hi, :]

    vm = pl.BlockSpec(memory_space=pltpu.VMEM)
    return pl.pallas_call(
        body,
        name="gather_w_in",
        in_specs=[vm] * 7,
        out_specs=[vm] * 4,
        out_shape=[jax.ShapeDtypeStruct((PW, D), BF16),
                   jax.ShapeDtypeStruct((N_DEV,) + wgu_s.shape, F32),
                   jax.ShapeDtypeStruct((N_DEV,) + conv_s.shape, F32),
                   jax.ShapeDtypeStruct((SLAB_ROWS, D), BF16)],
        scratch_shapes=[pltpu.VMEM((N_DEV, IN_ROWS, D), BF16)] + _gather_sems()
        + [pltpu.SemaphoreType.DMA((14,)), pltpu.SemaphoreType.DMA((14,))],
        compiler_params=_params(),
    )(w_it, w_gt, w_ut, w_d, w_o, wgu_s, conv_s)


def _w_in_core_reduce(dw_t, rider):
    n_in, n_out = len(rider["inputs"]), len(rider["out_shape"])

    def body(d_ref, *rest):
        ride_in, (own_ref, sib_ref, pb_ref) = rest[:n_in], rest[n_in:n_in + 3]
        ride_out = rest[n_in + 3:n_in + 3 + n_out]
        g, gb, r1, send_sems, recv_sems, ride_send, ride_recv = rest[n_in + 3 + n_out:]
        ride = rider["copies"](ride_in, ride_out, ride_send, ride_recv)
        for cp in ride:
            cp.start()
        x, y, c = _position()
        chip = 2 * x + y
        for j in range(N_DEV):
            g[j, IN_W:IN_ROWS, :] = jnp.zeros((IN_ROWS - IN_W, D), F32)
        for j, lo, hi, d in _in_segments():
            g[j, lo:hi, :] = d_ref[d:d + hi - lo, :]
        for j in range(N_DEV):
            gb[j] = g[j].astype(BF16)
        copies = _stage1_copies(gb, r1, send_sems, recv_sems)
        for cp in copies:
            cp.start()
        own_ref[0] = g[2 * chip + c]
        for cp in copies:
            cp.wait_recv()
        sib_ref[0] = r1[chip]
        for k in range(1, 4):
            t = chip ^ k
            pb_ref[k - 1] = (g[2 * t + c] + r1[t].astype(F32)).astype(BF16)
        for cp in copies:
            cp.wait_send()
        for cp in ride:
            cp.wait_recv()
        for cp in ride:
            cp.wait_send()

    vm = pl.BlockSpec(memory_space=pltpu.VMEM)
    hbm = pl.BlockSpec(memory_space=pl.ANY)
    return pl.pallas_call(
        body,
        name="w_in_core_reduce",
        in_specs=[vm] + [hbm] * n_in,
        out_specs=[vm, vm, vm] + [hbm] * n_out,
        out_shape=[jax.ShapeDtypeStruct((1, IN_ROWS, D), F32), jax.ShapeDtypeStruct((1, IN_ROWS, D), BF16),
                   jax.ShapeDtypeStruct((3, IN_ROWS, D), BF16)] + list(rider["out_shape"]),
        scratch_shapes=[pltpu.VMEM((N_DEV, IN_ROWS, D), F32), pltpu.VMEM((N_DEV, IN_ROWS, D), BF16),
                        pltpu.VMEM((4, IN_ROWS, D), BF16), pltpu.SemaphoreType.DMA((4,)),
                        pltpu.SemaphoreType.DMA((4,)), pltpu.SemaphoreType.DMA((rider["nsem"],)),
                        pltpu.SemaphoreType.DMA((rider["nsem"],))],
        compiler_params=_params(),
    )(dw_t, *rider["inputs"])


def _stage1_copies(g_ref, r_ref, send_sems, recv_sems):
    x, y, c = _position()
    return [pltpu.make_async_remote_copy(
        src_ref=g_ref.at[2 * i + 1 - c], dst_ref=r_ref.at[i], send_sem=send_sems.at[i], recv_sem=recv_sems.at[i],
        device_id=(x, y, 1 - c), device_id_type=MESH) for i in range(4)]


def _ffn_core_reduce(dw3, dwb3, dw_o, dwb_o, pos_arr, dx1b, gwb):
    def body(pos_ref, g0, g1, g2, go, gb3_hbm, gbo_hbm, dx1b_ref, gwb_hbm, p0, p1, p2, po, s0, s1, s2, so, dmix_ref,
             r1f, r1o, wo, send_sems, recv_sems, wsem):
        step = pl.program_id(0)
        k = jnp.minimum(step, 2)
        x, y, c = _position()
        chip = 2 * x + y

        def copies(p):
            src = 2 * (chip ^ ((p + 1) & 3)) + 1 - c
            pairs = [(gb3_hbm.at[a, src], r1f.at[a, p]) for a in range(3)] + [(gbo_hbm.at[src], r1o.at[p])]
            return [pltpu.make_async_remote_copy(
                src_ref=s, dst_ref=d, send_sem=send_sems.at[4 * p + a], recv_sem=recv_sems.at[4 * p + a],
                device_id=(x, y, 1 - c), device_id_type=MESH) for a, (s, d) in enumerate(pairs)]

        @pl.when(step == 0)
        def _():
            for p in range(4):
                for cp in copies(p):
                    cp.start()
            loads = [pltpu.make_async_copy(gwb_hbm.at[j, pl.ds(FF_W, OUT_ROWS), :],
                                           wo.at[pl.ds(OUT_ROWS * j, OUT_ROWS), :], wsem.at[j]) for j in range(N_DEV)]
            for cp in loads:
                cp.start()
            for cp in loads:
                cp.wait()

        dmix_ref[...] = _nt(dx1b_ref[...], wo[...])

        for p in range(3):
            @pl.when(step == p)
            def _():
                for cp in copies(p):
                    cp.wait_recv()

        for a, (g, pb) in enumerate(((g0, p0), (g1, p1), (g2, p2))):
            pb[...] = (g[...] + r1f[a, k][None].astype(F32)).astype(BF16)
        po[...] = (go[...] + r1o[k][None].astype(F32)).astype(BF16)

        @pl.when(step == 3)
        def _():
            for cp in copies(3):
                cp.wait_recv()
            for a, s in enumerate((s0, s1, s2)):
                s[0] = r1f[a, 3]
            so[0] = r1o[3]
            for p in range(4):
                for cp in copies(p):
                    cp.wait_send()

    t = dx1b.shape[0]
    other = lambda s, pos: 2 * (pos[1] ^ (jnp.minimum(s, 2) + 1)) + pos[0]
    g_spec = lambda lead: pl.BlockSpec((None, 1, FF_W, D), lambda s, pos: (lead, other(s, pos), 0, 0))
    slot = lambda rows: pl.BlockSpec((1, rows, D), lambda s, pos: (jnp.minimum(s, 2), 0, 0))
    one = lambda rows: pl.BlockSpec((1, rows, D), lambda s, pos: (0, 0, 0))
    quarter = pl.BlockSpec((t // 4, D), lambda s, pos: (s, 0))
    hbm = pl.BlockSpec(memory_space=pl.ANY)
    return pl.pallas_call(
        body,
        name="ffn_core_reduce",
        grid_spec=pltpu.PrefetchScalarGridSpec(
            num_scalar_prefetch=1, grid=(4,),
            in_specs=[g_spec(0), g_spec(1), g_spec(2),
                      pl.BlockSpec((1, OUT_ROWS, D), lambda s, pos: (other(s, pos), 0, 0)), hbm, hbm, quarter, hbm],
            out_specs=[slot(FF_W), slot(FF_W), slot(FF_W), slot(OUT_ROWS),
                       one(FF_W), one(FF_W), one(FF_W), one(OUT_ROWS), quarter],
            scratch_shapes=[pltpu.VMEM((3, 4, FF_W, D), BF16), pltpu.VMEM((4, OUT_ROWS, D), BF16),
                            pltpu.VMEM((D, D), BF16), pltpu.SemaphoreType.DMA((16,)),
                            pltpu.SemaphoreType.DMA((16,)), pltpu.SemaphoreType.DMA((N_DEV,))]),
        out_shape=[jax.ShapeDtypeStruct((3, FF_W, D), BF16)] * 3 + [jax.ShapeDtypeStruct((3, OUT_ROWS, D), BF16)]
        + [jax.ShapeDtypeStruct((1, FF_W, D), BF16)] * 3 + [jax.ShapeDtypeStruct((1, OUT_ROWS, D), BF16),
                                                             jax.ShapeDtypeStruct((t, D), F32)],
        compiler_params=_params(("arbitrary",)),
    )(pos_arr, dw3, dw3, dw3, dw_o, dwb3, dwb_o, dx1b, gwb)


def _stage2_copies(p_refs, r_refs, send_sems, recv_sems, ks=(1, 2, 3)):
    x, y, c = _position()
    copies = []
    for a in range(len(p_refs)):
        for i, k in enumerate(ks):
            copies.append(pltpu.make_async_remote_copy(
                src_ref=p_refs[a].at[k - 1], dst_ref=r_refs[a].at[i],
                send_sem=send_sems.at[len(ks) * a + i], recv_sem=recv_sems.at[len(ks) * a + i],
                device_id=(x ^ (k >> 1), y ^ (k & 1), c), device_id_type=MESH))
    return copies


def _finish_weights(items, pos_arr, name, nblk):
    in_specs, out_specs, out_shape, operands, wbs = [], [], [], [], []
    for g8, lead, r1, r2, w, m, v in items:
        rows, wr = g8.shape[-2], w.shape[0]
        assert rows % nblk == 0 and wr % nblk == 0 and (nblk == 1 or (rows == wr and rows % (16 * nblk) == 0))
        rb, wb = rows // nblk, wr // nblk
        if lead is not None:
            g_spec = pl.BlockSpec((None, 1, rb, D), lambda i, pos, lead=lead: (lead, 2 * pos[1] + pos[0], i, 0))
        elif g8.shape[0] == 1:
            g_spec = pl.BlockSpec((1, rb, D), lambda i, pos: (0, i, 0))
        else:
            g_spec = pl.BlockSpec((1, rb, D), lambda i, pos: (2 * pos[1] + pos[0], i, 0))
        r1_spec = pl.BlockSpec((1, rb, D), lambda i, pos: (0, i, 0))
        if w.ndim == 3:
            wblk = pl.BlockSpec((wb, 1, D), lambda i, pos: (i, 0, 0))
        else:
            wblk = pl.BlockSpec((wb, D), lambda i, pos: (i, 0))
        r2 = list(r2) if isinstance(r2, (list, tuple)) else [r2]
        assert sum(r.shape[0] for r in r2) == 3
        in_specs += ([g_spec, r1_spec] + [pl.BlockSpec((r.shape[0], rb, D), lambda i, pos: (0, i, 0)) for r in r2]
                     + [wblk, wblk, wblk])
        out_specs += [wblk] * 4
        out_shape += [jax.ShapeDtypeStruct(w.shape, F32)] * 4
        operands += [g8, r1, *r2, w, m, v]
        wbs.append((wb, len(r2)))

    def body(pos_ref, *refs):
        n_in = sum(5 + nr for _, nr in wbs)
        at = 0
        for a, (wb, nr) in enumerate(wbs):
            g_ref, r1_ref = refs[at:at + 2]
            r2_refs = refs[at + 2:at + 2 + nr]
            w_ref, m_ref, v_ref = refs[at + 2 + nr:at + 5 + nr]
            at += 5 + nr
            g_out, d_out, m_out, v_out = refs[n_in + 4 * a:n_in + 4 * a + 4]
            g = g_ref[0] + r1_ref[0].astype(F32)
            for r2_ref in r2_refs:
                for k in range(r2_ref.shape[0]):
                    g = g + r2_ref[k].astype(F32)
            g = g[0:wb, :]
            d, mn, vn = _adamw_math(_get_rows(w_ref), g, _get_rows(m_ref), _get_rows(v_ref))
            for out, val in ((g_out, g), (d_out, d), (m_out, mn), (v_out, vn)):
                _put_rows(out, val)

    return pl.pallas_call(
        body,
        name=name,
        grid_spec=pltpu.PrefetchScalarGridSpec(
            num_scalar_prefetch=1, grid=(nblk,), in_specs=in_specs, out_specs=out_specs),
        out_shape=out_shape,
        compiler_params=_params(("arbitrary",)),
    )(pos_arr, *operands)


SMALL_NAMES = ("norm1_g", "norm2_g", "norm_f_g", "b_gate", "gla_norm_g", "w_gate_up", "conv_w")
WGU_W = NQK // N_DEV
CONV_W = CW // N_DEV


def _small_adamw(sums, ws, ms, vs):
    n = len(SMALL_NAMES)

    def body(*refs):
        acc_ref = refs[0]
        w_refs, m_refs, v_refs = refs[1:1 + n], refs[1 + n:1 + 2 * n], refs[1 + 2 * n:1 + 3 * n]
        loss_ref = refs[1 + 3 * n]
        outs = refs[2 + 3 * n:]
        x, y, c = _position()
        me = 4 * x + 2 * y + c
        acc = acc_ref[...]
        loss_ref[...] = acc[3:4, NQK + DV:NQK + DV + 1]

        def my_columns(full, width):
            r = lax.broadcasted_iota(jnp.int32, (full.shape[1], width), 0)
            col = lax.broadcasted_iota(jnp.int32, (full.shape[1], width), 1)
            sel = (r == width * me + col).astype(F32)
            return _mm(full, sel, precision=HIGHEST)

        dwgu = jnp.concatenate([acc[row:row + 1, lane:lane + NQK] for row, lane in map(_wgu_slot, range(RANK))], axis=0)
        dcw = jnp.concatenate([acc[row:row + 1, lane:lane + CW] for row, lane in CONV_SLOTS], axis=0)
        grads = [acc[0:1, :], acc[1:2, :], acc[2:3, :], acc[3:4, 0:NQK], acc[3:4, NQK:NQK + DV],
                 my_columns(dwgu, WGU_W), my_columns(dcw, CONV_W)]
        for i, g in enumerate(grads):
            d, mn, vn = _adamw_math(_get_rows(w_refs[i]), g, _get_rows(m_refs[i]), _get_rows(v_refs[i]))
            for out, val in zip(outs[4 * i:4 * i + 4], (g, d, mn, vn)):
                _put_rows(out, val)

    vm = pl.BlockSpec(memory_space=pltpu.VMEM)
    out_shape = [jax.ShapeDtypeStruct((1, 1), F32)]
    for w in ws:
        out_shape += [jax.ShapeDtypeStruct(w.shape, F32)] * 4
    return pl.pallas_call(
        body,
        name="small_adamw",
        in_specs=[vm] * (1 + 3 * n),
        out_specs=[vm] * (1 + 4 * n),
        out_shape=out_shape,
        compiler_params=_params(),
    )(sums, *ws, *ms, *vs)


def kernel(x, norm1_g, w_in, w_gate_up, b_gate, gla_norm_g, conv_w, w_out, norm2_g, w_ffn_gate, w_ffn_up, w_ffn_down, norm_f_g, loss_target, m_norm1_g, m_w_in, m_w_gate_up, m_b_gate, m_gla_norm_g, m_conv_w, m_w_out, m_norm2_g, m_w_ffn_gate, m_w_ffn_up, m_w_ffn_down, m_norm_f_g, v_norm1_g, v_w_in, v_w_gate_up, v_b_gate, v_gla_norm_g, v_conv_w, v_w_out, v_norm2_g, v_w_ffn_gate, v_w_ffn_up, v_w_ffn_down, v_norm_f_g):
    xi, yi, ci = _position()
    pos_arr = jnp.stack([ci, 2 * xi + yi]).astype(jnp.int32)
    nb, s, _ = x.shape
    t = nb * s

    tr = lambda a: a[0].T
    rows_of = lambda a: a.transpose(2, 0, 1)
    conv_rows = lambda a: a.transpose(1, 0, 2)
    w_in_t, gwgu, gconv, stage = _gather_w_in(rows_of(w_in), tr(w_ffn_gate), tr(w_ffn_up), w_ffn_down[0], w_out[0],
                                              w_gate_up[0], conv_rows(conv_w))
    wgu_f = gwgu.transpose(1, 0, 2).reshape(RANK, NQK)
    conv_f = gconv.transpose(1, 2, 0, 3).reshape(CONV_K, CW)
    wgu_p = jnp.concatenate([wgu_f, jnp.zeros((A_PAD - RANK, NQK), F32)], axis=0).astype(BF16)

    x2d = x.reshape(t, D)
    tgt2d = loss_target.reshape(t, D)
    tm = 256
    tm_in = min(512, t)
    tk = min(2048, t)
    proj, z, h, gwb = _in_proj_fwd(x2d, norm1_g, w_in_t, wgu_p, b_gate, tm_in, stage)
    proj3 = proj.reshape(nb, s, PW)
    z3 = z.reshape(nb, s, NQK)
    mix3, opre3, sprev, x1, gwa = _mix_fwd(proj3, z3, gla_norm_g, conv_f, stage, x, gwb)
    mix2d = mix3.reshape(t, D)
    dx1, dx1b, adu, hb, dg2, dgf, loss_part = _ffn_fwd_bwd(
        x1.reshape(t, D), tgt2d, gwa, gwb, norm2_g, norm_f_g.reshape(1, D), tm)
    dw3, dwb3 = _dw_ffn(adu, hb, tk)
    dw3 = dw3.reshape(3, N_DEV, FF_W, D)
    dw_o, dwb_o = _tn_matmul(mix2d, dx1b, D // 2, D, tk, "dw_out", True)
    dw_o = dw_o.reshape(N_DEV, OUT_ROWS, D)
    *pb, sib_d, sib_g, sib_u, sib_o, dmix = _ffn_core_reduce(
        dw3, dwb3.reshape(3, N_DEV, FF_W, D), dw_o, dwb_o.reshape(N_DEV, OUT_ROWS, D), pos_arr, dx1b, gwb)
    g8 = [dw3, dw3, dw3, dw_o]
    leads = [0, 1, 2, None]
    tags = ("w_ffn_down", "w_ffn_gate", "w_ffn_up", "w_out")
    r1 = [sib_d, sib_g, sib_u, sib_o]
    mb = _mix_bwd(proj3, z3, sprev, opre3, dmix.reshape(nb, s, D), gla_norm_g, conv_f, wgu_p, [pb[0], pb[1], pb[3]])
    dproj3, dgng, dcw, dbg, dwgu = mb[:5]
    dproj2d = dproj3.reshape(t, PW)
    dw_in_t, r2_up_near = _tn_matmul(dproj2d, h, PW // 5, D, tk, "dw_in", False, _stage2_rider([pb[2]], (1, 2)))
    g_in, r1_in, pb_in, r2_up_far = _w_in_core_reduce(dw_in_t, _stage2_rider([pb[2]], (3,)))
    r2 = [mb[5], mb[6], (r2_up_near, r2_up_far), mb[7]]
    dx, small_sums, r2_in = _in_proj_bwd(dproj2d, x2d, dx1, norm1_g, w_in_t, tm_in, pb_in,
                                         (dg2, dgf, dbg, dgng, dwgu, dcw, loss_part))

    tags = ("w_in",) + tags
    g8 = [g_in] + g8
    leads = [None] + leads
    r1 = [r1_in] + list(r1)
    r2 = [r2_in] + r2
    shard_w = (rows_of(w_in), w_ffn_down[0], tr(w_ffn_gate), tr(w_ffn_up), w_out[0])
    shard_m = (rows_of(m_w_in), m_w_ffn_down[0], tr(m_w_ffn_gate), tr(m_w_ffn_up), m_w_out[0])
    shard_v = (rows_of(v_w_in), v_w_ffn_down[0], tr(v_w_ffn_gate), tr(v_w_ffn_up), v_w_out[0])
    back = (lambda o: o.transpose(1, 2, 0), lambda o: o[None], lambda o: o.T[None], lambda o: o.T[None],
            lambda o: o[None])
    items = list(zip(g8, leads, r1, r2, shard_w, shard_m, shard_v))
    flat = list(_finish_weights(items[1:], pos_arr, "finish_ffn_out", 2))
    flat = list(_finish_weights(items[:1], pos_arr, "finish_w_in", 1)) + flat
    results = {}
    for i, (tag, to_shard) in enumerate(zip(tags, back)):
        results[tag] = [to_shard(o) for o in flat[4 * i:4 * i + 4]]

    small_w = (norm1_g, norm2_g, norm_f_g.reshape(1, D), b_gate, gla_norm_g, w_gate_up[0], conv_rows(conv_w))
    small_m = (m_norm1_g, m_norm2_g, m_norm_f_g.reshape(1, D), m_b_gate, m_gla_norm_g, m_w_gate_up[0],
               conv_rows(m_conv_w))
    small_v = (v_norm1_g, v_norm2_g, v_norm_f_g.reshape(1, D), v_b_gate, v_gla_norm_g, v_w_gate_up[0],
               conv_rows(v_conv_w))
    so = _small_adamw(small_sums, small_w, small_m, small_v)
    loss = so[0].reshape(())
    to_shape = {"norm_f_g": lambda o: o.reshape(D), "w_gate_up": lambda o: o[None],
                "conv_w": lambda o: o.transpose(1, 0, 2)}
    for i, name in enumerate(SMALL_NAMES):
        results[name] = [to_shape.get(name, lambda o: o)(o) for o in so[1 + 4 * i:5 + 4 * i]]

    names = ("norm1_g", "w_in", "w_gate_up", "b_gate", "gla_norm_g", "conv_w", "w_out", "norm2_g",
             "w_ffn_gate", "w_ffn_up", "w_ffn_down", "norm_f_g")
    outs = [loss, dx.reshape(nb, s, D)]
    for kind in range(4):
        for name in names:
            outs.append(results[name][kind])
    return tuple(outs)
```

```python
import jax
import jax.numpy as jnp
from jax import lax
from jax.experimental import pallas as pl
from jax.experimental.pallas import tpu as pltpu

F32 = jnp.float32
BF16 = jnp.bfloat16
HIGHEST = lax.Precision.HIGHEST
MESH = pl.DeviceIdType.MESH

N_DEV = 8
D = 1024
DFF = 2816
HEADS = 4
DK = 64
DV = 128
NQK = HEADS * DK
NV = HEADS * DV
RANK = 16
CHUNK = 64
CW = 512
CONV_K = 3
IN_COLS = 3088
EPS = 1e-6
INV_GATE_NORM = 1.0 / 16.0
Q_SCALE = DK ** -0.5

PW = 3200
OQ, OK_, OV, OG, OCB, OCC, OCH, OA = 0, 256, 512, 1024, 1536, 2048, 2560, 3072
A_PAD = 128

ADAM_LR = 0.001
ADAM_B1 = 0.9
ADAM_B2 = 0.999
ADAM_EPS = 1e-08
ADAM_WD = 0.01
ADAM_STEP = 10

IN_W = IN_COLS // N_DEV
IN_ROWS = 400
FF_W = DFF // N_DEV
OUT_ROWS = D // N_DEV
SLAB_IN = 0
SLAB_G = SLAB_IN + IN_ROWS
SLAB_U = SLAB_G + FF_W
SLAB_D = SLAB_U + FF_W
SLAB_O = SLAB_D + FF_W
SLAB_ROWS = SLAB_O + OUT_ROWS

VMEM_LIMIT = 56 * 1024 * 1024


def _params(sem=None, vmem=VMEM_LIMIT):
    return pltpu.CompilerParams(dimension_semantics=sem, vmem_limit_bytes=vmem)


def _nt(a, b):
    return lax.dot_general(a, b, (((1,), (1,)), ((), ())), preferred_element_type=F32)


def _tn(a, b, precision=None):
    return lax.dot_general(a, b, (((0,), (0,)), ((), ())), preferred_element_type=F32, precision=precision)


def _mm(a, b, precision=None):
    return jnp.dot(a, b, preferred_element_type=F32, precision=precision)


def _in_segments():
    segs = []
    for j in range(N_DEV):
        lo, hi = IN_W * j, IN_W * (j + 1)
        cuts = sorted({lo, hi} | {c for c in (OCB, OCB + RANK) if lo < c < hi})
        for a, b in zip(cuts[:-1], cuts[1:]):
            if a < OCB:
                d = a
            elif a < OCB + RANK:
                d = OA + (a - OCB)
            else:
                d = a - RANK
            segs.append((j, a - lo, b - lo, d))
    return segs


def _in_proj_fwd(x2d, g1, w_in_t, wgu_p, b_gate, tm, stage):
    t = x2d.shape[0]
    nt = t // tm
    g_rows = SLAB_ROWS - SLAB_D

    def body(x_ref, g_ref, w_ref, wgu_ref, bg_ref, stage_hbm, proj_ref, z_ref, h_ref, gwb_ref,
             send_sems, recv_sems, local_sem):
        gargs = (stage_hbm, SLAB_D, g_rows, gwb_ref, send_sems, recv_sems, local_sem)

        @pl.when(pl.program_id(0) == 0)
        def _():
            _gather_start(*gargs)

        x = x_ref[...]
        r = lax.rsqrt(jnp.mean(x * x, axis=-1, keepdims=True) + EPS)
        h = ((x * r) * g_ref[...]).astype(BF16)
        h_ref[...] = h
        proj = _nt(h, w_ref[...])
        proj_ref[...] = proj
        pa = proj[:, OA:OA + A_PAD].astype(BF16)
        z_ref[...] = _mm(pa, wgu_ref[...]) + bg_ref[...]

        @pl.when(pl.program_id(0) == nt - 1)
        def _():
            _gather_finish(*gargs)

    return pl.pallas_call(
        body,
        name="in_proj_fwd",
        grid=(t // tm,),
        in_specs=[
            pl.BlockSpec((tm, D), lambda i: (i, 0)),
            pl.BlockSpec((1, D), lambda i: (0, 0)),
            pl.BlockSpec((PW, D), lambda i: (0, 0)),
            pl.BlockSpec((A_PAD, NQK), lambda i: (0, 0)),
            pl.BlockSpec((1, NQK), lambda i: (0, 0)),
            pl.BlockSpec(memory_space=pl.ANY),
        ],
        out_specs=[
            pl.BlockSpec((tm, PW), lambda i: (i, 0)),
            pl.BlockSpec((tm, NQK), lambda i: (i, 0)),
            pl.BlockSpec((tm, D), lambda i: (i, 0)),
            pl.BlockSpec(memory_space=pl.ANY),
        ],
        out_shape=[
            jax.ShapeDtypeStruct((t, PW), F32),
            jax.ShapeDtypeStruct((t, NQK), F32),
            jax.ShapeDtypeStruct((t, D), BF16),
            jax.ShapeDtypeStruct((N_DEV, g_rows, D), BF16),
        ],
        scratch_shapes=_gather_sems(),
        compiler_params=_params(("arbitrary",)),
    )(x2d, g1, w_in_t, wgu_p, b_gate, stage)


def _head_masks():
    lane = lax.broadcasted_iota(jnp.int32, (1, NQK), 1)
    return [(lane >= DK * h) & (lane < DK * (h + 1)) for h in range(HEADS)]


def _split_bf16(x, n):
    parts = []
    for _ in range(n):
        p = x.astype(BF16)
        parts.append(p)
        x = x - p.astype(F32)
    return parts


def _chunk_fwd_parts(q, k, z, tril16):
    la = (jnp.minimum(z, 0.0) - jnp.log1p(jnp.exp(-jnp.abs(z)))) * INV_GATE_NORM
    la_parts = _split_bf16(la, 3)
    bc = _mm(tril16, la_parts[0]) + _mm(tril16, la_parts[1]) + _mm(tril16, la_parts[2])
    bl = bc[CHUNK - 1:CHUNK, :]
    eb = jnp.exp(bc)
    enb = jnp.exp(-bc)
    ekl = jnp.exp(bl - bc)
    qi = (q * Q_SCALE) * eb
    ki = k * enb
    ks = k * ekl
    ones16 = jnp.ones((CHUNK, DV), BF16)
    decb = jnp.exp(_tn(la_parts[0], ones16) + _tn(la_parts[1], ones16) + _tn(la_parts[2], ones16))
    return la, eb, enb, ekl, qi, ki, ks, decb


def _stack_heads(a, masks):
    return jnp.concatenate([jnp.where(m, a, 0.0) for m in masks], axis=0)


def _merge_heads(blocks, masks):
    out = blocks[HEADS - 1]
    for h in range(HEADS - 2, -1, -1):
        out = jnp.where(masks[h], blocks[h], out)
    return out


def _causal_stack_mask():
    row = lax.broadcasted_iota(jnp.int32, (HEADS * CHUNK, CHUNK), 0)
    col = lax.broadcasted_iota(jnp.int32, (HEADS * CHUNK, CHUNK), 1)
    return (row & (CHUNK - 1)) >= col


def _conv_taps(u, uprev):
    row = lax.broadcasted_iota(jnp.int32, u.shape, 0)
    u1 = jnp.where(row < 1, pltpu.roll(uprev, 1, 0), pltpu.roll(u, 1, 0))
    u2 = jnp.where(row < 2, pltpu.roll(uprev, 2, 0), pltpu.roll(u, 2, 0))
    return u1, u2


def _mix_fwd(proj3, z3, gng, conv_w, stage, x3, gwb):
    nb, s, _ = proj3.shape
    nc = s // CHUNK
    g_rows = SLAB_D - SLAB_G

    def body(p_ref, z_ref, gng_ref, cw_ref, stage_hbm, x_ref, gwb_hbm, mix_ref, o_ref, sprev_ref, x1_ref, gwa_ref,
             s_ref, uprev_ref, wo, wsem, send_sems, recv_sems, local_sem):
        n = pl.program_id(0)
        gargs = (stage_hbm, SLAB_G, g_rows, gwa_ref, send_sems, recv_sems, local_sem)

        @pl.when(n == 0)
        def _():
            _gather_start(*gargs)
            loads = [pltpu.make_async_copy(gwb_hbm.at[j, pl.ds(FF_W, OUT_ROWS), :],
                                           wo.at[pl.ds(OUT_ROWS * j, OUT_ROWS), :], wsem.at[j]) for j in range(N_DEV)]
            for cp in loads:
                cp.start()
            s_ref[...] = jnp.zeros_like(s_ref)
            uprev_ref[...] = jnp.zeros_like(uprev_ref)
            for cp in loads:
                cp.wait()

        r_i = lax.broadcasted_iota(jnp.int32, (CHUNK, CHUNK), 0)
        c_i = lax.broadcasted_iota(jnp.int32, (CHUNK, CHUNK), 1)
        tril16 = (r_i >= c_i).astype(BF16)
        masks = _head_masks()
        cmask = _causal_stack_mask()
        gg = gng_ref[...]
        for b in range(nb):
            q = p_ref[b, :, OQ:OQ + NQK]
            k = p_ref[b, :, OK_:OK_ + NQK]
            _, _, _, _, qi, ki, ks, decb = _chunk_fwd_parts(q, k, z_ref[b], tril16)
            qs = _stack_heads(qi, masks).astype(BF16)
            sc = jnp.where(cmask, _nt(qs, ki.astype(BF16)), 0.0).astype(BF16)
            st = s_ref[b]
            sprev_ref[b, 0] = st
            o_inter = _mm(qs, st.astype(BF16))
            v16 = p_ref[b, :, OV:OV + NV].astype(BF16)
            kv = _tn(ks.astype(BF16), v16)
            for h in range(HEADS):
                rows = slice(CHUNK * h, CHUNK * (h + 1))
                cols = slice(DV * h, DV * (h + 1))
                o = _mm(sc[rows], v16[:, cols]) + o_inter[rows]
                o_ref[b, :, cols] = o
                r = lax.rsqrt(jnp.mean(o * o, axis=-1, keepdims=True) + EPS)
                on = (o * r) * gg
                g = p_ref[b, :, OG + DV * h:OG + DV * (h + 1)]
                mix_ref[b, :, cols] = (on * (g * jax.nn.sigmoid(g))).astype(BF16)
                s_ref[b, rows, :] = decb[rows] * st[rows] + kv[rows, cols]
            u = p_ref[b, :, OCC:OCC + CW] * p_ref[b, :, OCH:OCH + CW]
            u1, u2 = _conv_taps(u, uprev_ref[b])
            yc = cw_ref[0:1, :] * u2 + cw_ref[1:2, :] * u1 + cw_ref[2:3, :] * u
            mix_ref[b, :, NV:NV + CW] = (p_ref[b, :, OCB:OCB + CW] * yc).astype(BF16)
            uprev_ref[b] = u
        mixed = _mm(jnp.concatenate([mix_ref[b] for b in range(nb)], axis=0), wo[...])
        for b in range(nb):
            x1_ref[b] = x_ref[b] + mixed[CHUNK * b:CHUNK * (b + 1)]

        @pl.when(n == nc - 1)
        def _():
            _gather_finish(*gargs)

    return pl.pallas_call(
        body,
        name="mix_fwd",
        grid=(nc,),
        in_specs=[
            pl.BlockSpec((nb, CHUNK, PW), lambda n: (0, n, 0)),
            pl.BlockSpec((nb, CHUNK, NQK), lambda n: (0, n, 0)),
            pl.BlockSpec((1, DV), lambda n: (0, 0)),
            pl.BlockSpec((CONV_K, CW), lambda n: (0, 0)),
            pl.BlockSpec(memory_space=pl.ANY),
            pl.BlockSpec((nb, CHUNK, D), lambda n: (0, n, 0)),
            pl.BlockSpec(memory_space=pl.ANY),
        ],
        out_specs=[
            pl.BlockSpec((nb, CHUNK, D), lambda n: (0, n, 0)),
            pl.BlockSpec((nb, CHUNK, NV), lambda n: (0, n, 0)),
            pl.BlockSpec((nb, 1, NQK, DV), lambda n: (0, n, 0, 0)),
            pl.BlockSpec((nb, CHUNK, D), lambda n: (0, n, 0)),
            pl.BlockSpec(memory_space=pl.ANY),
        ],
        out_shape=[
            jax.ShapeDtypeStruct((nb, s, D), BF16),
            jax.ShapeDtypeStruct((nb, s, NV), F32),
            jax.ShapeDtypeStruct((nb, nc, NQK, DV), F32),
            jax.ShapeDtypeStruct((nb, s, D), F32),
            jax.ShapeDtypeStruct((N_DEV, g_rows, D), BF16),
        ],
        scratch_shapes=[pltpu.VMEM((nb, NQK, DV), F32), pltpu.VMEM((nb, CHUNK, CW), F32),
                        pltpu.VMEM((D, D), BF16), pltpu.SemaphoreType.DMA((N_DEV,))] + _gather_sems(),
        compiler_params=_params(("arbitrary",)),
    )(proj3, z3, gng, conv_w, stage, x3, gwb)


def _ffn_fwd_bwd(x1_2d, tgt2d, gwa, gwb, g2, gf, tm):
    t = x1_2d.shape[0]

    def body(x1_ref, tgt_ref, g2_ref, gf_ref, gwa_hbm, gwb_hbm,
             dx1_ref, dx1b_ref, adu_ref, hb_ref, dg2_ref, dgf_ref, loss_ref,
             wg, wu, wd, wsem):
        i = pl.program_id(0)

        def weight_copies(n, dst, src, off, rows):
            return [pltpu.make_async_copy(src.at[j, pl.ds(off, rows), :], dst.at[pl.ds(rows * j, rows), :],
                                          wsem.at[N_DEV * n + j]) for j in range(N_DEV)]

        loads = (weight_copies(0, wg, gwa_hbm, 0, FF_W), weight_copies(1, wu, gwa_hbm, FF_W, FF_W),
                 weight_copies(2, wd, gwb_hbm, 0, FF_W))

        @pl.when(i == 0)
        def _():
            for group in loads:
                for cp in group:
                    cp.start()
            dg2_ref[...] = jnp.zeros_like(dg2_ref)
            dgf_ref[...] = jnp.zeros_like(dgf_ref)
            loss_ref[...] = jnp.zeros_like(loss_ref)
            for group in loads:
                for cp in group:
                    cp.wait()

        g2v = g2_ref[...]
        gfv = gf_ref[...]
        x1 = x1_ref[...]
        r2 = lax.rsqrt(jnp.mean(x1 * x1, axis=-1, keepdims=True) + EPS)
        n2 = x1 * r2
        h2 = (n2 * g2v).astype(BF16)
        hb_ref[1] = h2
        gate = _nt(h2, wg[...])
        up = _nt(h2, wu[...])
        sg = jax.nn.sigmoid(gate)
        sil = gate * sg
        act = (sil * up).astype(BF16)
        adu_ref[0] = act
        x2 = x1 + _mm(act, wd[...])
        rf = lax.rsqrt(jnp.mean(x2 * x2, axis=-1, keepdims=True) + EPS)
        nf = x2 * rf
        err = nf * gfv - tgt_ref[...]
        loss_ref[...] += 0.5 * jnp.sum(jnp.mean(err * err, axis=-1, keepdims=True))
        dy = err * (1.0 / D)
        dgf_ref[...] += jnp.sum(dy * nf, axis=0, keepdims=True)
        dnf = dy * gfv
        dx2 = rf * (dnf - nf * jnp.mean(dnf * nf, axis=-1, keepdims=True))
        dx2b = dx2.astype(BF16)
        hb_ref[0] = dx2b
        dact = _nt(dx2b, wd[...])
        dup = (dact * sil).astype(BF16)
        dgate = ((dact * up) * (sg * (1.0 + gate * (1.0 - sg)))).astype(BF16)
        adu_ref[2] = dup
        adu_ref[1] = dgate
        dh2 = _mm(dgate, wg[...]) + _mm(dup, wu[...])
        dg2_ref[...] += jnp.sum(dh2 * n2, axis=0, keepdims=True)
        dn2 = dh2 * g2v
        dx1 = dx2 + r2 * (dn2 - n2 * jnp.mean(dn2 * n2, axis=-1, keepdims=True))
        dx1_ref[...] = dx1
        dx1b_ref[...] = dx1.astype(BF16)

    tile = lambda w: pl.BlockSpec((tm, w), lambda i: (i, 0))
    vec = pl.BlockSpec((1, D), lambda i: (0, 0))
    hbm = pl.BlockSpec(memory_space=pl.ANY)
    return pl.pallas_call(
        body,
        name="ffn_fwd_bwd",
        grid=(t // tm,),
        in_specs=[tile(D), tile(D), vec, vec, hbm, hbm],
        out_specs=[tile(D), tile(D), pl.BlockSpec((3, tm, DFF), lambda i: (0, i, 0)),
                   pl.BlockSpec((2, tm, D), lambda i: (0, i, 0)), vec, vec,
                   pl.BlockSpec((1, 128), lambda i: (0, 0))],
        out_shape=[
            jax.ShapeDtypeStruct((t, D), F32),
            jax.ShapeDtypeStruct((t, D), BF16),
            jax.ShapeDtypeStruct((3, t, DFF), BF16),
            jax.ShapeDtypeStruct((2, t, D), BF16),
            jax.ShapeDtypeStruct((1, D), F32),
            jax.ShapeDtypeStruct((1, D), F32),
            jax.ShapeDtypeStruct((1, 128), F32),
        ],
        scratch_shapes=[pltpu.VMEM((DFF, D), BF16), pltpu.VMEM((DFF, D), BF16), pltpu.VMEM((DFF, D), BF16),
                        pltpu.SemaphoreType.DMA((3 * N_DEV,))],
        compiler_params=_params(("arbitrary",)),
    )(x1_2d, tgt2d, g2, gf, gwa, gwb)


def _stage2_rider(pbs):
    return dict(inputs=list(pbs), out_shape=[jax.ShapeDtypeStruct(p.shape, BF16) for p in pbs], nsem=3 * len(pbs),
                copies=_stage2_copies)


def _tn_matmul(a, b, bm, bn, tk, name, with_bf16, rider=None):
    t, m = a.shape
    n = b.shape[1]
    nk = t // tk
    nout = 2 if with_bf16 else 1
    grid = (m // bm, n // bn, nk)
    r_in = [] if rider is None else rider["inputs"]
    r_out = [] if rider is None else rider["out_shape"]

    def body(a_ref, b_ref, *rest):
        ins, outs = rest[:len(r_in)], rest[len(r_in):len(r_in) + nout]
        r_outs, sems = rest[len(r_in) + nout:len(r_in) + nout + len(r_out)], rest[len(r_in) + nout + len(r_out):]
        o_ref = outs[0]
        i, j, k = pl.program_id(0), pl.program_id(1), pl.program_id(2)
        if rider is not None:
            @pl.when((i == 0) & (j == 0) & (k == 0))
            def _():
                for cp in rider["copies"](ins, r_outs, *sems):
                    cp.start()

        @pl.when(k == 0)
        def _():
            o_ref[...] = jnp.zeros_like(o_ref)

        o_ref[...] += _tn(a_ref[...].astype(BF16), b_ref[...].astype(BF16))
        if with_bf16:
            @pl.when(k == nk - 1)
            def _():
                outs[1][...] = o_ref[...].astype(BF16)
        if rider is not None:
            @pl.when((i == grid[0] - 1) & (j == grid[1] - 1) & (k == nk - 1))
            def _():
                copies = rider["copies"](ins, r_outs, *sems)
                for cp in copies:
                    cp.wait_recv()
                for cp in copies:
                    cp.wait_send()

    out_blk = pl.BlockSpec((bm, bn), lambda i, j, k: (i, j))
    hbm = pl.BlockSpec(memory_space=pl.ANY)
    out_shape = [jax.ShapeDtypeStruct((m, n), F32)] + ([jax.ShapeDtypeStruct((m, n), BF16)] if with_bf16 else [])
    res = pl.pallas_call(
        body,
        name=name,
        grid=grid,
        in_specs=[pl.BlockSpec((tk, bm), lambda i, j, k: (k, i)), pl.BlockSpec((tk, bn), lambda i, j, k: (k, j))]
        + [hbm] * len(r_in),
        out_specs=[out_blk] * nout + [hbm] * len(r_out),
        out_shape=out_shape + list(r_out),
        scratch_shapes=([] if rider is None else
                        [pltpu.SemaphoreType.DMA((rider["nsem"],)), pltpu.SemaphoreType.DMA((rider["nsem"],))]),
        compiler_params=_params(("parallel", "parallel", "arbitrary") if rider is None
                                else ("arbitrary", "arbitrary", "arbitrary")),
    )(a, b, *r_in)
    return res[0] if len(res) == 1 else res


def _dw_ffn(adu, hb, tk):
    _, t, _ = adu.shape
    bm = DFF // 2
    nk = t // tk

    def body(a_ref, b_ref, o_ref, ob_ref):
        k = pl.program_id(2)

        @pl.when(k == 0)
        def _():
            o_ref[...] = jnp.zeros_like(o_ref)

        o_ref[...] += _tn(a_ref[...], b_ref[...])

        @pl.when(k == nk - 1)
        def _():
            ob_ref[...] = o_ref[...].astype(BF16)

    out_blk = pl.BlockSpec((None, bm, D), lambda p, i, k: (p, i, 0))
    return pl.pallas_call(
        body,
        name="dw_ffn",
        grid=(3, DFF // bm, nk),
        in_specs=[pl.BlockSpec((None, tk, bm), lambda p, i, k: (p, k, i)),
                  pl.BlockSpec((None, tk, D), lambda p, i, k: (jnp.minimum(p, 1), k, 0))],
        out_specs=[out_blk, out_blk],
        out_shape=[jax.ShapeDtypeStruct((3, DFF, D), F32), jax.ShapeDtypeStruct((3, DFF, D), BF16)],
        compiler_params=_params(("arbitrary", "arbitrary", "arbitrary")),
    )(adu, hb)


def _mix_bwd(proj3, z3, sprev, opre3, dmix3, gng, conv_w, wgu_p, pbs):
    nb, s, _ = proj3.shape
    nc = s // CHUNK
    na = len(pbs)

    def body(*refs):
        (p_ref, pprev_ref, z_ref, sp_ref, o_ref, dm_ref, gng_ref, cw_ref, wgu_ref) = refs[:9]
        pb_refs = refs[9:9 + na]
        (dproj_ref, dgng_ref, dcw_ref, dbg_ref, dwgu_ref) = refs[9 + na:14 + na]
        r2_refs = refs[14 + na:14 + 2 * na]
        ds_ref, dycn_ref, send_sems, recv_sems = refs[14 + 2 * na:]
        step = pl.program_id(0)
        n = nc - 1 - step

        @pl.when(step == 0)
        def _():
            for cp in _stage2_copies(pb_refs, r2_refs, send_sems, recv_sems):
                cp.start()
            ds_ref[...] = jnp.zeros_like(ds_ref)
            dycn_ref[...] = jnp.zeros_like(dycn_ref)
            dgng_ref[...] = jnp.zeros_like(dgng_ref)
            dcw_ref[...] = jnp.zeros_like(dcw_ref)
            dbg_ref[...] = jnp.zeros_like(dbg_ref)
            dwgu_ref[...] = jnp.zeros_like(dwgu_ref)

        r_i = lax.broadcasted_iota(jnp.int32, (CHUNK, CHUNK), 0)
        c_i = lax.broadcasted_iota(jnp.int32, (CHUNK, CHUNK), 1)
        tril16 = (r_i >= c_i).astype(BF16)
        triu16 = (r_i <= c_i).astype(BF16)
        causal = r_i >= c_i
        masks = _head_masks()
        cmask = _causal_stack_mask()
        gg = gng_ref[...]
        last_row = lax.broadcasted_iota(jnp.int32, (CHUNK, NQK), 0) == CHUNK - 1
        ones_r = jnp.ones((16, DV), BF16)
        has_prev = (n > 0).astype(F32)
        for b in range(nb):
            q = p_ref[b, :, OQ:OQ + NQK]
            k = p_ref[b, :, OK_:OK_ + NQK]
            z = z_ref[b]
            _, eb, enb, ekl, qi, ki, ks, decb = _chunk_fwd_parts(q, k, z, tril16)
            qi16 = qi.astype(BF16)
            ki16 = ki.astype(BF16)
            qs = _stack_heads(qi, masks).astype(BF16)
            sc = jnp.where(cmask, _nt(qs, ki16), 0.0).astype(BF16)
            st = sp_ref[b, 0]
            st16 = st.astype(BF16)
            dsn = ds_ref[b]
            dsn16 = dsn.astype(BF16)
            v16 = p_ref[b, :, OV:OV + NV].astype(BF16)
            do16 = []
            dgng = jnp.zeros((1, DV), F32)
            for h in range(HEADS):
                cols = slice(DV * h, DV * (h + 1))
                o = o_ref[b, :, cols]
                r = lax.rsqrt(jnp.mean(o * o, axis=-1, keepdims=True) + EPS)
                nh = o * r
                g = p_ref[b, :, OG + DV * h:OG + DV * (h + 1)]
                sg = jax.nn.sigmoid(g)
                dog = dm_ref[b, :, cols]
                dproj_ref[b, :, OG + DV * h:OG + DV * (h + 1)] = (
                    (dog * (nh * gg)) * (sg * (1.0 + g * (1.0 - sg)))).astype(BF16)
                don = dog * (g * sg)
                dgng = dgng + jnp.sum(don * nh, axis=0, keepdims=True)
                dn = don * gg
                do = r * (dn - nh * jnp.mean(dn * nh, axis=-1, keepdims=True))
                do16.append(do.astype(BF16))
            dgng_ref[...] += dgng
            do_rows = jnp.concatenate(do16, axis=0)
            v_rows = jnp.concatenate([v16[:, DV * h:DV * (h + 1)] for h in range(HEADS)], axis=0)
            dp16 = [jnp.where(causal, _nt(do16[h], v16[:, DV * h:DV * (h + 1)]), 0.0).astype(BF16)
                    for h in range(HEADS)]
            ks_dsn = _mm(_stack_heads(ks, masks).astype(BF16), dsn16)
            do_st = _nt(do_rows, st16)
            v_dsn = _nt(v_rows, dsn16)
            dp_ki = _mm(jnp.concatenate(dp16, axis=0), ki16)
            q_do = _tn(qi16, jnp.concatenate(do16, axis=1))
            dki_h = []
            for h in range(HEADS):
                rows = slice(CHUNK * h, CHUNK * (h + 1))
                cols = slice(DV * h, DV * (h + 1))
                dv = _tn(sc[rows], do16[h]) + ks_dsn[rows]
                dproj_ref[b, :, OV + DV * h:OV + DV * (h + 1)] = dv.astype(BF16)
                dki_h.append(_tn(dp16[h], qi16))
                ds_ref[b, rows, :] = decb[rows] * dsn[rows] + q_do[rows, cols]
            blocks = lambda a: [a[CHUNK * h:CHUNK * (h + 1)] for h in range(HEADS)]
            dqi = _merge_heads(blocks(dp_ki + do_st), masks)
            dki = _merge_heads(dki_h, masks)
            dks = _merge_heads(blocks(v_dsn), masks)
            dproj_ref[b, :, OQ:OQ + NQK] = (dqi * (Q_SCALE * eb)).astype(BF16)
            dproj_ref[b, :, OK_:OK_ + NQK] = (dki * enb + dks * ekl).astype(BF16)
            dks_ks = dks * ks
            db = dqi * qi - dki * ki - dks_ks
            sd = _split_bf16(dsn * st * decb, 2)
            dbl = jnp.sum(dks_ks, axis=0, keepdims=True) + (_nt(ones_r, sd[0]) + _nt(ones_r, sd[1]))[0:1, :]
            db = db + jnp.where(last_row, dbl, 0.0)
            db_parts = _split_bf16(db, 3)
            dla = _mm(triu16, db_parts[0]) + _mm(triu16, db_parts[1]) + _mm(triu16, db_parts[2])
            dz = (dla * INV_GATE_NORM) * (1.0 / (1.0 + jnp.exp(z)))
            dbg_ref[...] += jnp.sum(dz, axis=0, keepdims=True)
            dz16 = dz.astype(BF16)
            pa16 = p_ref[b, :, OA:OA + A_PAD].astype(BF16)
            dwgu_ref[...] += _tn(pa16, dz16)
            dproj_ref[b, :, OA:OA + A_PAD] = _nt(dz16, wgu_ref[...]).astype(BF16)
            cb = p_ref[b, :, OCB:OCB + CW]
            cc = p_ref[b, :, OCC:OCC + CW]
            ch = p_ref[b, :, OCH:OCH + CW]
            u = cc * ch
            uprev = (pprev_ref[b, :, 0:CW] * pprev_ref[b, :, CW:2 * CW]) * has_prev
            u1, u2 = _conv_taps(u, uprev)
            w0 = cw_ref[0:1, :]
            w1 = cw_ref[1:2, :]
            w2 = cw_ref[2:3, :]
            yc = w0 * u2 + w1 * u1 + w2 * u
            doc = dm_ref[b, :, NV:NV + CW]
            dproj_ref[b, :, OCB:OCB + CW] = (doc * yc).astype(BF16)
            dyc = doc * cb
            dycn = dycn_ref[b]
            row = lax.broadcasted_iota(jnp.int32, dyc.shape, 0)
            d1 = jnp.where(row >= CHUNK - 1, pltpu.roll(dycn, CHUNK - 1, 0), pltpu.roll(dyc, CHUNK - 1, 0))
            d2 = jnp.where(row >= CHUNK - 2, pltpu.roll(dycn, CHUNK - 2, 0), pltpu.roll(dyc, CHUNK - 2, 0))
            du = w2 * dyc + w1 * d1 + w0 * d2
            dproj_ref[b, :, OCC:OCC + CW] = (du * ch).astype(BF16)
            dproj_ref[b, :, OCH:OCH + CW] = (du * cc).astype(BF16)
            dcw_ref[0:1, :] += jnp.sum(dyc * u2, axis=0, keepdims=True)
            dcw_ref[1:2, :] += jnp.sum(dyc * u1, axis=0, keepdims=True)
            dcw_ref[2:3, :] += jnp.sum(dyc * u, axis=0, keepdims=True)
            dycn_ref[b] = dyc

        @pl.when(step == nc - 1)
        def _():
            copies = _stage2_copies(pb_refs, r2_refs, send_sems, recv_sems)
            for cp in copies:
                cp.wait_recv()
            for cp in copies:
                cp.wait_send()

    rev = lambda w: pl.BlockSpec((nb, CHUNK, w), lambda i: (0, nc - 1 - i, 0))
    const = lambda r, c: pl.BlockSpec((r, c), lambda i: (0, 0))
    hbm = pl.BlockSpec(memory_space=pl.ANY)
    return pl.pallas_call(
        body,
        name="mix_bwd",
        grid=(nc,),
        in_specs=[
            rev(PW),
            pl.BlockSpec((nb, CHUNK, 2 * CW), lambda i: (0, jnp.maximum(nc - 2 - i, 0), OCC // (2 * CW))),
            rev(NQK),
            pl.BlockSpec((nb, 1, NQK, DV), lambda i: (0, nc - 1 - i, 0, 0)),
            rev(NV),
            rev(D),
            const(1, DV),
            const(CONV_K, CW),
            const(A_PAD, NQK),
        ] + [hbm] * na,
        out_specs=[rev(PW), const(1, DV), const(8, CW), const(1, NQK), const(A_PAD, NQK)] + [hbm] * na,
        out_shape=[
            jax.ShapeDtypeStruct((nb, s, PW), BF16),
            jax.ShapeDtypeStruct((1, DV), F32),
            jax.ShapeDtypeStruct((8, CW), F32),
            jax.ShapeDtypeStruct((1, NQK), F32),
            jax.ShapeDtypeStruct((A_PAD, NQK), F32),
        ] + [jax.ShapeDtypeStruct((3,) + p.shape[1:], BF16) for p in pbs],
        scratch_shapes=[pltpu.VMEM((nb, NQK, DV), F32), pltpu.VMEM((nb, CHUNK, CW), F32),
                        pltpu.SemaphoreType.DMA((3 * na,)), pltpu.SemaphoreType.DMA((3 * na,))],
        compiler_params=_params(("arbitrary",)),
    )(proj3, proj3, z3, sprev, opre3, dmix3, gng, conv_w, wgu_p, *pbs)


SMALL_PACK_ROWS = 16


def _wgu_slot(r):
    return 4 + r // 4, NQK * (r % 4)


CONV_SLOTS = ((8, 0), (8, CW), (9, 0))


def _in_proj_bwd(dproj2d, x2d, dx1, g1, w_in_t, tm, pb, small_parts):
    t = x2d.shape[0]
    nt = t // tm

    def body(dp_ref, x_ref, dx1_ref, g_ref, w_ref, pb_ref, dg2, dgf, dbg, dgng, dwgu, dcw, lp,
             dx_ref, sums_ref, r2_ref, dg1_acc, pack, gbuf, pack1, gbuf1, send_sems, recv_sems,
             ssend, srecv, ssend1, srecv1):
        x, y, c = _position()
        me = 4 * x + 2 * y + c
        flips = [(k >> 2, (k >> 1) & 1, k & 1) for k in range(1, N_DEV)]
        peers = [(x ^ fx, y ^ fy, c ^ fc) for fx, fy, fc in flips]

        def small_copies(src, dst, send, recv, arrivals):
            return [pltpu.make_async_remote_copy(
                src_ref=src, dst_ref=dst.at[4 * px + 2 * py + pc if arrivals else me],
                send_sem=send.at[k], recv_sem=recv.at[k], device_id=(px, py, pc), device_id_type=MESH)
                for k, (px, py, pc) in enumerate(peers)]

        @pl.when(pl.program_id(0) == 0)
        def _():
            for cp in _stage2_copies([pb_ref], [r2_ref], send_sems, recv_sems):
                cp.start()
            dg1_acc[...] = jnp.zeros_like(dg1_acc)
            pack[...] = jnp.zeros_like(pack)
            pack[1:2, :] = dg2[...]
            pack[2:3, :] = dgf[...]
            pack[3:4, 0:NQK] = dbg[...]
            pack[3:4, NQK:NQK + DV] = dgng[...]
            pack[3:4, NQK + DV:NQK + 2 * DV] = lp[...]
            for r in range(RANK):
                row, lane = _wgu_slot(r)
                pack[row:row + 1, lane:lane + NQK] = dwgu[r:r + 1, :]
            for r, (row, lane) in enumerate(CONV_SLOTS):
                pack[row:row + 1, lane:lane + CW] = dcw[r:r + 1, :]
            for cp in small_copies(pack, gbuf, ssend, srecv, False):
                cp.start()
            gbuf[me] = pack[...]

        xv = x_ref[...]
        r = lax.rsqrt(jnp.mean(xv * xv, axis=-1, keepdims=True) + EPS)
        n1 = xv * r
        dh = _mm(dp_ref[...], w_ref[...])
        dg1_acc[...] += jnp.sum(dh * n1, axis=0, keepdims=True)
        dn = dh * g_ref[...]
        dx_ref[...] = dx1_ref[...] + r * (dn - n1 * jnp.mean(dn * n1, axis=-1, keepdims=True))

        @pl.when(pl.program_id(0) == nt - 1)
        def _():
            pack1[...] = jnp.zeros_like(pack1)
            pack1[0:1, :] = dg1_acc[...]
            for cp in small_copies(pack1, gbuf1, ssend1, srecv1, False):
                cp.start()
            gbuf1[me] = pack1[...]
            copies = _stage2_copies([pb_ref], [r2_ref], send_sems, recv_sems)
            for cp in copies:
                cp.wait_recv()
            for cp in copies:
                cp.wait_send()
            for src, dst, send, recv in ((pack, gbuf, ssend, srecv), (pack1, gbuf1, ssend1, srecv1)):
                for cp in small_copies(src, dst, send, recv, True):
                    cp.wait_recv()
                    cp.wait_send()
            acc = gbuf[0]
            acc1 = gbuf1[0]
            for d in range(1, N_DEV):
                acc = acc + gbuf[d]
                acc1 = acc1 + gbuf1[d]
            sums_ref[...] = acc
            sums_ref[0:1, :] = acc1[0:1, :]

    tile = lambda w: pl.BlockSpec((tm, w), lambda i: (i, 0))
    vec = pl.BlockSpec((1, D), lambda i: (0, 0))
    hbm = pl.BlockSpec(memory_space=pl.ANY)
    whole = lambda a: pl.BlockSpec(a.shape, lambda i: (0,) * a.ndim)
    return pl.pallas_call(
        body,
        name="in_proj_bwd",
        grid=(nt,),
        in_specs=[tile(PW), tile(D), tile(D), vec, pl.BlockSpec((PW, D), lambda i: (0, 0)), hbm]
        + [whole(a) for a in small_parts],
        out_specs=[tile(D), pl.BlockSpec((SMALL_PACK_ROWS, D), lambda i: (0, 0)), hbm],
        out_shape=[jax.ShapeDtypeStruct((t, D), F32), jax.ShapeDtypeStruct((SMALL_PACK_ROWS, D), F32),
                   jax.ShapeDtypeStruct((3,) + pb.shape[1:], BF16)],
        scratch_shapes=[pltpu.VMEM((1, D), F32),
                        pltpu.VMEM((SMALL_PACK_ROWS, D), F32), pltpu.VMEM((N_DEV, SMALL_PACK_ROWS, D), F32),
                        pltpu.VMEM((8, D), F32), pltpu.VMEM((N_DEV, 8, D), F32),
                        pltpu.SemaphoreType.DMA((3,)), pltpu.SemaphoreType.DMA((3,)),
                        pltpu.SemaphoreType.DMA((7,)), pltpu.SemaphoreType.DMA((7,)),
                        pltpu.SemaphoreType.DMA((7,)), pltpu.SemaphoreType.DMA((7,))],
        compiler_params=_params(("arbitrary",)),
    )(dproj2d, x2d, dx1, g1, w_in_t, pb, *small_parts)


def _get_rows(ref):
    return ref[:, 0, :] if len(ref.shape) == 3 else ref[...]


def _put_rows(ref, val):
    if len(ref.shape) == 3:
        ref[:, 0, :] = val
    else:
        ref[...] = val


def _adamw_math(w, g, m, v):
    m = ADAM_B1 * m + (1.0 - ADAM_B1) * g
    v = ADAM_B2 * v + (1.0 - ADAM_B2) * (g * g)
    m_hat = m / (1.0 - ADAM_B1 ** ADAM_STEP)
    v_hat = v / (1.0 - ADAM_B2 ** ADAM_STEP)
    delta = -ADAM_LR * (m_hat / (jnp.sqrt(v_hat) + ADAM_EPS) + ADAM_WD * w)
    return delta, m, v


def _position():
    return lax.axis_index("x"), lax.axis_index("y"), lax.axis_index("c")


GATHER_PARTS = 2
GATHER_SEMS = 7 * GATHER_PARTS


def _gather_copies(stage, lo, rows, gx, send_sems, recv_sems, local_sem):
    x, y, c = _position()
    me = (x, y, c)
    sibling = (x, y, 1 - c)
    chips = [(1 - x, y), (x, 1 - y), (1 - x, 1 - y)]
    part = -(-rows // (16 * GATHER_PARTS)) * 16
    bounds = [(p * part, min(part, rows - p * part)) for p in range(GATHER_PARTS)]

    def blk(px, py, pc, off, n):
        return gx.at[4 * px + 2 * py + pc, pl.ds(off, n), :]

    mine = pltpu.make_async_copy(stage.at[pl.ds(lo, rows), :], gx.at[4 * x + 2 * y + c], local_sem)
    parts = []
    for p, (off, n) in enumerate(bounds):
        def copy(k, block, to, from_stage=False, p=p, off=off, n=n):
            return pltpu.make_async_remote_copy(
                src_ref=stage.at[pl.ds(lo + off, n), :] if from_stage else blk(*block, off, n),
                dst_ref=blk(*block, off, n), send_sem=send_sems.at[7 * p + k], recv_sem=recv_sems.at[7 * p + k],
                device_id=to, device_id_type=MESH)

        first = [copy(0, me, sibling, True)] + [copy(1 + j, me, (*chip, c), True) for j, chip in enumerate(chips)]
        passed = [copy(4 + j, (*chip, c), sibling) for j, chip in enumerate(chips)]
        arrivals = ([copy(0, sibling, me)] + [copy(1 + j, (*chip, c), me) for j, chip in enumerate(chips)]
                    + [copy(4 + j, (*chip, 1 - c), me) for j, chip in enumerate(chips)])
        parts.append((first, passed, arrivals))
    return mine, parts


def _gather_start(*args):
    mine, parts = _gather_copies(*args)
    mine.start()
    for first, _, _ in parts:
        for cp in first:
            cp.start()


def _gather_finish(*args):
    mine, parts = _gather_copies(*args)
    for _, passed, arrivals in parts:
        for j in range(3):
            arrivals[1 + j].wait_recv()
            passed[j].start()
    for first, passed, arrivals in parts:
        arrivals[0].wait_recv()
        for j in range(3):
            arrivals[4 + j].wait_recv()
        for cp in first + passed:
            cp.wait_send()
    mine.wait()


def _gather_sems():
    return [pltpu.SemaphoreType.DMA((GATHER_SEMS,)), pltpu.SemaphoreType.DMA((GATHER_SEMS,)), pltpu.SemaphoreType.DMA]


def _gather_w_in(w_it, w_gt, w_ut, w_d, w_o, wgu_s, conv_s):
    def body(wi_ref, wg_ref, wu_ref, wd_ref, wo_ref, wgu_ref, conv_ref, w_ref, gwgu_ref, gconv_ref, stage,
             buf, send_sems, recv_sems, local_sem, ssend, srecv):
        x, y, c = _position()
        me = 4 * x + 2 * y + c
        stage[SLAB_IN:SLAB_IN + IN_W, :] = wi_ref[:, 0, :].astype(BF16)
        stage[SLAB_IN + IN_W:SLAB_G, :] = jnp.zeros((IN_ROWS - IN_W, D), BF16)
        args = (stage, SLAB_IN, IN_ROWS, buf, send_sems, recv_sems, local_sem)
        _gather_start(*args)
        stage[SLAB_G:SLAB_U, :] = wg_ref[...].astype(BF16)
        stage[SLAB_U:SLAB_D, :] = wu_ref[...].astype(BF16)
        stage[SLAB_D:SLAB_O, :] = wd_ref[...].astype(BF16)
        stage[SLAB_O:SLAB_ROWS, :] = wo_ref[...].astype(BF16)
        flips = [(k >> 2, (k >> 1) & 1, k & 1) for k in range(1, N_DEV)]
        peers = [(x ^ fx, y ^ fy, c ^ fc) for fx, fy, fc in flips]

        def small(k, block_id, to):
            return [pltpu.make_async_remote_copy(
                src_ref=s, dst_ref=g.at[block_id], send_sem=ssend.at[2 * k + n], recv_sem=srecv.at[2 * k + n],
                device_id=to, device_id_type=MESH)
                for n, (s, g) in enumerate(((wgu_ref, gwgu_ref), (conv_ref, gconv_ref)))]

        gwgu_ref[me] = wgu_ref[...]
        gconv_ref[me] = conv_ref[...]
        for k, peer in enumerate(peers):
            for cp in small(k, me, peer):
                cp.start()
        w_ref[IN_COLS:PW, :] = jnp.zeros((PW - IN_COLS, D), BF16)
        _gather_finish(*args)
        for k, (px, py, pc) in enumerate(peers):
            for cp in small(k, 4 * px + 2 * py + pc, (px, py, pc)):
                cp.wait_recv()
                cp.wait_send()
        for j, lo, hi, d in _in_segments():
            w_ref[d:d + hi - lo, :] = buf[j, lo:hi, :]

    vm = pl.BlockSpec(memory_space=pltpu.VMEM)
    return pl.pallas_call(
        body,
        name="gather_w_in",
        in_specs=[vm] * 7,
        out_specs=[vm] * 4,
        out_shape=[jax.ShapeDtypeStruct((PW, D), BF16),
                   jax.ShapeDtypeStruct((N_DEV,) + wgu_s.shape, F32),
                   jax.ShapeDtypeStruct((N_DEV,) + conv_s.shape, F32),
                   jax.ShapeDtypeStruct((SLAB_ROWS, D), BF16)],
        scratch_shapes=[pltpu.VMEM((N_DEV, IN_ROWS, D), BF16)] + _gather_sems()
        + [pltpu.SemaphoreType.DMA((14,)), pltpu.SemaphoreType.DMA((14,))],
        compiler_params=_params(),
    )(w_it, w_gt, w_ut, w_d, w_o, wgu_s, conv_s)


def _w_in_core_reduce(dw_t):
    def body(d_ref, own_ref, sib_ref, pb_ref, g, gb, r1, send_sems, recv_sems):
        x, y, c = _position()
        chip = 2 * x + y
        for j in range(N_DEV):
            g[j, IN_W:IN_ROWS, :] = jnp.zeros((IN_ROWS - IN_W, D), F32)
        for j, lo, hi, d in _in_segments():
            g[j, lo:hi, :] = d_ref[d:d + hi - lo, :]
        for j in range(N_DEV):
            gb[j] = g[j].astype(BF16)
        copies = _stage1_copies(gb, r1, send_sems, recv_sems)
        for cp in copies:
            cp.start()
        own_ref[0] = g[2 * chip + c]
        for cp in copies:
            cp.wait_recv()
        sib_ref[0] = r1[chip]
        for k in range(1, 4):
            t = chip ^ k
            pb_ref[k - 1] = (g[2 * t + c] + r1[t].astype(F32)).astype(BF16)
        for cp in copies:
            cp.wait_send()

    vm = pl.BlockSpec(memory_space=pltpu.VMEM)
    return pl.pallas_call(
        body,
        name="w_in_core_reduce",
        in_specs=[vm],
        out_specs=[vm, vm, vm],
        out_shape=[jax.ShapeDtypeStruct((1, IN_ROWS, D), F32), jax.ShapeDtypeStruct((1, IN_ROWS, D), BF16),
                   jax.ShapeDtypeStruct((3, IN_ROWS, D), BF16)],
        scratch_shapes=[pltpu.VMEM((N_DEV, IN_ROWS, D), F32), pltpu.VMEM((N_DEV, IN_ROWS, D), BF16),
                        pltpu.VMEM((4, IN_ROWS, D), BF16), pltpu.SemaphoreType.DMA((4,)),
                        pltpu.SemaphoreType.DMA((4,))],
        compiler_params=_params(),
    )(dw_t)


def _stage1_copies(g_ref, r_ref, send_sems, recv_sems):
    x, y, c = _position()
    return [pltpu.make_async_remote_copy(
        src_ref=g_ref.at[2 * i + 1 - c], dst_ref=r_ref.at[i], send_sem=send_sems.at[i], recv_sem=recv_sems.at[i],
        device_id=(x, y, 1 - c), device_id_type=MESH) for i in range(4)]


def _ffn_core_reduce(dw3, dwb3, dw_o, dwb_o, pos_arr, dx1b, gwb):
    def body(pos_ref, g0, g1, g2, go, gb3_hbm, gbo_hbm, dx1b_ref, gwb_hbm, p0, p1, p2, po, s0, s1, s2, so, dmix_ref,
             r1f, r1o, wo, send_sems, recv_sems, wsem):
        step = pl.program_id(0)
        k = jnp.minimum(step, 2)
        x, y, c = _position()
        chip = 2 * x + y

        def copies(p):
            src = 2 * (chip ^ ((p + 1) & 3)) + 1 - c
            pairs = [(gb3_hbm.at[a, src], r1f.at[a, p]) for a in range(3)] + [(gbo_hbm.at[src], r1o.at[p])]
            return [pltpu.make_async_remote_copy(
                src_ref=s, dst_ref=d, send_sem=send_sems.at[4 * p + a], recv_sem=recv_sems.at[4 * p + a],
                device_id=(x, y, 1 - c), device_id_type=MESH) for a, (s, d) in enumerate(pairs)]

        @pl.when(step == 0)
        def _():
            for p in range(4):
                for cp in copies(p):
                    cp.start()
            loads = [pltpu.make_async_copy(gwb_hbm.at[j, pl.ds(FF_W, OUT_ROWS), :],
                                           wo.at[pl.ds(OUT_ROWS * j, OUT_ROWS), :], wsem.at[j]) for j in range(N_DEV)]
            for cp in loads:
                cp.start()
            for cp in loads:
                cp.wait()

        dmix_ref[...] = _nt(dx1b_ref[...], wo[...])

        for p in range(3):
            @pl.when(step == p)
            def _():
                for cp in copies(p):
                    cp.wait_recv()

        for a, (g, pb) in enumerate(((g0, p0), (g1, p1), (g2, p2))):
            pb[...] = (g[...] + r1f[a, k][None].astype(F32)).astype(BF16)
        po[...] = (go[...] + r1o[k][None].astype(F32)).astype(BF16)

        @pl.when(step == 3)
        def _():
            for cp in copies(3):
                cp.wait_recv()
            for a, s in enumerate((s0, s1, s2)):
                s[0] = r1f[a, 3]
            so[0] = r1o[3]
            for p in range(4):
                for cp in copies(p):
                    cp.wait_send()

    t = dx1b.shape[0]
    other = lambda s, pos: 2 * (pos[1] ^ (jnp.minimum(s, 2) + 1)) + pos[0]
    g_spec = lambda lead: pl.BlockSpec((None, 1, FF_W, D), lambda s, pos: (lead, other(s, pos), 0, 0))
    slot = lambda rows: pl.BlockSpec((1, rows, D), lambda s, pos: (jnp.minimum(s, 2), 0, 0))
    one = lambda rows: pl.BlockSpec((1, rows, D), lambda s, pos: (0, 0, 0))
    quarter = pl.BlockSpec((t // 4, D), lambda s, pos: (s, 0))
    hbm = pl.BlockSpec(memory_space=pl.ANY)
    return pl.pallas_call(
        body,
        name="ffn_core_reduce",
        grid_spec=pltpu.PrefetchScalarGridSpec(
            num_scalar_prefetch=1, grid=(4,),
            in_specs=[g_spec(0), g_spec(1), g_spec(2),
                      pl.BlockSpec((1, OUT_ROWS, D), lambda s, pos: (other(s, pos), 0, 0)), hbm, hbm, quarter, hbm],
            out_specs=[slot(FF_W), slot(FF_W), slot(FF_W), slot(OUT_ROWS),
                       one(FF_W), one(FF_W), one(FF_W), one(OUT_ROWS), quarter],
            scratch_shapes=[pltpu.VMEM((3, 4, FF_W, D), BF16), pltpu.VMEM((4, OUT_ROWS, D), BF16),
                            pltpu.VMEM((D, D), BF16), pltpu.SemaphoreType.DMA((16,)),
                            pltpu.SemaphoreType.DMA((16,)), pltpu.SemaphoreType.DMA((N_DEV,))]),
        out_shape=[jax.ShapeDtypeStruct((3, FF_W, D), BF16)] * 3 + [jax.ShapeDtypeStruct((3, OUT_ROWS, D), BF16)]
        + [jax.ShapeDtypeStruct((1, FF_W, D), BF16)] * 3 + [jax.ShapeDtypeStruct((1, OUT_ROWS, D), BF16),
                                                             jax.ShapeDtypeStruct((t, D), F32)],
        compiler_params=_params(("arbitrary",)),
    )(pos_arr, dw3, dw3, dw3, dw_o, dwb3, dwb_o, dx1b, gwb)


def _stage2_copies(p_refs, r_refs, send_sems, recv_sems):
    x, y, c = _position()
    copies = []
    for a in range(len(p_refs)):
        for k in range(1, 4):
            copies.append(pltpu.make_async_remote_copy(
                src_ref=p_refs[a].at[k - 1], dst_ref=r_refs[a].at[k - 1],
                send_sem=send_sems.at[3 * a + k - 1], recv_sem=recv_sems.at[3 * a + k - 1],
                device_id=(x ^ (k >> 1), y ^ (k & 1), c), device_id_type=MESH))
    return copies


def _finish_weights(items, pos_arr, name, nblk):
    n = len(items)
    in_specs, out_specs, out_shape, operands, wbs = [], [], [], [], []
    for g8, lead, r1, r2, w, m, v in items:
        rows, wr = g8.shape[-2], w.shape[0]
        assert rows % nblk == 0 and wr % nblk == 0 and (nblk == 1 or (rows == wr and rows % (16 * nblk) == 0))
        rb, wb = rows // nblk, wr // nblk
        if lead is not None:
            g_spec = pl.BlockSpec((None, 1, rb, D), lambda i, pos, lead=lead: (lead, 2 * pos[1] + pos[0], i, 0))
        elif g8.shape[0] == 1:
            g_spec = pl.BlockSpec((1, rb, D), lambda i, pos: (0, i, 0))
        else:
            g_spec = pl.BlockSpec((1, rb, D), lambda i, pos: (2 * pos[1] + pos[0], i, 0))
        r1_spec = pl.BlockSpec((1, rb, D), lambda i, pos: (0, i, 0))
        if w.ndim == 3:
            wblk = pl.BlockSpec((wb, 1, D), lambda i, pos: (i, 0, 0))
        else:
            wblk = pl.BlockSpec((wb, D), lambda i, pos: (i, 0))
        in_specs += [g_spec, r1_spec, pl.BlockSpec((3, rb, D), lambda i, pos: (0, i, 0)), wblk, wblk, wblk]
        out_specs += [wblk] * 4
        out_shape += [jax.ShapeDtypeStruct(w.shape, F32)] * 4
        operands += [g8, r1, r2, w, m, v]
        wbs.append(wb)

    def body(pos_ref, *refs):
        for a in range(n):
            g_ref, r1_ref, r2_ref, w_ref, m_ref, v_ref = refs[6 * a:6 * a + 6]
            g_out, d_out, m_out, v_out = refs[6 * n + 4 * a:6 * n + 4 * a + 4]
            g = g_ref[0] + r1_ref[0].astype(F32)
            for k in range(3):
                g = g + r2_ref[k].astype(F32)
            g = g[0:wbs[a], :]
            d, mn, vn = _adamw_math(_get_rows(w_ref), g, _get_rows(m_ref), _get_rows(v_ref))
            for out, val in ((g_out, g), (d_out, d), (m_out, mn), (v_out, vn)):
                _put_rows(out, val)

    return pl.pallas_call(
        body,
        name=name,
        grid_spec=pltpu.PrefetchScalarGridSpec(
            num_scalar_prefetch=1, grid=(nblk,), in_specs=in_specs, out_specs=out_specs),
        out_shape=out_shape,
        compiler_params=_params(("arbitrary",)),
    )(pos_arr, *operands)


SMALL_NAMES = ("norm1_g", "norm2_g", "norm_f_g", "b_gate", "gla_norm_g", "w_gate_up", "conv_w")
WGU_W = NQK // N_DEV
CONV_W = CW // N_DEV


def _small_adamw(sums, ws, ms, vs):
    n = len(SMALL_NAMES)

    def body(*refs):
        acc_ref = refs[0]
        w_refs, m_refs, v_refs = refs[1:1 + n], refs[1 + n:1 + 2 * n], refs[1 + 2 * n:1 + 3 * n]
        loss_ref = refs[1 + 3 * n]
        outs = refs[2 + 3 * n:]
        x, y, c = _position()
        me = 4 * x + 2 * y + c
        acc = acc_ref[...]
        loss_ref[...] = acc[3:4, NQK + DV:NQK + DV + 1]

        def my_columns(full, width):
            r = lax.broadcasted_iota(jnp.int32, (full.shape[1], width), 0)
            col = lax.broadcasted_iota(jnp.int32, (full.shape[1], width), 1)
            sel = (r == width * me + col).astype(F32)
            return _mm(full, sel, precision=HIGHEST)

        dwgu = jnp.concatenate([acc[row:row + 1, lane:lane + NQK] for row, lane in map(_wgu_slot, range(RANK))], axis=0)
        dcw = jnp.concatenate([acc[row:row + 1, lane:lane + CW] for row, lane in CONV_SLOTS], axis=0)
        grads = [acc[0:1, :], acc[1:2, :], acc[2:3, :], acc[3:4, 0:NQK], acc[3:4, NQK:NQK + DV],
                 my_columns(dwgu, WGU_W), my_columns(dcw, CONV_W)]
        for i, g in enumerate(grads):
            d, mn, vn = _adamw_math(_get_rows(w_refs[i]), g, _get_rows(m_refs[i]), _get_rows(v_refs[i]))
            for out, val in zip(outs[4 * i:4 * i + 4], (g, d, mn, vn)):
                _put_rows(out, val)

    vm = pl.BlockSpec(memory_space=pltpu.VMEM)
    out_shape = [jax.ShapeDtypeStruct((1, 1), F32)]
    for w in ws:
        out_shape += [jax.ShapeDtypeStruct(w.shape, F32)] * 4
    return pl.pallas_call(
        body,
        name="small_adamw",
        in_specs=[vm] * (1 + 3 * n),
        out_specs=[vm] * (1 + 4 * n),
        out_shape=out_shape,
        compiler_params=_params(),
    )(sums, *ws, *ms, *vs)


def kernel(x, norm1_g, w_in, w_gate_up, b_gate, gla_norm_g, conv_w, w_out, norm2_g, w_ffn_gate, w_ffn_up, w_ffn_down, norm_f_g, loss_target, m_norm1_g, m_w_in, m_w_gate_up, m_b_gate, m_gla_norm_g, m_conv_w, m_w_out, m_norm2_g, m_w_ffn_gate, m_w_ffn_up, m_w_ffn_down, m_norm_f_g, v_norm1_g, v_w_in, v_w_gate_up, v_b_gate, v_gla_norm_g, v_conv_w, v_w_out, v_norm2_g, v_w_ffn_gate, v_w_ffn_up, v_w_ffn_down, v_norm_f_g):
    xi, yi, ci = _position()
    pos_arr = jnp.stack([ci, 2 * xi + yi]).astype(jnp.int32)
    nb, s, _ = x.shape
    t = nb * s

    tr = lambda a: a[0].T
    rows_of = lambda a: a.transpose(2, 0, 1)
    conv_rows = lambda a: a.transpose(1, 0, 2)
    w_in_t, gwgu, gconv, stage = _gather_w_in(rows_of(w_in), tr(w_ffn_gate), tr(w_ffn_up), w_ffn_down[0], w_out[0],
                                              w_gate_up[0], conv_rows(conv_w))
    wgu_f = gwgu.transpose(1, 0, 2).reshape(RANK, NQK)
    conv_f = gconv.transpose(1, 2, 0, 3).reshape(CONV_K, CW)
    wgu_p = jnp.concatenate([wgu_f, jnp.zeros((A_PAD - RANK, NQK), F32)], axis=0).astype(BF16)

    x2d = x.reshape(t, D)
    tgt2d = loss_target.reshape(t, D)
    tm = 256
    tm_in = min(512, t)
    tk = min(2048, t)
    proj, z, h, gwb = _in_proj_fwd(x2d, norm1_g, w_in_t, wgu_p, b_gate, tm_in, stage)
    proj3 = proj.reshape(nb, s, PW)
    z3 = z.reshape(nb, s, NQK)
    mix3, opre3, sprev, x1, gwa = _mix_fwd(proj3, z3, gla_norm_g, conv_f, stage, x, gwb)
    mix2d = mix3.reshape(t, D)
    dx1, dx1b, adu, hb, dg2, dgf, loss_part = _ffn_fwd_bwd(
        x1.reshape(t, D), tgt2d, gwa, gwb, norm2_g, norm_f_g.reshape(1, D), tm)
    dw3, dwb3 = _dw_ffn(adu, hb, tk)
    dw3 = dw3.reshape(3, N_DEV, FF_W, D)
    dw_o, dwb_o = _tn_matmul(mix2d, dx1b, D // 2, D, tk, "dw_out", True)
    dw_o = dw_o.reshape(N_DEV, OUT_ROWS, D)
    *pb, sib_d, sib_g, sib_u, sib_o, dmix = _ffn_core_reduce(
        dw3, dwb3.reshape(3, N_DEV, FF_W, D), dw_o, dwb_o.reshape(N_DEV, OUT_ROWS, D), pos_arr, dx1b, gwb)
    g8 = [dw3, dw3, dw3, dw_o]
    leads = [0, 1, 2, None]
    tags = ("w_ffn_down", "w_ffn_gate", "w_ffn_up", "w_out")
    r1 = [sib_d, sib_g, sib_u, sib_o]
    mb = _mix_bwd(proj3, z3, sprev, opre3, dmix.reshape(nb, s, D), gla_norm_g, conv_f, wgu_p, [pb[0], pb[1], pb[3]])
    dproj3, dgng, dcw, dbg, dwgu = mb[:5]
    dproj2d = dproj3.reshape(t, PW)
    dw_in_t, r2_up = _tn_matmul(dproj2d, h, PW, D, min(1024, t), "dw_in", False, _stage2_rider([pb[2]]))
    r2 = [mb[5], mb[6], r2_up, mb[7]]
    g_in, r1_in, pb_in = _w_in_core_reduce(dw_in_t)
    dx, small_sums, r2_in = _in_proj_bwd(dproj2d, x2d, dx1, norm1_g, w_in_t, tm_in, pb_in,
                                         (dg2, dgf, dbg, dgng, dwgu, dcw, loss_part))

    tags = ("w_in",) + tags
    g8 = [g_in] + g8
    leads = [None] + leads
    r1 = [r1_in] + list(r1)
    r2 = [r2_in] + r2
    shard_w = (rows_of(w_in), w_ffn_down[0], tr(w_ffn_gate), tr(w_ffn_up), w_out[0])
    shard_m = (rows_of(m_w_in), m_w_ffn_down[0], tr(m_w_ffn_gate), tr(m_w_ffn_up), m_w_out[0])
    shard_v = (rows_of(v_w_in), v_w_ffn_down[0], tr(v_w_ffn_gate), tr(v_w_ffn_up), v_w_out[0])
    back = (lambda o: o.transpose(1, 2, 0), lambda o: o[None], lambda o: o.T[None], lambda o: o.T[None],
            lambda o: o[None])
    items = list(zip(g8, leads, r1, r2, shard_w, shard_m, shard_v))
    flat = list(_finish_weights(items[1:], pos_arr, "finish_ffn_out", 2))
    flat = list(_finish_weights(items[:1], pos_arr, "finish_w_in", 1)) + flat
    results = {}
    for i, (tag, to_shard) in enumerate(zip(tags, back)):
        results[tag] = [to_shard(o) for o in flat[4 * i:4 * i + 4]]

    small_w = (norm1_g, norm2_g, norm_f_g.reshape(1, D), b_gate, gla_norm_g, w_gate_up[0], conv_rows(conv_w))
    small_m = (m_norm1_g, m_norm2_g, m_norm_f_g.reshape(1, D), m_b_gate, m_gla_norm_g, m_w_gate_up[0],
               conv_rows(m_conv_w))
    small_v = (v_norm1_g, v_norm2_g, v_norm_f_g.reshape(1, D), v_b_gate, v_gla_norm_g, v_w_gate_up[0],
               conv_rows(v_conv_w))
    so = _small_adamw(small_sums, small_w, small_m, small_v)
    loss = so[0].reshape(())
    to_shape = {"norm_f_g": lambda o: o.reshape(D), "w_gate_up": lambda o: o[None],
                "conv_w": lambda o: o.transpose(1, 0, 2)}
    for i, name in enumerate(SMALL_NAMES):
        results[name] = [to_shape.get(name, lambda o: o)(o) for o in so[1 + 4 * i:5 + 4 * i]]

    names = ("norm1_g", "w_in", "w_gate_up", "b_gate", "gla_norm_g", "conv_w", "w_out", "norm2_g",
             "w_ffn_gate", "w_ffn_up", "w_ffn_down", "norm_f_g")
    outs = [loss, dx.reshape(nb, s, D)]
    for kind in range(4):
        for name in names:
            outs.append(results[name][kind])
    return tuple(outs)
```

```python
import jax
import jax.numpy as jnp
from jax import lax
from jax.experimental import pallas as pl
from jax.experimental.pallas import tpu as pltpu

F32 = jnp.float32
BF16 = jnp.bfloat16
HIGHEST = lax.Precision.HIGHEST
MESH = pl.DeviceIdType.MESH

N_DEV = 8
D = 1024
DFF = 2816
HEADS = 4
DK = 64
DV = 128
NQK = HEADS * DK
NV = HEADS * DV
RANK = 16
CHUNK = 64
CW = 512
CONV_K = 3
IN_COLS = 3088
EPS = 1e-6
INV_GATE_NORM = 1.0 / 16.0
Q_SCALE = DK ** -0.5

PW = 3200
OQ, OK_, OV, OG, OCB, OCC, OCH, OA = 0, 256, 512, 1024, 1536, 2048, 2560, 3072
A_PAD = 128

ADAM_LR = 0.001
ADAM_B1 = 0.9
ADAM_B2 = 0.999
ADAM_EPS = 1e-08
ADAM_WD = 0.01
ADAM_STEP = 10

IN_W = IN_COLS // N_DEV
IN_ROWS = 400
FF_W = DFF // N_DEV
OUT_ROWS = D // N_DEV
SLAB_IN = 0
SLAB_G = SLAB_IN + IN_ROWS
SLAB_U = SLAB_G + FF_W
SLAB_D = SLAB_U + FF_W
SLAB_O = SLAB_D + FF_W
SLAB_ROWS = SLAB_O + OUT_ROWS

VMEM_LIMIT = 56 * 1024 * 1024


def _params(sem=None, vmem=VMEM_LIMIT):
    return pltpu.CompilerParams(dimension_semantics=sem, vmem_limit_bytes=vmem)


def _nt(a, b):
    return lax.dot_general(a, b, (((1,), (1,)), ((), ())), preferred_element_type=F32)


def _tn(a, b, precision=None):
    return lax.dot_general(a, b, (((0,), (0,)), ((), ())), preferred_element_type=F32, precision=precision)


def _mm(a, b, precision=None):
    return jnp.dot(a, b, preferred_element_type=F32, precision=precision)


def _in_segments():
    segs = []
    for j in range(N_DEV):
        lo, hi = IN_W * j, IN_W * (j + 1)
        cuts = sorted({lo, hi} | {c for c in (OCB, OCB + RANK) if lo < c < hi})
        for a, b in zip(cuts[:-1], cuts[1:]):
            if a < OCB:
                d = a
            elif a < OCB + RANK:
                d = OA + (a - OCB)
            else:
                d = a - RANK
            segs.append((j, a - lo, b - lo, d))
    return segs


def _in_proj_fwd(x2d, g1, w_in_t, wgu_p, b_gate, tm, stage):
    t = x2d.shape[0]
    nt = t // tm
    g_rows = SLAB_ROWS - SLAB_D

    def body(x_ref, g_ref, w_ref, wgu_ref, bg_ref, stage_hbm, proj_ref, z_ref, h_ref, gwb_ref,
             send_sems, recv_sems, local_sem):
        gargs = (stage_hbm, SLAB_D, g_rows, gwb_ref, send_sems, recv_sems, local_sem)

        @pl.when(pl.program_id(0) == 0)
        def _():
            _gather_start(*gargs)

        x = x_ref[...]
        r = lax.rsqrt(jnp.mean(x * x, axis=-1, keepdims=True) + EPS)
        h = ((x * r) * g_ref[...]).astype(BF16)
        h_ref[...] = h
        proj = _nt(h, w_ref[...])
        proj_ref[...] = proj
        pa = proj[:, OA:OA + A_PAD].astype(BF16)
        z_ref[...] = _mm(pa, wgu_ref[...]) + bg_ref[...]

        @pl.when(pl.program_id(0) == nt - 1)
        def _():
            _gather_finish(*gargs)

    return pl.pallas_call(
        body,
        name="in_proj_fwd",
        grid=(t // tm,),
        in_specs=[
            pl.BlockSpec((tm, D), lambda i: (i, 0)),
            pl.BlockSpec((1, D), lambda i: (0, 0)),
            pl.BlockSpec((PW, D), lambda i: (0, 0)),
            pl.BlockSpec((A_PAD, NQK), lambda i: (0, 0)),
            pl.BlockSpec((1, NQK), lambda i: (0, 0)),
            pl.BlockSpec(memory_space=pl.ANY),
        ],
        out_specs=[
            pl.BlockSpec((tm, PW), lambda i: (i, 0)),
            pl.BlockSpec((tm, NQK), lambda i: (i, 0)),
            pl.BlockSpec((tm, D), lambda i: (i, 0)),
            pl.BlockSpec(memory_space=pl.ANY),
        ],
        out_shape=[
            jax.ShapeDtypeStruct((t, PW), F32),
            jax.ShapeDtypeStruct((t, NQK), F32),
            jax.ShapeDtypeStruct((t, D), BF16),
            jax.ShapeDtypeStruct((N_DEV, g_rows, D), BF16),
        ],
        scratch_shapes=_gather_sems(),
        compiler_params=_params(("arbitrary",)),
    )(x2d, g1, w_in_t, wgu_p, b_gate, stage)


def _head_masks():
    lane = lax.broadcasted_iota(jnp.int32, (1, NQK), 1)
    return [(lane >= DK * h) & (lane < DK * (h + 1)) for h in range(HEADS)]


def _split_bf16(x, n):
    parts = []
    for _ in range(n):
        p = x.astype(BF16)
        parts.append(p)
        x = x - p.astype(F32)
    return parts


def _chunk_fwd_parts(q, k, z, tril16):
    la = (jnp.minimum(z, 0.0) - jnp.log1p(jnp.exp(-jnp.abs(z)))) * INV_GATE_NORM
    la_parts = _split_bf16(la, 3)
    bc = _mm(tril16, la_parts[0]) + _mm(tril16, la_parts[1]) + _mm(tril16, la_parts[2])
    bl = bc[CHUNK - 1:CHUNK, :]
    eb = jnp.exp(bc)
    enb = jnp.exp(-bc)
    ekl = jnp.exp(bl - bc)
    qi = (q * Q_SCALE) * eb
    ki = k * enb
    ks = k * ekl
    ones16 = jnp.ones((CHUNK, DV), BF16)
    decb = jnp.exp(_tn(la_parts[0], ones16) + _tn(la_parts[1], ones16) + _tn(la_parts[2], ones16))
    return la, eb, enb, ekl, qi, ki, ks, decb


def _stack_heads(a, masks):
    return jnp.concatenate([jnp.where(m, a, 0.0) for m in masks], axis=0)


def _merge_heads(blocks, masks):
    out = blocks[HEADS - 1]
    for h in range(HEADS - 2, -1, -1):
        out = jnp.where(masks[h], blocks[h], out)
    return out


def _causal_stack_mask():
    row = lax.broadcasted_iota(jnp.int32, (HEADS * CHUNK, CHUNK), 0)
    col = lax.broadcasted_iota(jnp.int32, (HEADS * CHUNK, CHUNK), 1)
    return (row & (CHUNK - 1)) >= col


def _conv_taps(u, uprev):
    row = lax.broadcasted_iota(jnp.int32, u.shape, 0)
    u1 = jnp.where(row < 1, pltpu.roll(uprev, 1, 0), pltpu.roll(u, 1, 0))
    u2 = jnp.where(row < 2, pltpu.roll(uprev, 2, 0), pltpu.roll(u, 2, 0))
    return u1, u2


def _mix_fwd(proj3, z3, gng, conv_w, stage, x3, gwb):
    nb, s, _ = proj3.shape
    nc = s // CHUNK
    g_rows = SLAB_D - SLAB_G

    def body(p_ref, z_ref, gng_ref, cw_ref, stage_hbm, x_ref, gwb_hbm, mix_ref, o_ref, sprev_ref, x1_ref, gwa_ref,
             s_ref, uprev_ref, wo, wsem, send_sems, recv_sems, local_sem):
        n = pl.program_id(0)
        gargs = (stage_hbm, SLAB_G, g_rows, gwa_ref, send_sems, recv_sems, local_sem)

        @pl.when(n == 0)
        def _():
            _gather_start(*gargs)
            loads = [pltpu.make_async_copy(gwb_hbm.at[j, pl.ds(FF_W, OUT_ROWS), :],
                                           wo.at[pl.ds(OUT_ROWS * j, OUT_ROWS), :], wsem.at[j]) for j in range(N_DEV)]
            for cp in loads:
                cp.start()
            s_ref[...] = jnp.zeros_like(s_ref)
            uprev_ref[...] = jnp.zeros_like(uprev_ref)
            for cp in loads:
                cp.wait()

        r_i = lax.broadcasted_iota(jnp.int32, (CHUNK, CHUNK), 0)
        c_i = lax.broadcasted_iota(jnp.int32, (CHUNK, CHUNK), 1)
        tril16 = (r_i >= c_i).astype(BF16)
        masks = _head_masks()
        cmask = _causal_stack_mask()
        gg = gng_ref[...]
        for b in range(nb):
            q = p_ref[b, :, OQ:OQ + NQK]
            k = p_ref[b, :, OK_:OK_ + NQK]
            _, _, _, _, qi, ki, ks, decb = _chunk_fwd_parts(q, k, z_ref[b], tril16)
            qs = _stack_heads(qi, masks).astype(BF16)
            sc = jnp.where(cmask, _nt(qs, ki.astype(BF16)), 0.0).astype(BF16)
            st = s_ref[b]
            sprev_ref[b, 0] = st
            o_inter = _mm(qs, st.astype(BF16))
            v16 = p_ref[b, :, OV:OV + NV].astype(BF16)
            kv = _tn(ks.astype(BF16), v16)
            for h in range(HEADS):
                rows = slice(CHUNK * h, CHUNK * (h + 1))
                cols = slice(DV * h, DV * (h + 1))
                o = _mm(sc[rows], v16[:, cols]) + o_inter[rows]
                o_ref[b, :, cols] = o
                r = lax.rsqrt(jnp.mean(o * o, axis=-1, keepdims=True) + EPS)
                on = (o * r) * gg
                g = p_ref[b, :, OG + DV * h:OG + DV * (h + 1)]
                mix_ref[b, :, cols] = (on * (g * jax.nn.sigmoid(g))).astype(BF16)
                s_ref[b, rows, :] = decb[rows] * st[rows] + kv[rows, cols]
            u = p_ref[b, :, OCC:OCC + CW] * p_ref[b, :, OCH:OCH + CW]
            u1, u2 = _conv_taps(u, uprev_ref[b])
            yc = cw_ref[0:1, :] * u2 + cw_ref[1:2, :] * u1 + cw_ref[2:3, :] * u
            mix_ref[b, :, NV:NV + CW] = (p_ref[b, :, OCB:OCB + CW] * yc).astype(BF16)
            uprev_ref[b] = u
        mixed = _mm(jnp.concatenate([mix_ref[b] for b in range(nb)], axis=0), wo[...])
        for b in range(nb):
            x1_ref[b] = x_ref[b] + mixed[CHUNK * b:CHUNK * (b + 1)]

        @pl.when(n == nc - 1)
        def _():
            _gather_finish(*gargs)

    return pl.pallas_call(
        body,
        name="mix_fwd",
        grid=(nc,),
        in_specs=[
            pl.BlockSpec((nb, CHUNK, PW), lambda n: (0, n, 0)),
            pl.BlockSpec((nb, CHUNK, NQK), lambda n: (0, n, 0)),
            pl.BlockSpec((1, DV), lambda n: (0, 0)),
            pl.BlockSpec((CONV_K, CW), lambda n: (0, 0)),
            pl.BlockSpec(memory_space=pl.ANY),
            pl.BlockSpec((nb, CHUNK, D), lambda n: (0, n, 0)),
            pl.BlockSpec(memory_space=pl.ANY),
        ],
        out_specs=[
            pl.BlockSpec((nb, CHUNK, D), lambda n: (0, n, 0)),
            pl.BlockSpec((nb, CHUNK, NV), lambda n: (0, n, 0)),
            pl.BlockSpec((nb, 1, NQK, DV), lambda n: (0, n, 0, 0)),
            pl.BlockSpec((nb, CHUNK, D), lambda n: (0, n, 0)),
            pl.BlockSpec(memory_space=pl.ANY),
        ],
        out_shape=[
            jax.ShapeDtypeStruct((nb, s, D), BF16),
            jax.ShapeDtypeStruct((nb, s, NV), F32),
            jax.ShapeDtypeStruct((nb, nc, NQK, DV), F32),
            jax.ShapeDtypeStruct((nb, s, D), F32),
            jax.ShapeDtypeStruct((N_DEV, g_rows, D), BF16),
        ],
        scratch_shapes=[pltpu.VMEM((nb, NQK, DV), F32), pltpu.VMEM((nb, CHUNK, CW), F32),
                        pltpu.VMEM((D, D), BF16), pltpu.SemaphoreType.DMA((N_DEV,))] + _gather_sems(),
        compiler_params=_params(("arbitrary",)),
    )(proj3, z3, gng, conv_w, stage, x3, gwb)


def _ffn_fwd_bwd(x1_2d, tgt2d, gwa, gwb, g2, gf, tm):
    t = x1_2d.shape[0]

    def body(x1_ref, tgt_ref, g2_ref, gf_ref, gwa_hbm, gwb_hbm,
             dx1_ref, dx1b_ref, adu_ref, hb_ref, dg2_ref, dgf_ref, loss_ref,
             wg, wu, wd, wsem):
        i = pl.program_id(0)

        def weight_copies(n, dst, src, off, rows):
            return [pltpu.make_async_copy(src.at[j, pl.ds(off, rows), :], dst.at[pl.ds(rows * j, rows), :],
                                          wsem.at[N_DEV * n + j]) for j in range(N_DEV)]

        loads = (weight_copies(0, wg, gwa_hbm, 0, FF_W), weight_copies(1, wu, gwa_hbm, FF_W, FF_W),
                 weight_copies(2, wd, gwb_hbm, 0, FF_W))

        @pl.when(i == 0)
        def _():
            for group in loads:
                for cp in group:
                    cp.start()
            dg2_ref[...] = jnp.zeros_like(dg2_ref)
            dgf_ref[...] = jnp.zeros_like(dgf_ref)
            loss_ref[...] = jnp.zeros_like(loss_ref)
            for group in loads:
                for cp in group:
                    cp.wait()

        g2v = g2_ref[...]
        gfv = gf_ref[...]
        x1 = x1_ref[...]
        r2 = lax.rsqrt(jnp.mean(x1 * x1, axis=-1, keepdims=True) + EPS)
        n2 = x1 * r2
        h2 = (n2 * g2v).astype(BF16)
        hb_ref[1] = h2
        gate = _nt(h2, wg[...])
        up = _nt(h2, wu[...])
        sg = jax.nn.sigmoid(gate)
        sil = gate * sg
        act = (sil * up).astype(BF16)
        adu_ref[0] = act
        x2 = x1 + _mm(act, wd[...])
        rf = lax.rsqrt(jnp.mean(x2 * x2, axis=-1, keepdims=True) + EPS)
        nf = x2 * rf
        err = nf * gfv - tgt_ref[...]
        loss_ref[...] += 0.5 * jnp.sum(jnp.mean(err * err, axis=-1, keepdims=True))
        dy = err * (1.0 / D)
        dgf_ref[...] += jnp.sum(dy * nf, axis=0, keepdims=True)
        dnf = dy * gfv
        dx2 = rf * (dnf - nf * jnp.mean(dnf * nf, axis=-1, keepdims=True))
        dx2b = dx2.astype(BF16)
        hb_ref[0] = dx2b
        dact = _nt(dx2b, wd[...])
        dup = (dact * sil).astype(BF16)
        dgate = ((dact * up) * (sg * (1.0 + gate * (1.0 - sg)))).astype(BF16)
        adu_ref[2] = dup
        adu_ref[1] = dgate
        dh2 = _mm(dgate, wg[...]) + _mm(dup, wu[...])
        dg2_ref[...] += jnp.sum(dh2 * n2, axis=0, keepdims=True)
        dn2 = dh2 * g2v
        dx1 = dx2 + r2 * (dn2 - n2 * jnp.mean(dn2 * n2, axis=-1, keepdims=True))
        dx1_ref[...] = dx1
        dx1b_ref[...] = dx1.astype(BF16)

    tile = lambda w: pl.BlockSpec((tm, w), lambda i: (i, 0))
    vec = pl.BlockSpec((1, D), lambda i: (0, 0))
    hbm = pl.BlockSpec(memory_space=pl.ANY)
    return pl.pallas_call(
        body,
        name="ffn_fwd_bwd",
        grid=(t // tm,),
        in_specs=[tile(D), tile(D), vec, vec, hbm, hbm],
        out_specs=[tile(D), tile(D), pl.BlockSpec((3, tm, DFF), lambda i: (0, i, 0)),
                   pl.BlockSpec((2, tm, D), lambda i: (0, i, 0)), vec, vec,
                   pl.BlockSpec((1, 128), lambda i: (0, 0))],
        out_shape=[
            jax.ShapeDtypeStruct((t, D), F32),
            jax.ShapeDtypeStruct((t, D), BF16),
            jax.ShapeDtypeStruct((3, t, DFF), BF16),
            jax.ShapeDtypeStruct((2, t, D), BF16),
            jax.ShapeDtypeStruct((1, D), F32),
            jax.ShapeDtypeStruct((1, D), F32),
            jax.ShapeDtypeStruct((1, 128), F32),
        ],
        scratch_shapes=[pltpu.VMEM((DFF, D), BF16), pltpu.VMEM((DFF, D), BF16), pltpu.VMEM((DFF, D), BF16),
                        pltpu.SemaphoreType.DMA((3 * N_DEV,))],
        compiler_params=_params(("arbitrary",)),
    )(x1_2d, tgt2d, g2, gf, gwa, gwb)


def _stage2_rider(pbs):
    return dict(inputs=list(pbs), out_shape=[jax.ShapeDtypeStruct(p.shape, BF16) for p in pbs], nsem=3 * len(pbs),
                copies=_stage2_copies)


def _tn_matmul(a, b, bm, bn, tk, name, with_bf16, rider=None):
    t, m = a.shape
    n = b.shape[1]
    nk = t // tk
    nout = 2 if with_bf16 else 1
    grid = (m // bm, n // bn, nk)
    r_in = [] if rider is None else rider["inputs"]
    r_out = [] if rider is None else rider["out_shape"]

    def body(a_ref, b_ref, *rest):
        ins, outs = rest[:len(r_in)], rest[len(r_in):len(r_in) + nout]
        r_outs, sems = rest[len(r_in) + nout:len(r_in) + nout + len(r_out)], rest[len(r_in) + nout + len(r_out):]
        o_ref = outs[0]
        i, j, k = pl.program_id(0), pl.program_id(1), pl.program_id(2)
        if rider is not None:
            @pl.when((i == 0) & (j == 0) & (k == 0))
            def _():
                for cp in rider["copies"](ins, r_outs, *sems):
                    cp.start()

        @pl.when(k == 0)
        def _():
            o_ref[...] = jnp.zeros_like(o_ref)

        o_ref[...] += _tn(a_ref[...].astype(BF16), b_ref[...].astype(BF16))
        if with_bf16:
            @pl.when(k == nk - 1)
            def _():
                outs[1][...] = o_ref[...].astype(BF16)
        if rider is not None:
            @pl.when((i == grid[0] - 1) & (j == grid[1] - 1) & (k == nk - 1))
            def _():
                copies = rider["copies"](ins, r_outs, *sems)
                for cp in copies:
                    cp.wait_recv()
                for cp in copies:
                    cp.wait_send()

    out_blk = pl.BlockSpec((bm, bn), lambda i, j, k: (i, j))
    hbm = pl.BlockSpec(memory_space=pl.ANY)
    out_shape = [jax.ShapeDtypeStruct((m, n), F32)] + ([jax.ShapeDtypeStruct((m, n), BF16)] if with_bf16 else [])
    res = pl.pallas_call(
        body,
        name=name,
        grid=grid,
        in_specs=[pl.BlockSpec((tk, bm), lambda i, j, k: (k, i)), pl.BlockSpec((tk, bn), lambda i, j, k: (k, j))]
        + [hbm] * len(r_in),
        out_specs=[out_blk] * nout + [hbm] * len(r_out),
        out_shape=out_shape + list(r_out),
        scratch_shapes=([] if rider is None else
                        [pltpu.SemaphoreType.DMA((rider["nsem"],)), pltpu.SemaphoreType.DMA((rider["nsem"],))]),
        compiler_params=_params(("parallel", "parallel", "arbitrary") if rider is None
                                else ("arbitrary", "arbitrary", "arbitrary")),
    )(a, b, *r_in)
    return res[0] if len(res) == 1 else res


def _dw_ffn(adu, hb, tk):
    _, t, _ = adu.shape
    bm = DFF // 2
    nk = t // tk

    def body(a_ref, b_ref, o_ref, ob_ref):
        k = pl.program_id(2)

        @pl.when(k == 0)
        def _():
            o_ref[...] = jnp.zeros_like(o_ref)

        o_ref[...] += _tn(a_ref[...], b_ref[...])

        @pl.when(k == nk - 1)
        def _():
            ob_ref[...] = o_ref[...].astype(BF16)

    out_blk = pl.BlockSpec((None, bm, D), lambda p, i, k: (p, i, 0))
    return pl.pallas_call(
        body,
        name="dw_ffn",
        grid=(3, DFF // bm, nk),
        in_specs=[pl.BlockSpec((None, tk, bm), lambda p, i, k: (p, k, i)),
                  pl.BlockSpec((None, tk, D), lambda p, i, k: (jnp.minimum(p, 1), k, 0))],
        out_specs=[out_blk, out_blk],
        out_shape=[jax.ShapeDtypeStruct((3, DFF, D), F32), jax.ShapeDtypeStruct((3, DFF, D), BF16)],
        compiler_params=_params(("arbitrary", "arbitrary", "arbitrary")),
    )(adu, hb)


def _mix_bwd(proj3, z3, sprev, opre3, dmix3, gng, conv_w, wgu_p, pbs):
    nb, s, _ = proj3.shape
    nc = s // CHUNK
    na = len(pbs)

    def body(*refs):
        (p_ref, pprev_ref, z_ref, sp_ref, o_ref, dm_ref, gng_ref, cw_ref, wgu_ref) = refs[:9]
        pb_refs = refs[9:9 + na]
        (dproj_ref, dgng_ref, dcw_ref, dbg_ref, dwgu_ref) = refs[9 + na:14 + na]
        r2_refs = refs[14 + na:14 + 2 * na]
        ds_ref, dycn_ref, send_sems, recv_sems = refs[14 + 2 * na:]
        step = pl.program_id(0)
        n = nc - 1 - step

        @pl.when(step == 0)
        def _():
            for cp in _stage2_copies(pb_refs, r2_refs, send_sems, recv_sems):
                cp.start()
            ds_ref[...] = jnp.zeros_like(ds_ref)
            dycn_ref[...] = jnp.zeros_like(dycn_ref)
            dgng_ref[...] = jnp.zeros_like(dgng_ref)
            dcw_ref[...] = jnp.zeros_like(dcw_ref)
            dbg_ref[...] = jnp.zeros_like(dbg_ref)
            dwgu_ref[...] = jnp.zeros_like(dwgu_ref)

        r_i = lax.broadcasted_iota(jnp.int32, (CHUNK, CHUNK), 0)
        c_i = lax.broadcasted_iota(jnp.int32, (CHUNK, CHUNK), 1)
        tril16 = (r_i >= c_i).astype(BF16)
        triu16 = (r_i <= c_i).astype(BF16)
        causal = r_i >= c_i
        masks = _head_masks()
        cmask = _causal_stack_mask()
        gg = gng_ref[...]
        last_row = lax.broadcasted_iota(jnp.int32, (CHUNK, NQK), 0) == CHUNK - 1
        ones_r = jnp.ones((16, DV), BF16)
        has_prev = (n > 0).astype(F32)
        for b in range(nb):
            q = p_ref[b, :, OQ:OQ + NQK]
            k = p_ref[b, :, OK_:OK_ + NQK]
            z = z_ref[b]
            _, eb, enb, ekl, qi, ki, ks, decb = _chunk_fwd_parts(q, k, z, tril16)
            qi16 = qi.astype(BF16)
            ki16 = ki.astype(BF16)
            qs = _stack_heads(qi, masks).astype(BF16)
            sc = jnp.where(cmask, _nt(qs, ki16), 0.0).astype(BF16)
            st = sp_ref[b, 0]
            st16 = st.astype(BF16)
            dsn = ds_ref[b]
            dsn16 = dsn.astype(BF16)
            v16 = p_ref[b, :, OV:OV + NV].astype(BF16)
            do16 = []
            dgng = jnp.zeros((1, DV), F32)
            for h in range(HEADS):
                cols = slice(DV * h, DV * (h + 1))
                o = o_ref[b, :, cols]
                r = lax.rsqrt(jnp.mean(o * o, axis=-1, keepdims=True) + EPS)
                nh = o * r
                g = p_ref[b, :, OG + DV * h:OG + DV * (h + 1)]
                sg = jax.nn.sigmoid(g)
                dog = dm_ref[b, :, cols]
                dproj_ref[b, :, OG + DV * h:OG + DV * (h + 1)] = (
                    (dog * (nh * gg)) * (sg * (1.0 + g * (1.0 - sg)))).astype(BF16)
                don = dog * (g * sg)
                dgng = dgng + jnp.sum(don * nh, axis=0, keepdims=True)
                dn = don * gg
                do = r * (dn - nh * jnp.mean(dn * nh, axis=-1, keepdims=True))
                do16.append(do.astype(BF16))
            dgng_ref[...] += dgng
            do_rows = jnp.concatenate(do16, axis=0)
            v_rows = jnp.concatenate([v16[:, DV * h:DV * (h + 1)] for h in range(HEADS)], axis=0)
            dp16 = [jnp.where(causal, _nt(do16[h], v16[:, DV * h:DV * (h + 1)]), 0.0).astype(BF16)
                    for h in range(HEADS)]
            ks_dsn = _mm(_stack_heads(ks, masks).astype(BF16), dsn16)
            do_st = _nt(do_rows, st16)
            v_dsn = _nt(v_rows, dsn16)
            dp_ki = _mm(jnp.concatenate(dp16, axis=0), ki16)
            q_do = _tn(qi16, jnp.concatenate(do16, axis=1))
            dki_h = []
            for h in range(HEADS):
                rows = slice(CHUNK * h, CHUNK * (h + 1))
                cols = slice(DV * h, DV * (h + 1))
                dv = _tn(sc[rows], do16[h]) + ks_dsn[rows]
                dproj_ref[b, :, OV + DV * h:OV + DV * (h + 1)] = dv.astype(BF16)
                dki_h.append(_tn(dp16[h], qi16))
                ds_ref[b, rows, :] = decb[rows] * dsn[rows] + q_do[rows, cols]
            blocks = lambda a: [a[CHUNK * h:CHUNK * (h + 1)] for h in range(HEADS)]
            dqi = _merge_heads(blocks(dp_ki + do_st), masks)
            dki = _merge_heads(dki_h, masks)
            dks = _merge_heads(blocks(v_dsn), masks)
            dproj_ref[b, :, OQ:OQ + NQK] = (dqi * (Q_SCALE * eb)).astype(BF16)
            dproj_ref[b, :, OK_:OK_ + NQK] = (dki * enb + dks * ekl).astype(BF16)
            dks_ks = dks * ks
            db = dqi * qi - dki * ki - dks_ks
            sd = _split_bf16(dsn * st * decb, 2)
            dbl = jnp.sum(dks_ks, axis=0, keepdims=True) + (_nt(ones_r, sd[0]) + _nt(ones_r, sd[1]))[0:1, :]
            db = db + jnp.where(last_row, dbl, 0.0)
            db_parts = _split_bf16(db, 3)
            dla = _mm(triu16, db_parts[0]) + _mm(triu16, db_parts[1]) + _mm(triu16, db_parts[2])
            dz = (dla * INV_GATE_NORM) * (1.0 / (1.0 + jnp.exp(z)))
            dbg_ref[...] += jnp.sum(dz, axis=0, keepdims=True)
            dz16 = dz.astype(BF16)
            pa16 = p_ref[b, :, OA:OA + A_PAD].astype(BF16)
            dwgu_ref[...] += _tn(pa16, dz16)
            dproj_ref[b, :, OA:OA + A_PAD] = _nt(dz16, wgu_ref[...]).astype(BF16)
            cb = p_ref[b, :, OCB:OCB + CW]
            cc = p_ref[b, :, OCC:OCC + CW]
            ch = p_ref[b, :, OCH:OCH + CW]
            u = cc * ch
            uprev = (pprev_ref[b, :, 0:CW] * pprev_ref[b, :, CW:2 * CW]) * has_prev
            u1, u2 = _conv_taps(u, uprev)
            w0 = cw_ref[0:1, :]
            w1 = cw_ref[1:2, :]
            w2 = cw_ref[2:3, :]
            yc = w0 * u2 + w1 * u1 + w2 * u
            doc = dm_ref[b, :, NV:NV + CW]
            dproj_ref[b, :, OCB:OCB + CW] = (doc * yc).astype(BF16)
            dyc = doc * cb
            dycn = dycn_ref[b]
            row = lax.broadcasted_iota(jnp.int32, dyc.shape, 0)
            d1 = jnp.where(row >= CHUNK - 1, pltpu.roll(dycn, CHUNK - 1, 0), pltpu.roll(dyc, CHUNK - 1, 0))
            d2 = jnp.where(row >= CHUNK - 2, pltpu.roll(dycn, CHUNK - 2, 0), pltpu.roll(dyc, CHUNK - 2, 0))
            du = w2 * dyc + w1 * d1 + w0 * d2
            dproj_ref[b, :, OCC:OCC + CW] = (du * ch).astype(BF16)
            dproj_ref[b, :, OCH:OCH + CW] = (du * cc).astype(BF16)
            dcw_ref[0:1, :] += jnp.sum(dyc * u2, axis=0, keepdims=True)
            dcw_ref[1:2, :] += jnp.sum(dyc * u1, axis=0, keepdims=True)
            dcw_ref[2:3, :] += jnp.sum(dyc * u, axis=0, keepdims=True)
            dycn_ref[b] = dyc

        @pl.when(step == nc - 1)
        def _():
            copies = _stage2_copies(pb_refs, r2_refs, send_sems, recv_sems)
            for cp in copies:
                cp.wait_recv()
            for cp in copies:
                cp.wait_send()

    rev = lambda w: pl.BlockSpec((nb, CHUNK, w), lambda i: (0, nc - 1 - i, 0))
    const = lambda r, c: pl.BlockSpec((r, c), lambda i: (0, 0))
    hbm = pl.BlockSpec(memory_space=pl.ANY)
    return pl.pallas_call(
        body,
        name="mix_bwd",
        grid=(nc,),
        in_specs=[
            rev(PW),
            pl.BlockSpec((nb, CHUNK, 2 * CW), lambda i: (0, jnp.maximum(nc - 2 - i, 0), OCC // (2 * CW))),
            rev(NQK),
            pl.BlockSpec((nb, 1, NQK, DV), lambda i: (0, nc - 1 - i, 0, 0)),
            rev(NV),
            rev(D),
            const(1, DV),
            const(CONV_K, CW),
            const(A_PAD, NQK),
        ] + [hbm] * na,
        out_specs=[rev(PW), const(1, DV), const(8, CW), const(1, NQK), const(A_PAD, NQK)] + [hbm] * na,
        out_shape=[
            jax.ShapeDtypeStruct((nb, s, PW), BF16),
            jax.ShapeDtypeStruct((1, DV), F32),
            jax.ShapeDtypeStruct((8, CW), F32),
            jax.ShapeDtypeStruct((1, NQK), F32),
            jax.ShapeDtypeStruct((A_PAD, NQK), F32),
        ] + [jax.ShapeDtypeStruct((3,) + p.shape[1:], BF16) for p in pbs],
        scratch_shapes=[pltpu.VMEM((nb, NQK, DV), F32), pltpu.VMEM((nb, CHUNK, CW), F32),
                        pltpu.SemaphoreType.DMA((3 * na,)), pltpu.SemaphoreType.DMA((3 * na,))],
        compiler_params=_params(("arbitrary",)),
    )(proj3, proj3, z3, sprev, opre3, dmix3, gng, conv_w, wgu_p, *pbs)


SMALL_PACK_ROWS = 16


def _wgu_slot(r):
    return 4 + r // 4, NQK * (r % 4)


CONV_SLOTS = ((8, 0), (8, CW), (9, 0))


def _in_proj_bwd(dproj2d, x2d, dx1, g1, w_in_t, tm, pb, small_parts):
    t = x2d.shape[0]
    nt = t // tm

    def body(dp_ref, x_ref, dx1_ref, g_ref, w_ref, pb_ref, dg2, dgf, dbg, dgng, dwgu, dcw, lp,
             dx_ref, sums_ref, r2_ref, dg1_acc, pack, gbuf, pack1, gbuf1, send_sems, recv_sems,
             ssend, srecv, ssend1, srecv1):
        x, y, c = _position()
        me = 4 * x + 2 * y + c
        flips = [(k >> 2, (k >> 1) & 1, k & 1) for k in range(1, N_DEV)]
        peers = [(x ^ fx, y ^ fy, c ^ fc) for fx, fy, fc in flips]

        def small_copies(src, dst, send, recv, arrivals):
            return [pltpu.make_async_remote_copy(
                src_ref=src, dst_ref=dst.at[4 * px + 2 * py + pc if arrivals else me],
                send_sem=send.at[k], recv_sem=recv.at[k], device_id=(px, py, pc), device_id_type=MESH)
                for k, (px, py, pc) in enumerate(peers)]

        @pl.when(pl.program_id(0) == 0)
        def _():
            for cp in _stage2_copies([pb_ref], [r2_ref], send_sems, recv_sems):
                cp.start()
            dg1_acc[...] = jnp.zeros_like(dg1_acc)
            pack[...] = jnp.zeros_like(pack)
            pack[1:2, :] = dg2[...]
            pack[2:3, :] = dgf[...]
            pack[3:4, 0:NQK] = dbg[...]
            pack[3:4, NQK:NQK + DV] = dgng[...]
            pack[3:4, NQK + DV:NQK + 2 * DV] = lp[...]
            for r in range(RANK):
                row, lane = _wgu_slot(r)
                pack[row:row + 1, lane:lane + NQK] = dwgu[r:r + 1, :]
            for r, (row, lane) in enumerate(CONV_SLOTS):
                pack[row:row + 1, lane:lane + CW] = dcw[r:r + 1, :]
            for cp in small_copies(pack, gbuf, ssend, srecv, False):
                cp.start()
            gbuf[me] = pack[...]

        xv = x_ref[...]
        r = lax.rsqrt(jnp.mean(xv * xv, axis=-1, keepdims=True) + EPS)
        n1 = xv * r
        dh = _mm(dp_ref[...], w_ref[...])
        dg1_acc[...] += jnp.sum(dh * n1, axis=0, keepdims=True)
        dn = dh * g_ref[...]
        dx_ref[...] = dx1_ref[...] + r * (dn - n1 * jnp.mean(dn * n1, axis=-1, keepdims=True))

        @pl.when(pl.program_id(0) == nt - 1)
        def _():
            pack1[...] = jnp.zeros_like(pack1)
            pack1[0:1, :] = dg1_acc[...]
            for cp in small_copies(pack1, gbuf1, ssend1, srecv1, False):
                cp.start()
            gbuf1[me] = pack1[...]
            copies = _stage2_copies([pb_ref], [r2_ref], send_sems, recv_sems)
            for cp in copies:
                cp.wait_recv()
            for cp in copies:
                cp.wait_send()
            for src, dst, send, recv in ((pack, gbuf, ssend, srecv), (pack1, gbuf1, ssend1, srecv1)):
                for cp in small_copies(src, dst, send, recv, True):
                    cp.wait_recv()
                    cp.wait_send()
            acc = gbuf[0]
            acc1 = gbuf1[0]
            for d in range(1, N_DEV):
                acc = acc + gbuf[d]
                acc1 = acc1 + gbuf1[d]
            sums_ref[...] = acc
            sums_ref[0:1, :] = acc1[0:1, :]

    tile = lambda w: pl.BlockSpec((tm, w), lambda i: (i, 0))
    vec = pl.BlockSpec((1, D), lambda i: (0, 0))
    hbm = pl.BlockSpec(memory_space=pl.ANY)
    whole = lambda a: pl.BlockSpec(a.shape, lambda i: (0,) * a.ndim)
    return pl.pallas_call(
        body,
        name="in_proj_bwd",
        grid=(nt,),
        in_specs=[tile(PW), tile(D), tile(D), vec, pl.BlockSpec((PW, D), lambda i: (0, 0)), hbm]
        + [whole(a) for a in small_parts],
        out_specs=[tile(D), pl.BlockSpec((SMALL_PACK_ROWS, D), lambda i: (0, 0)), hbm],
        out_shape=[jax.ShapeDtypeStruct((t, D), F32), jax.ShapeDtypeStruct((SMALL_PACK_ROWS, D), F32),
                   jax.ShapeDtypeStruct((3,) + pb.shape[1:], BF16)],
        scratch_shapes=[pltpu.VMEM((1, D), F32),
                        pltpu.VMEM((SMALL_PACK_ROWS, D), F32), pltpu.VMEM((N_DEV, SMALL_PACK_ROWS, D), F32),
                        pltpu.VMEM((8, D), F32), pltpu.VMEM((N_DEV, 8, D), F32),
                        pltpu.SemaphoreType.DMA((3,)), pltpu.SemaphoreType.DMA((3,)),
                        pltpu.SemaphoreType.DMA((7,)), pltpu.SemaphoreType.DMA((7,)),
                        pltpu.SemaphoreType.DMA((7,)), pltpu.SemaphoreType.DMA((7,))],
        compiler_params=_params(("arbitrary",)),
    )(dproj2d, x2d, dx1, g1, w_in_t, pb, *small_parts)


def _get_rows(ref):
    return ref[:, 0, :] if len(ref.shape) == 3 else ref[...]


def _put_rows(ref, val):
    if len(ref.shape) == 3:
        ref[:, 0, :] = val
    else:
        ref[...] = val


def _adamw_math(w, g, m, v):
    m = ADAM_B1 * m + (1.0 - ADAM_B1) * g
    v = ADAM_B2 * v + (1.0 - ADAM_B2) * (g * g)
    m_hat = m / (1.0 - ADAM_B1 ** ADAM_STEP)
    v_hat = v / (1.0 - ADAM_B2 ** ADAM_STEP)
    delta = -ADAM_LR * (m_hat / (jnp.sqrt(v_hat) + ADAM_EPS) + ADAM_WD * w)
    return delta, m, v


def _position():
    return lax.axis_index("x"), lax.axis_index("y"), lax.axis_index("c")


GATHER_PARTS = 2
GATHER_SEMS = 7 * GATHER_PARTS


def _gather_copies(stage, lo, rows, gx, send_sems, recv_sems, local_sem):
    x, y, c = _position()
    me = (x, y, c)
    sibling = (x, y, 1 - c)
    chips = [(1 - x, y), (x, 1 - y), (1 - x, 1 - y)]
    part = -(-rows // (16 * GATHER_PARTS)) * 16
    bounds = [(p * part, min(part, rows - p * part)) for p in range(GATHER_PARTS)]

    def blk(px, py, pc, off, n):
        return gx.at[4 * px + 2 * py + pc, pl.ds(off, n), :]

    mine = pltpu.make_async_copy(stage.at[pl.ds(lo, rows), :], gx.at[4 * x + 2 * y + c], local_sem)
    parts = []
    for p, (off, n) in enumerate(bounds):
        def copy(k, block, to, from_stage=False, p=p, off=off, n=n):
            return pltpu.make_async_remote_copy(
                src_ref=stage.at[pl.ds(lo + off, n), :] if from_stage else blk(*block, off, n),
                dst_ref=blk(*block, off, n), send_sem=send_sems.at[7 * p + k], recv_sem=recv_sems.at[7 * p + k],
                device_id=to, device_id_type=MESH)

        first = [copy(0, me, sibling, True)] + [copy(1 + j, me, (*chip, c), True) for j, chip in enumerate(chips)]
        passed = [copy(4 + j, (*chip, c), sibling) for j, chip in enumerate(chips)]
        arrivals = ([copy(0, sibling, me)] + [copy(1 + j, (*chip, c), me) for j, chip in enumerate(chips)]
                    + [copy(4 + j, (*chip, 1 - c), me) for j, chip in enumerate(chips)])
        parts.append((first, passed, arrivals))
    return mine, parts


def _gather_start(*args):
    mine, parts = _gather_copies(*args)
    mine.start()
    for first, _, _ in parts:
        for cp in first:
            cp.start()


def _gather_finish(*args):
    mine, parts = _gather_copies(*args)
    for _, passed, arrivals in parts:
        for j in range(3):
            arrivals[1 + j].wait_recv()
            passed[j].start()
    for first, passed, arrivals in parts:
        arrivals[0].wait_recv()
        for j in range(3):
            arrivals[4 + j].wait_recv()
        for cp in first + passed:
            cp.wait_send()
    mine.wait()


def _gather_sems():
    return [pltpu.SemaphoreType.DMA((GATHER_SEMS,)), pltpu.SemaphoreType.DMA((GATHER_SEMS,)), pltpu.SemaphoreType.DMA]


def _gather_w_in(w_it, w_gt, w_ut, w_d, w_o, wgu_s, conv_s):
    def body(wi_ref, wg_ref, wu_ref, wd_ref, wo_ref, wgu_ref, conv_ref, w_ref, gwgu_ref, gconv_ref, stage,
             buf, send_sems, recv_sems, local_sem, ssend, srecv):
        x, y, c = _position()
        me = 4 * x + 2 * y + c
        stage[SLAB_IN:SLAB_IN + IN_W, :] = wi_ref[:, 0, :].astype(BF16)
        stage[SLAB_IN + IN_W:SLAB_G, :] = jnp.zeros((IN_ROWS - IN_W, D), BF16)
        args = (stage, SLAB_IN, IN_ROWS, buf, send_sems, recv_sems, local_sem)
        _gather_start(*args)
        stage[SLAB_G:SLAB_U, :] = wg_ref[...].astype(BF16)
        stage[SLAB_U:SLAB_D, :] = wu_ref[...].astype(BF16)
        stage[SLAB_D:SLAB_O, :] = wd_ref[...].astype(BF16)
        stage[SLAB_O:SLAB_ROWS, :] = wo_ref[...].astype(BF16)
        flips = [(k >> 2, (k >> 1) & 1, k & 1) for k in range(1, N_DEV)]
        peers = [(x ^ fx, y ^ fy, c ^ fc) for fx, fy, fc in flips]

        def small(k, block_id, to):
            return [pltpu.make_async_remote_copy(
                src_ref=s, dst_ref=g.at[block_id], send_sem=ssend.at[2 * k + n], recv_sem=srecv.at[2 * k + n],
                device_id=to, device_id_type=MESH)
                for n, (s, g) in enumerate(((wgu_ref, gwgu_ref), (conv_ref, gconv_ref)))]

        gwgu_ref[me] = wgu_ref[...]
        gconv_ref[me] = conv_ref[...]
        for k, peer in enumerate(peers):
            for cp in small(k, me, peer):
                cp.start()
        w_ref[IN_COLS:PW, :] = jnp.zeros((PW - IN_COLS, D), BF16)
        _gather_finish(*args)
        for k, (px, py, pc) in enumerate(peers):
            for cp in small(k, 4 * px + 2 * py + pc, (px, py, pc)):
                cp.wait_recv()
                cp.wait_send()
        for j, lo, hi, d in _in_segments():
            w_ref[d:d + hi - lo, :] = buf[j, lo:hi, :]

    vm = pl.BlockSpec(memory_space=pltpu.VMEM)
    return pl.pallas_call(
        body,
        name="gather_w_in",
        in_specs=[vm] * 7,
        out_specs=[vm] * 4,
        out_shape=[jax.ShapeDtypeStruct((PW, D), BF16),
                   jax.ShapeDtypeStruct((N_DEV,) + wgu_s.shape, F32),
                   jax.ShapeDtypeStruct((N_DEV,) + conv_s.shape, F32),
                   jax.ShapeDtypeStruct((SLAB_ROWS, D), BF16)],
        scratch_shapes=[pltpu.VMEM((N_DEV, IN_ROWS, D), BF16)] + _gather_sems()
        + [pltpu.SemaphoreType.DMA((14,)), pltpu.SemaphoreType.DMA((14,))],
        compiler_params=_params(),
    )(w_it, w_gt, w_ut, w_d, w_o, wgu_s, conv_s)


def _w_in_core_reduce(dw_t):
    def body(d_ref, own_ref, sib_ref, pb_ref, g, gb, r1, send_sems, recv_sems):
        x, y, c = _position()
        chip = 2 * x + y
        for j in range(N_DEV):
            g[j, IN_W:IN_ROWS, :] = jnp.zeros((IN_ROWS - IN_W, D), F32)
        for j, lo, hi, d in _in_segments():
            g[j, lo:hi, :] = d_ref[d:d + hi - lo, :]
        for j in range(N_DEV):
            gb[j] = g[j].astype(BF16)
        copies = _stage1_copies(gb, r1, send_sems, recv_sems)
        for cp in copies:
            cp.start()
        own_ref[0] = g[2 * chip + c]
        for cp in copies:
            cp.wait_recv()
        sib_ref[0] = r1[chip]
        for k in range(1, 4):
            t = chip ^ k
            pb_ref[k - 1] = (g[2 * t + c] + r1[t].astype(F32)).astype(BF16)
        for cp in copies:
            cp.wait_send()

    vm = pl.BlockSpec(memory_space=pltpu.VMEM)
    return pl.pallas_call(
        body,
        name="w_in_core_reduce",
        in_specs=[vm],
        out_specs=[vm, vm, vm],
        out_shape=[jax.ShapeDtypeStruct((1, IN_ROWS, D), F32), jax.ShapeDtypeStruct((1, IN_ROWS, D), BF16),
                   jax.ShapeDtypeStruct((3, IN_ROWS, D), BF16)],
        scratch_shapes=[pltpu.VMEM((N_DEV, IN_ROWS, D), F32), pltpu.VMEM((N_DEV, IN_ROWS, D), BF16),
                        pltpu.VMEM((4, IN_ROWS, D), BF16), pltpu.SemaphoreType.DMA((4,)),
                        pltpu.SemaphoreType.DMA((4,))],
        compiler_params=_params(),
    )(dw_t)


def _stage1_copies(g_ref, r_ref, send_sems, recv_sems):
    x, y, c = _position()
    return [pltpu.make_async_remote_copy(
        src_ref=g_ref.at[2 * i + 1 - c], dst_ref=r_ref.at[i], send_sem=send_sems.at[i], recv_sem=recv_sems.at[i],
        device_id=(x, y, 1 - c), device_id_type=MESH) for i in range(4)]


def _ffn_core_reduce(dw3, dwb3, dw_o, dwb_o, pos_arr, dx1b, gwb):
    def body(pos_ref, g0, g1, g2, go, gb3_hbm, gbo_hbm, dx1b_ref, gwb_hbm, p0, p1, p2, po, s0, s1, s2, so, dmix_ref,
             r1f, r1o, wo, send_sems, recv_sems, wsem):
        step = pl.program_id(0)
        k = jnp.minimum(step, 2)
        x, y, c = _position()
        chip = 2 * x + y

        def copies(p):
            src = 2 * (chip ^ ((p + 1) & 3)) + 1 - c
            pairs = [(gb3_hbm.at[a, src], r1f.at[a, p]) for a in range(3)] + [(gbo_hbm.at[src], r1o.at[p])]
            return [pltpu.make_async_remote_copy(
                src_ref=s, dst_ref=d, send_sem=send_sems.at[4 * p + a], recv_sem=recv_sems.at[4 * p + a],
                device_id=(x, y, 1 - c), device_id_type=MESH) for a, (s, d) in enumerate(pairs)]

        @pl.when(step == 0)
        def _():
            for p in range(4):
                for cp in copies(p):
                    cp.start()
            loads = [pltpu.make_async_copy(gwb_hbm.at[j, pl.ds(FF_W, OUT_ROWS), :],
                                           wo.at[pl.ds(OUT_ROWS * j, OUT_ROWS), :], wsem.at[j]) for j in range(N_DEV)]
            for cp in loads:
                cp.start()
            for cp in loads:
                cp.wait()

        dmix_ref[...] = _nt(dx1b_ref[...], wo[...])

        for p in range(3):
            @pl.when(step == p)
            def _():
                for cp in copies(p):
                    cp.wait_recv()

        for a, (g, pb) in enumerate(((g0, p0), (g1, p1), (g2, p2))):
            pb[...] = (g[...] + r1f[a, k][None].astype(F32)).astype(BF16)
        po[...] = (go[...] + r1o[k][None].astype(F32)).astype(BF16)

        @pl.when(step == 3)
        def _():
            for cp in copies(3):
                cp.wait_recv()
            for a, s in enumerate((s0, s1, s2)):
                s[0] = r1f[a, 3]
            so[0] = r1o[3]
            for p in range(4):
                for cp in copies(p):
                    cp.wait_send()

    t = dx1b.shape[0]
    other = lambda s, pos: 2 * (pos[1] ^ (jnp.minimum(s, 2) + 1)) + pos[0]
    g_spec = lambda lead: pl.BlockSpec((None, 1, FF_W, D), lambda s, pos: (lead, other(s, pos), 0, 0))
    slot = lambda rows: pl.BlockSpec((1, rows, D), lambda s, pos: (jnp.minimum(s, 2), 0, 0))
    one = lambda rows: pl.BlockSpec((1, rows, D), lambda s, pos: (0, 0, 0))
    quarter = pl.BlockSpec((t // 4, D), lambda s, pos: (s, 0))
    hbm = pl.BlockSpec(memory_space=pl.ANY)
    return pl.pallas_call(
        body,
        name="ffn_core_reduce",
        grid_spec=pltpu.PrefetchScalarGridSpec(
            num_scalar_prefetch=1, grid=(4,),
            in_specs=[g_spec(0), g_spec(1), g_spec(2),
                      pl.BlockSpec((1, OUT_ROWS, D), lambda s, pos: (other(s, pos), 0, 0)), hbm, hbm, quarter, hbm],
            out_specs=[slot(FF_W), slot(FF_W), slot(FF_W), slot(OUT_ROWS),
                       one(FF_W), one(FF_W), one(FF_W), one(OUT_ROWS), quarter],
            scratch_shapes=[pltpu.VMEM((3, 4, FF_W, D), BF16), pltpu.VMEM((4, OUT_ROWS, D), BF16),
                            pltpu.VMEM((D, D), BF16), pltpu.SemaphoreType.DMA((16,)),
                            pltpu.SemaphoreType.DMA((16,)), pltpu.SemaphoreType.DMA((N_DEV,))]),
        out_shape=[jax.ShapeDtypeStruct((3, FF_W, D), BF16)] * 3 + [jax.ShapeDtypeStruct((3, OUT_ROWS, D), BF16)]
        + [jax.ShapeDtypeStruct((1, FF_W, D), BF16)] * 3 + [jax.ShapeDtypeStruct((1, OUT_ROWS, D), BF16),
                                                             jax.ShapeDtypeStruct((t, D), F32)],
        compiler_params=_params(("arbitrary",)),
    )(pos_arr, dw3, dw3, dw3, dw_o, dwb3, dwb_o, dx1b, gwb)


def _stage2_copies(p_refs, r_refs, send_sems, recv_sems):
    x, y, c = _position()
    copies = []
    for a in range(len(p_refs)):
        for k in range(1, 4):
            copies.append(pltpu.make_async_remote_copy(
                src_ref=p_refs[a].at[k - 1], dst_ref=r_refs[a].at[k - 1],
                send_sem=send_sems.at[3 * a + k - 1], recv_sem=recv_sems.at[3 * a + k - 1],
                device_id=(x ^ (k >> 1), y ^ (k & 1), c), device_id_type=MESH))
    return copies


def _finish_weights(items, pos_arr, name, nblk):
    n = len(items)
    in_specs, out_specs, out_shape, operands, wbs = [], [], [], [], []
    for g8, lead, r1, r2, w, m, v in items:
        rows, wr = g8.shape[-2], w.shape[0]
        assert rows % nblk == 0 and wr % nblk == 0 and (nblk == 1 or (rows == wr and rows % (16 * nblk) == 0))
        rb, wb = rows // nblk, wr // nblk
        if lead is not None:
            g_spec = pl.BlockSpec((None, 1, rb, D), lambda i, pos, lead=lead: (lead, 2 * pos[1] + pos[0], i, 0))
        elif g8.shape[0] == 1:
            g_spec = pl.BlockSpec((1, rb, D), lambda i, pos: (0, i, 0))
        else:
            g_spec = pl.BlockSpec((1, rb, D), lambda i, pos: (2 * pos[1] + pos[0], i, 0))
        r1_spec = pl.BlockSpec((1, rb, D), lambda i, pos: (0, i, 0))
        if w.ndim == 3:
            wblk = pl.BlockSpec((wb, 1, D), lambda i, pos: (i, 0, 0))
        else:
            wblk = pl.BlockSpec((wb, D), lambda i, pos: (i, 0))
        in_specs += [g_spec, r1_spec, pl.BlockSpec((3, rb, D), lambda i, pos: (0, i, 0)), wblk, wblk, wblk]
        out_specs += [wblk] * 4
        out_shape += [jax.ShapeDtypeStruct(w.shape, F32)] * 4
        operands += [g8, r1, r2, w, m, v]
        wbs.append(wb)

    def body(pos_ref, *refs):
        for a in range(n):
            g_ref, r1_ref, r2_ref, w_ref, m_ref, v_ref = refs[6 * a:6 * a + 6]
            g_out, d_out, m_out, v_out = refs[6 * n + 4 * a:6 * n + 4 * a + 4]
            g = g_ref[0] + r1_ref[0].astype(F32)
            for k in range(3):
                g = g + r2_ref[k].astype(F32)
            g = g[0:wbs[a], :]
            d, mn, vn = _adamw_math(_get_rows(w_ref), g, _get_rows(m_ref), _get_rows(v_ref))
            for out, val in ((g_out, g), (d_out, d), (m_out, mn), (v_out, vn)):
                _put_rows(out, val)

    return pl.pallas_call(
        body,
        name=name,
        grid_spec=pltpu.PrefetchScalarGridSpec(
            num_scalar_prefetch=1, grid=(nblk,), in_specs=in_specs, out_specs=out_specs),
        out_shape=out_shape,
        compiler_params=_params(("arbitrary",)),
    )(pos_arr, *operands)


SMALL_NAMES = ("norm1_g", "norm2_g", "norm_f_g", "b_gate", "gla_norm_g", "w_gate_up", "conv_w")
WGU_W = NQK // N_DEV
CONV_W = CW // N_DEV


def _small_adamw(sums, ws, ms, vs):
    n = len(SMALL_NAMES)

    def body(*refs):
        acc_ref = refs[0]
        w_refs, m_refs, v_refs = refs[1:1 + n], refs[1 + n:1 + 2 * n], refs[1 + 2 * n:1 + 3 * n]
        loss_ref = refs[1 + 3 * n]
        outs = refs[2 + 3 * n:]
        x, y, c = _position()
        me = 4 * x + 2 * y + c
        acc = acc_ref[...]
        loss_ref[...] = acc[3:4, NQK + DV:NQK + DV + 1]

        def my_columns(full, width):
            r = lax.broadcasted_iota(jnp.int32, (full.shape[1], width), 0)
            col = lax.broadcasted_iota(jnp.int32, (full.shape[1], width), 1)
            sel = (r == width * me + col).astype(F32)
            return _mm(full, sel, precision=HIGHEST)

        dwgu = jnp.concatenate([acc[row:row + 1, lane:lane + NQK] for row, lane in map(_wgu_slot, range(RANK))], axis=0)
        dcw = jnp.concatenate([acc[row:row + 1, lane:lane + CW] for row, lane in CONV_SLOTS], axis=0)
        grads = [acc[0:1, :], acc[1:2, :], acc[2:3, :], acc[3:4, 0:NQK], acc[3:4, NQK:NQK + DV],
                 my_columns(dwgu, WGU_W), my_columns(dcw, CONV_W)]
        for i, g in enumerate(grads):
            d, mn, vn = _adamw_math(_get_rows(w_refs[i]), g, _get_rows(m_refs[i]), _get_rows(v_refs[i]))
            for out, val in zip(outs[4 * i:4 * i + 4], (g, d, mn, vn)):
                _put_rows(out, val)

    vm = pl.BlockSpec(memory_space=pltpu.VMEM)
    out_shape = [jax.ShapeDtypeStruct((1, 1), F32)]
    for w in ws:
        out_shape += [jax.ShapeDtypeStruct(w.shape, F32)] * 4
    return pl.pallas_call(
        body,
        name="small_adamw",
        in_specs=[vm] * (1 + 3 * n),
        out_specs=[vm] * (1 + 4 * n),
        out_shape=out_shape,
        compiler_params=_params(),
    )(sums, *ws, *ms, *vs)


def kernel(x, norm1_g, w_in, w_gate_up, b_gate, gla_norm_g, conv_w, w_out, norm2_g, w_ffn_gate, w_ffn_up, w_ffn_down, norm_f_g, loss_target, m_norm1_g, m_w_in, m_w_gate_up, m_b_gate, m_gla_norm_g, m_conv_w, m_w_out, m_norm2_g, m_w_ffn_gate, m_w_ffn_up, m_w_ffn_down, m_norm_f_g, v_norm1_g, v_w_in, v_w_gate_up, v_b_gate, v_gla_norm_g, v_conv_w, v_w_out, v_norm2_g, v_w_ffn_gate, v_w_ffn_up, v_w_ffn_down, v_norm_f_g):
    xi, yi, ci = _position()
    pos_arr = jnp.stack([ci, 2 * xi + yi]).astype(jnp.int32)
    nb, s, _ = x.shape
    t = nb * s

    tr = lambda a: a[0].T
    rows_of = lambda a: a.transpose(2, 0, 1)
    conv_rows = lambda a: a.transpose(1, 0, 2)
    w_in_t, gwgu, gconv, stage = _gather_w_in(rows_of(w_in), tr(w_ffn_gate), tr(w_ffn_up), w_ffn_down[0], w_out[0],
                                              w_gate_up[0], conv_rows(conv_w))
    wgu_f = gwgu.transpose(1, 0, 2).reshape(RANK, NQK)
    conv_f = gconv.transpose(1, 2, 0, 3).reshape(CONV_K, CW)
    wgu_p = jnp.concatenate([wgu_f, jnp.zeros((A_PAD - RANK, NQK), F32)], axis=0).astype(BF16)

    x2d = x.reshape(t, D)
    tgt2d = loss_target.reshape(t, D)
    tm = 256
    tm_in = min(512, t)
    tk = min(2048, t)
    proj, z, h, gwb = _in_proj_fwd(x2d, norm1_g, w_in_t, wgu_p, b_gate, tm_in, stage)
    proj3 = proj.reshape(nb, s, PW)
    z3 = z.reshape(nb, s, NQK)
    mix3, opre3, sprev, x1, gwa = _mix_fwd(proj3, z3, gla_norm_g, conv_f, stage, x, gwb)
    mix2d = mix3.reshape(t, D)
    dx1, dx1b, adu, hb, dg2, dgf, loss_part = _ffn_fwd_bwd(
        x1.reshape(t, D), tgt2d, gwa, gwb, norm2_g, norm_f_g.reshape(1, D), tm)
    dw3, dwb3 = _dw_ffn(adu, hb, tk)
    dw3 = dw3.reshape(3, N_DEV, FF_W, D)
    dw_o, dwb_o = _tn_matmul(mix2d, dx1b, D // 2, D, tk, "dw_out", True)
    dw_o = dw_o.reshape(N_DEV, OUT_ROWS, D)
    *pb, sib_d, sib_g, sib_u, sib_o, dmix = _ffn_core_reduce(
        dw3, dwb3.reshape(3, N_DEV, FF_W, D), dw_o, dwb_o.reshape(N_DEV, OUT_ROWS, D), pos_arr, dx1b, gwb)
    g8 = [dw3, dw3, dw3, dw_o]
    leads = [0, 1, 2, None]
    tags = ("w_ffn_down", "w_ffn_gate", "w_ffn_up", "w_out")
    r1 = [sib_d, sib_g, sib_u, sib_o]
    mb = _mix_bwd(proj3, z3, sprev, opre3, dmix.reshape(nb, s, D), gla_norm_g, conv_f, wgu_p, [pb[0], pb[1], pb[3]])
    dproj3, dgng, dcw, dbg, dwgu = mb[:5]
    dproj2d = dproj3.reshape(t, PW)
    dw_in_t, r2_up = _tn_matmul(dproj2d, h, PW // 5, D, t, "dw_in", False, _stage2_rider([pb[2]]))
    r2 = [mb[5], mb[6], r2_up, mb[7]]
    g_in, r1_in, pb_in = _w_in_core_reduce(dw_in_t)
    dx, small_sums, r2_in = _in_proj_bwd(dproj2d, x2d, dx1, norm1_g, w_in_t, tm_in, pb_in,
                                         (dg2, dgf, dbg, dgng, dwgu, dcw, loss_part))

    tags = ("w_in",) + tags
    g8 = [g_in] + g8
    leads = [None] + leads
    r1 = [r1_in] + list(r1)
    r2 = [r2_in] + r2
    shard_w = (rows_of(w_in), w_ffn_down[0], tr(w_ffn_gate), tr(w_ffn_up), w_out[0])
    shard_m = (rows_of(m_w_in), m_w_ffn_down[0], tr(m_w_ffn_gate), tr(m_w_ffn_up), m_w_out[0])
    shard_v = (rows_of(v_w_in), v_w_ffn_down[0], tr(v_w_ffn_gate), tr(v_w_ffn_up), v_w_out[0])
    back = (lambda o: o.transpose(1, 2, 0), lambda o: o[None], lambda o: o.T[None], lambda o: o.T[None],
            lambda o: o[None])
    items = list(zip(g8, leads, r1, r2, shard_w, shard_m, shard_v))
    flat = list(_finish_weights(items[1:], pos_arr, "finish_ffn_out", 2))
    flat = list(_finish_weights(items[:1], pos_arr, "finish_w_in", 1)) + flat
    results = {}
    for i, (tag, to_shard) in enumerate(zip(tags, back)):
        results[tag] = [to_shard(o) for o in flat[4 * i:4 * i + 4]]

    small_w = (norm1_g, norm2_g, norm_f_g.reshape(1, D), b_gate, gla_norm_g, w_gate_up[0], conv_rows(conv_w))
    small_m = (m_norm1_g, m_norm2_g, m_norm_f_g.reshape(1, D), m_b_gate, m_gla_norm_g, m_w_gate_up[0],
               conv_rows(m_conv_w))
    small_v = (v_norm1_g, v_norm2_g, v_norm_f_g.reshape(1, D), v_b_gate, v_gla_norm_g, v_w_gate_up[0],
               conv_rows(v_conv_w))
    so = _small_adamw(small_sums, small_w, small_m, small_v)
    loss = so[0].reshape(())
    to_shape = {"norm_f_g": lambda o: o.reshape(D), "w_gate_up": lambda o: o[None],
                "conv_w": lambda o: o.transpose(1, 0, 2)}
    for i, name in enumerate(SMALL_NAMES):
        results[name] = [to_shape.get(name, lambda o: o)(o) for o in so[1 + 4 * i:5 + 4 * i]]

    names = ("norm1_g", "w_in", "w_gate_up", "b_gate", "gla_norm_g", "conv_w", "w_out", "norm2_g",
             "w_ffn_gate", "w_ffn_up", "w_ffn_down", "norm_f_g")
    outs = [loss, dx.reshape(nb, s, D)]
    for kind in range(4):
        for name in names:
            outs.append(results[name][kind])
    return tuple(outs)
```

```python
import jax
import jax.numpy as jnp
from jax import lax
from jax.experimental import pallas as pl
from jax.experimental.pallas import tpu as pltpu

F32 = jnp.float32
BF16 = jnp.bfloat16
HIGHEST = lax.Precision.HIGHEST
MESH = pl.DeviceIdType.MESH

N_DEV = 8
D = 1024
DFF = 2816
HEADS = 4
DK = 64
DV = 128
NQK = HEADS * DK
NV = HEADS * DV
RANK = 16
CHUNK = 64
CW = 512
CONV_K = 3
IN_COLS = 3088
EPS = 1e-6
INV_GATE_NORM = 1.0 / 16.0
Q_SCALE = DK ** -0.5

PW = 3200
OQ, OK_, OV, OG, OCB, OCC, OCH, OA = 0, 256, 512, 1024, 1536, 2048, 2560, 3072
A_PAD = 128

ADAM_LR = 0.001
ADAM_B1 = 0.9
ADAM_B2 = 0.999
ADAM_EPS = 1e-08
ADAM_WD = 0.01
ADAM_STEP = 10

IN_W = IN_COLS // N_DEV
IN_ROWS = 400
FF_W = DFF // N_DEV
OUT_ROWS = D // N_DEV
SLAB_IN = 0
SLAB_G = SLAB_IN + IN_ROWS
SLAB_U = SLAB_G + FF_W
SLAB_D = SLAB_U + FF_W
SLAB_O = SLAB_D + FF_W
SLAB_ROWS = SLAB_O + OUT_ROWS

VMEM_LIMIT = 56 * 1024 * 1024


def _params(sem=None, vmem=VMEM_LIMIT):
    return pltpu.CompilerParams(dimension_semantics=sem, vmem_limit_bytes=vmem)


def _nt(a, b):
    return lax.dot_general(a, b, (((1,), (1,)), ((), ())), preferred_element_type=F32)


def _tn(a, b, precision=None):
    return lax.dot_general(a, b, (((0,), (0,)), ((), ())), preferred_element_type=F32, precision=precision)


def _mm(a, b, precision=None):
    return jnp.dot(a, b, preferred_element_type=F32, precision=precision)


def _in_segments():
    segs = []
    for j in range(N_DEV):
        lo, hi = IN_W * j, IN_W * (j + 1)
        cuts = sorted({lo, hi} | {c for c in (OCB, OCB + RANK) if lo < c < hi})
        for a, b in zip(cuts[:-1], cuts[1:]):
            if a < OCB:
                d = a
            elif a < OCB + RANK:
                d = OA + (a - OCB)
            else:
                d = a - RANK
            segs.append((j, a - lo, b - lo, d))
    return segs


def _in_proj_fwd(x2d, g1, w_in_t, wgu_p, b_gate, tm, stage):
    t = x2d.shape[0]
    nt = t // tm
    g_rows = SLAB_ROWS - SLAB_D

    def body(x_ref, g_ref, w_ref, wgu_ref, bg_ref, stage_hbm, proj_ref, z_ref, h_ref, gwb_ref,
             send_sems, recv_sems, local_sem):
        gargs = (stage_hbm, SLAB_D, g_rows, gwb_ref, send_sems, recv_sems, local_sem)

        @pl.when(pl.program_id(0) == 0)
        def _():
            _gather_start(*gargs)

        @pl.when(pl.program_id(0) == RELAY_AT * nt // 8)
        def _():
            _gather_relay(*gargs)

        x = x_ref[...]
        r = lax.rsqrt(jnp.mean(x * x, axis=-1, keepdims=True) + EPS)
        h = ((x * r) * g_ref[...]).astype(BF16)
        h_ref[...] = h
        proj = _nt(h, w_ref[...])
        proj_ref[...] = proj
        pa = proj[:, OA:OA + A_PAD].astype(BF16)
        z_ref[...] = _mm(pa, wgu_ref[...]) + bg_ref[...]

        @pl.when(pl.program_id(0) == nt - 1)
        def _():
            _gather_finish(*gargs)

    return pl.pallas_call(
        body,
        name="in_proj_fwd",
        grid=(t // tm,),
        in_specs=[
            pl.BlockSpec((tm, D), lambda i: (i, 0)),
            pl.BlockSpec((1, D), lambda i: (0, 0)),
            pl.BlockSpec((PW, D), lambda i: (0, 0)),
            pl.BlockSpec((A_PAD, NQK), lambda i: (0, 0)),
            pl.BlockSpec((1, NQK), lambda i: (0, 0)),
            pl.BlockSpec(memory_space=pl.ANY),
        ],
        out_specs=[
            pl.BlockSpec((tm, PW), lambda i: (i, 0)),
            pl.BlockSpec((tm, NQK), lambda i: (i, 0)),
            pl.BlockSpec((tm, D), lambda i: (i, 0)),
            pl.BlockSpec(memory_space=pl.ANY),
        ],
        out_shape=[
            jax.ShapeDtypeStruct((t, PW), F32),
            jax.ShapeDtypeStruct((t, NQK), F32),
            jax.ShapeDtypeStruct((t, D), BF16),
            jax.ShapeDtypeStruct((N_DEV, g_rows, D), BF16),
        ],
        scratch_shapes=_gather_sems(),
        compiler_params=_params(("arbitrary",)),
    )(x2d, g1, w_in_t, wgu_p, b_gate, stage)


def _head_masks():
    lane = lax.broadcasted_iota(jnp.int32, (1, NQK), 1)
    return [(lane >= DK * h) & (lane < DK * (h + 1)) for h in range(HEADS)]


def _split_bf16(x, n):
    parts = []
    for _ in range(n):
        p = x.astype(BF16)
        parts.append(p)
        x = x - p.astype(F32)
    return parts


def _chunk_fwd_parts(q, k, z, tril16):
    la = (jnp.minimum(z, 0.0) - jnp.log1p(jnp.exp(-jnp.abs(z)))) * INV_GATE_NORM
    la_parts = _split_bf16(la, 3)
    bc = _mm(tril16, la_parts[0]) + _mm(tril16, la_parts[1]) + _mm(tril16, la_parts[2])
    bl = bc[CHUNK - 1:CHUNK, :]
    eb = jnp.exp(bc)
    enb = jnp.exp(-bc)
    ekl = jnp.exp(bl - bc)
    qi = (q * Q_SCALE) * eb
    ki = k * enb
    ks = k * ekl
    ones16 = jnp.ones((CHUNK, DV), BF16)
    decb = jnp.exp(_tn(la_parts[0], ones16) + _tn(la_parts[1], ones16) + _tn(la_parts[2], ones16))
    return la, eb, enb, ekl, qi, ki, ks, decb


def _stack_heads(a, masks):
    return jnp.concatenate([jnp.where(m, a, 0.0) for m in masks], axis=0)


def _merge_heads(blocks, masks):
    out = blocks[HEADS - 1]
    for h in range(HEADS - 2, -1, -1):
        out = jnp.where(masks[h], blocks[h], out)
    return out


def _causal_stack_mask():
    row = lax.broadcasted_iota(jnp.int32, (HEADS * CHUNK, CHUNK), 0)
    col = lax.broadcasted_iota(jnp.int32, (HEADS * CHUNK, CHUNK), 1)
    return (row & (CHUNK - 1)) >= col


def _conv_taps(u, uprev):
    row = lax.broadcasted_iota(jnp.int32, u.shape, 0)
    u1 = jnp.where(row < 1, pltpu.roll(uprev, 1, 0), pltpu.roll(u, 1, 0))
    u2 = jnp.where(row < 2, pltpu.roll(uprev, 2, 0), pltpu.roll(u, 2, 0))
    return u1, u2


def _mix_fwd(proj3, z3, gng, conv_w, stage, x3, gwb):
    nb, s, _ = proj3.shape
    nc = s // CHUNK
    g_rows = SLAB_D - SLAB_G

    def body(p_ref, z_ref, gng_ref, cw_ref, stage_hbm, x_ref, gwb_hbm, mix_ref, o_ref, sprev_ref, x1_ref, gwa_ref,
             s_ref, uprev_ref, wo, wsem, send_sems, recv_sems, local_sem):
        n = pl.program_id(0)
        gargs = (stage_hbm, SLAB_G, g_rows, gwa_ref, send_sems, recv_sems, local_sem)

        @pl.when(n == 0)
        def _():
            _gather_start(*gargs)
            loads = [pltpu.make_async_copy(gwb_hbm.at[j, pl.ds(FF_W, OUT_ROWS), :],
                                           wo.at[pl.ds(OUT_ROWS * j, OUT_ROWS), :], wsem.at[j]) for j in range(N_DEV)]
            for cp in loads:
                cp.start()
            s_ref[...] = jnp.zeros_like(s_ref)
            uprev_ref[...] = jnp.zeros_like(uprev_ref)
            for cp in loads:
                cp.wait()

        @pl.when(n == RELAY_AT * nc // 8)
        def _():
            _gather_relay(*gargs)

        r_i = lax.broadcasted_iota(jnp.int32, (CHUNK, CHUNK), 0)
        c_i = lax.broadcasted_iota(jnp.int32, (CHUNK, CHUNK), 1)
        tril16 = (r_i >= c_i).astype(BF16)
        masks = _head_masks()
        cmask = _causal_stack_mask()
        gg = gng_ref[...]
        for b in range(nb):
            q = p_ref[b, :, OQ:OQ + NQK]
            k = p_ref[b, :, OK_:OK_ + NQK]
            _, _, _, _, qi, ki, ks, decb = _chunk_fwd_parts(q, k, z_ref[b], tril16)
            qs = _stack_heads(qi, masks).astype(BF16)
            sc = jnp.where(cmask, _nt(qs, ki.astype(BF16)), 0.0).astype(BF16)
            st = s_ref[b]
            sprev_ref[b, 0] = st
            o_inter = _mm(qs, st.astype(BF16))
            v16 = p_ref[b, :, OV:OV + NV].astype(BF16)
            kv = _tn(ks.astype(BF16), v16)
            for h in range(HEADS):
                rows = slice(CHUNK * h, CHUNK * (h + 1))
                cols = slice(DV * h, DV * (h + 1))
                o = _mm(sc[rows], v16[:, cols]) + o_inter[rows]
                o_ref[b, :, cols] = o
                r = lax.rsqrt(jnp.mean(o * o, axis=-1, keepdims=True) + EPS)
                on = (o * r) * gg
                g = p_ref[b, :, OG + DV * h:OG + DV * (h + 1)]
                mix_ref[b, :, cols] = (on * (g * jax.nn.sigmoid(g))).astype(BF16)
                s_ref[b, rows, :] = decb[rows] * st[rows] + kv[rows, cols]
            u = p_ref[b, :, OCC:OCC + CW] * p_ref[b, :, OCH:OCH + CW]
            u1, u2 = _conv_taps(u, uprev_ref[b])
            yc = cw_ref[0:1, :] * u2 + cw_ref[1:2, :] * u1 + cw_ref[2:3, :] * u
            mix_ref[b, :, NV:NV + CW] = (p_ref[b, :, OCB:OCB + CW] * yc).astype(BF16)
            uprev_ref[b] = u
        mixed = _mm(jnp.concatenate([mix_ref[b] for b in range(nb)], axis=0), wo[...])
        for b in range(nb):
            x1_ref[b] = x_ref[b] + mixed[CHUNK * b:CHUNK * (b + 1)]

        @pl.when(n == nc - 1)
        def _():
            _gather_finish(*gargs)

    return pl.pallas_call(
        body,
        name="mix_fwd",
        grid=(nc,),
        in_specs=[
            pl.BlockSpec((nb, CHUNK, PW), lambda n: (0, n, 0)),
            pl.BlockSpec((nb, CHUNK, NQK), lambda n: (0, n, 0)),
            pl.BlockSpec((1, DV), lambda n: (0, 0)),
            pl.BlockSpec((CONV_K, CW), lambda n: (0, 0)),
            pl.BlockSpec(memory_space=pl.ANY),
            pl.BlockSpec((nb, CHUNK, D), lambda n: (0, n, 0)),
            pl.BlockSpec(memory_space=pl.ANY),
        ],
        out_specs=[
            pl.BlockSpec((nb, CHUNK, D), lambda n: (0, n, 0)),
            pl.BlockSpec((nb, CHUNK, NV), lambda n: (0, n, 0)),
            pl.BlockSpec((nb, 1, NQK, DV), lambda n: (0, n, 0, 0)),
            pl.BlockSpec((nb, CHUNK, D), lambda n: (0, n, 0)),
            pl.BlockSpec(memory_space=pl.ANY),
        ],
        out_shape=[
            jax.ShapeDtypeStruct((nb, s, D), BF16),
            jax.ShapeDtypeStruct((nb, s, NV), F32),
            jax.ShapeDtypeStruct((nb, nc, NQK, DV), F32),
            jax.ShapeDtypeStruct((nb, s, D), F32),
            jax.ShapeDtypeStruct((N_DEV, g_rows, D), BF16),
        ],
        scratch_shapes=[pltpu.VMEM((nb, NQK, DV), F32), pltpu.VMEM((nb, CHUNK, CW), F32),
                        pltpu.VMEM((D, D), BF16), pltpu.SemaphoreType.DMA((N_DEV,))] + _gather_sems(),
        compiler_params=_params(("arbitrary",)),
    )(proj3, z3, gng, conv_w, stage, x3, gwb)


def _ffn_fwd_bwd(x1_2d, tgt2d, gwa, gwb, g2, gf, tm):
    t = x1_2d.shape[0]

    def body(x1_ref, tgt_ref, g2_ref, gf_ref, gwa_hbm, gwb_hbm,
             dx1_ref, dx1b_ref, adu_ref, hb_ref, dg2_ref, dgf_ref, loss_ref,
             wg, wu, wd, wsem):
        i = pl.program_id(0)

        def weight_copies(n, dst, src, off, rows):
            return [pltpu.make_async_copy(src.at[j, pl.ds(off, rows), :], dst.at[pl.ds(rows * j, rows), :],
                                          wsem.at[N_DEV * n + j]) for j in range(N_DEV)]

        loads = (weight_copies(0, wg, gwa_hbm, 0, FF_W), weight_copies(1, wu, gwa_hbm, FF_W, FF_W),
                 weight_copies(2, wd, gwb_hbm, 0, FF_W))

        @pl.when(i == 0)
        def _():
            for group in loads:
                for cp in group:
                    cp.start()
            dg2_ref[...] = jnp.zeros_like(dg2_ref)
            dgf_ref[...] = jnp.zeros_like(dgf_ref)
            loss_ref[...] = jnp.zeros_like(loss_ref)
            for group in loads:
                for cp in group:
                    cp.wait()

        g2v = g2_ref[...]
        gfv = gf_ref[...]
        x1 = x1_ref[...]
        r2 = lax.rsqrt(jnp.mean(x1 * x1, axis=-1, keepdims=True) + EPS)
        n2 = x1 * r2
        h2 = (n2 * g2v).astype(BF16)
        hb_ref[1] = h2
        gate = _nt(h2, wg[...])
        up = _nt(h2, wu[...])
        sg = jax.nn.sigmoid(gate)
        sil = gate * sg
        act = (sil * up).astype(BF16)
        adu_ref[0] = act
        x2 = x1 + _mm(act, wd[...])
        rf = lax.rsqrt(jnp.mean(x2 * x2, axis=-1, keepdims=True) + EPS)
        nf = x2 * rf
        err = nf * gfv - tgt_ref[...]
        loss_ref[...] += 0.5 * jnp.sum(jnp.mean(err * err, axis=-1, keepdims=True))
        dy = err * (1.0 / D)
        dgf_ref[...] += jnp.sum(dy * nf, axis=0, keepdims=True)
        dnf = dy * gfv
        dx2 = rf * (dnf - nf * jnp.mean(dnf * nf, axis=-1, keepdims=True))
        dx2b = dx2.astype(BF16)
        hb_ref[0] = dx2b
        dact = _nt(dx2b, wd[...])
        dup = (dact * sil).astype(BF16)
        dgate = ((dact * up) * (sg * (1.0 + gate * (1.0 - sg)))).astype(BF16)
        adu_ref[2] = dup
        adu_ref[1] = dgate
        dh2 = _mm(dgate, wg[...]) + _mm(dup, wu[...])
        dg2_ref[...] += jnp.sum(dh2 * n2, axis=0, keepdims=True)
        dn2 = dh2 * g2v
        dx1 = dx2 + r2 * (dn2 - n2 * jnp.mean(dn2 * n2, axis=-1, keepdims=True))
        dx1_ref[...] = dx1
        dx1b_ref[...] = dx1.astype(BF16)

    tile = lambda w: pl.BlockSpec((tm, w), lambda i: (i, 0))
    vec = pl.BlockSpec((1, D), lambda i: (0, 0))
    hbm = pl.BlockSpec(memory_space=pl.ANY)
    return pl.pallas_call(
        body,
        name="ffn_fwd_bwd",
        grid=(t // tm,),
        in_specs=[tile(D), tile(D), vec, vec, hbm, hbm],
        out_specs=[tile(D), tile(D), pl.BlockSpec((3, tm, DFF), lambda i: (0, i, 0)),
                   pl.BlockSpec((2, tm, D), lambda i: (0, i, 0)), vec, vec,
                   pl.BlockSpec((1, 128), lambda i: (0, 0))],
        out_shape=[
            jax.ShapeDtypeStruct((t, D), F32),
            jax.ShapeDtypeStruct((t, D), BF16),
            jax.ShapeDtypeStruct((3, t, DFF), BF16),
            jax.ShapeDtypeStruct((2, t, D), BF16),
            jax.ShapeDtypeStruct((1, D), F32),
            jax.ShapeDtypeStruct((1, D), F32),
            jax.ShapeDtypeStruct((1, 128), F32),
        ],
        scratch_shapes=[pltpu.VMEM((DFF, D), BF16), pltpu.VMEM((DFF, D), BF16), pltpu.VMEM((DFF, D), BF16),
                        pltpu.SemaphoreType.DMA((3 * N_DEV,))],
        compiler_params=_params(("arbitrary",)),
    )(x1_2d, tgt2d, g2, gf, gwa, gwb)


def _stage2_rider(pbs):
    return dict(inputs=list(pbs), out_shape=[jax.ShapeDtypeStruct(p.shape, BF16) for p in pbs], nsem=3 * len(pbs),
                copies=_stage2_copies)


def _tn_matmul(a, b, bm, bn, tk, name, with_bf16, rider=None):
    t, m = a.shape
    n = b.shape[1]
    nk = t // tk
    nout = 2 if with_bf16 else 1
    grid = (m // bm, n // bn, nk)
    r_in = [] if rider is None else rider["inputs"]
    r_out = [] if rider is None else rider["out_shape"]

    def body(a_ref, b_ref, *rest):
        ins, outs = rest[:len(r_in)], rest[len(r_in):len(r_in) + nout]
        r_outs, sems = rest[len(r_in) + nout:len(r_in) + nout + len(r_out)], rest[len(r_in) + nout + len(r_out):]
        o_ref = outs[0]
        i, j, k = pl.program_id(0), pl.program_id(1), pl.program_id(2)
        if rider is not None:
            @pl.when((i == 0) & (j == 0) & (k == 0))
            def _():
                for cp in rider["copies"](ins, r_outs, *sems):
                    cp.start()

        @pl.when(k == 0)
        def _():
            o_ref[...] = jnp.zeros_like(o_ref)

        o_ref[...] += _tn(a_ref[...].astype(BF16), b_ref[...].astype(BF16))
        if with_bf16:
            @pl.when(k == nk - 1)
            def _():
                outs[1][...] = o_ref[...].astype(BF16)
        if rider is not None:
            @pl.when((i == grid[0] - 1) & (j == grid[1] - 1) & (k == nk - 1))
            def _():
                copies = rider["copies"](ins, r_outs, *sems)
                for cp in copies:
                    cp.wait_recv()
                for cp in copies:
                    cp.wait_send()

    out_blk = pl.BlockSpec((bm, bn), lambda i, j, k: (i, j))
    hbm = pl.BlockSpec(memory_space=pl.ANY)
    out_shape = [jax.ShapeDtypeStruct((m, n), F32)] + ([jax.ShapeDtypeStruct((m, n), BF16)] if with_bf16 else [])
    res = pl.pallas_call(
        body,
        name=name,
        grid=grid,
        in_specs=[pl.BlockSpec((tk, bm), lambda i, j, k: (k, i)), pl.BlockSpec((tk, bn), lambda i, j, k: (k, j))]
        + [hbm] * len(r_in),
        out_specs=[out_blk] * nout + [hbm] * len(r_out),
        out_shape=out_shape + list(r_out),
        scratch_shapes=([] if rider is None else
                        [pltpu.SemaphoreType.DMA((rider["nsem"],)), pltpu.SemaphoreType.DMA((rider["nsem"],))]),
        compiler_params=_params(("parallel", "parallel", "arbitrary") if rider is None
                                else ("arbitrary", "arbitrary", "arbitrary")),
    )(a, b, *r_in)
    return res[0] if len(res) == 1 else res


def _dw_ffn(adu, hb, tk):
    _, t, _ = adu.shape
    bm = DFF // 2
    nk = t // tk

    def body(a_ref, b_ref, o_ref, ob_ref):
        k = pl.program_id(2)

        @pl.when(k == 0)
        def _():
            o_ref[...] = jnp.zeros_like(o_ref)

        o_ref[...] += _tn(a_ref[...], b_ref[...])

        @pl.when(k == nk - 1)
        def _():
            ob_ref[...] = o_ref[...].astype(BF16)

    out_blk = pl.BlockSpec((None, bm, D), lambda p, i, k: (p, i, 0))
    return pl.pallas_call(
        body,
        name="dw_ffn",
        grid=(3, DFF // bm, nk),
        in_specs=[pl.BlockSpec((None, tk, bm), lambda p, i, k: (p, k, i)),
                  pl.BlockSpec((None, tk, D), lambda p, i, k: (jnp.minimum(p, 1), k, 0))],
        out_specs=[out_blk, out_blk],
        out_shape=[jax.ShapeDtypeStruct((3, DFF, D), F32), jax.ShapeDtypeStruct((3, DFF, D), BF16)],
        compiler_params=_params(("arbitrary", "arbitrary", "arbitrary")),
    )(adu, hb)


def _mix_bwd(proj3, z3, sprev, opre3, dmix3, gng, conv_w, wgu_p, pbs):
    nb, s, _ = proj3.shape
    nc = s // CHUNK
    na = len(pbs)

    def body(*refs):
        (p_ref, pprev_ref, z_ref, sp_ref, o_ref, dm_ref, gng_ref, cw_ref, wgu_ref) = refs[:9]
        pb_refs = refs[9:9 + na]
        (dproj_ref, dgng_ref, dcw_ref, dbg_ref, dwgu_ref) = refs[9 + na:14 + na]
        r2_refs = refs[14 + na:14 + 2 * na]
        ds_ref, dycn_ref, send_sems, recv_sems = refs[14 + 2 * na:]
        step = pl.program_id(0)
        n = nc - 1 - step

        @pl.when(step == 0)
        def _():
            for cp in _stage2_copies(pb_refs, r2_refs, send_sems, recv_sems):
                cp.start()
            ds_ref[...] = jnp.zeros_like(ds_ref)
            dycn_ref[...] = jnp.zeros_like(dycn_ref)
            dgng_ref[...] = jnp.zeros_like(dgng_ref)
            dcw_ref[...] = jnp.zeros_like(dcw_ref)
            dbg_ref[...] = jnp.zeros_like(dbg_ref)
            dwgu_ref[...] = jnp.zeros_like(dwgu_ref)

        r_i = lax.broadcasted_iota(jnp.int32, (CHUNK, CHUNK), 0)
        c_i = lax.broadcasted_iota(jnp.int32, (CHUNK, CHUNK), 1)
        tril16 = (r_i >= c_i).astype(BF16)
        triu16 = (r_i <= c_i).astype(BF16)
        causal = r_i >= c_i
        masks = _head_masks()
        cmask = _causal_stack_mask()
        gg = gng_ref[...]
        last_row = lax.broadcasted_iota(jnp.int32, (CHUNK, NQK), 0) == CHUNK - 1
        ones_r = jnp.ones((16, DV), BF16)
        has_prev = (n > 0).astype(F32)
        for b in range(nb):
            q = p_ref[b, :, OQ:OQ + NQK]
            k = p_ref[b, :, OK_:OK_ + NQK]
            z = z_ref[b]
            _, eb, enb, ekl, qi, ki, ks, decb = _chunk_fwd_parts(q, k, z, tril16)
            qi16 = qi.astype(BF16)
            ki16 = ki.astype(BF16)
            qs = _stack_heads(qi, masks).astype(BF16)
            sc = jnp.where(cmask, _nt(qs, ki16), 0.0).astype(BF16)
            st = sp_ref[b, 0]
            st16 = st.astype(BF16)
            dsn = ds_ref[b]
            dsn16 = dsn.astype(BF16)
            v16 = p_ref[b, :, OV:OV + NV].astype(BF16)
            do16 = []
            dgng = jnp.zeros((1, DV), F32)
            for h in range(HEADS):
                cols = slice(DV * h, DV * (h + 1))
                o = o_ref[b, :, cols]
                r = lax.rsqrt(jnp.mean(o * o, axis=-1, keepdims=True) + EPS)
                nh = o * r
                g = p_ref[b, :, OG + DV * h:OG + DV * (h + 1)]
                sg = jax.nn.sigmoid(g)
                dog = dm_ref[b, :, cols]
                dproj_ref[b, :, OG + DV * h:OG + DV * (h + 1)] = (
                    (dog * (nh * gg)) * (sg * (1.0 + g * (1.0 - sg)))).astype(BF16)
                don = dog * (g * sg)
                dgng = dgng + jnp.sum(don * nh, axis=0, keepdims=True)
                dn = don * gg
                do = r * (dn - nh * jnp.mean(dn * nh, axis=-1, keepdims=True))
                do16.append(do.astype(BF16))
            dgng_ref[...] += dgng
            do_rows = jnp.concatenate(do16, axis=0)
            v_rows = jnp.concatenate([v16[:, DV * h:DV * (h + 1)] for h in range(HEADS)], axis=0)
            dp16 = [jnp.where(causal, _nt(do16[h], v16[:, DV * h:DV * (h + 1)]), 0.0).astype(BF16)
                    for h in range(HEADS)]
            ks_dsn = _mm(_stack_heads(ks, masks).astype(BF16), dsn16)
            do_st = _nt(do_rows, st16)
            v_dsn = _nt(v_rows, dsn16)
            dp_ki = _mm(jnp.concatenate(dp16, axis=0), ki16)
            q_do = _tn(qi16, jnp.concatenate(do16, axis=1))
            dki_h = []
            for h in range(HEADS):
                rows = slice(CHUNK * h, CHUNK * (h + 1))
                cols = slice(DV * h, DV * (h + 1))
                dv = _tn(sc[rows], do16[h]) + ks_dsn[rows]
                dproj_ref[b, :, OV + DV * h:OV + DV * (h + 1)] = dv.astype(BF16)
                dki_h.append(_tn(dp16[h], qi16))
                ds_ref[b, rows, :] = decb[rows] * dsn[rows] + q_do[rows, cols]
            blocks = lambda a: [a[CHUNK * h:CHUNK * (h + 1)] for h in range(HEADS)]
            dqi = _merge_heads(blocks(dp_ki + do_st), masks)
            dki = _merge_heads(dki_h, masks)
            dks = _merge_heads(blocks(v_dsn), masks)
            dproj_ref[b, :, OQ:OQ + NQK] = (dqi * (Q_SCALE * eb)).astype(BF16)
            dproj_ref[b, :, OK_:OK_ + NQK] = (dki * enb + dks * ekl).astype(BF16)
            dks_ks = dks * ks
            db = dqi * qi - dki * ki - dks_ks
            sd = _split_bf16(dsn * st * decb, 2)
            dbl = jnp.sum(dks_ks, axis=0, keepdims=True) + (_nt(ones_r, sd[0]) + _nt(ones_r, sd[1]))[0:1, :]
            db = db + jnp.where(last_row, dbl, 0.0)
            db_parts = _split_bf16(db, 3)
            dla = _mm(triu16, db_parts[0]) + _mm(triu16, db_parts[1]) + _mm(triu16, db_parts[2])
            dz = (dla * INV_GATE_NORM) * (1.0 / (1.0 + jnp.exp(z)))
            dbg_ref[...] += jnp.sum(dz, axis=0, keepdims=True)
            dz16 = dz.astype(BF16)
            pa16 = p_ref[b, :, OA:OA + A_PAD].astype(BF16)
            dwgu_ref[...] += _tn(pa16, dz16)
            dproj_ref[b, :, OA:OA + A_PAD] = _nt(dz16, wgu_ref[...]).astype(BF16)
            cb = p_ref[b, :, OCB:OCB + CW]
            cc = p_ref[b, :, OCC:OCC + CW]
            ch = p_ref[b, :, OCH:OCH + CW]
            u = cc * ch
            uprev = (pprev_ref[b, :, 0:CW] * pprev_ref[b, :, CW:2 * CW]) * has_prev
            u1, u2 = _conv_taps(u, uprev)
            w0 = cw_ref[0:1, :]
            w1 = cw_ref[1:2, :]
            w2 = cw_ref[2:3, :]
            yc = w0 * u2 + w1 * u1 + w2 * u
            doc = dm_ref[b, :, NV:NV + CW]
            dproj_ref[b, :, OCB:OCB + CW] = (doc * yc).astype(BF16)
            dyc = doc * cb
            dycn = dycn_ref[b]
            row = lax.broadcasted_iota(jnp.int32, dyc.shape, 0)
            d1 = jnp.where(row >= CHUNK - 1, pltpu.roll(dycn, CHUNK - 1, 0), pltpu.roll(dyc, CHUNK - 1, 0))
            d2 = jnp.where(row >= CHUNK - 2, pltpu.roll(dycn, CHUNK - 2, 0), pltpu.roll(dyc, CHUNK - 2, 0))
            du = w2 * dyc + w1 * d1 + w0 * d2
            dproj_ref[b, :, OCC:OCC + CW] = (du * ch).astype(BF16)
            dproj_ref[b, :, OCH:OCH + CW] = (du * cc).astype(BF16)
            dcw_ref[0:1, :] += jnp.sum(dyc * u2, axis=0, keepdims=True)
            dcw_ref[1:2, :] += jnp.sum(dyc * u1, axis=0, keepdims=True)
            dcw_ref[2:3, :] += jnp.sum(dyc * u, axis=0, keepdims=True)
            dycn_ref[b] = dyc

        @pl.when(step == nc - 1)
        def _():
            copies = _stage2_copies(pb_refs, r2_refs, send_sems, recv_sems)
            for cp in copies:
                cp.wait_recv()
            for cp in copies:
                cp.wait_send()

    rev = lambda w: pl.BlockSpec((nb, CHUNK, w), lambda i: (0, nc - 1 - i, 0))
    const = lambda r, c: pl.BlockSpec((r, c), lambda i: (0, 0))
    hbm = pl.BlockSpec(memory_space=pl.ANY)
    return pl.pallas_call(
        body,
        name="mix_bwd",
        grid=(nc,),
        in_specs=[
            rev(PW),
            pl.BlockSpec((nb, CHUNK, 2 * CW), lambda i: (0, jnp.maximum(nc - 2 - i, 0), OCC // (2 * CW))),
            rev(NQK),
            pl.BlockSpec((nb, 1, NQK, DV), lambda i: (0, nc - 1 - i, 0, 0)),
            rev(NV),
            rev(D),
            const(1, DV),
            const(CONV_K, CW),
            const(A_PAD, NQK),
        ] + [hbm] * na,
        out_specs=[rev(PW), const(1, DV), const(8, CW), const(1, NQK), const(A_PAD, NQK)] + [hbm] * na,
        out_shape=[
            jax.ShapeDtypeStruct((nb, s, PW), BF16),
            jax.ShapeDtypeStruct((1, DV), F32),
            jax.ShapeDtypeStruct((8, CW), F32),
            jax.ShapeDtypeStruct((1, NQK), F32),
            jax.ShapeDtypeStruct((A_PAD, NQK), F32),
        ] + [jax.ShapeDtypeStruct((3,) + p.shape[1:], BF16) for p in pbs],
        scratch_shapes=[pltpu.VMEM((nb, NQK, DV), F32), pltpu.VMEM((nb, CHUNK, CW), F32),
                        pltpu.SemaphoreType.DMA((3 * na,)), pltpu.SemaphoreType.DMA((3 * na,))],
        compiler_params=_params(("arbitrary",)),
    )(proj3, proj3, z3, sprev, opre3, dmix3, gng, conv_w, wgu_p, *pbs)


SMALL_PACK_ROWS = 16


def _wgu_slot(r):
    return 4 + r // 4, NQK * (r % 4)


CONV_SLOTS = ((8, 0), (8, CW), (9, 0))


def _in_proj_bwd(dproj2d, x2d, dx1, g1, w_in_t, tm, pb, small_parts):
    t = x2d.shape[0]
    nt = t // tm

    def body(dp_ref, x_ref, dx1_ref, g_ref, w_ref, pb_ref, dg2, dgf, dbg, dgng, dwgu, dcw, lp,
             dx_ref, sums_ref, r2_ref, dg1_acc, pack, gbuf, pack1, gbuf1, send_sems, recv_sems,
             ssend, srecv, ssend1, srecv1):
        x, y, c = _position()
        me = 4 * x + 2 * y + c
        flips = [(k >> 2, (k >> 1) & 1, k & 1) for k in range(1, N_DEV)]
        peers = [(x ^ fx, y ^ fy, c ^ fc) for fx, fy, fc in flips]

        def small_copies(src, dst, send, recv, arrivals):
            return [pltpu.make_async_remote_copy(
                src_ref=src, dst_ref=dst.at[4 * px + 2 * py + pc if arrivals else me],
                send_sem=send.at[k], recv_sem=recv.at[k], device_id=(px, py, pc), device_id_type=MESH)
                for k, (px, py, pc) in enumerate(peers)]

        @pl.when(pl.program_id(0) == 0)
        def _():
            for cp in _stage2_copies([pb_ref], [r2_ref], send_sems, recv_sems):
                cp.start()
            dg1_acc[...] = jnp.zeros_like(dg1_acc)
            pack[...] = jnp.zeros_like(pack)
            pack[1:2, :] = dg2[...]
            pack[2:3, :] = dgf[...]
            pack[3:4, 0:NQK] = dbg[...]
            pack[3:4, NQK:NQK + DV] = dgng[...]
            pack[3:4, NQK + DV:NQK + 2 * DV] = lp[...]
            for r in range(RANK):
                row, lane = _wgu_slot(r)
                pack[row:row + 1, lane:lane + NQK] = dwgu[r:r + 1, :]
            for r, (row, lane) in enumerate(CONV_SLOTS):
                pack[row:row + 1, lane:lane + CW] = dcw[r:r + 1, :]
            for cp in small_copies(pack, gbuf, ssend, srecv, False):
                cp.start()
            gbuf[me] = pack[...]

        xv = x_ref[...]
        r = lax.rsqrt(jnp.mean(xv * xv, axis=-1, keepdims=True) + EPS)
        n1 = xv * r
        dh = _mm(dp_ref[...], w_ref[...])
        dg1_acc[...] += jnp.sum(dh * n1, axis=0, keepdims=True)
        dn = dh * g_ref[...]
        dx_ref[...] = dx1_ref[...] + r * (dn - n1 * jnp.mean(dn * n1, axis=-1, keepdims=True))

        @pl.when(pl.program_id(0) == nt - 1)
        def _():
            pack1[...] = jnp.zeros_like(pack1)
            pack1[0:1, :] = dg1_acc[...]
            for cp in small_copies(pack1, gbuf1, ssend1, srecv1, False):
                cp.start()
            gbuf1[me] = pack1[...]
            copies = _stage2_copies([pb_ref], [r2_ref], send_sems, recv_sems)
            for cp in copies:
                cp.wait_recv()
            for cp in copies:
                cp.wait_send()
            for src, dst, send, recv in ((pack, gbuf, ssend, srecv), (pack1, gbuf1, ssend1, srecv1)):
                for cp in small_copies(src, dst, send, recv, True):
                    cp.wait_recv()
                    cp.wait_send()
            acc = gbuf[0]
            acc1 = gbuf1[0]
            for d in range(1, N_DEV):
                acc = acc + gbuf[d]
                acc1 = acc1 + gbuf1[d]
            sums_ref[...] = acc
            sums_ref[0:1, :] = acc1[0:1, :]

    tile = lambda w: pl.BlockSpec((tm, w), lambda i: (i, 0))
    vec = pl.BlockSpec((1, D), lambda i: (0, 0))
    hbm = pl.BlockSpec(memory_space=pl.ANY)
    whole = lambda a: pl.BlockSpec(a.shape, lambda i: (0,) * a.ndim)
    return pl.pallas_call(
        body,
        name="in_proj_bwd",
        grid=(nt,),
        in_specs=[tile(PW), tile(D), tile(D), vec, pl.BlockSpec((PW, D), lambda i: (0, 0)), hbm]
        + [whole(a) for a in small_parts],
        out_specs=[tile(D), pl.BlockSpec((SMALL_PACK_ROWS, D), lambda i: (0, 0)), hbm],
        out_shape=[jax.ShapeDtypeStruct((t, D), F32), jax.ShapeDtypeStruct((SMALL_PACK_ROWS, D), F32),
                   jax.ShapeDtypeStruct((3,) + pb.shape[1:], BF16)],
        scratch_shapes=[pltpu.VMEM((1, D), F32),
                        pltpu.VMEM((SMALL_PACK_ROWS, D), F32), pltpu.VMEM((N_DEV, SMALL_PACK_ROWS, D), F32),
                        pltpu.VMEM((8, D), F32), pltpu.VMEM((N_DEV, 8, D), F32),
                        pltpu.SemaphoreType.DMA((3,)), pltpu.SemaphoreType.DMA((3,)),
                        pltpu.SemaphoreType.DMA((7,)), pltpu.SemaphoreType.DMA((7,)),
                        pltpu.SemaphoreType.DMA((7,)), pltpu.SemaphoreType.DMA((7,))],
        compiler_params=_params(("arbitrary",)),
    )(dproj2d, x2d, dx1, g1, w_in_t, pb, *small_parts)


def _get_rows(ref):
    return ref[:, 0, :] if len(ref.shape) == 3 else ref[...]


def _put_rows(ref, val):
    if len(ref.shape) == 3:
        ref[:, 0, :] = val
    else:
        ref[...] = val


def _adamw_math(w, g, m, v):
    m = ADAM_B1 * m + (1.0 - ADAM_B1) * g
    v = ADAM_B2 * v + (1.0 - ADAM_B2) * (g * g)
    m_hat = m / (1.0 - ADAM_B1 ** ADAM_STEP)
    v_hat = v / (1.0 - ADAM_B2 ** ADAM_STEP)
    delta = -ADAM_LR * (m_hat / (jnp.sqrt(v_hat) + ADAM_EPS) + ADAM_WD * w)
    return delta, m, v


def _position():
    return lax.axis_index("x"), lax.axis_index("y"), lax.axis_index("c")


GATHER_PARTS = 2
GATHER_SEMS = 7 * GATHER_PARTS
RELAY_AT = 3


def _gather_copies(stage, lo, rows, gx, send_sems, recv_sems, local_sem):
    x, y, c = _position()
    me = (x, y, c)
    sibling = (x, y, 1 - c)
    chips = [(1 - x, y), (x, 1 - y), (1 - x, 1 - y)]
    part = -(-rows // (16 * GATHER_PARTS)) * 16
    bounds = [(p * part, min(part, rows - p * part)) for p in range(GATHER_PARTS)]

    def blk(px, py, pc, off, n):
        return gx.at[4 * px + 2 * py + pc, pl.ds(off, n), :]

    mine = pltpu.make_async_copy(stage.at[pl.ds(lo, rows), :], gx.at[4 * x + 2 * y + c], local_sem)
    parts = []
    for p, (off, n) in enumerate(bounds):
        def copy(k, block, to, from_stage=False, p=p, off=off, n=n):
            return pltpu.make_async_remote_copy(
                src_ref=stage.at[pl.ds(lo + off, n), :] if from_stage else blk(*block, off, n),
                dst_ref=blk(*block, off, n), send_sem=send_sems.at[7 * p + k], recv_sem=recv_sems.at[7 * p + k],
                device_id=to, device_id_type=MESH)

        first = [copy(0, me, sibling, True)] + [copy(1 + j, me, (*chips[j], c), True) for j in range(2)]
        relay = copy(3, (*chips[p], c), (*chips[1 - p], c))
        passed = [copy(4 + j, (*chip, c), sibling) for j, chip in enumerate(chips)]
        arrivals = ([copy(0, sibling, me)] + [copy(1 + j, (*chip, c), me) for j, chip in enumerate(chips)]
                    + [copy(4 + j, (*chip, 1 - c), me) for j, chip in enumerate(chips)])
        parts.append((first, relay, passed, arrivals))
    return mine, parts


def _gather_start(*args):
    mine, parts = _gather_copies(*args)
    mine.start()
    for first, _, _, _ in parts:
        first[0].start()
    for p, q in ((0, 0), (1, 1), (0, 1), (1, 0)):
        parts[p][0][1 + q].start()


def _gather_relay(*args):
    _, parts = _gather_copies(*args)
    for p, (_, relay, passed, arrivals) in enumerate(parts):
        arrivals[1 + p].wait_recv()
        relay.start()
        passed[p].start()


def _gather_finish(*args):
    mine, parts = _gather_copies(*args)
    for p, (_, _, passed, arrivals) in enumerate(parts):
        for j in range(3):
            if j != p:
                arrivals[1 + j].wait_recv()
                passed[j].start()
    for first, relay, passed, arrivals in parts:
        arrivals[0].wait_recv()
        for j in range(3):
            arrivals[4 + j].wait_recv()
        for cp in first + [relay] + passed:
            cp.wait_send()
    mine.wait()


def _gather_sems():
    return [pltpu.SemaphoreType.DMA((GATHER_SEMS,)), pltpu.SemaphoreType.DMA((GATHER_SEMS,)), pltpu.SemaphoreType.DMA]


def _gather_w_in(w_it, w_gt, w_ut, w_d, w_o, wgu_s, conv_s):
    def body(wi_ref, wg_ref, wu_ref, wd_ref, wo_ref, wgu_ref, conv_ref, w_ref, gwgu_ref, gconv_ref, stage,
             buf, send_sems, recv_sems, local_sem, ssend, srecv):
        x, y, c = _position()
        me = 4 * x + 2 * y + c
        stage[SLAB_IN:SLAB_IN + IN_W, :] = wi_ref[:, 0, :].astype(BF16)
        stage[SLAB_IN + IN_W:SLAB_G, :] = jnp.zeros((IN_ROWS - IN_W, D), BF16)
        args = (stage, SLAB_IN, IN_ROWS, buf, send_sems, recv_sems, local_sem)
        _gather_start(*args)
        stage[SLAB_G:SLAB_U, :] = wg_ref[...].astype(BF16)
        stage[SLAB_U:SLAB_D, :] = wu_ref[...].astype(BF16)
        stage[SLAB_D:SLAB_O, :] = wd_ref[...].astype(BF16)
        stage[SLAB_O:SLAB_ROWS, :] = wo_ref[...].astype(BF16)
        flips = [(k >> 2, (k >> 1) & 1, k & 1) for k in range(1, N_DEV)]
        peers = [(x ^ fx, y ^ fy, c ^ fc) for fx, fy, fc in flips]

        def small(k, block_id, to):
            return [pltpu.make_async_remote_copy(
                src_ref=s, dst_ref=g.at[block_id], send_sem=ssend.at[2 * k + n], recv_sem=srecv.at[2 * k + n],
                device_id=to, device_id_type=MESH)
                for n, (s, g) in enumerate(((wgu_ref, gwgu_ref), (conv_ref, gconv_ref)))]

        gwgu_ref[me] = wgu_ref[...]
        gconv_ref[me] = conv_ref[...]
        for k, peer in enumerate(peers):
            for cp in small(k, me, peer):
                cp.start()
        w_ref[IN_COLS:PW, :] = jnp.zeros((PW - IN_COLS, D), BF16)
        _gather_relay(*args)
        _gather_finish(*args)
        for k, (px, py, pc) in enumerate(peers):
            for cp in small(k, 4 * px + 2 * py + pc, (px, py, pc)):
                cp.wait_recv()
                cp.wait_send()
        for j, lo, hi, d in _in_segments():
            w_ref[d:d + hi - lo, :] = buf[j, lo:hi, :]

    vm = pl.BlockSpec(memory_space=pltpu.VMEM)
    return pl.pallas_call(
        body,
        name="gather_w_in",
        in_specs=[vm] * 7,
        out_specs=[vm] * 4,
        out_shape=[jax.ShapeDtypeStruct((PW, D), BF16),
                   jax.ShapeDtypeStruct((N_DEV,) + wgu_s.shape, F32),
                   jax.ShapeDtypeStruct((N_DEV,) + conv_s.shape, F32),
                   jax.ShapeDtypeStruct((SLAB_ROWS, D), BF16)],
        scratch_shapes=[pltpu.VMEM((N_DEV, IN_ROWS, D), BF16)] + _gather_sems()
        + [pltpu.SemaphoreType.DMA((14,)), pltpu.SemaphoreType.DMA((14,))],
        compiler_params=_params(),
    )(w_it, w_gt, w_ut, w_d, w_o, wgu_s, conv_s)


def _w_in_core_reduce(dw_t):
    def body(d_ref, own_ref, sib_ref, pb_ref, g, gb, r1, send_sems, recv_sems):
        x, y, c = _position()
        chip = 2 * x + y
        for j in range(N_DEV):
            g[j, IN_W:IN_ROWS, :] = jnp.zeros((IN_ROWS - IN_W, D), F32)
        for j, lo, hi, d in _in_segments():
            g[j, lo:hi, :] = d_ref[d:d + hi - lo, :]
        for j in range(N_DEV):
            gb[j] = g[j].astype(BF16)
        copies = _stage1_copies(gb, r1, send_sems, recv_sems)
        for cp in copies:
            cp.start()
        own_ref[0] = g[2 * chip + c]
        for cp in copies:
            cp.wait_recv()
        sib_ref[0] = r1[chip]
        for k in range(1, 4):
            t = chip ^ k
            pb_ref[k - 1] = (g[2 * t + c] + r1[t].astype(F32)).astype(BF16)
        for cp in copies:
            cp.wait_send()

    vm = pl.BlockSpec(memory_space=pltpu.VMEM)
    return pl.pallas_call(
        body,
        name="w_in_core_reduce",
        in_specs=[vm],
        out_specs=[vm, vm, vm],
        out_shape=[jax.ShapeDtypeStruct((1, IN_ROWS, D), F32), jax.ShapeDtypeStruct((1, IN_ROWS, D), BF16),
                   jax.ShapeDtypeStruct((3, IN_ROWS, D), BF16)],
        scratch_shapes=[pltpu.VMEM((N_DEV, IN_ROWS, D), F32), pltpu.VMEM((N_DEV, IN_ROWS, D), BF16),
                        pltpu.VMEM((4, IN_ROWS, D), BF16), pltpu.SemaphoreType.DMA((4,)),
                        pltpu.SemaphoreType.DMA((4,))],
        compiler_params=_params(),
    )(dw_t)


def _stage1_copies(g_ref, r_ref, send_sems, recv_sems):
    x, y, c = _position()
    return [pltpu.make_async_remote_copy(
        src_ref=g_ref.at[2 * i + 1 - c], dst_ref=r_ref.at[i], send_sem=send_sems.at[i], recv_sem=recv_sems.at[i],
        device_id=(x, y, 1 - c), device_id_type=MESH) for i in range(4)]


def _ffn_core_reduce(dw3, dwb3, dw_o, dwb_o, pos_arr, dx1b, gwb):
    def body(pos_ref, g0, g1, g2, go, gb3_hbm, gbo_hbm, dx1b_ref, gwb_hbm, p0, p1, p2, po, s0, s1, s2, so, dmix_ref,
             r1f, r1o, wo, send_sems, recv_sems, wsem):
        step = pl.program_id(0)
        k = jnp.minimum(step, 2)
        x, y, c = _position()
        chip = 2 * x + y

        def copies(p):
            src = 2 * (chip ^ ((p + 1) & 3)) + 1 - c
            pairs = [(gb3_hbm.at[a, src], r1f.at[a, p]) for a in range(3)] + [(gbo_hbm.at[src], r1o.at[p])]
            return [pltpu.make_async_remote_copy(
                src_ref=s, dst_ref=d, send_sem=send_sems.at[4 * p + a], recv_sem=recv_sems.at[4 * p + a],
                device_id=(x, y, 1 - c), device_id_type=MESH) for a, (s, d) in enumerate(pairs)]

        @pl.when(step == 0)
        def _():
            for p in range(4):
                for cp in copies(p):
                    cp.start()
            loads = [pltpu.make_async_copy(gwb_hbm.at[j, pl.ds(FF_W, OUT_ROWS), :],
                                           wo.at[pl.ds(OUT_ROWS * j, OUT_ROWS), :], wsem.at[j]) for j in range(N_DEV)]
            for cp in loads:
                cp.start()
            for cp in loads:
                cp.wait()

        dmix_ref[...] = _nt(dx1b_ref[...], wo[...])

        for p in range(3):
            @pl.when(step == p)
            def _():
                for cp in copies(p):
                    cp.wait_recv()

        for a, (g, pb) in enumerate(((g0, p0), (g1, p1), (g2, p2))):
            pb[...] = (g[...] + r1f[a, k][None].astype(F32)).astype(BF16)
        po[...] = (go[...] + r1o[k][None].astype(F32)).astype(BF16)

        @pl.when(step == 3)
        def _():
            for cp in copies(3):
                cp.wait_recv()
            for a, s in enumerate((s0, s1, s2)):
                s[0] = r1f[a, 3]
            so[0] = r1o[3]
            for p in range(4):
                for cp in copies(p):
                    cp.wait_send()

    t = dx1b.shape[0]
    other = lambda s, pos: 2 * (pos[1] ^ (jnp.minimum(s, 2) + 1)) + pos[0]
    g_spec = lambda lead: pl.BlockSpec((None, 1, FF_W, D), lambda s, pos: (lead, other(s, pos), 0, 0))
    slot = lambda rows: pl.BlockSpec((1, rows, D), lambda s, pos: (jnp.minimum(s, 2), 0, 0))
    one = lambda rows: pl.BlockSpec((1, rows, D), lambda s, pos: (0, 0, 0))
    quarter = pl.BlockSpec((t // 4, D), lambda s, pos: (s, 0))
    hbm = pl.BlockSpec(memory_space=pl.ANY)
    return pl.pallas_call(
        body,
        name="ffn_core_reduce",
        grid_spec=pltpu.PrefetchScalarGridSpec(
            num_scalar_prefetch=1, grid=(4,),
            in_specs=[g_spec(0), g_spec(1), g_spec(2),
                      pl.BlockSpec((1, OUT_ROWS, D), lambda s, pos: (other(s, pos), 0, 0)), hbm, hbm, quarter, hbm],
            out_specs=[slot(FF_W), slot(FF_W), slot(FF_W), slot(OUT_ROWS),
                       one(FF_W), one(FF_W), one(FF_W), one(OUT_ROWS), quarter],
            scratch_shapes=[pltpu.VMEM((3, 4, FF_W, D), BF16), pltpu.VMEM((4, OUT_ROWS, D), BF16),
                            pltpu.VMEM((D, D), BF16), pltpu.SemaphoreType.DMA((16,)),
                            pltpu.SemaphoreType.DMA((16,)), pltpu.SemaphoreType.DMA((N_DEV,))]),
        out_shape=[jax.ShapeDtypeStruct((3, FF_W, D), BF16)] * 3 + [jax.ShapeDtypeStruct((3, OUT_ROWS, D), BF16)]
        + [jax.ShapeDtypeStruct((1, FF_W, D), BF16)] * 3 + [jax.ShapeDtypeStruct((1, OUT_ROWS, D), BF16),
                                                             jax.ShapeDtypeStruct((t, D), F32)],
        compiler_params=_params(("arbitrary",)),
    )(pos_arr, dw3, dw3, dw3, dw_o, dwb3, dwb_o, dx1b, gwb)


def _stage2_copies(p_refs, r_refs, send_sems, recv_sems):
    x, y, c = _position()
    copies = []
    for a in range(len(p_refs)):
        for k in range(1, 4):
            copies.append(pltpu.make_async_remote_copy(
                src_ref=p_refs[a].at[k - 1], dst_ref=r_refs[a].at[k - 1],
                send_sem=send_sems.at[3 * a + k - 1], recv_sem=recv_sems.at[3 * a + k - 1],
                device_id=(x ^ (k >> 1), y ^ (k & 1), c), device_id_type=MESH))
    return copies


def _finish_weights(items, pos_arr, name, nblk):
    n = len(items)
    in_specs, out_specs, out_shape, operands, wbs = [], [], [], [], []
    for g8, lead, r1, r2, w, m, v in items:
        rows, wr = g8.shape[-2], w.shape[0]
        assert rows % nblk == 0 and wr % nblk == 0 and (nblk == 1 or (rows == wr and rows % (16 * nblk) == 0))
        rb, wb = rows // nblk, wr // nblk
        if lead is not None:
            g_spec = pl.BlockSpec((None, 1, rb, D), lambda i, pos, lead=lead: (lead, 2 * pos[1] + pos[0], i, 0))
        elif g8.shape[0] == 1:
            g_spec = pl.BlockSpec((1, rb, D), lambda i, pos: (0, i, 0))
        else:
            g_spec = pl.BlockSpec((1, rb, D), lambda i, pos: (2 * pos[1] + pos[0], i, 0))
        r1_spec = pl.BlockSpec((1, rb, D), lambda i, pos: (0, i, 0))
        if w.ndim == 3:
            wblk = pl.BlockSpec((wb, 1, D), lambda i, pos: (i, 0, 0))
        else:
            wblk = pl.BlockSpec((wb, D), lambda i, pos: (i, 0))
        in_specs += [g_spec, r1_spec, pl.BlockSpec((3, rb, D), lambda i, pos: (0, i, 0)), wblk, wblk, wblk]
        out_specs += [wblk] * 4
        out_shape += [jax.ShapeDtypeStruct(w.shape, F32)] * 4
        operands += [g8, r1, r2, w, m, v]
        wbs.append(wb)

    def body(pos_ref, *refs):
        for a in range(n):
            g_ref, r1_ref, r2_ref, w_ref, m_ref, v_ref = refs[6 * a:6 * a + 6]
            g_out, d_out, m_out, v_out = refs[6 * n + 4 * a:6 * n + 4 * a + 4]
            g = g_ref[0] + r1_ref[0].astype(F32)
            for k in range(3):
                g = g + r2_ref[k].astype(F32)
            g = g[0:wbs[a], :]
            d, mn, vn = _adamw_math(_get_rows(w_ref), g, _get_rows(m_ref), _get_rows(v_ref))
            for out, val in ((g_out, g), (d_out, d), (m_out, mn), (v_out, vn)):
                _put_rows(out, val)

    return pl.pallas_call(
        body,
        name=name,
        grid_spec=pltpu.PrefetchScalarGridSpec(
            num_scalar_prefetch=1, grid=(nblk,), in_specs=in_specs, out_specs=out_specs),
        out_shape=out_shape,
        compiler_params=_params(("arbitrary",)),
    )(pos_arr, *operands)


SMALL_NAMES = ("norm1_g", "norm2_g", "norm_f_g", "b_gate", "gla_norm_g", "w_gate_up", "conv_w")
WGU_W = NQK // N_DEV
CONV_W = CW // N_DEV


def _small_adamw(sums, ws, ms, vs):
    n = len(SMALL_NAMES)

    def body(*refs):
        acc_ref = refs[0]
        w_refs, m_refs, v_refs = refs[1:1 + n], refs[1 + n:1 + 2 * n], refs[1 + 2 * n:1 + 3 * n]
        loss_ref = refs[1 + 3 * n]
        outs = refs[2 + 3 * n:]
        x, y, c = _position()
        me = 4 * x + 2 * y + c
        acc = acc_ref[...]
        loss_ref[...] = acc[3:4, NQK + DV:NQK + DV + 1]

        def my_columns(full, width):
            r = lax.broadcasted_iota(jnp.int32, (full.shape[1], width), 0)
            col = lax.broadcasted_iota(jnp.int32, (full.shape[1], width), 1)
            sel = (r == width * me + col).astype(F32)
            return _mm(full, sel, precision=HIGHEST)

        dwgu = jnp.concatenate([acc[row:row + 1, lane:lane + NQK] for row, lane in map(_wgu_slot, range(RANK))], axis=0)
        dcw = jnp.concatenate([acc[row:row + 1, lane:lane + CW] for row, lane in CONV_SLOTS], axis=0)
        grads = [acc[0:1, :], acc[1:2, :], acc[2:3, :], acc[3:4, 0:NQK], acc[3:4, NQK:NQK + DV],
                 my_columns(dwgu, WGU_W), my_columns(dcw, CONV_W)]
        for i, g in enumerate(grads):
            d, mn, vn = _adamw_math(_get_rows(w_refs[i]), g, _get_rows(m_refs[i]), _get_rows(v_refs[i]))
            for out, val in zip(outs[4 * i:4 * i + 4], (g, d, mn, vn)):
                _put_rows(out, val)

    vm = pl.BlockSpec(memory_space=pltpu.VMEM)
    out_shape = [jax.ShapeDtypeStruct((1, 1), F32)]
    for w in ws:
        out_shape += [jax.ShapeDtypeStruct(w.shape, F32)] * 4
    return pl.pallas_call(
        body,
        name="small_adamw",
        in_specs=[vm] * (1 + 3 * n),
        out_specs=[vm] * (1 + 4 * n),
        out_shape=out_shape,
        compiler_params=_params(),
    )(sums, *ws, *ms, *vs)


def kernel(x, norm1_g, w_in, w_gate_up, b_gate, gla_norm_g, conv_w, w_out, norm2_g, w_ffn_gate, w_ffn_up, w_ffn_down, norm_f_g, loss_target, m_norm1_g, m_w_in, m_w_gate_up, m_b_gate, m_gla_norm_g, m_conv_w, m_w_out, m_norm2_g, m_w_ffn_gate, m_w_ffn_up, m_w_ffn_down, m_norm_f_g, v_norm1_g, v_w_in, v_w_gate_up, v_b_gate, v_gla_norm_g, v_conv_w, v_w_out, v_norm2_g, v_w_ffn_gate, v_w_ffn_up, v_w_ffn_down, v_norm_f_g):
    xi, yi, ci = _position()
    pos_arr = jnp.stack([ci, 2 * xi + yi]).astype(jnp.int32)
    nb, s, _ = x.shape
    t = nb * s

    tr = lambda a: a[0].T
    rows_of = lambda a: a.transpose(2, 0, 1)
    conv_rows = lambda a: a.transpose(1, 0, 2)
    w_in_t, gwgu, gconv, stage = _gather_w_in(rows_of(w_in), tr(w_ffn_gate), tr(w_ffn_up), w_ffn_down[0], w_out[0],
                                              w_gate_up[0], conv_rows(conv_w))
    wgu_f = gwgu.transpose(1, 0, 2).reshape(RANK, NQK)
    conv_f = gconv.transpose(1, 2, 0, 3).reshape(CONV_K, CW)
    wgu_p = jnp.concatenate([wgu_f, jnp.zeros((A_PAD - RANK, NQK), F32)], axis=0).astype(BF16)

    x2d = x.reshape(t, D)
    tgt2d = loss_target.reshape(t, D)
    tm = 256
    tm_in = min(512, t)
    tk = min(2048, t)
    proj, z, h, gwb = _in_proj_fwd(x2d, norm1_g, w_in_t, wgu_p, b_gate, tm_in, stage)
    proj3 = proj.reshape(nb, s, PW)
    z3 = z.reshape(nb, s, NQK)
    mix3, opre3, sprev, x1, gwa = _mix_fwd(proj3, z3, gla_norm_g, conv_f, stage, x, gwb)
    mix2d = mix3.reshape(t, D)
    dx1, dx1b, adu, hb, dg2, dgf, loss_part = _ffn_fwd_bwd(
        x1.reshape(t, D), tgt2d, gwa, gwb, norm2_g, norm_f_g.reshape(1, D), tm)
    dw3, dwb3 = _dw_ffn(adu, hb, tk)
    dw3 = dw3.reshape(3, N_DEV, FF_W, D)
    dw_o, dwb_o = _tn_matmul(mix2d, dx1b, D // 2, D, tk, "dw_out", True)
    dw_o = dw_o.reshape(N_DEV, OUT_ROWS, D)
    *pb, sib_d, sib_g, sib_u, sib_o, dmix = _ffn_core_reduce(
        dw3, dwb3.reshape(3, N_DEV, FF_W, D), dw_o, dwb_o.reshape(N_DEV, OUT_ROWS, D), pos_arr, dx1b, gwb)
    g8 = [dw3, dw3, dw3, dw_o]
    leads = [0, 1, 2, None]
    tags = ("w_ffn_down", "w_ffn_gate", "w_ffn_up", "w_out")
    r1 = [sib_d, sib_g, sib_u, sib_o]
    mb = _mix_bwd(proj3, z3, sprev, opre3, dmix.reshape(nb, s, D), gla_norm_g, conv_f, wgu_p, [pb[0], pb[1], pb[3]])
    dproj3, dgng, dcw, dbg, dwgu = mb[:5]
    dproj2d = dproj3.reshape(t, PW)
    dw_in_t, r2_up = _tn_matmul(dproj2d, h, PW // 5, D, t, "dw_in", False, _stage2_rider([pb[2]]))
    r2 = [mb[5], mb[6], r2_up, mb[7]]
    g_in, r1_in, pb_in = _w_in_core_reduce(dw_in_t)
    dx, small_sums, r2_in = _in_proj_bwd(dproj2d, x2d, dx1, norm1_g, w_in_t, tm_in, pb_in,
                                         (dg2, dgf, dbg, dgng, dwgu, dcw, loss_part))

    tags = ("w_in",) + tags
    g8 = [g_in] + g8
    leads = [None] + leads
    r1 = [r1_in] + list(r1)
    r2 = [r2_in] + r2
    shard_w = (rows_of(w_in), w_ffn_down[0], tr(w_ffn_gate), tr(w_ffn_up), w_out[0])
    shard_m = (rows_of(m_w_in), m_w_ffn_down[0], tr(m_w_ffn_gate), tr(m_w_ffn_up), m_w_out[0])
    shard_v = (rows_of(v_w_in), v_w_ffn_down[0], tr(v_w_ffn_gate), tr(v_w_ffn_up), v_w_out[0])
    back = (lambda o: o.transpose(1, 2, 0), lambda o: o[None], lambda o: o.T[None], lambda o: o.T[None],
            lambda o: o[None])
    items = list(zip(g8, leads, r1, r2, shard_w, shard_m, shard_v))
    flat = list(_finish_weights(items[1:], pos_arr, "finish_ffn_out", 2))
    flat = list(_finish_weights(items[:1], pos_arr, "finish_w_in", 1)) + flat
    results = {}
    for i, (tag, to_shard) in enumerate(zip(tags, back)):
        results[tag] = [to_shard(o) for o in flat[4 * i:4 * i + 4]]

    small_w = (norm1_g, norm2_g, norm_f_g.reshape(1, D), b_gate, gla_norm_g, w_gate_up[0], conv_rows(conv_w))
    small_m = (m_norm1_g, m_norm2_g, m_norm_f_g.reshape(1, D), m_b_gate, m_gla_norm_g, m_w_gate_up[0],
               conv_rows(m_conv_w))
    small_v = (v_norm1_g, v_norm2_g, v_norm_f_g.reshape(1, D), v_b_gate, v_gla_norm_g, v_w_gate_up[0],
               conv_rows(v_conv_w))
    so = _small_adamw(small_sums, small_w, small_m, small_v)
    loss = so[0].reshape(())
    to_shape = {"norm_f_g": lambda o: o.reshape(D), "w_gate_up": lambda o: o[None],
                "conv_w": lambda o: o.transpose(1, 0, 2)}
    for i, name in enumerate(SMALL_NAMES):
        results[name] = [to_shape.get(name, lambda o: o)(o) for o in so[1 + 4 * i:5 + 4 * i]]

    names = ("norm1_g", "w_in", "w_gate_up", "b_gate", "gla_norm_g", "conv_w", "w_out", "norm2_g",
             "w_ffn_gate", "w_ffn_up", "w_ffn_down", "norm_f_g")
    outs = [loss, dx.reshape(nb, s, D)]
    for kind in range(4):
        for name in names:
            outs.append(results[name][kind])
    return tuple(outs)
```

```python
import jax
import jax.numpy as jnp
from jax import lax
from jax.experimental import pallas as pl
from jax.experimental.pallas import tpu as pltpu

F32 = jnp.float32
BF16 = jnp.bfloat16
HIGHEST = lax.Precision.HIGHEST
MESH = pl.DeviceIdType.MESH

N_DEV = 8
D = 1024
DFF = 2816
HEADS = 4
DK = 64
DV = 128
NQK = HEADS * DK
NV = HEADS * DV
RANK = 16
CHUNK = 64
CW = 512
CONV_K = 3
IN_COLS = 3088
EPS = 1e-6
INV_GATE_NORM = 1.0 / 16.0
Q_SCALE = DK ** -0.5

PW = 3200
OQ, OK_, OV, OG, OCB, OCC, OCH, OA = 0, 256, 512, 1024, 1536, 2048, 2560, 3072
A_PAD = 128

ADAM_LR = 0.001
ADAM_B1 = 0.9
ADAM_B2 = 0.999
ADAM_EPS = 1e-08
ADAM_WD = 0.01
ADAM_STEP = 10

IN_W = IN_COLS // N_DEV
IN_ROWS = 400
FF_W = DFF // N_DEV
OUT_ROWS = D // N_DEV
SLAB_IN = 0
SLAB_G = SLAB_IN + IN_ROWS
SLAB_U = SLAB_G + FF_W
SLAB_D = SLAB_U + FF_W
SLAB_O = SLAB_D + FF_W
SLAB_ROWS = SLAB_O + OUT_ROWS

VMEM_LIMIT = 56 * 1024 * 1024


def _params(sem=None, vmem=VMEM_LIMIT):
    return pltpu.CompilerParams(dimension_semantics=sem, vmem_limit_bytes=vmem)


def _nt(a, b):
    return lax.dot_general(a, b, (((1,), (1,)), ((), ())), preferred_element_type=F32)


def _tn(a, b, precision=None):
    return lax.dot_general(a, b, (((0,), (0,)), ((), ())), preferred_element_type=F32, precision=precision)


def _mm(a, b, precision=None):
    return jnp.dot(a, b, preferred_element_type=F32, precision=precision)


def _in_segments():
    segs = []
    for j in range(N_DEV):
        lo, hi = IN_W * j, IN_W * (j + 1)
        cuts = sorted({lo, hi} | {c for c in (OCB, OCB + RANK) if lo < c < hi})
        for a, b in zip(cuts[:-1], cuts[1:]):
            if a < OCB:
                d = a
            elif a < OCB + RANK:
                d = OA + (a - OCB)
            else:
                d = a - RANK
            segs.append((j, a - lo, b - lo, d))
    return segs


def _in_proj_fwd(x2d, g1, w_in_t, wgu_p, b_gate, tm, stage):
    t = x2d.shape[0]
    nt = t // tm
    g_rows = SLAB_ROWS - SLAB_D

    def body(x_ref, g_ref, w_ref, wgu_ref, bg_ref, stage_hbm, proj_ref, z_ref, h_ref, gwb_ref,
             send_sems, recv_sems, local_sem):
        gargs = (stage_hbm, SLAB_D, g_rows, gwb_ref, send_sems, recv_sems, local_sem)

        @pl.when(pl.program_id(0) == 0)
        def _():
            _gather_start(*gargs)

        @pl.when(pl.program_id(0) == RELAY_AT * nt // 8)
        def _():
            _gather_relay(*gargs)

        x = x_ref[...]
        r = lax.rsqrt(jnp.mean(x * x, axis=-1, keepdims=True) + EPS)
        h = ((x * r) * g_ref[...]).astype(BF16)
        h_ref[...] = h
        proj = _nt(h, w_ref[...])
        proj_ref[...] = proj
        pa = proj[:, OA:OA + A_PAD].astype(BF16)
        z_ref[...] = _mm(pa, wgu_ref[...]) + bg_ref[...]

        @pl.when(pl.program_id(0) == nt - 1)
        def _():
            _gather_finish(*gargs)

    return pl.pallas_call(
        body,
        name="in_proj_fwd",
        grid=(t // tm,),
        in_specs=[
            pl.BlockSpec((tm, D), lambda i: (i, 0)),
            pl.BlockSpec((1, D), lambda i: (0, 0)),
            pl.BlockSpec((PW, D), lambda i: (0, 0)),
            pl.BlockSpec((A_PAD, NQK), lambda i: (0, 0)),
            pl.BlockSpec((1, NQK), lambda i: (0, 0)),
            pl.BlockSpec(memory_space=pl.ANY),
        ],
        out_specs=[
            pl.BlockSpec((tm, PW), lambda i: (i, 0)),
            pl.BlockSpec((tm, NQK), lambda i: (i, 0)),
            pl.BlockSpec((tm, D), lambda i: (i, 0)),
            pl.BlockSpec(memory_space=pl.ANY),
        ],
        out_shape=[
            jax.ShapeDtypeStruct((t, PW), F32),
            jax.ShapeDtypeStruct((t, NQK), F32),
            jax.ShapeDtypeStruct((t, D), BF16),
            jax.ShapeDtypeStruct((N_DEV, g_rows, D), BF16),
        ],
        scratch_shapes=_gather_sems(),
        compiler_params=_params(("arbitrary",)),
    )(x2d, g1, w_in_t, wgu_p, b_gate, stage)


def _head_masks():
    lane = lax.broadcasted_iota(jnp.int32, (1, NQK), 1)
    return [(lane >= DK * h) & (lane < DK * (h + 1)) for h in range(HEADS)]


def _split_bf16(x, n):
    parts = []
    for _ in range(n):
        p = x.astype(BF16)
        parts.append(p)
        x = x - p.astype(F32)
    return parts


def _chunk_fwd_parts(q, k, z, tril16):
    la = (jnp.minimum(z, 0.0) - jnp.log1p(jnp.exp(-jnp.abs(z)))) * INV_GATE_NORM
    la_parts = _split_bf16(la, 3)
    bc = _mm(tril16, la_parts[0]) + _mm(tril16, la_parts[1]) + _mm(tril16, la_parts[2])
    bl = bc[CHUNK - 1:CHUNK, :]
    eb = jnp.exp(bc)
    enb = jnp.exp(-bc)
    ekl = jnp.exp(bl - bc)
    qi = (q * Q_SCALE) * eb
    ki = k * enb
    ks = k * ekl
    ones16 = jnp.ones((CHUNK, DV), BF16)
    decb = jnp.exp(_tn(la_parts[0], ones16) + _tn(la_parts[1], ones16) + _tn(la_parts[2], ones16))
    return la, eb, enb, ekl, qi, ki, ks, decb


def _stack_heads(a, masks):
    return jnp.concatenate([jnp.where(m, a, 0.0) for m in masks], axis=0)


def _merge_heads(blocks, masks):
    out = blocks[HEADS - 1]
    for h in range(HEADS - 2, -1, -1):
        out = jnp.where(masks[h], blocks[h], out)
    return out


def _causal_stack_mask():
    row = lax.broadcasted_iota(jnp.int32, (HEADS * CHUNK, CHUNK), 0)
    col = lax.broadcasted_iota(jnp.int32, (HEADS * CHUNK, CHUNK), 1)
    return (row & (CHUNK - 1)) >= col


def _conv_taps(u, uprev):
    row = lax.broadcasted_iota(jnp.int32, u.shape, 0)
    u1 = jnp.where(row < 1, pltpu.roll(uprev, 1, 0), pltpu.roll(u, 1, 0))
    u2 = jnp.where(row < 2, pltpu.roll(uprev, 2, 0), pltpu.roll(u, 2, 0))
    return u1, u2


def _mix_fwd(proj3, z3, gng, conv_w, stage, x3, gwb):
    nb, s, _ = proj3.shape
    nc = s // CHUNK
    g_rows = SLAB_D - SLAB_G

    def body(p_ref, z_ref, gng_ref, cw_ref, stage_hbm, x_ref, gwb_hbm, mix_ref, o_ref, sprev_ref, x1_ref, gwa_ref,
             s_ref, uprev_ref, wo, wsem, send_sems, recv_sems, local_sem):
        n = pl.program_id(0)
        gargs = (stage_hbm, SLAB_G, g_rows, gwa_ref, send_sems, recv_sems, local_sem)

        @pl.when(n == 0)
        def _():
            _gather_start(*gargs)
            loads = [pltpu.make_async_copy(gwb_hbm.at[j, pl.ds(FF_W, OUT_ROWS), :],
                                           wo.at[pl.ds(OUT_ROWS * j, OUT_ROWS), :], wsem.at[j]) for j in range(N_DEV)]
            for cp in loads:
                cp.start()
            s_ref[...] = jnp.zeros_like(s_ref)
            uprev_ref[...] = jnp.zeros_like(uprev_ref)
            for cp in loads:
                cp.wait()

        @pl.when(n == RELAY_AT * nc // 8)
        def _():
            _gather_relay(*gargs)

        r_i = lax.broadcasted_iota(jnp.int32, (CHUNK, CHUNK), 0)
        c_i = lax.broadcasted_iota(jnp.int32, (CHUNK, CHUNK), 1)
        tril16 = (r_i >= c_i).astype(BF16)
        masks = _head_masks()
        cmask = _causal_stack_mask()
        gg = gng_ref[...]
        for b in range(nb):
            q = p_ref[b, :, OQ:OQ + NQK]
            k = p_ref[b, :, OK_:OK_ + NQK]
            _, _, _, _, qi, ki, ks, decb = _chunk_fwd_parts(q, k, z_ref[b], tril16)
            qs = _stack_heads(qi, masks).astype(BF16)
            sc = jnp.where(cmask, _nt(qs, ki.astype(BF16)), 0.0).astype(BF16)
            st = s_ref[b]
            sprev_ref[b, 0] = st
            o_inter = _mm(qs, st.astype(BF16))
            v16 = p_ref[b, :, OV:OV + NV].astype(BF16)
            kv = _tn(ks.astype(BF16), v16)
            for h in range(HEADS):
                rows = slice(CHUNK * h, CHUNK * (h + 1))
                cols = slice(DV * h, DV * (h + 1))
                o = _mm(sc[rows], v16[:, cols]) + o_inter[rows]
                o_ref[b, :, cols] = o
                r = lax.rsqrt(jnp.mean(o * o, axis=-1, keepdims=True) + EPS)
                on = (o * r) * gg
                g = p_ref[b, :, OG + DV * h:OG + DV * (h + 1)]
                mix_ref[b, :, cols] = (on * (g * jax.nn.sigmoid(g))).astype(BF16)
                s_ref[b, rows, :] = decb[rows] * st[rows] + kv[rows, cols]
            u = p_ref[b, :, OCC:OCC + CW] * p_ref[b, :, OCH:OCH + CW]
            u1, u2 = _conv_taps(u, uprev_ref[b])
            yc = cw_ref[0:1, :] * u2 + cw_ref[1:2, :] * u1 + cw_ref[2:3, :] * u
            mix_ref[b, :, NV:NV + CW] = (p_ref[b, :, OCB:OCB + CW] * yc).astype(BF16)
            uprev_ref[b] = u
        mixed = _mm(jnp.concatenate([mix_ref[b] for b in range(nb)], axis=0), wo[...])
        for b in range(nb):
            x1_ref[b] = x_ref[b] + mixed[CHUNK * b:CHUNK * (b + 1)]

        @pl.when(n == nc - 1)
        def _():
            _gather_finish(*gargs)

    return pl.pallas_call(
        body,
        name="mix_fwd",
        grid=(nc,),
        in_specs=[
            pl.BlockSpec((nb, CHUNK, PW), lambda n: (0, n, 0)),
            pl.BlockSpec((nb, CHUNK, NQK), lambda n: (0, n, 0)),
            pl.BlockSpec((1, DV), lambda n: (0, 0)),
            pl.BlockSpec((CONV_K, CW), lambda n: (0, 0)),
            pl.BlockSpec(memory_space=pl.ANY),
            pl.BlockSpec((nb, CHUNK, D), lambda n: (0, n, 0)),
            pl.BlockSpec(memory_space=pl.ANY),
        ],
        out_specs=[
            pl.BlockSpec((nb, CHUNK, D), lambda n: (0, n, 0)),
            pl.BlockSpec((nb, CHUNK, NV), lambda n: (0, n, 0)),
            pl.BlockSpec((nb, 1, NQK, DV), lambda n: (0, n, 0, 0)),
            pl.BlockSpec((nb, CHUNK, D), lambda n: (0, n, 0)),
            pl.BlockSpec(memory_space=pl.ANY),
        ],
        out_shape=[
            jax.ShapeDtypeStruct((nb, s, D), BF16),
            jax.ShapeDtypeStruct((nb, s, NV), F32),
            jax.ShapeDtypeStruct((nb, nc, NQK, DV), F32),
            jax.ShapeDtypeStruct((nb, s, D), F32),
            jax.ShapeDtypeStruct((N_DEV, g_rows, D), BF16),
        ],
        scratch_shapes=[pltpu.VMEM((nb, NQK, DV), F32), pltpu.VMEM((nb, CHUNK, CW), F32),
                        pltpu.VMEM((D, D), BF16), pltpu.SemaphoreType.DMA((N_DEV,))] + _gather_sems(),
        compiler_params=_params(("arbitrary",)),
    )(proj3, z3, gng, conv_w, stage, x3, gwb)


def _ffn_fwd_bwd(x1_2d, tgt2d, gwa, gwb, g2, gf, tm):
    t = x1_2d.shape[0]

    def body(x1_ref, tgt_ref, g2_ref, gf_ref, gwa_hbm, gwb_hbm,
             dx1_ref, dx1b_ref, adu_ref, hb_ref, dg2_ref, dgf_ref, loss_ref,
             wg, wu, wd, wsem):
        i = pl.program_id(0)

        def weight_copies(n, dst, src, off, rows):
            return [pltpu.make_async_copy(src.at[j, pl.ds(off, rows), :], dst.at[pl.ds(rows * j, rows), :],
                                          wsem.at[N_DEV * n + j]) for j in range(N_DEV)]

        loads = (weight_copies(0, wg, gwa_hbm, 0, FF_W), weight_copies(1, wu, gwa_hbm, FF_W, FF_W),
                 weight_copies(2, wd, gwb_hbm, 0, FF_W))

        @pl.when(i == 0)
        def _():
            for group in loads:
                for cp in group:
                    cp.start()
            dg2_ref[...] = jnp.zeros_like(dg2_ref)
            dgf_ref[...] = jnp.zeros_like(dgf_ref)
            loss_ref[...] = jnp.zeros_like(loss_ref)
            for group in loads:
                for cp in group:
                    cp.wait()

        g2v = g2_ref[...]
        gfv = gf_ref[...]
        x1 = x1_ref[...]
        r2 = lax.rsqrt(jnp.mean(x1 * x1, axis=-1, keepdims=True) + EPS)
        n2 = x1 * r2
        h2 = (n2 * g2v).astype(BF16)
        hb_ref[1] = h2
        gate = _nt(h2, wg[...])
        up = _nt(h2, wu[...])
        sg = jax.nn.sigmoid(gate)
        sil = gate * sg
        act = (sil * up).astype(BF16)
        adu_ref[0] = act
        x2 = x1 + _mm(act, wd[...])
        rf = lax.rsqrt(jnp.mean(x2 * x2, axis=-1, keepdims=True) + EPS)
        nf = x2 * rf
        err = nf * gfv - tgt_ref[...]
        loss_ref[...] += 0.5 * jnp.sum(jnp.mean(err * err, axis=-1, keepdims=True))
        dy = err * (1.0 / D)
        dgf_ref[...] += jnp.sum(dy * nf, axis=0, keepdims=True)
        dnf = dy * gfv
        dx2 = rf * (dnf - nf * jnp.mean(dnf * nf, axis=-1, keepdims=True))
        dx2b = dx2.astype(BF16)
        hb_ref[0] = dx2b
        dact = _nt(dx2b, wd[...])
        dup = (dact * sil).astype(BF16)
        dgate = ((dact * up) * (sg * (1.0 + gate * (1.0 - sg)))).astype(BF16)
        adu_ref[2] = dup
        adu_ref[1] = dgate
        dh2 = _mm(dgate, wg[...]) + _mm(dup, wu[...])
        dg2_ref[...] += jnp.sum(dh2 * n2, axis=0, keepdims=True)
        dn2 = dh2 * g2v
        dx1 = dx2 + r2 * (dn2 - n2 * jnp.mean(dn2 * n2, axis=-1, keepdims=True))
        dx1_ref[...] = dx1
        dx1b_ref[...] = dx1.astype(BF16)

    tile = lambda w: pl.BlockSpec((tm, w), lambda i: (i, 0))
    vec = pl.BlockSpec((1, D), lambda i: (0, 0))
    hbm = pl.BlockSpec(memory_space=pl.ANY)
    return pl.pallas_call(
        body,
        name="ffn_fwd_bwd",
        grid=(t // tm,),
        in_specs=[tile(D), tile(D), vec, vec, hbm, hbm],
        out_specs=[tile(D), tile(D), pl.BlockSpec((3, tm, DFF), lambda i: (0, i, 0)),
                   pl.BlockSpec((2, tm, D), lambda i: (0, i, 0)), vec, vec,
                   pl.BlockSpec((1, 128), lambda i: (0, 0))],
        out_shape=[
            jax.ShapeDtypeStruct((t, D), F32),
            jax.ShapeDtypeStruct((t, D), BF16),
            jax.ShapeDtypeStruct((3, t, DFF), BF16),
            jax.ShapeDtypeStruct((2, t, D), BF16),
            jax.ShapeDtypeStruct((1, D), F32),
            jax.ShapeDtypeStruct((1, D), F32),
            jax.ShapeDtypeStruct((1, 128), F32),
        ],
        scratch_shapes=[pltpu.VMEM((DFF, D), BF16), pltpu.VMEM((DFF, D), BF16), pltpu.VMEM((DFF, D), BF16),
                        pltpu.SemaphoreType.DMA((3 * N_DEV,))],
        compiler_params=_params(("arbitrary",)),
    )(x1_2d, tgt2d, g2, gf, gwa, gwb)


def _stage2_rider(pbs):
    return dict(inputs=list(pbs), out_shape=[jax.ShapeDtypeStruct((2,) + p.shape[1:], BF16) for p in pbs],
                scratch=_stage2_scratch(pbs))


def _tn_matmul(a, b, bm, bn, tk, name, with_bf16, rider=None):
    t, m = a.shape
    n = b.shape[1]
    nk = t // tk
    nout = 2 if with_bf16 else 1
    grid = (m // bm, n // bn, nk)
    steps = grid[0] * grid[1] * nk
    r_in = [] if rider is None else rider["inputs"]
    r_out = [] if rider is None else rider["out_shape"]

    def body(a_ref, b_ref, *rest):
        ins, outs = rest[:len(r_in)], rest[len(r_in):len(r_in) + nout]
        r_outs, scratch = rest[len(r_in) + nout:len(r_in) + nout + len(r_out)], rest[len(r_in) + nout + len(r_out):]
        o_ref = outs[0]
        i, j, k = pl.program_id(0), pl.program_id(1), pl.program_id(2)
        step = (i * grid[1] + j) * nk + k
        if rider is not None:
            @pl.when(step == 0)
            def _():
                _stage2_start(ins, r_outs, scratch)

            @pl.when(step == RELAY_AT * steps // 8)
            def _():
                _stage2_combine(ins, r_outs, scratch)

        @pl.when(k == 0)
        def _():
            o_ref[...] = jnp.zeros_like(o_ref)

        o_ref[...] += _tn(a_ref[...].astype(BF16), b_ref[...].astype(BF16))
        if with_bf16:
            @pl.when(k == nk - 1)
            def _():
                outs[1][...] = o_ref[...].astype(BF16)
        if rider is not None:
            @pl.when(step == steps - 1)
            def _():
                _stage2_finish(ins, r_outs, scratch)

    out_blk = pl.BlockSpec((bm, bn), lambda i, j, k: (i, j))
    hbm = pl.BlockSpec(memory_space=pl.ANY)
    out_shape = [jax.ShapeDtypeStruct((m, n), F32)] + ([jax.ShapeDtypeStruct((m, n), BF16)] if with_bf16 else [])
    res = pl.pallas_call(
        body,
        name=name,
        grid=grid,
        in_specs=[pl.BlockSpec((tk, bm), lambda i, j, k: (k, i)), pl.BlockSpec((tk, bn), lambda i, j, k: (k, j))]
        + [hbm] * len(r_in),
        out_specs=[out_blk] * nout + [hbm] * len(r_out),
        out_shape=out_shape + list(r_out),
        scratch_shapes=[] if rider is None else rider["scratch"],
        compiler_params=_params(("parallel", "parallel", "arbitrary") if rider is None
                                else ("arbitrary", "arbitrary", "arbitrary")),
    )(a, b, *r_in)
    return res[0] if len(res) == 1 else res


def _dw_ffn(adu, hb, tk):
    _, t, _ = adu.shape
    bm = DFF // 2
    nk = t // tk

    def body(a_ref, b_ref, o_ref, ob_ref):
        k = pl.program_id(2)

        @pl.when(k == 0)
        def _():
            o_ref[...] = jnp.zeros_like(o_ref)

        o_ref[...] += _tn(a_ref[...], b_ref[...])

        @pl.when(k == nk - 1)
        def _():
            ob_ref[...] = o_ref[...].astype(BF16)

    out_blk = pl.BlockSpec((None, bm, D), lambda p, i, k: (p, i, 0))
    return pl.pallas_call(
        body,
        name="dw_ffn",
        grid=(3, DFF // bm, nk),
        in_specs=[pl.BlockSpec((None, tk, bm), lambda p, i, k: (p, k, i)),
                  pl.BlockSpec((None, tk, D), lambda p, i, k: (jnp.minimum(p, 1), k, 0))],
        out_specs=[out_blk, out_blk],
        out_shape=[jax.ShapeDtypeStruct((3, DFF, D), F32), jax.ShapeDtypeStruct((3, DFF, D), BF16)],
        compiler_params=_params(("arbitrary", "arbitrary", "arbitrary")),
    )(adu, hb)


def _mix_bwd(proj3, z3, sprev, opre3, dmix3, gng, conv_w, wgu_p, pbs):
    nb, s, _ = proj3.shape
    nc = s // CHUNK
    na = len(pbs)

    def body(*refs):
        (p_ref, pprev_ref, z_ref, sp_ref, o_ref, dm_ref, gng_ref, cw_ref, wgu_ref) = refs[:9]
        pb_refs = refs[9:9 + na]
        (dproj_ref, dgng_ref, dcw_ref, dbg_ref, dwgu_ref) = refs[9 + na:14 + na]
        r2_refs = refs[14 + na:14 + 2 * na]
        ds_ref, dycn_ref = refs[14 + 2 * na:16 + 2 * na]
        stage2 = (pb_refs, r2_refs, refs[16 + 2 * na:])
        step = pl.program_id(0)
        n = nc - 1 - step

        @pl.when(step == 0)
        def _():
            _stage2_start(*stage2)
            ds_ref[...] = jnp.zeros_like(ds_ref)
            dycn_ref[...] = jnp.zeros_like(dycn_ref)
            dgng_ref[...] = jnp.zeros_like(dgng_ref)
            dcw_ref[...] = jnp.zeros_like(dcw_ref)
            dbg_ref[...] = jnp.zeros_like(dbg_ref)
            dwgu_ref[...] = jnp.zeros_like(dwgu_ref)

        @pl.when(step == RELAY_AT * nc // 8)
        def _():
            _stage2_combine(*stage2)

        r_i = lax.broadcasted_iota(jnp.int32, (CHUNK, CHUNK), 0)
        c_i = lax.broadcasted_iota(jnp.int32, (CHUNK, CHUNK), 1)
        tril16 = (r_i >= c_i).astype(BF16)
        triu16 = (r_i <= c_i).astype(BF16)
        causal = r_i >= c_i
        masks = _head_masks()
        cmask = _causal_stack_mask()
        gg = gng_ref[...]
        last_row = lax.broadcasted_iota(jnp.int32, (CHUNK, NQK), 0) == CHUNK - 1
        ones_r = jnp.ones((16, DV), BF16)
        has_prev = (n > 0).astype(F32)
        for b in range(nb):
            q = p_ref[b, :, OQ:OQ + NQK]
            k = p_ref[b, :, OK_:OK_ + NQK]
            z = z_ref[b]
            _, eb, enb, ekl, qi, ki, ks, decb = _chunk_fwd_parts(q, k, z, tril16)
            qi16 = qi.astype(BF16)
            ki16 = ki.astype(BF16)
            qs = _stack_heads(qi, masks).astype(BF16)
            sc = jnp.where(cmask, _nt(qs, ki16), 0.0).astype(BF16)
            st = sp_ref[b, 0]
            st16 = st.astype(BF16)
            dsn = ds_ref[b]
            dsn16 = dsn.astype(BF16)
            v16 = p_ref[b, :, OV:OV + NV].astype(BF16)
            do16 = []
            dgng = jnp.zeros((1, DV), F32)
            for h in range(HEADS):
                cols = slice(DV * h, DV * (h + 1))
                o = o_ref[b, :, cols]
                r = lax.rsqrt(jnp.mean(o * o, axis=-1, keepdims=True) + EPS)
                nh = o * r
                g = p_ref[b, :, OG + DV * h:OG + DV * (h + 1)]
                sg = jax.nn.sigmoid(g)
                dog = dm_ref[b, :, cols]
                dproj_ref[b, :, OG + DV * h:OG + DV * (h + 1)] = (
                    (dog * (nh * gg)) * (sg * (1.0 + g * (1.0 - sg)))).astype(BF16)
                don = dog * (g * sg)
                dgng = dgng + jnp.sum(don * nh, axis=0, keepdims=True)
                dn = don * gg
                do = r * (dn - nh * jnp.mean(dn * nh, axis=-1, keepdims=True))
                do16.append(do.astype(BF16))
            dgng_ref[...] += dgng
            do_rows = jnp.concatenate(do16, axis=0)
            v_rows = jnp.concatenate([v16[:, DV * h:DV * (h + 1)] for h in range(HEADS)], axis=0)
            dp16 = [jnp.where(causal, _nt(do16[h], v16[:, DV * h:DV * (h + 1)]), 0.0).astype(BF16)
                    for h in range(HEADS)]
            ks_dsn = _mm(_stack_heads(ks, masks).astype(BF16), dsn16)
            do_st = _nt(do_rows, st16)
            v_dsn = _nt(v_rows, dsn16)
            dp_ki = _mm(jnp.concatenate(dp16, axis=0), ki16)
            q_do = _tn(qi16, jnp.concatenate(do16, axis=1))
            dki_h = []
            for h in range(HEADS):
                rows = slice(CHUNK * h, CHUNK * (h + 1))
                cols = slice(DV * h, DV * (h + 1))
                dv = _tn(sc[rows], do16[h]) + ks_dsn[rows]
                dproj_ref[b, :, OV + DV * h:OV + DV * (h + 1)] = dv.astype(BF16)
                dki_h.append(_tn(dp16[h], qi16))
                ds_ref[b, rows, :] = decb[rows] * dsn[rows] + q_do[rows, cols]
            blocks = lambda a: [a[CHUNK * h:CHUNK * (h + 1)] for h in range(HEADS)]
            dqi = _merge_heads(blocks(dp_ki + do_st), masks)
            dki = _merge_heads(dki_h, masks)
            dks = _merge_heads(blocks(v_dsn), masks)
            dproj_ref[b, :, OQ:OQ + NQK] = (dqi * (Q_SCALE * eb)).astype(BF16)
            dproj_ref[b, :, OK_:OK_ + NQK] = (dki * enb + dks * ekl).astype(BF16)
            dks_ks = dks * ks
            db = dqi * qi - dki * ki - dks_ks
            sd = _split_bf16(dsn * st * decb, 2)
            dbl = jnp.sum(dks_ks, axis=0, keepdims=True) + (_nt(ones_r, sd[0]) + _nt(ones_r, sd[1]))[0:1, :]
            db = db + jnp.where(last_row, dbl, 0.0)
            db_parts = _split_bf16(db, 3)
            dla = _mm(triu16, db_parts[0]) + _mm(triu16, db_parts[1]) + _mm(triu16, db_parts[2])
            dz = (dla * INV_GATE_NORM) * (1.0 / (1.0 + jnp.exp(z)))
            dbg_ref[...] += jnp.sum(dz, axis=0, keepdims=True)
            dz16 = dz.astype(BF16)
            pa16 = p_ref[b, :, OA:OA + A_PAD].astype(BF16)
            dwgu_ref[...] += _tn(pa16, dz16)
            dproj_ref[b, :, OA:OA + A_PAD] = _nt(dz16, wgu_ref[...]).astype(BF16)
            cb = p_ref[b, :, OCB:OCB + CW]
            cc = p_ref[b, :, OCC:OCC + CW]
            ch = p_ref[b, :, OCH:OCH + CW]
            u = cc * ch
            uprev = (pprev_ref[b, :, 0:CW] * pprev_ref[b, :, CW:2 * CW]) * has_prev
            u1, u2 = _conv_taps(u, uprev)
            w0 = cw_ref[0:1, :]
            w1 = cw_ref[1:2, :]
            w2 = cw_ref[2:3, :]
            yc = w0 * u2 + w1 * u1 + w2 * u
            doc = dm_ref[b, :, NV:NV + CW]
            dproj_ref[b, :, OCB:OCB + CW] = (doc * yc).astype(BF16)
            dyc = doc * cb
            dycn = dycn_ref[b]
            row = lax.broadcasted_iota(jnp.int32, dyc.shape, 0)
            d1 = jnp.where(row >= CHUNK - 1, pltpu.roll(dycn, CHUNK - 1, 0), pltpu.roll(dyc, CHUNK - 1, 0))
            d2 = jnp.where(row >= CHUNK - 2, pltpu.roll(dycn, CHUNK - 2, 0), pltpu.roll(dyc, CHUNK - 2, 0))
            du = w2 * dyc + w1 * d1 + w0 * d2
            dproj_ref[b, :, OCC:OCC + CW] = (du * ch).astype(BF16)
            dproj_ref[b, :, OCH:OCH + CW] = (du * cc).astype(BF16)
            dcw_ref[0:1, :] += jnp.sum(dyc * u2, axis=0, keepdims=True)
            dcw_ref[1:2, :] += jnp.sum(dyc * u1, axis=0, keepdims=True)
            dcw_ref[2:3, :] += jnp.sum(dyc * u, axis=0, keepdims=True)
            dycn_ref[b] = dyc

        @pl.when(step == nc - 1)
        def _():
            _stage2_finish(*stage2)

    rev =lambda w: pl.BlockSpec((nb, CHUNK, w), lambda i: (0, nc - 1 - i, 0))
    const = lambda r, c: pl.BlockSpec((r, c), lambda i: (0, 0))
    hbm = pl.BlockSpec(memory_space=pl.ANY)
    return pl.pallas_call(
        body,
        name="mix_bwd",
        grid=(nc,),
        in_specs=[
            rev(PW),
            pl.BlockSpec((nb, CHUNK, 2 * CW), lambda i: (0, jnp.maximum(nc - 2 - i, 0), OCC // (2 * CW))),
            rev(NQK),
            pl.BlockSpec((nb, 1, NQK, DV), lambda i: (0, nc - 1 - i, 0, 0)),
            rev(NV),
            rev(D),
            const(1, DV),
            const(CONV_K, CW),
            const(A_PAD, NQK),
        ] + [hbm] * na,
        out_specs=[rev(PW), const(1, DV), const(8, CW), const(1, NQK), const(A_PAD, NQK)] + [hbm] * na,
        out_shape=[
            jax.ShapeDtypeStruct((nb, s, PW), BF16),
            jax.ShapeDtypeStruct((1, DV), F32),
            jax.ShapeDtypeStruct((8, CW), F32),
            jax.ShapeDtypeStruct((1, NQK), F32),
            jax.ShapeDtypeStruct((A_PAD, NQK), F32),
        ] + [jax.ShapeDtypeStruct((2,) + p.shape[1:], BF16) for p in pbs],
        scratch_shapes=[pltpu.VMEM((nb, NQK, DV), F32), pltpu.VMEM((nb, CHUNK, CW), F32)] + _stage2_scratch(pbs),
        compiler_params=_params(("arbitrary",)),
    )(proj3, proj3, z3, sprev, opre3, dmix3, gng, conv_w, wgu_p, *pbs)


SMALL_PACK_ROWS = 16


def _wgu_slot(r):
    return 4 + r // 4, NQK * (r % 4)


CONV_SLOTS = ((8, 0), (8, CW), (9, 0))


def _in_proj_bwd(dproj2d, x2d, dx1, g1, w_in_t, tm, pb, small_parts):
    t = x2d.shape[0]
    nt = t // tm

    def body(dp_ref, x_ref, dx1_ref, g_ref, w_ref, pb_ref, dg2, dgf, dbg, dgng, dwgu, dcw, lp,
             dx_ref, sums_ref, r2_ref, dg1_acc, pack, gbuf, pack1, gbuf1, ssend, srecv, ssend1, srecv1, *scratch2):
        stage2 = ([pb_ref], [r2_ref], scratch2)
        x, y, c = _position()
        me = 4 * x + 2 * y + c
        flips = [(k >> 2, (k >> 1) & 1, k & 1) for k in range(1, N_DEV)]
        peers = [(x ^ fx, y ^ fy, c ^ fc) for fx, fy, fc in flips]

        def small_copies(src, dst, send, recv, arrivals):
            return [pltpu.make_async_remote_copy(
                src_ref=src, dst_ref=dst.at[4 * px + 2 * py + pc if arrivals else me],
                send_sem=send.at[k], recv_sem=recv.at[k], device_id=(px, py, pc), device_id_type=MESH)
                for k, (px, py, pc) in enumerate(peers)]

        @pl.when(pl.program_id(0) == 0)
        def _():
            _stage2_start(*stage2)
            dg1_acc[...] = jnp.zeros_like(dg1_acc)
            pack[...] = jnp.zeros_like(pack)
            pack[1:2, :] = dg2[...]
            pack[2:3, :] = dgf[...]
            pack[3:4, 0:NQK] = dbg[...]
            pack[3:4, NQK:NQK + DV] = dgng[...]
            pack[3:4, NQK + DV:NQK + 2 * DV] = lp[...]
            for r in range(RANK):
                row, lane = _wgu_slot(r)
                pack[row:row + 1, lane:lane + NQK] = dwgu[r:r + 1, :]
            for r, (row, lane) in enumerate(CONV_SLOTS):
                pack[row:row + 1, lane:lane + CW] = dcw[r:r + 1, :]
            for cp in small_copies(pack, gbuf, ssend, srecv, False):
                cp.start()
            gbuf[me] = pack[...]

        @pl.when(pl.program_id(0) == RELAY_AT * nt // 8)
        def _():
            _stage2_combine(*stage2)

        xv = x_ref[...]
        r = lax.rsqrt(jnp.mean(xv * xv, axis=-1, keepdims=True) + EPS)
        n1 = xv * r
        dh = _mm(dp_ref[...], w_ref[...])
        dg1_acc[...] += jnp.sum(dh * n1, axis=0, keepdims=True)
        dn = dh * g_ref[...]
        dx_ref[...] = dx1_ref[...] + r * (dn - n1 * jnp.mean(dn * n1, axis=-1, keepdims=True))

        @pl.when(pl.program_id(0) == nt - 1)
        def _():
            pack1[...] = jnp.zeros_like(pack1)
            pack1[0:1, :] = dg1_acc[...]
            for cp in small_copies(pack1, gbuf1, ssend1, srecv1, False):
                cp.start()
            gbuf1[me] = pack1[...]
            _stage2_finish(*stage2)
            for src, dst, send, recv in ((pack, gbuf, ssend, srecv), (pack1, gbuf1, ssend1, srecv1)):
                for cp in small_copies(src, dst, send, recv, True):
                    cp.wait_recv()
                    cp.wait_send()
            acc = gbuf[0]
            acc1 = gbuf1[0]
            for d in range(1, N_DEV):
                acc = acc + gbuf[d]
                acc1 = acc1 + gbuf1[d]
            sums_ref[...] = acc
            sums_ref[0:1, :] = acc1[0:1, :]

    tile = lambda w: pl.BlockSpec((tm, w), lambda i: (i, 0))
    vec = pl.BlockSpec((1, D), lambda i: (0, 0))
    hbm = pl.BlockSpec(memory_space=pl.ANY)
    whole = lambda a: pl.BlockSpec(a.shape, lambda i: (0,) * a.ndim)
    return pl.pallas_call(
        body,
        name="in_proj_bwd",
        grid=(nt,),
        in_specs=[tile(PW), tile(D), tile(D), vec, pl.BlockSpec((PW, D), lambda i: (0, 0)), hbm]
        + [whole(a) for a in small_parts],
        out_specs=[tile(D), pl.BlockSpec((SMALL_PACK_ROWS, D), lambda i: (0, 0)), hbm],
        out_shape=[jax.ShapeDtypeStruct((t, D), F32), jax.ShapeDtypeStruct((SMALL_PACK_ROWS, D), F32),
                   jax.ShapeDtypeStruct((2,) + pb.shape[1:], BF16)],
        scratch_shapes=[pltpu.VMEM((1, D), F32),
                        pltpu.VMEM((SMALL_PACK_ROWS, D), F32), pltpu.VMEM((N_DEV, SMALL_PACK_ROWS, D), F32),
                        pltpu.VMEM((8, D), F32), pltpu.VMEM((N_DEV, 8, D), F32),
                        pltpu.SemaphoreType.DMA((7,)), pltpu.SemaphoreType.DMA((7,)),
                        pltpu.SemaphoreType.DMA((7,)), pltpu.SemaphoreType.DMA((7,))] + _stage2_scratch([pb]),
        compiler_params=_params(("arbitrary",)),
    )(dproj2d, x2d, dx1, g1, w_in_t, pb, *small_parts)


def _get_rows(ref):
    return ref[:, 0, :] if len(ref.shape) == 3 else ref[...]


def _put_rows(ref, val):
    if len(ref.shape) == 3:
        ref[:, 0, :] = val
    else:
        ref[...] = val


def _adamw_math(w, g, m, v):
    m = ADAM_B1 * m + (1.0 - ADAM_B1) * g
    v = ADAM_B2 * v + (1.0 - ADAM_B2) * (g * g)
    m_hat = m / (1.0 - ADAM_B1 ** ADAM_STEP)
    v_hat = v / (1.0 - ADAM_B2 ** ADAM_STEP)
    delta = -ADAM_LR * (m_hat / (jnp.sqrt(v_hat) + ADAM_EPS) + ADAM_WD * w)
    return delta, m, v


def _position():
    return lax.axis_index("x"), lax.axis_index("y"), lax.axis_index("c")


GATHER_PARTS = 2
GATHER_SEMS = 7 * GATHER_PARTS
RELAY_AT = 3


def _gather_copies(stage, lo, rows, gx, send_sems, recv_sems, local_sem):
    x, y, c = _position()
    me = (x, y, c)
    sibling = (x, y, 1 - c)
    chips = [(1 - x, y), (x, 1 - y), (1 - x, 1 - y)]
    part = -(-rows // (16 * GATHER_PARTS)) * 16
    bounds = [(p * part, min(part, rows - p * part)) for p in range(GATHER_PARTS)]

    def blk(px, py, pc, off, n):
        return gx.at[4 * px + 2 * py + pc, pl.ds(off, n), :]

    mine = pltpu.make_async_copy(stage.at[pl.ds(lo, rows), :], gx.at[4 * x + 2 * y + c], local_sem)
    parts = []
    for p, (off, n) in enumerate(bounds):
        def copy(k, block, to, from_stage=False, p=p, off=off, n=n):
            return pltpu.make_async_remote_copy(
                src_ref=stage.at[pl.ds(lo + off, n), :] if from_stage else blk(*block, off, n),
                dst_ref=blk(*block, off, n), send_sem=send_sems.at[7 * p + k], recv_sem=recv_sems.at[7 * p + k],
                device_id=to, device_id_type=MESH)

        first = [copy(0, me, sibling, True)] + [copy(1 + j, me, (*chips[j], c), True) for j in range(2)]
        relay = copy(3, (*chips[p], c), (*chips[1 - p], c))
        passed = [copy(4 + j, (*chip, c), sibling) for j, chip in enumerate(chips)]
        arrivals = ([copy(0, sibling, me)] + [copy(1 + j, (*chip, c), me) for j, chip in enumerate(chips)]
                    + [copy(4 + j, (*chip, 1 - c), me) for j, chip in enumerate(chips)])
        parts.append((first, relay, passed, arrivals))
    return mine, parts


def _gather_start(*args):
    mine, parts = _gather_copies(*args)
    mine.start()
    for first, _, _, _ in parts:
        first[0].start()
    for p, q in ((0, 0), (1, 1), (0, 1), (1, 0)):
        parts[p][0][1 + q].start()


def _gather_relay(*args):
    _, parts = _gather_copies(*args)
    for p, (_, relay, passed, arrivals) in enumerate(parts):
        arrivals[1 + p].wait_recv()
        relay.start()
        passed[p].start()


def _gather_finish(*args):
    mine, parts = _gather_copies(*args)
    for p, (_, _, passed, arrivals) in enumerate(parts):
        for j in range(3):
            if j != p:
                arrivals[1 + j].wait_recv()
                passed[j].start()
    for first, relay, passed, arrivals in parts:
        arrivals[0].wait_recv()
        for j in range(3):
            arrivals[4 + j].wait_recv()
        for cp in first + [relay] + passed:
            cp.wait_send()
    mine.wait()


def _gather_sems():
    return [pltpu.SemaphoreType.DMA((GATHER_SEMS,)), pltpu.SemaphoreType.DMA((GATHER_SEMS,)), pltpu.SemaphoreType.DMA]


def _gather_w_in(w_it, w_gt, w_ut, w_d, w_o, wgu_s, conv_s):
    def body(wi_ref, wg_ref, wu_ref, wd_ref, wo_ref, wgu_ref, conv_ref, w_ref, gwgu_ref, gconv_ref, stage,
             buf, send_sems, recv_sems, local_sem, ssend, srecv):
        x, y, c = _position()
        me = 4 * x + 2 * y + c
        stage[SLAB_IN:SLAB_IN + IN_W, :] = wi_ref[:, 0, :].astype(BF16)
        stage[SLAB_IN + IN_W:SLAB_G, :] = jnp.zeros((IN_ROWS - IN_W, D), BF16)
        args = (stage, SLAB_IN, IN_ROWS, buf, send_sems, recv_sems, local_sem)
        _gather_start(*args)
        stage[SLAB_G:SLAB_U, :] = wg_ref[...].astype(BF16)
        stage[SLAB_U:SLAB_D, :] = wu_ref[...].astype(BF16)
        stage[SLAB_D:SLAB_O, :] = wd_ref[...].astype(BF16)
        stage[SLAB_O:SLAB_ROWS, :] = wo_ref[...].astype(BF16)
        flips = [(k >> 2, (k >> 1) & 1, k & 1) for k in range(1, N_DEV)]
        peers = [(x ^ fx, y ^ fy, c ^ fc) for fx, fy, fc in flips]

        def small(k, block_id, to):
            return [pltpu.make_async_remote_copy(
                src_ref=s, dst_ref=g.at[block_id], send_sem=ssend.at[2 * k + n], recv_sem=srecv.at[2 * k + n],
                device_id=to, device_id_type=MESH)
                for n, (s, g) in enumerate(((wgu_ref, gwgu_ref), (conv_ref, gconv_ref)))]

        gwgu_ref[me] = wgu_ref[...]
        gconv_ref[me] = conv_ref[...]
        for k, peer in enumerate(peers):
            for cp in small(k, me, peer):
                cp.start()
        w_ref[IN_COLS:PW, :] = jnp.zeros((PW - IN_COLS, D), BF16)
        _gather_relay(*args)
        _gather_finish(*args)
        for k, (px, py, pc) in enumerate(peers):
            for cp in small(k, 4 * px + 2 * py + pc, (px, py, pc)):
                cp.wait_recv()
                cp.wait_send()
        for j, lo, hi, d in _in_segments():
            w_ref[d:d + hi - lo, :] = buf[j, lo:hi, :]

    vm = pl.BlockSpec(memory_space=pltpu.VMEM)
    return pl.pallas_call(
        body,
        name="gather_w_in",
        in_specs=[vm] * 7,
        out_specs=[vm] * 4,
        out_shape=[jax.ShapeDtypeStruct((PW, D), BF16),
                   jax.ShapeDtypeStruct((N_DEV,) + wgu_s.shape, F32),
                   jax.ShapeDtypeStruct((N_DEV,) + conv_s.shape, F32),
                   jax.ShapeDtypeStruct((SLAB_ROWS, D), BF16)],
        scratch_shapes=[pltpu.VMEM((N_DEV, IN_ROWS, D), BF16)] + _gather_sems()
        + [pltpu.SemaphoreType.DMA((14,)), pltpu.SemaphoreType.DMA((14,))],
        compiler_params=_params(),
    )(w_it, w_gt, w_ut, w_d, w_o, wgu_s, conv_s)


def _w_in_core_reduce(dw_t):
    def body(d_ref, own_ref, sib_ref, pb_ref, g, gb, r1, send_sems, recv_sems):
        x, y, c = _position()
        chip = 2 * x + y
        for j in range(N_DEV):
            g[j, IN_W:IN_ROWS, :] = jnp.zeros((IN_ROWS - IN_W, D), F32)
        for j, lo, hi, d in _in_segments():
            g[j, lo:hi, :] = d_ref[d:d + hi - lo, :]
        for j in range(N_DEV):
            gb[j] = g[j].astype(BF16)
        copies = _stage1_copies(gb, r1, send_sems, recv_sems)
        for cp in copies:
            cp.start()
        own_ref[0] = g[2 * chip + c]
        for cp in copies:
            cp.wait_recv()
        sib_ref[0] = r1[chip]
        for k in range(1, 4):
            t = chip ^ k
            pb_ref[k - 1] = (g[2 * t + c] + r1[t].astype(F32)).astype(BF16)
        for cp in copies:
            cp.wait_send()

    vm = pl.BlockSpec(memory_space=pltpu.VMEM)
    return pl.pallas_call(
        body,
        name="w_in_core_reduce",
        in_specs=[vm],
        out_specs=[vm, vm, vm],
        out_shape=[jax.ShapeDtypeStruct((1, IN_ROWS, D), F32), jax.ShapeDtypeStruct((1, IN_ROWS, D), BF16),
                   jax.ShapeDtypeStruct((3, IN_ROWS, D), BF16)],
        scratch_shapes=[pltpu.VMEM((N_DEV, IN_ROWS, D), F32), pltpu.VMEM((N_DEV, IN_ROWS, D), BF16),
                        pltpu.VMEM((4, IN_ROWS, D), BF16), pltpu.SemaphoreType.DMA((4,)),
                        pltpu.SemaphoreType.DMA((4,))],
        compiler_params=_params(),
    )(dw_t)


def _stage1_copies(g_ref, r_ref, send_sems, recv_sems):
    x, y, c = _position()
    return [pltpu.make_async_remote_copy(
        src_ref=g_ref.at[2 * i + 1 - c], dst_ref=r_ref.at[i], send_sem=send_sems.at[i], recv_sem=recv_sems.at[i],
        device_id=(x, y, 1 - c), device_id_type=MESH) for i in range(4)]


def _ffn_core_reduce(dw3, dwb3, dw_o, dwb_o, pos_arr, dx1b, gwb):
    def body(pos_ref, g0, g1, g2, go, gb3_hbm, gbo_hbm, dx1b_ref, gwb_hbm, p0, p1, p2, po, s0, s1, s2, so, dmix_ref,
             r1f, r1o, wo, send_sems, recv_sems, wsem):
        step = pl.program_id(0)
        k = jnp.minimum(step, 2)
        x, y, c = _position()
        chip = 2 * x + y

        def copies(p):
            src = 2 * (chip ^ ((p + 1) & 3)) + 1 - c
            pairs = [(gb3_hbm.at[a, src], r1f.at[a, p]) for a in range(3)] + [(gbo_hbm.at[src], r1o.at[p])]
            return [pltpu.make_async_remote_copy(
                src_ref=s, dst_ref=d, send_sem=send_sems.at[4 * p + a], recv_sem=recv_sems.at[4 * p + a],
                device_id=(x, y, 1 - c), device_id_type=MESH) for a, (s, d) in enumerate(pairs)]

        @pl.when(step == 0)
        def _():
            for p in range(4):
                for cp in copies(p):
                    cp.start()
            loads = [pltpu.make_async_copy(gwb_hbm.at[j, pl.ds(FF_W, OUT_ROWS), :],
                                           wo.at[pl.ds(OUT_ROWS * j, OUT_ROWS), :], wsem.at[j]) for j in range(N_DEV)]
            for cp in loads:
                cp.start()
            for cp in loads:
                cp.wait()

        dmix_ref[...] = _nt(dx1b_ref[...], wo[...])

        for p in range(3):
            @pl.when(step == p)
            def _():
                for cp in copies(p):
                    cp.wait_recv()

        for a, (g, pb) in enumerate(((g0, p0), (g1, p1), (g2, p2))):
            pb[...] = (g[...] + r1f[a, k][None].astype(F32)).astype(BF16)
        po[...] = (go[...] + r1o[k][None].astype(F32)).astype(BF16)

        @pl.when(step == 3)
        def _():
            for cp in copies(3):
                cp.wait_recv()
            for a, s in enumerate((s0, s1, s2)):
                s[0] = r1f[a, 3]
            so[0] = r1o[3]
            for p in range(4):
                for cp in copies(p):
                    cp.wait_send()

    t = dx1b.shape[0]
    other = lambda s, pos: 2 * (pos[1] ^ (jnp.minimum(s, 2) + 1)) + pos[0]
    g_spec = lambda lead: pl.BlockSpec((None, 1, FF_W, D), lambda s, pos: (lead, other(s, pos), 0, 0))
    slot = lambda rows: pl.BlockSpec((1, rows, D), lambda s, pos: (jnp.minimum(s, 2), 0, 0))
    one = lambda rows: pl.BlockSpec((1, rows, D), lambda s, pos: (0, 0, 0))
    quarter = pl.BlockSpec((t // 4, D), lambda s, pos: (s, 0))
    hbm = pl.BlockSpec(memory_space=pl.ANY)
    return pl.pallas_call(
        body,
        name="ffn_core_reduce",
        grid_spec=pltpu.PrefetchScalarGridSpec(
            num_scalar_prefetch=1, grid=(4,),
            in_specs=[g_spec(0), g_spec(1), g_spec(2),
                      pl.BlockSpec((1, OUT_ROWS, D), lambda s, pos: (other(s, pos), 0, 0)), hbm, hbm, quarter, hbm],
            out_specs=[slot(FF_W), slot(FF_W), slot(FF_W), slot(OUT_ROWS),
                       one(FF_W), one(FF_W), one(FF_W), one(OUT_ROWS), quarter],
            scratch_shapes=[pltpu.VMEM((3, 4, FF_W, D), BF16), pltpu.VMEM((4, OUT_ROWS, D), BF16),
                            pltpu.VMEM((D, D), BF16), pltpu.SemaphoreType.DMA((16,)),
                            pltpu.SemaphoreType.DMA((16,)), pltpu.SemaphoreType.DMA((N_DEV,))]),
        out_shape=[jax.ShapeDtypeStruct((3, FF_W, D), BF16)] * 3 + [jax.ShapeDtypeStruct((3, OUT_ROWS, D), BF16)]
        + [jax.ShapeDtypeStruct((1, FF_W, D), BF16)] * 3 + [jax.ShapeDtypeStruct((1, OUT_ROWS, D), BF16),
                                                             jax.ShapeDtypeStruct((t, D), F32)],
        compiler_params=_params(("arbitrary",)),
    )(pos_arr, dw3, dw3, dw3, dw_o, dwb3, dwb_o, dx1b, gwb)


def _stage2_scratch(pbs):
    n = len(pbs)
    return ([pltpu.VMEM(p.shape[1:], BF16) for p in pbs] * 2
            + [pltpu.SemaphoreType.DMA((6 * n,)), pltpu.SemaphoreType.DMA((6 * n,)), pltpu.SemaphoreType.DMA((2 * n,))])


def _stage2_copies(p_refs, r_refs, scratch):
    n = len(p_refs)
    owns, gots = scratch[:n], scratch[n:2 * n]
    send_sems, recv_sems, load_sems = scratch[2 * n:]
    x, y, c = _position()
    xn, yn = (1 - x, y, c), (x, 1 - y, c)
    loads, first, second = [], [], []
    for a, (p, r, own, got) in enumerate(zip(p_refs, r_refs, owns, gots)):
        rows = p.shape[1]
        half = -(-rows // 32) * 16
        h0, h1 = pl.ds(0, half), pl.ds(half, rows - half)

        def remote(k, src, dst, to, a=a):
            return pltpu.make_async_remote_copy(
                src_ref=src, dst_ref=dst, send_sem=send_sems.at[6 * a + k], recv_sem=recv_sems.at[6 * a + k],
                device_id=to, device_id_type=MESH)

        loads += [pltpu.make_async_copy(p.at[0, h0, :], own.at[h0, :], load_sems.at[2 * a]),
                  pltpu.make_async_copy(p.at[1, h1, :], own.at[h1, :], load_sems.at[2 * a + 1])]
        first += [remote(0, p.at[2, h0, :], got.at[h0, :], xn), remote(1, p.at[2, h1, :], got.at[h1, :], yn),
                  remote(2, p.at[1, h0, :], r.at[1, h0, :], xn), remote(3, p.at[0, h1, :], r.at[0, h1, :], yn)]
        second += [remote(4, own.at[h0, :], r.at[0, h0, :], yn), remote(5, own.at[h1, :], r.at[1, h1, :], xn)]
    return loads, first, second


def _stage2_start(p_refs, r_refs, scratch):
    loads, first, _ = _stage2_copies(p_refs, r_refs, scratch)
    for cp in loads:
        cp.start()
    for k in range(4):
        for cp in first[k::4]:
            cp.start()


def _stage2_combine(p_refs, r_refs, scratch):
    n = len(p_refs)
    loads, first, second = _stage2_copies(p_refs, r_refs, scratch)
    for a in range(n):
        for cp in loads[2 * a:2 * a + 2]:
            cp.wait()
        for cp in first[4 * a:4 * a + 2]:
            cp.wait_recv()
        own, got = scratch[a], scratch[n + a]
        own[...] = (own[...].astype(F32) + got[...].astype(F32)).astype(BF16)
        for cp in second[2 * a:2 * a + 2]:
            cp.start()


def _stage2_finish(p_refs, r_refs, scratch):
    _, first, second = _stage2_copies(p_refs, r_refs, scratch)
    for a in range(len(p_refs)):
        for cp in first[4 * a + 2:4 * a + 4] + second[2 * a:2 * a + 2]:
            cp.wait_recv()
    for cp in first + second:
        cp.wait_send()


def _finish_weights(items, pos_arr, name, nblk):
    n = len(items)
    in_specs, out_specs, out_shape, operands, wbs = [], [], [], [], []
    for g8, lead, r1, r2, w, m, v in items:
        rows, wr = g8.shape[-2], w.shape[0]
        assert rows % nblk == 0 and wr % nblk == 0 and (nblk == 1 or (rows == wr and rows % (16 * nblk) == 0))
        rb, wb = rows // nblk, wr // nblk
        if lead is not None:
            g_spec = pl.BlockSpec((None, 1, rb, D), lambda i, pos, lead=lead: (lead, 2 * pos[1] + pos[0], i, 0))
        elif g8.shape[0] == 1:
            g_spec = pl.BlockSpec((1, rb, D), lambda i, pos: (0, i, 0))
        else:
            g_spec = pl.BlockSpec((1, rb, D), lambda i, pos: (2 * pos[1] + pos[0], i, 0))
        r1_spec = pl.BlockSpec((1, rb, D), lambda i, pos: (0, i, 0))
        if w.ndim == 3:
            wblk = pl.BlockSpec((wb, 1, D), lambda i, pos: (i, 0, 0))
        else:
            wblk = pl.BlockSpec((wb, D), lambda i, pos: (i, 0))
        in_specs += [g_spec, r1_spec, pl.BlockSpec((2, rb, D), lambda i, pos: (0, i, 0)), wblk, wblk, wblk]
        out_specs += [wblk] * 4
        out_shape += [jax.ShapeDtypeStruct(w.shape, F32)] * 4
        operands += [g8, r1, r2, w, m, v]
        wbs.append(wb)

    def body(pos_ref, *refs):
        for a in range(n):
            g_ref, r1_ref, r2_ref, w_ref, m_ref, v_ref = refs[6 * a:6 * a + 6]
            g_out, d_out, m_out, v_out = refs[6 * n + 4 * a:6 * n + 4 * a + 4]
            g = g_ref[0] + r1_ref[0].astype(F32)
            for k in range(2):
                g = g + r2_ref[k].astype(F32)
            g = g[0:wbs[a], :]
            d, mn, vn = _adamw_math(_get_rows(w_ref), g, _get_rows(m_ref), _get_rows(v_ref))
            for out, val in ((g_out, g), (d_out, d), (m_out, mn), (v_out, vn)):
                _put_rows(out, val)

    return pl.pallas_call(
        body,
        name=name,
        grid_spec=pltpu.PrefetchScalarGridSpec(
            num_scalar_prefetch=1, grid=(nblk,), in_specs=in_specs, out_specs=out_specs),
        out_shape=out_shape,
        compiler_params=_params(("arbitrary",)),
    )(pos_arr, *operands)


SMALL_NAMES = ("norm1_g", "norm2_g", "norm_f_g", "b_gate", "gla_norm_g", "w_gate_up", "conv_w")
WGU_W = NQK // N_DEV
CONV_W = CW // N_DEV


def _small_adamw(sums, ws, ms, vs):
    n = len(SMALL_NAMES)

    def body(*refs):
        acc_ref = refs[0]
        w_refs, m_refs, v_refs = refs[1:1 + n], refs[1 + n:1 + 2 * n], refs[1 + 2 * n:1 + 3 * n]
        loss_ref = refs[1 + 3 * n]
        outs = refs[2 + 3 * n:]
        x, y, c = _position()
        me = 4 * x + 2 * y + c
        acc = acc_ref[...]
        loss_ref[...] = acc[3:4, NQK + DV:NQK + DV + 1]

        def my_columns(full, width):
            r = lax.broadcasted_iota(jnp.int32, (full.shape[1], width), 0)
            col = lax.broadcasted_iota(jnp.int32, (full.shape[1], width), 1)
            sel = (r == width * me + col).astype(F32)
            return _mm(full, sel, precision=HIGHEST)

        dwgu = jnp.concatenate([acc[row:row + 1, lane:lane + NQK] for row, lane in map(_wgu_slot, range(RANK))], axis=0)
        dcw = jnp.concatenate([acc[row:row + 1, lane:lane + CW] for row, lane in CONV_SLOTS], axis=0)
        grads = [acc[0:1, :], acc[1:2, :], acc[2:3, :], acc[3:4, 0:NQK], acc[3:4, NQK:NQK + DV],
                 my_columns(dwgu, WGU_W), my_columns(dcw, CONV_W)]
        for i, g in enumerate(grads):
            d, mn, vn = _adamw_math(_get_rows(w_refs[i]), g, _get_rows(m_refs[i]), _get_rows(v_refs[i]))
            for out, val in zip(outs[4 * i:4 * i + 4], (g, d, mn, vn)):
                _put_rows(out, val)

    vm = pl.BlockSpec(memory_space=pltpu.VMEM)
    out_shape = [jax.ShapeDtypeStruct((1, 1), F32)]
    for w in ws:
        out_shape += [jax.ShapeDtypeStruct(w.shape, F32)] * 4
    return pl.pallas_call(
        body,
        name="small_adamw",
        in_specs=[vm] * (1 + 3 * n),
        out_specs=[vm] * (1 + 4 * n),
        out_shape=out_shape,
        compiler_params=_params(),
    )(sums, *ws, *ms, *vs)


def kernel(x, norm1_g, w_in, w_gate_up, b_gate, gla_norm_g, conv_w, w_out, norm2_g, w_ffn_gate, w_ffn_up, w_ffn_down, norm_f_g, loss_target, m_norm1_g, m_w_in, m_w_gate_up, m_b_gate, m_gla_norm_g, m_conv_w, m_w_out, m_norm2_g, m_w_ffn_gate, m_w_ffn_up, m_w_ffn_down, m_norm_f_g, v_norm1_g, v_w_in, v_w_gate_up, v_b_gate, v_gla_norm_g, v_conv_w, v_w_out, v_norm2_g, v_w_ffn_gate, v_w_ffn_up, v_w_ffn_down, v_norm_f_g):
    xi, yi, ci = _position()
    pos_arr = jnp.stack([ci, 2 * xi + yi]).astype(jnp.int32)
    nb, s, _ = x.shape
    t = nb * s

    tr = lambda a: a[0].T
    rows_of = lambda a: a.transpose(2, 0, 1)
    conv_rows = lambda a: a.transpose(1, 0, 2)
    w_in_t, gwgu, gconv, stage = _gather_w_in(rows_of(w_in), tr(w_ffn_gate), tr(w_ffn_up), w_ffn_down[0], w_out[0],
                                              w_gate_up[0], conv_rows(conv_w))
    wgu_f = gwgu.transpose(1, 0, 2).reshape(RANK, NQK)
    conv_f = gconv.transpose(1, 2, 0, 3).reshape(CONV_K, CW)
    wgu_p = jnp.concatenate([wgu_f, jnp.zeros((A_PAD - RANK, NQK), F32)], axis=0).astype(BF16)

    x2d = x.reshape(t, D)
    tgt2d = loss_target.reshape(t, D)
    tm = 256
    tm_in = min(512, t)
    tk = min(2048, t)
    proj, z, h, gwb = _in_proj_fwd(x2d, norm1_g, w_in_t, wgu_p, b_gate, tm_in, stage)
    proj3 = proj.reshape(nb, s, PW)
    z3 = z.reshape(nb, s, NQK)
    mix3, opre3, sprev, x1, gwa = _mix_fwd(proj3, z3, gla_norm_g, conv_f, stage, x, gwb)
    mix2d = mix3.reshape(t, D)
    dx1, dx1b, adu, hb, dg2, dgf, loss_part = _ffn_fwd_bwd(
        x1.reshape(t, D), tgt2d, gwa, gwb, norm2_g, norm_f_g.reshape(1, D), tm)
    dw3, dwb3 = _dw_ffn(adu, hb, tk)
    dw3 = dw3.reshape(3, N_DEV, FF_W, D)
    dw_o, dwb_o = _tn_matmul(mix2d, dx1b, D // 2, D, tk, "dw_out", True)
    dw_o = dw_o.reshape(N_DEV, OUT_ROWS, D)
    *pb, sib_d, sib_g, sib_u, sib_o, dmix = _ffn_core_reduce(
        dw3, dwb3.reshape(3, N_DEV, FF_W, D), dw_o, dwb_o.reshape(N_DEV, OUT_ROWS, D), pos_arr, dx1b, gwb)
    g8 = [dw3, dw3, dw3, dw_o]
    leads = [0, 1, 2, None]
    tags = ("w_ffn_down", "w_ffn_gate", "w_ffn_up", "w_out")
    r1 = [sib_d, sib_g, sib_u, sib_o]
    mb = _mix_bwd(proj3, z3, sprev, opre3, dmix.reshape(nb, s, D), gla_norm_g, conv_f, wgu_p, [pb[0], pb[1], pb[3]])
    dproj3, dgng, dcw, dbg, dwgu = mb[:5]
    dproj2d = dproj3.reshape(t, PW)
    dw_in_t, r2_up = _tn_matmul(dproj2d, h, PW // 5, D, t, "dw_in", False, _stage2_rider([pb[2]]))
    r2 = [mb[5], mb[6], r2_up, mb[7]]
    g_in, r1_in, pb_in = _w_in_core_reduce(dw_in_t)
    dx, small_sums, r2_in = _in_proj_bwd(dproj2d, x2d, dx1, norm1_g, w_in_t, tm_in, pb_in,
                                         (dg2, dgf, dbg, dgng, dwgu, dcw, loss_part))

    tags = ("w_in",) + tags
    g8 = [g_in] + g8
    leads = [None] + leads
    r1 = [r1_in] + list(r1)
    r2 = [r2_in] + r2
    shard_w = (rows_of(w_in), w_ffn_down[0], tr(w_ffn_gate), tr(w_ffn_up), w_out[0])
    shard_m = (rows_of(m_w_in), m_w_ffn_down[0], tr(m_w_ffn_gate), tr(m_w_ffn_up), m_w_out[0])
    shard_v = (rows_of(v_w_in), v_w_ffn_down[0], tr(v_w_ffn_gate), tr(v_w_ffn_up), v_w_out[0])
    back = (lambda o: o.transpose(1, 2, 0), lambda o: o[None], lambda o: o.T[None], lambda o: o.T[None],
            lambda o: o[None])
    items = list(zip(g8, leads, r1, r2, shard_w, shard_m, shard_v))
    flat = list(_finish_weights(items[1:], pos_arr, "finish_ffn_out", 2))
    flat = list(_finish_weights(items[:1], pos_arr, "finish_w_in", 1)) + flat
    results = {}
    for i, (tag, to_shard) in enumerate(zip(tags, back)):
        results[tag] = [to_shard(o) for o in flat[4 * i:4 * i + 4]]

    small_w = (norm1_g, norm2_g, norm_f_g.reshape(1, D), b_gate, gla_norm_g, w_gate_up[0], conv_rows(conv_w))
    small_m = (m_norm1_g, m_norm2_g, m_norm_f_g.reshape(1, D), m_b_gate, m_gla_norm_g, m_w_gate_up[0],
               conv_rows(m_conv_w))
    small_v = (v_norm1_g, v_norm2_g, v_norm_f_g.reshape(1, D), v_b_gate, v_gla_norm_g, v_w_gate_up[0],
               conv_rows(v_conv_w))
    so = _small_adamw(small_sums, small_w, small_m, small_v)
    loss = so[0].reshape(())
    to_shape = {"norm_f_g": lambda o: o.reshape(D), "w_gate_up": lambda o: o[None],
                "conv_w": lambda o: o.transpose(1, 0, 2)}
    for i, name in enumerate(SMALL_NAMES):
        results[name] = [to_shape.get(name, lambda o: o)(o) for o in so[1 + 4 * i:5 + 4 * i]]

    names = ("norm1_g", "w_in", "w_gate_up", "b_gate", "gla_norm_g", "conv_w", "w_out", "norm2_g",
             "w_ffn_gate", "w_ffn_up", "w_ffn_down", "norm_f_g")
    outs = [loss, dx.reshape(nb, s, D)]
    for kind in range(4):
        for name in names:
            outs.append(results[name][kind])
    return tuple(outs)
```

```python
import jax
import jax.numpy as jnp
from jax import lax
from jax.experimental import pallas as pl
from jax.experimental.pallas import tpu as pltpu

F32 = jnp.float32
BF16 = jnp.bfloat16
HIGHEST = lax.Precision.HIGHEST
MESH = pl.DeviceIdType.MESH

N_DEV = 8
D = 1024
DFF = 2816
HEADS = 4
DK = 64
DV = 128
NQK = HEADS * DK
NV = HEADS * DV
RANK = 16
CHUNK = 64
CW = 512
CONV_K = 3
IN_COLS = 3088
EPS = 1e-6
INV_GATE_NORM = 1.0 / 16.0
Q_SCALE = DK ** -0.5

PW = 3200
OQ, OK_, OV, OG, OCB, OCC, OCH, OA = 0, 256, 512, 1024, 1536, 2048, 2560, 3072
A_PAD = 128

ADAM_LR = 0.001
ADAM_B1 = 0.9
ADAM_B2 = 0.999
ADAM_EPS = 1e-08
ADAM_WD = 0.01
ADAM_STEP = 10

IN_W = IN_COLS // N_DEV
IN_ROWS = 400
FF_W = DFF // N_DEV
OUT_ROWS = D // N_DEV
SLAB_IN = 0
SLAB_G = SLAB_IN + IN_ROWS
SLAB_U = SLAB_G + FF_W
SLAB_D = SLAB_U + FF_W
SLAB_O = SLAB_D + FF_W
SLAB_ROWS = SLAB_O + OUT_ROWS
D_HEAD = 128
D_TAIL = FF_W - D_HEAD
SLAB_SPLIT = SLAB_D + D_HEAD

VMEM_LIMIT = 56 * 1024 * 1024


def _params(sem=None, vmem=VMEM_LIMIT):
    return pltpu.CompilerParams(dimension_semantics=sem, vmem_limit_bytes=vmem)


def _nt(a, b):
    return lax.dot_general(a, b, (((1,), (1,)), ((), ())), preferred_element_type=F32)


def _tn(a, b, precision=None):
    return lax.dot_general(a, b, (((0,), (0,)), ((), ())), preferred_element_type=F32, precision=precision)


def _mm(a, b, precision=None):
    return jnp.dot(a, b, preferred_element_type=F32, precision=precision)


def _in_segments():
    segs = []
    for j in range(N_DEV):
        lo, hi = IN_W * j, IN_W * (j + 1)
        cuts = sorted({lo, hi} | {c for c in (OCB, OCB + RANK) if lo < c < hi})
        for a, b in zip(cuts[:-1], cuts[1:]):
            if a < OCB:
                d = a
            elif a < OCB + RANK:
                d = OA + (a - OCB)
            else:
                d = a - RANK
            segs.append((j, a - lo, b - lo, d))
    return segs


def _in_proj_fwd(x2d, g1, w_in_t, wgu_p, b_gate, tm, stage):
    t = x2d.shape[0]
    nt = t // tm
    g_rows = SLAB_ROWS - SLAB_SPLIT

    def body(x_ref, g_ref, w_ref, wgu_ref, bg_ref, stage_hbm, proj_ref, z_ref, h_ref, gwb_ref,
             send_sems, recv_sems, local_sem):
        gargs = (stage_hbm, SLAB_SPLIT, g_rows, gwb_ref, send_sems, recv_sems, local_sem)

        @pl.when(pl.program_id(0) == 0)
        def _():
            _gather_start(*gargs)

        @pl.when(pl.program_id(0) == RELAY_AT * nt // 8)
        def _():
            _gather_relay(*gargs)

        x = x_ref[...]
        r = lax.rsqrt(jnp.mean(x * x, axis=-1, keepdims=True) + EPS)
        h = ((x * r) * g_ref[...]).astype(BF16)
        h_ref[...] = h
        proj = _nt(h, w_ref[...])
        proj_ref[...] = proj
        pa = proj[:, OA:OA + A_PAD].astype(BF16)
        z_ref[...] = _mm(pa, wgu_ref[...]) + bg_ref[...]

        @pl.when(pl.program_id(0) == nt - 1)
        def _():
            _gather_finish(*gargs)

    return pl.pallas_call(
        body,
        name="in_proj_fwd",
        grid=(t // tm,),
        in_specs=[
            pl.BlockSpec((tm, D), lambda i: (i, 0)),
            pl.BlockSpec((1, D), lambda i: (0, 0)),
            pl.BlockSpec((PW, D), lambda i: (0, 0)),
            pl.BlockSpec((A_PAD, NQK), lambda i: (0, 0)),
            pl.BlockSpec((1, NQK), lambda i: (0, 0)),
            pl.BlockSpec(memory_space=pl.ANY),
        ],
        out_specs=[
            pl.BlockSpec((tm, PW), lambda i: (i, 0)),
            pl.BlockSpec((tm, NQK), lambda i: (i, 0)),
            pl.BlockSpec((tm, D), lambda i: (i, 0)),
            pl.BlockSpec(memory_space=pl.ANY),
        ],
        out_shape=[
            jax.ShapeDtypeStruct((t, PW), F32),
            jax.ShapeDtypeStruct((t, NQK), F32),
            jax.ShapeDtypeStruct((t, D), BF16),
            jax.ShapeDtypeStruct((N_DEV, g_rows, D), BF16),
        ],
        scratch_shapes=_gather_sems(),
        compiler_params=_params(("arbitrary",)),
    )(x2d, g1, w_in_t, wgu_p, b_gate, stage)


def _head_masks():
    lane = lax.broadcasted_iota(jnp.int32, (1, NQK), 1)
    return [(lane >= DK * h) & (lane < DK * (h + 1)) for h in range(HEADS)]


def _split_bf16(x, n):
    parts = []
    for _ in range(n):
        p = x.astype(BF16)
        parts.append(p)
        x = x - p.astype(F32)
    return parts


def _chunk_fwd_parts(q, k, z, tril16):
    la = (jnp.minimum(z, 0.0) - jnp.log1p(jnp.exp(-jnp.abs(z)))) * INV_GATE_NORM
    la_parts = _split_bf16(la, 3)
    bc = _mm(tril16, la_parts[0]) + _mm(tril16, la_parts[1]) + _mm(tril16, la_parts[2])
    bl = bc[CHUNK - 1:CHUNK, :]
    eb = jnp.exp(bc)
    enb = jnp.exp(-bc)
    ekl = jnp.exp(bl - bc)
    qi = (q * Q_SCALE) * eb
    ki = k * enb
    ks = k * ekl
    ones16 = jnp.ones((CHUNK, DV), BF16)
    decb = jnp.exp(_tn(la_parts[0], ones16) + _tn(la_parts[1], ones16) + _tn(la_parts[2], ones16))
    return la, eb, enb, ekl, qi, ki, ks, decb


def _stack_heads(a, masks):
    return jnp.concatenate([jnp.where(m, a, 0.0) for m in masks], axis=0)


def _merge_heads(blocks, masks):
    out = blocks[HEADS - 1]
    for h in range(HEADS - 2, -1, -1):
        out = jnp.where(masks[h], blocks[h], out)
    return out


def _causal_stack_mask():
    row = lax.broadcasted_iota(jnp.int32, (HEADS * CHUNK, CHUNK), 0)
    col = lax.broadcasted_iota(jnp.int32, (HEADS * CHUNK, CHUNK), 1)
    return (row & (CHUNK - 1)) >= col


def _conv_taps(u, uprev):
    row = lax.broadcasted_iota(jnp.int32, u.shape, 0)
    u1 = jnp.where(row < 1, pltpu.roll(uprev, 1, 0), pltpu.roll(u, 1, 0))
    u2 = jnp.where(row < 2, pltpu.roll(uprev, 2, 0), pltpu.roll(u, 2, 0))
    return u1, u2


def _mix_fwd(proj3, z3, gng, conv_w, stage, x3, gwb):
    nb, s, _ = proj3.shape
    nc = s // CHUNK
    g_rows = SLAB_SPLIT - SLAB_G

    def body(p_ref, z_ref, gng_ref, cw_ref, stage_hbm, x_ref, gwb_hbm, mix_ref, o_ref, sprev_ref, x1_ref, gwa_ref,
             s_ref, uprev_ref, wo, wsem, send_sems, recv_sems, local_sem):
        n = pl.program_id(0)
        gargs = (stage_hbm, SLAB_G, g_rows, gwa_ref, send_sems, recv_sems, local_sem)

        @pl.when(n == 0)
        def _():
            _gather_start(*gargs)
            loads = [pltpu.make_async_copy(gwb_hbm.at[j, pl.ds(D_TAIL, OUT_ROWS), :],
                                           wo.at[pl.ds(OUT_ROWS * j, OUT_ROWS), :], wsem.at[j]) for j in range(N_DEV)]
            for cp in loads:
                cp.start()
            s_ref[...] = jnp.zeros_like(s_ref)
            uprev_ref[...] = jnp.zeros_like(uprev_ref)
            for cp in loads:
                cp.wait()

        @pl.when(n == RELAY_AT * nc // 8)
        def _():
            _gather_relay(*gargs)

        r_i = lax.broadcasted_iota(jnp.int32, (CHUNK, CHUNK), 0)
        c_i = lax.broadcasted_iota(jnp.int32, (CHUNK, CHUNK), 1)
        tril16 = (r_i >= c_i).astype(BF16)
        masks = _head_masks()
        cmask = _causal_stack_mask()
        gg = gng_ref[...]
        for b in range(nb):
            q = p_ref[b, :, OQ:OQ + NQK]
            k = p_ref[b, :, OK_:OK_ + NQK]
            _, _, _, _, qi, ki, ks, decb = _chunk_fwd_parts(q, k, z_ref[b], tril16)
            qs = _stack_heads(qi, masks).astype(BF16)
            sc = jnp.where(cmask, _nt(qs, ki.astype(BF16)), 0.0).astype(BF16)
            st = s_ref[b]
            sprev_ref[b, 0] = st
            o_inter = _mm(qs, st.astype(BF16))
            v16 = p_ref[b, :, OV:OV + NV].astype(BF16)
            kv = _tn(ks.astype(BF16), v16)
            for h in range(HEADS):
                rows = slice(CHUNK * h, CHUNK * (h + 1))
                cols = slice(DV * h, DV * (h + 1))
                o = _mm(sc[rows], v16[:, cols]) + o_inter[rows]
                o_ref[b, :, cols] = o
                r = lax.rsqrt(jnp.mean(o * o, axis=-1, keepdims=True) + EPS)
                on = (o * r) * gg
                g = p_ref[b, :, OG + DV * h:OG + DV * (h + 1)]
                mix_ref[b, :, cols] = (on * (g * jax.nn.sigmoid(g))).astype(BF16)
                s_ref[b, rows, :] = decb[rows] * st[rows] + kv[rows, cols]
            u = p_ref[b, :, OCC:OCC + CW] * p_ref[b, :, OCH:OCH + CW]
            u1, u2 = _conv_taps(u, uprev_ref[b])
            yc = cw_ref[0:1, :] * u2 + cw_ref[1:2, :] * u1 + cw_ref[2:3, :] * u
            mix_ref[b, :, NV:NV + CW] = (p_ref[b, :, OCB:OCB + CW] * yc).astype(BF16)
            uprev_ref[b] = u
        mixed = _mm(jnp.concatenate([mix_ref[b] for b in range(nb)], axis=0), wo[...])
        for b in range(nb):
            x1_ref[b] = x_ref[b] + mixed[CHUNK * b:CHUNK * (b + 1)]

        @pl.when(n == nc - 1)
        def _():
            _gather_finish(*gargs)

    return pl.pallas_call(
        body,
        name="mix_fwd",
        grid=(nc,),
        in_specs=[
            pl.BlockSpec((nb, CHUNK, PW), lambda n: (0, n, 0)),
            pl.BlockSpec((nb, CHUNK, NQK), lambda n: (0, n, 0)),
            pl.BlockSpec((1, DV), lambda n: (0, 0)),
            pl.BlockSpec((CONV_K, CW), lambda n: (0, 0)),
            pl.BlockSpec(memory_space=pl.ANY),
            pl.BlockSpec((nb, CHUNK, D), lambda n: (0, n, 0)),
            pl.BlockSpec(memory_space=pl.ANY),
        ],
        out_specs=[
            pl.BlockSpec((nb, CHUNK, D), lambda n: (0, n, 0)),
            pl.BlockSpec((nb, CHUNK, NV), lambda n: (0, n, 0)),
            pl.BlockSpec((nb, 1, NQK, DV), lambda n: (0, n, 0, 0)),
            pl.BlockSpec((nb, CHUNK, D), lambda n: (0, n, 0)),
            pl.BlockSpec(memory_space=pl.ANY),
        ],
        out_shape=[
            jax.ShapeDtypeStruct((nb, s, D), BF16),
            jax.ShapeDtypeStruct((nb, s, NV), F32),
            jax.ShapeDtypeStruct((nb, nc, NQK, DV), F32),
            jax.ShapeDtypeStruct((nb, s, D), F32),
            jax.ShapeDtypeStruct((N_DEV, g_rows, D), BF16),
        ],
        scratch_shapes=[pltpu.VMEM((nb, NQK, DV), F32), pltpu.VMEM((nb, CHUNK, CW), F32),
                        pltpu.VMEM((D, D), BF16), pltpu.SemaphoreType.DMA((N_DEV,))] + _gather_sems(),
        compiler_params=_params(("arbitrary",)),
    )(proj3, z3, gng, conv_w, stage, x3, gwb)


def _ffn_fwd_bwd(x1_2d, tgt2d, gwa, gwb, g2, gf, tm):
    t = x1_2d.shape[0]

    def body(x1_ref, tgt_ref, g2_ref, gf_ref, gwa_hbm, gwb_hbm,
             dx1_ref, dx1b_ref, adu_ref, hb_ref, dg2_ref, dgf_ref, loss_ref,
             wg, wu, wd, wsem):
        i = pl.program_id(0)

        def weight_copies(n, dst, src, off, rows, at=0):
            return [pltpu.make_async_copy(src.at[j, pl.ds(off, rows), :], dst.at[pl.ds(FF_W * j + at, rows), :],
                                          wsem.at[N_DEV * n + j]) for j in range(N_DEV)]

        loads = (weight_copies(0, wg, gwa_hbm, 0, FF_W), weight_copies(1, wu, gwa_hbm, FF_W, FF_W),
                 weight_copies(2, wd, gwa_hbm, 2 * FF_W, D_HEAD), weight_copies(3, wd, gwb_hbm, 0, D_TAIL, D_HEAD))

        @pl.when(i == 0)
        def _():
            for group in loads:
                for cp in group:
                    cp.start()
            dg2_ref[...] = jnp.zeros_like(dg2_ref)
            dgf_ref[...] = jnp.zeros_like(dgf_ref)
            loss_ref[...] = jnp.zeros_like(loss_ref)
            for group in loads:
                for cp in group:
                    cp.wait()

        g2v = g2_ref[...]
        gfv = gf_ref[...]
        x1 = x1_ref[...]
        r2 = lax.rsqrt(jnp.mean(x1 * x1, axis=-1, keepdims=True) + EPS)
        n2 = x1 * r2
        h2 = (n2 * g2v).astype(BF16)
        hb_ref[1] = h2
        gate = _nt(h2, wg[...])
        up = _nt(h2, wu[...])
        sg = jax.nn.sigmoid(gate)
        sil = gate * sg
        act = (sil * up).astype(BF16)
        adu_ref[0] = act
        x2 = x1 + _mm(act, wd[...])
        rf = lax.rsqrt(jnp.mean(x2 * x2, axis=-1, keepdims=True) + EPS)
        nf = x2 * rf
        err = nf * gfv - tgt_ref[...]
        loss_ref[...] += 0.5 * jnp.sum(jnp.mean(err * err, axis=-1, keepdims=True))
        dy = err * (1.0 / D)
        dgf_ref[...] += jnp.sum(dy * nf, axis=0, keepdims=True)
        dnf = dy * gfv
        dx2 = rf * (dnf - nf * jnp.mean(dnf * nf, axis=-1, keepdims=True))
        dx2b = dx2.astype(BF16)
        hb_ref[0] = dx2b
        dact = _nt(dx2b, wd[...])
        dup = (dact * sil).astype(BF16)
        dgate = ((dact * up) * (sg * (1.0 + gate * (1.0 - sg)))).astype(BF16)
        adu_ref[2] = dup
        adu_ref[1] = dgate
        dh2 = _mm(dgate, wg[...]) + _mm(dup, wu[...])
        dg2_ref[...] += jnp.sum(dh2 * n2, axis=0, keepdims=True)
        dn2 = dh2 * g2v
        dx1 = dx2 + r2 * (dn2 - n2 * jnp.mean(dn2 * n2, axis=-1, keepdims=True))
        dx1_ref[...] = dx1
        dx1b_ref[...] = dx1.astype(BF16)

    tile = lambda w: pl.BlockSpec((tm, w), lambda i: (i, 0))
    vec = pl.BlockSpec((1, D), lambda i: (0, 0))
    hbm = pl.BlockSpec(memory_space=pl.ANY)
    return pl.pallas_call(
        body,
        name="ffn_fwd_bwd",
        grid=(t // tm,),
        in_specs=[tile(D), tile(D), vec, vec, hbm, hbm],
        out_specs=[tile(D), tile(D), pl.BlockSpec((3, tm, DFF), lambda i: (0, i, 0)),
                   pl.BlockSpec((2, tm, D), lambda i: (0, i, 0)), vec, vec,
                   pl.BlockSpec((1, 128), lambda i: (0, 0))],
        out_shape=[
            jax.ShapeDtypeStruct((t, D), F32),
            jax.ShapeDtypeStruct((t, D), BF16),
            jax.ShapeDtypeStruct((3, t, DFF), BF16),
            jax.ShapeDtypeStruct((2, t, D), BF16),
            jax.ShapeDtypeStruct((1, D), F32),
            jax.ShapeDtypeStruct((1, D), F32),
            jax.ShapeDtypeStruct((1, 128), F32),
        ],
        scratch_shapes=[pltpu.VMEM((DFF, D), BF16), pltpu.VMEM((DFF, D), BF16), pltpu.VMEM((DFF, D), BF16),
                        pltpu.SemaphoreType.DMA((4 * N_DEV,))],
        compiler_params=_params(("arbitrary",)),
    )(x1_2d, tgt2d, g2, gf, gwa, gwb)


def _stage2_rider(pbs):
    return dict(inputs=list(pbs), out_shape=[jax.ShapeDtypeStruct((2,) + p.shape[1:], BF16) for p in pbs],
                scratch=_stage2_scratch(pbs))


def _tn_matmul(a, b, bm, bn, tk, name, with_bf16, rider=None):
    t, m = a.shape
    n = b.shape[1]
    nk = t // tk
    nout = 2 if with_bf16 else 1
    grid = (m // bm, n // bn, nk)
    steps = grid[0] * grid[1] * nk
    r_in = [] if rider is None else rider["inputs"]
    r_out = [] if rider is None else rider["out_shape"]

    def body(a_ref, b_ref, *rest):
        ins, outs = rest[:len(r_in)], rest[len(r_in):len(r_in) + nout]
        r_outs, scratch = rest[len(r_in) + nout:len(r_in) + nout + len(r_out)], rest[len(r_in) + nout + len(r_out):]
        o_ref = outs[0]
        i, j, k = pl.program_id(0), pl.program_id(1), pl.program_id(2)
        step = (i * grid[1] + j) * nk + k
        if rider is not None:
            @pl.when(step == 0)
            def _():
                _stage2_start(ins, r_outs, scratch)

            @pl.when(step == RELAY_AT * steps // 8)
            def _():
                _stage2_combine(ins, r_outs, scratch)

        @pl.when(k == 0)
        def _():
            o_ref[...] = jnp.zeros_like(o_ref)

        o_ref[...] += _tn(a_ref[...].astype(BF16), b_ref[...].astype(BF16))
        if with_bf16:
            @pl.when(k == nk - 1)
            def _():
                outs[1][...] = o_ref[...].astype(BF16)
        if rider is not None:
            @pl.when(step == steps - 1)
            def _():
                _stage2_finish(ins, r_outs, scratch)

    out_blk = pl.BlockSpec((bm, bn), lambda i, j, k: (i, j))
    hbm = pl.BlockSpec(memory_space=pl.ANY)
    out_shape = [jax.ShapeDtypeStruct((m, n), F32)] + ([jax.ShapeDtypeStruct((m, n), BF16)] if with_bf16 else [])
    res = pl.pallas_call(
        body,
        name=name,
        grid=grid,
        in_specs=[pl.BlockSpec((tk, bm), lambda i, j, k: (k, i)), pl.BlockSpec((tk, bn), lambda i, j, k: (k, j))]
        + [hbm] * len(r_in),
        out_specs=[out_blk] * nout + [hbm] * len(r_out),
        out_shape=out_shape + list(r_out),
        scratch_shapes=[] if rider is None else rider["scratch"],
        compiler_params=_params(("parallel", "parallel", "arbitrary") if rider is None
                                else ("arbitrary", "arbitrary", "arbitrary")),
    )(a, b, *r_in)
    return res[0] if len(res) == 1 else res


def _dw_ffn(adu, hb, tk):
    _, t, _ = adu.shape
    bm = DFF // 2
    nk = t // tk

    def body(a_ref, b_ref, o_ref, ob_ref):
        k = pl.program_id(2)

        @pl.when(k == 0)
        def _():
            o_ref[...] = jnp.zeros_like(o_ref)

        o_ref[...] += _tn(a_ref[...], b_ref[...])

        @pl.when(k == nk - 1)
        def _():
            ob_ref[...] = o_ref[...].astype(BF16)

    out_blk = pl.BlockSpec((None, bm, D), lambda p, i, k: (p, i, 0))
    return pl.pallas_call(
        body,
        name="dw_ffn",
        grid=(3, DFF // bm, nk),
        in_specs=[pl.BlockSpec((None, tk, bm), lambda p, i, k: (p, k, i)),
                  pl.BlockSpec((None, tk, D), lambda p, i, k: (jnp.minimum(p, 1), k, 0))],
        out_specs=[out_blk, out_blk],
        out_shape=[jax.ShapeDtypeStruct((3, DFF, D), F32), jax.ShapeDtypeStruct((3, DFF, D), BF16)],
        compiler_params=_params(("arbitrary", "arbitrary", "arbitrary")),
    )(adu, hb)


def _mix_bwd(proj3, z3, sprev, opre3, dmix3, gng, conv_w, wgu_p, pbs):
    nb, s, _ = proj3.shape
    nc = s // CHUNK
    na = len(pbs)

    def body(*refs):
        (p_ref, pprev_ref, z_ref, sp_ref, o_ref, dm_ref, gng_ref, cw_ref, wgu_ref) = refs[:9]
        pb_refs = refs[9:9 + na]
        (dproj_ref, dgng_ref, dcw_ref, dbg_ref, dwgu_ref) = refs[9 + na:14 + na]
        r2_refs = refs[14 + na:14 + 2 * na]
        ds_ref, dycn_ref = refs[14 + 2 * na:16 + 2 * na]
        stage2 = (pb_refs, r2_refs, refs[16 + 2 * na:])
        step = pl.program_id(0)
        n = nc - 1 - step

        @pl.when(step == 0)
        def _():
            _stage2_start(*stage2)
            ds_ref[...] = jnp.zeros_like(ds_ref)
            dycn_ref[...] = jnp.zeros_like(dycn_ref)
            dgng_ref[...] = jnp.zeros_like(dgng_ref)
            dcw_ref[...] = jnp.zeros_like(dcw_ref)
            dbg_ref[...] = jnp.zeros_like(dbg_ref)
            dwgu_ref[...] = jnp.zeros_like(dwgu_ref)

        @pl.when(step == RELAY_AT * nc // 8)
        def _():
            _stage2_combine(*stage2)

        r_i = lax.broadcasted_iota(jnp.int32, (CHUNK, CHUNK), 0)
        c_i = lax.broadcasted_iota(jnp.int32, (CHUNK, CHUNK), 1)
        tril16 = (r_i >= c_i).astype(BF16)
        triu16 = (r_i <= c_i).astype(BF16)
        causal = r_i >= c_i
        masks = _head_masks()
        cmask = _causal_stack_mask()
        gg = gng_ref[...]
        last_row = lax.broadcasted_iota(jnp.int32, (CHUNK, NQK), 0) == CHUNK - 1
        ones_r = jnp.ones((16, DV), BF16)
        has_prev = (n > 0).astype(F32)
        for b in range(nb):
            q = p_ref[b, :, OQ:OQ + NQK]
            k = p_ref[b, :, OK_:OK_ + NQK]
            z = z_ref[b]
            _, eb, enb, ekl, qi, ki, ks, decb = _chunk_fwd_parts(q, k, z, tril16)
            qi16 = qi.astype(BF16)
            ki16 = ki.astype(BF16)
            qs = _stack_heads(qi, masks).astype(BF16)
            sc = jnp.where(cmask, _nt(qs, ki16), 0.0).astype(BF16)
            st = sp_ref[b, 0]
            st16 = st.astype(BF16)
            dsn = ds_ref[b]
            dsn16 = dsn.astype(BF16)
            v16 = p_ref[b, :, OV:OV + NV].astype(BF16)
            do16 = []
            dgng = jnp.zeros((1, DV), F32)
            for h in range(HEADS):
                cols = slice(DV * h, DV * (h + 1))
                o = o_ref[b, :, cols]
                r = lax.rsqrt(jnp.mean(o * o, axis=-1, keepdims=True) + EPS)
                nh = o * r
                g = p_ref[b, :, OG + DV * h:OG + DV * (h + 1)]
                sg = jax.nn.sigmoid(g)
                dog = dm_ref[b, :, cols]
                dproj_ref[b, :, OG + DV * h:OG + DV * (h + 1)] = (
                    (dog * (nh * gg)) * (sg * (1.0 + g * (1.0 - sg)))).astype(BF16)
                don = dog * (g * sg)
                dgng = dgng + jnp.sum(don * nh, axis=0, keepdims=True)
                dn = don * gg
                do = r * (dn - nh * jnp.mean(dn * nh, axis=-1, keepdims=True))
                do16.append(do.astype(BF16))
            dgng_ref[...] += dgng
            do_rows = jnp.concatenate(do16, axis=0)
            v_rows = jnp.concatenate([v16[:, DV * h:DV * (h + 1)] for h in range(HEADS)], axis=0)
            dp16 = [jnp.where(causal, _nt(do16[h], v16[:, DV * h:DV * (h + 1)]), 0.0).astype(BF16)
                    for h in range(HEADS)]
            ks_dsn = _mm(_stack_heads(ks, masks).astype(BF16), dsn16)
            do_st = _nt(do_rows, st16)
            v_dsn = _nt(v_rows, dsn16)
            dp_ki = _mm(jnp.concatenate(dp16, axis=0), ki16)
            q_do = _tn(qi16, jnp.concatenate(do16, axis=1))
            dki_h = []
            for h in range(HEADS):
                rows = slice(CHUNK * h, CHUNK * (h + 1))
                cols = slice(DV * h, DV * (h + 1))
                dv = _tn(sc[rows], do16[h]) + ks_dsn[rows]
                dproj_ref[b, :, OV + DV * h:OV + DV * (h + 1)] = dv.astype(BF16)
                dki_h.append(_tn(dp16[h], qi16))
                ds_ref[b, rows, :] = decb[rows] * dsn[rows] + q_do[rows, cols]
            blocks = lambda a: [a[CHUNK * h:CHUNK * (h + 1)] for h in range(HEADS)]
            dqi = _merge_heads(blocks(dp_ki + do_st), masks)
            dki = _merge_heads(dki_h, masks)
            dks = _merge_heads(blocks(v_dsn), masks)
            dproj_ref[b, :, OQ:OQ + NQK] = (dqi * (Q_SCALE * eb)).astype(BF16)
            dproj_ref[b, :, OK_:OK_ + NQK] = (dki * enb + dks * ekl).astype(BF16)
            dks_ks = dks * ks
            db = dqi * qi - dki * ki - dks_ks
            sd = _split_bf16(dsn * st * decb, 2)
            dbl = jnp.sum(dks_ks, axis=0, keepdims=True) + (_nt(ones_r, sd[0]) + _nt(ones_r, sd[1]))[0:1, :]
            db = db + jnp.where(last_row, dbl, 0.0)
            db_parts = _split_bf16(db, 3)
            dla = _mm(triu16, db_parts[0]) + _mm(triu16, db_parts[1]) + _mm(triu16, db_parts[2])
            dz = (dla * INV_GATE_NORM) * (1.0 / (1.0 + jnp.exp(z)))
            dbg_ref[...] += jnp.sum(dz, axis=0, keepdims=True)
            dz16 = dz.astype(BF16)
            pa16 = p_ref[b, :, OA:OA + A_PAD].astype(BF16)
            dwgu_ref[...] += _tn(pa16, dz16)
            dproj_ref[b, :, OA:OA + A_PAD] = _nt(dz16, wgu_ref[...]).astype(BF16)
            cb = p_ref[b, :, OCB:OCB + CW]
            cc = p_ref[b, :, OCC:OCC + CW]
            ch = p_ref[b, :, OCH:OCH + CW]
            u = cc * ch
            uprev = (pprev_ref[b, :, 0:CW] * pprev_ref[b, :, CW:2 * CW]) * has_prev
            u1, u2 = _conv_taps(u, uprev)
            w0 = cw_ref[0:1, :]
            w1 = cw_ref[1:2, :]
            w2 = cw_ref[2:3, :]
            yc = w0 * u2 + w1 * u1 + w2 * u
            doc = dm_ref[b, :, NV:NV + CW]
            dproj_ref[b, :, OCB:OCB + CW] = (doc * yc).astype(BF16)
            dyc = doc * cb
            dycn = dycn_ref[b]
            row = lax.broadcasted_iota(jnp.int32, dyc.shape, 0)
            d1 = jnp.where(row >= CHUNK - 1, pltpu.roll(dycn, CHUNK - 1, 0), pltpu.roll(dyc, CHUNK - 1, 0))
            d2 = jnp.where(row >= CHUNK - 2, pltpu.roll(dycn, CHUNK - 2, 0), pltpu.roll(dyc, CHUNK - 2, 0))
            du = w2 * dyc + w1 * d1 + w0 * d2
            dproj_ref[b, :, OCC:OCC + CW] = (du * ch).astype(BF16)
            dproj_ref[b, :, OCH:OCH + CW] = (du * cc).astype(BF16)
            dcw_ref[0:1, :] += jnp.sum(dyc * u2, axis=0, keepdims=True)
            dcw_ref[1:2, :] += jnp.sum(dyc * u1, axis=0, keepdims=True)
            dcw_ref[2:3, :] += jnp.sum(dyc * u, axis=0, keepdims=True)
            dycn_ref[b] = dyc

        @pl.when(step == nc - 1)
        def _():
            _stage2_finish(*stage2)

    rev =lambda w: pl.BlockSpec((nb, CHUNK, w), lambda i: (0, nc - 1 - i, 0))
    const = lambda r, c: pl.BlockSpec((r, c), lambda i: (0, 0))
    hbm = pl.BlockSpec(memory_space=pl.ANY)
    return pl.pallas_call(
        body,
        name="mix_bwd",
        grid=(nc,),
        in_specs=[
            rev(PW),
            pl.BlockSpec((nb, CHUNK, 2 * CW), lambda i: (0, jnp.maximum(nc - 2 - i, 0), OCC // (2 * CW))),
            rev(NQK),
            pl.BlockSpec((nb, 1, NQK, DV), lambda i: (0, nc - 1 - i, 0, 0)),
            rev(NV),
            rev(D),
            const(1, DV),
            const(CONV_K, CW),
            const(A_PAD, NQK),
        ] + [hbm] * na,
        out_specs=[rev(PW), const(1, DV), const(8, CW), const(1, NQK), const(A_PAD, NQK)] + [hbm] * na,
        out_shape=[
            jax.ShapeDtypeStruct((nb, s, PW), BF16),
            jax.ShapeDtypeStruct((1, DV), F32),
            jax.ShapeDtypeStruct((8, CW), F32),
            jax.ShapeDtypeStruct((1, NQK), F32),
            jax.ShapeDtypeStruct((A_PAD, NQK), F32),
        ] + [jax.ShapeDtypeStruct((2,) + p.shape[1:], BF16) for p in pbs],
        scratch_shapes=[pltpu.VMEM((nb, NQK, DV), F32), pltpu.VMEM((nb, CHUNK, CW), F32)] + _stage2_scratch(pbs),
        compiler_params=_params(("arbitrary",)),
    )(proj3, proj3, z3, sprev, opre3, dmix3, gng, conv_w, wgu_p, *pbs)


SMALL_PACK_ROWS = 16


def _wgu_slot(r):
    return 4 + r // 4, NQK * (r % 4)


CONV_SLOTS = ((8, 0), (8, CW), (9, 0))


def _in_proj_bwd(dproj2d, x2d, dx1, g1, w_in_t, tm, pb, small_parts):
    t = x2d.shape[0]
    nt = t // tm

    def body(dp_ref, x_ref, dx1_ref, g_ref, w_ref, pb_ref, dg2, dgf, dbg, dgng, dwgu, dcw, lp,
             dx_ref, sums_ref, r2_ref, dg1_acc, pack, gbuf, pack1, gbuf1, ssend, srecv, ssend1, srecv1, *scratch2):
        stage2 = ([pb_ref], [r2_ref], scratch2)
        x, y, c = _position()
        me = 4 * x + 2 * y + c
        flips = [(k >> 2, (k >> 1) & 1, k & 1) for k in range(1, N_DEV)]
        peers = [(x ^ fx, y ^ fy, c ^ fc) for fx, fy, fc in flips]

        def small_copies(src, dst, send, recv, arrivals):
            return [pltpu.make_async_remote_copy(
                src_ref=src, dst_ref=dst.at[4 * px + 2 * py + pc if arrivals else me],
                send_sem=send.at[k], recv_sem=recv.at[k], device_id=(px, py, pc), device_id_type=MESH)
                for k, (px, py, pc) in enumerate(peers)]

        @pl.when(pl.program_id(0) == 0)
        def _():
            _stage2_start(*stage2)
            dg1_acc[...] = jnp.zeros_like(dg1_acc)
            pack[...] = jnp.zeros_like(pack)
            pack[1:2, :] = dg2[...]
            pack[2:3, :] = dgf[...]
            pack[3:4, 0:NQK] = dbg[...]
            pack[3:4, NQK:NQK + DV] = dgng[...]
            pack[3:4, NQK + DV:NQK + 2 * DV] = lp[...]
            for r in range(RANK):
                row, lane = _wgu_slot(r)
                pack[row:row + 1, lane:lane + NQK] = dwgu[r:r + 1, :]
            for r, (row, lane) in enumerate(CONV_SLOTS):
                pack[row:row + 1, lane:lane + CW] = dcw[r:r + 1, :]
            for cp in small_copies(pack, gbuf, ssend, srecv, False):
                cp.start()
            gbuf[me] = pack[...]

        @pl.when(pl.program_id(0) == RELAY_AT * nt // 8)
        def _():
            _stage2_combine(*stage2)

        xv = x_ref[...]
        r = lax.rsqrt(jnp.mean(xv * xv, axis=-1, keepdims=True) + EPS)
        n1 = xv * r
        dh = _mm(dp_ref[...], w_ref[...])
        dg1_acc[...] += jnp.sum(dh * n1, axis=0, keepdims=True)
        dn = dh * g_ref[...]
        dx_ref[...] = dx1_ref[...] + r * (dn - n1 * jnp.mean(dn * n1, axis=-1, keepdims=True))

        @pl.when(pl.program_id(0) == nt - 1)
        def _():
            pack1[...] = jnp.zeros_like(pack1)
            pack1[0:1, :] = dg1_acc[...]
            for cp in small_copies(pack1, gbuf1, ssend1, srecv1, False):
                cp.start()
            gbuf1[me] = pack1[...]
            _stage2_finish(*stage2)
            for src, dst, send, recv in ((pack, gbuf, ssend, srecv), (pack1, gbuf1, ssend1, srecv1)):
                for cp in small_copies(src, dst, send, recv, True):
                    cp.wait_recv()
                    cp.wait_send()
            acc = gbuf[0]
            acc1 = gbuf1[0]
            for d in range(1, N_DEV):
                acc = acc + gbuf[d]
                acc1 = acc1 + gbuf1[d]
            sums_ref[...] = acc
            sums_ref[0:1, :] = acc1[0:1, :]

    tile = lambda w: pl.BlockSpec((tm, w), lambda i: (i, 0))
    vec = pl.BlockSpec((1, D), lambda i: (0, 0))
    hbm = pl.BlockSpec(memory_space=pl.ANY)
    whole = lambda a: pl.BlockSpec(a.shape, lambda i: (0,) * a.ndim)
    return pl.pallas_call(
        body,
        name="in_proj_bwd",
        grid=(nt,),
        in_specs=[tile(PW), tile(D), tile(D), vec, pl.BlockSpec((PW, D), lambda i: (0, 0)), hbm]
        + [whole(a) for a in small_parts],
        out_specs=[tile(D), pl.BlockSpec((SMALL_PACK_ROWS, D), lambda i: (0, 0)), hbm],
        out_shape=[jax.ShapeDtypeStruct((t, D), F32), jax.ShapeDtypeStruct((SMALL_PACK_ROWS, D), F32),
                   jax.ShapeDtypeStruct((2,) + pb.shape[1:], BF16)],
        scratch_shapes=[pltpu.VMEM((1, D), F32),
                        pltpu.VMEM((SMALL_PACK_ROWS, D), F32), pltpu.VMEM((N_DEV, SMALL_PACK_ROWS, D), F32),
                        pltpu.VMEM((8, D), F32), pltpu.VMEM((N_DEV, 8, D), F32),
                        pltpu.SemaphoreType.DMA((7,)), pltpu.SemaphoreType.DMA((7,)),
                        pltpu.SemaphoreType.DMA((7,)), pltpu.SemaphoreType.DMA((7,))] + _stage2_scratch([pb]),
        compiler_params=_params(("arbitrary",)),
    )(dproj2d, x2d, dx1, g1, w_in_t, pb, *small_parts)


def _get_rows(ref):
    return ref[:, 0, :] if len(ref.shape) == 3 else ref[...]


def _put_rows(ref, val):
    if len(ref.shape) == 3:
        ref[:, 0, :] = val
    else:
        ref[...] = val


def _adamw_math(w, g, m, v):
    m = ADAM_B1 * m + (1.0 - ADAM_B1) * g
    v = ADAM_B2 * v + (1.0 - ADAM_B2) * (g * g)
    m_hat = m / (1.0 - ADAM_B1 ** ADAM_STEP)
    v_hat = v / (1.0 - ADAM_B2 ** ADAM_STEP)
    delta = -ADAM_LR * (m_hat / (jnp.sqrt(v_hat) + ADAM_EPS) + ADAM_WD * w)
    return delta, m, v


def _position():
    return lax.axis_index("x"), lax.axis_index("y"), lax.axis_index("c")


GATHER_PARTS = 2
GATHER_SEMS = 7 * GATHER_PARTS
RELAY_AT = 3


def _gather_copies(stage, lo, rows, gx, send_sems, recv_sems, local_sem):
    x, y, c = _position()
    me = (x, y, c)
    sibling = (x, y, 1 - c)
    chips = [(1 - x, y), (x, 1 - y), (1 - x, 1 - y)]
    part = -(-rows // (16 * GATHER_PARTS)) * 16
    bounds = [(p * part, min(part, rows - p * part)) for p in range(GATHER_PARTS)]

    def blk(px, py, pc, off, n):
        return gx.at[4 * px + 2 * py + pc, pl.ds(off, n), :]

    mine = pltpu.make_async_copy(stage.at[pl.ds(lo, rows), :], gx.at[4 * x + 2 * y + c], local_sem)
    parts = []
    for p, (off, n) in enumerate(bounds):
        def copy(k, block, to, from_stage=False, p=p, off=off, n=n):
            return pltpu.make_async_remote_copy(
                src_ref=stage.at[pl.ds(lo + off, n), :] if from_stage else blk(*block, off, n),
                dst_ref=blk(*block, off, n), send_sem=send_sems.at[7 * p + k], recv_sem=recv_sems.at[7 * p + k],
                device_id=to, device_id_type=MESH)

        first = [copy(0, me, sibling, True)] + [copy(1 + j, me, (*chips[j], c), True) for j in range(2)]
        relay = copy(3, (*chips[p], c), (*chips[1 - p], c))
        passed = [copy(4 + j, (*chip, c), sibling) for j, chip in enumerate(chips)]
        arrivals = ([copy(0, sibling, me)] + [copy(1 + j, (*chip, c), me) for j, chip in enumerate(chips)]
                    + [copy(4 + j, (*chip, 1 - c), me) for j, chip in enumerate(chips)])
        parts.append((first, relay, passed, arrivals))
    return mine, parts


def _gather_start(*args):
    mine, parts = _gather_copies(*args)
    mine.start()
    for first, _, _, _ in parts:
        first[0].start()
    for p, q in ((0, 0), (1, 1), (0, 1), (1, 0)):
        parts[p][0][1 + q].start()


def _gather_relay(*args):
    _, parts = _gather_copies(*args)
    for p, (_, relay, passed, arrivals) in enumerate(parts):
        arrivals[1 + p].wait_recv()
        relay.start()
        passed[p].start()


def _gather_finish(*args):
    mine, parts = _gather_copies(*args)
    for p, (_, _, passed, arrivals) in enumerate(parts):
        for j in range(3):
            if j != p:
                arrivals[1 + j].wait_recv()
                passed[j].start()
    for first, relay, passed, arrivals in parts:
        arrivals[0].wait_recv()
        for j in range(3):
            arrivals[4 + j].wait_recv()
        for cp in first + [relay] + passed:
            cp.wait_send()
    mine.wait()


def _gather_sems():
    return [pltpu.SemaphoreType.DMA((GATHER_SEMS,)), pltpu.SemaphoreType.DMA((GATHER_SEMS,)), pltpu.SemaphoreType.DMA]


def _gather_w_in(w_it, w_gt, w_ut, w_d, w_o, wgu_s, conv_s):
    def body(wi_ref, wg_ref, wu_ref, wd_ref, wo_ref, wgu_ref, conv_ref, w_ref, gwgu_ref, gconv_ref, stage,
             buf, send_sems, recv_sems, local_sem, ssend, srecv):
        x, y, c = _position()
        me = 4 * x + 2 * y + c
        stage[SLAB_IN:SLAB_IN + IN_W, :] = wi_ref[:, 0, :].astype(BF16)
        stage[SLAB_IN + IN_W:SLAB_G, :] = jnp.zeros((IN_ROWS - IN_W, D), BF16)
        args = (stage, SLAB_IN, IN_ROWS, buf, send_sems, recv_sems, local_sem)
        _gather_start(*args)
        stage[SLAB_G:SLAB_U, :] = wg_ref[...].astype(BF16)
        stage[SLAB_U:SLAB_D, :] = wu_ref[...].astype(BF16)
        stage[SLAB_D:SLAB_O, :] = wd_ref[...].astype(BF16)
        stage[SLAB_O:SLAB_ROWS, :] = wo_ref[...].astype(BF16)
        flips = [(k >> 2, (k >> 1) & 1, k & 1) for k in range(1, N_DEV)]
        peers = [(x ^ fx, y ^ fy, c ^ fc) for fx, fy, fc in flips]

        def small(k, block_id, to):
            return [pltpu.make_async_remote_copy(
                src_ref=s, dst_ref=g.at[block_id], send_sem=ssend.at[2 * k + n], recv_sem=srecv.at[2 * k + n],
                device_id=to, device_id_type=MESH)
                for n, (s, g) in enumerate(((wgu_ref, gwgu_ref), (conv_ref, gconv_ref)))]

        gwgu_ref[me] = wgu_ref[...]
        gconv_ref[me] = conv_ref[...]
        for k, peer in enumerate(peers):
            for cp in small(k, me, peer):
                cp.start()
        w_ref[IN_COLS:PW, :] = jnp.zeros((PW - IN_COLS, D), BF16)
        _gather_relay(*args)
        _gather_finish(*args)
        for k, (px, py, pc) in enumerate(peers):
            for cp in small(k, 4 * px + 2 * py + pc, (px, py, pc)):
                cp.wait_recv()
                cp.wait_send()
        for j, lo, hi, d in _in_segments():
            w_ref[d:d + hi - lo, :] = buf[j, lo:hi, :]

    vm = pl.BlockSpec(memory_space=pltpu.VMEM)
    return pl.pallas_call(
        body,
        name="gather_w_in",
        in_specs=[vm] * 7,
        out_specs=[vm] * 4,
        out_shape=[jax.ShapeDtypeStruct((PW, D), BF16),
                   jax.ShapeDtypeStruct((N_DEV,) + wgu_s.shape, F32),
                   jax.ShapeDtypeStruct((N_DEV,) + conv_s.shape, F32),
                   jax.ShapeDtypeStruct((SLAB_ROWS, D), BF16)],
        scratch_shapes=[pltpu.VMEM((N_DEV, IN_ROWS, D), BF16)] + _gather_sems()
        + [pltpu.SemaphoreType.DMA((14,)), pltpu.SemaphoreType.DMA((14,))],
        compiler_params=_params(),
    )(w_it, w_gt, w_ut, w_d, w_o, wgu_s, conv_s)


def _w_in_core_reduce(dw_t):
    def body(d_ref, own_ref, sib_ref, pb_ref, g, gb, r1, send_sems, recv_sems):
        x, y, c = _position()
        chip = 2 * x + y
        for j in range(N_DEV):
            g[j, IN_W:IN_ROWS, :] = jnp.zeros((IN_ROWS - IN_W, D), F32)
        for j, lo, hi, d in _in_segments():
            g[j, lo:hi, :] = d_ref[d:d + hi - lo, :]
        for j in range(N_DEV):
            gb[j] = g[j].astype(BF16)
        copies = _stage1_copies(gb, r1, send_sems, recv_sems)
        for cp in copies:
            cp.start()
        own_ref[0] = g[2 * chip + c]
        for cp in copies:
            cp.wait_recv()
        sib_ref[0] = r1[chip]
        for k in range(1, 4):
            t = chip ^ k
            pb_ref[k - 1] = (g[2 * t + c] + r1[t].astype(F32)).astype(BF16)
        for cp in copies:
            cp.wait_send()

    vm = pl.BlockSpec(memory_space=pltpu.VMEM)
    return pl.pallas_call(
        body,
        name="w_in_core_reduce",
        in_specs=[vm],
        out_specs=[vm, vm, vm],
        out_shape=[jax.ShapeDtypeStruct((1, IN_ROWS, D), F32), jax.ShapeDtypeStruct((1, IN_ROWS, D), BF16),
                   jax.ShapeDtypeStruct((3, IN_ROWS, D), BF16)],
        scratch_shapes=[pltpu.VMEM((N_DEV, IN_ROWS, D), F32), pltpu.VMEM((N_DEV, IN_ROWS, D), BF16),
                        pltpu.VMEM((4, IN_ROWS, D), BF16), pltpu.SemaphoreType.DMA((4,)),
                        pltpu.SemaphoreType.DMA((4,))],
        compiler_params=_params(),
    )(dw_t)


def _stage1_copies(g_ref, r_ref, send_sems, recv_sems):
    x, y, c = _position()
    return [pltpu.make_async_remote_copy(
        src_ref=g_ref.at[2 * i + 1 - c], dst_ref=r_ref.at[i], send_sem=send_sems.at[i], recv_sem=recv_sems.at[i],
        device_id=(x, y, 1 - c), device_id_type=MESH) for i in range(4)]


def _ffn_core_reduce(dw3, dwb3, dw_o, dwb_o, pos_arr, dx1b, gwb):
    def body(pos_ref, g0, g1, g2, go, gb3_hbm, gbo_hbm, dx1b_ref, gwb_hbm, p0, p1, p2, po, s0, s1, s2, so, dmix_ref,
             r1f, r1o, wo, send_sems, recv_sems, wsem):
        step = pl.program_id(0)
        k = jnp.minimum(step, 2)
        x, y, c = _position()
        chip = 2 * x + y

        def copies(p):
            src = 2 * (chip ^ ((p + 1) & 3)) + 1 - c
            pairs = [(gb3_hbm.at[a, src], r1f.at[a, p]) for a in range(3)] + [(gbo_hbm.at[src], r1o.at[p])]
            return [pltpu.make_async_remote_copy(
                src_ref=s, dst_ref=d, send_sem=send_sems.at[4 * p + a], recv_sem=recv_sems.at[4 * p + a],
                device_id=(x, y, 1 - c), device_id_type=MESH) for a, (s, d) in enumerate(pairs)]

        @pl.when(step == 0)
        def _():
            for p in range(4):
                for cp in copies(p):
                    cp.start()
            loads = [pltpu.make_async_copy(gwb_hbm.at[j, pl.ds(D_TAIL, OUT_ROWS), :],
                                           wo.at[pl.ds(OUT_ROWS * j, OUT_ROWS), :], wsem.at[j]) for j in range(N_DEV)]
            for cp in loads:
                cp.start()
            for cp in loads:
                cp.wait()

        dmix_ref[...] = _nt(dx1b_ref[...], wo[...])

        for p in range(3):
            @pl.when(step == p)
            def _():
                for cp in copies(p):
                    cp.wait_recv()

        for a, (g, pb) in enumerate(((g0, p0), (g1, p1), (g2, p2))):
            pb[...] = (g[...] + r1f[a, k][None].astype(F32)).astype(BF16)
        po[...] = (go[...] + r1o[k][None].astype(F32)).astype(BF16)

        @pl.when(step == 3)
        def _():
            for cp in copies(3):
                cp.wait_recv()
            for a, s in enumerate((s0, s1, s2)):
                s[0] = r1f[a, 3]
            so[0] = r1o[3]
            for p in range(4):
                for cp in copies(p):
                    cp.wait_send()

    t = dx1b.shape[0]
    other = lambda s, pos: 2 * (pos[1] ^ (jnp.minimum(s, 2) + 1)) + pos[0]
    g_spec = lambda lead: pl.BlockSpec((None, 1, FF_W, D), lambda s, pos: (lead, other(s, pos), 0, 0))
    slot = lambda rows: pl.BlockSpec((1, rows, D), lambda s, pos: (jnp.minimum(s, 2), 0, 0))
    one = lambda rows: pl.BlockSpec((1, rows, D), lambda s, pos: (0, 0, 0))
    quarter = pl.BlockSpec((t // 4, D), lambda s, pos: (s, 0))
    hbm = pl.BlockSpec(memory_space=pl.ANY)
    return pl.pallas_call(
        body,
        name="ffn_core_reduce",
        grid_spec=pltpu.PrefetchScalarGridSpec(
            num_scalar_prefetch=1, grid=(4,),
            in_specs=[g_spec(0), g_spec(1), g_spec(2),
                      pl.BlockSpec((1, OUT_ROWS, D), lambda s, pos: (other(s, pos), 0, 0)), hbm, hbm, quarter, hbm],
            out_specs=[slot(FF_W), slot(FF_W), slot(FF_W), slot(OUT_ROWS),
                       one(FF_W), one(FF_W), one(FF_W), one(OUT_ROWS), quarter],
            scratch_shapes=[pltpu.VMEM((3, 4, FF_W, D), BF16), pltpu.VMEM((4, OUT_ROWS, D), BF16),
                            pltpu.VMEM((D, D), BF16), pltpu.SemaphoreType.DMA((16,)),
                            pltpu.SemaphoreType.DMA((16,)), pltpu.SemaphoreType.DMA((N_DEV,))]),
        out_shape=[jax.ShapeDtypeStruct((3, FF_W, D), BF16)] * 3 + [jax.ShapeDtypeStruct((3, OUT_ROWS, D), BF16)]
        + [jax.ShapeDtypeStruct((1, FF_W, D), BF16)] * 3 + [jax.ShapeDtypeStruct((1, OUT_ROWS, D), BF16),
                                                             jax.ShapeDtypeStruct((t, D), F32)],
        compiler_params=_params(("arbitrary",)),
    )(pos_arr, dw3, dw3, dw3, dw_o, dwb3, dwb_o, dx1b, gwb)


def _stage2_scratch(pbs):
    n = len(pbs)
    return ([pltpu.VMEM(p.shape[1:], BF16) for p in pbs] * 2
            + [pltpu.SemaphoreType.DMA((6 * n,)), pltpu.SemaphoreType.DMA((6 * n,)), pltpu.SemaphoreType.DMA((2 * n,))])


def _stage2_copies(p_refs, r_refs, scratch):
    n = len(p_refs)
    owns, gots = scratch[:n], scratch[n:2 * n]
    send_sems, recv_sems, load_sems = scratch[2 * n:]
    x, y, c = _position()
    xn, yn = (1 - x, y, c), (x, 1 - y, c)
    loads, first, second = [], [], []
    for a, (p, r, own, got) in enumerate(zip(p_refs, r_refs, owns, gots)):
        rows = p.shape[1]
        half = -(-rows // 32) * 16
        h0, h1 = pl.ds(0, half), pl.ds(half, rows - half)

        def remote(k, src, dst, to, a=a):
            return pltpu.make_async_remote_copy(
                src_ref=src, dst_ref=dst, send_sem=send_sems.at[6 * a + k], recv_sem=recv_sems.at[6 * a + k],
                device_id=to, device_id_type=MESH)

        loads += [pltpu.make_async_copy(p.at[0, h0, :], own.at[h0, :], load_sems.at[2 * a]),
                  pltpu.make_async_copy(p.at[1, h1, :], own.at[h1, :], load_sems.at[2 * a + 1])]
        first += [remote(0, p.at[2, h0, :], got.at[h0, :], xn), remote(1, p.at[2, h1, :], got.at[h1, :], yn),
                  remote(2, p.at[1, h0, :], r.at[1, h0, :], xn), remote(3, p.at[0, h1, :], r.at[0, h1, :], yn)]
        second += [remote(4, own.at[h0, :], r.at[0, h0, :], yn), remote(5, own.at[h1, :], r.at[1, h1, :], xn)]
    return loads, first, second


def _stage2_start(p_refs, r_refs, scratch):
    loads, first, _ = _stage2_copies(p_refs, r_refs, scratch)
    for cp in loads:
        cp.start()
    for k in range(4):
        for cp in first[k::4]:
            cp.start()


def _stage2_combine(p_refs, r_refs, scratch):
    n = len(p_refs)
    loads, first, second = _stage2_copies(p_refs, r_refs, scratch)
    for a in range(n):
        for cp in loads[2 * a:2 * a + 2]:
            cp.wait()
        for cp in first[4 * a:4 * a + 2]:
            cp.wait_recv()
        own, got = scratch[a], scratch[n + a]
        own[...] = (own[...].astype(F32) + got[...].astype(F32)).astype(BF16)
        for cp in second[2 * a:2 * a + 2]:
            cp.start()


def _stage2_finish(p_refs, r_refs, scratch):
    _, first, second = _stage2_copies(p_refs, r_refs, scratch)
    for a in range(len(p_refs)):
        for cp in first[4 * a + 2:4 * a + 4] + second[2 * a:2 * a + 2]:
            cp.wait_recv()
    for cp in first + second:
        cp.wait_send()


def _finish_weights(items, pos_arr, name, nblk):
    n = len(items)
    in_specs, out_specs, out_shape, operands, wbs = [], [], [], [], []
    for g8, lead, r1, r2, w, m, v in items:
        rows, wr = g8.shape[-2], w.shape[0]
        assert rows % nblk == 0 and wr % nblk == 0 and (nblk == 1 or (rows == wr and rows % (16 * nblk) == 0))
        rb, wb = rows // nblk, wr // nblk
        if lead is not None:
            g_spec = pl.BlockSpec((None, 1, rb, D), lambda i, pos, lead=lead: (lead, 2 * pos[1] + pos[0], i, 0))
        elif g8.shape[0] == 1:
            g_spec = pl.BlockSpec((1, rb, D), lambda i, pos: (0, i, 0))
        else:
            g_spec = pl.BlockSpec((1, rb, D), lambda i, pos: (2 * pos[1] + pos[0], i, 0))
        r1_spec = pl.BlockSpec((1, rb, D), lambda i, pos: (0, i, 0))
        if w.ndim == 3:
            wblk = pl.BlockSpec((wb, 1, D), lambda i, pos: (i, 0, 0))
        else:
            wblk = pl.BlockSpec((wb, D), lambda i, pos: (i, 0))
        in_specs += [g_spec, r1_spec, pl.BlockSpec((2, rb, D), lambda i, pos: (0, i, 0)), wblk, wblk, wblk]
        out_specs += [wblk] * 4
        out_shape += [jax.ShapeDtypeStruct(w.shape, F32)] * 4
        operands += [g8, r1, r2, w, m, v]
        wbs.append(wb)

    def body(pos_ref, *refs):
        for a in range(n):
            g_ref, r1_ref, r2_ref, w_ref, m_ref, v_ref = refs[6 * a:6 * a + 6]
            g_out, d_out, m_out, v_out = refs[6 * n + 4 * a:6 * n + 4 * a + 4]
            g = g_ref[0] + r1_ref[0].astype(F32)
            for k in range(2):
                g = g + r2_ref[k].astype(F32)
            g = g[0:wbs[a], :]
            d, mn, vn = _adamw_math(_get_rows(w_ref), g, _get_rows(m_ref), _get_rows(v_ref))
            for out, val in ((g_out, g), (d_out, d), (m_out, mn), (v_out, vn)):
                _put_rows(out, val)

    return pl.pallas_call(
        body,
        name=name,
        grid_spec=pltpu.PrefetchScalarGridSpec(
            num_scalar_prefetch=1, grid=(nblk,), in_specs=in_specs, out_specs=out_specs),
        out_shape=out_shape,
        compiler_params=_params(("arbitrary",)),
    )(pos_arr, *operands)


SMALL_NAMES = ("norm1_g", "norm2_g", "norm_f_g", "b_gate", "gla_norm_g", "w_gate_up", "conv_w")
WGU_W = NQK // N_DEV
CONV_W = CW // N_DEV


def _small_adamw(sums, ws, ms, vs):
    n = len(SMALL_NAMES)

    def body(*refs):
        acc_ref = refs[0]
        w_refs, m_refs, v_refs = refs[1:1 + n], refs[1 + n:1 + 2 * n], refs[1 + 2 * n:1 + 3 * n]
        loss_ref = refs[1 + 3 * n]
        outs = refs[2 + 3 * n:]
        x, y, c = _position()
        me = 4 * x + 2 * y + c
        acc = acc_ref[...]
        loss_ref[...] = acc[3:4, NQK + DV:NQK + DV + 1]

        def my_columns(full, width):
            r = lax.broadcasted_iota(jnp.int32, (full.shape[1], width), 0)
            col = lax.broadcasted_iota(jnp.int32, (full.shape[1], width), 1)
            sel = (r == width * me + col).astype(F32)
            return _mm(full, sel, precision=HIGHEST)

        dwgu = jnp.concatenate([acc[row:row + 1, lane:lane + NQK] for row, lane in map(_wgu_slot, range(RANK))], axis=0)
        dcw = jnp.concatenate([acc[row:row + 1, lane:lane + CW] for row, lane in CONV_SLOTS], axis=0)
        grads = [acc[0:1, :], acc[1:2, :], acc[2:3, :], acc[3:4, 0:NQK], acc[3:4, NQK:NQK + DV],
                 my_columns(dwgu, WGU_W), my_columns(dcw, CONV_W)]
        for i, g in enumerate(grads):
            d, mn, vn = _adamw_math(_get_rows(w_refs[i]), g, _get_rows(m_refs[i]), _get_rows(v_refs[i]))
            for out, val in zip(outs[4 * i:4 * i + 4], (g, d, mn, vn)):
                _put_rows(out, val)

    vm = pl.BlockSpec(memory_space=pltpu.VMEM)
    out_shape = [jax.ShapeDtypeStruct((1, 1), F32)]
    for w in ws:
        out_shape += [jax.ShapeDtypeStruct(w.shape, F32)] * 4
    return pl.pallas_call(
        body,
        name="small_adamw",
        in_specs=[vm] * (1 + 3 * n),
        out_specs=[vm] * (1 + 4 * n),
        out_shape=out_shape,
        compiler_params=_params(),
    )(sums, *ws, *ms, *vs)


def kernel(x, norm1_g, w_in, w_gate_up, b_gate, gla_norm_g, conv_w, w_out, norm2_g, w_ffn_gate, w_ffn_up, w_ffn_down, norm_f_g, loss_target, m_norm1_g, m_w_in, m_w_gate_up, m_b_gate, m_gla_norm_g, m_conv_w, m_w_out, m_norm2_g, m_w_ffn_gate, m_w_ffn_up, m_w_ffn_down, m_norm_f_g, v_norm1_g, v_w_in, v_w_gate_up, v_b_gate, v_gla_norm_g, v_conv_w, v_w_out, v_norm2_g, v_w_ffn_gate, v_w_ffn_up, v_w_ffn_down, v_norm_f_g):
    xi, yi, ci = _position()
    pos_arr = jnp.stack([ci, 2 * xi + yi]).astype(jnp.int32)
    nb, s, _ = x.shape
    t = nb * s

    tr = lambda a: a[0].T
    rows_of = lambda a: a.transpose(2, 0, 1)
    conv_rows = lambda a: a.transpose(1, 0, 2)
    w_in_t, gwgu, gconv, stage = _gather_w_in(rows_of(w_in), tr(w_ffn_gate), tr(w_ffn_up), w_ffn_down[0], w_out[0],
                                              w_gate_up[0], conv_rows(conv_w))
    wgu_f = gwgu.transpose(1, 0, 2).reshape(RANK, NQK)
    conv_f = gconv.transpose(1, 2, 0, 3).reshape(CONV_K, CW)
    wgu_p = jnp.concatenate([wgu_f, jnp.zeros((A_PAD - RANK, NQK), F32)], axis=0).astype(BF16)

    x2d = x.reshape(t, D)
    tgt2d = loss_target.reshape(t, D)
    tm = 256
    tm_in = min(512, t)
    tk = min(2048, t)
    proj, z, h, gwb = _in_proj_fwd(x2d, norm1_g, w_in_t, wgu_p, b_gate, tm_in, stage)
    proj3 = proj.reshape(nb, s, PW)
    z3 = z.reshape(nb, s, NQK)
    mix3, opre3, sprev, x1, gwa = _mix_fwd(proj3, z3, gla_norm_g, conv_f, stage, x, gwb)
    mix2d = mix3.reshape(t, D)
    dx1, dx1b, adu, hb, dg2, dgf, loss_part = _ffn_fwd_bwd(
        x1.reshape(t, D), tgt2d, gwa, gwb, norm2_g, norm_f_g.reshape(1, D), tm)
    dw3, dwb3 = _dw_ffn(adu, hb, tk)
    dw3 = dw3.reshape(3, N_DEV, FF_W, D)
    dw_o, dwb_o = _tn_matmul(mix2d, dx1b, D // 2, D, tk, "dw_out", True)
    dw_o = dw_o.reshape(N_DEV, OUT_ROWS, D)
    *pb, sib_d, sib_g, sib_u, sib_o, dmix = _ffn_core_reduce(
        dw3, dwb3.reshape(3, N_DEV, FF_W, D), dw_o, dwb_o.reshape(N_DEV, OUT_ROWS, D), pos_arr, dx1b, gwb)
    g8 = [dw3, dw3, dw3, dw_o]
    leads = [0, 1, 2, None]
    tags = ("w_ffn_down", "w_ffn_gate", "w_ffn_up", "w_out")
    r1 = [sib_d, sib_g, sib_u, sib_o]
    mb = _mix_bwd(proj3, z3, sprev, opre3, dmix.reshape(nb, s, D), gla_norm_g, conv_f, wgu_p, [pb[0], pb[1], pb[3]])
    dproj3, dgng, dcw, dbg, dwgu = mb[:5]
    dproj2d = dproj3.reshape(t, PW)
    dw_in_t, r2_up = _tn_matmul(dproj2d, h, PW // 5, D, t, "dw_in", False, _stage2_rider([pb[2]]))
    r2 = [mb[5], mb[6], r2_up, mb[7]]
    g_in, r1_in, pb_in = _w_in_core_reduce(dw_in_t)
    dx, small_sums, r2_in = _in_proj_bwd(dproj2d, x2d, dx1, norm1_g, w_in_t, tm_in, pb_in,
                                         (dg2, dgf, dbg, dgng, dwgu, dcw, loss_part))

    tags = ("w_in",) + tags
    g8 = [g_in] + g8
    leads = [None] + leads
    r1 = [r1_in] + list(r1)
    r2 = [r2_in] + r2
    shard_w = (rows_of(w_in), w_ffn_down[0], tr(w_ffn_gate), tr(w_ffn_up), w_out[0])
    shard_m = (rows_of(m_w_in), m_w_ffn_down[0], tr(m_w_ffn_gate), tr(m_w_ffn_up), m_w_out[0])
    shard_v = (rows_of(v_w_in), v_w_ffn_down[0], tr(v_w_ffn_gate), tr(v_w_ffn_up), v_w_out[0])
    back = (lambda o: o.transpose(1, 2, 0), lambda o: o[None], lambda o: o.T[None], lambda o: o.T[None],
            lambda o: o[None])
    items = list(zip(g8, leads, r1, r2, shard_w, shard_m, shard_v))
    flat = list(_finish_weights(items[1:], pos_arr, "finish_ffn_out", 2))
    flat = list(_finish_weights(items[:1], pos_arr, "finish_w_in", 1)) + flat
    results = {}
    for i, (tag, to_shard) in enumerate(zip(tags, back)):
        results[tag] = [to_shard(o) for o in flat[4 * i:4 * i + 4]]

    small_w = (norm1_g, norm2_g, norm_f_g.reshape(1, D), b_gate, gla_norm_g, w_gate_up[0], conv_rows(conv_w))
    small_m = (m_norm1_g, m_norm2_g, m_norm_f_g.reshape(1, D), m_b_gate, m_gla_norm_g, m_w_gate_up[0],
               conv_rows(m_conv_w))
    small_v = (v_norm1_g, v_norm2_g, v_norm_f_g.reshape(1, D), v_b_gate, v_gla_norm_g, v_w_gate_up[0],
               conv_rows(v_conv_w))
    so = _small_adamw(small_sums, small_w, small_m, small_v)
    loss = so[0].reshape(())
    to_shape = {"norm_f_g": lambda o: o.reshape(D), "w_gate_up": lambda o: o[None],
                "conv_w": lambda o: o.transpose(1, 0, 2)}
    for i, name in enumerate(SMALL_NAMES):
        results[name] = [to_shape.get(name, lambda o: o)(o) for o in so[1 + 4 * i:5 + 4 * i]]

    names = ("norm1_g", "w_in", "w_gate_up", "b_gate", "gla_norm_g", "conv_w", "w_out", "norm2_g",
             "w_ffn_gate", "w_ffn_up", "w_ffn_down", "norm_f_g")
    outs = [loss, dx.reshape(nb, s, D)]
    for kind in range(4):
        for name in names:
            outs.append(results[name][kind])
    return tuple(outs)
```

```python
import jax
import jax.numpy as jnp
from jax import lax
from jax.experimental import pallas as pl
from jax.experimental.pallas import tpu as pltpu

F32 = jnp.float32
BF16 = jnp.bfloat16
HIGHEST = lax.Precision.HIGHEST
MESH = pl.DeviceIdType.MESH

N_DEV = 8
D = 1024
DFF = 2816
HEADS = 4
DK = 64
DV = 128
NQK = HEADS * DK
NV = HEADS * DV
RANK = 16
CHUNK = 64
CW = 512
CONV_K = 3
IN_COLS = 3088
EPS = 1e-6
INV_GATE_NORM = 1.0 / 16.0
Q_SCALE = DK ** -0.5

PW = 3200
OQ, OK_, OV, OG, OCB, OCC, OCH, OA = 0, 256, 512, 1024, 1536, 2048, 2560, 3072
A_PAD = 128

ADAM_LR = 0.001
ADAM_B1 = 0.9
ADAM_B2 = 0.999
ADAM_EPS = 1e-08
ADAM_WD = 0.01
ADAM_STEP = 10

IN_W = IN_COLS // N_DEV
IN_ROWS = 400
FF_W = DFF // N_DEV
OUT_ROWS = D // N_DEV
SLAB_IN = 0
SLAB_G = SLAB_IN + IN_ROWS
SLAB_U = SLAB_G + FF_W
SLAB_D = SLAB_U + FF_W
SLAB_O = SLAB_D + FF_W
SLAB_ROWS = SLAB_O + OUT_ROWS
D_HEAD = 128
D_TAIL = FF_W - D_HEAD
SLAB_SPLIT = SLAB_D + D_HEAD

VMEM_LIMIT = 56 * 1024 * 1024


def _params(sem=None, vmem=VMEM_LIMIT):
    return pltpu.CompilerParams(dimension_semantics=sem, vmem_limit_bytes=vmem)


def _nt(a, b):
    return lax.dot_general(a, b, (((1,), (1,)), ((), ())), preferred_element_type=F32)


def _tn(a, b, precision=None):
    return lax.dot_general(a, b, (((0,), (0,)), ((), ())), preferred_element_type=F32, precision=precision)


def _mm(a, b, precision=None):
    return jnp.dot(a, b, preferred_element_type=F32, precision=precision)


def _in_segments():
    segs = []
    for j in range(N_DEV):
        lo, hi = IN_W * j, IN_W * (j + 1)
        cuts = sorted({lo, hi} | {c for c in (OCB, OCB + RANK) if lo < c < hi})
        for a, b in zip(cuts[:-1], cuts[1:]):
            if a < OCB:
                d = a
            elif a < OCB + RANK:
                d = OA + (a - OCB)
            else:
                d = a - RANK
            segs.append((j, a - lo, b - lo, d))
    return segs


def _in_proj_fwd(x2d, g1, w_in_t, wgu_p, b_gate, tm, stage):
    t = x2d.shape[0]
    nt = t // tm
    g_rows = SLAB_ROWS - SLAB_SPLIT

    def body(x_ref, g_ref, w_ref, wgu_ref, bg_ref, stage_hbm, proj_ref, z_ref, h_ref, gwb_ref,
             send_sems, recv_sems, local_sem):
        gargs = (stage_hbm, SLAB_SPLIT, g_rows, gwb_ref, send_sems, recv_sems, local_sem)

        @pl.when(pl.program_id(0) == 0)
        def _():
            _gather_start(*gargs)

        @pl.when(pl.program_id(0) == RELAY_AT * nt // 8)
        def _():
            _gather_relay(*gargs)

        @pl.when(pl.program_id(0) == FORWARD_AT * nt // 8)
        def _():
            _gather_forward(*gargs)

        x = x_ref[...]
        r = lax.rsqrt(jnp.mean(x * x, axis=-1, keepdims=True) + EPS)
        h = ((x * r) * g_ref[...]).astype(BF16)
        h_ref[...] = h
        proj = _nt(h, w_ref[...])
        proj_ref[...] = proj
        pa = proj[:, OA:OA + A_PAD].astype(BF16)
        z_ref[...] = _mm(pa, wgu_ref[...]) + bg_ref[...]

        @pl.when(pl.program_id(0) == nt - 1)
        def _():
            _gather_finish(*gargs)

    return pl.pallas_call(
        body,
        name="in_proj_fwd",
        grid=(t // tm,),
        in_specs=[
            pl.BlockSpec((tm, D), lambda i: (i, 0)),
            pl.BlockSpec((1, D), lambda i: (0, 0)),
            pl.BlockSpec((PW, D), lambda i: (0, 0)),
            pl.BlockSpec((A_PAD, NQK), lambda i: (0, 0)),
            pl.BlockSpec((1, NQK), lambda i: (0, 0)),
            pl.BlockSpec(memory_space=pl.ANY),
        ],
        out_specs=[
            pl.BlockSpec((tm, PW), lambda i: (i, 0)),
            pl.BlockSpec((tm, NQK), lambda i: (i, 0)),
            pl.BlockSpec((tm, D), lambda i: (i, 0)),
            pl.BlockSpec(memory_space=pl.ANY),
        ],
        out_shape=[
            jax.ShapeDtypeStruct((t, PW), F32),
            jax.ShapeDtypeStruct((t, NQK), F32),
            jax.ShapeDtypeStruct((t, D), BF16),
            jax.ShapeDtypeStruct((N_DEV, g_rows, D), BF16),
        ],
        scratch_shapes=_gather_sems(),
        compiler_params=_params(("arbitrary",)),
    )(x2d, g1, w_in_t, wgu_p, b_gate, stage)


def _head_masks():
    lane = lax.broadcasted_iota(jnp.int32, (1, NQK), 1)
    return [(lane >= DK * h) & (lane < DK * (h + 1)) for h in range(HEADS)]


def _split_bf16(x, n):
    parts = []
    for _ in range(n):
        p = x.astype(BF16)
        parts.append(p)
        x = x - p.astype(F32)
    return parts


def _chunk_fwd_parts(q, k, z, tril16):
    la = (jnp.minimum(z, 0.0) - jnp.log1p(jnp.exp(-jnp.abs(z)))) * INV_GATE_NORM
    la_parts = _split_bf16(la, 3)
    bc = _mm(tril16, la_parts[0]) + _mm(tril16, la_parts[1]) + _mm(tril16, la_parts[2])
    bl = bc[CHUNK - 1:CHUNK, :]
    eb = jnp.exp(bc)
    enb = jnp.exp(-bc)
    ekl = jnp.exp(bl - bc)
    qi = (q * Q_SCALE) * eb
    ki = k * enb
    ks = k * ekl
    ones16 = jnp.ones((CHUNK, DV), BF16)
    decb = jnp.exp(_tn(la_parts[0], ones16) + _tn(la_parts[1], ones16) + _tn(la_parts[2], ones16))
    return la, eb, enb, ekl, qi, ki, ks, decb


def _stack_heads(a, masks):
    return jnp.concatenate([jnp.where(m, a, 0.0) for m in masks], axis=0)


def _merge_heads(blocks, masks):
    out = blocks[HEADS - 1]
    for h in range(HEADS - 2, -1, -1):
        out = jnp.where(masks[h], blocks[h], out)
    return out


def _causal_stack_mask():
    row = lax.broadcasted_iota(jnp.int32, (HEADS * CHUNK, CHUNK), 0)
    col = lax.broadcasted_iota(jnp.int32, (HEADS * CHUNK, CHUNK), 1)
    return (row & (CHUNK - 1)) >= col


def _conv_taps(u, uprev):
    row = lax.broadcasted_iota(jnp.int32, u.shape, 0)
    u1 = jnp.where(row < 1, pltpu.roll(uprev, 1, 0), pltpu.roll(u, 1, 0))
    u2 = jnp.where(row < 2, pltpu.roll(uprev, 2, 0), pltpu.roll(u, 2, 0))
    return u1, u2


def _mix_fwd(proj3, z3, gng, conv_w, stage, x3, gwb):
    nb, s, _ = proj3.shape
    nc = s // CHUNK
    g_rows = SLAB_SPLIT - SLAB_G

    def body(p_ref, z_ref, gng_ref, cw_ref, stage_hbm, x_ref, gwb_hbm, mix_ref, o_ref, sprev_ref, x1_ref, gwa_ref,
             s_ref, uprev_ref, wo, wsem, send_sems, recv_sems, local_sem):
        n = pl.program_id(0)
        gargs = (stage_hbm, SLAB_G, g_rows, gwa_ref, send_sems, recv_sems, local_sem)

        @pl.when(n == 0)
        def _():
            _gather_start(*gargs)
            loads = [pltpu.make_async_copy(gwb_hbm.at[j, pl.ds(D_TAIL, OUT_ROWS), :],
                                           wo.at[pl.ds(OUT_ROWS * j, OUT_ROWS), :], wsem.at[j]) for j in range(N_DEV)]
            for cp in loads:
                cp.start()
            s_ref[...] = jnp.zeros_like(s_ref)
            uprev_ref[...] = jnp.zeros_like(uprev_ref)
            for cp in loads:
                cp.wait()

        @pl.when(n == RELAY_AT * nc // 8)
        def _():
            _gather_relay(*gargs)

        @pl.when(n == FORWARD_AT * nc // 8)
        def _():
            _gather_forward(*gargs)

        r_i = lax.broadcasted_iota(jnp.int32, (CHUNK, CHUNK), 0)
        c_i = lax.broadcasted_iota(jnp.int32, (CHUNK, CHUNK), 1)
        tril16 = (r_i >= c_i).astype(BF16)
        masks = _head_masks()
        cmask = _causal_stack_mask()
        gg = gng_ref[...]
        for b in range(nb):
            q = p_ref[b, :, OQ:OQ + NQK]
            k = p_ref[b, :, OK_:OK_ + NQK]
            _, _, _, _, qi, ki, ks, decb = _chunk_fwd_parts(q, k, z_ref[b], tril16)
            qs = _stack_heads(qi, masks).astype(BF16)
            sc = jnp.where(cmask, _nt(qs, ki.astype(BF16)), 0.0).astype(BF16)
            st = s_ref[b]
            sprev_ref[b, 0] = st
            o_inter = _mm(qs, st.astype(BF16))
            v16 = p_ref[b, :, OV:OV + NV].astype(BF16)
            kv = _tn(ks.astype(BF16), v16)
            for h in range(HEADS):
                rows = slice(CHUNK * h, CHUNK * (h + 1))
                cols = slice(DV * h, DV * (h + 1))
                o = _mm(sc[rows], v16[:, cols]) + o_inter[rows]
                o_ref[b, :, cols] = o
                r = lax.rsqrt(jnp.mean(o * o, axis=-1, keepdims=True) + EPS)
                on = (o * r) * gg
                g = p_ref[b, :, OG + DV * h:OG + DV * (h + 1)]
                mix_ref[b, :, cols] = (on * (g * jax.nn.sigmoid(g))).astype(BF16)
                s_ref[b, rows, :] = decb[rows] * st[rows] + kv[rows, cols]
            u = p_ref[b, :, OCC:OCC + CW] * p_ref[b, :, OCH:OCH + CW]
            u1, u2 = _conv_taps(u, uprev_ref[b])
            yc = cw_ref[0:1, :] * u2 + cw_ref[1:2, :] * u1 + cw_ref[2:3, :] * u
            mix_ref[b, :, NV:NV + CW] = (p_ref[b, :, OCB:OCB + CW] * yc).astype(BF16)
            uprev_ref[b] = u
        mixed = _mm(jnp.concatenate([mix_ref[b] for b in range(nb)], axis=0), wo[...])
        for b in range(nb):
            x1_ref[b] = x_ref[b] + mixed[CHUNK * b:CHUNK * (b + 1)]

        @pl.when(n == nc - 1)
        def _():
            _gather_finish(*gargs)

    return pl.pallas_call(
        body,
        name="mix_fwd",
        grid=(nc,),
        in_specs=[
            pl.BlockSpec((nb, CHUNK, PW), lambda n: (0, n, 0)),
            pl.BlockSpec((nb, CHUNK, NQK), lambda n: (0, n, 0)),
            pl.BlockSpec((1, DV), lambda n: (0, 0)),
            pl.BlockSpec((CONV_K, CW), lambda n: (0, 0)),
            pl.BlockSpec(memory_space=pl.ANY),
            pl.BlockSpec((nb, CHUNK, D), lambda n: (0, n, 0)),
            pl.BlockSpec(memory_space=pl.ANY),
        ],
        out_specs=[
            pl.BlockSpec((nb, CHUNK, D), lambda n: (0, n, 0)),
            pl.BlockSpec((nb, CHUNK, NV), lambda n: (0, n, 0)),
            pl.BlockSpec((nb, 1, NQK, DV), lambda n: (0, n, 0, 0)),
            pl.BlockSpec((nb, CHUNK, D), lambda n: (0, n, 0)),
            pl.BlockSpec(memory_space=pl.ANY),
        ],
        out_shape=[
            jax.ShapeDtypeStruct((nb, s, D), BF16),
            jax.ShapeDtypeStruct((nb, s, NV), F32),
            jax.ShapeDtypeStruct((nb, nc, NQK, DV), F32),
            jax.ShapeDtypeStruct((nb, s, D), F32),
            jax.ShapeDtypeStruct((N_DEV, g_rows, D), BF16),
        ],
        scratch_shapes=[pltpu.VMEM((nb, NQK, DV), F32), pltpu.VMEM((nb, CHUNK, CW), F32),
                        pltpu.VMEM((D, D), BF16), pltpu.SemaphoreType.DMA((N_DEV,))] + _gather_sems(),
        compiler_params=_params(("arbitrary",)),
    )(proj3, z3, gng, conv_w, stage, x3, gwb)


def _ffn_fwd_bwd(x1_2d, tgt2d, gwa, gwb, g2, gf, tm):
    t = x1_2d.shape[0]

    def body(x1_ref, tgt_ref, g2_ref, gf_ref, gwa_hbm, gwb_hbm,
             dx1_ref, dx1b_ref, adu_ref, hb_ref, dg2_ref, dgf_ref, loss_ref,
             wg, wu, wd, wsem):
        i = pl.program_id(0)

        def weight_copies(n, dst, src, off, rows, at=0):
            return [pltpu.make_async_copy(src.at[j, pl.ds(off, rows), :], dst.at[pl.ds(FF_W * j + at, rows), :],
                                          wsem.at[N_DEV * n + j]) for j in range(N_DEV)]

        loads = (weight_copies(0, wg, gwa_hbm, 0, FF_W), weight_copies(1, wu, gwa_hbm, FF_W, FF_W),
                 weight_copies(2, wd, gwa_hbm, 2 * FF_W, D_HEAD), weight_copies(3, wd, gwb_hbm, 0, D_TAIL, D_HEAD))

        @pl.when(i == 0)
        def _():
            for group in loads:
                for cp in group:
                    cp.start()
            dg2_ref[...] = jnp.zeros_like(dg2_ref)
            dgf_ref[...] = jnp.zeros_like(dgf_ref)
            loss_ref[...] = jnp.zeros_like(loss_ref)
            for group in loads:
                for cp in group:
                    cp.wait()

        g2v = g2_ref[...]
        gfv = gf_ref[...]
        x1 = x1_ref[...]
        r2 = lax.rsqrt(jnp.mean(x1 * x1, axis=-1, keepdims=True) + EPS)
        n2 = x1 * r2
        h2 = (n2 * g2v).astype(BF16)
        hb_ref[1] = h2
        gate = _nt(h2, wg[...])
        up = _nt(h2, wu[...])
        sg = jax.nn.sigmoid(gate)
        sil = gate * sg
        act = (sil * up).astype(BF16)
        adu_ref[0] = act
        x2 = x1 + _mm(act, wd[...])
        rf = lax.rsqrt(jnp.mean(x2 * x2, axis=-1, keepdims=True) + EPS)
        nf = x2 * rf
        err = nf * gfv - tgt_ref[...]
        loss_ref[...] += 0.5 * jnp.sum(jnp.mean(err * err, axis=-1, keepdims=True))
        dy = err * (1.0 / D)
        dgf_ref[...] += jnp.sum(dy * nf, axis=0, keepdims=True)
        dnf = dy * gfv
        dx2 = rf * (dnf - nf * jnp.mean(dnf * nf, axis=-1, keepdims=True))
        dx2b = dx2.astype(BF16)
        hb_ref[0] = dx2b
        dact = _nt(dx2b, wd[...])
        dup = (dact * sil).astype(BF16)
        dgate = ((dact * up) * (sg * (1.0 + gate * (1.0 - sg)))).astype(BF16)
        adu_ref[2] = dup
        adu_ref[1] = dgate
        dh2 = _mm(dgate, wg[...]) + _mm(dup, wu[...])
        dg2_ref[...] += jnp.sum(dh2 * n2, axis=0, keepdims=True)
        dn2 = dh2 * g2v
        dx1 = dx2 + r2 * (dn2 - n2 * jnp.mean(dn2 * n2, axis=-1, keepdims=True))
        dx1_ref[...] = dx1
        dx1b_ref[...] = dx1.astype(BF16)

    tile = lambda w: pl.BlockSpec((tm, w), lambda i: (i, 0))
    vec = pl.BlockSpec((1, D), lambda i: (0, 0))
    hbm = pl.BlockSpec(memory_space=pl.ANY)
    return pl.pallas_call(
        body,
        name="ffn_fwd_bwd",
        grid=(t // tm,),
        in_specs=[tile(D), tile(D), vec, vec, hbm, hbm],
        out_specs=[tile(D), tile(D), pl.BlockSpec((3, tm, DFF), lambda i: (0, i, 0)),
                   pl.BlockSpec((2, tm, D), lambda i: (0, i, 0)), vec, vec,
                   pl.BlockSpec((1, 128), lambda i: (0, 0))],
        out_shape=[
            jax.ShapeDtypeStruct((t, D), F32),
            jax.ShapeDtypeStruct((t, D), BF16),
            jax.ShapeDtypeStruct((3, t, DFF), BF16),
            jax.ShapeDtypeStruct((2, t, D), BF16),
            jax.ShapeDtypeStruct((1, D), F32),
            jax.ShapeDtypeStruct((1, D), F32),
            jax.ShapeDtypeStruct((1, 128), F32),
        ],
        scratch_shapes=[pltpu.VMEM((DFF, D), BF16), pltpu.VMEM((DFF, D), BF16), pltpu.VMEM((DFF, D), BF16),
                        pltpu.SemaphoreType.DMA((4 * N_DEV,))],
        compiler_params=_params(("arbitrary",)),
    )(x1_2d, tgt2d, g2, gf, gwa, gwb)


def _stage2_rider(pbs):
    return dict(inputs=list(pbs), out_shape=[jax.ShapeDtypeStruct((2,) + p.shape[1:], BF16) for p in pbs],
                scratch=_stage2_scratch(pbs))


def _tn_matmul(a, b, bm, bn, tk, name, with_bf16, rider=None):
    t, m = a.shape
    n = b.shape[1]
    nk = t // tk
    nout = 2 if with_bf16 else 1
    grid = (m // bm, n // bn, nk)
    steps = grid[0] * grid[1] * nk
    r_in = [] if rider is None else rider["inputs"]
    r_out = [] if rider is None else rider["out_shape"]

    def body(a_ref, b_ref, *rest):
        ins, outs = rest[:len(r_in)], rest[len(r_in):len(r_in) + nout]
        r_outs, scratch = rest[len(r_in) + nout:len(r_in) + nout + len(r_out)], rest[len(r_in) + nout + len(r_out):]
        o_ref = outs[0]
        i, j, k = pl.program_id(0), pl.program_id(1), pl.program_id(2)
        step = (i * grid[1] + j) * nk + k
        if rider is not None:
            @pl.when(step == 0)
            def _():
                _stage2_start(ins, r_outs, scratch)

            @pl.when(step == RELAY_AT * steps // 8)
            def _():
                _stage2_combine(ins, r_outs, scratch)

        @pl.when(k == 0)
        def _():
            o_ref[...] = jnp.zeros_like(o_ref)

        o_ref[...] += _tn(a_ref[...].astype(BF16), b_ref[...].astype(BF16))
        if with_bf16:
            @pl.when(k == nk - 1)
            def _():
                outs[1][...] = o_ref[...].astype(BF16)
        if rider is not None:
            @pl.when(step == steps - 1)
            def _():
                _stage2_finish(ins, r_outs, scratch)

    out_blk = pl.BlockSpec((bm, bn), lambda i, j, k: (i, j))
    hbm = pl.BlockSpec(memory_space=pl.ANY)
    out_shape = [jax.ShapeDtypeStruct((m, n), F32)] + ([jax.ShapeDtypeStruct((m, n), BF16)] if with_bf16 else [])
    res = pl.pallas_call(
        body,
        name=name,
        grid=grid,
        in_specs=[pl.BlockSpec((tk, bm), lambda i, j, k: (k, i)), pl.BlockSpec((tk, bn), lambda i, j, k: (k, j))]
        + [hbm] * len(r_in),
        out_specs=[out_blk] * nout + [hbm] * len(r_out),
        out_shape=out_shape + list(r_out),
        scratch_shapes=[] if rider is None else rider["scratch"],
        compiler_params=_params(("parallel", "parallel", "arbitrary") if rider is None
                                else ("arbitrary", "arbitrary", "arbitrary")),
    )(a, b, *r_in)
    return res[0] if len(res) == 1 else res


def _dw_ffn(adu, hb, tk):
    _, t, _ = adu.shape
    bm = DFF // 2
    nk = t // tk

    def body(a_ref, b_ref, o_ref, ob_ref):
        k = pl.program_id(2)

        @pl.when(k == 0)
        def _():
            o_ref[...] = jnp.zeros_like(o_ref)

        o_ref[...] += _tn(a_ref[...], b_ref[...])

        @pl.when(k == nk - 1)
        def _():
            ob_ref[...] = o_ref[...].astype(BF16)

    out_blk = pl.BlockSpec((None, bm, D), lambda p, i, k: (p, i, 0))
    return pl.pallas_call(
        body,
        name="dw_ffn",
        grid=(3, DFF // bm, nk),
        in_specs=[pl.BlockSpec((None, tk, bm), lambda p, i, k: (p, k, i)),
                  pl.BlockSpec((None, tk, D), lambda p, i, k: (jnp.minimum(p, 1), k, 0))],
        out_specs=[out_blk, out_blk],
        out_shape=[jax.ShapeDtypeStruct((3, DFF, D), F32), jax.ShapeDtypeStruct((3, DFF, D), BF16)],
        compiler_params=_params(("arbitrary", "arbitrary", "arbitrary")),
    )(adu, hb)


def _mix_bwd(proj3, z3, sprev, opre3, dmix3, gng, conv_w, wgu_p, pbs):
    nb, s, _ = proj3.shape
    nc = s // CHUNK
    na = len(pbs)

    def body(*refs):
        (p_ref, pprev_ref, z_ref, sp_ref, o_ref, dm_ref, gng_ref, cw_ref, wgu_ref) = refs[:9]
        pb_refs = refs[9:9 + na]
        (dproj_ref, dgng_ref, dcw_ref, dbg_ref, dwgu_ref) = refs[9 + na:14 + na]
        r2_refs = refs[14 + na:14 + 2 * na]
        ds_ref, dycn_ref = refs[14 + 2 * na:16 + 2 * na]
        stage2 = (pb_refs, r2_refs, refs[16 + 2 * na:])
        step = pl.program_id(0)
        n = nc - 1 - step

        @pl.when(step == 0)
        def _():
            _stage2_start(*stage2)
            ds_ref[...] = jnp.zeros_like(ds_ref)
            dycn_ref[...] = jnp.zeros_like(dycn_ref)
            dgng_ref[...] = jnp.zeros_like(dgng_ref)
            dcw_ref[...] = jnp.zeros_like(dcw_ref)
            dbg_ref[...] = jnp.zeros_like(dbg_ref)
            dwgu_ref[...] = jnp.zeros_like(dwgu_ref)

        @pl.when(step == RELAY_AT * nc // 8)
        def _():
            _stage2_combine(*stage2)

        r_i = lax.broadcasted_iota(jnp.int32, (CHUNK, CHUNK), 0)
        c_i = lax.broadcasted_iota(jnp.int32, (CHUNK, CHUNK), 1)
        tril16 = (r_i >= c_i).astype(BF16)
        triu16 = (r_i <= c_i).astype(BF16)
        causal = r_i >= c_i
        masks = _head_masks()
        cmask = _causal_stack_mask()
        gg = gng_ref[...]
        last_row = lax.broadcasted_iota(jnp.int32, (CHUNK, NQK), 0) == CHUNK - 1
        ones_r = jnp.ones((16, DV), BF16)
        has_prev = (n > 0).astype(F32)
        for b in range(nb):
            q = p_ref[b, :, OQ:OQ + NQK]
            k = p_ref[b, :, OK_:OK_ + NQK]
            z = z_ref[b]
            _, eb, enb, ekl, qi, ki, ks, decb = _chunk_fwd_parts(q, k, z, tril16)
            qi16 = qi.astype(BF16)
            ki16 = ki.astype(BF16)
            qs = _stack_heads(qi, masks).astype(BF16)
            sc = jnp.where(cmask, _nt(qs, ki16), 0.0).astype(BF16)
            st = sp_ref[b, 0]
            st16 = st.astype(BF16)
            dsn = ds_ref[b]
            dsn16 = dsn.astype(BF16)
            v16 = p_ref[b, :, OV:OV + NV].astype(BF16)
            do16 = []
            dgng = jnp.zeros((1, DV), F32)
            for h in range(HEADS):
                cols = slice(DV * h, DV * (h + 1))
                o = o_ref[b, :, cols]
                r = lax.rsqrt(jnp.mean(o * o, axis=-1, keepdims=True) + EPS)
                nh = o * r
                g = p_ref[b, :, OG + DV * h:OG + DV * (h + 1)]
                sg = jax.nn.sigmoid(g)
                dog = dm_ref[b, :, cols]
                dproj_ref[b, :, OG + DV * h:OG + DV * (h + 1)] = (
                    (dog * (nh * gg)) * (sg * (1.0 + g * (1.0 - sg)))).astype(BF16)
                don = dog * (g * sg)
                dgng = dgng + jnp.sum(don * nh, axis=0, keepdims=True)
                dn = don * gg
                do = r * (dn - nh * jnp.mean(dn * nh, axis=-1, keepdims=True))
                do16.append(do.astype(BF16))
            dgng_ref[...] += dgng
            do_rows = jnp.concatenate(do16, axis=0)
            v_rows = jnp.concatenate([v16[:, DV * h:DV * (h + 1)] for h in range(HEADS)], axis=0)
            dp16 = [jnp.where(causal, _nt(do16[h], v16[:, DV * h:DV * (h + 1)]), 0.0).astype(BF16)
                    for h in range(HEADS)]
            ks_dsn = _mm(_stack_heads(ks, masks).astype(BF16), dsn16)
            do_st = _nt(do_rows, st16)
            v_dsn = _nt(v_rows, dsn16)
            dp_ki = _mm(jnp.concatenate(dp16, axis=0), ki16)
            q_do = _tn(qi16, jnp.concatenate(do16, axis=1))
            dki_h = []
            for h in range(HEADS):
                rows = slice(CHUNK * h, CHUNK * (h + 1))
                cols = slice(DV * h, DV * (h + 1))
                dv = _tn(sc[rows], do16[h]) + ks_dsn[rows]
                dproj_ref[b, :, OV + DV * h:OV + DV * (h + 1)] = dv.astype(BF16)
                dki_h.append(_tn(dp16[h], qi16))
                ds_ref[b, rows, :] = decb[rows] * dsn[rows] + q_do[rows, cols]
            blocks = lambda a: [a[CHUNK * h:CHUNK * (h + 1)] for h in range(HEADS)]
            dqi = _merge_heads(blocks(dp_ki + do_st), masks)
            dki = _merge_heads(dki_h, masks)
            dks = _merge_heads(blocks(v_dsn), masks)
            dproj_ref[b, :, OQ:OQ + NQK] = (dqi * (Q_SCALE * eb)).astype(BF16)
            dproj_ref[b, :, OK_:OK_ + NQK] = (dki * enb + dks * ekl).astype(BF16)
            dks_ks = dks * ks
            db = dqi * qi - dki * ki - dks_ks
            sd = _split_bf16(dsn * st * decb, 2)
            dbl = jnp.sum(dks_ks, axis=0, keepdims=True) + (_nt(ones_r, sd[0]) + _nt(ones_r, sd[1]))[0:1, :]
            db = db + jnp.where(last_row, dbl, 0.0)
            db_parts = _split_bf16(db, 3)
            dla = _mm(triu16, db_parts[0]) + _mm(triu16, db_parts[1]) + _mm(triu16, db_parts[2])
            dz = (dla * INV_GATE_NORM) * (1.0 / (1.0 + jnp.exp(z)))
            dbg_ref[...] += jnp.sum(dz, axis=0, keepdims=True)
            dz16 = dz.astype(BF16)
            pa16 = p_ref[b, :, OA:OA + A_PAD].astype(BF16)
            dwgu_ref[...] += _tn(pa16, dz16)
            dproj_ref[b, :, OA:OA + A_PAD] = _nt(dz16, wgu_ref[...]).astype(BF16)
            cb = p_ref[b, :, OCB:OCB + CW]
            cc = p_ref[b, :, OCC:OCC + CW]
            ch = p_ref[b, :, OCH:OCH + CW]
            u = cc * ch
            uprev = (pprev_ref[b, :, 0:CW] * pprev_ref[b, :, CW:2 * CW]) * has_prev
            u1, u2 = _conv_taps(u, uprev)
            w0 = cw_ref[0:1, :]
            w1 = cw_ref[1:2, :]
            w2 = cw_ref[2:3, :]
            yc = w0 * u2 + w1 * u1 + w2 * u
            doc = dm_ref[b, :, NV:NV + CW]
            dproj_ref[b, :, OCB:OCB + CW] = (doc * yc).astype(BF16)
            dyc = doc * cb
            dycn = dycn_ref[b]
            row = lax.broadcasted_iota(jnp.int32, dyc.shape, 0)
            d1 = jnp.where(row >= CHUNK - 1, pltpu.roll(dycn, CHUNK - 1, 0), pltpu.roll(dyc, CHUNK - 1, 0))
            d2 = jnp.where(row >= CHUNK - 2, pltpu.roll(dycn, CHUNK - 2, 0), pltpu.roll(dyc, CHUNK - 2, 0))
            du = w2 * dyc + w1 * d1 + w0 * d2
            dproj_ref[b, :, OCC:OCC + CW] = (du * ch).astype(BF16)
            dproj_ref[b, :, OCH:OCH + CW] = (du * cc).astype(BF16)
            dcw_ref[0:1, :] += jnp.sum(dyc * u2, axis=0, keepdims=True)
            dcw_ref[1:2, :] += jnp.sum(dyc * u1, axis=0, keepdims=True)
            dcw_ref[2:3, :] += jnp.sum(dyc * u, axis=0, keepdims=True)
            dycn_ref[b] = dyc

        @pl.when(step == nc - 1)
        def _():
            _stage2_finish(*stage2)

    rev =lambda w: pl.BlockSpec((nb, CHUNK, w), lambda i: (0, nc - 1 - i, 0))
    const = lambda r, c: pl.BlockSpec((r, c), lambda i: (0, 0))
    hbm = pl.BlockSpec(memory_space=pl.ANY)
    return pl.pallas_call(
        body,
        name="mix_bwd",
        grid=(nc,),
        in_specs=[
            rev(PW),
            pl.BlockSpec((nb, CHUNK, 2 * CW), lambda i: (0, jnp.maximum(nc - 2 - i, 0), OCC // (2 * CW))),
            rev(NQK),
            pl.BlockSpec((nb, 1, NQK, DV), lambda i: (0, nc - 1 - i, 0, 0)),
            rev(NV),
            rev(D),
            const(1, DV),
            const(CONV_K, CW),
            const(A_PAD, NQK),
        ] + [hbm] * na,
        out_specs=[rev(PW), const(1, DV), const(8, CW), const(1, NQK), const(A_PAD, NQK)] + [hbm] * na,
        out_shape=[
            jax.ShapeDtypeStruct((nb, s, PW), BF16),
            jax.ShapeDtypeStruct((1, DV), F32),
            jax.ShapeDtypeStruct((8, CW), F32),
            jax.ShapeDtypeStruct((1, NQK), F32),
            jax.ShapeDtypeStruct((A_PAD, NQK), F32),
        ] + [jax.ShapeDtypeStruct((2,) + p.shape[1:], BF16) for p in pbs],
        scratch_shapes=[pltpu.VMEM((nb, NQK, DV), F32), pltpu.VMEM((nb, CHUNK, CW), F32)] + _stage2_scratch(pbs),
        compiler_params=_params(("arbitrary",)),
    )(proj3, proj3, z3, sprev, opre3, dmix3, gng, conv_w, wgu_p, *pbs)


SMALL_PACK_ROWS = 16


def _wgu_slot(r):
    return 4 + r // 4, NQK * (r % 4)


CONV_SLOTS = ((8, 0), (8, CW), (9, 0))


def _in_proj_bwd(dproj2d, x2d, dx1, g1, w_in_t, tm, pb, small_parts):
    t = x2d.shape[0]
    nt = t // tm

    def body(dp_ref, x_ref, dx1_ref, g_ref, w_ref, pb_ref, dg2, dgf, dbg, dgng, dwgu, dcw, lp,
             dx_ref, sums_ref, r2_ref, dg1_acc, pack, gbuf, pack1, gbuf1, ssend, srecv, ssend1, srecv1, *scratch2):
        stage2 = ([pb_ref], [r2_ref], scratch2)
        x, y, c = _position()
        me = 4 * x + 2 * y + c
        flips = [(k >> 2, (k >> 1) & 1, k & 1) for k in range(1, N_DEV)]
        peers = [(x ^ fx, y ^ fy, c ^ fc) for fx, fy, fc in flips]

        def small_copies(src, dst, send, recv, arrivals):
            return [pltpu.make_async_remote_copy(
                src_ref=src, dst_ref=dst.at[4 * px + 2 * py + pc if arrivals else me],
                send_sem=send.at[k], recv_sem=recv.at[k], device_id=(px, py, pc), device_id_type=MESH)
                for k, (px, py, pc) in enumerate(peers)]

        @pl.when(pl.program_id(0) == 0)
        def _():
            _stage2_start(*stage2)
            dg1_acc[...] = jnp.zeros_like(dg1_acc)
            pack[...] = jnp.zeros_like(pack)
            pack[1:2, :] = dg2[...]
            pack[2:3, :] = dgf[...]
            pack[3:4, 0:NQK] = dbg[...]
            pack[3:4, NQK:NQK + DV] = dgng[...]
            pack[3:4, NQK + DV:NQK + 2 * DV] = lp[...]
            for r in range(RANK):
                row, lane = _wgu_slot(r)
                pack[row:row + 1, lane:lane + NQK] = dwgu[r:r + 1, :]
            for r, (row, lane) in enumerate(CONV_SLOTS):
                pack[row:row + 1, lane:lane + CW] = dcw[r:r + 1, :]
            for cp in small_copies(pack, gbuf, ssend, srecv, False):
                cp.start()
            gbuf[me] = pack[...]

        @pl.when(pl.program_id(0) == RELAY_AT * nt // 8)
        def _():
            _stage2_combine(*stage2)

        xv = x_ref[...]
        r = lax.rsqrt(jnp.mean(xv * xv, axis=-1, keepdims=True) + EPS)
        n1 = xv * r
        dh = _mm(dp_ref[...], w_ref[...])
        dg1_acc[...] += jnp.sum(dh * n1, axis=0, keepdims=True)
        dn = dh * g_ref[...]
        dx_ref[...] = dx1_ref[...] + r * (dn - n1 * jnp.mean(dn * n1, axis=-1, keepdims=True))

        @pl.when(pl.program_id(0) == nt - 1)
        def _():
            pack1[...] = jnp.zeros_like(pack1)
            pack1[0:1, :] = dg1_acc[...]
            for cp in small_copies(pack1, gbuf1, ssend1, srecv1, False):
                cp.start()
            gbuf1[me] = pack1[...]
            _stage2_finish(*stage2)
            for src, dst, send, recv in ((pack, gbuf, ssend, srecv), (pack1, gbuf1, ssend1, srecv1)):
                for cp in small_copies(src, dst, send, recv, True):
                    cp.wait_recv()
                    cp.wait_send()
            acc = gbuf[0]
            acc1 = gbuf1[0]
            for d in range(1, N_DEV):
                acc = acc + gbuf[d]
                acc1 = acc1 + gbuf1[d]
            sums_ref[...] = acc
            sums_ref[0:1, :] = acc1[0:1, :]

    tile = lambda w: pl.BlockSpec((tm, w), lambda i: (i, 0))
    vec = pl.BlockSpec((1, D), lambda i: (0, 0))
    hbm = pl.BlockSpec(memory_space=pl.ANY)
    whole = lambda a: pl.BlockSpec(a.shape, lambda i: (0,) * a.ndim)
    return pl.pallas_call(
        body,
        name="in_proj_bwd",
        grid=(nt,),
        in_specs=[tile(PW), tile(D), tile(D), vec, pl.BlockSpec((PW, D), lambda i: (0, 0)), hbm]
        + [whole(a) for a in small_parts],
        out_specs=[tile(D), pl.BlockSpec((SMALL_PACK_ROWS, D), lambda i: (0, 0)), hbm],
        out_shape=[jax.ShapeDtypeStruct((t, D), F32), jax.ShapeDtypeStruct((SMALL_PACK_ROWS, D), F32),
                   jax.ShapeDtypeStruct((2,) + pb.shape[1:], BF16)],
        scratch_shapes=[pltpu.VMEM((1, D), F32),
                        pltpu.VMEM((SMALL_PACK_ROWS, D), F32), pltpu.VMEM((N_DEV, SMALL_PACK_ROWS, D), F32),
                        pltpu.VMEM((8, D), F32), pltpu.VMEM((N_DEV, 8, D), F32),
                        pltpu.SemaphoreType.DMA((7,)), pltpu.SemaphoreType.DMA((7,)),
                        pltpu.SemaphoreType.DMA((7,)), pltpu.SemaphoreType.DMA((7,))] + _stage2_scratch([pb]),
        compiler_params=_params(("arbitrary",)),
    )(dproj2d, x2d, dx1, g1, w_in_t, pb, *small_parts)


def _get_rows(ref):
    return ref[:, 0, :] if len(ref.shape) == 3 else ref[...]


def _put_rows(ref, val):
    if len(ref.shape) == 3:
        ref[:, 0, :] = val
    else:
        ref[...] = val


def _adamw_math(w, g, m, v):
    m = ADAM_B1 * m + (1.0 - ADAM_B1) * g
    v = ADAM_B2 * v + (1.0 - ADAM_B2) * (g * g)
    m_hat = m / (1.0 - ADAM_B1 ** ADAM_STEP)
    v_hat = v / (1.0 - ADAM_B2 ** ADAM_STEP)
    delta = -ADAM_LR * (m_hat / (jnp.sqrt(v_hat) + ADAM_EPS) + ADAM_WD * w)
    return delta, m, v


def _position():
    return lax.axis_index("x"), lax.axis_index("y"), lax.axis_index("c")


GATHER_PARTS = 2
GATHER_SEMS = 7 * GATHER_PARTS
RELAY_AT = 3
FORWARD_AT = 7


def _gather_copies(stage, lo, rows, gx, send_sems, recv_sems, local_sem):
    x, y, c = _position()
    me = (x, y, c)
    sibling = (x, y, 1 - c)
    chips = [(1 - x, y), (x, 1 - y), (1 - x, 1 - y)]
    part = -(-rows // (16 * GATHER_PARTS)) * 16
    bounds = [(p * part, min(part, rows - p * part)) for p in range(GATHER_PARTS)]

    def blk(px, py, pc, off, n):
        return gx.at[4 * px + 2 * py + pc, pl.ds(off, n), :]

    mine = pltpu.make_async_copy(stage.at[pl.ds(lo, rows), :], gx.at[4 * x + 2 * y + c], local_sem)
    parts = []
    for p, (off, n) in enumerate(bounds):
        def copy(k, block, to, from_stage=False, p=p, off=off, n=n):
            return pltpu.make_async_remote_copy(
                src_ref=stage.at[pl.ds(lo + off, n), :] if from_stage else blk(*block, off, n),
                dst_ref=blk(*block, off, n), send_sem=send_sems.at[7 * p + k], recv_sem=recv_sems.at[7 * p + k],
                device_id=to, device_id_type=MESH)

        first = [copy(0, me, sibling, True)] + [copy(1 + j, me, (*chips[j], c), True) for j in range(2)]
        relay = copy(3, (*chips[p], c), (*chips[1 - p], c))
        passed = [copy(4 + j, (*chip, c), sibling) for j, chip in enumerate(chips)]
        arrivals = ([copy(0, sibling, me)] + [copy(1 + j, (*chip, c), me) for j, chip in enumerate(chips)]
                    + [copy(4 + j, (*chip, 1 - c), me) for j, chip in enumerate(chips)])
        parts.append((first, relay, passed, arrivals))
    return mine, parts


def _gather_start(*args):
    mine, parts = _gather_copies(*args)
    mine.start()
    for first, _, _, _ in parts:
        first[0].start()
    for p, q in ((0, 0), (1, 1), (0, 1), (1, 0)):
        parts[p][0][1 + q].start()


def _gather_relay(*args):
    _, parts = _gather_copies(*args)
    for p, (_, relay, passed, arrivals) in enumerate(parts):
        arrivals[1 + p].wait_recv()
        relay.start()
        passed[p].start()


def _gather_forward(*args):
    _, parts = _gather_copies(*args)
    for p, j in ((0, 1), (1, 0), (0, 2), (1, 2)):
        _, _, passed, arrivals = parts[p]
        arrivals[1 + j].wait_recv()
        passed[j].start()


def _gather_finish(*args):
    mine, parts = _gather_copies(*args)
    for first, relay, passed, arrivals in parts:
        arrivals[0].wait_recv()
        for j in range(3):
            arrivals[4 + j].wait_recv()
        for cp in first + [relay] + passed:
            cp.wait_send()
    mine.wait()


def _gather_sems():
    return [pltpu.SemaphoreType.DMA((GATHER_SEMS,)), pltpu.SemaphoreType.DMA((GATHER_SEMS,)), pltpu.SemaphoreType.DMA]


def _gather_w_in(w_it, w_gt, w_ut, w_d, w_o, wgu_s, conv_s):
    def body(wi_ref, wg_ref, wu_ref, wd_ref, wo_ref, wgu_ref, conv_ref, w_ref, gwgu_ref, gconv_ref, stage,
             buf, send_sems, recv_sems, local_sem, ssend, srecv):
        x, y, c = _position()
        me = 4 * x + 2 * y + c
        stage[SLAB_IN:SLAB_IN + IN_W, :] = wi_ref[:, 0, :].astype(BF16)
        stage[SLAB_IN + IN_W:SLAB_G, :] = jnp.zeros((IN_ROWS - IN_W, D), BF16)
        args = (stage, SLAB_IN, IN_ROWS, buf, send_sems, recv_sems, local_sem)
        _gather_start(*args)
        stage[SLAB_G:SLAB_U, :] = wg_ref[...].astype(BF16)
        stage[SLAB_U:SLAB_D, :] = wu_ref[...].astype(BF16)
        stage[SLAB_D:SLAB_O, :] = wd_ref[...].astype(BF16)
        stage[SLAB_O:SLAB_ROWS, :] = wo_ref[...].astype(BF16)
        flips = [(k >> 2, (k >> 1) & 1, k & 1) for k in range(1, N_DEV)]
        peers = [(x ^ fx, y ^ fy, c ^ fc) for fx, fy, fc in flips]

        def small(k, block_id, to):
            return [pltpu.make_async_remote_copy(
                src_ref=s, dst_ref=g.at[block_id], send_sem=ssend.at[2 * k + n], recv_sem=srecv.at[2 * k + n],
                device_id=to, device_id_type=MESH)
                for n, (s, g) in enumerate(((wgu_ref, gwgu_ref), (conv_ref, gconv_ref)))]

        gwgu_ref[me] = wgu_ref[...]
        gconv_ref[me] = conv_ref[...]
        for k, peer in enumerate(peers):
            for cp in small(k, me, peer):
                cp.start()
        w_ref[IN_COLS:PW, :] = jnp.zeros((PW - IN_COLS, D), BF16)
        _gather_relay(*args)
        _gather_forward(*args)
        _gather_finish(*args)
        for k, (px, py, pc) in enumerate(peers):
            for cp in small(k, 4 * px + 2 * py + pc, (px, py, pc)):
                cp.wait_recv()
                cp.wait_send()
        for j, lo, hi, d in _in_segments():
            w_ref[d:d + hi - lo, :] = buf[j, lo:hi, :]

    vm = pl.BlockSpec(memory_space=pltpu.VMEM)
    return pl.pallas_call(
        body,
        name="gather_w_in",
        in_specs=[vm] * 7,
        out_specs=[vm] * 4,
        out_shape=[jax.ShapeDtypeStruct((PW, D), BF16),
                   jax.ShapeDtypeStruct((N_DEV,) + wgu_s.shape, F32),
                   jax.ShapeDtypeStruct((N_DEV,) + conv_s.shape, F32),
                   jax.ShapeDtypeStruct((SLAB_ROWS, D), BF16)],
        scratch_shapes=[pltpu.VMEM((N_DEV, IN_ROWS, D), BF16)] + _gather_sems()
        + [pltpu.SemaphoreType.DMA((14,)), pltpu.SemaphoreType.DMA((14,))],
        compiler_params=_params(),
    )(w_it, w_gt, w_ut, w_d, w_o, wgu_s, conv_s)


def _w_in_core_reduce(dw_t):
    def body(d_ref, own_ref, sib_ref, pb_ref, g, gb, r1, send_sems, recv_sems):
        x, y, c = _position()
        chip = 2 * x + y
        for j in range(N_DEV):
            g[j, IN_W:IN_ROWS, :] = jnp.zeros((IN_ROWS - IN_W, D), F32)
        for j, lo, hi, d in _in_segments():
            g[j, lo:hi, :] = d_ref[d:d + hi - lo, :]
        for j in range(N_DEV):
            gb[j] = g[j].astype(BF16)
        copies = _stage1_copies(gb, r1, send_sems, recv_sems)
        for cp in copies:
            cp.start()
        own_ref[0] = g[2 * chip + c]
        for cp in copies:
            cp.wait_recv()
        sib_ref[0] = r1[chip]
        for k in range(1, 4):
            t = chip ^ k
            pb_ref[k - 1] = (g[2 * t + c] + r1[t].astype(F32)).astype(BF16)
        for cp in copies:
            cp.wait_send()

    vm = pl.BlockSpec(memory_space=pltpu.VMEM)
    return pl.pallas_call(
        body,
        name="w_in_core_reduce",
        in_specs=[vm],
        out_specs=[vm, vm, vm],
        out_shape=[jax.ShapeDtypeStruct((1, IN_ROWS, D), F32), jax.ShapeDtypeStruct((1, IN_ROWS, D), BF16),
                   jax.ShapeDtypeStruct((3, IN_ROWS, D), BF16)],
        scratch_shapes=[pltpu.VMEM((N_DEV, IN_ROWS, D), F32), pltpu.VMEM((N_DEV, IN_ROWS, D), BF16),
                        pltpu.VMEM((4, IN_ROWS, D), BF16), pltpu.SemaphoreType.DMA((4,)),
                        pltpu.SemaphoreType.DMA((4,))],
        compiler_params=_params(),
    )(dw_t)


def _stage1_copies(g_ref, r_ref, send_sems, recv_sems):
    x, y, c = _position()
    return [pltpu.make_async_remote_copy(
        src_ref=g_ref.at[2 * i + 1 - c], dst_ref=r_ref.at[i], send_sem=send_sems.at[i], recv_sem=recv_sems.at[i],
        device_id=(x, y, 1 - c), device_id_type=MESH) for i in range(4)]


def _ffn_core_reduce(dw3, dwb3, dw_o, dwb_o, pos_arr, dx1b, gwb):
    def body(pos_ref, g0, g1, g2, go, gb3_hbm, gbo_hbm, dx1b_ref, gwb_hbm, p0, p1, p2, po, s0, s1, s2, so, dmix_ref,
             r1f, r1o, wo, send_sems, recv_sems, wsem):
        step = pl.program_id(0)
        k = jnp.minimum(step, 2)
        x, y, c = _position()
        chip = 2 * x + y

        def copies(p):
            src = 2 * (chip ^ ((p + 1) & 3)) + 1 - c
            pairs = [(gb3_hbm.at[a, src], r1f.at[a, p]) for a in range(3)] + [(gbo_hbm.at[src], r1o.at[p])]
            return [pltpu.make_async_remote_copy(
                src_ref=s, dst_ref=d, send_sem=send_sems.at[4 * p + a], recv_sem=recv_sems.at[4 * p + a],
                device_id=(x, y, 1 - c), device_id_type=MESH) for a, (s, d) in enumerate(pairs)]

        @pl.when(step == 0)
        def _():
            for p in range(4):
                for cp in copies(p):
                    cp.start()
            loads = [pltpu.make_async_copy(gwb_hbm.at[j, pl.ds(D_TAIL, OUT_ROWS), :],
                                           wo.at[pl.ds(OUT_ROWS * j, OUT_ROWS), :], wsem.at[j]) for j in range(N_DEV)]
            for cp in loads:
                cp.start()
            for cp in loads:
                cp.wait()

        dmix_ref[...] = _nt(dx1b_ref[...], wo[...])

        for p in range(3):
            @pl.when(step == p)
            def _():
                for cp in copies(p):
                    cp.wait_recv()

        for a, (g, pb) in enumerate(((g0, p0), (g1, p1), (g2, p2))):
            pb[...] = (g[...] + r1f[a, k][None].astype(F32)).astype(BF16)
        po[...] = (go[...] + r1o[k][None].astype(F32)).astype(BF16)

        @pl.when(step == 3)
        def _():
            for cp in copies(3):
                cp.wait_recv()
            for a, s in enumerate((s0, s1, s2)):
                s[0] = r1f[a, 3]
            so[0] = r1o[3]
            for p in range(4):
                for cp in copies(p):
                    cp.wait_send()

    t = dx1b.shape[0]
    other = lambda s, pos: 2 * (pos[1] ^ (jnp.minimum(s, 2) + 1)) + pos[0]
    g_spec = lambda lead: pl.BlockSpec((None, 1, FF_W, D), lambda s, pos: (lead, other(s, pos), 0, 0))
    slot = lambda rows: pl.BlockSpec((1, rows, D), lambda s, pos: (jnp.minimum(s, 2), 0, 0))
    one = lambda rows: pl.BlockSpec((1, rows, D), lambda s, pos: (0, 0, 0))
    quarter = pl.BlockSpec((t // 4, D), lambda s, pos: (s, 0))
    hbm = pl.BlockSpec(memory_space=pl.ANY)
    return pl.pallas_call(
        body,
        name="ffn_core_reduce",
        grid_spec=pltpu.PrefetchScalarGridSpec(
            num_scalar_prefetch=1, grid=(4,),
            in_specs=[g_spec(0), g_spec(1), g_spec(2),
                      pl.BlockSpec((1, OUT_ROWS, D), lambda s, pos: (other(s, pos), 0, 0)), hbm, hbm, quarter, hbm],
            out_specs=[slot(FF_W), slot(FF_W), slot(FF_W), slot(OUT_ROWS),
                       one(FF_W), one(FF_W), one(FF_W), one(OUT_ROWS), quarter],
            scratch_shapes=[pltpu.VMEM((3, 4, FF_W, D), BF16), pltpu.VMEM((4, OUT_ROWS, D), BF16),
                            pltpu.VMEM((D, D), BF16), pltpu.SemaphoreType.DMA((16,)),
                            pltpu.SemaphoreType.DMA((16,)), pltpu.SemaphoreType.DMA((N_DEV,))]),
        out_shape=[jax.ShapeDtypeStruct((3, FF_W, D), BF16)] * 3 + [jax.ShapeDtypeStruct((3, OUT_ROWS, D), BF16)]
        + [jax.ShapeDtypeStruct((1, FF_W, D), BF16)] * 3 + [jax.ShapeDtypeStruct((1, OUT_ROWS, D), BF16),
                                                             jax.ShapeDtypeStruct((t, D), F32)],
        compiler_params=_params(("arbitrary",)),
    )(pos_arr, dw3, dw3, dw3, dw_o, dwb3, dwb_o, dx1b, gwb)


def _stage2_scratch(pbs):
    n = len(pbs)
    return ([pltpu.VMEM(p.shape[1:], BF16) for p in pbs] * 2
            + [pltpu.SemaphoreType.DMA((6 * n,)), pltpu.SemaphoreType.DMA((6 * n,)), pltpu.SemaphoreType.DMA((2 * n,))])


def _stage2_copies(p_refs, r_refs, scratch):
    n = len(p_refs)
    owns, gots = scratch[:n], scratch[n:2 * n]
    send_sems, recv_sems, load_sems = scratch[2 * n:]
    x, y, c = _position()
    xn, yn = (1 - x, y, c), (x, 1 - y, c)
    loads, first, second = [], [], []
    for a, (p, r, own, got) in enumerate(zip(p_refs, r_refs, owns, gots)):
        rows = p.shape[1]
        half = -(-rows // 32) * 16
        h0, h1 = pl.ds(0, half), pl.ds(half, rows - half)

        def remote(k, src, dst, to, a=a):
            return pltpu.make_async_remote_copy(
                src_ref=src, dst_ref=dst, send_sem=send_sems.at[6 * a + k], recv_sem=recv_sems.at[6 * a + k],
                device_id=to, device_id_type=MESH)

        loads += [pltpu.make_async_copy(p.at[0, h0, :], own.at[h0, :], load_sems.at[2 * a]),
                  pltpu.make_async_copy(p.at[1, h1, :], own.at[h1, :], load_sems.at[2 * a + 1])]
        first += [remote(0, p.at[2, h0, :], got.at[h0, :], xn), remote(1, p.at[2, h1, :], got.at[h1, :], yn),
                  remote(2, p.at[1, h0, :], r.at[1, h0, :], xn), remote(3, p.at[0, h1, :], r.at[0, h1, :], yn)]
        second += [remote(4, own.at[h0, :], r.at[0, h0, :], yn), remote(5, own.at[h1, :], r.at[1, h1, :], xn)]
    return loads, first, second


def _stage2_start(p_refs, r_refs, scratch):
    loads, first, _ = _stage2_copies(p_refs, r_refs, scratch)
    for cp in loads:
        cp.start()
    for k in range(4):
        for cp in first[k::4]:
            cp.start()


def _stage2_combine(p_refs, r_refs, scratch):
    n = len(p_refs)
    loads, first, second = _stage2_copies(p_refs, r_refs, scratch)
    for a in range(n):
        for cp in loads[2 * a:2 * a + 2]:
            cp.wait()
        for cp in first[4 * a:4 * a + 2]:
            cp.wait_recv()
        own, got = scratch[a], scratch[n + a]
        own[...] = (own[...].astype(F32) + got[...].astype(F32)).astype(BF16)
        for cp in second[2 * a:2 * a + 2]:
            cp.start()


def _stage2_finish(p_refs, r_refs, scratch):
    _, first, second = _stage2_copies(p_refs, r_refs, scratch)
    for a in range(len(p_refs)):
        for cp in first[4 * a + 2:4 * a + 4] + second[2 * a:2 * a + 2]:
            cp.wait_recv()
    for cp in first + second:
        cp.wait_send()


def _finish_weights(items, pos_arr, name, nblk):
    n = len(items)
    in_specs, out_specs, out_shape, operands, wbs = [], [], [], [], []
    for g8, lead, r1, r2, w, m, v in items:
        rows, wr = g8.shape[-2], w.shape[0]
        assert rows % nblk == 0 and wr % nblk == 0 and (nblk == 1 or (rows == wr and rows % (16 * nblk) == 0))
        rb, wb = rows // nblk, wr // nblk
        if lead is not None:
            g_spec = pl.BlockSpec((None, 1, rb, D), lambda i, pos, lead=lead: (lead, 2 * pos[1] + pos[0], i, 0))
        elif g8.shape[0] == 1:
            g_spec = pl.BlockSpec((1, rb, D), lambda i, pos: (0, i, 0))
        else:
            g_spec = pl.BlockSpec((1, rb, D), lambda i, pos: (2 * pos[1] + pos[0], i, 0))
        r1_spec = pl.BlockSpec((1, rb, D), lambda i, pos: (0, i, 0))
        if w.ndim == 3:
            wblk = pl.BlockSpec((wb, 1, D), lambda i, pos: (i, 0, 0))
        else:
            wblk = pl.BlockSpec((wb, D), lambda i, pos: (i, 0))
        in_specs += [g_spec, r1_spec, pl.BlockSpec((2, rb, D), lambda i, pos: (0, i, 0)), wblk, wblk, wblk]
        out_specs += [wblk] * 4
        out_shape += [jax.ShapeDtypeStruct(w.shape, F32)] * 4
        operands += [g8, r1, r2, w, m, v]
        wbs.append(wb)

    def body(pos_ref, *refs):
        for a in range(n):
            g_ref, r1_ref, r2_ref, w_ref, m_ref, v_ref = refs[6 * a:6 * a + 6]
            g_out, d_out, m_out, v_out = refs[6 * n + 4 * a:6 * n + 4 * a + 4]
            g = g_ref[0] + r1_ref[0].astype(F32)
            for k in range(2):
                g = g + r2_ref[k].astype(F32)
            g = g[0:wbs[a], :]
            d, mn, vn = _adamw_math(_get_rows(w_ref), g, _get_rows(m_ref), _get_rows(v_ref))
            for out, val in ((g_out, g), (d_out, d), (m_out, mn), (v_out, vn)):
                _put_rows(out, val)

    return pl.pallas_call(
        body,
        name=name,
        grid_spec=pltpu.PrefetchScalarGridSpec(
            num_scalar_prefetch=1, grid=(nblk,), in_specs=in_specs, out_specs=out_specs),
        out_shape=out_shape,
        compiler_params=_params(("arbitrary",)),
    )(pos_arr, *operands)


SMALL_NAMES = ("norm1_g", "norm2_g", "norm_f_g", "b_gate", "gla_norm_g", "w_gate_up", "conv_w")
WGU_W = NQK // N_DEV
CONV_W = CW // N_DEV


def _small_adamw(sums, ws, ms, vs):
    n = len(SMALL_NAMES)

    def body(*refs):
        acc_ref = refs[0]
        w_refs, m_refs, v_refs = refs[1:1 + n], refs[1 + n:1 + 2 * n], refs[1 + 2 * n:1 + 3 * n]
        loss_ref = refs[1 + 3 * n]
        outs = refs[2 + 3 * n:]
        x, y, c = _position()
        me = 4 * x + 2 * y + c
        acc = acc_ref[...]
        loss_ref[...] = acc[3:4, NQK + DV:NQK + DV + 1]

        def my_columns(full, width):
            r = lax.broadcasted_iota(jnp.int32, (full.shape[1], width), 0)
            col = lax.broadcasted_iota(jnp.int32, (full.shape[1], width), 1)
            sel = (r == width * me + col).astype(F32)
            return _mm(full, sel, precision=HIGHEST)

        dwgu = jnp.concatenate([acc[row:row + 1, lane:lane + NQK] for row, lane in map(_wgu_slot, range(RANK))], axis=0)
        dcw = jnp.concatenate([acc[row:row + 1, lane:lane + CW] for row, lane in CONV_SLOTS], axis=0)
        grads = [acc[0:1, :], acc[1:2, :], acc[2:3, :], acc[3:4, 0:NQK], acc[3:4, NQK:NQK + DV],
                 my_columns(dwgu, WGU_W), my_columns(dcw, CONV_W)]
        for i, g in enumerate(grads):
            d, mn, vn = _adamw_math(_get_rows(w_refs[i]), g, _get_rows(m_refs[i]), _get_rows(v_refs[i]))
            for out, val in zip(outs[4 * i:4 * i + 4], (g, d, mn, vn)):
                _put_rows(out, val)

    vm = pl.BlockSpec(memory_space=pltpu.VMEM)
    out_shape = [jax.ShapeDtypeStruct((1, 1), F32)]
    for w in ws:
        out_shape += [jax.ShapeDtypeStruct(w.shape, F32)] * 4
    return pl.pallas_call(
        body,
        name="small_adamw",
        in_specs=[vm] * (1 + 3 * n),
        out_specs=[vm] * (1 + 4 * n),
        out_shape=out_shape,
        compiler_params=_params(),
    )(sums, *ws, *ms, *vs)


def kernel(x, norm1_g, w_in, w_gate_up, b_gate, gla_norm_g, conv_w, w_out, norm2_g, w_ffn_gate, w_ffn_up, w_ffn_down, norm_f_g, loss_target, m_norm1_g, m_w_in, m_w_gate_up, m_b_gate, m_gla_norm_g, m_conv_w, m_w_out, m_norm2_g, m_w_ffn_gate, m_w_ffn_up, m_w_ffn_down, m_norm_f_g, v_norm1_g, v_w_in, v_w_gate_up, v_b_gate, v_gla_norm_g, v_conv_w, v_w_out, v_norm2_g, v_w_ffn_gate, v_w_ffn_up, v_w_ffn_down, v_norm_f_g):
    xi, yi, ci = _position()
    pos_arr = jnp.stack([ci, 2 * xi + yi]).astype(jnp.int32)
    nb, s, _ = x.shape
    t = nb * s

    tr = lambda a: a[0].T
    rows_of = lambda a: a.transpose(2, 0, 1)
    conv_rows = lambda a: a.transpose(1, 0, 2)
    w_in_t, gwgu, gconv, stage = _gather_w_in(rows_of(w_in), tr(w_ffn_gate), tr(w_ffn_up), w_ffn_down[0], w_out[0],
                                              w_gate_up[0], conv_rows(conv_w))
    wgu_f = gwgu.transpose(1, 0, 2).reshape(RANK, NQK)
    conv_f = gconv.transpose(1, 2, 0, 3).reshape(CONV_K, CW)
    wgu_p = jnp.concatenate([wgu_f, jnp.zeros((A_PAD - RANK, NQK), F32)], axis=0).astype(BF16)

    x2d = x.reshape(t, D)
    tgt2d = loss_target.reshape(t, D)
    tm = 256
    tm_in = min(512, t)
    tk = min(2048, t)
    proj, z, h, gwb = _in_proj_fwd(x2d, norm1_g, w_in_t, wgu_p, b_gate, tm_in, stage)
    proj3 = proj.reshape(nb, s, PW)
    z3 = z.reshape(nb, s, NQK)
    mix3, opre3, sprev, x1, gwa = _mix_fwd(proj3, z3, gla_norm_g, conv_f, stage, x, gwb)
    mix2d = mix3.reshape(t, D)
    dx1, dx1b, adu, hb, dg2, dgf, loss_part = _ffn_fwd_bwd(
        x1.reshape(t, D), tgt2d, gwa, gwb, norm2_g, norm_f_g.reshape(1, D), tm)
    dw3, dwb3 = _dw_ffn(adu, hb, tk)
    dw3 = dw3.reshape(3, N_DEV, FF_W, D)
    dw_o, dwb_o = _tn_matmul(mix2d, dx1b, D // 2, D, tk, "dw_out", True)
    dw_o = dw_o.reshape(N_DEV, OUT_ROWS, D)
    *pb, sib_d, sib_g, sib_u, sib_o, dmix = _ffn_core_reduce(
        dw3, dwb3.reshape(3, N_DEV, FF_W, D), dw_o, dwb_o.reshape(N_DEV, OUT_ROWS, D), pos_arr, dx1b, gwb)
    g8 = [dw3, dw3, dw3, dw_o]
    leads = [0, 1, 2, None]
    tags = ("w_ffn_down", "w_ffn_gate", "w_ffn_up", "w_out")
    r1 = [sib_d, sib_g, sib_u, sib_o]
    mb = _mix_bwd(proj3, z3, sprev, opre3, dmix.reshape(nb, s, D), gla_norm_g, conv_f, wgu_p, [pb[0], pb[1], pb[3]])
    dproj3, dgng, dcw, dbg, dwgu = mb[:5]
    dproj2d = dproj3.reshape(t, PW)
    dw_in_t, r2_up = _tn_matmul(dproj2d, h, PW // 5, D, t, "dw_in", False, _stage2_rider([pb[2]]))
    r2 = [mb[5], mb[6], r2_up, mb[7]]
    g_in, r1_in, pb_in = _w_in_core_reduce(dw_in_t)
    dx, small_sums, r2_in = _in_proj_bwd(dproj2d, x2d, dx1, norm1_g, w_in_t, tm_in, pb_in,
                                         (dg2, dgf, dbg, dgng, dwgu, dcw, loss_part))

    tags = ("w_in",) + tags
    g8 = [g_in] + g8
    leads = [None] + leads
    r1 = [r1_in] + list(r1)
    r2 = [r2_in] + r2
    shard_w = (rows_of(w_in), w_ffn_down[0], tr(w_ffn_gate), tr(w_ffn_up), w_out[0])
    shard_m = (rows_of(m_w_in), m_w_ffn_down[0], tr(m_w_ffn_gate), tr(m_w_ffn_up), m_w_out[0])
    shard_v = (rows_of(v_w_in), v_w_ffn_down[0], tr(v_w_ffn_gate), tr(v_w_ffn_up), v_w_out[0])
    back = (lambda o: o.transpose(1, 2, 0), lambda o: o[None], lambda o: o.T[None], lambda o: o.T[None],
            lambda o: o[None])
    items = list(zip(g8, leads, r1, r2, shard_w, shard_m, shard_v))
    flat = list(_finish_weights(items[1:], pos_arr, "finish_ffn_out", 2))
    flat = list(_finish_weights(items[:1], pos_arr, "finish_w_in", 1)) + flat
    results = {}
    for i, (tag, to_shard) in enumerate(zip(tags, back)):
        results[tag] = [to_shard(o) for o in flat[4 * i:4 * i + 4]]

    small_w = (norm1_g, norm2_g, norm_f_g.reshape(1, D), b_gate, gla_norm_g, w_gate_up[0], conv_rows(conv_w))
    small_m = (m_norm1_g, m_norm2_g, m_norm_f_g.reshape(1, D), m_b_gate, m_gla_norm_g, m_w_gate_up[0],
               conv_rows(m_conv_w))
    small_v = (v_norm1_g, v_norm2_g, v_norm_f_g.reshape(1, D), v_b_gate, v_gla_norm_g, v_w_gate_up[0],
               conv_rows(v_conv_w))
    so = _small_adamw(small_sums, small_w, small_m, small_v)
    loss = so[0].reshape(())
    to_shape = {"norm_f_g": lambda o: o.reshape(D), "w_gate_up": lambda o: o[None],
                "conv_w": lambda o: o.transpose(1, 0, 2)}
    for i, name in enumerate(SMALL_NAMES):
        results[name] = [to_shape.get(name, lambda o: o)(o) for o in so[1 + 4 * i:5 + 4 * i]]

    names = ("norm1_g", "w_in", "w_gate_up", "b_gate", "gla_norm_g", "conv_w", "w_out", "norm2_g",
             "w_ffn_gate", "w_ffn_up", "w_ffn_down", "norm_f_g")
    outs = [loss, dx.reshape(nb, s, D)]
    for kind in range(4):
        for name in names:
            outs.append(results[name][kind])
    return tuple(outs)
```

```python
import jax
import jax.numpy as jnp
from jax import lax
from jax.experimental import pallas as pl
from jax.experimental.pallas import tpu as pltpu

F32 = jnp.float32
BF16 = jnp.bfloat16
HIGHEST = lax.Precision.HIGHEST
MESH = pl.DeviceIdType.MESH

N_DEV = 8
D = 1024
DFF = 2816
HEADS = 4
DK = 64
DV = 128
NQK = HEADS * DK
NV = HEADS * DV
RANK = 16
CHUNK = 64
CW = 512
CONV_K = 3
IN_COLS = 3088
EPS = 1e-6
INV_GATE_NORM = 1.0 / 16.0
Q_SCALE = DK ** -0.5

PW = 3200
OQ, OK_, OV, OG, OCB, OCC, OCH, OA = 0, 256, 512, 1024, 1536, 2048, 2560, 3072
A_PAD = 128

ADAM_LR = 0.001
ADAM_B1 = 0.9
ADAM_B2 = 0.999
ADAM_EPS = 1e-08
ADAM_WD = 0.01
ADAM_STEP = 10

IN_W = IN_COLS // N_DEV
IN_ROWS = 400
FF_W = DFF // N_DEV
OUT_ROWS = D // N_DEV
SLAB_IN = 0
SLAB_G = SLAB_IN + IN_ROWS
SLAB_U = SLAB_G + FF_W
SLAB_D = SLAB_U + FF_W
SLAB_O = SLAB_D + FF_W
SLAB_ROWS = SLAB_O + OUT_ROWS
D_HEAD = 128
D_TAIL = FF_W - D_HEAD
SLAB_SPLIT = SLAB_D + D_HEAD

VMEM_LIMIT = 56 * 1024 * 1024


def _params(sem=None, vmem=VMEM_LIMIT):
    return pltpu.CompilerParams(dimension_semantics=sem, vmem_limit_bytes=vmem)


def _nt(a, b):
    return lax.dot_general(a, b, (((1,), (1,)), ((), ())), preferred_element_type=F32)


def _tn(a, b, precision=None):
    return lax.dot_general(a, b, (((0,), (0,)), ((), ())), preferred_element_type=F32, precision=precision)


def _mm(a, b, precision=None):
    return jnp.dot(a, b, preferred_element_type=F32, precision=precision)


def _in_segments():
    segs = []
    for j in range(N_DEV):
        lo, hi = IN_W * j, IN_W * (j + 1)
        cuts = sorted({lo, hi} | {c for c in (OCB, OCB + RANK) if lo < c < hi})
        for a, b in zip(cuts[:-1], cuts[1:]):
            if a < OCB:
                d = a
            elif a < OCB + RANK:
                d = OA + (a - OCB)
            else:
                d = a - RANK
            segs.append((j, a - lo, b - lo, d))
    return segs


def _in_proj_fwd(h2d, w_in_t, wgu_p, b_gate, tm, stage):
    t = h2d.shape[0]
    nt = t // tm
    g_rows = SLAB_ROWS - SLAB_SPLIT

    def body(h_ref, w_ref, wgu_ref, bg_ref, stage_hbm, proj_ref, z_ref, gwb_ref, send_sems, recv_sems, local_sem):
        gargs = (stage_hbm, SLAB_SPLIT, g_rows, gwb_ref, send_sems, recv_sems, local_sem)

        @pl.when(pl.program_id(0) == 0)
        def _():
            _gather_start(*gargs)

        @pl.when(pl.program_id(0) == RELAY_AT * nt // 8)
        def _():
            _gather_relay(*gargs)

        @pl.when(pl.program_id(0) == FORWARD_AT * nt // 8)
        def _():
            _gather_forward(*gargs)

        proj = _nt(h_ref[...], w_ref[...])
        proj_ref[...] = proj
        pa = proj[:, OA:OA + A_PAD].astype(BF16)
        z_ref[...] = _mm(pa, wgu_ref[...]) + bg_ref[...]

        @pl.when(pl.program_id(0) == nt - 1)
        def _():
            _gather_finish(*gargs)

    return pl.pallas_call(
        body,
        name="in_proj_fwd",
        grid=(t // tm,),
        in_specs=[
            pl.BlockSpec((tm, D), lambda i: (i, 0)),
            pl.BlockSpec((PW, D), lambda i: (0, 0)),
            pl.BlockSpec((A_PAD, NQK), lambda i: (0, 0)),
            pl.BlockSpec((1, NQK), lambda i: (0, 0)),
            pl.BlockSpec(memory_space=pl.ANY),
        ],
        out_specs=[
            pl.BlockSpec((tm, PW), lambda i: (i, 0)),
            pl.BlockSpec((tm, NQK), lambda i: (i, 0)),
            pl.BlockSpec(memory_space=pl.ANY),
        ],
        out_shape=[
            jax.ShapeDtypeStruct((t, PW), F32),
            jax.ShapeDtypeStruct((t, NQK), F32),
            jax.ShapeDtypeStruct((N_DEV, g_rows, D), BF16),
        ],
        scratch_shapes=_gather_sems(),
        compiler_params=_params(("arbitrary",)),
    )(h2d, w_in_t, wgu_p, b_gate, stage)


def _head_masks():
    lane = lax.broadcasted_iota(jnp.int32, (1, NQK), 1)
    return [(lane >= DK * h) & (lane < DK * (h + 1)) for h in range(HEADS)]


def _split_bf16(x, n):
    parts = []
    for _ in range(n):
        p = x.astype(BF16)
        parts.append(p)
        x = x - p.astype(F32)
    return parts


def _chunk_fwd_parts(q, k, z, tril16):
    la = (jnp.minimum(z, 0.0) - jnp.log1p(jnp.exp(-jnp.abs(z)))) * INV_GATE_NORM
    la_parts = _split_bf16(la, 3)
    bc = _mm(tril16, la_parts[0]) + _mm(tril16, la_parts[1]) + _mm(tril16, la_parts[2])
    bl = bc[CHUNK - 1:CHUNK, :]
    eb = jnp.exp(bc)
    enb = jnp.exp(-bc)
    ekl = jnp.exp(bl - bc)
    qi = (q * Q_SCALE) * eb
    ki = k * enb
    ks = k * ekl
    ones16 = jnp.ones((CHUNK, DV), BF16)
    decb = jnp.exp(_tn(la_parts[0], ones16) + _tn(la_parts[1], ones16) + _tn(la_parts[2], ones16))
    return la, eb, enb, ekl, qi, ki, ks, decb


def _stack_heads(a, masks):
    return jnp.concatenate([jnp.where(m, a, 0.0) for m in masks], axis=0)


def _merge_heads(blocks, masks):
    out = blocks[HEADS - 1]
    for h in range(HEADS - 2, -1, -1):
        out = jnp.where(masks[h], blocks[h], out)
    return out


def _causal_stack_mask():
    row = lax.broadcasted_iota(jnp.int32, (HEADS * CHUNK, CHUNK), 0)
    col = lax.broadcasted_iota(jnp.int32, (HEADS * CHUNK, CHUNK), 1)
    return (row & (CHUNK - 1)) >= col


def _conv_taps(u, uprev):
    row = lax.broadcasted_iota(jnp.int32, u.shape, 0)
    u1 = jnp.where(row < 1, pltpu.roll(uprev, 1, 0), pltpu.roll(u, 1, 0))
    u2 = jnp.where(row < 2, pltpu.roll(uprev, 2, 0), pltpu.roll(u, 2, 0))
    return u1, u2


def _mix_fwd(proj3, z3, gng, conv_w, stage, x3, gwb):
    nb, s, _ = proj3.shape
    nc = s // CHUNK
    g_rows = SLAB_SPLIT - SLAB_G

    def body(p_ref, z_ref, gng_ref, cw_ref, stage_hbm, x_ref, gwb_hbm, mix_ref, o_ref, sprev_ref, x1_ref, gwa_ref,
             s_ref, uprev_ref, wo, wsem, send_sems, recv_sems, local_sem):
        n = pl.program_id(0)
        gargs = (stage_hbm, SLAB_G, g_rows, gwa_ref, send_sems, recv_sems, local_sem)

        @pl.when(n == 0)
        def _():
            _gather_start(*gargs)
            loads = [pltpu.make_async_copy(gwb_hbm.at[j, pl.ds(D_TAIL, OUT_ROWS), :],
                                           wo.at[pl.ds(OUT_ROWS * j, OUT_ROWS), :], wsem.at[j]) for j in range(N_DEV)]
            for cp in loads:
                cp.start()
            s_ref[...] = jnp.zeros_like(s_ref)
            uprev_ref[...] = jnp.zeros_like(uprev_ref)
            for cp in loads:
                cp.wait()

        @pl.when(n == RELAY_AT * nc // 8)
        def _():
            _gather_relay(*gargs)

        @pl.when(n == FORWARD_AT * nc // 8)
        def _():
            _gather_forward(*gargs)

        r_i = lax.broadcasted_iota(jnp.int32, (CHUNK, CHUNK), 0)
        c_i = lax.broadcasted_iota(jnp.int32, (CHUNK, CHUNK), 1)
        tril16 = (r_i >= c_i).astype(BF16)
        masks = _head_masks()
        cmask = _causal_stack_mask()
        gg = gng_ref[...]
        for b in range(nb):
            q = p_ref[b, :, OQ:OQ + NQK]
            k = p_ref[b, :, OK_:OK_ + NQK]
            _, _, _, _, qi, ki, ks, decb = _chunk_fwd_parts(q, k, z_ref[b], tril16)
            qs = _stack_heads(qi, masks).astype(BF16)
            sc = jnp.where(cmask, _nt(qs, ki.astype(BF16)), 0.0).astype(BF16)
            st = s_ref[b]
            sprev_ref[b, 0] = st
            o_inter = _mm(qs, st.astype(BF16))
            v16 = p_ref[b, :, OV:OV + NV].astype(BF16)
            kv = _tn(ks.astype(BF16), v16)
            for h in range(HEADS):
                rows = slice(CHUNK * h, CHUNK * (h + 1))
                cols = slice(DV * h, DV * (h + 1))
                o = _mm(sc[rows], v16[:, cols]) + o_inter[rows]
                o_ref[b, :, cols] = o
                r = lax.rsqrt(jnp.mean(o * o, axis=-1, keepdims=True) + EPS)
                on = (o * r) * gg
                g = p_ref[b, :, OG + DV * h:OG + DV * (h + 1)]
                mix_ref[b, :, cols] = (on * (g * jax.nn.sigmoid(g))).astype(BF16)
                s_ref[b, rows, :] = decb[rows] * st[rows] + kv[rows, cols]
            u = p_ref[b, :, OCC:OCC + CW] * p_ref[b, :, OCH:OCH + CW]
            u1, u2 = _conv_taps(u, uprev_ref[b])
            yc = cw_ref[0:1, :] * u2 + cw_ref[1:2, :] * u1 + cw_ref[2:3, :] * u
            mix_ref[b, :, NV:NV + CW] = (p_ref[b, :, OCB:OCB + CW] * yc).astype(BF16)
            uprev_ref[b] = u
        mixed = _mm(jnp.concatenate([mix_ref[b] for b in range(nb)], axis=0), wo[...])
        for b in range(nb):
            x1_ref[b] = x_ref[b] + mixed[CHUNK * b:CHUNK * (b + 1)]

        @pl.when(n == nc - 1)
        def _():
            _gather_finish(*gargs)

    return pl.pallas_call(
        body,
        name="mix_fwd",
        grid=(nc,),
        in_specs=[
            pl.BlockSpec((nb, CHUNK, PW), lambda n: (0, n, 0)),
            pl.BlockSpec((nb, CHUNK, NQK), lambda n: (0, n, 0)),
            pl.BlockSpec((1, DV), lambda n: (0, 0)),
            pl.BlockSpec((CONV_K, CW), lambda n: (0, 0)),
            pl.BlockSpec(memory_space=pl.ANY),
            pl.BlockSpec((nb, CHUNK, D), lambda n: (0, n, 0)),
            pl.BlockSpec(memory_space=pl.ANY),
        ],
        out_specs=[
            pl.BlockSpec((nb, CHUNK, D), lambda n: (0, n, 0)),
            pl.BlockSpec((nb, CHUNK, NV), lambda n: (0, n, 0)),
            pl.BlockSpec((nb, 1, NQK, DV), lambda n: (0, n, 0, 0)),
            pl.BlockSpec((nb, CHUNK, D), lambda n: (0, n, 0)),
            pl.BlockSpec(memory_space=pl.ANY),
        ],
        out_shape=[
            jax.ShapeDtypeStruct((nb, s, D), BF16),
            jax.ShapeDtypeStruct((nb, s, NV), F32),
            jax.ShapeDtypeStruct((nb, nc, NQK, DV), F32),
            jax.ShapeDtypeStruct((nb, s, D), F32),
            jax.ShapeDtypeStruct((N_DEV, g_rows, D), BF16),
        ],
        scratch_shapes=[pltpu.VMEM((nb, NQK, DV), F32), pltpu.VMEM((nb, CHUNK, CW), F32),
                        pltpu.VMEM((D, D), BF16), pltpu.SemaphoreType.DMA((N_DEV,))] + _gather_sems(),
        compiler_params=_params(("arbitrary",)),
    )(proj3, z3, gng, conv_w, stage, x3, gwb)


def _ffn_fwd_bwd(x1_2d, tgt2d, gwa, gwb, g2, gf, tm):
    t = x1_2d.shape[0]

    def body(x1_ref, tgt_ref, g2_ref, gf_ref, gwa_hbm, gwb_hbm,
             dx1_ref, dx1b_ref, adu_ref, hb_ref, dg2_ref, dgf_ref, loss_ref,
             wg, wu, wd, wsem):
        i = pl.program_id(0)

        def weight_copies(n, dst, src, off, rows, at=0):
            return [pltpu.make_async_copy(src.at[j, pl.ds(off, rows), :], dst.at[pl.ds(FF_W * j + at, rows), :],
                                          wsem.at[N_DEV * n + j]) for j in range(N_DEV)]

        loads = (weight_copies(0, wg, gwa_hbm, 0, FF_W), weight_copies(1, wu, gwa_hbm, FF_W, FF_W),
                 weight_copies(2, wd, gwa_hbm, 2 * FF_W, D_HEAD), weight_copies(3, wd, gwb_hbm, 0, D_TAIL, D_HEAD))

        @pl.when(i == 0)
        def _():
            for group in loads:
                for cp in group:
                    cp.start()
            dg2_ref[...] = jnp.zeros_like(dg2_ref)
            dgf_ref[...] = jnp.zeros_like(dgf_ref)
            loss_ref[...] = jnp.zeros_like(loss_ref)
            for group in loads:
                for cp in group:
                    cp.wait()

        g2v = g2_ref[...]
        gfv = gf_ref[...]
        x1 = x1_ref[...]
        r2 = lax.rsqrt(jnp.mean(x1 * x1, axis=-1, keepdims=True) + EPS)
        n2 = x1 * r2
        h2 = (n2 * g2v).astype(BF16)
        hb_ref[1] = h2
        gate = _nt(h2, wg[...])
        up = _nt(h2, wu[...])
        sg = jax.nn.sigmoid(gate)
        sil = gate * sg
        act = (sil * up).astype(BF16)
        adu_ref[0] = act
        x2 = x1 + _mm(act, wd[...])
        rf = lax.rsqrt(jnp.mean(x2 * x2, axis=-1, keepdims=True) + EPS)
        nf = x2 * rf
        err = nf * gfv - tgt_ref[...]
        loss_ref[...] += 0.5 * jnp.sum(jnp.mean(err * err, axis=-1, keepdims=True))
        dy = err * (1.0 / D)
        dgf_ref[...] += jnp.sum(dy * nf, axis=0, keepdims=True)
        dnf = dy * gfv
        dx2 = rf * (dnf - nf * jnp.mean(dnf * nf, axis=-1, keepdims=True))
        dx2b = dx2.astype(BF16)
        hb_ref[0] = dx2b
        dact = _nt(dx2b, wd[...])
        dup = (dact * sil).astype(BF16)
        dgate = ((dact * up) * (sg * (1.0 + gate * (1.0 - sg)))).astype(BF16)
        adu_ref[2] = dup
        adu_ref[1] = dgate
        dh2 = _mm(dgate, wg[...]) + _mm(dup, wu[...])
        dg2_ref[...] += jnp.sum(dh2 * n2, axis=0, keepdims=True)
        dn2 = dh2 * g2v
        dx1 = dx2 + r2 * (dn2 - n2 * jnp.mean(dn2 * n2, axis=-1, keepdims=True))
        dx1_ref[...] = dx1
        dx1b_ref[...] = dx1.astype(BF16)

    tile = lambda w: pl.BlockSpec((tm, w), lambda i: (i, 0))
    vec = pl.BlockSpec((1, D), lambda i: (0, 0))
    hbm = pl.BlockSpec(memory_space=pl.ANY)
    return pl.pallas_call(
        body,
        name="ffn_fwd_bwd",
        grid=(t // tm,),
        in_specs=[tile(D), tile(D), vec, vec, hbm, hbm],
        out_specs=[tile(D), tile(D), pl.BlockSpec((3, tm, DFF), lambda i: (0, i, 0)),
                   pl.BlockSpec((2, tm, D), lambda i: (0, i, 0)), vec, vec,
                   pl.BlockSpec((1, 128), lambda i: (0, 0))],
        out_shape=[
            jax.ShapeDtypeStruct((t, D), F32),
            jax.ShapeDtypeStruct((t, D), BF16),
            jax.ShapeDtypeStruct((3, t, DFF), BF16),
            jax.ShapeDtypeStruct((2, t, D), BF16),
            jax.ShapeDtypeStruct((1, D), F32),
            jax.ShapeDtypeStruct((1, D), F32),
            jax.ShapeDtypeStruct((1, 128), F32),
        ],
        scratch_shapes=[pltpu.VMEM((DFF, D), BF16), pltpu.VMEM((DFF, D), BF16), pltpu.VMEM((DFF, D), BF16),
                        pltpu.SemaphoreType.DMA((4 * N_DEV,))],
        compiler_params=_params(("arbitrary",)),
    )(x1_2d, tgt2d, g2, gf, gwa, gwb)


def _stage2_rider(pbs):
    return dict(inputs=list(pbs), out_shape=[jax.ShapeDtypeStruct((2,) + p.shape[1:], BF16) for p in pbs],
                scratch=_stage2_scratch(pbs))


def _tn_matmul(a, b, bm, bn, tk, name, with_bf16, rider=None):
    t, m = a.shape
    n = b.shape[1]
    nk = t // tk
    nout = 2 if with_bf16 else 1
    grid = (m // bm, n // bn, nk)
    steps = grid[0] * grid[1] * nk
    r_in = [] if rider is None else rider["inputs"]
    r_out = [] if rider is None else rider["out_shape"]

    def body(a_ref, b_ref, *rest):
        ins, outs = rest[:len(r_in)], rest[len(r_in):len(r_in) + nout]
        r_outs, scratch = rest[len(r_in) + nout:len(r_in) + nout + len(r_out)], rest[len(r_in) + nout + len(r_out):]
        o_ref = outs[0]
        i, j, k = pl.program_id(0), pl.program_id(1), pl.program_id(2)
        step = (i * grid[1] + j) * nk + k
        if rider is not None:
            @pl.when(step == 0)
            def _():
                _stage2_start(ins, r_outs, scratch)

            @pl.when(step == RELAY_AT * steps // 8)
            def _():
                _stage2_combine(ins, r_outs, scratch)

        @pl.when(k == 0)
        def _():
            o_ref[...] = jnp.zeros_like(o_ref)

        o_ref[...] += _tn(a_ref[...].astype(BF16), b_ref[...].astype(BF16))
        if with_bf16:
            @pl.when(k == nk - 1)
            def _():
                outs[1][...] = o_ref[...].astype(BF16)
        if rider is not None:
            @pl.when(step == steps - 1)
            def _():
                _stage2_finish(ins, r_outs, scratch)

    out_blk = pl.BlockSpec((bm, bn), lambda i, j, k: (i, j))
    hbm = pl.BlockSpec(memory_space=pl.ANY)
    out_shape = [jax.ShapeDtypeStruct((m, n), F32)] + ([jax.ShapeDtypeStruct((m, n), BF16)] if with_bf16 else [])
    res = pl.pallas_call(
        body,
        name=name,
        grid=grid,
        in_specs=[pl.BlockSpec((tk, bm), lambda i, j, k: (k, i)), pl.BlockSpec((tk, bn), lambda i, j, k: (k, j))]
        + [hbm] * len(r_in),
        out_specs=[out_blk] * nout + [hbm] * len(r_out),
        out_shape=out_shape + list(r_out),
        scratch_shapes=[] if rider is None else rider["scratch"],
        compiler_params=_params(("parallel", "parallel", "arbitrary") if rider is None
                                else ("arbitrary", "arbitrary", "arbitrary")),
    )(a, b, *r_in)
    return res[0] if len(res) == 1 else res


def _dw_ffn(adu, hb, tk):
    _, t, _ = adu.shape
    bm = DFF // 2
    nk = t // tk

    def body(a_ref, b_ref, o_ref, ob_ref):
        k = pl.program_id(2)

        @pl.when(k == 0)
        def _():
            o_ref[...] = jnp.zeros_like(o_ref)

        o_ref[...] += _tn(a_ref[...], b_ref[...])

        @pl.when(k == nk - 1)
        def _():
            ob_ref[...] = o_ref[...].astype(BF16)

    out_blk = pl.BlockSpec((None, bm, D), lambda p, i, k: (p, i, 0))
    return pl.pallas_call(
        body,
        name="dw_ffn",
        grid=(3, DFF // bm, nk),
        in_specs=[pl.BlockSpec((None, tk, bm), lambda p, i, k: (p, k, i)),
                  pl.BlockSpec((None, tk, D), lambda p, i, k: (jnp.minimum(p, 1), k, 0))],
        out_specs=[out_blk, out_blk],
        out_shape=[jax.ShapeDtypeStruct((3, DFF, D), F32), jax.ShapeDtypeStruct((3, DFF, D), BF16)],
        compiler_params=_params(("arbitrary", "arbitrary", "arbitrary")),
    )(adu, hb)


def _mix_bwd(proj3, z3, sprev, opre3, dmix3, gng, conv_w, wgu_p, pbs):
    nb, s, _ = proj3.shape
    nc = s // CHUNK
    na = len(pbs)

    def body(*refs):
        (p_ref, pprev_ref, z_ref, sp_ref, o_ref, dm_ref, gng_ref, cw_ref, wgu_ref) = refs[:9]
        pb_refs = refs[9:9 + na]
        (dproj_ref, dgng_ref, dcw_ref, dbg_ref, dwgu_ref) = refs[9 + na:14 + na]
        r2_refs = refs[14 + na:14 + 2 * na]
        ds_ref, dycn_ref = refs[14 + 2 * na:16 + 2 * na]
        stage2 = (pb_refs, r2_refs, refs[16 + 2 * na:])
        step = pl.program_id(0)
        n = nc - 1 - step

        @pl.when(step == 0)
        def _():
            _stage2_start(*stage2)
            ds_ref[...] = jnp.zeros_like(ds_ref)
            dycn_ref[...] = jnp.zeros_like(dycn_ref)
            dgng_ref[...] = jnp.zeros_like(dgng_ref)
            dcw_ref[...] = jnp.zeros_like(dcw_ref)
            dbg_ref[...] = jnp.zeros_like(dbg_ref)
            dwgu_ref[...] = jnp.zeros_like(dwgu_ref)

        @pl.when(step == RELAY_AT * nc // 8)
        def _():
            _stage2_combine(*stage2)

        r_i = lax.broadcasted_iota(jnp.int32, (CHUNK, CHUNK), 0)
        c_i = lax.broadcasted_iota(jnp.int32, (CHUNK, CHUNK), 1)
        tril16 = (r_i >= c_i).astype(BF16)
        triu16 = (r_i <= c_i).astype(BF16)
        causal = r_i >= c_i
        masks = _head_masks()
        cmask = _causal_stack_mask()
        gg = gng_ref[...]
        last_row = lax.broadcasted_iota(jnp.int32, (CHUNK, NQK), 0) == CHUNK - 1
        ones_r = jnp.ones((16, DV), BF16)
        has_prev = (n > 0).astype(F32)
        for b in range(nb):
            q = p_ref[b, :, OQ:OQ + NQK]
            k = p_ref[b, :, OK_:OK_ + NQK]
            z = z_ref[b]
            _, eb, enb, ekl, qi, ki, ks, decb = _chunk_fwd_parts(q, k, z, tril16)
            qi16 = qi.astype(BF16)
            ki16 = ki.astype(BF16)
            qs = _stack_heads(qi, masks).astype(BF16)
            sc = jnp.where(cmask, _nt(qs, ki16), 0.0).astype(BF16)
            st = sp_ref[b, 0]
            st16 = st.astype(BF16)
            dsn = ds_ref[b]
            dsn16 = dsn.astype(BF16)
            v16 = p_ref[b, :, OV:OV + NV].astype(BF16)
            do16 = []
            dgng = jnp.zeros((1, DV), F32)
            for h in range(HEADS):
                cols = slice(DV * h, DV * (h + 1))
                o = o_ref[b, :, cols]
                r = lax.rsqrt(jnp.mean(o * o, axis=-1, keepdims=True) + EPS)
                nh = o * r
                g = p_ref[b, :, OG + DV * h:OG + DV * (h + 1)]
                sg = jax.nn.sigmoid(g)
                dog = dm_ref[b, :, cols]
                dproj_ref[b, :, OG + DV * h:OG + DV * (h + 1)] = (
                    (dog * (nh * gg)) * (sg * (1.0 + g * (1.0 - sg)))).astype(BF16)
                don = dog * (g * sg)
                dgng = dgng + jnp.sum(don * nh, axis=0, keepdims=True)
                dn = don * gg
                do = r * (dn - nh * jnp.mean(dn * nh, axis=-1, keepdims=True))
                do16.append(do.astype(BF16))
            dgng_ref[...] += dgng
            do_rows = jnp.concatenate(do16, axis=0)
            v_rows = jnp.concatenate([v16[:, DV * h:DV * (h + 1)] for h in range(HEADS)], axis=0)
            dp16 = [jnp.where(causal, _nt(do16[h], v16[:, DV * h:DV * (h + 1)]), 0.0).astype(BF16)
                    for h in range(HEADS)]
            ks_dsn = _mm(_stack_heads(ks, masks).astype(BF16), dsn16)
            do_st = _nt(do_rows, st16)
            v_dsn = _nt(v_rows, dsn16)
            dp_ki = _mm(jnp.concatenate(dp16, axis=0), ki16)
            q_do = _tn(qi16, jnp.concatenate(do16, axis=1))
            dki_h = []
            for h in range(HEADS):
                rows = slice(CHUNK * h, CHUNK * (h + 1))
                cols = slice(DV * h, DV * (h + 1))
                dv = _tn(sc[rows], do16[h]) + ks_dsn[rows]
                dproj_ref[b, :, OV + DV * h:OV + DV * (h + 1)] = dv.astype(BF16)
                dki_h.append(_tn(dp16[h], qi16))
                ds_ref[b, rows, :] = decb[rows] * dsn[rows] + q_do[rows, cols]
            blocks = lambda a: [a[CHUNK * h:CHUNK * (h + 1)] for h in range(HEADS)]
            dqi = _merge_heads(blocks(dp_ki + do_st), masks)
            dki = _merge_heads(dki_h, masks)
            dks = _merge_heads(blocks(v_dsn), masks)
            dproj_ref[b, :, OQ:OQ + NQK] = (dqi * (Q_SCALE * eb)).astype(BF16)
            dproj_ref[b, :, OK_:OK_ + NQK] = (dki * enb + dks * ekl).astype(BF16)
            dks_ks = dks * ks
            db = dqi * qi - dki * ki - dks_ks
            sd = _split_bf16(dsn * st * decb, 2)
            dbl = jnp.sum(dks_ks, axis=0, keepdims=True) + (_nt(ones_r, sd[0]) + _nt(ones_r, sd[1]))[0:1, :]
            db = db + jnp.where(last_row, dbl, 0.0)
            db_parts = _split_bf16(db, 3)
            dla = _mm(triu16, db_parts[0]) + _mm(triu16, db_parts[1]) + _mm(triu16, db_parts[2])
            dz = (dla * INV_GATE_NORM) * (1.0 / (1.0 + jnp.exp(z)))
            dbg_ref[...] += jnp.sum(dz, axis=0, keepdims=True)
            dz16 = dz.astype(BF16)
            pa16 = p_ref[b, :, OA:OA + A_PAD].astype(BF16)
            dwgu_ref[...] += _tn(pa16, dz16)
            dproj_ref[b, :, OA:OA + A_PAD] = _nt(dz16, wgu_ref[...]).astype(BF16)
            cb = p_ref[b, :, OCB:OCB + CW]
            cc = p_ref[b, :, OCC:OCC + CW]
            ch = p_ref[b, :, OCH:OCH + CW]
            u = cc * ch
            uprev = (pprev_ref[b, :, 0:CW] * pprev_ref[b, :, CW:2 * CW]) * has_prev
            u1, u2 = _conv_taps(u, uprev)
            w0 = cw_ref[0:1, :]
            w1 = cw_ref[1:2, :]
            w2 = cw_ref[2:3, :]
            yc = w0 * u2 + w1 * u1 + w2 * u
            doc = dm_ref[b, :, NV:NV + CW]
            dproj_ref[b, :, OCB:OCB + CW] = (doc * yc).astype(BF16)
            dyc = doc * cb
            dycn = dycn_ref[b]
            row = lax.broadcasted_iota(jnp.int32, dyc.shape, 0)
            d1 = jnp.where(row >= CHUNK - 1, pltpu.roll(dycn, CHUNK - 1, 0), pltpu.roll(dyc, CHUNK - 1, 0))
            d2 = jnp.where(row >= CHUNK - 2, pltpu.roll(dycn, CHUNK - 2, 0), pltpu.roll(dyc, CHUNK - 2, 0))
            du = w2 * dyc + w1 * d1 + w0 * d2
            dproj_ref[b, :, OCC:OCC + CW] = (du * ch).astype(BF16)
            dproj_ref[b, :, OCH:OCH + CW] = (du * cc).astype(BF16)
            dcw_ref[0:1, :] += jnp.sum(dyc * u2, axis=0, keepdims=True)
            dcw_ref[1:2, :] += jnp.sum(dyc * u1, axis=0, keepdims=True)
            dcw_ref[2:3, :] += jnp.sum(dyc * u, axis=0, keepdims=True)
            dycn_ref[b] = dyc

        @pl.when(step == nc - 1)
        def _():
            _stage2_finish(*stage2)

    rev =lambda w: pl.BlockSpec((nb, CHUNK, w), lambda i: (0, nc - 1 - i, 0))
    const = lambda r, c: pl.BlockSpec((r, c), lambda i: (0, 0))
    hbm = pl.BlockSpec(memory_space=pl.ANY)
    return pl.pallas_call(
        body,
        name="mix_bwd",
        grid=(nc,),
        in_specs=[
            rev(PW),
            pl.BlockSpec((nb, CHUNK, 2 * CW), lambda i: (0, jnp.maximum(nc - 2 - i, 0), OCC // (2 * CW))),
            rev(NQK),
            pl.BlockSpec((nb, 1, NQK, DV), lambda i: (0, nc - 1 - i, 0, 0)),
            rev(NV),
            rev(D),
            const(1, DV),
            const(CONV_K, CW),
            const(A_PAD, NQK),
        ] + [hbm] * na,
        out_specs=[rev(PW), const(1, DV), const(8, CW), const(1, NQK), const(A_PAD, NQK)] + [hbm] * na,
        out_shape=[
            jax.ShapeDtypeStruct((nb, s, PW), BF16),
            jax.ShapeDtypeStruct((1, DV), F32),
            jax.ShapeDtypeStruct((8, CW), F32),
            jax.ShapeDtypeStruct((1, NQK), F32),
            jax.ShapeDtypeStruct((A_PAD, NQK), F32),
        ] + [jax.ShapeDtypeStruct((2,) + p.shape[1:], BF16) for p in pbs],
        scratch_shapes=[pltpu.VMEM((nb, NQK, DV), F32), pltpu.VMEM((nb, CHUNK, CW), F32)] + _stage2_scratch(pbs),
        compiler_params=_params(("arbitrary",)),
    )(proj3, proj3, z3, sprev, opre3, dmix3, gng, conv_w, wgu_p, *pbs)


SMALL_PACK_ROWS = 16


def _wgu_slot(r):
    return 4 + r // 4, NQK * (r % 4)


CONV_SLOTS = ((8, 0), (8, CW), (9, 0))


def _in_proj_bwd(dproj2d, x2d, dx1, g1, w_in_t, tm, pb, small_parts):
    t = x2d.shape[0]
    nt = t // tm

    def body(dp_ref, x_ref, dx1_ref, g_ref, w_ref, pb_ref, dg2, dgf, dbg, dgng, dwgu, dcw, lp,
             dx_ref, sums_ref, r2_ref, dg1_acc, pack, gbuf, pack1, gbuf1, ssend, srecv, ssend1, srecv1, *scratch2):
        stage2 = ([pb_ref], [r2_ref], scratch2)
        x, y, c = _position()
        me = 4 * x + 2 * y + c
        flips = [(k >> 2, (k >> 1) & 1, k & 1) for k in range(1, N_DEV)]
        peers = [(x ^ fx, y ^ fy, c ^ fc) for fx, fy, fc in flips]

        def small_copies(src, dst, send, recv, arrivals):
            return [pltpu.make_async_remote_copy(
                src_ref=src, dst_ref=dst.at[4 * px + 2 * py + pc if arrivals else me],
                send_sem=send.at[k], recv_sem=recv.at[k], device_id=(px, py, pc), device_id_type=MESH)
                for k, (px, py, pc) in enumerate(peers)]

        @pl.when(pl.program_id(0) == 0)
        def _():
            _stage2_start(*stage2)
            dg1_acc[...] = jnp.zeros_like(dg1_acc)
            pack[...] = jnp.zeros_like(pack)
            pack[1:2, :] = dg2[...]
            pack[2:3, :] = dgf[...]
            pack[3:4, 0:NQK] = dbg[...]
            pack[3:4, NQK:NQK + DV] = dgng[...]
            pack[3:4, NQK + DV:NQK + 2 * DV] = lp[...]
            for r in range(RANK):
                row, lane = _wgu_slot(r)
                pack[row:row + 1, lane:lane + NQK] = dwgu[r:r + 1, :]
            for r, (row, lane) in enumerate(CONV_SLOTS):
                pack[row:row + 1, lane:lane + CW] = dcw[r:r + 1, :]
            for cp in small_copies(pack, gbuf, ssend, srecv, False):
                cp.start()
            gbuf[me] = pack[...]

        @pl.when(pl.program_id(0) == RELAY_AT * nt // 8)
        def _():
            _stage2_combine(*stage2)

        xv = x_ref[...]
        r = lax.rsqrt(jnp.mean(xv * xv, axis=-1, keepdims=True) + EPS)
        n1 = xv * r
        dh = _mm(dp_ref[...], w_ref[...])
        dg1_acc[...] += jnp.sum(dh * n1, axis=0, keepdims=True)
        dn = dh * g_ref[...]
        dx_ref[...] = dx1_ref[...] + r * (dn - n1 * jnp.mean(dn * n1, axis=-1, keepdims=True))

        @pl.when(pl.program_id(0) == nt - 1)
        def _():
            pack1[...] = jnp.zeros_like(pack1)
            pack1[0:1, :] = dg1_acc[...]
            for cp in small_copies(pack1, gbuf1, ssend1, srecv1, False):
                cp.start()
            gbuf1[me] = pack1[...]
            _stage2_finish(*stage2)
            for src, dst, send, recv in ((pack, gbuf, ssend, srecv), (pack1, gbuf1, ssend1, srecv1)):
                for cp in small_copies(src, dst, send, recv, True):
                    cp.wait_recv()
                    cp.wait_send()
            acc = gbuf[0]
            acc1 = gbuf1[0]
            for d in range(1, N_DEV):
                acc = acc + gbuf[d]
                acc1 = acc1 + gbuf1[d]
            sums_ref[...] = acc
            sums_ref[0:1, :] = acc1[0:1, :]

    tile = lambda w: pl.BlockSpec((tm, w), lambda i: (i, 0))
    vec = pl.BlockSpec((1, D), lambda i: (0, 0))
    hbm = pl.BlockSpec(memory_space=pl.ANY)
    whole = lambda a: pl.BlockSpec(a.shape, lambda i: (0,) * a.ndim)
    return pl.pallas_call(
        body,
        name="in_proj_bwd",
        grid=(nt,),
        in_specs=[tile(PW), tile(D), tile(D), vec, pl.BlockSpec((PW, D), lambda i: (0, 0)), hbm]
        + [whole(a) for a in small_parts],
        out_specs=[tile(D), pl.BlockSpec((SMALL_PACK_ROWS, D), lambda i: (0, 0)), hbm],
        out_shape=[jax.ShapeDtypeStruct((t, D), F32), jax.ShapeDtypeStruct((SMALL_PACK_ROWS, D), F32),
                   jax.ShapeDtypeStruct((2,) + pb.shape[1:], BF16)],
        scratch_shapes=[pltpu.VMEM((1, D), F32),
                        pltpu.VMEM((SMALL_PACK_ROWS, D), F32), pltpu.VMEM((N_DEV, SMALL_PACK_ROWS, D), F32),
                        pltpu.VMEM((8, D), F32), pltpu.VMEM((N_DEV, 8, D), F32),
                        pltpu.SemaphoreType.DMA((7,)), pltpu.SemaphoreType.DMA((7,)),
                        pltpu.SemaphoreType.DMA((7,)), pltpu.SemaphoreType.DMA((7,))] + _stage2_scratch([pb]),
        compiler_params=_params(("arbitrary",)),
    )(dproj2d, x2d, dx1, g1, w_in_t, pb, *small_parts)


def _get_rows(ref):
    return ref[:, 0, :] if len(ref.shape) == 3 else ref[...]


def _put_rows(ref, val):
    if len(ref.shape) == 3:
        ref[:, 0, :] = val
    else:
        ref[...] = val


def _adamw_math(w, g, m, v):
    m = ADAM_B1 * m + (1.0 - ADAM_B1) * g
    v = ADAM_B2 * v + (1.0 - ADAM_B2) * (g * g)
    m_hat = m / (1.0 - ADAM_B1 ** ADAM_STEP)
    v_hat = v / (1.0 - ADAM_B2 ** ADAM_STEP)
    delta = -ADAM_LR * (m_hat / (jnp.sqrt(v_hat) + ADAM_EPS) + ADAM_WD * w)
    return delta, m, v


def _position():
    return lax.axis_index("x"), lax.axis_index("y"), lax.axis_index("c")


GATHER_PARTS = 2
GATHER_SEMS = 7 * GATHER_PARTS
RELAY_AT = 3
FORWARD_AT = 7


def _gather_copies(stage, lo, rows, gx, send_sems, recv_sems, local_sem):
    x, y, c = _position()
    me = (x, y, c)
    sibling = (x, y, 1 - c)
    chips = [(1 - x, y), (x, 1 - y), (1 - x, 1 - y)]
    part = -(-rows // (16 * GATHER_PARTS)) * 16
    bounds = [(p * part, min(part, rows - p * part)) for p in range(GATHER_PARTS)]

    def blk(px, py, pc, off, n):
        return gx.at[4 * px + 2 * py + pc, pl.ds(off, n), :]

    mine = pltpu.make_async_copy(stage.at[pl.ds(lo, rows), :], gx.at[4 * x + 2 * y + c], local_sem)
    parts = []
    for p, (off, n) in enumerate(bounds):
        def copy(k, block, to, from_stage=False, p=p, off=off, n=n):
            return pltpu.make_async_remote_copy(
                src_ref=stage.at[pl.ds(lo + off, n), :] if from_stage else blk(*block, off, n),
                dst_ref=blk(*block, off, n), send_sem=send_sems.at[7 * p + k], recv_sem=recv_sems.at[7 * p + k],
                device_id=to, device_id_type=MESH)

        first = [copy(0, me, sibling, True)] + [copy(1 + j, me, (*chips[j], c), True) for j in range(2)]
        relay = copy(3, (*chips[p], c), (*chips[1 - p], c))
        passed = [copy(4 + j, (*chip, c), sibling) for j, chip in enumerate(chips)]
        arrivals = ([copy(0, sibling, me)] + [copy(1 + j, (*chip, c), me) for j, chip in enumerate(chips)]
                    + [copy(4 + j, (*chip, 1 - c), me) for j, chip in enumerate(chips)])
        parts.append((first, relay, passed, arrivals))
    return mine, parts


def _gather_start(*args):
    mine, parts = _gather_copies(*args)
    mine.start()
    for first, _, _, _ in parts:
        first[0].start()
    for p, q in ((0, 0), (1, 1), (0, 1), (1, 0)):
        parts[p][0][1 + q].start()


def _gather_relay(*args):
    _, parts = _gather_copies(*args)
    for p, (_, relay, passed, arrivals) in enumerate(parts):
        arrivals[1 + p].wait_recv()
        relay.start()
        passed[p].start()


def _gather_forward(*args):
    _, parts = _gather_copies(*args)
    for p, j in ((0, 1), (1, 0), (0, 2), (1, 2)):
        _, _, passed, arrivals = parts[p]
        arrivals[1 + j].wait_recv()
        passed[j].start()


def _gather_finish(*args):
    mine, parts = _gather_copies(*args)
    for first, relay, passed, arrivals in parts:
        arrivals[0].wait_recv()
        for j in range(3):
            arrivals[4 + j].wait_recv()
        for cp in first + [relay] + passed:
            cp.wait_send()
    mine.wait()


def _gather_sems():
    return [pltpu.SemaphoreType.DMA((GATHER_SEMS,)), pltpu.SemaphoreType.DMA((GATHER_SEMS,)), pltpu.SemaphoreType.DMA]


def _gather_w_in(w_it, w_gt, w_ut, w_d, w_o, wgu_s, conv_s, x2d, g1, tm):
    t = x2d.shape[0]
    nt = t // tm

    def body(wi_ref, wg_ref, wu_ref, wd_ref, wo_ref, wgu_ref, conv_ref, x_ref, g_ref,
             w_ref, gwgu_ref, gconv_ref, stage, h_ref, buf, send_sems, recv_sems, local_sem, ssend, srecv, own_sems):
        x, y, c = _position()
        me = 4 * x + 2 * y + c
        step = pl.program_id(0)
        args = (stage, SLAB_IN, IN_ROWS, buf, send_sems, recv_sems, local_sem)
        flips = [(k >> 2, (k >> 1) & 1, k & 1) for k in range(1, N_DEV)]
        peers = [(x ^ fx, y ^ fy, c ^ fc) for fx, fy, fc in flips]

        def small(k, block_id, to):
            return [pltpu.make_async_remote_copy(
                src_ref=s, dst_ref=g.at[block_id], send_sem=ssend.at[2 * k + n], recv_sem=srecv.at[2 * k + n],
                device_id=to, device_id_type=MESH)
                for n, (s, g) in enumerate(((wgu_ref, gwgu_ref), (conv_ref, gconv_ref)))]

        own = [pltpu.make_async_copy(wgu_ref, gwgu_ref.at[me], own_sems.at[0]),
               pltpu.make_async_copy(conv_ref, gconv_ref.at[me], own_sems.at[1])]

        @pl.when(step == 0)
        def _():
            stage[SLAB_IN:SLAB_IN + IN_W, :] = wi_ref[:, 0, :].astype(BF16)
            stage[SLAB_IN + IN_W:SLAB_G, :] = jnp.zeros((IN_ROWS - IN_W, D), BF16)
            _gather_start(*args)
            stage[SLAB_G:SLAB_U, :] = wg_ref[...].astype(BF16)
            stage[SLAB_U:SLAB_D, :] = wu_ref[...].astype(BF16)
            stage[SLAB_D:SLAB_O, :] = wd_ref[...].astype(BF16)
            stage[SLAB_O:SLAB_ROWS, :] = wo_ref[...].astype(BF16)
            for cp in own:
                cp.start()
            for k, peer in enumerate(peers):
                for cp in small(k, me, peer):
                    cp.start()
            w_ref[IN_COLS:PW, :] = jnp.zeros((PW - IN_COLS, D), BF16)

        @pl.when(step == RELAY_AT * nt // 8)
        def _():
            _gather_relay(*args)

        xv = x_ref[...]
        r = lax.rsqrt(jnp.mean(xv * xv, axis=-1, keepdims=True) + EPS)
        h_ref[...] = ((xv * r) * g_ref[...]).astype(BF16)

        @pl.when(step == nt - 1)
        def _():
            _gather_forward(*args)
            _gather_finish(*args)
            for k, (px, py, pc) in enumerate(peers):
                for cp in small(k, 4 * px + 2 * py + pc, (px, py, pc)):
                    cp.wait_recv()
                    cp.wait_send()
            for cp in own:
                cp.wait()
            for j, lo, hi, d in _in_segments():
                w_ref[d:d + hi - lo, :] = buf[j, lo:hi, :]

    vm = pl.BlockSpec(memory_space=pltpu.VMEM)
    return pl.pallas_call(
        body,
        name="gather_w_in",
        grid=(nt,),
        in_specs=[vm] * 7 + [pl.BlockSpec((tm, D), lambda i: (i, 0)), pl.BlockSpec((1, D), lambda i: (0, 0))],
        out_specs=[vm, pl.BlockSpec(memory_space=pl.ANY), pl.BlockSpec(memory_space=pl.ANY), vm,
                   pl.BlockSpec((tm, D), lambda i: (i, 0))],
        out_shape=[jax.ShapeDtypeStruct((PW, D), BF16),
                   jax.ShapeDtypeStruct((N_DEV,) + wgu_s.shape, F32),
                   jax.ShapeDtypeStruct((N_DEV,) + conv_s.shape, F32),
                   jax.ShapeDtypeStruct((SLAB_ROWS, D), BF16),
                   jax.ShapeDtypeStruct((t, D), BF16)],
        scratch_shapes=[pltpu.VMEM((N_DEV, IN_ROWS, D), BF16)] + _gather_sems()
        + [pltpu.SemaphoreType.DMA((14,)), pltpu.SemaphoreType.DMA((14,)), pltpu.SemaphoreType.DMA((2,))],
        compiler_params=_params(("arbitrary",)),
    )(w_it, w_gt, w_ut, w_d, w_o, wgu_s, conv_s, x2d, g1)


def _w_in_core_reduce(dw_t):
    def body(d_ref, own_ref, sib_ref, pb_ref, g, gb, r1, send_sems, recv_sems):
        x, y, c = _position()
        chip = 2 * x + y
        for j in range(N_DEV):
            g[j, IN_W:IN_ROWS, :] = jnp.zeros((IN_ROWS - IN_W, D), F32)
        for j, lo, hi, d in _in_segments():
            g[j, lo:hi, :] = d_ref[d:d + hi - lo, :]
        for j in range(N_DEV):
            gb[j] = g[j].astype(BF16)
        copies = _stage1_copies(gb, r1, send_sems, recv_sems)
        for cp in copies:
            cp.start()
        own_ref[0] = g[2 * chip + c]
        for cp in copies:
            cp.wait_recv()
        sib_ref[0] = r1[chip]
        for k in range(1, 4):
            t = chip ^ k
            pb_ref[k - 1] = (g[2 * t + c] + r1[t].astype(F32)).astype(BF16)
        for cp in copies:
            cp.wait_send()

    vm = pl.BlockSpec(memory_space=pltpu.VMEM)
    return pl.pallas_call(
        body,
        name="w_in_core_reduce",
        in_specs=[vm],
        out_specs=[vm, vm, vm],
        out_shape=[jax.ShapeDtypeStruct((1, IN_ROWS, D), F32), jax.ShapeDtypeStruct((1, IN_ROWS, D), BF16),
                   jax.ShapeDtypeStruct((3, IN_ROWS, D), BF16)],
        scratch_shapes=[pltpu.VMEM((N_DEV, IN_ROWS, D), F32), pltpu.VMEM((N_DEV, IN_ROWS, D), BF16),
                        pltpu.VMEM((4, IN_ROWS, D), BF16), pltpu.SemaphoreType.DMA((4,)),
                        pltpu.SemaphoreType.DMA((4,))],
        compiler_params=_params(),
    )(dw_t)


def _stage1_copies(g_ref, r_ref, send_sems, recv_sems):
    x, y, c = _position()
    return [pltpu.make_async_remote_copy(
        src_ref=g_ref.at[2 * i + 1 - c], dst_ref=r_ref.at[i], send_sem=send_sems.at[i], recv_sem=recv_sems.at[i],
        device_id=(x, y, 1 - c), device_id_type=MESH) for i in range(4)]


def _ffn_core_reduce(dw3, dwb3, dw_o, dwb_o, pos_arr, dx1b, gwb):
    def body(pos_ref, g0, g1, g2, go, gb3_hbm, gbo_hbm, dx1b_ref, gwb_hbm, p0, p1, p2, po, s0, s1, s2, so, dmix_ref,
             r1f, r1o, wo, send_sems, recv_sems, wsem):
        step = pl.program_id(0)
        k = jnp.minimum(step, 2)
        x, y, c = _position()
        chip = 2 * x + y

        def copies(p):
            src = 2 * (chip ^ ((p + 1) & 3)) + 1 - c
            pairs = [(gb3_hbm.at[a, src], r1f.at[a, p]) for a in range(3)] + [(gbo_hbm.at[src], r1o.at[p])]
            return [pltpu.make_async_remote_copy(
                src_ref=s, dst_ref=d, send_sem=send_sems.at[4 * p + a], recv_sem=recv_sems.at[4 * p + a],
                device_id=(x, y, 1 - c), device_id_type=MESH) for a, (s, d) in enumerate(pairs)]

        @pl.when(step == 0)
        def _():
            for p in range(4):
                for cp in copies(p):
                    cp.start()
            loads = [pltpu.make_async_copy(gwb_hbm.at[j, pl.ds(D_TAIL, OUT_ROWS), :],
                                           wo.at[pl.ds(OUT_ROWS * j, OUT_ROWS), :], wsem.at[j]) for j in range(N_DEV)]
            for cp in loads:
                cp.start()
            for cp in loads:
                cp.wait()

        dmix_ref[...] = _nt(dx1b_ref[...], wo[...])

        for p in range(3):
            @pl.when(step == p)
            def _():
                for cp in copies(p):
                    cp.wait_recv()

        for a, (g, pb) in enumerate(((g0, p0), (g1, p1), (g2, p2))):
            pb[...] = (g[...] + r1f[a, k][None].astype(F32)).astype(BF16)
        po[...] = (go[...] + r1o[k][None].astype(F32)).astype(BF16)

        @pl.when(step == 3)
        def _():
            for cp in copies(3):
                cp.wait_recv()
            for a, s in enumerate((s0, s1, s2)):
                s[0] = r1f[a, 3]
            so[0] = r1o[3]
            for p in range(4):
                for cp in copies(p):
                    cp.wait_send()

    t = dx1b.shape[0]
    other = lambda s, pos: 2 * (pos[1] ^ (jnp.minimum(s, 2) + 1)) + pos[0]
    g_spec = lambda lead: pl.BlockSpec((None, 1, FF_W, D), lambda s, pos: (lead, other(s, pos), 0, 0))
    slot = lambda rows: pl.BlockSpec((1, rows, D), lambda s, pos: (jnp.minimum(s, 2), 0, 0))
    one = lambda rows: pl.BlockSpec((1, rows, D), lambda s, pos: (0, 0, 0))
    quarter = pl.BlockSpec((t // 4, D), lambda s, pos: (s, 0))
    hbm = pl.BlockSpec(memory_space=pl.ANY)
    return pl.pallas_call(
        body,
        name="ffn_core_reduce",
        grid_spec=pltpu.PrefetchScalarGridSpec(
            num_scalar_prefetch=1, grid=(4,),
            in_specs=[g_spec(0), g_spec(1), g_spec(2),
                      pl.BlockSpec((1, OUT_ROWS, D), lambda s, pos: (other(s, pos), 0, 0)), hbm, hbm, quarter, hbm],
            out_specs=[slot(FF_W), slot(FF_W), slot(FF_W), slot(OUT_ROWS),
                       one(FF_W), one(FF_W), one(FF_W), one(OUT_ROWS), quarter],
            scratch_shapes=[pltpu.VMEM((3, 4, FF_W, D), BF16), pltpu.VMEM((4, OUT_ROWS, D), BF16),
                            pltpu.VMEM((D, D), BF16), pltpu.SemaphoreType.DMA((16,)),
                            pltpu.SemaphoreType.DMA((16,)), pltpu.SemaphoreType.DMA((N_DEV,))]),
        out_shape=[jax.ShapeDtypeStruct((3, FF_W, D), BF16)] * 3 + [jax.ShapeDtypeStruct((3, OUT_ROWS, D), BF16)]
        + [jax.ShapeDtypeStruct((1, FF_W, D), BF16)] * 3 + [jax.ShapeDtypeStruct((1, OUT_ROWS, D), BF16),
                                                             jax.ShapeDtypeStruct((t, D), F32)],
        compiler_params=_params(("arbitrary",)),
    )(pos_arr, dw3, dw3, dw3, dw_o, dwb3, dwb_o, dx1b, gwb)


def _stage2_scratch(pbs):
    n = len(pbs)
    return ([pltpu.VMEM(p.shape[1:], BF16) for p in pbs] * 2
            + [pltpu.SemaphoreType.DMA((6 * n,)), pltpu.SemaphoreType.DMA((6 * n,)), pltpu.SemaphoreType.DMA((2 * n,))])


def _stage2_copies(p_refs, r_refs, scratch):
    n = len(p_refs)
    owns, gots = scratch[:n], scratch[n:2 * n]
    send_sems, recv_sems, load_sems = scratch[2 * n:]
    x, y, c = _position()
    xn, yn = (1 - x, y, c), (x, 1 - y, c)
    loads, first, second = [], [], []
    for a, (p, r, own, got) in enumerate(zip(p_refs, r_refs, owns, gots)):
        rows = p.shape[1]
        half = -(-rows // 32) * 16
        h0, h1 = pl.ds(0, half), pl.ds(half, rows - half)

        def remote(k, src, dst, to, a=a):
            return pltpu.make_async_remote_copy(
                src_ref=src, dst_ref=dst, send_sem=send_sems.at[6 * a + k], recv_sem=recv_sems.at[6 * a + k],
                device_id=to, device_id_type=MESH)

        loads += [pltpu.make_async_copy(p.at[0, h0, :], own.at[h0, :], load_sems.at[2 * a]),
                  pltpu.make_async_copy(p.at[1, h1, :], own.at[h1, :], load_sems.at[2 * a + 1])]
        first += [remote(0, p.at[2, h0, :], got.at[h0, :], xn), remote(1, p.at[2, h1, :], got.at[h1, :], yn),
                  remote(2, p.at[1, h0, :], r.at[1, h0, :], xn), remote(3, p.at[0, h1, :], r.at[0, h1, :], yn)]
        second += [remote(4, own.at[h0, :], r.at[0, h0, :], yn), remote(5, own.at[h1, :], r.at[1, h1, :], xn)]
    return loads, first, second


def _stage2_start(p_refs, r_refs, scratch):
    loads, first, _ = _stage2_copies(p_refs, r_refs, scratch)
    for cp in loads:
        cp.start()
    for k in range(4):
        for cp in first[k::4]:
            cp.start()


def _stage2_combine(p_refs, r_refs, scratch):
    n = len(p_refs)
    loads, first, second = _stage2_copies(p_refs, r_refs, scratch)
    for a in range(n):
        for cp in loads[2 * a:2 * a + 2]:
            cp.wait()
        for cp in first[4 * a:4 * a + 2]:
            cp.wait_recv()
        own, got = scratch[a], scratch[n + a]
        own[...] = (own[...].astype(F32) + got[...].astype(F32)).astype(BF16)
        for cp in second[2 * a:2 * a + 2]:
            cp.start()


def _stage2_finish(p_refs, r_refs, scratch):
    _, first, second = _stage2_copies(p_refs, r_refs, scratch)
    for a in range(len(p_refs)):
        for cp in first[4 * a + 2:4 * a + 4] + second[2 * a:2 * a + 2]:
            cp.wait_recv()
    for cp in first + second:
        cp.wait_send()


def _finish_weights(items, pos_arr, name, nblk):
    n = len(items)
    in_specs, out_specs, out_shape, operands, wbs = [], [], [], [], []
    for g8, lead, r1, r2, w, m, v in items:
        rows, wr = g8.shape[-2], w.shape[0]
        assert rows % nblk == 0 and wr % nblk == 0 and (nblk == 1 or (rows == wr and rows % (16 * nblk) == 0))
        rb, wb = rows // nblk, wr // nblk
        if lead is not None:
            g_spec = pl.BlockSpec((None, 1, rb, D), lambda i, pos, lead=lead: (lead, 2 * pos[1] + pos[0], i, 0))
        elif g8.shape[0] == 1:
            g_spec = pl.BlockSpec((1, rb, D), lambda i, pos: (0, i, 0))
        else:
            g_spec = pl.BlockSpec((1, rb, D), lambda i, pos: (2 * pos[1] + pos[0], i, 0))
        r1_spec = pl.BlockSpec((1, rb, D), lambda i, pos: (0, i, 0))
        if w.ndim == 3:
            wblk = pl.BlockSpec((wb, 1, D), lambda i, pos: (i, 0, 0))
        else:
            wblk = pl.BlockSpec((wb, D), lambda i, pos: (i, 0))
        in_specs += [g_spec, r1_spec, pl.BlockSpec((2, rb, D), lambda i, pos: (0, i, 0)), wblk, wblk, wblk]
        out_specs += [wblk] * 4
        out_shape += [jax.ShapeDtypeStruct(w.shape, F32)] * 4
        operands += [g8, r1, r2, w, m, v]
        wbs.append(wb)

    def body(pos_ref, *refs):
        for a in range(n):
            g_ref, r1_ref, r2_ref, w_ref, m_ref, v_ref = refs[6 * a:6 * a + 6]
            g_out, d_out, m_out, v_out = refs[6 * n + 4 * a:6 * n + 4 * a + 4]
            g = g_ref[0] + r1_ref[0].astype(F32)
            for k in range(2):
                g = g + r2_ref[k].astype(F32)
            g = g[0:wbs[a], :]
            d, mn, vn = _adamw_math(_get_rows(w_ref), g, _get_rows(m_ref), _get_rows(v_ref))
            for out, val in ((g_out, g), (d_out, d), (m_out, mn), (v_out, vn)):
                _put_rows(out, val)

    return pl.pallas_call(
        body,
        name=name,
        grid_spec=pltpu.PrefetchScalarGridSpec(
            num_scalar_prefetch=1, grid=(nblk,), in_specs=in_specs, out_specs=out_specs),
        out_shape=out_shape,
        compiler_params=_params(("arbitrary",)),
    )(pos_arr, *operands)


SMALL_NAMES = ("norm1_g", "norm2_g", "norm_f_g", "b_gate", "gla_norm_g", "w_gate_up", "conv_w")
WGU_W = NQK // N_DEV
CONV_W = CW // N_DEV


def _small_adamw(sums, ws, ms, vs):
    n = len(SMALL_NAMES)

    def body(*refs):
        acc_ref = refs[0]
        w_refs, m_refs, v_refs = refs[1:1 + n], refs[1 + n:1 + 2 * n], refs[1 + 2 * n:1 + 3 * n]
        loss_ref = refs[1 + 3 * n]
        outs = refs[2 + 3 * n:]
        x, y, c = _position()
        me = 4 * x + 2 * y + c
        acc = acc_ref[...]
        loss_ref[...] = acc[3:4, NQK + DV:NQK + DV + 1]

        def my_columns(full, width):
            r = lax.broadcasted_iota(jnp.int32, (full.shape[1], width), 0)
            col = lax.broadcasted_iota(jnp.int32, (full.shape[1], width), 1)
            sel = (r == width * me + col).astype(F32)
            return _mm(full, sel, precision=HIGHEST)

        dwgu = jnp.concatenate([acc[row:row + 1, lane:lane + NQK] for row, lane in map(_wgu_slot, range(RANK))], axis=0)
        dcw = jnp.concatenate([acc[row:row + 1, lane:lane + CW] for row, lane in CONV_SLOTS], axis=0)
        grads = [acc[0:1, :], acc[1:2, :], acc[2:3, :], acc[3:4, 0:NQK], acc[3:4, NQK:NQK + DV],
                 my_columns(dwgu, WGU_W), my_columns(dcw, CONV_W)]
        for i, g in enumerate(grads):
            d, mn, vn = _adamw_math(_get_rows(w_refs[i]), g, _get_rows(m_refs[i]), _get_rows(v_refs[i]))
            for out, val in zip(outs[4 * i:4 * i + 4], (g, d, mn, vn)):
                _put_rows(out, val)

    vm = pl.BlockSpec(memory_space=pltpu.VMEM)
    out_shape = [jax.ShapeDtypeStruct((1, 1), F32)]
    for w in ws:
        out_shape += [jax.ShapeDtypeStruct(w.shape, F32)] * 4
    return pl.pallas_call(
        body,
        name="small_adamw",
        in_specs=[vm] * (1 + 3 * n),
        out_specs=[vm] * (1 + 4 * n),
        out_shape=out_shape,
        compiler_params=_params(),
    )(sums, *ws, *ms, *vs)


def kernel(x, norm1_g, w_in, w_gate_up, b_gate, gla_norm_g, conv_w, w_out, norm2_g, w_ffn_gate, w_ffn_up, w_ffn_down, norm_f_g, loss_target, m_norm1_g, m_w_in, m_w_gate_up, m_b_gate, m_gla_norm_g, m_conv_w, m_w_out, m_norm2_g, m_w_ffn_gate, m_w_ffn_up, m_w_ffn_down, m_norm_f_g, v_norm1_g, v_w_in, v_w_gate_up, v_b_gate, v_gla_norm_g, v_conv_w, v_w_out, v_norm2_g, v_w_ffn_gate, v_w_ffn_up, v_w_ffn_down, v_norm_f_g):
    xi, yi, ci = _position()
    pos_arr = jnp.stack([ci, 2 * xi + yi]).astype(jnp.int32)
    nb, s, _ = x.shape
    t = nb * s

    tr = lambda a: a[0].T
    rows_of = lambda a: a.transpose(2, 0, 1)
    conv_rows = lambda a: a.transpose(1, 0, 2)
    x2d = x.reshape(t, D)
    tm_in = min(512, t)
    w_in_t, gwgu, gconv, stage, h = _gather_w_in(rows_of(w_in), tr(w_ffn_gate), tr(w_ffn_up), w_ffn_down[0],
                                                 w_out[0], w_gate_up[0], conv_rows(conv_w), x2d, norm1_g, tm_in)
    wgu_f = gwgu.transpose(1, 0, 2).reshape(RANK, NQK)
    conv_f = gconv.transpose(1, 2, 0, 3).reshape(CONV_K, CW)
    wgu_p = jnp.concatenate([wgu_f, jnp.zeros((A_PAD - RANK, NQK), F32)], axis=0).astype(BF16)

    tgt2d = loss_target.reshape(t, D)
    tm = 256
    tk = min(2048, t)
    proj, z, gwb = _in_proj_fwd(h, w_in_t, wgu_p, b_gate, tm_in, stage)
    proj3 = proj.reshape(nb, s, PW)
    z3 = z.reshape(nb, s, NQK)
    mix3, opre3, sprev, x1, gwa = _mix_fwd(proj3, z3, gla_norm_g, conv_f, stage, x, gwb)
    mix2d = mix3.reshape(t, D)
    dx1, dx1b, adu, hb, dg2, dgf, loss_part = _ffn_fwd_bwd(
        x1.reshape(t, D), tgt2d, gwa, gwb, norm2_g, norm_f_g.reshape(1, D), tm)
    dw3, dwb3 = _dw_ffn(adu, hb, tk)
    dw3 = dw3.reshape(3, N_DEV, FF_W, D)
    dw_o, dwb_o = _tn_matmul(mix2d, dx1b, D // 2, D, tk, "dw_out", True)
    dw_o = dw_o.reshape(N_DEV, OUT_ROWS, D)
    *pb, sib_d, sib_g, sib_u, sib_o, dmix = _ffn_core_reduce(
        dw3, dwb3.reshape(3, N_DEV, FF_W, D), dw_o, dwb_o.reshape(N_DEV, OUT_ROWS, D), pos_arr, dx1b, gwb)
    g8 = [dw3, dw3, dw3, dw_o]
    leads = [0, 1, 2, None]
    tags = ("w_ffn_down", "w_ffn_gate", "w_ffn_up", "w_out")
    r1 = [sib_d, sib_g, sib_u, sib_o]
    mb = _mix_bwd(proj3, z3, sprev, opre3, dmix.reshape(nb, s, D), gla_norm_g, conv_f, wgu_p, [pb[0], pb[1], pb[3]])
    dproj3, dgng, dcw, dbg, dwgu = mb[:5]
    dproj2d = dproj3.reshape(t, PW)
    dw_in_t, r2_up = _tn_matmul(dproj2d, h, PW // 5, D, t, "dw_in", False, _stage2_rider([pb[2]]))
    r2 = [mb[5], mb[6], r2_up, mb[7]]
    g_in, r1_in, pb_in = _w_in_core_reduce(dw_in_t)
    dx, small_sums, r2_in = _in_proj_bwd(dproj2d, x2d, dx1, norm1_g, w_in_t, tm_in, pb_in,
                                         (dg2, dgf, dbg, dgng, dwgu, dcw, loss_part))

    tags = ("w_in",) + tags
    g8 = [g_in] + g8
    leads = [None] + leads
    r1 = [r1_in] + list(r1)
    r2 = [r2_in] + r2
    shard_w = (rows_of(w_in), w_ffn_down[0], tr(w_ffn_gate), tr(w_ffn_up), w_out[0])
    shard_m = (rows_of(m_w_in), m_w_ffn_down[0], tr(m_w_ffn_gate), tr(m_w_ffn_up), m_w_out[0])
    shard_v = (rows_of(v_w_in), v_w_ffn_down[0], tr(v_w_ffn_gate), tr(v_w_ffn_up), v_w_out[0])
    back = (lambda o: o.transpose(1, 2, 0), lambda o: o[None], lambda o: o.T[None], lambda o: o.T[None],
            lambda o: o[None])
    items = list(zip(g8, leads, r1, r2, shard_w, shard_m, shard_v))
    flat = list(_finish_weights(items[1:], pos_arr, "finish_ffn_out", 2))
    flat = list(_finish_weights(items[:1], pos_arr, "finish_w_in", 1)) + flat
    results = {}
    for i, (tag, to_shard) in enumerate(zip(tags, back)):
        results[tag] = [to_shard(o) for o in flat[4 * i:4 * i + 4]]

    small_w = (norm1_g, norm2_g, norm_f_g.reshape(1, D), b_gate, gla_norm_g, w_gate_up[0], conv_rows(conv_w))
    small_m = (m_norm1_g, m_norm2_g, m_norm_f_g.reshape(1, D), m_b_gate, m_gla_norm_g, m_w_gate_up[0],
               conv_rows(m_conv_w))
    small_v = (v_norm1_g, v_norm2_g, v_norm_f_g.reshape(1, D), v_b_gate, v_gla_norm_g, v_w_gate_up[0],
               conv_rows(v_conv_w))
    so = _small_adamw(small_sums, small_w, small_m, small_v)
    loss = so[0].reshape(())
    to_shape = {"norm_f_g": lambda o: o.reshape(D), "w_gate_up": lambda o: o[None],
                "conv_w": lambda o: o.transpose(1, 0, 2)}
    for i, name in enumerate(SMALL_NAMES):
        results[name] = [to_shape.get(name, lambda o: o)(o) for o in so[1 + 4 * i:5 + 4 * i]]

    names = ("norm1_g", "w_in", "w_gate_up", "b_gate", "gla_norm_g", "conv_w", "w_out", "norm2_g",
             "w_ffn_gate", "w_ffn_up", "w_ffn_down", "norm_f_g")
    outs = [loss, dx.reshape(nb, s, D)]
    for kind in range(4):
        for name in names:
            outs.append(results[name][kind])
    return tuple(outs)
```

```python
import jax
import jax.numpy as jnp
from jax import lax
from jax.experimental import pallas as pl
from jax.experimental.pallas import tpu as pltpu

F32 = jnp.float32
BF16 = jnp.bfloat16
HIGHEST = lax.Precision.HIGHEST
MESH = pl.DeviceIdType.MESH

N_DEV = 8
D = 1024
DFF = 2816
HEADS = 4
DK = 64
DV = 128
NQK = HEADS * DK
NV = HEADS * DV
RANK = 16
CHUNK = 64
CW = 512
CONV_K = 3
IN_COLS = 3088
EPS = 1e-6
INV_GATE_NORM = 1.0 / 16.0
Q_SCALE = DK ** -0.5

PW = 3200
OQ, OK_, OV, OG, OCB, OCC, OCH, OA = 0, 256, 512, 1024, 1536, 2048, 2560, 3072
A_PAD = 128

ADAM_LR = 0.001
ADAM_B1 = 0.9
ADAM_B2 = 0.999
ADAM_EPS = 1e-08
ADAM_WD = 0.01
ADAM_STEP = 10

IN_W = IN_COLS // N_DEV
IN_ROWS = 400
FF_W = DFF // N_DEV
OUT_ROWS = D // N_DEV
SLAB_IN = 0
SLAB_G = SLAB_IN + IN_ROWS
SLAB_U = SLAB_G + FF_W
SLAB_D = SLAB_U + FF_W
SLAB_O = SLAB_D + FF_W
SLAB_ROWS = SLAB_O + OUT_ROWS
D_HEAD = 128
D_TAIL = FF_W - D_HEAD
SLAB_SPLIT = SLAB_D + D_HEAD

VMEM_LIMIT = 56 * 1024 * 1024


def _params(sem=None, vmem=VMEM_LIMIT):
    return pltpu.CompilerParams(dimension_semantics=sem, vmem_limit_bytes=vmem)


def _nt(a, b):
    return lax.dot_general(a, b, (((1,), (1,)), ((), ())), preferred_element_type=F32)


def _tn(a, b, precision=None):
    return lax.dot_general(a, b, (((0,), (0,)), ((), ())), preferred_element_type=F32, precision=precision)


def _mm(a, b, precision=None):
    return jnp.dot(a, b, preferred_element_type=F32, precision=precision)


def _in_segments():
    segs = []
    for j in range(N_DEV):
        lo, hi = IN_W * j, IN_W * (j + 1)
        cuts = sorted({lo, hi} | {c for c in (OCB, OCB + RANK) if lo < c < hi})
        for a, b in zip(cuts[:-1], cuts[1:]):
            if a < OCB:
                d = a
            elif a < OCB + RANK:
                d = OA + (a - OCB)
            else:
                d = a - RANK
            segs.append((j, a - lo, b - lo, d))
    return segs


def _in_proj_fwd(x2d, g1, w_in_t, wgu_p, b_gate, tm, stage):
    t = x2d.shape[0]
    nt = t // tm
    g_rows = SLAB_ROWS - SLAB_SPLIT

    def body(x_ref, g_ref, w_ref, wgu_ref, bg_ref, stage_hbm, proj_ref, z_ref, h_ref, gwb_ref,
             send_sems, recv_sems, local_sem):
        gargs = (stage_hbm, SLAB_SPLIT, g_rows, gwb_ref, send_sems, recv_sems, local_sem)

        @pl.when(pl.program_id(0) == 0)
        def _():
            _gather_start(*gargs)

        @pl.when(pl.program_id(0) == RELAY_AT * nt // 8)
        def _():
            _gather_relay(*gargs)

        @pl.when(pl.program_id(0) == FORWARD_AT * nt // 8)
        def _():
            _gather_forward(*gargs)

        x = x_ref[...]
        r = lax.rsqrt(jnp.mean(x * x, axis=-1, keepdims=True) + EPS)
        h = ((x * r) * g_ref[...]).astype(BF16)
        h_ref[...] = h
        proj = _nt(h, w_ref[...])
        proj_ref[...] = proj
        pa = proj[:, OA:OA + A_PAD].astype(BF16)
        z_ref[...] = _mm(pa, wgu_ref[...]) + bg_ref[...]

        @pl.when(pl.program_id(0) == nt - 1)
        def _():
            _gather_finish(*gargs)

    return pl.pallas_call(
        body,
        name="in_proj_fwd",
        grid=(t // tm,),
        in_specs=[
            pl.BlockSpec((tm, D), lambda i: (i, 0)),
            pl.BlockSpec((1, D), lambda i: (0, 0)),
            pl.BlockSpec((PW, D), lambda i: (0, 0)),
            pl.BlockSpec((A_PAD, NQK), lambda i: (0, 0)),
            pl.BlockSpec((1, NQK), lambda i: (0, 0)),
            pl.BlockSpec(memory_space=pl.ANY),
        ],
        out_specs=[
            pl.BlockSpec((tm, PW), lambda i: (i, 0)),
            pl.BlockSpec((tm, NQK), lambda i: (i, 0)),
            pl.BlockSpec((tm, D), lambda i: (i, 0)),
            pl.BlockSpec(memory_space=pl.ANY),
        ],
        out_shape=[
            jax.ShapeDtypeStruct((t, PW), F32),
            jax.ShapeDtypeStruct((t, NQK), F32),
            jax.ShapeDtypeStruct((t, D), BF16),
            jax.ShapeDtypeStruct((N_DEV, g_rows, D), BF16),
        ],
        scratch_shapes=_gather_sems(),
        compiler_params=_params(("arbitrary",)),
    )(x2d, g1, w_in_t, wgu_p, b_gate, stage)


def _head_masks():
    lane = lax.broadcasted_iota(jnp.int32, (1, NQK), 1)
    return [(lane >= DK * h) & (lane < DK * (h + 1)) for h in range(HEADS)]


def _split_bf16(x, n):
    parts = []
    for _ in range(n):
        p = x.astype(BF16)
        parts.append(p)
        x = x - p.astype(F32)
    return parts


def _chunk_fwd_parts(q, k, z, tril16):
    la = (jnp.minimum(z, 0.0) - jnp.log1p(jnp.exp(-jnp.abs(z)))) * INV_GATE_NORM
    la_parts = _split_bf16(la, 3)
    bc = _mm(tril16, la_parts[0]) + _mm(tril16, la_parts[1]) + _mm(tril16, la_parts[2])
    bl = bc[CHUNK - 1:CHUNK, :]
    eb = jnp.exp(bc)
    enb = jnp.exp(-bc)
    ekl = jnp.exp(bl - bc)
    qi = (q * Q_SCALE) * eb
    ki = k * enb
    ks = k * ekl
    ones16 = jnp.ones((CHUNK, DV), BF16)
    decb = jnp.exp(_tn(la_parts[0], ones16) + _tn(la_parts[1], ones16) + _tn(la_parts[2], ones16))
    return la, eb, enb, ekl, qi, ki, ks, decb


def _stack_heads(a, masks):
    return jnp.concatenate([jnp.where(m, a, 0.0) for m in masks], axis=0)


def _merge_heads(blocks, masks):
    out = blocks[HEADS - 1]
    for h in range(HEADS - 2, -1, -1):
        out = jnp.where(masks[h], blocks[h], out)
    return out


def _causal_stack_mask():
    row = lax.broadcasted_iota(jnp.int32, (HEADS * CHUNK, CHUNK), 0)
    col = lax.broadcasted_iota(jnp.int32, (HEADS * CHUNK, CHUNK), 1)
    return (row & (CHUNK - 1)) >= col


def _conv_taps(u, uprev):
    row = lax.broadcasted_iota(jnp.int32, u.shape, 0)
    u1 = jnp.where(row < 1, pltpu.roll(uprev, 1, 0), pltpu.roll(u, 1, 0))
    u2 = jnp.where(row < 2, pltpu.roll(uprev, 2, 0), pltpu.roll(u, 2, 0))
    return u1, u2


def _mix_fwd(proj3, z3, gng, conv_w, stage, x3, gwb):
    nb, s, _ = proj3.shape
    nc = s // CHUNK
    g_rows = SLAB_SPLIT - SLAB_G

    def body(p_ref, z_ref, gng_ref, cw_ref, stage_hbm, x_ref, gwb_hbm, mix_ref, o_ref, sprev_ref, x1_ref, gwa_ref,
             s_ref, uprev_ref, wo, wsem, send_sems, recv_sems, local_sem):
        n = pl.program_id(0)
        gargs = (stage_hbm, SLAB_G, g_rows, gwa_ref, send_sems, recv_sems, local_sem)

        @pl.when(n == 0)
        def _():
            _gather_start(*gargs)
            loads = [pltpu.make_async_copy(gwb_hbm.at[j, pl.ds(D_TAIL, OUT_ROWS), :],
                                           wo.at[pl.ds(OUT_ROWS * j, OUT_ROWS), :], wsem.at[j]) for j in range(N_DEV)]
            for cp in loads:
                cp.start()
            s_ref[...] = jnp.zeros_like(s_ref)
            uprev_ref[...] = jnp.zeros_like(uprev_ref)
            for cp in loads:
                cp.wait()

        @pl.when(n == RELAY_AT * nc // 8)
        def _():
            _gather_relay(*gargs)

        @pl.when(n == FORWARD_AT * nc // 8)
        def _():
            _gather_forward(*gargs)

        r_i = lax.broadcasted_iota(jnp.int32, (CHUNK, CHUNK), 0)
        c_i = lax.broadcasted_iota(jnp.int32, (CHUNK, CHUNK), 1)
        tril16 = (r_i >= c_i).astype(BF16)
        masks = _head_masks()
        cmask = _causal_stack_mask()
        gg = gng_ref[...]
        for b in range(nb):
            q = p_ref[b, :, OQ:OQ + NQK]
            k = p_ref[b, :, OK_:OK_ + NQK]
            _, _, _, _, qi, ki, ks, decb = _chunk_fwd_parts(q, k, z_ref[b], tril16)
            qs = _stack_heads(qi, masks).astype(BF16)
            sc = jnp.where(cmask, _nt(qs, ki.astype(BF16)), 0.0).astype(BF16)
            st = s_ref[b]
            sprev_ref[b, 0] = st
            o_inter = _mm(qs, st.astype(BF16))
            v16 = p_ref[b, :, OV:OV + NV].astype(BF16)
            kv = _tn(ks.astype(BF16), v16)
            for h in range(HEADS):
                rows = slice(CHUNK * h, CHUNK * (h + 1))
                cols = slice(DV * h, DV * (h + 1))
                o = _mm(sc[rows], v16[:, cols]) + o_inter[rows]
                o_ref[b, :, cols] = o
                r = lax.rsqrt(jnp.mean(o * o, axis=-1, keepdims=True) + EPS)
                on = (o * r) * gg
                g = p_ref[b, :, OG + DV * h:OG + DV * (h + 1)]
                mix_ref[b, :, cols] = (on * (g * jax.nn.sigmoid(g))).astype(BF16)
                s_ref[b, rows, :] = decb[rows] * st[rows] + kv[rows, cols]
            u = p_ref[b, :, OCC:OCC + CW] * p_ref[b, :, OCH:OCH + CW]
            u1, u2 = _conv_taps(u, uprev_ref[b])
            yc = cw_ref[0:1, :] * u2 + cw_ref[1:2, :] * u1 + cw_ref[2:3, :] * u
            mix_ref[b, :, NV:NV + CW] = (p_ref[b, :, OCB:OCB + CW] * yc).astype(BF16)
            uprev_ref[b] = u
        mixed = _mm(jnp.concatenate([mix_ref[b] for b in range(nb)], axis=0), wo[...])
        for b in range(nb):
            x1_ref[b] = x_ref[b] + mixed[CHUNK * b:CHUNK * (b + 1)]

        @pl.when(n == nc - 1)
        def _():
            _gather_finish(*gargs)

    return pl.pallas_call(
        body,
        name="mix_fwd",
        grid=(nc,),
        in_specs=[
            pl.BlockSpec((nb, CHUNK, PW), lambda n: (0, n, 0)),
            pl.BlockSpec((nb, CHUNK, NQK), lambda n: (0, n, 0)),
            pl.BlockSpec((1, DV), lambda n: (0, 0)),
            pl.BlockSpec((CONV_K, CW), lambda n: (0, 0)),
            pl.BlockSpec(memory_space=pl.ANY),
            pl.BlockSpec((nb, CHUNK, D), lambda n: (0, n, 0)),
            pl.BlockSpec(memory_space=pl.ANY),
        ],
        out_specs=[
            pl.BlockSpec((nb, CHUNK, D), lambda n: (0, n, 0)),
            pl.BlockSpec((nb, CHUNK, NV), lambda n: (0, n, 0)),
            pl.BlockSpec((nb, 1, NQK, DV), lambda n: (0, n, 0, 0)),
            pl.BlockSpec((nb, CHUNK, D), lambda n: (0, n, 0)),
            pl.BlockSpec(memory_space=pl.ANY),
        ],
        out_shape=[
            jax.ShapeDtypeStruct((nb, s, D), BF16),
            jax.ShapeDtypeStruct((nb, s, NV), F32),
            jax.ShapeDtypeStruct((nb, nc, NQK, DV), F32),
            jax.ShapeDtypeStruct((nb, s, D), F32),
            jax.ShapeDtypeStruct((N_DEV, g_rows, D), BF16),
        ],
        scratch_shapes=[pltpu.VMEM((nb, NQK, DV), F32), pltpu.VMEM((nb, CHUNK, CW), F32),
                        pltpu.VMEM((D, D), BF16), pltpu.SemaphoreType.DMA((N_DEV,))] + _gather_sems(),
        compiler_params=_params(("arbitrary",)),
    )(proj3, z3, gng, conv_w, stage, x3, gwb)


def _ffn_fwd_bwd(x1_2d, tgt2d, gwa, gwb, g2, gf, tm):
    t = x1_2d.shape[0]

    def body(x1_ref, tgt_ref, g2_ref, gf_ref, gwa_hbm, gwb_hbm,
             dx1_ref, dx1b_ref, adu_ref, hb_ref, dg2_ref, dgf_ref, loss_ref,
             wg, wu, wd, wsem):
        i = pl.program_id(0)

        def weight_copies(n, dst, src, off, rows, at=0):
            return [pltpu.make_async_copy(src.at[j, pl.ds(off, rows), :], dst.at[pl.ds(FF_W * j + at, rows), :],
                                          wsem.at[N_DEV * n + j]) for j in range(N_DEV)]

        loads = (weight_copies(0, wg, gwa_hbm, 0, FF_W), weight_copies(1, wu, gwa_hbm, FF_W, FF_W),
                 weight_copies(2, wd, gwa_hbm, 2 * FF_W, D_HEAD), weight_copies(3, wd, gwb_hbm, 0, D_TAIL, D_HEAD))

        @pl.when(i == 0)
        def _():
            for group in loads:
                for cp in group:
                    cp.start()
            dg2_ref[...] = jnp.zeros_like(dg2_ref)
            dgf_ref[...] = jnp.zeros_like(dgf_ref)
            loss_ref[...] = jnp.zeros_like(loss_ref)
            for group in loads:
                for cp in group:
                    cp.wait()

        g2v = g2_ref[...]
        gfv = gf_ref[...]
        x1 = x1_ref[...]
        r2 = lax.rsqrt(jnp.mean(x1 * x1, axis=-1, keepdims=True) + EPS)
        n2 = x1 * r2
        h2 = (n2 * g2v).astype(BF16)
        hb_ref[1] = h2
        gate = _nt(h2, wg[...])
        up = _nt(h2, wu[...])
        sg = jax.nn.sigmoid(gate)
        sil = gate * sg
        act = (sil * up).astype(BF16)
        adu_ref[0] = act
        x2 = x1 + _mm(act, wd[...])
        rf = lax.rsqrt(jnp.mean(x2 * x2, axis=-1, keepdims=True) + EPS)
        nf = x2 * rf
        err = nf * gfv - tgt_ref[...]
        loss_ref[...] += 0.5 * jnp.sum(jnp.mean(err * err, axis=-1, keepdims=True))
        dy = err * (1.0 / D)
        dgf_ref[...] += jnp.sum(dy * nf, axis=0, keepdims=True)
        dnf = dy * gfv
        dx2 = rf * (dnf - nf * jnp.mean(dnf * nf, axis=-1, keepdims=True))
        dx2b = dx2.astype(BF16)
        hb_ref[0] = dx2b
        dact = _nt(dx2b, wd[...])
        dup = (dact * sil).astype(BF16)
        dgate = ((dact * up) * (sg * (1.0 + gate * (1.0 - sg)))).astype(BF16)
        adu_ref[2] = dup
        adu_ref[1] = dgate
        dh2 = _mm(dgate, wg[...]) + _mm(dup, wu[...])
        dg2_ref[...] += jnp.sum(dh2 * n2, axis=0, keepdims=True)
        dn2 = dh2 * g2v
        dx1 = dx2 + r2 * (dn2 - n2 * jnp.mean(dn2 * n2, axis=-1, keepdims=True))
        dx1_ref[...] = dx1
        dx1b_ref[...] = dx1.astype(BF16)

    tile = lambda w: pl.BlockSpec((tm, w), lambda i: (i, 0))
    vec = pl.BlockSpec((1, D), lambda i: (0, 0))
    hbm = pl.BlockSpec(memory_space=pl.ANY)
    return pl.pallas_call(
        body,
        name="ffn_fwd_bwd",
        grid=(t // tm,),
        in_specs=[tile(D), tile(D), vec, vec, hbm, hbm],
        out_specs=[tile(D), tile(D), pl.BlockSpec((3, tm, DFF), lambda i: (0, i, 0)),
                   pl.BlockSpec((2, tm, D), lambda i: (0, i, 0)), vec, vec,
                   pl.BlockSpec((1, 128), lambda i: (0, 0))],
        out_shape=[
            jax.ShapeDtypeStruct((t, D), F32),
            jax.ShapeDtypeStruct((t, D), BF16),
            jax.ShapeDtypeStruct((3, t, DFF), BF16),
            jax.ShapeDtypeStruct((2, t, D), BF16),
            jax.ShapeDtypeStruct((1, D), F32),
            jax.ShapeDtypeStruct((1, D), F32),
            jax.ShapeDtypeStruct((1, 128), F32),
        ],
        scratch_shapes=[pltpu.VMEM((DFF, D), BF16), pltpu.VMEM((DFF, D), BF16), pltpu.VMEM((DFF, D), BF16),
                        pltpu.SemaphoreType.DMA((4 * N_DEV,))],
        compiler_params=_params(("arbitrary",)),
    )(x1_2d, tgt2d, g2, gf, gwa, gwb)


def _stage2_rider(pbs):
    return dict(inputs=list(pbs), out_shape=[jax.ShapeDtypeStruct((2,) + p.shape[1:], BF16) for p in pbs],
                scratch=_stage2_scratch(pbs))


def _tn_matmul(a, b, bm, bn, tk, name, with_bf16, rider=None):
    t, m = a.shape
    n = b.shape[1]
    nk = t // tk
    nout = 2 if with_bf16 else 1
    grid = (m // bm, n // bn, nk)
    steps = grid[0] * grid[1] * nk
    r_in = [] if rider is None else rider["inputs"]
    r_out = [] if rider is None else rider["out_shape"]

    def body(a_ref, b_ref, *rest):
        ins, outs = rest[:len(r_in)], rest[len(r_in):len(r_in) + nout]
        r_outs, scratch = rest[len(r_in) + nout:len(r_in) + nout + len(r_out)], rest[len(r_in) + nout + len(r_out):]
        o_ref = outs[0]
        i, j, k = pl.program_id(0), pl.program_id(1), pl.program_id(2)
        step = (i * grid[1] + j) * nk + k
        if rider is not None:
            @pl.when(step == 0)
            def _():
                _stage2_start(ins, r_outs, scratch)

            @pl.when(step == steps // 2)
            def _():
                _stage2_combine(ins, r_outs, scratch)

        @pl.when(k == 0)
        def _():
            o_ref[...] = jnp.zeros_like(o_ref)

        o_ref[...] += _tn(a_ref[...].astype(BF16), b_ref[...].astype(BF16))
        if with_bf16:
            @pl.when(k == nk - 1)
            def _():
                outs[1][...] = o_ref[...].astype(BF16)
        if rider is not None:
            @pl.when(step == steps - 1)
            def _():
                _stage2_finish(ins, r_outs, scratch)

    out_blk = pl.BlockSpec((bm, bn), lambda i, j, k: (i, j))
    hbm = pl.BlockSpec(memory_space=pl.ANY)
    out_shape = [jax.ShapeDtypeStruct((m, n), F32)] + ([jax.ShapeDtypeStruct((m, n), BF16)] if with_bf16 else [])
    res = pl.pallas_call(
        body,
        name=name,
        grid=grid,
        in_specs=[pl.BlockSpec((tk, bm), lambda i, j, k: (k, i)), pl.BlockSpec((tk, bn), lambda i, j, k: (k, j))]
        + [hbm] * len(r_in),
        out_specs=[out_blk] * nout + [hbm] * len(r_out),
        out_shape=out_shape + list(r_out),
        scratch_shapes=[] if rider is None else rider["scratch"],
        compiler_params=_params(("parallel", "parallel", "arbitrary") if rider is None
                                else ("arbitrary", "arbitrary", "arbitrary")),
    )(a, b, *r_in)
    return res[0] if len(res) == 1 else res


def _dw_ffn(adu, hb, tk):
    _, t, _ = adu.shape
    bm = DFF // 2
    nk = t // tk

    def body(a_ref, b_ref, o_ref, ob_ref):
        k = pl.program_id(2)

        @pl.when(k == 0)
        def _():
            o_ref[...] = jnp.zeros_like(o_ref)

        o_ref[...] += _tn(a_ref[...], b_ref[...])

        @pl.when(k == nk - 1)
        def _():
            ob_ref[...] = o_ref[...].astype(BF16)

    out_blk = pl.BlockSpec((None, bm, D), lambda p, i, k: (p, i, 0))
    return pl.pallas_call(
        body,
        name="dw_ffn",
        grid=(3, DFF // bm, nk),
        in_specs=[pl.BlockSpec((None, tk, bm), lambda p, i, k: (p, k, i)),
                  pl.BlockSpec((None, tk, D), lambda p, i, k: (jnp.minimum(p, 1), k, 0))],
        out_specs=[out_blk, out_blk],
        out_shape=[jax.ShapeDtypeStruct((3, DFF, D), F32), jax.ShapeDtypeStruct((3, DFF, D), BF16)],
        compiler_params=_params(("arbitrary", "arbitrary", "arbitrary")),
    )(adu, hb)


def _mix_bwd(proj3, z3, sprev, opre3, dmix3, gng, conv_w, wgu_p, pbs):
    nb, s, _ = proj3.shape
    nc = s // CHUNK
    na = len(pbs)

    def body(*refs):
        (p_ref, pprev_ref, z_ref, sp_ref, o_ref, dm_ref, gng_ref, cw_ref, wgu_ref) = refs[:9]
        pb_refs = refs[9:9 + na]
        (dproj_ref, dgng_ref, dcw_ref, dbg_ref, dwgu_ref) = refs[9 + na:14 + na]
        r2_refs = refs[14 + na:14 + 2 * na]
        ds_ref, dycn_ref = refs[14 + 2 * na:16 + 2 * na]
        stage2 = (pb_refs, r2_refs, refs[16 + 2 * na:])
        step = pl.program_id(0)
        n = nc - 1 - step

        @pl.when(step == 0)
        def _():
            _stage2_start(*stage2)
            ds_ref[...] = jnp.zeros_like(ds_ref)
            dycn_ref[...] = jnp.zeros_like(dycn_ref)
            dgng_ref[...] = jnp.zeros_like(dgng_ref)
            dcw_ref[...] = jnp.zeros_like(dcw_ref)
            dbg_ref[...] = jnp.zeros_like(dbg_ref)
            dwgu_ref[...] = jnp.zeros_like(dwgu_ref)

        @pl.when(step == RELAY_AT * nc // 8)
        def _():
            _stage2_combine(*stage2)

        r_i = lax.broadcasted_iota(jnp.int32, (CHUNK, CHUNK), 0)
        c_i = lax.broadcasted_iota(jnp.int32, (CHUNK, CHUNK), 1)
        tril16 = (r_i >= c_i).astype(BF16)
        triu16 = (r_i <= c_i).astype(BF16)
        causal = r_i >= c_i
        masks = _head_masks()
        cmask = _causal_stack_mask()
        gg = gng_ref[...]
        last_row = lax.broadcasted_iota(jnp.int32, (CHUNK, NQK), 0) == CHUNK - 1
        ones_r = jnp.ones((16, DV), BF16)
        has_prev = (n > 0).astype(F32)
        for b in range(nb):
            q = p_ref[b, :, OQ:OQ + NQK]
            k = p_ref[b, :, OK_:OK_ + NQK]
            z = z_ref[b]
            _, eb, enb, ekl, qi, ki, ks, decb = _chunk_fwd_parts(q, k, z, tril16)
            qi16 = qi.astype(BF16)
            ki16 = ki.astype(BF16)
            qs = _stack_heads(qi, masks).astype(BF16)
            sc = jnp.where(cmask, _nt(qs, ki16), 0.0).astype(BF16)
            st = sp_ref[b, 0]
            st16 = st.astype(BF16)
            dsn = ds_ref[b]
            dsn16 = dsn.astype(BF16)
            v16 = p_ref[b, :, OV:OV + NV].astype(BF16)
            do16 = []
            dgng = jnp.zeros((1, DV), F32)
            for h in range(HEADS):
                cols = slice(DV * h, DV * (h + 1))
                o = o_ref[b, :, cols]
                r = lax.rsqrt(jnp.mean(o * o, axis=-1, keepdims=True) + EPS)
                nh = o * r
                g = p_ref[b, :, OG + DV * h:OG + DV * (h + 1)]
                sg = jax.nn.sigmoid(g)
                dog = dm_ref[b, :, cols]
                dproj_ref[b, :, OG + DV * h:OG + DV * (h + 1)] = (
                    (dog * (nh * gg)) * (sg * (1.0 + g * (1.0 - sg)))).astype(BF16)
                don = dog * (g * sg)
                dgng = dgng + jnp.sum(don * nh, axis=0, keepdims=True)
                dn = don * gg
                do = r * (dn - nh * jnp.mean(dn * nh, axis=-1, keepdims=True))
                do16.append(do.astype(BF16))
            dgng_ref[...] += dgng
            do_rows = jnp.concatenate(do16, axis=0)
            v_rows = jnp.concatenate([v16[:, DV * h:DV * (h + 1)] for h in range(HEADS)], axis=0)
            dp16 = [jnp.where(causal, _nt(do16[h], v16[:, DV * h:DV * (h + 1)]), 0.0).astype(BF16)
                    for h in range(HEADS)]
            ks_dsn = _mm(_stack_heads(ks, masks).astype(BF16), dsn16)
            do_st = _nt(do_rows, st16)
            v_dsn = _nt(v_rows, dsn16)
            dp_ki = _mm(jnp.concatenate(dp16, axis=0), ki16)
            q_do = _tn(qi16, jnp.concatenate(do16, axis=1))
            dki_h = []
            for h in range(HEADS):
                rows = slice(CHUNK * h, CHUNK * (h + 1))
                cols = slice(DV * h, DV * (h + 1))
                dv = _tn(sc[rows], do16[h]) + ks_dsn[rows]
                dproj_ref[b, :, OV + DV * h:OV + DV * (h + 1)] = dv.astype(BF16)
                dki_h.append(_tn(dp16[h], qi16))
                ds_ref[b, rows, :] = decb[rows] * dsn[rows] + q_do[rows, cols]
            blocks = lambda a: [a[CHUNK * h:CHUNK * (h + 1)] for h in range(HEADS)]
            dqi = _merge_heads(blocks(dp_ki + do_st), masks)
            dki = _merge_heads(dki_h, masks)
            dks = _merge_heads(blocks(v_dsn), masks)
            dproj_ref[b, :, OQ:OQ + NQK] = (dqi * (Q_SCALE * eb)).astype(BF16)
            dproj_ref[b, :, OK_:OK_ + NQK] = (dki * enb + dks * ekl).astype(BF16)
            dks_ks = dks * ks
            db = dqi * qi - dki * ki - dks_ks
            sd = _split_bf16(dsn * st * decb, 2)
            dbl = jnp.sum(dks_ks, axis=0, keepdims=True) + (_nt(ones_r, sd[0]) + _nt(ones_r, sd[1]))[0:1, :]
            db = db + jnp.where(last_row, dbl, 0.0)
            db_parts = _split_bf16(db, 3)
            dla = _mm(triu16, db_parts[0]) + _mm(triu16, db_parts[1]) + _mm(triu16, db_parts[2])
            dz = (dla * INV_GATE_NORM) * (1.0 / (1.0 + jnp.exp(z)))
            dbg_ref[...] += jnp.sum(dz, axis=0, keepdims=True)
            dz16 = dz.astype(BF16)
            pa16 = p_ref[b, :, OA:OA + A_PAD].astype(BF16)
            dwgu_ref[...] += _tn(pa16, dz16)
            dproj_ref[b, :, OA:OA + A_PAD] = _nt(dz16, wgu_ref[...]).astype(BF16)
            cb = p_ref[b, :, OCB:OCB + CW]
            cc = p_ref[b, :, OCC:OCC + CW]
            ch = p_ref[b, :, OCH:OCH + CW]
            u = cc * ch
            uprev = (pprev_ref[b, :, 0:CW] * pprev_ref[b, :, CW:2 * CW]) * has_prev
            u1, u2 = _conv_taps(u, uprev)
            w0 = cw_ref[0:1, :]
            w1 = cw_ref[1:2, :]
            w2 = cw_ref[2:3, :]
            yc = w0 * u2 + w1 * u1 + w2 * u
            doc = dm_ref[b, :, NV:NV + CW]
            dproj_ref[b, :, OCB:OCB + CW] = (doc * yc).astype(BF16)
            dyc = doc * cb
            dycn = dycn_ref[b]
            row = lax.broadcasted_iota(jnp.int32, dyc.shape, 0)
            d1 = jnp.where(row >= CHUNK - 1, pltpu.roll(dycn, CHUNK - 1, 0), pltpu.roll(dyc, CHUNK - 1, 0))
            d2 = jnp.where(row >= CHUNK - 2, pltpu.roll(dycn, CHUNK - 2, 0), pltpu.roll(dyc, CHUNK - 2, 0))
            du = w2 * dyc + w1 * d1 + w0 * d2
            dproj_ref[b, :, OCC:OCC + CW] = (du * ch).astype(BF16)
            dproj_ref[b, :, OCH:OCH + CW] = (du * cc).astype(BF16)
            dcw_ref[0:1, :] += jnp.sum(dyc * u2, axis=0, keepdims=True)
            dcw_ref[1:2, :] += jnp.sum(dyc * u1, axis=0, keepdims=True)
            dcw_ref[2:3, :] += jnp.sum(dyc * u, axis=0, keepdims=True)
            dycn_ref[b] = dyc

        @pl.when(step == nc - 1)
        def _():
            _stage2_finish(*stage2)

    rev =lambda w: pl.BlockSpec((nb, CHUNK, w), lambda i: (0, nc - 1 - i, 0))
    const = lambda r, c: pl.BlockSpec((r, c), lambda i: (0, 0))
    hbm = pl.BlockSpec(memory_space=pl.ANY)
    return pl.pallas_call(
        body,
        name="mix_bwd",
        grid=(nc,),
        in_specs=[
            rev(PW),
            pl.BlockSpec((nb, CHUNK, 2 * CW), lambda i: (0, jnp.maximum(nc - 2 - i, 0), OCC // (2 * CW))),
            rev(NQK),
            pl.BlockSpec((nb, 1, NQK, DV), lambda i: (0, nc - 1 - i, 0, 0)),
            rev(NV),
            rev(D),
            const(1, DV),
            const(CONV_K, CW),
            const(A_PAD, NQK),
        ] + [hbm] * na,
        out_specs=[rev(PW), const(1, DV), const(8, CW), const(1, NQK), const(A_PAD, NQK)] + [hbm] * na,
        out_shape=[
            jax.ShapeDtypeStruct((nb, s, PW), BF16),
            jax.ShapeDtypeStruct((1, DV), F32),
            jax.ShapeDtypeStruct((8, CW), F32),
            jax.ShapeDtypeStruct((1, NQK), F32),
            jax.ShapeDtypeStruct((A_PAD, NQK), F32),
        ] + [jax.ShapeDtypeStruct((2,) + p.shape[1:], BF16) for p in pbs],
        scratch_shapes=[pltpu.VMEM((nb, NQK, DV), F32), pltpu.VMEM((nb, CHUNK, CW), F32)] + _stage2_scratch(pbs),
        compiler_params=_params(("arbitrary",)),
    )(proj3, proj3, z3, sprev, opre3, dmix3, gng, conv_w, wgu_p, *pbs)


SMALL_PACK_ROWS = 16


def _wgu_slot(r):
    return 4 + r // 4, NQK * (r % 4)


CONV_SLOTS = ((8, 0), (8, CW), (9, 0))


def _in_proj_bwd(dproj2d, x2d, dx1, g1, w_in_t, tm, pb, small_parts):
    t = x2d.shape[0]
    nt = t // tm

    def body(dp_ref, x_ref, dx1_ref, g_ref, w_ref, pb_ref, dg2, dgf, dbg, dgng, dwgu, dcw, lp,
             dx_ref, sums_ref, r2_ref, dg1_acc, pack, gbuf, pack1, gbuf1, ssend, srecv, ssend1, srecv1, *scratch2):
        stage2 = ([pb_ref], [r2_ref], scratch2)
        x, y, c = _position()
        me = 4 * x + 2 * y + c
        flips = [(k >> 2, (k >> 1) & 1, k & 1) for k in range(1, N_DEV)]
        peers = [(x ^ fx, y ^ fy, c ^ fc) for fx, fy, fc in flips]

        def small_copies(src, dst, send, recv, arrivals):
            return [pltpu.make_async_remote_copy(
                src_ref=src, dst_ref=dst.at[4 * px + 2 * py + pc if arrivals else me],
                send_sem=send.at[k], recv_sem=recv.at[k], device_id=(px, py, pc), device_id_type=MESH)
                for k, (px, py, pc) in enumerate(peers)]

        @pl.when(pl.program_id(0) == 0)
        def _():
            _stage2_start(*stage2)
            dg1_acc[...] = jnp.zeros_like(dg1_acc)
            pack[...] = jnp.zeros_like(pack)
            pack[1:2, :] = dg2[...]
            pack[2:3, :] = dgf[...]
            pack[3:4, 0:NQK] = dbg[...]
            pack[3:4, NQK:NQK + DV] = dgng[...]
            pack[3:4, NQK + DV:NQK + 2 * DV] = lp[...]
            for r in range(RANK):
                row, lane = _wgu_slot(r)
                pack[row:row + 1, lane:lane + NQK] = dwgu[r:r + 1, :]
            for r, (row, lane) in enumerate(CONV_SLOTS):
                pack[row:row + 1, lane:lane + CW] = dcw[r:r + 1, :]
            for cp in small_copies(pack, gbuf, ssend, srecv, False):
                cp.start()
            gbuf[me] = pack[...]

        @pl.when(pl.program_id(0) == RELAY_AT * nt // 8)
        def _():
            _stage2_combine(*stage2)

        xv = x_ref[...]
        r = lax.rsqrt(jnp.mean(xv * xv, axis=-1, keepdims=True) + EPS)
        n1 = xv * r
        dh = _mm(dp_ref[...], w_ref[...])
        dg1_acc[...] += jnp.sum(dh * n1, axis=0, keepdims=True)
        dn = dh * g_ref[...]
        dx_ref[...] = dx1_ref[...] + r * (dn - n1 * jnp.mean(dn * n1, axis=-1, keepdims=True))

        @pl.when(pl.program_id(0) == nt - 1)
        def _():
            pack1[...] = jnp.zeros_like(pack1)
            pack1[0:1, :] = dg1_acc[...]
            for cp in small_copies(pack1, gbuf1, ssend1, srecv1, False):
                cp.start()
            gbuf1[me] = pack1[...]
            _stage2_finish(*stage2)
            for src, dst, send, recv in ((pack, gbuf, ssend, srecv), (pack1, gbuf1, ssend1, srecv1)):
                for cp in small_copies(src, dst, send, recv, True):
                    cp.wait_recv()
                    cp.wait_send()
            acc = gbuf[0]
            acc1 = gbuf1[0]
            for d in range(1, N_DEV):
                acc = acc + gbuf[d]
                acc1 = acc1 + gbuf1[d]
            sums_ref[...] = acc
            sums_ref[0:1, :] = acc1[0:1, :]

    tile = lambda w: pl.BlockSpec((tm, w), lambda i: (i, 0))
    vec = pl.BlockSpec((1, D), lambda i: (0, 0))
    hbm = pl.BlockSpec(memory_space=pl.ANY)
    whole = lambda a: pl.BlockSpec(a.shape, lambda i: (0,) * a.ndim)
    return pl.pallas_call(
        body,
        name="in_proj_bwd",
        grid=(nt,),
        in_specs=[tile(PW), tile(D), tile(D), vec, pl.BlockSpec((PW, D), lambda i: (0, 0)), hbm]
        + [whole(a) for a in small_parts],
        out_specs=[tile(D), pl.BlockSpec((SMALL_PACK_ROWS, D), lambda i: (0, 0)), hbm],
        out_shape=[jax.ShapeDtypeStruct((t, D), F32), jax.ShapeDtypeStruct((SMALL_PACK_ROWS, D), F32),
                   jax.ShapeDtypeStruct((2,) + pb.shape[1:], BF16)],
        scratch_shapes=[pltpu.VMEM((1, D), F32),
                        pltpu.VMEM((SMALL_PACK_ROWS, D), F32), pltpu.VMEM((N_DEV, SMALL_PACK_ROWS, D), F32),
                        pltpu.VMEM((8, D), F32), pltpu.VMEM((N_DEV, 8, D), F32),
                        pltpu.SemaphoreType.DMA((7,)), pltpu.SemaphoreType.DMA((7,)),
                        pltpu.SemaphoreType.DMA((7,)), pltpu.SemaphoreType.DMA((7,))] + _stage2_scratch([pb]),
        compiler_params=_params(("arbitrary",)),
    )(dproj2d, x2d, dx1, g1, w_in_t, pb, *small_parts)


def _get_rows(ref):
    return ref[:, 0, :] if len(ref.shape) == 3 else ref[...]


def _put_rows(ref, val):
    if len(ref.shape) == 3:
        ref[:, 0, :] = val
    else:
        ref[...] = val


def _adamw_math(w, g, m, v):
    m = ADAM_B1 * m + (1.0 - ADAM_B1) * g
    v = ADAM_B2 * v + (1.0 - ADAM_B2) * (g * g)
    m_hat = m / (1.0 - ADAM_B1 ** ADAM_STEP)
    v_hat = v / (1.0 - ADAM_B2 ** ADAM_STEP)
    delta = -ADAM_LR * (m_hat / (jnp.sqrt(v_hat) + ADAM_EPS) + ADAM_WD * w)
    return delta, m, v


def _position():
    return lax.axis_index("x"), lax.axis_index("y"), lax.axis_index("c")


GATHER_PARTS = 2
GATHER_SEMS = 7 * GATHER_PARTS
RELAY_AT = 3
FORWARD_AT = 7


def _gather_copies(stage, lo, rows, gx, send_sems, recv_sems, local_sem):
    x, y, c = _position()
    me = (x, y, c)
    sibling = (x, y, 1 - c)
    chips = [(1 - x, y), (x, 1 - y), (1 - x, 1 - y)]
    part = -(-rows // (16 * GATHER_PARTS)) * 16
    bounds = [(p * part, min(part, rows - p * part)) for p in range(GATHER_PARTS)]

    def blk(px, py, pc, off, n):
        return gx.at[4 * px + 2 * py + pc, pl.ds(off, n), :]

    mine = pltpu.make_async_copy(stage.at[pl.ds(lo, rows), :], gx.at[4 * x + 2 * y + c], local_sem)
    parts = []
    for p, (off, n) in enumerate(bounds):
        def copy(k, block, to, from_stage=False, p=p, off=off, n=n):
            return pltpu.make_async_remote_copy(
                src_ref=stage.at[pl.ds(lo + off, n), :] if from_stage else blk(*block, off, n),
                dst_ref=blk(*block, off, n), send_sem=send_sems.at[7 * p + k], recv_sem=recv_sems.at[7 * p + k],
                device_id=to, device_id_type=MESH)

        first = [copy(0, me, sibling, True)] + [copy(1 + j, me, (*chips[j], c), True) for j in range(2)]
        relay = copy(3, (*chips[p], c), (*chips[1 - p], c))
        passed = [copy(4 + j, (*chip, c), sibling) for j, chip in enumerate(chips)]
        arrivals = ([copy(0, sibling, me)] + [copy(1 + j, (*chip, c), me) for j, chip in enumerate(chips)]
                    + [copy(4 + j, (*chip, 1 - c), me) for j, chip in enumerate(chips)])
        parts.append((first, relay, passed, arrivals))
    return mine, parts


def _gather_start(*args):
    mine, parts = _gather_copies(*args)
    mine.start()
    for first, _, _, _ in parts:
        first[0].start()
    for p, q in ((0, 0), (1, 1), (0, 1), (1, 0)):
        parts[p][0][1 + q].start()


def _gather_relay(*args):
    _, parts = _gather_copies(*args)
    for p, (_, relay, passed, arrivals) in enumerate(parts):
        arrivals[1 + p].wait_recv()
        relay.start()
        passed[p].start()


def _gather_forward(*args):
    _, parts = _gather_copies(*args)
    for p, j in ((0, 1), (1, 0), (0, 2), (1, 2)):
        _, _, passed, arrivals = parts[p]
        arrivals[1 + j].wait_recv()
        passed[j].start()


def _gather_finish(*args):
    mine, parts = _gather_copies(*args)
    for first, relay, passed, arrivals in parts:
        arrivals[0].wait_recv()
        for j in range(3):
            arrivals[4 + j].wait_recv()
        for cp in first + [relay] + passed:
            cp.wait_send()
    mine.wait()


def _gather_sems():
    return [pltpu.SemaphoreType.DMA((GATHER_SEMS,)), pltpu.SemaphoreType.DMA((GATHER_SEMS,)), pltpu.SemaphoreType.DMA]


def _gather_w_in(w_it, w_gt, w_ut, w_d, w_o, wgu_s, conv_s):
    def body(wi_ref, wg_hbm, wu_hbm, wd_hbm, wo_hbm, wgu_ref, conv_ref, w_ref, gwgu_ref, gconv_ref, stage,
             buf, wf, wof, send_sems, recv_sems, local_sem, ssend, srecv, load_sems):
        x, y, c = _position()
        me = 4 * x + 2 * y + c
        loads = [pltpu.make_async_copy(src, dst, load_sems.at[n]) for n, (src, dst) in enumerate(
            ((wg_hbm, wf.at[0]), (wu_hbm, wf.at[1]), (wd_hbm, wf.at[2]), (wo_hbm, wof)))]
        for cp in loads:
            cp.start()
        stage[SLAB_IN:SLAB_IN + IN_W, :] = wi_ref[:, 0, :].astype(BF16)
        stage[SLAB_IN + IN_W:SLAB_G, :] = jnp.zeros((IN_ROWS - IN_W, D), BF16)
        args = (stage, SLAB_IN, IN_ROWS, buf, send_sems, recv_sems, local_sem)
        _gather_start(*args)
        for n, lo in enumerate((SLAB_G, SLAB_U, SLAB_D)):
            loads[n].wait()
            stage[lo:lo + FF_W, :] = wf[n].astype(BF16)
        loads[3].wait()
        stage[SLAB_O:SLAB_ROWS, :] = wof[...].astype(BF16)
        flips = [(k >> 2, (k >> 1) & 1, k & 1) for k in range(1, N_DEV)]
        peers = [(x ^ fx, y ^ fy, c ^ fc) for fx, fy, fc in flips]

        def small(k, block_id, to):
            return [pltpu.make_async_remote_copy(
                src_ref=s, dst_ref=g.at[block_id], send_sem=ssend.at[2 * k + n], recv_sem=srecv.at[2 * k + n],
                device_id=to, device_id_type=MESH)
                for n, (s, g) in enumerate(((wgu_ref, gwgu_ref), (conv_ref, gconv_ref)))]

        gwgu_ref[me] = wgu_ref[...]
        gconv_ref[me] = conv_ref[...]
        for k, peer in enumerate(peers):
            for cp in small(k, me, peer):
                cp.start()
        w_ref[IN_COLS:PW, :] = jnp.zeros((PW - IN_COLS, D), BF16)
        _gather_relay(*args)
        _gather_forward(*args)
        _gather_finish(*args)
        for k, (px, py, pc) in enumerate(peers):
            for cp in small(k, 4 * px + 2 * py + pc, (px, py, pc)):
                cp.wait_recv()
                cp.wait_send()
        for j, lo, hi, d in _in_segments():
            w_ref[d:d + hi - lo, :] = buf[j, lo:hi, :]

    vm = pl.BlockSpec(memory_space=pltpu.VMEM)
    return pl.pallas_call(
        body,
        name="gather_w_in",
        in_specs=[vm] + [pl.BlockSpec(memory_space=pl.ANY)] * 4 + [vm] * 2,
        out_specs=[vm] * 4,
        out_shape=[jax.ShapeDtypeStruct((PW, D), BF16),
                   jax.ShapeDtypeStruct((N_DEV,) + wgu_s.shape, F32),
                   jax.ShapeDtypeStruct((N_DEV,) + conv_s.shape, F32),
                   jax.ShapeDtypeStruct((SLAB_ROWS, D), BF16)],
        scratch_shapes=[pltpu.VMEM((N_DEV, IN_ROWS, D), BF16), pltpu.VMEM((3, FF_W, D), F32),
                        pltpu.VMEM((OUT_ROWS, D), F32)] + _gather_sems()
        + [pltpu.SemaphoreType.DMA((14,)), pltpu.SemaphoreType.DMA((14,)), pltpu.SemaphoreType.DMA((4,))],
        compiler_params=_params(),
    )(w_it, w_gt, w_ut, w_d, w_o, wgu_s, conv_s)


def _w_in_core_reduce(dw_t):
    def body(d_ref, own_ref, sib_ref, pb_ref, g, gb, r1, send_sems, recv_sems):
        x, y, c = _position()
        chip = 2 * x + y
        for j in range(N_DEV):
            g[j, IN_W:IN_ROWS, :] = jnp.zeros((IN_ROWS - IN_W, D), F32)
        for j, lo, hi, d in _in_segments():
            g[j, lo:hi, :] = d_ref[d:d + hi - lo, :]
        for j in range(N_DEV):
            gb[j] = g[j].astype(BF16)
        copies = _stage1_copies(gb, r1, send_sems, recv_sems)
        for cp in copies:
            cp.start()
        own_ref[0] = g[2 * chip + c]
        for cp in copies:
            cp.wait_recv()
        sib_ref[0] = r1[chip]
        for k in range(1, 4):
            t = chip ^ k
            pb_ref[k - 1] = (g[2 * t + c] + r1[t].astype(F32)).astype(BF16)
        for cp in copies:
            cp.wait_send()

    vm = pl.BlockSpec(memory_space=pltpu.VMEM)
    return pl.pallas_call(
        body,
        name="w_in_core_reduce",
        in_specs=[vm],
        out_specs=[vm, vm, vm],
        out_shape=[jax.ShapeDtypeStruct((1, IN_ROWS, D), F32), jax.ShapeDtypeStruct((1, IN_ROWS, D), BF16),
                   jax.ShapeDtypeStruct((3, IN_ROWS, D), BF16)],
        scratch_shapes=[pltpu.VMEM((N_DEV, IN_ROWS, D), F32), pltpu.VMEM((N_DEV, IN_ROWS, D), BF16),
                        pltpu.VMEM((4, IN_ROWS, D), BF16), pltpu.SemaphoreType.DMA((4,)),
                        pltpu.SemaphoreType.DMA((4,))],
        compiler_params=_params(),
    )(dw_t)


def _stage1_copies(g_ref, r_ref, send_sems, recv_sems):
    x, y, c = _position()
    return [pltpu.make_async_remote_copy(
        src_ref=g_ref.at[2 * i + 1 - c], dst_ref=r_ref.at[i], send_sem=send_sems.at[i], recv_sem=recv_sems.at[i],
        device_id=(x, y, 1 - c), device_id_type=MESH) for i in range(4)]


def _ffn_core_reduce(dw3, dwb3, dw_o, dwb_o, pos_arr, dx1b, gwb):
    def body(pos_ref, g0, g1, g2, go, gb3_hbm, gbo_hbm, dx1b_ref, gwb_hbm, p0, p1, p2, po, s0, s1, s2, so, dmix_ref,
             r1f, r1o, wo, send_sems, recv_sems, wsem):
        step = pl.program_id(0)
        k = jnp.minimum(step, 2)
        x, y, c = _position()
        chip = 2 * x + y

        def copies(p):
            src = 2 * (chip ^ ((p + 1) & 3)) + 1 - c
            pairs = [(gb3_hbm.at[a, src], r1f.at[a, p]) for a in range(3)] + [(gbo_hbm.at[src], r1o.at[p])]
            return [pltpu.make_async_remote_copy(
                src_ref=s, dst_ref=d, send_sem=send_sems.at[4 * p + a], recv_sem=recv_sems.at[4 * p + a],
                device_id=(x, y, 1 - c), device_id_type=MESH) for a, (s, d) in enumerate(pairs)]

        @pl.when(step == 0)
        def _():
            for p in range(4):
                for cp in copies(p):
                    cp.start()
            loads = [pltpu.make_async_copy(gwb_hbm.at[j, pl.ds(D_TAIL, OUT_ROWS), :],
                                           wo.at[pl.ds(OUT_ROWS * j, OUT_ROWS), :], wsem.at[j]) for j in range(N_DEV)]
            for cp in loads:
                cp.start()
            for cp in loads:
                cp.wait()

        dmix_ref[...] = _nt(dx1b_ref[...], wo[...])

        for p in range(3):
            @pl.when(step == p)
            def _():
                for cp in copies(p):
                    cp.wait_recv()

        for a, (g, pb) in enumerate(((g0, p0), (g1, p1), (g2, p2))):
            pb[...] = (g[...] + r1f[a, k][None].astype(F32)).astype(BF16)
        po[...] = (go[...] + r1o[k][None].astype(F32)).astype(BF16)

        @pl.when(step == 3)
        def _():
            for cp in copies(3):
                cp.wait_recv()
            for a, s in enumerate((s0, s1, s2)):
                s[0] = r1f[a, 3]
            so[0] = r1o[3]
            for p in range(4):
                for cp in copies(p):
                    cp.wait_send()

    t = dx1b.shape[0]
    other = lambda s, pos: 2 * (pos[1] ^ (jnp.minimum(s, 2) + 1)) + pos[0]
    g_spec = lambda lead: pl.BlockSpec((None, 1, FF_W, D), lambda s, pos: (lead, other(s, pos), 0, 0))
    slot = lambda rows: pl.BlockSpec((1, rows, D), lambda s, pos: (jnp.minimum(s, 2), 0, 0))
    one = lambda rows: pl.BlockSpec((1, rows, D), lambda s, pos: (0, 0, 0))
    quarter = pl.BlockSpec((t // 4, D), lambda s, pos: (s, 0))
    hbm = pl.BlockSpec(memory_space=pl.ANY)
    return pl.pallas_call(
        body,
        name="ffn_core_reduce",
        grid_spec=pltpu.PrefetchScalarGridSpec(
            num_scalar_prefetch=1, grid=(4,),
            in_specs=[g_spec(0), g_spec(1), g_spec(2),
                      pl.BlockSpec((1, OUT_ROWS, D), lambda s, pos: (other(s, pos), 0, 0)), hbm, hbm, quarter, hbm],
            out_specs=[slot(FF_W), slot(FF_W), slot(FF_W), slot(OUT_ROWS),
                       one(FF_W), one(FF_W), one(FF_W), one(OUT_ROWS), quarter],
            scratch_shapes=[pltpu.VMEM((3, 4, FF_W, D), BF16), pltpu.VMEM((4, OUT_ROWS, D), BF16),
                            pltpu.VMEM((D, D), BF16), pltpu.SemaphoreType.DMA((16,)),
                            pltpu.SemaphoreType.DMA((16,)), pltpu.SemaphoreType.DMA((N_DEV,))]),
        out_shape=[jax.ShapeDtypeStruct((3, FF_W, D), BF16)] * 3 + [jax.ShapeDtypeStruct((3, OUT_ROWS, D), BF16)]
        + [jax.ShapeDtypeStruct((1, FF_W, D), BF16)] * 3 + [jax.ShapeDtypeStruct((1, OUT_ROWS, D), BF16),
                                                             jax.ShapeDtypeStruct((t, D), F32)],
        compiler_params=_params(("arbitrary",)),
    )(pos_arr, dw3, dw3, dw3, dw_o, dwb3, dwb_o, dx1b, gwb)


def _stage2_scratch(pbs):
    n = len(pbs)
    return ([pltpu.VMEM(p.shape[1:], BF16) for p in pbs] * 2
            + [pltpu.SemaphoreType.DMA((6 * n,)), pltpu.SemaphoreType.DMA((6 * n,)), pltpu.SemaphoreType.DMA((2 * n,))])


def _stage2_copies(p_refs, r_refs, scratch):
    n = len(p_refs)
    owns, gots = scratch[:n], scratch[n:2 * n]
    send_sems, recv_sems, load_sems = scratch[2 * n:]
    x, y, c = _position()
    xn, yn = (1 - x, y, c), (x, 1 - y, c)
    loads, first, second = [], [], []
    for a, (p, r, own, got) in enumerate(zip(p_refs, r_refs, owns, gots)):
        rows = p.shape[1]
        half = -(-rows // 32) * 16
        h0, h1 = pl.ds(0, half), pl.ds(half, rows - half)

        def remote(k, src, dst, to, a=a):
            return pltpu.make_async_remote_copy(
                src_ref=src, dst_ref=dst, send_sem=send_sems.at[6 * a + k], recv_sem=recv_sems.at[6 * a + k],
                device_id=to, device_id_type=MESH)

        loads += [pltpu.make_async_copy(p.at[0, h0, :], own.at[h0, :], load_sems.at[2 * a]),
                  pltpu.make_async_copy(p.at[1, h1, :], own.at[h1, :], load_sems.at[2 * a + 1])]
        first += [remote(0, p.at[2, h0, :], got.at[h0, :], xn), remote(1, p.at[2, h1, :], got.at[h1, :], yn),
                  remote(2, p.at[1, h0, :], r.at[1, h0, :], xn), remote(3, p.at[0, h1, :], r.at[0, h1, :], yn)]
        second += [remote(4, own.at[h0, :], r.at[0, h0, :], yn), remote(5, own.at[h1, :], r.at[1, h1, :], xn)]
    return loads, first, second


def _stage2_start(p_refs, r_refs, scratch):
    loads, first, _ = _stage2_copies(p_refs, r_refs, scratch)
    for cp in loads:
        cp.start()
    for k in range(4):
        for cp in first[k::4]:
            cp.start()


def _stage2_combine(p_refs, r_refs, scratch):
    n = len(p_refs)
    loads, first, second = _stage2_copies(p_refs, r_refs, scratch)
    for a in range(n):
        for cp in loads[2 * a:2 * a + 2]:
            cp.wait()
        for cp in first[4 * a:4 * a + 2]:
            cp.wait_recv()
        own, got = scratch[a], scratch[n + a]
        own[...] = (own[...].astype(F32) + got[...].astype(F32)).astype(BF16)
        for cp in second[2 * a:2 * a + 2]:
            cp.start()


def _stage2_finish(p_refs, r_refs, scratch):
    _, first, second = _stage2_copies(p_refs, r_refs, scratch)
    for a in range(len(p_refs)):
        for cp in first[4 * a + 2:4 * a + 4] + second[2 * a:2 * a + 2]:
            cp.wait_recv()
    for cp in first + second:
        cp.wait_send()


def _finish_weights(items, pos_arr, name, nblk):
    n = len(items)
    in_specs, out_specs, out_shape, operands, wbs = [], [], [], [], []
    for g8, lead, r1, r2, w, m, v in items:
        rows, wr = g8.shape[-2], w.shape[0]
        assert rows % nblk == 0 and wr % nblk == 0 and (nblk == 1 or (rows == wr and rows % (16 * nblk) == 0))
        rb, wb = rows // nblk, wr // nblk
        if lead is not None:
            g_spec = pl.BlockSpec((None, 1, rb, D), lambda i, pos, lead=lead: (lead, 2 * pos[1] + pos[0], i, 0))
        elif g8.shape[0] == 1:
            g_spec = pl.BlockSpec((1, rb, D), lambda i, pos: (0, i, 0))
        else:
            g_spec = pl.BlockSpec((1, rb, D), lambda i, pos: (2 * pos[1] + pos[0], i, 0))
        r1_spec = pl.BlockSpec((1, rb, D), lambda i, pos: (0, i, 0))
        if w.ndim == 3:
            wblk = pl.BlockSpec((wb, 1, D), lambda i, pos: (i, 0, 0))
        else:
            wblk = pl.BlockSpec((wb, D), lambda i, pos: (i, 0))
        in_specs += [g_spec, r1_spec, pl.BlockSpec((2, rb, D), lambda i, pos: (0, i, 0)), wblk, wblk, wblk]
        out_specs += [wblk] * 4
        out_shape += [jax.ShapeDtypeStruct(w.shape, F32)] * 4
        operands += [g8, r1, r2, w, m, v]
        wbs.append(wb)

    def body(pos_ref, *refs):
        for a in range(n):
            g_ref, r1_ref, r2_ref, w_ref, m_ref, v_ref = refs[6 * a:6 * a + 6]
            g_out, d_out, m_out, v_out = refs[6 * n + 4 * a:6 * n + 4 * a + 4]
            g = g_ref[0] + r1_ref[0].astype(F32)
            for k in range(2):
                g = g + r2_ref[k].astype(F32)
            g = g[0:wbs[a], :]
            d, mn, vn = _adamw_math(_get_rows(w_ref), g, _get_rows(m_ref), _get_rows(v_ref))
            for out, val in ((g_out, g), (d_out, d), (m_out, mn), (v_out, vn)):
                _put_rows(out, val)

    return pl.pallas_call(
        body,
        name=name,
        grid_spec=pltpu.PrefetchScalarGridSpec(
            num_scalar_prefetch=1, grid=(nblk,), in_specs=in_specs, out_specs=out_specs),
        out_shape=out_shape,
        compiler_params=_params(("arbitrary",)),
    )(pos_arr, *operands)


SMALL_NAMES = ("norm1_g", "norm2_g", "norm_f_g", "b_gate", "gla_norm_g", "w_gate_up", "conv_w")
WGU_W = NQK // N_DEV
CONV_W = CW // N_DEV


def _small_adamw(sums, ws, ms, vs):
    n = len(SMALL_NAMES)

    def body(*refs):
        acc_ref = refs[0]
        w_refs, m_refs, v_refs = refs[1:1 + n], refs[1 + n:1 + 2 * n], refs[1 + 2 * n:1 + 3 * n]
        loss_ref = refs[1 + 3 * n]
        outs = refs[2 + 3 * n:]
        x, y, c = _position()
        me = 4 * x + 2 * y + c
        acc = acc_ref[...]
        loss_ref[...] = acc[3:4, NQK + DV:NQK + DV + 1]

        def my_columns(full, width):
            r = lax.broadcasted_iota(jnp.int32, (full.shape[1], width), 0)
            col = lax.broadcasted_iota(jnp.int32, (full.shape[1], width), 1)
            sel = (r == width * me + col).astype(F32)
            return _mm(full, sel, precision=HIGHEST)

        dwgu = jnp.concatenate([acc[row:row + 1, lane:lane + NQK] for row, lane in map(_wgu_slot, range(RANK))], axis=0)
        dcw = jnp.concatenate([acc[row:row + 1, lane:lane + CW] for row, lane in CONV_SLOTS], axis=0)
        grads = [acc[0:1, :], acc[1:2, :], acc[2:3, :], acc[3:4, 0:NQK], acc[3:4, NQK:NQK + DV],
                 my_columns(dwgu, WGU_W), my_columns(dcw, CONV_W)]
        for i, g in enumerate(grads):
            d, mn, vn = _adamw_math(_get_rows(w_refs[i]), g, _get_rows(m_refs[i]), _get_rows(v_refs[i]))
            for out, val in zip(outs[4 * i:4 * i + 4], (g, d, mn, vn)):
                _put_rows(out, val)

    vm = pl.BlockSpec(memory_space=pltpu.VMEM)
    out_shape = [jax.ShapeDtypeStruct((1, 1), F32)]
    for w in ws:
        out_shape += [jax.ShapeDtypeStruct(w.shape, F32)] * 4
    return pl.pallas_call(
        body,
        name="small_adamw",
        in_specs=[vm] * (1 + 3 * n),
        out_specs=[vm] * (1 + 4 * n),
        out_shape=out_shape,
        compiler_params=_params(),
    )(sums, *ws, *ms, *vs)


def kernel(x, norm1_g, w_in, w_gate_up, b_gate, gla_norm_g, conv_w, w_out, norm2_g, w_ffn_gate, w_ffn_up, w_ffn_down, norm_f_g, loss_target, m_norm1_g, m_w_in, m_w_gate_up, m_b_gate, m_gla_norm_g, m_conv_w, m_w_out, m_norm2_g, m_w_ffn_gate, m_w_ffn_up, m_w_ffn_down, m_norm_f_g, v_norm1_g, v_w_in, v_w_gate_up, v_b_gate, v_gla_norm_g, v_conv_w, v_w_out, v_norm2_g, v_w_ffn_gate, v_w_ffn_up, v_w_ffn_down, v_norm_f_g):
    xi, yi, ci = _position()
    pos_arr = jnp.stack([ci, 2 * xi + yi]).astype(jnp.int32)
    nb, s, _ = x.shape
    t = nb * s

    tr = lambda a: a[0].T
    rows_of = lambda a: a.transpose(2, 0, 1)
    conv_rows = lambda a: a.transpose(1, 0, 2)
    w_in_t, gwgu, gconv, stage = _gather_w_in(rows_of(w_in), tr(w_ffn_gate), tr(w_ffn_up), w_ffn_down[0], w_out[0],
                                              w_gate_up[0], conv_rows(conv_w))
    wgu_f = gwgu.transpose(1, 0, 2).reshape(RANK, NQK)
    conv_f = gconv.transpose(1, 2, 0, 3).reshape(CONV_K, CW)
    wgu_p = jnp.concatenate([wgu_f, jnp.zeros((A_PAD - RANK, NQK), F32)], axis=0).astype(BF16)

    x2d = x.reshape(t, D)
    tgt2d = loss_target.reshape(t, D)
    tm = 256
    tm_in = min(512, t)
    tk = min(2048, t)
    proj, z, h, gwb = _in_proj_fwd(x2d, norm1_g, w_in_t, wgu_p, b_gate, tm_in, stage)
    proj3 = proj.reshape(nb, s, PW)
    z3 = z.reshape(nb, s, NQK)
    mix3, opre3, sprev, x1, gwa = _mix_fwd(proj3, z3, gla_norm_g, conv_f, stage, x, gwb)
    mix2d = mix3.reshape(t, D)
    dx1, dx1b, adu, hb, dg2, dgf, loss_part = _ffn_fwd_bwd(
        x1.reshape(t, D), tgt2d, gwa, gwb, norm2_g, norm_f_g.reshape(1, D), tm)
    dw3, dwb3 = _dw_ffn(adu, hb, tk)
    dw3 = dw3.reshape(3, N_DEV, FF_W, D)
    dw_o, dwb_o = _tn_matmul(mix2d, dx1b, D // 2, D, tk, "dw_out", True)
    dw_o = dw_o.reshape(N_DEV, OUT_ROWS, D)
    *pb, sib_d, sib_g, sib_u, sib_o, dmix = _ffn_core_reduce(
        dw3, dwb3.reshape(3, N_DEV, FF_W, D), dw_o, dwb_o.reshape(N_DEV, OUT_ROWS, D), pos_arr, dx1b, gwb)
    g8 = [dw3, dw3, dw3, dw_o]
    leads = [0, 1, 2, None]
    tags = ("w_ffn_down", "w_ffn_gate", "w_ffn_up", "w_out")
    r1 = [sib_d, sib_g, sib_u, sib_o]
    mb = _mix_bwd(proj3, z3, sprev, opre3, dmix.reshape(nb, s, D), gla_norm_g, conv_f, wgu_p, [pb[0], pb[1], pb[3]])
    dproj3, dgng, dcw, dbg, dwgu = mb[:5]
    dproj2d = dproj3.reshape(t, PW)
    dw_in_t, r2_up = _tn_matmul(dproj2d, h, PW // 5, D, t, "dw_in", False, _stage2_rider([pb[2]]))
    r2 = [mb[5], mb[6], r2_up, mb[7]]
    g_in, r1_in, pb_in = _w_in_core_reduce(dw_in_t)
    dx, small_sums, r2_in = _in_proj_bwd(dproj2d, x2d, dx1, norm1_g, w_in_t, tm_in, pb_in,
                                         (dg2, dgf, dbg, dgng, dwgu, dcw, loss_part))

    tags = ("w_in",) + tags
    g8 = [g_in] + g8
    leads = [None] + leads
    r1 = [r1_in] + list(r1)
    r2 = [r2_in] + r2
    shard_w = (rows_of(w_in), w_ffn_down[0], tr(w_ffn_gate), tr(w_ffn_up), w_out[0])
    shard_m = (rows_of(m_w_in), m_w_ffn_down[0], tr(m_w_ffn_gate), tr(m_w_ffn_up), m_w_out[0])
    shard_v = (rows_of(v_w_in), v_w_ffn_down[0], tr(v_w_ffn_gate), tr(v_w_ffn_up), v_w_out[0])
    back = (lambda o: o.transpose(1, 2, 0), lambda o: o[None], lambda o: o.T[None], lambda o: o.T[None],
            lambda o: o[None])
    items = list(zip(g8, leads, r1, r2, shard_w, shard_m, shard_v))
    flat = list(_finish_weights(items[1:], pos_arr, "finish_ffn_out", 2))
    flat = list(_finish_weights(items[:1], pos_arr, "finish_w_in", 1)) + flat
    results = {}
    for i, (tag, to_shard) in enumerate(zip(tags, back)):
        results[tag] = [to_shard(o) for o in flat[4 * i:4 * i + 4]]

    small_w = (norm1_g, norm2_g, norm_f_g.reshape(1, D), b_gate, gla_norm_g, w_gate_up[0], conv_rows(conv_w))
    small_m = (m_norm1_g, m_norm2_g, m_norm_f_g.reshape(1, D), m_b_gate, m_gla_norm_g, m_w_gate_up[0],
               conv_rows(m_conv_w))
    small_v = (v_norm1_g, v_norm2_g, v_norm_f_g.reshape(1, D), v_b_gate, v_gla_norm_g, v_w_gate_up[0],
               conv_rows(v_conv_w))
    so = _small_adamw(small_sums, small_w, small_m, small_v)
    loss = so[0].reshape(())
    to_shape = {"norm_f_g": lambda o: o.reshape(D), "w_gate_up": lambda o: o[None],
                "conv_w": lambda o: o.transpose(1, 0, 2)}
    for i, name in enumerate(SMALL_NAMES):
        results[name] = [to_shape.get(name, lambda o: o)(o) for o in so[1 + 4 * i:5 + 4 * i]]

    names = ("norm1_g", "w_in", "w_gate_up", "b_gate", "gla_norm_g", "conv_w", "w_out", "norm2_g",
             "w_ffn_gate", "w_ffn_up", "w_ffn_down", "norm_f_g")
    outs = [loss, dx.reshape(nb, s, D)]
    for kind in range(4):
        for name in names:
            outs.append(results[name][kind])
    return tuple(outs)
```

```python
import jax
import jax.numpy as jnp
from jax import lax
from jax.experimental import pallas as pl
from jax.experimental.pallas import tpu as pltpu

F32 = jnp.float32
BF16 = jnp.bfloat16
HIGHEST = lax.Precision.HIGHEST
MESH = pl.DeviceIdType.MESH

N_DEV = 8
D = 1024
DFF = 2816
HEADS = 4
DK = 64
DV = 128
NQK = HEADS * DK
NV = HEADS * DV
RANK = 16
CHUNK = 64
CW = 512
CONV_K = 3
IN_COLS = 3088
EPS = 1e-6
INV_GATE_NORM = 1.0 / 16.0
Q_SCALE = DK ** -0.5

PW = 3200
OQ, OK_, OV, OG, OCB, OCC, OCH, OA = 0, 256, 512, 1024, 1536, 2048, 2560, 3072
A_PAD = 128

ADAM_LR = 0.001
ADAM_B1 = 0.9
ADAM_B2 = 0.999
ADAM_EPS = 1e-08
ADAM_WD = 0.01
ADAM_STEP = 10

IN_W = IN_COLS // N_DEV
IN_ROWS = 400
FF_W = DFF // N_DEV
OUT_ROWS = D // N_DEV
SLAB_IN = 0
SLAB_G = SLAB_IN + IN_ROWS
SLAB_U = SLAB_G + FF_W
SLAB_D = SLAB_U + FF_W
SLAB_O = SLAB_D + FF_W
SLAB_ROWS = SLAB_O + OUT_ROWS
D_HEAD = 128
D_TAIL = FF_W - D_HEAD
SLAB_SPLIT = SLAB_D + D_HEAD

VMEM_LIMIT = 56 * 1024 * 1024


def _params(sem=None, vmem=VMEM_LIMIT):
    return pltpu.CompilerParams(dimension_semantics=sem, vmem_limit_bytes=vmem)


def _nt(a, b):
    return lax.dot_general(a, b, (((1,), (1,)), ((), ())), preferred_element_type=F32)


def _tn(a, b, precision=None):
    return lax.dot_general(a, b, (((0,), (0,)), ((), ())), preferred_element_type=F32, precision=precision)


def _mm(a, b, precision=None):
    return jnp.dot(a, b, preferred_element_type=F32, precision=precision)


def _in_segments():
    segs = []
    for j in range(N_DEV):
        lo, hi = IN_W * j, IN_W * (j + 1)
        cuts = sorted({lo, hi} | {c for c in (OCB, OCB + RANK) if lo < c < hi})
        for a, b in zip(cuts[:-1], cuts[1:]):
            if a < OCB:
                d = a
            elif a < OCB + RANK:
                d = OA + (a - OCB)
            else:
                d = a - RANK
            segs.append((j, a - lo, b - lo, d))
    return segs


def _in_proj_fwd(x2d, g1, w_in_t, wgu_p, b_gate, tm, stage):
    t = x2d.shape[0]
    nt = t // tm
    g_rows = SLAB_ROWS - SLAB_SPLIT

    def body(x_ref, g_ref, w_ref, wgu_ref, bg_ref, stage_hbm, proj_ref, z_ref, h_ref, gwb_ref,
             send_sems, recv_sems, local_sem):
        gargs = (stage_hbm, SLAB_SPLIT, g_rows, gwb_ref, send_sems, recv_sems, local_sem)

        @pl.when(pl.program_id(0) == 0)
        def _():
            _gather_start(*gargs)

        @pl.when(pl.program_id(0) == RELAY_AT * nt // 8)
        def _():
            _gather_relay(*gargs)

        @pl.when(pl.program_id(0) == FORWARD_AT * nt // 8)
        def _():
            _gather_forward(*gargs)

        x = x_ref[...]
        r = lax.rsqrt(jnp.mean(x * x, axis=-1, keepdims=True) + EPS)
        h = ((x * r) * g_ref[...]).astype(BF16)
        h_ref[...] = h
        proj = _nt(h, w_ref[...])
        proj_ref[...] = proj
        pa = proj[:, OA:OA + A_PAD].astype(BF16)
        z_ref[...] = _mm(pa, wgu_ref[...]) + bg_ref[...]

        @pl.when(pl.program_id(0) == nt - 1)
        def _():
            _gather_finish(*gargs)

    return pl.pallas_call(
        body,
        name="in_proj_fwd",
        grid=(t // tm,),
        in_specs=[
            pl.BlockSpec((tm, D), lambda i: (i, 0)),
            pl.BlockSpec((1, D), lambda i: (0, 0)),
            pl.BlockSpec((PW, D), lambda i: (0, 0)),
            pl.BlockSpec((A_PAD, NQK), lambda i: (0, 0)),
            pl.BlockSpec((1, NQK), lambda i: (0, 0)),
            pl.BlockSpec(memory_space=pl.ANY),
        ],
        out_specs=[
            pl.BlockSpec((tm, PW), lambda i: (i, 0)),
            pl.BlockSpec((tm, NQK), lambda i: (i, 0)),
            pl.BlockSpec((tm, D), lambda i: (i, 0)),
            pl.BlockSpec(memory_space=pl.ANY),
        ],
        out_shape=[
            jax.ShapeDtypeStruct((t, PW), F32),
            jax.ShapeDtypeStruct((t, NQK), F32),
            jax.ShapeDtypeStruct((t, D), BF16),
            jax.ShapeDtypeStruct((N_DEV, g_rows, D), BF16),
        ],
        scratch_shapes=_gather_sems(),
        compiler_params=_params(("arbitrary",)),
    )(x2d, g1, w_in_t, wgu_p, b_gate, stage)


def _head_masks():
    lane = lax.broadcasted_iota(jnp.int32, (1, NQK), 1)
    return [(lane >= DK * h) & (lane < DK * (h + 1)) for h in range(HEADS)]


def _split_bf16(x, n):
    parts = []
    for _ in range(n):
        p = x.astype(BF16)
        parts.append(p)
        x = x - p.astype(F32)
    return parts


def _chunk_fwd_parts(q, k, z, tril16):
    la = (jnp.minimum(z, 0.0) - jnp.log1p(jnp.exp(-jnp.abs(z)))) * INV_GATE_NORM
    la_parts = _split_bf16(la, 3)
    bc = _mm(tril16, la_parts[0]) + _mm(tril16, la_parts[1]) + _mm(tril16, la_parts[2])
    bl = bc[CHUNK - 1:CHUNK, :]
    eb = jnp.exp(bc)
    enb = jnp.exp(-bc)
    ekl = jnp.exp(bl - bc)
    qi = (q * Q_SCALE) * eb
    ki = k * enb
    ks = k * ekl
    ones16 = jnp.ones((CHUNK, DV), BF16)
    decb = jnp.exp(_tn(la_parts[0], ones16) + _tn(la_parts[1], ones16) + _tn(la_parts[2], ones16))
    return la, eb, enb, ekl, qi, ki, ks, decb


def _stack_heads(a, masks):
    return jnp.concatenate([jnp.where(m, a, 0.0) for m in masks], axis=0)


def _merge_heads(blocks, masks):
    out = blocks[HEADS - 1]
    for h in range(HEADS - 2, -1, -1):
        out = jnp.where(masks[h], blocks[h], out)
    return out


def _causal_stack_mask():
    row = lax.broadcasted_iota(jnp.int32, (HEADS * CHUNK, CHUNK), 0)
    col = lax.broadcasted_iota(jnp.int32, (HEADS * CHUNK, CHUNK), 1)
    return (row & (CHUNK - 1)) >= col


def _conv_taps(u, uprev):
    row = lax.broadcasted_iota(jnp.int32, u.shape, 0)
    u1 = jnp.where(row < 1, pltpu.roll(uprev, 1, 0), pltpu.roll(u, 1, 0))
    u2 = jnp.where(row < 2, pltpu.roll(uprev, 2, 0), pltpu.roll(u, 2, 0))
    return u1, u2


def _mix_fwd(proj3, z3, gng, conv_w, stage, x3, gwb):
    nb, s, _ = proj3.shape
    nc = s // CHUNK
    g_rows = SLAB_SPLIT - SLAB_G

    def body(p_ref, z_ref, gng_ref, cw_ref, stage_hbm, x_ref, gwb_hbm, mix_ref, o_ref, sprev_ref, x1_ref, gwa_ref,
             s_ref, uprev_ref, wo, wsem, send_sems, recv_sems, local_sem):
        n = pl.program_id(0)
        gargs = (stage_hbm, SLAB_G, g_rows, gwa_ref, send_sems, recv_sems, local_sem)

        @pl.when(n == 0)
        def _():
            _gather_start(*gargs)
            loads = [pltpu.make_async_copy(gwb_hbm.at[j, pl.ds(D_TAIL, OUT_ROWS), :],
                                           wo.at[pl.ds(OUT_ROWS * j, OUT_ROWS), :], wsem.at[j]) for j in range(N_DEV)]
            for cp in loads:
                cp.start()
            s_ref[...] = jnp.zeros_like(s_ref)
            uprev_ref[...] = jnp.zeros_like(uprev_ref)
            for cp in loads:
                cp.wait()

        @pl.when(n == RELAY_AT * nc // 8)
        def _():
            _gather_relay(*gargs)

        @pl.when(n == FORWARD_AT * nc // 8)
        def _():
            _gather_forward(*gargs)

        r_i = lax.broadcasted_iota(jnp.int32, (CHUNK, CHUNK), 0)
        c_i = lax.broadcasted_iota(jnp.int32, (CHUNK, CHUNK), 1)
        tril16 = (r_i >= c_i).astype(BF16)
        masks = _head_masks()
        cmask = _causal_stack_mask()
        gg = gng_ref[...]
        for b in range(nb):
            q = p_ref[b, :, OQ:OQ + NQK]
            k = p_ref[b, :, OK_:OK_ + NQK]
            _, _, _, _, qi, ki, ks, decb = _chunk_fwd_parts(q, k, z_ref[b], tril16)
            qs = _stack_heads(qi, masks).astype(BF16)
            sc = jnp.where(cmask, _nt(qs, ki.astype(BF16)), 0.0).astype(BF16)
            st = s_ref[b]
            sprev_ref[b, 0] = st
            o_inter = _mm(qs, st.astype(BF16))
            v16 = p_ref[b, :, OV:OV + NV].astype(BF16)
            kv = _tn(ks.astype(BF16), v16)
            for h in range(HEADS):
                rows = slice(CHUNK * h, CHUNK * (h + 1))
                cols = slice(DV * h, DV * (h + 1))
                o = _mm(sc[rows], v16[:, cols]) + o_inter[rows]
                o_ref[b, :, cols] = o
                r = lax.rsqrt(jnp.mean(o * o, axis=-1, keepdims=True) + EPS)
                on = (o * r) * gg
                g = p_ref[b, :, OG + DV * h:OG + DV * (h + 1)]
                mix_ref[b, :, cols] = (on * (g * jax.nn.sigmoid(g))).astype(BF16)
                s_ref[b, rows, :] = decb[rows] * st[rows] + kv[rows, cols]
            u = p_ref[b, :, OCC:OCC + CW] * p_ref[b, :, OCH:OCH + CW]
            u1, u2 = _conv_taps(u, uprev_ref[b])
            yc = cw_ref[0:1, :] * u2 + cw_ref[1:2, :] * u1 + cw_ref[2:3, :] * u
            mix_ref[b, :, NV:NV + CW] = (p_ref[b, :, OCB:OCB + CW] * yc).astype(BF16)
            uprev_ref[b] = u
        mixed = _mm(jnp.concatenate([mix_ref[b] for b in range(nb)], axis=0), wo[...])
        for b in range(nb):
            x1_ref[b] = x_ref[b] + mixed[CHUNK * b:CHUNK * (b + 1)]

        @pl.when(n == nc - 1)
        def _():
            _gather_finish(*gargs)

    return pl.pallas_call(
        body,
        name="mix_fwd",
        grid=(nc,),
        in_specs=[
            pl.BlockSpec((nb, CHUNK, PW), lambda n: (0, n, 0)),
            pl.BlockSpec((nb, CHUNK, NQK), lambda n: (0, n, 0)),
            pl.BlockSpec((1, DV), lambda n: (0, 0)),
            pl.BlockSpec((CONV_K, CW), lambda n: (0, 0)),
            pl.BlockSpec(memory_space=pl.ANY),
            pl.BlockSpec((nb, CHUNK, D), lambda n: (0, n, 0)),
            pl.BlockSpec(memory_space=pl.ANY),
        ],
        out_specs=[
            pl.BlockSpec((nb, CHUNK, D), lambda n: (0, n, 0)),
            pl.BlockSpec((nb, CHUNK, NV), lambda n: (0, n, 0)),
            pl.BlockSpec((nb, 1, NQK, DV), lambda n: (0, n, 0, 0)),
            pl.BlockSpec((nb, CHUNK, D), lambda n: (0, n, 0)),
            pl.BlockSpec(memory_space=pl.ANY),
        ],
        out_shape=[
            jax.ShapeDtypeStruct((nb, s, D), BF16),
            jax.ShapeDtypeStruct((nb, s, NV), F32),
            jax.ShapeDtypeStruct((nb, nc, NQK, DV), F32),
            jax.ShapeDtypeStruct((nb, s, D), F32),
            jax.ShapeDtypeStruct((N_DEV, g_rows, D), BF16),
        ],
        scratch_shapes=[pltpu.VMEM((nb, NQK, DV), F32), pltpu.VMEM((nb, CHUNK, CW), F32),
                        pltpu.VMEM((D, D), BF16), pltpu.SemaphoreType.DMA((N_DEV,))] + _gather_sems(),
        compiler_params=_params(("arbitrary",)),
    )(proj3, z3, gng, conv_w, stage, x3, gwb)


def _ffn_fwd_bwd(x1_2d, tgt2d, gwa, gwb, g2, gf, tm):
    t = x1_2d.shape[0]

    def body(x1_ref, tgt_ref, g2_ref, gf_ref, gwa_hbm, gwb_hbm,
             dx1_ref, dx1b_ref, adu_ref, hb_ref, dg2_ref, dgf_ref, loss_ref,
             wg, wu, wd, wsem):
        i = pl.program_id(0)

        def weight_copies(n, dst, src, off, rows, at=0):
            return [pltpu.make_async_copy(src.at[j, pl.ds(off, rows), :], dst.at[pl.ds(FF_W * j + at, rows), :],
                                          wsem.at[N_DEV * n + j]) for j in range(N_DEV)]

        loads = (weight_copies(0, wg, gwa_hbm, 0, FF_W), weight_copies(1, wu, gwa_hbm, FF_W, FF_W),
                 weight_copies(2, wd, gwa_hbm, 2 * FF_W, D_HEAD), weight_copies(3, wd, gwb_hbm, 0, D_TAIL, D_HEAD))

        @pl.when(i == 0)
        def _():
            for group in loads:
                for cp in group:
                    cp.start()
            dg2_ref[...] = jnp.zeros_like(dg2_ref)
            dgf_ref[...] = jnp.zeros_like(dgf_ref)
            loss_ref[...] = jnp.zeros_like(loss_ref)
            for group in loads:
                for cp in group:
                    cp.wait()

        g2v = g2_ref[...]
        gfv = gf_ref[...]
        x1 = x1_ref[...]
        r2 = lax.rsqrt(jnp.mean(x1 * x1, axis=-1, keepdims=True) + EPS)
        n2 = x1 * r2
        h2 = (n2 * g2v).astype(BF16)
        hb_ref[1] = h2
        gate = _nt(h2, wg[...])
        up = _nt(h2, wu[...])
        sg = jax.nn.sigmoid(gate)
        sil = gate * sg
        act = (sil * up).astype(BF16)
        adu_ref[0] = act
        x2 = x1 + _mm(act, wd[...])
        rf = lax.rsqrt(jnp.mean(x2 * x2, axis=-1, keepdims=True) + EPS)
        nf = x2 * rf
        err = nf * gfv - tgt_ref[...]
        loss_ref[...] += 0.5 * jnp.sum(jnp.mean(err * err, axis=-1, keepdims=True))
        dy = err * (1.0 / D)
        dgf_ref[...] += jnp.sum(dy * nf, axis=0, keepdims=True)
        dnf = dy * gfv
        dx2 = rf * (dnf - nf * jnp.mean(dnf * nf, axis=-1, keepdims=True))
        dx2b = dx2.astype(BF16)
        hb_ref[0] = dx2b
        dact = _nt(dx2b, wd[...])
        dup = (dact * sil).astype(BF16)
        dgate = ((dact * up) * (sg * (1.0 + gate * (1.0 - sg)))).astype(BF16)
        adu_ref[2] = dup
        adu_ref[1] = dgate
        dh2 = _mm(dgate, wg[...]) + _mm(dup, wu[...])
        dg2_ref[...] += jnp.sum(dh2 * n2, axis=0, keepdims=True)
        dn2 = dh2 * g2v
        dx1 = dx2 + r2 * (dn2 - n2 * jnp.mean(dn2 * n2, axis=-1, keepdims=True))
        dx1_ref[...] = dx1
        dx1b_ref[...] = dx1.astype(BF16)

    tile = lambda w: pl.BlockSpec((tm, w), lambda i: (i, 0))
    vec = pl.BlockSpec((1, D), lambda i: (0, 0))
    hbm = pl.BlockSpec(memory_space=pl.ANY)
    return pl.pallas_call(
        body,
        name="ffn_fwd_bwd",
        grid=(t // tm,),
        in_specs=[tile(D), tile(D), vec, vec, hbm, hbm],
        out_specs=[tile(D), tile(D), pl.BlockSpec((3, tm, DFF), lambda i: (0, i, 0)),
                   pl.BlockSpec((2, tm, D), lambda i: (0, i, 0)), vec, vec,
                   pl.BlockSpec((1, 128), lambda i: (0, 0))],
        out_shape=[
            jax.ShapeDtypeStruct((t, D), F32),
            jax.ShapeDtypeStruct((t, D), BF16),
            jax.ShapeDtypeStruct((3, t, DFF), BF16),
            jax.ShapeDtypeStruct((2, t, D), BF16),
            jax.ShapeDtypeStruct((1, D), F32),
            jax.ShapeDtypeStruct((1, D), F32),
            jax.ShapeDtypeStruct((1, 128), F32),
        ],
        scratch_shapes=[pltpu.VMEM((DFF, D), BF16), pltpu.VMEM((DFF, D), BF16), pltpu.VMEM((DFF, D), BF16),
                        pltpu.SemaphoreType.DMA((4 * N_DEV,))],
        compiler_params=_params(("arbitrary",)),
    )(x1_2d, tgt2d, g2, gf, gwa, gwb)


def _stage2_rider(pbs):
    return dict(inputs=list(pbs), out_shape=[jax.ShapeDtypeStruct((2,) + p.shape[1:], BF16) for p in pbs],
                scratch=_stage2_scratch(pbs))


def _tn_matmul(a, b, bm, bn, tk, name, with_bf16, rider=None):
    t, m = a.shape
    n = b.shape[1]
    nk = t // tk
    nout = 2 if with_bf16 else 1
    grid = (m // bm, n // bn, nk)
    steps = grid[0] * grid[1] * nk
    r_in = [] if rider is None else rider["inputs"]
    r_out = [] if rider is None else rider["out_shape"]

    def body(a_ref, b_ref, *rest):
        ins, outs = rest[:len(r_in)], rest[len(r_in):len(r_in) + nout]
        r_outs, scratch = rest[len(r_in) + nout:len(r_in) + nout + len(r_out)], rest[len(r_in) + nout + len(r_out):]
        o_ref = outs[0]
        i, j, k = pl.program_id(0), pl.program_id(1), pl.program_id(2)
        step = (i * grid[1] + j) * nk + k
        if rider is not None:
            @pl.when(step == 0)
            def _():
                _stage2_start(ins, r_outs, scratch)

            @pl.when(step == RELAY_AT * steps // 8)
            def _():
                _stage2_combine(ins, r_outs, scratch)

        @pl.when(k == 0)
        def _():
            o_ref[...] = jnp.zeros_like(o_ref)

        o_ref[...] += _tn(a_ref[...].astype(BF16), b_ref[...].astype(BF16))
        if with_bf16:
            @pl.when(k == nk - 1)
            def _():
                outs[1][...] = o_ref[...].astype(BF16)
        if rider is not None:
            @pl.when(step == steps - 1)
            def _():
                _stage2_finish(ins, r_outs, scratch)

    out_blk = pl.BlockSpec((bm, bn), lambda i, j, k: (i, j))
    hbm = pl.BlockSpec(memory_space=pl.ANY)
    out_shape = [jax.ShapeDtypeStruct((m, n), F32)] + ([jax.ShapeDtypeStruct((m, n), BF16)] if with_bf16 else [])
    res = pl.pallas_call(
        body,
        name=name,
        grid=grid,
        in_specs=[pl.BlockSpec((tk, bm), lambda i, j, k: (k, i)), pl.BlockSpec((tk, bn), lambda i, j, k: (k, j))]
        + [hbm] * len(r_in),
        out_specs=[out_blk] * nout + [hbm] * len(r_out),
        out_shape=out_shape + list(r_out),
        scratch_shapes=[] if rider is None else rider["scratch"],
        compiler_params=_params(("parallel", "parallel", "arbitrary") if rider is None
                                else ("arbitrary", "arbitrary", "arbitrary")),
    )(a, b, *r_in)
    return res[0] if len(res) == 1 else res


def _dw_ffn(adu, hb, tk):
    _, t, _ = adu.shape
    bm = DFF // 2
    nk = t // tk

    def body(a_ref, b_ref, o_ref, ob_ref):
        k = pl.program_id(2)

        @pl.when(k == 0)
        def _():
            o_ref[...] = jnp.zeros_like(o_ref)

        o_ref[...] += _tn(a_ref[...], b_ref[...])

        @pl.when(k == nk - 1)
        def _():
            ob_ref[...] = o_ref[...].astype(BF16)

    out_blk = pl.BlockSpec((None, bm, D), lambda p, i, k: (p, i, 0))
    return pl.pallas_call(
        body,
        name="dw_ffn",
        grid=(3, DFF // bm, nk),
        in_specs=[pl.BlockSpec((None, tk, bm), lambda p, i, k: (p, k, i)),
                  pl.BlockSpec((None, tk, D), lambda p, i, k: (jnp.minimum(p, 1), k, 0))],
        out_specs=[out_blk, out_blk],
        out_shape=[jax.ShapeDtypeStruct((3, DFF, D), F32), jax.ShapeDtypeStruct((3, DFF, D), BF16)],
        compiler_params=_params(("arbitrary", "arbitrary", "arbitrary")),
    )(adu, hb)


def _mix_bwd(proj3, z3, sprev, opre3, dmix3, gng, conv_w, wgu_p, pbs):
    nb, s, _ = proj3.shape
    nc = s // CHUNK
    na = len(pbs)

    def body(*refs):
        (p_ref, pprev_ref, z_ref, sp_ref, o_ref, dm_ref, gng_ref, cw_ref, wgu_ref) = refs[:9]
        pb_refs = refs[9:9 + na]
        (dproj_ref, dgng_ref, dcw_ref, dbg_ref, dwgu_ref) = refs[9 + na:14 + na]
        r2_refs = refs[14 + na:14 + 2 * na]
        ds_ref, dycn_ref = refs[14 + 2 * na:16 + 2 * na]
        stage2 = (pb_refs, r2_refs, refs[16 + 2 * na:])
        step = pl.program_id(0)
        n = nc - 1 - step

        @pl.when(step == 0)
        def _():
            _stage2_start(*stage2)
            ds_ref[...] = jnp.zeros_like(ds_ref)
            dycn_ref[...] = jnp.zeros_like(dycn_ref)
            dgng_ref[...] = jnp.zeros_like(dgng_ref)
            dcw_ref[...] = jnp.zeros_like(dcw_ref)
            dbg_ref[...] = jnp.zeros_like(dbg_ref)
            dwgu_ref[...] = jnp.zeros_like(dwgu_ref)

        @pl.when(step == RELAY_AT * nc // 8)
        def _():
            _stage2_combine(*stage2)

        r_i = lax.broadcasted_iota(jnp.int32, (CHUNK, CHUNK), 0)
        c_i = lax.broadcasted_iota(jnp.int32, (CHUNK, CHUNK), 1)
        tril16 = (r_i >= c_i).astype(BF16)
        triu16 = (r_i <= c_i).astype(BF16)
        causal = r_i >= c_i
        masks = _head_masks()
        cmask = _causal_stack_mask()
        gg = gng_ref[...]
        last_row = lax.broadcasted_iota(jnp.int32, (CHUNK, NQK), 0) == CHUNK - 1
        ones_r = jnp.ones((16, DV), BF16)
        has_prev = (n > 0).astype(F32)
        for b in range(nb):
            q = p_ref[b, :, OQ:OQ + NQK]
            k = p_ref[b, :, OK_:OK_ + NQK]
            z = z_ref[b]
            _, eb, enb, ekl, qi, ki, ks, decb = _chunk_fwd_parts(q, k, z, tril16)
            qi16 = qi.astype(BF16)
            ki16 = ki.astype(BF16)
            qs = _stack_heads(qi, masks).astype(BF16)
            sc = jnp.where(cmask, _nt(qs, ki16), 0.0).astype(BF16)
            st = sp_ref[b, 0]
            st16 = st.astype(BF16)
            dsn = ds_ref[b]
            dsn16 = dsn.astype(BF16)
            v16 = p_ref[b, :, OV:OV + NV].astype(BF16)
            do16 = []
            dgng = jnp.zeros((1, DV), F32)
            for h in range(HEADS):
                cols = slice(DV * h, DV * (h + 1))
                o = o_ref[b, :, cols]
                r = lax.rsqrt(jnp.mean(o * o, axis=-1, keepdims=True) + EPS)
                nh = o * r
                g = p_ref[b, :, OG + DV * h:OG + DV * (h + 1)]
                sg = jax.nn.sigmoid(g)
                dog = dm_ref[b, :, cols]
                dproj_ref[b, :, OG + DV * h:OG + DV * (h + 1)] = (
                    (dog * (nh * gg)) * (sg * (1.0 + g * (1.0 - sg)))).astype(BF16)
                don = dog * (g * sg)
                dgng = dgng + jnp.sum(don * nh, axis=0, keepdims=True)
                dn = don * gg
                do = r * (dn - nh * jnp.mean(dn * nh, axis=-1, keepdims=True))
                do16.append(do.astype(BF16))
            dgng_ref[...] += dgng
            do_rows = jnp.concatenate(do16, axis=0)
            v_rows = jnp.concatenate([v16[:, DV * h:DV * (h + 1)] for h in range(HEADS)], axis=0)
            dp16 = [jnp.where(causal, _nt(do16[h], v16[:, DV * h:DV * (h + 1)]), 0.0).astype(BF16)
                    for h in range(HEADS)]
            ks_dsn = _mm(_stack_heads(ks, masks).astype(BF16), dsn16)
            do_st = _nt(do_rows, st16)
            v_dsn = _nt(v_rows, dsn16)
            dp_ki = _mm(jnp.concatenate(dp16, axis=0), ki16)
            q_do = _tn(qi16, jnp.concatenate(do16, axis=1))
            dki_h = []
            for h in range(HEADS):
                rows = slice(CHUNK * h, CHUNK * (h + 1))
                cols = slice(DV * h, DV * (h + 1))
                dv = _tn(sc[rows], do16[h]) + ks_dsn[rows]
                dproj_ref[b, :, OV + DV * h:OV + DV * (h + 1)] = dv.astype(BF16)
                dki_h.append(_tn(dp16[h], qi16))
                ds_ref[b, rows, :] = decb[rows] * dsn[rows] + q_do[rows, cols]
            blocks = lambda a: [a[CHUNK * h:CHUNK * (h + 1)] for h in range(HEADS)]
            dqi = _merge_heads(blocks(dp_ki + do_st), masks)
            dki = _merge_heads(dki_h, masks)
            dks = _merge_heads(blocks(v_dsn), masks)
            dproj_ref[b, :, OQ:OQ + NQK] = (dqi * (Q_SCALE * eb)).astype(BF16)
            dproj_ref[b, :, OK_:OK_ + NQK] = (dki * enb + dks * ekl).astype(BF16)
            dks_ks = dks * ks
            db = dqi * qi - dki * ki - dks_ks
            sd = _split_bf16(dsn * st * decb, 2)
            dbl = jnp.sum(dks_ks, axis=0, keepdims=True) + (_nt(ones_r, sd[0]) + _nt(ones_r, sd[1]))[0:1, :]
            db = db + jnp.where(last_row, dbl, 0.0)
            db_parts = _split_bf16(db, 3)
            dla = _mm(triu16, db_parts[0]) + _mm(triu16, db_parts[1]) + _mm(triu16, db_parts[2])
            dz = (dla * INV_GATE_NORM) * (1.0 / (1.0 + jnp.exp(z)))
            dbg_ref[...] += jnp.sum(dz, axis=0, keepdims=True)
            dz16 = dz.astype(BF16)
            pa16 = p_ref[b, :, OA:OA + A_PAD].astype(BF16)
            dwgu_ref[...] += _tn(pa16, dz16)
            dproj_ref[b, :, OA:OA + A_PAD] = _nt(dz16, wgu_ref[...]).astype(BF16)
            cb = p_ref[b, :, OCB:OCB + CW]
            cc = p_ref[b, :, OCC:OCC + CW]
            ch = p_ref[b, :, OCH:OCH + CW]
            u = cc * ch
            uprev = (pprev_ref[b, :, 0:CW] * pprev_ref[b, :, CW:2 * CW]) * has_prev
            u1, u2 = _conv_taps(u, uprev)
            w0 = cw_ref[0:1, :]
            w1 = cw_ref[1:2, :]
            w2 = cw_ref[2:3, :]
            yc = w0 * u2 + w1 * u1 + w2 * u
            doc = dm_ref[b, :, NV:NV + CW]
            dproj_ref[b, :, OCB:OCB + CW] = (doc * yc).astype(BF16)
            dyc = doc * cb
            dycn = dycn_ref[b]
            row = lax.broadcasted_iota(jnp.int32, dyc.shape, 0)
            d1 = jnp.where(row >= CHUNK - 1, pltpu.roll(dycn, CHUNK - 1, 0), pltpu.roll(dyc, CHUNK - 1, 0))
            d2 = jnp.where(row >= CHUNK - 2, pltpu.roll(dycn, CHUNK - 2, 0), pltpu.roll(dyc, CHUNK - 2, 0))
            du = w2 * dyc + w1 * d1 + w0 * d2
            dproj_ref[b, :, OCC:OCC + CW] = (du * ch).astype(BF16)
            dproj_ref[b, :, OCH:OCH + CW] = (du * cc).astype(BF16)
            dcw_ref[0:1, :] += jnp.sum(dyc * u2, axis=0, keepdims=True)
            dcw_ref[1:2, :] += jnp.sum(dyc * u1, axis=0, keepdims=True)
            dcw_ref[2:3, :] += jnp.sum(dyc * u, axis=0, keepdims=True)
            dycn_ref[b] = dyc

        @pl.when(step == nc - 1)
        def _():
            _stage2_finish(*stage2)

    rev =lambda w: pl.BlockSpec((nb, CHUNK, w), lambda i: (0, nc - 1 - i, 0))
    const = lambda r, c: pl.BlockSpec((r, c), lambda i: (0, 0))
    hbm = pl.BlockSpec(memory_space=pl.ANY)
    return pl.pallas_call(
        body,
        name="mix_bwd",
        grid=(nc,),
        in_specs=[
            rev(PW),
            pl.BlockSpec((nb, CHUNK, 2 * CW), lambda i: (0, jnp.maximum(nc - 2 - i, 0), OCC // (2 * CW))),
            rev(NQK),
            pl.BlockSpec((nb, 1, NQK, DV), lambda i: (0, nc - 1 - i, 0, 0)),
            rev(NV),
            rev(D),
            const(1, DV),
            const(CONV_K, CW),
            const(A_PAD, NQK),
        ] + [hbm] * na,
        out_specs=[rev(PW), const(1, DV), const(8, CW), const(1, NQK), const(A_PAD, NQK)] + [hbm] * na,
        out_shape=[
            jax.ShapeDtypeStruct((nb, s, PW), BF16),
            jax.ShapeDtypeStruct((1, DV), F32),
            jax.ShapeDtypeStruct((8, CW), F32),
            jax.ShapeDtypeStruct((1, NQK), F32),
            jax.ShapeDtypeStruct((A_PAD, NQK), F32),
        ] + [jax.ShapeDtypeStruct((2,) + p.shape[1:], BF16) for p in pbs],
        scratch_shapes=[pltpu.VMEM((nb, NQK, DV), F32), pltpu.VMEM((nb, CHUNK, CW), F32)] + _stage2_scratch(pbs),
        compiler_params=_params(("arbitrary",)),
    )(proj3, proj3, z3, sprev, opre3, dmix3, gng, conv_w, wgu_p, *pbs)


SMALL_PACK_ROWS = 16


def _wgu_slot(r):
    return 4 + r // 4, NQK * (r % 4)


CONV_SLOTS = ((8, 0), (8, CW), (9, 0))


def _in_proj_bwd(dproj2d, x2d, dx1, g1, w_in_t, tm, pb, small_parts):
    t = x2d.shape[0]
    nt = t // tm

    def body(dp_ref, x_ref, dx1_ref, g_ref, w_ref, pb_ref, dg2, dgf, dbg, dgng, dwgu, dcw, lp,
             dx_ref, sums_ref, r2_ref, dg1_acc, pack, gbuf, pack1, gbuf1, ssend, srecv, ssend1, srecv1, *scratch2):
        stage2 = ([pb_ref], [r2_ref], scratch2)
        x, y, c = _position()
        me = 4 * x + 2 * y + c
        flips = [(k >> 2, (k >> 1) & 1, k & 1) for k in range(1, N_DEV)]
        peers = [(x ^ fx, y ^ fy, c ^ fc) for fx, fy, fc in flips]

        def small_copies(src, dst, send, recv, arrivals):
            return [pltpu.make_async_remote_copy(
                src_ref=src, dst_ref=dst.at[4 * px + 2 * py + pc if arrivals else me],
                send_sem=send.at[k], recv_sem=recv.at[k], device_id=(px, py, pc), device_id_type=MESH)
                for k, (px, py, pc) in enumerate(peers)]

        @pl.when(pl.program_id(0) == 0)
        def _():
            _stage2_start(*stage2)
            dg1_acc[...] = jnp.zeros_like(dg1_acc)
            pack[...] = jnp.zeros_like(pack)
            pack[1:2, :] = dg2[...]
            pack[2:3, :] = dgf[...]
            pack[3:4, 0:NQK] = dbg[...]
            pack[3:4, NQK:NQK + DV] = dgng[...]
            pack[3:4, NQK + DV:NQK + 2 * DV] = lp[...]
            for r in range(RANK):
                row, lane = _wgu_slot(r)
                pack[row:row + 1, lane:lane + NQK] = dwgu[r:r + 1, :]
            for r, (row, lane) in enumerate(CONV_SLOTS):
                pack[row:row + 1, lane:lane + CW] = dcw[r:r + 1, :]
            for cp in small_copies(pack, gbuf, ssend, srecv, False):
                cp.start()
            gbuf[me] = pack[...]

        @pl.when(pl.program_id(0) == RELAY_AT * nt // 8)
        def _():
            _stage2_combine(*stage2)

        xv = x_ref[...]
        r = lax.rsqrt(jnp.mean(xv * xv, axis=-1, keepdims=True) + EPS)
        n1 = xv * r
        dh = _mm(dp_ref[...], w_ref[...])
        dg1_acc[...] += jnp.sum(dh * n1, axis=0, keepdims=True)
        dn = dh * g_ref[...]
        dx_ref[...] = dx1_ref[...] + r * (dn - n1 * jnp.mean(dn * n1, axis=-1, keepdims=True))

        @pl.when(pl.program_id(0) == nt - 1)
        def _():
            pack1[...] = jnp.zeros_like(pack1)
            pack1[0:1, :] = dg1_acc[...]
            for cp in small_copies(pack1, gbuf1, ssend1, srecv1, False):
                cp.start()
            gbuf1[me] = pack1[...]
            _stage2_finish(*stage2)
            for src, dst, send, recv in ((pack, gbuf, ssend, srecv), (pack1, gbuf1, ssend1, srecv1)):
                for cp in small_copies(src, dst, send, recv, True):
                    cp.wait_recv()
                    cp.wait_send()
            acc = gbuf[0]
            acc1 = gbuf1[0]
            for d in range(1, N_DEV):
                acc = acc + gbuf[d]
                acc1 = acc1 + gbuf1[d]
            sums_ref[...] = acc
            sums_ref[0:1, :] = acc1[0:1, :]

    tile = lambda w: pl.BlockSpec((tm, w), lambda i: (i, 0))
    vec = pl.BlockSpec((1, D), lambda i: (0, 0))
    hbm = pl.BlockSpec(memory_space=pl.ANY)
    whole = lambda a: pl.BlockSpec(a.shape, lambda i: (0,) * a.ndim)
    return pl.pallas_call(
        body,
        name="in_proj_bwd",
        grid=(nt,),
        in_specs=[tile(PW), tile(D), tile(D), vec, pl.BlockSpec((PW, D), lambda i: (0, 0)), hbm]
        + [whole(a) for a in small_parts],
        out_specs=[tile(D), pl.BlockSpec((SMALL_PACK_ROWS, D), lambda i: (0, 0)), hbm],
        out_shape=[jax.ShapeDtypeStruct((t, D), F32), jax.ShapeDtypeStruct((SMALL_PACK_ROWS, D), F32),
                   jax.ShapeDtypeStruct((2,) + pb.shape[1:], BF16)],
        scratch_shapes=[pltpu.VMEM((1, D), F32),
                        pltpu.VMEM((SMALL_PACK_ROWS, D), F32), pltpu.VMEM((N_DEV, SMALL_PACK_ROWS, D), F32),
                        pltpu.VMEM((8, D), F32), pltpu.VMEM((N_DEV, 8, D), F32),
                        pltpu.SemaphoreType.DMA((7,)), pltpu.SemaphoreType.DMA((7,)),
                        pltpu.SemaphoreType.DMA((7,)), pltpu.SemaphoreType.DMA((7,))] + _stage2_scratch([pb]),
        compiler_params=_params(("arbitrary",)),
    )(dproj2d, x2d, dx1, g1, w_in_t, pb, *small_parts)


def _get_rows(ref):
    return ref[:, 0, :] if len(ref.shape) == 3 else ref[...]


def _put_rows(ref, val):
    if len(ref.shape) == 3:
        ref[:, 0, :] = val
    else:
        ref[...] = val


def _adamw_math(w, g, m, v):
    m = ADAM_B1 * m + (1.0 - ADAM_B1) * g
    v = ADAM_B2 * v + (1.0 - ADAM_B2) * (g * g)
    m_hat = m / (1.0 - ADAM_B1 ** ADAM_STEP)
    v_hat = v / (1.0 - ADAM_B2 ** ADAM_STEP)
    delta = -ADAM_LR * (m_hat / (jnp.sqrt(v_hat) + ADAM_EPS) + ADAM_WD * w)
    return delta, m, v


def _position():
    return lax.axis_index("x"), lax.axis_index("y"), lax.axis_index("c")


GATHER_PARTS = 2
GATHER_SEMS = 7 * GATHER_PARTS
RELAY_AT = 4
FORWARD_AT = 7


def _gather_copies(stage, lo, rows, gx, send_sems, recv_sems, local_sem):
    x, y, c = _position()
    me = (x, y, c)
    sibling = (x, y, 1 - c)
    chips = [(1 - x, y), (x, 1 - y), (1 - x, 1 - y)]
    part = -(-rows // (16 * GATHER_PARTS)) * 16
    bounds = [(p * part, min(part, rows - p * part)) for p in range(GATHER_PARTS)]

    def blk(px, py, pc, off, n):
        return gx.at[4 * px + 2 * py + pc, pl.ds(off, n), :]

    mine = pltpu.make_async_copy(stage.at[pl.ds(lo, rows), :], gx.at[4 * x + 2 * y + c], local_sem)
    parts = []
    for p, (off, n) in enumerate(bounds):
        def copy(k, block, to, from_stage=False, p=p, off=off, n=n):
            return pltpu.make_async_remote_copy(
                src_ref=stage.at[pl.ds(lo + off, n), :] if from_stage else blk(*block, off, n),
                dst_ref=blk(*block, off, n), send_sem=send_sems.at[7 * p + k], recv_sem=recv_sems.at[7 * p + k],
                device_id=to, device_id_type=MESH)

        first = [copy(0, me, sibling, True)] + [copy(1 + j, me, (*chips[j], c), True) for j in range(2)]
        relay = copy(3, (*chips[p], c), (*chips[1 - p], c))
        passed = [copy(4 + j, (*chip, c), sibling) for j, chip in enumerate(chips)]
        arrivals = ([copy(0, sibling, me)] + [copy(1 + j, (*chip, c), me) for j, chip in enumerate(chips)]
                    + [copy(4 + j, (*chip, 1 - c), me) for j, chip in enumerate(chips)])
        parts.append((first, relay, passed, arrivals))
    return mine, parts


def _gather_start(*args):
    mine, parts = _gather_copies(*args)
    mine.start()
    for first, _, _, _ in parts:
        first[0].start()
    for p, q in ((0, 0), (1, 1), (0, 1), (1, 0)):
        parts[p][0][1 + q].start()


def _gather_relay(*args):
    _, parts = _gather_copies(*args)
    for p, (_, relay, passed, arrivals) in enumerate(parts):
        arrivals[1 + p].wait_recv()
        relay.start()
        passed[p].start()


def _gather_forward(*args):
    _, parts = _gather_copies(*args)
    for p, j in ((0, 1), (1, 0), (0, 2), (1, 2)):
        _, _, passed, arrivals = parts[p]
        arrivals[1 + j].wait_recv()
        passed[j].start()


def _gather_finish(*args):
    mine, parts = _gather_copies(*args)
    for first, relay, passed, arrivals in parts:
        arrivals[0].wait_recv()
        for j in range(3):
            arrivals[4 + j].wait_recv()
        for cp in first + [relay] + passed:
            cp.wait_send()
    mine.wait()


def _gather_sems():
    return [pltpu.SemaphoreType.DMA((GATHER_SEMS,)), pltpu.SemaphoreType.DMA((GATHER_SEMS,)), pltpu.SemaphoreType.DMA]


def _gather_w_in(w_it, w_gt, w_ut, w_d, w_o, wgu_s, conv_s):
    def body(wi_ref, wg_hbm, wu_hbm, wd_hbm, wo_hbm, wgu_ref, conv_ref, w_ref, gwgu_ref, gconv_ref, stage,
             buf, wf, wof, send_sems, recv_sems, local_sem, ssend, srecv, load_sems):
        x, y, c = _position()
        me = 4 * x + 2 * y + c
        loads = [pltpu.make_async_copy(src, dst, load_sems.at[n]) for n, (src, dst) in enumerate(
            ((wg_hbm, wf.at[0]), (wu_hbm, wf.at[1]), (wd_hbm, wf.at[2]), (wo_hbm, wof)))]
        for cp in loads:
            cp.start()
        stage[SLAB_IN:SLAB_IN + IN_W, :] = wi_ref[:, 0, :].astype(BF16)
        stage[SLAB_IN + IN_W:SLAB_G, :] = jnp.zeros((IN_ROWS - IN_W, D), BF16)
        args = (stage, SLAB_IN, IN_ROWS, buf, send_sems, recv_sems, local_sem)
        _gather_start(*args)
        for n, lo in enumerate((SLAB_G, SLAB_U, SLAB_D)):
            loads[n].wait()
            stage[lo:lo + FF_W, :] = wf[n].astype(BF16)
        loads[3].wait()
        stage[SLAB_O:SLAB_ROWS, :] = wof[...].astype(BF16)
        flips = [(k >> 2, (k >> 1) & 1, k & 1) for k in range(1, N_DEV)]
        peers = [(x ^ fx, y ^ fy, c ^ fc) for fx, fy, fc in flips]

        def small(k, block_id, to):
            return [pltpu.make_async_remote_copy(
                src_ref=s, dst_ref=g.at[block_id], send_sem=ssend.at[2 * k + n], recv_sem=srecv.at[2 * k + n],
                device_id=to, device_id_type=MESH)
                for n, (s, g) in enumerate(((wgu_ref, gwgu_ref), (conv_ref, gconv_ref)))]

        gwgu_ref[me] = wgu_ref[...]
        gconv_ref[me] = conv_ref[...]
        for k, peer in enumerate(peers):
            for cp in small(k, me, peer):
                cp.start()
        w_ref[IN_COLS:PW, :] = jnp.zeros((PW - IN_COLS, D), BF16)
        _gather_relay(*args)
        _gather_forward(*args)
        _gather_finish(*args)
        for k, (px, py, pc) in enumerate(peers):
            for cp in small(k, 4 * px + 2 * py + pc, (px, py, pc)):
                cp.wait_recv()
                cp.wait_send()
        for j, lo, hi, d in _in_segments():
            w_ref[d:d + hi - lo, :] = buf[j, lo:hi, :]

    vm = pl.BlockSpec(memory_space=pltpu.VMEM)
    return pl.pallas_call(
        body,
        name="gather_w_in",
        in_specs=[vm] + [pl.BlockSpec(memory_space=pl.ANY)] * 4 + [vm] * 2,
        out_specs=[vm] * 4,
        out_shape=[jax.ShapeDtypeStruct((PW, D), BF16),
                   jax.ShapeDtypeStruct((N_DEV,) + wgu_s.shape, F32),
                   jax.ShapeDtypeStruct((N_DEV,) + conv_s.shape, F32),
                   jax.ShapeDtypeStruct((SLAB_ROWS, D), BF16)],
        scratch_shapes=[pltpu.VMEM((N_DEV, IN_ROWS, D), BF16), pltpu.VMEM((3, FF_W, D), F32),
                        pltpu.VMEM((OUT_ROWS, D), F32)] + _gather_sems()
        + [pltpu.SemaphoreType.DMA((14,)), pltpu.SemaphoreType.DMA((14,)), pltpu.SemaphoreType.DMA((4,))],
        compiler_params=_params(),
    )(w_it, w_gt, w_ut, w_d, w_o, wgu_s, conv_s)


def _w_in_core_reduce(dw_t):
    def body(d_ref, own_ref, sib_ref, pb_ref, g, gb, r1, send_sems, recv_sems):
        x, y, c = _position()
        chip = 2 * x + y
        for j in range(N_DEV):
            g[j, IN_W:IN_ROWS, :] = jnp.zeros((IN_ROWS - IN_W, D), F32)
        for j, lo, hi, d in _in_segments():
            g[j, lo:hi, :] = d_ref[d:d + hi - lo, :]
        for j in range(N_DEV):
            gb[j] = g[j].astype(BF16)
        copies = _stage1_copies(gb, r1, send_sems, recv_sems)
        for cp in copies:
            cp.start()
        own_ref[0] = g[2 * chip + c]
        for cp in copies:
            cp.wait_recv()
        sib_ref[0] = r1[chip]
        for k in range(1, 4):
            t = chip ^ k
            pb_ref[k - 1] = (g[2 * t + c] + r1[t].astype(F32)).astype(BF16)
        for cp in copies:
            cp.wait_send()

    vm = pl.BlockSpec(memory_space=pltpu.VMEM)
    return pl.pallas_call(
        body,
        name="w_in_core_reduce",
        in_specs=[vm],
        out_specs=[vm, vm, vm],
        out_shape=[jax.ShapeDtypeStruct((1, IN_ROWS, D), F32), jax.ShapeDtypeStruct((1, IN_ROWS, D), BF16),
                   jax.ShapeDtypeStruct((3, IN_ROWS, D), BF16)],
        scratch_shapes=[pltpu.VMEM((N_DEV, IN_ROWS, D), F32), pltpu.VMEM((N_DEV, IN_ROWS, D), BF16),
                        pltpu.VMEM((4, IN_ROWS, D), BF16), pltpu.SemaphoreType.DMA((4,)),
                        pltpu.SemaphoreType.DMA((4,))],
        compiler_params=_params(),
    )(dw_t)


def _stage1_copies(g_ref, r_ref, send_sems, recv_sems):
    x, y, c = _position()
    return [pltpu.make_async_remote_copy(
        src_ref=g_ref.at[2 * i + 1 - c], dst_ref=r_ref.at[i], send_sem=send_sems.at[i], recv_sem=recv_sems.at[i],
        device_id=(x, y, 1 - c), device_id_type=MESH) for i in range(4)]


def _ffn_core_reduce(dw3, dwb3, dw_o, dwb_o, pos_arr, dx1b, gwb):
    def body(pos_ref, g0, g1, g2, go, gb3_hbm, gbo_hbm, dx1b_ref, gwb_hbm, p0, p1, p2, po, s0, s1, s2, so, dmix_ref,
             r1f, r1o, wo, send_sems, recv_sems, wsem):
        step = pl.program_id(0)
        k = jnp.minimum(step, 2)
        x, y, c = _position()
        chip = 2 * x + y

        def copies(p):
            src = 2 * (chip ^ ((p + 1) & 3)) + 1 - c
            pairs = [(gb3_hbm.at[a, src], r1f.at[a, p]) for a in range(3)] + [(gbo_hbm.at[src], r1o.at[p])]
            return [pltpu.make_async_remote_copy(
                src_ref=s, dst_ref=d, send_sem=send_sems.at[4 * p + a], recv_sem=recv_sems.at[4 * p + a],
                device_id=(x, y, 1 - c), device_id_type=MESH) for a, (s, d) in enumerate(pairs)]

        @pl.when(step == 0)
        def _():
            for p in range(4):
                for cp in copies(p):
                    cp.start()
            loads = [pltpu.make_async_copy(gwb_hbm.at[j, pl.ds(D_TAIL, OUT_ROWS), :],
                                           wo.at[pl.ds(OUT_ROWS * j, OUT_ROWS), :], wsem.at[j]) for j in range(N_DEV)]
            for cp in loads:
                cp.start()
            for cp in loads:
                cp.wait()

        dmix_ref[...] = _nt(dx1b_ref[...], wo[...])

        for p in range(3):
            @pl.when(step == p)
            def _():
                for cp in copies(p):
                    cp.wait_recv()

        for a, (g, pb) in enumerate(((g0, p0), (g1, p1), (g2, p2))):
            pb[...] = (g[...] + r1f[a, k][None].astype(F32)).astype(BF16)
        po[...] = (go[...] + r1o[k][None].astype(F32)).astype(BF16)

        @pl.when(step == 3)
        def _():
            for cp in copies(3):
                cp.wait_recv()
            for a, s in enumerate((s0, s1, s2)):
                s[0] = r1f[a, 3]
            so[0] = r1o[3]
            for p in range(4):
                for cp in copies(p):
                    cp.wait_send()

    t = dx1b.shape[0]
    other = lambda s, pos: 2 * (pos[1] ^ (jnp.minimum(s, 2) + 1)) + pos[0]
    g_spec = lambda lead: pl.BlockSpec((None, 1, FF_W, D), lambda s, pos: (lead, other(s, pos), 0, 0))
    slot = lambda rows: pl.BlockSpec((1, rows, D), lambda s, pos: (jnp.minimum(s, 2), 0, 0))
    one = lambda rows: pl.BlockSpec((1, rows, D), lambda s, pos: (0, 0, 0))
    quarter = pl.BlockSpec((t // 4, D), lambda s, pos: (s, 0))
    hbm = pl.BlockSpec(memory_space=pl.ANY)
    return pl.pallas_call(
        body,
        name="ffn_core_reduce",
        grid_spec=pltpu.PrefetchScalarGridSpec(
            num_scalar_prefetch=1, grid=(4,),
            in_specs=[g_spec(0), g_spec(1), g_spec(2),
                      pl.BlockSpec((1, OUT_ROWS, D), lambda s, pos: (other(s, pos), 0, 0)), hbm, hbm, quarter, hbm],
            out_specs=[slot(FF_W), slot(FF_W), slot(FF_W), slot(OUT_ROWS),
                       one(FF_W), one(FF_W), one(FF_W), one(OUT_ROWS), quarter],
            scratch_shapes=[pltpu.VMEM((3, 4, FF_W, D), BF16), pltpu.VMEM((4, OUT_ROWS, D), BF16),
                            pltpu.VMEM((D, D), BF16), pltpu.SemaphoreType.DMA((16,)),
                            pltpu.SemaphoreType.DMA((16,)), pltpu.SemaphoreType.DMA((N_DEV,))]),
        out_shape=[jax.ShapeDtypeStruct((3, FF_W, D), BF16)] * 3 + [jax.ShapeDtypeStruct((3, OUT_ROWS, D), BF16)]
        + [jax.ShapeDtypeStruct((1, FF_W, D), BF16)] * 3 + [jax.ShapeDtypeStruct((1, OUT_ROWS, D), BF16),
                                                             jax.ShapeDtypeStruct((t, D), F32)],
        compiler_params=_params(("arbitrary",)),
    )(pos_arr, dw3, dw3, dw3, dw_o, dwb3, dwb_o, dx1b, gwb)


def _stage2_scratch(pbs):
    n = len(pbs)
    return ([pltpu.VMEM(p.shape[1:], BF16) for p in pbs] * 2
            + [pltpu.SemaphoreType.DMA((6 * n,)), pltpu.SemaphoreType.DMA((6 * n,)), pltpu.SemaphoreType.DMA((2 * n,))])


def _stage2_copies(p_refs, r_refs, scratch):
    n = len(p_refs)
    owns, gots = scratch[:n], scratch[n:2 * n]
    send_sems, recv_sems, load_sems = scratch[2 * n:]
    x, y, c = _position()
    xn, yn = (1 - x, y, c), (x, 1 - y, c)
    loads, first, second = [], [], []
    for a, (p, r, own, got) in enumerate(zip(p_refs, r_refs, owns, gots)):
        rows = p.shape[1]
        half = -(-rows // 32) * 16
        h0, h1 = pl.ds(0, half), pl.ds(half, rows - half)

        def remote(k, src, dst, to, a=a):
            return pltpu.make_async_remote_copy(
                src_ref=src, dst_ref=dst, send_sem=send_sems.at[6 * a + k], recv_sem=recv_sems.at[6 * a + k],
                device_id=to, device_id_type=MESH)

        loads += [pltpu.make_async_copy(p.at[0, h0, :], own.at[h0, :], load_sems.at[2 * a]),
                  pltpu.make_async_copy(p.at[1, h1, :], own.at[h1, :], load_sems.at[2 * a + 1])]
        first += [remote(0, p.at[2, h0, :], got.at[h0, :], xn), remote(1, p.at[2, h1, :], got.at[h1, :], yn),
                  remote(2, p.at[1, h0, :], r.at[1, h0, :], xn), remote(3, p.at[0, h1, :], r.at[0, h1, :], yn)]
        second += [remote(4, own.at[h0, :], r.at[0, h0, :], yn), remote(5, own.at[h1, :], r.at[1, h1, :], xn)]
    return loads, first, second


def _stage2_start(p_refs, r_refs, scratch):
    loads, first, _ = _stage2_copies(p_refs, r_refs, scratch)
    for cp in loads:
        cp.start()
    for k in range(4):
        for cp in first[k::4]:
            cp.start()


def _stage2_combine(p_refs, r_refs, scratch):
    n = len(p_refs)
    loads, first, second = _stage2_copies(p_refs, r_refs, scratch)
    for a in range(n):
        for cp in loads[2 * a:2 * a + 2]:
            cp.wait()
        for cp in first[4 * a:4 * a + 2]:
            cp.wait_recv()
        own, got = scratch[a], scratch[n + a]
        own[...] = (own[...].astype(F32) + got[...].astype(F32)).astype(BF16)
        for cp in second[2 * a:2 * a + 2]:
            cp.start()


def _stage2_finish(p_refs, r_refs, scratch):
    _, first, second = _stage2_copies(p_refs, r_refs, scratch)
    for a in range(len(p_refs)):
        for cp in first[4 * a + 2:4 * a + 4] + second[2 * a:2 * a + 2]:
            cp.wait_recv()
    for cp in first + second:
        cp.wait_send()


def _finish_weights(items, pos_arr, name, nblk):
    n = len(items)
    in_specs, out_specs, out_shape, operands, wbs = [], [], [], [], []
    for g8, lead, r1, r2, w, m, v in items:
        rows, wr = g8.shape[-2], w.shape[0]
        assert rows % nblk == 0 and wr % nblk == 0 and (nblk == 1 or (rows == wr and rows % (16 * nblk) == 0))
        rb, wb = rows // nblk, wr // nblk
        if lead is not None:
            g_spec = pl.BlockSpec((None, 1, rb, D), lambda i, pos, lead=lead: (lead, 2 * pos[1] + pos[0], i, 0))
        elif g8.shape[0] == 1:
            g_spec = pl.BlockSpec((1, rb, D), lambda i, pos: (0, i, 0))
        else:
            g_spec = pl.BlockSpec((1, rb, D), lambda i, pos: (2 * pos[1] + pos[0], i, 0))
        r1_spec = pl.BlockSpec((1, rb, D), lambda i, pos: (0, i, 0))
        if w.ndim == 3:
            wblk = pl.BlockSpec((wb, 1, D), lambda i, pos: (i, 0, 0))
        else:
            wblk = pl.BlockSpec((wb, D), lambda i, pos: (i, 0))
        in_specs += [g_spec, r1_spec, pl.BlockSpec((2, rb, D), lambda i, pos: (0, i, 0)), wblk, wblk, wblk]
        out_specs += [wblk] * 4
        out_shape += [jax.ShapeDtypeStruct(w.shape, F32)] * 4
        operands += [g8, r1, r2, w, m, v]
        wbs.append(wb)

    def body(pos_ref, *refs):
        for a in range(n):
            g_ref, r1_ref, r2_ref, w_ref, m_ref, v_ref = refs[6 * a:6 * a + 6]
            g_out, d_out, m_out, v_out = refs[6 * n + 4 * a:6 * n + 4 * a + 4]
            g = g_ref[0] + r1_ref[0].astype(F32)
            for k in range(2):
                g = g + r2_ref[k].astype(F32)
            g = g[0:wbs[a], :]
            d, mn, vn = _adamw_math(_get_rows(w_ref), g, _get_rows(m_ref), _get_rows(v_ref))
            for out, val in ((g_out, g), (d_out, d), (m_out, mn), (v_out, vn)):
                _put_rows(out, val)

    return pl.pallas_call(
        body,
        name=name,
        grid_spec=pltpu.PrefetchScalarGridSpec(
            num_scalar_prefetch=1, grid=(nblk,), in_specs=in_specs, out_specs=out_specs),
        out_shape=out_shape,
        compiler_params=_params(("arbitrary",)),
    )(pos_arr, *operands)


SMALL_NAMES = ("norm1_g", "norm2_g", "norm_f_g", "b_gate", "gla_norm_g", "w_gate_up", "conv_w")
WGU_W = NQK // N_DEV
CONV_W = CW // N_DEV


def _small_adamw(sums, ws, ms, vs):
    n = len(SMALL_NAMES)

    def body(*refs):
        acc_ref = refs[0]
        w_refs, m_refs, v_refs = refs[1:1 + n], refs[1 + n:1 + 2 * n], refs[1 + 2 * n:1 + 3 * n]
        loss_ref = refs[1 + 3 * n]
        outs = refs[2 + 3 * n:]
        x, y, c = _position()
        me = 4 * x + 2 * y + c
        acc = acc_ref[...]
        loss_ref[...] = acc[3:4, NQK + DV:NQK + DV + 1]

        def my_columns(full, width):
            r = lax.broadcasted_iota(jnp.int32, (full.shape[1], width), 0)
            col = lax.broadcasted_iota(jnp.int32, (full.shape[1], width), 1)
            sel = (r == width * me + col).astype(F32)
            return _mm(full, sel, precision=HIGHEST)

        dwgu = jnp.concatenate([acc[row:row + 1, lane:lane + NQK] for row, lane in map(_wgu_slot, range(RANK))], axis=0)
        dcw = jnp.concatenate([acc[row:row + 1, lane:lane + CW] for row, lane in CONV_SLOTS], axis=0)
        grads = [acc[0:1, :], acc[1:2, :], acc[2:3, :], acc[3:4, 0:NQK], acc[3:4, NQK:NQK + DV],
                 my_columns(dwgu, WGU_W), my_columns(dcw, CONV_W)]
        for i, g in enumerate(grads):
            d, mn, vn = _adamw_math(_get_rows(w_refs[i]), g, _get_rows(m_refs[i]), _get_rows(v_refs[i]))
            for out, val in zip(outs[4 * i:4 * i + 4], (g, d, mn, vn)):
                _put_rows(out, val)

    vm = pl.BlockSpec(memory_space=pltpu.VMEM)
    out_shape = [jax.ShapeDtypeStruct((1, 1), F32)]
    for w in ws:
        out_shape += [jax.ShapeDtypeStruct(w.shape, F32)] * 4
    return pl.pallas_call(
        body,
        name="small_adamw",
        in_specs=[vm] * (1 + 3 * n),
        out_specs=[vm] * (1 + 4 * n),
        out_shape=out_shape,
        compiler_params=_params(),
    )(sums, *ws, *ms, *vs)


def kernel(x, norm1_g, w_in, w_gate_up, b_gate, gla_norm_g, conv_w, w_out, norm2_g, w_ffn_gate, w_ffn_up, w_ffn_down, norm_f_g, loss_target, m_norm1_g, m_w_in, m_w_gate_up, m_b_gate, m_gla_norm_g, m_conv_w, m_w_out, m_norm2_g, m_w_ffn_gate, m_w_ffn_up, m_w_ffn_down, m_norm_f_g, v_norm1_g, v_w_in, v_w_gate_up, v_b_gate, v_gla_norm_g, v_conv_w, v_w_out, v_norm2_g, v_w_ffn_gate, v_w_ffn_up, v_w_ffn_down, v_norm_f_g):
    xi, yi, ci = _position()
    pos_arr = jnp.stack([ci, 2 * xi + yi]).astype(jnp.int32)
    nb, s, _ = x.shape
    t = nb * s

    tr = lambda a: a[0].T
    rows_of = lambda a: a.transpose(2, 0, 1)
    conv_rows = lambda a: a.transpose(1, 0, 2)
    w_in_t, gwgu, gconv, stage = _gather_w_in(rows_of(w_in), tr(w_ffn_gate), tr(w_ffn_up), w_ffn_down[0], w_out[0],
                                              w_gate_up[0], conv_rows(conv_w))
    wgu_f = gwgu.transpose(1, 0, 2).reshape(RANK, NQK)
    conv_f = gconv.transpose(1, 2, 0, 3).reshape(CONV_K, CW)
    wgu_p = jnp.concatenate([wgu_f, jnp.zeros((A_PAD - RANK, NQK), F32)], axis=0).astype(BF16)

    x2d = x.reshape(t, D)
    tgt2d = loss_target.reshape(t, D)
    tm = 256
    tm_in = min(512, t)
    tk = min(2048, t)
    proj, z, h, gwb = _in_proj_fwd(x2d, norm1_g, w_in_t, wgu_p, b_gate, tm_in, stage)
    proj3 = proj.reshape(nb, s, PW)
    z3 = z.reshape(nb, s, NQK)
    mix3, opre3, sprev, x1, gwa = _mix_fwd(proj3, z3, gla_norm_g, conv_f, stage, x, gwb)
    mix2d = mix3.reshape(t, D)
    dx1, dx1b, adu, hb, dg2, dgf, loss_part = _ffn_fwd_bwd(
        x1.reshape(t, D), tgt2d, gwa, gwb, norm2_g, norm_f_g.reshape(1, D), tm)
    dw3, dwb3 = _dw_ffn(adu, hb, tk)
    dw3 = dw3.reshape(3, N_DEV, FF_W, D)
    dw_o, dwb_o = _tn_matmul(mix2d, dx1b, D // 2, D, tk, "dw_out", True)
    dw_o = dw_o.reshape(N_DEV, OUT_ROWS, D)
    *pb, sib_d, sib_g, sib_u, sib_o, dmix = _ffn_core_reduce(
        dw3, dwb3.reshape(3, N_DEV, FF_W, D), dw_o, dwb_o.reshape(N_DEV, OUT_ROWS, D), pos_arr, dx1b, gwb)
    g8 = [dw3, dw3, dw3, dw_o]
    leads = [0, 1, 2, None]
    tags = ("w_ffn_down", "w_ffn_gate", "w_ffn_up", "w_out")
    r1 = [sib_d, sib_g, sib_u, sib_o]
    mb = _mix_bwd(proj3, z3, sprev, opre3, dmix.reshape(nb, s, D), gla_norm_g, conv_f, wgu_p, [pb[0], pb[1], pb[3]])
    dproj3, dgng, dcw, dbg, dwgu = mb[:5]
    dproj2d = dproj3.reshape(t, PW)
    dw_in_t, r2_up = _tn_matmul(dproj2d, h, PW // 5, D, t, "dw_in", False, _stage2_rider([pb[2]]))
    r2 = [mb[5], mb[6], r2_up, mb[7]]
    g_in, r1_in, pb_in = _w_in_core_reduce(dw_in_t)
    dx, small_sums, r2_in = _in_proj_bwd(dproj2d, x2d, dx1, norm1_g, w_in_t, tm_in, pb_in,
                                         (dg2, dgf, dbg, dgng, dwgu, dcw, loss_part))

    tags = ("w_in",) + tags
    g8 = [g_in] + g8
    leads = [None] + leads
    r1 = [r1_in] + list(r1)
    r2 = [r2_in] + r2
    shard_w = (rows_of(w_in), w_ffn_down[0], tr(w_ffn_gate), tr(w_ffn_up), w_out[0])
    shard_m = (rows_of(m_w_in), m_w_ffn_down[0], tr(m_w_ffn_gate), tr(m_w_ffn_up), m_w_out[0])
    shard_v = (rows_of(v_w_in), v_w_ffn_down[0], tr(v_w_ffn_gate), tr(v_w_ffn_up), v_w_out[0])
    back = (lambda o: o.transpose(1, 2, 0), lambda o: o[None], lambda o: o.T[None], lambda o: o.T[None],
            lambda o: o[None])
    items = list(zip(g8, leads, r1, r2, shard_w, shard_m, shard_v))
    flat = list(_finish_weights(items[1:], pos_arr, "finish_ffn_out", 2))
    flat = list(_finish_weights(items[:1], pos_arr, "finish_w_in", 1)) + flat
    results = {}
    for i, (tag, to_shard) in enumerate(zip(tags, back)):
        results[tag] = [to_shard(o) for o in flat[4 * i:4 * i + 4]]

    small_w = (norm1_g, norm2_g, norm_f_g.reshape(1, D), b_gate, gla_norm_g, w_gate_up[0], conv_rows(conv_w))
    small_m = (m_norm1_g, m_norm2_g, m_norm_f_g.reshape(1, D), m_b_gate, m_gla_norm_g, m_w_gate_up[0],
               conv_rows(m_conv_w))
    small_v = (v_norm1_g, v_norm2_g, v_norm_f_g.reshape(1, D), v_b_gate, v_gla_norm_g, v_w_gate_up[0],
               conv_rows(v_conv_w))
    so = _small_adamw(small_sums, small_w, small_m, small_v)
    loss = so[0].reshape(())
    to_shape = {"norm_f_g": lambda o: o.reshape(D), "w_gate_up": lambda o: o[None],
                "conv_w": lambda o: o.transpose(1, 0, 2)}
    for i, name in enumerate(SMALL_NAMES):
        results[name] = [to_shape.get(name, lambda o: o)(o) for o in so[1 + 4 * i:5 + 4 * i]]

    names = ("norm1_g", "w_in", "w_gate_up", "b_gate", "gla_norm_g", "conv_w", "w_out", "norm2_g",
             "w_ffn_gate", "w_ffn_up", "w_ffn_down", "norm_f_g")
    outs = [loss, dx.reshape(nb, s, D)]
    for kind in range(4):
        for name in names:
            outs.append(results[name][kind])
    return tuple(outs)
```

```python
import jax
import jax.numpy as jnp
from jax import lax
from jax.experimental import pallas as pl
from jax.experimental.pallas import tpu as pltpu

F32 = jnp.float32
BF16 = jnp.bfloat16
HIGHEST = lax.Precision.HIGHEST
MESH = pl.DeviceIdType.MESH

N_DEV = 8
D = 1024
DFF = 2816
HEADS = 4
DK = 64
DV = 128
NQK = HEADS * DK
NV = HEADS * DV
RANK = 16
CHUNK = 64
CW = 512
CONV_K = 3
IN_COLS = 3088
EPS = 1e-6
INV_GATE_NORM = 1.0 / 16.0
Q_SCALE = DK ** -0.5

PW = 3200
OQ, OK_, OV, OG, OCB, OCC, OCH, OA = 0, 256, 512, 1024, 1536, 2048, 2560, 3072
A_PAD = 128

ADAM_LR = 0.001
ADAM_B1 = 0.9
ADAM_B2 = 0.999
ADAM_EPS = 1e-08
ADAM_WD = 0.01
ADAM_STEP = 10

IN_W = IN_COLS // N_DEV
IN_ROWS = 400
FF_W = DFF // N_DEV
OUT_ROWS = D // N_DEV
SLAB_IN = 0
SLAB_G = SLAB_IN + IN_ROWS
SLAB_U = SLAB_G + FF_W
SLAB_D = SLAB_U + FF_W
SLAB_O = SLAB_D + FF_W
SLAB_ROWS = SLAB_O + OUT_ROWS
D_HEAD = 128
D_TAIL = FF_W - D_HEAD
SLAB_SPLIT = SLAB_D + D_HEAD

VMEM_LIMIT = 56 * 1024 * 1024


def _params(sem=None, vmem=VMEM_LIMIT):
    return pltpu.CompilerParams(dimension_semantics=sem, vmem_limit_bytes=vmem)


def _nt(a, b):
    return lax.dot_general(a, b, (((1,), (1,)), ((), ())), preferred_element_type=F32)


def _tn(a, b, precision=None):
    return lax.dot_general(a, b, (((0,), (0,)), ((), ())), preferred_element_type=F32, precision=precision)


def _mm(a, b, precision=None):
    return jnp.dot(a, b, preferred_element_type=F32, precision=precision)


def _in_segments():
    segs = []
    for j in range(N_DEV):
        lo, hi = IN_W * j, IN_W * (j + 1)
        cuts = sorted({lo, hi} | {c for c in (OCB, OCB + RANK) if lo < c < hi})
        for a, b in zip(cuts[:-1], cuts[1:]):
            if a < OCB:
                d = a
            elif a < OCB + RANK:
                d = OA + (a - OCB)
            else:
                d = a - RANK
            segs.append((j, a - lo, b - lo, d))
    return segs


def _in_proj_fwd(x2d, g1, w_in_t, wgu_p, b_gate, tm, stage):
    t = x2d.shape[0]
    nt = t // tm
    g_rows = SLAB_ROWS - SLAB_SPLIT

    def body(x_ref, g_ref, w_ref, wgu_ref, bg_ref, stage_hbm, proj_ref, z_ref, h_ref, gwb_ref,
             send_sems, recv_sems, local_sem):
        gargs = (stage_hbm, SLAB_SPLIT, g_rows, gwb_ref, send_sems, recv_sems, local_sem)

        @pl.when(pl.program_id(0) == 0)
        def _():
            _gather_start(*gargs)

        @pl.when(pl.program_id(0) == RELAY_AT * nt // 8)
        def _():
            _gather_relay(*gargs)

        @pl.when(pl.program_id(0) == FORWARD_AT * nt // 8)
        def _():
            _gather_forward(*gargs)

        x = x_ref[...]
        r = lax.rsqrt(jnp.mean(x * x, axis=-1, keepdims=True) + EPS)
        h = ((x * r) * g_ref[...]).astype(BF16)
        h_ref[...] = h
        proj = _nt(h, w_ref[...])
        proj_ref[...] = proj
        pa = proj[:, OA:OA + A_PAD].astype(BF16)
        z_ref[...] = _mm(pa, wgu_ref[...]) + bg_ref[...]

        @pl.when(pl.program_id(0) == nt - 1)
        def _():
            _gather_finish(*gargs)

    return pl.pallas_call(
        body,
        name="in_proj_fwd",
        grid=(t // tm,),
        in_specs=[
            pl.BlockSpec((tm, D), lambda i: (i, 0)),
            pl.BlockSpec((1, D), lambda i: (0, 0)),
            pl.BlockSpec((PW, D), lambda i: (0, 0)),
            pl.BlockSpec((A_PAD, NQK), lambda i: (0, 0)),
            pl.BlockSpec((1, NQK), lambda i: (0, 0)),
            pl.BlockSpec(memory_space=pl.ANY),
        ],
        out_specs=[
            pl.BlockSpec((tm, PW), lambda i: (i, 0)),
            pl.BlockSpec((tm, NQK), lambda i: (i, 0)),
            pl.BlockSpec((tm, D), lambda i: (i, 0)),
            pl.BlockSpec(memory_space=pl.ANY),
        ],
        out_shape=[
            jax.ShapeDtypeStruct((t, PW), F32),
            jax.ShapeDtypeStruct((t, NQK), F32),
            jax.ShapeDtypeStruct((t, D), BF16),
            jax.ShapeDtypeStruct((N_DEV, g_rows, D), BF16),
        ],
        scratch_shapes=_gather_sems(),
        compiler_params=_params(("arbitrary",)),
    )(x2d, g1, w_in_t, wgu_p, b_gate, stage)


def _head_masks():
    lane = lax.broadcasted_iota(jnp.int32, (1, NQK), 1)
    return [(lane >= DK * h) & (lane < DK * (h + 1)) for h in range(HEADS)]


def _split_bf16(x, n):
    parts = []
    for _ in range(n):
        p = x.astype(BF16)
        parts.append(p)
        x = x - p.astype(F32)
    return parts


def _chunk_fwd_parts(q, k, z, tril16):
    la = (jnp.minimum(z, 0.0) - jnp.log1p(jnp.exp(-jnp.abs(z)))) * INV_GATE_NORM
    la_parts = _split_bf16(la, 3)
    bc = _mm(tril16, la_parts[0]) + _mm(tril16, la_parts[1]) + _mm(tril16, la_parts[2])
    bl = bc[CHUNK - 1:CHUNK, :]
    eb = jnp.exp(bc)
    enb = jnp.exp(-bc)
    ekl = jnp.exp(bl - bc)
    qi = (q * Q_SCALE) * eb
    ki = k * enb
    ks = k * ekl
    ones16 = jnp.ones((CHUNK, DV), BF16)
    decb = jnp.exp(_tn(la_parts[0], ones16) + _tn(la_parts[1], ones16) + _tn(la_parts[2], ones16))
    return la, eb, enb, ekl, qi, ki, ks, decb


def _stack_heads(a, masks):
    return jnp.concatenate([jnp.where(m, a, 0.0) for m in masks], axis=0)


def _merge_heads(blocks, masks):
    out = blocks[HEADS - 1]
    for h in range(HEADS - 2, -1, -1):
        out = jnp.where(masks[h], blocks[h], out)
    return out


def _causal_stack_mask():
    row = lax.broadcasted_iota(jnp.int32, (HEADS * CHUNK, CHUNK), 0)
    col = lax.broadcasted_iota(jnp.int32, (HEADS * CHUNK, CHUNK), 1)
    return (row & (CHUNK - 1)) >= col


def _conv_taps(u, uprev):
    row = lax.broadcasted_iota(jnp.int32, u.shape, 0)
    u1 = jnp.where(row < 1, pltpu.roll(uprev, 1, 0), pltpu.roll(u, 1, 0))
    u2 = jnp.where(row < 2, pltpu.roll(uprev, 2, 0), pltpu.roll(u, 2, 0))
    return u1, u2


def _mix_fwd(proj3, z3, gng, conv_w, stage, x3, gwb):
    nb, s, _ = proj3.shape
    nc = s // CHUNK
    g_rows = SLAB_SPLIT - SLAB_G

    def body(p_ref, z_ref, gng_ref, cw_ref, stage_hbm, x_ref, gwb_hbm, mix_ref, o_ref, sprev_ref, x1_ref, gwa_ref,
             s_ref, uprev_ref, wo, wsem, send_sems, recv_sems, local_sem):
        n = pl.program_id(0)
        gargs = (stage_hbm, SLAB_G, g_rows, gwa_ref, send_sems, recv_sems, local_sem)

        @pl.when(n == 0)
        def _():
            _gather_start(*gargs)
            loads = [pltpu.make_async_copy(gwb_hbm.at[j, pl.ds(D_TAIL, OUT_ROWS), :],
                                           wo.at[pl.ds(OUT_ROWS * j, OUT_ROWS), :], wsem.at[j]) for j in range(N_DEV)]
            for cp in loads:
                cp.start()
            s_ref[...] = jnp.zeros_like(s_ref)
            uprev_ref[...] = jnp.zeros_like(uprev_ref)
            for cp in loads:
                cp.wait()

        @pl.when(n == RELAY_AT * nc // 8)
        def _():
            _gather_relay(*gargs)

        @pl.when(n == FORWARD_AT * nc // 8)
        def _():
            _gather_forward(*gargs)

        r_i = lax.broadcasted_iota(jnp.int32, (CHUNK, CHUNK), 0)
        c_i = lax.broadcasted_iota(jnp.int32, (CHUNK, CHUNK), 1)
        tril16 = (r_i >= c_i).astype(BF16)
        masks = _head_masks()
        cmask = _causal_stack_mask()
        gg = gng_ref[...]
        for b in range(nb):
            q = p_ref[b, :, OQ:OQ + NQK]
            k = p_ref[b, :, OK_:OK_ + NQK]
            _, _, _, _, qi, ki, ks, decb = _chunk_fwd_parts(q, k, z_ref[b], tril16)
            qs = _stack_heads(qi, masks).astype(BF16)
            sc = jnp.where(cmask, _nt(qs, ki.astype(BF16)), 0.0).astype(BF16)
            st = s_ref[b]
            sprev_ref[b, 0] = st
            o_inter = _mm(qs, st.astype(BF16))
            v16 = p_ref[b, :, OV:OV + NV].astype(BF16)
            kv = _tn(ks.astype(BF16), v16)
            for h in range(HEADS):
                rows = slice(CHUNK * h, CHUNK * (h + 1))
                cols = slice(DV * h, DV * (h + 1))
                o = _mm(sc[rows], v16[:, cols]) + o_inter[rows]
                o_ref[b, :, cols] = o
                r = lax.rsqrt(jnp.mean(o * o, axis=-1, keepdims=True) + EPS)
                on = (o * r) * gg
                g = p_ref[b, :, OG + DV * h:OG + DV * (h + 1)]
                mix_ref[b, :, cols] = (on * (g * jax.nn.sigmoid(g))).astype(BF16)
                s_ref[b, rows, :] = decb[rows] * st[rows] + kv[rows, cols]
            u = p_ref[b, :, OCC:OCC + CW] * p_ref[b, :, OCH:OCH + CW]
            u1, u2 = _conv_taps(u, uprev_ref[b])
            yc = cw_ref[0:1, :] * u2 + cw_ref[1:2, :] * u1 + cw_ref[2:3, :] * u
            mix_ref[b, :, NV:NV + CW] = (p_ref[b, :, OCB:OCB + CW] * yc).astype(BF16)
            uprev_ref[b] = u
        mixed = _mm(jnp.concatenate([mix_ref[b] for b in range(nb)], axis=0), wo[...])
        for b in range(nb):
            x1_ref[b] = x_ref[b] + mixed[CHUNK * b:CHUNK * (b + 1)]

        @pl.when(n == nc - 1)
        def _():
            _gather_finish(*gargs)

    return pl.pallas_call(
        body,
        name="mix_fwd",
        grid=(nc,),
        in_specs=[
            pl.BlockSpec((nb, CHUNK, PW), lambda n: (0, n, 0)),
            pl.BlockSpec((nb, CHUNK, NQK), lambda n: (0, n, 0)),
            pl.BlockSpec((1, DV), lambda n: (0, 0)),
            pl.BlockSpec((CONV_K, CW), lambda n: (0, 0)),
            pl.BlockSpec(memory_space=pl.ANY),
            pl.BlockSpec((nb, CHUNK, D), lambda n: (0, n, 0)),
            pl.BlockSpec(memory_space=pl.ANY),
        ],
        out_specs=[
            pl.BlockSpec((nb, CHUNK, D), lambda n: (0, n, 0)),
            pl.BlockSpec((nb, CHUNK, NV), lambda n: (0, n, 0)),
            pl.BlockSpec((nb, 1, NQK, DV), lambda n: (0, n, 0, 0)),
            pl.BlockSpec((nb, CHUNK, D), lambda n: (0, n, 0)),
            pl.BlockSpec(memory_space=pl.ANY),
        ],
        out_shape=[
            jax.ShapeDtypeStruct((nb, s, D), BF16),
            jax.ShapeDtypeStruct((nb, s, NV), F32),
            jax.ShapeDtypeStruct((nb, nc, NQK, DV), F32),
            jax.ShapeDtypeStruct((nb, s, D), F32),
            jax.ShapeDtypeStruct((N_DEV, g_rows, D), BF16),
        ],
        scratch_shapes=[pltpu.VMEM((nb, NQK, DV), F32), pltpu.VMEM((nb, CHUNK, CW), F32),
                        pltpu.VMEM((D, D), BF16), pltpu.SemaphoreType.DMA((N_DEV,))] + _gather_sems(),
        compiler_params=_params(("arbitrary",)),
    )(proj3, z3, gng, conv_w, stage, x3, gwb)


def _ffn_fwd_bwd(x1_2d, tgt2d, gwa, gwb, g2, gf, tm):
    t = x1_2d.shape[0]

    def body(x1_ref, tgt_ref, g2_ref, gf_ref, gwa_hbm, gwb_hbm,
             dx1_ref, dx1b_ref, adu_ref, hb_ref, dg2_ref, dgf_ref, loss_ref,
             wg, wu, wd, wsem):
        i = pl.program_id(0)

        def weight_copies(n, dst, src, off, rows, at=0):
            return [pltpu.make_async_copy(src.at[j, pl.ds(off, rows), :], dst.at[pl.ds(FF_W * j + at, rows), :],
                                          wsem.at[N_DEV * n + j]) for j in range(N_DEV)]

        loads = (weight_copies(0, wg, gwa_hbm, 0, FF_W), weight_copies(1, wu, gwa_hbm, FF_W, FF_W),
                 weight_copies(2, wd, gwa_hbm, 2 * FF_W, D_HEAD), weight_copies(3, wd, gwb_hbm, 0, D_TAIL, D_HEAD))

        @pl.when(i == 0)
        def _():
            for group in loads:
                for cp in group:
                    cp.start()
            dg2_ref[...] = jnp.zeros_like(dg2_ref)
            dgf_ref[...] = jnp.zeros_like(dgf_ref)
            loss_ref[...] = jnp.zeros_like(loss_ref)
            for group in loads:
                for cp in group:
                    cp.wait()

        g2v = g2_ref[...]
        gfv = gf_ref[...]
        x1 = x1_ref[...]
        r2 = lax.rsqrt(jnp.mean(x1 * x1, axis=-1, keepdims=True) + EPS)
        n2 = x1 * r2
        h2 = (n2 * g2v).astype(BF16)
        hb_ref[1] = h2
        gate = _nt(h2, wg[...])
        up = _nt(h2, wu[...])
        sg = jax.nn.sigmoid(gate)
        sil = gate * sg
        act = (sil * up).astype(BF16)
        adu_ref[0] = act
        x2 = x1 + _mm(act, wd[...])
        rf = lax.rsqrt(jnp.mean(x2 * x2, axis=-1, keepdims=True) + EPS)
        nf = x2 * rf
        err = nf * gfv - tgt_ref[...]
        loss_ref[...] += 0.5 * jnp.sum(jnp.mean(err * err, axis=-1, keepdims=True))
        dy = err * (1.0 / D)
        dgf_ref[...] += jnp.sum(dy * nf, axis=0, keepdims=True)
        dnf = dy * gfv
        dx2 = rf * (dnf - nf * jnp.mean(dnf * nf, axis=-1, keepdims=True))
        dx2b = dx2.astype(BF16)
        hb_ref[0] = dx2b
        dact = _nt(dx2b, wd[...])
        dup = (dact * sil).astype(BF16)
        dgate = ((dact * up) * (sg * (1.0 + gate * (1.0 - sg)))).astype(BF16)
        adu_ref[2] = dup
        adu_ref[1] = dgate
        dh2 = _mm(dgate, wg[...]) + _mm(dup, wu[...])
        dg2_ref[...] += jnp.sum(dh2 * n2, axis=0, keepdims=True)
        dn2 = dh2 * g2v
        dx1 = dx2 + r2 * (dn2 - n2 * jnp.mean(dn2 * n2, axis=-1, keepdims=True))
        dx1_ref[...] = dx1
        dx1b_ref[...] = dx1.astype(BF16)

    tile = lambda w: pl.BlockSpec((tm, w), lambda i: (i, 0))
    vec = pl.BlockSpec((1, D), lambda i: (0, 0))
    hbm = pl.BlockSpec(memory_space=pl.ANY)
    return pl.pallas_call(
        body,
        name="ffn_fwd_bwd",
        grid=(t // tm,),
        in_specs=[tile(D), tile(D), vec, vec, hbm, hbm],
        out_specs=[tile(D), tile(D), pl.BlockSpec((3, tm, DFF), lambda i: (0, i, 0)),
                   pl.BlockSpec((2, tm, D), lambda i: (0, i, 0)), vec, vec,
                   pl.BlockSpec((1, 128), lambda i: (0, 0))],
        out_shape=[
            jax.ShapeDtypeStruct((t, D), F32),
            jax.ShapeDtypeStruct((t, D), BF16),
            jax.ShapeDtypeStruct((3, t, DFF), BF16),
            jax.ShapeDtypeStruct((2, t, D), BF16),
            jax.ShapeDtypeStruct((1, D), F32),
            jax.ShapeDtypeStruct((1, D), F32),
            jax.ShapeDtypeStruct((1, 128), F32),
        ],
        scratch_shapes=[pltpu.VMEM((DFF, D), BF16), pltpu.VMEM((DFF, D), BF16), pltpu.VMEM((DFF, D), BF16),
                        pltpu.SemaphoreType.DMA((4 * N_DEV,))],
        compiler_params=_params(("arbitrary",)),
    )(x1_2d, tgt2d, g2, gf, gwa, gwb)


def _stage2_rider(pbs):
    return dict(inputs=list(pbs), out_shape=[jax.ShapeDtypeStruct((2,) + p.shape[1:], BF16) for p in pbs],
                scratch=_stage2_scratch(pbs))


def _tn_matmul(a, b, bm, bn, tk, name, with_bf16, rider=None):
    t, m = a.shape
    n = b.shape[1]
    nk = t // tk
    nout = 2 if with_bf16 else 1
    grid = (m // bm, n // bn, nk)
    steps = grid[0] * grid[1] * nk
    r_in = [] if rider is None else rider["inputs"]
    r_out = [] if rider is None else rider["out_shape"]

    def body(a_ref, b_ref, *rest):
        ins, outs = rest[:len(r_in)], rest[len(r_in):len(r_in) + nout]
        r_outs, scratch = rest[len(r_in) + nout:len(r_in) + nout + len(r_out)], rest[len(r_in) + nout + len(r_out):]
        o_ref = outs[0]
        i, j, k = pl.program_id(0), pl.program_id(1), pl.program_id(2)
        step = (i * grid[1] + j) * nk + k
        if rider is not None:
            @pl.when(step == 0)
            def _():
                _stage2_start(ins, r_outs, scratch)

            @pl.when(step == steps // 2)
            def _():
                _stage2_combine(ins, r_outs, scratch)

        @pl.when(k == 0)
        def _():
            o_ref[...] = jnp.zeros_like(o_ref)

        o_ref[...] += _tn(a_ref[...].astype(BF16), b_ref[...].astype(BF16))
        if with_bf16:
            @pl.when(k == nk - 1)
            def _():
                outs[1][...] = o_ref[...].astype(BF16)
        if rider is not None:
            @pl.when(step == steps - 1)
            def _():
                _stage2_finish(ins, r_outs, scratch)

    out_blk = pl.BlockSpec((bm, bn), lambda i, j, k: (i, j))
    hbm = pl.BlockSpec(memory_space=pl.ANY)
    out_shape = [jax.ShapeDtypeStruct((m, n), F32)] + ([jax.ShapeDtypeStruct((m, n), BF16)] if with_bf16 else [])
    res = pl.pallas_call(
        body,
        name=name,
        grid=grid,
        in_specs=[pl.BlockSpec((tk, bm), lambda i, j, k: (k, i)), pl.BlockSpec((tk, bn), lambda i, j, k: (k, j))]
        + [hbm] * len(r_in),
        out_specs=[out_blk] * nout + [hbm] * len(r_out),
        out_shape=out_shape + list(r_out),
        scratch_shapes=[] if rider is None else rider["scratch"],
        compiler_params=_params(("parallel", "parallel", "arbitrary") if rider is None
                                else ("arbitrary", "arbitrary", "arbitrary")),
    )(a, b, *r_in)
    return res[0] if len(res) == 1 else res


def _dw_ffn(adu, hb, tk):
    _, t, _ = adu.shape
    bm = DFF // 2
    nk = t // tk

    def body(a_ref, b_ref, o_ref, ob_ref):
        k = pl.program_id(2)

        @pl.when(k == 0)
        def _():
            o_ref[...] = jnp.zeros_like(o_ref)

        o_ref[...] += _tn(a_ref[...], b_ref[...])

        @pl.when(k == nk - 1)
        def _():
            ob_ref[...] = o_ref[...].astype(BF16)

    out_blk = pl.BlockSpec((None, bm, D), lambda p, i, k: (p, i, 0))
    return pl.pallas_call(
        body,
        name="dw_ffn",
        grid=(3, DFF // bm, nk),
        in_specs=[pl.BlockSpec((None, tk, bm), lambda p, i, k: (p, k, i)),
                  pl.BlockSpec((None, tk, D), lambda p, i, k: (jnp.minimum(p, 1), k, 0))],
        out_specs=[out_blk, out_blk],
        out_shape=[jax.ShapeDtypeStruct((3, DFF, D), F32), jax.ShapeDtypeStruct((3, DFF, D), BF16)],
        compiler_params=_params(("arbitrary", "arbitrary", "arbitrary")),
    )(adu, hb)


def _mix_bwd(proj3, z3, sprev, opre3, dmix3, gng, conv_w, wgu_p, pbs):
    nb, s, _ = proj3.shape
    nc = s // CHUNK
    na = len(pbs)

    def body(*refs):
        (p_ref, pprev_ref, z_ref, sp_ref, o_ref, dm_ref, gng_ref, cw_ref, wgu_ref) = refs[:9]
        pb_refs = refs[9:9 + na]
        (dproj_ref, dgng_ref, dcw_ref, dbg_ref, dwgu_ref) = refs[9 + na:14 + na]
        r2_refs = refs[14 + na:14 + 2 * na]
        ds_ref, dycn_ref = refs[14 + 2 * na:16 + 2 * na]
        stage2 = (pb_refs, r2_refs, refs[16 + 2 * na:])
        step = pl.program_id(0)
        n = nc - 1 - step

        @pl.when(step == 0)
        def _():
            _stage2_start(*stage2)
            ds_ref[...] = jnp.zeros_like(ds_ref)
            dycn_ref[...] = jnp.zeros_like(dycn_ref)
            dgng_ref[...] = jnp.zeros_like(dgng_ref)
            dcw_ref[...] = jnp.zeros_like(dcw_ref)
            dbg_ref[...] = jnp.zeros_like(dbg_ref)
            dwgu_ref[...] = jnp.zeros_like(dwgu_ref)

        @pl.when(step == RELAY_AT * nc // 8)
        def _():
            _stage2_combine(*stage2)

        r_i = lax.broadcasted_iota(jnp.int32, (CHUNK, CHUNK), 0)
        c_i = lax.broadcasted_iota(jnp.int32, (CHUNK, CHUNK), 1)
        tril16 = (r_i >= c_i).astype(BF16)
        triu16 = (r_i <= c_i).astype(BF16)
        causal = r_i >= c_i
        masks = _head_masks()
        cmask = _causal_stack_mask()
        gg = gng_ref[...]
        last_row = lax.broadcasted_iota(jnp.int32, (CHUNK, NQK), 0) == CHUNK - 1
        ones_r = jnp.ones((16, DV), BF16)
        has_prev = (n > 0).astype(F32)
        for b in range(nb):
            q = p_ref[b, :, OQ:OQ + NQK]
            k = p_ref[b, :, OK_:OK_ + NQK]
            z = z_ref[b]
            _, eb, enb, ekl, qi, ki, ks, decb = _chunk_fwd_parts(q, k, z, tril16)
            qi16 = qi.astype(BF16)
            ki16 = ki.astype(BF16)
            qs = _stack_heads(qi, masks).astype(BF16)
            sc = jnp.where(cmask, _nt(qs, ki16), 0.0).astype(BF16)
            st = sp_ref[b, 0]
            st16 = st.astype(BF16)
            dsn = ds_ref[b]
            dsn16 = dsn.astype(BF16)
            v16 = p_ref[b, :, OV:OV + NV].astype(BF16)
            do16 = []
            dgng = jnp.zeros((1, DV), F32)
            for h in range(HEADS):
                cols = slice(DV * h, DV * (h + 1))
                o = o_ref[b, :, cols]
                r = lax.rsqrt(jnp.mean(o * o, axis=-1, keepdims=True) + EPS)
                nh = o * r
                g = p_ref[b, :, OG + DV * h:OG + DV * (h + 1)]
                sg = jax.nn.sigmoid(g)
                dog = dm_ref[b, :, cols]
                dproj_ref[b, :, OG + DV * h:OG + DV * (h + 1)] = (
                    (dog * (nh * gg)) * (sg * (1.0 + g * (1.0 - sg)))).astype(BF16)
                don = dog * (g * sg)
                dgng = dgng + jnp.sum(don * nh, axis=0, keepdims=True)
                dn = don * gg
                do = r * (dn - nh * jnp.mean(dn * nh, axis=-1, keepdims=True))
                do16.append(do.astype(BF16))
            dgng_ref[...] += dgng
            do_rows = jnp.concatenate(do16, axis=0)
            v_rows = jnp.concatenate([v16[:, DV * h:DV * (h + 1)] for h in range(HEADS)], axis=0)
            dp16 = [jnp.where(causal, _nt(do16[h], v16[:, DV * h:DV * (h + 1)]), 0.0).astype(BF16)
                    for h in range(HEADS)]
            ks_dsn = _mm(_stack_heads(ks, masks).astype(BF16), dsn16)
            do_st = _nt(do_rows, st16)
            v_dsn = _nt(v_rows, dsn16)
            dp_ki = _mm(jnp.concatenate(dp16, axis=0), ki16)
            q_do = _tn(qi16, jnp.concatenate(do16, axis=1))
            dki_h = []
            for h in range(HEADS):
                rows = slice(CHUNK * h, CHUNK * (h + 1))
                cols = slice(DV * h, DV * (h + 1))
                dv = _tn(sc[rows], do16[h]) + ks_dsn[rows]
                dproj_ref[b, :, OV + DV * h:OV + DV * (h + 1)] = dv.astype(BF16)
                dki_h.append(_tn(dp16[h], qi16))
                ds_ref[b, rows, :] = decb[rows] * dsn[rows] + q_do[rows, cols]
            blocks = lambda a: [a[CHUNK * h:CHUNK * (h + 1)] for h in range(HEADS)]
            dqi = _merge_heads(blocks(dp_ki + do_st), masks)
            dki = _merge_heads(dki_h, masks)
            dks = _merge_heads(blocks(v_dsn), masks)
            dproj_ref[b, :, OQ:OQ + NQK] = (dqi * (Q_SCALE * eb)).astype(BF16)
            dproj_ref[b, :, OK_:OK_ + NQK] = (dki * enb + dks * ekl).astype(BF16)
            dks_ks = dks * ks
            db = dqi * qi - dki * ki - dks_ks
            sd = _split_bf16(dsn * st * decb, 2)
            dbl = jnp.sum(dks_ks, axis=0, keepdims=True) + (_nt(ones_r, sd[0]) + _nt(ones_r, sd[1]))[0:1, :]
            db = db + jnp.where(last_row, dbl, 0.0)
            db_parts = _split_bf16(db, 3)
            dla = _mm(triu16, db_parts[0]) + _mm(triu16, db_parts[1]) + _mm(triu16, db_parts[2])
            dz = (dla * INV_GATE_NORM) * (1.0 / (1.0 + jnp.exp(z)))
            dbg_ref[...] += jnp.sum(dz, axis=0, keepdims=True)
            dz16 = dz.astype(BF16)
            pa16 = p_ref[b, :, OA:OA + A_PAD].astype(BF16)
            dwgu_ref[...] += _tn(pa16, dz16)
            dproj_ref[b, :, OA:OA + A_PAD] = _nt(dz16, wgu_ref[...]).astype(BF16)
            cb = p_ref[b, :, OCB:OCB + CW]
            cc = p_ref[b, :, OCC:OCC + CW]
            ch = p_ref[b, :, OCH:OCH + CW]
            u = cc * ch
            uprev = (pprev_ref[b, :, 0:CW] * pprev_ref[b, :, CW:2 * CW]) * has_prev
            u1, u2 = _conv_taps(u, uprev)
            w0 = cw_ref[0:1, :]
            w1 = cw_ref[1:2, :]
            w2 = cw_ref[2:3, :]
            yc = w0 * u2 + w1 * u1 + w2 * u
            doc = dm_ref[b, :, NV:NV + CW]
            dproj_ref[b, :, OCB:OCB + CW] = (doc * yc).astype(BF16)
            dyc = doc * cb
            dycn = dycn_ref[b]
            row = lax.broadcasted_iota(jnp.int32, dyc.shape, 0)
            d1 = jnp.where(row >= CHUNK - 1, pltpu.roll(dycn, CHUNK - 1, 0), pltpu.roll(dyc, CHUNK - 1, 0))
            d2 = jnp.where(row >= CHUNK - 2, pltpu.roll(dycn, CHUNK - 2, 0), pltpu.roll(dyc, CHUNK - 2, 0))
            du = w2 * dyc + w1 * d1 + w0 * d2
            dproj_ref[b, :, OCC:OCC + CW] = (du * ch).astype(BF16)
            dproj_ref[b, :, OCH:OCH + CW] = (du * cc).astype(BF16)
            dcw_ref[0:1, :] += jnp.sum(dyc * u2, axis=0, keepdims=True)
            dcw_ref[1:2, :] += jnp.sum(dyc * u1, axis=0, keepdims=True)
            dcw_ref[2:3, :] += jnp.sum(dyc * u, axis=0, keepdims=True)
            dycn_ref[b] = dyc

        @pl.when(step == nc - 1)
        def _():
            _stage2_finish(*stage2)

    rev =lambda w: pl.BlockSpec((nb, CHUNK, w), lambda i: (0, nc - 1 - i, 0))
    const = lambda r, c: pl.BlockSpec((r, c), lambda i: (0, 0))
    hbm = pl.BlockSpec(memory_space=pl.ANY)
    return pl.pallas_call(
        body,
        name="mix_bwd",
        grid=(nc,),
        in_specs=[
            rev(PW),
            pl.BlockSpec((nb, CHUNK, 2 * CW), lambda i: (0, jnp.maximum(nc - 2 - i, 0), OCC // (2 * CW))),
            rev(NQK),
            pl.BlockSpec((nb, 1, NQK, DV), lambda i: (0, nc - 1 - i, 0, 0)),
            rev(NV),
            rev(D),
            const(1, DV),
            const(CONV_K, CW),
            const(A_PAD, NQK),
        ] + [hbm] * na,
        out_specs=[rev(PW), const(1, DV), const(8, CW), const(1, NQK), const(A_PAD, NQK)] + [hbm] * na,
        out_shape=[
            jax.ShapeDtypeStruct((nb, s, PW), BF16),
            jax.ShapeDtypeStruct((1, DV), F32),
            jax.ShapeDtypeStruct((8, CW), F32),
            jax.ShapeDtypeStruct((1, NQK), F32),
            jax.ShapeDtypeStruct((A_PAD, NQK), F32),
        ] + [jax.ShapeDtypeStruct((2,) + p.shape[1:], BF16) for p in pbs],
        scratch_shapes=[pltpu.VMEM((nb, NQK, DV), F32), pltpu.VMEM((nb, CHUNK, CW), F32)] + _stage2_scratch(pbs),
        compiler_params=_params(("arbitrary",)),
    )(proj3, proj3, z3, sprev, opre3, dmix3, gng, conv_w, wgu_p, *pbs)


SMALL_PACK_ROWS = 16


def _wgu_slot(r):
    return 4 + r // 4, NQK * (r % 4)


CONV_SLOTS = ((8, 0), (8, CW), (9, 0))


def _in_proj_bwd(dproj2d, x2d, dx1, g1, w_in_t, tm, pb, small_parts):
    t = x2d.shape[0]
    nt = t // tm

    def body(dp_ref, x_ref, dx1_ref, g_ref, w_ref, pb_ref, dg2, dgf, dbg, dgng, dwgu, dcw, lp,
             dx_ref, sums_ref, r2_ref, dg1_acc, pack, gbuf, pack1, gbuf1, ssend, srecv, ssend1, srecv1, *scratch2):
        stage2 = ([pb_ref], [r2_ref], scratch2)
        x, y, c = _position()
        me = 4 * x + 2 * y + c
        flips = [(k >> 2, (k >> 1) & 1, k & 1) for k in range(1, N_DEV)]
        peers = [(x ^ fx, y ^ fy, c ^ fc) for fx, fy, fc in flips]

        def small_copies(src, dst, send, recv, arrivals):
            return [pltpu.make_async_remote_copy(
                src_ref=src, dst_ref=dst.at[4 * px + 2 * py + pc if arrivals else me],
                send_sem=send.at[k], recv_sem=recv.at[k], device_id=(px, py, pc), device_id_type=MESH)
                for k, (px, py, pc) in enumerate(peers)]

        @pl.when(pl.program_id(0) == 0)
        def _():
            _stage2_start(*stage2)
            dg1_acc[...] = jnp.zeros_like(dg1_acc)
            pack[...] = jnp.zeros_like(pack)
            pack[1:2, :] = dg2[...]
            pack[2:3, :] = dgf[...]
            pack[3:4, 0:NQK] = dbg[...]
            pack[3:4, NQK:NQK + DV] = dgng[...]
            pack[3:4, NQK + DV:NQK + 2 * DV] = lp[...]
            for r in range(RANK):
                row, lane = _wgu_slot(r)
                pack[row:row + 1, lane:lane + NQK] = dwgu[r:r + 1, :]
            for r, (row, lane) in enumerate(CONV_SLOTS):
                pack[row:row + 1, lane:lane + CW] = dcw[r:r + 1, :]
            for cp in small_copies(pack, gbuf, ssend, srecv, False):
                cp.start()
            gbuf[me] = pack[...]

        @pl.when(pl.program_id(0) == RELAY_AT * nt // 8)
        def _():
            _stage2_combine(*stage2)

        xv = x_ref[...]
        r = lax.rsqrt(jnp.mean(xv * xv, axis=-1, keepdims=True) + EPS)
        n1 = xv * r
        dh = _mm(dp_ref[...], w_ref[...])
        dg1_acc[...] += jnp.sum(dh * n1, axis=0, keepdims=True)
        dn = dh * g_ref[...]
        dx_ref[...] = dx1_ref[...] + r * (dn - n1 * jnp.mean(dn * n1, axis=-1, keepdims=True))

        @pl.when(pl.program_id(0) == nt - 1)
        def _():
            pack1[...] = jnp.zeros_like(pack1)
            pack1[0:1, :] = dg1_acc[...]
            for cp in small_copies(pack1, gbuf1, ssend1, srecv1, False):
                cp.start()
            gbuf1[me] = pack1[...]
            _stage2_finish(*stage2)
            for src, dst, send, recv in ((pack, gbuf, ssend, srecv), (pack1, gbuf1, ssend1, srecv1)):
                for cp in small_copies(src, dst, send, recv, True):
                    cp.wait_recv()
                    cp.wait_send()
            acc = gbuf[0]
            acc1 = gbuf1[0]
            for d in range(1, N_DEV):
                acc = acc + gbuf[d]
                acc1 = acc1 + gbuf1[d]
            sums_ref[...] = acc
            sums_ref[0:1, :] = acc1[0:1, :]

    tile = lambda w: pl.BlockSpec((tm, w), lambda i: (i, 0))
    vec = pl.BlockSpec((1, D), lambda i: (0, 0))
    hbm = pl.BlockSpec(memory_space=pl.ANY)
    whole = lambda a: pl.BlockSpec(a.shape, lambda i: (0,) * a.ndim)
    return pl.pallas_call(
        body,
        name="in_proj_bwd",
        grid=(nt,),
        in_specs=[tile(PW), tile(D), tile(D), vec, pl.BlockSpec((PW, D), lambda i: (0, 0)), hbm]
        + [whole(a) for a in small_parts],
        out_specs=[tile(D), pl.BlockSpec((SMALL_PACK_ROWS, D), lambda i: (0, 0)), hbm],
        out_shape=[jax.ShapeDtypeStruct((t, D), F32), jax.ShapeDtypeStruct((SMALL_PACK_ROWS, D), F32),
                   jax.ShapeDtypeStruct((2,) + pb.shape[1:], BF16)],
        scratch_shapes=[pltpu.VMEM((1, D), F32),
                        pltpu.VMEM((SMALL_PACK_ROWS, D), F32), pltpu.VMEM((N_DEV, SMALL_PACK_ROWS, D), F32),
                        pltpu.VMEM((8, D), F32), pltpu.VMEM((N_DEV, 8, D), F32),
                        pltpu.SemaphoreType.DMA((7,)), pltpu.SemaphoreType.DMA((7,)),
                        pltpu.SemaphoreType.DMA((7,)), pltpu.SemaphoreType.DMA((7,))] + _stage2_scratch([pb]),
        compiler_params=_params(("arbitrary",)),
    )(dproj2d, x2d, dx1, g1, w_in_t, pb, *small_parts)


def _get_rows(ref):
    return ref[:, 0, :] if len(ref.shape) == 3 else ref[...]


def _put_rows(ref, val):
    if len(ref.shape) == 3:
        ref[:, 0, :] = val
    else:
        ref[...] = val


def _adamw_math(w, g, m, v):
    m = ADAM_B1 * m + (1.0 - ADAM_B1) * g
    v = ADAM_B2 * v + (1.0 - ADAM_B2) * (g * g)
    m_hat = m / (1.0 - ADAM_B1 ** ADAM_STEP)
    v_hat = v / (1.0 - ADAM_B2 ** ADAM_STEP)
    delta = -ADAM_LR * (m_hat / (jnp.sqrt(v_hat) + ADAM_EPS) + ADAM_WD * w)
    return delta, m, v


def _position():
    return lax.axis_index("x"), lax.axis_index("y"), lax.axis_index("c")


GATHER_PARTS = 2
GATHER_SEMS = 7 * GATHER_PARTS
RELAY_AT = 3
FORWARD_AT = 7


def _gather_copies(stage, lo, rows, gx, send_sems, recv_sems, local_sem):
    x, y, c = _position()
    me = (x, y, c)
    sibling = (x, y, 1 - c)
    chips = [(1 - x, y), (x, 1 - y), (1 - x, 1 - y)]
    part = -(-rows // (16 * GATHER_PARTS)) * 16
    bounds = [(p * part, min(part, rows - p * part)) for p in range(GATHER_PARTS)]

    def blk(px, py, pc, off, n):
        return gx.at[4 * px + 2 * py + pc, pl.ds(off, n), :]

    mine = pltpu.make_async_copy(stage.at[pl.ds(lo, rows), :], gx.at[4 * x + 2 * y + c], local_sem)
    parts = []
    for p, (off, n) in enumerate(bounds):
        def copy(k, block, to, from_stage=False, p=p, off=off, n=n):
            return pltpu.make_async_remote_copy(
                src_ref=stage.at[pl.ds(lo + off, n), :] if from_stage else blk(*block, off, n),
                dst_ref=blk(*block, off, n), send_sem=send_sems.at[7 * p + k], recv_sem=recv_sems.at[7 * p + k],
                device_id=to, device_id_type=MESH)

        first = [copy(0, me, sibling, True)] + [copy(1 + j, me, (*chips[j], c), True) for j in range(2)]
        relay = copy(3, (*chips[p], c), (*chips[1 - p], c))
        passed = [copy(4 + j, (*chip, c), sibling) for j, chip in enumerate(chips)]
        arrivals = ([copy(0, sibling, me)] + [copy(1 + j, (*chip, c), me) for j, chip in enumerate(chips)]
                    + [copy(4 + j, (*chip, 1 - c), me) for j, chip in enumerate(chips)])
        parts.append((first, relay, passed, arrivals))
    return mine, parts


def _gather_start(*args):
    mine, parts = _gather_copies(*args)
    mine.start()
    for first, _, _, _ in parts:
        first[0].start()
    for p, q in ((0, 0), (1, 1), (0, 1), (1, 0)):
        parts[p][0][1 + q].start()


def _gather_relay(*args):
    _, parts = _gather_copies(*args)
    for p, (_, relay, passed, arrivals) in enumerate(parts):
        arrivals[1 + p].wait_recv()
        relay.start()
        passed[p].start()


def _gather_forward(*args):
    _, parts = _gather_copies(*args)
    for p, j in ((0, 1), (1, 0), (0, 2), (1, 2)):
        _, _, passed, arrivals = parts[p]
        arrivals[1 + j].wait_recv()
        passed[j].start()


def _gather_finish(*args):
    mine, parts = _gather_copies(*args)
    for first, relay, passed, arrivals in parts:
        arrivals[0].wait_recv()
        for j in range(3):
            arrivals[4 + j].wait_recv()
        for cp in first + [relay] + passed:
            cp.wait_send()
    mine.wait()


def _gather_sems():
    return [pltpu.SemaphoreType.DMA((GATHER_SEMS,)), pltpu.SemaphoreType.DMA((GATHER_SEMS,)), pltpu.SemaphoreType.DMA]


def _gather_w_in(w_it, w_gt, w_ut, w_d, w_o, wgu_s, conv_s):
    def body(wi_hbm, wg_hbm, wu_hbm, wd_hbm, wo_hbm, wgu_ref, conv_ref, w_ref, gwgu_ref, gconv_ref, stage,
             buf, wif, wf, wof, send_sems, recv_sems, local_sem, ssend, srecv, load_sems):
        x, y, c = _position()
        me = 4 * x + 2 * y + c
        loads = [pltpu.make_async_copy(src, dst, load_sems.at[n]) for n, (src, dst) in enumerate(
            ((wi_hbm.at[:, 0, :], wif), (wg_hbm, wf.at[0]), (wu_hbm, wf.at[1]), (wd_hbm, wf.at[2]), (wo_hbm, wof)))]
        for cp in loads:
            cp.start()
        loads.pop(0).wait()
        stage[SLAB_IN:SLAB_IN + IN_W, :] = wif[...].astype(BF16)
        stage[SLAB_IN + IN_W:SLAB_G, :] = jnp.zeros((IN_ROWS - IN_W, D), BF16)
        args = (stage, SLAB_IN, IN_ROWS, buf, send_sems, recv_sems, local_sem)
        _gather_start(*args)
        for n, lo in enumerate((SLAB_G, SLAB_U, SLAB_D)):
            loads[n].wait()
            stage[lo:lo + FF_W, :] = wf[n].astype(BF16)
        loads[3].wait()
        stage[SLAB_O:SLAB_ROWS, :] = wof[...].astype(BF16)
        flips = [(k >> 2, (k >> 1) & 1, k & 1) for k in range(1, N_DEV)]
        peers = [(x ^ fx, y ^ fy, c ^ fc) for fx, fy, fc in flips]

        def small(k, block_id, to):
            return [pltpu.make_async_remote_copy(
                src_ref=s, dst_ref=g.at[block_id], send_sem=ssend.at[2 * k + n], recv_sem=srecv.at[2 * k + n],
                device_id=to, device_id_type=MESH)
                for n, (s, g) in enumerate(((wgu_ref, gwgu_ref), (conv_ref, gconv_ref)))]

        gwgu_ref[me] = wgu_ref[...]
        gconv_ref[me] = conv_ref[...]
        for k, peer in enumerate(peers):
            for cp in small(k, me, peer):
                cp.start()
        w_ref[IN_COLS:PW, :] = jnp.zeros((PW - IN_COLS, D), BF16)
        _gather_relay(*args)
        _gather_forward(*args)
        _gather_finish(*args)
        for k, (px, py, pc) in enumerate(peers):
            for cp in small(k, 4 * px + 2 * py + pc, (px, py, pc)):
                cp.wait_recv()
                cp.wait_send()
        for j, lo, hi, d in _in_segments():
            w_ref[d:d + hi - lo, :] = buf[j, lo:hi, :]

    vm = pl.BlockSpec(memory_space=pltpu.VMEM)
    return pl.pallas_call(
        body,
        name="gather_w_in",
        in_specs=[pl.BlockSpec(memory_space=pl.ANY)] * 5 + [vm] * 2,
        out_specs=[vm] * 4,
        out_shape=[jax.ShapeDtypeStruct((PW, D), BF16),
                   jax.ShapeDtypeStruct((N_DEV,) + wgu_s.shape, F32),
                   jax.ShapeDtypeStruct((N_DEV,) + conv_s.shape, F32),
                   jax.ShapeDtypeStruct((SLAB_ROWS, D), BF16)],
        scratch_shapes=[pltpu.VMEM((N_DEV, IN_ROWS, D), BF16), pltpu.VMEM((IN_W, D), F32),
                        pltpu.VMEM((3, FF_W, D), F32), pltpu.VMEM((OUT_ROWS, D), F32)] + _gather_sems()
        + [pltpu.SemaphoreType.DMA((14,)), pltpu.SemaphoreType.DMA((14,)), pltpu.SemaphoreType.DMA((5,))],
        compiler_params=_params(),
    )(w_it, w_gt, w_ut, w_d, w_o, wgu_s, conv_s)


def _w_in_core_reduce(dw_t):
    def body(d_ref, own_ref, sib_ref, pb_ref, g, gb, r1, send_sems, recv_sems):
        x, y, c = _position()
        chip = 2 * x + y
        for j in range(N_DEV):
            g[j, IN_W:IN_ROWS, :] = jnp.zeros((IN_ROWS - IN_W, D), F32)
        for j, lo, hi, d in _in_segments():
            g[j, lo:hi, :] = d_ref[d:d + hi - lo, :]
        for j in range(N_DEV):
            gb[j] = g[j].astype(BF16)
        copies = _stage1_copies(gb, r1, send_sems, recv_sems)
        for cp in copies:
            cp.start()
        own_ref[0] = g[2 * chip + c]
        for cp in copies:
            cp.wait_recv()
        sib_ref[0] = r1[chip]
        for k in range(1, 4):
            t = chip ^ k
            pb_ref[k - 1] = (g[2 * t + c] + r1[t].astype(F32)).astype(BF16)
        for cp in copies:
            cp.wait_send()

    vm = pl.BlockSpec(memory_space=pltpu.VMEM)
    return pl.pallas_call(
        body,
        name="w_in_core_reduce",
        in_specs=[vm],
        out_specs=[vm, vm, vm],
        out_shape=[jax.ShapeDtypeStruct((1, IN_ROWS, D), F32), jax.ShapeDtypeStruct((1, IN_ROWS, D), BF16),
                   jax.ShapeDtypeStruct((3, IN_ROWS, D), BF16)],
        scratch_shapes=[pltpu.VMEM((N_DEV, IN_ROWS, D), F32), pltpu.VMEM((N_DEV, IN_ROWS, D), BF16),
                        pltpu.VMEM((4, IN_ROWS, D), BF16), pltpu.SemaphoreType.DMA((4,)),
                        pltpu.SemaphoreType.DMA((4,))],
        compiler_params=_params(),
    )(dw_t)


def _stage1_copies(g_ref, r_ref, send_sems, recv_sems):
    x, y, c = _position()
    return [pltpu.make_async_remote_copy(
        src_ref=g_ref.at[2 * i + 1 - c], dst_ref=r_ref.at[i], send_sem=send_sems.at[i], recv_sem=recv_sems.at[i],
        device_id=(x, y, 1 - c), device_id_type=MESH) for i in range(4)]


def _ffn_core_reduce(dw3, dwb3, dw_o, dwb_o, pos_arr, dx1b, gwb):
    def body(pos_ref, g0, g1, g2, go, gb3_hbm, gbo_hbm, dx1b_ref, gwb_hbm, p0, p1, p2, po, s0, s1, s2, so, dmix_ref,
             r1f, r1o, wo, send_sems, recv_sems, wsem):
        step = pl.program_id(0)
        k = jnp.minimum(step, 2)
        x, y, c = _position()
        chip = 2 * x + y

        def copies(p):
            src = 2 * (chip ^ ((p + 1) & 3)) + 1 - c
            pairs = [(gb3_hbm.at[a, src], r1f.at[a, p]) for a in range(3)] + [(gbo_hbm.at[src], r1o.at[p])]
            return [pltpu.make_async_remote_copy(
                src_ref=s, dst_ref=d, send_sem=send_sems.at[4 * p + a], recv_sem=recv_sems.at[4 * p + a],
                device_id=(x, y, 1 - c), device_id_type=MESH) for a, (s, d) in enumerate(pairs)]

        @pl.when(step == 0)
        def _():
            for p in range(4):
                for cp in copies(p):
                    cp.start()
            loads = [pltpu.make_async_copy(gwb_hbm.at[j, pl.ds(D_TAIL, OUT_ROWS), :],
                                           wo.at[pl.ds(OUT_ROWS * j, OUT_ROWS), :], wsem.at[j]) for j in range(N_DEV)]
            for cp in loads:
                cp.start()
            for cp in loads:
                cp.wait()

        dmix_ref[...] = _nt(dx1b_ref[...], wo[...])

        for p in range(3):
            @pl.when(step == p)
            def _():
                for cp in copies(p):
                    cp.wait_recv()

        for a, (g, pb) in enumerate(((g0, p0), (g1, p1), (g2, p2))):
            pb[...] = (g[...] + r1f[a, k][None].astype(F32)).astype(BF16)
        po[...] = (go[...] + r1o[k][None].astype(F32)).astype(BF16)

        @pl.when(step == 3)
        def _():
            for cp in copies(3):
                cp.wait_recv()
            for a, s in enumerate((s0, s1, s2)):
                s[0] = r1f[a, 3]
            so[0] = r1o[3]
            for p in range(4):
                for cp in copies(p):
                    cp.wait_send()

    t = dx1b.shape[0]
    other = lambda s, pos: 2 * (pos[1] ^ (jnp.minimum(s, 2) + 1)) + pos[0]
    g_spec = lambda lead: pl.BlockSpec((None, 1, FF_W, D), lambda s, pos: (lead, other(s, pos), 0, 0))
    slot = lambda rows: pl.BlockSpec((1, rows, D), lambda s, pos: (jnp.minimum(s, 2), 0, 0))
    one = lambda rows: pl.BlockSpec((1, rows, D), lambda s, pos: (0, 0, 0))
    quarter = pl.BlockSpec((t // 4, D), lambda s, pos: (s, 0))
    hbm = pl.BlockSpec(memory_space=pl.ANY)
    return pl.pallas_call(
        body,
        name="ffn_core_reduce",
        grid_spec=pltpu.PrefetchScalarGridSpec(
            num_scalar_prefetch=1, grid=(4,),
            in_specs=[g_spec(0), g_spec(1), g_spec(2),
                      pl.BlockSpec((1, OUT_ROWS, D), lambda s, pos: (other(s, pos), 0, 0)), hbm, hbm, quarter, hbm],
            out_specs=[slot(FF_W), slot(FF_W), slot(FF_W), slot(OUT_ROWS),
                       one(FF_W), one(FF_W), one(FF_W), one(OUT_ROWS), quarter],
            scratch_shapes=[pltpu.VMEM((3, 4, FF_W, D), BF16), pltpu.VMEM((4, OUT_ROWS, D), BF16),
                            pltpu.VMEM((D, D), BF16), pltpu.SemaphoreType.DMA((16,)),
                            pltpu.SemaphoreType.DMA((16,)), pltpu.SemaphoreType.DMA((N_DEV,))]),
        out_shape=[jax.ShapeDtypeStruct((3, FF_W, D), BF16)] * 3 + [jax.ShapeDtypeStruct((3, OUT_ROWS, D), BF16)]
        + [jax.ShapeDtypeStruct((1, FF_W, D), BF16)] * 3 + [jax.ShapeDtypeStruct((1, OUT_ROWS, D), BF16),
                                                             jax.ShapeDtypeStruct((t, D), F32)],
        compiler_params=_params(("arbitrary",)),
    )(pos_arr, dw3, dw3, dw3, dw_o, dwb3, dwb_o, dx1b, gwb)


def _stage2_scratch(pbs):
    n = len(pbs)
    return ([pltpu.VMEM(p.shape[1:], BF16) for p in pbs] * 2
            + [pltpu.SemaphoreType.DMA((6 * n,)), pltpu.SemaphoreType.DMA((6 * n,)), pltpu.SemaphoreType.DMA((2 * n,))])


def _stage2_copies(p_refs, r_refs, scratch):
    n = len(p_refs)
    owns, gots = scratch[:n], scratch[n:2 * n]
    send_sems, recv_sems, load_sems = scratch[2 * n:]
    x, y, c = _position()
    xn, yn = (1 - x, y, c), (x, 1 - y, c)
    loads, first, second = [], [], []
    for a, (p, r, own, got) in enumerate(zip(p_refs, r_refs, owns, gots)):
        rows = p.shape[1]
        half = -(-rows // 32) * 16
        h0, h1 = pl.ds(0, half), pl.ds(half, rows - half)

        def remote(k, src, dst, to, a=a):
            return pltpu.make_async_remote_copy(
                src_ref=src, dst_ref=dst, send_sem=send_sems.at[6 * a + k], recv_sem=recv_sems.at[6 * a + k],
                device_id=to, device_id_type=MESH)

        loads += [pltpu.make_async_copy(p.at[0, h0, :], own.at[h0, :], load_sems.at[2 * a]),
                  pltpu.make_async_copy(p.at[1, h1, :], own.at[h1, :], load_sems.at[2 * a + 1])]
        first += [remote(0, p.at[2, h0, :], got.at[h0, :], xn), remote(1, p.at[2, h1, :], got.at[h1, :], yn),
                  remote(2, p.at[1, h0, :], r.at[1, h0, :], xn), remote(3, p.at[0, h1, :], r.at[0, h1, :], yn)]
        second += [remote(4, own.at[h0, :], r.at[0, h0, :], yn), remote(5, own.at[h1, :], r.at[1, h1, :], xn)]
    return loads, first, second


def _stage2_start(p_refs, r_refs, scratch):
    loads, first, _ = _stage2_copies(p_refs, r_refs, scratch)
    for cp in loads:
        cp.start()
    for k in range(4):
        for cp in first[k::4]:
            cp.start()


def _stage2_combine(p_refs, r_refs, scratch):
    n = len(p_refs)
    loads, first, second = _stage2_copies(p_refs, r_refs, scratch)
    for a in range(n):
        for cp in loads[2 * a:2 * a + 2]:
            cp.wait()
        for cp in first[4 * a:4 * a + 2]:
            cp.wait_recv()
        own, got = scratch[a], scratch[n + a]
        own[...] = (own[...].astype(F32) + got[...].astype(F32)).astype(BF16)
        for cp in second[2 * a:2 * a + 2]:
            cp.start()


def _stage2_finish(p_refs, r_refs, scratch):
    _, first, second = _stage2_copies(p_refs, r_refs, scratch)
    for a in range(len(p_refs)):
        for cp in first[4 * a + 2:4 * a + 4] + second[2 * a:2 * a + 2]:
            cp.wait_recv()
    for cp in first + second:
        cp.wait_send()


def _finish_weights(items, pos_arr, name, nblk):
    n = len(items)
    in_specs, out_specs, out_shape, operands, wbs = [], [], [], [], []
    for g8, lead, r1, r2, w, m, v in items:
        rows, wr = g8.shape[-2], w.shape[0]
        assert rows % nblk == 0 and wr % nblk == 0 and (nblk == 1 or (rows == wr and rows % (16 * nblk) == 0))
        rb, wb = rows // nblk, wr // nblk
        if lead is not None:
            g_spec = pl.BlockSpec((None, 1, rb, D), lambda i, pos, lead=lead: (lead, 2 * pos[1] + pos[0], i, 0))
        elif g8.shape[0] == 1:
            g_spec = pl.BlockSpec((1, rb, D), lambda i, pos: (0, i, 0))
        else:
            g_spec = pl.BlockSpec((1, rb, D), lambda i, pos: (2 * pos[1] + pos[0], i, 0))
        r1_spec = pl.BlockSpec((1, rb, D), lambda i, pos: (0, i, 0))
        if w.ndim == 3:
            wblk = pl.BlockSpec((wb, 1, D), lambda i, pos: (i, 0, 0))
        else:
            wblk = pl.BlockSpec((wb, D), lambda i, pos: (i, 0))
        in_specs += [g_spec, r1_spec, pl.BlockSpec((2, rb, D), lambda i, pos: (0, i, 0)), wblk, wblk, wblk]
        out_specs += [wblk] * 4
        out_shape += [jax.ShapeDtypeStruct(w.shape, F32)] * 4
        operands += [g8, r1, r2, w, m, v]
        wbs.append(wb)

    def body(pos_ref, *refs):
        for a in range(n):
            g_ref, r1_ref, r2_ref, w_ref, m_ref, v_ref = refs[6 * a:6 * a + 6]
            g_out, d_out, m_out, v_out = refs[6 * n + 4 * a:6 * n + 4 * a + 4]
            g = g_ref[0] + r1_ref[0].astype(F32)
            for k in range(2):
                g = g + r2_ref[k].astype(F32)
            g = g[0:wbs[a], :]
            d, mn, vn = _adamw_math(_get_rows(w_ref), g, _get_rows(m_ref), _get_rows(v_ref))
            for out, val in ((g_out, g), (d_out, d), (m_out, mn), (v_out, vn)):
                _put_rows(out, val)

    return pl.pallas_call(
        body,
        name=name,
        grid_spec=pltpu.PrefetchScalarGridSpec(
            num_scalar_prefetch=1, grid=(nblk,), in_specs=in_specs, out_specs=out_specs),
        out_shape=out_shape,
        compiler_params=_params(("arbitrary",)),
    )(pos_arr, *operands)


SMALL_NAMES = ("norm1_g", "norm2_g", "norm_f_g", "b_gate", "gla_norm_g", "w_gate_up", "conv_w")
WGU_W = NQK // N_DEV
CONV_W = CW // N_DEV


def _small_adamw(sums, ws, ms, vs):
    n = len(SMALL_NAMES)

    def body(*refs):
        acc_ref = refs[0]
        w_refs, m_refs, v_refs = refs[1:1 + n], refs[1 + n:1 + 2 * n], refs[1 + 2 * n:1 + 3 * n]
        loss_ref = refs[1 + 3 * n]
        outs = refs[2 + 3 * n:]
        x, y, c = _position()
        me = 4 * x + 2 * y + c
        acc = acc_ref[...]
        loss_ref[...] = acc[3:4, NQK + DV:NQK + DV + 1]

        def my_columns(full, width):
            r = lax.broadcasted_iota(jnp.int32, (full.shape[1], width), 0)
            col = lax.broadcasted_iota(jnp.int32, (full.shape[1], width), 1)
            sel = (r == width * me + col).astype(F32)
            return _mm(full, sel, precision=HIGHEST)

        dwgu = jnp.concatenate([acc[row:row + 1, lane:lane + NQK] for row, lane in map(_wgu_slot, range(RANK))], axis=0)
        dcw = jnp.concatenate([acc[row:row + 1, lane:lane + CW] for row, lane in CONV_SLOTS], axis=0)
        grads = [acc[0:1, :], acc[1:2, :], acc[2:3, :], acc[3:4, 0:NQK], acc[3:4, NQK:NQK + DV],
                 my_columns(dwgu, WGU_W), my_columns(dcw, CONV_W)]
        for i, g in enumerate(grads):
            d, mn, vn = _adamw_math(_get_rows(w_refs[i]), g, _get_rows(m_refs[i]), _get_rows(v_refs[i]))
            for out, val in zip(outs[4 * i:4 * i + 4], (g, d, mn, vn)):
                _put_rows(out, val)

    vm = pl.BlockSpec(memory_space=pltpu.VMEM)
    out_shape = [jax.ShapeDtypeStruct((1, 1), F32)]
    for w in ws:
        out_shape += [jax.ShapeDtypeStruct(w.shape, F32)] * 4
    return pl.pallas_call(
        body,
        name="small_adamw",
        in_specs=[vm] * (1 + 3 * n),
        out_specs=[vm] * (1 + 4 * n),
        out_shape=out_shape,
        compiler_params=_params(),
    )(sums, *ws, *ms, *vs)


def kernel(x, norm1_g, w_in, w_gate_up, b_gate, gla_norm_g, conv_w, w_out, norm2_g, w_ffn_gate, w_ffn_up, w_ffn_down, norm_f_g, loss_target, m_norm1_g, m_w_in, m_w_gate_up, m_b_gate, m_gla_norm_g, m_conv_w, m_w_out, m_norm2_g, m_w_ffn_gate, m_w_ffn_up, m_w_ffn_down, m_norm_f_g, v_norm1_g, v_w_in, v_w_gate_up, v_b_gate, v_gla_norm_g, v_conv_w, v_w_out, v_norm2_g, v_w_ffn_gate, v_w_ffn_up, v_w_ffn_down, v_norm_f_g):
    xi, yi, ci = _position()
    pos_arr = jnp.stack([ci, 2 * xi + yi]).astype(jnp.int32)
    nb, s, _ = x.shape
    t = nb * s

    tr = lambda a: a[0].T
    rows_of = lambda a: a.transpose(2, 0, 1)
    conv_rows = lambda a: a.transpose(1, 0, 2)
    w_in_t, gwgu, gconv, stage = _gather_w_in(rows_of(w_in), tr(w_ffn_gate), tr(w_ffn_up), w_ffn_down[0], w_out[0],
                                              w_gate_up[0], conv_rows(conv_w))
    wgu_f = gwgu.transpose(1, 0, 2).reshape(RANK, NQK)
    conv_f = gconv.transpose(1, 2, 0, 3).reshape(CONV_K, CW)
    wgu_p = jnp.concatenate([wgu_f, jnp.zeros((A_PAD - RANK, NQK), F32)], axis=0).astype(BF16)

    x2d = x.reshape(t, D)
    tgt2d = loss_target.reshape(t, D)
    tm = 256
    tm_in = min(512, t)
    tk = min(2048, t)
    proj, z, h, gwb = _in_proj_fwd(x2d, norm1_g, w_in_t, wgu_p, b_gate, tm_in, stage)
    proj3 = proj.reshape(nb, s, PW)
    z3 = z.reshape(nb, s, NQK)
    mix3, opre3, sprev, x1, gwa = _mix_fwd(proj3, z3, gla_norm_g, conv_f, stage, x, gwb)
    mix2d = mix3.reshape(t, D)
    dx1, dx1b, adu, hb, dg2, dgf, loss_part = _ffn_fwd_bwd(
        x1.reshape(t, D), tgt2d, gwa, gwb, norm2_g, norm_f_g.reshape(1, D), tm)
    dw3, dwb3 = _dw_ffn(adu, hb, tk)
    dw3 = dw3.reshape(3, N_DEV, FF_W, D)
    dw_o, dwb_o = _tn_matmul(mix2d, dx1b, D // 2, D, tk, "dw_out", True)
    dw_o = dw_o.reshape(N_DEV, OUT_ROWS, D)
    *pb, sib_d, sib_g, sib_u, sib_o, dmix = _ffn_core_reduce(
        dw3, dwb3.reshape(3, N_DEV, FF_W, D), dw_o, dwb_o.reshape(N_DEV, OUT_ROWS, D), pos_arr, dx1b, gwb)
    g8 = [dw3, dw3, dw3, dw_o]
    leads = [0, 1, 2, None]
    tags = ("w_ffn_down", "w_ffn_gate", "w_ffn_up", "w_out")
    r1 = [sib_d, sib_g, sib_u, sib_o]
    mb = _mix_bwd(proj3, z3, sprev, opre3, dmix.reshape(nb, s, D), gla_norm_g, conv_f, wgu_p, [pb[0], pb[1], pb[3]])
    dproj3, dgng, dcw, dbg, dwgu = mb[:5]
    dproj2d = dproj3.reshape(t, PW)
    dw_in_t, r2_up = _tn_matmul(dproj2d, h, PW // 5, D, t, "dw_in", False, _stage2_rider([pb[2]]))
    r2 = [mb[5], mb[6], r2_up, mb[7]]
    g_in, r1_in, pb_in = _w_in_core_reduce(dw_in_t)
    dx, small_sums, r2_in = _in_proj_bwd(dproj2d, x2d, dx1, norm1_g, w_in_t, tm_in, pb_in,
                                         (dg2, dgf, dbg, dgng, dwgu, dcw, loss_part))

    tags = ("w_in",) + tags
    g8 = [g_in] + g8
    leads = [None] + leads
    r1 = [r1_in] + list(r1)
    r2 = [r2_in] + r2
    shard_w = (rows_of(w_in), w_ffn_down[0], tr(w_ffn_gate), tr(w_ffn_up), w_out[0])
    shard_m = (rows_of(m_w_in), m_w_ffn_down[0], tr(m_w_ffn_gate), tr(m_w_ffn_up), m_w_out[0])
    shard_v = (rows_of(v_w_in), v_w_ffn_down[0], tr(v_w_ffn_gate), tr(v_w_ffn_up), v_w_out[0])
    back = (lambda o: o.transpose(1, 2, 0), lambda o: o[None], lambda o: o.T[None], lambda o: o.T[None],
            lambda o: o[None])
    items = list(zip(g8, leads, r1, r2, shard_w, shard_m, shard_v))
    flat = list(_finish_weights(items[1:], pos_arr, "finish_ffn_out", 2))
    flat = list(_finish_weights(items[:1], pos_arr, "finish_w_in", 1)) + flat
    results = {}
    for i, (tag, to_shard) in enumerate(zip(tags, back)):
        results[tag] = [to_shard(o) for o in flat[4 * i:4 * i + 4]]

    small_w = (norm1_g, norm2_g, norm_f_g.reshape(1, D), b_gate, gla_norm_g, w_gate_up[0], conv_rows(conv_w))
    small_m = (m_norm1_g, m_norm2_g, m_norm_f_g.reshape(1, D), m_b_gate, m_gla_norm_g, m_w_gate_up[0],
               conv_rows(m_conv_w))
    small_v = (v_norm1_g, v_norm2_g, v_norm_f_g.reshape(1, D), v_b_gate, v_gla_norm_g, v_w_gate_up[0],
               conv_rows(v_conv_w))
    so = _small_adamw(small_sums, small_w, small_m, small_v)
    loss = so[0].reshape(())
    to_shape = {"norm_f_g": lambda o: o.reshape(D), "w_gate_up": lambda o: o[None],
                "conv_w": lambda o: o.transpose(1, 0, 2)}
    for i, name in enumerate(SMALL_NAMES):
        results[name] = [to_shape.get(name, lambda o: o)(o) for o in so[1 + 4 * i:5 + 4 * i]]

    names = ("norm1_g", "w_in", "w_gate_up", "b_gate", "gla_norm_g", "conv_w", "w_out", "norm2_g",
             "w_ffn_gate", "w_ffn_up", "w_ffn_down", "norm_f_g")
    outs = [loss, dx.reshape(nb, s, D)]
    for kind in range(4):
        for name in names:
            outs.append(results[name][kind])
    return tuple(outs)
```

```python
import jax
import jax.numpy as jnp
from jax import lax
from jax.experimental import pallas as pl
from jax.experimental.pallas import tpu as pltpu

F32 = jnp.float32
BF16 = jnp.bfloat16
HIGHEST = lax.Precision.HIGHEST
MESH = pl.DeviceIdType.MESH

N_DEV = 8
D = 1024
DFF = 2816
HEADS = 4
DK = 64
DV = 128
NQK = HEADS * DK
NV = HEADS * DV
RANK = 16
CHUNK = 64
CW = 512
CONV_K = 3
IN_COLS = 3088
EPS = 1e-6
INV_GATE_NORM = 1.0 / 16.0
Q_SCALE = DK ** -0.5

PW = 3200
OQ, OK_, OV, OG, OCB, OCC, OCH, OA = 0, 256, 512, 1024, 1536, 2048, 2560, 3072
A_PAD = 128

ADAM_LR = 0.001
ADAM_B1 = 0.9
ADAM_B2 = 0.999
ADAM_EPS = 1e-08
ADAM_WD = 0.01
ADAM_STEP = 10

IN_W = IN_COLS // N_DEV
IN_ROWS = 400
FF_W = DFF // N_DEV
OUT_ROWS = D // N_DEV
SLAB_IN = 0
SLAB_G = SLAB_IN + IN_ROWS
SLAB_U = SLAB_G + FF_W
SLAB_D = SLAB_U + FF_W
SLAB_O = SLAB_D + FF_W
SLAB_ROWS = SLAB_O + OUT_ROWS
D_HEAD = 128
D_TAIL = FF_W - D_HEAD
SLAB_SPLIT = SLAB_D + D_HEAD

VMEM_LIMIT = 56 * 1024 * 1024


def _params(sem=None, vmem=VMEM_LIMIT):
    return pltpu.CompilerParams(dimension_semantics=sem, vmem_limit_bytes=vmem)


def _nt(a, b):
    return lax.dot_general(a, b, (((1,), (1,)), ((), ())), preferred_element_type=F32)


def _tn(a, b, precision=None):
    return lax.dot_general(a, b, (((0,), (0,)), ((), ())), preferred_element_type=F32, precision=precision)


def _mm(a, b, precision=None):
    return jnp.dot(a, b, preferred_element_type=F32, precision=precision)


def _in_segments():
    segs = []
    for j in range(N_DEV):
        lo, hi = IN_W * j, IN_W * (j + 1)
        cuts = sorted({lo, hi} | {c for c in (OCB, OCB + RANK) if lo < c < hi})
        for a, b in zip(cuts[:-1], cuts[1:]):
            if a < OCB:
                d = a
            elif a < OCB + RANK:
                d = OA + (a - OCB)
            else:
                d = a - RANK
            segs.append((j, a - lo, b - lo, d))
    return segs


def _in_proj_fwd(x2d, g1, w_in_t, wgu_p, b_gate, tm, stage):
    t = x2d.shape[0]
    nt = t // tm
    g_rows = SLAB_ROWS - SLAB_SPLIT

    def body(x_ref, g_ref, w_ref, wgu_ref, bg_ref, stage_hbm, proj_ref, z_ref, h_ref, gwb_ref,
             send_sems, recv_sems, local_sem):
        gargs = (stage_hbm, SLAB_SPLIT, g_rows, gwb_ref, send_sems, recv_sems, local_sem)

        @pl.when(pl.program_id(0) == 0)
        def _():
            _gather_start(*gargs)

        @pl.when(pl.program_id(0) == RELAY_AT * nt // 8)
        def _():
            _gather_relay(*gargs)

        @pl.when(pl.program_id(0) == FORWARD_AT * nt // 8)
        def _():
            _gather_forward(*gargs)

        x = x_ref[...]
        r = lax.rsqrt(jnp.mean(x * x, axis=-1, keepdims=True) + EPS)
        h = ((x * r) * g_ref[...]).astype(BF16)
        h_ref[...] = h
        proj = _nt(h, w_ref[...])
        proj_ref[...] = proj
        pa = proj[:, OA:OA + A_PAD].astype(BF16)
        z_ref[...] = _mm(pa, wgu_ref[...]) + bg_ref[...]

        @pl.when(pl.program_id(0) == nt - 1)
        def _():
            _gather_finish(*gargs)

    return pl.pallas_call(
        body,
        name="in_proj_fwd",
        grid=(t // tm,),
        in_specs=[
            pl.BlockSpec((tm, D), lambda i: (i, 0)),
            pl.BlockSpec((1, D), lambda i: (0, 0)),
            pl.BlockSpec((PW, D), lambda i: (0, 0)),
            pl.BlockSpec((A_PAD, NQK), lambda i: (0, 0)),
            pl.BlockSpec((1, NQK), lambda i: (0, 0)),
            pl.BlockSpec(memory_space=pl.ANY),
        ],
        out_specs=[
            pl.BlockSpec((tm, PW), lambda i: (i, 0)),
            pl.BlockSpec((tm, NQK), lambda i: (i, 0)),
            pl.BlockSpec((tm, D), lambda i: (i, 0)),
            pl.BlockSpec(memory_space=pl.ANY),
        ],
        out_shape=[
            jax.ShapeDtypeStruct((t, PW), F32),
            jax.ShapeDtypeStruct((t, NQK), F32),
            jax.ShapeDtypeStruct((t, D), BF16),
            jax.ShapeDtypeStruct((N_DEV, g_rows, D), BF16),
        ],
        scratch_shapes=_gather_sems(),
        compiler_params=_params(("arbitrary",)),
    )(x2d, g1, w_in_t, wgu_p, b_gate, stage)


def _head_masks():
    lane = lax.broadcasted_iota(jnp.int32, (1, NQK), 1)
    return [(lane >= DK * h) & (lane < DK * (h + 1)) for h in range(HEADS)]


def _split_bf16(x, n):
    parts = []
    for _ in range(n):
        p = x.astype(BF16)
        parts.append(p)
        x = x - p.astype(F32)
    return parts


def _chunk_fwd_parts(q, k, z, tril16):
    la = (jnp.minimum(z, 0.0) - jnp.log1p(jnp.exp(-jnp.abs(z)))) * INV_GATE_NORM
    la_parts = _split_bf16(la, 3)
    bc = _mm(tril16, la_parts[0]) + _mm(tril16, la_parts[1]) + _mm(tril16, la_parts[2])
    bl = bc[CHUNK - 1:CHUNK, :]
    eb = jnp.exp(bc)
    enb = jnp.exp(-bc)
    ekl = jnp.exp(bl - bc)
    qi = (q * Q_SCALE) * eb
    ki = k * enb
    ks = k * ekl
    ones16 = jnp.ones((CHUNK, DV), BF16)
    decb = jnp.exp(_tn(la_parts[0], ones16) + _tn(la_parts[1], ones16) + _tn(la_parts[2], ones16))
    return la, eb, enb, ekl, qi, ki, ks, decb


def _stack_heads(a, masks):
    return jnp.concatenate([jnp.where(m, a, 0.0) for m in masks], axis=0)


def _merge_heads(blocks, masks):
    out = blocks[HEADS - 1]
    for h in range(HEADS - 2, -1, -1):
        out = jnp.where(masks[h], blocks[h], out)
    return out


def _causal_stack_mask():
    row = lax.broadcasted_iota(jnp.int32, (HEADS * CHUNK, CHUNK), 0)
    col = lax.broadcasted_iota(jnp.int32, (HEADS * CHUNK, CHUNK), 1)
    return (row & (CHUNK - 1)) >= col


def _conv_taps(u, uprev):
    row = lax.broadcasted_iota(jnp.int32, u.shape, 0)
    u1 = jnp.where(row < 1, pltpu.roll(uprev, 1, 0), pltpu.roll(u, 1, 0))
    u2 = jnp.where(row < 2, pltpu.roll(uprev, 2, 0), pltpu.roll(u, 2, 0))
    return u1, u2


def _mix_fwd(proj3, z3, gng, conv_w, stage, x3, gwb):
    nb, s, _ = proj3.shape
    nc = s // CHUNK
    g_rows = SLAB_SPLIT - SLAB_G

    def body(p_ref, z_ref, gng_ref, cw_ref, stage_hbm, x_ref, gwb_hbm, mix_ref, o_ref, sprev_ref, x1_ref, gwa_ref,
             s_ref, uprev_ref, wo, wsem, send_sems, recv_sems, local_sem):
        n = pl.program_id(0)
        gargs = (stage_hbm, SLAB_G, g_rows, gwa_ref, send_sems, recv_sems, local_sem)

        @pl.when(n == 0)
        def _():
            _gather_start(*gargs)
            loads = [pltpu.make_async_copy(gwb_hbm.at[j, pl.ds(D_TAIL, OUT_ROWS), :],
                                           wo.at[pl.ds(OUT_ROWS * j, OUT_ROWS), :], wsem.at[j]) for j in range(N_DEV)]
            for cp in loads:
                cp.start()
            s_ref[...] = jnp.zeros_like(s_ref)
            uprev_ref[...] = jnp.zeros_like(uprev_ref)
            for cp in loads:
                cp.wait()

        @pl.when(n == RELAY_AT * nc // 8)
        def _():
            _gather_relay(*gargs)

        @pl.when(n == FORWARD_AT * nc // 8)
        def _():
            _gather_forward(*gargs)

        r_i = lax.broadcasted_iota(jnp.int32, (CHUNK, CHUNK), 0)
        c_i = lax.broadcasted_iota(jnp.int32, (CHUNK, CHUNK), 1)
        tril16 = (r_i >= c_i).astype(BF16)
        masks = _head_masks()
        cmask = _causal_stack_mask()
        gg = gng_ref[...]
        for b in range(nb):
            q = p_ref[b, :, OQ:OQ + NQK]
            k = p_ref[b, :, OK_:OK_ + NQK]
            _, _, _, _, qi, ki, ks, decb = _chunk_fwd_parts(q, k, z_ref[b], tril16)
            qs = _stack_heads(qi, masks).astype(BF16)
            sc = jnp.where(cmask, _nt(qs, ki.astype(BF16)), 0.0).astype(BF16)
            st = s_ref[b]
            sprev_ref[b, 0] = st
            o_inter = _mm(qs, st.astype(BF16))
            v16 = p_ref[b, :, OV:OV + NV].astype(BF16)
            kv = _tn(ks.astype(BF16), v16)
            for h in range(HEADS):
                rows = slice(CHUNK * h, CHUNK * (h + 1))
                cols = slice(DV * h, DV * (h + 1))
                o = _mm(sc[rows], v16[:, cols]) + o_inter[rows]
                o_ref[b, :, cols] = o
                r = lax.rsqrt(jnp.mean(o * o, axis=-1, keepdims=True) + EPS)
                on = (o * r) * gg
                g = p_ref[b, :, OG + DV * h:OG + DV * (h + 1)]
                mix_ref[b, :, cols] = (on * (g * jax.nn.sigmoid(g))).astype(BF16)
                s_ref[b, rows, :] = decb[rows] * st[rows] + kv[rows, cols]
            u = p_ref[b, :, OCC:OCC + CW] * p_ref[b, :, OCH:OCH + CW]
            u1, u2 = _conv_taps(u, uprev_ref[b])
            yc = cw_ref[0:1, :] * u2 + cw_ref[1:2, :] * u1 + cw_ref[2:3, :] * u
            mix_ref[b, :, NV:NV + CW] = (p_ref[b, :, OCB:OCB + CW] * yc).astype(BF16)
            uprev_ref[b] = u
        mixed = _mm(jnp.concatenate([mix_ref[b] for b in range(nb)], axis=0), wo[...])
        for b in range(nb):
            x1_ref[b] = x_ref[b] + mixed[CHUNK * b:CHUNK * (b + 1)]

        @pl.when(n == nc - 1)
        def _():
            _gather_finish(*gargs)

    return pl.pallas_call(
        body,
        name="mix_fwd",
        grid=(nc,),
        in_specs=[
            pl.BlockSpec((nb, CHUNK, PW), lambda n: (0, n, 0)),
            pl.BlockSpec((nb, CHUNK, NQK), lambda n: (0, n, 0)),
            pl.BlockSpec((1, DV), lambda n: (0, 0)),
            pl.BlockSpec((CONV_K, CW), lambda n: (0, 0)),
            pl.BlockSpec(memory_space=pl.ANY),
            pl.BlockSpec((nb, CHUNK, D), lambda n: (0, n, 0)),
            pl.BlockSpec(memory_space=pl.ANY),
        ],
        out_specs=[
            pl.BlockSpec((nb, CHUNK, D), lambda n: (0, n, 0)),
            pl.BlockSpec((nb, CHUNK, NV), lambda n: (0, n, 0)),
            pl.BlockSpec((nb, 1, NQK, DV), lambda n: (0, n, 0, 0)),
            pl.BlockSpec((nb, CHUNK, D), lambda n: (0, n, 0)),
            pl.BlockSpec(memory_space=pl.ANY),
        ],
        out_shape=[
            jax.ShapeDtypeStruct((nb, s, D), BF16),
            jax.ShapeDtypeStruct((nb, s, NV), F32),
            jax.ShapeDtypeStruct((nb, nc, NQK, DV), F32),
            jax.ShapeDtypeStruct((nb, s, D), F32),
            jax.ShapeDtypeStruct((N_DEV, g_rows, D), BF16),
        ],
        scratch_shapes=[pltpu.VMEM((nb, NQK, DV), F32), pltpu.VMEM((nb, CHUNK, CW), F32),
                        pltpu.VMEM((D, D), BF16), pltpu.SemaphoreType.DMA((N_DEV,))] + _gather_sems(),
        compiler_params=_params(("arbitrary",)),
    )(proj3, z3, gng, conv_w, stage, x3, gwb)


def _ffn_fwd_bwd(x1_2d, tgt2d, gwa, gwb, g2, gf, tm):
    t = x1_2d.shape[0]

    def body(x1_ref, tgt_ref, g2_ref, gf_ref, gwa_hbm, gwb_hbm,
             dx1_ref, dx1b_ref, adu_ref, hb_ref, dg2_ref, dgf_ref, loss_ref,
             wg, wu, wd, wsem):
        i = pl.program_id(0)

        def weight_copies(n, dst, src, off, rows, at=0):
            return [pltpu.make_async_copy(src.at[j, pl.ds(off, rows), :], dst.at[pl.ds(FF_W * j + at, rows), :],
                                          wsem.at[N_DEV * n + j]) for j in range(N_DEV)]

        loads = (weight_copies(0, wg, gwa_hbm, 0, FF_W), weight_copies(1, wu, gwa_hbm, FF_W, FF_W),
                 weight_copies(2, wd, gwa_hbm, 2 * FF_W, D_HEAD), weight_copies(3, wd, gwb_hbm, 0, D_TAIL, D_HEAD))

        @pl.when(i == 0)
        def _():
            for group in loads:
                for cp in group:
                    cp.start()
            dg2_ref[...] = jnp.zeros_like(dg2_ref)
            dgf_ref[...] = jnp.zeros_like(dgf_ref)
            loss_ref[...] = jnp.zeros_like(loss_ref)
            for group in loads:
                for cp in group:
                    cp.wait()

        g2v = g2_ref[...]
        gfv = gf_ref[...]
        x1 = x1_ref[...]
        r2 = lax.rsqrt(jnp.mean(x1 * x1, axis=-1, keepdims=True) + EPS)
        n2 = x1 * r2
        h2 = (n2 * g2v).astype(BF16)
        hb_ref[1] = h2
        gate = _nt(h2, wg[...])
        up = _nt(h2, wu[...])
        sg = jax.nn.sigmoid(gate)
        sil = gate * sg
        act = (sil * up).astype(BF16)
        adu_ref[0] = act
        x2 = x1 + _mm(act, wd[...])
        rf = lax.rsqrt(jnp.mean(x2 * x2, axis=-1, keepdims=True) + EPS)
        nf = x2 * rf
        err = nf * gfv - tgt_ref[...]
        loss_ref[...] += 0.5 * jnp.sum(jnp.mean(err * err, axis=-1, keepdims=True))
        dy = err * (1.0 / D)
        dgf_ref[...] += jnp.sum(dy * nf, axis=0, keepdims=True)
        dnf = dy * gfv
        dx2 = rf * (dnf - nf * jnp.mean(dnf * nf, axis=-1, keepdims=True))
        dx2b = dx2.astype(BF16)
        hb_ref[0] = dx2b
        dact = _nt(dx2b, wd[...])
        dup = (dact * sil).astype(BF16)
        dgate = ((dact * up) * (sg * (1.0 + gate * (1.0 - sg)))).astype(BF16)
        adu_ref[2] = dup
        adu_ref[1] = dgate
        dh2 = _mm(dgate, wg[...]) + _mm(dup, wu[...])
        dg2_ref[...] += jnp.sum(dh2 * n2, axis=0, keepdims=True)
        dn2 = dh2 * g2v
        dx1 = dx2 + r2 * (dn2 - n2 * jnp.mean(dn2 * n2, axis=-1, keepdims=True))
        dx1_ref[...] = dx1
        dx1b_ref[...] = dx1.astype(BF16)

    tile = lambda w: pl.BlockSpec((tm, w), lambda i: (i, 0))
    vec = pl.BlockSpec((1, D), lambda i: (0, 0))
    hbm = pl.BlockSpec(memory_space=pl.ANY)
    return pl.pallas_call(
        body,
        name="ffn_fwd_bwd",
        grid=(t // tm,),
        in_specs=[tile(D), tile(D), vec, vec, hbm, hbm],
        out_specs=[tile(D), tile(D), pl.BlockSpec((3, tm, DFF), lambda i: (0, i, 0)),
                   pl.BlockSpec((2, tm, D), lambda i: (0, i, 0)), vec, vec,
                   pl.BlockSpec((1, 128), lambda i: (0, 0))],
        out_shape=[
            jax.ShapeDtypeStruct((t, D), F32),
            jax.ShapeDtypeStruct((t, D), BF16),
            jax.ShapeDtypeStruct((3, t, DFF), BF16),
            jax.ShapeDtypeStruct((2, t, D), BF16),
            jax.ShapeDtypeStruct((1, D), F32),
            jax.ShapeDtypeStruct((1, D), F32),
            jax.ShapeDtypeStruct((1, 128), F32),
        ],
        scratch_shapes=[pltpu.VMEM((DFF, D), BF16), pltpu.VMEM((DFF, D), BF16), pltpu.VMEM((DFF, D), BF16),
                        pltpu.SemaphoreType.DMA((4 * N_DEV,))],
        compiler_params=_params(("arbitrary",)),
    )(x1_2d, tgt2d, g2, gf, gwa, gwb)


def _stage2_rider(pbs):
    return dict(inputs=list(pbs), out_shape=[jax.ShapeDtypeStruct((2,) + p.shape[1:], BF16) for p in pbs],
                scratch=_stage2_scratch(pbs))


def _tn_matmul(a, b, bm, bn, tk, name, with_bf16, rider=None):
    t, m = a.shape
    n = b.shape[1]
    nk = t // tk
    nout = 2 if with_bf16 else 1
    grid = (m // bm, n // bn, nk)
    steps = grid[0] * grid[1] * nk
    r_in = [] if rider is None else rider["inputs"]
    r_out = [] if rider is None else rider["out_shape"]

    def body(a_ref, b_ref, *rest):
        ins, outs = rest[:len(r_in)], rest[len(r_in):len(r_in) + nout]
        r_outs, scratch = rest[len(r_in) + nout:len(r_in) + nout + len(r_out)], rest[len(r_in) + nout + len(r_out):]
        o_ref = outs[0]
        i, j, k = pl.program_id(0), pl.program_id(1), pl.program_id(2)
        step = (i * grid[1] + j) * nk + k
        if rider is not None:
            @pl.when(step == 0)
            def _():
                _stage2_start(ins, r_outs, scratch)

            @pl.when(step == steps // 2)
            def _():
                _stage2_combine(ins, r_outs, scratch)

        @pl.when(k == 0)
        def _():
            o_ref[...] = jnp.zeros_like(o_ref)

        o_ref[...] += _tn(a_ref[...].astype(BF16), b_ref[...].astype(BF16))
        if with_bf16:
            @pl.when(k == nk - 1)
            def _():
                outs[1][...] = o_ref[...].astype(BF16)
        if rider is not None:
            @pl.when(step == steps - 1)
            def _():
                _stage2_finish(ins, r_outs, scratch)

    out_blk = pl.BlockSpec((bm, bn), lambda i, j, k: (i, j))
    hbm = pl.BlockSpec(memory_space=pl.ANY)
    out_shape = [jax.ShapeDtypeStruct((m, n), F32)] + ([jax.ShapeDtypeStruct((m, n), BF16)] if with_bf16 else [])
    res = pl.pallas_call(
        body,
        name=name,
        grid=grid,
        in_specs=[pl.BlockSpec((tk, bm), lambda i, j, k: (k, i)), pl.BlockSpec((tk, bn), lambda i, j, k: (k, j))]
        + [hbm] * len(r_in),
        out_specs=[out_blk] * nout + [hbm] * len(r_out),
        out_shape=out_shape + list(r_out),
        scratch_shapes=[] if rider is None else rider["scratch"],
        compiler_params=_params(("parallel", "parallel", "arbitrary") if rider is None
                                else ("arbitrary", "arbitrary", "arbitrary")),
    )(a, b, *r_in)
    return res[0] if len(res) == 1 else res


def _dw_ffn(adu, hb, tk):
    _, t, _ = adu.shape
    bm = DFF // 2
    nk = t // tk

    def body(a_ref, b_ref, o_ref, ob_ref):
        k = pl.program_id(2)

        @pl.when(k == 0)
        def _():
            o_ref[...] = jnp.zeros_like(o_ref)

        o_ref[...] += _tn(a_ref[...], b_ref[...])

        @pl.when(k == nk - 1)
        def _():
            ob_ref[...] = o_ref[...].astype(BF16)

    out_blk = pl.BlockSpec((None, bm, D), lambda p, i, k: (p, i, 0))
    return pl.pallas_call(
        body,
        name="dw_ffn",
        grid=(3, DFF // bm, nk),
        in_specs=[pl.BlockSpec((None, tk, bm), lambda p, i, k: (p, k, i)),
                  pl.BlockSpec((None, tk, D), lambda p, i, k: (jnp.minimum(p, 1), k, 0))],
        out_specs=[out_blk, out_blk],
        out_shape=[jax.ShapeDtypeStruct((3, DFF, D), F32), jax.ShapeDtypeStruct((3, DFF, D), BF16)],
        compiler_params=_params(("arbitrary", "arbitrary", "arbitrary")),
    )(adu, hb)


def _mix_bwd(proj3, z3, sprev, opre3, dmix3, gng, conv_w, wgu_p, pbs):
    nb, s, _ = proj3.shape
    nc = s // CHUNK
    na = len(pbs)

    def body(*refs):
        (p_ref, pprev_ref, z_ref, sp_ref, o_ref, dm_ref, gng_ref, cw_ref, wgu_ref) = refs[:9]
        pb_refs = refs[9:9 + na]
        (dproj_ref, dgng_ref, dcw_ref, dbg_ref, dwgu_ref) = refs[9 + na:14 + na]
        r2_refs = refs[14 + na:14 + 2 * na]
        ds_ref, dycn_ref = refs[14 + 2 * na:16 + 2 * na]
        stage2 = (pb_refs, r2_refs, refs[16 + 2 * na:])
        step = pl.program_id(0)
        n = nc - 1 - step

        @pl.when(step == 0)
        def _():
            _stage2_start(*stage2)
            ds_ref[...] = jnp.zeros_like(ds_ref)
            dycn_ref[...] = jnp.zeros_like(dycn_ref)
            dgng_ref[...] = jnp.zeros_like(dgng_ref)
            dcw_ref[...] = jnp.zeros_like(dcw_ref)
            dbg_ref[...] = jnp.zeros_like(dbg_ref)
            dwgu_ref[...] = jnp.zeros_like(dwgu_ref)

        @pl.when(step == RELAY_AT * nc // 8)
        def _():
            _stage2_combine(*stage2)

        r_i = lax.broadcasted_iota(jnp.int32, (CHUNK, CHUNK), 0)
        c_i = lax.broadcasted_iota(jnp.int32, (CHUNK, CHUNK), 1)
        tril16 = (r_i >= c_i).astype(BF16)
        triu16 = (r_i <= c_i).astype(BF16)
        causal = r_i >= c_i
        masks = _head_masks()
        cmask = _causal_stack_mask()
        gg = gng_ref[...]
        last_row = lax.broadcasted_iota(jnp.int32, (CHUNK, NQK), 0) == CHUNK - 1
        ones_r = jnp.ones((16, DV), BF16)
        has_prev = (n > 0).astype(F32)
        for b in range(nb):
            q = p_ref[b, :, OQ:OQ + NQK]
            k = p_ref[b, :, OK_:OK_ + NQK]
            z = z_ref[b]
            _, eb, enb, ekl, qi, ki, ks, decb = _chunk_fwd_parts(q, k, z, tril16)
            qi16 = qi.astype(BF16)
            ki16 = ki.astype(BF16)
            qs = _stack_heads(qi, masks).astype(BF16)
            sc = jnp.where(cmask, _nt(qs, ki16), 0.0).astype(BF16)
            st = sp_ref[b, 0]
            st16 = st.astype(BF16)
            dsn = ds_ref[b]
            dsn16 = dsn.astype(BF16)
            v16 = p_ref[b, :, OV:OV + NV].astype(BF16)
            do16 = []
            dgng = jnp.zeros((1, DV), F32)
            for h in range(HEADS):
                cols = slice(DV * h, DV * (h + 1))
                o = o_ref[b, :, cols]
                r = lax.rsqrt(jnp.mean(o * o, axis=-1, keepdims=True) + EPS)
                nh = o * r
                g = p_ref[b, :, OG + DV * h:OG + DV * (h + 1)]
                sg = jax.nn.sigmoid(g)
                dog = dm_ref[b, :, cols]
                dproj_ref[b, :, OG + DV * h:OG + DV * (h + 1)] = (
                    (dog * (nh * gg)) * (sg * (1.0 + g * (1.0 - sg)))).astype(BF16)
                don = dog * (g * sg)
                dgng = dgng + jnp.sum(don * nh, axis=0, keepdims=True)
                dn = don * gg
                do = r * (dn - nh * jnp.mean(dn * nh, axis=-1, keepdims=True))
                do16.append(do.astype(BF16))
            dgng_ref[...] += dgng
            do_rows = jnp.concatenate(do16, axis=0)
            v_rows = jnp.concatenate([v16[:, DV * h:DV * (h + 1)] for h in range(HEADS)], axis=0)
            dp16 = [jnp.where(causal, _nt(do16[h], v16[:, DV * h:DV * (h + 1)]), 0.0).astype(BF16)
                    for h in range(HEADS)]
            ks_dsn = _mm(_stack_heads(ks, masks).astype(BF16), dsn16)
            do_st = _nt(do_rows, st16)
            v_dsn = _nt(v_rows, dsn16)
            dp_ki = _mm(jnp.concatenate(dp16, axis=0), ki16)
            q_do = _tn(qi16, jnp.concatenate(do16, axis=1))
            dki_h = []
            for h in range(HEADS):
                rows = slice(CHUNK * h, CHUNK * (h + 1))
                cols = slice(DV * h, DV * (h + 1))
                dv = _tn(sc[rows], do16[h]) + ks_dsn[rows]
                dproj_ref[b, :, OV + DV * h:OV + DV * (h + 1)] = dv.astype(BF16)
                dki_h.append(_tn(dp16[h], qi16))
                ds_ref[b, rows, :] = decb[rows] * dsn[rows] + q_do[rows, cols]
            blocks = lambda a: [a[CHUNK * h:CHUNK * (h + 1)] for h in range(HEADS)]
            dqi = _merge_heads(blocks(dp_ki + do_st), masks)
            dki = _merge_heads(dki_h, masks)
            dks = _merge_heads(blocks(v_dsn), masks)
            dproj_ref[b, :, OQ:OQ + NQK] = (dqi * (Q_SCALE * eb)).astype(BF16)
            dproj_ref[b, :, OK_:OK_ + NQK] = (dki * enb + dks * ekl).astype(BF16)
            dks_ks = dks * ks
            db = dqi * qi - dki * ki - dks_ks
            sd = _split_bf16(dsn * st * decb, 2)
            dbl = jnp.sum(dks_ks, axis=0, keepdims=True) + (_nt(ones_r, sd[0]) + _nt(ones_r, sd[1]))[0:1, :]
            db = db + jnp.where(last_row, dbl, 0.0)
            db_parts = _split_bf16(db, 3)
            dla = _mm(triu16, db_parts[0]) + _mm(triu16, db_parts[1]) + _mm(triu16, db_parts[2])
            dz = (dla * INV_GATE_NORM) * (1.0 / (1.0 + jnp.exp(z)))
            dbg_ref[...] += jnp.sum(dz, axis=0, keepdims=True)
            dz16 = dz.astype(BF16)
            pa16 = p_ref[b, :, OA:OA + A_PAD].astype(BF16)
            dwgu_ref[...] += _tn(pa16, dz16)
            dproj_ref[b, :, OA:OA + A_PAD] = _nt(dz16, wgu_ref[...]).astype(BF16)
            cb = p_ref[b, :, OCB:OCB + CW]
            cc = p_ref[b, :, OCC:OCC + CW]
            ch = p_ref[b, :, OCH:OCH + CW]
            u = cc * ch
            uprev = (pprev_ref[b, :, 0:CW] * pprev_ref[b, :, CW:2 * CW]) * has_prev
            u1, u2 = _conv_taps(u, uprev)
            w0 = cw_ref[0:1, :]
            w1 = cw_ref[1:2, :]
            w2 = cw_ref[2:3, :]
            yc = w0 * u2 + w1 * u1 + w2 * u
            doc = dm_ref[b, :, NV:NV + CW]
            dproj_ref[b, :, OCB:OCB + CW] = (doc * yc).astype(BF16)
            dyc = doc * cb
            dycn = dycn_ref[b]
            row = lax.broadcasted_iota(jnp.int32, dyc.shape, 0)
            d1 = jnp.where(row >= CHUNK - 1, pltpu.roll(dycn, CHUNK - 1, 0), pltpu.roll(dyc, CHUNK - 1, 0))
            d2 = jnp.where(row >= CHUNK - 2, pltpu.roll(dycn, CHUNK - 2, 0), pltpu.roll(dyc, CHUNK - 2, 0))
            du = w2 * dyc + w1 * d1 + w0 * d2
            dproj_ref[b, :, OCC:OCC + CW] = (du * ch).astype(BF16)
            dproj_ref[b, :, OCH:OCH + CW] = (du * cc).astype(BF16)
            dcw_ref[0:1, :] += jnp.sum(dyc * u2, axis=0, keepdims=True)
            dcw_ref[1:2, :] += jnp.sum(dyc * u1, axis=0, keepdims=True)
            dcw_ref[2:3, :] += jnp.sum(dyc * u, axis=0, keepdims=True)
            dycn_ref[b] = dyc

        @pl.when(step == nc - 1)
        def _():
            _stage2_finish(*stage2)

    rev =lambda w: pl.BlockSpec((nb, CHUNK, w), lambda i: (0, nc - 1 - i, 0))
    const = lambda r, c: pl.BlockSpec((r, c), lambda i: (0, 0))
    hbm = pl.BlockSpec(memory_space=pl.ANY)
    return pl.pallas_call(
        body,
        name="mix_bwd",
        grid=(nc,),
        in_specs=[
            rev(PW),
            pl.BlockSpec((nb, CHUNK, 2 * CW), lambda i: (0, jnp.maximum(nc - 2 - i, 0), OCC // (2 * CW))),
            rev(NQK),
            pl.BlockSpec((nb, 1, NQK, DV), lambda i: (0, nc - 1 - i, 0, 0)),
            rev(NV),
            rev(D),
            const(1, DV),
            const(CONV_K, CW),
            const(A_PAD, NQK),
        ] + [hbm] * na,
        out_specs=[rev(PW), const(1, DV), const(8, CW), const(1, NQK), const(A_PAD, NQK)] + [hbm] * na,
        out_shape=[
            jax.ShapeDtypeStruct((nb, s, PW), BF16),
            jax.ShapeDtypeStruct((1, DV), F32),
            jax.ShapeDtypeStruct((8, CW), F32),
            jax.ShapeDtypeStruct((1, NQK), F32),
            jax.ShapeDtypeStruct((A_PAD, NQK), F32),
        ] + [jax.ShapeDtypeStruct((2,) + p.shape[1:], BF16) for p in pbs],
        scratch_shapes=[pltpu.VMEM((nb, NQK, DV), F32), pltpu.VMEM((nb, CHUNK, CW), F32)] + _stage2_scratch(pbs),
        compiler_params=_params(("arbitrary",)),
    )(proj3, proj3, z3, sprev, opre3, dmix3, gng, conv_w, wgu_p, *pbs)


SMALL_PACK_ROWS = 16


def _wgu_slot(r):
    return 4 + r // 4, NQK * (r % 4)


CONV_SLOTS = ((8, 0), (8, CW), (9, 0))


def _in_proj_bwd(dproj2d, x2d, dx1, g1, w_in_t, tm, pb, small_parts):
    t = x2d.shape[0]
    nt = t // tm

    def body(dp_ref, x_ref, dx1_ref, g_ref, w_ref, pb_ref, dg2, dgf, dbg, dgng, dwgu, dcw, lp,
             dx_ref, sums_ref, r2_ref, dg1_acc, pack, gbuf, pack1, gbuf1, ssend, srecv, ssend1, srecv1, *scratch2):
        stage2 = ([pb_ref], [r2_ref], scratch2)
        x, y, c = _position()
        me = 4 * x + 2 * y + c
        flips = [(k >> 2, (k >> 1) & 1, k & 1) for k in range(1, N_DEV)]
        peers = [(x ^ fx, y ^ fy, c ^ fc) for fx, fy, fc in flips]

        def small_copies(src, dst, send, recv, arrivals):
            return [pltpu.make_async_remote_copy(
                src_ref=src, dst_ref=dst.at[4 * px + 2 * py + pc if arrivals else me],
                send_sem=send.at[k], recv_sem=recv.at[k], device_id=(px, py, pc), device_id_type=MESH)
                for k, (px, py, pc) in enumerate(peers)]

        @pl.when(pl.program_id(0) == 0)
        def _():
            _stage2_start(*stage2)
            dg1_acc[...] = jnp.zeros_like(dg1_acc)
            pack[...] = jnp.zeros_like(pack)
            pack[1:2, :] = dg2[...]
            pack[2:3, :] = dgf[...]
            pack[3:4, 0:NQK] = dbg[...]
            pack[3:4, NQK:NQK + DV] = dgng[...]
            pack[3:4, NQK + DV:NQK + 2 * DV] = lp[...]
            for r in range(RANK):
                row, lane = _wgu_slot(r)
                pack[row:row + 1, lane:lane + NQK] = dwgu[r:r + 1, :]
            for r, (row, lane) in enumerate(CONV_SLOTS):
                pack[row:row + 1, lane:lane + CW] = dcw[r:r + 1, :]
            for cp in small_copies(pack, gbuf, ssend, srecv, False):
                cp.start()
            gbuf[me] = pack[...]

        @pl.when(pl.program_id(0) == RELAY_AT * nt // 8)
        def _():
            _stage2_combine(*stage2)

        xv = x_ref[...]
        r = lax.rsqrt(jnp.mean(xv * xv, axis=-1, keepdims=True) + EPS)
        n1 = xv * r
        dh = _mm(dp_ref[...], w_ref[...])
        dg1_acc[...] += jnp.sum(dh * n1, axis=0, keepdims=True)
        dn = dh * g_ref[...]
        dx_ref[...] = dx1_ref[...] + r * (dn - n1 * jnp.mean(dn * n1, axis=-1, keepdims=True))

        @pl.when(pl.program_id(0) == nt - 1)
        def _():
            pack1[...] = jnp.zeros_like(pack1)
            pack1[0:1, :] = dg1_acc[...]
            for cp in small_copies(pack1, gbuf1, ssend1, srecv1, False):
                cp.start()
            gbuf1[me] = pack1[...]
            _stage2_finish(*stage2)
            for src, dst, send, recv in ((pack, gbuf, ssend, srecv), (pack1, gbuf1, ssend1, srecv1)):
                for cp in small_copies(src, dst, send, recv, True):
                    cp.wait_recv()
                    cp.wait_send()
            acc = gbuf[0]
            acc1 = gbuf1[0]
            for d in range(1, N_DEV):
                acc = acc + gbuf[d]
                acc1 = acc1 + gbuf1[d]
            sums_ref[...] = acc
            sums_ref[0:1, :] = acc1[0:1, :]

    tile = lambda w: pl.BlockSpec((tm, w), lambda i: (i, 0))
    vec = pl.BlockSpec((1, D), lambda i: (0, 0))
    hbm = pl.BlockSpec(memory_space=pl.ANY)
    whole = lambda a: pl.BlockSpec(a.shape, lambda i: (0,) * a.ndim)
    return pl.pallas_call(
        body,
        name="in_proj_bwd",
        grid=(nt,),
        in_specs=[tile(PW), tile(D), tile(D), vec, pl.BlockSpec((PW, D), lambda i: (0, 0)), hbm]
        + [whole(a) for a in small_parts],
        out_specs=[tile(D), pl.BlockSpec((SMALL_PACK_ROWS, D), lambda i: (0, 0)), hbm],
        out_shape=[jax.ShapeDtypeStruct((t, D), F32), jax.ShapeDtypeStruct((SMALL_PACK_ROWS, D), F32),
                   jax.ShapeDtypeStruct((2,) + pb.shape[1:], BF16)],
        scratch_shapes=[pltpu.VMEM((1, D), F32),
                        pltpu.VMEM((SMALL_PACK_ROWS, D), F32), pltpu.VMEM((N_DEV, SMALL_PACK_ROWS, D), F32),
                        pltpu.VMEM((8, D), F32), pltpu.VMEM((N_DEV, 8, D), F32),
                        pltpu.SemaphoreType.DMA((7,)), pltpu.SemaphoreType.DMA((7,)),
                        pltpu.SemaphoreType.DMA((7,)), pltpu.SemaphoreType.DMA((7,))] + _stage2_scratch([pb]),
        compiler_params=_params(("arbitrary",)),
    )(dproj2d, x2d, dx1, g1, w_in_t, pb, *small_parts)


def _get_rows(ref):
    return ref[:, 0, :] if len(ref.shape) == 3 else ref[...]


def _put_rows(ref, val):
    if len(ref.shape) == 3:
        ref[:, 0, :] = val
    else:
        ref[...] = val


def _adamw_math(w, g, m, v):
    m = ADAM_B1 * m + (1.0 - ADAM_B1) * g
    v = ADAM_B2 * v + (1.0 - ADAM_B2) * (g * g)
    m_hat = m / (1.0 - ADAM_B1 ** ADAM_STEP)
    v_hat = v / (1.0 - ADAM_B2 ** ADAM_STEP)
    delta = -ADAM_LR * (m_hat / (jnp.sqrt(v_hat) + ADAM_EPS) + ADAM_WD * w)
    return delta, m, v


def _position():
    return lax.axis_index("x"), lax.axis_index("y"), lax.axis_index("c")


GATHER_PARTS = 2
GATHER_SEMS = 7 * GATHER_PARTS
RELAY_AT = 3
FORWARD_AT = 7


def _gather_copies(stage, lo, rows, gx, send_sems, recv_sems, local_sem):
    x, y, c = _position()
    me = (x, y, c)
    sibling = (x, y, 1 - c)
    chips = [(1 - x, y), (x, 1 - y), (1 - x, 1 - y)]
    part = -(-rows // (16 * GATHER_PARTS)) * 16
    bounds = [(p * part, min(part, rows - p * part)) for p in range(GATHER_PARTS)]

    def blk(px, py, pc, off, n):
        return gx.at[4 * px + 2 * py + pc, pl.ds(off, n), :]

    mine = pltpu.make_async_copy(stage.at[pl.ds(lo, rows), :], gx.at[4 * x + 2 * y + c], local_sem)
    parts = []
    for p, (off, n) in enumerate(bounds):
        def copy(k, block, to, from_stage=False, p=p, off=off, n=n):
            return pltpu.make_async_remote_copy(
                src_ref=stage.at[pl.ds(lo + off, n), :] if from_stage else blk(*block, off, n),
                dst_ref=blk(*block, off, n), send_sem=send_sems.at[7 * p + k], recv_sem=recv_sems.at[7 * p + k],
                device_id=to, device_id_type=MESH)

        first = [copy(0, me, sibling, True)] + [copy(1 + j, me, (*chips[j], c), True) for j in range(2)]
        relay = copy(3, (*chips[p], c), (*chips[1 - p], c))
        passed = [copy(4 + j, (*chip, c), sibling) for j, chip in enumerate(chips)]
        arrivals = ([copy(0, sibling, me)] + [copy(1 + j, (*chip, c), me) for j, chip in enumerate(chips)]
                    + [copy(4 + j, (*chip, 1 - c), me) for j, chip in enumerate(chips)])
        parts.append((first, relay, passed, arrivals))
    return mine, parts


def _gather_start(*args):
    mine, parts = _gather_copies(*args)
    mine.start()
    for first, _, _, _ in parts:
        first[0].start()
    for p, q in ((0, 0), (1, 1), (0, 1), (1, 0)):
        parts[p][0][1 + q].start()


def _gather_relay(*args):
    _, parts = _gather_copies(*args)
    for p, (_, relay, passed, arrivals) in enumerate(parts):
        arrivals[1 + p].wait_recv()
        relay.start()
        passed[p].start()


def _gather_forward(*args):
    _, parts = _gather_copies(*args)
    for p, j in ((0, 1), (1, 0), (0, 2), (1, 2)):
        _, _, passed, arrivals = parts[p]
        arrivals[1 + j].wait_recv()
        passed[j].start()


def _gather_finish(*args):
    mine, parts = _gather_copies(*args)
    for first, relay, passed, arrivals in parts:
        arrivals[0].wait_recv()
        for j in range(3):
            arrivals[4 + j].wait_recv()
        for cp in first + [relay] + passed:
            cp.wait_send()
    mine.wait()


def _gather_sems():
    return [pltpu.SemaphoreType.DMA((GATHER_SEMS,)), pltpu.SemaphoreType.DMA((GATHER_SEMS,)), pltpu.SemaphoreType.DMA]


def _gather_w_in(w_it, w_gt, w_ut, w_d, w_o, wgu_s, conv_s):
    def body(wi_hbm, wg_hbm, wu_hbm, wd_hbm, wo_hbm, wgu_ref, conv_ref, w_ref, wgup_ref, convf_ref, stage,
             buf, wif, wf, wof, gwgu_ref, gconv_ref, wgu_full, send_sems, recv_sems, local_sem, ssend, srecv,
             load_sems):
        x, y, c = _position()
        me = 4 * x + 2 * y + c
        loads = [pltpu.make_async_copy(src, dst, load_sems.at[n]) for n, (src, dst) in enumerate(
            ((wi_hbm.at[:, 0, :], wif), (wg_hbm, wf.at[0]), (wu_hbm, wf.at[1]), (wd_hbm, wf.at[2]), (wo_hbm, wof)))]
        for cp in loads:
            cp.start()
        loads.pop(0).wait()
        stage[SLAB_IN:SLAB_IN + IN_W, :] = wif[...].astype(BF16)
        stage[SLAB_IN + IN_W:SLAB_G, :] = jnp.zeros((IN_ROWS - IN_W, D), BF16)
        args = (stage, SLAB_IN, IN_ROWS, buf, send_sems, recv_sems, local_sem)
        _gather_start(*args)
        for n, lo in enumerate((SLAB_G, SLAB_U, SLAB_D)):
            loads[n].wait()
            stage[lo:lo + FF_W, :] = wf[n].astype(BF16)
        loads[3].wait()
        stage[SLAB_O:SLAB_ROWS, :] = wof[...].astype(BF16)
        flips = [(k >> 2, (k >> 1) & 1, k & 1) for k in range(1, N_DEV)]
        peers = [(x ^ fx, y ^ fy, c ^ fc) for fx, fy, fc in flips]

        def small(k, block_id, to):
            return [pltpu.make_async_remote_copy(
                src_ref=s, dst_ref=g.at[block_id], send_sem=ssend.at[2 * k + n], recv_sem=srecv.at[2 * k + n],
                device_id=to, device_id_type=MESH)
                for n, (s, g) in enumerate(((wgu_ref, gwgu_ref), (conv_ref, gconv_ref)))]

        gwgu_ref[me] = wgu_ref[...]
        gconv_ref[me] = conv_ref[...]
        for k, peer in enumerate(peers):
            for cp in small(k, me, peer):
                cp.start()
        w_ref[IN_COLS:PW, :] = jnp.zeros((PW - IN_COLS, D), BF16)
        _gather_relay(*args)
        _gather_forward(*args)
        _gather_finish(*args)
        for k, (px, py, pc) in enumerate(peers):
            for cp in small(k, 4 * px + 2 * py + pc, (px, py, pc)):
                cp.wait_recv()
                cp.wait_send()
        wgu_full[...] = jnp.zeros_like(wgu_full)
        for j in range(N_DEV):
            wgu_full[0:RANK, WGU_W * j:WGU_W * (j + 1)] = gwgu_ref[j]
            convf_ref[:, CONV_W * j:CONV_W * (j + 1)] = gconv_ref[j, :, 0, :]
        wgup_ref[...] = wgu_full[...].astype(BF16)
        for j, lo, hi, d in _in_segments():
            w_ref[d:d + hi - lo, :] = buf[j, lo:hi, :]

    vm = pl.BlockSpec(memory_space=pltpu.VMEM)
    return pl.pallas_call(
        body,
        name="gather_w_in",
        in_specs=[pl.BlockSpec(memory_space=pl.ANY)] * 5 + [vm] * 2,
        out_specs=[vm] * 4,
        out_shape=[jax.ShapeDtypeStruct((PW, D), BF16),
                   jax.ShapeDtypeStruct((A_PAD, NQK), BF16),
                   jax.ShapeDtypeStruct((CONV_K, CW), F32),
                   jax.ShapeDtypeStruct((SLAB_ROWS, D), BF16)],
        scratch_shapes=[pltpu.VMEM((N_DEV, IN_ROWS, D), BF16), pltpu.VMEM((IN_W, D), F32),
                        pltpu.VMEM((3, FF_W, D), F32), pltpu.VMEM((OUT_ROWS, D), F32),
                        pltpu.VMEM((N_DEV,) + wgu_s.shape, F32), pltpu.VMEM((N_DEV,) + conv_s.shape, F32),
                        pltpu.VMEM((A_PAD, NQK), F32)] + _gather_sems()
        + [pltpu.SemaphoreType.DMA((14,)), pltpu.SemaphoreType.DMA((14,)), pltpu.SemaphoreType.DMA((5,))],
        compiler_params=_params(),
    )(w_it, w_gt, w_ut, w_d, w_o, wgu_s, conv_s)


def _w_in_core_reduce(dw_t):
    def body(d_ref, own_ref, sib_ref, pb_ref, g, gb, r1, send_sems, recv_sems):
        x, y, c = _position()
        chip = 2 * x + y
        for j in range(N_DEV):
            g[j, IN_W:IN_ROWS, :] = jnp.zeros((IN_ROWS - IN_W, D), F32)
        for j, lo, hi, d in _in_segments():
            g[j, lo:hi, :] = d_ref[d:d + hi - lo, :]
        for j in range(N_DEV):
            gb[j] = g[j].astype(BF16)
        copies = _stage1_copies(gb, r1, send_sems, recv_sems)
        for cp in copies:
            cp.start()
        own_ref[0] = g[2 * chip + c]
        for cp in copies:
            cp.wait_recv()
        sib_ref[0] = r1[chip]
        for k in range(1, 4):
            t = chip ^ k
            pb_ref[k - 1] = (g[2 * t + c] + r1[t].astype(F32)).astype(BF16)
        for cp in copies:
            cp.wait_send()

    vm = pl.BlockSpec(memory_space=pltpu.VMEM)
    return pl.pallas_call(
        body,
        name="w_in_core_reduce",
        in_specs=[vm],
        out_specs=[vm, vm, vm],
        out_shape=[jax.ShapeDtypeStruct((1, IN_ROWS, D), F32), jax.ShapeDtypeStruct((1, IN_ROWS, D), BF16),
                   jax.ShapeDtypeStruct((3, IN_ROWS, D), BF16)],
        scratch_shapes=[pltpu.VMEM((N_DEV, IN_ROWS, D), F32), pltpu.VMEM((N_DEV, IN_ROWS, D), BF16),
                        pltpu.VMEM((4, IN_ROWS, D), BF16), pltpu.SemaphoreType.DMA((4,)),
                        pltpu.SemaphoreType.DMA((4,))],
        compiler_params=_params(),
    )(dw_t)


def _stage1_copies(g_ref, r_ref, send_sems, recv_sems):
    x, y, c = _position()
    return [pltpu.make_async_remote_copy(
        src_ref=g_ref.at[2 * i + 1 - c], dst_ref=r_ref.at[i], send_sem=send_sems.at[i], recv_sem=recv_sems.at[i],
        device_id=(x, y, 1 - c), device_id_type=MESH) for i in range(4)]


def _ffn_core_reduce(dw3, dwb3, dw_o, dwb_o, pos_arr, dx1b, gwb):
    def body(pos_ref, g0, g1, g2, go, gb3_hbm, gbo_hbm, dx1b_ref, gwb_hbm, p0, p1, p2, po, s0, s1, s2, so, dmix_ref,
             r1f, r1o, wo, send_sems, recv_sems, wsem):
        step = pl.program_id(0)
        k = jnp.minimum(step, 2)
        x, y, c = _position()
        chip = 2 * x + y

        def copies(p):
            src = 2 * (chip ^ ((p + 1) & 3)) + 1 - c
            pairs = [(gb3_hbm.at[a, src], r1f.at[a, p]) for a in range(3)] + [(gbo_hbm.at[src], r1o.at[p])]
            return [pltpu.make_async_remote_copy(
                src_ref=s, dst_ref=d, send_sem=send_sems.at[4 * p + a], recv_sem=recv_sems.at[4 * p + a],
                device_id=(x, y, 1 - c), device_id_type=MESH) for a, (s, d) in enumerate(pairs)]

        @pl.when(step == 0)
        def _():
            for p in range(4):
                for cp in copies(p):
                    cp.start()
            loads = [pltpu.make_async_copy(gwb_hbm.at[j, pl.ds(D_TAIL, OUT_ROWS), :],
                                           wo.at[pl.ds(OUT_ROWS * j, OUT_ROWS), :], wsem.at[j]) for j in range(N_DEV)]
            for cp in loads:
                cp.start()
            for cp in loads:
                cp.wait()

        dmix_ref[...] = _nt(dx1b_ref[...], wo[...])

        for p in range(3):
            @pl.when(step == p)
            def _():
                for cp in copies(p):
                    cp.wait_recv()

        for a, (g, pb) in enumerate(((g0, p0), (g1, p1), (g2, p2))):
            pb[...] = (g[...] + r1f[a, k][None].astype(F32)).astype(BF16)
        po[...] = (go[...] + r1o[k][None].astype(F32)).astype(BF16)

        @pl.when(step == 3)
        def _():
            for cp in copies(3):
                cp.wait_recv()
            for a, s in enumerate((s0, s1, s2)):
                s[0] = r1f[a, 3]
            so[0] = r1o[3]
            for p in range(4):
                for cp in copies(p):
                    cp.wait_send()

    t = dx1b.shape[0]
    other = lambda s, pos: 2 * (pos[1] ^ (jnp.minimum(s, 2) + 1)) + pos[0]
    g_spec = lambda lead: pl.BlockSpec((None, 1, FF_W, D), lambda s, pos: (lead, other(s, pos), 0, 0))
    slot = lambda rows: pl.BlockSpec((1, rows, D), lambda s, pos: (jnp.minimum(s, 2), 0, 0))
    one = lambda rows: pl.BlockSpec((1, rows, D), lambda s, pos: (0, 0, 0))
    quarter = pl.BlockSpec((t // 4, D), lambda s, pos: (s, 0))
    hbm = pl.BlockSpec(memory_space=pl.ANY)
    return pl.pallas_call(
        body,
        name="ffn_core_reduce",
        grid_spec=pltpu.PrefetchScalarGridSpec(
            num_scalar_prefetch=1, grid=(4,),
            in_specs=[g_spec(0), g_spec(1), g_spec(2),
                      pl.BlockSpec((1, OUT_ROWS, D), lambda s, pos: (other(s, pos), 0, 0)), hbm, hbm, quarter, hbm],
            out_specs=[slot(FF_W), slot(FF_W), slot(FF_W), slot(OUT_ROWS),
                       one(FF_W), one(FF_W), one(FF_W), one(OUT_ROWS), quarter],
            scratch_shapes=[pltpu.VMEM((3, 4, FF_W, D), BF16), pltpu.VMEM((4, OUT_ROWS, D), BF16),
                            pltpu.VMEM((D, D), BF16), pltpu.SemaphoreType.DMA((16,)),
                            pltpu.SemaphoreType.DMA((16,)), pltpu.SemaphoreType.DMA((N_DEV,))]),
        out_shape=[jax.ShapeDtypeStruct((3, FF_W, D), BF16)] * 3 + [jax.ShapeDtypeStruct((3, OUT_ROWS, D), BF16)]
        + [jax.ShapeDtypeStruct((1, FF_W, D), BF16)] * 3 + [jax.ShapeDtypeStruct((1, OUT_ROWS, D), BF16),
                                                             jax.ShapeDtypeStruct((t, D), F32)],
        compiler_params=_params(("arbitrary",)),
    )(pos_arr, dw3, dw3, dw3, dw_o, dwb3, dwb_o, dx1b, gwb)


def _stage2_scratch(pbs):
    n = len(pbs)
    return ([pltpu.VMEM(p.shape[1:], BF16) for p in pbs] * 2
            + [pltpu.SemaphoreType.DMA((6 * n,)), pltpu.SemaphoreType.DMA((6 * n,)), pltpu.SemaphoreType.DMA((2 * n,))])


def _stage2_copies(p_refs, r_refs, scratch):
    n = len(p_refs)
    owns, gots = scratch[:n], scratch[n:2 * n]
    send_sems, recv_sems, load_sems = scratch[2 * n:]
    x, y, c = _position()
    xn, yn = (1 - x, y, c), (x, 1 - y, c)
    loads, first, second = [], [], []
    for a, (p, r, own, got) in enumerate(zip(p_refs, r_refs, owns, gots)):
        rows = p.shape[1]
        half = -(-rows // 32) * 16
        h0, h1 = pl.ds(0, half), pl.ds(half, rows - half)

        def remote(k, src, dst, to, a=a):
            return pltpu.make_async_remote_copy(
                src_ref=src, dst_ref=dst, send_sem=send_sems.at[6 * a + k], recv_sem=recv_sems.at[6 * a + k],
                device_id=to, device_id_type=MESH)

        loads += [pltpu.make_async_copy(p.at[0, h0, :], own.at[h0, :], load_sems.at[2 * a]),
                  pltpu.make_async_copy(p.at[1, h1, :], own.at[h1, :], load_sems.at[2 * a + 1])]
        first += [remote(0, p.at[2, h0, :], got.at[h0, :], xn), remote(1, p.at[2, h1, :], got.at[h1, :], yn),
                  remote(2, p.at[1, h0, :], r.at[1, h0, :], xn), remote(3, p.at[0, h1, :], r.at[0, h1, :], yn)]
        second += [remote(4, own.at[h0, :], r.at[0, h0, :], yn), remote(5, own.at[h1, :], r.at[1, h1, :], xn)]
    return loads, first, second


def _stage2_start(p_refs, r_refs, scratch):
    loads, first, _ = _stage2_copies(p_refs, r_refs, scratch)
    for cp in loads:
        cp.start()
    for k in range(4):
        for cp in first[k::4]:
            cp.start()


def _stage2_combine(p_refs, r_refs, scratch):
    n = len(p_refs)
    loads, first, second = _stage2_copies(p_refs, r_refs, scratch)
    for a in range(n):
        for cp in loads[2 * a:2 * a + 2]:
            cp.wait()
        for cp in first[4 * a:4 * a + 2]:
            cp.wait_recv()
        own, got = scratch[a], scratch[n + a]
        own[...] = (own[...].astype(F32) + got[...].astype(F32)).astype(BF16)
        for cp in second[2 * a:2 * a + 2]:
            cp.start()


def _stage2_finish(p_refs, r_refs, scratch):
    _, first, second = _stage2_copies(p_refs, r_refs, scratch)
    for a in range(len(p_refs)):
        for cp in first[4 * a + 2:4 * a + 4] + second[2 * a:2 * a + 2]:
            cp.wait_recv()
    for cp in first + second:
        cp.wait_send()


def _finish_weights(items, pos_arr, name, nblk):
    n = len(items)
    in_specs, out_specs, out_shape, operands, wbs = [], [], [], [], []
    for g8, lead, r1, r2, w, m, v in items:
        rows, wr = g8.shape[-2], w.shape[0]
        assert rows % nblk == 0 and wr % nblk == 0 and (nblk == 1 or (rows == wr and rows % (16 * nblk) == 0))
        rb, wb = rows // nblk, wr // nblk
        if lead is not None:
            g_spec = pl.BlockSpec((None, 1, rb, D), lambda i, pos, lead=lead: (lead, 2 * pos[1] + pos[0], i, 0))
        elif g8.shape[0] == 1:
            g_spec = pl.BlockSpec((1, rb, D), lambda i, pos: (0, i, 0))
        else:
            g_spec = pl.BlockSpec((1, rb, D), lambda i, pos: (2 * pos[1] + pos[0], i, 0))
        r1_spec = pl.BlockSpec((1, rb, D), lambda i, pos: (0, i, 0))
        if w.ndim == 3:
            wblk = pl.BlockSpec((wb, 1, D), lambda i, pos: (i, 0, 0))
        else:
            wblk = pl.BlockSpec((wb, D), lambda i, pos: (i, 0))
        in_specs += [g_spec, r1_spec, pl.BlockSpec((2, rb, D), lambda i, pos: (0, i, 0)), wblk, wblk, wblk]
        out_specs += [wblk] * 4
        out_shape += [jax.ShapeDtypeStruct(w.shape, F32)] * 4
        operands += [g8, r1, r2, w, m, v]
        wbs.append(wb)

    def body(pos_ref, *refs):
        for a in range(n):
            g_ref, r1_ref, r2_ref, w_ref, m_ref, v_ref = refs[6 * a:6 * a + 6]
            g_out, d_out, m_out, v_out = refs[6 * n + 4 * a:6 * n + 4 * a + 4]
            g = g_ref[0] + r1_ref[0].astype(F32)
            for k in range(2):
                g = g + r2_ref[k].astype(F32)
            g = g[0:wbs[a], :]
            d, mn, vn = _adamw_math(_get_rows(w_ref), g, _get_rows(m_ref), _get_rows(v_ref))
            for out, val in ((g_out, g), (d_out, d), (m_out, mn), (v_out, vn)):
                _put_rows(out, val)

    return pl.pallas_call(
        body,
        name=name,
        grid_spec=pltpu.PrefetchScalarGridSpec(
            num_scalar_prefetch=1, grid=(nblk,), in_specs=in_specs, out_specs=out_specs),
        out_shape=out_shape,
        compiler_params=_params(("arbitrary",)),
    )(pos_arr, *operands)


SMALL_NAMES = ("norm1_g", "norm2_g", "norm_f_g", "b_gate", "gla_norm_g", "w_gate_up", "conv_w")
WGU_W = NQK // N_DEV
CONV_W = CW // N_DEV


def _small_adamw(sums, ws, ms, vs):
    n = len(SMALL_NAMES)

    def body(*refs):
        acc_ref = refs[0]
        w_refs, m_refs, v_refs = refs[1:1 + n], refs[1 + n:1 + 2 * n], refs[1 + 2 * n:1 + 3 * n]
        loss_ref = refs[1 + 3 * n]
        outs = refs[2 + 3 * n:]
        x, y, c = _position()
        me = 4 * x + 2 * y + c
        acc = acc_ref[...]
        loss_ref[...] = acc[3:4, NQK + DV:NQK + DV + 1]

        def my_columns(full, width):
            r = lax.broadcasted_iota(jnp.int32, (full.shape[1], width), 0)
            col = lax.broadcasted_iota(jnp.int32, (full.shape[1], width), 1)
            sel = (r == width * me + col).astype(F32)
            return _mm(full, sel, precision=HIGHEST)

        dwgu = jnp.concatenate([acc[row:row + 1, lane:lane + NQK] for row, lane in map(_wgu_slot, range(RANK))], axis=0)
        dcw = jnp.concatenate([acc[row:row + 1, lane:lane + CW] for row, lane in CONV_SLOTS], axis=0)
        grads = [acc[0:1, :], acc[1:2, :], acc[2:3, :], acc[3:4, 0:NQK], acc[3:4, NQK:NQK + DV],
                 my_columns(dwgu, WGU_W), my_columns(dcw, CONV_W)]
        for i, g in enumerate(grads):
            d, mn, vn = _adamw_math(_get_rows(w_refs[i]), g, _get_rows(m_refs[i]), _get_rows(v_refs[i]))
            for out, val in zip(outs[4 * i:4 * i + 4], (g, d, mn, vn)):
                _put_rows(out, val)

    vm = pl.BlockSpec(memory_space=pltpu.VMEM)
    out_shape = [jax.ShapeDtypeStruct((1, 1), F32)]
    for w in ws:
        out_shape += [jax.ShapeDtypeStruct(w.shape, F32)] * 4
    return pl.pallas_call(
        body,
        name="small_adamw",
        in_specs=[vm] * (1 + 3 * n),
        out_specs=[vm] * (1 + 4 * n),
        out_shape=out_shape,
        compiler_params=_params(),
    )(sums, *ws, *ms, *vs)


def kernel(x, norm1_g, w_in, w_gate_up, b_gate, gla_norm_g, conv_w, w_out, norm2_g, w_ffn_gate, w_ffn_up, w_ffn_down, norm_f_g, loss_target, m_norm1_g, m_w_in, m_w_gate_up, m_b_gate, m_gla_norm_g, m_conv_w, m_w_out, m_norm2_g, m_w_ffn_gate, m_w_ffn_up, m_w_ffn_down, m_norm_f_g, v_norm1_g, v_w_in, v_w_gate_up, v_b_gate, v_gla_norm_g, v_conv_w, v_w_out, v_norm2_g, v_w_ffn_gate, v_w_ffn_up, v_w_ffn_down, v_norm_f_g):
    xi, yi, ci = _position()
    pos_arr = jnp.stack([ci, 2 * xi + yi]).astype(jnp.int32)
    nb, s, _ = x.shape
    t = nb * s

    tr = lambda a: a[0].T
    rows_of = lambda a: a.transpose(2, 0, 1)
    conv_rows = lambda a: a.transpose(1, 0, 2)
    w_in_t, wgu_p, conv_f, stage = _gather_w_in(rows_of(w_in), tr(w_ffn_gate), tr(w_ffn_up), w_ffn_down[0], w_out[0],
                                                w_gate_up[0], conv_rows(conv_w))

    x2d = x.reshape(t, D)
    tgt2d = loss_target.reshape(t, D)
    tm = 256
    tm_in = min(512, t)
    tk = min(2048, t)
    proj, z, h, gwb = _in_proj_fwd(x2d, norm1_g, w_in_t, wgu_p, b_gate, tm_in, stage)
    proj3 = proj.reshape(nb, s, PW)
    z3 = z.reshape(nb, s, NQK)
    mix3, opre3, sprev, x1, gwa = _mix_fwd(proj3, z3, gla_norm_g, conv_f, stage, x, gwb)
    mix2d = mix3.reshape(t, D)
    dx1, dx1b, adu, hb, dg2, dgf, loss_part = _ffn_fwd_bwd(
        x1.reshape(t, D), tgt2d, gwa, gwb, norm2_g, norm_f_g.reshape(1, D), tm)
    dw3, dwb3 = _dw_ffn(adu, hb, tk)
    dw3 = dw3.reshape(3, N_DEV, FF_W, D)
    dw_o, dwb_o = _tn_matmul(mix2d, dx1b, D // 2, D, tk, "dw_out", True)
    dw_o = dw_o.reshape(N_DEV, OUT_ROWS, D)
    *pb, sib_d, sib_g, sib_u, sib_o, dmix = _ffn_core_reduce(
        dw3, dwb3.reshape(3, N_DEV, FF_W, D), dw_o, dwb_o.reshape(N_DEV, OUT_ROWS, D), pos_arr, dx1b, gwb)
    g8 = [dw3, dw3, dw3, dw_o]
    leads = [0, 1, 2, None]
    tags = ("w_ffn_down", "w_ffn_gate", "w_ffn_up", "w_out")
    r1 = [sib_d, sib_g, sib_u, sib_o]
    mb = _mix_bwd(proj3, z3, sprev, opre3, dmix.reshape(nb, s, D), gla_norm_g, conv_f, wgu_p, [pb[0], pb[1], pb[3]])
    dproj3, dgng, dcw, dbg, dwgu = mb[:5]
    dproj2d = dproj3.reshape(t, PW)
    dw_in_t, r2_up = _tn_matmul(dproj2d, h, PW // 5, D, t, "dw_in", False, _stage2_rider([pb[2]]))
    r2 = [mb[5], mb[6], r2_up, mb[7]]
    g_in, r1_in, pb_in = _w_in_core_reduce(dw_in_t)
    dx, small_sums, r2_in = _in_proj_bwd(dproj2d, x2d, dx1, norm1_g, w_in_t, tm_in, pb_in,
                                         (dg2, dgf, dbg, dgng, dwgu, dcw, loss_part))

    tags = ("w_in",) + tags
    g8 = [g_in] + g8
    leads = [None] + leads
    r1 = [r1_in] + list(r1)
    r2 = [r2_in] + r2
    shard_w = (rows_of(w_in), w_ffn_down[0], tr(w_ffn_gate), tr(w_ffn_up), w_out[0])
    shard_m = (rows_of(m_w_in), m_w_ffn_down[0], tr(m_w_ffn_gate), tr(m_w_ffn_up), m_w_out[0])
    shard_v = (rows_of(v_w_in), v_w_ffn_down[0], tr(v_w_ffn_gate), tr(v_w_ffn_up), v_w_out[0])
    back = (lambda o: o.transpose(1, 2, 0), lambda o: o[None], lambda o: o.T[None], lambda o: o.T[None],
            lambda o: o[None])
    items = list(zip(g8, leads, r1, r2, shard_w, shard_m, shard_v))
    flat = list(_finish_weights(items[1:], pos_arr, "finish_ffn_out", 2))
    flat = list(_finish_weights(items[:1], pos_arr, "finish_w_in", 1)) + flat
    results = {}
    for i, (tag, to_shard) in enumerate(zip(tags, back)):
        results[tag] = [to_shard(o) for o in flat[4 * i:4 * i + 4]]

    small_w = (norm1_g, norm2_g, norm_f_g.reshape(1, D), b_gate, gla_norm_g, w_gate_up[0], conv_rows(conv_w))
    small_m = (m_norm1_g, m_norm2_g, m_norm_f_g.reshape(1, D), m_b_gate, m_gla_norm_g, m_w_gate_up[0],
               conv_rows(m_conv_w))
    small_v = (v_norm1_g, v_norm2_g, v_norm_f_g.reshape(1, D), v_b_gate, v_gla_norm_g, v_w_gate_up[0],
               conv_rows(v_conv_w))
    so = _small_adamw(small_sums, small_w, small_m, small_v)
    loss = so[0].reshape(())
    to_shape = {"norm_f_g": lambda o: o.reshape(D), "w_gate_up": lambda o: o[None],
                "conv_w": lambda o: o.transpose(1, 0, 2)}
    for i, name in enumerate(SMALL_NAMES):
        results[name] = [to_shape.get(name, lambda o: o)(o) for o in so[1 + 4 * i:5 + 4 * i]]

    names = ("norm1_g", "w_in", "w_gate_up", "b_gate", "gla_norm_g", "conv_w", "w_out", "norm2_g",
             "w_ffn_gate", "w_ffn_up", "w_ffn_down", "norm_f_g")
    outs = [loss, dx.reshape(nb, s, D)]
    for kind in range(4):
        for name in names:
            outs.append(results[name][kind])
    return tuple(outs)
```

```python
import jax
import jax.numpy as jnp
from jax import lax
from jax.experimental import pallas as pl
from jax.experimental.pallas import tpu as pltpu

F32 = jnp.float32
BF16 = jnp.bfloat16
HIGHEST = lax.Precision.HIGHEST
MESH = pl.DeviceIdType.MESH

N_DEV = 8
D = 1024
DFF = 2816
HEADS = 4
DK = 64
DV = 128
NQK = HEADS * DK
NV = HEADS * DV
RANK = 16
CHUNK = 64
CW = 512
CONV_K = 3
IN_COLS = 3088
EPS = 1e-6
INV_GATE_NORM = 1.0 / 16.0
Q_SCALE = DK ** -0.5

PW = 3200
OQ, OK_, OV, OG, OCB, OCC, OCH, OA = 0, 256, 512, 1024, 1536, 2048, 2560, 3072
A_PAD = 128

ADAM_LR = 0.001
ADAM_B1 = 0.9
ADAM_B2 = 0.999
ADAM_EPS = 1e-08
ADAM_WD = 0.01
ADAM_STEP = 10

IN_W = IN_COLS // N_DEV
IN_ROWS = 400
FF_W = DFF // N_DEV
OUT_ROWS = D // N_DEV
SLAB_IN = 0
SLAB_G = SLAB_IN + IN_ROWS
SLAB_U = SLAB_G + FF_W
SLAB_D = SLAB_U + FF_W
SLAB_O = SLAB_D + FF_W
SLAB_ROWS = SLAB_O + OUT_ROWS
D_HEAD = 128
D_TAIL = FF_W - D_HEAD
SLAB_SPLIT = SLAB_D + D_HEAD

VMEM_LIMIT = 56 * 1024 * 1024


def _params(sem=None, vmem=VMEM_LIMIT):
    return pltpu.CompilerParams(dimension_semantics=sem, vmem_limit_bytes=vmem)


def _nt(a, b):
    return lax.dot_general(a, b, (((1,), (1,)), ((), ())), preferred_element_type=F32)


def _tn(a, b, precision=None):
    return lax.dot_general(a, b, (((0,), (0,)), ((), ())), preferred_element_type=F32, precision=precision)


def _mm(a, b, precision=None):
    return jnp.dot(a, b, preferred_element_type=F32, precision=precision)


def _in_segments():
    segs = []
    for j in range(N_DEV):
        lo, hi = IN_W * j, IN_W * (j + 1)
        cuts = sorted({lo, hi} | {c for c in (OCB, OCB + RANK) if lo < c < hi})
        for a, b in zip(cuts[:-1], cuts[1:]):
            if a < OCB:
                d = a
            elif a < OCB + RANK:
                d = OA + (a - OCB)
            else:
                d = a - RANK
            segs.append((j, a - lo, b - lo, d))
    return segs


def _in_proj_fwd(x2d, g1, w_in_t, wgu_p, b_gate, tm, stage):
    t = x2d.shape[0]
    nt = t // tm
    g_rows = SLAB_ROWS - SLAB_SPLIT

    def body(x_ref, g_ref, w_ref, wgu_ref, bg_ref, stage_hbm, proj_ref, z_ref, h_ref, gwb_ref,
             send_sems, recv_sems, local_sem):
        gargs = (stage_hbm, SLAB_SPLIT, g_rows, gwb_ref, send_sems, recv_sems, local_sem)

        @pl.when(pl.program_id(0) == 0)
        def _():
            _gather_start(*gargs)

        @pl.when(pl.program_id(0) == RELAY_AT * nt // 8)
        def _():
            _gather_relay(*gargs)

        @pl.when(pl.program_id(0) == FORWARD_AT * nt // 8)
        def _():
            _gather_forward(*gargs)

        x = x_ref[...]
        r = lax.rsqrt(jnp.mean(x * x, axis=-1, keepdims=True) + EPS)
        h = ((x * r) * g_ref[...]).astype(BF16)
        h_ref[...] = h
        proj = _nt(h, w_ref[...])
        proj_ref[...] = proj
        pa = proj[:, OA:OA + A_PAD].astype(BF16)
        z_ref[...] = _mm(pa, wgu_ref[...]) + bg_ref[...]

        @pl.when(pl.program_id(0) == nt - 1)
        def _():
            _gather_finish(*gargs)

    return pl.pallas_call(
        body,
        name="in_proj_fwd",
        grid=(t // tm,),
        in_specs=[
            pl.BlockSpec((tm, D), lambda i: (i, 0)),
            pl.BlockSpec((1, D), lambda i: (0, 0)),
            pl.BlockSpec((PW, D), lambda i: (0, 0)),
            pl.BlockSpec((A_PAD, NQK), lambda i: (0, 0)),
            pl.BlockSpec((1, NQK), lambda i: (0, 0)),
            pl.BlockSpec(memory_space=pl.ANY),
        ],
        out_specs=[
            pl.BlockSpec((tm, PW), lambda i: (i, 0)),
            pl.BlockSpec((tm, NQK), lambda i: (i, 0)),
            pl.BlockSpec((tm, D), lambda i: (i, 0)),
            pl.BlockSpec(memory_space=pl.ANY),
        ],
        out_shape=[
            jax.ShapeDtypeStruct((t, PW), F32),
            jax.ShapeDtypeStruct((t, NQK), F32),
            jax.ShapeDtypeStruct((t, D), BF16),
            jax.ShapeDtypeStruct((N_DEV, g_rows, D), BF16),
        ],
        scratch_shapes=_gather_sems(),
        compiler_params=_params(("arbitrary",)),
    )(x2d, g1, w_in_t, wgu_p, b_gate, stage)


def _head_masks():
    lane = lax.broadcasted_iota(jnp.int32, (1, NQK), 1)
    return [(lane >= DK * h) & (lane < DK * (h + 1)) for h in range(HEADS)]


def _split_bf16(x, n):
    parts = []
    for _ in range(n):
        p = x.astype(BF16)
        parts.append(p)
        x = x - p.astype(F32)
    return parts


def _chunk_fwd_parts(q, k, z, tril16):
    la = (jnp.minimum(z, 0.0) - jnp.log1p(jnp.exp(-jnp.abs(z)))) * INV_GATE_NORM
    la_parts = _split_bf16(la, 3)
    bc = _mm(tril16, la_parts[0]) + _mm(tril16, la_parts[1]) + _mm(tril16, la_parts[2])
    bl = bc[CHUNK - 1:CHUNK, :]
    eb = jnp.exp(bc)
    enb = jnp.exp(-bc)
    ekl = jnp.exp(bl - bc)
    qi = (q * Q_SCALE) * eb
    ki = k * enb
    ks = k * ekl
    ones16 = jnp.ones((CHUNK, DV), BF16)
    decb = jnp.exp(_tn(la_parts[0], ones16) + _tn(la_parts[1], ones16) + _tn(la_parts[2], ones16))
    return la, eb, enb, ekl, qi, ki, ks, decb


def _stack_heads(a, masks):
    return jnp.concatenate([jnp.where(m, a, 0.0) for m in masks], axis=0)


def _merge_heads(blocks, masks):
    out = blocks[HEADS - 1]
    for h in range(HEADS - 2, -1, -1):
        out = jnp.where(masks[h], blocks[h], out)
    return out


def _causal_stack_mask():
    row = lax.broadcasted_iota(jnp.int32, (HEADS * CHUNK, CHUNK), 0)
    col = lax.broadcasted_iota(jnp.int32, (HEADS * CHUNK, CHUNK), 1)
    return (row & (CHUNK - 1)) >= col


def _conv_taps(u, uprev):
    row = lax.broadcasted_iota(jnp.int32, u.shape, 0)
    u1 = jnp.where(row < 1, pltpu.roll(uprev, 1, 0), pltpu.roll(u, 1, 0))
    u2 = jnp.where(row < 2, pltpu.roll(uprev, 2, 0), pltpu.roll(u, 2, 0))
    return u1, u2


def _mix_fwd(proj3, z3, gng, conv_w, stage, x3, gwb):
    nb, s, _ = proj3.shape
    nc = s // CHUNK
    g_rows = SLAB_SPLIT - SLAB_G

    def body(p_ref, z_ref, gng_ref, cw_ref, stage_hbm, x_ref, gwb_hbm, mix_ref, o_ref, sprev_ref, x1_ref, gwa_ref,
             s_ref, uprev_ref, wo, wsem, send_sems, recv_sems, local_sem):
        n = pl.program_id(0)
        gargs = (stage_hbm, SLAB_G, g_rows, gwa_ref, send_sems, recv_sems, local_sem)

        @pl.when(n == 0)
        def _():
            _gather_start(*gargs)
            loads = [pltpu.make_async_copy(gwb_hbm.at[j, pl.ds(D_TAIL, OUT_ROWS), :],
                                           wo.at[pl.ds(OUT_ROWS * j, OUT_ROWS), :], wsem.at[j]) for j in range(N_DEV)]
            for cp in loads:
                cp.start()
            s_ref[...] = jnp.zeros_like(s_ref)
            uprev_ref[...] = jnp.zeros_like(uprev_ref)
            for cp in loads:
                cp.wait()

        @pl.when(n == RELAY_AT * nc // 8)
        def _():
            _gather_relay(*gargs)

        @pl.when(n == FORWARD_AT * nc // 8)
        def _():
            _gather_forward(*gargs)

        r_i = lax.broadcasted_iota(jnp.int32, (CHUNK, CHUNK), 0)
        c_i = lax.broadcasted_iota(jnp.int32, (CHUNK, CHUNK), 1)
        tril16 = (r_i >= c_i).astype(BF16)
        masks = _head_masks()
        cmask = _causal_stack_mask()
        gg = gng_ref[...]
        for b in range(nb):
            q = p_ref[b, :, OQ:OQ + NQK]
            k = p_ref[b, :, OK_:OK_ + NQK]
            _, _, _, _, qi, ki, ks, decb = _chunk_fwd_parts(q, k, z_ref[b], tril16)
            qs = _stack_heads(qi, masks).astype(BF16)
            sc = jnp.where(cmask, _nt(qs, ki.astype(BF16)), 0.0).astype(BF16)
            st = s_ref[b]
            sprev_ref[b, 0] = st
            o_inter = _mm(qs, st.astype(BF16))
            v16 = p_ref[b, :, OV:OV + NV].astype(BF16)
            kv = _tn(ks.astype(BF16), v16)
            for h in range(HEADS):
                rows = slice(CHUNK * h, CHUNK * (h + 1))
                cols = slice(DV * h, DV * (h + 1))
                o = _mm(sc[rows], v16[:, cols]) + o_inter[rows]
                o_ref[b, :, cols] = o
                r = lax.rsqrt(jnp.mean(o * o, axis=-1, keepdims=True) + EPS)
                on = (o * r) * gg
                g = p_ref[b, :, OG + DV * h:OG + DV * (h + 1)]
                mix_ref[b, :, cols] = (on * (g * jax.nn.sigmoid(g))).astype(BF16)
                s_ref[b, rows, :] = decb[rows] * st[rows] + kv[rows, cols]
            u = p_ref[b, :, OCC:OCC + CW] * p_ref[b, :, OCH:OCH + CW]
            u1, u2 = _conv_taps(u, uprev_ref[b])
            yc = cw_ref[0:1, :] * u2 + cw_ref[1:2, :] * u1 + cw_ref[2:3, :] * u
            mix_ref[b, :, NV:NV + CW] = (p_ref[b, :, OCB:OCB + CW] * yc).astype(BF16)
            uprev_ref[b] = u
        mixed = _mm(jnp.concatenate([mix_ref[b] for b in range(nb)], axis=0), wo[...])
        for b in range(nb):
            x1_ref[b] = x_ref[b] + mixed[CHUNK * b:CHUNK * (b + 1)]

        @pl.when(n == nc - 1)
        def _():
            _gather_finish(*gargs)

    return pl.pallas_call(
        body,
        name="mix_fwd",
        grid=(nc,),
        in_specs=[
            pl.BlockSpec((nb, CHUNK, PW), lambda n: (0, n, 0)),
            pl.BlockSpec((nb, CHUNK, NQK), lambda n: (0, n, 0)),
            pl.BlockSpec((1, DV), lambda n: (0, 0)),
            pl.BlockSpec((CONV_K, CW), lambda n: (0, 0)),
            pl.BlockSpec(memory_space=pl.ANY),
            pl.BlockSpec((nb, CHUNK, D), lambda n: (0, n, 0)),
            pl.BlockSpec(memory_space=pl.ANY),
        ],
        out_specs=[
            pl.BlockSpec((nb, CHUNK, D), lambda n: (0, n, 0)),
            pl.BlockSpec((nb, CHUNK, NV), lambda n: (0, n, 0)),
            pl.BlockSpec((nb, 1, NQK, DV), lambda n: (0, n, 0, 0)),
            pl.BlockSpec((nb, CHUNK, D), lambda n: (0, n, 0)),
            pl.BlockSpec(memory_space=pl.ANY),
        ],
        out_shape=[
            jax.ShapeDtypeStruct((nb, s, D), BF16),
            jax.ShapeDtypeStruct((nb, s, NV), F32),
            jax.ShapeDtypeStruct((nb, nc, NQK, DV), F32),
            jax.ShapeDtypeStruct((nb, s, D), F32),
            jax.ShapeDtypeStruct((N_DEV, g_rows, D), BF16),
        ],
        scratch_shapes=[pltpu.VMEM((nb, NQK, DV), F32), pltpu.VMEM((nb, CHUNK, CW), F32),
                        pltpu.VMEM((D, D), BF16), pltpu.SemaphoreType.DMA((N_DEV,))] + _gather_sems(),
        compiler_params=_params(("arbitrary",)),
    )(proj3, z3, gng, conv_w, stage, x3, gwb)


def _ffn_fwd_bwd(x1_2d, tgt2d, gwa, gwb, g2, gf, tm):
    t = x1_2d.shape[0]

    def body(x1_ref, tgt_ref, g2_ref, gf_ref, gwa_hbm, gwb_hbm,
             dx1_ref, dx1b_ref, adu_ref, hb_ref, dg2_ref, dgf_ref, loss_ref,
             wg, wu, wd, wsem):
        i = pl.program_id(0)

        def weight_copies(n, dst, src, off, rows, at=0):
            return [pltpu.make_async_copy(src.at[j, pl.ds(off, rows), :], dst.at[pl.ds(FF_W * j + at, rows), :],
                                          wsem.at[N_DEV * n + j]) for j in range(N_DEV)]

        loads = (weight_copies(0, wg, gwa_hbm, 0, FF_W), weight_copies(1, wu, gwa_hbm, FF_W, FF_W),
                 weight_copies(2, wd, gwa_hbm, 2 * FF_W, D_HEAD), weight_copies(3, wd, gwb_hbm, 0, D_TAIL, D_HEAD))

        @pl.when(i == 0)
        def _():
            for group in loads:
                for cp in group:
                    cp.start()
            dg2_ref[...] = jnp.zeros_like(dg2_ref)
            dgf_ref[...] = jnp.zeros_like(dgf_ref)
            loss_ref[...] = jnp.zeros_like(loss_ref)
            for group in loads:
                for cp in group:
                    cp.wait()

        g2v = g2_ref[...]
        gfv = gf_ref[...]
        x1 = x1_ref[...]
        r2 = lax.rsqrt(jnp.mean(x1 * x1, axis=-1, keepdims=True) + EPS)
        n2 = x1 * r2
        h2 = (n2 * g2v).astype(BF16)
        hb_ref[1] = h2
        gate = _nt(h2, wg[...])
        up = _nt(h2, wu[...])
        sg = jax.nn.sigmoid(gate)
        sil = gate * sg
        act = (sil * up).astype(BF16)
        adu_ref[0] = act
        x2 = x1 + _mm(act, wd[...])
        rf = lax.rsqrt(jnp.mean(x2 * x2, axis=-1, keepdims=True) + EPS)
        nf = x2 * rf
        err = nf * gfv - tgt_ref[...]
        loss_ref[...] += 0.5 * jnp.sum(jnp.mean(err * err, axis=-1, keepdims=True))
        dy = err * (1.0 / D)
        dgf_ref[...] += jnp.sum(dy * nf, axis=0, keepdims=True)
        dnf = dy * gfv
        dx2 = rf * (dnf - nf * jnp.mean(dnf * nf, axis=-1, keepdims=True))
        dx2b = dx2.astype(BF16)
        hb_ref[0] = dx2b
        dact = _nt(dx2b, wd[...])
        dup = (dact * sil).astype(BF16)
        dgate = ((dact * up) * (sg * (1.0 + gate * (1.0 - sg)))).astype(BF16)
        adu_ref[2] = dup
        adu_ref[1] = dgate
        dh2 = _mm(dgate, wg[...]) + _mm(dup, wu[...])
        dg2_ref[...] += jnp.sum(dh2 * n2, axis=0, keepdims=True)
        dn2 = dh2 * g2v
        dx1 = dx2 + r2 * (dn2 - n2 * jnp.mean(dn2 * n2, axis=-1, keepdims=True))
        dx1_ref[...] = dx1
        dx1b_ref[...] = dx1.astype(BF16)

    tile = lambda w: pl.BlockSpec((tm, w), lambda i: (i, 0))
    vec = pl.BlockSpec((1, D), lambda i: (0, 0))
    hbm = pl.BlockSpec(memory_space=pl.ANY)
    return pl.pallas_call(
        body,
        name="ffn_fwd_bwd",
        grid=(t // tm,),
        in_specs=[tile(D), tile(D), vec, vec, hbm, hbm],
        out_specs=[tile(D), tile(D), pl.BlockSpec((3, tm, DFF), lambda i: (0, i, 0)),
                   pl.BlockSpec((2, tm, D), lambda i: (0, i, 0)), vec, vec,
                   pl.BlockSpec((1, 128), lambda i: (0, 0))],
        out_shape=[
            jax.ShapeDtypeStruct((t, D), F32),
            jax.ShapeDtypeStruct((t, D), BF16),
            jax.ShapeDtypeStruct((3, t, DFF), BF16),
            jax.ShapeDtypeStruct((2, t, D), BF16),
            jax.ShapeDtypeStruct((1, D), F32),
            jax.ShapeDtypeStruct((1, D), F32),
            jax.ShapeDtypeStruct((1, 128), F32),
        ],
        scratch_shapes=[pltpu.VMEM((DFF, D), BF16), pltpu.VMEM((DFF, D), BF16), pltpu.VMEM((DFF, D), BF16),
                        pltpu.SemaphoreType.DMA((4 * N_DEV,))],
        compiler_params=_params(("arbitrary",)),
    )(x1_2d, tgt2d, g2, gf, gwa, gwb)


def _stage2_rider(pbs):
    return dict(inputs=list(pbs), out_shape=[jax.ShapeDtypeStruct((2,) + p.shape[1:], BF16) for p in pbs],
                scratch=_stage2_scratch(pbs))


def _tn_matmul(a, b, bm, bn, tk, name, with_bf16, rider=None):
    t, m = a.shape
    n = b.shape[1]
    nk = t // tk
    nout = 2 if with_bf16 else 1
    grid = (m // bm, n // bn, nk)
    steps = grid[0] * grid[1] * nk
    r_in = [] if rider is None else rider["inputs"]
    r_out = [] if rider is None else rider["out_shape"]

    def body(a_ref, b_ref, *rest):
        ins, outs = rest[:len(r_in)], rest[len(r_in):len(r_in) + nout]
        r_outs, scratch = rest[len(r_in) + nout:len(r_in) + nout + len(r_out)], rest[len(r_in) + nout + len(r_out):]
        o_ref = outs[0]
        i, j, k = pl.program_id(0), pl.program_id(1), pl.program_id(2)
        step = (i * grid[1] + j) * nk + k
        if rider is not None:
            @pl.when(step == 0)
            def _():
                _stage2_start(ins, r_outs, scratch)

            @pl.when(step == steps // 2)
            def _():
                _stage2_combine(ins, r_outs, scratch)

        @pl.when(k == 0)
        def _():
            o_ref[...] = jnp.zeros_like(o_ref)

        o_ref[...] += _tn(a_ref[...].astype(BF16), b_ref[...].astype(BF16))
        if with_bf16:
            @pl.when(k == nk - 1)
            def _():
                outs[1][...] = o_ref[...].astype(BF16)
        if rider is not None:
            @pl.when(step == steps - 1)
            def _():
                _stage2_finish(ins, r_outs, scratch)

    out_blk = pl.BlockSpec((bm, bn), lambda i, j, k: (i, j))
    hbm = pl.BlockSpec(memory_space=pl.ANY)
    out_shape = [jax.ShapeDtypeStruct((m, n), F32)] + ([jax.ShapeDtypeStruct((m, n), BF16)] if with_bf16 else [])
    res = pl.pallas_call(
        body,
        name=name,
        grid=grid,
        in_specs=[pl.BlockSpec((tk, bm), lambda i, j, k: (k, i)), pl.BlockSpec((tk, bn), lambda i, j, k: (k, j))]
        + [hbm] * len(r_in),
        out_specs=[out_blk] * nout + [hbm] * len(r_out),
        out_shape=out_shape + list(r_out),
        scratch_shapes=[] if rider is None else rider["scratch"],
        compiler_params=_params(("parallel", "parallel", "arbitrary") if rider is None
                                else ("arbitrary", "arbitrary", "arbitrary")),
    )(a, b, *r_in)
    return res[0] if len(res) == 1 else res


def _dw_ffn(adu, hb, tk):
    _, t, _ = adu.shape
    bm = DFF // 2
    nk = t // tk

    def body(a_ref, b_ref, o_ref, ob_ref):
        k = pl.program_id(2)

        @pl.when(k == 0)
        def _():
            o_ref[...] = jnp.zeros_like(o_ref)

        o_ref[...] += _tn(a_ref[...], b_ref[...])

        @pl.when(k == nk - 1)
        def _():
            ob_ref[...] = o_ref[...].astype(BF16)

    out_blk = pl.BlockSpec((None, bm, D), lambda p, i, k: (p, i, 0))
    return pl.pallas_call(
        body,
        name="dw_ffn",
        grid=(3, DFF // bm, nk),
        in_specs=[pl.BlockSpec((None, tk, bm), lambda p, i, k: (p, k, i)),
                  pl.BlockSpec((None, tk, D), lambda p, i, k: (jnp.minimum(p, 1), k, 0))],
        out_specs=[out_blk, out_blk],
        out_shape=[jax.ShapeDtypeStruct((3, DFF, D), F32), jax.ShapeDtypeStruct((3, DFF, D), BF16)],
        compiler_params=_params(("arbitrary", "arbitrary", "arbitrary")),
    )(adu, hb)


def _mix_bwd(proj3, z3, sprev, opre3, dmix3, gng, conv_w, wgu_p, pbs):
    nb, s, _ = proj3.shape
    nc = s // CHUNK
    na = len(pbs)

    def body(*refs):
        (p_ref, pprev_ref, z_ref, sp_ref, o_ref, dm_ref, gng_ref, cw_ref, wgu_ref) = refs[:9]
        pb_refs = refs[9:9 + na]
        (dproj_ref, dgng_ref, dcw_ref, dbg_ref, dwgu_ref) = refs[9 + na:14 + na]
        r2_refs = refs[14 + na:14 + 2 * na]
        ds_ref, dycn_ref = refs[14 + 2 * na:16 + 2 * na]
        stage2 = (pb_refs, r2_refs, refs[16 + 2 * na:])
        step = pl.program_id(0)
        n = nc - 1 - step

        @pl.when(step == 0)
        def _():
            _stage2_start(*stage2)
            ds_ref[...] = jnp.zeros_like(ds_ref)
            dycn_ref[...] = jnp.zeros_like(dycn_ref)
            dgng_ref[...] = jnp.zeros_like(dgng_ref)
            dcw_ref[...] = jnp.zeros_like(dcw_ref)
            dbg_ref[...] = jnp.zeros_like(dbg_ref)
            dwgu_ref[...] = jnp.zeros_like(dwgu_ref)

        @pl.when(step == RELAY_AT * nc // 8)
        def _():
            _stage2_combine(*stage2)

        r_i = lax.broadcasted_iota(jnp.int32, (CHUNK, CHUNK), 0)
        c_i = lax.broadcasted_iota(jnp.int32, (CHUNK, CHUNK), 1)
        tril16 = (r_i >= c_i).astype(BF16)
        triu16 = (r_i <= c_i).astype(BF16)
        causal = r_i >= c_i
        masks = _head_masks()
        cmask = _causal_stack_mask()
        gg = gng_ref[...]
        last_row = lax.broadcasted_iota(jnp.int32, (CHUNK, NQK), 0) == CHUNK - 1
        ones_r = jnp.ones((16, DV), BF16)
        has_prev = (n > 0).astype(F32)
        for b in range(nb):
            q = p_ref[b, :, OQ:OQ + NQK]
            k = p_ref[b, :, OK_:OK_ + NQK]
            z = z_ref[b]
            _, eb, enb, ekl, qi, ki, ks, decb = _chunk_fwd_parts(q, k, z, tril16)
            qi16 = qi.astype(BF16)
            ki16 = ki.astype(BF16)
            qs = _stack_heads(qi, masks).astype(BF16)
            sc = jnp.where(cmask, _nt(qs, ki16), 0.0).astype(BF16)
            st = sp_ref[b, 0]
            st16 = st.astype(BF16)
            dsn = ds_ref[b]
            dsn16 = dsn.astype(BF16)
            v16 = p_ref[b, :, OV:OV + NV].astype(BF16)
            do16 = []
            dgng = jnp.zeros((1, DV), F32)
            for h in range(HEADS):
                cols = slice(DV * h, DV * (h + 1))
                o = o_ref[b, :, cols]
                r = lax.rsqrt(jnp.mean(o * o, axis=-1, keepdims=True) + EPS)
                nh = o * r
                g = p_ref[b, :, OG + DV * h:OG + DV * (h + 1)]
                sg = jax.nn.sigmoid(g)
                dog = dm_ref[b, :, cols]
                dproj_ref[b, :, OG + DV * h:OG + DV * (h + 1)] = (
                    (dog * (nh * gg)) * (sg * (1.0 + g * (1.0 - sg)))).astype(BF16)
                don = dog * (g * sg)
                dgng = dgng + jnp.sum(don * nh, axis=0, keepdims=True)
                dn = don * gg
                do = r * (dn - nh * jnp.mean(dn * nh, axis=-1, keepdims=True))
                do16.append(do.astype(BF16))
            dgng_ref[...] += dgng
            do_rows = jnp.concatenate(do16, axis=0)
            v_rows = jnp.concatenate([v16[:, DV * h:DV * (h + 1)] for h in range(HEADS)], axis=0)
            dp16 = [jnp.where(causal, _nt(do16[h], v16[:, DV * h:DV * (h + 1)]), 0.0).astype(BF16)
                    for h in range(HEADS)]
            ks_dsn = _mm(_stack_heads(ks, masks).astype(BF16), dsn16)
            do_st = _nt(do_rows, st16)
            v_dsn = _nt(v_rows, dsn16)
            dp_ki = _mm(jnp.concatenate(dp16, axis=0), ki16)
            q_do = _tn(qi16, jnp.concatenate(do16, axis=1))
            dki_h = []
            for h in range(HEADS):
                rows = slice(CHUNK * h, CHUNK * (h + 1))
                cols = slice(DV * h, DV * (h + 1))
                dv = _tn(sc[rows], do16[h]) + ks_dsn[rows]
                dproj_ref[b, :, OV + DV * h:OV + DV * (h + 1)] = dv.astype(BF16)
                dki_h.append(_tn(dp16[h], qi16))
                ds_ref[b, rows, :] = decb[rows] * dsn[rows] + q_do[rows, cols]
            blocks = lambda a: [a[CHUNK * h:CHUNK * (h + 1)] for h in range(HEADS)]
            dqi = _merge_heads(blocks(dp_ki + do_st), masks)
            dki = _merge_heads(dki_h, masks)
            dks = _merge_heads(blocks(v_dsn), masks)
            dproj_ref[b, :, OQ:OQ + NQK] = (dqi * (Q_SCALE * eb)).astype(BF16)
            dproj_ref[b, :, OK_:OK_ + NQK] = (dki * enb + dks * ekl).astype(BF16)
            dks_ks = dks * ks
            db = dqi * qi - dki * ki - dks_ks
            sd = _split_bf16(dsn * st * decb, 2)
            dbl = jnp.sum(dks_ks, axis=0, keepdims=True) + (_nt(ones_r, sd[0]) + _nt(ones_r, sd[1]))[0:1, :]
            db = db + jnp.where(last_row, dbl, 0.0)
            db_parts = _split_bf16(db, 3)
            dla = _mm(triu16, db_parts[0]) + _mm(triu16, db_parts[1]) + _mm(triu16, db_parts[2])
            dz = (dla * INV_GATE_NORM) * (1.0 / (1.0 + jnp.exp(z)))
            dbg_ref[...] += jnp.sum(dz, axis=0, keepdims=True)
            dz16 = dz.astype(BF16)
            pa16 = p_ref[b, :, OA:OA + A_PAD].astype(BF16)
            dwgu_ref[...] += _tn(pa16, dz16)
            dproj_ref[b, :, OA:OA + A_PAD] = _nt(dz16, wgu_ref[...]).astype(BF16)
            cb = p_ref[b, :, OCB:OCB + CW]
            cc = p_ref[b, :, OCC:OCC + CW]
            ch = p_ref[b, :, OCH:OCH + CW]
            u = cc * ch
            uprev = (pprev_ref[b, :, 0:CW] * pprev_ref[b, :, CW:2 * CW]) * has_prev
            u1, u2 = _conv_taps(u, uprev)
            w0 = cw_ref[0:1, :]
            w1 = cw_ref[1:2, :]
            w2 = cw_ref[2:3, :]
            yc = w0 * u2 + w1 * u1 + w2 * u
            doc = dm_ref[b, :, NV:NV + CW]
            dproj_ref[b, :, OCB:OCB + CW] = (doc * yc).astype(BF16)
            dyc = doc * cb
            dycn = dycn_ref[b]
            row = lax.broadcasted_iota(jnp.int32, dyc.shape, 0)
            d1 = jnp.where(row >= CHUNK - 1, pltpu.roll(dycn, CHUNK - 1, 0), pltpu.roll(dyc, CHUNK - 1, 0))
            d2 = jnp.where(row >= CHUNK - 2, pltpu.roll(dycn, CHUNK - 2, 0), pltpu.roll(dyc, CHUNK - 2, 0))
            du = w2 * dyc + w1 * d1 + w0 * d2
            dproj_ref[b, :, OCC:OCC + CW] = (du * ch).astype(BF16)
            dproj_ref[b, :, OCH:OCH + CW] = (du * cc).astype(BF16)
            dcw_ref[0:1, :] += jnp.sum(dyc * u2, axis=0, keepdims=True)
            dcw_ref[1:2, :] += jnp.sum(dyc * u1, axis=0, keepdims=True)
            dcw_ref[2:3, :] += jnp.sum(dyc * u, axis=0, keepdims=True)
            dycn_ref[b] = dyc

        @pl.when(step == nc - 1)
        def _():
            _stage2_finish(*stage2)

    rev =lambda w: pl.BlockSpec((nb, CHUNK, w), lambda i: (0, nc - 1 - i, 0))
    const = lambda r, c: pl.BlockSpec((r, c), lambda i: (0, 0))
    hbm = pl.BlockSpec(memory_space=pl.ANY)
    return pl.pallas_call(
        body,
        name="mix_bwd",
        grid=(nc,),
        in_specs=[
            rev(PW),
            pl.BlockSpec((nb, CHUNK, 2 * CW), lambda i: (0, jnp.maximum(nc - 2 - i, 0), OCC // (2 * CW))),
            rev(NQK),
            pl.BlockSpec((nb, 1, NQK, DV), lambda i: (0, nc - 1 - i, 0, 0)),
            rev(NV),
            rev(D),
            const(1, DV),
            const(CONV_K, CW),
            const(A_PAD, NQK),
        ] + [hbm] * na,
        out_specs=[rev(PW), const(1, DV), const(8, CW), const(1, NQK), const(A_PAD, NQK)] + [hbm] * na,
        out_shape=[
            jax.ShapeDtypeStruct((nb, s, PW), BF16),
            jax.ShapeDtypeStruct((1, DV), F32),
            jax.ShapeDtypeStruct((8, CW), F32),
            jax.ShapeDtypeStruct((1, NQK), F32),
            jax.ShapeDtypeStruct((A_PAD, NQK), F32),
        ] + [jax.ShapeDtypeStruct((2,) + p.shape[1:], BF16) for p in pbs],
        scratch_shapes=[pltpu.VMEM((nb, NQK, DV), F32), pltpu.VMEM((nb, CHUNK, CW), F32)] + _stage2_scratch(pbs),
        compiler_params=_params(("arbitrary",)),
    )(proj3, proj3, z3, sprev, opre3, dmix3, gng, conv_w, wgu_p, *pbs)


SMALL_PACK_ROWS = 16


def _wgu_slot(r):
    return 4 + r // 4, NQK * (r % 4)


CONV_SLOTS = ((8, 0), (8, CW), (9, 0))


def _in_proj_bwd(dproj2d, x2d, dx1, g1, w_in_t, tm, pb, small_parts):
    t = x2d.shape[0]
    nt = t // tm

    def body(dp_ref, x_ref, dx1_ref, g_ref, w_ref, pb_ref, dg2, dgf, dbg, dgng, dwgu, dcw, lp,
             dx_ref, sums_ref, r2_ref, dg1_acc, pack, gbuf, pack1, gbuf1, ssend, srecv, ssend1, srecv1, *scratch2):
        stage2 = ([pb_ref], [r2_ref], scratch2)
        x, y, c = _position()
        me = 4 * x + 2 * y + c
        flips = [(k >> 2, (k >> 1) & 1, k & 1) for k in range(1, N_DEV)]
        peers = [(x ^ fx, y ^ fy, c ^ fc) for fx, fy, fc in flips]

        def small_copies(src, dst, send, recv, arrivals):
            return [pltpu.make_async_remote_copy(
                src_ref=src, dst_ref=dst.at[4 * px + 2 * py + pc if arrivals else me],
                send_sem=send.at[k], recv_sem=recv.at[k], device_id=(px, py, pc), device_id_type=MESH)
                for k, (px, py, pc) in enumerate(peers)]

        @pl.when(pl.program_id(0) == 0)
        def _():
            _stage2_start(*stage2)
            dg1_acc[...] = jnp.zeros_like(dg1_acc)
            pack[...] = jnp.zeros_like(pack)
            pack[1:2, :] = dg2[...]
            pack[2:3, :] = dgf[...]
            pack[3:4, 0:NQK] = dbg[...]
            pack[3:4, NQK:NQK + DV] = dgng[...]
            pack[3:4, NQK + DV:NQK + 2 * DV] = lp[...]
            for r in range(RANK):
                row, lane = _wgu_slot(r)
                pack[row:row + 1, lane:lane + NQK] = dwgu[r:r + 1, :]
            for r, (row, lane) in enumerate(CONV_SLOTS):
                pack[row:row + 1, lane:lane + CW] = dcw[r:r + 1, :]
            for cp in small_copies(pack, gbuf, ssend, srecv, False):
                cp.start()
            gbuf[me] = pack[...]

        @pl.when(pl.program_id(0) == RELAY_AT * nt // 8)
        def _():
            _stage2_combine(*stage2)

        xv = x_ref[...]
        r = lax.rsqrt(jnp.mean(xv * xv, axis=-1, keepdims=True) + EPS)
        n1 = xv * r
        dh = _mm(dp_ref[...], w_ref[...])
        dg1_acc[...] += jnp.sum(dh * n1, axis=0, keepdims=True)
        dn = dh * g_ref[...]
        dx_ref[...] = dx1_ref[...] + r * (dn - n1 * jnp.mean(dn * n1, axis=-1, keepdims=True))

        @pl.when(pl.program_id(0) == nt - 1)
        def _():
            pack1[...] = jnp.zeros_like(pack1)
            pack1[0:1, :] = dg1_acc[...]
            for cp in small_copies(pack1, gbuf1, ssend1, srecv1, False):
                cp.start()
            gbuf1[me] = pack1[...]
            _stage2_finish(*stage2)
            for src, dst, send, recv in ((pack, gbuf, ssend, srecv), (pack1, gbuf1, ssend1, srecv1)):
                for cp in small_copies(src, dst, send, recv, True):
                    cp.wait_recv()
                    cp.wait_send()
            acc = gbuf[0]
            acc1 = gbuf1[0]
            for d in range(1, N_DEV):
                acc = acc + gbuf[d]
                acc1 = acc1 + gbuf1[d]
            sums_ref[...] = acc
            sums_ref[0:1, :] = acc1[0:1, :]

    tile = lambda w: pl.BlockSpec((tm, w), lambda i: (i, 0))
    vec = pl.BlockSpec((1, D), lambda i: (0, 0))
    hbm = pl.BlockSpec(memory_space=pl.ANY)
    whole = lambda a: pl.BlockSpec(a.shape, lambda i: (0,) * a.ndim)
    return pl.pallas_call(
        body,
        name="in_proj_bwd",
        grid=(nt,),
        in_specs=[tile(PW), tile(D), tile(D), vec, pl.BlockSpec((PW, D), lambda i: (0, 0)), hbm]
        + [whole(a) for a in small_parts],
        out_specs=[tile(D), pl.BlockSpec((SMALL_PACK_ROWS, D), lambda i: (0, 0)), hbm],
        out_shape=[jax.ShapeDtypeStruct((t, D), F32), jax.ShapeDtypeStruct((SMALL_PACK_ROWS, D), F32),
                   jax.ShapeDtypeStruct((2,) + pb.shape[1:], BF16)],
        scratch_shapes=[pltpu.VMEM((1, D), F32),
                        pltpu.VMEM((SMALL_PACK_ROWS, D), F32), pltpu.VMEM((N_DEV, SMALL_PACK_ROWS, D), F32),
                        pltpu.VMEM((8, D), F32), pltpu.VMEM((N_DEV, 8, D), F32),
                        pltpu.SemaphoreType.DMA((7,)), pltpu.SemaphoreType.DMA((7,)),
                        pltpu.SemaphoreType.DMA((7,)), pltpu.SemaphoreType.DMA((7,))] + _stage2_scratch([pb]),
        compiler_params=_params(("arbitrary",)),
    )(dproj2d, x2d, dx1, g1, w_in_t, pb, *small_parts)


def _get_rows(ref):
    return ref[:, 0, :] if len(ref.shape) == 3 else ref[...]


def _put_rows(ref, val):
    if len(ref.shape) == 3:
        ref[:, 0, :] = val
    else:
        ref[...] = val


def _adamw_math(w, g, m, v):
    m = ADAM_B1 * m + (1.0 - ADAM_B1) * g
    v = ADAM_B2 * v + (1.0 - ADAM_B2) * (g * g)
    m_hat = m / (1.0 - ADAM_B1 ** ADAM_STEP)
    v_hat = v / (1.0 - ADAM_B2 ** ADAM_STEP)
    delta = -ADAM_LR * (m_hat / (jnp.sqrt(v_hat) + ADAM_EPS) + ADAM_WD * w)
    return delta, m, v


def _position():
    return lax.axis_index("x"), lax.axis_index("y"), lax.axis_index("c")


GATHER_PARTS = 2
GATHER_SEMS = 7 * GATHER_PARTS
RELAY_AT = 3
FORWARD_AT = 7


def _gather_copies(stage, lo, rows, gx, send_sems, recv_sems, local_sem):
    x, y, c = _position()
    me = (x, y, c)
    sibling = (x, y, 1 - c)
    chips = [(1 - x, y), (x, 1 - y), (1 - x, 1 - y)]
    part = -(-rows // (16 * GATHER_PARTS)) * 16
    bounds = [(p * part, min(part, rows - p * part)) for p in range(GATHER_PARTS)]

    def blk(px, py, pc, off, n):
        return gx.at[4 * px + 2 * py + pc, pl.ds(off, n), :]

    mine = pltpu.make_async_copy(stage.at[pl.ds(lo, rows), :], gx.at[4 * x + 2 * y + c], local_sem)
    parts = []
    for p, (off, n) in enumerate(bounds):
        def copy(k, block, to, from_stage=False, p=p, off=off, n=n):
            return pltpu.make_async_remote_copy(
                src_ref=stage.at[pl.ds(lo + off, n), :] if from_stage else blk(*block, off, n),
                dst_ref=blk(*block, off, n), send_sem=send_sems.at[7 * p + k], recv_sem=recv_sems.at[7 * p + k],
                device_id=to, device_id_type=MESH)

        first = [copy(0, me, sibling, True)] + [copy(1 + j, me, (*chips[j], c), True) for j in range(2)]
        relay = copy(3, (*chips[p], c), (*chips[1 - p], c))
        passed = [copy(4 + j, (*chip, c), sibling) for j, chip in enumerate(chips)]
        arrivals = ([copy(0, sibling, me)] + [copy(1 + j, (*chip, c), me) for j, chip in enumerate(chips)]
                    + [copy(4 + j, (*chip, 1 - c), me) for j, chip in enumerate(chips)])
        parts.append((first, relay, passed, arrivals))
    return mine, parts


def _gather_start(*args):
    mine, parts = _gather_copies(*args)
    mine.start()
    for first, _, _, _ in parts:
        first[0].start()
    for p, q in ((0, 0), (1, 1), (0, 1), (1, 0)):
        parts[p][0][1 + q].start()


def _gather_relay(*args):
    _, parts = _gather_copies(*args)
    for p, (_, relay, passed, arrivals) in enumerate(parts):
        arrivals[1 + p].wait_recv()
        relay.start()
        passed[p].start()


def _gather_forward(*args):
    _, parts = _gather_copies(*args)
    for p, j in ((0, 1), (1, 0), (0, 2), (1, 2)):
        _, _, passed, arrivals = parts[p]
        arrivals[1 + j].wait_recv()
        passed[j].start()


def _gather_finish(*args):
    mine, parts = _gather_copies(*args)
    for first, relay, passed, arrivals in parts:
        arrivals[0].wait_recv()
        for j in range(3):
            arrivals[4 + j].wait_recv()
        for cp in first + [relay] + passed:
            cp.wait_send()
    mine.wait()


def _gather_sems():
    return [pltpu.SemaphoreType.DMA((GATHER_SEMS,)), pltpu.SemaphoreType.DMA((GATHER_SEMS,)), pltpu.SemaphoreType.DMA]


def _gather_w_in(w_it, w_gt, w_ut, w_d, w_o, wgu_s, conv_s):
    def body(wi_hbm, wg_hbm, wu_hbm, wd_hbm, wo_hbm, wgu_ref, conv_ref, w_ref, gwgu_ref, gconv_ref, stage,
             buf, wif, wf, wof, send_sems, recv_sems, local_sem, ssend, srecv, load_sems):
        x, y, c = _position()
        me = 4 * x + 2 * y + c
        loads = [pltpu.make_async_copy(src, dst, load_sems.at[n]) for n, (src, dst) in enumerate(
            ((wi_hbm.at[:, 0, :], wif), (wg_hbm, wf.at[0]), (wu_hbm, wf.at[1]), (wd_hbm, wf.at[2]), (wo_hbm, wof)))]
        for cp in loads:
            cp.start()
        loads.pop(0).wait()
        stage[SLAB_IN:SLAB_IN + IN_W, :] = wif[...].astype(BF16)
        stage[SLAB_IN + IN_W:SLAB_G, :] = jnp.zeros((IN_ROWS - IN_W, D), BF16)
        args = (stage, SLAB_IN, IN_ROWS, buf, send_sems, recv_sems, local_sem)
        _gather_start(*args)
        for n, lo in enumerate((SLAB_G, SLAB_U, SLAB_D)):
            loads[n].wait()
            stage[lo:lo + FF_W, :] = wf[n].astype(BF16)
        loads[3].wait()
        stage[SLAB_O:SLAB_ROWS, :] = wof[...].astype(BF16)
        flips = [(k >> 2, (k >> 1) & 1, k & 1) for k in range(1, N_DEV)]
        peers = [(x ^ fx, y ^ fy, c ^ fc) for fx, fy, fc in flips]

        def small(k, block_id, to):
            return [pltpu.make_async_remote_copy(
                src_ref=s, dst_ref=g.at[block_id], send_sem=ssend.at[2 * k + n], recv_sem=srecv.at[2 * k + n],
                device_id=to, device_id_type=MESH)
                for n, (s, g) in enumerate(((wgu_ref, gwgu_ref), (conv_ref, gconv_ref)))]

        gwgu_ref[me] = wgu_ref[...]
        gconv_ref[me] = conv_ref[...]
        for k, peer in enumerate(peers):
            for cp in small(k, me, peer):
                cp.start()
        w_ref[IN_COLS:PW, :] = jnp.zeros((PW - IN_COLS, D), BF16)
        _gather_relay(*args)
        _gather_forward(*args)
        _gather_finish(*args)
        for k, (px, py, pc) in enumerate(peers):
            for cp in small(k, 4 * px + 2 * py + pc, (px, py, pc)):
                cp.wait_recv()
                cp.wait_send()
        for j, lo, hi, d in _in_segments():
            w_ref[d:d + hi - lo, :] = buf[j, lo:hi, :]

    vm = pl.BlockSpec(memory_space=pltpu.VMEM)
    return pl.pallas_call(
        body,
        name="gather_w_in",
        in_specs=[pl.BlockSpec(memory_space=pl.ANY)] * 5 + [vm] * 2,
        out_specs=[vm] * 4,
        out_shape=[jax.ShapeDtypeStruct((PW, D), BF16),
                   jax.ShapeDtypeStruct((N_DEV,) + wgu_s.shape, F32),
                   jax.ShapeDtypeStruct((N_DEV,) + conv_s.shape, F32),
                   jax.ShapeDtypeStruct((SLAB_ROWS, D), BF16)],
        scratch_shapes=[pltpu.VMEM((N_DEV, IN_ROWS, D), BF16), pltpu.VMEM((IN_W, D), F32),
                        pltpu.VMEM((3, FF_W, D), F32), pltpu.VMEM((OUT_ROWS, D), F32)] + _gather_sems()
        + [pltpu.SemaphoreType.DMA((14,)), pltpu.SemaphoreType.DMA((14,)), pltpu.SemaphoreType.DMA((5,))],
        compiler_params=_params(),
    )(w_it, w_gt, w_ut, w_d, w_o, wgu_s, conv_s)


LOAD_CHUNKS = 4


def _w_in_core_reduce(dw_t):
    def body(d_hbm, own_ref, sib_ref, pb_ref, d_ref, g, gb, r1, send_sems, recv_sems, load_sems):
        x, y, c = _position()
        chip = 2 * x + y
        chunk = PW // LOAD_CHUNKS
        loads = [pltpu.make_async_copy(d_hbm.at[pl.ds(chunk * n, chunk), :], d_ref.at[pl.ds(chunk * n, chunk), :],
                                       load_sems.at[n]) for n in range(LOAD_CHUNKS)]
        for cp in loads:
            cp.start()
        for j in range(N_DEV):
            g[j, IN_W:IN_ROWS, :] = jnp.zeros((IN_ROWS - IN_W, D), F32)
        there = 0
        for j, lo, hi, d in sorted(_in_segments(), key=lambda seg: seg[3]):
            while chunk * there < d + hi - lo:
                loads[there].wait()
                there += 1
            g[j, lo:hi, :] = d_ref[d:d + hi - lo, :]
        for cp in loads[there:]:
            cp.wait()
        for j in range(N_DEV):
            gb[j] = g[j].astype(BF16)
        copies = _stage1_copies(gb, r1, send_sems, recv_sems)
        for cp in copies:
            cp.start()
        own_ref[0] = g[2 * chip + c]
        for cp in copies:
            cp.wait_recv()
        sib_ref[0] = r1[chip]
        for k in range(1, 4):
            t = chip ^ k
            pb_ref[k - 1] = (g[2 * t + c] + r1[t].astype(F32)).astype(BF16)
        for cp in copies:
            cp.wait_send()

    vm = pl.BlockSpec(memory_space=pltpu.VMEM)
    return pl.pallas_call(
        body,
        name="w_in_core_reduce",
        in_specs=[pl.BlockSpec(memory_space=pl.ANY)],
        out_specs=[vm, vm, vm],
        out_shape=[jax.ShapeDtypeStruct((1, IN_ROWS, D), F32), jax.ShapeDtypeStruct((1, IN_ROWS, D), BF16),
                   jax.ShapeDtypeStruct((3, IN_ROWS, D), BF16)],
        scratch_shapes=[pltpu.VMEM((PW, D), F32),
                        pltpu.VMEM((N_DEV, IN_ROWS, D), F32), pltpu.VMEM((N_DEV, IN_ROWS, D), BF16),
                        pltpu.VMEM((4, IN_ROWS, D), BF16), pltpu.SemaphoreType.DMA((4,)),
                        pltpu.SemaphoreType.DMA((4,)), pltpu.SemaphoreType.DMA((LOAD_CHUNKS,))],
        compiler_params=_params(),
    )(dw_t)


def _stage1_copies(g_ref, r_ref, send_sems, recv_sems):
    x, y, c = _position()
    return [pltpu.make_async_remote_copy(
        src_ref=g_ref.at[2 * i + 1 - c], dst_ref=r_ref.at[i], send_sem=send_sems.at[i], recv_sem=recv_sems.at[i],
        device_id=(x, y, 1 - c), device_id_type=MESH) for i in range(4)]


def _ffn_core_reduce(dw3, dwb3, dw_o, dwb_o, pos_arr, dx1b, gwb):
    def body(pos_ref, g0, g1, g2, go, gb3_hbm, gbo_hbm, dx1b_ref, gwb_hbm, p0, p1, p2, po, s0, s1, s2, so, dmix_ref,
             r1f, r1o, wo, send_sems, recv_sems, wsem):
        step = pl.program_id(0)
        k = jnp.minimum(step, 2)
        x, y, c = _position()
        chip = 2 * x + y

        def copies(p):
            src = 2 * (chip ^ ((p + 1) & 3)) + 1 - c
            pairs = [(gb3_hbm.at[a, src], r1f.at[a, p]) for a in range(3)] + [(gbo_hbm.at[src], r1o.at[p])]
            return [pltpu.make_async_remote_copy(
                src_ref=s, dst_ref=d, send_sem=send_sems.at[4 * p + a], recv_sem=recv_sems.at[4 * p + a],
                device_id=(x, y, 1 - c), device_id_type=MESH) for a, (s, d) in enumerate(pairs)]

        @pl.when(step == 0)
        def _():
            for p in range(4):
                for cp in copies(p):
                    cp.start()
            loads = [pltpu.make_async_copy(gwb_hbm.at[j, pl.ds(D_TAIL, OUT_ROWS), :],
                                           wo.at[pl.ds(OUT_ROWS * j, OUT_ROWS), :], wsem.at[j]) for j in range(N_DEV)]
            for cp in loads:
                cp.start()
            for cp in loads:
                cp.wait()

        dmix_ref[...] = _nt(dx1b_ref[...], wo[...])

        for p in range(3):
            @pl.when(step == p)
            def _():
                for cp in copies(p):
                    cp.wait_recv()

        for a, (g, pb) in enumerate(((g0, p0), (g1, p1), (g2, p2))):
            pb[...] = (g[...] + r1f[a, k][None].astype(F32)).astype(BF16)
        po[...] = (go[...] + r1o[k][None].astype(F32)).astype(BF16)

        @pl.when(step == 3)
        def _():
            for cp in copies(3):
                cp.wait_recv()
            for a, s in enumerate((s0, s1, s2)):
                s[0] = r1f[a, 3]
            so[0] = r1o[3]
            for p in range(4):
                for cp in copies(p):
                    cp.wait_send()

    t = dx1b.shape[0]
    other = lambda s, pos: 2 * (pos[1] ^ (jnp.minimum(s, 2) + 1)) + pos[0]
    g_spec = lambda lead: pl.BlockSpec((None, 1, FF_W, D), lambda s, pos: (lead, other(s, pos), 0, 0))
    slot = lambda rows: pl.BlockSpec((1, rows, D), lambda s, pos: (jnp.minimum(s, 2), 0, 0))
    one = lambda rows: pl.BlockSpec((1, rows, D), lambda s, pos: (0, 0, 0))
    quarter = pl.BlockSpec((t // 4, D), lambda s, pos: (s, 0))
    hbm = pl.BlockSpec(memory_space=pl.ANY)
    return pl.pallas_call(
        body,
        name="ffn_core_reduce",
        grid_spec=pltpu.PrefetchScalarGridSpec(
            num_scalar_prefetch=1, grid=(4,),
            in_specs=[g_spec(0), g_spec(1), g_spec(2),
                      pl.BlockSpec((1, OUT_ROWS, D), lambda s, pos: (other(s, pos), 0, 0)), hbm, hbm, quarter, hbm],
            out_specs=[slot(FF_W), slot(FF_W), slot(FF_W), slot(OUT_ROWS),
                       one(FF_W), one(FF_W), one(FF_W), one(OUT_ROWS), quarter],
            scratch_shapes=[pltpu.VMEM((3, 4, FF_W, D), BF16), pltpu.VMEM((4, OUT_ROWS, D), BF16),
                            pltpu.VMEM((D, D), BF16), pltpu.SemaphoreType.DMA((16,)),
                            pltpu.SemaphoreType.DMA((16,)), pltpu.SemaphoreType.DMA((N_DEV,))]),
        out_shape=[jax.ShapeDtypeStruct((3, FF_W, D), BF16)] * 3 + [jax.ShapeDtypeStruct((3, OUT_ROWS, D), BF16)]
        + [jax.ShapeDtypeStruct((1, FF_W, D), BF16)] * 3 + [jax.ShapeDtypeStruct((1, OUT_ROWS, D), BF16),
                                                             jax.ShapeDtypeStruct((t, D), F32)],
        compiler_params=_params(("arbitrary",)),
    )(pos_arr, dw3, dw3, dw3, dw_o, dwb3, dwb_o, dx1b, gwb)


def _stage2_scratch(pbs):
    n = len(pbs)
    return ([pltpu.VMEM(p.shape[1:], BF16) for p in pbs] * 2
            + [pltpu.SemaphoreType.DMA((6 * n,)), pltpu.SemaphoreType.DMA((6 * n,)), pltpu.SemaphoreType.DMA((2 * n,))])


def _stage2_copies(p_refs, r_refs, scratch):
    n = len(p_refs)
    owns, gots = scratch[:n], scratch[n:2 * n]
    send_sems, recv_sems, load_sems = scratch[2 * n:]
    x, y, c = _position()
    xn, yn = (1 - x, y, c), (x, 1 - y, c)
    loads, first, second = [], [], []
    for a, (p, r, own, got) in enumerate(zip(p_refs, r_refs, owns, gots)):
        rows = p.shape[1]
        half = -(-rows // 32) * 16
        h0, h1 = pl.ds(0, half), pl.ds(half, rows - half)

        def remote(k, src, dst, to, a=a):
            return pltpu.make_async_remote_copy(
                src_ref=src, dst_ref=dst, send_sem=send_sems.at[6 * a + k], recv_sem=recv_sems.at[6 * a + k],
                device_id=to, device_id_type=MESH)

        loads += [pltpu.make_async_copy(p.at[0, h0, :], own.at[h0, :], load_sems.at[2 * a]),
                  pltpu.make_async_copy(p.at[1, h1, :], own.at[h1, :], load_sems.at[2 * a + 1])]
        first += [remote(0, p.at[2, h0, :], got.at[h0, :], xn), remote(1, p.at[2, h1, :], got.at[h1, :], yn),
                  remote(2, p.at[1, h0, :], r.at[1, h0, :], xn), remote(3, p.at[0, h1, :], r.at[0, h1, :], yn)]
        second += [remote(4, own.at[h0, :], r.at[0, h0, :], yn), remote(5, own.at[h1, :], r.at[1, h1, :], xn)]
    return loads, first, second


def _stage2_start(p_refs, r_refs, scratch):
    loads, first, _ = _stage2_copies(p_refs, r_refs, scratch)
    for cp in loads:
        cp.start()
    for k in range(4):
        for cp in first[k::4]:
            cp.start()


def _stage2_combine(p_refs, r_refs, scratch):
    n = len(p_refs)
    loads, first, second = _stage2_copies(p_refs, r_refs, scratch)
    for a in range(n):
        for cp in loads[2 * a:2 * a + 2]:
            cp.wait()
        for cp in first[4 * a:4 * a + 2]:
            cp.wait_recv()
        own, got = scratch[a], scratch[n + a]
        own[...] = (own[...].astype(F32) + got[...].astype(F32)).astype(BF16)
        for cp in second[2 * a:2 * a + 2]:
            cp.start()


def _stage2_finish(p_refs, r_refs, scratch):
    _, first, second = _stage2_copies(p_refs, r_refs, scratch)
    for a in range(len(p_refs)):
        for cp in first[4 * a + 2:4 * a + 4] + second[2 * a:2 * a + 2]:
            cp.wait_recv()
    for cp in first + second:
        cp.wait_send()


def _finish_weights(items, pos_arr, name, nblk):
    n = len(items)
    in_specs, out_specs, out_shape, operands, wbs = [], [], [], [], []
    for g8, lead, r1, r2, w, m, v in items:
        rows, wr = g8.shape[-2], w.shape[0]
        assert rows % nblk == 0 and wr % nblk == 0 and (nblk == 1 or (rows == wr and rows % (16 * nblk) == 0))
        rb, wb = rows // nblk, wr // nblk
        if lead is not None:
            g_spec = pl.BlockSpec((None, 1, rb, D), lambda i, pos, lead=lead: (lead, 2 * pos[1] + pos[0], i, 0))
        elif g8.shape[0] == 1:
            g_spec = pl.BlockSpec((1, rb, D), lambda i, pos: (0, i, 0))
        else:
            g_spec = pl.BlockSpec((1, rb, D), lambda i, pos: (2 * pos[1] + pos[0], i, 0))
        r1_spec = pl.BlockSpec((1, rb, D), lambda i, pos: (0, i, 0))
        if w.ndim == 3:
            wblk = pl.BlockSpec((wb, 1, D), lambda i, pos: (i, 0, 0))
        else:
            wblk = pl.BlockSpec((wb, D), lambda i, pos: (i, 0))
        in_specs += [g_spec, r1_spec, pl.BlockSpec((2, rb, D), lambda i, pos: (0, i, 0)), wblk, wblk, wblk]
        out_specs += [wblk] * 4
        out_shape += [jax.ShapeDtypeStruct(w.shape, F32)] * 4
        operands += [g8, r1, r2, w, m, v]
        wbs.append(wb)

    def body(pos_ref, *refs):
        for a in range(n):
            g_ref, r1_ref, r2_ref, w_ref, m_ref, v_ref = refs[6 * a:6 * a + 6]
            g_out, d_out, m_out, v_out = refs[6 * n + 4 * a:6 * n + 4 * a + 4]
            g = g_ref[0] + r1_ref[0].astype(F32)
            for k in range(2):
                g = g + r2_ref[k].astype(F32)
            g = g[0:wbs[a], :]
            d, mn, vn = _adamw_math(_get_rows(w_ref), g, _get_rows(m_ref), _get_rows(v_ref))
            for out, val in ((g_out, g), (d_out, d), (m_out, mn), (v_out, vn)):
                _put_rows(out, val)

    return pl.pallas_call(
        body,
        name=name,
        grid_spec=pltpu.PrefetchScalarGridSpec(
            num_scalar_prefetch=1, grid=(nblk,), in_specs=in_specs, out_specs=out_specs),
        out_shape=out_shape,
        compiler_params=_params(("arbitrary",)),
    )(pos_arr, *operands)


SMALL_NAMES = ("norm1_g", "norm2_g", "norm_f_g", "b_gate", "gla_norm_g", "w_gate_up", "conv_w")
WGU_W = NQK // N_DEV
CONV_W = CW // N_DEV


def _small_adamw(sums, ws, ms, vs):
    n = len(SMALL_NAMES)

    def body(*refs):
        acc_ref = refs[0]
        w_refs, m_refs, v_refs = refs[1:1 + n], refs[1 + n:1 + 2 * n], refs[1 + 2 * n:1 + 3 * n]
        loss_ref = refs[1 + 3 * n]
        outs = refs[2 + 3 * n:]
        x, y, c = _position()
        me = 4 * x + 2 * y + c
        acc = acc_ref[...]
        loss_ref[...] = acc[3:4, NQK + DV:NQK + DV + 1]

        def my_columns(full, width):
            r = lax.broadcasted_iota(jnp.int32, (full.shape[1], width), 0)
            col = lax.broadcasted_iota(jnp.int32, (full.shape[1], width), 1)
            sel = (r == width * me + col).astype(F32)
            return _mm(full, sel, precision=HIGHEST)

        dwgu = jnp.concatenate([acc[row:row + 1, lane:lane + NQK] for row, lane in map(_wgu_slot, range(RANK))], axis=0)
        dcw = jnp.concatenate([acc[row:row + 1, lane:lane + CW] for row, lane in CONV_SLOTS], axis=0)
        grads = [acc[0:1, :], acc[1:2, :], acc[2:3, :], acc[3:4, 0:NQK], acc[3:4, NQK:NQK + DV],
                 my_columns(dwgu, WGU_W), my_columns(dcw, CONV_W)]
        for i, g in enumerate(grads):
            d, mn, vn = _adamw_math(_get_rows(w_refs[i]), g, _get_rows(m_refs[i]), _get_rows(v_refs[i]))
            for out, val in zip(outs[4 * i:4 * i + 4], (g, d, mn, vn)):
                _put_rows(out, val)

    vm = pl.BlockSpec(memory_space=pltpu.VMEM)
    out_shape = [jax.ShapeDtypeStruct((1, 1), F32)]
    for w in ws:
        out_shape += [jax.ShapeDtypeStruct(w.shape, F32)] * 4
    return pl.pallas_call(
        body,
        name="small_adamw",
        in_specs=[vm] * (1 + 3 * n),
        out_specs=[vm] * (1 + 4 * n),
        out_shape=out_shape,
        compiler_params=_params(),
    )(sums, *ws, *ms, *vs)


def kernel(x, norm1_g, w_in, w_gate_up, b_gate, gla_norm_g, conv_w, w_out, norm2_g, w_ffn_gate, w_ffn_up, w_ffn_down, norm_f_g, loss_target, m_norm1_g, m_w_in, m_w_gate_up, m_b_gate, m_gla_norm_g, m_conv_w, m_w_out, m_norm2_g, m_w_ffn_gate, m_w_ffn_up, m_w_ffn_down, m_norm_f_g, v_norm1_g, v_w_in, v_w_gate_up, v_b_gate, v_gla_norm_g, v_conv_w, v_w_out, v_norm2_g, v_w_ffn_gate, v_w_ffn_up, v_w_ffn_down, v_norm_f_g):
    xi, yi, ci = _position()
    pos_arr = jnp.stack([ci, 2 * xi + yi]).astype(jnp.int32)
    nb, s, _ = x.shape
    t = nb * s

    tr = lambda a: a[0].T
    rows_of = lambda a: a.transpose(2, 0, 1)
    conv_rows = lambda a: a.transpose(1, 0, 2)
    w_in_t, gwgu, gconv, stage = _gather_w_in(rows_of(w_in), tr(w_ffn_gate), tr(w_ffn_up), w_ffn_down[0], w_out[0],
                                              w_gate_up[0], conv_rows(conv_w))
    wgu_f = gwgu.transpose(1, 0, 2).reshape(RANK, NQK)
    conv_f = gconv.transpose(1, 2, 0, 3).reshape(CONV_K, CW)
    wgu_p = jnp.concatenate([wgu_f, jnp.zeros((A_PAD - RANK, NQK), F32)], axis=0).astype(BF16)

    x2d = x.reshape(t, D)
    tgt2d = loss_target.reshape(t, D)
    tm = 256
    tm_in = min(512, t)
    tk = min(2048, t)
    proj, z, h, gwb = _in_proj_fwd(x2d, norm1_g, w_in_t, wgu_p, b_gate, tm_in, stage)
    proj3 = proj.reshape(nb, s, PW)
    z3 = z.reshape(nb, s, NQK)
    mix3, opre3, sprev, x1, gwa = _mix_fwd(proj3, z3, gla_norm_g, conv_f, stage, x, gwb)
    mix2d = mix3.reshape(t, D)
    dx1, dx1b, adu, hb, dg2, dgf, loss_part = _ffn_fwd_bwd(
        x1.reshape(t, D), tgt2d, gwa, gwb, norm2_g, norm_f_g.reshape(1, D), tm)
    dw3, dwb3 = _dw_ffn(adu, hb, tk)
    dw3 = dw3.reshape(3, N_DEV, FF_W, D)
    dw_o, dwb_o = _tn_matmul(mix2d, dx1b, D // 2, D, tk, "dw_out", True)
    dw_o = dw_o.reshape(N_DEV, OUT_ROWS, D)
    *pb, sib_d, sib_g, sib_u, sib_o, dmix = _ffn_core_reduce(
        dw3, dwb3.reshape(3, N_DEV, FF_W, D), dw_o, dwb_o.reshape(N_DEV, OUT_ROWS, D), pos_arr, dx1b, gwb)
    g8 = [dw3, dw3, dw3, dw_o]
    leads = [0, 1, 2, None]
    tags = ("w_ffn_down", "w_ffn_gate", "w_ffn_up", "w_out")
    r1 = [sib_d, sib_g, sib_u, sib_o]
    mb = _mix_bwd(proj3, z3, sprev, opre3, dmix.reshape(nb, s, D), gla_norm_g, conv_f, wgu_p, [pb[0], pb[1], pb[3]])
    dproj3, dgng, dcw, dbg, dwgu = mb[:5]
    dproj2d = dproj3.reshape(t, PW)
    dw_in_t, r2_up = _tn_matmul(dproj2d, h, PW // 5, D, t, "dw_in", False, _stage2_rider([pb[2]]))
    r2 = [mb[5], mb[6], r2_up, mb[7]]
    g_in, r1_in, pb_in = _w_in_core_reduce(dw_in_t)
    dx, small_sums, r2_in = _in_proj_bwd(dproj2d, x2d, dx1, norm1_g, w_in_t, tm_in, pb_in,
                                         (dg2, dgf, dbg, dgng, dwgu, dcw, loss_part))

    tags = ("w_in",) + tags
    g8 = [g_in] + g8
    leads = [None] + leads
    r1 = [r1_in] + list(r1)
    r2 = [r2_in] + r2
    shard_w = (rows_of(w_in), w_ffn_down[0], tr(w_ffn_gate), tr(w_ffn_up), w_out[0])
    shard_m = (rows_of(m_w_in), m_w_ffn_down[0], tr(m_w_ffn_gate), tr(m_w_ffn_up), m_w_out[0])
    shard_v = (rows_of(v_w_in), v_w_ffn_down[0], tr(v_w_ffn_gate), tr(v_w_ffn_up), v_w_out[0])
    back = (lambda o: o.transpose(1, 2, 0), lambda o: o[None], lambda o: o.T[None], lambda o: o.T[None],
            lambda o: o[None])
    items = list(zip(g8, leads, r1, r2, shard_w, shard_m, shard_v))
    flat = list(_finish_weights(items[1:], pos_arr, "finish_ffn_out", 2))
    flat = list(_finish_weights(items[:1], pos_arr, "finish_w_in", 1)) + flat
    results = {}
    for i, (tag, to_shard) in enumerate(zip(tags, back)):
        results[tag] = [to_shard(o) for o in flat[4 * i:4 * i + 4]]

    small_w = (norm1_g, norm2_g, norm_f_g.reshape(1, D), b_gate, gla_norm_g, w_gate_up[0], conv_rows(conv_w))
    small_m = (m_norm1_g, m_norm2_g, m_norm_f_g.reshape(1, D), m_b_gate, m_gla_norm_g, m_w_gate_up[0],
               conv_rows(m_conv_w))
    small_v = (v_norm1_g, v_norm2_g, v_norm_f_g.reshape(1, D), v_b_gate, v_gla_norm_g, v_w_gate_up[0],
               conv_rows(v_conv_w))
    so = _small_adamw(small_sums, small_w, small_m, small_v)
    loss = so[0].reshape(())
    to_shape = {"norm_f_g": lambda o: o.reshape(D), "w_gate_up": lambda o: o[None],
                "conv_w": lambda o: o.transpose(1, 0, 2)}
    for i, name in enumerate(SMALL_NAMES):
        results[name] = [to_shape.get(name, lambda o: o)(o) for o in so[1 + 4 * i:5 + 4 * i]]

    names = ("norm1_g", "w_in", "w_gate_up", "b_gate", "gla_norm_g", "conv_w", "w_out", "norm2_g",
             "w_ffn_gate", "w_ffn_up", "w_ffn_down", "norm_f_g")
    outs = [loss, dx.reshape(nb, s, D)]
    for kind in range(4):
        for name in names:
            outs.append(results[name][kind])
    return tuple(outs)
```

```python
import jax
import jax.numpy as jnp
from jax import lax
from jax.experimental import pallas as pl
from jax.experimental.pallas import tpu as pltpu

F32 = jnp.float32
BF16 = jnp.bfloat16
HIGHEST = lax.Precision.HIGHEST
MESH = pl.DeviceIdType.MESH

N_DEV = 8
D = 1024
DFF = 2816
HEADS = 4
DK = 64
DV = 128
NQK = HEADS * DK
NV = HEADS * DV
RANK = 16
CHUNK = 64
CW = 512
CONV_K = 3
IN_COLS = 3088
EPS = 1e-6
INV_GATE_NORM = 1.0 / 16.0
Q_SCALE = DK ** -0.5

PW = 3200
OQ, OK_, OV, OG, OCB, OCC, OCH, OA = 0, 256, 512, 1024, 1536, 2048, 2560, 3072
A_PAD = 128

ADAM_LR = 0.001
ADAM_B1 = 0.9
ADAM_B2 = 0.999
ADAM_EPS = 1e-08
ADAM_WD = 0.01
ADAM_STEP = 10

IN_W = IN_COLS // N_DEV
IN_ROWS = 400
FF_W = DFF // N_DEV
OUT_ROWS = D // N_DEV
SLAB_IN = 0
SLAB_G = SLAB_IN + IN_ROWS
SLAB_U = SLAB_G + FF_W
SLAB_D = SLAB_U + FF_W
SLAB_O = SLAB_D + FF_W
SLAB_ROWS = SLAB_O + OUT_ROWS
D_HEAD = 128
D_TAIL = FF_W - D_HEAD
SLAB_SPLIT = SLAB_D + D_HEAD

VMEM_LIMIT = 56 * 1024 * 1024


def _params(sem=None, vmem=VMEM_LIMIT):
    return pltpu.CompilerParams(dimension_semantics=sem, vmem_limit_bytes=vmem)


def _nt(a, b):
    return lax.dot_general(a, b, (((1,), (1,)), ((), ())), preferred_element_type=F32)


def _tn(a, b, precision=None):
    return lax.dot_general(a, b, (((0,), (0,)), ((), ())), preferred_element_type=F32, precision=precision)


def _mm(a, b, precision=None):
    return jnp.dot(a, b, preferred_element_type=F32, precision=precision)


def _in_segments():
    segs = []
    for j in range(N_DEV):
        lo, hi = IN_W * j, IN_W * (j + 1)
        cuts = sorted({lo, hi} | {c for c in (OCB, OCB + RANK) if lo < c < hi})
        for a, b in zip(cuts[:-1], cuts[1:]):
            if a < OCB:
                d = a
            elif a < OCB + RANK:
                d = OA + (a - OCB)
            else:
                d = a - RANK
            segs.append((j, a - lo, b - lo, d))
    return segs


def _in_proj_fwd(x2d, g1, w_in_t, wgu_p, b_gate, tm, stage):
    t = x2d.shape[0]
    nt = t // tm
    g_rows = SLAB_ROWS - SLAB_SPLIT

    def body(x_ref, g_ref, w_ref, wgu_ref, bg_ref, stage_hbm, proj_ref, z_ref, h_ref, gwb_ref,
             send_sems, recv_sems, local_sem):
        gargs = (stage_hbm, SLAB_SPLIT, g_rows, gwb_ref, send_sems, recv_sems, local_sem)

        @pl.when(pl.program_id(0) == 0)
        def _():
            _gather_start(*gargs)

        @pl.when(pl.program_id(0) == RELAY_AT * nt // 8)
        def _():
            _gather_relay(*gargs)

        @pl.when(pl.program_id(0) == FORWARD_AT * nt // 8)
        def _():
            _gather_forward(*gargs)

        x = x_ref[...]
        r = lax.rsqrt(jnp.mean(x * x, axis=-1, keepdims=True) + EPS)
        h = ((x * r) * g_ref[...]).astype(BF16)
        h_ref[...] = h
        proj = _nt(h, w_ref[...])
        proj_ref[...] = proj
        pa = proj[:, OA:OA + A_PAD].astype(BF16)
        z_ref[...] = _mm(pa, wgu_ref[...]) + bg_ref[...]

        @pl.when(pl.program_id(0) == nt - 1)
        def _():
            _gather_finish(*gargs)

    return pl.pallas_call(
        body,
        name="in_proj_fwd",
        grid=(t // tm,),
        in_specs=[
            pl.BlockSpec((tm, D), lambda i: (i, 0)),
            pl.BlockSpec((1, D), lambda i: (0, 0)),
            pl.BlockSpec((PW, D), lambda i: (0, 0)),
            pl.BlockSpec((A_PAD, NQK), lambda i: (0, 0)),
            pl.BlockSpec((1, NQK), lambda i: (0, 0)),
            pl.BlockSpec(memory_space=pl.ANY),
        ],
        out_specs=[
            pl.BlockSpec((tm, PW), lambda i: (i, 0)),
            pl.BlockSpec((tm, NQK), lambda i: (i, 0)),
            pl.BlockSpec((tm, D), lambda i: (i, 0)),
            pl.BlockSpec(memory_space=pl.ANY),
        ],
        out_shape=[
            jax.ShapeDtypeStruct((t, PW), F32),
            jax.ShapeDtypeStruct((t, NQK), F32),
            jax.ShapeDtypeStruct((t, D), BF16),
            jax.ShapeDtypeStruct((N_DEV, g_rows, D), BF16),
        ],
        scratch_shapes=_gather_sems(),
        compiler_params=_params(("arbitrary",)),
    )(x2d, g1, w_in_t, wgu_p, b_gate, stage)


def _head_masks():
    lane = lax.broadcasted_iota(jnp.int32, (1, NQK), 1)
    return [(lane >= DK * h) & (lane < DK * (h + 1)) for h in range(HEADS)]


def _split_bf16(x, n):
    parts = []
    for _ in range(n):
        p = x.astype(BF16)
        parts.append(p)
        x = x - p.astype(F32)
    return parts


def _chunk_fwd_parts(q, k, z, tril16):
    la = (jnp.minimum(z, 0.0) - jnp.log1p(jnp.exp(-jnp.abs(z)))) * INV_GATE_NORM
    la_parts = _split_bf16(la, 3)
    bc = _mm(tril16, la_parts[0]) + _mm(tril16, la_parts[1]) + _mm(tril16, la_parts[2])
    bl = bc[CHUNK - 1:CHUNK, :]
    eb = jnp.exp(bc)
    enb = jnp.exp(-bc)
    ekl = jnp.exp(bl - bc)
    qi = (q * Q_SCALE) * eb
    ki = k * enb
    ks = k * ekl
    ones16 = jnp.ones((CHUNK, DV), BF16)
    decb = jnp.exp(_tn(la_parts[0], ones16) + _tn(la_parts[1], ones16) + _tn(la_parts[2], ones16))
    return la, eb, enb, ekl, qi, ki, ks, decb


def _stack_heads(a, masks):
    return jnp.concatenate([jnp.where(m, a, 0.0) for m in masks], axis=0)


def _merge_heads(blocks, masks):
    out = blocks[HEADS - 1]
    for h in range(HEADS - 2, -1, -1):
        out = jnp.where(masks[h], blocks[h], out)
    return out


def _causal_stack_mask():
    row = lax.broadcasted_iota(jnp.int32, (HEADS * CHUNK, CHUNK), 0)
    col = lax.broadcasted_iota(jnp.int32, (HEADS * CHUNK, CHUNK), 1)
    return (row & (CHUNK - 1)) >= col


def _conv_taps(u, uprev):
    row = lax.broadcasted_iota(jnp.int32, u.shape, 0)
    u1 = jnp.where(row < 1, pltpu.roll(uprev, 1, 0), pltpu.roll(u, 1, 0))
    u2 = jnp.where(row < 2, pltpu.roll(uprev, 2, 0), pltpu.roll(u, 2, 0))
    return u1, u2


def _mix_fwd(proj3, z3, gng, conv_w, stage, x3, gwb):
    nb, s, _ = proj3.shape
    nc = s // CHUNK
    g_rows = SLAB_SPLIT - SLAB_G

    def body(p_ref, z_ref, gng_ref, cw_ref, stage_hbm, x_ref, gwb_hbm, mix_ref, o_ref, sprev_ref, x1_ref, gwa_ref,
             s_ref, uprev_ref, wo, wsem, send_sems, recv_sems, local_sem):
        n = pl.program_id(0)
        gargs = (stage_hbm, SLAB_G, g_rows, gwa_ref, send_sems, recv_sems, local_sem)

        @pl.when(n == 0)
        def _():
            _gather_start(*gargs)
            loads = [pltpu.make_async_copy(gwb_hbm.at[j, pl.ds(D_TAIL, OUT_ROWS), :],
                                           wo.at[pl.ds(OUT_ROWS * j, OUT_ROWS), :], wsem.at[j]) for j in range(N_DEV)]
            for cp in loads:
                cp.start()
            s_ref[...] = jnp.zeros_like(s_ref)
            uprev_ref[...] = jnp.zeros_like(uprev_ref)
            for cp in loads:
                cp.wait()

        @pl.when(n == RELAY_AT * nc // 8)
        def _():
            _gather_relay(*gargs)

        @pl.when(n == FORWARD_AT * nc // 8)
        def _():
            _gather_forward(*gargs)

        r_i = lax.broadcasted_iota(jnp.int32, (CHUNK, CHUNK), 0)
        c_i = lax.broadcasted_iota(jnp.int32, (CHUNK, CHUNK), 1)
        tril16 = (r_i >= c_i).astype(BF16)
        masks = _head_masks()
        cmask = _causal_stack_mask()
        gg = gng_ref[...]
        for b in range(nb):
            q = p_ref[b, :, OQ:OQ + NQK]
            k = p_ref[b, :, OK_:OK_ + NQK]
            _, _, _, _, qi, ki, ks, decb = _chunk_fwd_parts(q, k, z_ref[b], tril16)
            qs = _stack_heads(qi, masks).astype(BF16)
            sc = jnp.where(cmask, _nt(qs, ki.astype(BF16)), 0.0).astype(BF16)
            st = s_ref[b]
            sprev_ref[b, 0] = st
            o_inter = _mm(qs, st.astype(BF16))
            v16 = p_ref[b, :, OV:OV + NV].astype(BF16)
            kv = _tn(ks.astype(BF16), v16)
            for h in range(HEADS):
                rows = slice(CHUNK * h, CHUNK * (h + 1))
                cols = slice(DV * h, DV * (h + 1))
                o = _mm(sc[rows], v16[:, cols]) + o_inter[rows]
                o_ref[b, :, cols] = o
                r = lax.rsqrt(jnp.mean(o * o, axis=-1, keepdims=True) + EPS)
                on = (o * r) * gg
                g = p_ref[b, :, OG + DV * h:OG + DV * (h + 1)]
                mix_ref[b, :, cols] = (on * (g * jax.nn.sigmoid(g))).astype(BF16)
                s_ref[b, rows, :] = decb[rows] * st[rows] + kv[rows, cols]
            u = p_ref[b, :, OCC:OCC + CW] * p_ref[b, :, OCH:OCH + CW]
            u1, u2 = _conv_taps(u, uprev_ref[b])
            yc = cw_ref[0:1, :] * u2 + cw_ref[1:2, :] * u1 + cw_ref[2:3, :] * u
            mix_ref[b, :, NV:NV + CW] = (p_ref[b, :, OCB:OCB + CW] * yc).astype(BF16)
            uprev_ref[b] = u
        mixed = _mm(jnp.concatenate([mix_ref[b] for b in range(nb)], axis=0), wo[...])
        for b in range(nb):
            x1_ref[b] = x_ref[b] + mixed[CHUNK * b:CHUNK * (b + 1)]

        @pl.when(n == nc - 1)
        def _():
            _gather_finish(*gargs)

    return pl.pallas_call(
        body,
        name="mix_fwd",
        grid=(nc,),
        in_specs=[
            pl.BlockSpec((nb, CHUNK, PW), lambda n: (0, n, 0)),
            pl.BlockSpec((nb, CHUNK, NQK), lambda n: (0, n, 0)),
            pl.BlockSpec((1, DV), lambda n: (0, 0)),
            pl.BlockSpec((CONV_K, CW), lambda n: (0, 0)),
            pl.BlockSpec(memory_space=pl.ANY),
            pl.BlockSpec((nb, CHUNK, D), lambda n: (0, n, 0)),
            pl.BlockSpec(memory_space=pl.ANY),
        ],
        out_specs=[
            pl.BlockSpec((nb, CHUNK, D), lambda n: (0, n, 0)),
            pl.BlockSpec((nb, CHUNK, NV), lambda n: (0, n, 0)),
            pl.BlockSpec((nb, 1, NQK, DV), lambda n: (0, n, 0, 0)),
            pl.BlockSpec((nb, CHUNK, D), lambda n: (0, n, 0)),
            pl.BlockSpec(memory_space=pl.ANY),
        ],
        out_shape=[
            jax.ShapeDtypeStruct((nb, s, D), BF16),
            jax.ShapeDtypeStruct((nb, s, NV), F32),
            jax.ShapeDtypeStruct((nb, nc, NQK, DV), F32),
            jax.ShapeDtypeStruct((nb, s, D), F32),
            jax.ShapeDtypeStruct((N_DEV, g_rows, D), BF16),
        ],
        scratch_shapes=[pltpu.VMEM((nb, NQK, DV), F32), pltpu.VMEM((nb, CHUNK, CW), F32),
                        pltpu.VMEM((D, D), BF16), pltpu.SemaphoreType.DMA((N_DEV,))] + _gather_sems(),
        compiler_params=_params(("arbitrary",)),
    )(proj3, z3, gng, conv_w, stage, x3, gwb)


def _ffn_fwd_bwd(x1_2d, tgt2d, gwa, gwb, g2, gf, tm):
    t = x1_2d.shape[0]

    def body(x1_ref, tgt_ref, g2_ref, gf_ref, gwa_hbm, gwb_hbm,
             dx1_ref, dx1b_ref, adu_ref, hb_ref, dg2_ref, dgf_ref, loss_ref,
             wg, wu, wd, wsem):
        i = pl.program_id(0)

        def weight_copies(n, dst, src, off, rows, at=0):
            return [pltpu.make_async_copy(src.at[j, pl.ds(off, rows), :], dst.at[pl.ds(FF_W * j + at, rows), :],
                                          wsem.at[N_DEV * n + j]) for j in range(N_DEV)]

        loads = (weight_copies(0, wg, gwa_hbm, 0, FF_W), weight_copies(1, wu, gwa_hbm, FF_W, FF_W),
                 weight_copies(2, wd, gwa_hbm, 2 * FF_W, D_HEAD), weight_copies(3, wd, gwb_hbm, 0, D_TAIL, D_HEAD))

        @pl.when(i == 0)
        def _():
            for group in loads:
                for cp in group:
                    cp.start()
            dg2_ref[...] = jnp.zeros_like(dg2_ref)
            dgf_ref[...] = jnp.zeros_like(dgf_ref)
            loss_ref[...] = jnp.zeros_like(loss_ref)
            for group in loads:
                for cp in group:
                    cp.wait()

        g2v = g2_ref[...]
        gfv = gf_ref[...]
        x1 = x1_ref[...]
        r2 = lax.rsqrt(jnp.mean(x1 * x1, axis=-1, keepdims=True) + EPS)
        n2 = x1 * r2
        h2 = (n2 * g2v).astype(BF16)
        hb_ref[1] = h2
        gate = _nt(h2, wg[...])
        up = _nt(h2, wu[...])
        sg = jax.nn.sigmoid(gate)
        sil = gate * sg
        act = (sil * up).astype(BF16)
        adu_ref[0] = act
        x2 = x1 + _mm(act, wd[...])
        rf = lax.rsqrt(jnp.mean(x2 * x2, axis=-1, keepdims=True) + EPS)
        nf = x2 * rf
        err = nf * gfv - tgt_ref[...]
        loss_ref[...] += 0.5 * jnp.sum(jnp.mean(err * err, axis=-1, keepdims=True))
        dy = err * (1.0 / D)
        dgf_ref[...] += jnp.sum(dy * nf, axis=0, keepdims=True)
        dnf = dy * gfv
        dx2 = rf * (dnf - nf * jnp.mean(dnf * nf, axis=-1, keepdims=True))
        dx2b = dx2.astype(BF16)
        hb_ref[0] = dx2b
        dact = _nt(dx2b, wd[...])
        dup = (dact * sil).astype(BF16)
        dgate = ((dact * up) * (sg * (1.0 + gate * (1.0 - sg)))).astype(BF16)
        adu_ref[2] = dup
        adu_ref[1] = dgate
        dh2 = _mm(dgate, wg[...]) + _mm(dup, wu[...])
        dg2_ref[...] += jnp.sum(dh2 * n2, axis=0, keepdims=True)
        dn2 = dh2 * g2v
        dx1 = dx2 + r2 * (dn2 - n2 * jnp.mean(dn2 * n2, axis=-1, keepdims=True))
        dx1_ref[...] = dx1
        dx1b_ref[...] = dx1.astype(BF16)

    tile = lambda w: pl.BlockSpec((tm, w), lambda i: (i, 0))
    vec = pl.BlockSpec((1, D), lambda i: (0, 0))
    hbm = pl.BlockSpec(memory_space=pl.ANY)
    return pl.pallas_call(
        body,
        name="ffn_fwd_bwd",
        grid=(t // tm,),
        in_specs=[tile(D), tile(D), vec, vec, hbm, hbm],
        out_specs=[tile(D), tile(D), pl.BlockSpec((3, tm, DFF), lambda i: (0, i, 0)),
                   pl.BlockSpec((2, tm, D), lambda i: (0, i, 0)), vec, vec,
                   pl.BlockSpec((1, 128), lambda i: (0, 0))],
        out_shape=[
            jax.ShapeDtypeStruct((t, D), F32),
            jax.ShapeDtypeStruct((t, D), BF16),
            jax.ShapeDtypeStruct((3, t, DFF), BF16),
            jax.ShapeDtypeStruct((2, t, D), BF16),
            jax.ShapeDtypeStruct((1, D), F32),
            jax.ShapeDtypeStruct((1, D), F32),
            jax.ShapeDtypeStruct((1, 128), F32),
        ],
        scratch_shapes=[pltpu.VMEM((DFF, D), BF16), pltpu.VMEM((DFF, D), BF16), pltpu.VMEM((DFF, D), BF16),
                        pltpu.SemaphoreType.DMA((4 * N_DEV,))],
        compiler_params=_params(("arbitrary",)),
    )(x1_2d, tgt2d, g2, gf, gwa, gwb)


def _stage2_rider(pbs):
    return dict(inputs=list(pbs), out_shape=[jax.ShapeDtypeStruct((2,) + p.shape[1:], BF16) for p in pbs],
                scratch=_stage2_scratch(pbs))


def _tn_matmul(a, b, bm, bn, tk, name, with_bf16, rider=None):
    t, m = a.shape
    n = b.shape[1]
    nk = t // tk
    nout = 2 if with_bf16 else 1
    grid = (m // bm, n // bn, nk)
    steps = grid[0] * grid[1] * nk
    r_in = [] if rider is None else rider["inputs"]
    r_out = [] if rider is None else rider["out_shape"]

    def body(a_ref, b_ref, *rest):
        ins, outs = rest[:len(r_in)], rest[len(r_in):len(r_in) + nout]
        r_outs, scratch = rest[len(r_in) + nout:len(r_in) + nout + len(r_out)], rest[len(r_in) + nout + len(r_out):]
        o_ref = outs[0]
        i, j, k = pl.program_id(0), pl.program_id(1), pl.program_id(2)
        step = (i * grid[1] + j) * nk + k
        if rider is not None:
            @pl.when(step == 0)
            def _():
                _stage2_start(ins, r_outs, scratch)

            @pl.when(step == steps // 2)
            def _():
                _stage2_combine(ins, r_outs, scratch)

        @pl.when(k == 0)
        def _():
            o_ref[...] = jnp.zeros_like(o_ref)

        o_ref[...] += _tn(a_ref[...].astype(BF16), b_ref[...].astype(BF16))
        if with_bf16:
            @pl.when(k == nk - 1)
            def _():
                outs[1][...] = o_ref[...].astype(BF16)
        if rider is not None:
            @pl.when(step == steps - 1)
            def _():
                _stage2_finish(ins, r_outs, scratch)

    out_blk = pl.BlockSpec((bm, bn), lambda i, j, k: (i, j))
    hbm = pl.BlockSpec(memory_space=pl.ANY)
    out_shape = [jax.ShapeDtypeStruct((m, n), F32)] + ([jax.ShapeDtypeStruct((m, n), BF16)] if with_bf16 else [])
    res = pl.pallas_call(
        body,
        name=name,
        grid=grid,
        in_specs=[pl.BlockSpec((tk, bm), lambda i, j, k: (k, i)), pl.BlockSpec((tk, bn), lambda i, j, k: (k, j))]
        + [hbm] * len(r_in),
        out_specs=[out_blk] * nout + [hbm] * len(r_out),
        out_shape=out_shape + list(r_out),
        scratch_shapes=[] if rider is None else rider["scratch"],
        compiler_params=_params(("parallel", "parallel", "arbitrary") if rider is None
                                else ("arbitrary", "arbitrary", "arbitrary")),
    )(a, b, *r_in)
    return res[0] if len(res) == 1 else res


def _dw_ffn(adu, hb, tk):
    _, t, _ = adu.shape
    bm = DFF // 2
    nk = t // tk

    def body(a_ref, b_ref, o_ref, ob_ref):
        k = pl.program_id(2)

        @pl.when(k == 0)
        def _():
            o_ref[...] = jnp.zeros_like(o_ref)

        o_ref[...] += _tn(a_ref[...], b_ref[...])

        @pl.when(k == nk - 1)
        def _():
            ob_ref[...] = o_ref[...].astype(BF16)

    out_blk = pl.BlockSpec((None, bm, D), lambda p, i, k: (p, i, 0))
    return pl.pallas_call(
        body,
        name="dw_ffn",
        grid=(3, DFF // bm, nk),
        in_specs=[pl.BlockSpec((None, tk, bm), lambda p, i, k: (p, k, i)),
                  pl.BlockSpec((None, tk, D), lambda p, i, k: (jnp.minimum(p, 1), k, 0))],
        out_specs=[out_blk, out_blk],
        out_shape=[jax.ShapeDtypeStruct((3, DFF, D), F32), jax.ShapeDtypeStruct((3, DFF, D), BF16)],
        compiler_params=_params(("arbitrary", "arbitrary", "arbitrary")),
    )(adu, hb)


def _mix_bwd(proj3, z3, sprev, opre3, dmix3, gng, conv_w, wgu_p, pbs):
    nb, s, _ = proj3.shape
    nc = s // CHUNK
    na = len(pbs)

    def body(*refs):
        (p_ref, pprev_ref, z_ref, sp_ref, o_ref, dm_ref, gng_ref, cw_ref, wgu_ref) = refs[:9]
        pb_refs = refs[9:9 + na]
        (dproj_ref, dgng_ref, dcw_ref, dbg_ref, dwgu_ref) = refs[9 + na:14 + na]
        r2_refs = refs[14 + na:14 + 2 * na]
        ds_ref, dycn_ref = refs[14 + 2 * na:16 + 2 * na]
        stage2 = (pb_refs, r2_refs, refs[16 + 2 * na:])
        step = pl.program_id(0)
        n = nc - 1 - step

        @pl.when(step == 0)
        def _():
            _stage2_start(*stage2)
            ds_ref[...] = jnp.zeros_like(ds_ref)
            dycn_ref[...] = jnp.zeros_like(dycn_ref)
            dgng_ref[...] = jnp.zeros_like(dgng_ref)
            dcw_ref[...] = jnp.zeros_like(dcw_ref)
            dbg_ref[...] = jnp.zeros_like(dbg_ref)
            dwgu_ref[...] = jnp.zeros_like(dwgu_ref)

        @pl.when(step == RELAY_AT * nc // 8)
        def _():
            _stage2_combine(*stage2)

        r_i = lax.broadcasted_iota(jnp.int32, (CHUNK, CHUNK), 0)
        c_i = lax.broadcasted_iota(jnp.int32, (CHUNK, CHUNK), 1)
        tril16 = (r_i >= c_i).astype(BF16)
        triu16 = (r_i <= c_i).astype(BF16)
        causal = r_i >= c_i
        masks = _head_masks()
        cmask = _causal_stack_mask()
        gg = gng_ref[...]
        last_row = lax.broadcasted_iota(jnp.int32, (CHUNK, NQK), 0) == CHUNK - 1
        ones_r = jnp.ones((16, DV), BF16)
        has_prev = (n > 0).astype(F32)
        for b in range(nb):
            q = p_ref[b, :, OQ:OQ + NQK]
            k = p_ref[b, :, OK_:OK_ + NQK]
            z = z_ref[b]
            _, eb, enb, ekl, qi, ki, ks, decb = _chunk_fwd_parts(q, k, z, tril16)
            qi16 = qi.astype(BF16)
            ki16 = ki.astype(BF16)
            qs = _stack_heads(qi, masks).astype(BF16)
            sc = jnp.where(cmask, _nt(qs, ki16), 0.0).astype(BF16)
            st = sp_ref[b, 0]
            st16 = st.astype(BF16)
            dsn = ds_ref[b]
            dsn16 = dsn.astype(BF16)
            v16 = p_ref[b, :, OV:OV + NV].astype(BF16)
            do16 = []
            dgng = jnp.zeros((1, DV), F32)
            for h in range(HEADS):
                cols = slice(DV * h, DV * (h + 1))
                o = o_ref[b, :, cols]
                r = lax.rsqrt(jnp.mean(o * o, axis=-1, keepdims=True) + EPS)
                nh = o * r
                g = p_ref[b, :, OG + DV * h:OG + DV * (h + 1)]
                sg = jax.nn.sigmoid(g)
                dog = dm_ref[b, :, cols]
                dproj_ref[b, :, OG + DV * h:OG + DV * (h + 1)] = (
                    (dog * (nh * gg)) * (sg * (1.0 + g * (1.0 - sg)))).astype(BF16)
                don = dog * (g * sg)
                dgng = dgng + jnp.sum(don * nh, axis=0, keepdims=True)
                dn = don * gg
                do = r * (dn - nh * jnp.mean(dn * nh, axis=-1, keepdims=True))
                do16.append(do.astype(BF16))
            dgng_ref[...] += dgng
            do_rows = jnp.concatenate(do16, axis=0)
            v_rows = jnp.concatenate([v16[:, DV * h:DV * (h + 1)] for h in range(HEADS)], axis=0)
            dp16 = [jnp.where(causal, _nt(do16[h], v16[:, DV * h:DV * (h + 1)]), 0.0).astype(BF16)
                    for h in range(HEADS)]
            ks_dsn = _mm(_stack_heads(ks, masks).astype(BF16), dsn16)
            do_st = _nt(do_rows, st16)
            v_dsn = _nt(v_rows, dsn16)
            dp_ki = _mm(jnp.concatenate(dp16, axis=0), ki16)
            q_do = _tn(qi16, jnp.concatenate(do16, axis=1))
            dki_h = []
            for h in range(HEADS):
                rows = slice(CHUNK * h, CHUNK * (h + 1))
                cols = slice(DV * h, DV * (h + 1))
                dv = _tn(sc[rows], do16[h]) + ks_dsn[rows]
                dproj_ref[b, :, OV + DV * h:OV + DV * (h + 1)] = dv.astype(BF16)
                dki_h.append(_tn(dp16[h], qi16))
                ds_ref[b, rows, :] = decb[rows] * dsn[rows] + q_do[rows, cols]
            blocks = lambda a: [a[CHUNK * h:CHUNK * (h + 1)] for h in range(HEADS)]
            dqi = _merge_heads(blocks(dp_ki + do_st), masks)
            dki = _merge_heads(dki_h, masks)
            dks = _merge_heads(blocks(v_dsn), masks)
            dproj_ref[b, :, OQ:OQ + NQK] = (dqi * (Q_SCALE * eb)).astype(BF16)
            dproj_ref[b, :, OK_:OK_ + NQK] = (dki * enb + dks * ekl).astype(BF16)
            dks_ks = dks * ks
            db = dqi * qi - dki * ki - dks_ks
            sd = _split_bf16(dsn * st * decb, 2)
            dbl = jnp.sum(dks_ks, axis=0, keepdims=True) + (_nt(ones_r, sd[0]) + _nt(ones_r, sd[1]))[0:1, :]
            db = db + jnp.where(last_row, dbl, 0.0)
            db_parts = _split_bf16(db, 3)
            dla = _mm(triu16, db_parts[0]) + _mm(triu16, db_parts[1]) + _mm(triu16, db_parts[2])
            dz = (dla * INV_GATE_NORM) * (1.0 / (1.0 + jnp.exp(z)))
            dbg_ref[...] += jnp.sum(dz, axis=0, keepdims=True)
            dz16 = dz.astype(BF16)
            pa16 = p_ref[b, :, OA:OA + A_PAD].astype(BF16)
            dwgu_ref[...] += _tn(pa16, dz16)
            dproj_ref[b, :, OA:OA + A_PAD] = _nt(dz16, wgu_ref[...]).astype(BF16)
            cb = p_ref[b, :, OCB:OCB + CW]
            cc = p_ref[b, :, OCC:OCC + CW]
            ch = p_ref[b, :, OCH:OCH + CW]
            u = cc * ch
            uprev = (pprev_ref[b, :, 0:CW] * pprev_ref[b, :, CW:2 * CW]) * has_prev
            u1, u2 = _conv_taps(u, uprev)
            w0 = cw_ref[0:1, :]
            w1 = cw_ref[1:2, :]
            w2 = cw_ref[2:3, :]
            yc = w0 * u2 + w1 * u1 + w2 * u
            doc = dm_ref[b, :, NV:NV + CW]
            dproj_ref[b, :, OCB:OCB + CW] = (doc * yc).astype(BF16)
            dyc = doc * cb
            dycn = dycn_ref[b]
            row = lax.broadcasted_iota(jnp.int32, dyc.shape, 0)
            d1 = jnp.where(row >= CHUNK - 1, pltpu.roll(dycn, CHUNK - 1, 0), pltpu.roll(dyc, CHUNK - 1, 0))
            d2 = jnp.where(row >= CHUNK - 2, pltpu.roll(dycn, CHUNK - 2, 0), pltpu.roll(dyc, CHUNK - 2, 0))
            du = w2 * dyc + w1 * d1 + w0 * d2
            dproj_ref[b, :, OCC:OCC + CW] = (du * ch).astype(BF16)
            dproj_ref[b, :, OCH:OCH + CW] = (du * cc).astype(BF16)
            dcw_ref[0:1, :] += jnp.sum(dyc * u2, axis=0, keepdims=True)
            dcw_ref[1:2, :] += jnp.sum(dyc * u1, axis=0, keepdims=True)
            dcw_ref[2:3, :] += jnp.sum(dyc * u, axis=0, keepdims=True)
            dycn_ref[b] = dyc

        @pl.when(step == nc - 1)
        def _():
            _stage2_finish(*stage2)

    rev =lambda w: pl.BlockSpec((nb, CHUNK, w), lambda i: (0, nc - 1 - i, 0))
    const = lambda r, c: pl.BlockSpec((r, c), lambda i: (0, 0))
    hbm = pl.BlockSpec(memory_space=pl.ANY)
    return pl.pallas_call(
        body,
        name="mix_bwd",
        grid=(nc,),
        in_specs=[
            rev(PW),
            pl.BlockSpec((nb, CHUNK, 2 * CW), lambda i: (0, jnp.maximum(nc - 2 - i, 0), OCC // (2 * CW))),
            rev(NQK),
            pl.BlockSpec((nb, 1, NQK, DV), lambda i: (0, nc - 1 - i, 0, 0)),
            rev(NV),
            rev(D),
            const(1, DV),
            const(CONV_K, CW),
            const(A_PAD, NQK),
        ] + [hbm] * na,
        out_specs=[rev(PW), const(1, DV), const(8, CW), const(1, NQK), const(A_PAD, NQK)] + [hbm] * na,
        out_shape=[
            jax.ShapeDtypeStruct((nb, s, PW), BF16),
            jax.ShapeDtypeStruct((1, DV), F32),
            jax.ShapeDtypeStruct((8, CW), F32),
            jax.ShapeDtypeStruct((1, NQK), F32),
            jax.ShapeDtypeStruct((A_PAD, NQK), F32),
        ] + [jax.ShapeDtypeStruct((2,) + p.shape[1:], BF16) for p in pbs],
        scratch_shapes=[pltpu.VMEM((nb, NQK, DV), F32), pltpu.VMEM((nb, CHUNK, CW), F32)] + _stage2_scratch(pbs),
        compiler_params=_params(("arbitrary",)),
    )(proj3, proj3, z3, sprev, opre3, dmix3, gng, conv_w, wgu_p, *pbs)


RING = 3
SMALL_PACK_ROWS = 16


def _wgu_slot(r):
    return 4 + r // 4, NQK * (r % 4)


CONV_SLOTS = ((8, 0), (8, CW), (9, 0))


def _in_proj_bwd(dproj2d, x2d, dx1, g1, w_in_t, tm, pb, small_parts):
    t = x2d.shape[0]
    nt = t // tm

    def body(dp_hbm, x_hbm, dx1_hbm, g_ref, w_ref, pb_ref, dg2, dgf, dbg, dgng, dwgu, dcw, lp,
             dx_ref, sums_ref, r2_ref, dg1_acc, pack, gbuf, pack1, gbuf1, ssend, srecv, ssend1, srecv1,
             dp_ring, x_ring, dx1_ring, ring_sems, *scratch2):
        stage2 = ([pb_ref], [r2_ref], scratch2)
        step = pl.program_id(0)

        def fetch(s):
            slot = s % RING
            return [pltpu.make_async_copy(src.at[pl.ds(s * tm, tm), :], ring.at[slot], ring_sems.at[n, slot])
                    for n, (src, ring) in enumerate(((dp_hbm, dp_ring), (x_hbm, x_ring), (dx1_hbm, dx1_ring)))]

        @pl.when(step == 0)
        def _():
            for s in range(min(RING - 1, nt)):
                for cp in fetch(s):
                    cp.start()

        @pl.when(step + RING - 1 < nt)
        def _():
            for cp in fetch(step + RING - 1):
                cp.start()

        for cp in fetch(step):
            cp.wait()
        dp_ref, x_ref, dx1_ref = (ring.at[step % RING] for ring in (dp_ring, x_ring, dx1_ring))
        x, y, c = _position()
        me = 4 * x + 2 * y + c
        flips = [(k >> 2, (k >> 1) & 1, k & 1) for k in range(1, N_DEV)]
        peers = [(x ^ fx, y ^ fy, c ^ fc) for fx, fy, fc in flips]

        def small_copies(src, dst, send, recv, arrivals):
            return [pltpu.make_async_remote_copy(
                src_ref=src, dst_ref=dst.at[4 * px + 2 * py + pc if arrivals else me],
                send_sem=send.at[k], recv_sem=recv.at[k], device_id=(px, py, pc), device_id_type=MESH)
                for k, (px, py, pc) in enumerate(peers)]

        @pl.when(pl.program_id(0) == 0)
        def _():
            _stage2_start(*stage2)
            dg1_acc[...] = jnp.zeros_like(dg1_acc)
            pack[...] = jnp.zeros_like(pack)
            pack[1:2, :] = dg2[...]
            pack[2:3, :] = dgf[...]
            pack[3:4, 0:NQK] = dbg[...]
            pack[3:4, NQK:NQK + DV] = dgng[...]
            pack[3:4, NQK + DV:NQK + 2 * DV] = lp[...]
            for r in range(RANK):
                row, lane = _wgu_slot(r)
                pack[row:row + 1, lane:lane + NQK] = dwgu[r:r + 1, :]
            for r, (row, lane) in enumerate(CONV_SLOTS):
                pack[row:row + 1, lane:lane + CW] = dcw[r:r + 1, :]
            for cp in small_copies(pack, gbuf, ssend, srecv, False):
                cp.start()
            gbuf[me] = pack[...]

        @pl.when(pl.program_id(0) == RELAY_AT * nt // 8)
        def _():
            _stage2_combine(*stage2)

        xv = x_ref[...]
        r = lax.rsqrt(jnp.mean(xv * xv, axis=-1, keepdims=True) + EPS)
        n1 = xv * r
        dh = _mm(dp_ref[...], w_ref[...])
        dg1_acc[...] += jnp.sum(dh * n1, axis=0, keepdims=True)
        dn = dh * g_ref[...]
        dx_ref[...] = dx1_ref[...] + r * (dn - n1 * jnp.mean(dn * n1, axis=-1, keepdims=True))

        @pl.when(pl.program_id(0) == nt - 1)
        def _():
            pack1[...] = jnp.zeros_like(pack1)
            pack1[0:1, :] = dg1_acc[...]
            for cp in small_copies(pack1, gbuf1, ssend1, srecv1, False):
                cp.start()
            gbuf1[me] = pack1[...]
            _stage2_finish(*stage2)
            for src, dst, send, recv in ((pack, gbuf, ssend, srecv), (pack1, gbuf1, ssend1, srecv1)):
                for cp in small_copies(src, dst, send, recv, True):
                    cp.wait_recv()
                    cp.wait_send()
            acc = gbuf[0]
            acc1 = gbuf1[0]
            for d in range(1, N_DEV):
                acc = acc + gbuf[d]
                acc1 = acc1 + gbuf1[d]
            sums_ref[...] = acc
            sums_ref[0:1, :] = acc1[0:1, :]

    tile = lambda w: pl.BlockSpec((tm, w), lambda i: (i, 0))
    vec = pl.BlockSpec((1, D), lambda i: (0, 0))
    hbm = pl.BlockSpec(memory_space=pl.ANY)
    whole = lambda a: pl.BlockSpec(a.shape, lambda i: (0,) * a.ndim)
    return pl.pallas_call(
        body,
        name="in_proj_bwd",
        grid=(nt,),
        in_specs=[hbm, hbm, hbm, vec, pl.BlockSpec((PW, D), lambda i: (0, 0)), hbm]
        + [whole(a) for a in small_parts],
        out_specs=[tile(D), pl.BlockSpec((SMALL_PACK_ROWS, D), lambda i: (0, 0)), hbm],
        out_shape=[jax.ShapeDtypeStruct((t, D), F32), jax.ShapeDtypeStruct((SMALL_PACK_ROWS, D), F32),
                   jax.ShapeDtypeStruct((2,) + pb.shape[1:], BF16)],
        scratch_shapes=[pltpu.VMEM((1, D), F32),
                        pltpu.VMEM((SMALL_PACK_ROWS, D), F32), pltpu.VMEM((N_DEV, SMALL_PACK_ROWS, D), F32),
                        pltpu.VMEM((8, D), F32), pltpu.VMEM((N_DEV, 8, D), F32),
                        pltpu.SemaphoreType.DMA((7,)), pltpu.SemaphoreType.DMA((7,)),
                        pltpu.SemaphoreType.DMA((7,)), pltpu.SemaphoreType.DMA((7,)),
                        pltpu.VMEM((RING, tm, PW), dproj2d.dtype), pltpu.VMEM((RING, tm, D), F32),
                        pltpu.VMEM((RING, tm, D), F32), pltpu.SemaphoreType.DMA((3, RING))] + _stage2_scratch([pb]),
        compiler_params=_params(("arbitrary",)),
    )(dproj2d, x2d, dx1, g1, w_in_t, pb, *small_parts)


def _get_rows(ref):
    return ref[:, 0, :] if len(ref.shape) == 3 else ref[...]


def _put_rows(ref, val):
    if len(ref.shape) == 3:
        ref[:, 0, :] = val
    else:
        ref[...] = val


def _adamw_math(w, g, m, v):
    m = ADAM_B1 * m + (1.0 - ADAM_B1) * g
    v = ADAM_B2 * v + (1.0 - ADAM_B2) * (g * g)
    m_hat = m / (1.0 - ADAM_B1 ** ADAM_STEP)
    v_hat = v / (1.0 - ADAM_B2 ** ADAM_STEP)
    delta = -ADAM_LR * (m_hat / (jnp.sqrt(v_hat) + ADAM_EPS) + ADAM_WD * w)
    return delta, m, v


def _position():
    return lax.axis_index("x"), lax.axis_index("y"), lax.axis_index("c")


GATHER_PARTS = 2
GATHER_SEMS = 7 * GATHER_PARTS
RELAY_AT = 3
FORWARD_AT = 7


def _gather_copies(stage, lo, rows, gx, send_sems, recv_sems, local_sem):
    x, y, c = _position()
    me = (x, y, c)
    sibling = (x, y, 1 - c)
    chips = [(1 - x, y), (x, 1 - y), (1 - x, 1 - y)]
    part = -(-rows // (16 * GATHER_PARTS)) * 16
    bounds = [(p * part, min(part, rows - p * part)) for p in range(GATHER_PARTS)]

    def blk(px, py, pc, off, n):
        return gx.at[4 * px + 2 * py + pc, pl.ds(off, n), :]

    mine = pltpu.make_async_copy(stage.at[pl.ds(lo, rows), :], gx.at[4 * x + 2 * y + c], local_sem)
    parts = []
    for p, (off, n) in enumerate(bounds):
        def copy(k, block, to, from_stage=False, p=p, off=off, n=n):
            return pltpu.make_async_remote_copy(
                src_ref=stage.at[pl.ds(lo + off, n), :] if from_stage else blk(*block, off, n),
                dst_ref=blk(*block, off, n), send_sem=send_sems.at[7 * p + k], recv_sem=recv_sems.at[7 * p + k],
                device_id=to, device_id_type=MESH)

        first = [copy(0, me, sibling, True)] + [copy(1 + j, me, (*chips[j], c), True) for j in range(2)]
        relay = copy(3, (*chips[p], c), (*chips[1 - p], c))
        passed = [copy(4 + j, (*chip, c), sibling) for j, chip in enumerate(chips)]
        arrivals = ([copy(0, sibling, me)] + [copy(1 + j, (*chip, c), me) for j, chip in enumerate(chips)]
                    + [copy(4 + j, (*chip, 1 - c), me) for j, chip in enumerate(chips)])
        parts.append((first, relay, passed, arrivals))
    return mine, parts


def _gather_start(*args):
    mine, parts = _gather_copies(*args)
    mine.start()
    for first, _, _, _ in parts:
        first[0].start()
    for p, q in ((0, 0), (1, 1), (0, 1), (1, 0)):
        parts[p][0][1 + q].start()


def _gather_relay(*args):
    _, parts = _gather_copies(*args)
    for p, (_, relay, passed, arrivals) in enumerate(parts):
        arrivals[1 + p].wait_recv()
        relay.start()
        passed[p].start()


def _gather_forward(*args):
    _, parts = _gather_copies(*args)
    for p, j in ((0, 1), (1, 0), (0, 2), (1, 2)):
        _, _, passed, arrivals = parts[p]
        arrivals[1 + j].wait_recv()
        passed[j].start()


def _gather_finish(*args):
    mine, parts = _gather_copies(*args)
    for first, relay, passed, arrivals in parts:
        arrivals[0].wait_recv()
        for j in range(3):
            arrivals[4 + j].wait_recv()
        for cp in first + [relay] + passed:
            cp.wait_send()
    mine.wait()


def _gather_sems():
    return [pltpu.SemaphoreType.DMA((GATHER_SEMS,)), pltpu.SemaphoreType.DMA((GATHER_SEMS,)), pltpu.SemaphoreType.DMA]


def _gather_w_in(w_it, w_gt, w_ut, w_d, w_o, wgu_s, conv_s):
    def body(wi_hbm, wg_hbm, wu_hbm, wd_hbm, wo_hbm, wgu_ref, conv_ref, w_ref, gwgu_ref, gconv_ref, stage,
             buf, wif, wf, wof, send_sems, recv_sems, local_sem, ssend, srecv, load_sems):
        x, y, c = _position()
        me = 4 * x + 2 * y + c
        loads = [pltpu.make_async_copy(src, dst, load_sems.at[n]) for n, (src, dst) in enumerate(
            ((wi_hbm.at[:, 0, :], wif), (wg_hbm, wf.at[0]), (wu_hbm, wf.at[1]), (wd_hbm, wf.at[2]), (wo_hbm, wof)))]
        for cp in loads:
            cp.start()
        loads.pop(0).wait()
        stage[SLAB_IN:SLAB_IN + IN_W, :] = wif[...].astype(BF16)
        stage[SLAB_IN + IN_W:SLAB_G, :] = jnp.zeros((IN_ROWS - IN_W, D), BF16)
        args = (stage, SLAB_IN, IN_ROWS, buf, send_sems, recv_sems, local_sem)
        _gather_start(*args)
        for n, lo in enumerate((SLAB_G, SLAB_U, SLAB_D)):
            loads[n].wait()
            stage[lo:lo + FF_W, :] = wf[n].astype(BF16)
        loads[3].wait()
        stage[SLAB_O:SLAB_ROWS, :] = wof[...].astype(BF16)
        flips = [(k >> 2, (k >> 1) & 1, k & 1) for k in range(1, N_DEV)]
        peers = [(x ^ fx, y ^ fy, c ^ fc) for fx, fy, fc in flips]

        def small(k, block_id, to):
            return [pltpu.make_async_remote_copy(
                src_ref=s, dst_ref=g.at[block_id], send_sem=ssend.at[2 * k + n], recv_sem=srecv.at[2 * k + n],
                device_id=to, device_id_type=MESH)
                for n, (s, g) in enumerate(((wgu_ref, gwgu_ref), (conv_ref, gconv_ref)))]

        gwgu_ref[me] = wgu_ref[...]
        gconv_ref[me] = conv_ref[...]
        for k, peer in enumerate(peers):
            for cp in small(k, me, peer):
                cp.start()
        w_ref[IN_COLS:PW, :] = jnp.zeros((PW - IN_COLS, D), BF16)
        _gather_relay(*args)
        _gather_forward(*args)
        _gather_finish(*args)
        for k, (px, py, pc) in enumerate(peers):
            for cp in small(k, 4 * px + 2 * py + pc, (px, py, pc)):
                cp.wait_recv()
                cp.wait_send()
        for j, lo, hi, d in _in_segments():
            w_ref[d:d + hi - lo, :] = buf[j, lo:hi, :]

    vm = pl.BlockSpec(memory_space=pltpu.VMEM)
    return pl.pallas_call(
        body,
        name="gather_w_in",
        in_specs=[pl.BlockSpec(memory_space=pl.ANY)] * 5 + [vm] * 2,
        out_specs=[vm] * 4,
        out_shape=[jax.ShapeDtypeStruct((PW, D), BF16),
                   jax.ShapeDtypeStruct((N_DEV,) + wgu_s.shape, F32),
                   jax.ShapeDtypeStruct((N_DEV,) + conv_s.shape, F32),
                   jax.ShapeDtypeStruct((SLAB_ROWS, D), BF16)],
        scratch_shapes=[pltpu.VMEM((N_DEV, IN_ROWS, D), BF16), pltpu.VMEM((IN_W, D), F32),
                        pltpu.VMEM((3, FF_W, D), F32), pltpu.VMEM((OUT_ROWS, D), F32)] + _gather_sems()
        + [pltpu.SemaphoreType.DMA((14,)), pltpu.SemaphoreType.DMA((14,)), pltpu.SemaphoreType.DMA((5,))],
        compiler_params=_params(),
    )(w_it, w_gt, w_ut, w_d, w_o, wgu_s, conv_s)


def _w_in_core_reduce(dw_t):
    def body(d_ref, own_ref, sib_ref, pb_ref, g, gb, r1, send_sems, recv_sems):
        x, y, c = _position()
        chip = 2 * x + y
        for j in range(N_DEV):
            g[j, IN_W:IN_ROWS, :] = jnp.zeros((IN_ROWS - IN_W, D), F32)
        for j, lo, hi, d in _in_segments():
            g[j, lo:hi, :] = d_ref[d:d + hi - lo, :]
        for j in range(N_DEV):
            gb[j] = g[j].astype(BF16)
        copies = _stage1_copies(gb, r1, send_sems, recv_sems)
        for cp in copies:
            cp.start()
        own_ref[0] = g[2 * chip + c]
        for cp in copies:
            cp.wait_recv()
        sib_ref[0] = r1[chip]
        for k in range(1, 4):
            t = chip ^ k
            pb_ref[k - 1] = (g[2 * t + c] + r1[t].astype(F32)).astype(BF16)
        for cp in copies:
            cp.wait_send()

    vm = pl.BlockSpec(memory_space=pltpu.VMEM)
    return pl.pallas_call(
        body,
        name="w_in_core_reduce",
        in_specs=[vm],
        out_specs=[vm, vm, vm],
        out_shape=[jax.ShapeDtypeStruct((1, IN_ROWS, D), F32), jax.ShapeDtypeStruct((1, IN_ROWS, D), BF16),
                   jax.ShapeDtypeStruct((3, IN_ROWS, D), BF16)],
        scratch_shapes=[pltpu.VMEM((N_DEV, IN_ROWS, D), F32), pltpu.VMEM((N_DEV, IN_ROWS, D), BF16),
                        pltpu.VMEM((4, IN_ROWS, D), BF16), pltpu.SemaphoreType.DMA((4,)),
                        pltpu.SemaphoreType.DMA((4,))],
        compiler_params=_params(),
    )(dw_t)


def _stage1_copies(g_ref, r_ref, send_sems, recv_sems):
    x, y, c = _position()
    return [pltpu.make_async_remote_copy(
        src_ref=g_ref.at[2 * i + 1 - c], dst_ref=r_ref.at[i], send_sem=send_sems.at[i], recv_sem=recv_sems.at[i],
        device_id=(x, y, 1 - c), device_id_type=MESH) for i in range(4)]


def _ffn_core_reduce(dw3, dwb3, dw_o, dwb_o, pos_arr, dx1b, gwb):
    def body(pos_ref, g0, g1, g2, go, gb3_hbm, gbo_hbm, dx1b_ref, gwb_hbm, p0, p1, p2, po, s0, s1, s2, so, dmix_ref,
             r1f, r1o, wo, send_sems, recv_sems, wsem):
        step = pl.program_id(0)
        k = jnp.minimum(step, 2)
        x, y, c = _position()
        chip = 2 * x + y

        def copies(p):
            src = 2 * (chip ^ ((p + 1) & 3)) + 1 - c
            pairs = [(gb3_hbm.at[a, src], r1f.at[a, p]) for a in range(3)] + [(gbo_hbm.at[src], r1o.at[p])]
            return [pltpu.make_async_remote_copy(
                src_ref=s, dst_ref=d, send_sem=send_sems.at[4 * p + a], recv_sem=recv_sems.at[4 * p + a],
                device_id=(x, y, 1 - c), device_id_type=MESH) for a, (s, d) in enumerate(pairs)]

        @pl.when(step == 0)
        def _():
            for p in range(4):
                for cp in copies(p):
                    cp.start()
            loads = [pltpu.make_async_copy(gwb_hbm.at[j, pl.ds(D_TAIL, OUT_ROWS), :],
                                           wo.at[pl.ds(OUT_ROWS * j, OUT_ROWS), :], wsem.at[j]) for j in range(N_DEV)]
            for cp in loads:
                cp.start()
            for cp in loads:
                cp.wait()

        dmix_ref[...] = _nt(dx1b_ref[...], wo[...])

        for p in range(3):
            @pl.when(step == p)
            def _():
                for cp in copies(p):
                    cp.wait_recv()

        for a, (g, pb) in enumerate(((g0, p0), (g1, p1), (g2, p2))):
            pb[...] = (g[...] + r1f[a, k][None].astype(F32)).astype(BF16)
        po[...] = (go[...] + r1o[k][None].astype(F32)).astype(BF16)

        @pl.when(step == 3)
        def _():
            for cp in copies(3):
                cp.wait_recv()
            for a, s in enumerate((s0, s1, s2)):
                s[0] = r1f[a, 3]
            so[0] = r1o[3]
            for p in range(4):
                for cp in copies(p):
                    cp.wait_send()

    t = dx1b.shape[0]
    other = lambda s, pos: 2 * (pos[1] ^ (jnp.minimum(s, 2) + 1)) + pos[0]
    g_spec = lambda lead: pl.BlockSpec((None, 1, FF_W, D), lambda s, pos: (lead, other(s, pos), 0, 0))
    slot = lambda rows: pl.BlockSpec((1, rows, D), lambda s, pos: (jnp.minimum(s, 2), 0, 0))
    one = lambda rows: pl.BlockSpec((1, rows, D), lambda s, pos: (0, 0, 0))
    quarter = pl.BlockSpec((t // 4, D), lambda s, pos: (s, 0))
    hbm = pl.BlockSpec(memory_space=pl.ANY)
    return pl.pallas_call(
        body,
        name="ffn_core_reduce",
        grid_spec=pltpu.PrefetchScalarGridSpec(
            num_scalar_prefetch=1, grid=(4,),
            in_specs=[g_spec(0), g_spec(1), g_spec(2),
                      pl.BlockSpec((1, OUT_ROWS, D), lambda s, pos: (other(s, pos), 0, 0)), hbm, hbm, quarter, hbm],
            out_specs=[slot(FF_W), slot(FF_W), slot(FF_W), slot(OUT_ROWS),
                       one(FF_W), one(FF_W), one(FF_W), one(OUT_ROWS), quarter],
            scratch_shapes=[pltpu.VMEM((3, 4, FF_W, D), BF16), pltpu.VMEM((4, OUT_ROWS, D), BF16),
                            pltpu.VMEM((D, D), BF16), pltpu.SemaphoreType.DMA((16,)),
                            pltpu.SemaphoreType.DMA((16,)), pltpu.SemaphoreType.DMA((N_DEV,))]),
        out_shape=[jax.ShapeDtypeStruct((3, FF_W, D), BF16)] * 3 + [jax.ShapeDtypeStruct((3, OUT_ROWS, D), BF16)]
        + [jax.ShapeDtypeStruct((1, FF_W, D), BF16)] * 3 + [jax.ShapeDtypeStruct((1, OUT_ROWS, D), BF16),
                                                             jax.ShapeDtypeStruct((t, D), F32)],
        compiler_params=_params(("arbitrary",)),
    )(pos_arr, dw3, dw3, dw3, dw_o, dwb3, dwb_o, dx1b, gwb)


def _stage2_scratch(pbs):
    n = len(pbs)
    return ([pltpu.VMEM(p.shape[1:], BF16) for p in pbs] * 2
            + [pltpu.SemaphoreType.DMA((6 * n,)), pltpu.SemaphoreType.DMA((6 * n,)), pltpu.SemaphoreType.DMA((2 * n,))])


def _stage2_copies(p_refs, r_refs, scratch):
    n = len(p_refs)
    owns, gots = scratch[:n], scratch[n:2 * n]
    send_sems, recv_sems, load_sems = scratch[2 * n:]
    x, y, c = _position()
    xn, yn = (1 - x, y, c), (x, 1 - y, c)
    loads, first, second = [], [], []
    for a, (p, r, own, got) in enumerate(zip(p_refs, r_refs, owns, gots)):
        rows = p.shape[1]
        half = -(-rows // 32) * 16
        h0, h1 = pl.ds(0, half), pl.ds(half, rows - half)

        def remote(k, src, dst, to, a=a):
            return pltpu.make_async_remote_copy(
                src_ref=src, dst_ref=dst, send_sem=send_sems.at[6 * a + k], recv_sem=recv_sems.at[6 * a + k],
                device_id=to, device_id_type=MESH)

        loads += [pltpu.make_async_copy(p.at[0, h0, :], own.at[h0, :], load_sems.at[2 * a]),
                  pltpu.make_async_copy(p.at[1, h1, :], own.at[h1, :], load_sems.at[2 * a + 1])]
        first += [remote(0, p.at[2, h0, :], got.at[h0, :], xn), remote(1, p.at[2, h1, :], got.at[h1, :], yn),
                  remote(2, p.at[1, h0, :], r.at[1, h0, :], xn), remote(3, p.at[0, h1, :], r.at[0, h1, :], yn)]
        second += [remote(4, own.at[h0, :], r.at[0, h0, :], yn), remote(5, own.at[h1, :], r.at[1, h1, :], xn)]
    return loads, first, second


def _stage2_start(p_refs, r_refs, scratch):
    loads, first, _ = _stage2_copies(p_refs, r_refs, scratch)
    for cp in loads:
        cp.start()
    for k in range(4):
        for cp in first[k::4]:
            cp.start()


def _stage2_combine(p_refs, r_refs, scratch):
    n = len(p_refs)
    loads, first, second = _stage2_copies(p_refs, r_refs, scratch)
    for a in range(n):
        for cp in loads[2 * a:2 * a + 2]:
            cp.wait()
        for cp in first[4 * a:4 * a + 2]:
            cp.wait_recv()
        own, got = scratch[a], scratch[n + a]
        own[...] = (own[...].astype(F32) + got[...].astype(F32)).astype(BF16)
        for cp in second[2 * a:2 * a + 2]:
            cp.start()


def _stage2_finish(p_refs, r_refs, scratch):
    _, first, second = _stage2_copies(p_refs, r_refs, scratch)
    for a in range(len(p_refs)):
        for cp in first[4 * a + 2:4 * a + 4] + second[2 * a:2 * a + 2]:
            cp.wait_recv()
    for cp in first + second:
        cp.wait_send()


def _finish_weights(items, pos_arr, name, nblk):
    n = len(items)
    in_specs, out_specs, out_shape, operands, wbs = [], [], [], [], []
    for g8, lead, r1, r2, w, m, v in items:
        rows, wr = g8.shape[-2], w.shape[0]
        assert rows % nblk == 0 and wr % nblk == 0 and (nblk == 1 or (rows == wr and rows % (16 * nblk) == 0))
        rb, wb = rows // nblk, wr // nblk
        if lead is not None:
            g_spec = pl.BlockSpec((None, 1, rb, D), lambda i, pos, lead=lead: (lead, 2 * pos[1] + pos[0], i, 0))
        elif g8.shape[0] == 1:
            g_spec = pl.BlockSpec((1, rb, D), lambda i, pos: (0, i, 0))
        else:
            g_spec = pl.BlockSpec((1, rb, D), lambda i, pos: (2 * pos[1] + pos[0], i, 0))
        r1_spec = pl.BlockSpec((1, rb, D), lambda i, pos: (0, i, 0))
        if w.ndim == 3:
            wblk = pl.BlockSpec((wb, 1, D), lambda i, pos: (i, 0, 0))
        else:
            wblk = pl.BlockSpec((wb, D), lambda i, pos: (i, 0))
        in_specs += [g_spec, r1_spec, pl.BlockSpec((2, rb, D), lambda i, pos: (0, i, 0)), wblk, wblk, wblk]
        out_specs += [wblk] * 4
        out_shape += [jax.ShapeDtypeStruct(w.shape, F32)] * 4
        operands += [g8, r1, r2, w, m, v]
        wbs.append(wb)

    def body(pos_ref, *refs):
        for a in range(n):
            g_ref, r1_ref, r2_ref, w_ref, m_ref, v_ref = refs[6 * a:6 * a + 6]
            g_out, d_out, m_out, v_out = refs[6 * n + 4 * a:6 * n + 4 * a + 4]
            g = g_ref[0] + r1_ref[0].astype(F32)
            for k in range(2):
                g = g + r2_ref[k].astype(F32)
            g = g[0:wbs[a], :]
            d, mn, vn = _adamw_math(_get_rows(w_ref), g, _get_rows(m_ref), _get_rows(v_ref))
            for out, val in ((g_out, g), (d_out, d), (m_out, mn), (v_out, vn)):
                _put_rows(out, val)

    return pl.pallas_call(
        body,
        name=name,
        grid_spec=pltpu.PrefetchScalarGridSpec(
            num_scalar_prefetch=1, grid=(nblk,), in_specs=in_specs, out_specs=out_specs),
        out_shape=out_shape,
        compiler_params=_params(("arbitrary",)),
    )(pos_arr, *operands)


SMALL_NAMES = ("norm1_g", "norm2_g", "norm_f_g", "b_gate", "gla_norm_g", "w_gate_up", "conv_w")
WGU_W = NQK // N_DEV
CONV_W = CW // N_DEV


def _small_adamw(sums, ws, ms, vs):
    n = len(SMALL_NAMES)

    def body(*refs):
        acc_ref = refs[0]
        w_refs, m_refs, v_refs = refs[1:1 + n], refs[1 + n:1 + 2 * n], refs[1 + 2 * n:1 + 3 * n]
        loss_ref = refs[1 + 3 * n]
        outs = refs[2 + 3 * n:]
        x, y, c = _position()
        me = 4 * x + 2 * y + c
        acc = acc_ref[...]
        loss_ref[...] = acc[3:4, NQK + DV:NQK + DV + 1]

        def my_columns(full, width):
            r = lax.broadcasted_iota(jnp.int32, (full.shape[1], width), 0)
            col = lax.broadcasted_iota(jnp.int32, (full.shape[1], width), 1)
            sel = (r == width * me + col).astype(F32)
            return _mm(full, sel, precision=HIGHEST)

        dwgu = jnp.concatenate([acc[row:row + 1, lane:lane + NQK] for row, lane in map(_wgu_slot, range(RANK))], axis=0)
        dcw = jnp.concatenate([acc[row:row + 1, lane:lane + CW] for row, lane in CONV_SLOTS], axis=0)
        grads = [acc[0:1, :], acc[1:2, :], acc[2:3, :], acc[3:4, 0:NQK], acc[3:4, NQK:NQK + DV],
                 my_columns(dwgu, WGU_W), my_columns(dcw, CONV_W)]
        for i, g in enumerate(grads):
            d, mn, vn = _adamw_math(_get_rows(w_refs[i]), g, _get_rows(m_refs[i]), _get_rows(v_refs[i]))
            for out, val in zip(outs[4 * i:4 * i + 4], (g, d, mn, vn)):
                _put_rows(out, val)

    vm = pl.BlockSpec(memory_space=pltpu.VMEM)
    out_shape = [jax.ShapeDtypeStruct((1, 1), F32)]
    for w in ws:
        out_shape += [jax.ShapeDtypeStruct(w.shape, F32)] * 4
    return pl.pallas_call(
        body,
        name="small_adamw",
        in_specs=[vm] * (1 + 3 * n),
        out_specs=[vm] * (1 + 4 * n),
        out_shape=out_shape,
        compiler_params=_params(),
    )(sums, *ws, *ms, *vs)


def kernel(x, norm1_g, w_in, w_gate_up, b_gate, gla_norm_g, conv_w, w_out, norm2_g, w_ffn_gate, w_ffn_up, w_ffn_down, norm_f_g, loss_target, m_norm1_g, m_w_in, m_w_gate_up, m_b_gate, m_gla_norm_g, m_conv_w, m_w_out, m_norm2_g, m_w_ffn_gate, m_w_ffn_up, m_w_ffn_down, m_norm_f_g, v_norm1_g, v_w_in, v_w_gate_up, v_b_gate, v_gla_norm_g, v_conv_w, v_w_out, v_norm2_g, v_w_ffn_gate, v_w_ffn_up, v_w_ffn_down, v_norm_f_g):
    xi, yi, ci = _position()
    pos_arr = jnp.stack([ci, 2 * xi + yi]).astype(jnp.int32)
    nb, s, _ = x.shape
    t = nb * s

    tr = lambda a: a[0].T
    rows_of = lambda a: a.transpose(2, 0, 1)
    conv_rows = lambda a: a.transpose(1, 0, 2)
    w_in_t, gwgu, gconv, stage = _gather_w_in(rows_of(w_in), tr(w_ffn_gate), tr(w_ffn_up), w_ffn_down[0], w_out[0],
                                              w_gate_up[0], conv_rows(conv_w))
    wgu_f = gwgu.transpose(1, 0, 2).reshape(RANK, NQK)
    conv_f = gconv.transpose(1, 2, 0, 3).reshape(CONV_K, CW)
    wgu_p = jnp.concatenate([wgu_f, jnp.zeros((A_PAD - RANK, NQK), F32)], axis=0).astype(BF16)

    x2d = x.reshape(t, D)
    tgt2d = loss_target.reshape(t, D)
    tm = 256
    tm_in = min(512, t)
    tk = min(2048, t)
    proj, z, h, gwb = _in_proj_fwd(x2d, norm1_g, w_in_t, wgu_p, b_gate, tm_in, stage)
    proj3 = proj.reshape(nb, s, PW)
    z3 = z.reshape(nb, s, NQK)
    mix3, opre3, sprev, x1, gwa = _mix_fwd(proj3, z3, gla_norm_g, conv_f, stage, x, gwb)
    mix2d = mix3.reshape(t, D)
    dx1, dx1b, adu, hb, dg2, dgf, loss_part = _ffn_fwd_bwd(
        x1.reshape(t, D), tgt2d, gwa, gwb, norm2_g, norm_f_g.reshape(1, D), tm)
    dw3, dwb3 = _dw_ffn(adu, hb, tk)
    dw3 = dw3.reshape(3, N_DEV, FF_W, D)
    dw_o, dwb_o = _tn_matmul(mix2d, dx1b, D // 2, D, tk, "dw_out", True)
    dw_o = dw_o.reshape(N_DEV, OUT_ROWS, D)
    *pb, sib_d, sib_g, sib_u, sib_o, dmix = _ffn_core_reduce(
        dw3, dwb3.reshape(3, N_DEV, FF_W, D), dw_o, dwb_o.reshape(N_DEV, OUT_ROWS, D), pos_arr, dx1b, gwb)
    g8 = [dw3, dw3, dw3, dw_o]
    leads = [0, 1, 2, None]
    tags = ("w_ffn_down", "w_ffn_gate", "w_ffn_up", "w_out")
    r1 = [sib_d, sib_g, sib_u, sib_o]
    mb = _mix_bwd(proj3, z3, sprev, opre3, dmix.reshape(nb, s, D), gla_norm_g, conv_f, wgu_p, [pb[0], pb[1], pb[3]])
    dproj3, dgng, dcw, dbg, dwgu = mb[:5]
    dproj2d = dproj3.reshape(t, PW)
    dw_in_t, r2_up = _tn_matmul(dproj2d, h, PW // 5, D, t, "dw_in", False, _stage2_rider([pb[2]]))
    r2 = [mb[5], mb[6], r2_up, mb[7]]
    g_in, r1_in, pb_in = _w_in_core_reduce(dw_in_t)
    dx, small_sums, r2_in = _in_proj_bwd(dproj2d, x2d, dx1, norm1_g, w_in_t, tm_in, pb_in,
                                         (dg2, dgf, dbg, dgng, dwgu, dcw, loss_part))

    tags = ("w_in",) + tags
    g8 = [g_in] + g8
    leads = [None] + leads
    r1 = [r1_in] + list(r1)
    r2 = [r2_in] + r2
    shard_w = (rows_of(w_in), w_ffn_down[0], tr(w_ffn_gate), tr(w_ffn_up), w_out[0])
    shard_m = (rows_of(m_w_in), m_w_ffn_down[0], tr(m_w_ffn_gate), tr(m_w_ffn_up), m_w_out[0])
    shard_v = (rows_of(v_w_in), v_w_ffn_down[0], tr(v_w_ffn_gate), tr(v_w_ffn_up), v_w_out[0])
    back = (lambda o: o.transpose(1, 2, 0), lambda o: o[None], lambda o: o.T[None], lambda o: o.T[None],
            lambda o: o[None])
    items = list(zip(g8, leads, r1, r2, shard_w, shard_m, shard_v))
    flat = list(_finish_weights(items[1:], pos_arr, "finish_ffn_out", 2))
    flat = list(_finish_weights(items[:1], pos_arr, "finish_w_in", 1)) + flat
    results = {}
    for i, (tag, to_shard) in enumerate(zip(tags, back)):
        results[tag] = [to_shard(o) for o in flat[4 * i:4 * i + 4]]

    small_w = (norm1_g, norm2_g, norm_f_g.reshape(1, D), b_gate, gla_norm_g, w_gate_up[0], conv_rows(conv_w))
    small_m = (m_norm1_g, m_norm2_g, m_norm_f_g.reshape(1, D), m_b_gate, m_gla_norm_g, m_w_gate_up[0],
               conv_rows(m_conv_w))
    small_v = (v_norm1_g, v_norm2_g, v_norm_f_g.reshape(1, D), v_b_gate, v_gla_norm_g, v_w_gate_up[0],
               conv_rows(v_conv_w))
    so = _small_adamw(small_sums, small_w, small_m, small_v)
    loss = so[0].reshape(())
    to_shape = {"norm_f_g": lambda o: o.reshape(D), "w_gate_up": lambda o: o[None],
                "conv_w": lambda o: o.transpose(1, 0, 2)}
    for i, name in enumerate(SMALL_NAMES):
        results[name] = [to_shape.get(name, lambda o: o)(o) for o in so[1 + 4 * i:5 + 4 * i]]

    names = ("norm1_g", "w_in", "w_gate_up", "b_gate", "gla_norm_g", "conv_w", "w_out", "norm2_g",
             "w_ffn_gate", "w_ffn_up", "w_ffn_down", "norm_f_g")
    outs = [loss, dx.reshape(nb, s, D)]
    for kind in range(4):
        for name in names:
            outs.append(results[name][kind])
    return tuple(outs)
```

```python
import jax
import jax.numpy as jnp
from jax import lax
from jax.experimental import pallas as pl
from jax.experimental.pallas import tpu as pltpu

F32 = jnp.float32
BF16 = jnp.bfloat16
HIGHEST = lax.Precision.HIGHEST
MESH = pl.DeviceIdType.MESH

N_DEV = 8
D = 1024
DFF = 2816
HEADS = 4
DK = 64
DV = 128
NQK = HEADS * DK
NV = HEADS * DV
RANK = 16
CHUNK = 64
CW = 512
CONV_K = 3
IN_COLS = 3088
EPS = 1e-6
INV_GATE_NORM = 1.0 / 16.0
Q_SCALE = DK ** -0.5

PW = 3200
OQ, OK_, OV, OG, OCB, OCC, OCH, OA = 0, 256, 512, 1024, 1536, 2048, 2560, 3072
A_PAD = 128

ADAM_LR = 0.001
ADAM_B1 = 0.9
ADAM_B2 = 0.999
ADAM_EPS = 1e-08
ADAM_WD = 0.01
ADAM_STEP = 10

IN_W = IN_COLS // N_DEV
IN_ROWS = 400
FF_W = DFF // N_DEV
OUT_ROWS = D // N_DEV
SLAB_IN = 0
SLAB_G = SLAB_IN + IN_ROWS
SLAB_U = SLAB_G + FF_W
SLAB_D = SLAB_U + FF_W
SLAB_O = SLAB_D + FF_W
SLAB_ROWS = SLAB_O + OUT_ROWS
D_HEAD = 128
D_TAIL = FF_W - D_HEAD
SLAB_SPLIT = SLAB_D + D_HEAD

VMEM_LIMIT = 56 * 1024 * 1024


def _params(sem=None, vmem=VMEM_LIMIT):
    return pltpu.CompilerParams(dimension_semantics=sem, vmem_limit_bytes=vmem)


def _nt(a, b):
    return lax.dot_general(a, b, (((1,), (1,)), ((), ())), preferred_element_type=F32)


def _tn(a, b, precision=None):
    return lax.dot_general(a, b, (((0,), (0,)), ((), ())), preferred_element_type=F32, precision=precision)


def _mm(a, b, precision=None):
    return jnp.dot(a, b, preferred_element_type=F32, precision=precision)


def _in_segments():
    segs = []
    for j in range(N_DEV):
        lo, hi = IN_W * j, IN_W * (j + 1)
        cuts = sorted({lo, hi} | {c for c in (OCB, OCB + RANK) if lo < c < hi})
        for a, b in zip(cuts[:-1], cuts[1:]):
            if a < OCB:
                d = a
            elif a < OCB + RANK:
                d = OA + (a - OCB)
            else:
                d = a - RANK
            segs.append((j, a - lo, b - lo, d))
    return segs


def _in_proj_fwd(x2d, g1, w_in_t, wgu_p, b_gate, tm, stage):
    t = x2d.shape[0]
    nt = t // tm
    g_rows = SLAB_ROWS - SLAB_SPLIT

    def body(x_ref, g_ref, w_ref, wgu_ref, bg_ref, stage_hbm, proj_ref, z_ref, h_ref, gwb_ref,
             send_sems, recv_sems, local_sem):
        gargs = (stage_hbm, SLAB_SPLIT, g_rows, gwb_ref, send_sems, recv_sems, local_sem)

        @pl.when(pl.program_id(0) == 0)
        def _():
            _gather_start(*gargs)

        @pl.when(pl.program_id(0) == RELAY_AT * nt // 8)
        def _():
            _gather_relay(*gargs)

        @pl.when(pl.program_id(0) == DIRECT_AT * nt // 8)
        def _():
            _gather_forward(*gargs, relayed=False)

        @pl.when(pl.program_id(0) == FORWARD_AT * nt // 8)
        def _():
            _gather_forward(*gargs, relayed=True)

        x = x_ref[...]
        r = lax.rsqrt(jnp.mean(x * x, axis=-1, keepdims=True) + EPS)
        h = ((x * r) * g_ref[...]).astype(BF16)
        h_ref[...] = h
        proj = _nt(h, w_ref[...])
        proj_ref[...] = proj
        pa = proj[:, OA:OA + A_PAD].astype(BF16)
        z_ref[...] = _mm(pa, wgu_ref[...]) + bg_ref[...]

        @pl.when(pl.program_id(0) == nt - 1)
        def _():
            _gather_finish(*gargs)

    return pl.pallas_call(
        body,
        name="in_proj_fwd",
        grid=(t // tm,),
        in_specs=[
            pl.BlockSpec((tm, D), lambda i: (i, 0)),
            pl.BlockSpec((1, D), lambda i: (0, 0)),
            pl.BlockSpec((PW, D), lambda i: (0, 0)),
            pl.BlockSpec((A_PAD, NQK), lambda i: (0, 0)),
            pl.BlockSpec((1, NQK), lambda i: (0, 0)),
            pl.BlockSpec(memory_space=pl.ANY),
        ],
        out_specs=[
            pl.BlockSpec((tm, PW), lambda i: (i, 0)),
            pl.BlockSpec((tm, NQK), lambda i: (i, 0)),
            pl.BlockSpec((tm, D), lambda i: (i, 0)),
            pl.BlockSpec(memory_space=pl.ANY),
        ],
        out_shape=[
            jax.ShapeDtypeStruct((t, PW), F32),
            jax.ShapeDtypeStruct((t, NQK), F32),
            jax.ShapeDtypeStruct((t, D), BF16),
            jax.ShapeDtypeStruct((N_DEV, g_rows, D), BF16),
        ],
        scratch_shapes=_gather_sems(),
        compiler_params=_params(("arbitrary",)),
    )(x2d, g1, w_in_t, wgu_p, b_gate, stage)


def _head_masks():
    lane = lax.broadcasted_iota(jnp.int32, (1, NQK), 1)
    return [(lane >= DK * h) & (lane < DK * (h + 1)) for h in range(HEADS)]


def _split_bf16(x, n):
    parts = []
    for _ in range(n):
        p = x.astype(BF16)
        parts.append(p)
        x = x - p.astype(F32)
    return parts


def _chunk_fwd_parts(q, k, z, tril16):
    la = (jnp.minimum(z, 0.0) - jnp.log1p(jnp.exp(-jnp.abs(z)))) * INV_GATE_NORM
    la_parts = _split_bf16(la, 3)
    bc = _mm(tril16, la_parts[0]) + _mm(tril16, la_parts[1]) + _mm(tril16, la_parts[2])
    bl = bc[CHUNK - 1:CHUNK, :]
    eb = jnp.exp(bc)
    enb = jnp.exp(-bc)
    ekl = jnp.exp(bl - bc)
    qi = (q * Q_SCALE) * eb
    ki = k * enb
    ks = k * ekl
    ones16 = jnp.ones((CHUNK, DV), BF16)
    decb = jnp.exp(_tn(la_parts[0], ones16) + _tn(la_parts[1], ones16) + _tn(la_parts[2], ones16))
    return la, eb, enb, ekl, qi, ki, ks, decb


def _stack_heads(a, masks):
    return jnp.concatenate([jnp.where(m, a, 0.0) for m in masks], axis=0)


def _merge_heads(blocks, masks):
    out = blocks[HEADS - 1]
    for h in range(HEADS - 2, -1, -1):
        out = jnp.where(masks[h], blocks[h], out)
    return out


def _causal_stack_mask():
    row = lax.broadcasted_iota(jnp.int32, (HEADS * CHUNK, CHUNK), 0)
    col = lax.broadcasted_iota(jnp.int32, (HEADS * CHUNK, CHUNK), 1)
    return (row & (CHUNK - 1)) >= col


def _conv_taps(u, uprev):
    row = lax.broadcasted_iota(jnp.int32, u.shape, 0)
    u1 = jnp.where(row < 1, pltpu.roll(uprev, 1, 0), pltpu.roll(u, 1, 0))
    u2 = jnp.where(row < 2, pltpu.roll(uprev, 2, 0), pltpu.roll(u, 2, 0))
    return u1, u2


def _mix_fwd(proj3, z3, gng, conv_w, stage, x3, gwb):
    nb, s, _ = proj3.shape
    nc = s // CHUNK
    g_rows = SLAB_SPLIT - SLAB_G

    def body(p_ref, z_ref, gng_ref, cw_ref, stage_hbm, x_ref, gwb_hbm, mix_ref, o_ref, sprev_ref, x1_ref, gwa_ref,
             s_ref, uprev_ref, wo, wsem, send_sems, recv_sems, local_sem):
        n = pl.program_id(0)
        gargs = (stage_hbm, SLAB_G, g_rows, gwa_ref, send_sems, recv_sems, local_sem)

        @pl.when(n == 0)
        def _():
            _gather_start(*gargs)
            loads = [pltpu.make_async_copy(gwb_hbm.at[j, pl.ds(D_TAIL, OUT_ROWS), :],
                                           wo.at[pl.ds(OUT_ROWS * j, OUT_ROWS), :], wsem.at[j]) for j in range(N_DEV)]
            for cp in loads:
                cp.start()
            s_ref[...] = jnp.zeros_like(s_ref)
            uprev_ref[...] = jnp.zeros_like(uprev_ref)
            for cp in loads:
                cp.wait()

        @pl.when(n == RELAY_AT * nc // 8)
        def _():
            _gather_relay(*gargs)

        @pl.when(n == DIRECT_AT * nc // 8)
        def _():
            _gather_forward(*gargs, relayed=False)

        @pl.when(n == FORWARD_AT * nc // 8)
        def _():
            _gather_forward(*gargs, relayed=True)

        r_i = lax.broadcasted_iota(jnp.int32, (CHUNK, CHUNK), 0)
        c_i = lax.broadcasted_iota(jnp.int32, (CHUNK, CHUNK), 1)
        tril16 = (r_i >= c_i).astype(BF16)
        masks = _head_masks()
        cmask = _causal_stack_mask()
        gg = gng_ref[...]
        for b in range(nb):
            q = p_ref[b, :, OQ:OQ + NQK]
            k = p_ref[b, :, OK_:OK_ + NQK]
            _, _, _, _, qi, ki, ks, decb = _chunk_fwd_parts(q, k, z_ref[b], tril16)
            qs = _stack_heads(qi, masks).astype(BF16)
            sc = jnp.where(cmask, _nt(qs, ki.astype(BF16)), 0.0).astype(BF16)
            st = s_ref[b]
            sprev_ref[b, 0] = st
            o_inter = _mm(qs, st.astype(BF16))
            v16 = p_ref[b, :, OV:OV + NV].astype(BF16)
            kv = _tn(ks.astype(BF16), v16)
            for h in range(HEADS):
                rows = slice(CHUNK * h, CHUNK * (h + 1))
                cols = slice(DV * h, DV * (h + 1))
                o = _mm(sc[rows], v16[:, cols]) + o_inter[rows]
                o_ref[b, :, cols] = o
                r = lax.rsqrt(jnp.mean(o * o, axis=-1, keepdims=True) + EPS)
                on = (o * r) * gg
                g = p_ref[b, :, OG + DV * h:OG + DV * (h + 1)]
                mix_ref[b, :, cols] = (on * (g * jax.nn.sigmoid(g))).astype(BF16)
                s_ref[b, rows, :] = decb[rows] * st[rows] + kv[rows, cols]
            u = p_ref[b, :, OCC:OCC + CW] * p_ref[b, :, OCH:OCH + CW]
            u1, u2 = _conv_taps(u, uprev_ref[b])
            yc = cw_ref[0:1, :] * u2 + cw_ref[1:2, :] * u1 + cw_ref[2:3, :] * u
            mix_ref[b, :, NV:NV + CW] = (p_ref[b, :, OCB:OCB + CW] * yc).astype(BF16)
            uprev_ref[b] = u
        mixed = _mm(jnp.concatenate([mix_ref[b] for b in range(nb)], axis=0), wo[...])
        for b in range(nb):
            x1_ref[b] = x_ref[b] + mixed[CHUNK * b:CHUNK * (b + 1)]

        @pl.when(n == nc - 1)
        def _():
            _gather_finish(*gargs)

    return pl.pallas_call(
        body,
        name="mix_fwd",
        grid=(nc,),
        in_specs=[
            pl.BlockSpec((nb, CHUNK, PW), lambda n: (0, n, 0)),
            pl.BlockSpec((nb, CHUNK, NQK), lambda n: (0, n, 0)),
            pl.BlockSpec((1, DV), lambda n: (0, 0)),
            pl.BlockSpec((CONV_K, CW), lambda n: (0, 0)),
            pl.BlockSpec(memory_space=pl.ANY),
            pl.BlockSpec((nb, CHUNK, D), lambda n: (0, n, 0)),
            pl.BlockSpec(memory_space=pl.ANY),
        ],
        out_specs=[
            pl.BlockSpec((nb, CHUNK, D), lambda n: (0, n, 0)),
            pl.BlockSpec((nb, CHUNK, NV), lambda n: (0, n, 0)),
            pl.BlockSpec((nb, 1, NQK, DV), lambda n: (0, n, 0, 0)),
            pl.BlockSpec((nb, CHUNK, D), lambda n: (0, n, 0)),
            pl.BlockSpec(memory_space=pl.ANY),
        ],
        out_shape=[
            jax.ShapeDtypeStruct((nb, s, D), BF16),
            jax.ShapeDtypeStruct((nb, s, NV), F32),
            jax.ShapeDtypeStruct((nb, nc, NQK, DV), F32),
            jax.ShapeDtypeStruct((nb, s, D), F32),
            jax.ShapeDtypeStruct((N_DEV, g_rows, D), BF16),
        ],
        scratch_shapes=[pltpu.VMEM((nb, NQK, DV), F32), pltpu.VMEM((nb, CHUNK, CW), F32),
                        pltpu.VMEM((D, D), BF16), pltpu.SemaphoreType.DMA((N_DEV,))] + _gather_sems(),
        compiler_params=_params(("arbitrary",)),
    )(proj3, z3, gng, conv_w, stage, x3, gwb)


def _ffn_fwd_bwd(x1_2d, tgt2d, gwa, gwb, g2, gf, tm):
    t = x1_2d.shape[0]

    def body(x1_ref, tgt_ref, g2_ref, gf_ref, gwa_hbm, gwb_hbm,
             dx1_ref, dx1b_ref, adu_ref, hb_ref, dg2_ref, dgf_ref, loss_ref,
             wg, wu, wd, wsem):
        i = pl.program_id(0)

        def weight_copies(n, dst, src, off, rows, at=0):
            return [pltpu.make_async_copy(src.at[j, pl.ds(off, rows), :], dst.at[pl.ds(FF_W * j + at, rows), :],
                                          wsem.at[N_DEV * n + j]) for j in range(N_DEV)]

        loads = (weight_copies(0, wg, gwa_hbm, 0, FF_W), weight_copies(1, wu, gwa_hbm, FF_W, FF_W),
                 weight_copies(2, wd, gwa_hbm, 2 * FF_W, D_HEAD), weight_copies(3, wd, gwb_hbm, 0, D_TAIL, D_HEAD))

        @pl.when(i == 0)
        def _():
            for group in loads:
                for cp in group:
                    cp.start()
            dg2_ref[...] = jnp.zeros_like(dg2_ref)
            dgf_ref[...] = jnp.zeros_like(dgf_ref)
            loss_ref[...] = jnp.zeros_like(loss_ref)
            for group in loads:
                for cp in group:
                    cp.wait()

        g2v = g2_ref[...]
        gfv = gf_ref[...]
        x1 = x1_ref[...]
        r2 = lax.rsqrt(jnp.mean(x1 * x1, axis=-1, keepdims=True) + EPS)
        n2 = x1 * r2
        h2 = (n2 * g2v).astype(BF16)
        hb_ref[1] = h2
        gate = _nt(h2, wg[...])
        up = _nt(h2, wu[...])
        sg = jax.nn.sigmoid(gate)
        sil = gate * sg
        act = (sil * up).astype(BF16)
        adu_ref[0] = act
        x2 = x1 + _mm(act, wd[...])
        rf = lax.rsqrt(jnp.mean(x2 * x2, axis=-1, keepdims=True) + EPS)
        nf = x2 * rf
        err = nf * gfv - tgt_ref[...]
        loss_ref[...] += 0.5 * jnp.sum(jnp.mean(err * err, axis=-1, keepdims=True))
        dy = err * (1.0 / D)
        dgf_ref[...] += jnp.sum(dy * nf, axis=0, keepdims=True)
        dnf = dy * gfv
        dx2 = rf * (dnf - nf * jnp.mean(dnf * nf, axis=-1, keepdims=True))
        dx2b = dx2.astype(BF16)
        hb_ref[0] = dx2b
        dact = _nt(dx2b, wd[...])
        dup = (dact * sil).astype(BF16)
        dgate = ((dact * up) * (sg * (1.0 + gate * (1.0 - sg)))).astype(BF16)
        adu_ref[2] = dup
        adu_ref[1] = dgate
        dh2 = _mm(dgate, wg[...]) + _mm(dup, wu[...])
        dg2_ref[...] += jnp.sum(dh2 * n2, axis=0, keepdims=True)
        dn2 = dh2 * g2v
        dx1 = dx2 + r2 * (dn2 - n2 * jnp.mean(dn2 * n2, axis=-1, keepdims=True))
        dx1_ref[...] = dx1
        dx1b_ref[...] = dx1.astype(BF16)

    tile = lambda w: pl.BlockSpec((tm, w), lambda i: (i, 0))
    vec = pl.BlockSpec((1, D), lambda i: (0, 0))
    hbm = pl.BlockSpec(memory_space=pl.ANY)
    return pl.pallas_call(
        body,
        name="ffn_fwd_bwd",
        grid=(t // tm,),
        in_specs=[tile(D), tile(D), vec, vec, hbm, hbm],
        out_specs=[tile(D), tile(D), pl.BlockSpec((3, tm, DFF), lambda i: (0, i, 0)),
                   pl.BlockSpec((2, tm, D), lambda i: (0, i, 0)), vec, vec,
                   pl.BlockSpec((1, 128), lambda i: (0, 0))],
        out_shape=[
            jax.ShapeDtypeStruct((t, D), F32),
            jax.ShapeDtypeStruct((t, D), BF16),
            jax.ShapeDtypeStruct((3, t, DFF), BF16),
            jax.ShapeDtypeStruct((2, t, D), BF16),
            jax.ShapeDtypeStruct((1, D), F32),
            jax.ShapeDtypeStruct((1, D), F32),
            jax.ShapeDtypeStruct((1, 128), F32),
        ],
        scratch_shapes=[pltpu.VMEM((DFF, D), BF16), pltpu.VMEM((DFF, D), BF16), pltpu.VMEM((DFF, D), BF16),
                        pltpu.SemaphoreType.DMA((4 * N_DEV,))],
        compiler_params=_params(("arbitrary",)),
    )(x1_2d, tgt2d, g2, gf, gwa, gwb)


def _stage2_rider(pbs):
    return dict(inputs=list(pbs), out_shape=[jax.ShapeDtypeStruct((2,) + p.shape[1:], BF16) for p in pbs],
                scratch=_stage2_scratch(pbs))


def _tn_matmul(a, b, bm, bn, tk, name, with_bf16, rider=None):
    t, m = a.shape
    n = b.shape[1]
    nk = t // tk
    nout = 2 if with_bf16 else 1
    grid = (m // bm, n // bn, nk)
    steps = grid[0] * grid[1] * nk
    r_in = [] if rider is None else rider["inputs"]
    r_out = [] if rider is None else rider["out_shape"]

    def body(a_ref, b_ref, *rest):
        ins, outs = rest[:len(r_in)], rest[len(r_in):len(r_in) + nout]
        r_outs, scratch = rest[len(r_in) + nout:len(r_in) + nout + len(r_out)], rest[len(r_in) + nout + len(r_out):]
        o_ref = outs[0]
        i, j, k = pl.program_id(0), pl.program_id(1), pl.program_id(2)
        step = (i * grid[1] + j) * nk + k
        if rider is not None:
            @pl.when(step == 0)
            def _():
                _stage2_start(ins, r_outs, scratch)

            @pl.when(step == steps // 2)
            def _():
                _stage2_combine(ins, r_outs, scratch)

        @pl.when(k == 0)
        def _():
            o_ref[...] = jnp.zeros_like(o_ref)

        o_ref[...] += _tn(a_ref[...].astype(BF16), b_ref[...].astype(BF16))
        if with_bf16:
            @pl.when(k == nk - 1)
            def _():
                outs[1][...] = o_ref[...].astype(BF16)
        if rider is not None:
            @pl.when(step == steps - 1)
            def _():
                _stage2_finish(ins, r_outs, scratch)

    out_blk = pl.BlockSpec((bm, bn), lambda i, j, k: (i, j))
    hbm = pl.BlockSpec(memory_space=pl.ANY)
    out_shape = [jax.ShapeDtypeStruct((m, n), F32)] + ([jax.ShapeDtypeStruct((m, n), BF16)] if with_bf16 else [])
    res = pl.pallas_call(
        body,
        name=name,
        grid=grid,
        in_specs=[pl.BlockSpec((tk, bm), lambda i, j, k: (k, i)), pl.BlockSpec((tk, bn), lambda i, j, k: (k, j))]
        + [hbm] * len(r_in),
        out_specs=[out_blk] * nout + [hbm] * len(r_out),
        out_shape=out_shape + list(r_out),
        scratch_shapes=[] if rider is None else rider["scratch"],
        compiler_params=_params(("parallel", "parallel", "arbitrary") if rider is None
                                else ("arbitrary", "arbitrary", "arbitrary")),
    )(a, b, *r_in)
    return res[0] if len(res) == 1 else res


def _dw_ffn(adu, hb, tk):
    _, t, _ = adu.shape
    bm = DFF // 2
    nk = t // tk

    def body(a_ref, b_ref, o_ref, ob_ref):
        k = pl.program_id(2)

        @pl.when(k == 0)
        def _():
            o_ref[...] = jnp.zeros_like(o_ref)

        o_ref[...] += _tn(a_ref[...], b_ref[...])

        @pl.when(k == nk - 1)
        def _():
            ob_ref[...] = o_ref[...].astype(BF16)

    out_blk = pl.BlockSpec((None, bm, D), lambda p, i, k: (p, i, 0))
    return pl.pallas_call(
        body,
        name="dw_ffn",
        grid=(3, DFF // bm, nk),
        in_specs=[pl.BlockSpec((None, tk, bm), lambda p, i, k: (p, k, i)),
                  pl.BlockSpec((None, tk, D), lambda p, i, k: (jnp.minimum(p, 1), k, 0))],
        out_specs=[out_blk, out_blk],
        out_shape=[jax.ShapeDtypeStruct((3, DFF, D), F32), jax.ShapeDtypeStruct((3, DFF, D), BF16)],
        compiler_params=_params(("arbitrary", "arbitrary", "arbitrary")),
    )(adu, hb)


def _mix_bwd(proj3, z3, sprev, opre3, dmix3, gng, conv_w, wgu_p, pbs):
    nb, s, _ = proj3.shape
    nc = s // CHUNK
    na = len(pbs)

    def body(*refs):
        (p_ref, pprev_ref, z_ref, sp_ref, o_ref, dm_ref, gng_ref, cw_ref, wgu_ref) = refs[:9]
        pb_refs = refs[9:9 + na]
        (dproj_ref, dgng_ref, dcw_ref, dbg_ref, dwgu_ref) = refs[9 + na:14 + na]
        r2_refs = refs[14 + na:14 + 2 * na]
        ds_ref, dycn_ref = refs[14 + 2 * na:16 + 2 * na]
        stage2 = (pb_refs, r2_refs, refs[16 + 2 * na:])
        step = pl.program_id(0)
        n = nc - 1 - step

        @pl.when(step == 0)
        def _():
            _stage2_start(*stage2)
            ds_ref[...] = jnp.zeros_like(ds_ref)
            dycn_ref[...] = jnp.zeros_like(dycn_ref)
            dgng_ref[...] = jnp.zeros_like(dgng_ref)
            dcw_ref[...] = jnp.zeros_like(dcw_ref)
            dbg_ref[...] = jnp.zeros_like(dbg_ref)
            dwgu_ref[...] = jnp.zeros_like(dwgu_ref)

        @pl.when(step == RELAY_AT * nc // 8)
        def _():
            _stage2_combine(*stage2)

        r_i = lax.broadcasted_iota(jnp.int32, (CHUNK, CHUNK), 0)
        c_i = lax.broadcasted_iota(jnp.int32, (CHUNK, CHUNK), 1)
        tril16 = (r_i >= c_i).astype(BF16)
        triu16 = (r_i <= c_i).astype(BF16)
        causal = r_i >= c_i
        masks = _head_masks()
        cmask = _causal_stack_mask()
        gg = gng_ref[...]
        last_row = lax.broadcasted_iota(jnp.int32, (CHUNK, NQK), 0) == CHUNK - 1
        ones_r = jnp.ones((16, DV), BF16)
        has_prev = (n > 0).astype(F32)
        for b in range(nb):
            q = p_ref[b, :, OQ:OQ + NQK]
            k = p_ref[b, :, OK_:OK_ + NQK]
            z = z_ref[b]
            _, eb, enb, ekl, qi, ki, ks, decb = _chunk_fwd_parts(q, k, z, tril16)
            qi16 = qi.astype(BF16)
            ki16 = ki.astype(BF16)
            qs = _stack_heads(qi, masks).astype(BF16)
            sc = jnp.where(cmask, _nt(qs, ki16), 0.0).astype(BF16)
            st = sp_ref[b, 0]
            st16 = st.astype(BF16)
            dsn = ds_ref[b]
            dsn16 = dsn.astype(BF16)
            v16 = p_ref[b, :, OV:OV + NV].astype(BF16)
            do16 = []
            dgng = jnp.zeros((1, DV), F32)
            for h in range(HEADS):
                cols = slice(DV * h, DV * (h + 1))
                o = o_ref[b, :, cols]
                r = lax.rsqrt(jnp.mean(o * o, axis=-1, keepdims=True) + EPS)
                nh = o * r
                g = p_ref[b, :, OG + DV * h:OG + DV * (h + 1)]
                sg = jax.nn.sigmoid(g)
                dog = dm_ref[b, :, cols]
                dproj_ref[b, :, OG + DV * h:OG + DV * (h + 1)] = (
                    (dog * (nh * gg)) * (sg * (1.0 + g * (1.0 - sg)))).astype(BF16)
                don = dog * (g * sg)
                dgng = dgng + jnp.sum(don * nh, axis=0, keepdims=True)
                dn = don * gg
                do = r * (dn - nh * jnp.mean(dn * nh, axis=-1, keepdims=True))
                do16.append(do.astype(BF16))
            dgng_ref[...] += dgng
            do_rows = jnp.concatenate(do16, axis=0)
            v_rows = jnp.concatenate([v16[:, DV * h:DV * (h + 1)] for h in range(HEADS)], axis=0)
            dp16 = [jnp.where(causal, _nt(do16[h], v16[:, DV * h:DV * (h + 1)]), 0.0).astype(BF16)
                    for h in range(HEADS)]
            ks_dsn = _mm(_stack_heads(ks, masks).astype(BF16), dsn16)
            do_st = _nt(do_rows, st16)
            v_dsn = _nt(v_rows, dsn16)
            dp_ki = _mm(jnp.concatenate(dp16, axis=0), ki16)
            q_do = _tn(qi16, jnp.concatenate(do16, axis=1))
            dki_h = []
            for h in range(HEADS):
                rows = slice(CHUNK * h, CHUNK * (h + 1))
                cols = slice(DV * h, DV * (h + 1))
                dv = _tn(sc[rows], do16[h]) + ks_dsn[rows]
                dproj_ref[b, :, OV + DV * h:OV + DV * (h + 1)] = dv.astype(BF16)
                dki_h.append(_tn(dp16[h], qi16))
                ds_ref[b, rows, :] = decb[rows] * dsn[rows] + q_do[rows, cols]
            blocks = lambda a: [a[CHUNK * h:CHUNK * (h + 1)] for h in range(HEADS)]
            dqi = _merge_heads(blocks(dp_ki + do_st), masks)
            dki = _merge_heads(dki_h, masks)
            dks = _merge_heads(blocks(v_dsn), masks)
            dproj_ref[b, :, OQ:OQ + NQK] = (dqi * (Q_SCALE * eb)).astype(BF16)
            dproj_ref[b, :, OK_:OK_ + NQK] = (dki * enb + dks * ekl).astype(BF16)
            dks_ks = dks * ks
            db = dqi * qi - dki * ki - dks_ks
            sd = _split_bf16(dsn * st * decb, 2)
            dbl = jnp.sum(dks_ks, axis=0, keepdims=True) + (_nt(ones_r, sd[0]) + _nt(ones_r, sd[1]))[0:1, :]
            db = db + jnp.where(last_row, dbl, 0.0)
            db_parts = _split_bf16(db, 3)
            dla = _mm(triu16, db_parts[0]) + _mm(triu16, db_parts[1]) + _mm(triu16, db_parts[2])
            dz = (dla * INV_GATE_NORM) * (1.0 / (1.0 + jnp.exp(z)))
            dbg_ref[...] += jnp.sum(dz, axis=0, keepdims=True)
            dz16 = dz.astype(BF16)
            pa16 = p_ref[b, :, OA:OA + A_PAD].astype(BF16)
            dwgu_ref[...] += _tn(pa16, dz16)
            dproj_ref[b, :, OA:OA + A_PAD] = _nt(dz16, wgu_ref[...]).astype(BF16)
            cb = p_ref[b, :, OCB:OCB + CW]
            cc = p_ref[b, :, OCC:OCC + CW]
            ch = p_ref[b, :, OCH:OCH + CW]
            u = cc * ch
            uprev = (pprev_ref[b, :, 0:CW] * pprev_ref[b, :, CW:2 * CW]) * has_prev
            u1, u2 = _conv_taps(u, uprev)
            w0 = cw_ref[0:1, :]
            w1 = cw_ref[1:2, :]
            w2 = cw_ref[2:3, :]
            yc = w0 * u2 + w1 * u1 + w2 * u
            doc = dm_ref[b, :, NV:NV + CW]
            dproj_ref[b, :, OCB:OCB + CW] = (doc * yc).astype(BF16)
            dyc = doc * cb
            dycn = dycn_ref[b]
            row = lax.broadcasted_iota(jnp.int32, dyc.shape, 0)
            d1 = jnp.where(row >= CHUNK - 1, pltpu.roll(dycn, CHUNK - 1, 0), pltpu.roll(dyc, CHUNK - 1, 0))
            d2 = jnp.where(row >= CHUNK - 2, pltpu.roll(dycn, CHUNK - 2, 0), pltpu.roll(dyc, CHUNK - 2, 0))
            du = w2 * dyc + w1 * d1 + w0 * d2
            dproj_ref[b, :, OCC:OCC + CW] = (du * ch).astype(BF16)
            dproj_ref[b, :, OCH:OCH + CW] = (du * cc).astype(BF16)
            dcw_ref[0:1, :] += jnp.sum(dyc * u2, axis=0, keepdims=True)
            dcw_ref[1:2, :] += jnp.sum(dyc * u1, axis=0, keepdims=True)
            dcw_ref[2:3, :] += jnp.sum(dyc * u, axis=0, keepdims=True)
            dycn_ref[b] = dyc

        @pl.when(step == nc - 1)
        def _():
            _stage2_finish(*stage2)

    rev =lambda w: pl.BlockSpec((nb, CHUNK, w), lambda i: (0, nc - 1 - i, 0))
    const = lambda r, c: pl.BlockSpec((r, c), lambda i: (0, 0))
    hbm = pl.BlockSpec(memory_space=pl.ANY)
    return pl.pallas_call(
        body,
        name="mix_bwd",
        grid=(nc,),
        in_specs=[
            rev(PW),
            pl.BlockSpec((nb, CHUNK, 2 * CW), lambda i: (0, jnp.maximum(nc - 2 - i, 0), OCC // (2 * CW))),
            rev(NQK),
            pl.BlockSpec((nb, 1, NQK, DV), lambda i: (0, nc - 1 - i, 0, 0)),
            rev(NV),
            rev(D),
            const(1, DV),
            const(CONV_K, CW),
            const(A_PAD, NQK),
        ] + [hbm] * na,
        out_specs=[rev(PW), const(1, DV), const(8, CW), const(1, NQK), const(A_PAD, NQK)] + [hbm] * na,
        out_shape=[
            jax.ShapeDtypeStruct((nb, s, PW), BF16),
            jax.ShapeDtypeStruct((1, DV), F32),
            jax.ShapeDtypeStruct((8, CW), F32),
            jax.ShapeDtypeStruct((1, NQK), F32),
            jax.ShapeDtypeStruct((A_PAD, NQK), F32),
        ] + [jax.ShapeDtypeStruct((2,) + p.shape[1:], BF16) for p in pbs],
        scratch_shapes=[pltpu.VMEM((nb, NQK, DV), F32), pltpu.VMEM((nb, CHUNK, CW), F32)] + _stage2_scratch(pbs),
        compiler_params=_params(("arbitrary",)),
    )(proj3, proj3, z3, sprev, opre3, dmix3, gng, conv_w, wgu_p, *pbs)


SMALL_PACK_ROWS = 16


def _wgu_slot(r):
    return 4 + r // 4, NQK * (r % 4)


CONV_SLOTS = ((8, 0), (8, CW), (9, 0))


def _in_proj_bwd(dproj2d, x2d, dx1, g1, w_in_t, tm, pb, small_parts):
    t = x2d.shape[0]
    nt = t // tm

    def body(dp_ref, x_ref, dx1_ref, g_ref, w_ref, pb_ref, dg2, dgf, dbg, dgng, dwgu, dcw, lp,
             dx_ref, sums_ref, r2_ref, dg1_acc, pack, gbuf, pack1, gbuf1, ssend, srecv, ssend1, srecv1, *scratch2):
        stage2 = ([pb_ref], [r2_ref], scratch2)
        x, y, c = _position()
        me = 4 * x + 2 * y + c
        flips = [(k >> 2, (k >> 1) & 1, k & 1) for k in range(1, N_DEV)]
        peers = [(x ^ fx, y ^ fy, c ^ fc) for fx, fy, fc in flips]

        def small_copies(src, dst, send, recv, arrivals):
            return [pltpu.make_async_remote_copy(
                src_ref=src, dst_ref=dst.at[4 * px + 2 * py + pc if arrivals else me],
                send_sem=send.at[k], recv_sem=recv.at[k], device_id=(px, py, pc), device_id_type=MESH)
                for k, (px, py, pc) in enumerate(peers)]

        @pl.when(pl.program_id(0) == 0)
        def _():
            _stage2_start(*stage2)
            dg1_acc[...] = jnp.zeros_like(dg1_acc)
            pack[...] = jnp.zeros_like(pack)
            pack[1:2, :] = dg2[...]
            pack[2:3, :] = dgf[...]
            pack[3:4, 0:NQK] = dbg[...]
            pack[3:4, NQK:NQK + DV] = dgng[...]
            pack[3:4, NQK + DV:NQK + 2 * DV] = lp[...]
            for r in range(RANK):
                row, lane = _wgu_slot(r)
                pack[row:row + 1, lane:lane + NQK] = dwgu[r:r + 1, :]
            for r, (row, lane) in enumerate(CONV_SLOTS):
                pack[row:row + 1, lane:lane + CW] = dcw[r:r + 1, :]
            for cp in small_copies(pack, gbuf, ssend, srecv, False):
                cp.start()
            gbuf[me] = pack[...]

        @pl.when(pl.program_id(0) == RELAY_AT * nt // 8)
        def _():
            _stage2_combine(*stage2)

        xv = x_ref[...]
        r = lax.rsqrt(jnp.mean(xv * xv, axis=-1, keepdims=True) + EPS)
        n1 = xv * r
        dh = _mm(dp_ref[...], w_ref[...])
        dg1_acc[...] += jnp.sum(dh * n1, axis=0, keepdims=True)
        dn = dh * g_ref[...]
        dx_ref[...] = dx1_ref[...] + r * (dn - n1 * jnp.mean(dn * n1, axis=-1, keepdims=True))

        @pl.when(pl.program_id(0) == nt - 1)
        def _():
            pack1[...] = jnp.zeros_like(pack1)
            pack1[0:1, :] = dg1_acc[...]
            for cp in small_copies(pack1, gbuf1, ssend1, srecv1, False):
                cp.start()
            gbuf1[me] = pack1[...]
            _stage2_finish(*stage2)
            for src, dst, send, recv in ((pack, gbuf, ssend, srecv), (pack1, gbuf1, ssend1, srecv1)):
                for cp in small_copies(src, dst, send, recv, True):
                    cp.wait_recv()
                    cp.wait_send()
            acc = gbuf[0]
            acc1 = gbuf1[0]
            for d in range(1, N_DEV):
                acc = acc + gbuf[d]
                acc1 = acc1 + gbuf1[d]
            sums_ref[...] = acc
            sums_ref[0:1, :] = acc1[0:1, :]

    tile = lambda w: pl.BlockSpec((tm, w), lambda i: (i, 0))
    vec = pl.BlockSpec((1, D), lambda i: (0, 0))
    hbm = pl.BlockSpec(memory_space=pl.ANY)
    whole = lambda a: pl.BlockSpec(a.shape, lambda i: (0,) * a.ndim)
    return pl.pallas_call(
        body,
        name="in_proj_bwd",
        grid=(nt,),
        in_specs=[tile(PW), tile(D), tile(D), vec, pl.BlockSpec((PW, D), lambda i: (0, 0)), hbm]
        + [whole(a) for a in small_parts],
        out_specs=[tile(D), pl.BlockSpec((SMALL_PACK_ROWS, D), lambda i: (0, 0)), hbm],
        out_shape=[jax.ShapeDtypeStruct((t, D), F32), jax.ShapeDtypeStruct((SMALL_PACK_ROWS, D), F32),
                   jax.ShapeDtypeStruct((2,) + pb.shape[1:], BF16)],
        scratch_shapes=[pltpu.VMEM((1, D), F32),
                        pltpu.VMEM((SMALL_PACK_ROWS, D), F32), pltpu.VMEM((N_DEV, SMALL_PACK_ROWS, D), F32),
                        pltpu.VMEM((8, D), F32), pltpu.VMEM((N_DEV, 8, D), F32),
                        pltpu.SemaphoreType.DMA((7,)), pltpu.SemaphoreType.DMA((7,)),
                        pltpu.SemaphoreType.DMA((7,)), pltpu.SemaphoreType.DMA((7,))] + _stage2_scratch([pb]),
        compiler_params=_params(("arbitrary",)),
    )(dproj2d, x2d, dx1, g1, w_in_t, pb, *small_parts)


def _get_rows(ref):
    return ref[:, 0, :] if len(ref.shape) == 3 else ref[...]


def _put_rows(ref, val):
    if len(ref.shape) == 3:
        ref[:, 0, :] = val
    else:
        ref[...] = val


def _adamw_math(w, g, m, v):
    m = ADAM_B1 * m + (1.0 - ADAM_B1) * g
    v = ADAM_B2 * v + (1.0 - ADAM_B2) * (g * g)
    m_hat = m / (1.0 - ADAM_B1 ** ADAM_STEP)
    v_hat = v / (1.0 - ADAM_B2 ** ADAM_STEP)
    delta = -ADAM_LR * (m_hat / (jnp.sqrt(v_hat) + ADAM_EPS) + ADAM_WD * w)
    return delta, m, v


def _position():
    return lax.axis_index("x"), lax.axis_index("y"), lax.axis_index("c")


GATHER_PARTS = 2
GATHER_SEMS = 7 * GATHER_PARTS
RELAY_AT = 3
DIRECT_AT = 6
FORWARD_AT = 7


def _gather_copies(stage, lo, rows, gx, send_sems, recv_sems, local_sem):
    x, y, c = _position()
    me = (x, y, c)
    sibling = (x, y, 1 - c)
    chips = [(1 - x, y), (x, 1 - y), (1 - x, 1 - y)]
    part = -(-rows // (16 * GATHER_PARTS)) * 16
    bounds = [(p * part, min(part, rows - p * part)) for p in range(GATHER_PARTS)]

    def blk(px, py, pc, off, n):
        return gx.at[4 * px + 2 * py + pc, pl.ds(off, n), :]

    mine = pltpu.make_async_copy(stage.at[pl.ds(lo, rows), :], gx.at[4 * x + 2 * y + c], local_sem)
    parts = []
    for p, (off, n) in enumerate(bounds):
        def copy(k, block, to, from_stage=False, p=p, off=off, n=n):
            return pltpu.make_async_remote_copy(
                src_ref=stage.at[pl.ds(lo + off, n), :] if from_stage else blk(*block, off, n),
                dst_ref=blk(*block, off, n), send_sem=send_sems.at[7 * p + k], recv_sem=recv_sems.at[7 * p + k],
                device_id=to, device_id_type=MESH)

        first = [copy(0, me, sibling, True)] + [copy(1 + j, me, (*chips[j], c), True) for j in range(2)]
        relay = copy(3, (*chips[p], c), (*chips[1 - p], c))
        passed = [copy(4 + j, (*chip, c), sibling) for j, chip in enumerate(chips)]
        arrivals = ([copy(0, sibling, me)] + [copy(1 + j, (*chip, c), me) for j, chip in enumerate(chips)]
                    + [copy(4 + j, (*chip, 1 - c), me) for j, chip in enumerate(chips)])
        parts.append((first, relay, passed, arrivals))
    return mine, parts


def _gather_start(*args):
    mine, parts = _gather_copies(*args)
    mine.start()
    for first, _, _, _ in parts:
        first[0].start()
    for p, q in ((0, 0), (1, 1), (0, 1), (1, 0)):
        parts[p][0][1 + q].start()


def _gather_relay(*args):
    _, parts = _gather_copies(*args)
    for p, (_, relay, passed, arrivals) in enumerate(parts):
        arrivals[1 + p].wait_recv()
        relay.start()
        passed[p].start()


def _gather_forward(*args, relayed):
    _, parts = _gather_copies(*args)
    for p, j in (((0, 2), (1, 2)) if relayed else ((0, 1), (1, 0))):
        _, _, passed, arrivals = parts[p]
        arrivals[1 + j].wait_recv()
        passed[j].start()


def _gather_finish(*args):
    mine, parts = _gather_copies(*args)
    for first, relay, passed, arrivals in parts:
        arrivals[0].wait_recv()
        for j in range(3):
            arrivals[4 + j].wait_recv()
        for cp in first + [relay] + passed:
            cp.wait_send()
    mine.wait()


def _gather_sems():
    return [pltpu.SemaphoreType.DMA((GATHER_SEMS,)), pltpu.SemaphoreType.DMA((GATHER_SEMS,)), pltpu.SemaphoreType.DMA]


def _gather_w_in(w_it, w_gt, w_ut, w_d, w_o, wgu_s, conv_s):
    def body(wi_hbm, wg_hbm, wu_hbm, wd_hbm, wo_hbm, wgu_ref, conv_ref, w_ref, gwgu_ref, gconv_ref, stage,
             buf, wif, wf, wof, send_sems, recv_sems, local_sem, ssend, srecv, load_sems):
        x, y, c = _position()
        me = 4 * x + 2 * y + c
        loads = [pltpu.make_async_copy(src, dst, load_sems.at[n]) for n, (src, dst) in enumerate(
            ((wi_hbm.at[:, 0, :], wif), (wg_hbm, wf.at[0]), (wu_hbm, wf.at[1]), (wd_hbm, wf.at[2]), (wo_hbm, wof)))]
        for cp in loads:
            cp.start()
        loads.pop(0).wait()
        stage[SLAB_IN:SLAB_IN + IN_W, :] = wif[...].astype(BF16)
        stage[SLAB_IN + IN_W:SLAB_G, :] = jnp.zeros((IN_ROWS - IN_W, D), BF16)
        args = (stage, SLAB_IN, IN_ROWS, buf, send_sems, recv_sems, local_sem)
        _gather_start(*args)
        for n, lo in enumerate((SLAB_G, SLAB_U, SLAB_D)):
            loads[n].wait()
            stage[lo:lo + FF_W, :] = wf[n].astype(BF16)
        loads[3].wait()
        stage[SLAB_O:SLAB_ROWS, :] = wof[...].astype(BF16)
        flips = [(k >> 2, (k >> 1) & 1, k & 1) for k in range(1, N_DEV)]
        peers = [(x ^ fx, y ^ fy, c ^ fc) for fx, fy, fc in flips]

        def small(k, block_id, to):
            return [pltpu.make_async_remote_copy(
                src_ref=s, dst_ref=g.at[block_id], send_sem=ssend.at[2 * k + n], recv_sem=srecv.at[2 * k + n],
                device_id=to, device_id_type=MESH)
                for n, (s, g) in enumerate(((wgu_ref, gwgu_ref), (conv_ref, gconv_ref)))]

        gwgu_ref[me] = wgu_ref[...]
        gconv_ref[me] = conv_ref[...]
        for k, peer in enumerate(peers):
            for cp in small(k, me, peer):
                cp.start()
        w_ref[IN_COLS:PW, :] = jnp.zeros((PW - IN_COLS, D), BF16)
        _gather_relay(*args)
        _gather_forward(*args, relayed=False)
        _gather_forward(*args, relayed=True)
        _gather_finish(*args)
        for k, (px, py, pc) in enumerate(peers):
            for cp in small(k, 4 * px + 2 * py + pc, (px, py, pc)):
                cp.wait_recv()
                cp.wait_send()
        for j, lo, hi, d in _in_segments():
            w_ref[d:d + hi - lo, :] = buf[j, lo:hi, :]

    vm = pl.BlockSpec(memory_space=pltpu.VMEM)
    return pl.pallas_call(
        body,
        name="gather_w_in",
        in_specs=[pl.BlockSpec(memory_space=pl.ANY)] * 5 + [vm] * 2,
        out_specs=[vm] * 4,
        out_shape=[jax.ShapeDtypeStruct((PW, D), BF16),
                   jax.ShapeDtypeStruct((N_DEV,) + wgu_s.shape, F32),
                   jax.ShapeDtypeStruct((N_DEV,) + conv_s.shape, F32),
                   jax.ShapeDtypeStruct((SLAB_ROWS, D), BF16)],
        scratch_shapes=[pltpu.VMEM((N_DEV, IN_ROWS, D), BF16), pltpu.VMEM((IN_W, D), F32),
                        pltpu.VMEM((3, FF_W, D), F32), pltpu.VMEM((OUT_ROWS, D), F32)] + _gather_sems()
        + [pltpu.SemaphoreType.DMA((14,)), pltpu.SemaphoreType.DMA((14,)), pltpu.SemaphoreType.DMA((5,))],
        compiler_params=_params(),
    )(w_it, w_gt, w_ut, w_d, w_o, wgu_s, conv_s)


def _w_in_core_reduce(dw_t):
    def body(d_ref, own_ref, sib_ref, pb_ref, g, gb, r1, send_sems, recv_sems):
        x, y, c = _position()
        chip = 2 * x + y
        for j in range(N_DEV):
            g[j, IN_W:IN_ROWS, :] = jnp.zeros((IN_ROWS - IN_W, D), F32)
        for j, lo, hi, d in _in_segments():
            g[j, lo:hi, :] = d_ref[d:d + hi - lo, :]
        for j in range(N_DEV):
            gb[j] = g[j].astype(BF16)
        copies = _stage1_copies(gb, r1, send_sems, recv_sems)
        for cp in copies:
            cp.start()
        own_ref[0] = g[2 * chip + c]
        for cp in copies:
            cp.wait_recv()
        sib_ref[0] = r1[chip]
        for k in range(1, 4):
            t = chip ^ k
            pb_ref[k - 1] = (g[2 * t + c] + r1[t].astype(F32)).astype(BF16)
        for cp in copies:
            cp.wait_send()

    vm = pl.BlockSpec(memory_space=pltpu.VMEM)
    return pl.pallas_call(
        body,
        name="w_in_core_reduce",
        in_specs=[vm],
        out_specs=[vm, vm, vm],
        out_shape=[jax.ShapeDtypeStruct((1, IN_ROWS, D), F32), jax.ShapeDtypeStruct((1, IN_ROWS, D), BF16),
                   jax.ShapeDtypeStruct((3, IN_ROWS, D), BF16)],
        scratch_shapes=[pltpu.VMEM((N_DEV, IN_ROWS, D), F32), pltpu.VMEM((N_DEV, IN_ROWS, D), BF16),
                        pltpu.VMEM((4, IN_ROWS, D), BF16), pltpu.SemaphoreType.DMA((4,)),
                        pltpu.SemaphoreType.DMA((4,))],
        compiler_params=_params(),
    )(dw_t)


def _stage1_copies(g_ref, r_ref, send_sems, recv_sems):
    x, y, c = _position()
    return [pltpu.make_async_remote_copy(
        src_ref=g_ref.at[2 * i + 1 - c], dst_ref=r_ref.at[i], send_sem=send_sems.at[i], recv_sem=recv_sems.at[i],
        device_id=(x, y, 1 - c), device_id_type=MESH) for i in range(4)]


def _ffn_core_reduce(dw3, dwb3, dw_o, dwb_o, pos_arr, dx1b, gwb):
    def body(pos_ref, g0, g1, g2, go, gb3_hbm, gbo_hbm, dx1b_ref, gwb_hbm, p0, p1, p2, po, s0, s1, s2, so, dmix_ref,
             r1f, r1o, wo, send_sems, recv_sems, wsem):
        step = pl.program_id(0)
        k = jnp.minimum(step, 2)
        x, y, c = _position()
        chip = 2 * x + y

        def copies(p):
            src = 2 * (chip ^ ((p + 1) & 3)) + 1 - c
            pairs = [(gb3_hbm.at[a, src], r1f.at[a, p]) for a in range(3)] + [(gbo_hbm.at[src], r1o.at[p])]
            return [pltpu.make_async_remote_copy(
                src_ref=s, dst_ref=d, send_sem=send_sems.at[4 * p + a], recv_sem=recv_sems.at[4 * p + a],
                device_id=(x, y, 1 - c), device_id_type=MESH) for a, (s, d) in enumerate(pairs)]

        @pl.when(step == 0)
        def _():
            for p in range(4):
                for cp in copies(p):
                    cp.start()
            loads = [pltpu.make_async_copy(gwb_hbm.at[j, pl.ds(D_TAIL, OUT_ROWS), :],
                                           wo.at[pl.ds(OUT_ROWS * j, OUT_ROWS), :], wsem.at[j]) for j in range(N_DEV)]
            for cp in loads:
                cp.start()
            for cp in loads:
                cp.wait()

        dmix_ref[...] = _nt(dx1b_ref[...], wo[...])

        for p in range(3):
            @pl.when(step == p)
            def _():
                for cp in copies(p):
                    cp.wait_recv()

        for a, (g, pb) in enumerate(((g0, p0), (g1, p1), (g2, p2))):
            pb[...] = (g[...] + r1f[a, k][None].astype(F32)).astype(BF16)
        po[...] = (go[...] + r1o[k][None].astype(F32)).astype(BF16)

        @pl.when(step == 3)
        def _():
            for cp in copies(3):
                cp.wait_recv()
            for a, s in enumerate((s0, s1, s2)):
                s[0] = r1f[a, 3]
            so[0] = r1o[3]
            for p in range(4):
                for cp in copies(p):
                    cp.wait_send()

    t = dx1b.shape[0]
    other = lambda s, pos: 2 * (pos[1] ^ (jnp.minimum(s, 2) + 1)) + pos[0]
    g_spec = lambda lead: pl.BlockSpec((None, 1, FF_W, D), lambda s, pos: (lead, other(s, pos), 0, 0))
    slot = lambda rows: pl.BlockSpec((1, rows, D), lambda s, pos: (jnp.minimum(s, 2), 0, 0))
    one = lambda rows: pl.BlockSpec((1, rows, D), lambda s, pos: (0, 0, 0))
    quarter = pl.BlockSpec((t // 4, D), lambda s, pos: (s, 0))
    hbm = pl.BlockSpec(memory_space=pl.ANY)
    return pl.pallas_call(
        body,
        name="ffn_core_reduce",
        grid_spec=pltpu.PrefetchScalarGridSpec(
            num_scalar_prefetch=1, grid=(4,),
            in_specs=[g_spec(0), g_spec(1), g_spec(2),
                      pl.BlockSpec((1, OUT_ROWS, D), lambda s, pos: (other(s, pos), 0, 0)), hbm, hbm, quarter, hbm],
            out_specs=[slot(FF_W), slot(FF_W), slot(FF_W), slot(OUT_ROWS),
                       one(FF_W), one(FF_W), one(FF_W), one(OUT_ROWS), quarter],
            scratch_shapes=[pltpu.VMEM((3, 4, FF_W, D), BF16), pltpu.VMEM((4, OUT_ROWS, D), BF16),
                            pltpu.VMEM((D, D), BF16), pltpu.SemaphoreType.DMA((16,)),
                            pltpu.SemaphoreType.DMA((16,)), pltpu.SemaphoreType.DMA((N_DEV,))]),
        out_shape=[jax.ShapeDtypeStruct((3, FF_W, D), BF16)] * 3 + [jax.ShapeDtypeStruct((3, OUT_ROWS, D), BF16)]
        + [jax.ShapeDtypeStruct((1, FF_W, D), BF16)] * 3 + [jax.ShapeDtypeStruct((1, OUT_ROWS, D), BF16),
                                                             jax.ShapeDtypeStruct((t, D), F32)],
        compiler_params=_params(("arbitrary",)),
    )(pos_arr, dw3, dw3, dw3, dw_o, dwb3, dwb_o, dx1b, gwb)


def _stage2_scratch(pbs):
    n = len(pbs)
    return ([pltpu.VMEM(p.shape[1:], BF16) for p in pbs] * 2
            + [pltpu.SemaphoreType.DMA((6 * n,)), pltpu.SemaphoreType.DMA((6 * n,)), pltpu.SemaphoreType.DMA((2 * n,))])


def _stage2_copies(p_refs, r_refs, scratch):
    n = len(p_refs)
    owns, gots = scratch[:n], scratch[n:2 * n]
    send_sems, recv_sems, load_sems = scratch[2 * n:]
    x, y, c = _position()
    xn, yn = (1 - x, y, c), (x, 1 - y, c)
    loads, first, second = [], [], []
    for a, (p, r, own, got) in enumerate(zip(p_refs, r_refs, owns, gots)):
        rows = p.shape[1]
        half = -(-rows // 32) * 16
        h0, h1 = pl.ds(0, half), pl.ds(half, rows - half)

        def remote(k, src, dst, to, a=a):
            return pltpu.make_async_remote_copy(
                src_ref=src, dst_ref=dst, send_sem=send_sems.at[6 * a + k], recv_sem=recv_sems.at[6 * a + k],
                device_id=to, device_id_type=MESH)

        loads += [pltpu.make_async_copy(p.at[0, h0, :], own.at[h0, :], load_sems.at[2 * a]),
                  pltpu.make_async_copy(p.at[1, h1, :], own.at[h1, :], load_sems.at[2 * a + 1])]
        first += [remote(0, p.at[2, h0, :], got.at[h0, :], xn), remote(1, p.at[2, h1, :], got.at[h1, :], yn),
                  remote(2, p.at[1, h0, :], r.at[1, h0, :], xn), remote(3, p.at[0, h1, :], r.at[0, h1, :], yn)]
        second += [remote(4, own.at[h0, :], r.at[0, h0, :], yn), remote(5, own.at[h1, :], r.at[1, h1, :], xn)]
    return loads, first, second


def _stage2_start(p_refs, r_refs, scratch):
    loads, first, _ = _stage2_copies(p_refs, r_refs, scratch)
    for cp in loads:
        cp.start()
    for k in range(4):
        for cp in first[k::4]:
            cp.start()


def _stage2_combine(p_refs, r_refs, scratch):
    n = len(p_refs)
    loads, first, second = _stage2_copies(p_refs, r_refs, scratch)
    for a in range(n):
        for cp in loads[2 * a:2 * a + 2]:
            cp.wait()
        for cp in first[4 * a:4 * a + 2]:
            cp.wait_recv()
        own, got = scratch[a], scratch[n + a]
        own[...] = (own[...].astype(F32) + got[...].astype(F32)).astype(BF16)
        for cp in second[2 * a:2 * a + 2]:
            cp.start()


def _stage2_finish(p_refs, r_refs, scratch):
    _, first, second = _stage2_copies(p_refs, r_refs, scratch)
    for a in range(len(p_refs)):
        for cp in first[4 * a + 2:4 * a + 4] + second[2 * a:2 * a + 2]:
            cp.wait_recv()
    for cp in first + second:
        cp.wait_send()


def _finish_weights(items, pos_arr, name, nblk):
    n = len(items)
    in_specs, out_specs, out_shape, operands, wbs = [], [], [], [], []
    for g8, lead, r1, r2, w, m, v in items:
        rows, wr = g8.shape[-2], w.shape[0]
        assert rows % nblk == 0 and wr % nblk == 0 and (nblk == 1 or (rows == wr and rows % (16 * nblk) == 0))
        rb, wb = rows // nblk, wr // nblk
        if lead is not None:
            g_spec = pl.BlockSpec((None, 1, rb, D), lambda i, pos, lead=lead: (lead, 2 * pos[1] + pos[0], i, 0))
        elif g8.shape[0] == 1:
            g_spec = pl.BlockSpec((1, rb, D), lambda i, pos: (0, i, 0))
        else:
            g_spec = pl.BlockSpec((1, rb, D), lambda i, pos: (2 * pos[1] + pos[0], i, 0))
        r1_spec = pl.BlockSpec((1, rb, D), lambda i, pos: (0, i, 0))
        if w.ndim == 3:
            wblk = pl.BlockSpec((wb, 1, D), lambda i, pos: (i, 0, 0))
        else:
            wblk = pl.BlockSpec((wb, D), lambda i, pos: (i, 0))
        in_specs += [g_spec, r1_spec, pl.BlockSpec((2, rb, D), lambda i, pos: (0, i, 0)), wblk, wblk, wblk]
        out_specs += [wblk] * 4
        out_shape += [jax.ShapeDtypeStruct(w.shape, F32)] * 4
        operands += [g8, r1, r2, w, m, v]
        wbs.append(wb)

    def body(pos_ref, *refs):
        for a in range(n):
            g_ref, r1_ref, r2_ref, w_ref, m_ref, v_ref = refs[6 * a:6 * a + 6]
            g_out, d_out, m_out, v_out = refs[6 * n + 4 * a:6 * n + 4 * a + 4]
            g = g_ref[0] + r1_ref[0].astype(F32)
            for k in range(2):
                g = g + r2_ref[k].astype(F32)
            g = g[0:wbs[a], :]
            d, mn, vn = _adamw_math(_get_rows(w_ref), g, _get_rows(m_ref), _get_rows(v_ref))
            for out, val in ((g_out, g), (d_out, d), (m_out, mn), (v_out, vn)):
                _put_rows(out, val)

    return pl.pallas_call(
        body,
        name=name,
        grid_spec=pltpu.PrefetchScalarGridSpec(
            num_scalar_prefetch=1, grid=(nblk,), in_specs=in_specs, out_specs=out_specs),
        out_shape=out_shape,
        compiler_params=_params(("arbitrary",)),
    )(pos_arr, *operands)


SMALL_NAMES = ("norm1_g", "norm2_g", "norm_f_g", "b_gate", "gla_norm_g", "w_gate_up", "conv_w")
WGU_W = NQK // N_DEV
CONV_W = CW // N_DEV


def _small_adamw(sums, ws, ms, vs):
    n = len(SMALL_NAMES)

    def body(*refs):
        acc_ref = refs[0]
        w_refs, m_refs, v_refs = refs[1:1 + n], refs[1 + n:1 + 2 * n], refs[1 + 2 * n:1 + 3 * n]
        loss_ref = refs[1 + 3 * n]
        outs = refs[2 + 3 * n:]
        x, y, c = _position()
        me = 4 * x + 2 * y + c
        acc = acc_ref[...]
        loss_ref[...] = acc[3:4, NQK + DV:NQK + DV + 1]

        def my_columns(full, width):
            r = lax.broadcasted_iota(jnp.int32, (full.shape[1], width), 0)
            col = lax.broadcasted_iota(jnp.int32, (full.shape[1], width), 1)
            sel = (r == width * me + col).astype(F32)
            return _mm(full, sel, precision=HIGHEST)

        dwgu = jnp.concatenate([acc[row:row + 1, lane:lane + NQK] for row, lane in map(_wgu_slot, range(RANK))], axis=0)
        dcw = jnp.concatenate([acc[row:row + 1, lane:lane + CW] for row, lane in CONV_SLOTS], axis=0)
        grads = [acc[0:1, :], acc[1:2, :], acc[2:3, :], acc[3:4, 0:NQK], acc[3:4, NQK:NQK + DV],
                 my_columns(dwgu, WGU_W), my_columns(dcw, CONV_W)]
        for i, g in enumerate(grads):
            d, mn, vn = _adamw_math(_get_rows(w_refs[i]), g, _get_rows(m_refs[i]), _get_rows(v_refs[i]))
            for out, val in zip(outs[4 * i:4 * i + 4], (g, d, mn, vn)):
                _put_rows(out, val)

    vm = pl.BlockSpec(memory_space=pltpu.VMEM)
    out_shape = [jax.ShapeDtypeStruct((1, 1), F32)]
    for w in ws:
        out_shape += [jax.ShapeDtypeStruct(w.shape, F32)] * 4
    return pl.pallas_call(
        body,
        name="small_adamw",
        in_specs=[vm] * (1 + 3 * n),
        out_specs=[vm] * (1 + 4 * n),
        out_shape=out_shape,
        compiler_params=_params(),
    )(sums, *ws, *ms, *vs)


def kernel(x, norm1_g, w_in, w_gate_up, b_gate, gla_norm_g, conv_w, w_out, norm2_g, w_ffn_gate, w_ffn_up, w_ffn_down, norm_f_g, loss_target, m_norm1_g, m_w_in, m_w_gate_up, m_b_gate, m_gla_norm_g, m_conv_w, m_w_out, m_norm2_g, m_w_ffn_gate, m_w_ffn_up, m_w_ffn_down, m_norm_f_g, v_norm1_g, v_w_in, v_w_gate_up, v_b_gate, v_gla_norm_g, v_conv_w, v_w_out, v_norm2_g, v_w_ffn_gate, v_w_ffn_up, v_w_ffn_down, v_norm_f_g):
    xi, yi, ci = _position()
    pos_arr = jnp.stack([ci, 2 * xi + yi]).astype(jnp.int32)
    nb, s, _ = x.shape
    t = nb * s

    tr = lambda a: a[0].T
    rows_of = lambda a: a.transpose(2, 0, 1)
    conv_rows = lambda a: a.transpose(1, 0, 2)
    w_in_t, gwgu, gconv, stage = _gather_w_in(rows_of(w_in), tr(w_ffn_gate), tr(w_ffn_up), w_ffn_down[0], w_out[0],
                                              w_gate_up[0], conv_rows(conv_w))
    wgu_f = gwgu.transpose(1, 0, 2).reshape(RANK, NQK)
    conv_f = gconv.transpose(1, 2, 0, 3).reshape(CONV_K, CW)
    wgu_p = jnp.concatenate([wgu_f, jnp.zeros((A_PAD - RANK, NQK), F32)], axis=0).astype(BF16)

    x2d = x.reshape(t, D)
    tgt2d = loss_target.reshape(t, D)
    tm = 256
    tm_in = min(512, t)
    tk = min(2048, t)
    proj, z, h, gwb = _in_proj_fwd(x2d, norm1_g, w_in_t, wgu_p, b_gate, tm_in, stage)
    proj3 = proj.reshape(nb, s, PW)
    z3 = z.reshape(nb, s, NQK)
    mix3, opre3, sprev, x1, gwa = _mix_fwd(proj3, z3, gla_norm_g, conv_f, stage, x, gwb)
    mix2d = mix3.reshape(t, D)
    dx1, dx1b, adu, hb, dg2, dgf, loss_part = _ffn_fwd_bwd(
        x1.reshape(t, D), tgt2d, gwa, gwb, norm2_g, norm_f_g.reshape(1, D), tm)
    dw3, dwb3 = _dw_ffn(adu, hb, tk)
    dw3 = dw3.reshape(3, N_DEV, FF_W, D)
    dw_o, dwb_o = _tn_matmul(mix2d, dx1b, D // 2, D, tk, "dw_out", True)
    dw_o = dw_o.reshape(N_DEV, OUT_ROWS, D)
    *pb, sib_d, sib_g, sib_u, sib_o, dmix = _ffn_core_reduce(
        dw3, dwb3.reshape(3, N_DEV, FF_W, D), dw_o, dwb_o.reshape(N_DEV, OUT_ROWS, D), pos_arr, dx1b, gwb)
    g8 = [dw3, dw3, dw3, dw_o]
    leads = [0, 1, 2, None]
    tags = ("w_ffn_down", "w_ffn_gate", "w_ffn_up", "w_out")
    r1 = [sib_d, sib_g, sib_u, sib_o]
    mb = _mix_bwd(proj3, z3, sprev, opre3, dmix.reshape(nb, s, D), gla_norm_g, conv_f, wgu_p, [pb[0], pb[1], pb[3]])
    dproj3, dgng, dcw, dbg, dwgu = mb[:5]
    dproj2d = dproj3.reshape(t, PW)
    dw_in_t, r2_up = _tn_matmul(dproj2d, h, PW // 5, D, t, "dw_in", False, _stage2_rider([pb[2]]))
    r2 = [mb[5], mb[6], r2_up, mb[7]]
    g_in, r1_in, pb_in = _w_in_core_reduce(dw_in_t)
    dx, small_sums, r2_in = _in_proj_bwd(dproj2d, x2d, dx1, norm1_g, w_in_t, tm_in, pb_in,
                                         (dg2, dgf, dbg, dgng, dwgu, dcw, loss_part))

    tags = ("w_in",) + tags
    g8 = [g_in] + g8
    leads = [None] + leads
    r1 = [r1_in] + list(r1)
    r2 = [r2_in] + r2
    shard_w = (rows_of(w_in), w_ffn_down[0], tr(w_ffn_gate), tr(w_ffn_up), w_out[0])
    shard_m = (rows_of(m_w_in), m_w_ffn_down[0], tr(m_w_ffn_gate), tr(m_w_ffn_up), m_w_out[0])
    shard_v = (rows_of(v_w_in), v_w_ffn_down[0], tr(v_w_ffn_gate), tr(v_w_ffn_up), v_w_out[0])
    back = (lambda o: o.transpose(1, 2, 0), lambda o: o[None], lambda o: o.T[None], lambda o: o.T[None],
            lambda o: o[None])
    items = list(zip(g8, leads, r1, r2, shard_w, shard_m, shard_v))
    flat = list(_finish_weights(items[1:], pos_arr, "finish_ffn_out", 2))
    flat = list(_finish_weights(items[:1], pos_arr, "finish_w_in", 1)) + flat
    results = {}
    for i, (tag, to_shard) in enumerate(zip(tags, back)):
        results[tag] = [to_shard(o) for o in flat[4 * i:4 * i + 4]]

    small_w = (norm1_g, norm2_g, norm_f_g.reshape(1, D), b_gate, gla_norm_g, w_gate_up[0], conv_rows(conv_w))
    small_m = (m_norm1_g, m_norm2_g, m_norm_f_g.reshape(1, D), m_b_gate, m_gla_norm_g, m_w_gate_up[0],
               conv_rows(m_conv_w))
    small_v = (v_norm1_g, v_norm2_g, v_norm_f_g.reshape(1, D), v_b_gate, v_gla_norm_g, v_w_gate_up[0],
               conv_rows(v_conv_w))
    so = _small_adamw(small_sums, small_w, small_m, small_v)
    loss = so[0].reshape(())
    to_shape = {"norm_f_g": lambda o: o.reshape(D), "w_gate_up": lambda o: o[None],
                "conv_w": lambda o: o.transpose(1, 0, 2)}
    for i, name in enumerate(SMALL_NAMES):
        results[name] = [to_shape.get(name, lambda o: o)(o) for o in so[1 + 4 * i:5 + 4 * i]]

    names = ("norm1_g", "w_in", "w_gate_up", "b_gate", "gla_norm_g", "conv_w", "w_out", "norm2_g",
             "w_ffn_gate", "w_ffn_up", "w_ffn_down", "norm_f_g")
    outs = [loss, dx.reshape(nb, s, D)]
    for kind in range(4):
        for name in names:
            outs.append(results[name][kind])
    return tuple(outs)
```
